```python
import jax
import jax.numpy as jnp
from jax import lax

D_MODEL = 1024
BATCH = 8
SEQ = 8192
DEPTH = 1

HEAD_DIM = 64
WIN_Q_HEADS = 8
WIN_KV_HEADS = 2
WIN_HALF = 128
DIL_SLOTS = 8
DIL_PAIRS = ((128, 1), (512, 4), (2048, 16))
N_DIL = len(DIL_PAIRS)
ROT_DIM = HEAD_DIM // 4
ROPE_THETA = 500000.0
MEM_LEN = 256
X_HEADS = 4
X_HEAD_DIM = D_MODEL // X_HEADS
D_FF = 2816
CONV_WIDTH = 3
WIN_WIDTH = WIN_Q_HEADS * HEAD_DIM
DIL_WIDTH = DIL_SLOTS * HEAD_DIM
MIX_WIDTH = WIN_WIDTH + DIL_WIDTH
A_Q = WIN_WIDTH
A_KV = WIN_KV_HEADS * HEAD_DIM
B_QKV = N_DIL * DIL_WIDTH
IN_WIDTH = A_Q + 2 * A_KV + 3 * B_QKV
SPLITS = (A_Q, A_Q + A_KV, A_Q + 2 * A_KV, A_Q + 2 * A_KV + B_QKV, A_Q + 2 * A_KV + 2 * B_QKV)
DEEPNORM_ALPHA = (2 * DEPTH) ** 0.25
DEEPNORM_BETA = (8 * DEPTH) ** -0.25
LN_EPS = 1e-5
NEG_INF = -1e30
POS_OFFSET_MAX = 4096

kernel_name = 'hymba_window_dilated_deepnorm_encoder'


def layer_norm(x, g, b):
    xf = x.astype(jnp.float32)
    mu = jnp.mean(xf, -1, keepdims=True)
    var = jnp.mean(jnp.square(xf - mu), -1, keepdims=True)
    return ((xf - mu) * lax.rsqrt(var + LN_EPS) * g + b).astype(x.dtype)


def rms_norm(x, g):
    xf = x.astype(jnp.float32)
    return (xf * lax.rsqrt(jnp.mean(jnp.square(xf), -1, keepdims=True) + LN_EPS) * g).astype(x.dtype)


def partial_rope(t, positions):
    half = ROT_DIM // 2
    inv_freq = ROPE_THETA ** (-jnp.arange(0, ROT_DIM, 2, dtype=jnp.float32) / ROT_DIM)
    ang = positions.astype(jnp.float32)[:, :, None] * inv_freq
    cos = jnp.cos(ang)[:, :, None, :]
    sin = jnp.sin(ang)[:, :, None, :]
    tr = t[..., :ROT_DIM].astype(jnp.float32)
    t1, t2 = tr[..., :half], tr[..., half:]
    rot = jnp.concatenate([t1 * cos - t2 * sin, t2 * cos + t1 * sin], -1).astype(t.dtype)
    return jnp.concatenate([rot, t[..., ROT_DIM:]], -1)


def banded_attention(q, k, v, n_side, sink=None):
    bt, seq_len, hkv, grp, dh = q.shape
    blk = n_side
    nb = -(-seq_len // blk)
    pad = nb * blk - seq_len
    qb = jnp.pad(q, ((0, 0), (0, pad), (0, 0), (0, 0), (0, 0))).reshape(bt, nb, blk, hkv, grp, dh)

    def neighbourhood(t):
        tp = jnp.pad(t, ((0, 0), (blk, blk + pad), (0, 0), (0, 0))).reshape(bt, nb + 2, blk, hkv, dh)
        return jnp.concatenate([tp[:, :-2], tp[:, 1:-1], tp[:, 2:]], axis=2)

    kw = neighbourhood(k)
    vw = neighbourhood(v)
    s = jnp.einsum('bnqhgd,bnkhd->bnhgqk', qb, kw).astype(jnp.float32) * (dh ** -0.5)
    qi = jnp.arange(blk)[:, None]
    kj = jnp.arange(3 * blk)[None, :]
    kabs = jnp.arange(nb)[:, None, None] * blk + kj[None] - blk
    mask = (jnp.abs(kj - blk - qi) <= n_side)[None] & (kabs >= 0) & (kabs < seq_len)
    mask = mask[None, :, None, None]
    s = jnp.where(mask, s, NEG_INF)
    m = jnp.max(s, -1)
    if sink is not None:
        sink_f = sink.astype(jnp.float32)[None, None, :, :, None]
        m = jnp.maximum(m, sink_f)
    p = jnp.where(mask, jnp.exp(s - m[..., None]), 0.0)
    denom = jnp.sum(p, -1)
    if sink is not None:
        denom = denom + jnp.exp(sink_f - m)
    o = jnp.einsum('bnhgqk,bnkhd->bnqhgd', p, vw.astype(jnp.float32))
    o = o / jnp.moveaxis(denom, -1, 2)[..., None]
    o = o.astype(q.dtype).reshape(bt, nb * blk, hkv, grp, dh)[:, :seq_len]
    lse = jnp.moveaxis(m + jnp.log(denom), -1, 2).reshape(bt, nb * blk, hkv, grp)[:, :seq_len]
    return o, lse


def to_residue(t, r):
    b, s = t.shape[:2]
    t = t.reshape((b, s // r, r) + t.shape[2:])
    t = jnp.moveaxis(t, 2, 1)
    return t.reshape((b * r, s // r) + t.shape[3:])


def from_residue(t, r, batch):
    t = t.reshape((batch, r) + t.shape[1:])
    t = jnp.moveaxis(t, 1, 2)
    return t.reshape((batch, t.shape[1] * r) + t.shape[3:])


def parallel_mixer(h, positions, w_in, attn_sink, g_win, g_dil, w_out):
    b, s, _ = h.shape
    z = h @ w_in
    qa, ka, va, qb, kb, vb = jnp.split(z, SPLITS, axis=-1)
    qa = partial_rope(qa.reshape(b, s, WIN_Q_HEADS, HEAD_DIM), positions)
    ka = partial_rope(ka.reshape(b, s, WIN_KV_HEADS, HEAD_DIM), positions)
    va = va.reshape(b, s, WIN_KV_HEADS, HEAD_DIM)
    qa = qa.reshape(b, s, WIN_KV_HEADS, WIN_Q_HEADS // WIN_KV_HEADS, HEAD_DIM)
    out_a, _ = banded_attention(qa, ka, va, WIN_HALF, attn_sink.reshape(WIN_KV_HEADS, -1))
    out_a = out_a.reshape(b, s, WIN_WIDTH)
    n_heads_b = N_DIL * DIL_SLOTS
    qb = partial_rope(qb.reshape(b, s, n_heads_b, HEAD_DIM), positions).reshape(b, s, N_DIL, DIL_SLOTS, HEAD_DIM)
    kb = partial_rope(kb.reshape(b, s, n_heads_b, HEAD_DIM), positions).reshape(b, s, N_DIL, DIL_SLOTS, HEAD_DIM)
    vb = vb.reshape(b, s, N_DIL, DIL_SLOTS, HEAD_DIM)
    outs = []
    lses = []
    for gi, (window, dil) in enumerate(DIL_PAIRS):
        n_side = window // (2 * dil)
        o, lse = banded_attention(to_residue(qb[:, :, gi], dil)[:, :, :, None],
                                  to_residue(kb[:, :, gi], dil),
                                  to_residue(vb[:, :, gi], dil), n_side)
        outs.append(from_residue(o[:, :, :, 0], dil, b))
        lses.append(from_residue(lse[..., 0], dil, b))
    wts = jax.nn.softmax(jnp.stack(lses), axis=0)[..., None]
    out_b = jnp.sum(wts * jnp.stack(outs).astype(jnp.float32), axis=0).astype(h.dtype).reshape(b, s, DIL_WIDTH)
    mixed = jnp.concatenate([rms_norm(out_a, g_win), rms_norm(out_b, g_dil)], -1)
    return mixed @ w_out


def memory_cross_attention(h, mem_n, w_q, w_k, w_v, w_o):
    b, s, _ = h.shape
    m_len = mem_n.shape[1]
    q = (h @ w_q).reshape(b, s, X_HEADS, X_HEAD_DIM)
    k = (mem_n @ w_k).reshape(b, m_len, X_HEADS, X_HEAD_DIM)
    v = (mem_n @ w_v).reshape(b, m_len, X_HEADS, X_HEAD_DIM)
    sc = jnp.einsum('bshd,bmhd->bhsm', q, k).astype(jnp.float32) * (X_HEAD_DIM ** -0.5)
    p = jax.nn.softmax(sc, axis=-1)
    o = jnp.einsum('bhsm,bmhd->bshd', p, v.astype(jnp.float32)).astype(h.dtype)
    return o.reshape(b, s, D_MODEL) @ w_o


def conv_glu(h, w_gate, w_up, conv_w, conv_b, w_down):
    s = h.shape[1]
    g = h @ w_gate
    half = CONV_WIDTH // 2
    gp = jnp.pad(g, ((0, 0), (half, half), (0, 0)))
    g = sum(gp[:, j:j + s] * conv_w[j] for j in range(CONV_WIDTH)) + conv_b
    return (jax.nn.gelu(g, approximate=False) * (h @ w_up)) @ w_down


def _fwd_setup_inputs(seed: int = 0) -> dict:
    key = jax.random.key(seed)
    keys = list(jax.random.split(key, 40))
    f32 = jnp.float32
    d = D_MODEL
    nl = DEPTH
    beta = DEEPNORM_BETA

    def normal(shape, scale):
        return jax.random.normal(keys.pop(), shape, f32) * scale

    def gain(shape):
        return 1.0 + normal(shape, 0.02)

    def bias(shape):
        return normal(shape, 0.02)

    in_scale = jnp.concatenate([jnp.ones((A_Q + A_KV,), f32), jnp.full((A_KV,), beta, f32),
                                jnp.ones((2 * B_QKV,), f32), jnp.full((B_QKV,), beta, f32)])
    x = normal((BATCH, SEQ, d), 1.0)
    mem = normal((BATCH, MEM_LEN, d), 1.0)
    positions = jnp.arange(SEQ, dtype=jnp.int32)[None, :] + jax.random.randint(
        keys.pop(), (BATCH, 1), 0, POS_OFFSET_MAX, dtype=jnp.int32)
    return {
        'x': x,
        'mem': mem,
        'positions': positions,
        'ln_in_g': gain((d,)),
        'ln_in_b': bias((d,)),
        'w_in': normal((nl, d, IN_WIDTH), d ** -0.5) * in_scale,
        'attn_sink': normal((nl, WIN_Q_HEADS), 0.5),
        'g_win': gain((nl, WIN_WIDTH)),
        'g_dil': gain((nl, DIL_WIDTH)),
        'w_mix_out': normal((nl, MIX_WIDTH, d), MIX_WIDTH ** -0.5 * beta),
        'ln1_g': gain((nl, d)),
        'ln1_b': bias((nl, d)),
        'mem_ln_g': gain((nl, d)),
        'mem_ln_b': bias((nl, d)),
        'w_xq': normal((nl, d, d), d ** -0.5),
        'w_xk': normal((nl, d, d), d ** -0.5),
        'w_xv': normal((nl, d, d), d ** -0.5 * beta),
        'w_xo': normal((nl, d, d), d ** -0.5 * beta),
        'ln2_g': gain((nl, d)),
        'ln2_b': bias((nl, d)),
        'w_gate': normal((nl, d, D_FF), d ** -0.5),
        'w_up': normal((nl, d, D_FF), d ** -0.5 * beta),
        'conv_w': normal((nl, CONV_WIDTH, D_FF), CONV_WIDTH ** -0.5),
        'conv_b': bias((nl, D_FF)),
        'w_down': normal((nl, D_FF, d), D_FF ** -0.5 * beta),
        'ln3_g': gain((nl, d)),
        'ln3_b': bias((nl, d)),
    }


def _fwd_reference(x, mem, positions, ln_in_g, ln_in_b, w_in, attn_sink, g_win, g_dil, w_mix_out,
              ln1_g, ln1_b, mem_ln_g, mem_ln_b, w_xq, w_xk, w_xv, w_xo, ln2_g, ln2_b,
              w_gate, w_up, conv_w, conv_b, w_down, ln3_g, ln3_b):
    h = layer_norm(x, ln_in_g, ln_in_b)
    for l in range(DEPTH):
        mix = parallel_mixer(h, positions, w_in[l], attn_sink[l], g_win[l], g_dil[l], w_mix_out[l])
        h = layer_norm(DEEPNORM_ALPHA * h + mix, ln1_g[l], ln1_b[l])
        mem_n = layer_norm(mem, mem_ln_g[l], mem_ln_b[l])
        xa = memory_cross_attention(h, mem_n, w_xq[l], w_xk[l], w_xv[l], w_xo[l])
        h = layer_norm(DEEPNORM_ALPHA * h + xa, ln2_g[l], ln2_b[l])
        ff = conv_glu(h, w_gate[l], w_up[l], conv_w[l], conv_b[l], w_down[l])
        h = layer_norm(DEEPNORM_ALPHA * h + ff, ln3_g[l], ln3_b[l])
    return h


import jax as _jax
import jax.numpy as _jnp

TWIN_FORMAT = 'train_step'
FWD_PARAMS = ['x', 'mem', 'positions', 'ln_in_g', 'ln_in_b', 'w_in', 'attn_sink', 'g_win', 'g_dil', 'w_mix_out', 'ln1_g', 'ln1_b', 'mem_ln_g', 'mem_ln_b', 'w_xq', 'w_xk', 'w_xv', 'w_xo', 'ln2_g', 'ln2_b', 'w_gate', 'w_up', 'conv_w', 'conv_b', 'w_down', 'ln3_g', 'ln3_b']
TWIN_WEIGHTS = ['ln_in_g', 'ln_in_b', 'w_in', 'attn_sink', 'g_win', 'g_dil', 'w_mix_out', 'ln1_g', 'ln1_b', 'mem_ln_g', 'mem_ln_b', 'w_xq', 'w_xk', 'w_xv', 'w_xo', 'ln2_g', 'ln2_b', 'w_gate', 'w_up', 'conv_w', 'conv_b', 'w_down', 'ln3_g', 'ln3_b']
TWIN_DIFF_INPUT = 'x'
TWIN_INPUTS = ['x', 'mem', 'positions', 'ln_in_g', 'ln_in_b', 'w_in', 'attn_sink', 'g_win', 'g_dil', 'w_mix_out', 'ln1_g', 'ln1_b', 'mem_ln_g', 'mem_ln_b', 'w_xq', 'w_xk', 'w_xv', 'w_xo', 'ln2_g', 'ln2_b', 'w_gate', 'w_up', 'conv_w', 'conv_b', 'w_down', 'ln3_g', 'ln3_b', 'loss_target', 'm_ln_in_g', 'm_ln_in_b', 'm_w_in', 'm_attn_sink', 'm_g_win', 'm_g_dil', 'm_w_mix_out', 'm_ln1_g', 'm_ln1_b', 'm_mem_ln_g', 'm_mem_ln_b', 'm_w_xq', 'm_w_xk', 'm_w_xv', 'm_w_xo', 'm_ln2_g', 'm_ln2_b', 'm_w_gate', 'm_w_up', 'm_conv_w', 'm_conv_b', 'm_w_down', 'm_ln3_g', 'm_ln3_b', 'v_ln_in_g', 'v_ln_in_b', 'v_w_in', 'v_attn_sink', 'v_g_win', 'v_g_dil', 'v_w_mix_out', 'v_ln1_g', 'v_ln1_b', 'v_mem_ln_g', 'v_mem_ln_b', 'v_w_xq', 'v_w_xk', 'v_w_xv', 'v_w_xo', 'v_ln2_g', 'v_ln2_b', 'v_w_gate', 'v_w_up', 'v_conv_w', 'v_conv_b', 'v_w_down', 'v_ln3_g', 'v_ln3_b']
TWIN_OUTPUTS = ['loss', 'grad_x', 'grad_ln_in_g', 'grad_ln_in_b', 'grad_w_in', 'grad_attn_sink', 'grad_g_win', 'grad_g_dil', 'grad_w_mix_out', 'grad_ln1_g', 'grad_ln1_b', 'grad_mem_ln_g', 'grad_mem_ln_b', 'grad_w_xq', 'grad_w_xk', 'grad_w_xv', 'grad_w_xo', 'grad_ln2_g', 'grad_ln2_b', 'grad_w_gate', 'grad_w_up', 'grad_conv_w', 'grad_conv_b', 'grad_w_down', 'grad_ln3_g', 'grad_ln3_b', 'delta_ln_in_g', 'delta_ln_in_b', 'delta_w_in', 'delta_attn_sink', 'delta_g_win', 'delta_g_dil', 'delta_w_mix_out', 'delta_ln1_g', 'delta_ln1_b', 'delta_mem_ln_g', 'delta_mem_ln_b', 'delta_w_xq', 'delta_w_xk', 'delta_w_xv', 'delta_w_xo', 'delta_ln2_g', 'delta_ln2_b', 'delta_w_gate', 'delta_w_up', 'delta_conv_w', 'delta_conv_b', 'delta_w_down', 'delta_ln3_g', 'delta_ln3_b', 'new_m_ln_in_g', 'new_m_ln_in_b', 'new_m_w_in', 'new_m_attn_sink', 'new_m_g_win', 'new_m_g_dil', 'new_m_w_mix_out', 'new_m_ln1_g', 'new_m_ln1_b', 'new_m_mem_ln_g', 'new_m_mem_ln_b', 'new_m_w_xq', 'new_m_w_xk', 'new_m_w_xv', 'new_m_w_xo', 'new_m_ln2_g', 'new_m_ln2_b', 'new_m_w_gate', 'new_m_w_up', 'new_m_conv_w', 'new_m_conv_b', 'new_m_w_down', 'new_m_ln3_g', 'new_m_ln3_b', 'new_v_ln_in_g', 'new_v_ln_in_b', 'new_v_w_in', 'new_v_attn_sink', 'new_v_g_win', 'new_v_g_dil', 'new_v_w_mix_out', 'new_v_ln1_g', 'new_v_ln1_b', 'new_v_mem_ln_g', 'new_v_mem_ln_b', 'new_v_w_xq', 'new_v_w_xk', 'new_v_w_xv', 'new_v_w_xo', 'new_v_ln2_g', 'new_v_ln2_b', 'new_v_w_gate', 'new_v_w_up', 'new_v_conv_w', 'new_v_conv_b', 'new_v_w_down', 'new_v_ln3_g', 'new_v_ln3_b']
TWIN_LEAF_KINDS = {'loss': 'loss', 'grad_x': 'grad_x', 'grad_ln_in_g': 'grad_w', 'grad_ln_in_b': 'grad_w', 'grad_w_in': 'grad_w', 'grad_attn_sink': 'grad_w', 'grad_g_win': 'grad_w', 'grad_g_dil': 'grad_w', 'grad_w_mix_out': 'grad_w', 'grad_ln1_g': 'grad_w', 'grad_ln1_b': 'grad_w', 'grad_mem_ln_g': 'grad_w', 'grad_mem_ln_b': 'grad_w', 'grad_w_xq': 'grad_w', 'grad_w_xk': 'grad_w', 'grad_w_xv': 'grad_w', 'grad_w_xo': 'grad_w', 'grad_ln2_g': 'grad_w', 'grad_ln2_b': 'grad_w', 'grad_w_gate': 'grad_w', 'grad_w_up': 'grad_w', 'grad_conv_w': 'grad_w', 'grad_conv_b': 'grad_w', 'grad_w_down': 'grad_w', 'grad_ln3_g': 'grad_w', 'grad_ln3_b': 'grad_w', 'delta_ln_in_g': 'delta_w', 'delta_ln_in_b': 'delta_w', 'delta_w_in': 'delta_w', 'delta_attn_sink': 'delta_w', 'delta_g_win': 'delta_w', 'delta_g_dil': 'delta_w', 'delta_w_mix_out': 'delta_w', 'delta_ln1_g': 'delta_w', 'delta_ln1_b': 'delta_w', 'delta_mem_ln_g': 'delta_w', 'delta_mem_ln_b': 'delta_w', 'delta_w_xq': 'delta_w', 'delta_w_xk': 'delta_w', 'delta_w_xv': 'delta_w', 'delta_w_xo': 'delta_w', 'delta_ln2_g': 'delta_w', 'delta_ln2_b': 'delta_w', 'delta_w_gate': 'delta_w', 'delta_w_up': 'delta_w', 'delta_conv_w': 'delta_w', 'delta_conv_b': 'delta_w', 'delta_w_down': 'delta_w', 'delta_ln3_g': 'delta_w', 'delta_ln3_b': 'delta_w', 'new_m_ln_in_g': 'new_m', 'new_m_ln_in_b': 'new_m', 'new_m_w_in': 'new_m', 'new_m_attn_sink': 'new_m', 'new_m_g_win': 'new_m', 'new_m_g_dil': 'new_m', 'new_m_w_mix_out': 'new_m', 'new_m_ln1_g': 'new_m', 'new_m_ln1_b': 'new_m', 'new_m_mem_ln_g': 'new_m', 'new_m_mem_ln_b': 'new_m', 'new_m_w_xq': 'new_m', 'new_m_w_xk': 'new_m', 'new_m_w_xv': 'new_m', 'new_m_w_xo': 'new_m', 'new_m_ln2_g': 'new_m', 'new_m_ln2_b': 'new_m', 'new_m_w_gate': 'new_m', 'new_m_w_up': 'new_m', 'new_m_conv_w': 'new_m', 'new_m_conv_b': 'new_m', 'new_m_w_down': 'new_m', 'new_m_ln3_g': 'new_m', 'new_m_ln3_b': 'new_m', 'new_v_ln_in_g': 'new_v', 'new_v_ln_in_b': 'new_v', 'new_v_w_in': 'new_v', 'new_v_attn_sink': 'new_v', 'new_v_g_win': 'new_v', 'new_v_g_dil': 'new_v', 'new_v_w_mix_out': 'new_v', 'new_v_ln1_g': 'new_v', 'new_v_ln1_b': 'new_v', 'new_v_mem_ln_g': 'new_v', 'new_v_mem_ln_b': 'new_v', 'new_v_w_xq': 'new_v', 'new_v_w_xk': 'new_v', 'new_v_w_xv': 'new_v', 'new_v_w_xo': 'new_v', 'new_v_ln2_g': 'new_v', 'new_v_ln2_b': 'new_v', 'new_v_w_gate': 'new_v', 'new_v_w_up': 'new_v', 'new_v_conv_w': 'new_v', 'new_v_conv_b': 'new_v', 'new_v_w_down': 'new_v', 'new_v_ln3_g': 'new_v', 'new_v_ln3_b': 'new_v'}


def _forward(args):
    return _fwd_reference(*[args[k] for k in FWD_PARAMS])


def _output_shape():
    def fwd():
        inp = _fwd_setup_inputs(0)
        return _fwd_reference(*[inp[k] for k in FWD_PARAMS])
    out = _jax.eval_shape(fwd)
    return out.shape, out.dtype

N_MICROBATCH = 1
ADAM_LR = 0.001
ADAM_B1 = 0.9
ADAM_B2 = 0.999
ADAM_EPS = 1e-08
ADAM_WD = 0.01
ADAM_STEP = 10
PER_EXAMPLE_BATCH_AXIS = {'x': 0, 'mem': 0, 'positions': 0, 'loss_target': 0}
SHARED_INPUTS = []
_WEIGHT_DTYPES = {'ln_in_g': _jnp.float32, 'ln_in_b': _jnp.float32, 'w_in': _jnp.float32, 'attn_sink': _jnp.float32, 'g_win': _jnp.float32, 'g_dil': _jnp.float32, 'w_mix_out': _jnp.float32, 'ln1_g': _jnp.float32, 'ln1_b': _jnp.float32, 'mem_ln_g': _jnp.float32, 'mem_ln_b': _jnp.float32, 'w_xq': _jnp.float32, 'w_xk': _jnp.float32, 'w_xv': _jnp.float32, 'w_xo': _jnp.float32, 'ln2_g': _jnp.float32, 'ln2_b': _jnp.float32, 'w_gate': _jnp.float32, 'w_up': _jnp.float32, 'conv_w': _jnp.float32, 'conv_b': _jnp.float32, 'w_down': _jnp.float32, 'ln3_g': _jnp.float32, 'ln3_b': _jnp.float32}
MOMENT_SCALE = {'ln_in_g': 1.630601e+00, 'ln_in_b': 4.100319e+00, 'w_in': 1.154575e-01, 'attn_sink': 2.832759e-03, 'g_win': 1.484045e-01, 'g_dil': 1.158017e-01, 'w_mix_out': 2.137776e-01, 'ln1_g': 2.238091e+00, 'ln1_b': 1.047814e+00, 'mem_ln_g': 1.303976e-02, 'mem_ln_b': 3.034763e-01, 'w_xq': 7.995400e-03, 'w_xk': 7.974008e-03, 'w_xv': 1.772295e-02, 'w_xo': 1.778776e-02, 'ln2_g': 2.245660e+00, 'ln2_b': 1.047709e+00, 'w_gate': 2.977877e-02, 'w_up': 4.923208e-02, 'conv_w': 3.045687e-02, 'conv_b': 2.934425e-02, 'w_down': 8.100616e-02, 'ln3_g': 6.404396e+01, 'ln3_b': 4.973831e+00}


def _to_microbatches(a, axis):
    t = _jnp.moveaxis(a, axis, 0)
    t = t.reshape((N_MICROBATCH, t.shape[0] // N_MICROBATCH) + t.shape[1:])
    return _jnp.moveaxis(t, 1, axis + 1)


def setup_inputs(seed: int = 0) -> dict:
    inp = _fwd_setup_inputs(seed)
    key = _jax.random.fold_in(_jax.random.key(seed), 7919)
    shape, _ = _output_shape()
    out = dict(inp)
    out["loss_target"] = _jax.random.normal(_jax.random.fold_in(key, 0), shape, _jnp.float32)
    for i, name in enumerate(TWIN_WEIGHTS):
        w = inp[name].astype(_jnp.float32)
        if MOMENT_SCALE is None:
            s = _jnp.sqrt(_jnp.mean(_jnp.square(w)) + 1e-30)
        else:
            s = MOMENT_SCALE[name]
        km, kv = _jax.random.split(_jax.random.fold_in(key, i + 1))
        out[name] = w
        out["m_" + name] = s * _jax.random.normal(km, w.shape, _jnp.float32)
        out["v_" + name] = (s * s) * _jax.random.uniform(kv, w.shape, _jnp.float32, 0.5, 1.5)
    if N_MICROBATCH > 1:
        for name, axis in PER_EXAMPLE_BATCH_AXIS.items():
            out[name] = _to_microbatches(out[name], axis)
    return {'x': out['x'], 'mem': out['mem'], 'positions': out['positions'], 'ln_in_g': out['ln_in_g'], 'ln_in_b': out['ln_in_b'], 'w_in': out['w_in'], 'attn_sink': out['attn_sink'], 'g_win': out['g_win'], 'g_dil': out['g_dil'], 'w_mix_out': out['w_mix_out'], 'ln1_g': out['ln1_g'], 'ln1_b': out['ln1_b'], 'mem_ln_g': out['mem_ln_g'], 'mem_ln_b': out['mem_ln_b'], 'w_xq': out['w_xq'], 'w_xk': out['w_xk'], 'w_xv': out['w_xv'], 'w_xo': out['w_xo'], 'ln2_g': out['ln2_g'], 'ln2_b': out['ln2_b'], 'w_gate': out['w_gate'], 'w_up': out['w_up'], 'conv_w': out['conv_w'], 'conv_b': out['conv_b'], 'w_down': out['w_down'], 'ln3_g': out['ln3_g'], 'ln3_b': out['ln3_b'], 'loss_target': out['loss_target'], 'm_ln_in_g': out['m_ln_in_g'], 'm_ln_in_b': out['m_ln_in_b'], 'm_w_in': out['m_w_in'], 'm_attn_sink': out['m_attn_sink'], 'm_g_win': out['m_g_win'], 'm_g_dil': out['m_g_dil'], 'm_w_mix_out': out['m_w_mix_out'], 'm_ln1_g': out['m_ln1_g'], 'm_ln1_b': out['m_ln1_b'], 'm_mem_ln_g': out['m_mem_ln_g'], 'm_mem_ln_b': out['m_mem_ln_b'], 'm_w_xq': out['m_w_xq'], 'm_w_xk': out['m_w_xk'], 'm_w_xv': out['m_w_xv'], 'm_w_xo': out['m_w_xo'], 'm_ln2_g': out['m_ln2_g'], 'm_ln2_b': out['m_ln2_b'], 'm_w_gate': out['m_w_gate'], 'm_w_up': out['m_w_up'], 'm_conv_w': out['m_conv_w'], 'm_conv_b': out['m_conv_b'], 'm_w_down': out['m_w_down'], 'm_ln3_g': out['m_ln3_g'], 'm_ln3_b': out['m_ln3_b'], 'v_ln_in_g': out['v_ln_in_g'], 'v_ln_in_b': out['v_ln_in_b'], 'v_w_in': out['v_w_in'], 'v_attn_sink': out['v_attn_sink'], 'v_g_win': out['v_g_win'], 'v_g_dil': out['v_g_dil'], 'v_w_mix_out': out['v_w_mix_out'], 'v_ln1_g': out['v_ln1_g'], 'v_ln1_b': out['v_ln1_b'], 'v_mem_ln_g': out['v_mem_ln_g'], 'v_mem_ln_b': out['v_mem_ln_b'], 'v_w_xq': out['v_w_xq'], 'v_w_xk': out['v_w_xk'], 'v_w_xv': out['v_w_xv'], 'v_w_xo': out['v_w_xo'], 'v_ln2_g': out['v_ln2_g'], 'v_ln2_b': out['v_ln2_b'], 'v_w_gate': out['v_w_gate'], 'v_w_up': out['v_w_up'], 'v_conv_w': out['v_conv_w'], 'v_conv_b': out['v_conv_b'], 'v_w_down': out['v_w_down'], 'v_ln3_g': out['v_ln3_g'], 'v_ln3_b': out['v_ln3_b']}


def _loss(weights, diff, rest, loss_target):
    with _jax.named_scope("forward"):
        args = {**rest, TWIN_DIFF_INPUT: diff, **{k: w.astype(_WEIGHT_DTYPES[k]) for k, w in weights.items()}}
        y = _forward(args)
    with _jax.named_scope("loss_head"):
        err = _jnp.square(y.astype(_jnp.float32) - loss_target)
        return 0.5 * _jnp.sum(_jnp.mean(err, axis=-1)) if err.ndim else 0.5 * err


def _adamw(w, g, m, v):
    m = ADAM_B1 * m + (1.0 - ADAM_B1) * g
    v = ADAM_B2 * v + (1.0 - ADAM_B2) * _jnp.square(g)
    m_hat = m / (1.0 - ADAM_B1 ** ADAM_STEP)
    v_hat = v / (1.0 - ADAM_B2 ** ADAM_STEP)
    delta = -ADAM_LR * (m_hat / (_jnp.sqrt(v_hat) + ADAM_EPS) + ADAM_WD * w)
    return delta, m, v


def reference(x, mem, positions, ln_in_g, ln_in_b, w_in, attn_sink, g_win, g_dil, w_mix_out, ln1_g, ln1_b, mem_ln_g, mem_ln_b, w_xq, w_xk, w_xv, w_xo, ln2_g, ln2_b, w_gate, w_up, conv_w, conv_b, w_down, ln3_g, ln3_b, loss_target, m_ln_in_g, m_ln_in_b, m_w_in, m_attn_sink, m_g_win, m_g_dil, m_w_mix_out, m_ln1_g, m_ln1_b, m_mem_ln_g, m_mem_ln_b, m_w_xq, m_w_xk, m_w_xv, m_w_xo, m_ln2_g, m_ln2_b, m_w_gate, m_w_up, m_conv_w, m_conv_b, m_w_down, m_ln3_g, m_ln3_b, v_ln_in_g, v_ln_in_b, v_w_in, v_attn_sink, v_g_win, v_g_dil, v_w_mix_out, v_ln1_g, v_ln1_b, v_mem_ln_g, v_mem_ln_b, v_w_xq, v_w_xk, v_w_xv, v_w_xo, v_ln2_g, v_ln2_b, v_w_gate, v_w_up, v_conv_w, v_conv_b, v_w_down, v_ln3_g, v_ln3_b):
    given = dict(x=x, mem=mem, positions=positions, ln_in_g=ln_in_g, ln_in_b=ln_in_b, w_in=w_in, attn_sink=attn_sink, g_win=g_win, g_dil=g_dil, w_mix_out=w_mix_out, ln1_g=ln1_g, ln1_b=ln1_b, mem_ln_g=mem_ln_g, mem_ln_b=mem_ln_b, w_xq=w_xq, w_xk=w_xk, w_xv=w_xv, w_xo=w_xo, ln2_g=ln2_g, ln2_b=ln2_b, w_gate=w_gate, w_up=w_up, conv_w=conv_w, conv_b=conv_b, w_down=w_down, ln3_g=ln3_g, ln3_b=ln3_b, loss_target=loss_target, m_ln_in_g=m_ln_in_g, m_ln_in_b=m_ln_in_b, m_w_in=m_w_in, m_attn_sink=m_attn_sink, m_g_win=m_g_win, m_g_dil=m_g_dil, m_w_mix_out=m_w_mix_out, m_ln1_g=m_ln1_g, m_ln1_b=m_ln1_b, m_mem_ln_g=m_mem_ln_g, m_mem_ln_b=m_mem_ln_b, m_w_xq=m_w_xq, m_w_xk=m_w_xk, m_w_xv=m_w_xv, m_w_xo=m_w_xo, m_ln2_g=m_ln2_g, m_ln2_b=m_ln2_b, m_w_gate=m_w_gate, m_w_up=m_w_up, m_conv_w=m_conv_w, m_conv_b=m_conv_b, m_w_down=m_w_down, m_ln3_g=m_ln3_g, m_ln3_b=m_ln3_b, v_ln_in_g=v_ln_in_g, v_ln_in_b=v_ln_in_b, v_w_in=v_w_in, v_attn_sink=v_attn_sink, v_g_win=v_g_win, v_g_dil=v_g_dil, v_w_mix_out=v_w_mix_out, v_ln1_g=v_ln1_g, v_ln1_b=v_ln1_b, v_mem_ln_g=v_mem_ln_g, v_mem_ln_b=v_mem_ln_b, v_w_xq=v_w_xq, v_w_xk=v_w_xk, v_w_xv=v_w_xv, v_w_xo=v_w_xo, v_ln2_g=v_ln2_g, v_ln2_b=v_ln2_b, v_w_gate=v_w_gate, v_w_up=v_w_up, v_conv_w=v_conv_w, v_conv_b=v_conv_b, v_w_down=v_w_down, v_ln3_g=v_ln3_g, v_ln3_b=v_ln3_b)
    weights = {n: given[n] for n in TWIN_WEIGHTS}
    shared = {n: given[n] for n in SHARED_INPUTS}
    per_example = {n: given[n] for n in ['x', 'mem', 'positions']}
    grad_fn = _jax.value_and_grad(_loss, argnums=(0, 1))

    def one_microbatch(ex, loss_target):
        ex = dict(ex)
        diff = ex.pop(TWIN_DIFF_INPUT)
        return grad_fn(weights, diff, {**shared, **ex}, loss_target)

    if N_MICROBATCH == 1:
        loss, (grad_w, grad_x) = one_microbatch(per_example, given["loss_target"])
    else:
        def body(carry, xs):
            loss_sum, grad_sum = carry
            l_k, (gw_k, gx_k) = one_microbatch(xs[0], xs[1])
            with _jax.named_scope("update"):
                return (loss_sum + l_k, _jax.tree.map(_jnp.add, grad_sum, gw_k)), gx_k

        init = (_jnp.zeros((), _jnp.float32), _jax.tree.map(_jnp.zeros_like, weights))
        (loss, grad_w), grad_x = _jax.lax.scan(body, init, (per_example, given["loss_target"]))
    with _jax.named_scope("update"):
        delta_w, new_m, new_v = {}, {}, {}
        for n in TWIN_WEIGHTS:
            delta_w[n], new_m[n], new_v[n] = _adamw(weights[n], grad_w[n], given["m_" + n], given["v_" + n])
    return (loss, grad_x, *[grad_w[n] for n in TWIN_WEIGHTS], *[delta_w[n] for n in TWIN_WEIGHTS],
            *[new_m[n] for n in TWIN_WEIGHTS], *[new_v[n] for n in TWIN_WEIGHTS])
```

```python
import functools
import math

import jax
import jax.numpy as jnp
from jax import lax
from jax.experimental import pallas as pl
from jax.experimental.pallas import tpu as pltpu

F32 = jnp.float32
BF16 = jnp.bfloat16

D_MODEL = 1024
HEAD_DIM = 64
WIN_Q_HEADS = 8
WIN_KV_HEADS = 2
WIN_HALF = 128
DIL_SLOTS = 8
DILATIONS = (1, 4, 16)
DIL_HALF = 64
ROT_DIM = 16
ROPE_THETA = 500000.0
X_HEADS = 4
X_HEAD_DIM = 256
D_FF = 2816
A_Q = 512
A_KV = 128
A_WIDTH = A_Q + 2 * A_KV
B_QKV = 1536
IN_WIDTH = 5376
ALPHA = 2.0 ** 0.25
LN_EPS = 1e-5
NEG_INF = -1e30
LANES = 128
N_CHIPS = 4
N_DEV = 8

ADAM_LR = 0.001
ADAM_B1 = 0.9
ADAM_B2 = 0.999
ADAM_EPS = 1e-08
ADAM_WD = 0.01
ADAM_STEP = 10

VMEM_LIMIT = 56 * 1024 * 1024


def _cparams(**kw):
    return pltpu.CompilerParams(vmem_limit_bytes=VMEM_LIMIT, **kw)


def _dot(a, b):
    return lax.dot_general(a, b, (((1,), (0,)), ((), ())), preferred_element_type=F32)


def _dot_nt(a, b):
    return lax.dot_general(a, b, (((1,), (1,)), ((), ())), preferred_element_type=F32)


def _dot_tn(a, b):
    return lax.dot_general(a, b, (((0,), (0,)), ((), ())), preferred_element_type=F32)


def _ln(x, g, b):
    mu = jnp.mean(x, axis=-1, keepdims=True)
    xc = x - mu
    var = jnp.mean(xc * xc, axis=-1, keepdims=True)
    return xc * lax.rsqrt(var + LN_EPS) * g + b


def _ln_bwd_math(dy, r, g):
    mu = jnp.mean(r, axis=-1, keepdims=True)
    xc = r - mu
    var = jnp.mean(xc * xc, axis=-1, keepdims=True)
    rstd = lax.rsqrt(var + LN_EPS)
    xhat = xc * rstd
    dxhat = dy * g
    m1 = jnp.mean(dxhat, axis=-1, keepdims=True)
    m2 = jnp.mean(dxhat * xhat, axis=-1, keepdims=True)
    dr = rstd * (dxhat - m1 - xhat * m2)
    return dr, jnp.sum(dy * xhat, axis=0, keepdims=True), jnp.sum(dy, axis=0, keepdims=True)


def _rope(z, ta, tb, tc, sign):
    w = z.shape[1]
    reps = w // LANES
    a = jnp.tile(ta, (1, reps))
    b = jnp.tile(tb, (1, reps))
    c = jnp.tile(tc, (1, reps))
    return z * a + sign * (pltpu.roll(z, w - 8, 1) * b + pltpu.roll(z, 8, 1) * c)


def _shift_rows(x, prev_row, next_row):
    t = x.shape[0]
    row = lax.broadcasted_iota(jnp.int32, x.shape, 0)
    xm1 = jnp.where(row == 0, prev_row, pltpu.roll(x, 1, 0))
    xp1 = jnp.where(row == t - 1, next_row, pltpu.roll(x, t - 1, 0))
    return xm1, xp1


def _mm(a, b, *, mode, out_dtype, tm, tn, tk=None, add=None, add_scale=1.0, name):
    if mode in ("nn", "nt"):
        m, k = a.shape
        n = b.shape[1] if mode == "nn" else b.shape[0]
        assert m % tm == 0 and n % tn == 0
        dot = _dot if mode == "nn" else _dot_nt

        def body(*refs):
            if add is None:
                a_ref, b_ref, o_ref = refs
                o_ref[...] = dot(a_ref[...], b_ref[...]).astype(out_dtype)
            else:
                a_ref, b_ref, c_ref, o_ref = refs
                o_ref[...] = (dot(a_ref[...], b_ref[...]) + add_scale * c_ref[...]).astype(out_dtype)

        b_spec = (pl.BlockSpec((k, tn), lambda i, j: (0, j)) if mode == "nn"
                  else pl.BlockSpec((tn, k), lambda i, j: (j, 0)))
        in_specs = [pl.BlockSpec((tm, k), lambda i, j: (i, 0)), b_spec]
        args = [a, b]
        if add is not None:
            in_specs.append(pl.BlockSpec((tm, tn), lambda i, j: (i, j)))
            args.append(add)
        return pl.pallas_call(
            body, name=name, grid=(m // tm, n // tn), in_specs=in_specs,
            out_specs=pl.BlockSpec((tm, tn), lambda i, j: (i, j)),
            out_shape=jax.ShapeDtypeStruct((m, n), out_dtype),
            compiler_params=_cparams(dimension_semantics=("parallel", "parallel")),
        )(*args)
    assert mode == "tn" and add is None
    kk, m = a.shape
    n = b.shape[1]
    assert m % tm == 0 and n % tn == 0 and kk % tk == 0
    nk = kk // tk

    def body(a_ref, b_ref, o_ref, acc_ref):
        kstep = pl.program_id(2)

        @pl.when(kstep == 0)
        def _():
            acc_ref[...] = jnp.zeros_like(acc_ref)

        acc_ref[...] += _dot_tn(a_ref[...], b_ref[...])

        @pl.when(kstep == nk - 1)
        def _():
            o_ref[...] = acc_ref[...].astype(out_dtype)

    return pl.pallas_call(
        body, name=name, grid=(m // tm, n // tn, nk),
        in_specs=[pl.BlockSpec((tk, tm), lambda i, j, s: (s, i)), pl.BlockSpec((tk, tn), lambda i, j, s: (s, j))],
        out_specs=pl.BlockSpec((tm, tn), lambda i, j, s: (i, j)),
        out_shape=jax.ShapeDtypeStruct((m, n), out_dtype),
        scratch_shapes=[pltpu.VMEM((tm, tn), F32)],
        compiler_params=_cparams(dimension_semantics=("parallel", "parallel", "arbitrary")),
    )(a, b)


def _ln_in_fwd(x, g, b, *, t):
    s = x.shape[0]

    def body(x_ref, g_ref, b_ref, o_ref):
        o_ref[...] = _ln(x_ref[...], g_ref[...], b_ref[...]).astype(BF16)

    row = pl.BlockSpec((1, D_MODEL), lambda i: (0, 0))
    return pl.pallas_call(
        body, name="ln_in_fwd", grid=(s // t,),
        in_specs=[pl.BlockSpec((t, D_MODEL), lambda i: (i, 0)), row, row],
        out_specs=pl.BlockSpec((t, D_MODEL), lambda i: (i, 0)),
        out_shape=jax.ShapeDtypeStruct((s, D_MODEL), BF16),
        compiler_params=_cparams(dimension_semantics=("parallel",)),
    )(x, g, b)


def _ln_bwd(dy, r, g, *, t, name, want_bf16):
    s = r.shape[0]

    def body(dy_ref, r_ref, g_ref, *outs):
        i = pl.program_id(0)
        dr, dg, db = _ln_bwd_math(dy_ref[...], r_ref[...], g_ref[...])
        outs[0][...] = dr
        if want_bf16:
            outs[1][...] = dr.astype(BF16)
        st_ref = outs[-1]

        @pl.when(i == 0)
        def _():
            st_ref[...] = jnp.zeros_like(st_ref)

        st_ref[0:1, :] += dg
        st_ref[1:2, :] += db

    tile = pl.BlockSpec((t, D_MODEL), lambda i: (i, 0))
    out_specs = [tile] + ([tile] if want_bf16 else []) + [pl.BlockSpec((8, D_MODEL), lambda i: (0, 0))]
    out_shape = ([jax.ShapeDtypeStruct((s, D_MODEL), F32)]
                 + ([jax.ShapeDtypeStruct((s, D_MODEL), BF16)] if want_bf16 else [])
                 + [jax.ShapeDtypeStruct((8, D_MODEL), F32)])
    return pl.pallas_call(
        body, name=name, grid=(s // t,),
        in_specs=[tile, tile, pl.BlockSpec((1, D_MODEL), lambda i: (0, 0))],
        out_specs=out_specs, out_shape=out_shape,
        compiler_params=_cparams(dimension_semantics=("arbitrary",)),
    )(dy, r, g)


def _proj(h0b, w_in_b, tabs, *, blk_of_j, kind_of_j, nj, dil, t, name):
    s = h0b.shape[0]
    cb = 256
    ta, tb, tc = tabs

    def body(h_ref, w_ref, ta_ref, tb_ref, tc_ref, o_ref, scr):
        j = pl.program_id(1)
        kind = kind_of_j(j)
        acc = _dot(h_ref[...], w_ref[...])

        def put(z):
            for half in range(cb // LANES):
                scr[half] = z[:, half * LANES:(half + 1) * LANES]

        @pl.when(kind == 0)
        def _():
            put(acc)

        @pl.when(kind == 1)
        def _():
            put(_rope(acc, ta_ref[...], tb_ref[...], tc_ref[...], 1.0))

        @pl.when(kind == 2)
        def _():
            lane = lax.broadcasted_iota(jnp.int32, acc.shape, 1)
            put(jnp.where(lane < LANES, _rope(acc, ta_ref[...], tb_ref[...], tc_ref[...], 1.0), acc))

        for c in range(dil):
            for half in range(cb // LANES):
                rows = scr[half] if dil == 1 else scr[half, pl.ds(c, t // dil, stride=dil), :]
                o_ref[c, :, half * LANES:(half + 1) * LANES] = rows.astype(BF16)

    tab = pl.BlockSpec((t, LANES), lambda i, j: (i, 0))
    return pl.pallas_call(
        body, name=name, grid=(s // t, nj),
        in_specs=[pl.BlockSpec((t, D_MODEL), lambda i, j: (i, 0)),
                  pl.BlockSpec((D_MODEL, cb), lambda i, j: (0, blk_of_j(j))), tab, tab, tab],
        out_specs=pl.BlockSpec((dil, t // dil, cb), lambda i, j: (0, i, j)),
        out_shape=jax.ShapeDtypeStruct((dil, s // dil, nj * cb), BF16),
        scratch_shapes=[pltpu.VMEM((cb // LANES, t, LANES), F32)],
        compiler_params=_cparams(dimension_semantics=("parallel", "arbitrary")),
    )(h0b, w_in_b, ta, tb, tc)


def _window_mask(i, tq, w, seq_len):
    tk = tq + 2 * w
    qpos = i * tq + lax.broadcasted_iota(jnp.int32, (tq, tk), 0)
    kpos = i * tq - w + lax.broadcasted_iota(jnp.int32, (tq, tk), 1)
    return (jnp.abs(qpos - kpos) <= w) & (kpos >= 0) & (kpos < seq_len)


def _swa_specs(tq, hq, hkv, n, qcol, kcol, vcol):
    qw, kw = hq * HEAD_DIM, hkv * HEAD_DIM
    cur = lambda s, i: jnp.minimum(i, n - 1)
    prv = lambda s, i: jnp.maximum(jnp.minimum(i, n - 1) - 1, 0)
    nxt = lambda s, i: jnp.minimum(i + 1, n - 1)
    q_spec = pl.BlockSpec((None, tq, qw), lambda s, i: (s, cur(s, i), qcol))
    kv_specs = [pl.BlockSpec((None, tq, kw), (lambda s, i, f=f, c=c: (s, f(s, i), c)))
                for c in (kcol, vcol) for f in (prv, cur, nxt)]
    return q_spec, kv_specs, cur, prv


def _swa_fwd(qkv, *, qcol, kcol, vcol, hq, hkv, w, tq, sink, name):
    nseq, seq_len, _ = qkv.shape
    n = seq_len // tq
    rep = hq // hkv
    q_spec, kv_specs, _, _ = _swa_specs(tq, hq, hkv, n, qcol, kcol, vcol)

    def body(*refs):
        if sink is not None:
            sink_ref, refs = refs[0], refs[1:]
        q_ref, kp_ref, kc_ref, kn_ref, vp_ref, vc_ref, vn_ref, o_ref, lse_ref = refs
        i = pl.program_id(1)
        mask = _window_mask(i, tq, w, seq_len)
        lane = lax.broadcasted_iota(jnp.int32, (tq, LANES), 1)
        lse_acc = jnp.zeros((tq, LANES), F32)
        for g in range(hkv):
            cs = slice(g * HEAD_DIM, (g + 1) * HEAD_DIM)
            kcat = jnp.concatenate([kp_ref[tq - w:, cs], kc_ref[:, cs], kn_ref[:w, cs]], axis=0)
            vcat = jnp.concatenate([vp_ref[tq - w:, cs], vc_ref[:, cs], vn_ref[:w, cs]], axis=0)
            for r in range(rep):
                h = g * rep + r
                hs = slice(h * HEAD_DIM, (h + 1) * HEAD_DIM)
                qh = q_ref[:, hs] * 0.125
                sc = jnp.where(mask, _dot_nt(qh, kcat), NEG_INF)
                m = jnp.max(sc, axis=1, keepdims=True)
                if sink is not None:
                    m = jnp.maximum(m, sink_ref[0, h])
                p = jnp.exp(sc - m)
                den = jnp.sum(p, axis=1, keepdims=True)
                if sink is not None:
                    den = den + jnp.exp(sink_ref[0, h] - m)
                o_ref[:, hs] = _dot(p.astype(BF16), vcat) / den
                lse_acc = jnp.where(lane == h, m + jnp.log(den), lse_acc)
        lse_ref[...] = lse_acc

    in_specs = [q_spec] + kv_specs
    args = [qkv] * 7
    if sink is not None:
        in_specs = [pl.BlockSpec(memory_space=pltpu.SMEM)] + in_specs
        args = [sink] + args
    return pl.pallas_call(
        body, name=name, grid=(nseq, n), in_specs=in_specs,
        out_specs=[pl.BlockSpec((None, tq, hq * HEAD_DIM), lambda s, i: (s, i, 0)),
                   pl.BlockSpec((None, tq, LANES), lambda s, i: (s, i, 0))],
        out_shape=[jax.ShapeDtypeStruct((nseq, seq_len, hq * HEAD_DIM), F32),
                   jax.ShapeDtypeStruct((nseq, seq_len, LANES), F32)],
        compiler_params=_cparams(dimension_semantics=("parallel", "parallel")),
    )(*args)


def _swa_bwd(qkv, do, lse, delta, tabs, *, qcol, kcol, vcol, hq, hkv, w, tq, sink, name):
    nseq, seq_len, _ = qkv.shape
    n = seq_len // tq
    rep = hq // hkv
    qw, kw = hq * HEAD_DIM, hkv * HEAD_DIM
    tk = tq + 2 * w
    q_spec, kv_specs, cur, prv = _swa_specs(tq, hq, hkv, n, qcol, kcol, vcol)

    def body(*refs):
        if sink is not None:
            sink_ref, refs = refs[0], refs[1:]
        (q_ref, kp_ref, kc_ref, kn_ref, vp_ref, vc_ref, vn_ref, do_ref, lse_ref, dl_ref,
         ta_c, tb_c, tc_c, ta_p, tb_p, tc_p) = refs[:16]
        outs = refs[16:]
        if sink is not None:
            dq_ref, dk_ref, dv_ref, dsink_ref, dk_acc, dv_acc = outs
        else:
            dq_ref, dk_ref, dv_ref, dk_acc, dv_acc = outs
        s_id = pl.program_id(0)
        i = pl.program_id(1)
        slot_p, slot_c, slot_n = (i + 2) % 3, i % 3, (i + 1) % 3

        if sink is not None:
            @pl.when((s_id == 0) & (i == 0))
            def _():
                dsink_ref[...] = jnp.zeros_like(dsink_ref)

        @pl.when(i < n)
        def _():
            mask = _window_mask(i, tq, w, seq_len)
            dk_acc[slot_n] = jnp.zeros((tq, kw), F32)
            dv_acc[slot_n] = jnp.zeros((tq, kw), F32)

            @pl.when(i == 0)
            def _():
                dk_acc[slot_c] = jnp.zeros((tq, kw), F32)
                dv_acc[slot_c] = jnp.zeros((tq, kw), F32)

            dq_parts, dk_parts, dv_parts = [], [], []
            for g in range(hkv):
                cs = slice(g * HEAD_DIM, (g + 1) * HEAD_DIM)
                kcat = jnp.concatenate([kp_ref[tq - w:, cs], kc_ref[:, cs], kn_ref[:w, cs]], axis=0)
                vcat = jnp.concatenate([vp_ref[tq - w:, cs], vc_ref[:, cs], vn_ref[:w, cs]], axis=0)
                dkc = jnp.zeros((tk, HEAD_DIM), F32)
                dvc = jnp.zeros((tk, HEAD_DIM), F32)
                for r in range(rep):
                    h = g * rep + r
                    hs = slice(h * HEAD_DIM, (h + 1) * HEAD_DIM)
                    qh = q_ref[:, hs] * 0.125
                    sc = jnp.where(mask, _dot_nt(qh, kcat), NEG_INF)
                    lse_h = lse_ref[:, h:h + 1]
                    dl_h = dl_ref[:, h:h + 1]
                    p = jnp.exp(sc - lse_h)
                    doh = do_ref[:, hs]
                    dp = _dot_nt(doh, vcat)
                    dsb = (p * (dp - dl_h)).astype(BF16)
                    dq_parts.append(_dot(dsb, kcat) * 0.125)
                    dkc = dkc + _dot_tn(dsb, qh)
                    dvc = dvc + _dot_tn(p.astype(BF16), doh)
                    if sink is not None:
                        ds_sink = -jnp.sum(jnp.exp(sink_ref[0, h] - lse_h) * dl_h)
                        dsink_ref[h:h + 1, :] += jnp.full((1, LANES), ds_sink, F32)
                dk_parts.append(dkc)
                dv_parts.append(dvc)
            dq = jnp.concatenate(dq_parts, axis=1)
            dq_ref[...] = _rope(dq, ta_c[...], tb_c[...], tc_c[...], -1.0).astype(BF16)
            dk_all = jnp.concatenate(dk_parts, axis=1)
            dv_all = jnp.concatenate(dv_parts, axis=1)

            @pl.when(i > 0)
            def _():
                dk_acc[slot_p, tq - w:, :] += dk_all[:w]
                dv_acc[slot_p, tq - w:, :] += dv_all[:w]

            dk_acc[slot_c] += dk_all[w:w + tq]
            dv_acc[slot_c] += dv_all[w:w + tq]
            dk_acc[slot_n, :w, :] += dk_all[w + tq:]
            dv_acc[slot_n, :w, :] += dv_all[w + tq:]

        @pl.when(i >= 1)
        def _():
            dk_ref[...] = _rope(dk_acc[slot_p], ta_p[...], tb_p[...], tc_p[...], -1.0).astype(BF16)
            dv_ref[...] = dv_acc[slot_p].astype(BF16)

    row_c = lambda width: pl.BlockSpec((None, tq, width), lambda s, i: (s, cur(s, i), 0))
    row_p = lambda width: pl.BlockSpec((None, tq, width), lambda s, i: (s, jnp.maximum(i - 1, 0), 0))
    in_specs = ([q_spec] + kv_specs + [row_c(qw), row_c(LANES), row_c(LANES)]
                + [row_c(LANES)] * 3 + [row_p(LANES)] * 3)
    args = [qkv] * 7 + [do, lse, delta] + list(tabs) + list(tabs)
    out_specs = [row_c(qw), row_p(kw), row_p(kw)]
    out_shape = [jax.ShapeDtypeStruct((nseq, seq_len, qw), BF16),
                 jax.ShapeDtypeStruct((nseq, seq_len, kw), BF16),
                 jax.ShapeDtypeStruct((nseq, seq_len, kw), BF16)]
    if sink is not None:
        in_specs = [pl.BlockSpec(memory_space=pltpu.SMEM)] + in_specs
        args = [sink] + args
        out_specs.append(pl.BlockSpec((8, LANES), lambda s, i: (0, 0)))
        out_shape.append(jax.ShapeDtypeStruct((8, LANES), F32))
    return pl.pallas_call(
        body, name=name, grid=(nseq, n + 1), in_specs=in_specs, out_specs=out_specs, out_shape=out_shape,
        scratch_shapes=[pltpu.VMEM((3, tq, kw), F32), pltpu.VMEM((3, tq, kw), F32)],
        compiler_params=_cparams(dimension_semantics=("arbitrary", "arbitrary")),
    )(*args)


def _rms_parts(o, g):
    ms = jnp.mean(o * o, axis=-1, keepdims=True) + LN_EPS
    rinv = lax.rsqrt(ms)
    return o * rinv * g, rinv


def _combine_fwd(out_a, o_g, lse_g, g_win, g_dil, *, t):
    s = out_a.shape[0]
    wd = DIL_SLOTS * HEAD_DIM

    def body(oa_ref, o0, o1, o2, l0, l1, l2, gw_ref, gd_ref, mixed_ref, ob_ref, lt_ref):
        ls = [l0[...], l1[...], l2[...]]
        mx = jnp.maximum(jnp.maximum(ls[0], ls[1]), ls[2])
        ws = [jnp.exp(l - mx) for l in ls]
        tot = ws[0] + ws[1] + ws[2]
        lt_ref[...] = mx + jnp.log(tot)
        ws = [x / tot for x in ws]
        parts = []
        for h in range(DIL_SLOTS):
            hs = slice(h * HEAD_DIM, (h + 1) * HEAD_DIM)
            parts.append(ws[0][:, h:h + 1] * o0[:, hs] + ws[1][:, h:h + 1] * o1[:, hs] + ws[2][:, h:h + 1] * o2[:, hs])
        ob = jnp.concatenate(parts, axis=1)
        ob_ref[...] = ob
        na, _ = _rms_parts(oa_ref[...], gw_ref[...])
        nb, _ = _rms_parts(ob, gd_ref[...])
        mixed_ref[:, :wd] = na.astype(BF16)
        mixed_ref[:, wd:] = nb.astype(BF16)

    half = pl.BlockSpec((t, wd), lambda i: (i, 0))
    lanes = pl.BlockSpec((t, LANES), lambda i: (i, 0))
    grow = pl.BlockSpec((1, wd), lambda i: (0, 0))
    return pl.pallas_call(
        body, name="combine_fwd", grid=(s // t,),
        in_specs=[half, half, half, half, lanes, lanes, lanes, grow, grow],
        out_specs=[pl.BlockSpec((t, 2 * wd), lambda i: (i, 0)), half, lanes],
        out_shape=[jax.ShapeDtypeStruct((s, 2 * wd), BF16), jax.ShapeDtypeStruct((s, wd), F32),
                   jax.ShapeDtypeStruct((s, LANES), F32)],
        compiler_params=_cparams(dimension_semantics=("parallel",)),
    )(out_a, *o_g, *lse_g, g_win, g_dil)


def _combine_bwd(dmixed, out_a, out_b, g_win, g_dil, *, t):
    s = out_a.shape[0]
    wd = DIL_SLOTS * HEAD_DIM

    def body(dm_ref, oa_ref, ob_ref, gw_ref, gd_ref, doa_ref, dob_ref, dla_ref, dlb_ref, st_ref):
        i = pl.program_id(0)

        @pl.when(i == 0)
        def _():
            st_ref[...] = jnp.zeros_like(st_ref)

        lane = lax.broadcasted_iota(jnp.int32, (t, LANES), 1)
        for idx, (o_ref, g_ref, do_ref, dl_ref) in enumerate(
                ((oa_ref, gw_ref, doa_ref, dla_ref), (ob_ref, gd_ref, dob_ref, dlb_ref))):
            o = o_ref[...]
            dn = dm_ref[:, idx * wd:(idx + 1) * wd]
            _, rinv = _rms_parts(o, g_ref[...])
            wv = dn * g_ref[...]
            do = rinv * wv - o * (rinv * rinv * rinv) * jnp.mean(wv * o, axis=-1, keepdims=True)
            st_ref[idx:idx + 1, :] += jnp.sum(dn * o * rinv, axis=0, keepdims=True)
            do_ref[...] = do.astype(BF16)
            prod = do * o
            acc = jnp.zeros((t, LANES), F32)
            for h in range(DIL_SLOTS):
                hs = slice(h * HEAD_DIM, (h + 1) * HEAD_DIM)
                acc = jnp.where(lane == h, jnp.sum(prod[:, hs], axis=1, keepdims=True), acc)
            dl_ref[...] = acc

    half = pl.BlockSpec((t, wd), lambda i: (i, 0))
    lanes = pl.BlockSpec((t, LANES), lambda i: (i, 0))
    grow = pl.BlockSpec((1, wd), lambda i: (0, 0))
    return pl.pallas_call(
        body, name="combine_bwd", grid=(s // t,),
        in_specs=[pl.BlockSpec((t, 2 * wd), lambda i: (i, 0)), half, half, grow, grow],
        out_specs=[half, half, lanes, lanes, pl.BlockSpec((8, wd), lambda i: (0, 0))],
        out_shape=[jax.ShapeDtypeStruct((s, wd), BF16), jax.ShapeDtypeStruct((s, wd), BF16),
                   jax.ShapeDtypeStruct((s, LANES), F32), jax.ShapeDtypeStruct((s, LANES), F32),
                   jax.ShapeDtypeStruct((8, wd), F32)],
        compiler_params=_cparams(dimension_semantics=("arbitrary",)),
    )(dmixed, out_a, out_b, g_win, g_dil)


def _mixproj_fwd(mixed_b, w_mix_b, x, ln_in_g, ln_in_b, ln1_g, ln1_b, *, t):
    s = x.shape[0]

    def body(m_ref, w_ref, x_ref, g0, b0, g1, b1, r1_ref, h1_ref):
        h0 = _ln(x_ref[...], g0[...], b0[...])
        r1 = ALPHA * h0 + _dot(m_ref[...], w_ref[...])
        r1_ref[...] = r1
        h1_ref[...] = _ln(r1, g1[...], b1[...]).astype(BF16)

    tile = pl.BlockSpec((t, D_MODEL), lambda i: (i, 0))
    row = pl.BlockSpec((1, D_MODEL), lambda i: (0, 0))
    return pl.pallas_call(
        body, name="mixproj_fwd", grid=(s // t,),
        in_specs=[tile, pl.BlockSpec((D_MODEL, D_MODEL), lambda i: (0, 0)), tile, row, row, row, row],
        out_specs=[tile, tile],
        out_shape=[jax.ShapeDtypeStruct((s, D_MODEL), F32), jax.ShapeDtypeStruct((s, D_MODEL), BF16)],
        compiler_params=_cparams(dimension_semantics=("parallel",)),
    )(mixed_b, w_mix_b, x, ln_in_g, ln_in_b, ln1_g, ln1_b)


def _mem_fwd(mem, g, b, wk_b, wv_b):
    ml = mem.shape[0]

    def body(mem_ref, g_ref, b_ref, wk_ref, wv_ref, mn_ref, kx_ref, vx_ref):
        mn = _ln(mem_ref[...], g_ref[...], b_ref[...]).astype(BF16)
        mn_ref[...] = mn
        kx_ref[...] = _dot(mn, wk_ref[...]).astype(BF16)
        vx_ref[...] = _dot(mn, wv_ref[...]).astype(BF16)

    sh = jax.ShapeDtypeStruct((ml, D_MODEL), BF16)
    return pl.pallas_call(body, name="mem_fwd", out_shape=[sh, sh, sh], compiler_params=_cparams())(
        mem, g, b, wk_b, wv_b)


def _mem_bwd(dkx, dvx, mem, g, b, wk_b, wv_b):
    def body(dk_ref, dv_ref, mem_ref, g_ref, b_ref, wk_ref, wv_ref, dwk_ref, dwv_ref, st_ref):
        mem_v = mem_ref[...]
        mn = _ln(mem_v, g_ref[...], b_ref[...]).astype(BF16)
        dkb = dk_ref[...].astype(BF16)
        dvb = dv_ref[...].astype(BF16)
        dwk_ref[...] = _dot_tn(mn, dkb)
        dwv_ref[...] = _dot_tn(mn, dvb)
        dmn = _dot_nt(dkb, wk_ref[...]) + _dot_nt(dvb, wv_ref[...])
        _, dg, db = _ln_bwd_math(dmn, mem_v, g_ref[...])
        st_ref[...] = jnp.zeros_like(st_ref)
        st_ref[0:1, :] = dg
        st_ref[1:2, :] = db

    sw = jax.ShapeDtypeStruct((D_MODEL, D_MODEL), F32)
    return pl.pallas_call(body, name="mem_bwd", out_shape=[sw, sw, jax.ShapeDtypeStruct((8, D_MODEL), F32)],
                          compiler_params=_cparams())(dkx, dvx, mem, g, b, wk_b, wv_b)


def _xattn_fwd(h1b, r1, kx, vx, wq_b, wo_b, ln1_g, ln1_b, ln2_g, ln2_b, *, t):
    s = h1b.shape[0]
    scale = X_HEAD_DIM ** -0.5

    def body(h_ref, r1_ref, kx_ref, vx_ref, wq_ref, wo_ref, g1, b1, g2, b2, r2_ref, h2_ref, qx_ref, ox_ref, lse_ref):
        qxb = _dot(h_ref[...], wq_ref[...]).astype(BF16)
        qx_ref[...] = qxb
        lane = lax.broadcasted_iota(jnp.int32, (t, LANES), 1)
        lse_acc = jnp.zeros((t, LANES), F32)
        parts = []
        for h in range(X_HEADS):
            hs = slice(h * X_HEAD_DIM, (h + 1) * X_HEAD_DIM)
            sc = _dot_nt(qxb[:, hs] * scale, kx_ref[:, hs])
            m = jnp.max(sc, axis=1, keepdims=True)
            p = jnp.exp(sc - m)
            den = jnp.sum(p, axis=1, keepdims=True)
            parts.append(_dot(p.astype(BF16), vx_ref[:, hs]) / den)
            lse_acc = jnp.where(lane == h, m + jnp.log(den), lse_acc)
        lse_ref[...] = lse_acc
        oxb = jnp.concatenate(parts, axis=1).astype(BF16)
        ox_ref[...] = oxb
        h1 = _ln(r1_ref[...], g1[...], b1[...])
        r2 = ALPHA * h1 + _dot(oxb, wo_ref[...])
        r2_ref[...] = r2
        h2_ref[...] = _ln(r2, g2[...], b2[...]).astype(BF16)

    tile = pl.BlockSpec((t, D_MODEL), lambda i: (i, 0))
    row = pl.BlockSpec((1, D_MODEL), lambda i: (0, 0))
    full = lambda r: pl.BlockSpec((r, D_MODEL), lambda i: (0, 0))
    ml = kx.shape[0]
    bsh = jax.ShapeDtypeStruct((s, D_MODEL), BF16)
    return pl.pallas_call(
        body, name="xattn_fwd", grid=(s // t,),
        in_specs=[tile, tile, full(ml), full(ml), full(D_MODEL), full(D_MODEL), row, row, row, row],
        out_specs=[tile, tile, tile, tile, pl.BlockSpec((t, LANES), lambda i: (i, 0))],
        out_shape=[jax.ShapeDtypeStruct((s, D_MODEL), F32), bsh, bsh, bsh, jax.ShapeDtypeStruct((s, LANES), F32)],
        compiler_params=_cparams(dimension_semantics=("parallel",)),
    )(h1b, r1, kx, vx, wq_b, wo_b, ln1_g, ln1_b, ln2_g, ln2_b)


def _xattn_bwd(dr2, qxb, oxb, lse, kx, vx, wq_b, wo_b, *, t):
    s = dr2.shape[0]
    ml = kx.shape[0]
    scale = X_HEAD_DIM ** -0.5

    def body(dr2_ref, qx_ref, ox_ref, lse_ref, kx_ref, vx_ref, wq_ref, wo_ref, dh1_ref, dqx_ref, dkx_ref, dvx_ref):
        i = pl.program_id(0)

        @pl.when(i == 0)
        def _():
            dkx_ref[...] = jnp.zeros_like(dkx_ref)
            dvx_ref[...] = jnp.zeros_like(dvx_ref)

        dr2v = dr2_ref[...]
        dox = _dot_nt(dr2v.astype(BF16), wo_ref[...])
        parts = []
        for h in range(X_HEADS):
            hs = slice(h * X_HEAD_DIM, (h + 1) * X_HEAD_DIM)
            doh = dox[:, hs]
            dohb = doh.astype(BF16)
            dl = jnp.sum(doh * ox_ref[:, hs].astype(F32), axis=1, keepdims=True)
            qh = qx_ref[:, hs] * scale
            p = jnp.exp(_dot_nt(qh, kx_ref[:, hs]) - lse_ref[:, h:h + 1])
            dp = _dot_nt(dohb, vx_ref[:, hs])
            dsb = (p * (dp - dl)).astype(BF16)
            parts.append(_dot(dsb, kx_ref[:, hs]) * scale)
            dkx_ref[:, hs] += _dot_tn(dsb, qh)
            dvx_ref[:, hs] += _dot_tn(p.astype(BF16), dohb)
        dqxb = jnp.concatenate(parts, axis=1).astype(BF16)
        dqx_ref[...] = dqxb
        dh1_ref[...] = _dot_nt(dqxb, wq_ref[...]) + ALPHA * dr2v

    tile = pl.BlockSpec((t, D_MODEL), lambda i: (i, 0))
    full = lambda r: pl.BlockSpec((r, D_MODEL), lambda i: (0, 0))
    return pl.pallas_call(
        body, name="xattn_bwd", grid=(s // t,),
        in_specs=[tile, tile, tile, pl.BlockSpec((t, LANES), lambda i: (i, 0)), full(ml), full(ml),
                  full(D_MODEL), full(D_MODEL)],
        out_specs=[tile, tile, full(ml), full(ml)],
        out_shape=[jax.ShapeDtypeStruct((s, D_MODEL), F32), jax.ShapeDtypeStruct((s, D_MODEL), BF16),
                   jax.ShapeDtypeStruct((ml, D_MODEL), F32), jax.ShapeDtypeStruct((ml, D_MODEL), F32)],
        compiler_params=_cparams(dimension_semantics=("arbitrary",)),
    )(dr2, qxb, oxb, lse, kx, vx, wq_b, wo_b)


def _halo_specs(t, s, width):
    tb8 = t // 8
    return [pl.BlockSpec((t, width), lambda i: (i, 0)),
            pl.BlockSpec((8, width), lambda i: (jnp.maximum(i * tb8 - 1, 0), 0)),
            pl.BlockSpec((8, width), lambda i: (jnp.minimum((i + 1) * tb8, s // 8 - 1), 0))]


def _halo_rows(i, n, prev_ref, next_ref):
    prev_row = jnp.where(i > 0, prev_ref[7:8, :], 0.0)
    next_row = jnp.where(i < n - 1, next_ref[0:1, :], 0.0)
    return prev_row, next_row


def _gelu_parts(gc):
    cdf = 0.5 * (1.0 + lax.erf(gc * (2.0 ** -0.5)))
    pdf = jnp.exp(-0.5 * gc * gc) * (1.0 / math.sqrt(2.0 * math.pi))
    return gc * cdf, cdf + gc * pdf


def _conv_fwd(g, u, conv_w, conv_b, *, t):
    s = g.shape[0]
    n = s // t

    def body(g_ref, gp_ref, gn_ref, u_ref, cw_ref, cb_ref, o_ref):
        i = pl.program_id(0)
        gv = g_ref[...]
        prev_row, next_row = _halo_rows(i, n, gp_ref, gn_ref)
        gm1, gp1 = _shift_rows(gv, prev_row, next_row)
        gc = gm1 * cw_ref[0:1, :] + gv * cw_ref[1:2, :] + gp1 * cw_ref[2:3, :] + cb_ref[...]
        act, _ = _gelu_parts(gc)
        o_ref[...] = (act * u_ref[...]).astype(BF16)

    tile = pl.BlockSpec((t, D_FF), lambda i: (i, 0))
    return pl.pallas_call(
        body, name="conv_fwd", grid=(n,),
        in_specs=_halo_specs(t, s, D_FF) + [tile, pl.BlockSpec((3, D_FF), lambda i: (0, 0)),
                                            pl.BlockSpec((1, D_FF), lambda i: (0, 0))],
        out_specs=tile, out_shape=jax.ShapeDtypeStruct((s, D_FF), BF16),
        compiler_params=_cparams(dimension_semantics=("parallel",)),
    )(g, g, g, u, conv_w, conv_b)


def _down_ln3(tb, w_down_b, r2, target, ln2_g, ln2_b, ln3_g, ln3_b, *, t):
    s = r2.shape[0]

    def body(t_ref, w_ref, r2_ref, tg_ref, g2, b2, g3, b3, dr_ref, drb_ref, st_ref):
        i = pl.program_id(0)

        @pl.when(i == 0)
        def _():
            st_ref[...] = jnp.zeros_like(st_ref)

        h2 = _ln(r2_ref[...], g2[...], b2[...])
        r3 = ALPHA * h2 + _dot(t_ref[...], w_ref[...])
        y = _ln(r3, g3[...], b3[...])
        err = y - tg_ref[...]
        loss = 0.5 * jnp.sum(jnp.mean(err * err, axis=-1, keepdims=True))
        dy = err * (1.0 / D_MODEL)
        dr, dg, db = _ln_bwd_math(dy, r3, g3[...])
        dr_ref[...] = dr
        drb_ref[...] = dr.astype(BF16)
        st_ref[0:1, :] += dg
        st_ref[1:2, :] += db
        st_ref[2:3, :] += jnp.full((1, D_MODEL), loss, F32)

    tile = pl.BlockSpec((t, D_MODEL), lambda i: (i, 0))
    row = pl.BlockSpec((1, D_MODEL), lambda i: (0, 0))
    return pl.pallas_call(
        body, name="down_ln3", grid=(s // t,),
        in_specs=[pl.BlockSpec((t, D_FF), lambda i: (i, 0)), pl.BlockSpec((D_FF, D_MODEL), lambda i: (0, 0)),
                  tile, tile, row, row, row, row],
        out_specs=[tile, tile, pl.BlockSpec((8, D_MODEL), lambda i: (0, 0))],
        out_shape=[jax.ShapeDtypeStruct((s, D_MODEL), F32), jax.ShapeDtypeStruct((s, D_MODEL), BF16),
                   jax.ShapeDtypeStruct((8, D_MODEL), F32)],
        compiler_params=_cparams(dimension_semantics=("arbitrary",)),
    )(tb, w_down_b, r2, target, ln2_g, ln2_b, ln3_g, ln3_b)


def _conv_bwd_a(dr3b, w_down_b, g, u, conv_w, conv_b, *, t):
    s = g.shape[0]
    n = s // t

    def body(d_ref, w_ref, g_ref, gp_ref, gn_ref, u_ref, cw_ref, cb_ref, du_ref, dgc_ref, st_ref):
        i = pl.program_id(0)

        @pl.when(i == 0)
        def _():
            st_ref[...] = jnp.zeros_like(st_ref)

        dt = _dot_nt(d_ref[...], w_ref[...])
        gv = g_ref[...]
        prev_row, next_row = _halo_rows(i, n, gp_ref, gn_ref)
        gm1, gp1 = _shift_rows(gv, prev_row, next_row)
        gc = gm1 * cw_ref[0:1, :] + gv * cw_ref[1:2, :] + gp1 * cw_ref[2:3, :] + cb_ref[...]
        act, dact = _gelu_parts(gc)
        du_ref[...] = (dt * act).astype(BF16)
        dgc = dt * u_ref[...] * dact
        dgc_ref[...] = dgc
        st_ref[0:1, :] += jnp.sum(gm1 * dgc, axis=0, keepdims=True)
        st_ref[1:2, :] += jnp.sum(gv * dgc, axis=0, keepdims=True)
        st_ref[2:3, :] += jnp.sum(gp1 * dgc, axis=0, keepdims=True)
        st_ref[3:4, :] += jnp.sum(dgc, axis=0, keepdims=True)

    tile = pl.BlockSpec((t, D_FF), lambda i: (i, 0))
    return pl.pallas_call(
        body, name="conv_bwd_a", grid=(n,),
        in_specs=[pl.BlockSpec((t, D_MODEL), lambda i: (i, 0)), pl.BlockSpec((D_FF, D_MODEL), lambda i: (0, 0))]
        + _halo_specs(t, s, D_FF) + [tile, pl.BlockSpec((3, D_FF), lambda i: (0, 0)),
                                     pl.BlockSpec((1, D_FF), lambda i: (0, 0))],
        out_specs=[tile, tile, pl.BlockSpec((8, D_FF), lambda i: (0, 0))],
        out_shape=[jax.ShapeDtypeStruct((s, D_FF), BF16), jax.ShapeDtypeStruct((s, D_FF), F32),
                   jax.ShapeDtypeStruct((8, D_FF), F32)],
        compiler_params=_cparams(dimension_semantics=("arbitrary",)),
    )(dr3b, w_down_b, g, g, g, u, conv_w, conv_b)


def _conv_bwd_b(dgc, conv_w, *, t):
    s = dgc.shape[0]
    n = s // t

    def body(d_ref, dp_ref, dn_ref, cw_ref, o_ref):
        i = pl.program_id(0)
        dv = d_ref[...]
        prev_row, next_row = _halo_rows(i, n, dp_ref, dn_ref)
        dm1, dp1 = _shift_rows(dv, prev_row, next_row)
        o_ref[...] = (dp1 * cw_ref[0:1, :] + dv * cw_ref[1:2, :] + dm1 * cw_ref[2:3, :]).astype(BF16)

    return pl.pallas_call(
        body, name="conv_bwd_b", grid=(n,),
        in_specs=_halo_specs(t, s, D_FF) + [pl.BlockSpec((3, D_FF), lambda i: (0, 0))],
        out_specs=pl.BlockSpec((t, D_FF), lambda i: (i, 0)), out_shape=jax.ShapeDtypeStruct((s, D_FF), BF16),
        compiler_params=_cparams(dimension_semantics=("parallel",)),
    )(dgc, dgc, dgc, conv_w)


def _to_residue(a, dil):
    s, w = a.shape
    return a.reshape(s // dil, dil, w).transpose(1, 0, 2)


def _from_residue(a):
    dil, l, w = a.shape
    return a.transpose(1, 0, 2).reshape(dil * l, w)


def _rope_tables(positions):
    inv_freq = ROPE_THETA ** (-jnp.arange(0, ROT_DIM, 2, dtype=F32) / ROT_DIM)
    ang = positions.astype(F32)[:, None] * inv_freq
    cos, sin = jnp.cos(ang), jnp.sin(ang)
    s = positions.shape[0]
    half = ROT_DIM // 2
    rest = HEAD_DIM - ROT_DIM
    ta = jnp.concatenate([cos, cos, jnp.ones((s, rest), F32)], axis=1)
    tb = jnp.concatenate([-sin, jnp.zeros((s, HEAD_DIM - half), F32)], axis=1)
    tc = jnp.concatenate([jnp.zeros((s, half), F32), sin, jnp.zeros((s, rest), F32)], axis=1)
    return tuple(jnp.tile(x, (1, LANES // HEAD_DIM)) for x in (ta, tb, tc))


def _local_step(x, mem, positions, target, wb, sp, *, t_row=256, t_mm=512, tq_a=256, tq_b=128):
    s = x.shape[0]
    tabs = _rope_tables(positions)

    h0b = _ln_in_fwd(x, sp["ln_in_g"], sp["ln_in_b"], t=t_row)
    za = _proj(h0b, wb["w_in"], tabs, blk_of_j=lambda j: j, kind_of_j=lambda j: jnp.where(j < 2, 1, 2),
               nj=3, dil=1, t=t_mm, name="proj_a")
    zb = [_proj(h0b, wb["w_in"], tabs, blk_of_j=(lambda j, gi=gi: 3 + 6 * (j // 2) + 2 * gi + j % 2),
                kind_of_j=lambda j: jnp.where(j < 4, 1, 0), nj=6, dil=dil, t=t_mm, name=f"proj_b{gi}")
          for gi, dil in enumerate(DILATIONS)]
    out_a, lse_a = _swa_fwd(za, qcol=0, kcol=4, vcol=5, hq=WIN_Q_HEADS, hkv=WIN_KV_HEADS, w=WIN_HALF, tq=tq_a,
                            sink=sp["attn_sink"], name="attn_a_fwd")
    o_g, lse_g = [], []
    for gi in range(3):
        o, l = _swa_fwd(zb[gi], qcol=0, kcol=1, vcol=2, hq=DIL_SLOTS, hkv=DIL_SLOTS, w=DIL_HALF, tq=tq_b,
                        sink=None, name=f"attn_b{gi}_fwd")
        o_g.append(_from_residue(o))
        lse_g.append(_from_residue(l))
    out_a, lse_a = out_a[0], lse_a[0]
    mixed_b, out_b, lse_b = _combine_fwd(out_a, o_g, lse_g, sp["g_win"], sp["g_dil"], t=t_row)
    r1, h1b = _mixproj_fwd(mixed_b, wb["w_mix_out"], x, sp["ln_in_g"], sp["ln_in_b"], sp["ln1_g"], sp["ln1_b"],
                           t=t_row)
    mem_nb, kx, vx = _mem_fwd(mem, sp["mem_ln_g"], sp["mem_ln_b"], wb["w_xk"], wb["w_xv"])
    r2, h2b, qxb, oxb, lse_x = _xattn_fwd(h1b, r1, kx, vx, wb["w_xq"], wb["w_xo"], sp["ln1_g"], sp["ln1_b"],
                                          sp["ln2_g"], sp["ln2_b"], t=t_row)
    g = _mm(h2b, wb["w_gate"], mode="nn", out_dtype=F32, tm=t_mm, tn=D_FF, name="ff_gate")
    u = _mm(h2b, wb["w_up"], mode="nn", out_dtype=F32, tm=t_mm, tn=D_FF, name="ff_up")
    tb = _conv_fwd(g, u, sp["conv_w"], sp["conv_b"], t=t_row)
    dr3, dr3b, st3 = _down_ln3(tb, wb["w_down"], r2, target, sp["ln2_g"], sp["ln2_b"], sp["ln3_g"], sp["ln3_b"],
                               t=t_row)

    grads = {}
    du, dgc, st_conv = _conv_bwd_a(dr3b, wb["w_down"], g, u, sp["conv_w"], sp["conv_b"], t=t_row)
    dg = _conv_bwd_b(dgc, sp["conv_w"], t=t_row)
    tk = min(1024, s)
    grads["w_down"] = _mm(tb, dr3b, mode="tn", out_dtype=BF16, tm=D_FF // 2, tn=D_MODEL, tk=tk, name="dw_down")
    grads["w_gate"] = _mm(h2b, dg, mode="tn", out_dtype=BF16, tm=D_MODEL, tn=D_FF // 2, tk=tk, name="dw_gate")
    grads["w_up"] = _mm(h2b, du, mode="tn", out_dtype=BF16, tm=D_MODEL, tn=D_FF // 2, tk=tk, name="dw_up")
    dh2 = _mm(dg, wb["w_gate"], mode="nt", out_dtype=F32, tm=t_mm, tn=D_MODEL, add=dr3, add_scale=ALPHA,
              name="dh2_gate")
    dh2 = _mm(du, wb["w_up"], mode="nt", out_dtype=F32, tm=t_mm, tn=D_MODEL, add=dh2, name="dh2_up")

    dr2, dr2b, st2 = _ln_bwd(dh2, r2, sp["ln2_g"], t=t_row, name="ln2_bwd", want_bf16=True)
    dh1, dqxb, dkx, dvx = _xattn_bwd(dr2, qxb, oxb, lse_x, kx, vx, wb["w_xq"], wb["w_xo"], t=t_row)
    grads["w_xo"] = _mm(oxb, dr2b, mode="tn", out_dtype=BF16, tm=D_MODEL, tn=D_MODEL, tk=tk, name="dw_xo")
    grads["w_xq"] = _mm(h1b, dqxb, mode="tn", out_dtype=BF16, tm=D_MODEL, tn=D_MODEL, tk=tk, name="dw_xq")
    grads["w_xk"], grads["w_xv"], st_mem = _mem_bwd(dkx, dvx, mem, sp["mem_ln_g"], sp["mem_ln_b"],
                                                    wb["w_xk"], wb["w_xv"])

    dr1, dr1b, st1 = _ln_bwd(dh1, r1, sp["ln1_g"], t=t_row, name="ln1_bwd", want_bf16=True)
    grads["w_mix_out"] = _mm(mixed_b, dr1b, mode="tn", out_dtype=BF16, tm=D_MODEL, tn=D_MODEL, tk=tk,
                             name="dw_mix")
    dmixed = _mm(dr1b, wb["w_mix_out"], mode="nt", out_dtype=F32, tm=t_mm, tn=D_MODEL, name="dmixed")
    do_a, do_b, dl_a, dl_b, st_mix = _combine_bwd(dmixed, out_a, out_b, sp["g_win"], sp["g_dil"], t=t_row)
    dqa, dka, dva, dsink = _swa_bwd(za, do_a[None], lse_a[None], dl_a[None], [x_[None] for x_ in tabs],
                                    qcol=0, kcol=4, vcol=5, hq=WIN_Q_HEADS, hkv=WIN_KV_HEADS, w=WIN_HALF,
                                    tq=tq_a, sink=sp["attn_sink"], name="attn_a_bwd")
    dqs, dks, dvs = [], [], []
    for gi, dil in enumerate(DILATIONS):
        dq, dk, dv = _swa_bwd(zb[gi], _to_residue(do_b, dil), _to_residue(lse_b, dil), _to_residue(dl_b, dil),
                              [_to_residue(x_, dil) for x_ in tabs], qcol=0, kcol=1, vcol=2, hq=DIL_SLOTS,
                              hkv=DIL_SLOTS, w=DIL_HALF, tq=tq_b, sink=None, name=f"attn_b{gi}_bwd")
        dqs.append(_from_residue(dq))
        dks.append(_from_residue(dk))
        dvs.append(_from_residue(dv))
    dz = jnp.concatenate([dqa[0], dka[0], dva[0]] + dqs + dks + dvs, axis=1)
    grads["w_in"] = _mm(h0b, dz, mode="tn", out_dtype=BF16, tm=D_MODEL, tn=IN_WIDTH // 7, tk=tk, name="dw_in")
    dh0 = _mm(dz, wb["w_in"], mode="nt", out_dtype=F32, tm=t_mm, tn=D_MODEL, add=dr1, add_scale=ALPHA, name="dh0")
    grad_x, st0 = _ln_bwd(dh0, x, sp["ln_in_g"], t=t_row, name="ln_in_bwd", want_bf16=False)

    small = {
        "loss": st3[2:3, 0:1],
        "ln_in_g": st0[0:1], "ln_in_b": st0[1:2],
        "attn_sink": dsink[:, 0].reshape(1, WIN_Q_HEADS),
        "g_win": st_mix[0:1], "g_dil": st_mix[1:2],
        "ln1_g": st1[0:1], "ln1_b": st1[1:2],
        "mem_ln_g": st_mem[0:1], "mem_ln_b": st_mem[1:2],
        "ln2_g": st2[0:1], "ln2_b": st2[1:2],
        "conv_w": st_conv[0:3], "conv_b": st_conv[3:4],
        "ln3_g": st3[0:1], "ln3_b": st3[1:2],
    }
    return grad_x, grads, small


MESH_IDS = pl.DeviceIdType.MESH
ANY = pl.BlockSpec(memory_space=pl.ANY)


def _place():
    x, y, c = lax.axis_index("x"), lax.axis_index("y"), lax.axis_index("c")
    other_chips = [(1 - x, y), (x, 1 - y), (1 - x, 1 - y)]
    return x, y, c, other_chips


def _allgather_chips(shards):
    n = len(shards)

    def body(*refs):
        src, dst = refs[:n], refs[n:2 * n]
        send_sems, recv_sems, local_sems = refs[2 * n:]
        x, y, c, chips = _place()
        mine = 2 * x + y
        local = [pltpu.make_async_copy(src[a], dst[a].at[mine], local_sems.at[a]) for a in range(n)]
        for cp in local:
            cp.start()
        sends = []
        for a in range(n):
            for j, (px, py) in enumerate(chips):
                sends.append(pltpu.make_async_remote_copy(
                    src_ref=src[a], dst_ref=dst[a].at[mine], send_sem=send_sems.at[3 * a + j],
                    recv_sem=recv_sems.at[3 * a + j], device_id=(px, py, c), device_id_type=MESH_IDS))
        for cp in sends:
            cp.start()
        for a in range(n):
            for j, (px, py) in enumerate(chips):
                pltpu.make_async_remote_copy(
                    src_ref=src[a], dst_ref=dst[a].at[2 * px + py], send_sem=send_sems.at[3 * a + j],
                    recv_sem=recv_sems.at[3 * a + j], device_id=(px, py, c), device_id_type=MESH_IDS).wait_recv()
        for cp in sends:
            cp.wait_send()
        for cp in local:
            cp.wait()

    return pl.pallas_call(
        body, name="allgather_weights", in_specs=[ANY] * n, out_specs=[ANY] * n,
        out_shape=[jax.ShapeDtypeStruct((N_CHIPS,) + a.shape, a.dtype) for a in shards],
        scratch_shapes=[pltpu.SemaphoreType.DMA((3 * n,)), pltpu.SemaphoreType.DMA((3 * n,)),
                        pltpu.SemaphoreType.DMA((n,))],
    )(*shards)


def _exchange_grads(parts, small):
    n = len(parts)

    def body(*refs):
        src, small_src = refs[:n], refs[n]
        dst, small_dst = refs[n + 1:2 * n + 1], refs[2 * n + 1]
        send_sems, recv_sems, local_sems = refs[2 * n + 2:]
        x, y, c, chips = _place()
        mine = 2 * x + y
        me_dev = 4 * x + 2 * y + c
        local = [pltpu.make_async_copy(src[a].at[mine], dst[a].at[mine], local_sems.at[a]) for a in range(n)]
        local.append(pltpu.make_async_copy(small_src, small_dst.at[me_dev], local_sems.at[n]))
        for cp in local:
            cp.start()
        sends = []
        for a in range(n):
            for j, (px, py) in enumerate(chips):
                sends.append(pltpu.make_async_remote_copy(
                    src_ref=src[a].at[2 * px + py], dst_ref=dst[a].at[mine], send_sem=send_sems.at[3 * a + j],
                    recv_sem=recv_sems.at[3 * a + j], device_id=(px, py, c), device_id_type=MESH_IDS))
        peers = []
        for mask in range(1, N_DEV):
            fx, fy, fc = (mask >> 2) & 1, (mask >> 1) & 1, mask & 1
            peers.append((x ^ fx, y ^ fy, c ^ fc))
        for k, (px, py, pc) in enumerate(peers):
            sends.append(pltpu.make_async_remote_copy(
                src_ref=small_src, dst_ref=small_dst.at[me_dev], send_sem=send_sems.at[3 * n + k],
                recv_sem=recv_sems.at[3 * n + k], device_id=(px, py, pc), device_id_type=MESH_IDS))
        for cp in sends:
            cp.start()
        for a in range(n):
            for j, (px, py) in enumerate(chips):
                pltpu.make_async_remote_copy(
                    src_ref=src[a].at[mine], dst_ref=dst[a].at[2 * px + py], send_sem=send_sems.at[3 * a + j],
                    recv_sem=recv_sems.at[3 * a + j], device_id=(px, py, c), device_id_type=MESH_IDS).wait_recv()
        for k, (px, py, pc) in enumerate(peers):
            pltpu.make_async_remote_copy(
                src_ref=small_src, dst_ref=small_dst.at[4 * px + 2 * py + pc], send_sem=send_sems.at[3 * n + k],
                recv_sem=recv_sems.at[3 * n + k], device_id=(px, py, pc), device_id_type=MESH_IDS).wait_recv()
        for cp in sends:
            cp.wait_send()
        for cp in local:
            cp.wait()

    n_sem = 3 * n + N_DEV - 1
    return pl.pallas_call(
        body, name="exchange_grads", in_specs=[ANY] * (n + 1), out_specs=[ANY] * (n + 1),
        out_shape=[jax.ShapeDtypeStruct(a.shape, a.dtype) for a in parts]
        + [jax.ShapeDtypeStruct((N_DEV,) + small.shape, small.dtype)],
        scratch_shapes=[pltpu.SemaphoreType.DMA((n_sem,)), pltpu.SemaphoreType.DMA((n_sem,)),
                        pltpu.SemaphoreType.DMA((n + 1,))],
    )(*parts, small)


def _swap_sibling(arrays):
    n = len(arrays)

    def body(*refs):
        src, dst = refs[:n], refs[n:2 * n]
        send_sems, recv_sems = refs[2 * n:]
        x, y, c, _ = _place()
        copies = [pltpu.make_async_remote_copy(
            src_ref=src[a], dst_ref=dst[a], send_sem=send_sems.at[a], recv_sem=recv_sems.at[a],
            device_id=(x, y, 1 - c), device_id_type=MESH_IDS) for a in range(n)]
        for cp in copies:
            cp.start()
        for cp in copies:
            cp.wait_recv()
        for cp in copies:
            cp.wait_send()

    return pl.pallas_call(
        body, name="swap_sibling", in_specs=[ANY] * n, out_specs=[ANY] * n,
        out_shape=[jax.ShapeDtypeStruct(a.shape, a.dtype) for a in arrays],
        scratch_shapes=[pltpu.SemaphoreType.DMA((n,)), pltpu.SemaphoreType.DMA((n,))],
    )(*arrays)


def _row_tile(rows, cols, itemsize=4, budget=1 << 20):
    best = None
    for t in range(16, rows + 1, 16):
        if rows % t == 0 and t * cols * itemsize <= budget:
            best = t
    return best or rows


def _sum_slots(stack, *, name):
    n, r, c = stack.shape
    t = _row_tile(r, c)

    def body(s_ref, o_ref):
        acc = s_ref[0].astype(F32)
        for q in range(1, n):
            acc = acc + s_ref[q].astype(F32)
        o_ref[...] = acc

    return pl.pallas_call(
        body, name=name, grid=(r // t,), in_specs=[pl.BlockSpec((n, t, c), lambda i: (0, i, 0))],
        out_specs=pl.BlockSpec((t, c), lambda i: (i, 0)), out_shape=jax.ShapeDtypeStruct((r, c), F32),
        compiler_params=_cparams(dimension_semantics=("parallel",)),
    )(stack)


def _adamw(w, m, v, p, q, *, name):
    r, c = w.shape
    t = _row_tile(r, c, budget=1 << 19)

    def body(*refs):
        if q is None:
            w_ref, m_ref, v_ref, p_ref, g_ref, d_ref, nm_ref, nv_ref = refs
            g = p_ref[...]
        else:
            w_ref, m_ref, v_ref, p_ref, q_ref, g_ref, d_ref, nm_ref, nv_ref = refs
            g = p_ref[...] + q_ref[...]
        nm = ADAM_B1 * m_ref[...] + (1.0 - ADAM_B1) * g
        nv = ADAM_B2 * v_ref[...] + (1.0 - ADAM_B2) * (g * g)
        m_hat = nm / (1.0 - ADAM_B1 ** ADAM_STEP)
        v_hat = nv / (1.0 - ADAM_B2 ** ADAM_STEP)
        g_ref[...] = g
        d_ref[...] = -ADAM_LR * (m_hat / (jnp.sqrt(v_hat) + ADAM_EPS) + ADAM_WD * w_ref[...])
        nm_ref[...] = nm
        nv_ref[...] = nv

    tile = pl.BlockSpec((t, c), lambda i: (i, 0))
    args = [w, m, v, p] + ([] if q is None else [q])
    sh = jax.ShapeDtypeStruct((r, c), F32)
    return pl.pallas_call(
        body, name=name, grid=(r // t,), in_specs=[tile] * len(args), out_specs=[tile] * 4, out_shape=[sh] * 4,
        compiler_params=_cparams(dimension_semantics=("parallel",)),
    )(*args)


BIG = ("w_in", "w_mix_out", "w_xq", "w_xk", "w_xv", "w_xo", "w_gate", "w_up", "w_down")
COL_SHARDED = ("w_in", "w_gate", "w_up")
WEIGHTS = ("ln_in_g", "ln_in_b", "w_in", "attn_sink", "g_win", "g_dil", "w_mix_out", "ln1_g", "ln1_b",
           "mem_ln_g", "mem_ln_b", "w_xq", "w_xk", "w_xv", "w_xo", "ln2_g", "ln2_b", "w_gate", "w_up",
           "conv_w", "conv_b", "w_down", "ln3_g", "ln3_b")
SMALL = tuple(k for k in WEIGHTS if k not in BIG)
PACK_COLS = 1024
CONV_SHARD = D_FF // N_CHIPS
CONV_WIDTH_ROWS = 3
SMALL_ROWS = 32


def _pack_rows(a):
    r, n = a.shape
    per = -(-n // PACK_COLS)
    return jnp.pad(a, ((0, 0), (0, per * PACK_COLS - n))).reshape(r * per, PACK_COLS)


def _unpack_rows(p, r, n):
    per = -(-n // PACK_COLS)
    return p.reshape(r, per * PACK_COLS)[:, :n]


def _pack(pieces, rows_total):
    cat = jnp.concatenate([_pack_rows(a) for a in pieces], axis=0)
    return jnp.pad(cat, ((0, rows_total - cat.shape[0]), (0, 0)))


def _unpack(p, shapes):
    out, at = [], 0
    for r, n in shapes:
        per = -(-n // PACK_COLS)
        out.append(_unpack_rows(p[at:at + r * per], r, n))
        at += r * per
    return out


def kernel(x, mem, positions, ln_in_g, ln_in_b, w_in, attn_sink, g_win, g_dil, w_mix_out, ln1_g, ln1_b, mem_ln_g, mem_ln_b, w_xq, w_xk, w_xv, w_xo, ln2_g, ln2_b, w_gate, w_up, conv_w, conv_b, w_down, ln3_g, ln3_b, loss_target, m_ln_in_g, m_ln_in_b, m_w_in, m_attn_sink, m_g_win, m_g_dil, m_w_mix_out, m_ln1_g, m_ln1_b, m_mem_ln_g, m_mem_ln_b, m_w_xq, m_w_xk, m_w_xv, m_w_xo, m_ln2_g, m_ln2_b, m_w_gate, m_w_up, m_conv_w, m_conv_b, m_w_down, m_ln3_g, m_ln3_b, v_ln_in_g, v_ln_in_b, v_w_in, v_attn_sink, v_g_win, v_g_dil, v_w_mix_out, v_ln1_g, v_ln1_b, v_mem_ln_g, v_mem_ln_b, v_w_xq, v_w_xk, v_w_xv, v_w_xo, v_ln2_g, v_ln2_b, v_w_gate, v_w_up, v_conv_w, v_conv_b, v_w_down, v_ln3_g, v_ln3_b):
    given = dict(locals())
    shape_of = {k: given[k].shape for k in WEIGHTS}
    as2d = lambda a: a.reshape(-1, a.shape[-1])
    w2 = {k: as2d(given[k]) for k in WEIGHTS}
    m2 = {k: as2d(given["m_" + k]) for k in WEIGHTS}
    v2 = {k: as2d(given["v_" + k]) for k in WEIGHTS}
    chip = 2 * lax.axis_index("x") + lax.axis_index("y")

    conv_pack = jnp.pad(w2["conv_w"], ((0, 8 - CONV_WIDTH_ROWS), (0, PACK_COLS - CONV_SHARD)))
    gathered = _allgather_chips([w2[k].astype(BF16) for k in BIG] + [conv_pack])
    wb = {}
    for k, g4 in zip(BIG, gathered):
        if k in COL_SHARDED:
            wb[k] = g4.transpose(1, 0, 2).reshape(g4.shape[1], N_CHIPS * g4.shape[2])
        else:
            wb[k] = g4.reshape(N_CHIPS * g4.shape[1], g4.shape[2])
    conv_full = gathered[-1][:, :CONV_WIDTH_ROWS, :CONV_SHARD].transpose(1, 0, 2).reshape(CONV_WIDTH_ROWS, D_FF)
    sp = {k: w2[k] for k in SMALL}
    sp["conv_w"] = conv_full

    grad_x, grads, small = _local_step(x[0], mem[0], positions[0], loss_target[0], wb, sp)

    parts = []
    for k in BIG:
        gk = grads[k].astype(BF16)
        if k in COL_SHARDED:
            parts.append(gk.reshape(gk.shape[0], N_CHIPS, gk.shape[1] // N_CHIPS).transpose(1, 0, 2))
        else:
            parts.append(gk.reshape(N_CHIPS, gk.shape[0] // N_CHIPS, gk.shape[1]))
    small_keys = ("loss",) + SMALL
    small_shapes = [small[k].shape for k in small_keys]
    small_pack = _pack([small[k] for k in small_keys], SMALL_ROWS)
    *recv, small_all = _exchange_grads(parts, small_pack)
    chip_sums = [_sum_slots(r4, name=f"sum_chips_{k}") for k, r4 in zip(BIG, recv)]
    sibling_sums = _swap_sibling(chip_sums)
    small_sum = _sum_slots(small_all, name="sum_small")
    small_g = dict(zip(small_keys, _unpack(small_sum, small_shapes)))
    loss = small_g["loss"][0, 0]

    res = {}
    for k, p, q in zip(BIG, chip_sums, sibling_sums):
        res[k] = _adamw(w2[k], m2[k], v2[k], p, q, name=f"adamw_{k}")
    small_g["conv_w"] = lax.dynamic_slice_in_dim(small_g["conv_w"], chip * CONV_SHARD, CONV_SHARD, axis=1)
    adam_shapes = [w2[k].shape for k in SMALL]
    packs = [_pack([d[k] for k in SMALL], SMALL_ROWS) for d in (w2, m2, v2, small_g)]
    small_res = [_unpack(o, adam_shapes) for o in _adamw(*packs, None, name="adamw_small")]
    for i, k in enumerate(SMALL):
        res[k] = tuple(o[i] for o in small_res)

    outs = [loss, grad_x[None]]
    for slot in range(4):
        outs += [res[k][slot].reshape(shape_of[k]) for k in WEIGHTS]
    return tuple(outs)
```

```python
import functools
import math

import jax
import jax.numpy as jnp
from jax import lax
from jax.experimental import pallas as pl
from jax.experimental.pallas import tpu as pltpu

F32 = jnp.float32
BF16 = jnp.bfloat16

D_MODEL = 1024
HEAD_DIM = 64
WIN_Q_HEADS = 8
WIN_KV_HEADS = 2
WIN_HALF = 128
DIL_SLOTS = 8
DILATIONS = (1, 4, 16)
DIL_HALF = 64
ROT_DIM = 16
ROPE_THETA = 500000.0
X_HEADS = 4
X_HEAD_DIM = 256
D_FF = 2816
A_Q = 512
A_KV = 128
A_WIDTH = A_Q + 2 * A_KV
B_QKV = 1536
IN_WIDTH = 5376
ALPHA = 2.0 ** 0.25
LN_EPS = 1e-5
NEG_INF = -1e30
LANES = 128
N_CHIPS = 4
N_DEV = 8

ADAM_LR = 0.001
ADAM_B1 = 0.9
ADAM_B2 = 0.999
ADAM_EPS = 1e-08
ADAM_WD = 0.01
ADAM_STEP = 10

VMEM_LIMIT = 56 * 1024 * 1024


def _cparams(**kw):
    return pltpu.CompilerParams(vmem_limit_bytes=VMEM_LIMIT, **kw)


def _dot(a, b):
    return lax.dot_general(a, b, (((1,), (0,)), ((), ())), preferred_element_type=F32)


def _dot_nt(a, b):
    return lax.dot_general(a, b, (((1,), (1,)), ((), ())), preferred_element_type=F32)


def _dot_tn(a, b):
    return lax.dot_general(a, b, (((0,), (0,)), ((), ())), preferred_element_type=F32)


def _ln(x, g, b):
    mu = jnp.mean(x, axis=-1, keepdims=True)
    xc = x - mu
    var = jnp.mean(xc * xc, axis=-1, keepdims=True)
    return xc * lax.rsqrt(var + LN_EPS) * g + b


def _ln_bwd_math(dy, r, g):
    mu = jnp.mean(r, axis=-1, keepdims=True)
    xc = r - mu
    var = jnp.mean(xc * xc, axis=-1, keepdims=True)
    rstd = lax.rsqrt(var + LN_EPS)
    xhat = xc * rstd
    dxhat = dy * g
    m1 = jnp.mean(dxhat, axis=-1, keepdims=True)
    m2 = jnp.mean(dxhat * xhat, axis=-1, keepdims=True)
    dr = rstd * (dxhat - m1 - xhat * m2)
    return dr, jnp.sum(dy * xhat, axis=0, keepdims=True), jnp.sum(dy, axis=0, keepdims=True)


def _rope(z, ta, tb, tc, sign):
    w = z.shape[1]
    reps = w // LANES
    a = jnp.tile(ta, (1, reps))
    b = jnp.tile(tb, (1, reps))
    c = jnp.tile(tc, (1, reps))
    return z * a + sign * (pltpu.roll(z, w - 8, 1) * b + pltpu.roll(z, 8, 1) * c)


def _shift_rows(x, prev_row, next_row):
    t = x.shape[0]
    row = lax.broadcasted_iota(jnp.int32, x.shape, 0)
    xm1 = jnp.where(row == 0, prev_row, pltpu.roll(x, 1, 0))
    xp1 = jnp.where(row == t - 1, next_row, pltpu.roll(x, t - 1, 0))
    return xm1, xp1


def _rope_tabs(cs, e_mat):
    tabs = lax.dot_general(cs, e_mat, (((1,), (0,)), ((), ())), preferred_element_type=F32,
                           precision=lax.Precision.HIGHEST)
    lane = lax.broadcasted_iota(jnp.int32, (cs.shape[0], LANES), 1)
    ones = jnp.where((lane & (HEAD_DIM - 1)) >= ROT_DIM, 1.0, 0.0)
    return tabs[:, :LANES] + ones, tabs[:, LANES:2 * LANES], tabs[:, 2 * LANES:]


def _rope_select_matrix():
    half = ROT_DIM // 2
    e = [[0.0] * (3 * LANES) for _ in range(ROT_DIM)]
    for lane in range(LANES):
        d = lane % HEAD_DIM
        if d < half:
            e[d][lane] = 1.0
            e[half + d][LANES + lane] = -1.0
        elif d < ROT_DIM:
            e[d - half][lane] = 1.0
            e[d][2 * LANES + lane] = 1.0
    return jnp.array(e, F32)


MESH_IDS = pl.DeviceIdType.MESH
ANY = pl.BlockSpec(memory_space=pl.ANY)


def _place():
    x, y, c = lax.axis_index("x"), lax.axis_index("y"), lax.axis_index("c")
    other_chips = [(1 - x, y), (x, 1 - y), (1 - x, 1 - y)]
    return x, y, c, other_chips


class _ChipGather:
    def __init__(self, shards):
        self.inputs = list(shards)
        n = len(shards)
        self.out_shape = [jax.ShapeDtypeStruct((N_CHIPS,) + a.shape, a.dtype) for a in shards]
        self.scratch = [pltpu.SemaphoreType.DMA((3 * n,)), pltpu.SemaphoreType.DMA((3 * n,)),
                        pltpu.SemaphoreType.DMA((n,))]

    def _copies(self, src, dst, sems):
        send_sems, recv_sems, local_sems = sems
        x, y, c, chips = _place()
        mine = 2 * x + y
        local, sends, recvs = [], [], []
        for a in range(len(src)):
            local.append(pltpu.make_async_copy(src[a], dst[a].at[mine], local_sems.at[a]))
            for j, (px, py) in enumerate(chips):
                k = 3 * a + j
                sends.append(pltpu.make_async_remote_copy(
                    src_ref=src[a], dst_ref=dst[a].at[mine], send_sem=send_sems.at[k], recv_sem=recv_sems.at[k],
                    device_id=(px, py, c), device_id_type=MESH_IDS))
                recvs.append(pltpu.make_async_remote_copy(
                    src_ref=src[a], dst_ref=dst[a].at[2 * px + py], send_sem=send_sems.at[k],
                    recv_sem=recv_sems.at[k], device_id=(px, py, c), device_id_type=MESH_IDS))
        return local, sends, recvs

    def start(self, src, dst, sems):
        local, sends, _ = self._copies(src, dst, sems)
        for cp in local + sends:
            cp.start()

    def wait(self, src, dst, sems):
        local, sends, recvs = self._copies(src, dst, sems)
        for cp in recvs:
            cp.wait_recv()
        for cp in sends:
            cp.wait_send()
        for cp in local:
            cp.wait()


class _ChipExchange:
    def __init__(self, parts, small=None):
        self.inputs = list(parts) + ([small] if small is not None else [])
        self.n = len(parts)
        self.has_small = small is not None
        self.out_shape = [jax.ShapeDtypeStruct(a.shape, a.dtype) for a in parts]
        n_sem, n_loc = 3 * self.n, self.n
        if self.has_small:
            self.out_shape.append(jax.ShapeDtypeStruct((N_DEV,) + small.shape, small.dtype))
            n_sem, n_loc = n_sem + N_DEV - 1, n_loc + 1
        self.scratch = [pltpu.SemaphoreType.DMA((n_sem,)), pltpu.SemaphoreType.DMA((n_sem,)),
                        pltpu.SemaphoreType.DMA((n_loc,))]

    def _copies(self, src, dst, sems):
        send_sems, recv_sems, local_sems = sems
        x, y, c, chips = _place()
        mine = 2 * x + y
        n = self.n
        local, sends, recvs = [], [], []
        for a in range(n):
            local.append(pltpu.make_async_copy(src[a].at[mine], dst[a].at[mine], local_sems.at[a]))
            for j, (px, py) in enumerate(chips):
                k = 3 * a + j
                sends.append(pltpu.make_async_remote_copy(
                    src_ref=src[a].at[2 * px + py], dst_ref=dst[a].at[mine], send_sem=send_sems.at[k],
                    recv_sem=recv_sems.at[k], device_id=(px, py, c), device_id_type=MESH_IDS))
                recvs.append(pltpu.make_async_remote_copy(
                    src_ref=src[a].at[mine], dst_ref=dst[a].at[2 * px + py], send_sem=send_sems.at[k],
                    recv_sem=recv_sems.at[k], device_id=(px, py, c), device_id_type=MESH_IDS))
        if self.has_small:
            me_dev = 4 * x + 2 * y + c
            local.append(pltpu.make_async_copy(src[n], dst[n].at[me_dev], local_sems.at[n]))
            for mask in range(1, N_DEV):
                px, py, pc = x ^ ((mask >> 2) & 1), y ^ ((mask >> 1) & 1), c ^ (mask & 1)
                k = 3 * n + mask - 1
                sends.append(pltpu.make_async_remote_copy(
                    src_ref=src[n], dst_ref=dst[n].at[me_dev], send_sem=send_sems.at[k], recv_sem=recv_sems.at[k],
                    device_id=(px, py, pc), device_id_type=MESH_IDS))
                recvs.append(pltpu.make_async_remote_copy(
                    src_ref=src[n], dst_ref=dst[n].at[4 * px + 2 * py + pc], send_sem=send_sems.at[k],
                    recv_sem=recv_sems.at[k], device_id=(px, py, pc), device_id_type=MESH_IDS))
        return local, sends, recvs

    start = _ChipGather.start
    wait = _ChipGather.wait


def _pcall(body, *, name, grid, in_specs, out_specs, out_shape, args, scratch_shapes=(), dims=None, comm=None):
    in_specs, out_specs, out_shape = list(in_specs), list(out_specs), list(out_shape)
    scratch_shapes = list(scratch_shapes)
    if comm is None:
        outs = pl.pallas_call(
            body, name=name, grid=grid, in_specs=in_specs, out_specs=out_specs, out_shape=out_shape,
            scratch_shapes=scratch_shapes, compiler_params=_cparams(dimension_semantics=dims),
        )(*args)
        return list(outs), []
    n_in, n_out, n_scr = len(in_specs), len(out_specs), len(scratch_shapes)
    n_cin, n_cout = len(comm.inputs), len(comm.out_shape)

    def wrapped(*refs):
        ins, refs = refs[:n_in], refs[n_in:]
        cins, refs = refs[:n_cin], refs[n_cin:]
        outs, refs = refs[:n_out], refs[n_out:]
        couts, refs = refs[:n_cout], refs[n_cout:]
        scr, csems = refs[:n_scr], refs[n_scr:]
        first = last = None
        for axis, size in enumerate(grid):
            pid = pl.program_id(axis)
            f, l = pid == 0, pid == size - 1
            first = f if first is None else first & f
            last = l if last is None else last & l

        @pl.when(first)
        def _():
            comm.start(cins, couts, csems)

        body(*ins, *outs, *scr)

        @pl.when(last)
        def _():
            comm.wait(cins, couts, csems)

    res = pl.pallas_call(
        wrapped, name=name, grid=grid, in_specs=in_specs + [ANY] * n_cin, out_specs=out_specs + [ANY] * n_cout,
        out_shape=out_shape + list(comm.out_shape), scratch_shapes=scratch_shapes + list(comm.scratch),
        compiler_params=_cparams(dimension_semantics=("arbitrary",) * len(grid)),
    )(*args, *comm.inputs)
    return list(res[:n_out]), list(res[n_out:])


def _comm_only(comm, name):
    def body(*refs):
        n_cin, n_cout = len(comm.inputs), len(comm.out_shape)
        cins, couts, csems = refs[:n_cin], refs[n_cin:n_cin + n_cout], refs[n_cin + n_cout:]
        comm.start(cins, couts, csems)
        comm.wait(cins, couts, csems)

    return list(pl.pallas_call(
        body, name=name, in_specs=[ANY] * len(comm.inputs), out_specs=[ANY] * len(comm.out_shape),
        out_shape=list(comm.out_shape), scratch_shapes=list(comm.scratch),
    )(*comm.inputs))


def _mm(a, b, *, mode, out_dtype, tm, tn, tk=None, add=None, add_scale=1.0, name):
    if mode in ("nn", "nt"):
        m, k = a.shape
        n = b.shape[1] if mode == "nn" else b.shape[0]
        assert m % tm == 0 and n % tn == 0
        dot = _dot if mode == "nn" else _dot_nt

        def body(*refs):
            if add is None:
                a_ref, b_ref, o_ref = refs
                o_ref[...] = dot(a_ref[...], b_ref[...]).astype(out_dtype)
            else:
                a_ref, b_ref, c_ref, o_ref = refs
                o_ref[...] = (dot(a_ref[...], b_ref[...]) + add_scale * c_ref[...]).astype(out_dtype)

        b_spec = (pl.BlockSpec((k, tn), lambda i, j: (0, j)) if mode == "nn"
                  else pl.BlockSpec((tn, k), lambda i, j: (j, 0)))
        in_specs = [pl.BlockSpec((tm, k), lambda i, j: (i, 0)), b_spec]
        args = [a, b]
        if add is not None:
            in_specs.append(pl.BlockSpec((tm, tn), lambda i, j: (i, j)))
            args.append(add)
        return pl.pallas_call(
            body, name=name, grid=(m // tm, n // tn), in_specs=in_specs,
            out_specs=pl.BlockSpec((tm, tn), lambda i, j: (i, j)),
            out_shape=jax.ShapeDtypeStruct((m, n), out_dtype),
            compiler_params=_cparams(dimension_semantics=("parallel", "parallel")),
        )(*args)
    assert mode == "tn" and add is None
    kk, m = a.shape
    n = b.shape[1]
    assert m % tm == 0 and n % tn == 0 and kk % tk == 0
    nk = kk // tk

    def body(a_ref, b_ref, o_ref, acc_ref):
        kstep = pl.program_id(2)

        @pl.when(kstep == 0)
        def _():
            acc_ref[...] = jnp.zeros_like(acc_ref)

        acc_ref[...] += _dot_tn(a_ref[...], b_ref[...])

        @pl.when(kstep == nk - 1)
        def _():
            o_ref[...] = acc_ref[...].astype(out_dtype)

    return pl.pallas_call(
        body, name=name, grid=(m // tm, n // tn, nk),
        in_specs=[pl.BlockSpec((tk, tm), lambda i, j, s: (s, i)), pl.BlockSpec((tk, tn), lambda i, j, s: (s, j))],
        out_specs=pl.BlockSpec((tm, tn), lambda i, j, s: (i, j)),
        out_shape=jax.ShapeDtypeStruct((m, n), out_dtype),
        scratch_shapes=[pltpu.VMEM((tm, tn), F32)],
        compiler_params=_cparams(dimension_semantics=("parallel", "parallel", "arbitrary")),
    )(a, b)


def _mm2_nt(a1, b1, a2, b2, add, *, add_scale, tm, name, comm=None):
    m, k = a1.shape
    n = b1.shape[0]

    def body(a1_ref, b1_ref, a2_ref, b2_ref, c_ref, o_ref):
        o_ref[...] = (_dot_nt(a1_ref[...], b1_ref[...]) + _dot_nt(a2_ref[...], b2_ref[...])
                      + add_scale * c_ref[...])

    a_spec = pl.BlockSpec((tm, k), lambda i: (i, 0))
    b_spec = pl.BlockSpec((n, k), lambda i: (0, 0))
    o_spec = pl.BlockSpec((tm, n), lambda i: (i, 0))
    outs, couts = _pcall(body, name=name, grid=(m // tm,), in_specs=[a_spec, b_spec, a_spec, b_spec, o_spec],
                         out_specs=[o_spec], out_shape=[jax.ShapeDtypeStruct((m, n), F32)],
                         args=[a1, b1, a2, b2, add], dims=("parallel",), comm=comm)
    return outs[0], couts


def _ln_bwd(dy, r, g, *, t, name, want_bf16):
    s = r.shape[0]

    def body(dy_ref, r_ref, g_ref, *outs):
        i = pl.program_id(0)
        dr, dg, db = _ln_bwd_math(dy_ref[...], r_ref[...], g_ref[...])
        outs[0][...] = dr
        if want_bf16:
            outs[1][...] = dr.astype(BF16)
        st_ref = outs[-1]

        @pl.when(i == 0)
        def _():
            st_ref[...] = jnp.zeros_like(st_ref)

        st_ref[0:1, :] += dg
        st_ref[1:2, :] += db

    tile = pl.BlockSpec((t, D_MODEL), lambda i: (i, 0))
    out_specs = [tile] + ([tile] if want_bf16 else []) + [pl.BlockSpec((8, D_MODEL), lambda i: (0, 0))]
    out_shape = ([jax.ShapeDtypeStruct((s, D_MODEL), F32)]
                 + ([jax.ShapeDtypeStruct((s, D_MODEL), BF16)] if want_bf16 else [])
                 + [jax.ShapeDtypeStruct((8, D_MODEL), F32)])
    return pl.pallas_call(
        body, name=name, grid=(s // t,),
        in_specs=[tile, tile, pl.BlockSpec((1, D_MODEL), lambda i: (0, 0))],
        out_specs=out_specs, out_shape=out_shape,
        compiler_params=_cparams(dimension_semantics=("arbitrary",)),
    )(dy, r, g)


PROJ_COLS = 256
PROJ_SEGMENTS = ((1, 0, (1, 1, 2)),) + tuple(
    (dil, A_WIDTH + gi * B_QKV, (1, 1, 1, 1, 0, 0)) for gi, dil in enumerate(DILATIONS))


def _proj_column_ranges():
    wd = DIL_SLOTS * HEAD_DIM
    ranges = [(0, A_WIDTH)]
    for gi in range(len(DILATIONS)):
        ranges += [(A_WIDTH + part * B_QKV + gi * wd, A_WIDTH + part * B_QKV + (gi + 1) * wd) for part in range(3)]
    return ranges


def _proj_all(x, g, b, w_seg, cs, e_mat, *, t, comm=None):
    s = x.shape[0]
    cb = PROJ_COLS
    halves = cb // LANES

    def body(x_ref, g_ref, b_ref, w_ref, cs_ref, e_ref, h_ref, *rest):
        z_refs, scr = rest[:-1], rest[-1]
        h = _ln(x_ref[...], g_ref[...], b_ref[...]).astype(BF16)
        h_ref[...] = h
        ta, tb, tc = (jnp.tile(tab, (1, halves)) for tab in _rope_tabs(cs_ref[...], e_ref[...]))
        lane = lax.broadcasted_iota(jnp.int32, (t, cb), 1)
        slot = 0
        for z_ref, (dil, col0, kinds) in zip(z_refs, PROJ_SEGMENTS):
            for jb, kind in enumerate(kinds):
                acc = _dot(h, w_ref[:, col0 + cb * jb:col0 + cb * (jb + 1)])
                if kind:
                    z = acc * ta + (pltpu.roll(acc, cb - 8, 1) * tb + pltpu.roll(acc, 8, 1) * tc)
                    if kind == 2:
                        z = jnp.where(lane < LANES, z, acc)
                else:
                    z = acc
                if dil == 1:
                    z_ref[0, :, cb * jb:cb * (jb + 1)] = z.astype(BF16)
                    continue
                for half in range(halves):
                    scr[slot, half] = z[:, half * LANES:(half + 1) * LANES]
                for c in range(dil):
                    for half in range(halves):
                        rows = scr[slot, half, pl.ds(c, t // dil, stride=dil), :]
                        z_ref[c, :, cb * jb + half * LANES:cb * jb + (half + 1) * LANES] = rows.astype(BF16)
                slot = 1 - slot

    row = pl.BlockSpec((1, D_MODEL), lambda i: (0, 0))
    widths = [cb * len(kinds) for _, _, kinds in PROJ_SEGMENTS]
    dils = [dil for dil, _, _ in PROJ_SEGMENTS]
    outs, couts = _pcall(
        body, name="proj_all", grid=(s // t,),
        in_specs=[pl.BlockSpec((t, D_MODEL), lambda i: (i, 0)), row, row,
                  pl.BlockSpec((D_MODEL, IN_WIDTH), lambda i: (0, 0)),
                  pl.BlockSpec((t, ROT_DIM), lambda i: (i, 0)), pl.BlockSpec((ROT_DIM, 3 * LANES), lambda i: (0, 0))],
        out_specs=[pl.BlockSpec((t, D_MODEL), lambda i: (i, 0))]
        + [pl.BlockSpec((dil, t // dil, wd), lambda i: (0, i, 0)) for dil, wd in zip(dils, widths)],
        out_shape=[jax.ShapeDtypeStruct((s, D_MODEL), BF16)]
        + [jax.ShapeDtypeStruct((dil, s // dil, wd), BF16) for dil, wd in zip(dils, widths)],
        args=[x, g, b, w_seg, cs, e_mat], scratch_shapes=[pltpu.VMEM((2, halves, t, LANES), F32)],
        dims=("parallel",), comm=comm)
    return outs, couts


def _window_mask(i, tq, w, seq_len):
    tk = tq + 2 * w
    qpos = i * tq + lax.broadcasted_iota(jnp.int32, (tq, tk), 0)
    kpos = i * tq - w + lax.broadcasted_iota(jnp.int32, (tq, tk), 1)
    return (jnp.abs(qpos - kpos) <= w) & (kpos >= 0) & (kpos < seq_len)


def _swa_specs(tq, hq, hkv, n, qcol, kcol, vcol):
    qw, kw = hq * HEAD_DIM, hkv * HEAD_DIM
    cur = lambda s, i: jnp.minimum(i, n - 1)
    prv = lambda s, i: jnp.maximum(jnp.minimum(i, n - 1) - 1, 0)
    nxt = lambda s, i: jnp.minimum(i + 1, n - 1)
    q_spec = pl.BlockSpec((None, tq, qw), lambda s, i: (s, cur(s, i), qcol))
    kv_specs = [pl.BlockSpec((None, tq, kw), (lambda s, i, f=f, c=c: (s, f(s, i), c)))
                for c in (kcol, vcol) for f in (prv, cur, nxt)]
    return q_spec, kv_specs, cur, prv


def _swa_fwd(qkv, *, qcol, kcol, vcol, hq, hkv, w, tq, sink, name, comm=None):
    nseq, seq_len, _ = qkv.shape
    n = seq_len // tq
    rep = hq // hkv
    q_spec, kv_specs, _, _ = _swa_specs(tq, hq, hkv, n, qcol, kcol, vcol)

    def body(*refs):
        if sink is not None:
            sink_ref, refs = refs[0], refs[1:]
        q_ref, kp_ref, kc_ref, kn_ref, vp_ref, vc_ref, vn_ref, o_ref, lse_ref = refs
        i = pl.program_id(1)
        mask = _window_mask(i, tq, w, seq_len)
        lane = lax.broadcasted_iota(jnp.int32, (tq, LANES), 1)
        lse_acc = jnp.zeros((tq, LANES), F32)
        for g in range(hkv):
            cs = slice(g * HEAD_DIM, (g + 1) * HEAD_DIM)
            kcat = jnp.concatenate([kp_ref[tq - w:, cs], kc_ref[:, cs], kn_ref[:w, cs]], axis=0)
            vcat = jnp.concatenate([vp_ref[tq - w:, cs], vc_ref[:, cs], vn_ref[:w, cs]], axis=0)
            for r in range(rep):
                h = g * rep + r
                hs = slice(h * HEAD_DIM, (h + 1) * HEAD_DIM)
                qh = q_ref[:, hs] * 0.125
                sc = jnp.where(mask, _dot_nt(qh, kcat), NEG_INF)
                m = jnp.max(sc, axis=1, keepdims=True)
                if sink is not None:
                    m = jnp.maximum(m, sink_ref[0, h])
                p = jnp.exp(sc - m)
                den = jnp.sum(p, axis=1, keepdims=True)
                if sink is not None:
                    den = den + jnp.exp(sink_ref[0, h] - m)
                o_ref[:, hs] = _dot(p.astype(BF16), vcat) / den
                lse_acc = jnp.where(lane == h, m + jnp.log(den), lse_acc)
        lse_ref[...] = lse_acc

    in_specs = [q_spec] + kv_specs
    args = [qkv] * 7
    if sink is not None:
        in_specs = [pl.BlockSpec(memory_space=pltpu.SMEM)] + in_specs
        args = [sink] + args
    (o, lse), couts = _pcall(
        body, name=name, grid=(nseq, n), in_specs=in_specs,
        out_specs=[pl.BlockSpec((None, tq, hq * HEAD_DIM), lambda s, i: (s, i, 0)),
                   pl.BlockSpec((None, tq, LANES), lambda s, i: (s, i, 0))],
        out_shape=[jax.ShapeDtypeStruct((nseq, seq_len, hq * HEAD_DIM), F32),
                   jax.ShapeDtypeStruct((nseq, seq_len, LANES), F32)],
        args=args, dims=("parallel", "parallel"), comm=comm)
    return o, lse, couts


def _swa_bwd(qkv, do, lse, delta, cs, e_mat, *, qcol, kcol, vcol, hq, hkv, w, tq, sink, name, comm=None):
    nseq, seq_len, _ = qkv.shape
    n = seq_len // tq
    rep = hq // hkv
    qw, kw = hq * HEAD_DIM, hkv * HEAD_DIM
    tk = tq + 2 * w
    q_spec, kv_specs, cur, prv = _swa_specs(tq, hq, hkv, n, qcol, kcol, vcol)

    def body(*refs):
        if sink is not None:
            sink_ref, refs = refs[0], refs[1:]
        (q_ref, kp_ref, kc_ref, kn_ref, vp_ref, vc_ref, vn_ref, do_ref, lse_ref, dl_ref,
         cs_c, cs_p, e_ref) = refs[:13]
        outs = refs[13:]
        if sink is not None:
            dq_ref, dk_ref, dv_ref, dsink_ref, dk_acc, dv_acc = outs
        else:
            dq_ref, dk_ref, dv_ref, dk_acc, dv_acc = outs
        s_id = pl.program_id(0)
        i = pl.program_id(1)
        slot_p, slot_c, slot_n = (i + 2) % 3, i % 3, (i + 1) % 3

        if sink is not None:
            @pl.when((s_id == 0) & (i == 0))
            def _():
                dsink_ref[...] = jnp.zeros_like(dsink_ref)

        @pl.when(i < n)
        def _():
            mask = _window_mask(i, tq, w, seq_len)
            dk_acc[slot_n] = jnp.zeros((tq, kw), F32)
            dv_acc[slot_n] = jnp.zeros((tq, kw), F32)

            @pl.when(i == 0)
            def _():
                dk_acc[slot_c] = jnp.zeros((tq, kw), F32)
                dv_acc[slot_c] = jnp.zeros((tq, kw), F32)

            dq_parts, dk_parts, dv_parts = [], [], []
            for g in range(hkv):
                cs = slice(g * HEAD_DIM, (g + 1) * HEAD_DIM)
                kcat = jnp.concatenate([kp_ref[tq - w:, cs], kc_ref[:, cs], kn_ref[:w, cs]], axis=0)
                vcat = jnp.concatenate([vp_ref[tq - w:, cs], vc_ref[:, cs], vn_ref[:w, cs]], axis=0)
                dkc = jnp.zeros((tk, HEAD_DIM), F32)
                dvc = jnp.zeros((tk, HEAD_DIM), F32)
                for r in range(rep):
                    h = g * rep + r
                    hs = slice(h * HEAD_DIM, (h + 1) * HEAD_DIM)
                    qh = q_ref[:, hs] * 0.125
                    sc = jnp.where(mask, _dot_nt(qh, kcat), NEG_INF)
                    lse_h = lse_ref[:, h:h + 1]
                    dl_h = dl_ref[:, h:h + 1]
                    p = jnp.exp(sc - lse_h)
                    doh = do_ref[:, hs]
                    dp = _dot_nt(doh, vcat)
                    dsb = (p * (dp - dl_h)).astype(BF16)
                    dq_parts.append(_dot(dsb, kcat) * 0.125)
                    dkc = dkc + _dot_tn(dsb, qh)
                    dvc = dvc + _dot_tn(p.astype(BF16), doh)
                    if sink is not None:
                        ds_sink = -jnp.sum(jnp.exp(sink_ref[0, h] - lse_h) * dl_h)
                        dsink_ref[h:h + 1, :] += jnp.full((1, LANES), ds_sink, F32)
                dk_parts.append(dkc)
                dv_parts.append(dvc)
            dq = jnp.concatenate(dq_parts, axis=1)
            dq_ref[...] = _rope(dq, *_rope_tabs(cs_c[...], e_ref[...]), -1.0).astype(BF16)
            dk_all = jnp.concatenate(dk_parts, axis=1)
            dv_all = jnp.concatenate(dv_parts, axis=1)

            @pl.when(i > 0)
            def _():
                dk_acc[slot_p, tq - w:, :] += dk_all[:w]
                dv_acc[slot_p, tq - w:, :] += dv_all[:w]

            dk_acc[slot_c] += dk_all[w:w + tq]
            dv_acc[slot_c] += dv_all[w:w + tq]
            dk_acc[slot_n, :w, :] += dk_all[w + tq:]
            dv_acc[slot_n, :w, :] += dv_all[w + tq:]

        @pl.when(i >= 1)
        def _():
            dk_ref[...] = _rope(dk_acc[slot_p], *_rope_tabs(cs_p[...], e_ref[...]), -1.0).astype(BF16)
            dv_ref[...] = dv_acc[slot_p].astype(BF16)

    row_c = lambda width: pl.BlockSpec((None, tq, width), lambda s, i: (s, cur(s, i), 0))
    row_p = lambda width: pl.BlockSpec((None, tq, width), lambda s, i: (s, jnp.maximum(i - 1, 0), 0))
    in_specs = ([q_spec] + kv_specs + [row_c(qw), row_c(LANES), row_c(LANES), row_c(ROT_DIM), row_p(ROT_DIM),
                                       pl.BlockSpec((ROT_DIM, 3 * LANES), lambda s, i: (0, 0))])
    args = [qkv] * 7 + [do, lse, delta, cs, cs, e_mat]
    out_specs = [row_c(qw), row_p(kw), row_p(kw)]
    out_shape = [jax.ShapeDtypeStruct((nseq, seq_len, qw), BF16),
                 jax.ShapeDtypeStruct((nseq, seq_len, kw), BF16),
                 jax.ShapeDtypeStruct((nseq, seq_len, kw), BF16)]
    if sink is not None:
        in_specs = [pl.BlockSpec(memory_space=pltpu.SMEM)] + in_specs
        args = [sink] + args
        out_specs.append(pl.BlockSpec((8, LANES), lambda s, i: (0, 0)))
        out_shape.append(jax.ShapeDtypeStruct((8, LANES), F32))
    return _pcall(
        body, name=name, grid=(nseq, n + 1), in_specs=in_specs, out_specs=out_specs, out_shape=out_shape,
        scratch_shapes=[pltpu.VMEM((3, tq, kw), F32), pltpu.VMEM((3, tq, kw), F32)], args=args,
        dims=("arbitrary", "arbitrary"), comm=comm)


def _rms_parts(o, g):
    ms = jnp.mean(o * o, axis=-1, keepdims=True) + LN_EPS
    rinv = lax.rsqrt(ms)
    return o * rinv * g, rinv


def _combine_fwd(out_a, o_g, lse_g, g_win, g_dil, *, t):
    s = out_a.shape[0]
    wd = DIL_SLOTS * HEAD_DIM

    def body(oa_ref, o0, o1, o2, l0, l1, l2, gw_ref, gd_ref, mixed_ref, ob_ref, lt_ref):
        ls = [l0[...], l1[...], l2[...]]
        mx = jnp.maximum(jnp.maximum(ls[0], ls[1]), ls[2])
        ws = [jnp.exp(l - mx) for l in ls]
        tot = ws[0] + ws[1] + ws[2]
        lt_ref[...] = mx + jnp.log(tot)
        ws = [x / tot for x in ws]
        parts = []
        for h in range(DIL_SLOTS):
            hs = slice(h * HEAD_DIM, (h + 1) * HEAD_DIM)
            parts.append(ws[0][:, h:h + 1] * o0[:, hs] + ws[1][:, h:h + 1] * o1[:, hs] + ws[2][:, h:h + 1] * o2[:, hs])
        ob = jnp.concatenate(parts, axis=1)
        ob_ref[...] = ob
        na, _ = _rms_parts(oa_ref[...], gw_ref[...])
        nb, _ = _rms_parts(ob, gd_ref[...])
        mixed_ref[:, :wd] = na.astype(BF16)
        mixed_ref[:, wd:] = nb.astype(BF16)

    half = pl.BlockSpec((t, wd), lambda i: (i, 0))
    lanes = pl.BlockSpec((t, LANES), lambda i: (i, 0))
    grow = pl.BlockSpec((1, wd), lambda i: (0, 0))
    return pl.pallas_call(
        body, name="combine_fwd", grid=(s // t,),
        in_specs=[half, half, half, half, lanes, lanes, lanes, grow, grow],
        out_specs=[pl.BlockSpec((t, 2 * wd), lambda i: (i, 0)), half, lanes],
        out_shape=[jax.ShapeDtypeStruct((s, 2 * wd), BF16), jax.ShapeDtypeStruct((s, wd), F32),
                   jax.ShapeDtypeStruct((s, LANES), F32)],
        compiler_params=_cparams(dimension_semantics=("parallel",)),
    )(out_a, *o_g, *lse_g, g_win, g_dil)


def _combine_bwd(dmixed, out_a, out_b, g_win, g_dil, *, t):
    s = out_a.shape[0]
    wd = DIL_SLOTS * HEAD_DIM

    def body(dm_ref, oa_ref, ob_ref, gw_ref, gd_ref, doa_ref, dob_ref, dla_ref, dlb_ref, st_ref):
        i = pl.program_id(0)

        @pl.when(i == 0)
        def _():
            st_ref[...] = jnp.zeros_like(st_ref)

        lane = lax.broadcasted_iota(jnp.int32, (t, LANES), 1)
        for idx, (o_ref, g_ref, do_ref, dl_ref) in enumerate(
                ((oa_ref, gw_ref, doa_ref, dla_ref), (ob_ref, gd_ref, dob_ref, dlb_ref))):
            o = o_ref[...]
            dn = dm_ref[:, idx * wd:(idx + 1) * wd]
            _, rinv = _rms_parts(o, g_ref[...])
            wv = dn * g_ref[...]
            do = rinv * wv - o * (rinv * rinv * rinv) * jnp.mean(wv * o, axis=-1, keepdims=True)
            st_ref[idx:idx + 1, :] += jnp.sum(dn * o * rinv, axis=0, keepdims=True)
            do_ref[...] = do.astype(BF16)
            prod = do * o
            acc = jnp.zeros((t, LANES), F32)
            for h in range(DIL_SLOTS):
                hs = slice(h * HEAD_DIM, (h + 1) * HEAD_DIM)
                acc = jnp.where(lane == h, jnp.sum(prod[:, hs], axis=1, keepdims=True), acc)
            dl_ref[...] = acc

    half = pl.BlockSpec((t, wd), lambda i: (i, 0))
    lanes = pl.BlockSpec((t, LANES), lambda i: (i, 0))
    grow = pl.BlockSpec((1, wd), lambda i: (0, 0))
    return pl.pallas_call(
        body, name="combine_bwd", grid=(s // t,),
        in_specs=[pl.BlockSpec((t, 2 * wd), lambda i: (i, 0)), half, half, grow, grow],
        out_specs=[half, half, lanes, lanes, pl.BlockSpec((8, wd), lambda i: (0, 0))],
        out_shape=[jax.ShapeDtypeStruct((s, wd), BF16), jax.ShapeDtypeStruct((s, wd), BF16),
                   jax.ShapeDtypeStruct((s, LANES), F32), jax.ShapeDtypeStruct((s, LANES), F32),
                   jax.ShapeDtypeStruct((8, wd), F32)],
        compiler_params=_cparams(dimension_semantics=("arbitrary",)),
    )(dmixed, out_a, out_b, g_win, g_dil)


def _mixproj_fwd(mixed_b, w_mix_b, x, ln_in_g, ln_in_b, ln1_g, ln1_b, *, t):
    s = x.shape[0]

    def body(m_ref, w_ref, x_ref, g0, b0, g1, b1, r1_ref, h1_ref):
        h0 = _ln(x_ref[...], g0[...], b0[...])
        r1 = ALPHA * h0 + _dot(m_ref[...], w_ref[...])
        r1_ref[...] = r1
        h1_ref[...] = _ln(r1, g1[...], b1[...]).astype(BF16)

    tile = pl.BlockSpec((t, D_MODEL), lambda i: (i, 0))
    row = pl.BlockSpec((1, D_MODEL), lambda i: (0, 0))
    return pl.pallas_call(
        body, name="mixproj_fwd", grid=(s // t,),
        in_specs=[tile, pl.BlockSpec((D_MODEL, D_MODEL), lambda i: (0, 0)), tile, row, row, row, row],
        out_specs=[tile, tile],
        out_shape=[jax.ShapeDtypeStruct((s, D_MODEL), F32), jax.ShapeDtypeStruct((s, D_MODEL), BF16)],
        compiler_params=_cparams(dimension_semantics=("parallel",)),
    )(mixed_b, w_mix_b, x, ln_in_g, ln_in_b, ln1_g, ln1_b)


def _mem_fwd(mem, g, b, wk_b, wv_b):
    ml = mem.shape[0]

    def body(mem_ref, g_ref, b_ref, wk_ref, wv_ref, mn_ref, kx_ref, vx_ref):
        mn = _ln(mem_ref[...], g_ref[...], b_ref[...]).astype(BF16)
        mn_ref[...] = mn
        kx_ref[...] = _dot(mn, wk_ref[...]).astype(BF16)
        vx_ref[...] = _dot(mn, wv_ref[...]).astype(BF16)

    sh = jax.ShapeDtypeStruct((ml, D_MODEL), BF16)
    return pl.pallas_call(body, name="mem_fwd", out_shape=[sh, sh, sh], compiler_params=_cparams())(
        mem, g, b, wk_b, wv_b)


def _mem_bwd(dkx, dvx, mem, g, b, wk_b, wv_b):
    def body(dk_ref, dv_ref, mem_ref, g_ref, b_ref, wk_ref, wv_ref, dwk_ref, dwv_ref, st_ref):
        mem_v = mem_ref[...]
        mn = _ln(mem_v, g_ref[...], b_ref[...]).astype(BF16)
        dkb = dk_ref[...].astype(BF16)
        dvb = dv_ref[...].astype(BF16)
        dwk_ref[...] = _dot_tn(mn, dkb)
        dwv_ref[...] = _dot_tn(mn, dvb)
        dmn = _dot_nt(dkb, wk_ref[...]) + _dot_nt(dvb, wv_ref[...])
        _, dg, db = _ln_bwd_math(dmn, mem_v, g_ref[...])
        st_ref[...] = jnp.zeros_like(st_ref)
        st_ref[0:1, :] = dg
        st_ref[1:2, :] = db

    sw = jax.ShapeDtypeStruct((D_MODEL, D_MODEL), F32)
    return pl.pallas_call(body, name="mem_bwd", out_shape=[sw, sw, jax.ShapeDtypeStruct((8, D_MODEL), F32)],
                          compiler_params=_cparams())(dkx, dvx, mem, g, b, wk_b, wv_b)


def _xattn_fwd(h1b, r1, kx, vx, wq_b, wo_b, ln1_g, ln1_b, ln2_g, ln2_b, *, t):
    s = h1b.shape[0]
    scale = X_HEAD_DIM ** -0.5

    def body(h_ref, r1_ref, kx_ref, vx_ref, wq_ref, wo_ref, g1, b1, g2, b2, r2_ref, h2_ref, qx_ref, ox_ref, lse_ref):
        qxb = _dot(h_ref[...], wq_ref[...]).astype(BF16)
        qx_ref[...] = qxb
        lane = lax.broadcasted_iota(jnp.int32, (t, LANES), 1)
        lse_acc = jnp.zeros((t, LANES), F32)
        parts = []
        for h in range(X_HEADS):
            hs = slice(h * X_HEAD_DIM, (h + 1) * X_HEAD_DIM)
            sc = _dot_nt(qxb[:, hs] * scale, kx_ref[:, hs])
            m = jnp.max(sc, axis=1, keepdims=True)
            p = jnp.exp(sc - m)
            den = jnp.sum(p, axis=1, keepdims=True)
            parts.append(_dot(p.astype(BF16), vx_ref[:, hs]) / den)
            lse_acc = jnp.where(lane == h, m + jnp.log(den), lse_acc)
        lse_ref[...] = lse_acc
        oxb = jnp.concatenate(parts, axis=1).astype(BF16)
        ox_ref[...] = oxb
        h1 = _ln(r1_ref[...], g1[...], b1[...])
        r2 = ALPHA * h1 + _dot(oxb, wo_ref[...])
        r2_ref[...] = r2
        h2_ref[...] = _ln(r2, g2[...], b2[...]).astype(BF16)

    tile = pl.BlockSpec((t, D_MODEL), lambda i: (i, 0))
    row = pl.BlockSpec((1, D_MODEL), lambda i: (0, 0))
    full = lambda r: pl.BlockSpec((r, D_MODEL), lambda i: (0, 0))
    ml = kx.shape[0]
    bsh = jax.ShapeDtypeStruct((s, D_MODEL), BF16)
    return pl.pallas_call(
        body, name="xattn_fwd", grid=(s // t,),
        in_specs=[tile, tile, full(ml), full(ml), full(D_MODEL), full(D_MODEL), row, row, row, row],
        out_specs=[tile, tile, tile, tile, pl.BlockSpec((t, LANES), lambda i: (i, 0))],
        out_shape=[jax.ShapeDtypeStruct((s, D_MODEL), F32), bsh, bsh, bsh, jax.ShapeDtypeStruct((s, LANES), F32)],
        compiler_params=_cparams(dimension_semantics=("parallel",)),
    )(h1b, r1, kx, vx, wq_b, wo_b, ln1_g, ln1_b, ln2_g, ln2_b)


def _xattn_bwd(dr2, qxb, oxb, lse, kx, vx, wq_b, wo_b, *, t, comm=None):
    s = dr2.shape[0]
    ml = kx.shape[0]
    scale = X_HEAD_DIM ** -0.5

    def body(dr2_ref, qx_ref, ox_ref, lse_ref, kx_ref, vx_ref, wq_ref, wo_ref, dh1_ref, dqx_ref, dkx_ref, dvx_ref):
        i = pl.program_id(0)

        @pl.when(i == 0)
        def _():
            dkx_ref[...] = jnp.zeros_like(dkx_ref)
            dvx_ref[...] = jnp.zeros_like(dvx_ref)

        dr2v = dr2_ref[...]
        dox = _dot_nt(dr2v.astype(BF16), wo_ref[...])
        parts = []
        for h in range(X_HEADS):
            hs = slice(h * X_HEAD_DIM, (h + 1) * X_HEAD_DIM)
            doh = dox[:, hs]
            dohb = doh.astype(BF16)
            dl = jnp.sum(doh * ox_ref[:, hs].astype(F32), axis=1, keepdims=True)
            qh = qx_ref[:, hs] * scale
            p = jnp.exp(_dot_nt(qh, kx_ref[:, hs]) - lse_ref[:, h:h + 1])
            dp = _dot_nt(dohb, vx_ref[:, hs])
            dsb = (p * (dp - dl)).astype(BF16)
            parts.append(_dot(dsb, kx_ref[:, hs]) * scale)
            dkx_ref[:, hs] += _dot_tn(dsb, qh)
            dvx_ref[:, hs] += _dot_tn(p.astype(BF16), dohb)
        dqxb = jnp.concatenate(parts, axis=1).astype(BF16)
        dqx_ref[...] = dqxb
        dh1_ref[...] = _dot_nt(dqxb, wq_ref[...]) + ALPHA * dr2v

    tile = pl.BlockSpec((t, D_MODEL), lambda i: (i, 0))
    full = lambda r: pl.BlockSpec((r, D_MODEL), lambda i: (0, 0))
    return _pcall(
        body, name="xattn_bwd", grid=(s // t,),
        in_specs=[tile, tile, tile, pl.BlockSpec((t, LANES), lambda i: (i, 0)), full(ml), full(ml),
                  full(D_MODEL), full(D_MODEL)],
        out_specs=[tile, tile, full(ml), full(ml)],
        out_shape=[jax.ShapeDtypeStruct((s, D_MODEL), F32), jax.ShapeDtypeStruct((s, D_MODEL), BF16),
                   jax.ShapeDtypeStruct((ml, D_MODEL), F32), jax.ShapeDtypeStruct((ml, D_MODEL), F32)],
        args=[dr2, qxb, oxb, lse, kx, vx, wq_b, wo_b], dims=("arbitrary",), comm=comm)


def _halo_specs(t, s, width):
    tb8 = t // 8
    return [pl.BlockSpec((t, width), lambda i: (i, 0)),
            pl.BlockSpec((8, width), lambda i: (jnp.maximum(i * tb8 - 1, 0), 0)),
            pl.BlockSpec((8, width), lambda i: (jnp.minimum((i + 1) * tb8, s // 8 - 1), 0))]


def _halo_rows(i, n, prev_ref, next_ref):
    prev_row = jnp.where(i > 0, prev_ref[7:8, :], 0.0)
    next_row = jnp.where(i < n - 1, next_ref[0:1, :], 0.0)
    return prev_row, next_row


def _gelu_parts(gc):
    cdf = 0.5 * (1.0 + lax.erf(gc * (2.0 ** -0.5)))
    pdf = jnp.exp(-0.5 * gc * gc) * (1.0 / math.sqrt(2.0 * math.pi))
    return gc * cdf, cdf + gc * pdf


def _conv_fwd(g, u, conv_w, conv_b, *, t):
    s = g.shape[0]
    n = s // t

    def body(g_ref, gp_ref, gn_ref, u_ref, cw_ref, cb_ref, o_ref):
        i = pl.program_id(0)
        gv = g_ref[...]
        prev_row, next_row = _halo_rows(i, n, gp_ref, gn_ref)
        gm1, gp1 = _shift_rows(gv, prev_row, next_row)
        gc = gm1 * cw_ref[0:1, :] + gv * cw_ref[1:2, :] + gp1 * cw_ref[2:3, :] + cb_ref[...]
        act, _ = _gelu_parts(gc)
        o_ref[...] = (act * u_ref[...]).astype(BF16)

    tile = pl.BlockSpec((t, D_FF), lambda i: (i, 0))
    return pl.pallas_call(
        body, name="conv_fwd", grid=(n,),
        in_specs=_halo_specs(t, s, D_FF) + [tile, pl.BlockSpec((3, D_FF), lambda i: (0, 0)),
                                            pl.BlockSpec((1, D_FF), lambda i: (0, 0))],
        out_specs=tile, out_shape=jax.ShapeDtypeStruct((s, D_FF), BF16),
        compiler_params=_cparams(dimension_semantics=("parallel",)),
    )(g, g, g, u, conv_w, conv_b)


def _down_ln3(tb, w_down_b, r2, target, ln2_g, ln2_b, ln3_g, ln3_b, *, t):
    s = r2.shape[0]

    def body(t_ref, w_ref, r2_ref, tg_ref, g2, b2, g3, b3, dr_ref, drb_ref, st_ref):
        i = pl.program_id(0)

        @pl.when(i == 0)
        def _():
            st_ref[...] = jnp.zeros_like(st_ref)

        h2 = _ln(r2_ref[...], g2[...], b2[...])
        r3 = ALPHA * h2 + _dot(t_ref[...], w_ref[...])
        y = _ln(r3, g3[...], b3[...])
        err = y - tg_ref[...]
        loss = 0.5 * jnp.sum(jnp.mean(err * err, axis=-1, keepdims=True))
        dy = err * (1.0 / D_MODEL)
        dr, dg, db = _ln_bwd_math(dy, r3, g3[...])
        dr_ref[...] = dr
        drb_ref[...] = dr.astype(BF16)
        st_ref[0:1, :] += dg
        st_ref[1:2, :] += db
        st_ref[2:3, :] += jnp.full((1, D_MODEL), loss, F32)

    tile = pl.BlockSpec((t, D_MODEL), lambda i: (i, 0))
    row = pl.BlockSpec((1, D_MODEL), lambda i: (0, 0))
    return pl.pallas_call(
        body, name="down_ln3", grid=(s // t,),
        in_specs=[pl.BlockSpec((t, D_FF), lambda i: (i, 0)), pl.BlockSpec((D_FF, D_MODEL), lambda i: (0, 0)),
                  tile, tile, row, row, row, row],
        out_specs=[tile, tile, pl.BlockSpec((8, D_MODEL), lambda i: (0, 0))],
        out_shape=[jax.ShapeDtypeStruct((s, D_MODEL), F32), jax.ShapeDtypeStruct((s, D_MODEL), BF16),
                   jax.ShapeDtypeStruct((8, D_MODEL), F32)],
        compiler_params=_cparams(dimension_semantics=("arbitrary",)),
    )(tb, w_down_b, r2, target, ln2_g, ln2_b, ln3_g, ln3_b)


def _conv_bwd_a(dr3b, w_down_b, g, u, conv_w, conv_b, *, t):
    s = g.shape[0]
    n = s // t

    def body(d_ref, w_ref, g_ref, gp_ref, gn_ref, u_ref, cw_ref, cb_ref, du_ref, dgc_ref, st_ref):
        i = pl.program_id(0)

        @pl.when(i == 0)
        def _():
            st_ref[...] = jnp.zeros_like(st_ref)

        dt = _dot_nt(d_ref[...], w_ref[...])
        gv = g_ref[...]
        prev_row, next_row = _halo_rows(i, n, gp_ref, gn_ref)
        gm1, gp1 = _shift_rows(gv, prev_row, next_row)
        gc = gm1 * cw_ref[0:1, :] + gv * cw_ref[1:2, :] + gp1 * cw_ref[2:3, :] + cb_ref[...]
        act, dact = _gelu_parts(gc)
        du_ref[...] = (dt * act).astype(BF16)
        dgc = dt * u_ref[...] * dact
        dgc_ref[...] = dgc
        st_ref[0:1, :] += jnp.sum(gm1 * dgc, axis=0, keepdims=True)
        st_ref[1:2, :] += jnp.sum(gv * dgc, axis=0, keepdims=True)
        st_ref[2:3, :] += jnp.sum(gp1 * dgc, axis=0, keepdims=True)
        st_ref[3:4, :] += jnp.sum(dgc, axis=0, keepdims=True)

    tile = pl.BlockSpec((t, D_FF), lambda i: (i, 0))
    return pl.pallas_call(
        body, name="conv_bwd_a", grid=(n,),
        in_specs=[pl.BlockSpec((t, D_MODEL), lambda i: (i, 0)), pl.BlockSpec((D_FF, D_MODEL), lambda i: (0, 0))]
        + _halo_specs(t, s, D_FF) + [tile, pl.BlockSpec((3, D_FF), lambda i: (0, 0)),
                                     pl.BlockSpec((1, D_FF), lambda i: (0, 0))],
        out_specs=[tile, tile, pl.BlockSpec((8, D_FF), lambda i: (0, 0))],
        out_shape=[jax.ShapeDtypeStruct((s, D_FF), BF16), jax.ShapeDtypeStruct((s, D_FF), F32),
                   jax.ShapeDtypeStruct((8, D_FF), F32)],
        compiler_params=_cparams(dimension_semantics=("arbitrary",)),
    )(dr3b, w_down_b, g, g, g, u, conv_w, conv_b)


def _conv_bwd_b(dgc, conv_w, *, t):
    s = dgc.shape[0]
    n = s // t

    def body(d_ref, dp_ref, dn_ref, cw_ref, o_ref):
        i = pl.program_id(0)
        dv = d_ref[...]
        prev_row, next_row = _halo_rows(i, n, dp_ref, dn_ref)
        dm1, dp1 = _shift_rows(dv, prev_row, next_row)
        o_ref[...] = (dp1 * cw_ref[0:1, :] + dv * cw_ref[1:2, :] + dm1 * cw_ref[2:3, :]).astype(BF16)

    return pl.pallas_call(
        body, name="conv_bwd_b", grid=(n,),
        in_specs=_halo_specs(t, s, D_FF) + [pl.BlockSpec((3, D_FF), lambda i: (0, 0))],
        out_specs=pl.BlockSpec((t, D_FF), lambda i: (i, 0)), out_shape=jax.ShapeDtypeStruct((s, D_FF), BF16),
        compiler_params=_cparams(dimension_semantics=("parallel",)),
    )(dgc, dgc, dgc, conv_w)


def _to_residue(a, dil):
    s, w = a.shape
    return a.reshape(s // dil, dil, w).transpose(1, 0, 2)


def _from_residue(a):
    dil, l, w = a.shape
    return a.transpose(1, 0, 2).reshape(dil * l, w)


def _rope_angles(positions):
    inv_freq = ROPE_THETA ** (-jnp.arange(0, ROT_DIM, 2, dtype=F32) / ROT_DIM)
    ang = positions.astype(F32)[:, None] * inv_freq
    return jnp.concatenate([jnp.cos(ang), jnp.sin(ang)], axis=1)


class _NoPlan:
    def gather(self, stage):
        return None

    def gathered(self, stage, couts, wb):
        pass

    def exchange(self, stage, grads):
        return None

    def exchanged(self, stage, couts):
        pass


def _local_step(x, mem, positions, target, wb, sp, plan=None, *, t_row=256, t_mm=512, tq_a=256, tq_b=128):
    s = x.shape[0]
    plan = plan or _NoPlan()
    cs = _rope_angles(positions)
    e_mat = _rope_select_matrix()

    (h0b, za, *zb), couts = _proj_all(x, sp["ln_in_g"], sp["ln_in_b"], wb["w_in_seg"], cs, e_mat, t=t_mm,
                                      comm=plan.gather("proj"))
    plan.gathered("proj", couts, wb)
    out_a, lse_a, couts = _swa_fwd(za, qcol=0, kcol=4, vcol=5, hq=WIN_Q_HEADS, hkv=WIN_KV_HEADS, w=WIN_HALF,
                                   tq=tq_a, sink=sp["attn_sink"], name="attn_a_fwd", comm=plan.gather("attn_a"))
    plan.gathered("attn_a", couts, wb)
    o_g, lse_g = [], []
    for gi in range(3):
        o, l, couts = _swa_fwd(zb[gi], qcol=0, kcol=1, vcol=2, hq=DIL_SLOTS, hkv=DIL_SLOTS, w=DIL_HALF, tq=tq_b,
                               sink=None, name=f"attn_b{gi}_fwd", comm=plan.gather(f"attn_b{gi}"))
        plan.gathered(f"attn_b{gi}", couts, wb)
        o_g.append(_from_residue(o))
        lse_g.append(_from_residue(l))
    out_a, lse_a = out_a[0], lse_a[0]
    mixed_b, out_b, lse_b = _combine_fwd(out_a, o_g, lse_g, sp["g_win"], sp["g_dil"], t=t_row)
    r1, h1b = _mixproj_fwd(mixed_b, wb["w_mix_out"], x, sp["ln_in_g"], sp["ln_in_b"], sp["ln1_g"], sp["ln1_b"],
                           t=t_row)
    mem_nb, kx, vx = _mem_fwd(mem, sp["mem_ln_g"], sp["mem_ln_b"], wb["w_xk"], wb["w_xv"])
    r2, h2b, qxb, oxb, lse_x = _xattn_fwd(h1b, r1, kx, vx, wb["w_xq"], wb["w_xo"], sp["ln1_g"], sp["ln1_b"],
                                          sp["ln2_g"], sp["ln2_b"], t=t_row)
    g = _mm(h2b, wb["w_gate"], mode="nn", out_dtype=F32, tm=t_mm, tn=D_FF, name="ff_gate")
    u = _mm(h2b, wb["w_up"], mode="nn", out_dtype=F32, tm=t_mm, tn=D_FF, name="ff_up")
    tb = _conv_fwd(g, u, sp["conv_w"], sp["conv_b"], t=t_row)
    dr3, dr3b, st3 = _down_ln3(tb, wb["w_down"], r2, target, sp["ln2_g"], sp["ln2_b"], sp["ln3_g"], sp["ln3_b"],
                               t=t_row)

    grads = {}
    du, dgc, st_conv = _conv_bwd_a(dr3b, wb["w_down"], g, u, sp["conv_w"], sp["conv_b"], t=t_row)
    dg = _conv_bwd_b(dgc, sp["conv_w"], t=t_row)
    tk = min(1024, s)
    grads["w_down"] = _mm(tb, dr3b, mode="tn", out_dtype=BF16, tm=D_FF // 2, tn=D_MODEL, tk=tk, name="dw_down")
    grads["w_gate"] = _mm(h2b, dg, mode="tn", out_dtype=BF16, tm=D_MODEL, tn=D_FF // 2, tk=tk, name="dw_gate")
    grads["w_up"] = _mm(h2b, du, mode="tn", out_dtype=BF16, tm=D_MODEL, tn=D_FF // 2, tk=tk, name="dw_up")
    dh2, couts = _mm2_nt(dg, wb["w_gate"], du, wb["w_up"], dr3, add_scale=ALPHA, tm=t_mm, name="dh2",
                         comm=plan.exchange("dh2", grads))
    plan.exchanged("dh2", couts)

    dr2, dr2b, st2 = _ln_bwd(dh2, r2, sp["ln2_g"], t=t_row, name="ln2_bwd", want_bf16=True)
    (dh1, dqxb, dkx, dvx), couts = _xattn_bwd(dr2, qxb, oxb, lse_x, kx, vx, wb["w_xq"], wb["w_xo"], t=t_row,
                                              comm=plan.exchange("xattn", grads))
    plan.exchanged("xattn", couts)
    grads["w_xo"] = _mm(oxb, dr2b, mode="tn", out_dtype=BF16, tm=D_MODEL, tn=D_MODEL, tk=tk, name="dw_xo")
    grads["w_xq"] = _mm(h1b, dqxb, mode="tn", out_dtype=BF16, tm=D_MODEL, tn=D_MODEL, tk=tk, name="dw_xq")
    grads["w_xk"], grads["w_xv"], st_mem = _mem_bwd(dkx, dvx, mem, sp["mem_ln_g"], sp["mem_ln_b"],
                                                    wb["w_xk"], wb["w_xv"])

    dr1, dr1b, st1 = _ln_bwd(dh1, r1, sp["ln1_g"], t=t_row, name="ln1_bwd", want_bf16=True)
    grads["w_mix_out"] = _mm(mixed_b, dr1b, mode="tn", out_dtype=BF16, tm=D_MODEL, tn=D_MODEL, tk=tk,
                             name="dw_mix")
    dmixed = _mm(dr1b, wb["w_mix_out"], mode="nt", out_dtype=F32, tm=t_mm, tn=D_MODEL, name="dmixed")
    do_a, do_b, dl_a, dl_b, st_mix = _combine_bwd(dmixed, out_a, out_b, sp["g_win"], sp["g_dil"], t=t_row)
    (dqa, dka, dva, dsink), couts = _swa_bwd(
        za, do_a[None], lse_a[None], dl_a[None], cs[None], e_mat, qcol=0, kcol=4, vcol=5, hq=WIN_Q_HEADS,
        hkv=WIN_KV_HEADS, w=WIN_HALF, tq=tq_a, sink=sp["attn_sink"], name="attn_a_bwd",
        comm=plan.exchange("attn_a", grads))
    plan.exchanged("attn_a", couts)
    dqs, dks, dvs = [], [], []
    for gi, dil in enumerate(DILATIONS):
        (dq, dk, dv), _ = _swa_bwd(
            zb[gi], _to_residue(do_b, dil), _to_residue(lse_b, dil), _to_residue(dl_b, dil), _to_residue(cs, dil),
            e_mat, qcol=0, kcol=1, vcol=2, hq=DIL_SLOTS, hkv=DIL_SLOTS, w=DIL_HALF, tq=tq_b, sink=None,
            name=f"attn_b{gi}_bwd")
        dqs.append(_from_residue(dq))
        dks.append(_from_residue(dk))
        dvs.append(_from_residue(dv))
    dz = jnp.concatenate([dqa[0], dka[0], dva[0]] + dqs + dks + dvs, axis=1)
    grads["w_in"] = _mm(h0b, dz, mode="tn", out_dtype=BF16, tm=D_MODEL, tn=IN_WIDTH // 7, tk=tk, name="dw_in")
    dh0 = _mm(dz, wb["w_in"], mode="nt", out_dtype=F32, tm=t_mm, tn=D_MODEL, add=dr1, add_scale=ALPHA, name="dh0")
    grad_x, st0 = _ln_bwd(dh0, x, sp["ln_in_g"], t=t_row, name="ln_in_bwd", want_bf16=False)

    small = {
        "loss": st3[2:3, 0:1],
        "ln_in_g": st0[0:1], "ln_in_b": st0[1:2],
        "attn_sink": dsink[:, 0].reshape(1, WIN_Q_HEADS),
        "g_win": st_mix[0:1], "g_dil": st_mix[1:2],
        "ln1_g": st1[0:1], "ln1_b": st1[1:2],
        "mem_ln_g": st_mem[0:1], "mem_ln_b": st_mem[1:2],
        "ln2_g": st2[0:1], "ln2_b": st2[1:2],
        "conv_w": st_conv[0:3], "conv_b": st_conv[3:4],
        "ln3_g": st3[0:1], "ln3_b": st3[1:2],
    }
    return grad_x, grads, small


def _swap_sibling(arrays):
    n = len(arrays)

    def body(*refs):
        src, dst = refs[:n], refs[n:2 * n]
        send_sems, recv_sems = refs[2 * n:]
        x, y, c, _ = _place()
        copies = [pltpu.make_async_remote_copy(
            src_ref=src[a], dst_ref=dst[a], send_sem=send_sems.at[a], recv_sem=recv_sems.at[a],
            device_id=(x, y, 1 - c), device_id_type=MESH_IDS) for a in range(n)]
        for cp in copies:
            cp.start()
        for cp in copies:
            cp.wait_recv()
        for cp in copies:
            cp.wait_send()

    return pl.pallas_call(
        body, name="swap_sibling", in_specs=[ANY] * n, out_specs=[ANY] * n,
        out_shape=[jax.ShapeDtypeStruct(a.shape, a.dtype) for a in arrays],
        scratch_shapes=[pltpu.SemaphoreType.DMA((n,)), pltpu.SemaphoreType.DMA((n,))],
    )(*arrays)


def _row_tile(rows, cols, itemsize=4, budget=1 << 20):
    best = None
    for t in range(16, rows + 1, 16):
        if rows % t == 0 and t * cols * itemsize <= budget:
            best = t
    return best or rows


def _sum_slots(stack, *, name):
    n, r, c = stack.shape
    t = _row_tile(r, c)

    def body(s_ref, o_ref):
        acc = s_ref[0].astype(F32)
        for q in range(1, n):
            acc = acc + s_ref[q].astype(F32)
        o_ref[...] = acc

    return pl.pallas_call(
        body, name=name, grid=(r // t,), in_specs=[pl.BlockSpec((n, t, c), lambda i: (0, i, 0))],
        out_specs=pl.BlockSpec((t, c), lambda i: (i, 0)), out_shape=jax.ShapeDtypeStruct((r, c), F32),
        compiler_params=_cparams(dimension_semantics=("parallel",)),
    )(stack)


def _adamw(w, m, v, p, q, *, name):
    r, c = w.shape
    t = _row_tile(r, c, budget=1 << 19)

    def body(*refs):
        if q is None:
            w_ref, m_ref, v_ref, p_ref, g_ref, d_ref, nm_ref, nv_ref = refs
            g = p_ref[...]
        else:
            w_ref, m_ref, v_ref, p_ref, q_ref, g_ref, d_ref, nm_ref, nv_ref = refs
            g = p_ref[...] + q_ref[...]
        nm = ADAM_B1 * m_ref[...] + (1.0 - ADAM_B1) * g
        nv = ADAM_B2 * v_ref[...] + (1.0 - ADAM_B2) * (g * g)
        m_hat = nm / (1.0 - ADAM_B1 ** ADAM_STEP)
        v_hat = nv / (1.0 - ADAM_B2 ** ADAM_STEP)
        g_ref[...] = g
        d_ref[...] = -ADAM_LR * (m_hat / (jnp.sqrt(v_hat) + ADAM_EPS) + ADAM_WD * w_ref[...])
        nm_ref[...] = nm
        nv_ref[...] = nv

    tile = pl.BlockSpec((t, c), lambda i: (i, 0))
    args = [w, m, v, p] + ([] if q is None else [q])
    sh = jax.ShapeDtypeStruct((r, c), F32)
    return pl.pallas_call(
        body, name=name, grid=(r // t,), in_specs=[tile] * len(args), out_specs=[tile] * 4, out_shape=[sh] * 4,
        compiler_params=_cparams(dimension_semantics=("parallel",)),
    )(*args)


BIG = ("w_in", "w_mix_out", "w_xq", "w_xk", "w_xv", "w_xo", "w_gate", "w_up", "w_down")
COL_SHARDED = ("w_in", "w_gate", "w_up")
WEIGHTS = ("ln_in_g", "ln_in_b", "w_in", "attn_sink", "g_win", "g_dil", "w_mix_out", "ln1_g", "ln1_b",
           "mem_ln_g", "mem_ln_b", "w_xq", "w_xk", "w_xv", "w_xo", "ln2_g", "ln2_b", "w_gate", "w_up",
           "conv_w", "conv_b", "w_down", "ln3_g", "ln3_b")
SMALL = tuple(k for k in WEIGHTS if k not in BIG)
PACK_COLS = 1024
CONV_SHARD = D_FF // N_CHIPS
CONV_WIDTH_ROWS = 3
SMALL_ROWS = 32


GATHER_STAGES = {"proj": ("w_mix_out", "w_xq", "w_xk", "w_xv", "w_xo"), "attn_a": ("w_gate", "w_up"),
                 "attn_b0": ("w_down",)}
EXCHANGE_STAGES = {"dh2": ("w_down",), "xattn": ("w_gate", "w_up"),
                   "attn_a": ("w_xo", "w_xq", "w_xk", "w_xv", "w_mix_out")}


def _full_weight(k, g4):
    if k in COL_SHARDED:
        return g4.transpose(1, 0, 2).reshape(g4.shape[1], N_CHIPS * g4.shape[2])
    return g4.reshape(N_CHIPS * g4.shape[1], g4.shape[2])


def _grad_parts(k, gk):
    gk = gk.astype(BF16)
    if k in COL_SHARDED:
        return gk.reshape(gk.shape[0], N_CHIPS, gk.shape[1] // N_CHIPS).transpose(1, 0, 2)
    return gk.reshape(N_CHIPS, gk.shape[0] // N_CHIPS, gk.shape[1])


class _Plan:
    def __init__(self, shards):
        self.shards = shards
        self.recv = {}

    def gather(self, stage):
        names = GATHER_STAGES.get(stage)
        return _ChipGather([self.shards[k] for k in names]) if names else None

    def gathered(self, stage, couts, wb):
        for k, g4 in zip(GATHER_STAGES.get(stage, ()), couts):
            wb[k] = _full_weight(k, g4)

    def exchange(self, stage, grads):
        names = EXCHANGE_STAGES.get(stage)
        return _ChipExchange([_grad_parts(k, grads[k]) for k in names]) if names else None

    def exchanged(self, stage, couts):
        for k, r4 in zip(EXCHANGE_STAGES.get(stage, ()), couts):
            self.recv[k] = r4


def _pack_rows(a):
    r, n = a.shape
    per = -(-n // PACK_COLS)
    return jnp.pad(a, ((0, 0), (0, per * PACK_COLS - n))).reshape(r * per, PACK_COLS)


def _unpack_rows(p, r, n):
    per = -(-n // PACK_COLS)
    return p.reshape(r, per * PACK_COLS)[:, :n]


def _pack(pieces, rows_total):
    cat = jnp.concatenate([_pack_rows(a) for a in pieces], axis=0)
    return jnp.pad(cat, ((0, rows_total - cat.shape[0]), (0, 0)))


def _unpack(p, shapes):
    out, at = [], 0
    for r, n in shapes:
        per = -(-n // PACK_COLS)
        out.append(_unpack_rows(p[at:at + r * per], r, n))
        at += r * per
    return out


def kernel(x, mem, positions, ln_in_g, ln_in_b, w_in, attn_sink, g_win, g_dil, w_mix_out, ln1_g, ln1_b, mem_ln_g, mem_ln_b, w_xq, w_xk, w_xv, w_xo, ln2_g, ln2_b, w_gate, w_up, conv_w, conv_b, w_down, ln3_g, ln3_b, loss_target, m_ln_in_g, m_ln_in_b, m_w_in, m_attn_sink, m_g_win, m_g_dil, m_w_mix_out, m_ln1_g, m_ln1_b, m_mem_ln_g, m_mem_ln_b, m_w_xq, m_w_xk, m_w_xv, m_w_xo, m_ln2_g, m_ln2_b, m_w_gate, m_w_up, m_conv_w, m_conv_b, m_w_down, m_ln3_g, m_ln3_b, v_ln_in_g, v_ln_in_b, v_w_in, v_attn_sink, v_g_win, v_g_dil, v_w_mix_out, v_ln1_g, v_ln1_b, v_mem_ln_g, v_mem_ln_b, v_w_xq, v_w_xk, v_w_xv, v_w_xo, v_ln2_g, v_ln2_b, v_w_gate, v_w_up, v_conv_w, v_conv_b, v_w_down, v_ln3_g, v_ln3_b):
    given = dict(locals())
    shape_of = {k: given[k].shape for k in WEIGHTS}
    as2d = lambda a: a.reshape(-1, a.shape[-1])
    w2 = {k: as2d(given[k]) for k in WEIGHTS}
    m2 = {k: as2d(given["m_" + k]) for k in WEIGHTS}
    v2 = {k: as2d(given["v_" + k]) for k in WEIGHTS}
    chip = 2 * lax.axis_index("x") + lax.axis_index("y")

    plan = _Plan({k: w2[k].astype(BF16) for k in BIG})
    conv_pack = jnp.pad(w2["conv_w"], ((0, 8 - CONV_WIDTH_ROWS), (0, PACK_COLS - CONV_SHARD)))
    g_in, g_conv = _comm_only(_ChipGather([plan.shards["w_in"], conv_pack]), "gather_w_in")
    w_in_full = _full_weight("w_in", g_in)
    wb = {"w_in": w_in_full,
          "w_in_seg": jnp.concatenate([w_in_full[:, a:b] for a, b in _proj_column_ranges()], axis=1)}
    conv_full = g_conv[:, :CONV_WIDTH_ROWS, :CONV_SHARD].transpose(1, 0, 2).reshape(CONV_WIDTH_ROWS, D_FF)
    sp = {k: w2[k] for k in SMALL}
    sp["conv_w"] = conv_full

    grad_x, grads, small = _local_step(x[0], mem[0], positions[0], loss_target[0], wb, sp, plan)

    small_keys = ("loss",) + SMALL
    small_shapes = [small[k].shape for k in small_keys]
    small_pack = _pack([small[k] for k in small_keys], SMALL_ROWS)
    plan.recv["w_in"], small_all = _comm_only(
        _ChipExchange([_grad_parts("w_in", grads["w_in"])], small_pack), "exchange_w_in")
    chip_sums = [_sum_slots(plan.recv[k], name=f"sum_chips_{k}") for k in BIG]
    sibling_sums = _swap_sibling(chip_sums)
    small_sum = _sum_slots(small_all, name="sum_small")
    small_g = dict(zip(small_keys, _unpack(small_sum, small_shapes)))
    loss = small_g["loss"][0, 0]

    res = {}
    for k, p, q in zip(BIG, chip_sums, sibling_sums):
        res[k] = _adamw(w2[k], m2[k], v2[k], p, q, name=f"adamw_{k}")
    small_g["conv_w"] = lax.dynamic_slice_in_dim(small_g["conv_w"], chip * CONV_SHARD, CONV_SHARD, axis=1)
    adam_shapes = [w2[k].shape for k in SMALL]
    packs = [_pack([d[k] for k in SMALL], SMALL_ROWS) for d in (w2, m2, v2, small_g)]
    small_res = [_unpack(o, adam_shapes) for o in _adamw(*packs, None, name="adamw_small")]
    for i, k in enumerate(SMALL):
        res[k] = tuple(o[i] for o in small_res)

    outs = [loss, grad_x[None]]
    for slot in range(4):
        outs += [res[k][slot].reshape(shape_of[k]) for k in WEIGHTS]
    return tuple(outs)
```

```python
import functools
import math

import jax
import jax.numpy as jnp
from jax import lax
from jax.experimental import pallas as pl
from jax.experimental.pallas import tpu as pltpu

F32 = jnp.float32
BF16 = jnp.bfloat16

D_MODEL = 1024
HEAD_DIM = 64
WIN_Q_HEADS = 8
WIN_KV_HEADS = 2
WIN_HALF = 128
DIL_SLOTS = 8
DILATIONS = (1, 4, 16)
DIL_HALF = 64
ROT_DIM = 16
ROPE_THETA = 500000.0
X_HEADS = 4
X_HEAD_DIM = 256
D_FF = 2816
A_Q = 512
A_KV = 128
A_WIDTH = A_Q + 2 * A_KV
B_QKV = 1536
IN_WIDTH = 5376
ALPHA = 2.0 ** 0.25
LN_EPS = 1e-5
NEG_INF = -1e30
LANES = 128
N_CHIPS = 4
N_DEV = 8

ADAM_LR = 0.001
ADAM_B1 = 0.9
ADAM_B2 = 0.999
ADAM_EPS = 1e-08
ADAM_WD = 0.01
ADAM_STEP = 10

VMEM_LIMIT = 56 * 1024 * 1024


def _cparams(**kw):
    return pltpu.CompilerParams(vmem_limit_bytes=VMEM_LIMIT, **kw)


def _dot(a, b):
    return lax.dot_general(a, b, (((1,), (0,)), ((), ())), preferred_element_type=F32)


def _dot_nt(a, b):
    return lax.dot_general(a, b, (((1,), (1,)), ((), ())), preferred_element_type=F32)


def _dot_tn(a, b):
    return lax.dot_general(a, b, (((0,), (0,)), ((), ())), preferred_element_type=F32)


def _ln(x, g, b):
    mu = jnp.mean(x, axis=-1, keepdims=True)
    xc = x - mu
    var = jnp.mean(xc * xc, axis=-1, keepdims=True)
    return xc * lax.rsqrt(var + LN_EPS) * g + b


def _ln_bwd_math(dy, r, g):
    mu = jnp.mean(r, axis=-1, keepdims=True)
    xc = r - mu
    var = jnp.mean(xc * xc, axis=-1, keepdims=True)
    rstd = lax.rsqrt(var + LN_EPS)
    xhat = xc * rstd
    dxhat = dy * g
    m1 = jnp.mean(dxhat, axis=-1, keepdims=True)
    m2 = jnp.mean(dxhat * xhat, axis=-1, keepdims=True)
    dr = rstd * (dxhat - m1 - xhat * m2)
    return dr, jnp.sum(dy * xhat, axis=0, keepdims=True), jnp.sum(dy, axis=0, keepdims=True)


def _rope(z, ta, tb, tc, sign):
    w = z.shape[1]
    reps = w // LANES
    a = jnp.tile(ta, (1, reps))
    b = jnp.tile(tb, (1, reps))
    c = jnp.tile(tc, (1, reps))
    return z * a + sign * (pltpu.roll(z, w - 8, 1) * b + pltpu.roll(z, 8, 1) * c)


def _shift_rows(x, prev_row, next_row):
    t = x.shape[0]
    row = lax.broadcasted_iota(jnp.int32, x.shape, 0)
    xm1 = jnp.where(row == 0, prev_row, pltpu.roll(x, 1, 0))
    xp1 = jnp.where(row == t - 1, next_row, pltpu.roll(x, t - 1, 0))
    return xm1, xp1


def _rope_tabs(cs, e_mat):
    tabs = lax.dot_general(cs, e_mat, (((1,), (0,)), ((), ())), preferred_element_type=F32,
                           precision=lax.Precision.HIGHEST)
    lane = lax.broadcasted_iota(jnp.int32, (cs.shape[0], LANES), 1)
    ones = jnp.where((lane & (HEAD_DIM - 1)) >= ROT_DIM, 1.0, 0.0)
    return tabs[:, :LANES] + ones, tabs[:, LANES:2 * LANES], tabs[:, 2 * LANES:]


def _rope_select_matrix():
    half = ROT_DIM // 2
    e = [[0.0] * (3 * LANES) for _ in range(ROT_DIM)]
    for lane in range(LANES):
        d = lane % HEAD_DIM
        if d < half:
            e[d][lane] = 1.0
            e[half + d][LANES + lane] = -1.0
        elif d < ROT_DIM:
            e[d - half][lane] = 1.0
            e[d][2 * LANES + lane] = 1.0
    return jnp.array(e, F32)


MESH_IDS = pl.DeviceIdType.MESH
ANY = pl.BlockSpec(memory_space=pl.ANY)


def _place():
    x, y, c = lax.axis_index("x"), lax.axis_index("y"), lax.axis_index("c")
    other_chips = [(1 - x, y), (x, 1 - y), (1 - x, 1 - y)]
    return x, y, c, other_chips


class _ChipGather:
    def __init__(self, shards):
        self.inputs = list(shards)
        n = len(shards)
        self.out_shape = [jax.ShapeDtypeStruct((N_CHIPS,) + a.shape, a.dtype) for a in shards]
        self.scratch = [pltpu.SemaphoreType.DMA((3 * n,)), pltpu.SemaphoreType.DMA((3 * n,)),
                        pltpu.SemaphoreType.DMA((n,))]

    def _copies(self, src, dst, sems):
        send_sems, recv_sems, local_sems = sems
        x, y, c, chips = _place()
        mine = 2 * x + y
        local, sends, recvs = [], [], []
        for a in range(len(src)):
            local.append(pltpu.make_async_copy(src[a], dst[a].at[mine], local_sems.at[a]))
            for j, (px, py) in enumerate(chips):
                k = 3 * a + j
                sends.append(pltpu.make_async_remote_copy(
                    src_ref=src[a], dst_ref=dst[a].at[mine], send_sem=send_sems.at[k], recv_sem=recv_sems.at[k],
                    device_id=(px, py, c), device_id_type=MESH_IDS))
                recvs.append(pltpu.make_async_remote_copy(
                    src_ref=src[a], dst_ref=dst[a].at[2 * px + py], send_sem=send_sems.at[k],
                    recv_sem=recv_sems.at[k], device_id=(px, py, c), device_id_type=MESH_IDS))
        return local, sends, recvs

    def start(self, src, dst, sems):
        local, sends, _ = self._copies(src, dst, sems)
        for cp in local + sends:
            cp.start()

    def wait(self, src, dst, sems):
        local, sends, recvs = self._copies(src, dst, sems)
        for cp in recvs:
            cp.wait_recv()
        for cp in sends:
            cp.wait_send()
        for cp in local:
            cp.wait()


class _ChipExchange:
    def __init__(self, parts, small=None):
        self.inputs = list(parts) + ([small] if small is not None else [])
        self.n = len(parts)
        self.has_small = small is not None
        self.out_shape = [jax.ShapeDtypeStruct(a.shape, a.dtype) for a in parts]
        n_sem, n_loc = 3 * self.n, self.n
        if self.has_small:
            self.out_shape.append(jax.ShapeDtypeStruct((N_DEV,) + small.shape, small.dtype))
            n_sem, n_loc = n_sem + N_DEV - 1, n_loc + 1
        self.scratch = [pltpu.SemaphoreType.DMA((n_sem,)), pltpu.SemaphoreType.DMA((n_sem,)),
                        pltpu.SemaphoreType.DMA((n_loc,))]

    def _copies(self, src, dst, sems):
        send_sems, recv_sems, local_sems = sems
        x, y, c, chips = _place()
        mine = 2 * x + y
        n = self.n
        local, sends, recvs = [], [], []
        for a in range(n):
            local.append(pltpu.make_async_copy(src[a].at[mine], dst[a].at[mine], local_sems.at[a]))
            for j, (px, py) in enumerate(chips):
                k = 3 * a + j
                sends.append(pltpu.make_async_remote_copy(
                    src_ref=src[a].at[2 * px + py], dst_ref=dst[a].at[mine], send_sem=send_sems.at[k],
                    recv_sem=recv_sems.at[k], device_id=(px, py, c), device_id_type=MESH_IDS))
                recvs.append(pltpu.make_async_remote_copy(
                    src_ref=src[a].at[mine], dst_ref=dst[a].at[2 * px + py], send_sem=send_sems.at[k],
                    recv_sem=recv_sems.at[k], device_id=(px, py, c), device_id_type=MESH_IDS))
        if self.has_small:
            me_dev = 4 * x + 2 * y + c
            local.append(pltpu.make_async_copy(src[n], dst[n].at[me_dev], local_sems.at[n]))
            for mask in range(1, N_DEV):
                px, py, pc = x ^ ((mask >> 2) & 1), y ^ ((mask >> 1) & 1), c ^ (mask & 1)
                k = 3 * n + mask - 1
                sends.append(pltpu.make_async_remote_copy(
                    src_ref=src[n], dst_ref=dst[n].at[me_dev], send_sem=send_sems.at[k], recv_sem=recv_sems.at[k],
                    device_id=(px, py, pc), device_id_type=MESH_IDS))
                recvs.append(pltpu.make_async_remote_copy(
                    src_ref=src[n], dst_ref=dst[n].at[4 * px + 2 * py + pc], send_sem=send_sems.at[k],
                    recv_sem=recv_sems.at[k], device_id=(px, py, pc), device_id_type=MESH_IDS))
        return local, sends, recvs

    start = _ChipGather.start
    wait = _ChipGather.wait


def _pcall(body, *, name, grid, in_specs, out_specs, out_shape, args, scratch_shapes=(), dims=None, comm=None):
    in_specs, out_specs, out_shape = list(in_specs), list(out_specs), list(out_shape)
    scratch_shapes = list(scratch_shapes)
    if comm is None:
        outs = pl.pallas_call(
            body, name=name, grid=grid, in_specs=in_specs, out_specs=out_specs, out_shape=out_shape,
            scratch_shapes=scratch_shapes, compiler_params=_cparams(dimension_semantics=dims),
        )(*args)
        return list(outs), []
    n_in, n_out, n_scr = len(in_specs), len(out_specs), len(scratch_shapes)
    n_cin, n_cout = len(comm.inputs), len(comm.out_shape)

    def wrapped(*refs):
        ins, refs = refs[:n_in], refs[n_in:]
        cins, refs = refs[:n_cin], refs[n_cin:]
        outs, refs = refs[:n_out], refs[n_out:]
        couts, refs = refs[:n_cout], refs[n_cout:]
        scr, csems = refs[:n_scr], refs[n_scr:]
        first = last = None
        for axis, size in enumerate(grid):
            pid = pl.program_id(axis)
            f, l = pid == 0, pid == size - 1
            first = f if first is None else first & f
            last = l if last is None else last & l

        @pl.when(first)
        def _():
            comm.start(cins, couts, csems)

        body(*ins, *outs, *scr)

        @pl.when(last)
        def _():
            comm.wait(cins, couts, csems)

    res = pl.pallas_call(
        wrapped, name=name, grid=grid, in_specs=in_specs + [ANY] * n_cin, out_specs=out_specs + [ANY] * n_cout,
        out_shape=out_shape + list(comm.out_shape), scratch_shapes=scratch_shapes + list(comm.scratch),
        compiler_params=_cparams(dimension_semantics=("arbitrary",) * len(grid)),
    )(*args, *comm.inputs)
    return list(res[:n_out]), list(res[n_out:])


def _comm_only(comm, name):
    def body(*refs):
        n_cin, n_cout = len(comm.inputs), len(comm.out_shape)
        cins, couts, csems = refs[:n_cin], refs[n_cin:n_cin + n_cout], refs[n_cin + n_cout:]
        comm.start(cins, couts, csems)
        comm.wait(cins, couts, csems)

    return list(pl.pallas_call(
        body, name=name, in_specs=[ANY] * len(comm.inputs), out_specs=[ANY] * len(comm.out_shape),
        out_shape=list(comm.out_shape), scratch_shapes=list(comm.scratch),
    )(*comm.inputs))


def _mm(a, b, *, mode, out_dtype, tm, tn, tk=None, add=None, add_scale=1.0, name):
    if mode in ("nn", "nt"):
        m, k = a.shape
        n = b.shape[1] if mode == "nn" else b.shape[0]
        assert m % tm == 0 and n % tn == 0
        dot = _dot if mode == "nn" else _dot_nt

        def body(*refs):
            if add is None:
                a_ref, b_ref, o_ref = refs
                o_ref[...] = dot(a_ref[...], b_ref[...]).astype(out_dtype)
            else:
                a_ref, b_ref, c_ref, o_ref = refs
                o_ref[...] = (dot(a_ref[...], b_ref[...]) + add_scale * c_ref[...]).astype(out_dtype)

        b_spec = (pl.BlockSpec((k, tn), lambda i, j: (0, j)) if mode == "nn"
                  else pl.BlockSpec((tn, k), lambda i, j: (j, 0)))
        in_specs = [pl.BlockSpec((tm, k), lambda i, j: (i, 0)), b_spec]
        args = [a, b]
        if add is not None:
            in_specs.append(pl.BlockSpec((tm, tn), lambda i, j: (i, j)))
            args.append(add)
        return pl.pallas_call(
            body, name=name, grid=(m // tm, n // tn), in_specs=in_specs,
            out_specs=pl.BlockSpec((tm, tn), lambda i, j: (i, j)),
            out_shape=jax.ShapeDtypeStruct((m, n), out_dtype),
            compiler_params=_cparams(dimension_semantics=("parallel", "parallel")),
        )(*args)
    assert mode == "tn" and add is None
    kk, m = a.shape
    n = b.shape[1]
    assert m % tm == 0 and n % tn == 0 and kk % tk == 0
    nk = kk // tk

    def body(a_ref, b_ref, o_ref, acc_ref):
        kstep = pl.program_id(2)

        @pl.when(kstep == 0)
        def _():
            acc_ref[...] = jnp.zeros_like(acc_ref)

        acc_ref[...] += _dot_tn(a_ref[...], b_ref[...])

        @pl.when(kstep == nk - 1)
        def _():
            o_ref[...] = acc_ref[...].astype(out_dtype)

    return pl.pallas_call(
        body, name=name, grid=(m // tm, n // tn, nk),
        in_specs=[pl.BlockSpec((tk, tm), lambda i, j, s: (s, i)), pl.BlockSpec((tk, tn), lambda i, j, s: (s, j))],
        out_specs=pl.BlockSpec((tm, tn), lambda i, j, s: (i, j)),
        out_shape=jax.ShapeDtypeStruct((m, n), out_dtype),
        scratch_shapes=[pltpu.VMEM((tm, tn), F32)],
        compiler_params=_cparams(dimension_semantics=("parallel", "parallel", "arbitrary")),
    )(a, b)


def _mm2_nt(a1, b1, a2, b2, add, *, add_scale, tm, name, comm=None):
    m, k = a1.shape
    n = b1.shape[0]

    def body(a1_ref, b1_ref, a2_ref, b2_ref, c_ref, o_ref):
        o_ref[...] = (_dot_nt(a1_ref[...], b1_ref[...]) + _dot_nt(a2_ref[...], b2_ref[...])
                      + add_scale * c_ref[...])

    a_spec = pl.BlockSpec((tm, k), lambda i: (i, 0))
    b_spec = pl.BlockSpec((n, k), lambda i: (0, 0))
    o_spec = pl.BlockSpec((tm, n), lambda i: (i, 0))
    outs, couts = _pcall(body, name=name, grid=(m // tm,), in_specs=[a_spec, b_spec, a_spec, b_spec, o_spec],
                         out_specs=[o_spec], out_shape=[jax.ShapeDtypeStruct((m, n), F32)],
                         args=[a1, b1, a2, b2, add], dims=("parallel",), comm=comm)
    return outs[0], couts


def _ln_bwd(dy, r, g, *, t, name, want_bf16):
    s = r.shape[0]

    def body(dy_ref, r_ref, g_ref, *outs):
        i = pl.program_id(0)
        dr, dg, db = _ln_bwd_math(dy_ref[...], r_ref[...], g_ref[...])
        outs[0][...] = dr
        if want_bf16:
            outs[1][...] = dr.astype(BF16)
        st_ref = outs[-1]

        @pl.when(i == 0)
        def _():
            st_ref[...] = jnp.zeros_like(st_ref)

        st_ref[0:1, :] += dg
        st_ref[1:2, :] += db

    tile = pl.BlockSpec((t, D_MODEL), lambda i: (i, 0))
    out_specs = [tile] + ([tile] if want_bf16 else []) + [pl.BlockSpec((8, D_MODEL), lambda i: (0, 0))]
    out_shape = ([jax.ShapeDtypeStruct((s, D_MODEL), F32)]
                 + ([jax.ShapeDtypeStruct((s, D_MODEL), BF16)] if want_bf16 else [])
                 + [jax.ShapeDtypeStruct((8, D_MODEL), F32)])
    return pl.pallas_call(
        body, name=name, grid=(s // t,),
        in_specs=[tile, tile, pl.BlockSpec((1, D_MODEL), lambda i: (0, 0))],
        out_specs=out_specs, out_shape=out_shape,
        compiler_params=_cparams(dimension_semantics=("arbitrary",)),
    )(dy, r, g)


PROJ_COLS = 256
PROJ_SEGMENTS = ((1, 0, (1, 1, 2)),) + tuple(
    (dil, A_WIDTH + gi * B_QKV, (1, 1, 1, 1, 0, 0)) for gi, dil in enumerate(DILATIONS))


def _proj_column_ranges():
    wd = DIL_SLOTS * HEAD_DIM
    ranges = [(0, A_WIDTH)]
    for gi in range(len(DILATIONS)):
        ranges += [(A_WIDTH + part * B_QKV + gi * wd, A_WIDTH + part * B_QKV + (gi + 1) * wd) for part in range(3)]
    return ranges


def _proj_all(x, g, b, w_seg, cs, e_mat, *, t, comm=None):
    s = x.shape[0]
    cb = PROJ_COLS
    halves = cb // LANES

    def body(x_ref, g_ref, b_ref, w_ref, cs_ref, e_ref, h_ref, *rest):
        z_refs, scr = rest[:-1], rest[-1]
        h = _ln(x_ref[...], g_ref[...], b_ref[...]).astype(BF16)
        h_ref[...] = h
        ta, tb, tc = (jnp.tile(tab, (1, halves)) for tab in _rope_tabs(cs_ref[...], e_ref[...]))
        lane = lax.broadcasted_iota(jnp.int32, (t, cb), 1)
        slot = 0
        for z_ref, (dil, col0, kinds) in zip(z_refs, PROJ_SEGMENTS):
            for jb, kind in enumerate(kinds):
                acc = _dot(h, w_ref[:, col0 + cb * jb:col0 + cb * (jb + 1)])
                if kind:
                    z = acc * ta + (pltpu.roll(acc, cb - 8, 1) * tb + pltpu.roll(acc, 8, 1) * tc)
                    if kind == 2:
                        z = jnp.where(lane < LANES, z, acc)
                else:
                    z = acc
                if dil == 1:
                    z_ref[0, :, cb * jb:cb * (jb + 1)] = z.astype(BF16)
                    continue
                for half in range(halves):
                    scr[slot, half] = z[:, half * LANES:(half + 1) * LANES]
                for c in range(dil):
                    for half in range(halves):
                        rows = scr[slot, half, pl.ds(c, t // dil, stride=dil), :]
                        z_ref[c, :, cb * jb + half * LANES:cb * jb + (half + 1) * LANES] = rows.astype(BF16)
                slot = 1 - slot

    row = pl.BlockSpec((1, D_MODEL), lambda i: (0, 0))
    widths = [cb * len(kinds) for _, _, kinds in PROJ_SEGMENTS]
    dils = [dil for dil, _, _ in PROJ_SEGMENTS]
    outs, couts = _pcall(
        body, name="proj_all", grid=(s // t,),
        in_specs=[pl.BlockSpec((t, D_MODEL), lambda i: (i, 0)), row, row,
                  pl.BlockSpec((D_MODEL, IN_WIDTH), lambda i: (0, 0)),
                  pl.BlockSpec((t, ROT_DIM), lambda i: (i, 0)), pl.BlockSpec((ROT_DIM, 3 * LANES), lambda i: (0, 0))],
        out_specs=[pl.BlockSpec((t, D_MODEL), lambda i: (i, 0))]
        + [pl.BlockSpec((dil, t // dil, wd), lambda i: (0, i, 0)) for dil, wd in zip(dils, widths)],
        out_shape=[jax.ShapeDtypeStruct((s, D_MODEL), BF16)]
        + [jax.ShapeDtypeStruct((dil, s // dil, wd), BF16) for dil, wd in zip(dils, widths)],
        args=[x, g, b, w_seg, cs, e_mat], scratch_shapes=[pltpu.VMEM((2, halves, t, LANES), F32)],
        dims=("parallel",), comm=comm)
    return outs, couts


def _window_mask(i, tq, w, seq_len):
    tk = tq + 2 * w
    qpos = i * tq + lax.broadcasted_iota(jnp.int32, (tq, tk), 0)
    kpos = i * tq - w + lax.broadcasted_iota(jnp.int32, (tq, tk), 1)
    return (jnp.abs(qpos - kpos) <= w) & (kpos >= 0) & (kpos < seq_len)


def _swa_specs(tq, hq, hkv, n, qcol, kcol, vcol):
    qw, kw = hq * HEAD_DIM, hkv * HEAD_DIM
    cur = lambda s, i: jnp.minimum(i, n - 1)
    prv = lambda s, i: jnp.maximum(jnp.minimum(i, n - 1) - 1, 0)
    nxt = lambda s, i: jnp.minimum(i + 1, n - 1)
    q_spec = pl.BlockSpec((None, tq, qw), lambda s, i: (s, cur(s, i), qcol))
    kv_specs = [pl.BlockSpec((None, tq, kw), (lambda s, i, f=f, c=c: (s, f(s, i), c)))
                for c in (kcol, vcol) for f in (prv, cur, nxt)]
    return q_spec, kv_specs, cur, prv


def _swa_fwd(qkv, *, qcol, kcol, vcol, hq, hkv, w, tq, sink, name, comm=None):
    nseq, seq_len, _ = qkv.shape
    n = seq_len // tq
    rep = hq // hkv
    q_spec, kv_specs, _, _ = _swa_specs(tq, hq, hkv, n, qcol, kcol, vcol)

    def body(*refs):
        if sink is not None:
            sink_ref, refs = refs[0], refs[1:]
        q_ref, kp_ref, kc_ref, kn_ref, vp_ref, vc_ref, vn_ref, o_ref, lse_ref = refs
        i = pl.program_id(1)
        mask = _window_mask(i, tq, w, seq_len)
        lane = lax.broadcasted_iota(jnp.int32, (tq, LANES), 1)
        lse_acc = jnp.zeros((tq, LANES), F32)
        for g in range(hkv):
            cs = slice(g * HEAD_DIM, (g + 1) * HEAD_DIM)
            kcat = jnp.concatenate([kp_ref[tq - w:, cs], kc_ref[:, cs], kn_ref[:w, cs]], axis=0)
            vcat = jnp.concatenate([vp_ref[tq - w:, cs], vc_ref[:, cs], vn_ref[:w, cs]], axis=0)
            for r in range(rep):
                h = g * rep + r
                hs = slice(h * HEAD_DIM, (h + 1) * HEAD_DIM)
                qh = q_ref[:, hs] * 0.125
                sc = jnp.where(mask, _dot_nt(qh, kcat), NEG_INF)
                m = jnp.max(sc, axis=1, keepdims=True)
                if sink is not None:
                    m = jnp.maximum(m, sink_ref[0, h])
                p = jnp.exp(sc - m)
                den = jnp.sum(p, axis=1, keepdims=True)
                if sink is not None:
                    den = den + jnp.exp(sink_ref[0, h] - m)
                o_ref[:, hs] = _dot(p.astype(BF16), vcat) / den
                lse_acc = jnp.where(lane == h, m + jnp.log(den), lse_acc)
        lse_ref[...] = lse_acc

    in_specs = [q_spec] + kv_specs
    args = [qkv] * 7
    if sink is not None:
        in_specs = [pl.BlockSpec(memory_space=pltpu.SMEM)] + in_specs
        args = [sink] + args
    (o, lse), couts = _pcall(
        body, name=name, grid=(nseq, n), in_specs=in_specs,
        out_specs=[pl.BlockSpec((None, tq, hq * HEAD_DIM), lambda s, i: (s, i, 0)),
                   pl.BlockSpec((None, tq, LANES), lambda s, i: (s, i, 0))],
        out_shape=[jax.ShapeDtypeStruct((nseq, seq_len, hq * HEAD_DIM), F32),
                   jax.ShapeDtypeStruct((nseq, seq_len, LANES), F32)],
        args=args, dims=("parallel", "parallel"), comm=comm)
    return o, lse, couts


def _swa_bwd(qkv, do, lse, delta, cs, e_mat, *, qcol, kcol, vcol, hq, hkv, w, tq, sink, name, comm=None):
    nseq, seq_len, _ = qkv.shape
    n = seq_len // tq
    rep = hq // hkv
    qw, kw = hq * HEAD_DIM, hkv * HEAD_DIM
    tk = tq + 2 * w
    q_spec, kv_specs, cur, prv = _swa_specs(tq, hq, hkv, n, qcol, kcol, vcol)

    def body(*refs):
        if sink is not None:
            sink_ref, refs = refs[0], refs[1:]
        (q_ref, kp_ref, kc_ref, kn_ref, vp_ref, vc_ref, vn_ref, do_ref, lse_ref, dl_ref,
         cs_c, cs_p, e_ref) = refs[:13]
        outs = refs[13:]
        if sink is not None:
            dq_ref, dk_ref, dv_ref, dsink_ref, dk_acc, dv_acc = outs
        else:
            dq_ref, dk_ref, dv_ref, dk_acc, dv_acc = outs
        s_id = pl.program_id(0)
        i = pl.program_id(1)
        slot_p, slot_c, slot_n = (i + 2) % 3, i % 3, (i + 1) % 3

        if sink is not None:
            @pl.when((s_id == 0) & (i == 0))
            def _():
                dsink_ref[...] = jnp.zeros_like(dsink_ref)

        @pl.when(i < n)
        def _():
            mask = _window_mask(i, tq, w, seq_len)
            dk_acc[slot_n] = jnp.zeros((tq, kw), F32)
            dv_acc[slot_n] = jnp.zeros((tq, kw), F32)

            @pl.when(i == 0)
            def _():
                dk_acc[slot_c] = jnp.zeros((tq, kw), F32)
                dv_acc[slot_c] = jnp.zeros((tq, kw), F32)

            dq_parts, dk_parts, dv_parts = [], [], []
            for g in range(hkv):
                cs = slice(g * HEAD_DIM, (g + 1) * HEAD_DIM)
                kcat = jnp.concatenate([kp_ref[tq - w:, cs], kc_ref[:, cs], kn_ref[:w, cs]], axis=0)
                vcat = jnp.concatenate([vp_ref[tq - w:, cs], vc_ref[:, cs], vn_ref[:w, cs]], axis=0)
                dkc = jnp.zeros((tk, HEAD_DIM), F32)
                dvc = jnp.zeros((tk, HEAD_DIM), F32)
                for r in range(rep):
                    h = g * rep + r
                    hs = slice(h * HEAD_DIM, (h + 1) * HEAD_DIM)
                    qh = q_ref[:, hs] * 0.125
                    sc = jnp.where(mask, _dot_nt(qh, kcat), NEG_INF)
                    lse_h = lse_ref[:, h:h + 1]
                    dl_h = dl_ref[:, h:h + 1]
                    p = jnp.exp(sc - lse_h)
                    doh = do_ref[:, hs]
                    dp = _dot_nt(doh, vcat)
                    dsb = (p * (dp - dl_h)).astype(BF16)
                    dq_parts.append(_dot(dsb, kcat) * 0.125)
                    dkc = dkc + _dot_tn(dsb, qh)
                    dvc = dvc + _dot_tn(p.astype(BF16), doh)
                    if sink is not None:
                        ds_sink = -jnp.sum(jnp.exp(sink_ref[0, h] - lse_h) * dl_h)
                        dsink_ref[h:h + 1, :] += jnp.full((1, LANES), ds_sink, F32)
                dk_parts.append(dkc)
                dv_parts.append(dvc)
            dq = jnp.concatenate(dq_parts, axis=1)
            dq_ref[...] = _rope(dq, *_rope_tabs(cs_c[...], e_ref[...]), -1.0).astype(BF16)
            dk_all = jnp.concatenate(dk_parts, axis=1)
            dv_all = jnp.concatenate(dv_parts, axis=1)

            @pl.when(i > 0)
            def _():
                dk_acc[slot_p, tq - w:, :] += dk_all[:w]
                dv_acc[slot_p, tq - w:, :] += dv_all[:w]

            dk_acc[slot_c] += dk_all[w:w + tq]
            dv_acc[slot_c] += dv_all[w:w + tq]
            dk_acc[slot_n, :w, :] += dk_all[w + tq:]
            dv_acc[slot_n, :w, :] += dv_all[w + tq:]

        @pl.when(i >= 1)
        def _():
            dk_ref[...] = _rope(dk_acc[slot_p], *_rope_tabs(cs_p[...], e_ref[...]), -1.0).astype(BF16)
            dv_ref[...] = dv_acc[slot_p].astype(BF16)

    row_c = lambda width: pl.BlockSpec((None, tq, width), lambda s, i: (s, cur(s, i), 0))
    row_p = lambda width: pl.BlockSpec((None, tq, width), lambda s, i: (s, jnp.maximum(i - 1, 0), 0))
    in_specs = ([q_spec] + kv_specs + [row_c(qw), row_c(LANES), row_c(LANES), row_c(ROT_DIM), row_p(ROT_DIM),
                                       pl.BlockSpec((ROT_DIM, 3 * LANES), lambda s, i: (0, 0))])
    args = [qkv] * 7 + [do, lse, delta, cs, cs, e_mat]
    out_specs = [row_c(qw), row_p(kw), row_p(kw)]
    out_shape = [jax.ShapeDtypeStruct((nseq, seq_len, qw), BF16),
                 jax.ShapeDtypeStruct((nseq, seq_len, kw), BF16),
                 jax.ShapeDtypeStruct((nseq, seq_len, kw), BF16)]
    if sink is not None:
        in_specs = [pl.BlockSpec(memory_space=pltpu.SMEM)] + in_specs
        args = [sink] + args
        out_specs.append(pl.BlockSpec((8, LANES), lambda s, i: (0, 0)))
        out_shape.append(jax.ShapeDtypeStruct((8, LANES), F32))
    return _pcall(
        body, name=name, grid=(nseq, n + 1), in_specs=in_specs, out_specs=out_specs, out_shape=out_shape,
        scratch_shapes=[pltpu.VMEM((3, tq, kw), F32), pltpu.VMEM((3, tq, kw), F32)], args=args,
        dims=("arbitrary", "arbitrary"), comm=comm)


PAIR = 2 * HEAD_DIM


def _window_mask_t(i, tq, w, seq_len):
    tk = tq + 2 * w
    kpos = i * tq - w + lax.broadcasted_iota(jnp.int32, (tk, tq), 0)
    qpos = i * tq + lax.broadcasted_iota(jnp.int32, (tk, tq), 1)
    return (jnp.abs(qpos - kpos) <= w) & (kpos >= 0) & (kpos < seq_len)


def _place_head(x2, src_pos, dst_pos):
    hi = lax.broadcasted_iota(jnp.int32, x2.shape, 1) >= HEAD_DIM
    src = x2 if src_pos == dst_pos else pltpu.roll(x2, HEAD_DIM, 1)
    return jnp.where(hi == (dst_pos == 1), src, jnp.zeros_like(src))


def _swa_fwd_t(qkv, *, qcol, kcol, vcol, hq, hkv, w, tq, sink, name, comm=None):
    nseq, seq_len, _ = qkv.shape
    n = seq_len // tq
    rep = hq // hkv
    q_spec, kv_specs, _, _ = _swa_specs(tq, hq, hkv, n, qcol, kcol, vcol)

    def body(*refs):
        if sink is not None:
            sink_ref, refs = refs[0], refs[1:]
        q_ref, kp_ref, kc_ref, kn_ref, vp_ref, vc_ref, vn_ref, o_ref, lse_ref = refs
        i = pl.program_id(1)
        mask_t = _window_mask_t(i, tq, w, seq_len)
        o_t = [None] * (hq // 2)
        lse_rows = [None] * hq
        for a in range(hkv // 2):
            ls = slice(a * PAIR, (a + 1) * PAIR)
            kcat = jnp.concatenate([kp_ref[tq - w:, ls], kc_ref[:, ls], kn_ref[:w, ls]], axis=0) * 0.125
            vcat = jnp.concatenate([vp_ref[tq - w:, ls], vc_ref[:, ls], vn_ref[:w, ls]], axis=0)
            for e in range(2):
                g = 2 * a + e
                placed = {}
                for r in range(rep):
                    h = g * rep + r
                    qp, pos = h // 2, h % 2
                    if pos not in placed:
                        placed[pos] = (_place_head(kcat, e, pos), _place_head(vcat, e, pos))
                    k_g, v_g = placed[pos]
                    s_t = jnp.where(mask_t, _dot_nt(k_g, q_ref[:, qp * PAIR:(qp + 1) * PAIR]), NEG_INF)
                    m = jnp.max(s_t, axis=0, keepdims=True)
                    if sink is not None:
                        m = jnp.maximum(m, sink_ref[0, h])
                    p_t = jnp.exp(s_t - m)
                    den = jnp.sum(p_t, axis=0, keepdims=True)
                    if sink is not None:
                        den = den + jnp.exp(sink_ref[0, h] - m)
                    part = _dot_tn(v_g, p_t.astype(BF16)) / den
                    o_t[qp] = part if o_t[qp] is None else o_t[qp] + part
                    lse_rows[h] = m + jnp.log(den)
        o_ref[...] = jnp.concatenate(o_t, axis=0).T
        lse_ref[...] = jnp.concatenate(lse_rows, axis=0)

    in_specs = [q_spec] + kv_specs
    args = [qkv] * 7
    if sink is not None:
        in_specs = [pl.BlockSpec(memory_space=pltpu.SMEM)] + in_specs
        args = [sink] + args
    (o, lse), couts = _pcall(
        body, name=name, grid=(nseq, n), in_specs=in_specs,
        out_specs=[pl.BlockSpec((None, tq, hq * HEAD_DIM), lambda s, i: (s, i, 0)),
                   pl.BlockSpec((None, hq, tq), lambda s, i: (s, 0, i))],
        out_shape=[jax.ShapeDtypeStruct((nseq, seq_len, hq * HEAD_DIM), F32),
                   jax.ShapeDtypeStruct((nseq, hq, seq_len), F32)],
        args=args, dims=("parallel", "parallel"), comm=comm)
    return o, lse, couts


def _swa_bwd_t(qkv, do, lse, delta, cs, e_mat, *, qcol, kcol, vcol, hq, hkv, w, tq, sink, name, comm=None):
    nseq, seq_len, _ = qkv.shape
    n = seq_len // tq
    rep = hq // hkv
    qw, kw = hq * HEAD_DIM, hkv * HEAD_DIM
    tk = tq + 2 * w
    q_spec, kv_specs, cur, prv = _swa_specs(tq, hq, hkv, n, qcol, kcol, vcol)

    def body(*refs):
        if sink is not None:
            sink_ref, refs = refs[0], refs[1:]
        (q_ref, kp_ref, kc_ref, kn_ref, vp_ref, vc_ref, vn_ref, do_ref, lse_ref, dl_ref,
         cs_c, cs_p, e_ref) = refs[:13]
        outs = refs[13:]
        if sink is not None:
            dq_ref, dk_ref, dv_ref, dsink_ref, dk_acc, dv_acc = outs
        else:
            dq_ref, dk_ref, dv_ref, dk_acc, dv_acc = outs
        s_id = pl.program_id(0)
        i = pl.program_id(1)
        slot_p, slot_c, slot_n = (i + 2) % 3, i % 3, (i + 1) % 3

        if sink is not None:
            @pl.when((s_id == 0) & (i == 0))
            def _():
                dsink_ref[...] = jnp.zeros_like(dsink_ref)

        @pl.when(i < n)
        def _():
            mask_t = _window_mask_t(i, tq, w, seq_len)
            dk_acc[slot_n] = jnp.zeros((tq, kw), F32)
            dv_acc[slot_n] = jnp.zeros((tq, kw), F32)

            @pl.when(i == 0)
            def _():
                dk_acc[slot_c] = jnp.zeros((tq, kw), F32)
                dv_acc[slot_c] = jnp.zeros((tq, kw), F32)

            dq_t = [None] * (hq // 2)
            dk_pairs, dv_pairs = [], []
            for a in range(hkv // 2):
                ls = slice(a * PAIR, (a + 1) * PAIR)
                kcat = jnp.concatenate([kp_ref[tq - w:, ls], kc_ref[:, ls], kn_ref[:w, ls]], axis=0) * 0.125
                vcat = jnp.concatenate([vp_ref[tq - w:, ls], vc_ref[:, ls], vn_ref[:w, ls]], axis=0)
                dk2 = jnp.zeros((tk, PAIR), F32)
                dv2 = jnp.zeros((tk, PAIR), F32)
                for e in range(2):
                    g = 2 * a + e
                    placed = {}
                    for r in range(rep):
                        h = g * rep + r
                        qp, pos = h // 2, h % 2
                        if pos not in placed:
                            placed[pos] = (_place_head(kcat, e, pos), _place_head(vcat, e, pos))
                        k_g, v_g = placed[pos]
                        q2 = q_ref[:, qp * PAIR:(qp + 1) * PAIR]
                        do2 = do_ref[:, qp * PAIR:(qp + 1) * PAIR]
                        lse_h = lse_ref[h:h + 1, :]
                        dl_h = dl_ref[h:h + 1, :]
                        p_t = jnp.exp(jnp.where(mask_t, _dot_nt(k_g, q2), NEG_INF) - lse_h)
                        dp_t = _dot_nt(v_g, do2)
                        dsb = (p_t * (dp_t - dl_h)).astype(BF16)
                        part = _dot_tn(k_g, dsb)
                        dq_t[qp] = part if dq_t[qp] is None else dq_t[qp] + part
                        dk2 = dk2 + _dot(dsb, _place_head(q2, pos, e) * 0.125)
                        dv2 = dv2 + _dot(p_t.astype(BF16), _place_head(do2, pos, e))
                        if sink is not None:
                            ds_sink = -jnp.sum(jnp.exp(sink_ref[0, h] - lse_h) * dl_h)
                            dsink_ref[h:h + 1, :] += jnp.full((1, LANES), ds_sink, F32)
                dk_pairs.append(dk2)
                dv_pairs.append(dv2)
            dq = jnp.concatenate(dq_t, axis=0).T
            dq_ref[...] = _rope(dq, *_rope_tabs(cs_c[...], e_ref[...]), -1.0).astype(BF16)
            dk_all = dk_pairs[0] if len(dk_pairs) == 1 else jnp.concatenate(dk_pairs, axis=1)
            dv_all = dv_pairs[0] if len(dv_pairs) == 1 else jnp.concatenate(dv_pairs, axis=1)

            @pl.when(i > 0)
            def _():
                dk_acc[slot_p, tq - w:, :] += dk_all[:w]
                dv_acc[slot_p, tq - w:, :] += dv_all[:w]

            dk_acc[slot_c] += dk_all[w:w + tq]
            dv_acc[slot_c] += dv_all[w:w + tq]
            dk_acc[slot_n, :w, :] += dk_all[w + tq:]
            dv_acc[slot_n, :w, :] += dv_all[w + tq:]

        @pl.when(i >= 1)
        def _():
            dk_ref[...] = _rope(dk_acc[slot_p], *_rope_tabs(cs_p[...], e_ref[...]), -1.0).astype(BF16)
            dv_ref[...] = dv_acc[slot_p].astype(BF16)

    row_c = lambda width: pl.BlockSpec((None, tq, width), lambda s, i: (s, cur(s, i), 0))
    row_p = lambda width: pl.BlockSpec((None, tq, width), lambda s, i: (s, jnp.maximum(i - 1, 0), 0))
    stat = pl.BlockSpec((None, hq, tq), lambda s, i: (s, 0, cur(s, i)))
    in_specs = ([q_spec] + kv_specs + [row_c(qw), stat, stat, row_c(ROT_DIM), row_p(ROT_DIM),
                                       pl.BlockSpec((ROT_DIM, 3 * LANES), lambda s, i: (0, 0))])
    args = [qkv] * 7 + [do, lse, delta, cs, cs, e_mat]
    out_specs = [row_c(qw), row_p(kw), row_p(kw)]
    out_shape = [jax.ShapeDtypeStruct((nseq, seq_len, qw), BF16),
                 jax.ShapeDtypeStruct((nseq, seq_len, kw), BF16),
                 jax.ShapeDtypeStruct((nseq, seq_len, kw), BF16)]
    if sink is not None:
        in_specs = [pl.BlockSpec(memory_space=pltpu.SMEM)] + in_specs
        args = [sink] + args
        out_specs.append(pl.BlockSpec((8, LANES), lambda s, i: (0, 0)))
        out_shape.append(jax.ShapeDtypeStruct((8, LANES), F32))
    return _pcall(
        body, name=name, grid=(nseq, n + 1), in_specs=in_specs, out_specs=out_specs, out_shape=out_shape,
        scratch_shapes=[pltpu.VMEM((3, tq, kw), F32), pltpu.VMEM((3, tq, kw), F32)], args=args,
        dims=("arbitrary", "arbitrary"), comm=comm)


def _band_mask_t(row0, tq, w, seq_len):
    tk = tq + 2 * w
    kk = lax.broadcasted_iota(jnp.int32, (tk, tq), 0)
    qq = lax.broadcasted_iota(jnp.int32, (tk, tq), 1)
    kpos = row0 - w + kk
    return (jnp.abs(qq + w - kk) <= w) & (kpos >= 0) & (kpos < seq_len)


def _halo_kv_specs(t, w, hkv, n, seq_len, kcol, vcol):
    kw = hkv * HEAD_DIM
    per, last = t // w, seq_len // w - 1
    cur = lambda s, i: jnp.minimum(i, n - 1)
    specs = []
    for c in (kcol, vcol):
        specs += [pl.BlockSpec((None, w, kw), lambda s, i, c=c: (s, jnp.maximum(cur(s, i) * per - 1, 0), c)),
                  pl.BlockSpec((None, t, kw), lambda s, i, c=c: (s, cur(s, i), c)),
                  pl.BlockSpec((None, w, kw), lambda s, i, c=c: (s, jnp.minimum((cur(s, i) + 1) * per, last), c))]
    return specs, cur


def _swa_fwd_s(qkv, *, qcol, kcol, vcol, hq, hkv, w, tq, sub, sink, name, comm=None):
    nseq, seq_len, _ = qkv.shape
    t = tq * sub
    n = seq_len // t
    rep = hq // hkv
    tk = tq + 2 * w
    kv_specs, cur = _halo_kv_specs(t, w, hkv, n, seq_len, kcol, vcol)

    def body(*refs):
        if sink is not None:
            sink_ref, refs = refs[0], refs[1:]
        q_ref, kp_ref, kc_ref, kn_ref, vp_ref, vc_ref, vn_ref, o_ref, lse_ref = refs
        i = pl.program_id(1)
        kfull, vfull = [], []
        for a in range(hkv // 2):
            ls = slice(a * PAIR, (a + 1) * PAIR)
            kfull.append(jnp.concatenate([kp_ref[:, ls], kc_ref[:, ls], kn_ref[:, ls]], axis=0) * 0.125)
            vfull.append(jnp.concatenate([vp_ref[:, ls], vc_ref[:, ls], vn_ref[:, ls]], axis=0))
        for jj in range(sub):
            rows = slice(jj * tq, (jj + 1) * tq)
            mask_t = _band_mask_t(i * t + jj * tq, tq, w, seq_len)
            o_t = [None] * (hq // 2)
            lse_rows = [None] * hq
            for a in range(hkv // 2):
                kcat = kfull[a][jj * tq:jj * tq + tk]
                vcat = vfull[a][jj * tq:jj * tq + tk]
                for e in range(2):
                    g = 2 * a + e
                    placed = {}
                    for r in range(rep):
                        h = g * rep + r
                        qp, pos = h // 2, h % 2
                        if pos not in placed:
                            placed[pos] = (_place_head(kcat, e, pos), _place_head(vcat, e, pos))
                        k_g, v_g = placed[pos]
                        s_t = jnp.where(mask_t, _dot_nt(k_g, q_ref[rows, qp * PAIR:(qp + 1) * PAIR]), NEG_INF)
                        m = jnp.max(s_t, axis=0, keepdims=True)
                        if sink is not None:
                            m = jnp.maximum(m, sink_ref[0, h])
                        p_t = jnp.exp(s_t - m)
                        den = jnp.sum(p_t, axis=0, keepdims=True)
                        if sink is not None:
                            den = den + jnp.exp(sink_ref[0, h] - m)
                        part = _dot_tn(v_g, p_t.astype(BF16)) / den
                        o_t[qp] = part if o_t[qp] is None else o_t[qp] + part
                        lse_rows[h] = m + jnp.log(den)
            o_ref[rows, :] = jnp.concatenate(o_t, axis=0).T
            lse_ref[:, rows] = jnp.concatenate(lse_rows, axis=0)

    in_specs = [pl.BlockSpec((None, t, hq * HEAD_DIM), lambda s, i: (s, i, qcol))] + kv_specs
    args = [qkv] * 7
    if sink is not None:
        in_specs = [pl.BlockSpec(memory_space=pltpu.SMEM)] + in_specs
        args = [sink] + args
    (o, lse), couts = _pcall(
        body, name=name, grid=(nseq, n), in_specs=in_specs,
        out_specs=[pl.BlockSpec((None, t, hq * HEAD_DIM), lambda s, i: (s, i, 0)),
                   pl.BlockSpec((None, hq, t), lambda s, i: (s, 0, i))],
        out_shape=[jax.ShapeDtypeStruct((nseq, seq_len, hq * HEAD_DIM), F32),
                   jax.ShapeDtypeStruct((nseq, hq, seq_len), F32)],
        args=args, dims=("parallel", "parallel"), comm=comm)
    return o, lse, couts


def _swa_bwd_s(qkv, do, lse, delta, cs, e_mat, *, qcol, kcol, vcol, hq, hkv, w, tq, sub, sink, name, comm=None):
    nseq, seq_len, _ = qkv.shape
    t = tq * sub
    n = seq_len // t
    rep = hq // hkv
    qw, kw = hq * HEAD_DIM, hkv * HEAD_DIM
    tk = tq + 2 * w
    kv_specs, cur = _halo_kv_specs(t, w, hkv, n, seq_len, kcol, vcol)

    def body(*refs):
        if sink is not None:
            sink_ref, refs = refs[0], refs[1:]
        (q_ref, kp_ref, kc_ref, kn_ref, vp_ref, vc_ref, vn_ref, do_ref, lse_ref, dl_ref,
         cs_c, cs_p, e_ref) = refs[:13]
        outs = refs[13:]
        if sink is not None:
            dq_ref, dk_ref, dv_ref, dsink_ref, dk_acc, dv_acc, dk_win, dv_win = outs
        else:
            dq_ref, dk_ref, dv_ref, dk_acc, dv_acc, dk_win, dv_win = outs
        s_id = pl.program_id(0)
        i = pl.program_id(1)
        slot_p, slot_c, slot_n = (i + 2) % 3, i % 3, (i + 1) % 3

        if sink is not None:
            @pl.when((s_id == 0) & (i == 0))
            def _():
                dsink_ref[...] = jnp.zeros_like(dsink_ref)

        @pl.when(i < n)
        def _():
            dk_win[...] = jnp.zeros_like(dk_win)
            dv_win[...] = jnp.zeros_like(dv_win)
            kfull, vfull = [], []
            for a in range(hkv // 2):
                ls = slice(a * PAIR, (a + 1) * PAIR)
                kfull.append(jnp.concatenate([kp_ref[:, ls], kc_ref[:, ls], kn_ref[:, ls]], axis=0) * 0.125)
                vfull.append(jnp.concatenate([vp_ref[:, ls], vc_ref[:, ls], vn_ref[:, ls]], axis=0))
            for jj in range(sub):
                rows = slice(jj * tq, (jj + 1) * tq)
                krows = slice(jj * tq, jj * tq + tk)
                mask_t = _band_mask_t(i * t + jj * tq, tq, w, seq_len)
                dq_t = [None] * (hq // 2)
                for a in range(hkv // 2):
                    ls = slice(a * PAIR, (a + 1) * PAIR)
                    kcat, vcat = kfull[a][krows], vfull[a][krows]
                    dk2 = jnp.zeros((tk, PAIR), F32)
                    dv2 = jnp.zeros((tk, PAIR), F32)
                    for e in range(2):
                        g = 2 * a + e
                        placed = {}
                        for r in range(rep):
                            h = g * rep + r
                            qp, pos = h // 2, h % 2
                            if pos not in placed:
                                placed[pos] = (_place_head(kcat, e, pos), _place_head(vcat, e, pos))
                            k_g, v_g = placed[pos]
                            q2 = q_ref[rows, qp * PAIR:(qp + 1) * PAIR]
                            do2 = do_ref[rows, qp * PAIR:(qp + 1) * PAIR]
                            lse_h = lse_ref[h:h + 1, rows]
                            dl_h = dl_ref[h:h + 1, rows]
                            p_t = jnp.exp(jnp.where(mask_t, _dot_nt(k_g, q2), NEG_INF) - lse_h)
                            dp_t = _dot_nt(v_g, do2)
                            dsb = (p_t * (dp_t - dl_h)).astype(BF16)
                            part = _dot_tn(k_g, dsb)
                            dq_t[qp] = part if dq_t[qp] is None else dq_t[qp] + part
                            dk2 = dk2 + _dot(dsb, _place_head(q2, pos, e) * 0.125)
                            dv2 = dv2 + _dot(p_t.astype(BF16), _place_head(do2, pos, e))
                            if sink is not None:
                                ds_sink = -jnp.sum(jnp.exp(sink_ref[0, h] - lse_h) * dl_h)
                                dsink_ref[h:h + 1, :] += jnp.full((1, LANES), ds_sink, F32)
                    dk_win[krows, ls] += dk2
                    dv_win[krows, ls] += dv2
                dq = jnp.concatenate(dq_t, axis=0).T
                dq_ref[rows, :] = _rope(dq, *_rope_tabs(cs_c[rows, :], e_ref[...]), -1.0).astype(BF16)

            @pl.when(i > 0)
            def _():
                dk_acc[slot_p, t - w:, :] += dk_win[:w, :]
                dv_acc[slot_p, t - w:, :] += dv_win[:w, :]

            @pl.when(i == 0)
            def _():
                dk_acc[slot_c] = dk_win[w:w + t, :]
                dv_acc[slot_c] = dv_win[w:w + t, :]

            @pl.when(i > 0)
            def _():
                dk_acc[slot_c] += dk_win[w:w + t, :]
                dv_acc[slot_c] += dv_win[w:w + t, :]

            dk_acc[slot_n] = jnp.zeros((t, kw), F32)
            dv_acc[slot_n] = jnp.zeros((t, kw), F32)
            dk_acc[slot_n, :w, :] = dk_win[w + t:, :]
            dv_acc[slot_n, :w, :] = dv_win[w + t:, :]

        @pl.when(i >= 1)
        def _():
            dk_ref[...] = _rope(dk_acc[slot_p], *_rope_tabs(cs_p[...], e_ref[...]), -1.0).astype(BF16)
            dv_ref[...] = dv_acc[slot_p].astype(BF16)

    row_c = lambda width: pl.BlockSpec((None, t, width), lambda s, i: (s, cur(s, i), 0))
    row_p = lambda width: pl.BlockSpec((None, t, width), lambda s, i: (s, jnp.maximum(i - 1, 0), 0))
    stat = pl.BlockSpec((None, hq, t), lambda s, i: (s, 0, cur(s, i)))
    in_specs = ([pl.BlockSpec((None, t, qw), lambda s, i: (s, cur(s, i), qcol))] + kv_specs
                + [row_c(qw), stat, stat, row_c(ROT_DIM), row_p(ROT_DIM),
                   pl.BlockSpec((ROT_DIM, 3 * LANES), lambda s, i: (0, 0))])
    args = [qkv] * 7 + [do, lse, delta, cs, cs, e_mat]
    out_specs = [row_c(qw), row_p(kw), row_p(kw)]
    out_shape = [jax.ShapeDtypeStruct((nseq, seq_len, qw), BF16),
                 jax.ShapeDtypeStruct((nseq, seq_len, kw), BF16),
                 jax.ShapeDtypeStruct((nseq, seq_len, kw), BF16)]
    if sink is not None:
        in_specs = [pl.BlockSpec(memory_space=pltpu.SMEM)] + in_specs
        args = [sink] + args
        out_specs.append(pl.BlockSpec((8, LANES), lambda s, i: (0, 0)))
        out_shape.append(jax.ShapeDtypeStruct((8, LANES), F32))
    return _pcall(
        body, name=name, grid=(nseq, n + 1), in_specs=in_specs, out_specs=out_specs, out_shape=out_shape,
        scratch_shapes=[pltpu.VMEM((3, t, kw), F32), pltpu.VMEM((3, t, kw), F32),
                        pltpu.VMEM((t + 2 * w, kw), F32), pltpu.VMEM((t + 2 * w, kw), F32)], args=args,
        dims=("arbitrary", "arbitrary"), comm=comm)


def _rms_parts(o, g):
    ms = jnp.mean(o * o, axis=-1, keepdims=True) + LN_EPS
    rinv = lax.rsqrt(ms)
    return o * rinv * g, rinv


def _combine_fwd(out_a, o_g, lse_g, g_win, g_dil, *, t):
    s = out_a.shape[0]
    wd = DIL_SLOTS * HEAD_DIM

    def body(oa_ref, o0, o1, o2, l0, l1, l2, gw_ref, gd_ref, mixed_ref, ob_ref, lt_ref):
        ls = [l0[...], l1[...], l2[...]]
        mx = jnp.maximum(jnp.maximum(ls[0], ls[1]), ls[2])
        ws = [jnp.exp(l - mx) for l in ls]
        tot = ws[0] + ws[1] + ws[2]
        lt_ref[...] = mx + jnp.log(tot)
        ws = [x / tot for x in ws]
        parts = []
        for h in range(DIL_SLOTS):
            hs = slice(h * HEAD_DIM, (h + 1) * HEAD_DIM)
            parts.append(ws[0][:, h:h + 1] * o0[:, hs] + ws[1][:, h:h + 1] * o1[:, hs] + ws[2][:, h:h + 1] * o2[:, hs])
        ob = jnp.concatenate(parts, axis=1)
        ob_ref[...] = ob
        na, _ = _rms_parts(oa_ref[...], gw_ref[...])
        nb, _ = _rms_parts(ob, gd_ref[...])
        mixed_ref[:, :wd] = na.astype(BF16)
        mixed_ref[:, wd:] = nb.astype(BF16)

    half = pl.BlockSpec((t, wd), lambda i: (i, 0))
    lanes = pl.BlockSpec((t, LANES), lambda i: (i, 0))
    grow = pl.BlockSpec((1, wd), lambda i: (0, 0))
    return pl.pallas_call(
        body, name="combine_fwd", grid=(s // t,),
        in_specs=[half, half, half, half, lanes, lanes, lanes, grow, grow],
        out_specs=[pl.BlockSpec((t, 2 * wd), lambda i: (i, 0)), half, lanes],
        out_shape=[jax.ShapeDtypeStruct((s, 2 * wd), BF16), jax.ShapeDtypeStruct((s, wd), F32),
                   jax.ShapeDtypeStruct((s, LANES), F32)],
        compiler_params=_cparams(dimension_semantics=("parallel",)),
    )(out_a, *o_g, *lse_g, g_win, g_dil)


def _combine_bwd(dmixed, out_a, out_b, g_win, g_dil, *, t):
    s = out_a.shape[0]
    wd = DIL_SLOTS * HEAD_DIM

    def body(dm_ref, oa_ref, ob_ref, gw_ref, gd_ref, doa_ref, dob_ref, dla_ref, dlb_ref, st_ref):
        i = pl.program_id(0)

        @pl.when(i == 0)
        def _():
            st_ref[...] = jnp.zeros_like(st_ref)

        lane = lax.broadcasted_iota(jnp.int32, (t, LANES), 1)
        for idx, (o_ref, g_ref, do_ref, dl_ref) in enumerate(
                ((oa_ref, gw_ref, doa_ref, dla_ref), (ob_ref, gd_ref, dob_ref, dlb_ref))):
            o = o_ref[...]
            dn = dm_ref[:, idx * wd:(idx + 1) * wd]
            _, rinv = _rms_parts(o, g_ref[...])
            wv = dn * g_ref[...]
            do = rinv * wv - o * (rinv * rinv * rinv) * jnp.mean(wv * o, axis=-1, keepdims=True)
            st_ref[idx:idx + 1, :] += jnp.sum(dn * o * rinv, axis=0, keepdims=True)
            do_ref[...] = do.astype(BF16)
            prod = do * o
            acc = jnp.zeros((t, LANES), F32)
            for h in range(DIL_SLOTS):
                hs = slice(h * HEAD_DIM, (h + 1) * HEAD_DIM)
                acc = jnp.where(lane == h, jnp.sum(prod[:, hs], axis=1, keepdims=True), acc)
            dl_ref[...] = acc

    half = pl.BlockSpec((t, wd), lambda i: (i, 0))
    lanes = pl.BlockSpec((t, LANES), lambda i: (i, 0))
    grow = pl.BlockSpec((1, wd), lambda i: (0, 0))
    return pl.pallas_call(
        body, name="combine_bwd", grid=(s // t,),
        in_specs=[pl.BlockSpec((t, 2 * wd), lambda i: (i, 0)), half, half, grow, grow],
        out_specs=[half, half, lanes, lanes, pl.BlockSpec((8, wd), lambda i: (0, 0))],
        out_shape=[jax.ShapeDtypeStruct((s, wd), BF16), jax.ShapeDtypeStruct((s, wd), BF16),
                   jax.ShapeDtypeStruct((s, LANES), F32), jax.ShapeDtypeStruct((s, LANES), F32),
                   jax.ShapeDtypeStruct((8, wd), F32)],
        compiler_params=_cparams(dimension_semantics=("arbitrary",)),
    )(dmixed, out_a, out_b, g_win, g_dil)


def _mixproj_fwd(mixed_b, w_mix_b, x, ln_in_g, ln_in_b, ln1_g, ln1_b, *, t):
    s = x.shape[0]

    def body(m_ref, w_ref, x_ref, g0, b0, g1, b1, r1_ref, h1_ref):
        h0 = _ln(x_ref[...], g0[...], b0[...])
        r1 = ALPHA * h0 + _dot(m_ref[...], w_ref[...])
        r1_ref[...] = r1
        h1_ref[...] = _ln(r1, g1[...], b1[...]).astype(BF16)

    tile = pl.BlockSpec((t, D_MODEL), lambda i: (i, 0))
    row = pl.BlockSpec((1, D_MODEL), lambda i: (0, 0))
    return pl.pallas_call(
        body, name="mixproj_fwd", grid=(s // t,),
        in_specs=[tile, pl.BlockSpec((D_MODEL, D_MODEL), lambda i: (0, 0)), tile, row, row, row, row],
        out_specs=[tile, tile],
        out_shape=[jax.ShapeDtypeStruct((s, D_MODEL), F32), jax.ShapeDtypeStruct((s, D_MODEL), BF16)],
        compiler_params=_cparams(dimension_semantics=("parallel",)),
    )(mixed_b, w_mix_b, x, ln_in_g, ln_in_b, ln1_g, ln1_b)


def _mem_fwd(mem, g, b, wk_b, wv_b):
    ml = mem.shape[0]

    def body(mem_ref, g_ref, b_ref, wk_ref, wv_ref, mn_ref, kx_ref, vx_ref):
        mn = _ln(mem_ref[...], g_ref[...], b_ref[...]).astype(BF16)
        mn_ref[...] = mn
        kx_ref[...] = _dot(mn, wk_ref[...]).astype(BF16)
        vx_ref[...] = _dot(mn, wv_ref[...]).astype(BF16)

    sh = jax.ShapeDtypeStruct((ml, D_MODEL), BF16)
    return pl.pallas_call(body, name="mem_fwd", out_shape=[sh, sh, sh], compiler_params=_cparams())(
        mem, g, b, wk_b, wv_b)


def _mem_bwd(dkx, dvx, mem, g, b, wk_b, wv_b):
    def body(dk_ref, dv_ref, mem_ref, g_ref, b_ref, wk_ref, wv_ref, dwk_ref, dwv_ref, st_ref):
        mem_v = mem_ref[...]
        mn = _ln(mem_v, g_ref[...], b_ref[...]).astype(BF16)
        dkb = dk_ref[...].astype(BF16)
        dvb = dv_ref[...].astype(BF16)
        dwk_ref[...] = _dot_tn(mn, dkb)
        dwv_ref[...] = _dot_tn(mn, dvb)
        dmn = _dot_nt(dkb, wk_ref[...]) + _dot_nt(dvb, wv_ref[...])
        _, dg, db = _ln_bwd_math(dmn, mem_v, g_ref[...])
        st_ref[...] = jnp.zeros_like(st_ref)
        st_ref[0:1, :] = dg
        st_ref[1:2, :] = db

    sw = jax.ShapeDtypeStruct((D_MODEL, D_MODEL), F32)
    return pl.pallas_call(body, name="mem_bwd", out_shape=[sw, sw, jax.ShapeDtypeStruct((8, D_MODEL), F32)],
                          compiler_params=_cparams())(dkx, dvx, mem, g, b, wk_b, wv_b)


def _xattn_fwd(h1b, r1, kx, vx, wq_b, wo_b, ln1_g, ln1_b, ln2_g, ln2_b, *, t):
    s = h1b.shape[0]
    scale = X_HEAD_DIM ** -0.5

    def body(h_ref, r1_ref, kx_ref, vx_ref, wq_ref, wo_ref, g1, b1, g2, b2, r2_ref, h2_ref, qx_ref, ox_ref, lse_ref):
        qxb = _dot(h_ref[...], wq_ref[...]).astype(BF16)
        qx_ref[...] = qxb
        lane = lax.broadcasted_iota(jnp.int32, (t, LANES), 1)
        lse_acc = jnp.zeros((t, LANES), F32)
        parts = []
        for h in range(X_HEADS):
            hs = slice(h * X_HEAD_DIM, (h + 1) * X_HEAD_DIM)
            sc = _dot_nt(qxb[:, hs] * scale, kx_ref[:, hs])
            m = jnp.max(sc, axis=1, keepdims=True)
            p = jnp.exp(sc - m)
            den = jnp.sum(p, axis=1, keepdims=True)
            parts.append(_dot(p.astype(BF16), vx_ref[:, hs]) / den)
            lse_acc = jnp.where(lane == h, m + jnp.log(den), lse_acc)
        lse_ref[...] = lse_acc
        oxb = jnp.concatenate(parts, axis=1).astype(BF16)
        ox_ref[...] = oxb
        h1 = _ln(r1_ref[...], g1[...], b1[...])
        r2 = ALPHA * h1 + _dot(oxb, wo_ref[...])
        r2_ref[...] = r2
        h2_ref[...] = _ln(r2, g2[...], b2[...]).astype(BF16)

    tile = pl.BlockSpec((t, D_MODEL), lambda i: (i, 0))
    row = pl.BlockSpec((1, D_MODEL), lambda i: (0, 0))
    full = lambda r: pl.BlockSpec((r, D_MODEL), lambda i: (0, 0))
    ml = kx.shape[0]
    bsh = jax.ShapeDtypeStruct((s, D_MODEL), BF16)
    return pl.pallas_call(
        body, name="xattn_fwd", grid=(s // t,),
        in_specs=[tile, tile, full(ml), full(ml), full(D_MODEL), full(D_MODEL), row, row, row, row],
        out_specs=[tile, tile, tile, tile, pl.BlockSpec((t, LANES), lambda i: (i, 0))],
        out_shape=[jax.ShapeDtypeStruct((s, D_MODEL), F32), bsh, bsh, bsh, jax.ShapeDtypeStruct((s, LANES), F32)],
        compiler_params=_cparams(dimension_semantics=("parallel",)),
    )(h1b, r1, kx, vx, wq_b, wo_b, ln1_g, ln1_b, ln2_g, ln2_b)


def _xattn_bwd(dr2, qxb, oxb, lse, kx, vx, wq_b, wo_b, *, t, comm=None):
    s = dr2.shape[0]
    ml = kx.shape[0]
    scale = X_HEAD_DIM ** -0.5

    def body(dr2_ref, qx_ref, ox_ref, lse_ref, kx_ref, vx_ref, wq_ref, wo_ref, dh1_ref, dqx_ref, dkx_ref, dvx_ref):
        i = pl.program_id(0)

        @pl.when(i == 0)
        def _():
            dkx_ref[...] = jnp.zeros_like(dkx_ref)
            dvx_ref[...] = jnp.zeros_like(dvx_ref)

        dr2v = dr2_ref[...]
        dox = _dot_nt(dr2v.astype(BF16), wo_ref[...])
        parts = []
        for h in range(X_HEADS):
            hs = slice(h * X_HEAD_DIM, (h + 1) * X_HEAD_DIM)
            doh = dox[:, hs]
            dohb = doh.astype(BF16)
            dl = jnp.sum(doh * ox_ref[:, hs].astype(F32), axis=1, keepdims=True)
            qh = qx_ref[:, hs] * scale
            p = jnp.exp(_dot_nt(qh, kx_ref[:, hs]) - lse_ref[:, h:h + 1])
            dp = _dot_nt(dohb, vx_ref[:, hs])
            dsb = (p * (dp - dl)).astype(BF16)
            parts.append(_dot(dsb, kx_ref[:, hs]) * scale)
            dkx_ref[:, hs] += _dot_tn(dsb, qh)
            dvx_ref[:, hs] += _dot_tn(p.astype(BF16), dohb)
        dqxb = jnp.concatenate(parts, axis=1).astype(BF16)
        dqx_ref[...] = dqxb
        dh1_ref[...] = _dot_nt(dqxb, wq_ref[...]) + ALPHA * dr2v

    tile = pl.BlockSpec((t, D_MODEL), lambda i: (i, 0))
    full = lambda r: pl.BlockSpec((r, D_MODEL), lambda i: (0, 0))
    return _pcall(
        body, name="xattn_bwd", grid=(s // t,),
        in_specs=[tile, tile, tile, pl.BlockSpec((t, LANES), lambda i: (i, 0)), full(ml), full(ml),
                  full(D_MODEL), full(D_MODEL)],
        out_specs=[tile, tile, full(ml), full(ml)],
        out_shape=[jax.ShapeDtypeStruct((s, D_MODEL), F32), jax.ShapeDtypeStruct((s, D_MODEL), BF16),
                   jax.ShapeDtypeStruct((ml, D_MODEL), F32), jax.ShapeDtypeStruct((ml, D_MODEL), F32)],
        args=[dr2, qxb, oxb, lse, kx, vx, wq_b, wo_b], dims=("arbitrary",), comm=comm)


def _halo_specs(t, s, width):
    tb8 = t // 8
    return [pl.BlockSpec((t, width), lambda i: (i, 0)),
            pl.BlockSpec((8, width), lambda i: (jnp.maximum(i * tb8 - 1, 0), 0)),
            pl.BlockSpec((8, width), lambda i: (jnp.minimum((i + 1) * tb8, s // 8 - 1), 0))]


def _halo_rows(i, n, prev_ref, next_ref):
    prev_row = jnp.where(i > 0, prev_ref[7:8, :], 0.0)
    next_row = jnp.where(i < n - 1, next_ref[0:1, :], 0.0)
    return prev_row, next_row


def _gelu_parts(gc):
    cdf = 0.5 * (1.0 + lax.erf(gc * (2.0 ** -0.5)))
    pdf = jnp.exp(-0.5 * gc * gc) * (1.0 / math.sqrt(2.0 * math.pi))
    return gc * cdf, cdf + gc * pdf


def _conv_fwd(g, u, conv_w, conv_b, *, t):
    s = g.shape[0]
    n = s // t

    def body(g_ref, gp_ref, gn_ref, u_ref, cw_ref, cb_ref, o_ref):
        i = pl.program_id(0)
        gv = g_ref[...]
        prev_row, next_row = _halo_rows(i, n, gp_ref, gn_ref)
        gm1, gp1 = _shift_rows(gv, prev_row, next_row)
        gc = gm1 * cw_ref[0:1, :] + gv * cw_ref[1:2, :] + gp1 * cw_ref[2:3, :] + cb_ref[...]
        act, _ = _gelu_parts(gc)
        o_ref[...] = (act * u_ref[...]).astype(BF16)

    tile = pl.BlockSpec((t, D_FF), lambda i: (i, 0))
    return pl.pallas_call(
        body, name="conv_fwd", grid=(n,),
        in_specs=_halo_specs(t, s, D_FF) + [tile, pl.BlockSpec((3, D_FF), lambda i: (0, 0)),
                                            pl.BlockSpec((1, D_FF), lambda i: (0, 0))],
        out_specs=tile, out_shape=jax.ShapeDtypeStruct((s, D_FF), BF16),
        compiler_params=_cparams(dimension_semantics=("parallel",)),
    )(g, g, g, u, conv_w, conv_b)


def _down_ln3(tb, w_down_b, r2, target, ln2_g, ln2_b, ln3_g, ln3_b, *, t):
    s = r2.shape[0]

    def body(t_ref, w_ref, r2_ref, tg_ref, g2, b2, g3, b3, dr_ref, drb_ref, st_ref):
        i = pl.program_id(0)

        @pl.when(i == 0)
        def _():
            st_ref[...] = jnp.zeros_like(st_ref)

        h2 = _ln(r2_ref[...], g2[...], b2[...])
        r3 = ALPHA * h2 + _dot(t_ref[...], w_ref[...])
        y = _ln(r3, g3[...], b3[...])
        err = y - tg_ref[...]
        loss = 0.5 * jnp.sum(jnp.mean(err * err, axis=-1, keepdims=True))
        dy = err * (1.0 / D_MODEL)
        dr, dg, db = _ln_bwd_math(dy, r3, g3[...])
        dr_ref[...] = dr
        drb_ref[...] = dr.astype(BF16)
        st_ref[0:1, :] += dg
        st_ref[1:2, :] += db
        st_ref[2:3, :] += jnp.full((1, D_MODEL), loss, F32)

    tile = pl.BlockSpec((t, D_MODEL), lambda i: (i, 0))
    row = pl.BlockSpec((1, D_MODEL), lambda i: (0, 0))
    return pl.pallas_call(
        body, name="down_ln3", grid=(s // t,),
        in_specs=[pl.BlockSpec((t, D_FF), lambda i: (i, 0)), pl.BlockSpec((D_FF, D_MODEL), lambda i: (0, 0)),
                  tile, tile, row, row, row, row],
        out_specs=[tile, tile, pl.BlockSpec((8, D_MODEL), lambda i: (0, 0))],
        out_shape=[jax.ShapeDtypeStruct((s, D_MODEL), F32), jax.ShapeDtypeStruct((s, D_MODEL), BF16),
                   jax.ShapeDtypeStruct((8, D_MODEL), F32)],
        compiler_params=_cparams(dimension_semantics=("arbitrary",)),
    )(tb, w_down_b, r2, target, ln2_g, ln2_b, ln3_g, ln3_b)


def _conv_bwd_a(dr3b, w_down_b, g, u, conv_w, conv_b, *, t):
    s = g.shape[0]
    n = s // t

    def body(d_ref, w_ref, g_ref, gp_ref, gn_ref, u_ref, cw_ref, cb_ref, du_ref, dgc_ref, st_ref):
        i = pl.program_id(0)

        @pl.when(i == 0)
        def _():
            st_ref[...] = jnp.zeros_like(st_ref)

        dt = _dot_nt(d_ref[...], w_ref[...])
        gv = g_ref[...]
        prev_row, next_row = _halo_rows(i, n, gp_ref, gn_ref)
        gm1, gp1 = _shift_rows(gv, prev_row, next_row)
        gc = gm1 * cw_ref[0:1, :] + gv * cw_ref[1:2, :] + gp1 * cw_ref[2:3, :] + cb_ref[...]
        act, dact = _gelu_parts(gc)
        du_ref[...] = (dt * act).astype(BF16)
        dgc = dt * u_ref[...] * dact
        dgc_ref[...] = dgc
        st_ref[0:1, :] += jnp.sum(gm1 * dgc, axis=0, keepdims=True)
        st_ref[1:2, :] += jnp.sum(gv * dgc, axis=0, keepdims=True)
        st_ref[2:3, :] += jnp.sum(gp1 * dgc, axis=0, keepdims=True)
        st_ref[3:4, :] += jnp.sum(dgc, axis=0, keepdims=True)

    tile = pl.BlockSpec((t, D_FF), lambda i: (i, 0))
    return pl.pallas_call(
        body, name="conv_bwd_a", grid=(n,),
        in_specs=[pl.BlockSpec((t, D_MODEL), lambda i: (i, 0)), pl.BlockSpec((D_FF, D_MODEL), lambda i: (0, 0))]
        + _halo_specs(t, s, D_FF) + [tile, pl.BlockSpec((3, D_FF), lambda i: (0, 0)),
                                     pl.BlockSpec((1, D_FF), lambda i: (0, 0))],
        out_specs=[tile, tile, pl.BlockSpec((8, D_FF), lambda i: (0, 0))],
        out_shape=[jax.ShapeDtypeStruct((s, D_FF), BF16), jax.ShapeDtypeStruct((s, D_FF), F32),
                   jax.ShapeDtypeStruct((8, D_FF), F32)],
        compiler_params=_cparams(dimension_semantics=("arbitrary",)),
    )(dr3b, w_down_b, g, g, g, u, conv_w, conv_b)


def _conv_bwd_b(dgc, conv_w, *, t):
    s = dgc.shape[0]
    n = s // t

    def body(d_ref, dp_ref, dn_ref, cw_ref, o_ref):
        i = pl.program_id(0)
        dv = d_ref[...]
        prev_row, next_row = _halo_rows(i, n, dp_ref, dn_ref)
        dm1, dp1 = _shift_rows(dv, prev_row, next_row)
        o_ref[...] = (dp1 * cw_ref[0:1, :] + dv * cw_ref[1:2, :] + dm1 * cw_ref[2:3, :]).astype(BF16)

    return pl.pallas_call(
        body, name="conv_bwd_b", grid=(n,),
        in_specs=_halo_specs(t, s, D_FF) + [pl.BlockSpec((3, D_FF), lambda i: (0, 0))],
        out_specs=pl.BlockSpec((t, D_FF), lambda i: (i, 0)), out_shape=jax.ShapeDtypeStruct((s, D_FF), BF16),
        compiler_params=_cparams(dimension_semantics=("parallel",)),
    )(dgc, dgc, dgc, conv_w)


def _to_residue(a, dil):
    s, w = a.shape
    return a.reshape(s // dil, dil, w).transpose(1, 0, 2)


def _from_residue(a):
    dil, l, w = a.shape
    return a.transpose(1, 0, 2).reshape(dil * l, w)


def _stats_to_lanes(rows):
    dil, hq, l = rows.shape
    return jnp.pad(rows.transpose(2, 0, 1).reshape(dil * l, hq), ((0, 0), (0, LANES - hq)))


def _stats_to_rows(lanes, dil):
    s = lanes.shape[0]
    return lanes[:, :DIL_SLOTS].reshape(s // dil, dil, DIL_SLOTS).transpose(1, 2, 0)


def _rope_angles(positions):
    inv_freq = ROPE_THETA ** (-jnp.arange(0, ROT_DIM, 2, dtype=F32) / ROT_DIM)
    ang = positions.astype(F32)[:, None] * inv_freq
    return jnp.concatenate([jnp.cos(ang), jnp.sin(ang)], axis=1)


class _NoPlan:
    def gather(self, stage):
        return None

    def gathered(self, stage, couts, wb):
        pass

    def exchange(self, stage, grads):
        return None

    def exchanged(self, stage, couts):
        pass


def _local_step(x, mem, positions, target, wb, sp, plan=None, *, t_row=256, t_mm=512, tq_a=256, tq_b=128,
                sub_a=2, sub_b=4):
    s = x.shape[0]
    plan = plan or _NoPlan()
    cs = _rope_angles(positions)
    e_mat = _rope_select_matrix()

    (h0b, za, *zb), couts = _proj_all(x, sp["ln_in_g"], sp["ln_in_b"], wb["w_in_seg"], cs, e_mat, t=t_mm,
                                      comm=plan.gather("proj"))
    plan.gathered("proj", couts, wb)
    sub_a = max(1, min(sub_a, s // tq_a))
    subs_b = [max(1, min(sub_b, s // dil // tq_b)) for dil in DILATIONS]
    out_a, lse_a, couts = _swa_fwd_s(za, qcol=0, kcol=4, vcol=5, hq=WIN_Q_HEADS, hkv=WIN_KV_HEADS, w=WIN_HALF,
                                     tq=tq_a, sub=sub_a, sink=sp["attn_sink"], name="attn_a_fwd",
                                     comm=plan.gather("attn_a"))
    plan.gathered("attn_a", couts, wb)
    o_g, lse_g = [], []
    for gi in range(3):
        o, l, couts = _swa_fwd_s(zb[gi], qcol=0, kcol=1, vcol=2, hq=DIL_SLOTS, hkv=DIL_SLOTS, w=DIL_HALF, tq=tq_b,
                                 sub=subs_b[gi], sink=None, name=f"attn_b{gi}_fwd",
                                 comm=plan.gather(f"attn_b{gi}"))
        plan.gathered(f"attn_b{gi}", couts, wb)
        o_g.append(_from_residue(o))
        lse_g.append(_stats_to_lanes(l))
    out_a = out_a[0]
    mixed_b, out_b, lse_b = _combine_fwd(out_a, o_g, lse_g, sp["g_win"], sp["g_dil"], t=t_row)
    r1, h1b = _mixproj_fwd(mixed_b, wb["w_mix_out"], x, sp["ln_in_g"], sp["ln_in_b"], sp["ln1_g"], sp["ln1_b"],
                           t=t_row)
    mem_nb, kx, vx = _mem_fwd(mem, sp["mem_ln_g"], sp["mem_ln_b"], wb["w_xk"], wb["w_xv"])
    r2, h2b, qxb, oxb, lse_x = _xattn_fwd(h1b, r1, kx, vx, wb["w_xq"], wb["w_xo"], sp["ln1_g"], sp["ln1_b"],
                                          sp["ln2_g"], sp["ln2_b"], t=t_row)
    g = _mm(h2b, wb["w_gate"], mode="nn", out_dtype=F32, tm=t_mm, tn=D_FF, name="ff_gate")
    u = _mm(h2b, wb["w_up"], mode="nn", out_dtype=F32, tm=t_mm, tn=D_FF, name="ff_up")
    tb = _conv_fwd(g, u, sp["conv_w"], sp["conv_b"], t=t_row)
    dr3, dr3b, st3 = _down_ln3(tb, wb["w_down"], r2, target, sp["ln2_g"], sp["ln2_b"], sp["ln3_g"], sp["ln3_b"],
                               t=t_row)

    grads = {}
    du, dgc, st_conv = _conv_bwd_a(dr3b, wb["w_down"], g, u, sp["conv_w"], sp["conv_b"], t=t_row)
    dg = _conv_bwd_b(dgc, sp["conv_w"], t=t_row)
    tk = min(1024, s)
    grads["w_down"] = _mm(tb, dr3b, mode="tn", out_dtype=BF16, tm=D_FF // 2, tn=D_MODEL, tk=tk, name="dw_down")
    grads["w_gate"] = _mm(h2b, dg, mode="tn", out_dtype=BF16, tm=D_MODEL, tn=D_FF // 2, tk=tk, name="dw_gate")
    grads["w_up"] = _mm(h2b, du, mode="tn", out_dtype=BF16, tm=D_MODEL, tn=D_FF // 2, tk=tk, name="dw_up")
    dh2, couts = _mm2_nt(dg, wb["w_gate"], du, wb["w_up"], dr3, add_scale=ALPHA, tm=t_mm, name="dh2",
                         comm=plan.exchange("dh2", grads))
    plan.exchanged("dh2", couts)

    dr2, dr2b, st2 = _ln_bwd(dh2, r2, sp["ln2_g"], t=t_row, name="ln2_bwd", want_bf16=True)
    (dh1, dqxb, dkx, dvx), couts = _xattn_bwd(dr2, qxb, oxb, lse_x, kx, vx, wb["w_xq"], wb["w_xo"], t=t_row,
                                              comm=plan.exchange("xattn", grads))
    plan.exchanged("xattn", couts)
    grads["w_xo"] = _mm(oxb, dr2b, mode="tn", out_dtype=BF16, tm=D_MODEL, tn=D_MODEL, tk=tk, name="dw_xo")
    grads["w_xq"] = _mm(h1b, dqxb, mode="tn", out_dtype=BF16, tm=D_MODEL, tn=D_MODEL, tk=tk, name="dw_xq")
    grads["w_xk"], grads["w_xv"], st_mem = _mem_bwd(dkx, dvx, mem, sp["mem_ln_g"], sp["mem_ln_b"],
                                                    wb["w_xk"], wb["w_xv"])

    dr1, dr1b, st1 = _ln_bwd(dh1, r1, sp["ln1_g"], t=t_row, name="ln1_bwd", want_bf16=True)
    grads["w_mix_out"] = _mm(mixed_b, dr1b, mode="tn", out_dtype=BF16, tm=D_MODEL, tn=D_MODEL, tk=tk,
                             name="dw_mix")
    dmixed = _mm(dr1b, wb["w_mix_out"], mode="nt", out_dtype=F32, tm=t_mm, tn=D_MODEL, name="dmixed")
    do_a, do_b, dl_a, dl_b, st_mix = _combine_bwd(dmixed, out_a, out_b, sp["g_win"], sp["g_dil"], t=t_row)
    (dqa, dka, dva, dsink), couts = _swa_bwd_s(
        za, do_a[None], lse_a, _stats_to_rows(dl_a, 1), cs[None], e_mat, qcol=0, kcol=4, vcol=5, hq=WIN_Q_HEADS,
        hkv=WIN_KV_HEADS, w=WIN_HALF, tq=tq_a, sub=sub_a, sink=sp["attn_sink"], name="attn_a_bwd",
        comm=plan.exchange("attn_a", grads))
    plan.exchanged("attn_a", couts)
    dqs, dks, dvs = [], [], []
    for gi, dil in enumerate(DILATIONS):
        (dq, dk, dv), _ = _swa_bwd_s(
            zb[gi], _to_residue(do_b, dil), _stats_to_rows(lse_b, dil), _stats_to_rows(dl_b, dil),
            _to_residue(cs, dil), e_mat, qcol=0, kcol=1, vcol=2, hq=DIL_SLOTS, hkv=DIL_SLOTS, w=DIL_HALF, tq=tq_b,
            sub=subs_b[gi], sink=None, name=f"attn_b{gi}_bwd")
        dqs.append(_from_residue(dq))
        dks.append(_from_residue(dk))
        dvs.append(_from_residue(dv))
    dz = jnp.concatenate([dqa[0], dka[0], dva[0]] + dqs + dks + dvs, axis=1)
    grads["w_in"] = _mm(h0b, dz, mode="tn", out_dtype=BF16, tm=D_MODEL, tn=IN_WIDTH // 7, tk=tk, name="dw_in")
    dh0 = _mm(dz, wb["w_in"], mode="nt", out_dtype=F32, tm=t_mm, tn=D_MODEL, add=dr1, add_scale=ALPHA, name="dh0")
    grad_x, st0 = _ln_bwd(dh0, x, sp["ln_in_g"], t=t_row, name="ln_in_bwd", want_bf16=False)

    small = {
        "loss": st3[2:3, 0:1],
        "ln_in_g": st0[0:1], "ln_in_b": st0[1:2],
        "attn_sink": dsink[:, 0].reshape(1, WIN_Q_HEADS),
        "g_win": st_mix[0:1], "g_dil": st_mix[1:2],
        "ln1_g": st1[0:1], "ln1_b": st1[1:2],
        "mem_ln_g": st_mem[0:1], "mem_ln_b": st_mem[1:2],
        "ln2_g": st2[0:1], "ln2_b": st2[1:2],
        "conv_w": st_conv[0:3], "conv_b": st_conv[3:4],
        "ln3_g": st3[0:1], "ln3_b": st3[1:2],
    }
    return grad_x, grads, small


def _swap_sibling(arrays):
    n = len(arrays)

    def body(*refs):
        src, dst = refs[:n], refs[n:2 * n]
        send_sems, recv_sems = refs[2 * n:]
        x, y, c, _ = _place()
        copies = [pltpu.make_async_remote_copy(
            src_ref=src[a], dst_ref=dst[a], send_sem=send_sems.at[a], recv_sem=recv_sems.at[a],
            device_id=(x, y, 1 - c), device_id_type=MESH_IDS) for a in range(n)]
        for cp in copies:
            cp.start()
        for cp in copies:
            cp.wait_recv()
        for cp in copies:
            cp.wait_send()

    return pl.pallas_call(
        body, name="swap_sibling", in_specs=[ANY] * n, out_specs=[ANY] * n,
        out_shape=[jax.ShapeDtypeStruct(a.shape, a.dtype) for a in arrays],
        scratch_shapes=[pltpu.SemaphoreType.DMA((n,)), pltpu.SemaphoreType.DMA((n,))],
    )(*arrays)


def _row_tile(rows, cols, itemsize=4, budget=1 << 20):
    best = None
    for t in range(16, rows + 1, 16):
        if rows % t == 0 and t * cols * itemsize <= budget:
            best = t
    return best or rows


def _sum_slots(stack, *, name):
    n, r, c = stack.shape
    t = _row_tile(r, c)

    def body(s_ref, o_ref):
        acc = s_ref[0].astype(F32)
        for q in range(1, n):
            acc = acc + s_ref[q].astype(F32)
        o_ref[...] = acc

    return pl.pallas_call(
        body, name=name, grid=(r // t,), in_specs=[pl.BlockSpec((n, t, c), lambda i: (0, i, 0))],
        out_specs=pl.BlockSpec((t, c), lambda i: (i, 0)), out_shape=jax.ShapeDtypeStruct((r, c), F32),
        compiler_params=_cparams(dimension_semantics=("parallel",)),
    )(stack)


def _adamw(w, m, v, p, q, *, name):
    r, c = w.shape
    t = _row_tile(r, c, budget=1 << 19)

    def body(*refs):
        if q is None:
            w_ref, m_ref, v_ref, p_ref, g_ref, d_ref, nm_ref, nv_ref = refs
            g = p_ref[...]
        else:
            w_ref, m_ref, v_ref, p_ref, q_ref, g_ref, d_ref, nm_ref, nv_ref = refs
            g = p_ref[...] + q_ref[...]
        nm = ADAM_B1 * m_ref[...] + (1.0 - ADAM_B1) * g
        nv = ADAM_B2 * v_ref[...] + (1.0 - ADAM_B2) * (g * g)
        m_hat = nm / (1.0 - ADAM_B1 ** ADAM_STEP)
        v_hat = nv / (1.0 - ADAM_B2 ** ADAM_STEP)
        g_ref[...] = g
        d_ref[...] = -ADAM_LR * (m_hat / (jnp.sqrt(v_hat) + ADAM_EPS) + ADAM_WD * w_ref[...])
        nm_ref[...] = nm
        nv_ref[...] = nv

    tile = pl.BlockSpec((t, c), lambda i: (i, 0))
    args = [w, m, v, p] + ([] if q is None else [q])
    sh = jax.ShapeDtypeStruct((r, c), F32)
    return pl.pallas_call(
        body, name=name, grid=(r // t,), in_specs=[tile] * len(args), out_specs=[tile] * 4, out_shape=[sh] * 4,
        compiler_params=_cparams(dimension_semantics=("parallel",)),
    )(*args)


BIG = ("w_in", "w_mix_out", "w_xq", "w_xk", "w_xv", "w_xo", "w_gate", "w_up", "w_down")
COL_SHARDED = ("w_in", "w_gate", "w_up")
WEIGHTS = ("ln_in_g", "ln_in_b", "w_in", "attn_sink", "g_win", "g_dil", "w_mix_out", "ln1_g", "ln1_b",
           "mem_ln_g", "mem_ln_b", "w_xq", "w_xk", "w_xv", "w_xo", "ln2_g", "ln2_b", "w_gate", "w_up",
           "conv_w", "conv_b", "w_down", "ln3_g", "ln3_b")
SMALL = tuple(k for k in WEIGHTS if k not in BIG)
PACK_COLS = 1024
CONV_SHARD = D_FF // N_CHIPS
CONV_WIDTH_ROWS = 3
SMALL_ROWS = 32


GATHER_STAGES = {"proj": ("w_mix_out", "w_xq", "w_xk", "w_xv", "w_xo"), "attn_a": ("w_gate", "w_up"),
                 "attn_b0": ("w_down",)}
EXCHANGE_STAGES = {"dh2": ("w_down",), "xattn": ("w_gate", "w_up"),
                   "attn_a": ("w_xo", "w_xq", "w_xk", "w_xv", "w_mix_out")}


def _full_weight(k, g4):
    if k in COL_SHARDED:
        return g4.transpose(1, 0, 2).reshape(g4.shape[1], N_CHIPS * g4.shape[2])
    return g4.reshape(N_CHIPS * g4.shape[1], g4.shape[2])


def _grad_parts(k, gk):
    gk = gk.astype(BF16)
    if k in COL_SHARDED:
        return gk.reshape(gk.shape[0], N_CHIPS, gk.shape[1] // N_CHIPS).transpose(1, 0, 2)
    return gk.reshape(N_CHIPS, gk.shape[0] // N_CHIPS, gk.shape[1])


class _Plan:
    def __init__(self, shards):
        self.shards = shards
        self.recv = {}

    def gather(self, stage):
        names = GATHER_STAGES.get(stage)
        return _ChipGather([self.shards[k] for k in names]) if names else None

    def gathered(self, stage, couts, wb):
        for k, g4 in zip(GATHER_STAGES.get(stage, ()), couts):
            wb[k] = _full_weight(k, g4)

    def exchange(self, stage, grads):
        names = EXCHANGE_STAGES.get(stage)
        return _ChipExchange([_grad_parts(k, grads[k]) for k in names]) if names else None

    def exchanged(self, stage, couts):
        for k, r4 in zip(EXCHANGE_STAGES.get(stage, ()), couts):
            self.recv[k] = r4


def _pack_rows(a):
    r, n = a.shape
    per = -(-n // PACK_COLS)
    return jnp.pad(a, ((0, 0), (0, per * PACK_COLS - n))).reshape(r * per, PACK_COLS)


def _unpack_rows(p, r, n):
    per = -(-n // PACK_COLS)
    return p.reshape(r, per * PACK_COLS)[:, :n]


def _pack(pieces, rows_total):
    cat = jnp.concatenate([_pack_rows(a) for a in pieces], axis=0)
    return jnp.pad(cat, ((0, rows_total - cat.shape[0]), (0, 0)))


def _unpack(p, shapes):
    out, at = [], 0
    for r, n in shapes:
        per = -(-n // PACK_COLS)
        out.append(_unpack_rows(p[at:at + r * per], r, n))
        at += r * per
    return out


def kernel(x, mem, positions, ln_in_g, ln_in_b, w_in, attn_sink, g_win, g_dil, w_mix_out, ln1_g, ln1_b, mem_ln_g, mem_ln_b, w_xq, w_xk, w_xv, w_xo, ln2_g, ln2_b, w_gate, w_up, conv_w, conv_b, w_down, ln3_g, ln3_b, loss_target, m_ln_in_g, m_ln_in_b, m_w_in, m_attn_sink, m_g_win, m_g_dil, m_w_mix_out, m_ln1_g, m_ln1_b, m_mem_ln_g, m_mem_ln_b, m_w_xq, m_w_xk, m_w_xv, m_w_xo, m_ln2_g, m_ln2_b, m_w_gate, m_w_up, m_conv_w, m_conv_b, m_w_down, m_ln3_g, m_ln3_b, v_ln_in_g, v_ln_in_b, v_w_in, v_attn_sink, v_g_win, v_g_dil, v_w_mix_out, v_ln1_g, v_ln1_b, v_mem_ln_g, v_mem_ln_b, v_w_xq, v_w_xk, v_w_xv, v_w_xo, v_ln2_g, v_ln2_b, v_w_gate, v_w_up, v_conv_w, v_conv_b, v_w_down, v_ln3_g, v_ln3_b):
    given = dict(locals())
    shape_of = {k: given[k].shape for k in WEIGHTS}
    as2d = lambda a: a.reshape(-1, a.shape[-1])
    w2 = {k: as2d(given[k]) for k in WEIGHTS}
    m2 = {k: as2d(given["m_" + k]) for k in WEIGHTS}
    v2 = {k: as2d(given["v_" + k]) for k in WEIGHTS}
    chip = 2 * lax.axis_index("x") + lax.axis_index("y")

    plan = _Plan({k: w2[k].astype(BF16) for k in BIG})
    conv_pack = jnp.pad(w2["conv_w"], ((0, 8 - CONV_WIDTH_ROWS), (0, PACK_COLS - CONV_SHARD)))
    g_in, g_conv = _comm_only(_ChipGather([plan.shards["w_in"], conv_pack]), "gather_w_in")
    w_in_full = _full_weight("w_in", g_in)
    wb = {"w_in": w_in_full,
          "w_in_seg": jnp.concatenate([w_in_full[:, a:b] for a, b in _proj_column_ranges()], axis=1)}
    conv_full = g_conv[:, :CONV_WIDTH_ROWS, :CONV_SHARD].transpose(1, 0, 2).reshape(CONV_WIDTH_ROWS, D_FF)
    sp = {k: w2[k] for k in SMALL}
    sp["conv_w"] = conv_full

    grad_x, grads, small = _local_step(x[0], mem[0], positions[0], loss_target[0], wb, sp, plan)

    small_keys = ("loss",) + SMALL
    small_shapes = [small[k].shape for k in small_keys]
    small_pack = _pack([small[k] for k in small_keys], SMALL_ROWS)
    plan.recv["w_in"], small_all = _comm_only(
        _ChipExchange([_grad_parts("w_in", grads["w_in"])], small_pack), "exchange_w_in")
    chip_sums = [_sum_slots(plan.recv[k], name=f"sum_chips_{k}") for k in BIG]
    sibling_sums = _swap_sibling(chip_sums)
    small_sum = _sum_slots(small_all, name="sum_small")
    small_g = dict(zip(small_keys, _unpack(small_sum, small_shapes)))
    loss = small_g["loss"][0, 0]

    res = {}
    for k, p, q in zip(BIG, chip_sums, sibling_sums):
        res[k] = _adamw(w2[k], m2[k], v2[k], p, q, name=f"adamw_{k}")
    small_g["conv_w"] = lax.dynamic_slice_in_dim(small_g["conv_w"], chip * CONV_SHARD, CONV_SHARD, axis=1)
    adam_shapes = [w2[k].shape for k in SMALL]
    packs = [_pack([d[k] for k in SMALL], SMALL_ROWS) for d in (w2, m2, v2, small_g)]
    small_res = [_unpack(o, adam_shapes) for o in _adamw(*packs, None, name="adamw_small")]
    for i, k in enumerate(SMALL):
        res[k] = tuple(o[i] for o in small_res)

    outs = [loss, grad_x[None]]
    for slot in range(4):
        outs += [res[k][slot].reshape(shape_of[k]) for k in WEIGHTS]
    return tuple(outs)
```

```python
import functools
import math

import jax
import jax.numpy as jnp
from jax import lax
from jax.experimental import pallas as pl
from jax.experimental.pallas import tpu as pltpu

F32 = jnp.float32
BF16 = jnp.bfloat16

D_MODEL = 1024
HEAD_DIM = 64
WIN_Q_HEADS = 8
WIN_KV_HEADS = 2
WIN_HALF = 128
DIL_SLOTS = 8
DILATIONS = (1, 4, 16)
DIL_HALF = 64
ROT_DIM = 16
ROPE_THETA = 500000.0
X_HEADS = 4
X_HEAD_DIM = 256
D_FF = 2816
A_Q = 512
A_KV = 128
A_WIDTH = A_Q + 2 * A_KV
B_QKV = 1536
IN_WIDTH = 5376
ALPHA = 2.0 ** 0.25
LN_EPS = 1e-5
NEG_INF = -1e30
LANES = 128
N_CHIPS = 4
N_DEV = 8

ADAM_LR = 0.001
ADAM_B1 = 0.9
ADAM_B2 = 0.999
ADAM_EPS = 1e-08
ADAM_WD = 0.01
ADAM_STEP = 10

VMEM_LIMIT = 56 * 1024 * 1024


def _cparams(**kw):
    return pltpu.CompilerParams(vmem_limit_bytes=VMEM_LIMIT, **kw)


def _dot(a, b):
    return lax.dot_general(a, b, (((1,), (0,)), ((), ())), preferred_element_type=F32)


def _dot_nt(a, b):
    return lax.dot_general(a, b, (((1,), (1,)), ((), ())), preferred_element_type=F32)


def _dot_tn(a, b):
    return lax.dot_general(a, b, (((0,), (0,)), ((), ())), preferred_element_type=F32)


def _ln(x, g, b):
    mu = jnp.mean(x, axis=-1, keepdims=True)
    xc = x - mu
    var = jnp.mean(xc * xc, axis=-1, keepdims=True)
    return xc * lax.rsqrt(var + LN_EPS) * g + b


def _ln_bwd_math(dy, r, g):
    mu = jnp.mean(r, axis=-1, keepdims=True)
    xc = r - mu
    var = jnp.mean(xc * xc, axis=-1, keepdims=True)
    rstd = lax.rsqrt(var + LN_EPS)
    xhat = xc * rstd
    dxhat = dy * g
    m1 = jnp.mean(dxhat, axis=-1, keepdims=True)
    m2 = jnp.mean(dxhat * xhat, axis=-1, keepdims=True)
    dr = rstd * (dxhat - m1 - xhat * m2)
    return dr, jnp.sum(dy * xhat, axis=0, keepdims=True), jnp.sum(dy, axis=0, keepdims=True)


def _rope(z, ta, tb, tc, sign):
    w = z.shape[1]
    reps = w // LANES
    a = jnp.tile(ta, (1, reps))
    b = jnp.tile(tb, (1, reps))
    c = jnp.tile(tc, (1, reps))
    return z * a + sign * (pltpu.roll(z, w - 8, 1) * b + pltpu.roll(z, 8, 1) * c)


def _shift_rows(x, prev_row, next_row):
    t = x.shape[0]
    row = lax.broadcasted_iota(jnp.int32, x.shape, 0)
    xm1 = jnp.where(row == 0, prev_row, pltpu.roll(x, 1, 0))
    xp1 = jnp.where(row == t - 1, next_row, pltpu.roll(x, t - 1, 0))
    return xm1, xp1


def _rope_tabs(cs, e_mat):
    tabs = lax.dot_general(cs, e_mat, (((1,), (0,)), ((), ())), preferred_element_type=F32,
                           precision=lax.Precision.HIGHEST)
    lane = lax.broadcasted_iota(jnp.int32, (cs.shape[0], LANES), 1)
    ones = jnp.where((lane & (HEAD_DIM - 1)) >= ROT_DIM, 1.0, 0.0)
    return tabs[:, :LANES] + ones, tabs[:, LANES:2 * LANES], tabs[:, 2 * LANES:]


def _rope_select_matrix():
    half = ROT_DIM // 2
    e = [[0.0] * (3 * LANES) for _ in range(ROT_DIM)]
    for lane in range(LANES):
        d = lane % HEAD_DIM
        if d < half:
            e[d][lane] = 1.0
            e[half + d][LANES + lane] = -1.0
        elif d < ROT_DIM:
            e[d - half][lane] = 1.0
            e[d][2 * LANES + lane] = 1.0
    return jnp.array(e, F32)


MESH_IDS = pl.DeviceIdType.MESH
ANY = pl.BlockSpec(memory_space=pl.ANY)


def _place():
    x, y, c = lax.axis_index("x"), lax.axis_index("y"), lax.axis_index("c")
    other_chips = [(1 - x, y), (x, 1 - y), (1 - x, 1 - y)]
    return x, y, c, other_chips


class _ChipGather:
    def __init__(self, shards):
        self.inputs = list(shards)
        n = len(shards)
        self.out_shape = [jax.ShapeDtypeStruct((N_CHIPS,) + a.shape, a.dtype) for a in shards]
        self.scratch = [pltpu.SemaphoreType.DMA((6 * n,)), pltpu.SemaphoreType.DMA((6 * n,)),
                        pltpu.SemaphoreType.DMA((n,))]

    def _copies(self, src, dst, sems):
        send_sems, recv_sems, local_sems = sems
        x, y, c, chips = _place()
        mine = 2 * x + y
        n = len(src)
        local, sends, recvs, passes, pass_recvs = [], [], [], [], []
        for a in range(n):
            half = src[a].shape[0] // 2
            my_rows, other_rows = pl.ds(c * half, half), pl.ds((1 - c) * half, half)
            local.append(pltpu.make_async_copy(src[a], dst[a].at[mine], local_sems.at[a]))
            for j, (px, py) in enumerate(chips):
                k, k2, slot = 3 * a + j, 3 * n + 3 * a + j, 2 * px + py
                sends.append(pltpu.make_async_remote_copy(
                    src_ref=src[a].at[my_rows], dst_ref=dst[a].at[mine, my_rows], send_sem=send_sems.at[k],
                    recv_sem=recv_sems.at[k], device_id=(px, py, c), device_id_type=MESH_IDS))
                recvs.append(pltpu.make_async_remote_copy(
                    src_ref=src[a].at[my_rows], dst_ref=dst[a].at[slot, my_rows], send_sem=send_sems.at[k],
                    recv_sem=recv_sems.at[k], device_id=(px, py, c), device_id_type=MESH_IDS))
                passes.append(pltpu.make_async_remote_copy(
                    src_ref=dst[a].at[slot, my_rows], dst_ref=dst[a].at[slot, my_rows], send_sem=send_sems.at[k2],
                    recv_sem=recv_sems.at[k2], device_id=(x, y, 1 - c), device_id_type=MESH_IDS))
                pass_recvs.append(pltpu.make_async_remote_copy(
                    src_ref=dst[a].at[slot, my_rows], dst_ref=dst[a].at[slot, other_rows],
                    send_sem=send_sems.at[k2], recv_sem=recv_sems.at[k2], device_id=(x, y, 1 - c),
                    device_id_type=MESH_IDS))
        return local, sends, recvs, passes, pass_recvs

    def start(self, src, dst, sems):
        local, sends, _, _, _ = self._copies(src, dst, sems)
        for cp in local + sends:
            cp.start()

    def wait(self, src, dst, sems):
        local, sends, recvs, passes, pass_recvs = self._copies(src, dst, sems)
        for idx, landed in enumerate(recvs):
            landed.wait_recv()
            if passes:
                passes[idx].start()
        for cp in pass_recvs:
            cp.wait_recv()
        for cp in sends + passes:
            cp.wait_send()
        for cp in local:
            cp.wait()


class _ChipExchange:
    def __init__(self, parts, small=None):
        self.inputs = list(parts) + ([small] if small is not None else [])
        self.n = len(parts)
        self.has_small = small is not None
        self.out_shape = [jax.ShapeDtypeStruct(a.shape, a.dtype) for a in parts]
        n_sem, n_loc = 3 * self.n, self.n
        if self.has_small:
            self.out_shape.append(jax.ShapeDtypeStruct((N_DEV,) + small.shape, small.dtype))
            n_sem, n_loc = n_sem + N_DEV - 1, n_loc + 1
        self.scratch = [pltpu.SemaphoreType.DMA((n_sem,)), pltpu.SemaphoreType.DMA((n_sem,)),
                        pltpu.SemaphoreType.DMA((n_loc,))]

    def _copies(self, src, dst, sems):
        send_sems, recv_sems, local_sems = sems
        x, y, c, chips = _place()
        mine = 2 * x + y
        n = self.n
        local, sends, recvs = [], [], []
        for a in range(n):
            local.append(pltpu.make_async_copy(src[a].at[mine], dst[a].at[mine], local_sems.at[a]))
            for j, (px, py) in enumerate(chips):
                k = 3 * a + j
                sends.append(pltpu.make_async_remote_copy(
                    src_ref=src[a].at[2 * px + py], dst_ref=dst[a].at[mine], send_sem=send_sems.at[k],
                    recv_sem=recv_sems.at[k], device_id=(px, py, c), device_id_type=MESH_IDS))
                recvs.append(pltpu.make_async_remote_copy(
                    src_ref=src[a].at[mine], dst_ref=dst[a].at[2 * px + py], send_sem=send_sems.at[k],
                    recv_sem=recv_sems.at[k], device_id=(px, py, c), device_id_type=MESH_IDS))
        if self.has_small:
            me_dev = 4 * x + 2 * y + c
            local.append(pltpu.make_async_copy(src[n], dst[n].at[me_dev], local_sems.at[n]))
            for mask in range(1, N_DEV):
                px, py, pc = x ^ ((mask >> 2) & 1), y ^ ((mask >> 1) & 1), c ^ (mask & 1)
                k = 3 * n + mask - 1
                sends.append(pltpu.make_async_remote_copy(
                    src_ref=src[n], dst_ref=dst[n].at[me_dev], send_sem=send_sems.at[k], recv_sem=recv_sems.at[k],
                    device_id=(px, py, pc), device_id_type=MESH_IDS))
                recvs.append(pltpu.make_async_remote_copy(
                    src_ref=src[n], dst_ref=dst[n].at[4 * px + 2 * py + pc], send_sem=send_sems.at[k],
                    recv_sem=recv_sems.at[k], device_id=(px, py, pc), device_id_type=MESH_IDS))
        return local, sends, recvs, [], []

    start = _ChipGather.start
    wait = _ChipGather.wait


def _pcall(body, *, name, grid, in_specs, out_specs, out_shape, args, scratch_shapes=(), dims=None, comm=None):
    in_specs, out_specs, out_shape = list(in_specs), list(out_specs), list(out_shape)
    scratch_shapes = list(scratch_shapes)
    if comm is None:
        outs = pl.pallas_call(
            body, name=name, grid=grid, in_specs=in_specs, out_specs=out_specs, out_shape=out_shape,
            scratch_shapes=scratch_shapes, compiler_params=_cparams(dimension_semantics=dims),
        )(*args)
        return list(outs), []
    n_in, n_out, n_scr = len(in_specs), len(out_specs), len(scratch_shapes)
    n_cin, n_cout = len(comm.inputs), len(comm.out_shape)

    def wrapped(*refs):
        ins, refs = refs[:n_in], refs[n_in:]
        cins, refs = refs[:n_cin], refs[n_cin:]
        outs, refs = refs[:n_out], refs[n_out:]
        couts, refs = refs[:n_cout], refs[n_cout:]
        scr, csems = refs[:n_scr], refs[n_scr:]
        first = last = None
        for axis, size in enumerate(grid):
            pid = pl.program_id(axis)
            f, l = pid == 0, pid == size - 1
            first = f if first is None else first & f
            last = l if last is None else last & l

        @pl.when(first)
        def _():
            comm.start(cins, couts, csems)

        body(*ins, *outs, *scr)

        @pl.when(last)
        def _():
            comm.wait(cins, couts, csems)

    res = pl.pallas_call(
        wrapped, name=name, grid=grid, in_specs=in_specs + [ANY] * n_cin, out_specs=out_specs + [ANY] * n_cout,
        out_shape=out_shape + list(comm.out_shape), scratch_shapes=scratch_shapes + list(comm.scratch),
        compiler_params=_cparams(dimension_semantics=("arbitrary",) * len(grid)),
    )(*args, *comm.inputs)
    return list(res[:n_out]), list(res[n_out:])


def _comm_only(comm, name):
    def body(*refs):
        n_cin, n_cout = len(comm.inputs), len(comm.out_shape)
        cins, couts, csems = refs[:n_cin], refs[n_cin:n_cin + n_cout], refs[n_cin + n_cout:]
        comm.start(cins, couts, csems)
        comm.wait(cins, couts, csems)

    return list(pl.pallas_call(
        body, name=name, in_specs=[ANY] * len(comm.inputs), out_specs=[ANY] * len(comm.out_shape),
        out_shape=list(comm.out_shape), scratch_shapes=list(comm.scratch),
    )(*comm.inputs))


def _mm(a, b, *, mode, out_dtype, tm, tn, tk=None, add=None, add_scale=1.0, name, comm=None):
    if mode in ("nn", "nt"):
        m, k = a.shape
        n = b.shape[1] if mode == "nn" else b.shape[0]
        assert m % tm == 0 and n % tn == 0
        dot = _dot if mode == "nn" else _dot_nt

        def body(*refs):
            if add is None:
                a_ref, b_ref, o_ref = refs
                o_ref[...] = dot(a_ref[...], b_ref[...]).astype(out_dtype)
            else:
                a_ref, b_ref, c_ref, o_ref = refs
                o_ref[...] = (dot(a_ref[...], b_ref[...]) + add_scale * c_ref[...]).astype(out_dtype)

        b_spec = (pl.BlockSpec((k, tn), lambda i, j: (0, j)) if mode == "nn"
                  else pl.BlockSpec((tn, k), lambda i, j: (j, 0)))
        in_specs = [pl.BlockSpec((tm, k), lambda i, j: (i, 0)), b_spec]
        args = [a, b]
        if add is not None:
            in_specs.append(pl.BlockSpec((tm, tn), lambda i, j: (i, j)))
            args.append(add)
        outs, couts = _pcall(
            body, name=name, grid=(m // tm, n // tn), in_specs=in_specs,
            out_specs=[pl.BlockSpec((tm, tn), lambda i, j: (i, j))],
            out_shape=[jax.ShapeDtypeStruct((m, n), out_dtype)], args=args, dims=("parallel", "parallel"),
            comm=comm)
        return outs[0] if comm is None else (outs[0], couts)
    assert mode == "tn" and add is None and comm is None
    kk, m = a.shape
    n = b.shape[1]
    assert m % tm == 0 and n % tn == 0 and kk % tk == 0
    nk = kk // tk

    def body(a_ref, b_ref, o_ref, acc_ref):
        kstep = pl.program_id(2)

        @pl.when(kstep == 0)
        def _():
            acc_ref[...] = jnp.zeros_like(acc_ref)

        acc_ref[...] += _dot_tn(a_ref[...], b_ref[...])

        @pl.when(kstep == nk - 1)
        def _():
            o_ref[...] = acc_ref[...].astype(out_dtype)

    return pl.pallas_call(
        body, name=name, grid=(m // tm, n // tn, nk),
        in_specs=[pl.BlockSpec((tk, tm), lambda i, j, s: (s, i)), pl.BlockSpec((tk, tn), lambda i, j, s: (s, j))],
        out_specs=pl.BlockSpec((tm, tn), lambda i, j, s: (i, j)),
        out_shape=jax.ShapeDtypeStruct((m, n), out_dtype),
        scratch_shapes=[pltpu.VMEM((tm, tn), F32)],
        compiler_params=_cparams(dimension_semantics=("parallel", "parallel", "arbitrary")),
    )(a, b)


def _mm2_nt(a1, b1, a2, b2, add, *, add_scale, tm, name, comm=None):
    m, k = a1.shape
    n = b1.shape[0]

    def body(a1_ref, b1_ref, a2_ref, b2_ref, c_ref, o_ref):
        o_ref[...] = (_dot_nt(a1_ref[...], b1_ref[...]) + _dot_nt(a2_ref[...], b2_ref[...])
                      + add_scale * c_ref[...])

    a_spec = pl.BlockSpec((tm, k), lambda i: (i, 0))
    b_spec = pl.BlockSpec((n, k), lambda i: (0, 0))
    o_spec = pl.BlockSpec((tm, n), lambda i: (i, 0))
    outs, couts = _pcall(body, name=name, grid=(m // tm,), in_specs=[a_spec, b_spec, a_spec, b_spec, o_spec],
                         out_specs=[o_spec], out_shape=[jax.ShapeDtypeStruct((m, n), F32)],
                         args=[a1, b1, a2, b2, add], dims=("parallel",), comm=comm)
    return outs[0], couts


def _ln_bwd(dy, r, g, *, t, name, want_bf16):
    s = r.shape[0]

    def body(dy_ref, r_ref, g_ref, *outs):
        i = pl.program_id(0)
        dr, dg, db = _ln_bwd_math(dy_ref[...], r_ref[...], g_ref[...])
        outs[0][...] = dr
        if want_bf16:
            outs[1][...] = dr.astype(BF16)
        st_ref = outs[-1]

        @pl.when(i == 0)
        def _():
            st_ref[...] = jnp.zeros_like(st_ref)

        st_ref[0:1, :] += dg
        st_ref[1:2, :] += db

    tile = pl.BlockSpec((t, D_MODEL), lambda i: (i, 0))
    out_specs = [tile] + ([tile] if want_bf16 else []) + [pl.BlockSpec((8, D_MODEL), lambda i: (0, 0))]
    out_shape = ([jax.ShapeDtypeStruct((s, D_MODEL), F32)]
                 + ([jax.ShapeDtypeStruct((s, D_MODEL), BF16)] if want_bf16 else [])
                 + [jax.ShapeDtypeStruct((8, D_MODEL), F32)])
    return pl.pallas_call(
        body, name=name, grid=(s // t,),
        in_specs=[tile, tile, pl.BlockSpec((1, D_MODEL), lambda i: (0, 0))],
        out_specs=out_specs, out_shape=out_shape,
        compiler_params=_cparams(dimension_semantics=("arbitrary",)),
    )(dy, r, g)


PROJ_COLS = 256
PROJ_SEGMENTS = ((1, 0, (1, 1, 2)),) + tuple(
    (dil, A_WIDTH + gi * B_QKV, (1, 1, 1, 1, 0, 0)) for gi, dil in enumerate(DILATIONS))


def _proj_column_ranges():
    wd = DIL_SLOTS * HEAD_DIM
    ranges = [(0, A_WIDTH)]
    for gi in range(len(DILATIONS)):
        ranges += [(A_WIDTH + part * B_QKV + gi * wd, A_WIDTH + part * B_QKV + (gi + 1) * wd) for part in range(3)]
    return ranges


def _proj_all(x, g, b, w_seg, cs, e_mat, *, t, comm=None):
    s = x.shape[0]
    cb = PROJ_COLS
    halves = cb // LANES

    def body(x_ref, g_ref, b_ref, w_ref, cs_ref, e_ref, h_ref, *rest):
        z_refs, scr = rest[:-1], rest[-1]
        h = _ln(x_ref[...], g_ref[...], b_ref[...]).astype(BF16)
        h_ref[...] = h
        ta, tb, tc = (jnp.tile(tab, (1, halves)) for tab in _rope_tabs(cs_ref[...], e_ref[...]))
        lane = lax.broadcasted_iota(jnp.int32, (t, cb), 1)
        slot = 0
        for z_ref, (dil, col0, kinds) in zip(z_refs, PROJ_SEGMENTS):
            for jb, kind in enumerate(kinds):
                acc = _dot(h, w_ref[:, col0 + cb * jb:col0 + cb * (jb + 1)])
                if kind:
                    z = acc * ta + (pltpu.roll(acc, cb - 8, 1) * tb + pltpu.roll(acc, 8, 1) * tc)
                    if kind == 2:
                        z = jnp.where(lane < LANES, z, acc)
                else:
                    z = acc
                if dil == 1:
                    z_ref[0, :, cb * jb:cb * (jb + 1)] = z.astype(BF16)
                    continue
                for half in range(halves):
                    scr[slot, half] = z[:, half * LANES:(half + 1) * LANES]
                for c in range(dil):
                    for half in range(halves):
                        rows = scr[slot, half, pl.ds(c, t // dil, stride=dil), :]
                        z_ref[c, :, cb * jb + half * LANES:cb * jb + (half + 1) * LANES] = rows.astype(BF16)
                slot = 1 - slot

    row = pl.BlockSpec((1, D_MODEL), lambda i: (0, 0))
    widths = [cb * len(kinds) for _, _, kinds in PROJ_SEGMENTS]
    dils = [dil for dil, _, _ in PROJ_SEGMENTS]
    outs, couts = _pcall(
        body, name="proj_all", grid=(s // t,),
        in_specs=[pl.BlockSpec((t, D_MODEL), lambda i: (i, 0)), row, row,
                  pl.BlockSpec((D_MODEL, IN_WIDTH), lambda i: (0, 0)),
                  pl.BlockSpec((t, ROT_DIM), lambda i: (i, 0)), pl.BlockSpec((ROT_DIM, 3 * LANES), lambda i: (0, 0))],
        out_specs=[pl.BlockSpec((t, D_MODEL), lambda i: (i, 0))]
        + [pl.BlockSpec((dil, t // dil, wd), lambda i: (0, i, 0)) for dil, wd in zip(dils, widths)],
        out_shape=[jax.ShapeDtypeStruct((s, D_MODEL), BF16)]
        + [jax.ShapeDtypeStruct((dil, s // dil, wd), BF16) for dil, wd in zip(dils, widths)],
        args=[x, g, b, w_seg, cs, e_mat], scratch_shapes=[pltpu.VMEM((2, halves, t, LANES), F32)],
        dims=("parallel",), comm=comm)
    return outs, couts


def _window_mask(i, tq, w, seq_len):
    tk = tq + 2 * w
    qpos = i * tq + lax.broadcasted_iota(jnp.int32, (tq, tk), 0)
    kpos = i * tq - w + lax.broadcasted_iota(jnp.int32, (tq, tk), 1)
    return (jnp.abs(qpos - kpos) <= w) & (kpos >= 0) & (kpos < seq_len)


def _swa_specs(tq, hq, hkv, n, qcol, kcol, vcol):
    qw, kw = hq * HEAD_DIM, hkv * HEAD_DIM
    cur = lambda s, i: jnp.minimum(i, n - 1)
    prv = lambda s, i: jnp.maximum(jnp.minimum(i, n - 1) - 1, 0)
    nxt = lambda s, i: jnp.minimum(i + 1, n - 1)
    q_spec = pl.BlockSpec((None, tq, qw), lambda s, i: (s, cur(s, i), qcol))
    kv_specs = [pl.BlockSpec((None, tq, kw), (lambda s, i, f=f, c=c: (s, f(s, i), c)))
                for c in (kcol, vcol) for f in (prv, cur, nxt)]
    return q_spec, kv_specs, cur, prv


def _swa_fwd(qkv, *, qcol, kcol, vcol, hq, hkv, w, tq, sink, name, comm=None):
    nseq, seq_len, _ = qkv.shape
    n = seq_len // tq
    rep = hq // hkv
    q_spec, kv_specs, _, _ = _swa_specs(tq, hq, hkv, n, qcol, kcol, vcol)

    def body(*refs):
        if sink is not None:
            sink_ref, refs = refs[0], refs[1:]
        q_ref, kp_ref, kc_ref, kn_ref, vp_ref, vc_ref, vn_ref, o_ref, lse_ref = refs
        i = pl.program_id(1)
        mask = _window_mask(i, tq, w, seq_len)
        lane = lax.broadcasted_iota(jnp.int32, (tq, LANES), 1)
        lse_acc = jnp.zeros((tq, LANES), F32)
        for g in range(hkv):
            cs = slice(g * HEAD_DIM, (g + 1) * HEAD_DIM)
            kcat = jnp.concatenate([kp_ref[tq - w:, cs], kc_ref[:, cs], kn_ref[:w, cs]], axis=0)
            vcat = jnp.concatenate([vp_ref[tq - w:, cs], vc_ref[:, cs], vn_ref[:w, cs]], axis=0)
            for r in range(rep):
                h = g * rep + r
                hs = slice(h * HEAD_DIM, (h + 1) * HEAD_DIM)
                qh = q_ref[:, hs] * 0.125
                sc = jnp.where(mask, _dot_nt(qh, kcat), NEG_INF)
                m = jnp.max(sc, axis=1, keepdims=True)
                if sink is not None:
                    m = jnp.maximum(m, sink_ref[0, h])
                p = jnp.exp(sc - m)
                den = jnp.sum(p, axis=1, keepdims=True)
                if sink is not None:
                    den = den + jnp.exp(sink_ref[0, h] - m)
                o_ref[:, hs] = _dot(p.astype(BF16), vcat) / den
                lse_acc = jnp.where(lane == h, m + jnp.log(den), lse_acc)
        lse_ref[...] = lse_acc

    in_specs = [q_spec] + kv_specs
    args = [qkv] * 7
    if sink is not None:
        in_specs = [pl.BlockSpec(memory_space=pltpu.SMEM)] + in_specs
        args = [sink] + args
    (o, lse), couts = _pcall(
        body, name=name, grid=(nseq, n), in_specs=in_specs,
        out_specs=[pl.BlockSpec((None, tq, hq * HEAD_DIM), lambda s, i: (s, i, 0)),
                   pl.BlockSpec((None, tq, LANES), lambda s, i: (s, i, 0))],
        out_shape=[jax.ShapeDtypeStruct((nseq, seq_len, hq * HEAD_DIM), F32),
                   jax.ShapeDtypeStruct((nseq, seq_len, LANES), F32)],
        args=args, dims=("parallel", "parallel"), comm=comm)
    return o, lse, couts


def _swa_bwd(qkv, do, lse, delta, cs, e_mat, *, qcol, kcol, vcol, hq, hkv, w, tq, sink, name, comm=None):
    nseq, seq_len, _ = qkv.shape
    n = seq_len // tq
    rep = hq // hkv
    qw, kw = hq * HEAD_DIM, hkv * HEAD_DIM
    tk = tq + 2 * w
    q_spec, kv_specs, cur, prv = _swa_specs(tq, hq, hkv, n, qcol, kcol, vcol)

    def body(*refs):
        if sink is not None:
            sink_ref, refs = refs[0], refs[1:]
        (q_ref, kp_ref, kc_ref, kn_ref, vp_ref, vc_ref, vn_ref, do_ref, lse_ref, dl_ref,
         cs_c, cs_p, e_ref) = refs[:13]
        outs = refs[13:]
        if sink is not None:
            dq_ref, dk_ref, dv_ref, dsink_ref, dk_acc, dv_acc = outs
        else:
            dq_ref, dk_ref, dv_ref, dk_acc, dv_acc = outs
        s_id = pl.program_id(0)
        i = pl.program_id(1)
        slot_p, slot_c, slot_n = (i + 2) % 3, i % 3, (i + 1) % 3

        if sink is not None:
            @pl.when((s_id == 0) & (i == 0))
            def _():
                dsink_ref[...] = jnp.zeros_like(dsink_ref)

        @pl.when(i < n)
        def _():
            mask = _window_mask(i, tq, w, seq_len)
            dk_acc[slot_n] = jnp.zeros((tq, kw), F32)
            dv_acc[slot_n] = jnp.zeros((tq, kw), F32)

            @pl.when(i == 0)
            def _():
                dk_acc[slot_c] = jnp.zeros((tq, kw), F32)
                dv_acc[slot_c] = jnp.zeros((tq, kw), F32)

            dq_parts, dk_parts, dv_parts = [], [], []
            for g in range(hkv):
                cs = slice(g * HEAD_DIM, (g + 1) * HEAD_DIM)
                kcat = jnp.concatenate([kp_ref[tq - w:, cs], kc_ref[:, cs], kn_ref[:w, cs]], axis=0)
                vcat = jnp.concatenate([vp_ref[tq - w:, cs], vc_ref[:, cs], vn_ref[:w, cs]], axis=0)
                dkc = jnp.zeros((tk, HEAD_DIM), F32)
                dvc = jnp.zeros((tk, HEAD_DIM), F32)
                for r in range(rep):
                    h = g * rep + r
                    hs = slice(h * HEAD_DIM, (h + 1) * HEAD_DIM)
                    qh = q_ref[:, hs] * 0.125
                    sc = jnp.where(mask, _dot_nt(qh, kcat), NEG_INF)
                    lse_h = lse_ref[:, h:h + 1]
                    dl_h = dl_ref[:, h:h + 1]
                    p = jnp.exp(sc - lse_h)
                    doh = do_ref[:, hs]
                    dp = _dot_nt(doh, vcat)
                    dsb = (p * (dp - dl_h)).astype(BF16)
                    dq_parts.append(_dot(dsb, kcat) * 0.125)
                    dkc = dkc + _dot_tn(dsb, qh)
                    dvc = dvc + _dot_tn(p.astype(BF16), doh)
                    if sink is not None:
                        ds_sink = -jnp.sum(jnp.exp(sink_ref[0, h] - lse_h) * dl_h)
                        dsink_ref[h:h + 1, :] += jnp.full((1, LANES), ds_sink, F32)
                dk_parts.append(dkc)
                dv_parts.append(dvc)
            dq = jnp.concatenate(dq_parts, axis=1)
            dq_ref[...] = _rope(dq, *_rope_tabs(cs_c[...], e_ref[...]), -1.0).astype(BF16)
            dk_all = jnp.concatenate(dk_parts, axis=1)
            dv_all = jnp.concatenate(dv_parts, axis=1)

            @pl.when(i > 0)
            def _():
                dk_acc[slot_p, tq - w:, :] += dk_all[:w]
                dv_acc[slot_p, tq - w:, :] += dv_all[:w]

            dk_acc[slot_c] += dk_all[w:w + tq]
            dv_acc[slot_c] += dv_all[w:w + tq]
            dk_acc[slot_n, :w, :] += dk_all[w + tq:]
            dv_acc[slot_n, :w, :] += dv_all[w + tq:]

        @pl.when(i >= 1)
        def _():
            dk_ref[...] = _rope(dk_acc[slot_p], *_rope_tabs(cs_p[...], e_ref[...]), -1.0).astype(BF16)
            dv_ref[...] = dv_acc[slot_p].astype(BF16)

    row_c = lambda width: pl.BlockSpec((None, tq, width), lambda s, i: (s, cur(s, i), 0))
    row_p = lambda width: pl.BlockSpec((None, tq, width), lambda s, i: (s, jnp.maximum(i - 1, 0), 0))
    in_specs = ([q_spec] + kv_specs + [row_c(qw), row_c(LANES), row_c(LANES), row_c(ROT_DIM), row_p(ROT_DIM),
                                       pl.BlockSpec((ROT_DIM, 3 * LANES), lambda s, i: (0, 0))])
    args = [qkv] * 7 + [do, lse, delta, cs, cs, e_mat]
    out_specs = [row_c(qw), row_p(kw), row_p(kw)]
    out_shape = [jax.ShapeDtypeStruct((nseq, seq_len, qw), BF16),
                 jax.ShapeDtypeStruct((nseq, seq_len, kw), BF16),
                 jax.ShapeDtypeStruct((nseq, seq_len, kw), BF16)]
    if sink is not None:
        in_specs = [pl.BlockSpec(memory_space=pltpu.SMEM)] + in_specs
        args = [sink] + args
        out_specs.append(pl.BlockSpec((8, LANES), lambda s, i: (0, 0)))
        out_shape.append(jax.ShapeDtypeStruct((8, LANES), F32))
    return _pcall(
        body, name=name, grid=(nseq, n + 1), in_specs=in_specs, out_specs=out_specs, out_shape=out_shape,
        scratch_shapes=[pltpu.VMEM((3, tq, kw), F32), pltpu.VMEM((3, tq, kw), F32)], args=args,
        dims=("arbitrary", "arbitrary"), comm=comm)


PAIR = 2 * HEAD_DIM


def _window_mask_t(i, tq, w, seq_len):
    tk = tq + 2 * w
    kpos = i * tq - w + lax.broadcasted_iota(jnp.int32, (tk, tq), 0)
    qpos = i * tq + lax.broadcasted_iota(jnp.int32, (tk, tq), 1)
    return (jnp.abs(qpos - kpos) <= w) & (kpos >= 0) & (kpos < seq_len)


def _place_head(x2, src_pos, dst_pos):
    hi = lax.broadcasted_iota(jnp.int32, x2.shape, 1) >= HEAD_DIM
    src = x2 if src_pos == dst_pos else pltpu.roll(x2, HEAD_DIM, 1)
    return jnp.where(hi == (dst_pos == 1), src, jnp.zeros_like(src))


def _swa_fwd_t(qkv, *, qcol, kcol, vcol, hq, hkv, w, tq, sink, name, comm=None):
    nseq, seq_len, _ = qkv.shape
    n = seq_len // tq
    rep = hq // hkv
    q_spec, kv_specs, _, _ = _swa_specs(tq, hq, hkv, n, qcol, kcol, vcol)

    def body(*refs):
        if sink is not None:
            sink_ref, refs = refs[0], refs[1:]
        q_ref, kp_ref, kc_ref, kn_ref, vp_ref, vc_ref, vn_ref, o_ref, lse_ref = refs
        i = pl.program_id(1)
        mask_t = _window_mask_t(i, tq, w, seq_len)
        o_t = [None] * (hq // 2)
        lse_rows = [None] * hq
        for a in range(hkv // 2):
            ls = slice(a * PAIR, (a + 1) * PAIR)
            kcat = jnp.concatenate([kp_ref[tq - w:, ls], kc_ref[:, ls], kn_ref[:w, ls]], axis=0) * 0.125
            vcat = jnp.concatenate([vp_ref[tq - w:, ls], vc_ref[:, ls], vn_ref[:w, ls]], axis=0)
            for e in range(2):
                g = 2 * a + e
                placed = {}
                for r in range(rep):
                    h = g * rep + r
                    qp, pos = h // 2, h % 2
                    if pos not in placed:
                        placed[pos] = (_place_head(kcat, e, pos), _place_head(vcat, e, pos))
                    k_g, v_g = placed[pos]
                    s_t = jnp.where(mask_t, _dot_nt(k_g, q_ref[:, qp * PAIR:(qp + 1) * PAIR]), NEG_INF)
                    m = jnp.max(s_t, axis=0, keepdims=True)
                    if sink is not None:
                        m = jnp.maximum(m, sink_ref[0, h])
                    p_t = jnp.exp(s_t - m)
                    den = jnp.sum(p_t, axis=0, keepdims=True)
                    if sink is not None:
                        den = den + jnp.exp(sink_ref[0, h] - m)
                    part = _dot_tn(v_g, p_t.astype(BF16)) / den
                    o_t[qp] = part if o_t[qp] is None else o_t[qp] + part
                    lse_rows[h] = m + jnp.log(den)
        o_ref[...] = jnp.concatenate(o_t, axis=0).T
        lse_ref[...] = jnp.concatenate(lse_rows, axis=0)

    in_specs = [q_spec] + kv_specs
    args = [qkv] * 7
    if sink is not None:
        in_specs = [pl.BlockSpec(memory_space=pltpu.SMEM)] + in_specs
        args = [sink] + args
    (o, lse), couts = _pcall(
        body, name=name, grid=(nseq, n), in_specs=in_specs,
        out_specs=[pl.BlockSpec((None, tq, hq * HEAD_DIM), lambda s, i: (s, i, 0)),
                   pl.BlockSpec((None, hq, tq), lambda s, i: (s, 0, i))],
        out_shape=[jax.ShapeDtypeStruct((nseq, seq_len, hq * HEAD_DIM), F32),
                   jax.ShapeDtypeStruct((nseq, hq, seq_len), F32)],
        args=args, dims=("parallel", "parallel"), comm=comm)
    return o, lse, couts


def _swa_bwd_t(qkv, do, lse, delta, cs, e_mat, *, qcol, kcol, vcol, hq, hkv, w, tq, sink, name, comm=None):
    nseq, seq_len, _ = qkv.shape
    n = seq_len // tq
    rep = hq // hkv
    qw, kw = hq * HEAD_DIM, hkv * HEAD_DIM
    tk = tq + 2 * w
    q_spec, kv_specs, cur, prv = _swa_specs(tq, hq, hkv, n, qcol, kcol, vcol)

    def body(*refs):
        if sink is not None:
            sink_ref, refs = refs[0], refs[1:]
        (q_ref, kp_ref, kc_ref, kn_ref, vp_ref, vc_ref, vn_ref, do_ref, lse_ref, dl_ref,
         cs_c, cs_p, e_ref) = refs[:13]
        outs = refs[13:]
        if sink is not None:
            dq_ref, dk_ref, dv_ref, dsink_ref, dk_acc, dv_acc = outs
        else:
            dq_ref, dk_ref, dv_ref, dk_acc, dv_acc = outs
        s_id = pl.program_id(0)
        i = pl.program_id(1)
        slot_p, slot_c, slot_n = (i + 2) % 3, i % 3, (i + 1) % 3

        if sink is not None:
            @pl.when((s_id == 0) & (i == 0))
            def _():
                dsink_ref[...] = jnp.zeros_like(dsink_ref)

        @pl.when(i < n)
        def _():
            mask_t = _window_mask_t(i, tq, w, seq_len)
            dk_acc[slot_n] = jnp.zeros((tq, kw), F32)
            dv_acc[slot_n] = jnp.zeros((tq, kw), F32)

            @pl.when(i == 0)
            def _():
                dk_acc[slot_c] = jnp.zeros((tq, kw), F32)
                dv_acc[slot_c] = jnp.zeros((tq, kw), F32)

            dq_t = [None] * (hq // 2)
            dk_pairs, dv_pairs = [], []
            for a in range(hkv // 2):
                ls = slice(a * PAIR, (a + 1) * PAIR)
                kcat = jnp.concatenate([kp_ref[tq - w:, ls], kc_ref[:, ls], kn_ref[:w, ls]], axis=0) * 0.125
                vcat = jnp.concatenate([vp_ref[tq - w:, ls], vc_ref[:, ls], vn_ref[:w, ls]], axis=0)
                dk2 = jnp.zeros((tk, PAIR), F32)
                dv2 = jnp.zeros((tk, PAIR), F32)
                for e in range(2):
                    g = 2 * a + e
                    placed = {}
                    for r in range(rep):
                        h = g * rep + r
                        qp, pos = h // 2, h % 2
                        if pos not in placed:
                            placed[pos] = (_place_head(kcat, e, pos), _place_head(vcat, e, pos))
                        k_g, v_g = placed[pos]
                        q2 = q_ref[:, qp * PAIR:(qp + 1) * PAIR]
                        do2 = do_ref[:, qp * PAIR:(qp + 1) * PAIR]
                        lse_h = lse_ref[h:h + 1, :]
                        dl_h = dl_ref[h:h + 1, :]
                        p_t = jnp.exp(jnp.where(mask_t, _dot_nt(k_g, q2), NEG_INF) - lse_h)
                        dp_t = _dot_nt(v_g, do2)
                        dsb = (p_t * (dp_t - dl_h)).astype(BF16)
                        part = _dot_tn(k_g, dsb)
                        dq_t[qp] = part if dq_t[qp] is None else dq_t[qp] + part
                        dk2 = dk2 + _dot(dsb, _place_head(q2, pos, e) * 0.125)
                        dv2 = dv2 + _dot(p_t.astype(BF16), _place_head(do2, pos, e))
                        if sink is not None:
                            ds_sink = -jnp.sum(jnp.exp(sink_ref[0, h] - lse_h) * dl_h)
                            dsink_ref[h:h + 1, :] += jnp.full((1, LANES), ds_sink, F32)
                dk_pairs.append(dk2)
                dv_pairs.append(dv2)
            dq = jnp.concatenate(dq_t, axis=0).T
            dq_ref[...] = _rope(dq, *_rope_tabs(cs_c[...], e_ref[...]), -1.0).astype(BF16)
            dk_all = dk_pairs[0] if len(dk_pairs) == 1 else jnp.concatenate(dk_pairs, axis=1)
            dv_all = dv_pairs[0] if len(dv_pairs) == 1 else jnp.concatenate(dv_pairs, axis=1)

            @pl.when(i > 0)
            def _():
                dk_acc[slot_p, tq - w:, :] += dk_all[:w]
                dv_acc[slot_p, tq - w:, :] += dv_all[:w]

            dk_acc[slot_c] += dk_all[w:w + tq]
            dv_acc[slot_c] += dv_all[w:w + tq]
            dk_acc[slot_n, :w, :] += dk_all[w + tq:]
            dv_acc[slot_n, :w, :] += dv_all[w + tq:]

        @pl.when(i >= 1)
        def _():
            dk_ref[...] = _rope(dk_acc[slot_p], *_rope_tabs(cs_p[...], e_ref[...]), -1.0).astype(BF16)
            dv_ref[...] = dv_acc[slot_p].astype(BF16)

    row_c = lambda width: pl.BlockSpec((None, tq, width), lambda s, i: (s, cur(s, i), 0))
    row_p = lambda width: pl.BlockSpec((None, tq, width), lambda s, i: (s, jnp.maximum(i - 1, 0), 0))
    stat = pl.BlockSpec((None, hq, tq), lambda s, i: (s, 0, cur(s, i)))
    in_specs = ([q_spec] + kv_specs + [row_c(qw), stat, stat, row_c(ROT_DIM), row_p(ROT_DIM),
                                       pl.BlockSpec((ROT_DIM, 3 * LANES), lambda s, i: (0, 0))])
    args = [qkv] * 7 + [do, lse, delta, cs, cs, e_mat]
    out_specs = [row_c(qw), row_p(kw), row_p(kw)]
    out_shape = [jax.ShapeDtypeStruct((nseq, seq_len, qw), BF16),
                 jax.ShapeDtypeStruct((nseq, seq_len, kw), BF16),
                 jax.ShapeDtypeStruct((nseq, seq_len, kw), BF16)]
    if sink is not None:
        in_specs = [pl.BlockSpec(memory_space=pltpu.SMEM)] + in_specs
        args = [sink] + args
        out_specs.append(pl.BlockSpec((8, LANES), lambda s, i: (0, 0)))
        out_shape.append(jax.ShapeDtypeStruct((8, LANES), F32))
    return _pcall(
        body, name=name, grid=(nseq, n + 1), in_specs=in_specs, out_specs=out_specs, out_shape=out_shape,
        scratch_shapes=[pltpu.VMEM((3, tq, kw), F32), pltpu.VMEM((3, tq, kw), F32)], args=args,
        dims=("arbitrary", "arbitrary"), comm=comm)


def _band_mask_t(row0, tq, w, seq_len):
    tk = tq + 2 * w
    kk = lax.broadcasted_iota(jnp.int32, (tk, tq), 0)
    qq = lax.broadcasted_iota(jnp.int32, (tk, tq), 1)
    kpos = row0 - w + kk
    return (jnp.abs(qq + w - kk) <= w) & (kpos >= 0) & (kpos < seq_len)


def _halo_kv_specs(t, w, hkv, n, seq_len, kcol, vcol):
    kw = hkv * HEAD_DIM
    per, last = t // w, seq_len // w - 1
    cur = lambda s, i: jnp.minimum(i, n - 1)
    specs = []
    for c in (kcol, vcol):
        specs += [pl.BlockSpec((None, w, kw), lambda s, i, c=c: (s, jnp.maximum(cur(s, i) * per - 1, 0), c)),
                  pl.BlockSpec((None, t, kw), lambda s, i, c=c: (s, cur(s, i), c)),
                  pl.BlockSpec((None, w, kw), lambda s, i, c=c: (s, jnp.minimum((cur(s, i) + 1) * per, last), c))]
    return specs, cur


def _swa_fwd_s(qkv, *, qcol, kcol, vcol, hq, hkv, w, tq, sub, sink, name, comm=None):
    nseq, seq_len, _ = qkv.shape
    t = tq * sub
    n = seq_len // t
    rep = hq // hkv
    tk = tq + 2 * w
    kv_specs, cur = _halo_kv_specs(t, w, hkv, n, seq_len, kcol, vcol)

    def body(*refs):
        if sink is not None:
            sink_ref, refs = refs[0], refs[1:]
        q_ref, kp_ref, kc_ref, kn_ref, vp_ref, vc_ref, vn_ref, o_ref, lse_ref = refs
        i = pl.program_id(1)
        kfull, vfull = [], []
        for a in range(hkv // 2):
            ls = slice(a * PAIR, (a + 1) * PAIR)
            kfull.append(jnp.concatenate([kp_ref[:, ls], kc_ref[:, ls], kn_ref[:, ls]], axis=0) * 0.125)
            vfull.append(jnp.concatenate([vp_ref[:, ls], vc_ref[:, ls], vn_ref[:, ls]], axis=0))
        for jj in range(sub):
            rows = slice(jj * tq, (jj + 1) * tq)
            mask_t = _band_mask_t(i * t + jj * tq, tq, w, seq_len)
            o_t = [None] * (hq // 2)
            lse_rows = [None] * hq
            for a in range(hkv // 2):
                kcat = kfull[a][jj * tq:jj * tq + tk]
                vcat = vfull[a][jj * tq:jj * tq + tk]
                for e in range(2):
                    g = 2 * a + e
                    placed = {}
                    for r in range(rep):
                        h = g * rep + r
                        qp, pos = h // 2, h % 2
                        if pos not in placed:
                            placed[pos] = (_place_head(kcat, e, pos), _place_head(vcat, e, pos))
                        k_g, v_g = placed[pos]
                        s_t = jnp.where(mask_t, _dot_nt(k_g, q_ref[rows, qp * PAIR:(qp + 1) * PAIR]), NEG_INF)
                        m = jnp.max(s_t, axis=0, keepdims=True)
                        if sink is not None:
                            m = jnp.maximum(m, sink_ref[0, h])
                        p_t = jnp.exp(s_t - m)
                        den = jnp.sum(p_t, axis=0, keepdims=True)
                        if sink is not None:
                            den = den + jnp.exp(sink_ref[0, h] - m)
                        part = _dot_tn(v_g, p_t.astype(BF16)) / den
                        o_t[qp] = part if o_t[qp] is None else o_t[qp] + part
                        lse_rows[h] = m + jnp.log(den)
            o_ref[rows, :] = jnp.concatenate(o_t, axis=0).T
            lse_ref[:, rows] = jnp.concatenate(lse_rows, axis=0)

    in_specs = [pl.BlockSpec((None, t, hq * HEAD_DIM), lambda s, i: (s, i, qcol))] + kv_specs
    args = [qkv] * 7
    if sink is not None:
        in_specs = [pl.BlockSpec(memory_space=pltpu.SMEM)] + in_specs
        args = [sink] + args
    (o, lse), couts = _pcall(
        body, name=name, grid=(nseq, n), in_specs=in_specs,
        out_specs=[pl.BlockSpec((None, t, hq * HEAD_DIM), lambda s, i: (s, i, 0)),
                   pl.BlockSpec((None, hq, t), lambda s, i: (s, 0, i))],
        out_shape=[jax.ShapeDtypeStruct((nseq, seq_len, hq * HEAD_DIM), F32),
                   jax.ShapeDtypeStruct((nseq, hq, seq_len), F32)],
        args=args, dims=("parallel", "parallel"), comm=comm)
    return o, lse, couts


def _swa_bwd_s(qkv, do, lse, delta, cs, e_mat, *, qcol, kcol, vcol, hq, hkv, w, tq, sub, sink, name, comm=None):
    nseq, seq_len, _ = qkv.shape
    t = tq * sub
    n = seq_len // t
    rep = hq // hkv
    qw, kw = hq * HEAD_DIM, hkv * HEAD_DIM
    tk = tq + 2 * w
    kv_specs, cur = _halo_kv_specs(t, w, hkv, n, seq_len, kcol, vcol)

    def body(*refs):
        if sink is not None:
            sink_ref, refs = refs[0], refs[1:]
        (q_ref, kp_ref, kc_ref, kn_ref, vp_ref, vc_ref, vn_ref, do_ref, lse_ref, dl_ref,
         cs_c, cs_p, e_ref) = refs[:13]
        outs = refs[13:]
        if sink is not None:
            dq_ref, dk_ref, dv_ref, dsink_ref, dk_acc, dv_acc, dk_win, dv_win = outs
        else:
            dq_ref, dk_ref, dv_ref, dk_acc, dv_acc, dk_win, dv_win = outs
        s_id = pl.program_id(0)
        i = pl.program_id(1)
        slot_p, slot_c, slot_n = (i + 2) % 3, i % 3, (i + 1) % 3

        if sink is not None:
            @pl.when((s_id == 0) & (i == 0))
            def _():
                dsink_ref[...] = jnp.zeros_like(dsink_ref)

        @pl.when(i < n)
        def _():
            dk_win[...] = jnp.zeros_like(dk_win)
            dv_win[...] = jnp.zeros_like(dv_win)
            kfull, vfull = [], []
            for a in range(hkv // 2):
                ls = slice(a * PAIR, (a + 1) * PAIR)
                kfull.append(jnp.concatenate([kp_ref[:, ls], kc_ref[:, ls], kn_ref[:, ls]], axis=0) * 0.125)
                vfull.append(jnp.concatenate([vp_ref[:, ls], vc_ref[:, ls], vn_ref[:, ls]], axis=0))
            for jj in range(sub):
                rows = slice(jj * tq, (jj + 1) * tq)
                krows = slice(jj * tq, jj * tq + tk)
                mask_t = _band_mask_t(i * t + jj * tq, tq, w, seq_len)
                dq_t = [None] * (hq // 2)
                for a in range(hkv // 2):
                    ls = slice(a * PAIR, (a + 1) * PAIR)
                    kcat, vcat = kfull[a][krows], vfull[a][krows]
                    dk2 = jnp.zeros((tk, PAIR), F32)
                    dv2 = jnp.zeros((tk, PAIR), F32)
                    for e in range(2):
                        g = 2 * a + e
                        placed = {}
                        for r in range(rep):
                            h = g * rep + r
                            qp, pos = h // 2, h % 2
                            if pos not in placed:
                                placed[pos] = (_place_head(kcat, e, pos), _place_head(vcat, e, pos))
                            k_g, v_g = placed[pos]
                            q2 = q_ref[rows, qp * PAIR:(qp + 1) * PAIR]
                            do2 = do_ref[rows, qp * PAIR:(qp + 1) * PAIR]
                            lse_h = lse_ref[h:h + 1, rows]
                            dl_h = dl_ref[h:h + 1, rows]
                            p_t = jnp.exp(jnp.where(mask_t, _dot_nt(k_g, q2), NEG_INF) - lse_h)
                            dp_t = _dot_nt(v_g, do2)
                            dsb = (p_t * (dp_t - dl_h)).astype(BF16)
                            part = _dot_tn(k_g, dsb)
                            dq_t[qp] = part if dq_t[qp] is None else dq_t[qp] + part
                            dk2 = dk2 + _dot(dsb, _place_head(q2, pos, e) * 0.125)
                            dv2 = dv2 + _dot(p_t.astype(BF16), _place_head(do2, pos, e))
                            if sink is not None:
                                ds_sink = -jnp.sum(jnp.exp(sink_ref[0, h] - lse_h) * dl_h)
                                dsink_ref[h:h + 1, :] += jnp.full((1, LANES), ds_sink, F32)
                    dk_win[krows, ls] += dk2
                    dv_win[krows, ls] += dv2
                dq = jnp.concatenate(dq_t, axis=0).T
                dq_ref[rows, :] = _rope(dq, *_rope_tabs(cs_c[rows, :], e_ref[...]), -1.0).astype(BF16)

            @pl.when(i > 0)
            def _():
                dk_acc[slot_p, t - w:, :] += dk_win[:w, :]
                dv_acc[slot_p, t - w:, :] += dv_win[:w, :]

            @pl.when(i == 0)
            def _():
                dk_acc[slot_c] = dk_win[w:w + t, :]
                dv_acc[slot_c] = dv_win[w:w + t, :]

            @pl.when(i > 0)
            def _():
                dk_acc[slot_c] += dk_win[w:w + t, :]
                dv_acc[slot_c] += dv_win[w:w + t, :]

            dk_acc[slot_n] = jnp.zeros((t, kw), F32)
            dv_acc[slot_n] = jnp.zeros((t, kw), F32)
            dk_acc[slot_n, :w, :] = dk_win[w + t:, :]
            dv_acc[slot_n, :w, :] = dv_win[w + t:, :]

        @pl.when(i >= 1)
        def _():
            dk_ref[...] = _rope(dk_acc[slot_p], *_rope_tabs(cs_p[...], e_ref[...]), -1.0).astype(BF16)
            dv_ref[...] = dv_acc[slot_p].astype(BF16)

    row_c = lambda width: pl.BlockSpec((None, t, width), lambda s, i: (s, cur(s, i), 0))
    row_p = lambda width: pl.BlockSpec((None, t, width), lambda s, i: (s, jnp.maximum(i - 1, 0), 0))
    stat = pl.BlockSpec((None, hq, t), lambda s, i: (s, 0, cur(s, i)))
    in_specs = ([pl.BlockSpec((None, t, qw), lambda s, i: (s, cur(s, i), qcol))] + kv_specs
                + [row_c(qw), stat, stat, row_c(ROT_DIM), row_p(ROT_DIM),
                   pl.BlockSpec((ROT_DIM, 3 * LANES), lambda s, i: (0, 0))])
    args = [qkv] * 7 + [do, lse, delta, cs, cs, e_mat]
    out_specs = [row_c(qw), row_p(kw), row_p(kw)]
    out_shape = [jax.ShapeDtypeStruct((nseq, seq_len, qw), BF16),
                 jax.ShapeDtypeStruct((nseq, seq_len, kw), BF16),
                 jax.ShapeDtypeStruct((nseq, seq_len, kw), BF16)]
    if sink is not None:
        in_specs = [pl.BlockSpec(memory_space=pltpu.SMEM)] + in_specs
        args = [sink] + args
        out_specs.append(pl.BlockSpec((8, LANES), lambda s, i: (0, 0)))
        out_shape.append(jax.ShapeDtypeStruct((8, LANES), F32))
    return _pcall(
        body, name=name, grid=(nseq, n + 1), in_specs=in_specs, out_specs=out_specs, out_shape=out_shape,
        scratch_shapes=[pltpu.VMEM((3, t, kw), F32), pltpu.VMEM((3, t, kw), F32),
                        pltpu.VMEM((t + 2 * w, kw), F32), pltpu.VMEM((t + 2 * w, kw), F32)], args=args,
        dims=("arbitrary", "arbitrary"), comm=comm)


def _rms_parts(o, g):
    ms = jnp.mean(o * o, axis=-1, keepdims=True) + LN_EPS
    rinv = lax.rsqrt(ms)
    return o * rinv * g, rinv


def _combine_fwd(out_a, o_g, lse_g, g_win, g_dil, *, t):
    s = out_a.shape[0]
    wd = DIL_SLOTS * HEAD_DIM

    def body(oa_ref, o0, o1, o2, l0, l1, l2, gw_ref, gd_ref, mixed_ref, ob_ref, lt_ref):
        ls = [l0[...], l1[...], l2[...]]
        mx = jnp.maximum(jnp.maximum(ls[0], ls[1]), ls[2])
        ws = [jnp.exp(l - mx) for l in ls]
        tot = ws[0] + ws[1] + ws[2]
        lt_ref[...] = mx + jnp.log(tot)
        ws = [x / tot for x in ws]
        parts = []
        for h in range(DIL_SLOTS):
            hs = slice(h * HEAD_DIM, (h + 1) * HEAD_DIM)
            parts.append(ws[0][:, h:h + 1] * o0[:, hs] + ws[1][:, h:h + 1] * o1[:, hs] + ws[2][:, h:h + 1] * o2[:, hs])
        ob = jnp.concatenate(parts, axis=1)
        ob_ref[...] = ob
        na, _ = _rms_parts(oa_ref[...], gw_ref[...])
        nb, _ = _rms_parts(ob, gd_ref[...])
        mixed_ref[:, :wd] = na.astype(BF16)
        mixed_ref[:, wd:] = nb.astype(BF16)

    half = pl.BlockSpec((t, wd), lambda i: (i, 0))
    lanes = pl.BlockSpec((t, LANES), lambda i: (i, 0))
    grow = pl.BlockSpec((1, wd), lambda i: (0, 0))
    return pl.pallas_call(
        body, name="combine_fwd", grid=(s // t,),
        in_specs=[half, half, half, half, lanes, lanes, lanes, grow, grow],
        out_specs=[pl.BlockSpec((t, 2 * wd), lambda i: (i, 0)), half, lanes],
        out_shape=[jax.ShapeDtypeStruct((s, 2 * wd), BF16), jax.ShapeDtypeStruct((s, wd), F32),
                   jax.ShapeDtypeStruct((s, LANES), F32)],
        compiler_params=_cparams(dimension_semantics=("parallel",)),
    )(out_a, *o_g, *lse_g, g_win, g_dil)


def _combine_bwd(dmixed, out_a, out_b, g_win, g_dil, *, t):
    s = out_a.shape[0]
    wd = DIL_SLOTS * HEAD_DIM

    def body(dm_ref, oa_ref, ob_ref, gw_ref, gd_ref, doa_ref, dob_ref, dla_ref, dlb_ref, st_ref):
        i = pl.program_id(0)

        @pl.when(i == 0)
        def _():
            st_ref[...] = jnp.zeros_like(st_ref)

        lane = lax.broadcasted_iota(jnp.int32, (t, LANES), 1)
        for idx, (o_ref, g_ref, do_ref, dl_ref) in enumerate(
                ((oa_ref, gw_ref, doa_ref, dla_ref), (ob_ref, gd_ref, dob_ref, dlb_ref))):
            o = o_ref[...]
            dn = dm_ref[:, idx * wd:(idx + 1) * wd]
            _, rinv = _rms_parts(o, g_ref[...])
            wv = dn * g_ref[...]
            do = rinv * wv - o * (rinv * rinv * rinv) * jnp.mean(wv * o, axis=-1, keepdims=True)
            st_ref[idx:idx + 1, :] += jnp.sum(dn * o * rinv, axis=0, keepdims=True)
            do_ref[...] = do.astype(BF16)
            prod = do * o
            acc = jnp.zeros((t, LANES), F32)
            for h in range(DIL_SLOTS):
                hs = slice(h * HEAD_DIM, (h + 1) * HEAD_DIM)
                acc = jnp.where(lane == h, jnp.sum(prod[:, hs], axis=1, keepdims=True), acc)
            dl_ref[...] = acc

    half = pl.BlockSpec((t, wd), lambda i: (i, 0))
    lanes = pl.BlockSpec((t, LANES), lambda i: (i, 0))
    grow = pl.BlockSpec((1, wd), lambda i: (0, 0))
    return pl.pallas_call(
        body, name="combine_bwd", grid=(s // t,),
        in_specs=[pl.BlockSpec((t, 2 * wd), lambda i: (i, 0)), half, half, grow, grow],
        out_specs=[half, half, lanes, lanes, pl.BlockSpec((8, wd), lambda i: (0, 0))],
        out_shape=[jax.ShapeDtypeStruct((s, wd), BF16), jax.ShapeDtypeStruct((s, wd), BF16),
                   jax.ShapeDtypeStruct((s, LANES), F32), jax.ShapeDtypeStruct((s, LANES), F32),
                   jax.ShapeDtypeStruct((8, wd), F32)],
        compiler_params=_cparams(dimension_semantics=("arbitrary",)),
    )(dmixed, out_a, out_b, g_win, g_dil)


def _mixproj_fwd(mixed_b, w_mix_b, x, ln_in_g, ln_in_b, ln1_g, ln1_b, *, t):
    s = x.shape[0]

    def body(m_ref, w_ref, x_ref, g0, b0, g1, b1, r1_ref, h1_ref):
        h0 = _ln(x_ref[...], g0[...], b0[...])
        r1 = ALPHA * h0 + _dot(m_ref[...], w_ref[...])
        r1_ref[...] = r1
        h1_ref[...] = _ln(r1, g1[...], b1[...]).astype(BF16)

    tile = pl.BlockSpec((t, D_MODEL), lambda i: (i, 0))
    row = pl.BlockSpec((1, D_MODEL), lambda i: (0, 0))
    return pl.pallas_call(
        body, name="mixproj_fwd", grid=(s // t,),
        in_specs=[tile, pl.BlockSpec((D_MODEL, D_MODEL), lambda i: (0, 0)), tile, row, row, row, row],
        out_specs=[tile, tile],
        out_shape=[jax.ShapeDtypeStruct((s, D_MODEL), F32), jax.ShapeDtypeStruct((s, D_MODEL), BF16)],
        compiler_params=_cparams(dimension_semantics=("parallel",)),
    )(mixed_b, w_mix_b, x, ln_in_g, ln_in_b, ln1_g, ln1_b)


def _mem_fwd(mem, g, b, wk_b, wv_b):
    ml = mem.shape[0]

    def body(mem_ref, g_ref, b_ref, wk_ref, wv_ref, mn_ref, kx_ref, vx_ref):
        mn = _ln(mem_ref[...], g_ref[...], b_ref[...]).astype(BF16)
        mn_ref[...] = mn
        kx_ref[...] = _dot(mn, wk_ref[...]).astype(BF16)
        vx_ref[...] = _dot(mn, wv_ref[...]).astype(BF16)

    sh = jax.ShapeDtypeStruct((ml, D_MODEL), BF16)
    return pl.pallas_call(body, name="mem_fwd", out_shape=[sh, sh, sh], compiler_params=_cparams())(
        mem, g, b, wk_b, wv_b)


def _mem_bwd(dkx, dvx, mem, g, b, wk_b, wv_b):
    def body(dk_ref, dv_ref, mem_ref, g_ref, b_ref, wk_ref, wv_ref, dwk_ref, dwv_ref, st_ref):
        mem_v = mem_ref[...]
        mn = _ln(mem_v, g_ref[...], b_ref[...]).astype(BF16)
        dkb = dk_ref[...].astype(BF16)
        dvb = dv_ref[...].astype(BF16)
        dwk_ref[...] = _dot_tn(mn, dkb)
        dwv_ref[...] = _dot_tn(mn, dvb)
        dmn = _dot_nt(dkb, wk_ref[...]) + _dot_nt(dvb, wv_ref[...])
        _, dg, db = _ln_bwd_math(dmn, mem_v, g_ref[...])
        st_ref[...] = jnp.zeros_like(st_ref)
        st_ref[0:1, :] = dg
        st_ref[1:2, :] = db

    sw = jax.ShapeDtypeStruct((D_MODEL, D_MODEL), F32)
    return pl.pallas_call(body, name="mem_bwd", out_shape=[sw, sw, jax.ShapeDtypeStruct((8, D_MODEL), F32)],
                          compiler_params=_cparams())(dkx, dvx, mem, g, b, wk_b, wv_b)


def _xattn_fwd(h1b, r1, kx, vx, wq_b, wo_b, ln1_g, ln1_b, ln2_g, ln2_b, *, t):
    s = h1b.shape[0]
    scale = X_HEAD_DIM ** -0.5

    def body(h_ref, r1_ref, kx_ref, vx_ref, wq_ref, wo_ref, g1, b1, g2, b2, r2_ref, h2_ref, qx_ref, ox_ref, lse_ref):
        qxb = _dot(h_ref[...], wq_ref[...]).astype(BF16)
        qx_ref[...] = qxb
        lane = lax.broadcasted_iota(jnp.int32, (t, LANES), 1)
        lse_acc = jnp.zeros((t, LANES), F32)
        parts = []
        for h in range(X_HEADS):
            hs = slice(h * X_HEAD_DIM, (h + 1) * X_HEAD_DIM)
            sc = _dot_nt(qxb[:, hs] * scale, kx_ref[:, hs])
            m = jnp.max(sc, axis=1, keepdims=True)
            p = jnp.exp(sc - m)
            den = jnp.sum(p, axis=1, keepdims=True)
            parts.append(_dot(p.astype(BF16), vx_ref[:, hs]) / den)
            lse_acc = jnp.where(lane == h, m + jnp.log(den), lse_acc)
        lse_ref[...] = lse_acc
        oxb = jnp.concatenate(parts, axis=1).astype(BF16)
        ox_ref[...] = oxb
        h1 = _ln(r1_ref[...], g1[...], b1[...])
        r2 = ALPHA * h1 + _dot(oxb, wo_ref[...])
        r2_ref[...] = r2
        h2_ref[...] = _ln(r2, g2[...], b2[...]).astype(BF16)

    tile = pl.BlockSpec((t, D_MODEL), lambda i: (i, 0))
    row = pl.BlockSpec((1, D_MODEL), lambda i: (0, 0))
    full = lambda r: pl.BlockSpec((r, D_MODEL), lambda i: (0, 0))
    ml = kx.shape[0]
    bsh = jax.ShapeDtypeStruct((s, D_MODEL), BF16)
    return pl.pallas_call(
        body, name="xattn_fwd", grid=(s // t,),
        in_specs=[tile, tile, full(ml), full(ml), full(D_MODEL), full(D_MODEL), row, row, row, row],
        out_specs=[tile, tile, tile, tile, pl.BlockSpec((t, LANES), lambda i: (i, 0))],
        out_shape=[jax.ShapeDtypeStruct((s, D_MODEL), F32), bsh, bsh, bsh, jax.ShapeDtypeStruct((s, LANES), F32)],
        compiler_params=_cparams(dimension_semantics=("parallel",)),
    )(h1b, r1, kx, vx, wq_b, wo_b, ln1_g, ln1_b, ln2_g, ln2_b)


def _xattn_bwd(dr2, qxb, oxb, lse, kx, vx, wq_b, wo_b, *, t, comm=None):
    s = dr2.shape[0]
    ml = kx.shape[0]
    scale = X_HEAD_DIM ** -0.5

    def body(dr2_ref, qx_ref, ox_ref, lse_ref, kx_ref, vx_ref, wq_ref, wo_ref, dh1_ref, dqx_ref, dkx_ref, dvx_ref):
        i = pl.program_id(0)

        @pl.when(i == 0)
        def _():
            dkx_ref[...] = jnp.zeros_like(dkx_ref)
            dvx_ref[...] = jnp.zeros_like(dvx_ref)

        dr2v = dr2_ref[...]
        dox = _dot_nt(dr2v.astype(BF16), wo_ref[...])
        parts = []
        for h in range(X_HEADS):
            hs = slice(h * X_HEAD_DIM, (h + 1) * X_HEAD_DIM)
            doh = dox[:, hs]
            dohb = doh.astype(BF16)
            dl = jnp.sum(doh * ox_ref[:, hs].astype(F32), axis=1, keepdims=True)
            qh = qx_ref[:, hs] * scale
            p = jnp.exp(_dot_nt(qh, kx_ref[:, hs]) - lse_ref[:, h:h + 1])
            dp = _dot_nt(dohb, vx_ref[:, hs])
            dsb = (p * (dp - dl)).astype(BF16)
            parts.append(_dot(dsb, kx_ref[:, hs]) * scale)
            dkx_ref[:, hs] += _dot_tn(dsb, qh)
            dvx_ref[:, hs] += _dot_tn(p.astype(BF16), dohb)
        dqxb = jnp.concatenate(parts, axis=1).astype(BF16)
        dqx_ref[...] = dqxb
        dh1_ref[...] = _dot_nt(dqxb, wq_ref[...]) + ALPHA * dr2v

    tile = pl.BlockSpec((t, D_MODEL), lambda i: (i, 0))
    full = lambda r: pl.BlockSpec((r, D_MODEL), lambda i: (0, 0))
    return _pcall(
        body, name="xattn_bwd", grid=(s // t,),
        in_specs=[tile, tile, tile, pl.BlockSpec((t, LANES), lambda i: (i, 0)), full(ml), full(ml),
                  full(D_MODEL), full(D_MODEL)],
        out_specs=[tile, tile, full(ml), full(ml)],
        out_shape=[jax.ShapeDtypeStruct((s, D_MODEL), F32), jax.ShapeDtypeStruct((s, D_MODEL), BF16),
                   jax.ShapeDtypeStruct((ml, D_MODEL), F32), jax.ShapeDtypeStruct((ml, D_MODEL), F32)],
        args=[dr2, qxb, oxb, lse, kx, vx, wq_b, wo_b], dims=("arbitrary",), comm=comm)


def _halo_specs(t, s, width):
    tb8 = t // 8
    return [pl.BlockSpec((t, width), lambda i: (i, 0)),
            pl.BlockSpec((8, width), lambda i: (jnp.maximum(i * tb8 - 1, 0), 0)),
            pl.BlockSpec((8, width), lambda i: (jnp.minimum((i + 1) * tb8, s // 8 - 1), 0))]


def _halo_rows(i, n, prev_ref, next_ref):
    prev_row = jnp.where(i > 0, prev_ref[7:8, :], 0.0)
    next_row = jnp.where(i < n - 1, next_ref[0:1, :], 0.0)
    return prev_row, next_row


def _gelu_parts(gc):
    cdf = 0.5 * (1.0 + lax.erf(gc * (2.0 ** -0.5)))
    pdf = jnp.exp(-0.5 * gc * gc) * (1.0 / math.sqrt(2.0 * math.pi))
    return gc * cdf, cdf + gc * pdf


def _conv_fwd(g, u, conv_w, conv_b, *, t):
    s = g.shape[0]
    n = s // t

    def body(g_ref, gp_ref, gn_ref, u_ref, cw_ref, cb_ref, o_ref):
        i = pl.program_id(0)
        gv = g_ref[...]
        prev_row, next_row = _halo_rows(i, n, gp_ref, gn_ref)
        gm1, gp1 = _shift_rows(gv, prev_row, next_row)
        gc = gm1 * cw_ref[0:1, :] + gv * cw_ref[1:2, :] + gp1 * cw_ref[2:3, :] + cb_ref[...]
        act, _ = _gelu_parts(gc)
        o_ref[...] = (act * u_ref[...]).astype(BF16)

    tile = pl.BlockSpec((t, D_FF), lambda i: (i, 0))
    return pl.pallas_call(
        body, name="conv_fwd", grid=(n,),
        in_specs=_halo_specs(t, s, D_FF) + [tile, pl.BlockSpec((3, D_FF), lambda i: (0, 0)),
                                            pl.BlockSpec((1, D_FF), lambda i: (0, 0))],
        out_specs=tile, out_shape=jax.ShapeDtypeStruct((s, D_FF), BF16),
        compiler_params=_cparams(dimension_semantics=("parallel",)),
    )(g, g, g, u, conv_w, conv_b)


def _down_ln3(tb, w_down_b, r2, target, ln2_g, ln2_b, ln3_g, ln3_b, *, t):
    s = r2.shape[0]

    def body(t_ref, w_ref, r2_ref, tg_ref, g2, b2, g3, b3, dr_ref, drb_ref, st_ref):
        i = pl.program_id(0)

        @pl.when(i == 0)
        def _():
            st_ref[...] = jnp.zeros_like(st_ref)

        h2 = _ln(r2_ref[...], g2[...], b2[...])
        r3 = ALPHA * h2 + _dot(t_ref[...], w_ref[...])
        y = _ln(r3, g3[...], b3[...])
        err = y - tg_ref[...]
        loss = 0.5 * jnp.sum(jnp.mean(err * err, axis=-1, keepdims=True))
        dy = err * (1.0 / D_MODEL)
        dr, dg, db = _ln_bwd_math(dy, r3, g3[...])
        dr_ref[...] = dr
        drb_ref[...] = dr.astype(BF16)
        st_ref[0:1, :] += dg
        st_ref[1:2, :] += db
        st_ref[2:3, :] += jnp.full((1, D_MODEL), loss, F32)

    tile = pl.BlockSpec((t, D_MODEL), lambda i: (i, 0))
    row = pl.BlockSpec((1, D_MODEL), lambda i: (0, 0))
    return pl.pallas_call(
        body, name="down_ln3", grid=(s // t,),
        in_specs=[pl.BlockSpec((t, D_FF), lambda i: (i, 0)), pl.BlockSpec((D_FF, D_MODEL), lambda i: (0, 0)),
                  tile, tile, row, row, row, row],
        out_specs=[tile, tile, pl.BlockSpec((8, D_MODEL), lambda i: (0, 0))],
        out_shape=[jax.ShapeDtypeStruct((s, D_MODEL), F32), jax.ShapeDtypeStruct((s, D_MODEL), BF16),
                   jax.ShapeDtypeStruct((8, D_MODEL), F32)],
        compiler_params=_cparams(dimension_semantics=("arbitrary",)),
    )(tb, w_down_b, r2, target, ln2_g, ln2_b, ln3_g, ln3_b)


def _conv_bwd_a(dr3b, w_down_b, g, u, conv_w, conv_b, *, t):
    s = g.shape[0]
    n = s // t

    def body(d_ref, w_ref, g_ref, gp_ref, gn_ref, u_ref, cw_ref, cb_ref, du_ref, dgc_ref, st_ref):
        i = pl.program_id(0)

        @pl.when(i == 0)
        def _():
            st_ref[...] = jnp.zeros_like(st_ref)

        dt = _dot_nt(d_ref[...], w_ref[...])
        gv = g_ref[...]
        prev_row, next_row = _halo_rows(i, n, gp_ref, gn_ref)
        gm1, gp1 = _shift_rows(gv, prev_row, next_row)
        gc = gm1 * cw_ref[0:1, :] + gv * cw_ref[1:2, :] + gp1 * cw_ref[2:3, :] + cb_ref[...]
        act, dact = _gelu_parts(gc)
        du_ref[...] = (dt * act).astype(BF16)
        dgc = dt * u_ref[...] * dact
        dgc_ref[...] = dgc
        st_ref[0:1, :] += jnp.sum(gm1 * dgc, axis=0, keepdims=True)
        st_ref[1:2, :] += jnp.sum(gv * dgc, axis=0, keepdims=True)
        st_ref[2:3, :] += jnp.sum(gp1 * dgc, axis=0, keepdims=True)
        st_ref[3:4, :] += jnp.sum(dgc, axis=0, keepdims=True)

    tile = pl.BlockSpec((t, D_FF), lambda i: (i, 0))
    return pl.pallas_call(
        body, name="conv_bwd_a", grid=(n,),
        in_specs=[pl.BlockSpec((t, D_MODEL), lambda i: (i, 0)), pl.BlockSpec((D_FF, D_MODEL), lambda i: (0, 0))]
        + _halo_specs(t, s, D_FF) + [tile, pl.BlockSpec((3, D_FF), lambda i: (0, 0)),
                                     pl.BlockSpec((1, D_FF), lambda i: (0, 0))],
        out_specs=[tile, tile, pl.BlockSpec((8, D_FF), lambda i: (0, 0))],
        out_shape=[jax.ShapeDtypeStruct((s, D_FF), BF16), jax.ShapeDtypeStruct((s, D_FF), F32),
                   jax.ShapeDtypeStruct((8, D_FF), F32)],
        compiler_params=_cparams(dimension_semantics=("arbitrary",)),
    )(dr3b, w_down_b, g, g, g, u, conv_w, conv_b)


def _conv_bwd_b(dgc, conv_w, *, t):
    s = dgc.shape[0]
    n = s // t

    def body(d_ref, dp_ref, dn_ref, cw_ref, o_ref):
        i = pl.program_id(0)
        dv = d_ref[...]
        prev_row, next_row = _halo_rows(i, n, dp_ref, dn_ref)
        dm1, dp1 = _shift_rows(dv, prev_row, next_row)
        o_ref[...] = (dp1 * cw_ref[0:1, :] + dv * cw_ref[1:2, :] + dm1 * cw_ref[2:3, :]).astype(BF16)

    return pl.pallas_call(
        body, name="conv_bwd_b", grid=(n,),
        in_specs=_halo_specs(t, s, D_FF) + [pl.BlockSpec((3, D_FF), lambda i: (0, 0))],
        out_specs=pl.BlockSpec((t, D_FF), lambda i: (i, 0)), out_shape=jax.ShapeDtypeStruct((s, D_FF), BF16),
        compiler_params=_cparams(dimension_semantics=("parallel",)),
    )(dgc, dgc, dgc, conv_w)


def _to_residue(a, dil):
    s, w = a.shape
    return a.reshape(s // dil, dil, w).transpose(1, 0, 2)


def _from_residue(a):
    dil, l, w = a.shape
    return a.transpose(1, 0, 2).reshape(dil * l, w)


def _stats_to_lanes(rows):
    dil, hq, l = rows.shape
    return jnp.pad(rows.transpose(2, 0, 1).reshape(dil * l, hq), ((0, 0), (0, LANES - hq)))


def _stats_to_rows(lanes, dil):
    s = lanes.shape[0]
    return lanes[:, :DIL_SLOTS].reshape(s // dil, dil, DIL_SLOTS).transpose(1, 2, 0)


def _rope_angles(positions):
    inv_freq = ROPE_THETA ** (-jnp.arange(0, ROT_DIM, 2, dtype=F32) / ROT_DIM)
    ang = positions.astype(F32)[:, None] * inv_freq
    return jnp.concatenate([jnp.cos(ang), jnp.sin(ang)], axis=1)


class _NoPlan:
    def gather(self, stage):
        return None

    def gathered(self, stage, couts, wb):
        pass

    def exchange(self, stage, grads):
        return None

    def exchanged(self, stage, couts):
        pass


def _local_step(x, mem, positions, target, wb, sp, plan=None, *, t_row=256, t_mm=512, tq_a=256, tq_b=128,
                sub_a=2, sub_b=4):
    s = x.shape[0]
    plan = plan or _NoPlan()
    cs = _rope_angles(positions)
    e_mat = _rope_select_matrix()

    (h0b, za, *zb), couts = _proj_all(x, sp["ln_in_g"], sp["ln_in_b"], wb["w_in_seg"], cs, e_mat, t=t_mm,
                                      comm=plan.gather("proj"))
    plan.gathered("proj", couts, wb)
    sub_a = max(1, min(sub_a, s // tq_a))
    subs_b = [max(1, min(sub_b, s // dil // tq_b)) for dil in DILATIONS]
    out_a, lse_a, couts = _swa_fwd_s(za, qcol=0, kcol=4, vcol=5, hq=WIN_Q_HEADS, hkv=WIN_KV_HEADS, w=WIN_HALF,
                                     tq=tq_a, sub=sub_a, sink=sp["attn_sink"], name="attn_a_fwd",
                                     comm=plan.gather("attn_a"))
    plan.gathered("attn_a", couts, wb)
    o_g, lse_g = [], []
    for gi in range(3):
        o, l, couts = _swa_fwd_s(zb[gi], qcol=0, kcol=1, vcol=2, hq=DIL_SLOTS, hkv=DIL_SLOTS, w=DIL_HALF, tq=tq_b,
                                 sub=subs_b[gi], sink=None, name=f"attn_b{gi}_fwd",
                                 comm=plan.gather(f"attn_b{gi}"))
        plan.gathered(f"attn_b{gi}", couts, wb)
        o_g.append(_from_residue(o))
        lse_g.append(_stats_to_lanes(l))
    out_a = out_a[0]
    mixed_b, out_b, lse_b = _combine_fwd(out_a, o_g, lse_g, sp["g_win"], sp["g_dil"], t=t_row)
    r1, h1b = _mixproj_fwd(mixed_b, wb["w_mix_out"], x, sp["ln_in_g"], sp["ln_in_b"], sp["ln1_g"], sp["ln1_b"],
                           t=t_row)
    mem_nb, kx, vx = _mem_fwd(mem, sp["mem_ln_g"], sp["mem_ln_b"], wb["w_xk"], wb["w_xv"])
    r2, h2b, qxb, oxb, lse_x = _xattn_fwd(h1b, r1, kx, vx, wb["w_xq"], wb["w_xo"], sp["ln1_g"], sp["ln1_b"],
                                          sp["ln2_g"], sp["ln2_b"], t=t_row)
    g = _mm(h2b, wb["w_gate"], mode="nn", out_dtype=F32, tm=t_mm, tn=D_FF, name="ff_gate")
    u = _mm(h2b, wb["w_up"], mode="nn", out_dtype=F32, tm=t_mm, tn=D_FF, name="ff_up")
    tb = _conv_fwd(g, u, sp["conv_w"], sp["conv_b"], t=t_row)
    dr3, dr3b, st3 = _down_ln3(tb, wb["w_down"], r2, target, sp["ln2_g"], sp["ln2_b"], sp["ln3_g"], sp["ln3_b"],
                               t=t_row)

    grads = {}
    du, dgc, st_conv = _conv_bwd_a(dr3b, wb["w_down"], g, u, sp["conv_w"], sp["conv_b"], t=t_row)
    dg = _conv_bwd_b(dgc, sp["conv_w"], t=t_row)
    tk = min(1024, s)
    grads["w_down"] = _mm(tb, dr3b, mode="tn", out_dtype=BF16, tm=D_FF // 2, tn=D_MODEL, tk=tk, name="dw_down")
    grads["w_gate"] = _mm(h2b, dg, mode="tn", out_dtype=BF16, tm=D_MODEL, tn=D_FF // 2, tk=tk, name="dw_gate")
    grads["w_up"] = _mm(h2b, du, mode="tn", out_dtype=BF16, tm=D_MODEL, tn=D_FF // 2, tk=tk, name="dw_up")
    dh2, couts = _mm2_nt(dg, wb["w_gate"], du, wb["w_up"], dr3, add_scale=ALPHA, tm=t_mm, name="dh2",
                         comm=plan.exchange("dh2", grads))
    plan.exchanged("dh2", couts)

    dr2, dr2b, st2 = _ln_bwd(dh2, r2, sp["ln2_g"], t=t_row, name="ln2_bwd", want_bf16=True)
    (dh1, dqxb, dkx, dvx), couts = _xattn_bwd(dr2, qxb, oxb, lse_x, kx, vx, wb["w_xq"], wb["w_xo"], t=t_row,
                                              comm=plan.exchange("xattn", grads))
    plan.exchanged("xattn", couts)
    grads["w_xo"] = _mm(oxb, dr2b, mode="tn", out_dtype=BF16, tm=D_MODEL, tn=D_MODEL, tk=tk, name="dw_xo")
    grads["w_xq"] = _mm(h1b, dqxb, mode="tn", out_dtype=BF16, tm=D_MODEL, tn=D_MODEL, tk=tk, name="dw_xq")
    grads["w_xk"], grads["w_xv"], st_mem = _mem_bwd(dkx, dvx, mem, sp["mem_ln_g"], sp["mem_ln_b"],
                                                    wb["w_xk"], wb["w_xv"])

    dr1, dr1b, st1 = _ln_bwd(dh1, r1, sp["ln1_g"], t=t_row, name="ln1_bwd", want_bf16=True)
    grads["w_mix_out"] = _mm(mixed_b, dr1b, mode="tn", out_dtype=BF16, tm=D_MODEL, tn=D_MODEL, tk=tk,
                             name="dw_mix")
    dmixed = _mm(dr1b, wb["w_mix_out"], mode="nt", out_dtype=F32, tm=t_mm, tn=D_MODEL, name="dmixed")
    do_a, do_b, dl_a, dl_b, st_mix = _combine_bwd(dmixed, out_a, out_b, sp["g_win"], sp["g_dil"], t=t_row)
    (dqa, dka, dva, dsink), couts = _swa_bwd_s(
        za, do_a[None], lse_a, _stats_to_rows(dl_a, 1), cs[None], e_mat, qcol=0, kcol=4, vcol=5, hq=WIN_Q_HEADS,
        hkv=WIN_KV_HEADS, w=WIN_HALF, tq=tq_a, sub=sub_a, sink=sp["attn_sink"], name="attn_a_bwd",
        comm=plan.exchange("attn_a", grads))
    plan.exchanged("attn_a", couts)
    dqs, dks, dvs = [], [], []
    for gi, dil in enumerate(DILATIONS):
        (dq, dk, dv), _ = _swa_bwd_s(
            zb[gi], _to_residue(do_b, dil), _stats_to_rows(lse_b, dil), _stats_to_rows(dl_b, dil),
            _to_residue(cs, dil), e_mat, qcol=0, kcol=1, vcol=2, hq=DIL_SLOTS, hkv=DIL_SLOTS, w=DIL_HALF, tq=tq_b,
            sub=subs_b[gi], sink=None, name=f"attn_b{gi}_bwd")
        dqs.append(_from_residue(dq))
        dks.append(_from_residue(dk))
        dvs.append(_from_residue(dv))
    dz = jnp.concatenate([dqa[0], dka[0], dva[0]] + dqs + dks + dvs, axis=1)
    grads["w_in"] = _mm(h0b, dz, mode="tn", out_dtype=BF16, tm=D_MODEL, tn=IN_WIDTH // 7, tk=tk, name="dw_in")
    comm = plan.exchange("dh0", grads)
    dh0 = _mm(dz, wb["w_in"], mode="nt", out_dtype=F32, tm=t_mm, tn=D_MODEL, add=dr1, add_scale=ALPHA, name="dh0",
              comm=comm)
    if comm is not None:
        dh0, couts = dh0
        plan.exchanged("dh0", couts)
    grad_x, st0 = _ln_bwd(dh0, x, sp["ln_in_g"], t=t_row, name="ln_in_bwd", want_bf16=False)

    small = {
        "loss": st3[2:3, 0:1],
        "ln_in_g": st0[0:1], "ln_in_b": st0[1:2],
        "attn_sink": dsink[:, 0].reshape(1, WIN_Q_HEADS),
        "g_win": st_mix[0:1], "g_dil": st_mix[1:2],
        "ln1_g": st1[0:1], "ln1_b": st1[1:2],
        "mem_ln_g": st_mem[0:1], "mem_ln_b": st_mem[1:2],
        "ln2_g": st2[0:1], "ln2_b": st2[1:2],
        "conv_w": st_conv[0:3], "conv_b": st_conv[3:4],
        "ln3_g": st3[0:1], "ln3_b": st3[1:2],
    }
    return grad_x, grads, small


def _swap_sibling(arrays):
    n = len(arrays)

    def body(*refs):
        src, dst = refs[:n], refs[n:2 * n]
        send_sems, recv_sems = refs[2 * n:]
        x, y, c, _ = _place()
        copies = [pltpu.make_async_remote_copy(
            src_ref=src[a], dst_ref=dst[a], send_sem=send_sems.at[a], recv_sem=recv_sems.at[a],
            device_id=(x, y, 1 - c), device_id_type=MESH_IDS) for a in range(n)]
        for cp in copies:
            cp.start()
        for cp in copies:
            cp.wait_recv()
        for cp in copies:
            cp.wait_send()

    return pl.pallas_call(
        body, name="swap_sibling", in_specs=[ANY] * n, out_specs=[ANY] * n,
        out_shape=[jax.ShapeDtypeStruct(a.shape, a.dtype) for a in arrays],
        scratch_shapes=[pltpu.SemaphoreType.DMA((n,)), pltpu.SemaphoreType.DMA((n,))],
    )(*arrays)


def _row_tile(rows, cols, itemsize=4, budget=1 << 20):
    best = None
    for t in range(16, rows + 1, 16):
        if rows % t == 0 and t * cols * itemsize <= budget:
            best = t
    return best or rows


def _sum_slots(stack, *, name):
    n, r, c = stack.shape
    t = _row_tile(r, c)

    def body(s_ref, o_ref):
        acc = s_ref[0].astype(F32)
        for q in range(1, n):
            acc = acc + s_ref[q].astype(F32)
        o_ref[...] = acc

    return pl.pallas_call(
        body, name=name, grid=(r // t,), in_specs=[pl.BlockSpec((n, t, c), lambda i: (0, i, 0))],
        out_specs=pl.BlockSpec((t, c), lambda i: (i, 0)), out_shape=jax.ShapeDtypeStruct((r, c), F32),
        compiler_params=_cparams(dimension_semantics=("parallel",)),
    )(stack)


def _adamw(w, m, v, p, q, *, name):
    r, c = w.shape
    t = _row_tile(r, c, budget=1 << 19)

    def body(*refs):
        if q is None:
            w_ref, m_ref, v_ref, p_ref, g_ref, d_ref, nm_ref, nv_ref = refs
            g = p_ref[...]
        else:
            w_ref, m_ref, v_ref, p_ref, q_ref, g_ref, d_ref, nm_ref, nv_ref = refs
            g = p_ref[...] + q_ref[...]
        nm = ADAM_B1 * m_ref[...] + (1.0 - ADAM_B1) * g
        nv = ADAM_B2 * v_ref[...] + (1.0 - ADAM_B2) * (g * g)
        m_hat = nm / (1.0 - ADAM_B1 ** ADAM_STEP)
        v_hat = nv / (1.0 - ADAM_B2 ** ADAM_STEP)
        g_ref[...] = g
        d_ref[...] = -ADAM_LR * (m_hat / (jnp.sqrt(v_hat) + ADAM_EPS) + ADAM_WD * w_ref[...])
        nm_ref[...] = nm
        nv_ref[...] = nv

    tile = pl.BlockSpec((t, c), lambda i: (i, 0))
    args = [w, m, v, p] + ([] if q is None else [q])
    sh = jax.ShapeDtypeStruct((r, c), F32)
    return pl.pallas_call(
        body, name=name, grid=(r // t,), in_specs=[tile] * len(args), out_specs=[tile] * 4, out_shape=[sh] * 4,
        compiler_params=_cparams(dimension_semantics=("parallel",)),
    )(*args)


BIG = ("w_in", "w_mix_out", "w_xq", "w_xk", "w_xv", "w_xo", "w_gate", "w_up", "w_down")
COL_SHARDED = ("w_in", "w_gate", "w_up")
WEIGHTS = ("ln_in_g", "ln_in_b", "w_in", "attn_sink", "g_win", "g_dil", "w_mix_out", "ln1_g", "ln1_b",
           "mem_ln_g", "mem_ln_b", "w_xq", "w_xk", "w_xv", "w_xo", "ln2_g", "ln2_b", "w_gate", "w_up",
           "conv_w", "conv_b", "w_down", "ln3_g", "ln3_b")
SMALL = tuple(k for k in WEIGHTS if k not in BIG)
PACK_COLS = 1024
CONV_SHARD = D_FF // N_CHIPS
CONV_WIDTH_ROWS = 3
SMALL_ROWS = 32


GATHER_STAGES = {"proj": ("w_mix_out", "w_xq", "w_xk", "w_xv", "w_xo"), "attn_a": ("w_gate", "w_up"),
                 "attn_b0": ("w_down",)}
EXCHANGE_STAGES = {"dh2": ("w_down",), "xattn": ("w_gate", "w_up"),
                   "attn_a": ("w_xo", "w_xq", "w_xk", "w_xv", "w_mix_out"), "dh0": ("w_in",)}


def _full_weight(k, g4):
    if k in COL_SHARDED:
        return g4.transpose(1, 0, 2).reshape(g4.shape[1], N_CHIPS * g4.shape[2])
    return g4.reshape(N_CHIPS * g4.shape[1], g4.shape[2])


def _grad_parts(k, gk):
    gk = gk.astype(BF16)
    if k in COL_SHARDED:
        return gk.reshape(gk.shape[0], N_CHIPS, gk.shape[1] // N_CHIPS).transpose(1, 0, 2)
    return gk.reshape(N_CHIPS, gk.shape[0] // N_CHIPS, gk.shape[1])


class _Plan:
    def __init__(self, shards):
        self.shards = shards
        self.recv = {}

    def gather(self, stage):
        names = GATHER_STAGES.get(stage)
        return _ChipGather([self.shards[k] for k in names]) if names else None

    def gathered(self, stage, couts, wb):
        for k, g4 in zip(GATHER_STAGES.get(stage, ()), couts):
            wb[k] = _full_weight(k, g4)

    def exchange(self, stage, grads):
        names = EXCHANGE_STAGES.get(stage)
        return _ChipExchange([_grad_parts(k, grads[k]) for k in names]) if names else None

    def exchanged(self, stage, couts):
        for k, r4 in zip(EXCHANGE_STAGES.get(stage, ()), couts):
            self.recv[k] = r4


def _pack_rows(a):
    r, n = a.shape
    per = -(-n // PACK_COLS)
    return jnp.pad(a, ((0, 0), (0, per * PACK_COLS - n))).reshape(r * per, PACK_COLS)


def _unpack_rows(p, r, n):
    per = -(-n // PACK_COLS)
    return p.reshape(r, per * PACK_COLS)[:, :n]


def _pack(pieces, rows_total):
    cat = jnp.concatenate([_pack_rows(a) for a in pieces], axis=0)
    return jnp.pad(cat, ((0, rows_total - cat.shape[0]), (0, 0)))


def _unpack(p, shapes):
    out, at = [], 0
    for r, n in shapes:
        per = -(-n // PACK_COLS)
        out.append(_unpack_rows(p[at:at + r * per], r, n))
        at += r * per
    return out


def kernel(x, mem, positions, ln_in_g, ln_in_b, w_in, attn_sink, g_win, g_dil, w_mix_out, ln1_g, ln1_b, mem_ln_g, mem_ln_b, w_xq, w_xk, w_xv, w_xo, ln2_g, ln2_b, w_gate, w_up, conv_w, conv_b, w_down, ln3_g, ln3_b, loss_target, m_ln_in_g, m_ln_in_b, m_w_in, m_attn_sink, m_g_win, m_g_dil, m_w_mix_out, m_ln1_g, m_ln1_b, m_mem_ln_g, m_mem_ln_b, m_w_xq, m_w_xk, m_w_xv, m_w_xo, m_ln2_g, m_ln2_b, m_w_gate, m_w_up, m_conv_w, m_conv_b, m_w_down, m_ln3_g, m_ln3_b, v_ln_in_g, v_ln_in_b, v_w_in, v_attn_sink, v_g_win, v_g_dil, v_w_mix_out, v_ln1_g, v_ln1_b, v_mem_ln_g, v_mem_ln_b, v_w_xq, v_w_xk, v_w_xv, v_w_xo, v_ln2_g, v_ln2_b, v_w_gate, v_w_up, v_conv_w, v_conv_b, v_w_down, v_ln3_g, v_ln3_b):
    given = dict(locals())
    shape_of = {k: given[k].shape for k in WEIGHTS}
    as2d = lambda a: a.reshape(-1, a.shape[-1])
    w2 = {k: as2d(given[k]) for k in WEIGHTS}
    m2 = {k: as2d(given["m_" + k]) for k in WEIGHTS}
    v2 = {k: as2d(given["v_" + k]) for k in WEIGHTS}
    chip = 2 * lax.axis_index("x") + lax.axis_index("y")

    plan = _Plan({k: w2[k].astype(BF16) for k in BIG})
    conv_pack = jnp.pad(w2["conv_w"], ((0, 16 - CONV_WIDTH_ROWS), (0, PACK_COLS - CONV_SHARD)))
    g_in, g_conv = _comm_only(_ChipGather([plan.shards["w_in"], conv_pack]), "gather_w_in")
    w_in_full = _full_weight("w_in", g_in)
    wb = {"w_in": w_in_full,
          "w_in_seg": jnp.concatenate([w_in_full[:, a:b] for a, b in _proj_column_ranges()], axis=1)}
    conv_full = g_conv[:, :CONV_WIDTH_ROWS, :CONV_SHARD].transpose(1, 0, 2).reshape(CONV_WIDTH_ROWS, D_FF)
    sp = {k: w2[k] for k in SMALL}
    sp["conv_w"] = conv_full

    grad_x, grads, small = _local_step(x[0], mem[0], positions[0], loss_target[0], wb, sp, plan)

    small_keys = ("loss",) + SMALL
    small_shapes = [small[k].shape for k in small_keys]
    small_pack = _pack([small[k] for k in small_keys], SMALL_ROWS)
    (small_all,) = _comm_only(_ChipExchange([], small_pack), "exchange_small")
    chip_sums = [_sum_slots(plan.recv[k], name=f"sum_chips_{k}") for k in BIG]
    sibling_sums = _swap_sibling(chip_sums)
    small_sum = _sum_slots(small_all, name="sum_small")
    small_g = dict(zip(small_keys, _unpack(small_sum, small_shapes)))
    loss = small_g["loss"][0, 0]

    res = {}
    for k, p, q in zip(BIG, chip_sums, sibling_sums):
        res[k] = _adamw(w2[k], m2[k], v2[k], p, q, name=f"adamw_{k}")
    small_g["conv_w"] = lax.dynamic_slice_in_dim(small_g["conv_w"], chip * CONV_SHARD, CONV_SHARD, axis=1)
    adam_shapes = [w2[k].shape for k in SMALL]
    packs = [_pack([d[k] for k in SMALL], SMALL_ROWS) for d in (w2, m2, v2, small_g)]
    small_res = [_unpack(o, adam_shapes) for o in _adamw(*packs, None, name="adamw_small")]
    for i, k in enumerate(SMALL):
        res[k] = tuple(o[i] for o in small_res)

    outs = [loss, grad_x[None]]
    for slot in range(4):
        outs += [res[k][slot].reshape(shape_of[k]) for k in WEIGHTS]
    return tuple(outs)
```

```python
import functools
import math

import jax
import jax.numpy as jnp
from jax import lax
from jax.experimental import pallas as pl
from jax.experimental.pallas import tpu as pltpu

F32 = jnp.float32
BF16 = jnp.bfloat16

D_MODEL = 1024
HEAD_DIM = 64
WIN_Q_HEADS = 8
WIN_KV_HEADS = 2
WIN_HALF = 128
DIL_SLOTS = 8
DILATIONS = (1, 4, 16)
DIL_HALF = 64
ROT_DIM = 16
ROPE_THETA = 500000.0
X_HEADS = 4
X_HEAD_DIM = 256
D_FF = 2816
A_Q = 512
A_KV = 128
A_WIDTH = A_Q + 2 * A_KV
B_QKV = 1536
IN_WIDTH = 5376
ALPHA = 2.0 ** 0.25
LN_EPS = 1e-5
NEG_INF = -1e30
LANES = 128
N_CHIPS = 4
N_DEV = 8

ADAM_LR = 0.001
ADAM_B1 = 0.9
ADAM_B2 = 0.999
ADAM_EPS = 1e-08
ADAM_WD = 0.01
ADAM_STEP = 10

VMEM_LIMIT = 56 * 1024 * 1024


def _cparams(**kw):
    return pltpu.CompilerParams(vmem_limit_bytes=VMEM_LIMIT, **kw)


def _dot(a, b):
    return lax.dot_general(a, b, (((1,), (0,)), ((), ())), preferred_element_type=F32)


def _dot_nt(a, b):
    return lax.dot_general(a, b, (((1,), (1,)), ((), ())), preferred_element_type=F32)


def _dot_tn(a, b):
    return lax.dot_general(a, b, (((0,), (0,)), ((), ())), preferred_element_type=F32)


def _ln(x, g, b):
    mu = jnp.mean(x, axis=-1, keepdims=True)
    xc = x - mu
    var = jnp.mean(xc * xc, axis=-1, keepdims=True)
    return xc * lax.rsqrt(var + LN_EPS) * g + b


def _ln_bwd_math(dy, r, g):
    mu = jnp.mean(r, axis=-1, keepdims=True)
    xc = r - mu
    var = jnp.mean(xc * xc, axis=-1, keepdims=True)
    rstd = lax.rsqrt(var + LN_EPS)
    xhat = xc * rstd
    dxhat = dy * g
    m1 = jnp.mean(dxhat, axis=-1, keepdims=True)
    m2 = jnp.mean(dxhat * xhat, axis=-1, keepdims=True)
    dr = rstd * (dxhat - m1 - xhat * m2)
    return dr, jnp.sum(dy * xhat, axis=0, keepdims=True), jnp.sum(dy, axis=0, keepdims=True)


def _rope(z, ta, tb, tc, sign):
    w = z.shape[1]
    reps = w // LANES
    a = jnp.tile(ta, (1, reps))
    b = jnp.tile(tb, (1, reps))
    c = jnp.tile(tc, (1, reps))
    return z * a + sign * (pltpu.roll(z, w - 8, 1) * b + pltpu.roll(z, 8, 1) * c)


def _shift_rows(x, prev_row, next_row):
    t = x.shape[0]
    row = lax.broadcasted_iota(jnp.int32, x.shape, 0)
    xm1 = jnp.where(row == 0, prev_row, pltpu.roll(x, 1, 0))
    xp1 = jnp.where(row == t - 1, next_row, pltpu.roll(x, t - 1, 0))
    return xm1, xp1


def _rope_tabs(cs, e_mat):
    tabs = lax.dot_general(cs, e_mat, (((1,), (0,)), ((), ())), preferred_element_type=F32,
                           precision=lax.Precision.HIGHEST)
    lane = lax.broadcasted_iota(jnp.int32, (cs.shape[0], LANES), 1)
    ones = jnp.where((lane & (HEAD_DIM - 1)) >= ROT_DIM, 1.0, 0.0)
    return tabs[:, :LANES] + ones, tabs[:, LANES:2 * LANES], tabs[:, 2 * LANES:]


def _rope_select_matrix():
    half = ROT_DIM // 2
    e = [[0.0] * (3 * LANES) for _ in range(ROT_DIM)]
    for lane in range(LANES):
        d = lane % HEAD_DIM
        if d < half:
            e[d][lane] = 1.0
            e[half + d][LANES + lane] = -1.0
        elif d < ROT_DIM:
            e[d - half][lane] = 1.0
            e[d][2 * LANES + lane] = 1.0
    return jnp.array(e, F32)


MESH_IDS = pl.DeviceIdType.MESH
ANY = pl.BlockSpec(memory_space=pl.ANY)


def _place():
    x, y, c = lax.axis_index("x"), lax.axis_index("y"), lax.axis_index("c")
    other_chips = [(1 - x, y), (x, 1 - y), (1 - x, 1 - y)]
    return x, y, c, other_chips


class _ChipGather:
    def __init__(self, shards):
        self.inputs = list(shards)
        n = len(shards)
        self.out_shape = [jax.ShapeDtypeStruct((N_CHIPS,) + a.shape, a.dtype) for a in shards]
        self.scratch = [pltpu.SemaphoreType.DMA((6 * n,)), pltpu.SemaphoreType.DMA((6 * n,)),
                        pltpu.SemaphoreType.DMA((n,))]

    def _copies(self, src, dst, sems):
        send_sems, recv_sems, local_sems = sems
        x, y, c, chips = _place()
        mine = 2 * x + y
        n = len(src)
        local, sends, recvs, passes, pass_recvs = [], [], [], [], []
        for a in range(n):
            half = src[a].shape[0] // 2
            my_rows, other_rows = pl.ds(c * half, half), pl.ds((1 - c) * half, half)
            local.append(pltpu.make_async_copy(src[a], dst[a].at[mine], local_sems.at[a]))
            for j, (px, py) in enumerate(chips):
                k, k2, slot = 3 * a + j, 3 * n + 3 * a + j, 2 * px + py
                sends.append(pltpu.make_async_remote_copy(
                    src_ref=src[a].at[my_rows], dst_ref=dst[a].at[mine, my_rows], send_sem=send_sems.at[k],
                    recv_sem=recv_sems.at[k], device_id=(px, py, c), device_id_type=MESH_IDS))
                recvs.append(pltpu.make_async_remote_copy(
                    src_ref=src[a].at[my_rows], dst_ref=dst[a].at[slot, my_rows], send_sem=send_sems.at[k],
                    recv_sem=recv_sems.at[k], device_id=(px, py, c), device_id_type=MESH_IDS))
                passes.append(pltpu.make_async_remote_copy(
                    src_ref=dst[a].at[slot, my_rows], dst_ref=dst[a].at[slot, my_rows], send_sem=send_sems.at[k2],
                    recv_sem=recv_sems.at[k2], device_id=(x, y, 1 - c), device_id_type=MESH_IDS))
                pass_recvs.append(pltpu.make_async_remote_copy(
                    src_ref=dst[a].at[slot, my_rows], dst_ref=dst[a].at[slot, other_rows],
                    send_sem=send_sems.at[k2], recv_sem=recv_sems.at[k2], device_id=(x, y, 1 - c),
                    device_id_type=MESH_IDS))
        return local, sends, recvs, passes, pass_recvs

    def start(self, src, dst, sems):
        local, sends, _, _, _ = self._copies(src, dst, sems)
        for cp in local + sends:
            cp.start()

    def wait(self, src, dst, sems):
        local, sends, recvs, passes, pass_recvs = self._copies(src, dst, sems)
        for idx, landed in enumerate(recvs):
            landed.wait_recv()
            if passes:
                passes[idx].start()
        for cp in pass_recvs:
            cp.wait_recv()
        for cp in sends + passes:
            cp.wait_send()
        for cp in local:
            cp.wait()


class _ChipExchange:
    def __init__(self, parts, small=None):
        self.inputs = list(parts) + ([small] if small is not None else [])
        self.n = len(parts)
        self.has_small = small is not None
        self.out_shape = [jax.ShapeDtypeStruct(a.shape, a.dtype) for a in parts]
        n_sem, n_loc = 3 * self.n, self.n
        if self.has_small:
            self.out_shape.append(jax.ShapeDtypeStruct((N_DEV,) + small.shape, small.dtype))
            n_sem, n_loc = n_sem + N_DEV - 1, n_loc + 1
        self.scratch = [pltpu.SemaphoreType.DMA((n_sem,)), pltpu.SemaphoreType.DMA((n_sem,)),
                        pltpu.SemaphoreType.DMA((n_loc,))]

    def _copies(self, src, dst, sems):
        send_sems, recv_sems, local_sems = sems
        x, y, c, chips = _place()
        mine = 2 * x + y
        n = self.n
        local, sends, recvs = [], [], []
        for a in range(n):
            local.append(pltpu.make_async_copy(src[a].at[mine], dst[a].at[mine], local_sems.at[a]))
            for j, (px, py) in enumerate(chips):
                k = 3 * a + j
                sends.append(pltpu.make_async_remote_copy(
                    src_ref=src[a].at[2 * px + py], dst_ref=dst[a].at[mine], send_sem=send_sems.at[k],
                    recv_sem=recv_sems.at[k], device_id=(px, py, c), device_id_type=MESH_IDS))
                recvs.append(pltpu.make_async_remote_copy(
                    src_ref=src[a].at[mine], dst_ref=dst[a].at[2 * px + py], send_sem=send_sems.at[k],
                    recv_sem=recv_sems.at[k], device_id=(px, py, c), device_id_type=MESH_IDS))
        if self.has_small:
            me_dev = 4 * x + 2 * y + c
            local.append(pltpu.make_async_copy(src[n], dst[n].at[me_dev], local_sems.at[n]))
            for mask in range(1, N_DEV):
                px, py, pc = x ^ ((mask >> 2) & 1), y ^ ((mask >> 1) & 1), c ^ (mask & 1)
                k = 3 * n + mask - 1
                sends.append(pltpu.make_async_remote_copy(
                    src_ref=src[n], dst_ref=dst[n].at[me_dev], send_sem=send_sems.at[k], recv_sem=recv_sems.at[k],
                    device_id=(px, py, pc), device_id_type=MESH_IDS))
                recvs.append(pltpu.make_async_remote_copy(
                    src_ref=src[n], dst_ref=dst[n].at[4 * px + 2 * py + pc], send_sem=send_sems.at[k],
                    recv_sem=recv_sems.at[k], device_id=(px, py, pc), device_id_type=MESH_IDS))
        return local, sends, recvs, [], []

    start = _ChipGather.start
    wait = _ChipGather.wait


def _pcall(body, *, name, grid, in_specs, out_specs, out_shape, args, scratch_shapes=(), dims=None, comm=None):
    in_specs, out_specs, out_shape = list(in_specs), list(out_specs), list(out_shape)
    scratch_shapes = list(scratch_shapes)
    if comm is None:
        outs = pl.pallas_call(
            body, name=name, grid=grid, in_specs=in_specs, out_specs=out_specs, out_shape=out_shape,
            scratch_shapes=scratch_shapes, compiler_params=_cparams(dimension_semantics=dims),
        )(*args)
        return list(outs), []
    n_in, n_out, n_scr = len(in_specs), len(out_specs), len(scratch_shapes)
    n_cin, n_cout = len(comm.inputs), len(comm.out_shape)

    def wrapped(*refs):
        ins, refs = refs[:n_in], refs[n_in:]
        cins, refs = refs[:n_cin], refs[n_cin:]
        outs, refs = refs[:n_out], refs[n_out:]
        couts, refs = refs[:n_cout], refs[n_cout:]
        scr, csems = refs[:n_scr], refs[n_scr:]
        first = last = None
        for axis, size in enumerate(grid):
            pid = pl.program_id(axis)
            f, l = pid == 0, pid == size - 1
            first = f if first is None else first & f
            last = l if last is None else last & l

        @pl.when(first)
        def _():
            comm.start(cins, couts, csems)

        body(*ins, *outs, *scr)

        @pl.when(last)
        def _():
            comm.wait(cins, couts, csems)

    res = pl.pallas_call(
        wrapped, name=name, grid=grid, in_specs=in_specs + [ANY] * n_cin, out_specs=out_specs + [ANY] * n_cout,
        out_shape=out_shape + list(comm.out_shape), scratch_shapes=scratch_shapes + list(comm.scratch),
        compiler_params=_cparams(dimension_semantics=("arbitrary",) * len(grid)),
    )(*args, *comm.inputs)
    return list(res[:n_out]), list(res[n_out:])


def _comm_only(comm, name):
    def body(*refs):
        n_cin, n_cout = len(comm.inputs), len(comm.out_shape)
        cins, couts, csems = refs[:n_cin], refs[n_cin:n_cin + n_cout], refs[n_cin + n_cout:]
        comm.start(cins, couts, csems)
        comm.wait(cins, couts, csems)

    return list(pl.pallas_call(
        body, name=name, in_specs=[ANY] * len(comm.inputs), out_specs=[ANY] * len(comm.out_shape),
        out_shape=list(comm.out_shape), scratch_shapes=list(comm.scratch),
    )(*comm.inputs))


def _mm(a, b, *, mode, out_dtype, tm, tn, tk=None, add=None, add_scale=1.0, name, comm=None):
    if mode in ("nn", "nt"):
        m, k = a.shape
        n = b.shape[1] if mode == "nn" else b.shape[0]
        assert m % tm == 0 and n % tn == 0
        dot = _dot if mode == "nn" else _dot_nt

        def body(*refs):
            if add is None:
                a_ref, b_ref, o_ref = refs
                o_ref[...] = dot(a_ref[...], b_ref[...]).astype(out_dtype)
            else:
                a_ref, b_ref, c_ref, o_ref = refs
                o_ref[...] = (dot(a_ref[...], b_ref[...]) + add_scale * c_ref[...]).astype(out_dtype)

        b_spec = (pl.BlockSpec((k, tn), lambda i, j: (0, j)) if mode == "nn"
                  else pl.BlockSpec((tn, k), lambda i, j: (j, 0)))
        in_specs = [pl.BlockSpec((tm, k), lambda i, j: (i, 0)), b_spec]
        args = [a, b]
        if add is not None:
            in_specs.append(pl.BlockSpec((tm, tn), lambda i, j: (i, j)))
            args.append(add)
        outs, couts = _pcall(
            body, name=name, grid=(m // tm, n // tn), in_specs=in_specs,
            out_specs=[pl.BlockSpec((tm, tn), lambda i, j: (i, j))],
            out_shape=[jax.ShapeDtypeStruct((m, n), out_dtype)], args=args, dims=("parallel", "parallel"),
            comm=comm)
        return outs[0] if comm is None else (outs[0], couts)
    assert mode == "tn" and add is None and comm is None
    kk, m = a.shape
    n = b.shape[1]
    assert m % tm == 0 and n % tn == 0 and kk % tk == 0
    nk = kk // tk

    def body(a_ref, b_ref, o_ref, acc_ref):
        kstep = pl.program_id(2)

        @pl.when(kstep == 0)
        def _():
            acc_ref[...] = jnp.zeros_like(acc_ref)

        acc_ref[...] += _dot_tn(a_ref[...], b_ref[...])

        @pl.when(kstep == nk - 1)
        def _():
            o_ref[...] = acc_ref[...].astype(out_dtype)

    return pl.pallas_call(
        body, name=name, grid=(m // tm, n // tn, nk),
        in_specs=[pl.BlockSpec((tk, tm), lambda i, j, s: (s, i)), pl.BlockSpec((tk, tn), lambda i, j, s: (s, j))],
        out_specs=pl.BlockSpec((tm, tn), lambda i, j, s: (i, j)),
        out_shape=jax.ShapeDtypeStruct((m, n), out_dtype),
        scratch_shapes=[pltpu.VMEM((tm, tn), F32)],
        compiler_params=_cparams(dimension_semantics=("parallel", "parallel", "arbitrary")),
    )(a, b)


def _mm2_nt(a1, b1, a2, b2, add, *, add_scale, tm, name, comm=None):
    m, k = a1.shape
    n = b1.shape[0]

    def body(a1_ref, b1_ref, a2_ref, b2_ref, c_ref, o_ref):
        o_ref[...] = (_dot_nt(a1_ref[...], b1_ref[...]) + _dot_nt(a2_ref[...], b2_ref[...])
                      + add_scale * c_ref[...])

    a_spec = pl.BlockSpec((tm, k), lambda i: (i, 0))
    b_spec = pl.BlockSpec((n, k), lambda i: (0, 0))
    o_spec = pl.BlockSpec((tm, n), lambda i: (i, 0))
    outs, couts = _pcall(body, name=name, grid=(m // tm,), in_specs=[a_spec, b_spec, a_spec, b_spec, o_spec],
                         out_specs=[o_spec], out_shape=[jax.ShapeDtypeStruct((m, n), F32)],
                         args=[a1, b1, a2, b2, add], dims=("parallel",), comm=comm)
    return outs[0], couts


def _ln_bwd(dy, r, g, *, t, name, want_bf16):
    s = r.shape[0]

    def body(dy_ref, r_ref, g_ref, *outs):
        i = pl.program_id(0)
        dr, dg, db = _ln_bwd_math(dy_ref[...], r_ref[...], g_ref[...])
        outs[0][...] = dr
        if want_bf16:
            outs[1][...] = dr.astype(BF16)
        st_ref = outs[-1]

        @pl.when(i == 0)
        def _():
            st_ref[...] = jnp.zeros_like(st_ref)

        st_ref[0:1, :] += dg
        st_ref[1:2, :] += db

    tile = pl.BlockSpec((t, D_MODEL), lambda i: (i, 0))
    out_specs = [tile] + ([tile] if want_bf16 else []) + [pl.BlockSpec((8, D_MODEL), lambda i: (0, 0))]
    out_shape = ([jax.ShapeDtypeStruct((s, D_MODEL), F32)]
                 + ([jax.ShapeDtypeStruct((s, D_MODEL), BF16)] if want_bf16 else [])
                 + [jax.ShapeDtypeStruct((8, D_MODEL), F32)])
    return pl.pallas_call(
        body, name=name, grid=(s // t,),
        in_specs=[tile, tile, pl.BlockSpec((1, D_MODEL), lambda i: (0, 0))],
        out_specs=out_specs, out_shape=out_shape,
        compiler_params=_cparams(dimension_semantics=("arbitrary",)),
    )(dy, r, g)


PROJ_COLS = 256
PROJ_SEGMENTS = ((1, 0, (1, 1, 2)),) + tuple(
    (dil, A_WIDTH + gi * B_QKV, (1, 1, 1, 1, 0, 0)) for gi, dil in enumerate(DILATIONS))


def _proj_column_ranges():
    wd = DIL_SLOTS * HEAD_DIM
    ranges = [(0, A_WIDTH)]
    for gi in range(len(DILATIONS)):
        ranges += [(A_WIDTH + part * B_QKV + gi * wd, A_WIDTH + part * B_QKV + (gi + 1) * wd) for part in range(3)]
    return ranges


def _proj_all(x, g, b, w_seg, cs, e_mat, *, t, comm=None):
    s = x.shape[0]
    cb = PROJ_COLS
    halves = cb // LANES

    def body(x_ref, g_ref, b_ref, w_ref, cs_ref, e_ref, h_ref, *rest):
        z_refs, scr = rest[:-1], rest[-1]
        h = _ln(x_ref[...], g_ref[...], b_ref[...]).astype(BF16)
        h_ref[...] = h
        ta, tb, tc = (jnp.tile(tab, (1, halves)) for tab in _rope_tabs(cs_ref[...], e_ref[...]))
        lane = lax.broadcasted_iota(jnp.int32, (t, cb), 1)
        slot = 0
        for z_ref, (dil, col0, kinds) in zip(z_refs, PROJ_SEGMENTS):
            for jb, kind in enumerate(kinds):
                acc = _dot(h, w_ref[:, col0 + cb * jb:col0 + cb * (jb + 1)])
                if kind:
                    z = acc * ta + (pltpu.roll(acc, cb - 8, 1) * tb + pltpu.roll(acc, 8, 1) * tc)
                    if kind == 2:
                        z = jnp.where(lane < LANES, z, acc)
                else:
                    z = acc
                if dil == 1:
                    z_ref[0, :, cb * jb:cb * (jb + 1)] = z.astype(BF16)
                    continue
                for half in range(halves):
                    scr[slot, half] = z[:, half * LANES:(half + 1) * LANES]
                for c in range(dil):
                    for half in range(halves):
                        rows = scr[slot, half, pl.ds(c, t // dil, stride=dil), :]
                        z_ref[c, :, cb * jb + half * LANES:cb * jb + (half + 1) * LANES] = rows.astype(BF16)
                slot = 1 - slot

    row = pl.BlockSpec((1, D_MODEL), lambda i: (0, 0))
    widths = [cb * len(kinds) for _, _, kinds in PROJ_SEGMENTS]
    dils = [dil for dil, _, _ in PROJ_SEGMENTS]
    outs, couts = _pcall(
        body, name="proj_all", grid=(s // t,),
        in_specs=[pl.BlockSpec((t, D_MODEL), lambda i: (i, 0)), row, row,
                  pl.BlockSpec((D_MODEL, IN_WIDTH), lambda i: (0, 0)),
                  pl.BlockSpec((t, ROT_DIM), lambda i: (i, 0)), pl.BlockSpec((ROT_DIM, 3 * LANES), lambda i: (0, 0))],
        out_specs=[pl.BlockSpec((t, D_MODEL), lambda i: (i, 0))]
        + [pl.BlockSpec((dil, t // dil, wd), lambda i: (0, i, 0)) for dil, wd in zip(dils, widths)],
        out_shape=[jax.ShapeDtypeStruct((s, D_MODEL), BF16)]
        + [jax.ShapeDtypeStruct((dil, s // dil, wd), BF16) for dil, wd in zip(dils, widths)],
        args=[x, g, b, w_seg, cs, e_mat], scratch_shapes=[pltpu.VMEM((2, halves, t, LANES), F32)],
        dims=("parallel",), comm=comm)
    return outs, couts


def _window_mask(i, tq, w, seq_len):
    tk = tq + 2 * w
    qpos = i * tq + lax.broadcasted_iota(jnp.int32, (tq, tk), 0)
    kpos = i * tq - w + lax.broadcasted_iota(jnp.int32, (tq, tk), 1)
    return (jnp.abs(qpos - kpos) <= w) & (kpos >= 0) & (kpos < seq_len)


def _swa_specs(tq, hq, hkv, n, qcol, kcol, vcol):
    qw, kw = hq * HEAD_DIM, hkv * HEAD_DIM
    cur = lambda s, i: jnp.minimum(i, n - 1)
    prv = lambda s, i: jnp.maximum(jnp.minimum(i, n - 1) - 1, 0)
    nxt = lambda s, i: jnp.minimum(i + 1, n - 1)
    q_spec = pl.BlockSpec((None, tq, qw), lambda s, i: (s, cur(s, i), qcol))
    kv_specs = [pl.BlockSpec((None, tq, kw), (lambda s, i, f=f, c=c: (s, f(s, i), c)))
                for c in (kcol, vcol) for f in (prv, cur, nxt)]
    return q_spec, kv_specs, cur, prv


def _swa_fwd(qkv, *, qcol, kcol, vcol, hq, hkv, w, tq, sink, name, comm=None):
    nseq, seq_len, _ = qkv.shape
    n = seq_len // tq
    rep = hq // hkv
    q_spec, kv_specs, _, _ = _swa_specs(tq, hq, hkv, n, qcol, kcol, vcol)

    def body(*refs):
        if sink is not None:
            sink_ref, refs = refs[0], refs[1:]
        q_ref, kp_ref, kc_ref, kn_ref, vp_ref, vc_ref, vn_ref, o_ref, lse_ref = refs
        i = pl.program_id(1)
        mask = _window_mask(i, tq, w, seq_len)
        lane = lax.broadcasted_iota(jnp.int32, (tq, LANES), 1)
        lse_acc = jnp.zeros((tq, LANES), F32)
        for g in range(hkv):
            cs = slice(g * HEAD_DIM, (g + 1) * HEAD_DIM)
            kcat = jnp.concatenate([kp_ref[tq - w:, cs], kc_ref[:, cs], kn_ref[:w, cs]], axis=0)
            vcat = jnp.concatenate([vp_ref[tq - w:, cs], vc_ref[:, cs], vn_ref[:w, cs]], axis=0)
            for r in range(rep):
                h = g * rep + r
                hs = slice(h * HEAD_DIM, (h + 1) * HEAD_DIM)
                qh = q_ref[:, hs] * 0.125
                sc = jnp.where(mask, _dot_nt(qh, kcat), NEG_INF)
                m = jnp.max(sc, axis=1, keepdims=True)
                if sink is not None:
                    m = jnp.maximum(m, sink_ref[0, h])
                p = jnp.exp(sc - m)
                den = jnp.sum(p, axis=1, keepdims=True)
                if sink is not None:
                    den = den + jnp.exp(sink_ref[0, h] - m)
                o_ref[:, hs] = _dot(p.astype(BF16), vcat) / den
                lse_acc = jnp.where(lane == h, m + jnp.log(den), lse_acc)
        lse_ref[...] = lse_acc

    in_specs = [q_spec] + kv_specs
    args = [qkv] * 7
    if sink is not None:
        in_specs = [pl.BlockSpec(memory_space=pltpu.SMEM)] + in_specs
        args = [sink] + args
    (o, lse), couts = _pcall(
        body, name=name, grid=(nseq, n), in_specs=in_specs,
        out_specs=[pl.BlockSpec((None, tq, hq * HEAD_DIM), lambda s, i: (s, i, 0)),
                   pl.BlockSpec((None, tq, LANES), lambda s, i: (s, i, 0))],
        out_shape=[jax.ShapeDtypeStruct((nseq, seq_len, hq * HEAD_DIM), F32),
                   jax.ShapeDtypeStruct((nseq, seq_len, LANES), F32)],
        args=args, dims=("parallel", "parallel"), comm=comm)
    return o, lse, couts


def _swa_bwd(qkv, do, lse, delta, cs, e_mat, *, qcol, kcol, vcol, hq, hkv, w, tq, sink, name, comm=None):
    nseq, seq_len, _ = qkv.shape
    n = seq_len // tq
    rep = hq // hkv
    qw, kw = hq * HEAD_DIM, hkv * HEAD_DIM
    tk = tq + 2 * w
    q_spec, kv_specs, cur, prv = _swa_specs(tq, hq, hkv, n, qcol, kcol, vcol)

    def body(*refs):
        if sink is not None:
            sink_ref, refs = refs[0], refs[1:]
        (q_ref, kp_ref, kc_ref, kn_ref, vp_ref, vc_ref, vn_ref, do_ref, lse_ref, dl_ref,
         cs_c, cs_p, e_ref) = refs[:13]
        outs = refs[13:]
        if sink is not None:
            dq_ref, dk_ref, dv_ref, dsink_ref, dk_acc, dv_acc = outs
        else:
            dq_ref, dk_ref, dv_ref, dk_acc, dv_acc = outs
        s_id = pl.program_id(0)
        i = pl.program_id(1)
        slot_p, slot_c, slot_n = (i + 2) % 3, i % 3, (i + 1) % 3

        if sink is not None:
            @pl.when((s_id == 0) & (i == 0))
            def _():
                dsink_ref[...] = jnp.zeros_like(dsink_ref)

        @pl.when(i < n)
        def _():
            mask = _window_mask(i, tq, w, seq_len)
            dk_acc[slot_n] = jnp.zeros((tq, kw), F32)
            dv_acc[slot_n] = jnp.zeros((tq, kw), F32)

            @pl.when(i == 0)
            def _():
                dk_acc[slot_c] = jnp.zeros((tq, kw), F32)
                dv_acc[slot_c] = jnp.zeros((tq, kw), F32)

            dq_parts, dk_parts, dv_parts = [], [], []
            for g in range(hkv):
                cs = slice(g * HEAD_DIM, (g + 1) * HEAD_DIM)
                kcat = jnp.concatenate([kp_ref[tq - w:, cs], kc_ref[:, cs], kn_ref[:w, cs]], axis=0)
                vcat = jnp.concatenate([vp_ref[tq - w:, cs], vc_ref[:, cs], vn_ref[:w, cs]], axis=0)
                dkc = jnp.zeros((tk, HEAD_DIM), F32)
                dvc = jnp.zeros((tk, HEAD_DIM), F32)
                for r in range(rep):
                    h = g * rep + r
                    hs = slice(h * HEAD_DIM, (h + 1) * HEAD_DIM)
                    qh = q_ref[:, hs] * 0.125
                    sc = jnp.where(mask, _dot_nt(qh, kcat), NEG_INF)
                    lse_h = lse_ref[:, h:h + 1]
                    dl_h = dl_ref[:, h:h + 1]
                    p = jnp.exp(sc - lse_h)
                    doh = do_ref[:, hs]
                    dp = _dot_nt(doh, vcat)
                    dsb = (p * (dp - dl_h)).astype(BF16)
                    dq_parts.append(_dot(dsb, kcat) * 0.125)
                    dkc = dkc + _dot_tn(dsb, qh)
                    dvc = dvc + _dot_tn(p.astype(BF16), doh)
                    if sink is not None:
                        ds_sink = -jnp.sum(jnp.exp(sink_ref[0, h] - lse_h) * dl_h)
                        dsink_ref[h:h + 1, :] += jnp.full((1, LANES), ds_sink, F32)
                dk_parts.append(dkc)
                dv_parts.append(dvc)
            dq = jnp.concatenate(dq_parts, axis=1)
            dq_ref[...] = _rope(dq, *_rope_tabs(cs_c[...], e_ref[...]), -1.0).astype(BF16)
            dk_all = jnp.concatenate(dk_parts, axis=1)
            dv_all = jnp.concatenate(dv_parts, axis=1)

            @pl.when(i > 0)
            def _():
                dk_acc[slot_p, tq - w:, :] += dk_all[:w]
                dv_acc[slot_p, tq - w:, :] += dv_all[:w]

            dk_acc[slot_c] += dk_all[w:w + tq]
            dv_acc[slot_c] += dv_all[w:w + tq]
            dk_acc[slot_n, :w, :] += dk_all[w + tq:]
            dv_acc[slot_n, :w, :] += dv_all[w + tq:]

        @pl.when(i >= 1)
        def _():
            dk_ref[...] = _rope(dk_acc[slot_p], *_rope_tabs(cs_p[...], e_ref[...]), -1.0).astype(BF16)
            dv_ref[...] = dv_acc[slot_p].astype(BF16)

    row_c = lambda width: pl.BlockSpec((None, tq, width), lambda s, i: (s, cur(s, i), 0))
    row_p = lambda width: pl.BlockSpec((None, tq, width), lambda s, i: (s, jnp.maximum(i - 1, 0), 0))
    in_specs = ([q_spec] + kv_specs + [row_c(qw), row_c(LANES), row_c(LANES), row_c(ROT_DIM), row_p(ROT_DIM),
                                       pl.BlockSpec((ROT_DIM, 3 * LANES), lambda s, i: (0, 0))])
    args = [qkv] * 7 + [do, lse, delta, cs, cs, e_mat]
    out_specs = [row_c(qw), row_p(kw), row_p(kw)]
    out_shape = [jax.ShapeDtypeStruct((nseq, seq_len, qw), BF16),
                 jax.ShapeDtypeStruct((nseq, seq_len, kw), BF16),
                 jax.ShapeDtypeStruct((nseq, seq_len, kw), BF16)]
    if sink is not None:
        in_specs = [pl.BlockSpec(memory_space=pltpu.SMEM)] + in_specs
        args = [sink] + args
        out_specs.append(pl.BlockSpec((8, LANES), lambda s, i: (0, 0)))
        out_shape.append(jax.ShapeDtypeStruct((8, LANES), F32))
    return _pcall(
        body, name=name, grid=(nseq, n + 1), in_specs=in_specs, out_specs=out_specs, out_shape=out_shape,
        scratch_shapes=[pltpu.VMEM((3, tq, kw), F32), pltpu.VMEM((3, tq, kw), F32)], args=args,
        dims=("arbitrary", "arbitrary"), comm=comm)


PAIR = 2 * HEAD_DIM


def _window_mask_t(i, tq, w, seq_len):
    tk = tq + 2 * w
    kpos = i * tq - w + lax.broadcasted_iota(jnp.int32, (tk, tq), 0)
    qpos = i * tq + lax.broadcasted_iota(jnp.int32, (tk, tq), 1)
    return (jnp.abs(qpos - kpos) <= w) & (kpos >= 0) & (kpos < seq_len)


def _place_head(x2, src_pos, dst_pos):
    hi = lax.broadcasted_iota(jnp.int32, x2.shape, 1) >= HEAD_DIM
    src = x2 if src_pos == dst_pos else pltpu.roll(x2, HEAD_DIM, 1)
    return jnp.where(hi == (dst_pos == 1), src, jnp.zeros_like(src))


def _swa_fwd_t(qkv, *, qcol, kcol, vcol, hq, hkv, w, tq, sink, name, comm=None):
    nseq, seq_len, _ = qkv.shape
    n = seq_len // tq
    rep = hq // hkv
    q_spec, kv_specs, _, _ = _swa_specs(tq, hq, hkv, n, qcol, kcol, vcol)

    def body(*refs):
        if sink is not None:
            sink_ref, refs = refs[0], refs[1:]
        q_ref, kp_ref, kc_ref, kn_ref, vp_ref, vc_ref, vn_ref, o_ref, lse_ref = refs
        i = pl.program_id(1)
        mask_t = _window_mask_t(i, tq, w, seq_len)
        o_t = [None] * (hq // 2)
        lse_rows = [None] * hq
        for a in range(hkv // 2):
            ls = slice(a * PAIR, (a + 1) * PAIR)
            kcat = jnp.concatenate([kp_ref[tq - w:, ls], kc_ref[:, ls], kn_ref[:w, ls]], axis=0) * 0.125
            vcat = jnp.concatenate([vp_ref[tq - w:, ls], vc_ref[:, ls], vn_ref[:w, ls]], axis=0)
            for e in range(2):
                g = 2 * a + e
                placed = {}
                for r in range(rep):
                    h = g * rep + r
                    qp, pos = h // 2, h % 2
                    if pos not in placed:
                        placed[pos] = (_place_head(kcat, e, pos), _place_head(vcat, e, pos))
                    k_g, v_g = placed[pos]
                    s_t = jnp.where(mask_t, _dot_nt(k_g, q_ref[:, qp * PAIR:(qp + 1) * PAIR]), NEG_INF)
                    m = jnp.max(s_t, axis=0, keepdims=True)
                    if sink is not None:
                        m = jnp.maximum(m, sink_ref[0, h])
                    p_t = jnp.exp(s_t - m)
                    den = jnp.sum(p_t, axis=0, keepdims=True)
                    if sink is not None:
                        den = den + jnp.exp(sink_ref[0, h] - m)
                    part = _dot_tn(v_g, p_t.astype(BF16)) / den
                    o_t[qp] = part if o_t[qp] is None else o_t[qp] + part
                    lse_rows[h] = m + jnp.log(den)
        o_ref[...] = jnp.concatenate(o_t, axis=0).T
        lse_ref[...] = jnp.concatenate(lse_rows, axis=0)

    in_specs = [q_spec] + kv_specs
    args = [qkv] * 7
    if sink is not None:
        in_specs = [pl.BlockSpec(memory_space=pltpu.SMEM)] + in_specs
        args = [sink] + args
    (o, lse), couts = _pcall(
        body, name=name, grid=(nseq, n), in_specs=in_specs,
        out_specs=[pl.BlockSpec((None, tq, hq * HEAD_DIM), lambda s, i: (s, i, 0)),
                   pl.BlockSpec((None, hq, tq), lambda s, i: (s, 0, i))],
        out_shape=[jax.ShapeDtypeStruct((nseq, seq_len, hq * HEAD_DIM), F32),
                   jax.ShapeDtypeStruct((nseq, hq, seq_len), F32)],
        args=args, dims=("parallel", "parallel"), comm=comm)
    return o, lse, couts


def _swa_bwd_t(qkv, do, lse, delta, cs, e_mat, *, qcol, kcol, vcol, hq, hkv, w, tq, sink, name, comm=None):
    nseq, seq_len, _ = qkv.shape
    n = seq_len // tq
    rep = hq // hkv
    qw, kw = hq * HEAD_DIM, hkv * HEAD_DIM
    tk = tq + 2 * w
    q_spec, kv_specs, cur, prv = _swa_specs(tq, hq, hkv, n, qcol, kcol, vcol)

    def body(*refs):
        if sink is not None:
            sink_ref, refs = refs[0], refs[1:]
        (q_ref, kp_ref, kc_ref, kn_ref, vp_ref, vc_ref, vn_ref, do_ref, lse_ref, dl_ref,
         cs_c, cs_p, e_ref) = refs[:13]
        outs = refs[13:]
        if sink is not None:
            dq_ref, dk_ref, dv_ref, dsink_ref, dk_acc, dv_acc = outs
        else:
            dq_ref, dk_ref, dv_ref, dk_acc, dv_acc = outs
        s_id = pl.program_id(0)
        i = pl.program_id(1)
        slot_p, slot_c, slot_n = (i + 2) % 3, i % 3, (i + 1) % 3

        if sink is not None:
            @pl.when((s_id == 0) & (i == 0))
            def _():
                dsink_ref[...] = jnp.zeros_like(dsink_ref)

        @pl.when(i < n)
        def _():
            mask_t = _window_mask_t(i, tq, w, seq_len)
            dk_acc[slot_n] = jnp.zeros((tq, kw), F32)
            dv_acc[slot_n] = jnp.zeros((tq, kw), F32)

            @pl.when(i == 0)
            def _():
                dk_acc[slot_c] = jnp.zeros((tq, kw), F32)
                dv_acc[slot_c] = jnp.zeros((tq, kw), F32)

            dq_t = [None] * (hq // 2)
            dk_pairs, dv_pairs = [], []
            for a in range(hkv // 2):
                ls = slice(a * PAIR, (a + 1) * PAIR)
                kcat = jnp.concatenate([kp_ref[tq - w:, ls], kc_ref[:, ls], kn_ref[:w, ls]], axis=0) * 0.125
                vcat = jnp.concatenate([vp_ref[tq - w:, ls], vc_ref[:, ls], vn_ref[:w, ls]], axis=0)
                dk2 = jnp.zeros((tk, PAIR), F32)
                dv2 = jnp.zeros((tk, PAIR), F32)
                for e in range(2):
                    g = 2 * a + e
                    placed = {}
                    for r in range(rep):
                        h = g * rep + r
                        qp, pos = h // 2, h % 2
                        if pos not in placed:
                            placed[pos] = (_place_head(kcat, e, pos), _place_head(vcat, e, pos))
                        k_g, v_g = placed[pos]
                        q2 = q_ref[:, qp * PAIR:(qp + 1) * PAIR]
                        do2 = do_ref[:, qp * PAIR:(qp + 1) * PAIR]
                        lse_h = lse_ref[h:h + 1, :]
                        dl_h = dl_ref[h:h + 1, :]
                        p_t = jnp.exp(jnp.where(mask_t, _dot_nt(k_g, q2), NEG_INF) - lse_h)
                        dp_t = _dot_nt(v_g, do2)
                        dsb = (p_t * (dp_t - dl_h)).astype(BF16)
                        part = _dot_tn(k_g, dsb)
                        dq_t[qp] = part if dq_t[qp] is None else dq_t[qp] + part
                        dk2 = dk2 + _dot(dsb, _place_head(q2, pos, e) * 0.125)
                        dv2 = dv2 + _dot(p_t.astype(BF16), _place_head(do2, pos, e))
                        if sink is not None:
                            ds_sink = -jnp.sum(jnp.exp(sink_ref[0, h] - lse_h) * dl_h)
                            dsink_ref[h:h + 1, :] += jnp.full((1, LANES), ds_sink, F32)
                dk_pairs.append(dk2)
                dv_pairs.append(dv2)
            dq = jnp.concatenate(dq_t, axis=0).T
            dq_ref[...] = _rope(dq, *_rope_tabs(cs_c[...], e_ref[...]), -1.0).astype(BF16)
            dk_all = dk_pairs[0] if len(dk_pairs) == 1 else jnp.concatenate(dk_pairs, axis=1)
            dv_all = dv_pairs[0] if len(dv_pairs) == 1 else jnp.concatenate(dv_pairs, axis=1)

            @pl.when(i > 0)
            def _():
                dk_acc[slot_p, tq - w:, :] += dk_all[:w]
                dv_acc[slot_p, tq - w:, :] += dv_all[:w]

            dk_acc[slot_c] += dk_all[w:w + tq]
            dv_acc[slot_c] += dv_all[w:w + tq]
            dk_acc[slot_n, :w, :] += dk_all[w + tq:]
            dv_acc[slot_n, :w, :] += dv_all[w + tq:]

        @pl.when(i >= 1)
        def _():
            dk_ref[...] = _rope(dk_acc[slot_p], *_rope_tabs(cs_p[...], e_ref[...]), -1.0).astype(BF16)
            dv_ref[...] = dv_acc[slot_p].astype(BF16)

    row_c = lambda width: pl.BlockSpec((None, tq, width), lambda s, i: (s, cur(s, i), 0))
    row_p = lambda width: pl.BlockSpec((None, tq, width), lambda s, i: (s, jnp.maximum(i - 1, 0), 0))
    stat = pl.BlockSpec((None, hq, tq), lambda s, i: (s, 0, cur(s, i)))
    in_specs = ([q_spec] + kv_specs + [row_c(qw), stat, stat, row_c(ROT_DIM), row_p(ROT_DIM),
                                       pl.BlockSpec((ROT_DIM, 3 * LANES), lambda s, i: (0, 0))])
    args = [qkv] * 7 + [do, lse, delta, cs, cs, e_mat]
    out_specs = [row_c(qw), row_p(kw), row_p(kw)]
    out_shape = [jax.ShapeDtypeStruct((nseq, seq_len, qw), BF16),
                 jax.ShapeDtypeStruct((nseq, seq_len, kw), BF16),
                 jax.ShapeDtypeStruct((nseq, seq_len, kw), BF16)]
    if sink is not None:
        in_specs = [pl.BlockSpec(memory_space=pltpu.SMEM)] + in_specs
        args = [sink] + args
        out_specs.append(pl.BlockSpec((8, LANES), lambda s, i: (0, 0)))
        out_shape.append(jax.ShapeDtypeStruct((8, LANES), F32))
    return _pcall(
        body, name=name, grid=(nseq, n + 1), in_specs=in_specs, out_specs=out_specs, out_shape=out_shape,
        scratch_shapes=[pltpu.VMEM((3, tq, kw), F32), pltpu.VMEM((3, tq, kw), F32)], args=args,
        dims=("arbitrary", "arbitrary"), comm=comm)


def _band_mask_t(row0, tq, w, seq_len):
    tk = tq + 2 * w
    kk = lax.broadcasted_iota(jnp.int32, (tk, tq), 0)
    qq = lax.broadcasted_iota(jnp.int32, (tk, tq), 1)
    kpos = row0 - w + kk
    return (jnp.abs(qq + w - kk) <= w) & (kpos >= 0) & (kpos < seq_len)


def _halo_kv_specs(t, w, hkv, n, seq_len, kcol, vcol):
    kw = hkv * HEAD_DIM
    per, last = t // w, seq_len // w - 1
    cur = lambda s, i: jnp.minimum(i, n - 1)
    specs = []
    for c in (kcol, vcol):
        specs += [pl.BlockSpec((None, w, kw), lambda s, i, c=c: (s, jnp.maximum(cur(s, i) * per - 1, 0), c)),
                  pl.BlockSpec((None, t, kw), lambda s, i, c=c: (s, cur(s, i), c)),
                  pl.BlockSpec((None, w, kw), lambda s, i, c=c: (s, jnp.minimum((cur(s, i) + 1) * per, last), c))]
    return specs, cur


def _swa_fwd_s(qkv, *, qcol, kcol, vcol, hq, hkv, w, tq, sub, sink, name, comm=None):
    nseq, seq_len, _ = qkv.shape
    t = tq * sub
    n = seq_len // t
    rep = hq // hkv
    tk = tq + 2 * w
    kv_specs, cur = _halo_kv_specs(t, w, hkv, n, seq_len, kcol, vcol)

    def body(*refs):
        if sink is not None:
            sink_ref, refs = refs[0], refs[1:]
        q_ref, kp_ref, kc_ref, kn_ref, vp_ref, vc_ref, vn_ref, o_ref, lse_ref = refs
        i = pl.program_id(1)
        kfull, vfull = [], []
        for a in range(hkv // 2):
            ls = slice(a * PAIR, (a + 1) * PAIR)
            kfull.append(jnp.concatenate([kp_ref[:, ls], kc_ref[:, ls], kn_ref[:, ls]], axis=0) * 0.125)
            vfull.append(jnp.concatenate([vp_ref[:, ls], vc_ref[:, ls], vn_ref[:, ls]], axis=0))
        for jj in range(sub):
            rows = slice(jj * tq, (jj + 1) * tq)
            mask_t = _band_mask_t(i * t + jj * tq, tq, w, seq_len)
            o_t = [None] * (hq // 2)
            lse_rows = [None] * hq
            for a in range(hkv // 2):
                kcat = kfull[a][jj * tq:jj * tq + tk]
                vcat = vfull[a][jj * tq:jj * tq + tk]
                for e in range(2):
                    g = 2 * a + e
                    placed = {}
                    for r in range(rep):
                        h = g * rep + r
                        qp, pos = h // 2, h % 2
                        if pos not in placed:
                            placed[pos] = (_place_head(kcat, e, pos), _place_head(vcat, e, pos))
                        k_g, v_g = placed[pos]
                        s_t = jnp.where(mask_t, _dot_nt(k_g, q_ref[rows, qp * PAIR:(qp + 1) * PAIR]), NEG_INF)
                        m = jnp.max(s_t, axis=0, keepdims=True)
                        if sink is not None:
                            m = jnp.maximum(m, sink_ref[0, h])
                        p_t = jnp.exp(s_t - m)
                        den = jnp.sum(p_t, axis=0, keepdims=True)
                        if sink is not None:
                            den = den + jnp.exp(sink_ref[0, h] - m)
                        part = _dot_tn(v_g, p_t.astype(BF16)) / den
                        o_t[qp] = part if o_t[qp] is None else o_t[qp] + part
                        lse_rows[h] = m + jnp.log(den)
            o_ref[rows, :] = jnp.concatenate(o_t, axis=0).T
            lse_ref[:, rows] = jnp.concatenate(lse_rows, axis=0)

    in_specs = [pl.BlockSpec((None, t, hq * HEAD_DIM), lambda s, i: (s, i, qcol))] + kv_specs
    args = [qkv] * 7
    if sink is not None:
        in_specs = [pl.BlockSpec(memory_space=pltpu.SMEM)] + in_specs
        args = [sink] + args
    (o, lse), couts = _pcall(
        body, name=name, grid=(nseq, n), in_specs=in_specs,
        out_specs=[pl.BlockSpec((None, t, hq * HEAD_DIM), lambda s, i: (s, i, 0)),
                   pl.BlockSpec((None, hq, t), lambda s, i: (s, 0, i))],
        out_shape=[jax.ShapeDtypeStruct((nseq, seq_len, hq * HEAD_DIM), F32),
                   jax.ShapeDtypeStruct((nseq, hq, seq_len), F32)],
        args=args, dims=("parallel", "parallel"), comm=comm)
    return o, lse, couts


def _swa_bwd_s(qkv, do, lse, delta, cs, e_mat, *, qcol, kcol, vcol, hq, hkv, w, tq, sub, sink, name, comm=None):
    nseq, seq_len, _ = qkv.shape
    t = tq * sub
    n = seq_len // t
    rep = hq // hkv
    qw, kw = hq * HEAD_DIM, hkv * HEAD_DIM
    tk = tq + 2 * w
    kv_specs, cur = _halo_kv_specs(t, w, hkv, n, seq_len, kcol, vcol)

    def body(*refs):
        if sink is not None:
            sink_ref, refs = refs[0], refs[1:]
        (q_ref, kp_ref, kc_ref, kn_ref, vp_ref, vc_ref, vn_ref, do_ref, lse_ref, dl_ref,
         cs_c, cs_p, e_ref) = refs[:13]
        outs = refs[13:]
        if sink is not None:
            dq_ref, dk_ref, dv_ref, dsink_ref, dk_acc, dv_acc, dk_win, dv_win = outs
        else:
            dq_ref, dk_ref, dv_ref, dk_acc, dv_acc, dk_win, dv_win = outs
        s_id = pl.program_id(0)
        i = pl.program_id(1)
        slot_p, slot_c, slot_n = (i + 2) % 3, i % 3, (i + 1) % 3

        if sink is not None:
            @pl.when((s_id == 0) & (i == 0))
            def _():
                dsink_ref[...] = jnp.zeros_like(dsink_ref)

        @pl.when(i < n)
        def _():
            dk_win[...] = jnp.zeros_like(dk_win)
            dv_win[...] = jnp.zeros_like(dv_win)
            kfull, vfull = [], []
            for a in range(hkv // 2):
                ls = slice(a * PAIR, (a + 1) * PAIR)
                kfull.append(jnp.concatenate([kp_ref[:, ls], kc_ref[:, ls], kn_ref[:, ls]], axis=0) * 0.125)
                vfull.append(jnp.concatenate([vp_ref[:, ls], vc_ref[:, ls], vn_ref[:, ls]], axis=0))
            for jj in range(sub):
                rows = slice(jj * tq, (jj + 1) * tq)
                krows = slice(jj * tq, jj * tq + tk)
                mask_t = _band_mask_t(i * t + jj * tq, tq, w, seq_len)
                dq_t = [None] * (hq // 2)
                for a in range(hkv // 2):
                    ls = slice(a * PAIR, (a + 1) * PAIR)
                    kcat, vcat = kfull[a][krows], vfull[a][krows]
                    dk2 = jnp.zeros((tk, PAIR), F32)
                    dv2 = jnp.zeros((tk, PAIR), F32)
                    for e in range(2):
                        g = 2 * a + e
                        placed = {}
                        for r in range(rep):
                            h = g * rep + r
                            qp, pos = h // 2, h % 2
                            if pos not in placed:
                                placed[pos] = (_place_head(kcat, e, pos), _place_head(vcat, e, pos))
                            k_g, v_g = placed[pos]
                            q2 = q_ref[rows, qp * PAIR:(qp + 1) * PAIR]
                            do2 = do_ref[rows, qp * PAIR:(qp + 1) * PAIR]
                            lse_h = lse_ref[h:h + 1, rows]
                            dl_h = dl_ref[h:h + 1, rows]
                            p_t = jnp.exp(jnp.where(mask_t, _dot_nt(k_g, q2), NEG_INF) - lse_h)
                            dp_t = _dot_nt(v_g, do2)
                            dsb = (p_t * (dp_t - dl_h)).astype(BF16)
                            part = _dot_tn(k_g, dsb)
                            dq_t[qp] = part if dq_t[qp] is None else dq_t[qp] + part
                            dk2 = dk2 + _dot(dsb, _place_head(q2, pos, e) * 0.125)
                            dv2 = dv2 + _dot(p_t.astype(BF16), _place_head(do2, pos, e))
                            if sink is not None:
                                ds_sink = -jnp.sum(jnp.exp(sink_ref[0, h] - lse_h) * dl_h)
                                dsink_ref[h:h + 1, :] += jnp.full((1, LANES), ds_sink, F32)
                    dk_win[krows, ls] += dk2
                    dv_win[krows, ls] += dv2
                dq = jnp.concatenate(dq_t, axis=0).T
                dq_ref[rows, :] = _rope(dq, *_rope_tabs(cs_c[rows, :], e_ref[...]), -1.0).astype(BF16)

            @pl.when(i > 0)
            def _():
                dk_acc[slot_p, t - w:, :] += dk_win[:w, :]
                dv_acc[slot_p, t - w:, :] += dv_win[:w, :]

            @pl.when(i == 0)
            def _():
                dk_acc[slot_c] = dk_win[w:w + t, :]
                dv_acc[slot_c] = dv_win[w:w + t, :]

            @pl.when(i > 0)
            def _():
                dk_acc[slot_c] += dk_win[w:w + t, :]
                dv_acc[slot_c] += dv_win[w:w + t, :]

            dk_acc[slot_n] = jnp.zeros((t, kw), F32)
            dv_acc[slot_n] = jnp.zeros((t, kw), F32)
            dk_acc[slot_n, :w, :] = dk_win[w + t:, :]
            dv_acc[slot_n, :w, :] = dv_win[w + t:, :]

        @pl.when(i >= 1)
        def _():
            dk_ref[...] = _rope(dk_acc[slot_p], *_rope_tabs(cs_p[...], e_ref[...]), -1.0).astype(BF16)
            dv_ref[...] = dv_acc[slot_p].astype(BF16)

    row_c = lambda width: pl.BlockSpec((None, t, width), lambda s, i: (s, cur(s, i), 0))
    row_p = lambda width: pl.BlockSpec((None, t, width), lambda s, i: (s, jnp.maximum(i - 1, 0), 0))
    stat = pl.BlockSpec((None, hq, t), lambda s, i: (s, 0, cur(s, i)))
    in_specs = ([pl.BlockSpec((None, t, qw), lambda s, i: (s, cur(s, i), qcol))] + kv_specs
                + [row_c(qw), stat, stat, row_c(ROT_DIM), row_p(ROT_DIM),
                   pl.BlockSpec((ROT_DIM, 3 * LANES), lambda s, i: (0, 0))])
    args = [qkv] * 7 + [do, lse, delta, cs, cs, e_mat]
    out_specs = [row_c(qw), row_p(kw), row_p(kw)]
    out_shape = [jax.ShapeDtypeStruct((nseq, seq_len, qw), BF16),
                 jax.ShapeDtypeStruct((nseq, seq_len, kw), BF16),
                 jax.ShapeDtypeStruct((nseq, seq_len, kw), BF16)]
    if sink is not None:
        in_specs = [pl.BlockSpec(memory_space=pltpu.SMEM)] + in_specs
        args = [sink] + args
        out_specs.append(pl.BlockSpec((8, LANES), lambda s, i: (0, 0)))
        out_shape.append(jax.ShapeDtypeStruct((8, LANES), F32))
    return _pcall(
        body, name=name, grid=(nseq, n + 1), in_specs=in_specs, out_specs=out_specs, out_shape=out_shape,
        scratch_shapes=[pltpu.VMEM((3, t, kw), F32), pltpu.VMEM((3, t, kw), F32),
                        pltpu.VMEM((t + 2 * w, kw), F32), pltpu.VMEM((t + 2 * w, kw), F32)], args=args,
        dims=("arbitrary", "arbitrary"), comm=comm)


def _rms_parts(o, g):
    ms = jnp.mean(o * o, axis=-1, keepdims=True) + LN_EPS
    rinv = lax.rsqrt(ms)
    return o * rinv * g, rinv


def _from_subsequences(ref, scr, dil, t):
    slabs = ref.shape[-1] // LANES
    if dil == 1:
        return ref[0].astype(F32)
    for c in range(dil):
        for sl in range(slabs):
            scr[sl, pl.ds(c, t // dil, stride=dil), :] = ref[c, :, sl * LANES:(sl + 1) * LANES].astype(F32)
    return jnp.concatenate([scr[sl] for sl in range(slabs)], axis=1)


def _to_subsequences(val, ref, scr, dil, t):
    slabs = val.shape[-1] // LANES
    if dil == 1:
        ref[0] = val.astype(ref.dtype)
        return
    for sl in range(slabs):
        scr[sl] = val[:, sl * LANES:(sl + 1) * LANES]
    for c in range(dil):
        for sl in range(slabs):
            ref[c, :, sl * LANES:(sl + 1) * LANES] = scr[sl, pl.ds(c, t // dil, stride=dil), :].astype(ref.dtype)


def _combine_fwd(out_a, o_g, lse_g, g_win, g_dil, *, t):
    s = out_a.shape[1]
    wd = DIL_SLOTS * HEAD_DIM

    def body(oa_ref, o0, o1, o2, l0, l1, l2, gw_ref, gd_ref, mixed_ref, ob_ref, lt_ref, scr):
        ls = [l0[...], l1[...], l2[...]]
        mx = jnp.maximum(jnp.maximum(ls[0], ls[1]), ls[2])
        ws = [jnp.exp(l - mx) for l in ls]
        tot = ws[0] + ws[1] + ws[2]
        lt_ref[...] = mx + jnp.log(tot)
        ws = [x / tot for x in ws]
        og = [_from_subsequences(o_ref, scr.at[gi], dil, t)
              for gi, (o_ref, dil) in enumerate(zip((o0, o1, o2), DILATIONS))]
        parts = []
        for h in range(DIL_SLOTS):
            hs = slice(h * HEAD_DIM, (h + 1) * HEAD_DIM)
            parts.append(ws[0][:, h:h + 1] * og[0][:, hs] + ws[1][:, h:h + 1] * og[1][:, hs]
                         + ws[2][:, h:h + 1] * og[2][:, hs])
        ob = jnp.concatenate(parts, axis=1)
        ob_ref[...] = ob
        na, _ = _rms_parts(oa_ref[...], gw_ref[...])
        nb, _ = _rms_parts(ob, gd_ref[...])
        mixed_ref[:, :wd] = na.astype(BF16)
        mixed_ref[:, wd:] = nb.astype(BF16)

    half = pl.BlockSpec((t, wd), lambda i: (i, 0))
    lanes = pl.BlockSpec((t, LANES), lambda i: (i, 0))
    grow = pl.BlockSpec((1, wd), lambda i: (0, 0))
    subseq = [pl.BlockSpec((dil, t // dil, wd), lambda i: (0, i, 0)) for dil in DILATIONS]
    return pl.pallas_call(
        body, name="combine_fwd", grid=(s // t,),
        in_specs=[pl.BlockSpec((None, t, wd), lambda i: (0, i, 0))] + subseq + [lanes, lanes, lanes, grow, grow],
        out_specs=[pl.BlockSpec((t, 2 * wd), lambda i: (i, 0)), half, lanes],
        out_shape=[jax.ShapeDtypeStruct((s, 2 * wd), BF16), jax.ShapeDtypeStruct((s, wd), F32),
                   jax.ShapeDtypeStruct((s, LANES), F32)],
        scratch_shapes=[pltpu.VMEM((len(DILATIONS), wd // LANES, t, LANES), F32)],
        compiler_params=_cparams(dimension_semantics=("parallel",)),
    )(out_a, *o_g, *lse_g, g_win, g_dil)


def _combine_bwd(dmixed, out_a, out_b, g_win, g_dil, *, t):
    s = out_b.shape[0]
    wd = DIL_SLOTS * HEAD_DIM

    def body(dm_ref, oa_ref, ob_ref, gw_ref, gd_ref, doa_ref, dob0, dob1, dob2, dla_ref, dlb_ref, st_ref, scr):
        i = pl.program_id(0)

        @pl.when(i == 0)
        def _():
            st_ref[...] = jnp.zeros_like(st_ref)

        lane = lax.broadcasted_iota(jnp.int32, (t, LANES), 1)
        for idx, (o_ref, g_ref, dl_ref) in enumerate(((oa_ref, gw_ref, dla_ref), (ob_ref, gd_ref, dlb_ref))):
            o = o_ref[...]
            dn = dm_ref[:, idx * wd:(idx + 1) * wd]
            _, rinv = _rms_parts(o, g_ref[...])
            wv = dn * g_ref[...]
            do = rinv * wv - o * (rinv * rinv * rinv) * jnp.mean(wv * o, axis=-1, keepdims=True)
            st_ref[idx:idx + 1, :] += jnp.sum(dn * o * rinv, axis=0, keepdims=True)
            if idx == 0:
                doa_ref[...] = do.astype(BF16)
            else:
                for do_ref, dil in zip((dob0, dob1, dob2), DILATIONS):
                    _to_subsequences(do, do_ref, scr, dil, t)
            prod = do * o
            acc = jnp.zeros((t, LANES), F32)
            for h in range(DIL_SLOTS):
                hs = slice(h * HEAD_DIM, (h + 1) * HEAD_DIM)
                acc = jnp.where(lane == h, jnp.sum(prod[:, hs], axis=1, keepdims=True), acc)
            dl_ref[...] = acc

    half = pl.BlockSpec((t, wd), lambda i: (i, 0))
    lanes = pl.BlockSpec((t, LANES), lambda i: (i, 0))
    grow = pl.BlockSpec((1, wd), lambda i: (0, 0))
    a_spec = pl.BlockSpec((None, t, wd), lambda i: (0, i, 0))
    subseq = [pl.BlockSpec((dil, t // dil, wd), lambda i: (0, i, 0)) for dil in DILATIONS]
    doa, dob0, dob1, dob2, dla, dlb, st = pl.pallas_call(
        body, name="combine_bwd", grid=(s // t,),
        in_specs=[pl.BlockSpec((t, 2 * wd), lambda i: (i, 0)), a_spec, half, grow, grow],
        out_specs=[a_spec] + subseq + [lanes, lanes, pl.BlockSpec((8, wd), lambda i: (0, 0))],
        out_shape=[jax.ShapeDtypeStruct((1, s, wd), BF16)]
        + [jax.ShapeDtypeStruct((dil, s // dil, wd), BF16) for dil in DILATIONS]
        + [jax.ShapeDtypeStruct((s, LANES), F32), jax.ShapeDtypeStruct((s, LANES), F32),
           jax.ShapeDtypeStruct((8, wd), F32)],
        scratch_shapes=[pltpu.VMEM((wd // LANES, t, LANES), F32)],
        compiler_params=_cparams(dimension_semantics=("arbitrary",)),
    )(dmixed, out_a, out_b, g_win, g_dil)
    return doa, [dob0, dob1, dob2], dla, dlb, st


def _assemble_dz(dqa, dka, dva, dqs, dks, dvs, *, t):
    s = dqa.shape[1]
    wd = DIL_SLOTS * HEAD_DIM

    def body(*refs):
        a_refs, g_refs, o_ref, scr = refs[:3], refs[3:12], refs[12], refs[13]
        col = 0
        for r in a_refs:
            o_ref[:, col:col + r.shape[-1]] = r[...]
            col += r.shape[-1]
        for part in range(3):
            for gi, dil in enumerate(DILATIONS):
                val = _from_subsequences(g_refs[3 * part + gi], scr, dil, t)
                o_ref[:, col:col + wd] = val.astype(BF16)
                col += wd

    a_specs = [pl.BlockSpec((None, t, a.shape[-1]), lambda i: (0, i, 0)) for a in (dqa, dka, dva)]
    g_specs = [pl.BlockSpec((dil, t // dil, wd), lambda i: (0, i, 0)) for _ in range(3) for dil in DILATIONS]
    return pl.pallas_call(
        body, name="assemble_dz", grid=(s // t,), in_specs=a_specs + g_specs,
        out_specs=pl.BlockSpec((t, IN_WIDTH), lambda i: (i, 0)),
        out_shape=jax.ShapeDtypeStruct((s, IN_WIDTH), BF16),
        scratch_shapes=[pltpu.VMEM((wd // LANES, t, LANES), F32)],
        compiler_params=_cparams(dimension_semantics=("parallel",)),
    )(dqa, dka, dva, *dqs, *dks, *dvs)


def _mixproj_fwd(mixed_b, w_mix_b, x, ln_in_g, ln_in_b, ln1_g, ln1_b, *, t):
    s = x.shape[0]

    def body(m_ref, w_ref, x_ref, g0, b0, g1, b1, r1_ref, h1_ref):
        h0 = _ln(x_ref[...], g0[...], b0[...])
        r1 = ALPHA * h0 + _dot(m_ref[...], w_ref[...])
        r1_ref[...] = r1
        h1_ref[...] = _ln(r1, g1[...], b1[...]).astype(BF16)

    tile = pl.BlockSpec((t, D_MODEL), lambda i: (i, 0))
    row = pl.BlockSpec((1, D_MODEL), lambda i: (0, 0))
    return pl.pallas_call(
        body, name="mixproj_fwd", grid=(s // t,),
        in_specs=[tile, pl.BlockSpec((D_MODEL, D_MODEL), lambda i: (0, 0)), tile, row, row, row, row],
        out_specs=[tile, tile],
        out_shape=[jax.ShapeDtypeStruct((s, D_MODEL), F32), jax.ShapeDtypeStruct((s, D_MODEL), BF16)],
        compiler_params=_cparams(dimension_semantics=("parallel",)),
    )(mixed_b, w_mix_b, x, ln_in_g, ln_in_b, ln1_g, ln1_b)


def _mem_fwd(mem, g, b, wk_b, wv_b):
    ml = mem.shape[0]

    def body(mem_ref, g_ref, b_ref, wk_ref, wv_ref, mn_ref, kx_ref, vx_ref):
        mn = _ln(mem_ref[...], g_ref[...], b_ref[...]).astype(BF16)
        mn_ref[...] = mn
        kx_ref[...] = _dot(mn, wk_ref[...]).astype(BF16)
        vx_ref[...] = _dot(mn, wv_ref[...]).astype(BF16)

    sh = jax.ShapeDtypeStruct((ml, D_MODEL), BF16)
    return pl.pallas_call(body, name="mem_fwd", out_shape=[sh, sh, sh], compiler_params=_cparams())(
        mem, g, b, wk_b, wv_b)


def _mem_bwd(dkx, dvx, mem, g, b, wk_b, wv_b):
    def body(dk_ref, dv_ref, mem_ref, g_ref, b_ref, wk_ref, wv_ref, dwk_ref, dwv_ref, st_ref):
        mem_v = mem_ref[...]
        mn = _ln(mem_v, g_ref[...], b_ref[...]).astype(BF16)
        dkb = dk_ref[...].astype(BF16)
        dvb = dv_ref[...].astype(BF16)
        dwk_ref[...] = _dot_tn(mn, dkb)
        dwv_ref[...] = _dot_tn(mn, dvb)
        dmn = _dot_nt(dkb, wk_ref[...]) + _dot_nt(dvb, wv_ref[...])
        _, dg, db = _ln_bwd_math(dmn, mem_v, g_ref[...])
        st_ref[...] = jnp.zeros_like(st_ref)
        st_ref[0:1, :] = dg
        st_ref[1:2, :] = db

    sw = jax.ShapeDtypeStruct((D_MODEL, D_MODEL), F32)
    return pl.pallas_call(body, name="mem_bwd", out_shape=[sw, sw, jax.ShapeDtypeStruct((8, D_MODEL), F32)],
                          compiler_params=_cparams())(dkx, dvx, mem, g, b, wk_b, wv_b)


def _xattn_fwd(h1b, r1, kx, vx, wq_b, wo_b, ln1_g, ln1_b, ln2_g, ln2_b, *, t):
    s = h1b.shape[0]
    scale = X_HEAD_DIM ** -0.5

    def body(h_ref, r1_ref, kx_ref, vx_ref, wq_ref, wo_ref, g1, b1, g2, b2, r2_ref, h2_ref, qx_ref, ox_ref, lse_ref):
        qxb = _dot(h_ref[...], wq_ref[...]).astype(BF16)
        qx_ref[...] = qxb
        lane = lax.broadcasted_iota(jnp.int32, (t, LANES), 1)
        lse_acc = jnp.zeros((t, LANES), F32)
        parts = []
        for h in range(X_HEADS):
            hs = slice(h * X_HEAD_DIM, (h + 1) * X_HEAD_DIM)
            sc = _dot_nt(qxb[:, hs] * scale, kx_ref[:, hs])
            m = jnp.max(sc, axis=1, keepdims=True)
            p = jnp.exp(sc - m)
            den = jnp.sum(p, axis=1, keepdims=True)
            parts.append(_dot(p.astype(BF16), vx_ref[:, hs]) / den)
            lse_acc = jnp.where(lane == h, m + jnp.log(den), lse_acc)
        lse_ref[...] = lse_acc
        oxb = jnp.concatenate(parts, axis=1).astype(BF16)
        ox_ref[...] = oxb
        h1 = _ln(r1_ref[...], g1[...], b1[...])
        r2 = ALPHA * h1 + _dot(oxb, wo_ref[...])
        r2_ref[...] = r2
        h2_ref[...] = _ln(r2, g2[...], b2[...]).astype(BF16)

    tile = pl.BlockSpec((t, D_MODEL), lambda i: (i, 0))
    row = pl.BlockSpec((1, D_MODEL), lambda i: (0, 0))
    full = lambda r: pl.BlockSpec((r, D_MODEL), lambda i: (0, 0))
    ml = kx.shape[0]
    bsh = jax.ShapeDtypeStruct((s, D_MODEL), BF16)
    return pl.pallas_call(
        body, name="xattn_fwd", grid=(s // t,),
        in_specs=[tile, tile, full(ml), full(ml), full(D_MODEL), full(D_MODEL), row, row, row, row],
        out_specs=[tile, tile, tile, tile, pl.BlockSpec((t, LANES), lambda i: (i, 0))],
        out_shape=[jax.ShapeDtypeStruct((s, D_MODEL), F32), bsh, bsh, bsh, jax.ShapeDtypeStruct((s, LANES), F32)],
        compiler_params=_cparams(dimension_semantics=("parallel",)),
    )(h1b, r1, kx, vx, wq_b, wo_b, ln1_g, ln1_b, ln2_g, ln2_b)


def _xattn_bwd(dr2, qxb, oxb, lse, kx, vx, wq_b, wo_b, *, t, comm=None):
    s = dr2.shape[0]
    ml = kx.shape[0]
    scale = X_HEAD_DIM ** -0.5

    def body(dr2_ref, qx_ref, ox_ref, lse_ref, kx_ref, vx_ref, wq_ref, wo_ref, dh1_ref, dqx_ref, dkx_ref, dvx_ref):
        i = pl.program_id(0)

        @pl.when(i == 0)
        def _():
            dkx_ref[...] = jnp.zeros_like(dkx_ref)
            dvx_ref[...] = jnp.zeros_like(dvx_ref)

        dr2v = dr2_ref[...]
        dox = _dot_nt(dr2v.astype(BF16), wo_ref[...])
        parts = []
        for h in range(X_HEADS):
            hs = slice(h * X_HEAD_DIM, (h + 1) * X_HEAD_DIM)
            doh = dox[:, hs]
            dohb = doh.astype(BF16)
            dl = jnp.sum(doh * ox_ref[:, hs].astype(F32), axis=1, keepdims=True)
            qh = qx_ref[:, hs] * scale
            p = jnp.exp(_dot_nt(qh, kx_ref[:, hs]) - lse_ref[:, h:h + 1])
            dp = _dot_nt(dohb, vx_ref[:, hs])
            dsb = (p * (dp - dl)).astype(BF16)
            parts.append(_dot(dsb, kx_ref[:, hs]) * scale)
            dkx_ref[:, hs] += _dot_tn(dsb, qh)
            dvx_ref[:, hs] += _dot_tn(p.astype(BF16), dohb)
        dqxb = jnp.concatenate(parts, axis=1).astype(BF16)
        dqx_ref[...] = dqxb
        dh1_ref[...] = _dot_nt(dqxb, wq_ref[...]) + ALPHA * dr2v

    tile = pl.BlockSpec((t, D_MODEL), lambda i: (i, 0))
    full = lambda r: pl.BlockSpec((r, D_MODEL), lambda i: (0, 0))
    return _pcall(
        body, name="xattn_bwd", grid=(s // t,),
        in_specs=[tile, tile, tile, pl.BlockSpec((t, LANES), lambda i: (i, 0)), full(ml), full(ml),
                  full(D_MODEL), full(D_MODEL)],
        out_specs=[tile, tile, full(ml), full(ml)],
        out_shape=[jax.ShapeDtypeStruct((s, D_MODEL), F32), jax.ShapeDtypeStruct((s, D_MODEL), BF16),
                   jax.ShapeDtypeStruct((ml, D_MODEL), F32), jax.ShapeDtypeStruct((ml, D_MODEL), F32)],
        args=[dr2, qxb, oxb, lse, kx, vx, wq_b, wo_b], dims=("arbitrary",), comm=comm)


def _halo_specs(t, s, width):
    tb8 = t // 8
    return [pl.BlockSpec((t, width), lambda i: (i, 0)),
            pl.BlockSpec((8, width), lambda i: (jnp.maximum(i * tb8 - 1, 0), 0)),
            pl.BlockSpec((8, width), lambda i: (jnp.minimum((i + 1) * tb8, s // 8 - 1), 0))]


def _halo_rows(i, n, prev_ref, next_ref):
    prev_row = jnp.where(i > 0, prev_ref[7:8, :], 0.0)
    next_row = jnp.where(i < n - 1, next_ref[0:1, :], 0.0)
    return prev_row, next_row


def _gelu_parts(gc):
    cdf = 0.5 * (1.0 + lax.erf(gc * (2.0 ** -0.5)))
    pdf = jnp.exp(-0.5 * gc * gc) * (1.0 / math.sqrt(2.0 * math.pi))
    return gc * cdf, cdf + gc * pdf


def _conv_fwd(g, u, conv_w, conv_b, *, t):
    s = g.shape[0]
    n = s // t

    def body(g_ref, gp_ref, gn_ref, u_ref, cw_ref, cb_ref, o_ref):
        i = pl.program_id(0)
        gv = g_ref[...]
        prev_row, next_row = _halo_rows(i, n, gp_ref, gn_ref)
        gm1, gp1 = _shift_rows(gv, prev_row, next_row)
        gc = gm1 * cw_ref[0:1, :] + gv * cw_ref[1:2, :] + gp1 * cw_ref[2:3, :] + cb_ref[...]
        act, _ = _gelu_parts(gc)
        o_ref[...] = (act * u_ref[...]).astype(BF16)

    tile = pl.BlockSpec((t, D_FF), lambda i: (i, 0))
    return pl.pallas_call(
        body, name="conv_fwd", grid=(n,),
        in_specs=_halo_specs(t, s, D_FF) + [tile, pl.BlockSpec((3, D_FF), lambda i: (0, 0)),
                                            pl.BlockSpec((1, D_FF), lambda i: (0, 0))],
        out_specs=tile, out_shape=jax.ShapeDtypeStruct((s, D_FF), BF16),
        compiler_params=_cparams(dimension_semantics=("parallel",)),
    )(g, g, g, u, conv_w, conv_b)


def _down_ln3(tb, w_down_b, r2, target, ln2_g, ln2_b, ln3_g, ln3_b, *, t):
    s = r2.shape[0]

    def body(t_ref, w_ref, r2_ref, tg_ref, g2, b2, g3, b3, dr_ref, drb_ref, st_ref):
        i = pl.program_id(0)

        @pl.when(i == 0)
        def _():
            st_ref[...] = jnp.zeros_like(st_ref)

        h2 = _ln(r2_ref[...], g2[...], b2[...])
        r3 = ALPHA * h2 + _dot(t_ref[...], w_ref[...])
        y = _ln(r3, g3[...], b3[...])
        err = y - tg_ref[...]
        loss = 0.5 * jnp.sum(jnp.mean(err * err, axis=-1, keepdims=True))
        dy = err * (1.0 / D_MODEL)
        dr, dg, db = _ln_bwd_math(dy, r3, g3[...])
        dr_ref[...] = dr
        drb_ref[...] = dr.astype(BF16)
        st_ref[0:1, :] += dg
        st_ref[1:2, :] += db
        st_ref[2:3, :] += jnp.full((1, D_MODEL), loss, F32)

    tile = pl.BlockSpec((t, D_MODEL), lambda i: (i, 0))
    row = pl.BlockSpec((1, D_MODEL), lambda i: (0, 0))
    return pl.pallas_call(
        body, name="down_ln3", grid=(s // t,),
        in_specs=[pl.BlockSpec((t, D_FF), lambda i: (i, 0)), pl.BlockSpec((D_FF, D_MODEL), lambda i: (0, 0)),
                  tile, tile, row, row, row, row],
        out_specs=[tile, tile, pl.BlockSpec((8, D_MODEL), lambda i: (0, 0))],
        out_shape=[jax.ShapeDtypeStruct((s, D_MODEL), F32), jax.ShapeDtypeStruct((s, D_MODEL), BF16),
                   jax.ShapeDtypeStruct((8, D_MODEL), F32)],
        compiler_params=_cparams(dimension_semantics=("arbitrary",)),
    )(tb, w_down_b, r2, target, ln2_g, ln2_b, ln3_g, ln3_b)


def _conv_bwd_a(dr3b, w_down_b, g, u, conv_w, conv_b, *, t):
    s = g.shape[0]
    n = s // t

    def body(d_ref, w_ref, g_ref, gp_ref, gn_ref, u_ref, cw_ref, cb_ref, du_ref, dgc_ref, st_ref):
        i = pl.program_id(0)

        @pl.when(i == 0)
        def _():
            st_ref[...] = jnp.zeros_like(st_ref)

        dt = _dot_nt(d_ref[...], w_ref[...])
        gv = g_ref[...]
        prev_row, next_row = _halo_rows(i, n, gp_ref, gn_ref)
        gm1, gp1 = _shift_rows(gv, prev_row, next_row)
        gc = gm1 * cw_ref[0:1, :] + gv * cw_ref[1:2, :] + gp1 * cw_ref[2:3, :] + cb_ref[...]
        act, dact = _gelu_parts(gc)
        du_ref[...] = (dt * act).astype(BF16)
        dgc = dt * u_ref[...] * dact
        dgc_ref[...] = dgc
        st_ref[0:1, :] += jnp.sum(gm1 * dgc, axis=0, keepdims=True)
        st_ref[1:2, :] += jnp.sum(gv * dgc, axis=0, keepdims=True)
        st_ref[2:3, :] += jnp.sum(gp1 * dgc, axis=0, keepdims=True)
        st_ref[3:4, :] += jnp.sum(dgc, axis=0, keepdims=True)

    tile = pl.BlockSpec((t, D_FF), lambda i: (i, 0))
    return pl.pallas_call(
        body, name="conv_bwd_a", grid=(n,),
        in_specs=[pl.BlockSpec((t, D_MODEL), lambda i: (i, 0)), pl.BlockSpec((D_FF, D_MODEL), lambda i: (0, 0))]
        + _halo_specs(t, s, D_FF) + [tile, pl.BlockSpec((3, D_FF), lambda i: (0, 0)),
                                     pl.BlockSpec((1, D_FF), lambda i: (0, 0))],
        out_specs=[tile, tile, pl.BlockSpec((8, D_FF), lambda i: (0, 0))],
        out_shape=[jax.ShapeDtypeStruct((s, D_FF), BF16), jax.ShapeDtypeStruct((s, D_FF), F32),
                   jax.ShapeDtypeStruct((8, D_FF), F32)],
        compiler_params=_cparams(dimension_semantics=("arbitrary",)),
    )(dr3b, w_down_b, g, g, g, u, conv_w, conv_b)


def _conv_bwd_b(dgc, conv_w, *, t):
    s = dgc.shape[0]
    n = s // t

    def body(d_ref, dp_ref, dn_ref, cw_ref, o_ref):
        i = pl.program_id(0)
        dv = d_ref[...]
        prev_row, next_row = _halo_rows(i, n, dp_ref, dn_ref)
        dm1, dp1 = _shift_rows(dv, prev_row, next_row)
        o_ref[...] = (dp1 * cw_ref[0:1, :] + dv * cw_ref[1:2, :] + dm1 * cw_ref[2:3, :]).astype(BF16)

    return pl.pallas_call(
        body, name="conv_bwd_b", grid=(n,),
        in_specs=_halo_specs(t, s, D_FF) + [pl.BlockSpec((3, D_FF), lambda i: (0, 0))],
        out_specs=pl.BlockSpec((t, D_FF), lambda i: (i, 0)), out_shape=jax.ShapeDtypeStruct((s, D_FF), BF16),
        compiler_params=_cparams(dimension_semantics=("parallel",)),
    )(dgc, dgc, dgc, conv_w)


def _to_residue(a, dil):
    s, w = a.shape
    return a.reshape(s // dil, dil, w).transpose(1, 0, 2)


def _from_residue(a):
    dil, l, w = a.shape
    return a.transpose(1, 0, 2).reshape(dil * l, w)


def _stats_to_lanes(rows):
    dil, hq, l = rows.shape
    return jnp.pad(rows.transpose(2, 0, 1).reshape(dil * l, hq), ((0, 0), (0, LANES - hq)))


def _stats_to_rows(lanes, dil):
    s = lanes.shape[0]
    return lanes[:, :DIL_SLOTS].reshape(s // dil, dil, DIL_SLOTS).transpose(1, 2, 0)


def _rope_angles(positions):
    inv_freq = ROPE_THETA ** (-jnp.arange(0, ROT_DIM, 2, dtype=F32) / ROT_DIM)
    ang = positions.astype(F32)[:, None] * inv_freq
    return jnp.concatenate([jnp.cos(ang), jnp.sin(ang)], axis=1)


class _NoPlan:
    def gather(self, stage):
        return None

    def gathered(self, stage, couts, wb):
        pass

    def exchange(self, stage, grads):
        return None

    def exchanged(self, stage, couts):
        pass


def _local_step(x, mem, positions, target, wb, sp, plan=None, *, t_row=256, t_mm=512, tq_a=256, tq_b=128,
                sub_a=2, sub_b=4):
    s = x.shape[0]
    plan = plan or _NoPlan()
    cs = _rope_angles(positions)
    e_mat = _rope_select_matrix()

    (h0b, za, *zb), couts = _proj_all(x, sp["ln_in_g"], sp["ln_in_b"], wb["w_in_seg"], cs, e_mat, t=t_mm,
                                      comm=plan.gather("proj"))
    plan.gathered("proj", couts, wb)
    sub_a = max(1, min(sub_a, s // tq_a))
    subs_b = [max(1, min(sub_b, s // dil // tq_b)) for dil in DILATIONS]
    out_a, lse_a, couts = _swa_fwd_s(za, qcol=0, kcol=4, vcol=5, hq=WIN_Q_HEADS, hkv=WIN_KV_HEADS, w=WIN_HALF,
                                     tq=tq_a, sub=sub_a, sink=sp["attn_sink"], name="attn_a_fwd",
                                     comm=plan.gather("attn_a"))
    plan.gathered("attn_a", couts, wb)
    o_g, lse_g = [], []
    for gi in range(3):
        o, l, couts = _swa_fwd_s(zb[gi], qcol=0, kcol=1, vcol=2, hq=DIL_SLOTS, hkv=DIL_SLOTS, w=DIL_HALF, tq=tq_b,
                                 sub=subs_b[gi], sink=None, name=f"attn_b{gi}_fwd",
                                 comm=plan.gather(f"attn_b{gi}"))
        plan.gathered(f"attn_b{gi}", couts, wb)
        o_g.append(o)
        lse_g.append(_stats_to_lanes(l))
    mixed_b, out_b, lse_b = _combine_fwd(out_a, o_g, lse_g, sp["g_win"], sp["g_dil"], t=t_row)
    r1, h1b = _mixproj_fwd(mixed_b, wb["w_mix_out"], x, sp["ln_in_g"], sp["ln_in_b"], sp["ln1_g"], sp["ln1_b"],
                           t=t_row)
    mem_nb, kx, vx = _mem_fwd(mem, sp["mem_ln_g"], sp["mem_ln_b"], wb["w_xk"], wb["w_xv"])
    r2, h2b, qxb, oxb, lse_x = _xattn_fwd(h1b, r1, kx, vx, wb["w_xq"], wb["w_xo"], sp["ln1_g"], sp["ln1_b"],
                                          sp["ln2_g"], sp["ln2_b"], t=t_row)
    g = _mm(h2b, wb["w_gate"], mode="nn", out_dtype=F32, tm=t_mm, tn=D_FF, name="ff_gate")
    u = _mm(h2b, wb["w_up"], mode="nn", out_dtype=F32, tm=t_mm, tn=D_FF, name="ff_up")
    tb = _conv_fwd(g, u, sp["conv_w"], sp["conv_b"], t=t_row)
    dr3, dr3b, st3 = _down_ln3(tb, wb["w_down"], r2, target, sp["ln2_g"], sp["ln2_b"], sp["ln3_g"], sp["ln3_b"],
                               t=t_row)

    grads = {}
    du, dgc, st_conv = _conv_bwd_a(dr3b, wb["w_down"], g, u, sp["conv_w"], sp["conv_b"], t=t_row)
    dg = _conv_bwd_b(dgc, sp["conv_w"], t=t_row)
    tk = min(1024, s)
    grads["w_down"] = _mm(tb, dr3b, mode="tn", out_dtype=BF16, tm=D_FF // 2, tn=D_MODEL, tk=tk, name="dw_down")
    grads["w_gate"] = _mm(h2b, dg, mode="tn", out_dtype=BF16, tm=D_MODEL, tn=D_FF // 2, tk=tk, name="dw_gate")
    grads["w_up"] = _mm(h2b, du, mode="tn", out_dtype=BF16, tm=D_MODEL, tn=D_FF // 2, tk=tk, name="dw_up")
    dh2, couts = _mm2_nt(dg, wb["w_gate"], du, wb["w_up"], dr3, add_scale=ALPHA, tm=t_mm, name="dh2",
                         comm=plan.exchange("dh2", grads))
    plan.exchanged("dh2", couts)

    dr2, dr2b, st2 = _ln_bwd(dh2, r2, sp["ln2_g"], t=t_row, name="ln2_bwd", want_bf16=True)
    (dh1, dqxb, dkx, dvx), couts = _xattn_bwd(dr2, qxb, oxb, lse_x, kx, vx, wb["w_xq"], wb["w_xo"], t=t_row,
                                              comm=plan.exchange("xattn", grads))
    plan.exchanged("xattn", couts)
    grads["w_xo"] = _mm(oxb, dr2b, mode="tn", out_dtype=BF16, tm=D_MODEL, tn=D_MODEL, tk=tk, name="dw_xo")
    grads["w_xq"] = _mm(h1b, dqxb, mode="tn", out_dtype=BF16, tm=D_MODEL, tn=D_MODEL, tk=tk, name="dw_xq")
    grads["w_xk"], grads["w_xv"], st_mem = _mem_bwd(dkx, dvx, mem, sp["mem_ln_g"], sp["mem_ln_b"],
                                                    wb["w_xk"], wb["w_xv"])

    dr1, dr1b, st1 = _ln_bwd(dh1, r1, sp["ln1_g"], t=t_row, name="ln1_bwd", want_bf16=True)
    grads["w_mix_out"] = _mm(mixed_b, dr1b, mode="tn", out_dtype=BF16, tm=D_MODEL, tn=D_MODEL, tk=tk,
                             name="dw_mix")
    dmixed = _mm(dr1b, wb["w_mix_out"], mode="nt", out_dtype=F32, tm=t_mm, tn=D_MODEL, name="dmixed")
    do_a, do_b, dl_a, dl_b, st_mix = _combine_bwd(dmixed, out_a, out_b, sp["g_win"], sp["g_dil"], t=t_row)
    (dqa, dka, dva, dsink), couts = _swa_bwd_s(
        za, do_a, lse_a, _stats_to_rows(dl_a, 1), cs[None], e_mat, qcol=0, kcol=4, vcol=5, hq=WIN_Q_HEADS,
        hkv=WIN_KV_HEADS, w=WIN_HALF, tq=tq_a, sub=sub_a, sink=sp["attn_sink"], name="attn_a_bwd",
        comm=plan.exchange("attn_a", grads))
    plan.exchanged("attn_a", couts)
    dqs, dks, dvs = [], [], []
    for gi, dil in enumerate(DILATIONS):
        (dq, dk, dv), _ = _swa_bwd_s(
            zb[gi], do_b[gi], _stats_to_rows(lse_b, dil), _stats_to_rows(dl_b, dil),
            _to_residue(cs, dil), e_mat, qcol=0, kcol=1, vcol=2, hq=DIL_SLOTS, hkv=DIL_SLOTS, w=DIL_HALF, tq=tq_b,
            sub=subs_b[gi], sink=None, name=f"attn_b{gi}_bwd")
        dqs.append(dq)
        dks.append(dk)
        dvs.append(dv)
    dz = _assemble_dz(dqa, dka, dva, dqs, dks, dvs, t=t_row)
    grads["w_in"] = _mm(h0b, dz, mode="tn", out_dtype=BF16, tm=D_MODEL, tn=IN_WIDTH // 7, tk=tk, name="dw_in")
    comm = plan.exchange("dh0", grads)
    dh0 = _mm(dz, wb["w_in"], mode="nt", out_dtype=F32, tm=t_mm, tn=D_MODEL, add=dr1, add_scale=ALPHA, name="dh0",
              comm=comm)
    if comm is not None:
        dh0, couts = dh0
        plan.exchanged("dh0", couts)
    grad_x, st0 = _ln_bwd(dh0, x, sp["ln_in_g"], t=t_row, name="ln_in_bwd", want_bf16=False)

    small = {
        "loss": st3[2:3, 0:1],
        "ln_in_g": st0[0:1], "ln_in_b": st0[1:2],
        "attn_sink": dsink[:, 0].reshape(1, WIN_Q_HEADS),
        "g_win": st_mix[0:1], "g_dil": st_mix[1:2],
        "ln1_g": st1[0:1], "ln1_b": st1[1:2],
        "mem_ln_g": st_mem[0:1], "mem_ln_b": st_mem[1:2],
        "ln2_g": st2[0:1], "ln2_b": st2[1:2],
        "conv_w": st_conv[0:3], "conv_b": st_conv[3:4],
        "ln3_g": st3[0:1], "ln3_b": st3[1:2],
    }
    return grad_x, grads, small


def _swap_sibling(arrays):
    n = len(arrays)

    def body(*refs):
        src, dst = refs[:n], refs[n:2 * n]
        send_sems, recv_sems = refs[2 * n:]
        x, y, c, _ = _place()
        copies = [pltpu.make_async_remote_copy(
            src_ref=src[a], dst_ref=dst[a], send_sem=send_sems.at[a], recv_sem=recv_sems.at[a],
            device_id=(x, y, 1 - c), device_id_type=MESH_IDS) for a in range(n)]
        for cp in copies:
            cp.start()
        for cp in copies:
            cp.wait_recv()
        for cp in copies:
            cp.wait_send()

    return pl.pallas_call(
        body, name="swap_sibling", in_specs=[ANY] * n, out_specs=[ANY] * n,
        out_shape=[jax.ShapeDtypeStruct(a.shape, a.dtype) for a in arrays],
        scratch_shapes=[pltpu.SemaphoreType.DMA((n,)), pltpu.SemaphoreType.DMA((n,))],
    )(*arrays)


def _row_tile(rows, cols, itemsize=4, budget=1 << 20):
    best = None
    for t in range(16, rows + 1, 16):
        if rows % t == 0 and t * cols * itemsize <= budget:
            best = t
    return best or rows


def _sum_slots(stack, *, name):
    n, r, c = stack.shape
    t = _row_tile(r, c)

    def body(s_ref, o_ref):
        acc = s_ref[0].astype(F32)
        for q in range(1, n):
            acc = acc + s_ref[q].astype(F32)
        o_ref[...] = acc

    return pl.pallas_call(
        body, name=name, grid=(r // t,), in_specs=[pl.BlockSpec((n, t, c), lambda i: (0, i, 0))],
        out_specs=pl.BlockSpec((t, c), lambda i: (i, 0)), out_shape=jax.ShapeDtypeStruct((r, c), F32),
        compiler_params=_cparams(dimension_semantics=("parallel",)),
    )(stack)


def _adamw(w, m, v, p, q, *, name):
    r, c = w.shape
    t = _row_tile(r, c, budget=1 << 19)

    def body(*refs):
        if q is None:
            w_ref, m_ref, v_ref, p_ref, g_ref, d_ref, nm_ref, nv_ref = refs
            g = p_ref[...]
        else:
            w_ref, m_ref, v_ref, p_ref, q_ref, g_ref, d_ref, nm_ref, nv_ref = refs
            g = p_ref[...] + q_ref[...]
        nm = ADAM_B1 * m_ref[...] + (1.0 - ADAM_B1) * g
        nv = ADAM_B2 * v_ref[...] + (1.0 - ADAM_B2) * (g * g)
        m_hat = nm / (1.0 - ADAM_B1 ** ADAM_STEP)
        v_hat = nv / (1.0 - ADAM_B2 ** ADAM_STEP)
        g_ref[...] = g
        d_ref[...] = -ADAM_LR * (m_hat / (jnp.sqrt(v_hat) + ADAM_EPS) + ADAM_WD * w_ref[...])
        nm_ref[...] = nm
        nv_ref[...] = nv

    tile = pl.BlockSpec((t, c), lambda i: (i, 0))
    args = [w, m, v, p] + ([] if q is None else [q])
    sh = jax.ShapeDtypeStruct((r, c), F32)
    return pl.pallas_call(
        body, name=name, grid=(r // t,), in_specs=[tile] * len(args), out_specs=[tile] * 4, out_shape=[sh] * 4,
        compiler_params=_cparams(dimension_semantics=("parallel",)),
    )(*args)


BIG = ("w_in", "w_mix_out", "w_xq", "w_xk", "w_xv", "w_xo", "w_gate", "w_up", "w_down")
COL_SHARDED = ("w_in", "w_gate", "w_up")
WEIGHTS = ("ln_in_g", "ln_in_b", "w_in", "attn_sink", "g_win", "g_dil", "w_mix_out", "ln1_g", "ln1_b",
           "mem_ln_g", "mem_ln_b", "w_xq", "w_xk", "w_xv", "w_xo", "ln2_g", "ln2_b", "w_gate", "w_up",
           "conv_w", "conv_b", "w_down", "ln3_g", "ln3_b")
SMALL = tuple(k for k in WEIGHTS if k not in BIG)
PACK_COLS = 1024
CONV_SHARD = D_FF // N_CHIPS
CONV_WIDTH_ROWS = 3
SMALL_ROWS = 32


GATHER_STAGES = {"proj": ("w_mix_out", "w_xq", "w_xk", "w_xv", "w_xo"), "attn_a": ("w_gate", "w_up"),
                 "attn_b0": ("w_down",)}
EXCHANGE_STAGES = {"dh2": ("w_down",), "xattn": ("w_gate", "w_up"),
                   "attn_a": ("w_xo", "w_xq", "w_xk", "w_xv", "w_mix_out"), "dh0": ("w_in",)}


def _full_weight(k, g4):
    if k in COL_SHARDED:
        return g4.transpose(1, 0, 2).reshape(g4.shape[1], N_CHIPS * g4.shape[2])
    return g4.reshape(N_CHIPS * g4.shape[1], g4.shape[2])


def _grad_parts(k, gk):
    gk = gk.astype(BF16)
    if k in COL_SHARDED:
        return gk.reshape(gk.shape[0], N_CHIPS, gk.shape[1] // N_CHIPS).transpose(1, 0, 2)
    return gk.reshape(N_CHIPS, gk.shape[0] // N_CHIPS, gk.shape[1])


class _Plan:
    def __init__(self, shards):
        self.shards = shards
        self.recv = {}

    def gather(self, stage):
        names = GATHER_STAGES.get(stage)
        return _ChipGather([self.shards[k] for k in names]) if names else None

    def gathered(self, stage, couts, wb):
        for k, g4 in zip(GATHER_STAGES.get(stage, ()), couts):
            wb[k] = _full_weight(k, g4)

    def exchange(self, stage, grads):
        names = EXCHANGE_STAGES.get(stage)
        return _ChipExchange([_grad_parts(k, grads[k]) for k in names]) if names else None

    def exchanged(self, stage, couts):
        for k, r4 in zip(EXCHANGE_STAGES.get(stage, ()), couts):
            self.recv[k] = r4


def _pack_rows(a):
    r, n = a.shape
    per = -(-n // PACK_COLS)
    return jnp.pad(a, ((0, 0), (0, per * PACK_COLS - n))).reshape(r * per, PACK_COLS)


def _unpack_rows(p, r, n):
    per = -(-n // PACK_COLS)
    return p.reshape(r, per * PACK_COLS)[:, :n]


def _pack(pieces, rows_total):
    cat = jnp.concatenate([_pack_rows(a) for a in pieces], axis=0)
    return jnp.pad(cat, ((0, rows_total - cat.shape[0]), (0, 0)))


def _unpack(p, shapes):
    out, at = [], 0
    for r, n in shapes:
        per = -(-n // PACK_COLS)
        out.append(_unpack_rows(p[at:at + r * per], r, n))
        at += r * per
    return out


def kernel(x, mem, positions, ln_in_g, ln_in_b, w_in, attn_sink, g_win, g_dil, w_mix_out, ln1_g, ln1_b, mem_ln_g, mem_ln_b, w_xq, w_xk, w_xv, w_xo, ln2_g, ln2_b, w_gate, w_up, conv_w, conv_b, w_down, ln3_g, ln3_b, loss_target, m_ln_in_g, m_ln_in_b, m_w_in, m_attn_sink, m_g_win, m_g_dil, m_w_mix_out, m_ln1_g, m_ln1_b, m_mem_ln_g, m_mem_ln_b, m_w_xq, m_w_xk, m_w_xv, m_w_xo, m_ln2_g, m_ln2_b, m_w_gate, m_w_up, m_conv_w, m_conv_b, m_w_down, m_ln3_g, m_ln3_b, v_ln_in_g, v_ln_in_b, v_w_in, v_attn_sink, v_g_win, v_g_dil, v_w_mix_out, v_ln1_g, v_ln1_b, v_mem_ln_g, v_mem_ln_b, v_w_xq, v_w_xk, v_w_xv, v_w_xo, v_ln2_g, v_ln2_b, v_w_gate, v_w_up, v_conv_w, v_conv_b, v_w_down, v_ln3_g, v_ln3_b):
    given = dict(locals())
    shape_of = {k: given[k].shape for k in WEIGHTS}
    as2d = lambda a: a.reshape(-1, a.shape[-1])
    w2 = {k: as2d(given[k]) for k in WEIGHTS}
    m2 = {k: as2d(given["m_" + k]) for k in WEIGHTS}
    v2 = {k: as2d(given["v_" + k]) for k in WEIGHTS}
    chip = 2 * lax.axis_index("x") + lax.axis_index("y")

    plan = _Plan({k: w2[k].astype(BF16) for k in BIG})
    conv_pack = jnp.pad(w2["conv_w"], ((0, 16 - CONV_WIDTH_ROWS), (0, PACK_COLS - CONV_SHARD)))
    g_in, g_conv = _comm_only(_ChipGather([plan.shards["w_in"], conv_pack]), "gather_w_in")
    w_in_full = _full_weight("w_in", g_in)
    wb = {"w_in": w_in_full,
          "w_in_seg": jnp.concatenate([w_in_full[:, a:b] for a, b in _proj_column_ranges()], axis=1)}
    conv_full = g_conv[:, :CONV_WIDTH_ROWS, :CONV_SHARD].transpose(1, 0, 2).reshape(CONV_WIDTH_ROWS, D_FF)
    sp = {k: w2[k] for k in SMALL}
    sp["conv_w"] = conv_full

    grad_x, grads, small = _local_step(x[0], mem[0], positions[0], loss_target[0], wb, sp, plan)

    small_keys = ("loss",) + SMALL
    small_shapes = [small[k].shape for k in small_keys]
    small_pack = _pack([small[k] for k in small_keys], SMALL_ROWS)
    (small_all,) = _comm_only(_ChipExchange([], small_pack), "exchange_small")
    chip_sums = [_sum_slots(plan.recv[k], name=f"sum_chips_{k}") for k in BIG]
    sibling_sums = _swap_sibling(chip_sums)
    small_sum = _sum_slots(small_all, name="sum_small")
    small_g = dict(zip(small_keys, _unpack(small_sum, small_shapes)))
    loss = small_g["loss"][0, 0]

    res = {}
    for k, p, q in zip(BIG, chip_sums, sibling_sums):
        res[k] = _adamw(w2[k], m2[k], v2[k], p, q, name=f"adamw_{k}")
    small_g["conv_w"] = lax.dynamic_slice_in_dim(small_g["conv_w"], chip * CONV_SHARD, CONV_SHARD, axis=1)
    adam_shapes = [w2[k].shape for k in SMALL]
    packs = [_pack([d[k] for k in SMALL], SMALL_ROWS) for d in (w2, m2, v2, small_g)]
    small_res = [_unpack(o, adam_shapes) for o in _adamw(*packs, None, name="adamw_small")]
    for i, k in enumerate(SMALL):
        res[k] = tuple(o[i] for o in small_res)

    outs = [loss, grad_x[None]]
    for slot in range(4):
        outs += [res[k][slot].reshape(shape_of[k]) for k in WEIGHTS]
    return tuple(outs)
```

```python
import functools
import math

import jax
import jax.numpy as jnp
from jax import lax
from jax.experimental import pallas as pl
from jax.experimental.pallas import tpu as pltpu

F32 = jnp.float32
BF16 = jnp.bfloat16

D_MODEL = 1024
HEAD_DIM = 64
WIN_Q_HEADS = 8
WIN_KV_HEADS = 2
WIN_HALF = 128
DIL_SLOTS = 8
DILATIONS = (1, 4, 16)
DIL_HALF = 64
ROT_DIM = 16
ROPE_THETA = 500000.0
X_HEADS = 4
X_HEAD_DIM = 256
D_FF = 2816
A_Q = 512
A_KV = 128
A_WIDTH = A_Q + 2 * A_KV
B_QKV = 1536
IN_WIDTH = 5376
ALPHA = 2.0 ** 0.25
LN_EPS = 1e-5
NEG_INF = -1e30
LANES = 128
N_CHIPS = 4
N_DEV = 8

ADAM_LR = 0.001
ADAM_B1 = 0.9
ADAM_B2 = 0.999
ADAM_EPS = 1e-08
ADAM_WD = 0.01
ADAM_STEP = 10

VMEM_LIMIT = 56 * 1024 * 1024


def _cparams(**kw):
    return pltpu.CompilerParams(vmem_limit_bytes=VMEM_LIMIT, **kw)


def _dot(a, b):
    return lax.dot_general(a, b, (((1,), (0,)), ((), ())), preferred_element_type=F32)


def _dot_nt(a, b):
    return lax.dot_general(a, b, (((1,), (1,)), ((), ())), preferred_element_type=F32)


def _dot_tn(a, b):
    return lax.dot_general(a, b, (((0,), (0,)), ((), ())), preferred_element_type=F32)


def _ln(x, g, b):
    mu = jnp.mean(x, axis=-1, keepdims=True)
    xc = x - mu
    var = jnp.mean(xc * xc, axis=-1, keepdims=True)
    return xc * lax.rsqrt(var + LN_EPS) * g + b


def _ln_bwd_math(dy, r, g):
    mu = jnp.mean(r, axis=-1, keepdims=True)
    xc = r - mu
    var = jnp.mean(xc * xc, axis=-1, keepdims=True)
    rstd = lax.rsqrt(var + LN_EPS)
    xhat = xc * rstd
    dxhat = dy * g
    m1 = jnp.mean(dxhat, axis=-1, keepdims=True)
    m2 = jnp.mean(dxhat * xhat, axis=-1, keepdims=True)
    dr = rstd * (dxhat - m1 - xhat * m2)
    return dr, jnp.sum(dy * xhat, axis=0, keepdims=True), jnp.sum(dy, axis=0, keepdims=True)


def _rope(z, ta, tb, tc, sign):
    w = z.shape[1]
    reps = w // LANES
    a = jnp.tile(ta, (1, reps))
    b = jnp.tile(tb, (1, reps))
    c = jnp.tile(tc, (1, reps))
    return z * a + sign * (pltpu.roll(z, w - 8, 1) * b + pltpu.roll(z, 8, 1) * c)


def _shift_rows(x, prev_row, next_row):
    t = x.shape[0]
    row = lax.broadcasted_iota(jnp.int32, x.shape, 0)
    xm1 = jnp.where(row == 0, prev_row, pltpu.roll(x, 1, 0))
    xp1 = jnp.where(row == t - 1, next_row, pltpu.roll(x, t - 1, 0))
    return xm1, xp1


def _rope_tabs(cs, e_mat):
    tabs = lax.dot_general(cs, e_mat, (((1,), (0,)), ((), ())), preferred_element_type=F32,
                           precision=lax.Precision.HIGHEST)
    lane = lax.broadcasted_iota(jnp.int32, (cs.shape[0], LANES), 1)
    ones = jnp.where((lane & (HEAD_DIM - 1)) >= ROT_DIM, 1.0, 0.0)
    return tabs[:, :LANES] + ones, tabs[:, LANES:2 * LANES], tabs[:, 2 * LANES:]


def _rope_select_matrix():
    half = ROT_DIM // 2
    e = [[0.0] * (3 * LANES) for _ in range(ROT_DIM)]
    for lane in range(LANES):
        d = lane % HEAD_DIM
        if d < half:
            e[d][lane] = 1.0
            e[half + d][LANES + lane] = -1.0
        elif d < ROT_DIM:
            e[d - half][lane] = 1.0
            e[d][2 * LANES + lane] = 1.0
    return jnp.array(e, F32)


MESH_IDS = pl.DeviceIdType.MESH
ANY = pl.BlockSpec(memory_space=pl.ANY)


def _place():
    x, y, c = lax.axis_index("x"), lax.axis_index("y"), lax.axis_index("c")
    other_chips = [(1 - x, y), (x, 1 - y), (1 - x, 1 - y)]
    return x, y, c, other_chips


class _ChipGather:
    def __init__(self, shards):
        self.inputs = list(shards)
        n = len(shards)
        self.out_shape = [jax.ShapeDtypeStruct((N_CHIPS,) + a.shape, a.dtype) for a in shards]
        self.scratch = [pltpu.SemaphoreType.DMA((6 * n,)), pltpu.SemaphoreType.DMA((6 * n,)),
                        pltpu.SemaphoreType.DMA((n,))]

    def _copies(self, src, dst, sems):
        send_sems, recv_sems, local_sems = sems
        x, y, c, chips = _place()
        mine = 2 * x + y
        n = len(src)
        local, sends, recvs, passes, pass_recvs = [], [], [], [], []
        for a in range(n):
            half = src[a].shape[0] // 2
            my_rows, other_rows = pl.ds(c * half, half), pl.ds((1 - c) * half, half)
            local.append(pltpu.make_async_copy(src[a], dst[a].at[mine], local_sems.at[a]))
            for j, (px, py) in enumerate(chips):
                k, k2, slot = 3 * a + j, 3 * n + 3 * a + j, 2 * px + py
                sends.append(pltpu.make_async_remote_copy(
                    src_ref=src[a].at[my_rows], dst_ref=dst[a].at[mine, my_rows], send_sem=send_sems.at[k],
                    recv_sem=recv_sems.at[k], device_id=(px, py, c), device_id_type=MESH_IDS))
                recvs.append(pltpu.make_async_remote_copy(
                    src_ref=src[a].at[my_rows], dst_ref=dst[a].at[slot, my_rows], send_sem=send_sems.at[k],
                    recv_sem=recv_sems.at[k], device_id=(px, py, c), device_id_type=MESH_IDS))
                passes.append(pltpu.make_async_remote_copy(
                    src_ref=dst[a].at[slot, my_rows], dst_ref=dst[a].at[slot, my_rows], send_sem=send_sems.at[k2],
                    recv_sem=recv_sems.at[k2], device_id=(x, y, 1 - c), device_id_type=MESH_IDS))
                pass_recvs.append(pltpu.make_async_remote_copy(
                    src_ref=dst[a].at[slot, my_rows], dst_ref=dst[a].at[slot, other_rows],
                    send_sem=send_sems.at[k2], recv_sem=recv_sems.at[k2], device_id=(x, y, 1 - c),
                    device_id_type=MESH_IDS))
        return local, sends, recvs, passes, pass_recvs

    def start(self, src, dst, sems):
        local, sends, _, _, _ = self._copies(src, dst, sems)
        for cp in local + sends:
            cp.start()

    def wait(self, src, dst, sems):
        local, sends, recvs, passes, pass_recvs = self._copies(src, dst, sems)
        for idx, landed in enumerate(recvs):
            landed.wait_recv()
            if passes:
                passes[idx].start()
        for cp in pass_recvs:
            cp.wait_recv()
        for cp in sends + passes:
            cp.wait_send()
        for cp in local:
            cp.wait()


class _ChipExchange:
    def __init__(self, parts, small=None):
        self.inputs = list(parts) + ([small] if small is not None else [])
        self.n = len(parts)
        self.has_small = small is not None
        self.out_shape = [jax.ShapeDtypeStruct(a.shape, a.dtype) for a in parts]
        n_sem, n_loc = 3 * self.n, self.n
        if self.has_small:
            self.out_shape.append(jax.ShapeDtypeStruct((N_DEV,) + small.shape, small.dtype))
            n_sem, n_loc = n_sem + N_DEV - 1, n_loc + 1
        self.scratch = [pltpu.SemaphoreType.DMA((n_sem,)), pltpu.SemaphoreType.DMA((n_sem,)),
                        pltpu.SemaphoreType.DMA((n_loc,))]

    def _copies(self, src, dst, sems):
        send_sems, recv_sems, local_sems = sems
        x, y, c, chips = _place()
        mine = 2 * x + y
        n = self.n
        local, sends, recvs = [], [], []
        for a in range(n):
            local.append(pltpu.make_async_copy(src[a].at[mine], dst[a].at[mine], local_sems.at[a]))
            for j, (px, py) in enumerate(chips):
                k = 3 * a + j
                sends.append(pltpu.make_async_remote_copy(
                    src_ref=src[a].at[2 * px + py], dst_ref=dst[a].at[mine], send_sem=send_sems.at[k],
                    recv_sem=recv_sems.at[k], device_id=(px, py, c), device_id_type=MESH_IDS))
                recvs.append(pltpu.make_async_remote_copy(
                    src_ref=src[a].at[mine], dst_ref=dst[a].at[2 * px + py], send_sem=send_sems.at[k],
                    recv_sem=recv_sems.at[k], device_id=(px, py, c), device_id_type=MESH_IDS))
        if self.has_small:
            me_dev = 4 * x + 2 * y + c
            local.append(pltpu.make_async_copy(src[n], dst[n].at[me_dev], local_sems.at[n]))
            for mask in range(1, N_DEV):
                px, py, pc = x ^ ((mask >> 2) & 1), y ^ ((mask >> 1) & 1), c ^ (mask & 1)
                k = 3 * n + mask - 1
                sends.append(pltpu.make_async_remote_copy(
                    src_ref=src[n], dst_ref=dst[n].at[me_dev], send_sem=send_sems.at[k], recv_sem=recv_sems.at[k],
                    device_id=(px, py, pc), device_id_type=MESH_IDS))
                recvs.append(pltpu.make_async_remote_copy(
                    src_ref=src[n], dst_ref=dst[n].at[4 * px + 2 * py + pc], send_sem=send_sems.at[k],
                    recv_sem=recv_sems.at[k], device_id=(px, py, pc), device_id_type=MESH_IDS))
        return local, sends, recvs, [], []

    start = _ChipGather.start
    wait = _ChipGather.wait


def _pcall(body, *, name, grid, in_specs, out_specs, out_shape, args, scratch_shapes=(), dims=None, comm=None):
    in_specs, out_specs, out_shape = list(in_specs), list(out_specs), list(out_shape)
    scratch_shapes = list(scratch_shapes)
    if comm is None:
        outs = pl.pallas_call(
            body, name=name, grid=grid, in_specs=in_specs, out_specs=out_specs, out_shape=out_shape,
            scratch_shapes=scratch_shapes, compiler_params=_cparams(dimension_semantics=dims),
        )(*args)
        return list(outs), []
    n_in, n_out, n_scr = len(in_specs), len(out_specs), len(scratch_shapes)
    n_cin, n_cout = len(comm.inputs), len(comm.out_shape)

    def wrapped(*refs):
        ins, refs = refs[:n_in], refs[n_in:]
        cins, refs = refs[:n_cin], refs[n_cin:]
        outs, refs = refs[:n_out], refs[n_out:]
        couts, refs = refs[:n_cout], refs[n_cout:]
        scr, csems = refs[:n_scr], refs[n_scr:]
        first = last = None
        for axis, size in enumerate(grid):
            pid = pl.program_id(axis)
            f, l = pid == 0, pid == size - 1
            first = f if first is None else first & f
            last = l if last is None else last & l

        @pl.when(first)
        def _():
            comm.start(cins, couts, csems)

        body(*ins, *outs, *scr)

        @pl.when(last)
        def _():
            comm.wait(cins, couts, csems)

    res = pl.pallas_call(
        wrapped, name=name, grid=grid, in_specs=in_specs + [ANY] * n_cin, out_specs=out_specs + [ANY] * n_cout,
        out_shape=out_shape + list(comm.out_shape), scratch_shapes=scratch_shapes + list(comm.scratch),
        compiler_params=_cparams(dimension_semantics=("arbitrary",) * len(grid)),
    )(*args, *comm.inputs)
    return list(res[:n_out]), list(res[n_out:])


def _comm_only(comm, name):
    def body(*refs):
        n_cin, n_cout = len(comm.inputs), len(comm.out_shape)
        cins, couts, csems = refs[:n_cin], refs[n_cin:n_cin + n_cout], refs[n_cin + n_cout:]
        comm.start(cins, couts, csems)
        comm.wait(cins, couts, csems)

    return list(pl.pallas_call(
        body, name=name, in_specs=[ANY] * len(comm.inputs), out_specs=[ANY] * len(comm.out_shape),
        out_shape=list(comm.out_shape), scratch_shapes=list(comm.scratch),
    )(*comm.inputs))


def _mm(a, b, *, mode, out_dtype, tm, tn, tk=None, add=None, add_scale=1.0, name, comm=None):
    if mode in ("nn", "nt"):
        m, k = a.shape
        n = b.shape[1] if mode == "nn" else b.shape[0]
        assert m % tm == 0 and n % tn == 0
        dot = _dot if mode == "nn" else _dot_nt

        def body(*refs):
            if add is None:
                a_ref, b_ref, o_ref = refs
                o_ref[...] = dot(a_ref[...], b_ref[...]).astype(out_dtype)
            else:
                a_ref, b_ref, c_ref, o_ref = refs
                o_ref[...] = (dot(a_ref[...], b_ref[...]) + add_scale * c_ref[...]).astype(out_dtype)

        b_spec = (pl.BlockSpec((k, tn), lambda i, j: (0, j)) if mode == "nn"
                  else pl.BlockSpec((tn, k), lambda i, j: (j, 0)))
        in_specs = [pl.BlockSpec((tm, k), lambda i, j: (i, 0)), b_spec]
        args = [a, b]
        if add is not None:
            in_specs.append(pl.BlockSpec((tm, tn), lambda i, j: (i, j)))
            args.append(add)
        outs, couts = _pcall(
            body, name=name, grid=(m // tm, n // tn), in_specs=in_specs,
            out_specs=[pl.BlockSpec((tm, tn), lambda i, j: (i, j))],
            out_shape=[jax.ShapeDtypeStruct((m, n), out_dtype)], args=args, dims=("parallel", "parallel"),
            comm=comm)
        return outs[0] if comm is None else (outs[0], couts)
    assert mode == "tn" and add is None and comm is None
    kk, m = a.shape
    n = b.shape[1]
    assert m % tm == 0 and n % tn == 0 and kk % tk == 0
    nk = kk // tk

    def body(a_ref, b_ref, o_ref, acc_ref):
        kstep = pl.program_id(2)

        @pl.when(kstep == 0)
        def _():
            acc_ref[...] = jnp.zeros_like(acc_ref)

        acc_ref[...] += _dot_tn(a_ref[...], b_ref[...])

        @pl.when(kstep == nk - 1)
        def _():
            o_ref[...] = acc_ref[...].astype(out_dtype)

    return pl.pallas_call(
        body, name=name, grid=(m // tm, n // tn, nk),
        in_specs=[pl.BlockSpec((tk, tm), lambda i, j, s: (s, i)), pl.BlockSpec((tk, tn), lambda i, j, s: (s, j))],
        out_specs=pl.BlockSpec((tm, tn), lambda i, j, s: (i, j)),
        out_shape=jax.ShapeDtypeStruct((m, n), out_dtype),
        scratch_shapes=[pltpu.VMEM((tm, tn), F32)],
        compiler_params=_cparams(dimension_semantics=("parallel", "parallel", "arbitrary")),
    )(a, b)


def _mm2_nt(a1, b1, a2, b2, add, *, add_scale, tm, name, comm=None):
    m, k = a1.shape
    n = b1.shape[0]

    def body(a1_ref, b1_ref, a2_ref, b2_ref, c_ref, o_ref):
        o_ref[...] = (_dot_nt(a1_ref[...], b1_ref[...]) + _dot_nt(a2_ref[...], b2_ref[...])
                      + add_scale * c_ref[...])

    a_spec = pl.BlockSpec((tm, k), lambda i: (i, 0))
    b_spec = pl.BlockSpec((n, k), lambda i: (0, 0))
    o_spec = pl.BlockSpec((tm, n), lambda i: (i, 0))
    outs, couts = _pcall(body, name=name, grid=(m // tm,), in_specs=[a_spec, b_spec, a_spec, b_spec, o_spec],
                         out_specs=[o_spec], out_shape=[jax.ShapeDtypeStruct((m, n), F32)],
                         args=[a1, b1, a2, b2, add], dims=("parallel",), comm=comm)
    return outs[0], couts


def _ln_bwd(dy, r, g, *, t, name, want_bf16):
    s = r.shape[0]

    def body(dy_ref, r_ref, g_ref, *outs):
        i = pl.program_id(0)
        dr, dg, db = _ln_bwd_math(dy_ref[...], r_ref[...], g_ref[...])
        outs[0][...] = dr
        if want_bf16:
            outs[1][...] = dr.astype(BF16)
        st_ref = outs[-1]

        @pl.when(i == 0)
        def _():
            st_ref[...] = jnp.zeros_like(st_ref)

        st_ref[0:1, :] += dg
        st_ref[1:2, :] += db

    tile = pl.BlockSpec((t, D_MODEL), lambda i: (i, 0))
    out_specs = [tile] + ([tile] if want_bf16 else []) + [pl.BlockSpec((8, D_MODEL), lambda i: (0, 0))]
    out_shape = ([jax.ShapeDtypeStruct((s, D_MODEL), F32)]
                 + ([jax.ShapeDtypeStruct((s, D_MODEL), BF16)] if want_bf16 else [])
                 + [jax.ShapeDtypeStruct((8, D_MODEL), F32)])
    return pl.pallas_call(
        body, name=name, grid=(s // t,),
        in_specs=[tile, tile, pl.BlockSpec((1, D_MODEL), lambda i: (0, 0))],
        out_specs=out_specs, out_shape=out_shape,
        compiler_params=_cparams(dimension_semantics=("arbitrary",)),
    )(dy, r, g)


PROJ_COLS = 256
PROJ_SEGMENTS = ((1, 0, (1, 1, 2)),) + tuple(
    (dil, A_WIDTH + gi * B_QKV, (1, 1, 1, 1, 0, 0)) for gi, dil in enumerate(DILATIONS))


def _proj_column_ranges():
    wd = DIL_SLOTS * HEAD_DIM
    ranges = [(0, A_WIDTH)]
    for gi in range(len(DILATIONS)):
        ranges += [(A_WIDTH + part * B_QKV + gi * wd, A_WIDTH + part * B_QKV + (gi + 1) * wd) for part in range(3)]
    return ranges


def _proj_all(x, g, b, w_seg, cs, e_mat, *, t, comm=None):
    s = x.shape[0]
    cb = PROJ_COLS
    halves = cb // LANES

    def body(x_ref, g_ref, b_ref, w_ref, cs_ref, e_ref, h_ref, *rest):
        z_refs, scr = rest[:-1], rest[-1]
        h = _ln(x_ref[...], g_ref[...], b_ref[...]).astype(BF16)
        h_ref[...] = h
        ta, tb, tc = (jnp.tile(tab, (1, halves)) for tab in _rope_tabs(cs_ref[...], e_ref[...]))
        lane = lax.broadcasted_iota(jnp.int32, (t, cb), 1)
        slot = 0
        for z_ref, (dil, col0, kinds) in zip(z_refs, PROJ_SEGMENTS):
            for jb, kind in enumerate(kinds):
                acc = _dot(h, w_ref[:, col0 + cb * jb:col0 + cb * (jb + 1)])
                if kind:
                    z = acc * ta + (pltpu.roll(acc, cb - 8, 1) * tb + pltpu.roll(acc, 8, 1) * tc)
                    if kind == 2:
                        z = jnp.where(lane < LANES, z, acc)
                else:
                    z = acc
                if dil == 1:
                    z_ref[0, :, cb * jb:cb * (jb + 1)] = z.astype(BF16)
                    continue
                for half in range(halves):
                    scr[slot, half] = z[:, half * LANES:(half + 1) * LANES]
                for c in range(dil):
                    for half in range(halves):
                        rows = scr[slot, half, pl.ds(c, t // dil, stride=dil), :]
                        z_ref[c, :, cb * jb + half * LANES:cb * jb + (half + 1) * LANES] = rows.astype(BF16)
                slot = 1 - slot

    row = pl.BlockSpec((1, D_MODEL), lambda i: (0, 0))
    widths = [cb * len(kinds) for _, _, kinds in PROJ_SEGMENTS]
    dils = [dil for dil, _, _ in PROJ_SEGMENTS]
    outs, couts = _pcall(
        body, name="proj_all", grid=(s // t,),
        in_specs=[pl.BlockSpec((t, D_MODEL), lambda i: (i, 0)), row, row,
                  pl.BlockSpec((D_MODEL, IN_WIDTH), lambda i: (0, 0)),
                  pl.BlockSpec((t, ROT_DIM), lambda i: (i, 0)), pl.BlockSpec((ROT_DIM, 3 * LANES), lambda i: (0, 0))],
        out_specs=[pl.BlockSpec((t, D_MODEL), lambda i: (i, 0))]
        + [pl.BlockSpec((dil, t // dil, wd), lambda i: (0, i, 0)) for dil, wd in zip(dils, widths)],
        out_shape=[jax.ShapeDtypeStruct((s, D_MODEL), BF16)]
        + [jax.ShapeDtypeStruct((dil, s // dil, wd), BF16) for dil, wd in zip(dils, widths)],
        args=[x, g, b, w_seg, cs, e_mat], scratch_shapes=[pltpu.VMEM((2, halves, t, LANES), F32)],
        dims=("parallel",), comm=comm)
    return outs, couts


def _window_mask(i, tq, w, seq_len):
    tk = tq + 2 * w
    qpos = i * tq + lax.broadcasted_iota(jnp.int32, (tq, tk), 0)
    kpos = i * tq - w + lax.broadcasted_iota(jnp.int32, (tq, tk), 1)
    return (jnp.abs(qpos - kpos) <= w) & (kpos >= 0) & (kpos < seq_len)


def _swa_specs(tq, hq, hkv, n, qcol, kcol, vcol):
    qw, kw = hq * HEAD_DIM, hkv * HEAD_DIM
    cur = lambda s, i: jnp.minimum(i, n - 1)
    prv = lambda s, i: jnp.maximum(jnp.minimum(i, n - 1) - 1, 0)
    nxt = lambda s, i: jnp.minimum(i + 1, n - 1)
    q_spec = pl.BlockSpec((None, tq, qw), lambda s, i: (s, cur(s, i), qcol))
    kv_specs = [pl.BlockSpec((None, tq, kw), (lambda s, i, f=f, c=c: (s, f(s, i), c)))
                for c in (kcol, vcol) for f in (prv, cur, nxt)]
    return q_spec, kv_specs, cur, prv


def _swa_fwd(qkv, *, qcol, kcol, vcol, hq, hkv, w, tq, sink, name, comm=None):
    nseq, seq_len, _ = qkv.shape
    n = seq_len // tq
    rep = hq // hkv
    q_spec, kv_specs, _, _ = _swa_specs(tq, hq, hkv, n, qcol, kcol, vcol)

    def body(*refs):
        if sink is not None:
            sink_ref, refs = refs[0], refs[1:]
        q_ref, kp_ref, kc_ref, kn_ref, vp_ref, vc_ref, vn_ref, o_ref, lse_ref = refs
        i = pl.program_id(1)
        mask = _window_mask(i, tq, w, seq_len)
        lane = lax.broadcasted_iota(jnp.int32, (tq, LANES), 1)
        lse_acc = jnp.zeros((tq, LANES), F32)
        for g in range(hkv):
            cs = slice(g * HEAD_DIM, (g + 1) * HEAD_DIM)
            kcat = jnp.concatenate([kp_ref[tq - w:, cs], kc_ref[:, cs], kn_ref[:w, cs]], axis=0)
            vcat = jnp.concatenate([vp_ref[tq - w:, cs], vc_ref[:, cs], vn_ref[:w, cs]], axis=0)
            for r in range(rep):
                h = g * rep + r
                hs = slice(h * HEAD_DIM, (h + 1) * HEAD_DIM)
                qh = q_ref[:, hs] * 0.125
                sc = jnp.where(mask, _dot_nt(qh, kcat), NEG_INF)
                m = jnp.max(sc, axis=1, keepdims=True)
                if sink is not None:
                    m = jnp.maximum(m, sink_ref[0, h])
                p = jnp.exp(sc - m)
                den = jnp.sum(p, axis=1, keepdims=True)
                if sink is not None:
                    den = den + jnp.exp(sink_ref[0, h] - m)
                o_ref[:, hs] = _dot(p.astype(BF16), vcat) / den
                lse_acc = jnp.where(lane == h, m + jnp.log(den), lse_acc)
        lse_ref[...] = lse_acc

    in_specs = [q_spec] + kv_specs
    args = [qkv] * 7
    if sink is not None:
        in_specs = [pl.BlockSpec(memory_space=pltpu.SMEM)] + in_specs
        args = [sink] + args
    (o, lse), couts = _pcall(
        body, name=name, grid=(nseq, n), in_specs=in_specs,
        out_specs=[pl.BlockSpec((None, tq, hq * HEAD_DIM), lambda s, i: (s, i, 0)),
                   pl.BlockSpec((None, tq, LANES), lambda s, i: (s, i, 0))],
        out_shape=[jax.ShapeDtypeStruct((nseq, seq_len, hq * HEAD_DIM), F32),
                   jax.ShapeDtypeStruct((nseq, seq_len, LANES), F32)],
        args=args, dims=("parallel", "parallel"), comm=comm)
    return o, lse, couts


def _swa_bwd(qkv, do, lse, delta, cs, e_mat, *, qcol, kcol, vcol, hq, hkv, w, tq, sink, name, comm=None):
    nseq, seq_len, _ = qkv.shape
    n = seq_len // tq
    rep = hq // hkv
    qw, kw = hq * HEAD_DIM, hkv * HEAD_DIM
    tk = tq + 2 * w
    q_spec, kv_specs, cur, prv = _swa_specs(tq, hq, hkv, n, qcol, kcol, vcol)

    def body(*refs):
        if sink is not None:
            sink_ref, refs = refs[0], refs[1:]
        (q_ref, kp_ref, kc_ref, kn_ref, vp_ref, vc_ref, vn_ref, do_ref, lse_ref, dl_ref,
         cs_c, cs_p, e_ref) = refs[:13]
        outs = refs[13:]
        if sink is not None:
            dq_ref, dk_ref, dv_ref, dsink_ref, dk_acc, dv_acc = outs
        else:
            dq_ref, dk_ref, dv_ref, dk_acc, dv_acc = outs
        s_id = pl.program_id(0)
        i = pl.program_id(1)
        slot_p, slot_c, slot_n = (i + 2) % 3, i % 3, (i + 1) % 3

        if sink is not None:
            @pl.when((s_id == 0) & (i == 0))
            def _():
                dsink_ref[...] = jnp.zeros_like(dsink_ref)

        @pl.when(i < n)
        def _():
            mask = _window_mask(i, tq, w, seq_len)
            dk_acc[slot_n] = jnp.zeros((tq, kw), F32)
            dv_acc[slot_n] = jnp.zeros((tq, kw), F32)

            @pl.when(i == 0)
            def _():
                dk_acc[slot_c] = jnp.zeros((tq, kw), F32)
                dv_acc[slot_c] = jnp.zeros((tq, kw), F32)

            dq_parts, dk_parts, dv_parts = [], [], []
            for g in range(hkv):
                cs = slice(g * HEAD_DIM, (g + 1) * HEAD_DIM)
                kcat = jnp.concatenate([kp_ref[tq - w:, cs], kc_ref[:, cs], kn_ref[:w, cs]], axis=0)
                vcat = jnp.concatenate([vp_ref[tq - w:, cs], vc_ref[:, cs], vn_ref[:w, cs]], axis=0)
                dkc = jnp.zeros((tk, HEAD_DIM), F32)
                dvc = jnp.zeros((tk, HEAD_DIM), F32)
                for r in range(rep):
                    h = g * rep + r
                    hs = slice(h * HEAD_DIM, (h + 1) * HEAD_DIM)
                    qh = q_ref[:, hs] * 0.125
                    sc = jnp.where(mask, _dot_nt(qh, kcat), NEG_INF)
                    lse_h = lse_ref[:, h:h + 1]
                    dl_h = dl_ref[:, h:h + 1]
                    p = jnp.exp(sc - lse_h)
                    doh = do_ref[:, hs]
                    dp = _dot_nt(doh, vcat)
                    dsb = (p * (dp - dl_h)).astype(BF16)
                    dq_parts.append(_dot(dsb, kcat) * 0.125)
                    dkc = dkc + _dot_tn(dsb, qh)
                    dvc = dvc + _dot_tn(p.astype(BF16), doh)
                    if sink is not None:
                        ds_sink = -jnp.sum(jnp.exp(sink_ref[0, h] - lse_h) * dl_h)
                        dsink_ref[h:h + 1, :] += jnp.full((1, LANES), ds_sink, F32)
                dk_parts.append(dkc)
                dv_parts.append(dvc)
            dq = jnp.concatenate(dq_parts, axis=1)
            dq_ref[...] = _rope(dq, *_rope_tabs(cs_c[...], e_ref[...]), -1.0).astype(BF16)
            dk_all = jnp.concatenate(dk_parts, axis=1)
            dv_all = jnp.concatenate(dv_parts, axis=1)

            @pl.when(i > 0)
            def _():
                dk_acc[slot_p, tq - w:, :] += dk_all[:w]
                dv_acc[slot_p, tq - w:, :] += dv_all[:w]

            dk_acc[slot_c] += dk_all[w:w + tq]
            dv_acc[slot_c] += dv_all[w:w + tq]
            dk_acc[slot_n, :w, :] += dk_all[w + tq:]
            dv_acc[slot_n, :w, :] += dv_all[w + tq:]

        @pl.when(i >= 1)
        def _():
            dk_ref[...] = _rope(dk_acc[slot_p], *_rope_tabs(cs_p[...], e_ref[...]), -1.0).astype(BF16)
            dv_ref[...] = dv_acc[slot_p].astype(BF16)

    row_c = lambda width: pl.BlockSpec((None, tq, width), lambda s, i: (s, cur(s, i), 0))
    row_p = lambda width: pl.BlockSpec((None, tq, width), lambda s, i: (s, jnp.maximum(i - 1, 0), 0))
    in_specs = ([q_spec] + kv_specs + [row_c(qw), row_c(LANES), row_c(LANES), row_c(ROT_DIM), row_p(ROT_DIM),
                                       pl.BlockSpec((ROT_DIM, 3 * LANES), lambda s, i: (0, 0))])
    args = [qkv] * 7 + [do, lse, delta, cs, cs, e_mat]
    out_specs = [row_c(qw), row_p(kw), row_p(kw)]
    out_shape = [jax.ShapeDtypeStruct((nseq, seq_len, qw), BF16),
                 jax.ShapeDtypeStruct((nseq, seq_len, kw), BF16),
                 jax.ShapeDtypeStruct((nseq, seq_len, kw), BF16)]
    if sink is not None:
        in_specs = [pl.BlockSpec(memory_space=pltpu.SMEM)] + in_specs
        args = [sink] + args
        out_specs.append(pl.BlockSpec((8, LANES), lambda s, i: (0, 0)))
        out_shape.append(jax.ShapeDtypeStruct((8, LANES), F32))
    return _pcall(
        body, name=name, grid=(nseq, n + 1), in_specs=in_specs, out_specs=out_specs, out_shape=out_shape,
        scratch_shapes=[pltpu.VMEM((3, tq, kw), F32), pltpu.VMEM((3, tq, kw), F32)], args=args,
        dims=("arbitrary", "arbitrary"), comm=comm)


PAIR = 2 * HEAD_DIM


def _window_mask_t(i, tq, w, seq_len):
    tk = tq + 2 * w
    kpos = i * tq - w + lax.broadcasted_iota(jnp.int32, (tk, tq), 0)
    qpos = i * tq + lax.broadcasted_iota(jnp.int32, (tk, tq), 1)
    return (jnp.abs(qpos - kpos) <= w) & (kpos >= 0) & (kpos < seq_len)


def _place_head(x2, src_pos, dst_pos):
    hi = lax.broadcasted_iota(jnp.int32, x2.shape, 1) >= HEAD_DIM
    src = x2 if src_pos == dst_pos else pltpu.roll(x2, HEAD_DIM, 1)
    return jnp.where(hi == (dst_pos == 1), src, jnp.zeros_like(src))


def _swa_fwd_t(qkv, *, qcol, kcol, vcol, hq, hkv, w, tq, sink, name, comm=None):
    nseq, seq_len, _ = qkv.shape
    n = seq_len // tq
    rep = hq // hkv
    q_spec, kv_specs, _, _ = _swa_specs(tq, hq, hkv, n, qcol, kcol, vcol)

    def body(*refs):
        if sink is not None:
            sink_ref, refs = refs[0], refs[1:]
        q_ref, kp_ref, kc_ref, kn_ref, vp_ref, vc_ref, vn_ref, o_ref, lse_ref = refs
        i = pl.program_id(1)
        mask_t = _window_mask_t(i, tq, w, seq_len)
        o_t = [None] * (hq // 2)
        lse_rows = [None] * hq
        for a in range(hkv // 2):
            ls = slice(a * PAIR, (a + 1) * PAIR)
            kcat = jnp.concatenate([kp_ref[tq - w:, ls], kc_ref[:, ls], kn_ref[:w, ls]], axis=0) * 0.125
            vcat = jnp.concatenate([vp_ref[tq - w:, ls], vc_ref[:, ls], vn_ref[:w, ls]], axis=0)
            for e in range(2):
                g = 2 * a + e
                placed = {}
                for r in range(rep):
                    h = g * rep + r
                    qp, pos = h // 2, h % 2
                    if pos not in placed:
                        placed[pos] = (_place_head(kcat, e, pos), _place_head(vcat, e, pos))
                    k_g, v_g = placed[pos]
                    s_t = jnp.where(mask_t, _dot_nt(k_g, q_ref[:, qp * PAIR:(qp + 1) * PAIR]), NEG_INF)
                    m = jnp.max(s_t, axis=0, keepdims=True)
                    if sink is not None:
                        m = jnp.maximum(m, sink_ref[0, h])
                    p_t = jnp.exp(s_t - m)
                    den = jnp.sum(p_t, axis=0, keepdims=True)
                    if sink is not None:
                        den = den + jnp.exp(sink_ref[0, h] - m)
                    part = _dot_tn(v_g, p_t.astype(BF16)) / den
                    o_t[qp] = part if o_t[qp] is None else o_t[qp] + part
                    lse_rows[h] = m + jnp.log(den)
        o_ref[...] = jnp.concatenate(o_t, axis=0).T
        lse_ref[...] = jnp.concatenate(lse_rows, axis=0)

    in_specs = [q_spec] + kv_specs
    args = [qkv] * 7
    if sink is not None:
        in_specs = [pl.BlockSpec(memory_space=pltpu.SMEM)] + in_specs
        args = [sink] + args
    (o, lse), couts = _pcall(
        body, name=name, grid=(nseq, n), in_specs=in_specs,
        out_specs=[pl.BlockSpec((None, tq, hq * HEAD_DIM), lambda s, i: (s, i, 0)),
                   pl.BlockSpec((None, hq, tq), lambda s, i: (s, 0, i))],
        out_shape=[jax.ShapeDtypeStruct((nseq, seq_len, hq * HEAD_DIM), F32),
                   jax.ShapeDtypeStruct((nseq, hq, seq_len), F32)],
        args=args, dims=("parallel", "parallel"), comm=comm)
    return o, lse, couts


def _swa_bwd_t(qkv, do, lse, delta, cs, e_mat, *, qcol, kcol, vcol, hq, hkv, w, tq, sink, name, comm=None):
    nseq, seq_len, _ = qkv.shape
    n = seq_len // tq
    rep = hq // hkv
    qw, kw = hq * HEAD_DIM, hkv * HEAD_DIM
    tk = tq + 2 * w
    q_spec, kv_specs, cur, prv = _swa_specs(tq, hq, hkv, n, qcol, kcol, vcol)

    def body(*refs):
        if sink is not None:
            sink_ref, refs = refs[0], refs[1:]
        (q_ref, kp_ref, kc_ref, kn_ref, vp_ref, vc_ref, vn_ref, do_ref, lse_ref, dl_ref,
         cs_c, cs_p, e_ref) = refs[:13]
        outs = refs[13:]
        if sink is not None:
            dq_ref, dk_ref, dv_ref, dsink_ref, dk_acc, dv_acc = outs
        else:
            dq_ref, dk_ref, dv_ref, dk_acc, dv_acc = outs
        s_id = pl.program_id(0)
        i = pl.program_id(1)
        slot_p, slot_c, slot_n = (i + 2) % 3, i % 3, (i + 1) % 3

        if sink is not None:
            @pl.when((s_id == 0) & (i == 0))
            def _():
                dsink_ref[...] = jnp.zeros_like(dsink_ref)

        @pl.when(i < n)
        def _():
            mask_t = _window_mask_t(i, tq, w, seq_len)
            dk_acc[slot_n] = jnp.zeros((tq, kw), F32)
            dv_acc[slot_n] = jnp.zeros((tq, kw), F32)

            @pl.when(i == 0)
            def _():
                dk_acc[slot_c] = jnp.zeros((tq, kw), F32)
                dv_acc[slot_c] = jnp.zeros((tq, kw), F32)

            dq_t = [None] * (hq // 2)
            dk_pairs, dv_pairs = [], []
            for a in range(hkv // 2):
                ls = slice(a * PAIR, (a + 1) * PAIR)
                kcat = jnp.concatenate([kp_ref[tq - w:, ls], kc_ref[:, ls], kn_ref[:w, ls]], axis=0) * 0.125
                vcat = jnp.concatenate([vp_ref[tq - w:, ls], vc_ref[:, ls], vn_ref[:w, ls]], axis=0)
                dk2 = jnp.zeros((tk, PAIR), F32)
                dv2 = jnp.zeros((tk, PAIR), F32)
                for e in range(2):
                    g = 2 * a + e
                    placed = {}
                    for r in range(rep):
                        h = g * rep + r
                        qp, pos = h // 2, h % 2
                        if pos not in placed:
                            placed[pos] = (_place_head(kcat, e, pos), _place_head(vcat, e, pos))
                        k_g, v_g = placed[pos]
                        q2 = q_ref[:, qp * PAIR:(qp + 1) * PAIR]
                        do2 = do_ref[:, qp * PAIR:(qp + 1) * PAIR]
                        lse_h = lse_ref[h:h + 1, :]
                        dl_h = dl_ref[h:h + 1, :]
                        p_t = jnp.exp(jnp.where(mask_t, _dot_nt(k_g, q2), NEG_INF) - lse_h)
                        dp_t = _dot_nt(v_g, do2)
                        dsb = (p_t * (dp_t - dl_h)).astype(BF16)
                        part = _dot_tn(k_g, dsb)
                        dq_t[qp] = part if dq_t[qp] is None else dq_t[qp] + part
                        dk2 = dk2 + _dot(dsb, _place_head(q2, pos, e) * 0.125)
                        dv2 = dv2 + _dot(p_t.astype(BF16), _place_head(do2, pos, e))
                        if sink is not None:
                            ds_sink = -jnp.sum(jnp.exp(sink_ref[0, h] - lse_h) * dl_h)
                            dsink_ref[h:h + 1, :] += jnp.full((1, LANES), ds_sink, F32)
                dk_pairs.append(dk2)
                dv_pairs.append(dv2)
            dq = jnp.concatenate(dq_t, axis=0).T
            dq_ref[...] = _rope(dq, *_rope_tabs(cs_c[...], e_ref[...]), -1.0).astype(BF16)
            dk_all = dk_pairs[0] if len(dk_pairs) == 1 else jnp.concatenate(dk_pairs, axis=1)
            dv_all = dv_pairs[0] if len(dv_pairs) == 1 else jnp.concatenate(dv_pairs, axis=1)

            @pl.when(i > 0)
            def _():
                dk_acc[slot_p, tq - w:, :] += dk_all[:w]
                dv_acc[slot_p, tq - w:, :] += dv_all[:w]

            dk_acc[slot_c] += dk_all[w:w + tq]
            dv_acc[slot_c] += dv_all[w:w + tq]
            dk_acc[slot_n, :w, :] += dk_all[w + tq:]
            dv_acc[slot_n, :w, :] += dv_all[w + tq:]

        @pl.when(i >= 1)
        def _():
            dk_ref[...] = _rope(dk_acc[slot_p], *_rope_tabs(cs_p[...], e_ref[...]), -1.0).astype(BF16)
            dv_ref[...] = dv_acc[slot_p].astype(BF16)

    row_c = lambda width: pl.BlockSpec((None, tq, width), lambda s, i: (s, cur(s, i), 0))
    row_p = lambda width: pl.BlockSpec((None, tq, width), lambda s, i: (s, jnp.maximum(i - 1, 0), 0))
    stat = pl.BlockSpec((None, hq, tq), lambda s, i: (s, 0, cur(s, i)))
    in_specs = ([q_spec] + kv_specs + [row_c(qw), stat, stat, row_c(ROT_DIM), row_p(ROT_DIM),
                                       pl.BlockSpec((ROT_DIM, 3 * LANES), lambda s, i: (0, 0))])
    args = [qkv] * 7 + [do, lse, delta, cs, cs, e_mat]
    out_specs = [row_c(qw), row_p(kw), row_p(kw)]
    out_shape = [jax.ShapeDtypeStruct((nseq, seq_len, qw), BF16),
                 jax.ShapeDtypeStruct((nseq, seq_len, kw), BF16),
                 jax.ShapeDtypeStruct((nseq, seq_len, kw), BF16)]
    if sink is not None:
        in_specs = [pl.BlockSpec(memory_space=pltpu.SMEM)] + in_specs
        args = [sink] + args
        out_specs.append(pl.BlockSpec((8, LANES), lambda s, i: (0, 0)))
        out_shape.append(jax.ShapeDtypeStruct((8, LANES), F32))
    return _pcall(
        body, name=name, grid=(nseq, n + 1), in_specs=in_specs, out_specs=out_specs, out_shape=out_shape,
        scratch_shapes=[pltpu.VMEM((3, tq, kw), F32), pltpu.VMEM((3, tq, kw), F32)], args=args,
        dims=("arbitrary", "arbitrary"), comm=comm)


def _band_mask_t(row0, tq, w, seq_len):
    tk = tq + 2 * w
    kk = lax.broadcasted_iota(jnp.int32, (tk, tq), 0)
    qq = lax.broadcasted_iota(jnp.int32, (tk, tq), 1)
    kpos = row0 - w + kk
    return (jnp.abs(qq + w - kk) <= w) & (kpos >= 0) & (kpos < seq_len)


def _halo_kv_specs(t, w, hkv, n, seq_len, kcol, vcol):
    kw = hkv * HEAD_DIM
    per, last = t // w, seq_len // w - 1
    cur = lambda s, i: jnp.minimum(i, n - 1)
    specs = []
    for c in (kcol, vcol):
        specs += [pl.BlockSpec((None, w, kw), lambda s, i, c=c: (s, jnp.maximum(cur(s, i) * per - 1, 0), c)),
                  pl.BlockSpec((None, t, kw), lambda s, i, c=c: (s, cur(s, i), c)),
                  pl.BlockSpec((None, w, kw), lambda s, i, c=c: (s, jnp.minimum((cur(s, i) + 1) * per, last), c))]
    return specs, cur


def _swa_fwd_s(qkv, *, qcol, kcol, vcol, hq, hkv, w, tq, sub, sink, name, comm=None):
    nseq, seq_len, _ = qkv.shape
    t = tq * sub
    n = seq_len // t
    rep = hq // hkv
    tk = tq + 2 * w
    kv_specs, cur = _halo_kv_specs(t, w, hkv, n, seq_len, kcol, vcol)

    def body(*refs):
        if sink is not None:
            sink_ref, refs = refs[0], refs[1:]
        q_ref, kp_ref, kc_ref, kn_ref, vp_ref, vc_ref, vn_ref, o_ref, lse_ref = refs
        i = pl.program_id(1)
        kfull, vfull = [], []
        for a in range(hkv // 2):
            ls = slice(a * PAIR, (a + 1) * PAIR)
            kfull.append(jnp.concatenate([kp_ref[:, ls], kc_ref[:, ls], kn_ref[:, ls]], axis=0) * 0.125)
            vfull.append(jnp.concatenate([vp_ref[:, ls], vc_ref[:, ls], vn_ref[:, ls]], axis=0))
        for jj in range(sub):
            rows = slice(jj * tq, (jj + 1) * tq)
            mask_t = _band_mask_t(i * t + jj * tq, tq, w, seq_len)
            o_t = [None] * (hq // 2)
            lse_rows = [None] * hq
            for a in range(hkv // 2):
                kcat = kfull[a][jj * tq:jj * tq + tk]
                vcat = vfull[a][jj * tq:jj * tq + tk]
                for e in range(2):
                    g = 2 * a + e
                    placed = {}
                    for r in range(rep):
                        h = g * rep + r
                        qp, pos = h // 2, h % 2
                        if pos not in placed:
                            placed[pos] = (_place_head(kcat, e, pos), _place_head(vcat, e, pos))
                        k_g, v_g = placed[pos]
                        s_t = jnp.where(mask_t, _dot_nt(k_g, q_ref[rows, qp * PAIR:(qp + 1) * PAIR]), NEG_INF)
                        m = jnp.max(s_t, axis=0, keepdims=True)
                        if sink is not None:
                            m = jnp.maximum(m, sink_ref[0, h])
                        p_t = jnp.exp(s_t - m)
                        den = jnp.sum(p_t, axis=0, keepdims=True)
                        if sink is not None:
                            den = den + jnp.exp(sink_ref[0, h] - m)
                        part = _dot_tn(v_g, p_t.astype(BF16)) / den
                        o_t[qp] = part if o_t[qp] is None else o_t[qp] + part
                        lse_rows[h] = m + jnp.log(den)
            o_ref[rows, :] = jnp.concatenate(o_t, axis=0).T
            lse_ref[:, rows] = jnp.concatenate(lse_rows, axis=0)

    in_specs = [pl.BlockSpec((None, t, hq * HEAD_DIM), lambda s, i: (s, i, qcol))] + kv_specs
    args = [qkv] * 7
    if sink is not None:
        in_specs = [pl.BlockSpec(memory_space=pltpu.SMEM)] + in_specs
        args = [sink] + args
    (o, lse), couts = _pcall(
        body, name=name, grid=(nseq, n), in_specs=in_specs,
        out_specs=[pl.BlockSpec((None, t, hq * HEAD_DIM), lambda s, i: (s, i, 0)),
                   pl.BlockSpec((None, hq, t), lambda s, i: (s, 0, i))],
        out_shape=[jax.ShapeDtypeStruct((nseq, seq_len, hq * HEAD_DIM), F32),
                   jax.ShapeDtypeStruct((nseq, hq, seq_len), F32)],
        args=args, dims=("parallel", "parallel"), comm=comm)
    return o, lse, couts


def _swa_bwd_s(qkv, do, lse, delta, cs, e_mat, *, qcol, kcol, vcol, hq, hkv, w, tq, sub, sink, name, comm=None):
    nseq, seq_len, _ = qkv.shape
    t = tq * sub
    n = seq_len // t
    rep = hq // hkv
    qw, kw = hq * HEAD_DIM, hkv * HEAD_DIM
    tk = tq + 2 * w
    kv_specs, cur = _halo_kv_specs(t, w, hkv, n, seq_len, kcol, vcol)

    def body(*refs):
        if sink is not None:
            sink_ref, refs = refs[0], refs[1:]
        (q_ref, kp_ref, kc_ref, kn_ref, vp_ref, vc_ref, vn_ref, do_ref, lse_ref, dl_ref,
         cs_c, cs_p, e_ref) = refs[:13]
        outs = refs[13:]
        if sink is not None:
            dq_ref, dk_ref, dv_ref, dsink_ref, dk_acc, dv_acc, dk_win, dv_win = outs
        else:
            dq_ref, dk_ref, dv_ref, dk_acc, dv_acc, dk_win, dv_win = outs
        s_id = pl.program_id(0)
        i = pl.program_id(1)
        slot_p, slot_c, slot_n = (i + 2) % 3, i % 3, (i + 1) % 3

        if sink is not None:
            @pl.when((s_id == 0) & (i == 0))
            def _():
                dsink_ref[...] = jnp.zeros_like(dsink_ref)

        @pl.when(i < n)
        def _():
            dk_win[...] = jnp.zeros_like(dk_win)
            dv_win[...] = jnp.zeros_like(dv_win)
            kfull, vfull = [], []
            for a in range(hkv // 2):
                ls = slice(a * PAIR, (a + 1) * PAIR)
                kfull.append(jnp.concatenate([kp_ref[:, ls], kc_ref[:, ls], kn_ref[:, ls]], axis=0) * 0.125)
                vfull.append(jnp.concatenate([vp_ref[:, ls], vc_ref[:, ls], vn_ref[:, ls]], axis=0))
            for jj in range(sub):
                rows = slice(jj * tq, (jj + 1) * tq)
                krows = slice(jj * tq, jj * tq + tk)
                mask_t = _band_mask_t(i * t + jj * tq, tq, w, seq_len)
                dq_t = [None] * (hq // 2)
                for a in range(hkv // 2):
                    ls = slice(a * PAIR, (a + 1) * PAIR)
                    kcat, vcat = kfull[a][krows], vfull[a][krows]
                    dk2 = jnp.zeros((tk, PAIR), F32)
                    dv2 = jnp.zeros((tk, PAIR), F32)
                    for e in range(2):
                        g = 2 * a + e
                        placed = {}
                        for r in range(rep):
                            h = g * rep + r
                            qp, pos = h // 2, h % 2
                            if pos not in placed:
                                placed[pos] = (_place_head(kcat, e, pos), _place_head(vcat, e, pos))
                            k_g, v_g = placed[pos]
                            q2 = q_ref[rows, qp * PAIR:(qp + 1) * PAIR]
                            do2 = do_ref[rows, qp * PAIR:(qp + 1) * PAIR]
                            lse_h = lse_ref[h:h + 1, rows]
                            dl_h = dl_ref[h:h + 1, rows]
                            p_t = jnp.exp(jnp.where(mask_t, _dot_nt(k_g, q2), NEG_INF) - lse_h)
                            dp_t = _dot_nt(v_g, do2)
                            dsb = (p_t * (dp_t - dl_h)).astype(BF16)
                            part = _dot_tn(k_g, dsb)
                            dq_t[qp] = part if dq_t[qp] is None else dq_t[qp] + part
                            dk2 = dk2 + _dot(dsb, _place_head(q2, pos, e) * 0.125)
                            dv2 = dv2 + _dot(p_t.astype(BF16), _place_head(do2, pos, e))
                            if sink is not None:
                                ds_sink = -jnp.sum(jnp.exp(sink_ref[0, h] - lse_h) * dl_h)
                                dsink_ref[h:h + 1, :] += jnp.full((1, LANES), ds_sink, F32)
                    dk_win[krows, ls] += dk2
                    dv_win[krows, ls] += dv2
                dq = jnp.concatenate(dq_t, axis=0).T
                dq_ref[rows, :] = _rope(dq, *_rope_tabs(cs_c[rows, :], e_ref[...]), -1.0).astype(BF16)

            @pl.when(i > 0)
            def _():
                dk_acc[slot_p, t - w:, :] += dk_win[:w, :]
                dv_acc[slot_p, t - w:, :] += dv_win[:w, :]

            @pl.when(i == 0)
            def _():
                dk_acc[slot_c] = dk_win[w:w + t, :]
                dv_acc[slot_c] = dv_win[w:w + t, :]

            @pl.when(i > 0)
            def _():
                dk_acc[slot_c] += dk_win[w:w + t, :]
                dv_acc[slot_c] += dv_win[w:w + t, :]

            dk_acc[slot_n] = jnp.zeros((t, kw), F32)
            dv_acc[slot_n] = jnp.zeros((t, kw), F32)
            dk_acc[slot_n, :w, :] = dk_win[w + t:, :]
            dv_acc[slot_n, :w, :] = dv_win[w + t:, :]

        @pl.when(i >= 1)
        def _():
            dk_ref[...] = _rope(dk_acc[slot_p], *_rope_tabs(cs_p[...], e_ref[...]), -1.0).astype(BF16)
            dv_ref[...] = dv_acc[slot_p].astype(BF16)

    row_c = lambda width: pl.BlockSpec((None, t, width), lambda s, i: (s, cur(s, i), 0))
    row_p = lambda width: pl.BlockSpec((None, t, width), lambda s, i: (s, jnp.maximum(i - 1, 0), 0))
    stat = pl.BlockSpec((None, hq, t), lambda s, i: (s, 0, cur(s, i)))
    in_specs = ([pl.BlockSpec((None, t, qw), lambda s, i: (s, cur(s, i), qcol))] + kv_specs
                + [row_c(qw), stat, stat, row_c(ROT_DIM), row_p(ROT_DIM),
                   pl.BlockSpec((ROT_DIM, 3 * LANES), lambda s, i: (0, 0))])
    args = [qkv] * 7 + [do, lse, delta, cs, cs, e_mat]
    out_specs = [row_c(qw), row_p(kw), row_p(kw)]
    out_shape = [jax.ShapeDtypeStruct((nseq, seq_len, qw), BF16),
                 jax.ShapeDtypeStruct((nseq, seq_len, kw), BF16),
                 jax.ShapeDtypeStruct((nseq, seq_len, kw), BF16)]
    if sink is not None:
        in_specs = [pl.BlockSpec(memory_space=pltpu.SMEM)] + in_specs
        args = [sink] + args
        out_specs.append(pl.BlockSpec((8, LANES), lambda s, i: (0, 0)))
        out_shape.append(jax.ShapeDtypeStruct((8, LANES), F32))
    return _pcall(
        body, name=name, grid=(nseq, n + 1), in_specs=in_specs, out_specs=out_specs, out_shape=out_shape,
        scratch_shapes=[pltpu.VMEM((3, t, kw), F32), pltpu.VMEM((3, t, kw), F32),
                        pltpu.VMEM((t + 2 * w, kw), F32), pltpu.VMEM((t + 2 * w, kw), F32)], args=args,
        dims=("arbitrary", "arbitrary"), comm=comm)


def _rms_parts(o, g):
    ms = jnp.mean(o * o, axis=-1, keepdims=True) + LN_EPS
    rinv = lax.rsqrt(ms)
    return o * rinv * g, rinv


def _from_subsequences(ref, scr, dil, t):
    slabs = ref.shape[-1] // LANES
    if dil == 1:
        return ref[0].astype(F32)
    for c in range(dil):
        for sl in range(slabs):
            scr[sl, pl.ds(c, t // dil, stride=dil), :] = ref[c, :, sl * LANES:(sl + 1) * LANES].astype(F32)
    return jnp.concatenate([scr[sl] for sl in range(slabs)], axis=1)


def _to_subsequences(val, ref, scr, dil, t):
    slabs = val.shape[-1] // LANES
    if dil == 1:
        ref[0] = val.astype(ref.dtype)
        return
    for sl in range(slabs):
        scr[sl] = val[:, sl * LANES:(sl + 1) * LANES]
    for c in range(dil):
        for sl in range(slabs):
            ref[c, :, sl * LANES:(sl + 1) * LANES] = scr[sl, pl.ds(c, t // dil, stride=dil), :].astype(ref.dtype)


def _combine_fwd(out_a, o_g, lse_g, g_win, g_dil, *, t):
    s = out_a.shape[1]
    wd = DIL_SLOTS * HEAD_DIM

    def body(oa_ref, o0, o1, o2, l0, l1, l2, gw_ref, gd_ref, mixed_ref, ob_ref, lt_ref, scr):
        ls = [l0[...], l1[...], l2[...]]
        mx = jnp.maximum(jnp.maximum(ls[0], ls[1]), ls[2])
        ws = [jnp.exp(l - mx) for l in ls]
        tot = ws[0] + ws[1] + ws[2]
        lt_ref[...] = mx + jnp.log(tot)
        ws = [x / tot for x in ws]
        og = [_from_subsequences(o_ref, scr.at[gi], dil, t)
              for gi, (o_ref, dil) in enumerate(zip((o0, o1, o2), DILATIONS))]
        parts = []
        for h in range(DIL_SLOTS):
            hs = slice(h * HEAD_DIM, (h + 1) * HEAD_DIM)
            parts.append(ws[0][:, h:h + 1] * og[0][:, hs] + ws[1][:, h:h + 1] * og[1][:, hs]
                         + ws[2][:, h:h + 1] * og[2][:, hs])
        ob = jnp.concatenate(parts, axis=1)
        ob_ref[...] = ob
        na, _ = _rms_parts(oa_ref[...], gw_ref[...])
        nb, _ = _rms_parts(ob, gd_ref[...])
        mixed_ref[:, :wd] = na.astype(BF16)
        mixed_ref[:, wd:] = nb.astype(BF16)

    half = pl.BlockSpec((t, wd), lambda i: (i, 0))
    lanes = pl.BlockSpec((t, LANES), lambda i: (i, 0))
    grow = pl.BlockSpec((1, wd), lambda i: (0, 0))
    subseq = [pl.BlockSpec((dil, t // dil, wd), lambda i: (0, i, 0)) for dil in DILATIONS]
    return pl.pallas_call(
        body, name="combine_fwd", grid=(s // t,),
        in_specs=[pl.BlockSpec((None, t, wd), lambda i: (0, i, 0))] + subseq + [lanes, lanes, lanes, grow, grow],
        out_specs=[pl.BlockSpec((t, 2 * wd), lambda i: (i, 0)), half, lanes],
        out_shape=[jax.ShapeDtypeStruct((s, 2 * wd), BF16), jax.ShapeDtypeStruct((s, wd), F32),
                   jax.ShapeDtypeStruct((s, LANES), F32)],
        scratch_shapes=[pltpu.VMEM((len(DILATIONS), wd // LANES, t, LANES), F32)],
        compiler_params=_cparams(dimension_semantics=("parallel",)),
    )(out_a, *o_g, *lse_g, g_win, g_dil)


def _combine_bwd(dmixed, out_a, out_b, g_win, g_dil, *, t):
    s = out_b.shape[0]
    wd = DIL_SLOTS * HEAD_DIM

    def body(dm_ref, oa_ref, ob_ref, gw_ref, gd_ref, doa_ref, dob0, dob1, dob2, dla_ref, dlb_ref, st_ref, scr):
        i = pl.program_id(0)

        @pl.when(i == 0)
        def _():
            st_ref[...] = jnp.zeros_like(st_ref)

        lane = lax.broadcasted_iota(jnp.int32, (t, LANES), 1)
        for idx, (o_ref, g_ref, dl_ref) in enumerate(((oa_ref, gw_ref, dla_ref), (ob_ref, gd_ref, dlb_ref))):
            o = o_ref[...]
            dn = dm_ref[:, idx * wd:(idx + 1) * wd]
            _, rinv = _rms_parts(o, g_ref[...])
            wv = dn * g_ref[...]
            do = rinv * wv - o * (rinv * rinv * rinv) * jnp.mean(wv * o, axis=-1, keepdims=True)
            st_ref[idx:idx + 1, :] += jnp.sum(dn * o * rinv, axis=0, keepdims=True)
            if idx == 0:
                doa_ref[...] = do.astype(BF16)
            else:
                for do_ref, dil in zip((dob0, dob1, dob2), DILATIONS):
                    _to_subsequences(do, do_ref, scr, dil, t)
            prod = do * o
            acc = jnp.zeros((t, LANES), F32)
            for h in range(DIL_SLOTS):
                hs = slice(h * HEAD_DIM, (h + 1) * HEAD_DIM)
                acc = jnp.where(lane == h, jnp.sum(prod[:, hs], axis=1, keepdims=True), acc)
            dl_ref[...] = acc

    half = pl.BlockSpec((t, wd), lambda i: (i, 0))
    lanes = pl.BlockSpec((t, LANES), lambda i: (i, 0))
    grow = pl.BlockSpec((1, wd), lambda i: (0, 0))
    a_spec = pl.BlockSpec((None, t, wd), lambda i: (0, i, 0))
    subseq = [pl.BlockSpec((dil, t // dil, wd), lambda i: (0, i, 0)) for dil in DILATIONS]
    doa, dob0, dob1, dob2, dla, dlb, st = pl.pallas_call(
        body, name="combine_bwd", grid=(s // t,),
        in_specs=[pl.BlockSpec((t, 2 * wd), lambda i: (i, 0)), a_spec, half, grow, grow],
        out_specs=[a_spec] + subseq + [lanes, lanes, pl.BlockSpec((8, wd), lambda i: (0, 0))],
        out_shape=[jax.ShapeDtypeStruct((1, s, wd), BF16)]
        + [jax.ShapeDtypeStruct((dil, s // dil, wd), BF16) for dil in DILATIONS]
        + [jax.ShapeDtypeStruct((s, LANES), F32), jax.ShapeDtypeStruct((s, LANES), F32),
           jax.ShapeDtypeStruct((8, wd), F32)],
        scratch_shapes=[pltpu.VMEM((wd // LANES, t, LANES), F32)],
        compiler_params=_cparams(dimension_semantics=("arbitrary",)),
    )(dmixed, out_a, out_b, g_win, g_dil)
    return doa, [dob0, dob1, dob2], dla, dlb, st


def _assemble_dz(dqa, dka, dva, dqs, dks, dvs, *, t):
    s = dqa.shape[1]
    wd = DIL_SLOTS * HEAD_DIM

    def body(*refs):
        a_refs, g_refs, o_ref, scr = refs[:3], refs[3:12], refs[12], refs[13]
        col = 0
        for r in a_refs:
            o_ref[:, col:col + r.shape[-1]] = r[...]
            col += r.shape[-1]
        for part in range(3):
            for gi, dil in enumerate(DILATIONS):
                val = _from_subsequences(g_refs[3 * part + gi], scr, dil, t)
                o_ref[:, col:col + wd] = val.astype(BF16)
                col += wd

    a_specs = [pl.BlockSpec((None, t, a.shape[-1]), lambda i: (0, i, 0)) for a in (dqa, dka, dva)]
    g_specs = [pl.BlockSpec((dil, t // dil, wd), lambda i: (0, i, 0)) for _ in range(3) for dil in DILATIONS]
    return pl.pallas_call(
        body, name="assemble_dz", grid=(s // t,), in_specs=a_specs + g_specs,
        out_specs=pl.BlockSpec((t, IN_WIDTH), lambda i: (i, 0)),
        out_shape=jax.ShapeDtypeStruct((s, IN_WIDTH), BF16),
        scratch_shapes=[pltpu.VMEM((wd // LANES, t, LANES), F32)],
        compiler_params=_cparams(dimension_semantics=("parallel",)),
    )(dqa, dka, dva, *dqs, *dks, *dvs)


def _mixproj_fwd(mixed_b, w_mix_b, x, ln_in_g, ln_in_b, ln1_g, ln1_b, *, t):
    s = x.shape[0]

    def body(m_ref, w_ref, x_ref, g0, b0, g1, b1, r1_ref, h1_ref):
        h0 = _ln(x_ref[...], g0[...], b0[...])
        r1 = ALPHA * h0 + _dot(m_ref[...], w_ref[...])
        r1_ref[...] = r1
        h1_ref[...] = _ln(r1, g1[...], b1[...]).astype(BF16)

    tile = pl.BlockSpec((t, D_MODEL), lambda i: (i, 0))
    row = pl.BlockSpec((1, D_MODEL), lambda i: (0, 0))
    return pl.pallas_call(
        body, name="mixproj_fwd", grid=(s // t,),
        in_specs=[tile, pl.BlockSpec((D_MODEL, D_MODEL), lambda i: (0, 0)), tile, row, row, row, row],
        out_specs=[tile, tile],
        out_shape=[jax.ShapeDtypeStruct((s, D_MODEL), F32), jax.ShapeDtypeStruct((s, D_MODEL), BF16)],
        compiler_params=_cparams(dimension_semantics=("parallel",)),
    )(mixed_b, w_mix_b, x, ln_in_g, ln_in_b, ln1_g, ln1_b)


def _mem_fwd(mem, g, b, wk_b, wv_b):
    ml = mem.shape[0]

    def body(mem_ref, g_ref, b_ref, wk_ref, wv_ref, mn_ref, kx_ref, vx_ref):
        mn = _ln(mem_ref[...], g_ref[...], b_ref[...]).astype(BF16)
        mn_ref[...] = mn
        kx_ref[...] = _dot(mn, wk_ref[...]).astype(BF16)
        vx_ref[...] = _dot(mn, wv_ref[...]).astype(BF16)

    sh = jax.ShapeDtypeStruct((ml, D_MODEL), BF16)
    return pl.pallas_call(body, name="mem_fwd", out_shape=[sh, sh, sh], compiler_params=_cparams())(
        mem, g, b, wk_b, wv_b)


def _mem_bwd(dkx, dvx, mem, g, b, wk_b, wv_b):
    def body(dk_ref, dv_ref, mem_ref, g_ref, b_ref, wk_ref, wv_ref, dwk_ref, dwv_ref, st_ref):
        mem_v = mem_ref[...]
        mn = _ln(mem_v, g_ref[...], b_ref[...]).astype(BF16)
        dkb = dk_ref[...].astype(BF16)
        dvb = dv_ref[...].astype(BF16)
        dwk_ref[...] = _dot_tn(mn, dkb)
        dwv_ref[...] = _dot_tn(mn, dvb)
        dmn = _dot_nt(dkb, wk_ref[...]) + _dot_nt(dvb, wv_ref[...])
        _, dg, db = _ln_bwd_math(dmn, mem_v, g_ref[...])
        st_ref[...] = jnp.zeros_like(st_ref)
        st_ref[0:1, :] = dg
        st_ref[1:2, :] = db

    sw = jax.ShapeDtypeStruct((D_MODEL, D_MODEL), F32)
    return pl.pallas_call(body, name="mem_bwd", out_shape=[sw, sw, jax.ShapeDtypeStruct((8, D_MODEL), F32)],
                          compiler_params=_cparams())(dkx, dvx, mem, g, b, wk_b, wv_b)


def _xattn_fwd(h1b, r1, kx, vx, wq_b, wo_b, ln1_g, ln1_b, ln2_g, ln2_b, *, t):
    s = h1b.shape[0]
    scale = X_HEAD_DIM ** -0.5

    def body(h_ref, r1_ref, kx_ref, vx_ref, wq_ref, wo_ref, g1, b1, g2, b2, r2_ref, h2_ref, qx_ref, ox_ref, lse_ref):
        qxb = _dot(h_ref[...], wq_ref[...]).astype(BF16)
        qx_ref[...] = qxb
        lane = lax.broadcasted_iota(jnp.int32, (t, LANES), 1)
        lse_acc = jnp.zeros((t, LANES), F32)
        parts = []
        for h in range(X_HEADS):
            hs = slice(h * X_HEAD_DIM, (h + 1) * X_HEAD_DIM)
            sc = _dot_nt(qxb[:, hs] * scale, kx_ref[:, hs])
            m = jnp.max(sc, axis=1, keepdims=True)
            p = jnp.exp(sc - m)
            den = jnp.sum(p, axis=1, keepdims=True)
            parts.append(_dot(p.astype(BF16), vx_ref[:, hs]) / den)
            lse_acc = jnp.where(lane == h, m + jnp.log(den), lse_acc)
        lse_ref[...] = lse_acc
        oxb = jnp.concatenate(parts, axis=1).astype(BF16)
        ox_ref[...] = oxb
        h1 = _ln(r1_ref[...], g1[...], b1[...])
        r2 = ALPHA * h1 + _dot(oxb, wo_ref[...])
        r2_ref[...] = r2
        h2_ref[...] = _ln(r2, g2[...], b2[...]).astype(BF16)

    tile = pl.BlockSpec((t, D_MODEL), lambda i: (i, 0))
    row = pl.BlockSpec((1, D_MODEL), lambda i: (0, 0))
    full = lambda r: pl.BlockSpec((r, D_MODEL), lambda i: (0, 0))
    ml = kx.shape[0]
    bsh = jax.ShapeDtypeStruct((s, D_MODEL), BF16)
    return pl.pallas_call(
        body, name="xattn_fwd", grid=(s // t,),
        in_specs=[tile, tile, full(ml), full(ml), full(D_MODEL), full(D_MODEL), row, row, row, row],
        out_specs=[tile, tile, tile, tile, pl.BlockSpec((t, LANES), lambda i: (i, 0))],
        out_shape=[jax.ShapeDtypeStruct((s, D_MODEL), F32), bsh, bsh, bsh, jax.ShapeDtypeStruct((s, LANES), F32)],
        compiler_params=_cparams(dimension_semantics=("parallel",)),
    )(h1b, r1, kx, vx, wq_b, wo_b, ln1_g, ln1_b, ln2_g, ln2_b)


def _xattn_bwd(dr2, qxb, oxb, lse, kx, vx, wq_b, wo_b, r1, ln1_g, *, t, comm=None):
    s = dr2.shape[0]
    ml = kx.shape[0]
    scale = X_HEAD_DIM ** -0.5

    def body(dr2_ref, qx_ref, ox_ref, lse_ref, kx_ref, vx_ref, wq_ref, wo_ref, r1_ref, g1_ref,
             dr1_ref, dr1b_ref, dqx_ref, dkx_ref, dvx_ref, st_ref):
        i = pl.program_id(0)

        @pl.when(i == 0)
        def _():
            dkx_ref[...] = jnp.zeros_like(dkx_ref)
            dvx_ref[...] = jnp.zeros_like(dvx_ref)
            st_ref[...] = jnp.zeros_like(st_ref)

        dr2v = dr2_ref[...]
        dox = _dot_nt(dr2v.astype(BF16), wo_ref[...])
        parts = []
        for h in range(X_HEADS):
            hs = slice(h * X_HEAD_DIM, (h + 1) * X_HEAD_DIM)
            doh = dox[:, hs]
            dohb = doh.astype(BF16)
            dl = jnp.sum(doh * ox_ref[:, hs].astype(F32), axis=1, keepdims=True)
            qh = qx_ref[:, hs] * scale
            p = jnp.exp(_dot_nt(qh, kx_ref[:, hs]) - lse_ref[:, h:h + 1])
            dp = _dot_nt(dohb, vx_ref[:, hs])
            dsb = (p * (dp - dl)).astype(BF16)
            parts.append(_dot(dsb, kx_ref[:, hs]) * scale)
            dkx_ref[:, hs] += _dot_tn(dsb, qh)
            dvx_ref[:, hs] += _dot_tn(p.astype(BF16), dohb)
        dqxb = jnp.concatenate(parts, axis=1).astype(BF16)
        dqx_ref[...] = dqxb
        dh1 = _dot_nt(dqxb, wq_ref[...]) + ALPHA * dr2v
        dr1, dg, db = _ln_bwd_math(dh1, r1_ref[...], g1_ref[...])
        dr1_ref[...] = dr1
        dr1b_ref[...] = dr1.astype(BF16)
        st_ref[0:1, :] += dg
        st_ref[1:2, :] += db

    tile = pl.BlockSpec((t, D_MODEL), lambda i: (i, 0))
    full = lambda r: pl.BlockSpec((r, D_MODEL), lambda i: (0, 0))
    bsh = jax.ShapeDtypeStruct((s, D_MODEL), BF16)
    return _pcall(
        body, name="xattn_bwd", grid=(s // t,),
        in_specs=[tile, tile, tile, pl.BlockSpec((t, LANES), lambda i: (i, 0)), full(ml), full(ml),
                  full(D_MODEL), full(D_MODEL), tile, full(1)],
        out_specs=[tile, tile, tile, full(ml), full(ml), full(8)],
        out_shape=[jax.ShapeDtypeStruct((s, D_MODEL), F32), bsh, bsh,
                   jax.ShapeDtypeStruct((ml, D_MODEL), F32), jax.ShapeDtypeStruct((ml, D_MODEL), F32),
                   jax.ShapeDtypeStruct((8, D_MODEL), F32)],
        args=[dr2, qxb, oxb, lse, kx, vx, wq_b, wo_b, r1, ln1_g], dims=("arbitrary",), comm=comm)


def _halo_specs(t, s, width):
    tb8 = t // 8
    return [pl.BlockSpec((t, width), lambda i: (i, 0)),
            pl.BlockSpec((8, width), lambda i: (jnp.maximum(i * tb8 - 1, 0), 0)),
            pl.BlockSpec((8, width), lambda i: (jnp.minimum((i + 1) * tb8, s // 8 - 1), 0))]


def _halo_rows(i, n, prev_ref, next_ref):
    prev_row = jnp.where(i > 0, prev_ref[7:8, :], 0.0)
    next_row = jnp.where(i < n - 1, next_ref[0:1, :], 0.0)
    return prev_row, next_row


def _gelu_parts(gc):
    cdf = 0.5 * (1.0 + lax.erf(gc * (2.0 ** -0.5)))
    pdf = jnp.exp(-0.5 * gc * gc) * (1.0 / math.sqrt(2.0 * math.pi))
    return gc * cdf, cdf + gc * pdf


def _conv_fwd(g, u, conv_w, conv_b, *, t):
    s = g.shape[0]
    n = s // t

    def body(g_ref, gp_ref, gn_ref, u_ref, cw_ref, cb_ref, o_ref):
        i = pl.program_id(0)
        gv = g_ref[...]
        prev_row, next_row = _halo_rows(i, n, gp_ref, gn_ref)
        gm1, gp1 = _shift_rows(gv, prev_row, next_row)
        gc = gm1 * cw_ref[0:1, :] + gv * cw_ref[1:2, :] + gp1 * cw_ref[2:3, :] + cb_ref[...]
        act, _ = _gelu_parts(gc)
        o_ref[...] = (act * u_ref[...]).astype(BF16)

    tile = pl.BlockSpec((t, D_FF), lambda i: (i, 0))
    return pl.pallas_call(
        body, name="conv_fwd", grid=(n,),
        in_specs=_halo_specs(t, s, D_FF) + [tile, pl.BlockSpec((3, D_FF), lambda i: (0, 0)),
                                            pl.BlockSpec((1, D_FF), lambda i: (0, 0))],
        out_specs=tile, out_shape=jax.ShapeDtypeStruct((s, D_FF), BF16),
        compiler_params=_cparams(dimension_semantics=("parallel",)),
    )(g, g, g, u, conv_w, conv_b)


def _down_ln3(tb, w_down_b, r2, target, ln2_g, ln2_b, ln3_g, ln3_b, *, t):
    s = r2.shape[0]

    def body(t_ref, w_ref, r2_ref, tg_ref, g2, b2, g3, b3, dr_ref, drb_ref, st_ref):
        i = pl.program_id(0)

        @pl.when(i == 0)
        def _():
            st_ref[...] = jnp.zeros_like(st_ref)

        h2 = _ln(r2_ref[...], g2[...], b2[...])
        r3 = ALPHA * h2 + _dot(t_ref[...], w_ref[...])
        y = _ln(r3, g3[...], b3[...])
        err = y - tg_ref[...]
        loss = 0.5 * jnp.sum(jnp.mean(err * err, axis=-1, keepdims=True))
        dy = err * (1.0 / D_MODEL)
        dr, dg, db = _ln_bwd_math(dy, r3, g3[...])
        dr_ref[...] = dr
        drb_ref[...] = dr.astype(BF16)
        st_ref[0:1, :] += dg
        st_ref[1:2, :] += db
        st_ref[2:3, :] += jnp.full((1, D_MODEL), loss, F32)

    tile = pl.BlockSpec((t, D_MODEL), lambda i: (i, 0))
    row = pl.BlockSpec((1, D_MODEL), lambda i: (0, 0))
    return pl.pallas_call(
        body, name="down_ln3", grid=(s // t,),
        in_specs=[pl.BlockSpec((t, D_FF), lambda i: (i, 0)), pl.BlockSpec((D_FF, D_MODEL), lambda i: (0, 0)),
                  tile, tile, row, row, row, row],
        out_specs=[tile, tile, pl.BlockSpec((8, D_MODEL), lambda i: (0, 0))],
        out_shape=[jax.ShapeDtypeStruct((s, D_MODEL), F32), jax.ShapeDtypeStruct((s, D_MODEL), BF16),
                   jax.ShapeDtypeStruct((8, D_MODEL), F32)],
        compiler_params=_cparams(dimension_semantics=("arbitrary",)),
    )(tb, w_down_b, r2, target, ln2_g, ln2_b, ln3_g, ln3_b)


def _ffn_out(g, u, conv_w, conv_b, w_down_b, r2, target, ln2_g, ln2_b, ln3_g, ln3_b, *, t):
    s = r2.shape[0]
    n = s // t

    def body(g_ref, gp_ref, gn_ref, u_ref, cw_ref, cb_ref, w_ref, r2_ref, tg_ref, g2, b2, g3, b3,
             t_ref, dr_ref, drb_ref, st_ref):
        i = pl.program_id(0)

        @pl.when(i == 0)
        def _():
            st_ref[...] = jnp.zeros_like(st_ref)

        gv = g_ref[...]
        prev_row, next_row = _halo_rows(i, n, gp_ref, gn_ref)
        gm1, gp1 = _shift_rows(gv, prev_row, next_row)
        gc = gm1 * cw_ref[0:1, :] + gv * cw_ref[1:2, :] + gp1 * cw_ref[2:3, :] + cb_ref[...]
        act, _ = _gelu_parts(gc)
        tb = (act * u_ref[...]).astype(BF16)
        t_ref[...] = tb
        h2 = _ln(r2_ref[...], g2[...], b2[...])
        r3 = ALPHA * h2 + _dot(tb, w_ref[...])
        y = _ln(r3, g3[...], b3[...])
        err = y - tg_ref[...]
        loss = 0.5 * jnp.sum(jnp.mean(err * err, axis=-1, keepdims=True))
        dr, dg, db = _ln_bwd_math(err * (1.0 / D_MODEL), r3, g3[...])
        dr_ref[...] = dr
        drb_ref[...] = dr.astype(BF16)
        st_ref[0:1, :] += dg
        st_ref[1:2, :] += db
        st_ref[2:3, :] += jnp.full((1, D_MODEL), loss, F32)

    wide = pl.BlockSpec((t, D_FF), lambda i: (i, 0))
    tile = pl.BlockSpec((t, D_MODEL), lambda i: (i, 0))
    row = pl.BlockSpec((1, D_MODEL), lambda i: (0, 0))
    return pl.pallas_call(
        body, name="ffn_out", grid=(n,),
        in_specs=_halo_specs(t, s, D_FF) + [wide, pl.BlockSpec((3, D_FF), lambda i: (0, 0)),
                                            pl.BlockSpec((1, D_FF), lambda i: (0, 0)),
                                            pl.BlockSpec((D_FF, D_MODEL), lambda i: (0, 0)),
                                            tile, tile, row, row, row, row],
        out_specs=[wide, tile, tile, pl.BlockSpec((8, D_MODEL), lambda i: (0, 0))],
        out_shape=[jax.ShapeDtypeStruct((s, D_FF), BF16), jax.ShapeDtypeStruct((s, D_MODEL), F32),
                   jax.ShapeDtypeStruct((s, D_MODEL), BF16), jax.ShapeDtypeStruct((8, D_MODEL), F32)],
        compiler_params=_cparams(dimension_semantics=("arbitrary",)),
    )(g, g, g, u, conv_w, conv_b, w_down_b, r2, target, ln2_g, ln2_b, ln3_g, ln3_b)


def _dh2_ln2(dgc, conv_w, du, w_gate_b, w_up_b, dr3, r2, ln2_g, *, t, comm=None):
    s = dgc.shape[0]
    n = s // t

    def body(d_ref, dp_ref, dn_ref, cw_ref, du_ref, wg_ref, wu_ref, dr3_ref, r2_ref, g2, dg_ref, dr_ref, drb_ref,
             st_ref):
        i = pl.program_id(0)

        @pl.when(i == 0)
        def _():
            st_ref[...] = jnp.zeros_like(st_ref)

        dv = d_ref[...]
        prev_row, next_row = _halo_rows(i, n, dp_ref, dn_ref)
        dm1, dp1 = _shift_rows(dv, prev_row, next_row)
        dgb = (dp1 * cw_ref[0:1, :] + dv * cw_ref[1:2, :] + dm1 * cw_ref[2:3, :]).astype(BF16)
        dg_ref[...] = dgb
        dh2 = _dot_nt(dgb, wg_ref[...]) + _dot_nt(du_ref[...], wu_ref[...]) + ALPHA * dr3_ref[...]
        dr, dg, db = _ln_bwd_math(dh2, r2_ref[...], g2[...])
        dr_ref[...] = dr
        drb_ref[...] = dr.astype(BF16)
        st_ref[0:1, :] += dg
        st_ref[1:2, :] += db

    wide = pl.BlockSpec((t, D_FF), lambda i: (i, 0))
    tile = pl.BlockSpec((t, D_MODEL), lambda i: (i, 0))
    wfull = pl.BlockSpec((D_MODEL, D_FF), lambda i: (0, 0))
    return _pcall(
        body, name="dh2_ln2", grid=(n,),
        in_specs=_halo_specs(t, s, D_FF) + [pl.BlockSpec((3, D_FF), lambda i: (0, 0)), wide, wfull, wfull,
                                            tile, tile, pl.BlockSpec((1, D_MODEL), lambda i: (0, 0))],
        out_specs=[wide, tile, tile, pl.BlockSpec((8, D_MODEL), lambda i: (0, 0))],
        out_shape=[jax.ShapeDtypeStruct((s, D_FF), BF16), jax.ShapeDtypeStruct((s, D_MODEL), F32),
                   jax.ShapeDtypeStruct((s, D_MODEL), BF16), jax.ShapeDtypeStruct((8, D_MODEL), F32)],
        args=[dgc, dgc, dgc, conv_w, du, w_gate_b, w_up_b, dr3, r2, ln2_g], dims=("arbitrary",), comm=comm)


def _conv_bwd_a(dr3b, w_down_b, g, u, conv_w, conv_b, *, t):
    s = g.shape[0]
    n = s // t

    def body(d_ref, w_ref, g_ref, gp_ref, gn_ref, u_ref, cw_ref, cb_ref, du_ref, dgc_ref, st_ref):
        i = pl.program_id(0)

        @pl.when(i == 0)
        def _():
            st_ref[...] = jnp.zeros_like(st_ref)

        dt = _dot_nt(d_ref[...], w_ref[...])
        gv = g_ref[...]
        prev_row, next_row = _halo_rows(i, n, gp_ref, gn_ref)
        gm1, gp1 = _shift_rows(gv, prev_row, next_row)
        gc = gm1 * cw_ref[0:1, :] + gv * cw_ref[1:2, :] + gp1 * cw_ref[2:3, :] + cb_ref[...]
        act, dact = _gelu_parts(gc)
        du_ref[...] = (dt * act).astype(BF16)
        dgc = dt * u_ref[...] * dact
        dgc_ref[...] = dgc
        st_ref[0:1, :] += jnp.sum(gm1 * dgc, axis=0, keepdims=True)
        st_ref[1:2, :] += jnp.sum(gv * dgc, axis=0, keepdims=True)
        st_ref[2:3, :] += jnp.sum(gp1 * dgc, axis=0, keepdims=True)
        st_ref[3:4, :] += jnp.sum(dgc, axis=0, keepdims=True)

    tile = pl.BlockSpec((t, D_FF), lambda i: (i, 0))
    return pl.pallas_call(
        body, name="conv_bwd_a", grid=(n,),
        in_specs=[pl.BlockSpec((t, D_MODEL), lambda i: (i, 0)), pl.BlockSpec((D_FF, D_MODEL), lambda i: (0, 0))]
        + _halo_specs(t, s, D_FF) + [tile, pl.BlockSpec((3, D_FF), lambda i: (0, 0)),
                                     pl.BlockSpec((1, D_FF), lambda i: (0, 0))],
        out_specs=[tile, tile, pl.BlockSpec((8, D_FF), lambda i: (0, 0))],
        out_shape=[jax.ShapeDtypeStruct((s, D_FF), BF16), jax.ShapeDtypeStruct((s, D_FF), F32),
                   jax.ShapeDtypeStruct((8, D_FF), F32)],
        compiler_params=_cparams(dimension_semantics=("arbitrary",)),
    )(dr3b, w_down_b, g, g, g, u, conv_w, conv_b)


def _conv_bwd_b(dgc, conv_w, *, t):
    s = dgc.shape[0]
    n = s // t

    def body(d_ref, dp_ref, dn_ref, cw_ref, o_ref):
        i = pl.program_id(0)
        dv = d_ref[...]
        prev_row, next_row = _halo_rows(i, n, dp_ref, dn_ref)
        dm1, dp1 = _shift_rows(dv, prev_row, next_row)
        o_ref[...] = (dp1 * cw_ref[0:1, :] + dv * cw_ref[1:2, :] + dm1 * cw_ref[2:3, :]).astype(BF16)

    return pl.pallas_call(
        body, name="conv_bwd_b", grid=(n,),
        in_specs=_halo_specs(t, s, D_FF) + [pl.BlockSpec((3, D_FF), lambda i: (0, 0))],
        out_specs=pl.BlockSpec((t, D_FF), lambda i: (i, 0)), out_shape=jax.ShapeDtypeStruct((s, D_FF), BF16),
        compiler_params=_cparams(dimension_semantics=("parallel",)),
    )(dgc, dgc, dgc, conv_w)


def _to_residue(a, dil):
    s, w = a.shape
    return a.reshape(s // dil, dil, w).transpose(1, 0, 2)


def _from_residue(a):
    dil, l, w = a.shape
    return a.transpose(1, 0, 2).reshape(dil * l, w)


def _stats_to_lanes(rows):
    dil, hq, l = rows.shape
    return jnp.pad(rows.transpose(2, 0, 1).reshape(dil * l, hq), ((0, 0), (0, LANES - hq)))


def _stats_to_rows(lanes, dil):
    s = lanes.shape[0]
    return lanes[:, :DIL_SLOTS].reshape(s // dil, dil, DIL_SLOTS).transpose(1, 2, 0)


def _rope_angles(positions):
    inv_freq = ROPE_THETA ** (-jnp.arange(0, ROT_DIM, 2, dtype=F32) / ROT_DIM)
    ang = positions.astype(F32)[:, None] * inv_freq
    return jnp.concatenate([jnp.cos(ang), jnp.sin(ang)], axis=1)


class _NoPlan:
    def gather(self, stage):
        return None

    def gathered(self, stage, couts, wb):
        pass

    def exchange(self, stage, grads):
        return None

    def exchanged(self, stage, couts):
        pass


def _local_step(x, mem, positions, target, wb, sp, plan=None, *, t_row=256, t_mm=512, tq_a=256, tq_b=128,
                sub_a=2, sub_b=4):
    s = x.shape[0]
    plan = plan or _NoPlan()
    cs = _rope_angles(positions)
    e_mat = _rope_select_matrix()

    (h0b, za, *zb), couts = _proj_all(x, sp["ln_in_g"], sp["ln_in_b"], wb["w_in_seg"], cs, e_mat, t=t_mm,
                                      comm=plan.gather("proj"))
    plan.gathered("proj", couts, wb)
    sub_a = max(1, min(sub_a, s // tq_a))
    subs_b = [max(1, min(sub_b, s // dil // tq_b)) for dil in DILATIONS]
    out_a, lse_a, couts = _swa_fwd_s(za, qcol=0, kcol=4, vcol=5, hq=WIN_Q_HEADS, hkv=WIN_KV_HEADS, w=WIN_HALF,
                                     tq=tq_a, sub=sub_a, sink=sp["attn_sink"], name="attn_a_fwd",
                                     comm=plan.gather("attn_a"))
    plan.gathered("attn_a", couts, wb)
    o_g, lse_g = [], []
    for gi in range(3):
        o, l, couts = _swa_fwd_s(zb[gi], qcol=0, kcol=1, vcol=2, hq=DIL_SLOTS, hkv=DIL_SLOTS, w=DIL_HALF, tq=tq_b,
                                 sub=subs_b[gi], sink=None, name=f"attn_b{gi}_fwd",
                                 comm=plan.gather(f"attn_b{gi}"))
        plan.gathered(f"attn_b{gi}", couts, wb)
        o_g.append(o)
        lse_g.append(_stats_to_lanes(l))
    mixed_b, out_b, lse_b = _combine_fwd(out_a, o_g, lse_g, sp["g_win"], sp["g_dil"], t=t_row)
    r1, h1b = _mixproj_fwd(mixed_b, wb["w_mix_out"], x, sp["ln_in_g"], sp["ln_in_b"], sp["ln1_g"], sp["ln1_b"],
                           t=t_row)
    mem_nb, kx, vx = _mem_fwd(mem, sp["mem_ln_g"], sp["mem_ln_b"], wb["w_xk"], wb["w_xv"])
    r2, h2b, qxb, oxb, lse_x = _xattn_fwd(h1b, r1, kx, vx, wb["w_xq"], wb["w_xo"], sp["ln1_g"], sp["ln1_b"],
                                          sp["ln2_g"], sp["ln2_b"], t=t_row)
    g = _mm(h2b, wb["w_gate"], mode="nn", out_dtype=F32, tm=t_mm, tn=D_FF, name="ff_gate")
    u = _mm(h2b, wb["w_up"], mode="nn", out_dtype=F32, tm=t_mm, tn=D_FF, name="ff_up")
    tb, dr3, dr3b, st3 = _ffn_out(g, u, sp["conv_w"], sp["conv_b"], wb["w_down"], r2, target, sp["ln2_g"],
                                  sp["ln2_b"], sp["ln3_g"], sp["ln3_b"], t=t_row)

    grads = {}
    du, dgc, st_conv = _conv_bwd_a(dr3b, wb["w_down"], g, u, sp["conv_w"], sp["conv_b"], t=t_row)
    tk = min(1024, s)
    grads["w_down"] = _mm(tb, dr3b, mode="tn", out_dtype=BF16, tm=D_FF // 2, tn=D_MODEL, tk=tk, name="dw_down")
    grads["w_up"] = _mm(h2b, du, mode="tn", out_dtype=BF16, tm=D_MODEL, tn=D_FF // 2, tk=tk, name="dw_up")
    (dg, dr2, dr2b, st2), couts = _dh2_ln2(dgc, sp["conv_w"], du, wb["w_gate"], wb["w_up"], dr3, r2, sp["ln2_g"],
                                           t=t_row, comm=plan.exchange("dh2", grads))
    plan.exchanged("dh2", couts)
    grads["w_gate"] = _mm(h2b, dg, mode="tn", out_dtype=BF16, tm=D_MODEL, tn=D_FF // 2, tk=tk, name="dw_gate")

    (dr1, dr1b, dqxb, dkx, dvx, st1), couts = _xattn_bwd(
        dr2, qxb, oxb, lse_x, kx, vx, wb["w_xq"], wb["w_xo"], r1, sp["ln1_g"], t=t_row,
        comm=plan.exchange("xattn", grads))
    plan.exchanged("xattn", couts)
    grads["w_xo"] = _mm(oxb, dr2b, mode="tn", out_dtype=BF16, tm=D_MODEL, tn=D_MODEL, tk=tk, name="dw_xo")
    grads["w_xq"] = _mm(h1b, dqxb, mode="tn", out_dtype=BF16, tm=D_MODEL, tn=D_MODEL, tk=tk, name="dw_xq")
    grads["w_xk"], grads["w_xv"], st_mem = _mem_bwd(dkx, dvx, mem, sp["mem_ln_g"], sp["mem_ln_b"],
                                                    wb["w_xk"], wb["w_xv"])

    grads["w_mix_out"] = _mm(mixed_b, dr1b, mode="tn", out_dtype=BF16, tm=D_MODEL, tn=D_MODEL, tk=tk,
                             name="dw_mix")
    dmixed = _mm(dr1b, wb["w_mix_out"], mode="nt", out_dtype=F32, tm=t_mm, tn=D_MODEL, name="dmixed")
    do_a, do_b, dl_a, dl_b, st_mix = _combine_bwd(dmixed, out_a, out_b, sp["g_win"], sp["g_dil"], t=t_row)
    (dqa, dka, dva, dsink), couts = _swa_bwd_s(
        za, do_a, lse_a, _stats_to_rows(dl_a, 1), cs[None], e_mat, qcol=0, kcol=4, vcol=5, hq=WIN_Q_HEADS,
        hkv=WIN_KV_HEADS, w=WIN_HALF, tq=tq_a, sub=sub_a, sink=sp["attn_sink"], name="attn_a_bwd",
        comm=plan.exchange("attn_a", grads))
    plan.exchanged("attn_a", couts)
    dqs, dks, dvs = [], [], []
    for gi, dil in enumerate(DILATIONS):
        (dq, dk, dv), _ = _swa_bwd_s(
            zb[gi], do_b[gi], _stats_to_rows(lse_b, dil), _stats_to_rows(dl_b, dil),
            _to_residue(cs, dil), e_mat, qcol=0, kcol=1, vcol=2, hq=DIL_SLOTS, hkv=DIL_SLOTS, w=DIL_HALF, tq=tq_b,
            sub=subs_b[gi], sink=None, name=f"attn_b{gi}_bwd")
        dqs.append(dq)
        dks.append(dk)
        dvs.append(dv)
    dz = _assemble_dz(dqa, dka, dva, dqs, dks, dvs, t=t_row)
    grads["w_in"] = _mm(h0b, dz, mode="tn", out_dtype=BF16, tm=D_MODEL, tn=IN_WIDTH // 7, tk=tk, name="dw_in")
    comm = plan.exchange("dh0", grads)
    dh0 = _mm(dz, wb["w_in"], mode="nt", out_dtype=F32, tm=t_mm, tn=D_MODEL, add=dr1, add_scale=ALPHA, name="dh0",
              comm=comm)
    if comm is not None:
        dh0, couts = dh0
        plan.exchanged("dh0", couts)
    grad_x, st0 = _ln_bwd(dh0, x, sp["ln_in_g"], t=t_row, name="ln_in_bwd", want_bf16=False)

    small = {
        "loss": st3[2:3, 0:1],
        "ln_in_g": st0[0:1], "ln_in_b": st0[1:2],
        "attn_sink": dsink[:, 0].reshape(1, WIN_Q_HEADS),
        "g_win": st_mix[0:1], "g_dil": st_mix[1:2],
        "ln1_g": st1[0:1], "ln1_b": st1[1:2],
        "mem_ln_g": st_mem[0:1], "mem_ln_b": st_mem[1:2],
        "ln2_g": st2[0:1], "ln2_b": st2[1:2],
        "conv_w": st_conv[0:3], "conv_b": st_conv[3:4],
        "ln3_g": st3[0:1], "ln3_b": st3[1:2],
    }
    return grad_x, grads, small


def _swap_sibling(arrays):
    n = len(arrays)

    def body(*refs):
        src, dst = refs[:n], refs[n:2 * n]
        send_sems, recv_sems = refs[2 * n:]
        x, y, c, _ = _place()
        copies = [pltpu.make_async_remote_copy(
            src_ref=src[a], dst_ref=dst[a], send_sem=send_sems.at[a], recv_sem=recv_sems.at[a],
            device_id=(x, y, 1 - c), device_id_type=MESH_IDS) for a in range(n)]
        for cp in copies:
            cp.start()
        for cp in copies:
            cp.wait_recv()
        for cp in copies:
            cp.wait_send()

    return pl.pallas_call(
        body, name="swap_sibling", in_specs=[ANY] * n, out_specs=[ANY] * n,
        out_shape=[jax.ShapeDtypeStruct(a.shape, a.dtype) for a in arrays],
        scratch_shapes=[pltpu.SemaphoreType.DMA((n,)), pltpu.SemaphoreType.DMA((n,))],
    )(*arrays)


def _row_tile(rows, cols, itemsize=4, budget=1 << 20):
    best = None
    for t in range(16, rows + 1, 16):
        if rows % t == 0 and t * cols * itemsize <= budget:
            best = t
    return best or rows


def _sum_slots(stack, *, name):
    n, r, c = stack.shape
    t = _row_tile(r, c)

    def body(s_ref, o_ref):
        acc = s_ref[0].astype(F32)
        for q in range(1, n):
            acc = acc + s_ref[q].astype(F32)
        o_ref[...] = acc

    return pl.pallas_call(
        body, name=name, grid=(r // t,), in_specs=[pl.BlockSpec((n, t, c), lambda i: (0, i, 0))],
        out_specs=pl.BlockSpec((t, c), lambda i: (i, 0)), out_shape=jax.ShapeDtypeStruct((r, c), F32),
        compiler_params=_cparams(dimension_semantics=("parallel",)),
    )(stack)


def _adamw(w, m, v, p, q, *, name):
    r, c = w.shape
    t = _row_tile(r, c, budget=1 << 19)

    def body(*refs):
        if q is None:
            w_ref, m_ref, v_ref, p_ref, g_ref, d_ref, nm_ref, nv_ref = refs
            g = p_ref[...]
        else:
            w_ref, m_ref, v_ref, p_ref, q_ref, g_ref, d_ref, nm_ref, nv_ref = refs
            g = p_ref[...] + q_ref[...]
        nm = ADAM_B1 * m_ref[...] + (1.0 - ADAM_B1) * g
        nv = ADAM_B2 * v_ref[...] + (1.0 - ADAM_B2) * (g * g)
        m_hat = nm / (1.0 - ADAM_B1 ** ADAM_STEP)
        v_hat = nv / (1.0 - ADAM_B2 ** ADAM_STEP)
        g_ref[...] = g
        d_ref[...] = -ADAM_LR * (m_hat / (jnp.sqrt(v_hat) + ADAM_EPS) + ADAM_WD * w_ref[...])
        nm_ref[...] = nm
        nv_ref[...] = nv

    tile = pl.BlockSpec((t, c), lambda i: (i, 0))
    args = [w, m, v, p] + ([] if q is None else [q])
    sh = jax.ShapeDtypeStruct((r, c), F32)
    return pl.pallas_call(
        body, name=name, grid=(r // t,), in_specs=[tile] * len(args), out_specs=[tile] * 4, out_shape=[sh] * 4,
        compiler_params=_cparams(dimension_semantics=("parallel",)),
    )(*args)


BIG = ("w_in", "w_mix_out", "w_xq", "w_xk", "w_xv", "w_xo", "w_gate", "w_up", "w_down")
COL_SHARDED = ("w_in", "w_gate", "w_up")
WEIGHTS = ("ln_in_g", "ln_in_b", "w_in", "attn_sink", "g_win", "g_dil", "w_mix_out", "ln1_g", "ln1_b",
           "mem_ln_g", "mem_ln_b", "w_xq", "w_xk", "w_xv", "w_xo", "ln2_g", "ln2_b", "w_gate", "w_up",
           "conv_w", "conv_b", "w_down", "ln3_g", "ln3_b")
SMALL = tuple(k for k in WEIGHTS if k not in BIG)
PACK_COLS = 1024
CONV_SHARD = D_FF // N_CHIPS
CONV_WIDTH_ROWS = 3
SMALL_ROWS = 32


GATHER_STAGES = {"proj": ("w_mix_out", "w_xq", "w_xk", "w_xv", "w_xo"), "attn_a": ("w_gate", "w_up"),
                 "attn_b0": ("w_down",)}
EXCHANGE_STAGES = {"dh2": ("w_down", "w_up"), "xattn": ("w_gate",),
                   "attn_a": ("w_xo", "w_xq", "w_xk", "w_xv", "w_mix_out"), "dh0": ("w_in",)}


def _full_weight(k, g4):
    if k in COL_SHARDED:
        return g4.transpose(1, 0, 2).reshape(g4.shape[1], N_CHIPS * g4.shape[2])
    return g4.reshape(N_CHIPS * g4.shape[1], g4.shape[2])


def _grad_parts(k, gk):
    gk = gk.astype(BF16)
    if k in COL_SHARDED:
        return gk.reshape(gk.shape[0], N_CHIPS, gk.shape[1] // N_CHIPS).transpose(1, 0, 2)
    return gk.reshape(N_CHIPS, gk.shape[0] // N_CHIPS, gk.shape[1])


class _Plan:
    def __init__(self, shards):
        self.shards = shards
        self.recv = {}

    def gather(self, stage):
        names = GATHER_STAGES.get(stage)
        return _ChipGather([self.shards[k] for k in names]) if names else None

    def gathered(self, stage, couts, wb):
        for k, g4 in zip(GATHER_STAGES.get(stage, ()), couts):
            wb[k] = _full_weight(k, g4)

    def exchange(self, stage, grads):
        names = EXCHANGE_STAGES.get(stage)
        return _ChipExchange([_grad_parts(k, grads[k]) for k in names]) if names else None

    def exchanged(self, stage, couts):
        for k, r4 in zip(EXCHANGE_STAGES.get(stage, ()), couts):
            self.recv[k] = r4


def _pack_rows(a):
    r, n = a.shape
    per = -(-n // PACK_COLS)
    return jnp.pad(a, ((0, 0), (0, per * PACK_COLS - n))).reshape(r * per, PACK_COLS)


def _unpack_rows(p, r, n):
    per = -(-n // PACK_COLS)
    return p.reshape(r, per * PACK_COLS)[:, :n]


def _pack(pieces, rows_total):
    cat = jnp.concatenate([_pack_rows(a) for a in pieces], axis=0)
    return jnp.pad(cat, ((0, rows_total - cat.shape[0]), (0, 0)))


def _unpack(p, shapes):
    out, at = [], 0
    for r, n in shapes:
        per = -(-n // PACK_COLS)
        out.append(_unpack_rows(p[at:at + r * per], r, n))
        at += r * per
    return out


def kernel(x, mem, positions, ln_in_g, ln_in_b, w_in, attn_sink, g_win, g_dil, w_mix_out, ln1_g, ln1_b, mem_ln_g, mem_ln_b, w_xq, w_xk, w_xv, w_xo, ln2_g, ln2_b, w_gate, w_up, conv_w, conv_b, w_down, ln3_g, ln3_b, loss_target, m_ln_in_g, m_ln_in_b, m_w_in, m_attn_sink, m_g_win, m_g_dil, m_w_mix_out, m_ln1_g, m_ln1_b, m_mem_ln_g, m_mem_ln_b, m_w_xq, m_w_xk, m_w_xv, m_w_xo, m_ln2_g, m_ln2_b, m_w_gate, m_w_up, m_conv_w, m_conv_b, m_w_down, m_ln3_g, m_ln3_b, v_ln_in_g, v_ln_in_b, v_w_in, v_attn_sink, v_g_win, v_g_dil, v_w_mix_out, v_ln1_g, v_ln1_b, v_mem_ln_g, v_mem_ln_b, v_w_xq, v_w_xk, v_w_xv, v_w_xo, v_ln2_g, v_ln2_b, v_w_gate, v_w_up, v_conv_w, v_conv_b, v_w_down, v_ln3_g, v_ln3_b):
    given = dict(locals())
    shape_of = {k: given[k].shape for k in WEIGHTS}
    as2d = lambda a: a.reshape(-1, a.shape[-1])
    w2 = {k: as2d(given[k]) for k in WEIGHTS}
    m2 = {k: as2d(given["m_" + k]) for k in WEIGHTS}
    v2 = {k: as2d(given["v_" + k]) for k in WEIGHTS}
    chip = 2 * lax.axis_index("x") + lax.axis_index("y")

    plan = _Plan({k: w2[k].astype(BF16) for k in BIG})
    conv_pack = jnp.pad(w2["conv_w"], ((0, 16 - CONV_WIDTH_ROWS), (0, PACK_COLS - CONV_SHARD)))
    g_in, g_conv = _comm_only(_ChipGather([plan.shards["w_in"], conv_pack]), "gather_w_in")
    w_in_full = _full_weight("w_in", g_in)
    wb = {"w_in": w_in_full,
          "w_in_seg": jnp.concatenate([w_in_full[:, a:b] for a, b in _proj_column_ranges()], axis=1)}
    conv_full = g_conv[:, :CONV_WIDTH_ROWS, :CONV_SHARD].transpose(1, 0, 2).reshape(CONV_WIDTH_ROWS, D_FF)
    sp = {k: w2[k] for k in SMALL}
    sp["conv_w"] = conv_full

    grad_x, grads, small = _local_step(x[0], mem[0], positions[0], loss_target[0], wb, sp, plan)

    small_keys = ("loss",) + SMALL
    small_shapes = [small[k].shape for k in small_keys]
    small_pack = _pack([small[k] for k in small_keys], SMALL_ROWS)
    (small_all,) = _comm_only(_ChipExchange([], small_pack), "exchange_small")
    chip_sums = [_sum_slots(plan.recv[k], name=f"sum_chips_{k}") for k in BIG]
    sibling_sums = _swap_sibling(chip_sums)
    small_sum = _sum_slots(small_all, name="sum_small")
    small_g = dict(zip(small_keys, _unpack(small_sum, small_shapes)))
    loss = small_g["loss"][0, 0]

    res = {}
    for k, p, q in zip(BIG, chip_sums, sibling_sums):
        res[k] = _adamw(w2[k], m2[k], v2[k], p, q, name=f"adamw_{k}")
    small_g["conv_w"] = lax.dynamic_slice_in_dim(small_g["conv_w"], chip * CONV_SHARD, CONV_SHARD, axis=1)
    adam_shapes = [w2[k].shape for k in SMALL]
    packs = [_pack([d[k] for k in SMALL], SMALL_ROWS) for d in (w2, m2, v2, small_g)]
    small_res = [_unpack(o, adam_shapes) for o in _adamw(*packs, None, name="adamw_small")]
    for i, k in enumerate(SMALL):
        res[k] = tuple(o[i] for o in small_res)

    outs = [loss, grad_x[None]]
    for slot in range(4):
        outs += [res[k][slot].reshape(shape_of[k]) for k in WEIGHTS]
    return tuple(outs)
```

```python
import functools
import math

import jax
import jax.numpy as jnp
from jax import lax
from jax.experimental import pallas as pl
from jax.experimental.pallas import tpu as pltpu

F32 = jnp.float32
BF16 = jnp.bfloat16

D_MODEL = 1024
HEAD_DIM = 64
WIN_Q_HEADS = 8
WIN_KV_HEADS = 2
WIN_HALF = 128
DIL_SLOTS = 8
DILATIONS = (1, 4, 16)
DIL_HALF = 64
ROT_DIM = 16
ROPE_THETA = 500000.0
X_HEADS = 4
X_HEAD_DIM = 256
D_FF = 2816
A_Q = 512
A_KV = 128
A_WIDTH = A_Q + 2 * A_KV
B_QKV = 1536
IN_WIDTH = 5376
ALPHA = 2.0 ** 0.25
LN_EPS = 1e-5
NEG_INF = -1e30
LANES = 128
N_CHIPS = 4
N_DEV = 8

ADAM_LR = 0.001
ADAM_B1 = 0.9
ADAM_B2 = 0.999
ADAM_EPS = 1e-08
ADAM_WD = 0.01
ADAM_STEP = 10

VMEM_LIMIT = 56 * 1024 * 1024


def _cparams(**kw):
    return pltpu.CompilerParams(vmem_limit_bytes=VMEM_LIMIT, **kw)


def _dot(a, b):
    return lax.dot_general(a, b, (((1,), (0,)), ((), ())), preferred_element_type=F32)


def _dot_nt(a, b):
    return lax.dot_general(a, b, (((1,), (1,)), ((), ())), preferred_element_type=F32)


def _dot_tn(a, b):
    return lax.dot_general(a, b, (((0,), (0,)), ((), ())), preferred_element_type=F32)


def _ln(x, g, b):
    mu = jnp.mean(x, axis=-1, keepdims=True)
    xc = x - mu
    var = jnp.mean(xc * xc, axis=-1, keepdims=True)
    return xc * lax.rsqrt(var + LN_EPS) * g + b


def _ln_bwd_math(dy, r, g):
    mu = jnp.mean(r, axis=-1, keepdims=True)
    xc = r - mu
    var = jnp.mean(xc * xc, axis=-1, keepdims=True)
    rstd = lax.rsqrt(var + LN_EPS)
    xhat = xc * rstd
    dxhat = dy * g
    m1 = jnp.mean(dxhat, axis=-1, keepdims=True)
    m2 = jnp.mean(dxhat * xhat, axis=-1, keepdims=True)
    dr = rstd * (dxhat - m1 - xhat * m2)
    return dr, jnp.sum(dy * xhat, axis=0, keepdims=True), jnp.sum(dy, axis=0, keepdims=True)


def _rope(z, ta, tb, tc, sign):
    w = z.shape[1]
    reps = w // LANES
    a = jnp.tile(ta, (1, reps))
    b = jnp.tile(tb, (1, reps))
    c = jnp.tile(tc, (1, reps))
    return z * a + sign * (pltpu.roll(z, w - 8, 1) * b + pltpu.roll(z, 8, 1) * c)


def _shift_rows(x, prev_row, next_row):
    t = x.shape[0]
    row = lax.broadcasted_iota(jnp.int32, x.shape, 0)
    xm1 = jnp.where(row == 0, prev_row, pltpu.roll(x, 1, 0))
    xp1 = jnp.where(row == t - 1, next_row, pltpu.roll(x, t - 1, 0))
    return xm1, xp1


def _rope_tabs(cs, e_mat):
    tabs = lax.dot_general(cs, e_mat, (((1,), (0,)), ((), ())), preferred_element_type=F32,
                           precision=lax.Precision.HIGHEST)
    lane = lax.broadcasted_iota(jnp.int32, (cs.shape[0], LANES), 1)
    ones = jnp.where((lane & (HEAD_DIM - 1)) >= ROT_DIM, 1.0, 0.0)
    return tabs[:, :LANES] + ones, tabs[:, LANES:2 * LANES], tabs[:, 2 * LANES:]


def _rope_select_matrix():
    half = ROT_DIM // 2
    e = [[0.0] * (3 * LANES) for _ in range(ROT_DIM)]
    for lane in range(LANES):
        d = lane % HEAD_DIM
        if d < half:
            e[d][lane] = 1.0
            e[half + d][LANES + lane] = -1.0
        elif d < ROT_DIM:
            e[d - half][lane] = 1.0
            e[d][2 * LANES + lane] = 1.0
    return jnp.array(e, F32)


MESH_IDS = pl.DeviceIdType.MESH
ANY = pl.BlockSpec(memory_space=pl.ANY)


def _place():
    x, y, c = lax.axis_index("x"), lax.axis_index("y"), lax.axis_index("c")
    other_chips = [(1 - x, y), (x, 1 - y), (1 - x, 1 - y)]
    return x, y, c, other_chips


class _ChipGather:
    def __init__(self, shards):
        self.inputs = list(shards)
        n = len(shards)
        self.out_shape = [jax.ShapeDtypeStruct((N_CHIPS,) + a.shape, a.dtype) for a in shards]
        self.scratch = [pltpu.SemaphoreType.DMA((6 * n,)), pltpu.SemaphoreType.DMA((6 * n,)),
                        pltpu.SemaphoreType.DMA((n,))]

    def _copies(self, src, dst, sems):
        send_sems, recv_sems, local_sems = sems
        x, y, c, chips = _place()
        mine = 2 * x + y
        n = len(src)
        local, sends, recvs, passes, pass_recvs = [], [], [], [], []
        for a in range(n):
            half = src[a].shape[0] // 2
            my_rows, other_rows = pl.ds(c * half, half), pl.ds((1 - c) * half, half)
            local.append(pltpu.make_async_copy(src[a], dst[a].at[mine], local_sems.at[a]))
            for j, (px, py) in enumerate(chips):
                k, k2, slot = 3 * a + j, 3 * n + 3 * a + j, 2 * px + py
                sends.append(pltpu.make_async_remote_copy(
                    src_ref=src[a].at[my_rows], dst_ref=dst[a].at[mine, my_rows], send_sem=send_sems.at[k],
                    recv_sem=recv_sems.at[k], device_id=(px, py, c), device_id_type=MESH_IDS))
                recvs.append(pltpu.make_async_remote_copy(
                    src_ref=src[a].at[my_rows], dst_ref=dst[a].at[slot, my_rows], send_sem=send_sems.at[k],
                    recv_sem=recv_sems.at[k], device_id=(px, py, c), device_id_type=MESH_IDS))
                passes.append(pltpu.make_async_remote_copy(
                    src_ref=dst[a].at[slot, my_rows], dst_ref=dst[a].at[slot, my_rows], send_sem=send_sems.at[k2],
                    recv_sem=recv_sems.at[k2], device_id=(x, y, 1 - c), device_id_type=MESH_IDS))
                pass_recvs.append(pltpu.make_async_remote_copy(
                    src_ref=dst[a].at[slot, my_rows], dst_ref=dst[a].at[slot, other_rows],
                    send_sem=send_sems.at[k2], recv_sem=recv_sems.at[k2], device_id=(x, y, 1 - c),
                    device_id_type=MESH_IDS))
        return local, sends, recvs, passes, pass_recvs

    def start(self, src, dst, sems):
        local, sends, _, _, _ = self._copies(src, dst, sems)
        for cp in local + sends:
            cp.start()

    def wait(self, src, dst, sems):
        local, sends, recvs, passes, pass_recvs = self._copies(src, dst, sems)
        for idx, landed in enumerate(recvs):
            landed.wait_recv()
            if passes:
                passes[idx].start()
        for cp in pass_recvs:
            cp.wait_recv()
        for cp in sends + passes:
            cp.wait_send()
        for cp in local:
            cp.wait()


class _ChipExchange:
    def __init__(self, parts, small=None):
        self.inputs = list(parts) + ([small] if small is not None else [])
        self.n = len(parts)
        self.has_small = small is not None
        self.out_shape = [jax.ShapeDtypeStruct(a.shape, a.dtype) for a in parts]
        n_sem, n_loc = 3 * self.n, self.n
        if self.has_small:
            self.out_shape.append(jax.ShapeDtypeStruct((N_DEV,) + small.shape, small.dtype))
            n_sem, n_loc = n_sem + N_DEV - 1, n_loc + 1
        self.scratch = [pltpu.SemaphoreType.DMA((n_sem,)), pltpu.SemaphoreType.DMA((n_sem,)),
                        pltpu.SemaphoreType.DMA((n_loc,))]

    def _copies(self, src, dst, sems):
        send_sems, recv_sems, local_sems = sems
        x, y, c, chips = _place()
        mine = 2 * x + y
        n = self.n
        local, sends, recvs = [], [], []
        for a in range(n):
            local.append(pltpu.make_async_copy(src[a].at[mine], dst[a].at[mine], local_sems.at[a]))
            for j, (px, py) in enumerate(chips):
                k = 3 * a + j
                sends.append(pltpu.make_async_remote_copy(
                    src_ref=src[a].at[2 * px + py], dst_ref=dst[a].at[mine], send_sem=send_sems.at[k],
                    recv_sem=recv_sems.at[k], device_id=(px, py, c), device_id_type=MESH_IDS))
                recvs.append(pltpu.make_async_remote_copy(
                    src_ref=src[a].at[mine], dst_ref=dst[a].at[2 * px + py], send_sem=send_sems.at[k],
                    recv_sem=recv_sems.at[k], device_id=(px, py, c), device_id_type=MESH_IDS))
        if self.has_small:
            me_dev = 4 * x + 2 * y + c
            local.append(pltpu.make_async_copy(src[n], dst[n].at[me_dev], local_sems.at[n]))
            for mask in range(1, N_DEV):
                px, py, pc = x ^ ((mask >> 2) & 1), y ^ ((mask >> 1) & 1), c ^ (mask & 1)
                k = 3 * n + mask - 1
                sends.append(pltpu.make_async_remote_copy(
                    src_ref=src[n], dst_ref=dst[n].at[me_dev], send_sem=send_sems.at[k], recv_sem=recv_sems.at[k],
                    device_id=(px, py, pc), device_id_type=MESH_IDS))
                recvs.append(pltpu.make_async_remote_copy(
                    src_ref=src[n], dst_ref=dst[n].at[4 * px + 2 * py + pc], send_sem=send_sems.at[k],
                    recv_sem=recv_sems.at[k], device_id=(px, py, pc), device_id_type=MESH_IDS))
        return local, sends, recvs, [], []

    start = _ChipGather.start
    wait = _ChipGather.wait


def _pcall(body, *, name, grid, in_specs, out_specs, out_shape, args, scratch_shapes=(), dims=None, comm=None):
    in_specs, out_specs, out_shape = list(in_specs), list(out_specs), list(out_shape)
    scratch_shapes = list(scratch_shapes)
    if comm is None:
        outs = pl.pallas_call(
            body, name=name, grid=grid, in_specs=in_specs, out_specs=out_specs, out_shape=out_shape,
            scratch_shapes=scratch_shapes, compiler_params=_cparams(dimension_semantics=dims),
        )(*args)
        return list(outs), []
    n_in, n_out, n_scr = len(in_specs), len(out_specs), len(scratch_shapes)
    n_cin, n_cout = len(comm.inputs), len(comm.out_shape)

    def wrapped(*refs):
        ins, refs = refs[:n_in], refs[n_in:]
        cins, refs = refs[:n_cin], refs[n_cin:]
        outs, refs = refs[:n_out], refs[n_out:]
        couts, refs = refs[:n_cout], refs[n_cout:]
        scr, csems = refs[:n_scr], refs[n_scr:]
        first = last = None
        for axis, size in enumerate(grid):
            pid = pl.program_id(axis)
            f, l = pid == 0, pid == size - 1
            first = f if first is None else first & f
            last = l if last is None else last & l

        @pl.when(first)
        def _():
            comm.start(cins, couts, csems)

        body(*ins, *outs, *scr)

        @pl.when(last)
        def _():
            comm.wait(cins, couts, csems)

    res = pl.pallas_call(
        wrapped, name=name, grid=grid, in_specs=in_specs + [ANY] * n_cin, out_specs=out_specs + [ANY] * n_cout,
        out_shape=out_shape + list(comm.out_shape), scratch_shapes=scratch_shapes + list(comm.scratch),
        compiler_params=_cparams(dimension_semantics=("arbitrary",) * len(grid)),
    )(*args, *comm.inputs)
    return list(res[:n_out]), list(res[n_out:])


def _comm_only(comm, name):
    def body(*refs):
        n_cin, n_cout = len(comm.inputs), len(comm.out_shape)
        cins, couts, csems = refs[:n_cin], refs[n_cin:n_cin + n_cout], refs[n_cin + n_cout:]
        comm.start(cins, couts, csems)
        comm.wait(cins, couts, csems)

    return list(pl.pallas_call(
        body, name=name, in_specs=[ANY] * len(comm.inputs), out_specs=[ANY] * len(comm.out_shape),
        out_shape=list(comm.out_shape), scratch_shapes=list(comm.scratch),
    )(*comm.inputs))


def _mm(a, b, *, mode, out_dtype, tm, tn, tk=None, add=None, add_scale=1.0, name, comm=None):
    if mode in ("nn", "nt"):
        m, k = a.shape
        n = b.shape[1] if mode == "nn" else b.shape[0]
        assert m % tm == 0 and n % tn == 0
        dot = _dot if mode == "nn" else _dot_nt

        def body(*refs):
            if add is None:
                a_ref, b_ref, o_ref = refs
                o_ref[...] = dot(a_ref[...], b_ref[...]).astype(out_dtype)
            else:
                a_ref, b_ref, c_ref, o_ref = refs
                o_ref[...] = (dot(a_ref[...], b_ref[...]) + add_scale * c_ref[...]).astype(out_dtype)

        b_spec = (pl.BlockSpec((k, tn), lambda i, j: (0, j)) if mode == "nn"
                  else pl.BlockSpec((tn, k), lambda i, j: (j, 0)))
        in_specs = [pl.BlockSpec((tm, k), lambda i, j: (i, 0)), b_spec]
        args = [a, b]
        if add is not None:
            in_specs.append(pl.BlockSpec((tm, tn), lambda i, j: (i, j)))
            args.append(add)
        outs, couts = _pcall(
            body, name=name, grid=(m // tm, n // tn), in_specs=in_specs,
            out_specs=[pl.BlockSpec((tm, tn), lambda i, j: (i, j))],
            out_shape=[jax.ShapeDtypeStruct((m, n), out_dtype)], args=args, dims=("parallel", "parallel"),
            comm=comm)
        return outs[0] if comm is None else (outs[0], couts)
    assert mode == "tn" and add is None and comm is None
    kk, m = a.shape
    n = b.shape[1]
    assert m % tm == 0 and n % tn == 0 and kk % tk == 0
    nk = kk // tk

    def body(a_ref, b_ref, o_ref, acc_ref):
        kstep = pl.program_id(2)

        @pl.when(kstep == 0)
        def _():
            acc_ref[...] = jnp.zeros_like(acc_ref)

        acc_ref[...] += _dot_tn(a_ref[...], b_ref[...])

        @pl.when(kstep == nk - 1)
        def _():
            o_ref[...] = acc_ref[...].astype(out_dtype)

    return pl.pallas_call(
        body, name=name, grid=(m // tm, n // tn, nk),
        in_specs=[pl.BlockSpec((tk, tm), lambda i, j, s: (s, i)), pl.BlockSpec((tk, tn), lambda i, j, s: (s, j))],
        out_specs=pl.BlockSpec((tm, tn), lambda i, j, s: (i, j)),
        out_shape=jax.ShapeDtypeStruct((m, n), out_dtype),
        scratch_shapes=[pltpu.VMEM((tm, tn), F32)],
        compiler_params=_cparams(dimension_semantics=("parallel", "parallel", "arbitrary")),
    )(a, b)


def _mm2_nt(a1, b1, a2, b2, add, *, add_scale, tm, name, comm=None):
    m, k = a1.shape
    n = b1.shape[0]

    def body(a1_ref, b1_ref, a2_ref, b2_ref, c_ref, o_ref):
        o_ref[...] = (_dot_nt(a1_ref[...], b1_ref[...]) + _dot_nt(a2_ref[...], b2_ref[...])
                      + add_scale * c_ref[...])

    a_spec = pl.BlockSpec((tm, k), lambda i: (i, 0))
    b_spec = pl.BlockSpec((n, k), lambda i: (0, 0))
    o_spec = pl.BlockSpec((tm, n), lambda i: (i, 0))
    outs, couts = _pcall(body, name=name, grid=(m // tm,), in_specs=[a_spec, b_spec, a_spec, b_spec, o_spec],
                         out_specs=[o_spec], out_shape=[jax.ShapeDtypeStruct((m, n), F32)],
                         args=[a1, b1, a2, b2, add], dims=("parallel",), comm=comm)
    return outs[0], couts


def _ln_bwd(dy, r, g, *, t, name, want_bf16):
    s = r.shape[0]

    def body(dy_ref, r_ref, g_ref, *outs):
        i = pl.program_id(0)
        dr, dg, db = _ln_bwd_math(dy_ref[...], r_ref[...], g_ref[...])
        outs[0][...] = dr
        if want_bf16:
            outs[1][...] = dr.astype(BF16)
        st_ref = outs[-1]

        @pl.when(i == 0)
        def _():
            st_ref[...] = jnp.zeros_like(st_ref)

        st_ref[0:1, :] += dg
        st_ref[1:2, :] += db

    tile = pl.BlockSpec((t, D_MODEL), lambda i: (i, 0))
    out_specs = [tile] + ([tile] if want_bf16 else []) + [pl.BlockSpec((8, D_MODEL), lambda i: (0, 0))]
    out_shape = ([jax.ShapeDtypeStruct((s, D_MODEL), F32)]
                 + ([jax.ShapeDtypeStruct((s, D_MODEL), BF16)] if want_bf16 else [])
                 + [jax.ShapeDtypeStruct((8, D_MODEL), F32)])
    return pl.pallas_call(
        body, name=name, grid=(s // t,),
        in_specs=[tile, tile, pl.BlockSpec((1, D_MODEL), lambda i: (0, 0))],
        out_specs=out_specs, out_shape=out_shape,
        compiler_params=_cparams(dimension_semantics=("arbitrary",)),
    )(dy, r, g)


PROJ_COLS = 256


def _proj_segments():
    wd = DIL_SLOTS * HEAD_DIM
    segs = [(1, [(0, 1), (PROJ_COLS, 1), (2 * PROJ_COLS, 2)])]
    for gi, dil in enumerate(DILATIONS):
        blocks = []
        for part, kind in enumerate((1, 1, 0)):
            col = A_WIDTH + part * B_QKV + gi * wd
            blocks += [(col, kind), (col + PROJ_COLS, kind)]
        segs.append((dil, blocks))
    return segs


PROJ_SEGMENTS = _proj_segments()


def _proj_all(x, g, b, w_t, cs, e_mat, *, t, comm=None):
    s = x.shape[0]
    cb = PROJ_COLS
    halves = cb // LANES

    def body(x_ref, g_ref, b_ref, w_ref, cs_ref, e_ref, h_ref, *rest):
        z_refs, scr = rest[:-1], rest[-1]
        h = _ln(x_ref[...], g_ref[...], b_ref[...]).astype(BF16)
        h_ref[...] = h
        ta, tb, tc = (jnp.tile(tab, (1, halves)) for tab in _rope_tabs(cs_ref[...], e_ref[...]))
        lane = lax.broadcasted_iota(jnp.int32, (t, cb), 1)
        slot = 0
        for z_ref, (dil, blocks) in zip(z_refs, PROJ_SEGMENTS):
            for jb, (col, kind) in enumerate(blocks):
                acc = _dot_nt(h, w_ref[col:col + cb, :])
                if kind:
                    z = acc * ta + (pltpu.roll(acc, cb - 8, 1) * tb + pltpu.roll(acc, 8, 1) * tc)
                    if kind == 2:
                        z = jnp.where(lane < LANES, z, acc)
                else:
                    z = acc
                if dil == 1:
                    z_ref[0, :, cb * jb:cb * (jb + 1)] = z.astype(BF16)
                    continue
                for half in range(halves):
                    scr[slot, half] = z[:, half * LANES:(half + 1) * LANES]
                for c in range(dil):
                    for half in range(halves):
                        rows = scr[slot, half, pl.ds(c, t // dil, stride=dil), :]
                        z_ref[c, :, cb * jb + half * LANES:cb * jb + (half + 1) * LANES] = rows.astype(BF16)
                slot = 1 - slot

    row = pl.BlockSpec((1, D_MODEL), lambda i: (0, 0))
    widths = [cb * len(blocks) for _, blocks in PROJ_SEGMENTS]
    dils = [dil for dil, _ in PROJ_SEGMENTS]
    outs, couts = _pcall(
        body, name="proj_all", grid=(s // t,),
        in_specs=[pl.BlockSpec((t, D_MODEL), lambda i: (i, 0)), row, row,
                  pl.BlockSpec((IN_WIDTH, D_MODEL), lambda i: (0, 0)),
                  pl.BlockSpec((t, ROT_DIM), lambda i: (i, 0)), pl.BlockSpec((ROT_DIM, 3 * LANES), lambda i: (0, 0))],
        out_specs=[pl.BlockSpec((t, D_MODEL), lambda i: (i, 0))]
        + [pl.BlockSpec((dil, t // dil, wd), lambda i: (0, i, 0)) for dil, wd in zip(dils, widths)],
        out_shape=[jax.ShapeDtypeStruct((s, D_MODEL), BF16)]
        + [jax.ShapeDtypeStruct((dil, s // dil, wd), BF16) for dil, wd in zip(dils, widths)],
        args=[x, g, b, w_t, cs, e_mat], scratch_shapes=[pltpu.VMEM((2, halves, t, LANES), F32)],
        dims=("parallel",), comm=comm)
    return outs, couts


def _window_mask(i, tq, w, seq_len):
    tk = tq + 2 * w
    qpos = i * tq + lax.broadcasted_iota(jnp.int32, (tq, tk), 0)
    kpos = i * tq - w + lax.broadcasted_iota(jnp.int32, (tq, tk), 1)
    return (jnp.abs(qpos - kpos) <= w) & (kpos >= 0) & (kpos < seq_len)


def _swa_specs(tq, hq, hkv, n, qcol, kcol, vcol):
    qw, kw = hq * HEAD_DIM, hkv * HEAD_DIM
    cur = lambda s, i: jnp.minimum(i, n - 1)
    prv = lambda s, i: jnp.maximum(jnp.minimum(i, n - 1) - 1, 0)
    nxt = lambda s, i: jnp.minimum(i + 1, n - 1)
    q_spec = pl.BlockSpec((None, tq, qw), lambda s, i: (s, cur(s, i), qcol))
    kv_specs = [pl.BlockSpec((None, tq, kw), (lambda s, i, f=f, c=c: (s, f(s, i), c)))
                for c in (kcol, vcol) for f in (prv, cur, nxt)]
    return q_spec, kv_specs, cur, prv


def _swa_fwd(qkv, *, qcol, kcol, vcol, hq, hkv, w, tq, sink, name, comm=None):
    nseq, seq_len, _ = qkv.shape
    n = seq_len // tq
    rep = hq // hkv
    q_spec, kv_specs, _, _ = _swa_specs(tq, hq, hkv, n, qcol, kcol, vcol)

    def body(*refs):
        if sink is not None:
            sink_ref, refs = refs[0], refs[1:]
        q_ref, kp_ref, kc_ref, kn_ref, vp_ref, vc_ref, vn_ref, o_ref, lse_ref = refs
        i = pl.program_id(1)
        mask = _window_mask(i, tq, w, seq_len)
        lane = lax.broadcasted_iota(jnp.int32, (tq, LANES), 1)
        lse_acc = jnp.zeros((tq, LANES), F32)
        for g in range(hkv):
            cs = slice(g * HEAD_DIM, (g + 1) * HEAD_DIM)
            kcat = jnp.concatenate([kp_ref[tq - w:, cs], kc_ref[:, cs], kn_ref[:w, cs]], axis=0)
            vcat = jnp.concatenate([vp_ref[tq - w:, cs], vc_ref[:, cs], vn_ref[:w, cs]], axis=0)
            for r in range(rep):
                h = g * rep + r
                hs = slice(h * HEAD_DIM, (h + 1) * HEAD_DIM)
                qh = q_ref[:, hs] * 0.125
                sc = jnp.where(mask, _dot_nt(qh, kcat), NEG_INF)
                m = jnp.max(sc, axis=1, keepdims=True)
                if sink is not None:
                    m = jnp.maximum(m, sink_ref[0, h])
                p = jnp.exp(sc - m)
                den = jnp.sum(p, axis=1, keepdims=True)
                if sink is not None:
                    den = den + jnp.exp(sink_ref[0, h] - m)
                o_ref[:, hs] = _dot(p.astype(BF16), vcat) / den
                lse_acc = jnp.where(lane == h, m + jnp.log(den), lse_acc)
        lse_ref[...] = lse_acc

    in_specs = [q_spec] + kv_specs
    args = [qkv] * 7
    if sink is not None:
        in_specs = [pl.BlockSpec(memory_space=pltpu.SMEM)] + in_specs
        args = [sink] + args
    (o, lse), couts = _pcall(
        body, name=name, grid=(nseq, n), in_specs=in_specs,
        out_specs=[pl.BlockSpec((None, tq, hq * HEAD_DIM), lambda s, i: (s, i, 0)),
                   pl.BlockSpec((None, tq, LANES), lambda s, i: (s, i, 0))],
        out_shape=[jax.ShapeDtypeStruct((nseq, seq_len, hq * HEAD_DIM), F32),
                   jax.ShapeDtypeStruct((nseq, seq_len, LANES), F32)],
        args=args, dims=("parallel", "parallel"), comm=comm)
    return o, lse, couts


def _swa_bwd(qkv, do, lse, delta, cs, e_mat, *, qcol, kcol, vcol, hq, hkv, w, tq, sink, name, comm=None):
    nseq, seq_len, _ = qkv.shape
    n = seq_len // tq
    rep = hq // hkv
    qw, kw = hq * HEAD_DIM, hkv * HEAD_DIM
    tk = tq + 2 * w
    q_spec, kv_specs, cur, prv = _swa_specs(tq, hq, hkv, n, qcol, kcol, vcol)

    def body(*refs):
        if sink is not None:
            sink_ref, refs = refs[0], refs[1:]
        (q_ref, kp_ref, kc_ref, kn_ref, vp_ref, vc_ref, vn_ref, do_ref, lse_ref, dl_ref,
         cs_c, cs_p, e_ref) = refs[:13]
        outs = refs[13:]
        if sink is not None:
            dq_ref, dk_ref, dv_ref, dsink_ref, dk_acc, dv_acc = outs
        else:
            dq_ref, dk_ref, dv_ref, dk_acc, dv_acc = outs
        s_id = pl.program_id(0)
        i = pl.program_id(1)
        slot_p, slot_c, slot_n = (i + 2) % 3, i % 3, (i + 1) % 3

        if sink is not None:
            @pl.when((s_id == 0) & (i == 0))
            def _():
                dsink_ref[...] = jnp.zeros_like(dsink_ref)

        @pl.when(i < n)
        def _():
            mask = _window_mask(i, tq, w, seq_len)
            dk_acc[slot_n] = jnp.zeros((tq, kw), F32)
            dv_acc[slot_n] = jnp.zeros((tq, kw), F32)

            @pl.when(i == 0)
            def _():
                dk_acc[slot_c] = jnp.zeros((tq, kw), F32)
                dv_acc[slot_c] = jnp.zeros((tq, kw), F32)

            dq_parts, dk_parts, dv_parts = [], [], []
            for g in range(hkv):
                cs = slice(g * HEAD_DIM, (g + 1) * HEAD_DIM)
                kcat = jnp.concatenate([kp_ref[tq - w:, cs], kc_ref[:, cs], kn_ref[:w, cs]], axis=0)
                vcat = jnp.concatenate([vp_ref[tq - w:, cs], vc_ref[:, cs], vn_ref[:w, cs]], axis=0)
                dkc = jnp.zeros((tk, HEAD_DIM), F32)
                dvc = jnp.zeros((tk, HEAD_DIM), F32)
                for r in range(rep):
                    h = g * rep + r
                    hs = slice(h * HEAD_DIM, (h + 1) * HEAD_DIM)
                    qh = q_ref[:, hs] * 0.125
                    sc = jnp.where(mask, _dot_nt(qh, kcat), NEG_INF)
                    lse_h = lse_ref[:, h:h + 1]
                    dl_h = dl_ref[:, h:h + 1]
                    p = jnp.exp(sc - lse_h)
                    doh = do_ref[:, hs]
                    dp = _dot_nt(doh, vcat)
                    dsb = (p * (dp - dl_h)).astype(BF16)
                    dq_parts.append(_dot(dsb, kcat) * 0.125)
                    dkc = dkc + _dot_tn(dsb, qh)
                    dvc = dvc + _dot_tn(p.astype(BF16), doh)
                    if sink is not None:
                        ds_sink = -jnp.sum(jnp.exp(sink_ref[0, h] - lse_h) * dl_h)
                        dsink_ref[h:h + 1, :] += jnp.full((1, LANES), ds_sink, F32)
                dk_parts.append(dkc)
                dv_parts.append(dvc)
            dq = jnp.concatenate(dq_parts, axis=1)
            dq_ref[...] = _rope(dq, *_rope_tabs(cs_c[...], e_ref[...]), -1.0).astype(BF16)
            dk_all = jnp.concatenate(dk_parts, axis=1)
            dv_all = jnp.concatenate(dv_parts, axis=1)

            @pl.when(i > 0)
            def _():
                dk_acc[slot_p, tq - w:, :] += dk_all[:w]
                dv_acc[slot_p, tq - w:, :] += dv_all[:w]

            dk_acc[slot_c] += dk_all[w:w + tq]
            dv_acc[slot_c] += dv_all[w:w + tq]
            dk_acc[slot_n, :w, :] += dk_all[w + tq:]
            dv_acc[slot_n, :w, :] += dv_all[w + tq:]

        @pl.when(i >= 1)
        def _():
            dk_ref[...] = _rope(dk_acc[slot_p], *_rope_tabs(cs_p[...], e_ref[...]), -1.0).astype(BF16)
            dv_ref[...] = dv_acc[slot_p].astype(BF16)

    row_c = lambda width: pl.BlockSpec((None, tq, width), lambda s, i: (s, cur(s, i), 0))
    row_p = lambda width: pl.BlockSpec((None, tq, width), lambda s, i: (s, jnp.maximum(i - 1, 0), 0))
    in_specs = ([q_spec] + kv_specs + [row_c(qw), row_c(LANES), row_c(LANES), row_c(ROT_DIM), row_p(ROT_DIM),
                                       pl.BlockSpec((ROT_DIM, 3 * LANES), lambda s, i: (0, 0))])
    args = [qkv] * 7 + [do, lse, delta, cs, cs, e_mat]
    out_specs = [row_c(qw), row_p(kw), row_p(kw)]
    out_shape = [jax.ShapeDtypeStruct((nseq, seq_len, qw), BF16),
                 jax.ShapeDtypeStruct((nseq, seq_len, kw), BF16),
                 jax.ShapeDtypeStruct((nseq, seq_len, kw), BF16)]
    if sink is not None:
        in_specs = [pl.BlockSpec(memory_space=pltpu.SMEM)] + in_specs
        args = [sink] + args
        out_specs.append(pl.BlockSpec((8, LANES), lambda s, i: (0, 0)))
        out_shape.append(jax.ShapeDtypeStruct((8, LANES), F32))
    return _pcall(
        body, name=name, grid=(nseq, n + 1), in_specs=in_specs, out_specs=out_specs, out_shape=out_shape,
        scratch_shapes=[pltpu.VMEM((3, tq, kw), F32), pltpu.VMEM((3, tq, kw), F32)], args=args,
        dims=("arbitrary", "arbitrary"), comm=comm)


PAIR = 2 * HEAD_DIM


def _window_mask_t(i, tq, w, seq_len):
    tk = tq + 2 * w
    kpos = i * tq - w + lax.broadcasted_iota(jnp.int32, (tk, tq), 0)
    qpos = i * tq + lax.broadcasted_iota(jnp.int32, (tk, tq), 1)
    return (jnp.abs(qpos - kpos) <= w) & (kpos >= 0) & (kpos < seq_len)


def _place_head(x2, src_pos, dst_pos):
    hi = lax.broadcasted_iota(jnp.int32, x2.shape, 1) >= HEAD_DIM
    src = x2 if src_pos == dst_pos else pltpu.roll(x2, HEAD_DIM, 1)
    return jnp.where(hi == (dst_pos == 1), src, jnp.zeros_like(src))


def _swa_fwd_t(qkv, *, qcol, kcol, vcol, hq, hkv, w, tq, sink, name, comm=None):
    nseq, seq_len, _ = qkv.shape
    n = seq_len // tq
    rep = hq // hkv
    q_spec, kv_specs, _, _ = _swa_specs(tq, hq, hkv, n, qcol, kcol, vcol)

    def body(*refs):
        if sink is not None:
            sink_ref, refs = refs[0], refs[1:]
        q_ref, kp_ref, kc_ref, kn_ref, vp_ref, vc_ref, vn_ref, o_ref, lse_ref = refs
        i = pl.program_id(1)
        mask_t = _window_mask_t(i, tq, w, seq_len)
        o_t = [None] * (hq // 2)
        lse_rows = [None] * hq
        for a in range(hkv // 2):
            ls = slice(a * PAIR, (a + 1) * PAIR)
            kcat = jnp.concatenate([kp_ref[tq - w:, ls], kc_ref[:, ls], kn_ref[:w, ls]], axis=0) * 0.125
            vcat = jnp.concatenate([vp_ref[tq - w:, ls], vc_ref[:, ls], vn_ref[:w, ls]], axis=0)
            for e in range(2):
                g = 2 * a + e
                placed = {}
                for r in range(rep):
                    h = g * rep + r
                    qp, pos = h // 2, h % 2
                    if pos not in placed:
                        placed[pos] = (_place_head(kcat, e, pos), _place_head(vcat, e, pos))
                    k_g, v_g = placed[pos]
                    s_t = jnp.where(mask_t, _dot_nt(k_g, q_ref[:, qp * PAIR:(qp + 1) * PAIR]), NEG_INF)
                    m = jnp.max(s_t, axis=0, keepdims=True)
                    if sink is not None:
                        m = jnp.maximum(m, sink_ref[0, h])
                    p_t = jnp.exp(s_t - m)
                    den = jnp.sum(p_t, axis=0, keepdims=True)
                    if sink is not None:
                        den = den + jnp.exp(sink_ref[0, h] - m)
                    part = _dot_tn(v_g, p_t.astype(BF16)) / den
                    o_t[qp] = part if o_t[qp] is None else o_t[qp] + part
                    lse_rows[h] = m + jnp.log(den)
        o_ref[...] = jnp.concatenate(o_t, axis=0).T
        lse_ref[...] = jnp.concatenate(lse_rows, axis=0)

    in_specs = [q_spec] + kv_specs
    args = [qkv] * 7
    if sink is not None:
        in_specs = [pl.BlockSpec(memory_space=pltpu.SMEM)] + in_specs
        args = [sink] + args
    (o, lse), couts = _pcall(
        body, name=name, grid=(nseq, n), in_specs=in_specs,
        out_specs=[pl.BlockSpec((None, tq, hq * HEAD_DIM), lambda s, i: (s, i, 0)),
                   pl.BlockSpec((None, hq, tq), lambda s, i: (s, 0, i))],
        out_shape=[jax.ShapeDtypeStruct((nseq, seq_len, hq * HEAD_DIM), F32),
                   jax.ShapeDtypeStruct((nseq, hq, seq_len), F32)],
        args=args, dims=("parallel", "parallel"), comm=comm)
    return o, lse, couts


def _swa_bwd_t(qkv, do, lse, delta, cs, e_mat, *, qcol, kcol, vcol, hq, hkv, w, tq, sink, name, comm=None):
    nseq, seq_len, _ = qkv.shape
    n = seq_len // tq
    rep = hq // hkv
    qw, kw = hq * HEAD_DIM, hkv * HEAD_DIM
    tk = tq + 2 * w
    q_spec, kv_specs, cur, prv = _swa_specs(tq, hq, hkv, n, qcol, kcol, vcol)

    def body(*refs):
        if sink is not None:
            sink_ref, refs = refs[0], refs[1:]
        (q_ref, kp_ref, kc_ref, kn_ref, vp_ref, vc_ref, vn_ref, do_ref, lse_ref, dl_ref,
         cs_c, cs_p, e_ref) = refs[:13]
        outs = refs[13:]
        if sink is not None:
            dq_ref, dk_ref, dv_ref, dsink_ref, dk_acc, dv_acc = outs
        else:
            dq_ref, dk_ref, dv_ref, dk_acc, dv_acc = outs
        s_id = pl.program_id(0)
        i = pl.program_id(1)
        slot_p, slot_c, slot_n = (i + 2) % 3, i % 3, (i + 1) % 3

        if sink is not None:
            @pl.when((s_id == 0) & (i == 0))
            def _():
                dsink_ref[...] = jnp.zeros_like(dsink_ref)

        @pl.when(i < n)
        def _():
            mask_t = _window_mask_t(i, tq, w, seq_len)
            dk_acc[slot_n] = jnp.zeros((tq, kw), F32)
            dv_acc[slot_n] = jnp.zeros((tq, kw), F32)

            @pl.when(i == 0)
            def _():
                dk_acc[slot_c] = jnp.zeros((tq, kw), F32)
                dv_acc[slot_c] = jnp.zeros((tq, kw), F32)

            dq_t = [None] * (hq // 2)
            dk_pairs, dv_pairs = [], []
            for a in range(hkv // 2):
                ls = slice(a * PAIR, (a + 1) * PAIR)
                kcat = jnp.concatenate([kp_ref[tq - w:, ls], kc_ref[:, ls], kn_ref[:w, ls]], axis=0) * 0.125
                vcat = jnp.concatenate([vp_ref[tq - w:, ls], vc_ref[:, ls], vn_ref[:w, ls]], axis=0)
                dk2 = jnp.zeros((tk, PAIR), F32)
                dv2 = jnp.zeros((tk, PAIR), F32)
                for e in range(2):
                    g = 2 * a + e
                    placed = {}
                    for r in range(rep):
                        h = g * rep + r
                        qp, pos = h // 2, h % 2
                        if pos not in placed:
                            placed[pos] = (_place_head(kcat, e, pos), _place_head(vcat, e, pos))
                        k_g, v_g = placed[pos]
                        q2 = q_ref[:, qp * PAIR:(qp + 1) * PAIR]
                        do2 = do_ref[:, qp * PAIR:(qp + 1) * PAIR]
                        lse_h = lse_ref[h:h + 1, :]
                        dl_h = dl_ref[h:h + 1, :]
                        p_t = jnp.exp(jnp.where(mask_t, _dot_nt(k_g, q2), NEG_INF) - lse_h)
                        dp_t = _dot_nt(v_g, do2)
                        dsb = (p_t * (dp_t - dl_h)).astype(BF16)
                        part = _dot_tn(k_g, dsb)
                        dq_t[qp] = part if dq_t[qp] is None else dq_t[qp] + part
                        dk2 = dk2 + _dot(dsb, _place_head(q2, pos, e) * 0.125)
                        dv2 = dv2 + _dot(p_t.astype(BF16), _place_head(do2, pos, e))
                        if sink is not None:
                            ds_sink = -jnp.sum(jnp.exp(sink_ref[0, h] - lse_h) * dl_h)
                            dsink_ref[h:h + 1, :] += jnp.full((1, LANES), ds_sink, F32)
                dk_pairs.append(dk2)
                dv_pairs.append(dv2)
            dq = jnp.concatenate(dq_t, axis=0).T
            dq_ref[...] = _rope(dq, *_rope_tabs(cs_c[...], e_ref[...]), -1.0).astype(BF16)
            dk_all = dk_pairs[0] if len(dk_pairs) == 1 else jnp.concatenate(dk_pairs, axis=1)
            dv_all = dv_pairs[0] if len(dv_pairs) == 1 else jnp.concatenate(dv_pairs, axis=1)

            @pl.when(i > 0)
            def _():
                dk_acc[slot_p, tq - w:, :] += dk_all[:w]
                dv_acc[slot_p, tq - w:, :] += dv_all[:w]

            dk_acc[slot_c] += dk_all[w:w + tq]
            dv_acc[slot_c] += dv_all[w:w + tq]
            dk_acc[slot_n, :w, :] += dk_all[w + tq:]
            dv_acc[slot_n, :w, :] += dv_all[w + tq:]

        @pl.when(i >= 1)
        def _():
            dk_ref[...] = _rope(dk_acc[slot_p], *_rope_tabs(cs_p[...], e_ref[...]), -1.0).astype(BF16)
            dv_ref[...] = dv_acc[slot_p].astype(BF16)

    row_c = lambda width: pl.BlockSpec((None, tq, width), lambda s, i: (s, cur(s, i), 0))
    row_p = lambda width: pl.BlockSpec((None, tq, width), lambda s, i: (s, jnp.maximum(i - 1, 0), 0))
    stat = pl.BlockSpec((None, hq, tq), lambda s, i: (s, 0, cur(s, i)))
    in_specs = ([q_spec] + kv_specs + [row_c(qw), stat, stat, row_c(ROT_DIM), row_p(ROT_DIM),
                                       pl.BlockSpec((ROT_DIM, 3 * LANES), lambda s, i: (0, 0))])
    args = [qkv] * 7 + [do, lse, delta, cs, cs, e_mat]
    out_specs = [row_c(qw), row_p(kw), row_p(kw)]
    out_shape = [jax.ShapeDtypeStruct((nseq, seq_len, qw), BF16),
                 jax.ShapeDtypeStruct((nseq, seq_len, kw), BF16),
                 jax.ShapeDtypeStruct((nseq, seq_len, kw), BF16)]
    if sink is not None:
        in_specs = [pl.BlockSpec(memory_space=pltpu.SMEM)] + in_specs
        args = [sink] + args
        out_specs.append(pl.BlockSpec((8, LANES), lambda s, i: (0, 0)))
        out_shape.append(jax.ShapeDtypeStruct((8, LANES), F32))
    return _pcall(
        body, name=name, grid=(nseq, n + 1), in_specs=in_specs, out_specs=out_specs, out_shape=out_shape,
        scratch_shapes=[pltpu.VMEM((3, tq, kw), F32), pltpu.VMEM((3, tq, kw), F32)], args=args,
        dims=("arbitrary", "arbitrary"), comm=comm)


def _band_mask_t(row0, tq, w, seq_len):
    tk = tq + 2 * w
    kk = lax.broadcasted_iota(jnp.int32, (tk, tq), 0)
    qq = lax.broadcasted_iota(jnp.int32, (tk, tq), 1)
    kpos = row0 - w + kk
    return (jnp.abs(qq + w - kk) <= w) & (kpos >= 0) & (kpos < seq_len)


def _halo_kv_specs(t, w, hkv, n, seq_len, kcol, vcol):
    kw = hkv * HEAD_DIM
    per, last = t // w, seq_len // w - 1
    cur = lambda s, i: jnp.minimum(i, n - 1)
    specs = []
    for c in (kcol, vcol):
        specs += [pl.BlockSpec((None, w, kw), lambda s, i, c=c: (s, jnp.maximum(cur(s, i) * per - 1, 0), c)),
                  pl.BlockSpec((None, t, kw), lambda s, i, c=c: (s, cur(s, i), c)),
                  pl.BlockSpec((None, w, kw), lambda s, i, c=c: (s, jnp.minimum((cur(s, i) + 1) * per, last), c))]
    return specs, cur


def _swa_fwd_s(qkv, *, qcol, kcol, vcol, hq, hkv, w, tq, sub, sink, name, comm=None):
    nseq, seq_len, _ = qkv.shape
    t = tq * sub
    n = seq_len // t
    rep = hq // hkv
    tk = tq + 2 * w
    kv_specs, cur = _halo_kv_specs(t, w, hkv, n, seq_len, kcol, vcol)

    def body(*refs):
        if sink is not None:
            sink_ref, refs = refs[0], refs[1:]
        q_ref, kp_ref, kc_ref, kn_ref, vp_ref, vc_ref, vn_ref, o_ref, lse_ref = refs
        i = pl.program_id(1)
        kfull, vfull = [], []
        for a in range(hkv // 2):
            ls = slice(a * PAIR, (a + 1) * PAIR)
            kfull.append(jnp.concatenate([kp_ref[:, ls], kc_ref[:, ls], kn_ref[:, ls]], axis=0) * 0.125)
            vfull.append(jnp.concatenate([vp_ref[:, ls], vc_ref[:, ls], vn_ref[:, ls]], axis=0))
        for jj in range(sub):
            rows = slice(jj * tq, (jj + 1) * tq)
            mask_t = _band_mask_t(i * t + jj * tq, tq, w, seq_len)
            o_t = [None] * (hq // 2)
            lse_rows = [None] * hq
            for a in range(hkv // 2):
                kcat = kfull[a][jj * tq:jj * tq + tk]
                vcat = vfull[a][jj * tq:jj * tq + tk]
                for e in range(2):
                    g = 2 * a + e
                    placed = {}
                    for r in range(rep):
                        h = g * rep + r
                        qp, pos = h // 2, h % 2
                        if pos not in placed:
                            placed[pos] = (_place_head(kcat, e, pos), _place_head(vcat, e, pos))
                        k_g, v_g = placed[pos]
                        s_t = jnp.where(mask_t, _dot_nt(k_g, q_ref[rows, qp * PAIR:(qp + 1) * PAIR]), NEG_INF)
                        m = jnp.max(s_t, axis=0, keepdims=True)
                        if sink is not None:
                            m = jnp.maximum(m, sink_ref[0, h])
                        p_t = jnp.exp(s_t - m)
                        den = jnp.sum(p_t, axis=0, keepdims=True)
                        if sink is not None:
                            den = den + jnp.exp(sink_ref[0, h] - m)
                        part = _dot_tn(v_g, p_t.astype(BF16)) / den
                        o_t[qp] = part if o_t[qp] is None else o_t[qp] + part
                        lse_rows[h] = m + jnp.log(den)
            o_ref[rows, :] = jnp.concatenate(o_t, axis=0).T
            lse_ref[:, rows] = jnp.concatenate(lse_rows, axis=0)

    in_specs = [pl.BlockSpec((None, t, hq * HEAD_DIM), lambda s, i: (s, i, qcol))] + kv_specs
    args = [qkv] * 7
    if sink is not None:
        in_specs = [pl.BlockSpec(memory_space=pltpu.SMEM)] + in_specs
        args = [sink] + args
    (o, lse), couts = _pcall(
        body, name=name, grid=(nseq, n), in_specs=in_specs,
        out_specs=[pl.BlockSpec((None, t, hq * HEAD_DIM), lambda s, i: (s, i, 0)),
                   pl.BlockSpec((None, hq, t), lambda s, i: (s, 0, i))],
        out_shape=[jax.ShapeDtypeStruct((nseq, seq_len, hq * HEAD_DIM), F32),
                   jax.ShapeDtypeStruct((nseq, hq, seq_len), F32)],
        args=args, dims=("parallel", "parallel"), comm=comm)
    return o, lse, couts


def _swa_bwd_s(qkv, do, lse, delta, cs, e_mat, *, qcol, kcol, vcol, hq, hkv, w, tq, sub, sink, name, comm=None):
    nseq, seq_len, _ = qkv.shape
    t = tq * sub
    n = seq_len // t
    rep = hq // hkv
    qw, kw = hq * HEAD_DIM, hkv * HEAD_DIM
    tk = tq + 2 * w
    kv_specs, cur = _halo_kv_specs(t, w, hkv, n, seq_len, kcol, vcol)

    def body(*refs):
        if sink is not None:
            sink_ref, refs = refs[0], refs[1:]
        (q_ref, kp_ref, kc_ref, kn_ref, vp_ref, vc_ref, vn_ref, do_ref, lse_ref, dl_ref,
         cs_c, cs_p, e_ref) = refs[:13]
        outs = refs[13:]
        if sink is not None:
            dq_ref, dk_ref, dv_ref, dsink_ref, dk_acc, dv_acc, dk_win, dv_win = outs
        else:
            dq_ref, dk_ref, dv_ref, dk_acc, dv_acc, dk_win, dv_win = outs
        s_id = pl.program_id(0)
        i = pl.program_id(1)
        slot_p, slot_c, slot_n = (i + 2) % 3, i % 3, (i + 1) % 3

        if sink is not None:
            @pl.when((s_id == 0) & (i == 0))
            def _():
                dsink_ref[...] = jnp.zeros_like(dsink_ref)

        @pl.when(i < n)
        def _():
            dk_win[...] = jnp.zeros_like(dk_win)
            dv_win[...] = jnp.zeros_like(dv_win)
            kfull, vfull = [], []
            for a in range(hkv // 2):
                ls = slice(a * PAIR, (a + 1) * PAIR)
                kfull.append(jnp.concatenate([kp_ref[:, ls], kc_ref[:, ls], kn_ref[:, ls]], axis=0) * 0.125)
                vfull.append(jnp.concatenate([vp_ref[:, ls], vc_ref[:, ls], vn_ref[:, ls]], axis=0))
            for jj in range(sub):
                rows = slice(jj * tq, (jj + 1) * tq)
                krows = slice(jj * tq, jj * tq + tk)
                mask_t = _band_mask_t(i * t + jj * tq, tq, w, seq_len)
                dq_t = [None] * (hq // 2)
                for a in range(hkv // 2):
                    ls = slice(a * PAIR, (a + 1) * PAIR)
                    kcat, vcat = kfull[a][krows], vfull[a][krows]
                    dk2 = jnp.zeros((tk, PAIR), F32)
                    dv2 = jnp.zeros((tk, PAIR), F32)
                    for e in range(2):
                        g = 2 * a + e
                        placed = {}
                        for r in range(rep):
                            h = g * rep + r
                            qp, pos = h // 2, h % 2
                            if pos not in placed:
                                placed[pos] = (_place_head(kcat, e, pos), _place_head(vcat, e, pos))
                            k_g, v_g = placed[pos]
                            q2 = q_ref[rows, qp * PAIR:(qp + 1) * PAIR]
                            do2 = do_ref[rows, qp * PAIR:(qp + 1) * PAIR]
                            lse_h = lse_ref[h:h + 1, rows]
                            dl_h = dl_ref[h:h + 1, rows]
                            p_t = jnp.exp(jnp.where(mask_t, _dot_nt(k_g, q2), NEG_INF) - lse_h)
                            dp_t = _dot_nt(v_g, do2)
                            dsb = (p_t * (dp_t - dl_h)).astype(BF16)
                            part = _dot_tn(k_g, dsb)
                            dq_t[qp] = part if dq_t[qp] is None else dq_t[qp] + part
                            dk2 = dk2 + _dot(dsb, _place_head(q2, pos, e) * 0.125)
                            dv2 = dv2 + _dot(p_t.astype(BF16), _place_head(do2, pos, e))
                            if sink is not None:
                                ds_sink = -jnp.sum(jnp.exp(sink_ref[0, h] - lse_h) * dl_h)
                                dsink_ref[h:h + 1, :] += jnp.full((1, LANES), ds_sink, F32)
                    dk_win[krows, ls] += dk2
                    dv_win[krows, ls] += dv2
                dq = jnp.concatenate(dq_t, axis=0).T
                dq_ref[rows, :] = _rope(dq, *_rope_tabs(cs_c[rows, :], e_ref[...]), -1.0).astype(BF16)

            @pl.when(i > 0)
            def _():
                dk_acc[slot_p, t - w:, :] += dk_win[:w, :]
                dv_acc[slot_p, t - w:, :] += dv_win[:w, :]

            @pl.when(i == 0)
            def _():
                dk_acc[slot_c] = dk_win[w:w + t, :]
                dv_acc[slot_c] = dv_win[w:w + t, :]

            @pl.when(i > 0)
            def _():
                dk_acc[slot_c] += dk_win[w:w + t, :]
                dv_acc[slot_c] += dv_win[w:w + t, :]

            dk_acc[slot_n] = jnp.zeros((t, kw), F32)
            dv_acc[slot_n] = jnp.zeros((t, kw), F32)
            dk_acc[slot_n, :w, :] = dk_win[w + t:, :]
            dv_acc[slot_n, :w, :] = dv_win[w + t:, :]

        @pl.when(i >= 1)
        def _():
            dk_ref[...] = _rope(dk_acc[slot_p], *_rope_tabs(cs_p[...], e_ref[...]), -1.0).astype(BF16)
            dv_ref[...] = dv_acc[slot_p].astype(BF16)

    row_c = lambda width: pl.BlockSpec((None, t, width), lambda s, i: (s, cur(s, i), 0))
    row_p = lambda width: pl.BlockSpec((None, t, width), lambda s, i: (s, jnp.maximum(i - 1, 0), 0))
    stat = pl.BlockSpec((None, hq, t), lambda s, i: (s, 0, cur(s, i)))
    in_specs = ([pl.BlockSpec((None, t, qw), lambda s, i: (s, cur(s, i), qcol))] + kv_specs
                + [row_c(qw), stat, stat, row_c(ROT_DIM), row_p(ROT_DIM),
                   pl.BlockSpec((ROT_DIM, 3 * LANES), lambda s, i: (0, 0))])
    args = [qkv] * 7 + [do, lse, delta, cs, cs, e_mat]
    out_specs = [row_c(qw), row_p(kw), row_p(kw)]
    out_shape = [jax.ShapeDtypeStruct((nseq, seq_len, qw), BF16),
                 jax.ShapeDtypeStruct((nseq, seq_len, kw), BF16),
                 jax.ShapeDtypeStruct((nseq, seq_len, kw), BF16)]
    if sink is not None:
        in_specs = [pl.BlockSpec(memory_space=pltpu.SMEM)] + in_specs
        args = [sink] + args
        out_specs.append(pl.BlockSpec((8, LANES), lambda s, i: (0, 0)))
        out_shape.append(jax.ShapeDtypeStruct((8, LANES), F32))
    return _pcall(
        body, name=name, grid=(nseq, n + 1), in_specs=in_specs, out_specs=out_specs, out_shape=out_shape,
        scratch_shapes=[pltpu.VMEM((3, t, kw), F32), pltpu.VMEM((3, t, kw), F32),
                        pltpu.VMEM((t + 2 * w, kw), F32), pltpu.VMEM((t + 2 * w, kw), F32)], args=args,
        dims=("arbitrary", "arbitrary"), comm=comm)


def _rms_parts(o, g):
    ms = jnp.mean(o * o, axis=-1, keepdims=True) + LN_EPS
    rinv = lax.rsqrt(ms)
    return o * rinv * g, rinv


def _from_subsequences(ref, scr, dil, t):
    slabs = ref.shape[-1] // LANES
    if dil == 1:
        return ref[0].astype(F32)
    for c in range(dil):
        for sl in range(slabs):
            scr[sl, pl.ds(c, t // dil, stride=dil), :] = ref[c, :, sl * LANES:(sl + 1) * LANES].astype(F32)
    return jnp.concatenate([scr[sl] for sl in range(slabs)], axis=1)


def _to_subsequences(val, ref, scr, dil, t):
    slabs = val.shape[-1] // LANES
    if dil == 1:
        ref[0] = val.astype(ref.dtype)
        return
    for sl in range(slabs):
        scr[sl] = val[:, sl * LANES:(sl + 1) * LANES]
    for c in range(dil):
        for sl in range(slabs):
            ref[c, :, sl * LANES:(sl + 1) * LANES] = scr[sl, pl.ds(c, t // dil, stride=dil), :].astype(ref.dtype)


def _combine_fwd(out_a, o_g, lse_g, g_win, g_dil, *, t):
    s = out_a.shape[1]
    wd = DIL_SLOTS * HEAD_DIM

    def body(oa_ref, o0, o1, o2, l0, l1, l2, gw_ref, gd_ref, mixed_ref, ob_ref, lt_ref, scr):
        ls = [l0[...], l1[...], l2[...]]
        mx = jnp.maximum(jnp.maximum(ls[0], ls[1]), ls[2])
        ws = [jnp.exp(l - mx) for l in ls]
        tot = ws[0] + ws[1] + ws[2]
        lt_ref[...] = mx + jnp.log(tot)
        ws = [x / tot for x in ws]
        og = [_from_subsequences(o_ref, scr.at[gi], dil, t)
              for gi, (o_ref, dil) in enumerate(zip((o0, o1, o2), DILATIONS))]
        parts = []
        for h in range(DIL_SLOTS):
            hs = slice(h * HEAD_DIM, (h + 1) * HEAD_DIM)
            parts.append(ws[0][:, h:h + 1] * og[0][:, hs] + ws[1][:, h:h + 1] * og[1][:, hs]
                         + ws[2][:, h:h + 1] * og[2][:, hs])
        ob = jnp.concatenate(parts, axis=1)
        ob_ref[...] = ob
        na, _ = _rms_parts(oa_ref[...], gw_ref[...])
        nb, _ = _rms_parts(ob, gd_ref[...])
        mixed_ref[:, :wd] = na.astype(BF16)
        mixed_ref[:, wd:] = nb.astype(BF16)

    half = pl.BlockSpec((t, wd), lambda i: (i, 0))
    lanes = pl.BlockSpec((t, LANES), lambda i: (i, 0))
    grow = pl.BlockSpec((1, wd), lambda i: (0, 0))
    subseq = [pl.BlockSpec((dil, t // dil, wd), lambda i: (0, i, 0)) for dil in DILATIONS]
    return pl.pallas_call(
        body, name="combine_fwd", grid=(s // t,),
        in_specs=[pl.BlockSpec((None, t, wd), lambda i: (0, i, 0))] + subseq + [lanes, lanes, lanes, grow, grow],
        out_specs=[pl.BlockSpec((t, 2 * wd), lambda i: (i, 0)), half, lanes],
        out_shape=[jax.ShapeDtypeStruct((s, 2 * wd), BF16), jax.ShapeDtypeStruct((s, wd), F32),
                   jax.ShapeDtypeStruct((s, LANES), F32)],
        scratch_shapes=[pltpu.VMEM((len(DILATIONS), wd // LANES, t, LANES), F32)],
        compiler_params=_cparams(dimension_semantics=("parallel",)),
    )(out_a, *o_g, *lse_g, g_win, g_dil)


def _combine_bwd(dmixed, out_a, out_b, g_win, g_dil, *, t):
    s = out_b.shape[0]
    wd = DIL_SLOTS * HEAD_DIM

    def body(dm_ref, oa_ref, ob_ref, gw_ref, gd_ref, doa_ref, dob0, dob1, dob2, dla_ref, dlb_ref, st_ref, scr):
        i = pl.program_id(0)

        @pl.when(i == 0)
        def _():
            st_ref[...] = jnp.zeros_like(st_ref)

        lane = lax.broadcasted_iota(jnp.int32, (t, LANES), 1)
        for idx, (o_ref, g_ref, dl_ref) in enumerate(((oa_ref, gw_ref, dla_ref), (ob_ref, gd_ref, dlb_ref))):
            o = o_ref[...]
            dn = dm_ref[:, idx * wd:(idx + 1) * wd]
            _, rinv = _rms_parts(o, g_ref[...])
            wv = dn * g_ref[...]
            do = rinv * wv - o * (rinv * rinv * rinv) * jnp.mean(wv * o, axis=-1, keepdims=True)
            st_ref[idx:idx + 1, :] += jnp.sum(dn * o * rinv, axis=0, keepdims=True)
            if idx == 0:
                doa_ref[...] = do.astype(BF16)
            else:
                for do_ref, dil in zip((dob0, dob1, dob2), DILATIONS):
                    _to_subsequences(do, do_ref, scr, dil, t)
            prod = do * o
            acc = jnp.zeros((t, LANES), F32)
            for h in range(DIL_SLOTS):
                hs = slice(h * HEAD_DIM, (h + 1) * HEAD_DIM)
                acc = jnp.where(lane == h, jnp.sum(prod[:, hs], axis=1, keepdims=True), acc)
            dl_ref[...] = acc

    half = pl.BlockSpec((t, wd), lambda i: (i, 0))
    lanes = pl.BlockSpec((t, LANES), lambda i: (i, 0))
    grow = pl.BlockSpec((1, wd), lambda i: (0, 0))
    a_spec = pl.BlockSpec((None, t, wd), lambda i: (0, i, 0))
    subseq = [pl.BlockSpec((dil, t // dil, wd), lambda i: (0, i, 0)) for dil in DILATIONS]
    doa, dob0, dob1, dob2, dla, dlb, st = pl.pallas_call(
        body, name="combine_bwd", grid=(s // t,),
        in_specs=[pl.BlockSpec((t, 2 * wd), lambda i: (i, 0)), a_spec, half, grow, grow],
        out_specs=[a_spec] + subseq + [lanes, lanes, pl.BlockSpec((8, wd), lambda i: (0, 0))],
        out_shape=[jax.ShapeDtypeStruct((1, s, wd), BF16)]
        + [jax.ShapeDtypeStruct((dil, s // dil, wd), BF16) for dil in DILATIONS]
        + [jax.ShapeDtypeStruct((s, LANES), F32), jax.ShapeDtypeStruct((s, LANES), F32),
           jax.ShapeDtypeStruct((8, wd), F32)],
        scratch_shapes=[pltpu.VMEM((wd // LANES, t, LANES), F32)],
        compiler_params=_cparams(dimension_semantics=("arbitrary",)),
    )(dmixed, out_a, out_b, g_win, g_dil)
    return doa, [dob0, dob1, dob2], dla, dlb, st


def _assemble_dz(dqa, dka, dva, dqs, dks, dvs, *, t):
    s = dqa.shape[1]
    wd = DIL_SLOTS * HEAD_DIM

    def body(*refs):
        a_refs, g_refs, o_ref, scr = refs[:3], refs[3:12], refs[12], refs[13]
        col = 0
        for r in a_refs:
            o_ref[:, col:col + r.shape[-1]] = r[...]
            col += r.shape[-1]
        for part in range(3):
            for gi, dil in enumerate(DILATIONS):
                val = _from_subsequences(g_refs[3 * part + gi], scr, dil, t)
                o_ref[:, col:col + wd] = val.astype(BF16)
                col += wd

    a_specs = [pl.BlockSpec((None, t, a.shape[-1]), lambda i: (0, i, 0)) for a in (dqa, dka, dva)]
    g_specs = [pl.BlockSpec((dil, t // dil, wd), lambda i: (0, i, 0)) for _ in range(3) for dil in DILATIONS]
    return pl.pallas_call(
        body, name="assemble_dz", grid=(s // t,), in_specs=a_specs + g_specs,
        out_specs=pl.BlockSpec((t, IN_WIDTH), lambda i: (i, 0)),
        out_shape=jax.ShapeDtypeStruct((s, IN_WIDTH), BF16),
        scratch_shapes=[pltpu.VMEM((wd // LANES, t, LANES), F32)],
        compiler_params=_cparams(dimension_semantics=("parallel",)),
    )(dqa, dka, dva, *dqs, *dks, *dvs)


def _mixproj_fwd(mixed_b, w_mix_b, x, ln_in_g, ln_in_b, ln1_g, ln1_b, *, t):
    s = x.shape[0]

    def body(m_ref, w_ref, x_ref, g0, b0, g1, b1, r1_ref, h1_ref):
        h0 = _ln(x_ref[...], g0[...], b0[...])
        r1 = ALPHA * h0 + _dot(m_ref[...], w_ref[...])
        r1_ref[...] = r1
        h1_ref[...] = _ln(r1, g1[...], b1[...]).astype(BF16)

    tile = pl.BlockSpec((t, D_MODEL), lambda i: (i, 0))
    row = pl.BlockSpec((1, D_MODEL), lambda i: (0, 0))
    return pl.pallas_call(
        body, name="mixproj_fwd", grid=(s // t,),
        in_specs=[tile, pl.BlockSpec((D_MODEL, D_MODEL), lambda i: (0, 0)), tile, row, row, row, row],
        out_specs=[tile, tile],
        out_shape=[jax.ShapeDtypeStruct((s, D_MODEL), F32), jax.ShapeDtypeStruct((s, D_MODEL), BF16)],
        compiler_params=_cparams(dimension_semantics=("parallel",)),
    )(mixed_b, w_mix_b, x, ln_in_g, ln_in_b, ln1_g, ln1_b)


def _mem_fwd(mem, g, b, wk_b, wv_b):
    ml = mem.shape[0]

    def body(mem_ref, g_ref, b_ref, wk_ref, wv_ref, mn_ref, kx_ref, vx_ref):
        mn = _ln(mem_ref[...], g_ref[...], b_ref[...]).astype(BF16)
        mn_ref[...] = mn
        kx_ref[...] = _dot(mn, wk_ref[...]).astype(BF16)
        vx_ref[...] = _dot(mn, wv_ref[...]).astype(BF16)

    sh = jax.ShapeDtypeStruct((ml, D_MODEL), BF16)
    return pl.pallas_call(body, name="mem_fwd", out_shape=[sh, sh, sh], compiler_params=_cparams())(
        mem, g, b, wk_b, wv_b)


def _mem_bwd(dkx, dvx, mem, g, b, wk_b, wv_b):
    def body(dk_ref, dv_ref, mem_ref, g_ref, b_ref, wk_ref, wv_ref, dwk_ref, dwv_ref, st_ref):
        mem_v = mem_ref[...]
        mn = _ln(mem_v, g_ref[...], b_ref[...]).astype(BF16)
        dkb = dk_ref[...].astype(BF16)
        dvb = dv_ref[...].astype(BF16)
        dwk_ref[...] = _dot_tn(mn, dkb)
        dwv_ref[...] = _dot_tn(mn, dvb)
        dmn = _dot_nt(dkb, wk_ref[...]) + _dot_nt(dvb, wv_ref[...])
        _, dg, db = _ln_bwd_math(dmn, mem_v, g_ref[...])
        st_ref[...] = jnp.zeros_like(st_ref)
        st_ref[0:1, :] = dg
        st_ref[1:2, :] = db

    sw = jax.ShapeDtypeStruct((D_MODEL, D_MODEL), F32)
    return pl.pallas_call(body, name="mem_bwd", out_shape=[sw, sw, jax.ShapeDtypeStruct((8, D_MODEL), F32)],
                          compiler_params=_cparams())(dkx, dvx, mem, g, b, wk_b, wv_b)


def _xattn_fwd(h1b, r1, kx, vx, wq_b, wo_b, ln1_g, ln1_b, ln2_g, ln2_b, *, t):
    s = h1b.shape[0]
    scale = X_HEAD_DIM ** -0.5

    def body(h_ref, r1_ref, kx_ref, vx_ref, wq_ref, wo_ref, g1, b1, g2, b2, r2_ref, h2_ref, qx_ref, ox_ref, lse_ref):
        qxb = _dot(h_ref[...], wq_ref[...]).astype(BF16)
        qx_ref[...] = qxb
        lane = lax.broadcasted_iota(jnp.int32, (t, LANES), 1)
        lse_acc = jnp.zeros((t, LANES), F32)
        parts = []
        for h in range(X_HEADS):
            hs = slice(h * X_HEAD_DIM, (h + 1) * X_HEAD_DIM)
            sc = _dot_nt(qxb[:, hs] * scale, kx_ref[:, hs])
            m = jnp.max(sc, axis=1, keepdims=True)
            p = jnp.exp(sc - m)
            den = jnp.sum(p, axis=1, keepdims=True)
            parts.append(_dot(p.astype(BF16), vx_ref[:, hs]) / den)
            lse_acc = jnp.where(lane == h, m + jnp.log(den), lse_acc)
        lse_ref[...] = lse_acc
        oxb = jnp.concatenate(parts, axis=1).astype(BF16)
        ox_ref[...] = oxb
        h1 = _ln(r1_ref[...], g1[...], b1[...])
        r2 = ALPHA * h1 + _dot(oxb, wo_ref[...])
        r2_ref[...] = r2
        h2_ref[...] = _ln(r2, g2[...], b2[...]).astype(BF16)

    tile = pl.BlockSpec((t, D_MODEL), lambda i: (i, 0))
    row = pl.BlockSpec((1, D_MODEL), lambda i: (0, 0))
    full = lambda r: pl.BlockSpec((r, D_MODEL), lambda i: (0, 0))
    ml = kx.shape[0]
    bsh = jax.ShapeDtypeStruct((s, D_MODEL), BF16)
    return pl.pallas_call(
        body, name="xattn_fwd", grid=(s // t,),
        in_specs=[tile, tile, full(ml), full(ml), full(D_MODEL), full(D_MODEL), row, row, row, row],
        out_specs=[tile, tile, tile, tile, pl.BlockSpec((t, LANES), lambda i: (i, 0))],
        out_shape=[jax.ShapeDtypeStruct((s, D_MODEL), F32), bsh, bsh, bsh, jax.ShapeDtypeStruct((s, LANES), F32)],
        compiler_params=_cparams(dimension_semantics=("parallel",)),
    )(h1b, r1, kx, vx, wq_b, wo_b, ln1_g, ln1_b, ln2_g, ln2_b)


def _xattn_bwd(dr2, qxb, oxb, lse, kx, vx, wq_b, wo_b, r1, ln1_g, *, t, comm=None):
    s = dr2.shape[0]
    ml = kx.shape[0]
    scale = X_HEAD_DIM ** -0.5

    def body(dr2_ref, qx_ref, ox_ref, lse_ref, kx_ref, vx_ref, wq_ref, wo_ref, r1_ref, g1_ref,
             dr1_ref, dr1b_ref, dqx_ref, dkx_ref, dvx_ref, st_ref):
        i = pl.program_id(0)

        @pl.when(i == 0)
        def _():
            dkx_ref[...] = jnp.zeros_like(dkx_ref)
            dvx_ref[...] = jnp.zeros_like(dvx_ref)
            st_ref[...] = jnp.zeros_like(st_ref)

        dr2v = dr2_ref[...]
        dox = _dot_nt(dr2v.astype(BF16), wo_ref[...])
        parts = []
        for h in range(X_HEADS):
            hs = slice(h * X_HEAD_DIM, (h + 1) * X_HEAD_DIM)
            doh = dox[:, hs]
            dohb = doh.astype(BF16)
            dl = jnp.sum(doh * ox_ref[:, hs].astype(F32), axis=1, keepdims=True)
            qh = qx_ref[:, hs] * scale
            p = jnp.exp(_dot_nt(qh, kx_ref[:, hs]) - lse_ref[:, h:h + 1])
            dp = _dot_nt(dohb, vx_ref[:, hs])
            dsb = (p * (dp - dl)).astype(BF16)
            parts.append(_dot(dsb, kx_ref[:, hs]) * scale)
            dkx_ref[:, hs] += _dot_tn(dsb, qh)
            dvx_ref[:, hs] += _dot_tn(p.astype(BF16), dohb)
        dqxb = jnp.concatenate(parts, axis=1).astype(BF16)
        dqx_ref[...] = dqxb
        dh1 = _dot_nt(dqxb, wq_ref[...]) + ALPHA * dr2v
        dr1, dg, db = _ln_bwd_math(dh1, r1_ref[...], g1_ref[...])
        dr1_ref[...] = dr1
        dr1b_ref[...] = dr1.astype(BF16)
        st_ref[0:1, :] += dg
        st_ref[1:2, :] += db

    tile = pl.BlockSpec((t, D_MODEL), lambda i: (i, 0))
    full = lambda r: pl.BlockSpec((r, D_MODEL), lambda i: (0, 0))
    bsh = jax.ShapeDtypeStruct((s, D_MODEL), BF16)
    return _pcall(
        body, name="xattn_bwd", grid=(s // t,),
        in_specs=[tile, tile, tile, pl.BlockSpec((t, LANES), lambda i: (i, 0)), full(ml), full(ml),
                  full(D_MODEL), full(D_MODEL), tile, full(1)],
        out_specs=[tile, tile, tile, full(ml), full(ml), full(8)],
        out_shape=[jax.ShapeDtypeStruct((s, D_MODEL), F32), bsh, bsh,
                   jax.ShapeDtypeStruct((ml, D_MODEL), F32), jax.ShapeDtypeStruct((ml, D_MODEL), F32),
                   jax.ShapeDtypeStruct((8, D_MODEL), F32)],
        args=[dr2, qxb, oxb, lse, kx, vx, wq_b, wo_b, r1, ln1_g], dims=("arbitrary",), comm=comm)


def _halo_specs(t, s, width):
    tb8 = t // 8
    return [pl.BlockSpec((t, width), lambda i: (i, 0)),
            pl.BlockSpec((8, width), lambda i: (jnp.maximum(i * tb8 - 1, 0), 0)),
            pl.BlockSpec((8, width), lambda i: (jnp.minimum((i + 1) * tb8, s // 8 - 1), 0))]


def _halo_rows(i, n, prev_ref, next_ref):
    prev_row = jnp.where(i > 0, prev_ref[7:8, :], 0.0)
    next_row = jnp.where(i < n - 1, next_ref[0:1, :], 0.0)
    return prev_row, next_row


def _gelu_parts(gc):
    cdf = 0.5 * (1.0 + lax.erf(gc * (2.0 ** -0.5)))
    pdf = jnp.exp(-0.5 * gc * gc) * (1.0 / math.sqrt(2.0 * math.pi))
    return gc * cdf, cdf + gc * pdf


def _conv_fwd(g, u, conv_w, conv_b, *, t):
    s = g.shape[0]
    n = s // t

    def body(g_ref, gp_ref, gn_ref, u_ref, cw_ref, cb_ref, o_ref):
        i = pl.program_id(0)
        gv = g_ref[...]
        prev_row, next_row = _halo_rows(i, n, gp_ref, gn_ref)
        gm1, gp1 = _shift_rows(gv, prev_row, next_row)
        gc = gm1 * cw_ref[0:1, :] + gv * cw_ref[1:2, :] + gp1 * cw_ref[2:3, :] + cb_ref[...]
        act, _ = _gelu_parts(gc)
        o_ref[...] = (act * u_ref[...]).astype(BF16)

    tile = pl.BlockSpec((t, D_FF), lambda i: (i, 0))
    return pl.pallas_call(
        body, name="conv_fwd", grid=(n,),
        in_specs=_halo_specs(t, s, D_FF) + [tile, pl.BlockSpec((3, D_FF), lambda i: (0, 0)),
                                            pl.BlockSpec((1, D_FF), lambda i: (0, 0))],
        out_specs=tile, out_shape=jax.ShapeDtypeStruct((s, D_FF), BF16),
        compiler_params=_cparams(dimension_semantics=("parallel",)),
    )(g, g, g, u, conv_w, conv_b)


def _down_ln3(tb, w_down_b, r2, target, ln2_g, ln2_b, ln3_g, ln3_b, *, t):
    s = r2.shape[0]

    def body(t_ref, w_ref, r2_ref, tg_ref, g2, b2, g3, b3, dr_ref, drb_ref, st_ref):
        i = pl.program_id(0)

        @pl.when(i == 0)
        def _():
            st_ref[...] = jnp.zeros_like(st_ref)

        h2 = _ln(r2_ref[...], g2[...], b2[...])
        r3 = ALPHA * h2 + _dot(t_ref[...], w_ref[...])
        y = _ln(r3, g3[...], b3[...])
        err = y - tg_ref[...]
        loss = 0.5 * jnp.sum(jnp.mean(err * err, axis=-1, keepdims=True))
        dy = err * (1.0 / D_MODEL)
        dr, dg, db = _ln_bwd_math(dy, r3, g3[...])
        dr_ref[...] = dr
        drb_ref[...] = dr.astype(BF16)
        st_ref[0:1, :] += dg
        st_ref[1:2, :] += db
        st_ref[2:3, :] += jnp.full((1, D_MODEL), loss, F32)

    tile = pl.BlockSpec((t, D_MODEL), lambda i: (i, 0))
    row = pl.BlockSpec((1, D_MODEL), lambda i: (0, 0))
    return pl.pallas_call(
        body, name="down_ln3", grid=(s // t,),
        in_specs=[pl.BlockSpec((t, D_FF), lambda i: (i, 0)), pl.BlockSpec((D_FF, D_MODEL), lambda i: (0, 0)),
                  tile, tile, row, row, row, row],
        out_specs=[tile, tile, pl.BlockSpec((8, D_MODEL), lambda i: (0, 0))],
        out_shape=[jax.ShapeDtypeStruct((s, D_MODEL), F32), jax.ShapeDtypeStruct((s, D_MODEL), BF16),
                   jax.ShapeDtypeStruct((8, D_MODEL), F32)],
        compiler_params=_cparams(dimension_semantics=("arbitrary",)),
    )(tb, w_down_b, r2, target, ln2_g, ln2_b, ln3_g, ln3_b)


def _ffn_out(g, u, conv_w, conv_b, w_down_b, r2, target, ln2_g, ln2_b, ln3_g, ln3_b, *, t):
    s = r2.shape[0]
    n = s // t

    def body(g_ref, gp_ref, gn_ref, u_ref, cw_ref, cb_ref, w_ref, r2_ref, tg_ref, g2, b2, g3, b3,
             t_ref, dr_ref, drb_ref, st_ref):
        i = pl.program_id(0)

        @pl.when(i == 0)
        def _():
            st_ref[...] = jnp.zeros_like(st_ref)

        gv = g_ref[...]
        prev_row, next_row = _halo_rows(i, n, gp_ref, gn_ref)
        gm1, gp1 = _shift_rows(gv, prev_row, next_row)
        gc = gm1 * cw_ref[0:1, :] + gv * cw_ref[1:2, :] + gp1 * cw_ref[2:3, :] + cb_ref[...]
        act, _ = _gelu_parts(gc)
        tb = (act * u_ref[...]).astype(BF16)
        t_ref[...] = tb
        h2 = _ln(r2_ref[...], g2[...], b2[...])
        r3 = ALPHA * h2 + _dot(tb, w_ref[...])
        y = _ln(r3, g3[...], b3[...])
        err = y - tg_ref[...]
        loss = 0.5 * jnp.sum(jnp.mean(err * err, axis=-1, keepdims=True))
        dr, dg, db = _ln_bwd_math(err * (1.0 / D_MODEL), r3, g3[...])
        dr_ref[...] = dr
        drb_ref[...] = dr.astype(BF16)
        st_ref[0:1, :] += dg
        st_ref[1:2, :] += db
        st_ref[2:3, :] += jnp.full((1, D_MODEL), loss, F32)

    wide = pl.BlockSpec((t, D_FF), lambda i: (i, 0))
    tile = pl.BlockSpec((t, D_MODEL), lambda i: (i, 0))
    row = pl.BlockSpec((1, D_MODEL), lambda i: (0, 0))
    return pl.pallas_call(
        body, name="ffn_out", grid=(n,),
        in_specs=_halo_specs(t, s, D_FF) + [wide, pl.BlockSpec((3, D_FF), lambda i: (0, 0)),
                                            pl.BlockSpec((1, D_FF), lambda i: (0, 0)),
                                            pl.BlockSpec((D_FF, D_MODEL), lambda i: (0, 0)),
                                            tile, tile, row, row, row, row],
        out_specs=[wide, tile, tile, pl.BlockSpec((8, D_MODEL), lambda i: (0, 0))],
        out_shape=[jax.ShapeDtypeStruct((s, D_FF), BF16), jax.ShapeDtypeStruct((s, D_MODEL), F32),
                   jax.ShapeDtypeStruct((s, D_MODEL), BF16), jax.ShapeDtypeStruct((8, D_MODEL), F32)],
        compiler_params=_cparams(dimension_semantics=("arbitrary",)),
    )(g, g, g, u, conv_w, conv_b, w_down_b, r2, target, ln2_g, ln2_b, ln3_g, ln3_b)


def _dh2_ln2(dgc, conv_w, du, w_gate_b, w_up_b, dr3, r2, ln2_g, *, t, comm=None):
    s = dgc.shape[0]
    n = s // t

    def body(d_ref, dp_ref, dn_ref, cw_ref, du_ref, wg_ref, wu_ref, dr3_ref, r2_ref, g2, dg_ref, dr_ref, drb_ref,
             st_ref):
        i = pl.program_id(0)

        @pl.when(i == 0)
        def _():
            st_ref[...] = jnp.zeros_like(st_ref)

        dv = d_ref[...]
        prev_row, next_row = _halo_rows(i, n, dp_ref, dn_ref)
        dm1, dp1 = _shift_rows(dv, prev_row, next_row)
        dgb = (dp1 * cw_ref[0:1, :] + dv * cw_ref[1:2, :] + dm1 * cw_ref[2:3, :]).astype(BF16)
        dg_ref[...] = dgb
        dh2 = _dot(dgb, wg_ref[...]) + _dot(du_ref[...], wu_ref[...]) + ALPHA * dr3_ref[...]
        dr, dg, db = _ln_bwd_math(dh2, r2_ref[...], g2[...])
        dr_ref[...] = dr
        drb_ref[...] = dr.astype(BF16)
        st_ref[0:1, :] += dg
        st_ref[1:2, :] += db

    wide = pl.BlockSpec((t, D_FF), lambda i: (i, 0))
    tile = pl.BlockSpec((t, D_MODEL), lambda i: (i, 0))
    wfull = pl.BlockSpec((D_FF, D_MODEL), lambda i: (0, 0))
    return _pcall(
        body, name="dh2_ln2", grid=(n,),
        in_specs=_halo_specs(t, s, D_FF) + [pl.BlockSpec((3, D_FF), lambda i: (0, 0)), wide, wfull, wfull,
                                            tile, tile, pl.BlockSpec((1, D_MODEL), lambda i: (0, 0))],
        out_specs=[wide, tile, tile, pl.BlockSpec((8, D_MODEL), lambda i: (0, 0))],
        out_shape=[jax.ShapeDtypeStruct((s, D_FF), BF16), jax.ShapeDtypeStruct((s, D_MODEL), F32),
                   jax.ShapeDtypeStruct((s, D_MODEL), BF16), jax.ShapeDtypeStruct((8, D_MODEL), F32)],
        args=[dgc, dgc, dgc, conv_w, du, w_gate_b, w_up_b, dr3, r2, ln2_g], dims=("arbitrary",), comm=comm)


def _conv_bwd_a(dr3b, w_down_b, g, u, conv_w, conv_b, *, t):
    s = g.shape[0]
    n = s // t

    def body(d_ref, w_ref, g_ref, gp_ref, gn_ref, u_ref, cw_ref, cb_ref, du_ref, dgc_ref, st_ref):
        i = pl.program_id(0)

        @pl.when(i == 0)
        def _():
            st_ref[...] = jnp.zeros_like(st_ref)

        dt = _dot_nt(d_ref[...], w_ref[...])
        gv = g_ref[...]
        prev_row, next_row = _halo_rows(i, n, gp_ref, gn_ref)
        gm1, gp1 = _shift_rows(gv, prev_row, next_row)
        gc = gm1 * cw_ref[0:1, :] + gv * cw_ref[1:2, :] + gp1 * cw_ref[2:3, :] + cb_ref[...]
        act, dact = _gelu_parts(gc)
        du_ref[...] = (dt * act).astype(BF16)
        dgc = dt * u_ref[...] * dact
        dgc_ref[...] = dgc
        st_ref[0:1, :] += jnp.sum(gm1 * dgc, axis=0, keepdims=True)
        st_ref[1:2, :] += jnp.sum(gv * dgc, axis=0, keepdims=True)
        st_ref[2:3, :] += jnp.sum(gp1 * dgc, axis=0, keepdims=True)
        st_ref[3:4, :] += jnp.sum(dgc, axis=0, keepdims=True)

    tile = pl.BlockSpec((t, D_FF), lambda i: (i, 0))
    return pl.pallas_call(
        body, name="conv_bwd_a", grid=(n,),
        in_specs=[pl.BlockSpec((t, D_MODEL), lambda i: (i, 0)), pl.BlockSpec((D_FF, D_MODEL), lambda i: (0, 0))]
        + _halo_specs(t, s, D_FF) + [tile, pl.BlockSpec((3, D_FF), lambda i: (0, 0)),
                                     pl.BlockSpec((1, D_FF), lambda i: (0, 0))],
        out_specs=[tile, tile, pl.BlockSpec((8, D_FF), lambda i: (0, 0))],
        out_shape=[jax.ShapeDtypeStruct((s, D_FF), BF16), jax.ShapeDtypeStruct((s, D_FF), F32),
                   jax.ShapeDtypeStruct((8, D_FF), F32)],
        compiler_params=_cparams(dimension_semantics=("arbitrary",)),
    )(dr3b, w_down_b, g, g, g, u, conv_w, conv_b)


def _conv_bwd_b(dgc, conv_w, *, t):
    s = dgc.shape[0]
    n = s // t

    def body(d_ref, dp_ref, dn_ref, cw_ref, o_ref):
        i = pl.program_id(0)
        dv = d_ref[...]
        prev_row, next_row = _halo_rows(i, n, dp_ref, dn_ref)
        dm1, dp1 = _shift_rows(dv, prev_row, next_row)
        o_ref[...] = (dp1 * cw_ref[0:1, :] + dv * cw_ref[1:2, :] + dm1 * cw_ref[2:3, :]).astype(BF16)

    return pl.pallas_call(
        body, name="conv_bwd_b", grid=(n,),
        in_specs=_halo_specs(t, s, D_FF) + [pl.BlockSpec((3, D_FF), lambda i: (0, 0))],
        out_specs=pl.BlockSpec((t, D_FF), lambda i: (i, 0)), out_shape=jax.ShapeDtypeStruct((s, D_FF), BF16),
        compiler_params=_cparams(dimension_semantics=("parallel",)),
    )(dgc, dgc, dgc, conv_w)


def _to_residue(a, dil):
    s, w = a.shape
    return a.reshape(s // dil, dil, w).transpose(1, 0, 2)


def _from_residue(a):
    dil, l, w = a.shape
    return a.transpose(1, 0, 2).reshape(dil * l, w)


def _stats_to_lanes(rows):
    dil, hq, l = rows.shape
    return jnp.pad(rows.transpose(2, 0, 1).reshape(dil * l, hq), ((0, 0), (0, LANES - hq)))


def _stats_to_rows(lanes, dil):
    s = lanes.shape[0]
    return lanes[:, :DIL_SLOTS].reshape(s // dil, dil, DIL_SLOTS).transpose(1, 2, 0)


def _rope_angles(positions):
    inv_freq = ROPE_THETA ** (-jnp.arange(0, ROT_DIM, 2, dtype=F32) / ROT_DIM)
    ang = positions.astype(F32)[:, None] * inv_freq
    return jnp.concatenate([jnp.cos(ang), jnp.sin(ang)], axis=1)


class _NoPlan:
    def gather(self, stage):
        return None

    def gathered(self, stage, couts, wb):
        pass

    def exchange(self, stage, grads):
        return None

    def exchanged(self, stage, couts):
        pass


def _local_step(x, mem, positions, target, wb, sp, plan=None, *, t_row=256, t_mm=512, tq_a=128, tq_b=128,
                sub_a=4, sub_b=4):
    s = x.shape[0]
    plan = plan or _NoPlan()
    cs = _rope_angles(positions)
    e_mat = _rope_select_matrix()

    (h0b, za, *zb), couts = _proj_all(x, sp["ln_in_g"], sp["ln_in_b"], wb["w_in"], cs, e_mat, t=t_mm,
                                      comm=plan.gather("proj"))
    plan.gathered("proj", couts, wb)
    sub_a = max(1, min(sub_a, s // tq_a))
    subs_b = [max(1, min(sub_b, s // dil // tq_b)) for dil in DILATIONS]
    out_a, lse_a, couts = _swa_fwd_s(za, qcol=0, kcol=4, vcol=5, hq=WIN_Q_HEADS, hkv=WIN_KV_HEADS, w=WIN_HALF,
                                     tq=tq_a, sub=sub_a, sink=sp["attn_sink"], name="attn_a_fwd",
                                     comm=plan.gather("attn_a"))
    plan.gathered("attn_a", couts, wb)
    o_g, lse_g = [], []
    for gi in range(3):
        o, l, couts = _swa_fwd_s(zb[gi], qcol=0, kcol=1, vcol=2, hq=DIL_SLOTS, hkv=DIL_SLOTS, w=DIL_HALF, tq=tq_b,
                                 sub=subs_b[gi], sink=None, name=f"attn_b{gi}_fwd",
                                 comm=plan.gather(f"attn_b{gi}"))
        plan.gathered(f"attn_b{gi}", couts, wb)
        o_g.append(o)
        lse_g.append(_stats_to_lanes(l))
    mixed_b, out_b, lse_b = _combine_fwd(out_a, o_g, lse_g, sp["g_win"], sp["g_dil"], t=t_row)
    r1, h1b = _mixproj_fwd(mixed_b, wb["w_mix_out"], x, sp["ln_in_g"], sp["ln_in_b"], sp["ln1_g"], sp["ln1_b"],
                           t=t_row)
    mem_nb, kx, vx = _mem_fwd(mem, sp["mem_ln_g"], sp["mem_ln_b"], wb["w_xk"], wb["w_xv"])
    r2, h2b, qxb, oxb, lse_x = _xattn_fwd(h1b, r1, kx, vx, wb["w_xq"], wb["w_xo"], sp["ln1_g"], sp["ln1_b"],
                                          sp["ln2_g"], sp["ln2_b"], t=t_row)
    g = _mm(h2b, wb["w_gate"], mode="nt", out_dtype=F32, tm=t_mm, tn=D_FF, name="ff_gate")
    u = _mm(h2b, wb["w_up"], mode="nt", out_dtype=F32, tm=t_mm, tn=D_FF, name="ff_up")
    tb, dr3, dr3b, st3 = _ffn_out(g, u, sp["conv_w"], sp["conv_b"], wb["w_down"], r2, target, sp["ln2_g"],
                                  sp["ln2_b"], sp["ln3_g"], sp["ln3_b"], t=t_row)

    grads = {}
    du, dgc, st_conv = _conv_bwd_a(dr3b, wb["w_down"], g, u, sp["conv_w"], sp["conv_b"], t=t_row)
    tk = min(1024, s)
    grads["w_down"] = _mm(tb, dr3b, mode="tn", out_dtype=BF16, tm=D_FF // 2, tn=D_MODEL, tk=tk, name="dw_down")
    grads["w_up"] = _mm(du, h2b, mode="tn", out_dtype=BF16, tm=D_FF // 2, tn=D_MODEL, tk=tk, name="dw_up")
    (dg, dr2, dr2b, st2), couts = _dh2_ln2(dgc, sp["conv_w"], du, wb["w_gate"], wb["w_up"], dr3, r2, sp["ln2_g"],
                                           t=t_row, comm=plan.exchange("dh2", grads))
    plan.exchanged("dh2", couts)
    grads["w_gate"] = _mm(dg, h2b, mode="tn", out_dtype=BF16, tm=D_FF // 2, tn=D_MODEL, tk=tk, name="dw_gate")

    (dr1, dr1b, dqxb, dkx, dvx, st1), couts = _xattn_bwd(
        dr2, qxb, oxb, lse_x, kx, vx, wb["w_xq"], wb["w_xo"], r1, sp["ln1_g"], t=t_row,
        comm=plan.exchange("xattn", grads))
    plan.exchanged("xattn", couts)
    grads["w_xo"] = _mm(oxb, dr2b, mode="tn", out_dtype=BF16, tm=D_MODEL, tn=D_MODEL, tk=tk, name="dw_xo")
    grads["w_xq"] = _mm(h1b, dqxb, mode="tn", out_dtype=BF16, tm=D_MODEL, tn=D_MODEL, tk=tk, name="dw_xq")
    grads["w_xk"], grads["w_xv"], st_mem = _mem_bwd(dkx, dvx, mem, sp["mem_ln_g"], sp["mem_ln_b"],
                                                    wb["w_xk"], wb["w_xv"])

    grads["w_mix_out"] = _mm(mixed_b, dr1b, mode="tn", out_dtype=BF16, tm=D_MODEL, tn=D_MODEL, tk=tk,
                             name="dw_mix")
    dmixed = _mm(dr1b, wb["w_mix_out"], mode="nt", out_dtype=F32, tm=t_mm, tn=D_MODEL, name="dmixed")
    do_a, do_b, dl_a, dl_b, st_mix = _combine_bwd(dmixed, out_a, out_b, sp["g_win"], sp["g_dil"], t=t_row)
    (dqa, dka, dva, dsink), couts = _swa_bwd_s(
        za, do_a, lse_a, _stats_to_rows(dl_a, 1), cs[None], e_mat, qcol=0, kcol=4, vcol=5, hq=WIN_Q_HEADS,
        hkv=WIN_KV_HEADS, w=WIN_HALF, tq=2 * tq_a, sub=max(1, sub_a // 2), sink=sp["attn_sink"], name="attn_a_bwd",
        comm=plan.exchange("attn_a", grads))
    plan.exchanged("attn_a", couts)
    dqs, dks, dvs = [], [], []
    for gi, dil in enumerate(DILATIONS):
        (dq, dk, dv), _ = _swa_bwd_s(
            zb[gi], do_b[gi], _stats_to_rows(lse_b, dil), _stats_to_rows(dl_b, dil),
            _to_residue(cs, dil), e_mat, qcol=0, kcol=1, vcol=2, hq=DIL_SLOTS, hkv=DIL_SLOTS, w=DIL_HALF, tq=tq_b,
            sub=subs_b[gi], sink=None, name=f"attn_b{gi}_bwd")
        dqs.append(dq)
        dks.append(dk)
        dvs.append(dv)
    dz = _assemble_dz(dqa, dka, dva, dqs, dks, dvs, t=t_row)
    grads["w_in"] = _mm(dz, h0b, mode="tn", out_dtype=BF16, tm=IN_WIDTH // 7, tn=D_MODEL, tk=tk, name="dw_in")
    comm = plan.exchange("dh0", grads)
    dh0 = _mm(dz, wb["w_in"], mode="nn", out_dtype=F32, tm=t_mm, tn=D_MODEL, add=dr1, add_scale=ALPHA, name="dh0",
              comm=comm)
    if comm is not None:
        dh0, couts = dh0
        plan.exchanged("dh0", couts)
    grad_x, st0 = _ln_bwd(dh0, x, sp["ln_in_g"], t=t_row, name="ln_in_bwd", want_bf16=False)

    small = {
        "loss": st3[2:3, 0:1],
        "ln_in_g": st0[0:1], "ln_in_b": st0[1:2],
        "attn_sink": dsink[:, 0].reshape(1, WIN_Q_HEADS),
        "g_win": st_mix[0:1], "g_dil": st_mix[1:2],
        "ln1_g": st1[0:1], "ln1_b": st1[1:2],
        "mem_ln_g": st_mem[0:1], "mem_ln_b": st_mem[1:2],
        "ln2_g": st2[0:1], "ln2_b": st2[1:2],
        "conv_w": st_conv[0:3], "conv_b": st_conv[3:4],
        "ln3_g": st3[0:1], "ln3_b": st3[1:2],
    }
    return grad_x, grads, small


def _swap_sibling(arrays):
    n = len(arrays)

    def body(*refs):
        src, dst = refs[:n], refs[n:2 * n]
        send_sems, recv_sems = refs[2 * n:]
        x, y, c, _ = _place()
        copies = [pltpu.make_async_remote_copy(
            src_ref=src[a], dst_ref=dst[a], send_sem=send_sems.at[a], recv_sem=recv_sems.at[a],
            device_id=(x, y, 1 - c), device_id_type=MESH_IDS) for a in range(n)]
        for cp in copies:
            cp.start()
        for cp in copies:
            cp.wait_recv()
        for cp in copies:
            cp.wait_send()

    return pl.pallas_call(
        body, name="swap_sibling", in_specs=[ANY] * n, out_specs=[ANY] * n,
        out_shape=[jax.ShapeDtypeStruct(a.shape, a.dtype) for a in arrays],
        scratch_shapes=[pltpu.SemaphoreType.DMA((n,)), pltpu.SemaphoreType.DMA((n,))],
    )(*arrays)


def _row_tile(rows, cols, itemsize=4, budget=1 << 20):
    best = None
    for t in range(16, rows + 1, 16):
        if rows % t == 0 and t * cols * itemsize <= budget:
            best = t
    return best or rows


def _sum_slots(stack, *, name):
    n, r, c = stack.shape
    t = _row_tile(r, c)

    def body(s_ref, o_ref):
        acc = s_ref[0].astype(F32)
        for q in range(1, n):
            acc = acc + s_ref[q].astype(F32)
        o_ref[...] = acc

    return pl.pallas_call(
        body, name=name, grid=(r // t,), in_specs=[pl.BlockSpec((n, t, c), lambda i: (0, i, 0))],
        out_specs=pl.BlockSpec((t, c), lambda i: (i, 0)), out_shape=jax.ShapeDtypeStruct((r, c), F32),
        compiler_params=_cparams(dimension_semantics=("parallel",)),
    )(stack)


def _adamw(w, m, v, p, q, *, name):
    r, c = w.shape
    t = _row_tile(r, c, budget=1 << 19)

    def body(*refs):
        if q is None:
            w_ref, m_ref, v_ref, p_ref, g_ref, d_ref, nm_ref, nv_ref = refs
            g = p_ref[...]
        else:
            w_ref, m_ref, v_ref, p_ref, q_ref, g_ref, d_ref, nm_ref, nv_ref = refs
            g = p_ref[...] + q_ref[...]
        nm = ADAM_B1 * m_ref[...] + (1.0 - ADAM_B1) * g
        nv = ADAM_B2 * v_ref[...] + (1.0 - ADAM_B2) * (g * g)
        m_hat = nm / (1.0 - ADAM_B1 ** ADAM_STEP)
        v_hat = nv / (1.0 - ADAM_B2 ** ADAM_STEP)
        g_ref[...] = g
        d_ref[...] = -ADAM_LR * (m_hat / (jnp.sqrt(v_hat) + ADAM_EPS) + ADAM_WD * w_ref[...])
        nm_ref[...] = nm
        nv_ref[...] = nv

    tile = pl.BlockSpec((t, c), lambda i: (i, 0))
    args = [w, m, v, p] + ([] if q is None else [q])
    sh = jax.ShapeDtypeStruct((r, c), F32)
    return pl.pallas_call(
        body, name=name, grid=(r // t,), in_specs=[tile] * len(args), out_specs=[tile] * 4, out_shape=[sh] * 4,
        compiler_params=_cparams(dimension_semantics=("parallel",)),
    )(*args)


BIG = ("w_in", "w_mix_out", "w_xq", "w_xk", "w_xv", "w_xo", "w_gate", "w_up", "w_down")
COL_SHARDED = ("w_in", "w_gate", "w_up")
WEIGHTS = ("ln_in_g", "ln_in_b", "w_in", "attn_sink", "g_win", "g_dil", "w_mix_out", "ln1_g", "ln1_b",
           "mem_ln_g", "mem_ln_b", "w_xq", "w_xk", "w_xv", "w_xo", "ln2_g", "ln2_b", "w_gate", "w_up",
           "conv_w", "conv_b", "w_down", "ln3_g", "ln3_b")
SMALL = tuple(k for k in WEIGHTS if k not in BIG)
PACK_COLS = 1024
CONV_SHARD = D_FF // N_CHIPS
CONV_WIDTH_ROWS = 3
SMALL_ROWS = 32


GATHER_STAGES = {"proj": ("w_mix_out", "w_xq", "w_xk", "w_xv", "w_xo"), "attn_a": ("w_gate", "w_up"),
                 "attn_b0": ("w_down",)}
EXCHANGE_STAGES = {"dh2": ("w_down", "w_up"), "xattn": ("w_gate",),
                   "attn_a": ("w_xo", "w_xq", "w_xk", "w_xv", "w_mix_out"), "dh0": ("w_in",)}


def _full_weight(k, g4):
    return g4.reshape(N_CHIPS * g4.shape[1], g4.shape[2])


def _grad_parts(k, gk):
    gk = gk.astype(BF16)
    return gk.reshape(N_CHIPS, gk.shape[0] // N_CHIPS, gk.shape[1])


class _Plan:
    def __init__(self, shards):
        self.shards = shards
        self.recv = {}

    def gather(self, stage):
        names = GATHER_STAGES.get(stage)
        return _ChipGather([self.shards[k] for k in names]) if names else None

    def gathered(self, stage, couts, wb):
        for k, g4 in zip(GATHER_STAGES.get(stage, ()), couts):
            wb[k] = _full_weight(k, g4)

    def exchange(self, stage, grads):
        names = EXCHANGE_STAGES.get(stage)
        return _ChipExchange([_grad_parts(k, grads[k]) for k in names]) if names else None

    def exchanged(self, stage, couts):
        for k, r4 in zip(EXCHANGE_STAGES.get(stage, ()), couts):
            self.recv[k] = r4


def _pack_rows(a):
    r, n = a.shape
    per = -(-n // PACK_COLS)
    return jnp.pad(a, ((0, 0), (0, per * PACK_COLS - n))).reshape(r * per, PACK_COLS)


def _unpack_rows(p, r, n):
    per = -(-n // PACK_COLS)
    return p.reshape(r, per * PACK_COLS)[:, :n]


def _pack(pieces, rows_total):
    cat = jnp.concatenate([_pack_rows(a) for a in pieces], axis=0)
    return jnp.pad(cat, ((0, rows_total - cat.shape[0]), (0, 0)))


def _unpack(p, shapes):
    out, at = [], 0
    for r, n in shapes:
        per = -(-n // PACK_COLS)
        out.append(_unpack_rows(p[at:at + r * per], r, n))
        at += r * per
    return out


def kernel(x, mem, positions, ln_in_g, ln_in_b, w_in, attn_sink, g_win, g_dil, w_mix_out, ln1_g, ln1_b, mem_ln_g, mem_ln_b, w_xq, w_xk, w_xv, w_xo, ln2_g, ln2_b, w_gate, w_up, conv_w, conv_b, w_down, ln3_g, ln3_b, loss_target, m_ln_in_g, m_ln_in_b, m_w_in, m_attn_sink, m_g_win, m_g_dil, m_w_mix_out, m_ln1_g, m_ln1_b, m_mem_ln_g, m_mem_ln_b, m_w_xq, m_w_xk, m_w_xv, m_w_xo, m_ln2_g, m_ln2_b, m_w_gate, m_w_up, m_conv_w, m_conv_b, m_w_down, m_ln3_g, m_ln3_b, v_ln_in_g, v_ln_in_b, v_w_in, v_attn_sink, v_g_win, v_g_dil, v_w_mix_out, v_ln1_g, v_ln1_b, v_mem_ln_g, v_mem_ln_b, v_w_xq, v_w_xk, v_w_xv, v_w_xo, v_ln2_g, v_ln2_b, v_w_gate, v_w_up, v_conv_w, v_conv_b, v_w_down, v_ln3_g, v_ln3_b):
    given = dict(locals())
    shape_of = {k: given[k].shape for k in WEIGHTS}
    as2d = lambda k, a: a.reshape(-1, a.shape[-1]).T if k in COL_SHARDED else a.reshape(-1, a.shape[-1])
    w2 = {k: as2d(k, given[k]) for k in WEIGHTS}
    m2 = {k: as2d(k, given["m_" + k]) for k in WEIGHTS}
    v2 = {k: as2d(k, given["v_" + k]) for k in WEIGHTS}
    chip = 2 * lax.axis_index("x") + lax.axis_index("y")

    plan = _Plan({k: w2[k].astype(BF16) for k in BIG})
    conv_pack = jnp.pad(w2["conv_w"], ((0, 16 - CONV_WIDTH_ROWS), (0, PACK_COLS - CONV_SHARD)))
    g_in, g_conv = _comm_only(_ChipGather([plan.shards["w_in"], conv_pack]), "gather_w_in")
    wb = {"w_in": _full_weight("w_in", g_in)}
    conv_full = g_conv[:, :CONV_WIDTH_ROWS, :CONV_SHARD].transpose(1, 0, 2).reshape(CONV_WIDTH_ROWS, D_FF)
    sp = {k: w2[k] for k in SMALL}
    sp["conv_w"] = conv_full

    grad_x, grads, small = _local_step(x[0], mem[0], positions[0], loss_target[0], wb, sp, plan)

    small_keys = ("loss",) + SMALL
    small_shapes = [small[k].shape for k in small_keys]
    small_pack = _pack([small[k] for k in small_keys], SMALL_ROWS)
    (small_all,) = _comm_only(_ChipExchange([], small_pack), "exchange_small")
    chip_sums = [_sum_slots(plan.recv[k], name=f"sum_chips_{k}") for k in BIG]
    sibling_sums = _swap_sibling(chip_sums)
    small_sum = _sum_slots(small_all, name="sum_small")
    small_g = dict(zip(small_keys, _unpack(small_sum, small_shapes)))
    loss = small_g["loss"][0, 0]

    res = {}
    for k, p, q in zip(BIG, chip_sums, sibling_sums):
        res[k] = _adamw(w2[k], m2[k], v2[k], p, q, name=f"adamw_{k}")
    small_g["conv_w"] = lax.dynamic_slice_in_dim(small_g["conv_w"], chip * CONV_SHARD, CONV_SHARD, axis=1)
    adam_shapes = [w2[k].shape for k in SMALL]
    packs = [_pack([d[k] for k in SMALL], SMALL_ROWS) for d in (w2, m2, v2, small_g)]
    small_res = [_unpack(o, adam_shapes) for o in _adamw(*packs, None, name="adamw_small")]
    for i, k in enumerate(SMALL):
        res[k] = tuple(o[i] for o in small_res)

    outs = [loss, grad_x[None]]
    for slot in range(4):
        outs += [(res[k][slot].T if k in COL_SHARDED else res[k][slot]).reshape(shape_of[k]) for k in WEIGHTS]
    return tuple(outs)
```

```python
import functools
import math

import jax
import jax.numpy as jnp
from jax import lax
from jax.experimental import pallas as pl
from jax.experimental.pallas import tpu as pltpu

F32 = jnp.float32
BF16 = jnp.bfloat16

D_MODEL = 1024
HEAD_DIM = 64
WIN_Q_HEADS = 8
WIN_KV_HEADS = 2
WIN_HALF = 128
DIL_SLOTS = 8
DILATIONS = (1, 4, 16)
DIL_HALF = 64
ROT_DIM = 16
ROPE_THETA = 500000.0
X_HEADS = 4
X_HEAD_DIM = 256
D_FF = 2816
A_Q = 512
A_KV = 128
A_WIDTH = A_Q + 2 * A_KV
B_QKV = 1536
IN_WIDTH = 5376
ALPHA = 2.0 ** 0.25
LN_EPS = 1e-5
NEG_INF = -1e30
LANES = 128
N_CHIPS = 4
N_DEV = 8

ADAM_LR = 0.001
ADAM_B1 = 0.9
ADAM_B2 = 0.999
ADAM_EPS = 1e-08
ADAM_WD = 0.01
ADAM_STEP = 10

VMEM_LIMIT = 56 * 1024 * 1024


def _cparams(**kw):
    return pltpu.CompilerParams(vmem_limit_bytes=VMEM_LIMIT, **kw)


def _dot(a, b):
    return lax.dot_general(a, b, (((1,), (0,)), ((), ())), preferred_element_type=F32)


def _dot_nt(a, b):
    return lax.dot_general(a, b, (((1,), (1,)), ((), ())), preferred_element_type=F32)


def _dot_tn(a, b):
    return lax.dot_general(a, b, (((0,), (0,)), ((), ())), preferred_element_type=F32)


def _ln(x, g, b):
    mu = jnp.mean(x, axis=-1, keepdims=True)
    xc = x - mu
    var = jnp.mean(xc * xc, axis=-1, keepdims=True)
    return xc * lax.rsqrt(var + LN_EPS) * g + b


def _ln_bwd_math(dy, r, g):
    mu = jnp.mean(r, axis=-1, keepdims=True)
    xc = r - mu
    var = jnp.mean(xc * xc, axis=-1, keepdims=True)
    rstd = lax.rsqrt(var + LN_EPS)
    xhat = xc * rstd
    dxhat = dy * g
    m1 = jnp.mean(dxhat, axis=-1, keepdims=True)
    m2 = jnp.mean(dxhat * xhat, axis=-1, keepdims=True)
    dr = rstd * (dxhat - m1 - xhat * m2)
    return dr, jnp.sum(dy * xhat, axis=0, keepdims=True), jnp.sum(dy, axis=0, keepdims=True)


def _rope(z, ta, tb, tc, sign):
    w = z.shape[1]
    reps = w // LANES
    a = jnp.tile(ta, (1, reps))
    b = jnp.tile(tb, (1, reps))
    c = jnp.tile(tc, (1, reps))
    return z * a + sign * (pltpu.roll(z, w - 8, 1) * b + pltpu.roll(z, 8, 1) * c)


def _shift_rows(x, prev_row, next_row):
    t = x.shape[0]
    row = lax.broadcasted_iota(jnp.int32, x.shape, 0)
    xm1 = jnp.where(row == 0, prev_row, pltpu.roll(x, 1, 0))
    xp1 = jnp.where(row == t - 1, next_row, pltpu.roll(x, t - 1, 0))
    return xm1, xp1


def _rope_tabs(cs, e_mat):
    hi = cs.astype(BF16)
    rest = cs - hi.astype(F32)
    mid = rest.astype(BF16)
    lo = (rest - mid.astype(F32)).astype(BF16)
    tabs = _dot(hi, e_mat) + _dot(mid, e_mat) + _dot(lo, e_mat)
    lane = lax.broadcasted_iota(jnp.int32, (cs.shape[0], LANES), 1)
    ones = jnp.where((lane & (HEAD_DIM - 1)) >= ROT_DIM, 1.0, 0.0)
    return tabs[:, :LANES] + ones, tabs[:, LANES:2 * LANES], tabs[:, 2 * LANES:]


def _rope_select_matrix():
    half = ROT_DIM // 2
    e = [[0.0] * (3 * LANES) for _ in range(ROT_DIM)]
    for lane in range(LANES):
        d = lane % HEAD_DIM
        if d < half:
            e[d][lane] = 1.0
            e[half + d][LANES + lane] = -1.0
        elif d < ROT_DIM:
            e[d - half][lane] = 1.0
            e[d][2 * LANES + lane] = 1.0
    return jnp.array(e, BF16)


def _rope_rows(x, cos_t, sin_t, sign):
    half = ROT_DIM // 2
    parts = []
    for base in (0, HEAD_DIM):
        r1, r2 = x[base:base + half], x[base + half:base + ROT_DIM]
        parts += [r1 * cos_t - sign * (r2 * sin_t), r2 * cos_t + sign * (r1 * sin_t), x[base + ROT_DIM:base + HEAD_DIM]]
    return jnp.concatenate(parts, axis=0)


MESH_IDS = pl.DeviceIdType.MESH
ANY = pl.BlockSpec(memory_space=pl.ANY)


def _place():
    x, y, c = lax.axis_index("x"), lax.axis_index("y"), lax.axis_index("c")
    other_chips = [(1 - x, y), (x, 1 - y), (1 - x, 1 - y)]
    return x, y, c, other_chips


class _ChipGather:
    def __init__(self, shards):
        self.inputs = list(shards)
        n = len(shards)
        self.out_shape = [jax.ShapeDtypeStruct((N_CHIPS,) + a.shape, a.dtype) for a in shards]
        self.scratch = [pltpu.SemaphoreType.DMA((6 * n,)), pltpu.SemaphoreType.DMA((6 * n,)),
                        pltpu.SemaphoreType.DMA((n,))]

    def _copies(self, src, dst, sems):
        send_sems, recv_sems, local_sems = sems
        x, y, c, chips = _place()
        mine = 2 * x + y
        n = len(src)
        local, sends, recvs, passes, pass_recvs = [], [], [], [], []
        for a in range(n):
            half = src[a].shape[0] // 2
            my_rows, other_rows = pl.ds(c * half, half), pl.ds((1 - c) * half, half)
            local.append(pltpu.make_async_copy(src[a], dst[a].at[mine], local_sems.at[a]))
            for j, (px, py) in enumerate(chips):
                k, k2, slot = 3 * a + j, 3 * n + 3 * a + j, 2 * px + py
                sends.append(pltpu.make_async_remote_copy(
                    src_ref=src[a].at[my_rows], dst_ref=dst[a].at[mine, my_rows], send_sem=send_sems.at[k],
                    recv_sem=recv_sems.at[k], device_id=(px, py, c), device_id_type=MESH_IDS))
                recvs.append(pltpu.make_async_remote_copy(
                    src_ref=src[a].at[my_rows], dst_ref=dst[a].at[slot, my_rows], send_sem=send_sems.at[k],
                    recv_sem=recv_sems.at[k], device_id=(px, py, c), device_id_type=MESH_IDS))
                passes.append(pltpu.make_async_remote_copy(
                    src_ref=dst[a].at[slot, my_rows], dst_ref=dst[a].at[slot, my_rows], send_sem=send_sems.at[k2],
                    recv_sem=recv_sems.at[k2], device_id=(x, y, 1 - c), device_id_type=MESH_IDS))
                pass_recvs.append(pltpu.make_async_remote_copy(
                    src_ref=dst[a].at[slot, my_rows], dst_ref=dst[a].at[slot, other_rows],
                    send_sem=send_sems.at[k2], recv_sem=recv_sems.at[k2], device_id=(x, y, 1 - c),
                    device_id_type=MESH_IDS))
        return local, sends, recvs, passes, pass_recvs

    def start(self, src, dst, sems):
        local, sends, _, _, _ = self._copies(src, dst, sems)
        for cp in local + sends:
            cp.start()

    def wait(self, src, dst, sems):
        local, sends, recvs, passes, pass_recvs = self._copies(src, dst, sems)
        for idx, landed in enumerate(recvs):
            landed.wait_recv()
            if passes:
                passes[idx].start()
        for cp in pass_recvs:
            cp.wait_recv()
        for cp in sends + passes:
            cp.wait_send()
        for cp in local:
            cp.wait()


class _ChipExchange:
    def __init__(self, parts, small=None):
        self.inputs = list(parts) + ([small] if small is not None else [])
        self.n = len(parts)
        self.has_small = small is not None
        self.out_shape = [jax.ShapeDtypeStruct(a.shape, a.dtype) for a in parts]
        n_sem, n_loc = 3 * self.n, self.n
        if self.has_small:
            self.out_shape.append(jax.ShapeDtypeStruct((N_DEV,) + small.shape, small.dtype))
            n_sem, n_loc = n_sem + N_DEV - 1, n_loc + 1
        self.scratch = [pltpu.SemaphoreType.DMA((n_sem,)), pltpu.SemaphoreType.DMA((n_sem,)),
                        pltpu.SemaphoreType.DMA((n_loc,))]

    def _copies(self, src, dst, sems):
        send_sems, recv_sems, local_sems = sems
        x, y, c, chips = _place()
        mine = 2 * x + y
        n = self.n
        local, sends, recvs = [], [], []
        for a in range(n):
            local.append(pltpu.make_async_copy(src[a].at[mine], dst[a].at[mine], local_sems.at[a]))
            for j, (px, py) in enumerate(chips):
                k = 3 * a + j
                sends.append(pltpu.make_async_remote_copy(
                    src_ref=src[a].at[2 * px + py], dst_ref=dst[a].at[mine], send_sem=send_sems.at[k],
                    recv_sem=recv_sems.at[k], device_id=(px, py, c), device_id_type=MESH_IDS))
                recvs.append(pltpu.make_async_remote_copy(
                    src_ref=src[a].at[mine], dst_ref=dst[a].at[2 * px + py], send_sem=send_sems.at[k],
                    recv_sem=recv_sems.at[k], device_id=(px, py, c), device_id_type=MESH_IDS))
        if self.has_small:
            me_dev = 4 * x + 2 * y + c
            local.append(pltpu.make_async_copy(src[n], dst[n].at[me_dev], local_sems.at[n]))
            for mask in range(1, N_DEV):
                px, py, pc = x ^ ((mask >> 2) & 1), y ^ ((mask >> 1) & 1), c ^ (mask & 1)
                k = 3 * n + mask - 1
                sends.append(pltpu.make_async_remote_copy(
                    src_ref=src[n], dst_ref=dst[n].at[me_dev], send_sem=send_sems.at[k], recv_sem=recv_sems.at[k],
                    device_id=(px, py, pc), device_id_type=MESH_IDS))
                recvs.append(pltpu.make_async_remote_copy(
                    src_ref=src[n], dst_ref=dst[n].at[4 * px + 2 * py + pc], send_sem=send_sems.at[k],
                    recv_sem=recv_sems.at[k], device_id=(px, py, pc), device_id_type=MESH_IDS))
        return local, sends, recvs, [], []

    start = _ChipGather.start
    wait = _ChipGather.wait


def _pcall(body, *, name, grid, in_specs, out_specs, out_shape, args, scratch_shapes=(), dims=None, comm=None):
    in_specs, out_specs, out_shape = list(in_specs), list(out_specs), list(out_shape)
    scratch_shapes = list(scratch_shapes)
    if comm is None:
        outs = pl.pallas_call(
            body, name=name, grid=grid, in_specs=in_specs, out_specs=out_specs, out_shape=out_shape,
            scratch_shapes=scratch_shapes, compiler_params=_cparams(dimension_semantics=dims),
        )(*args)
        return list(outs), []
    n_in, n_out, n_scr = len(in_specs), len(out_specs), len(scratch_shapes)
    n_cin, n_cout = len(comm.inputs), len(comm.out_shape)

    def wrapped(*refs):
        ins, refs = refs[:n_in], refs[n_in:]
        cins, refs = refs[:n_cin], refs[n_cin:]
        outs, refs = refs[:n_out], refs[n_out:]
        couts, refs = refs[:n_cout], refs[n_cout:]
        scr, csems = refs[:n_scr], refs[n_scr:]
        first = last = None
        for axis, size in enumerate(grid):
            pid = pl.program_id(axis)
            f, l = pid == 0, pid == size - 1
            first = f if first is None else first & f
            last = l if last is None else last & l

        @pl.when(first)
        def _():
            comm.start(cins, couts, csems)

        body(*ins, *outs, *scr)

        @pl.when(last)
        def _():
            comm.wait(cins, couts, csems)

    res = pl.pallas_call(
        wrapped, name=name, grid=grid, in_specs=in_specs + [ANY] * n_cin, out_specs=out_specs + [ANY] * n_cout,
        out_shape=out_shape + list(comm.out_shape), scratch_shapes=scratch_shapes + list(comm.scratch),
        compiler_params=_cparams(dimension_semantics=("arbitrary",) * len(grid)),
    )(*args, *comm.inputs)
    return list(res[:n_out]), list(res[n_out:])


def _comm_only(comm, name):
    def body(*refs):
        n_cin, n_cout = len(comm.inputs), len(comm.out_shape)
        cins, couts, csems = refs[:n_cin], refs[n_cin:n_cin + n_cout], refs[n_cin + n_cout:]
        comm.start(cins, couts, csems)
        comm.wait(cins, couts, csems)

    return list(pl.pallas_call(
        body, name=name, in_specs=[ANY] * len(comm.inputs), out_specs=[ANY] * len(comm.out_shape),
        out_shape=list(comm.out_shape), scratch_shapes=list(comm.scratch),
    )(*comm.inputs))


def _mm(a, b, *, mode, out_dtype, tm, tn, tk=None, add=None, add_scale=1.0, name, comm=None):
    if mode in ("nn", "nt"):
        m, k = a.shape
        n = b.shape[1] if mode == "nn" else b.shape[0]
        assert m % tm == 0 and n % tn == 0
        dot = _dot if mode == "nn" else _dot_nt

        def body(*refs):
            if add is None:
                a_ref, b_ref, o_ref = refs
                o_ref[...] = dot(a_ref[...], b_ref[...]).astype(out_dtype)
            else:
                a_ref, b_ref, c_ref, o_ref = refs
                o_ref[...] = (dot(a_ref[...], b_ref[...]) + add_scale * c_ref[...]).astype(out_dtype)

        b_spec = (pl.BlockSpec((k, tn), lambda i, j: (0, j)) if mode == "nn"
                  else pl.BlockSpec((tn, k), lambda i, j: (j, 0)))
        in_specs = [pl.BlockSpec((tm, k), lambda i, j: (i, 0)), b_spec]
        args = [a, b]
        if add is not None:
            in_specs.append(pl.BlockSpec((tm, tn), lambda i, j: (i, j)))
            args.append(add)
        outs, couts = _pcall(
            body, name=name, grid=(m // tm, n // tn), in_specs=in_specs,
            out_specs=[pl.BlockSpec((tm, tn), lambda i, j: (i, j))],
            out_shape=[jax.ShapeDtypeStruct((m, n), out_dtype)], args=args, dims=("parallel", "parallel"),
            comm=comm)
        return outs[0] if comm is None else (outs[0], couts)
    assert mode == "tn" and add is None and comm is None
    kk, m = a.shape
    n = b.shape[1]
    assert m % tm == 0 and n % tn == 0 and kk % tk == 0
    nk = kk // tk

    def body(a_ref, b_ref, o_ref, acc_ref):
        kstep = pl.program_id(2)

        @pl.when(kstep == 0)
        def _():
            acc_ref[...] = jnp.zeros_like(acc_ref)

        acc_ref[...] += _dot_tn(a_ref[...], b_ref[...])

        @pl.when(kstep == nk - 1)
        def _():
            o_ref[...] = acc_ref[...].astype(out_dtype)

    return pl.pallas_call(
        body, name=name, grid=(m // tm, n // tn, nk),
        in_specs=[pl.BlockSpec((tk, tm), lambda i, j, s: (s, i)), pl.BlockSpec((tk, tn), lambda i, j, s: (s, j))],
        out_specs=pl.BlockSpec((tm, tn), lambda i, j, s: (i, j)),
        out_shape=jax.ShapeDtypeStruct((m, n), out_dtype),
        scratch_shapes=[pltpu.VMEM((tm, tn), F32)],
        compiler_params=_cparams(dimension_semantics=("parallel", "parallel", "arbitrary")),
    )(a, b)


def _mm2_nt(a1, b1, a2, b2, add, *, add_scale, tm, name, comm=None):
    m, k = a1.shape
    n = b1.shape[0]

    def body(a1_ref, b1_ref, a2_ref, b2_ref, c_ref, o_ref):
        o_ref[...] = (_dot_nt(a1_ref[...], b1_ref[...]) + _dot_nt(a2_ref[...], b2_ref[...])
                      + add_scale * c_ref[...])

    a_spec = pl.BlockSpec((tm, k), lambda i: (i, 0))
    b_spec = pl.BlockSpec((n, k), lambda i: (0, 0))
    o_spec = pl.BlockSpec((tm, n), lambda i: (i, 0))
    outs, couts = _pcall(body, name=name, grid=(m // tm,), in_specs=[a_spec, b_spec, a_spec, b_spec, o_spec],
                         out_specs=[o_spec], out_shape=[jax.ShapeDtypeStruct((m, n), F32)],
                         args=[a1, b1, a2, b2, add], dims=("parallel",), comm=comm)
    return outs[0], couts


def _ln_bwd(dy, r, g, *, t, name, want_bf16):
    s = r.shape[0]

    def body(dy_ref, r_ref, g_ref, *outs):
        i = pl.program_id(0)
        dr, dg, db = _ln_bwd_math(dy_ref[...], r_ref[...], g_ref[...])
        outs[0][...] = dr
        if want_bf16:
            outs[1][...] = dr.astype(BF16)
        st_ref = outs[-1]

        @pl.when(i == 0)
        def _():
            st_ref[...] = jnp.zeros_like(st_ref)

        st_ref[0:1, :] += dg
        st_ref[1:2, :] += db

    tile = pl.BlockSpec((t, D_MODEL), lambda i: (i, 0))
    out_specs = [tile] + ([tile] if want_bf16 else []) + [pl.BlockSpec((8, D_MODEL), lambda i: (0, 0))]
    out_shape = ([jax.ShapeDtypeStruct((s, D_MODEL), F32)]
                 + ([jax.ShapeDtypeStruct((s, D_MODEL), BF16)] if want_bf16 else [])
                 + [jax.ShapeDtypeStruct((8, D_MODEL), F32)])
    return pl.pallas_call(
        body, name=name, grid=(s // t,),
        in_specs=[tile, tile, pl.BlockSpec((1, D_MODEL), lambda i: (0, 0))],
        out_specs=out_specs, out_shape=out_shape,
        compiler_params=_cparams(dimension_semantics=("arbitrary",)),
    )(dy, r, g)


PROJ_COLS = 256


def _proj_segments():
    wd = DIL_SLOTS * HEAD_DIM
    segs = [(1, [(0, 1), (PROJ_COLS, 1), (2 * PROJ_COLS, 2)])]
    for gi, dil in enumerate(DILATIONS):
        blocks = []
        for part, kind in enumerate((1, 1, 0)):
            col = A_WIDTH + part * B_QKV + gi * wd
            blocks += [(col, kind), (col + PROJ_COLS, kind)]
        segs.append((dil, blocks))
    return segs


PROJ_SEGMENTS = _proj_segments()


def _proj_all(x, g, b, w_t, cs, e_mat, *, t, comm=None):
    s = x.shape[0]
    cb = PROJ_COLS
    halves = cb // LANES

    def body(x_ref, g_ref, b_ref, w_ref, cs_ref, e_ref, h_ref, *rest):
        z_refs, scr = rest[:-1], rest[-1]
        h = _ln(x_ref[...], g_ref[...], b_ref[...]).astype(BF16)
        h_ref[...] = h
        ta, tb, tc = (jnp.tile(tab, (1, halves)) for tab in _rope_tabs(cs_ref[...], e_ref[...]))
        lane = lax.broadcasted_iota(jnp.int32, (t, cb), 1)
        slot = 0
        for z_ref, (dil, blocks) in zip(z_refs, PROJ_SEGMENTS):
            for jb, (col, kind) in enumerate(blocks):
                acc = _dot_nt(h, w_ref[col:col + cb, :])
                if kind:
                    z = acc * ta + (pltpu.roll(acc, cb - 8, 1) * tb + pltpu.roll(acc, 8, 1) * tc)
                    if kind == 2:
                        z = jnp.where(lane < LANES, z, acc)
                else:
                    z = acc
                if dil == 1:
                    z_ref[0, :, cb * jb:cb * (jb + 1)] = z.astype(BF16)
                    continue
                for half in range(halves):
                    scr[slot, half] = z[:, half * LANES:(half + 1) * LANES]
                for c in range(dil):
                    for half in range(halves):
                        rows = scr[slot, half, pl.ds(c, t // dil, stride=dil), :]
                        z_ref[c, :, cb * jb + half * LANES:cb * jb + (half + 1) * LANES] = rows.astype(BF16)
                slot = 1 - slot

    row = pl.BlockSpec((1, D_MODEL), lambda i: (0, 0))
    widths = [cb * len(blocks) for _, blocks in PROJ_SEGMENTS]
    dils = [dil for dil, _ in PROJ_SEGMENTS]
    outs, couts = _pcall(
        body, name="proj_all", grid=(s // t,),
        in_specs=[pl.BlockSpec((t, D_MODEL), lambda i: (i, 0)), row, row,
                  pl.BlockSpec((IN_WIDTH, D_MODEL), lambda i: (0, 0)),
                  pl.BlockSpec((t, ROT_DIM), lambda i: (i, 0)), pl.BlockSpec((ROT_DIM, 3 * LANES), lambda i: (0, 0))],
        out_specs=[pl.BlockSpec((t, D_MODEL), lambda i: (i, 0))]
        + [pl.BlockSpec((dil, t // dil, wd), lambda i: (0, i, 0)) for dil, wd in zip(dils, widths)],
        out_shape=[jax.ShapeDtypeStruct((s, D_MODEL), BF16)]
        + [jax.ShapeDtypeStruct((dil, s // dil, wd), BF16) for dil, wd in zip(dils, widths)],
        args=[x, g, b, w_t, cs, e_mat], scratch_shapes=[pltpu.VMEM((2, halves, t, LANES), F32)],
        dims=("parallel",), comm=comm)
    return outs, couts


def _window_mask(i, tq, w, seq_len):
    tk = tq + 2 * w
    qpos = i * tq + lax.broadcasted_iota(jnp.int32, (tq, tk), 0)
    kpos = i * tq - w + lax.broadcasted_iota(jnp.int32, (tq, tk), 1)
    return (jnp.abs(qpos - kpos) <= w) & (kpos >= 0) & (kpos < seq_len)


def _swa_specs(tq, hq, hkv, n, qcol, kcol, vcol):
    qw, kw = hq * HEAD_DIM, hkv * HEAD_DIM
    cur = lambda s, i: jnp.minimum(i, n - 1)
    prv = lambda s, i: jnp.maximum(jnp.minimum(i, n - 1) - 1, 0)
    nxt = lambda s, i: jnp.minimum(i + 1, n - 1)
    q_spec = pl.BlockSpec((None, tq, qw), lambda s, i: (s, cur(s, i), qcol))
    kv_specs = [pl.BlockSpec((None, tq, kw), (lambda s, i, f=f, c=c: (s, f(s, i), c)))
                for c in (kcol, vcol) for f in (prv, cur, nxt)]
    return q_spec, kv_specs, cur, prv


def _swa_fwd(qkv, *, qcol, kcol, vcol, hq, hkv, w, tq, sink, name, comm=None):
    nseq, seq_len, _ = qkv.shape
    n = seq_len // tq
    rep = hq // hkv
    q_spec, kv_specs, _, _ = _swa_specs(tq, hq, hkv, n, qcol, kcol, vcol)

    def body(*refs):
        if sink is not None:
            sink_ref, refs = refs[0], refs[1:]
        q_ref, kp_ref, kc_ref, kn_ref, vp_ref, vc_ref, vn_ref, o_ref, lse_ref = refs
        i = pl.program_id(1)
        mask = _window_mask(i, tq, w, seq_len)
        lane = lax.broadcasted_iota(jnp.int32, (tq, LANES), 1)
        lse_acc = jnp.zeros((tq, LANES), F32)
        for g in range(hkv):
            cs = slice(g * HEAD_DIM, (g + 1) * HEAD_DIM)
            kcat = jnp.concatenate([kp_ref[tq - w:, cs], kc_ref[:, cs], kn_ref[:w, cs]], axis=0)
            vcat = jnp.concatenate([vp_ref[tq - w:, cs], vc_ref[:, cs], vn_ref[:w, cs]], axis=0)
            for r in range(rep):
                h = g * rep + r
                hs = slice(h * HEAD_DIM, (h + 1) * HEAD_DIM)
                qh = q_ref[:, hs] * 0.125
                sc = jnp.where(mask, _dot_nt(qh, kcat), NEG_INF)
                m = jnp.max(sc, axis=1, keepdims=True)
                if sink is not None:
                    m = jnp.maximum(m, sink_ref[0, h])
                p = jnp.exp(sc - m)
                den = jnp.sum(p, axis=1, keepdims=True)
                if sink is not None:
                    den = den + jnp.exp(sink_ref[0, h] - m)
                o_ref[:, hs] = _dot(p.astype(BF16), vcat) / den
                lse_acc = jnp.where(lane == h, m + jnp.log(den), lse_acc)
        lse_ref[...] = lse_acc

    in_specs = [q_spec] + kv_specs
    args = [qkv] * 7
    if sink is not None:
        in_specs = [pl.BlockSpec(memory_space=pltpu.SMEM)] + in_specs
        args = [sink] + args
    (o, lse), couts = _pcall(
        body, name=name, grid=(nseq, n), in_specs=in_specs,
        out_specs=[pl.BlockSpec((None, tq, hq * HEAD_DIM), lambda s, i: (s, i, 0)),
                   pl.BlockSpec((None, tq, LANES), lambda s, i: (s, i, 0))],
        out_shape=[jax.ShapeDtypeStruct((nseq, seq_len, hq * HEAD_DIM), F32),
                   jax.ShapeDtypeStruct((nseq, seq_len, LANES), F32)],
        args=args, dims=("parallel", "parallel"), comm=comm)
    return o, lse, couts


def _swa_bwd(qkv, do, lse, delta, cs, e_mat, *, qcol, kcol, vcol, hq, hkv, w, tq, sink, name, comm=None):
    nseq, seq_len, _ = qkv.shape
    n = seq_len // tq
    rep = hq // hkv
    qw, kw = hq * HEAD_DIM, hkv * HEAD_DIM
    tk = tq + 2 * w
    q_spec, kv_specs, cur, prv = _swa_specs(tq, hq, hkv, n, qcol, kcol, vcol)

    def body(*refs):
        if sink is not None:
            sink_ref, refs = refs[0], refs[1:]
        (q_ref, kp_ref, kc_ref, kn_ref, vp_ref, vc_ref, vn_ref, do_ref, lse_ref, dl_ref,
         cs_c, cs_p, e_ref) = refs[:13]
        outs = refs[13:]
        if sink is not None:
            dq_ref, dk_ref, dv_ref, dsink_ref, dk_acc, dv_acc = outs
        else:
            dq_ref, dk_ref, dv_ref, dk_acc, dv_acc = outs
        s_id = pl.program_id(0)
        i = pl.program_id(1)
        slot_p, slot_c, slot_n = (i + 2) % 3, i % 3, (i + 1) % 3

        if sink is not None:
            @pl.when((s_id == 0) & (i == 0))
            def _():
                dsink_ref[...] = jnp.zeros_like(dsink_ref)

        @pl.when(i < n)
        def _():
            mask = _window_mask(i, tq, w, seq_len)
            dk_acc[slot_n] = jnp.zeros((tq, kw), F32)
            dv_acc[slot_n] = jnp.zeros((tq, kw), F32)

            @pl.when(i == 0)
            def _():
                dk_acc[slot_c] = jnp.zeros((tq, kw), F32)
                dv_acc[slot_c] = jnp.zeros((tq, kw), F32)

            dq_parts, dk_parts, dv_parts = [], [], []
            for g in range(hkv):
                cs = slice(g * HEAD_DIM, (g + 1) * HEAD_DIM)
                kcat = jnp.concatenate([kp_ref[tq - w:, cs], kc_ref[:, cs], kn_ref[:w, cs]], axis=0)
                vcat = jnp.concatenate([vp_ref[tq - w:, cs], vc_ref[:, cs], vn_ref[:w, cs]], axis=0)
                dkc = jnp.zeros((tk, HEAD_DIM), F32)
                dvc = jnp.zeros((tk, HEAD_DIM), F32)
                for r in range(rep):
                    h = g * rep + r
                    hs = slice(h * HEAD_DIM, (h + 1) * HEAD_DIM)
                    qh = q_ref[:, hs] * 0.125
                    sc = jnp.where(mask, _dot_nt(qh, kcat), NEG_INF)
                    lse_h = lse_ref[:, h:h + 1]
                    dl_h = dl_ref[:, h:h + 1]
                    p = jnp.exp(sc - lse_h)
                    doh = do_ref[:, hs]
                    dp = _dot_nt(doh, vcat)
                    dsb = (p * (dp - dl_h)).astype(BF16)
                    dq_parts.append(_dot(dsb, kcat) * 0.125)
                    dkc = dkc + _dot_tn(dsb, qh)
                    dvc = dvc + _dot_tn(p.astype(BF16), doh)
                    if sink is not None:
                        ds_sink = -jnp.sum(jnp.exp(sink_ref[0, h] - lse_h) * dl_h)
                        dsink_ref[h:h + 1, :] += jnp.full((1, LANES), ds_sink, F32)
                dk_parts.append(dkc)
                dv_parts.append(dvc)
            dq = jnp.concatenate(dq_parts, axis=1)
            dq_ref[...] = _rope(dq, *_rope_tabs(cs_c[...], e_ref[...]), -1.0).astype(BF16)
            dk_all = jnp.concatenate(dk_parts, axis=1)
            dv_all = jnp.concatenate(dv_parts, axis=1)

            @pl.when(i > 0)
            def _():
                dk_acc[slot_p, tq - w:, :] += dk_all[:w]
                dv_acc[slot_p, tq - w:, :] += dv_all[:w]

            dk_acc[slot_c] += dk_all[w:w + tq]
            dv_acc[slot_c] += dv_all[w:w + tq]
            dk_acc[slot_n, :w, :] += dk_all[w + tq:]
            dv_acc[slot_n, :w, :] += dv_all[w + tq:]

        @pl.when(i >= 1)
        def _():
            dk_ref[...] = _rope(dk_acc[slot_p], *_rope_tabs(cs_p[...], e_ref[...]), -1.0).astype(BF16)
            dv_ref[...] = dv_acc[slot_p].astype(BF16)

    row_c = lambda width: pl.BlockSpec((None, tq, width), lambda s, i: (s, cur(s, i), 0))
    row_p = lambda width: pl.BlockSpec((None, tq, width), lambda s, i: (s, jnp.maximum(i - 1, 0), 0))
    in_specs = ([q_spec] + kv_specs + [row_c(qw), row_c(LANES), row_c(LANES), row_c(ROT_DIM), row_p(ROT_DIM),
                                       pl.BlockSpec((ROT_DIM, 3 * LANES), lambda s, i: (0, 0))])
    args = [qkv] * 7 + [do, lse, delta, cs, cs, e_mat]
    out_specs = [row_c(qw), row_p(kw), row_p(kw)]
    out_shape = [jax.ShapeDtypeStruct((nseq, seq_len, qw), BF16),
                 jax.ShapeDtypeStruct((nseq, seq_len, kw), BF16),
                 jax.ShapeDtypeStruct((nseq, seq_len, kw), BF16)]
    if sink is not None:
        in_specs = [pl.BlockSpec(memory_space=pltpu.SMEM)] + in_specs
        args = [sink] + args
        out_specs.append(pl.BlockSpec((8, LANES), lambda s, i: (0, 0)))
        out_shape.append(jax.ShapeDtypeStruct((8, LANES), F32))
    return _pcall(
        body, name=name, grid=(nseq, n + 1), in_specs=in_specs, out_specs=out_specs, out_shape=out_shape,
        scratch_shapes=[pltpu.VMEM((3, tq, kw), F32), pltpu.VMEM((3, tq, kw), F32)], args=args,
        dims=("arbitrary", "arbitrary"), comm=comm)


PAIR = 2 * HEAD_DIM


def _window_mask_t(i, tq, w, seq_len):
    tk = tq + 2 * w
    kpos = i * tq - w + lax.broadcasted_iota(jnp.int32, (tk, tq), 0)
    qpos = i * tq + lax.broadcasted_iota(jnp.int32, (tk, tq), 1)
    return (jnp.abs(qpos - kpos) <= w) & (kpos >= 0) & (kpos < seq_len)


def _place_head(x2, src_pos, dst_pos):
    hi = lax.broadcasted_iota(jnp.int32, x2.shape, 1) >= HEAD_DIM
    src = x2 if src_pos == dst_pos else pltpu.roll(x2, HEAD_DIM, 1)
    return jnp.where(hi == (dst_pos == 1), src, jnp.zeros_like(src))


def _swa_fwd_t(qkv, *, qcol, kcol, vcol, hq, hkv, w, tq, sink, name, comm=None):
    nseq, seq_len, _ = qkv.shape
    n = seq_len // tq
    rep = hq // hkv
    q_spec, kv_specs, _, _ = _swa_specs(tq, hq, hkv, n, qcol, kcol, vcol)

    def body(*refs):
        if sink is not None:
            sink_ref, refs = refs[0], refs[1:]
        q_ref, kp_ref, kc_ref, kn_ref, vp_ref, vc_ref, vn_ref, o_ref, lse_ref = refs
        i = pl.program_id(1)
        mask_t = _window_mask_t(i, tq, w, seq_len)
        o_t = [None] * (hq // 2)
        lse_rows = [None] * hq
        for a in range(hkv // 2):
            ls = slice(a * PAIR, (a + 1) * PAIR)
            kcat = jnp.concatenate([kp_ref[tq - w:, ls], kc_ref[:, ls], kn_ref[:w, ls]], axis=0) * 0.125
            vcat = jnp.concatenate([vp_ref[tq - w:, ls], vc_ref[:, ls], vn_ref[:w, ls]], axis=0)
            for e in range(2):
                g = 2 * a + e
                placed = {}
                for r in range(rep):
                    h = g * rep + r
                    qp, pos = h // 2, h % 2
                    if pos not in placed:
                        placed[pos] = (_place_head(kcat, e, pos), _place_head(vcat, e, pos))
                    k_g, v_g = placed[pos]
                    s_t = jnp.where(mask_t, _dot_nt(k_g, q_ref[:, qp * PAIR:(qp + 1) * PAIR]), NEG_INF)
                    m = jnp.max(s_t, axis=0, keepdims=True)
                    if sink is not None:
                        m = jnp.maximum(m, sink_ref[0, h])
                    p_t = jnp.exp(s_t - m)
                    den = jnp.sum(p_t, axis=0, keepdims=True)
                    if sink is not None:
                        den = den + jnp.exp(sink_ref[0, h] - m)
                    part = _dot_tn(v_g, p_t.astype(BF16)) / den
                    o_t[qp] = part if o_t[qp] is None else o_t[qp] + part
                    lse_rows[h] = m + jnp.log(den)
        o_ref[...] = jnp.concatenate(o_t, axis=0).T
        lse_ref[...] = jnp.concatenate(lse_rows, axis=0)

    in_specs = [q_spec] + kv_specs
    args = [qkv] * 7
    if sink is not None:
        in_specs = [pl.BlockSpec(memory_space=pltpu.SMEM)] + in_specs
        args = [sink] + args
    (o, lse), couts = _pcall(
        body, name=name, grid=(nseq, n), in_specs=in_specs,
        out_specs=[pl.BlockSpec((None, tq, hq * HEAD_DIM), lambda s, i: (s, i, 0)),
                   pl.BlockSpec((None, hq, tq), lambda s, i: (s, 0, i))],
        out_shape=[jax.ShapeDtypeStruct((nseq, seq_len, hq * HEAD_DIM), F32),
                   jax.ShapeDtypeStruct((nseq, hq, seq_len), F32)],
        args=args, dims=("parallel", "parallel"), comm=comm)
    return o, lse, couts


def _swa_bwd_t(qkv, do, lse, delta, cs, e_mat, *, qcol, kcol, vcol, hq, hkv, w, tq, sink, name, comm=None):
    nseq, seq_len, _ = qkv.shape
    n = seq_len // tq
    rep = hq // hkv
    qw, kw = hq * HEAD_DIM, hkv * HEAD_DIM
    tk = tq + 2 * w
    q_spec, kv_specs, cur, prv = _swa_specs(tq, hq, hkv, n, qcol, kcol, vcol)

    def body(*refs):
        if sink is not None:
            sink_ref, refs = refs[0], refs[1:]
        (q_ref, kp_ref, kc_ref, kn_ref, vp_ref, vc_ref, vn_ref, do_ref, lse_ref, dl_ref,
         cs_c, cs_p, e_ref) = refs[:13]
        outs = refs[13:]
        if sink is not None:
            dq_ref, dk_ref, dv_ref, dsink_ref, dk_acc, dv_acc = outs
        else:
            dq_ref, dk_ref, dv_ref, dk_acc, dv_acc = outs
        s_id = pl.program_id(0)
        i = pl.program_id(1)
        slot_p, slot_c, slot_n = (i + 2) % 3, i % 3, (i + 1) % 3

        if sink is not None:
            @pl.when((s_id == 0) & (i == 0))
            def _():
                dsink_ref[...] = jnp.zeros_like(dsink_ref)

        @pl.when(i < n)
        def _():
            mask_t = _window_mask_t(i, tq, w, seq_len)
            dk_acc[slot_n] = jnp.zeros((tq, kw), F32)
            dv_acc[slot_n] = jnp.zeros((tq, kw), F32)

            @pl.when(i == 0)
            def _():
                dk_acc[slot_c] = jnp.zeros((tq, kw), F32)
                dv_acc[slot_c] = jnp.zeros((tq, kw), F32)

            dq_t = [None] * (hq // 2)
            dk_pairs, dv_pairs = [], []
            for a in range(hkv // 2):
                ls = slice(a * PAIR, (a + 1) * PAIR)
                kcat = jnp.concatenate([kp_ref[tq - w:, ls], kc_ref[:, ls], kn_ref[:w, ls]], axis=0) * 0.125
                vcat = jnp.concatenate([vp_ref[tq - w:, ls], vc_ref[:, ls], vn_ref[:w, ls]], axis=0)
                dk2 = jnp.zeros((tk, PAIR), F32)
                dv2 = jnp.zeros((tk, PAIR), F32)
                for e in range(2):
                    g = 2 * a + e
                    placed = {}
                    for r in range(rep):
                        h = g * rep + r
                        qp, pos = h // 2, h % 2
                        if pos not in placed:
                            placed[pos] = (_place_head(kcat, e, pos), _place_head(vcat, e, pos))
                        k_g, v_g = placed[pos]
                        q2 = q_ref[:, qp * PAIR:(qp + 1) * PAIR]
                        do2 = do_ref[:, qp * PAIR:(qp + 1) * PAIR]
                        lse_h = lse_ref[h:h + 1, :]
                        dl_h = dl_ref[h:h + 1, :]
                        p_t = jnp.exp(jnp.where(mask_t, _dot_nt(k_g, q2), NEG_INF) - lse_h)
                        dp_t = _dot_nt(v_g, do2)
                        dsb = (p_t * (dp_t - dl_h)).astype(BF16)
                        part = _dot_tn(k_g, dsb)
                        dq_t[qp] = part if dq_t[qp] is None else dq_t[qp] + part
                        dk2 = dk2 + _dot(dsb, _place_head(q2, pos, e) * 0.125)
                        dv2 = dv2 + _dot(p_t.astype(BF16), _place_head(do2, pos, e))
                        if sink is not None:
                            ds_sink = -jnp.sum(jnp.exp(sink_ref[0, h] - lse_h) * dl_h)
                            dsink_ref[h:h + 1, :] += jnp.full((1, LANES), ds_sink, F32)
                dk_pairs.append(dk2)
                dv_pairs.append(dv2)
            dq = jnp.concatenate(dq_t, axis=0).T
            dq_ref[...] = _rope(dq, *_rope_tabs(cs_c[...], e_ref[...]), -1.0).astype(BF16)
            dk_all = dk_pairs[0] if len(dk_pairs) == 1 else jnp.concatenate(dk_pairs, axis=1)
            dv_all = dv_pairs[0] if len(dv_pairs) == 1 else jnp.concatenate(dv_pairs, axis=1)

            @pl.when(i > 0)
            def _():
                dk_acc[slot_p, tq - w:, :] += dk_all[:w]
                dv_acc[slot_p, tq - w:, :] += dv_all[:w]

            dk_acc[slot_c] += dk_all[w:w + tq]
            dv_acc[slot_c] += dv_all[w:w + tq]
            dk_acc[slot_n, :w, :] += dk_all[w + tq:]
            dv_acc[slot_n, :w, :] += dv_all[w + tq:]

        @pl.when(i >= 1)
        def _():
            dk_ref[...] = _rope(dk_acc[slot_p], *_rope_tabs(cs_p[...], e_ref[...]), -1.0).astype(BF16)
            dv_ref[...] = dv_acc[slot_p].astype(BF16)

    row_c = lambda width: pl.BlockSpec((None, tq, width), lambda s, i: (s, cur(s, i), 0))
    row_p = lambda width: pl.BlockSpec((None, tq, width), lambda s, i: (s, jnp.maximum(i - 1, 0), 0))
    stat = pl.BlockSpec((None, hq, tq), lambda s, i: (s, 0, cur(s, i)))
    in_specs = ([q_spec] + kv_specs + [row_c(qw), stat, stat, row_c(ROT_DIM), row_p(ROT_DIM),
                                       pl.BlockSpec((ROT_DIM, 3 * LANES), lambda s, i: (0, 0))])
    args = [qkv] * 7 + [do, lse, delta, cs, cs, e_mat]
    out_specs = [row_c(qw), row_p(kw), row_p(kw)]
    out_shape = [jax.ShapeDtypeStruct((nseq, seq_len, qw), BF16),
                 jax.ShapeDtypeStruct((nseq, seq_len, kw), BF16),
                 jax.ShapeDtypeStruct((nseq, seq_len, kw), BF16)]
    if sink is not None:
        in_specs = [pl.BlockSpec(memory_space=pltpu.SMEM)] + in_specs
        args = [sink] + args
        out_specs.append(pl.BlockSpec((8, LANES), lambda s, i: (0, 0)))
        out_shape.append(jax.ShapeDtypeStruct((8, LANES), F32))
    return _pcall(
        body, name=name, grid=(nseq, n + 1), in_specs=in_specs, out_specs=out_specs, out_shape=out_shape,
        scratch_shapes=[pltpu.VMEM((3, tq, kw), F32), pltpu.VMEM((3, tq, kw), F32)], args=args,
        dims=("arbitrary", "arbitrary"), comm=comm)


def _band_mask_t(row0, tq, w, seq_len):
    tk = tq + 2 * w
    kk = lax.broadcasted_iota(jnp.int32, (tk, tq), 0)
    qq = lax.broadcasted_iota(jnp.int32, (tk, tq), 1)
    kpos = row0 - w + kk
    return (jnp.abs(qq + w - kk) <= w) & (kpos >= 0) & (kpos < seq_len)


def _halo_kv_specs(t, w, hkv, n, seq_len, kcol, vcol):
    kw = hkv * HEAD_DIM
    per, last = t // w, seq_len // w - 1
    cur = lambda s, i: jnp.minimum(i, n - 1)
    specs = []
    for c in (kcol, vcol):
        specs += [pl.BlockSpec((None, w, kw), lambda s, i, c=c: (s, jnp.maximum(cur(s, i) * per - 1, 0), c)),
                  pl.BlockSpec((None, t, kw), lambda s, i, c=c: (s, cur(s, i), c)),
                  pl.BlockSpec((None, w, kw), lambda s, i, c=c: (s, jnp.minimum((cur(s, i) + 1) * per, last), c))]
    return specs, cur


def _swa_fwd_s(qkv, *, qcol, kcol, vcol, hq, hkv, w, tq, sub, sink, name, comm=None):
    nseq, seq_len, _ = qkv.shape
    t = tq * sub
    n = seq_len // t
    rep = hq // hkv
    tk = tq + 2 * w
    kv_specs, cur = _halo_kv_specs(t, w, hkv, n, seq_len, kcol, vcol)

    def body(*refs):
        if sink is not None:
            sink_ref, refs = refs[0], refs[1:]
        q_ref, kp_ref, kc_ref, kn_ref, vp_ref, vc_ref, vn_ref, o_ref, lse_ref = refs
        i = pl.program_id(1)
        kfull, vfull = [], []
        for a in range(hkv // 2):
            ls = slice(a * PAIR, (a + 1) * PAIR)
            kfull.append(jnp.concatenate([kp_ref[:, ls], kc_ref[:, ls], kn_ref[:, ls]], axis=0) * 0.125)
            vfull.append(jnp.concatenate([vp_ref[:, ls], vc_ref[:, ls], vn_ref[:, ls]], axis=0))
        for jj in range(sub):
            rows = slice(jj * tq, (jj + 1) * tq)
            mask_t = _band_mask_t(i * t + jj * tq, tq, w, seq_len)
            o_t = [None] * (hq // 2)
            lse_rows = [None] * hq
            for a in range(hkv // 2):
                kcat = kfull[a][jj * tq:jj * tq + tk]
                vcat = vfull[a][jj * tq:jj * tq + tk]
                for e in range(2):
                    g = 2 * a + e
                    placed = {}
                    for r in range(rep):
                        h = g * rep + r
                        qp, pos = h // 2, h % 2
                        if pos not in placed:
                            placed[pos] = (_place_head(kcat, e, pos), _place_head(vcat, e, pos))
                        k_g, v_g = placed[pos]
                        s_t = jnp.where(mask_t, _dot_nt(k_g, q_ref[rows, qp * PAIR:(qp + 1) * PAIR]), NEG_INF)
                        m = jnp.max(s_t, axis=0, keepdims=True)
                        if sink is not None:
                            m = jnp.maximum(m, sink_ref[0, h])
                        p_t = jnp.exp(s_t - m)
                        den = jnp.sum(p_t, axis=0, keepdims=True)
                        if sink is not None:
                            den = den + jnp.exp(sink_ref[0, h] - m)
                        part = _dot_tn(v_g, p_t.astype(BF16)) / den
                        o_t[qp] = part if o_t[qp] is None else o_t[qp] + part
                        lse_rows[h] = m + jnp.log(den)
            o_ref[rows, :] = jnp.concatenate(o_t, axis=0).T
            lse_ref[:, rows] = jnp.concatenate(lse_rows, axis=0)

    in_specs = [pl.BlockSpec((None, t, hq * HEAD_DIM), lambda s, i: (s, i, qcol))] + kv_specs
    args = [qkv] * 7
    if sink is not None:
        in_specs = [pl.BlockSpec(memory_space=pltpu.SMEM)] + in_specs
        args = [sink] + args
    (o, lse), couts = _pcall(
        body, name=name, grid=(nseq, n), in_specs=in_specs,
        out_specs=[pl.BlockSpec((None, t, hq * HEAD_DIM), lambda s, i: (s, i, 0)),
                   pl.BlockSpec((None, hq, t), lambda s, i: (s, 0, i))],
        out_shape=[jax.ShapeDtypeStruct((nseq, seq_len, hq * HEAD_DIM), F32),
                   jax.ShapeDtypeStruct((nseq, hq, seq_len), F32)],
        args=args, dims=("parallel", "parallel"), comm=comm)
    return o, lse, couts


def _swa_bwd_s(qkv, do, lse, delta, cs, e_mat, *, qcol, kcol, vcol, hq, hkv, w, tq, sub, sink, name, comm=None):
    nseq, seq_len, _ = qkv.shape
    t = tq * sub
    n = seq_len // t
    rep = hq // hkv
    qw, kw = hq * HEAD_DIM, hkv * HEAD_DIM
    tk = tq + 2 * w
    kv_specs, cur = _halo_kv_specs(t, w, hkv, n, seq_len, kcol, vcol)

    def body(*refs):
        if sink is not None:
            sink_ref, refs = refs[0], refs[1:]
        (q_ref, kp_ref, kc_ref, kn_ref, vp_ref, vc_ref, vn_ref, do_ref, lse_ref, dl_ref,
         cs_c, cs_p, e_ref) = refs[:13]
        outs = refs[13:]
        if sink is not None:
            dq_ref, dk_ref, dv_ref, dsink_ref, dk_acc, dv_acc, dk_win, dv_win = outs
        else:
            dq_ref, dk_ref, dv_ref, dk_acc, dv_acc, dk_win, dv_win = outs
        s_id = pl.program_id(0)
        i = pl.program_id(1)
        slot_p, slot_c, slot_n = (i + 2) % 3, i % 3, (i + 1) % 3

        if sink is not None:
            @pl.when((s_id == 0) & (i == 0))
            def _():
                dsink_ref[...] = jnp.zeros_like(dsink_ref)

        @pl.when(i < n)
        def _():
            dk_win[...] = jnp.zeros_like(dk_win)
            dv_win[...] = jnp.zeros_like(dv_win)
            kfull, vfull = [], []
            for a in range(hkv // 2):
                ls = slice(a * PAIR, (a + 1) * PAIR)
                kfull.append(jnp.concatenate([kp_ref[:, ls], kc_ref[:, ls], kn_ref[:, ls]], axis=0) * 0.125)
                vfull.append(jnp.concatenate([vp_ref[:, ls], vc_ref[:, ls], vn_ref[:, ls]], axis=0))
            for jj in range(sub):
                rows = slice(jj * tq, (jj + 1) * tq)
                krows = slice(jj * tq, jj * tq + tk)
                mask_t = _band_mask_t(i * t + jj * tq, tq, w, seq_len)
                dq_t = [None] * (hq // 2)
                for a in range(hkv // 2):
                    ls = slice(a * PAIR, (a + 1) * PAIR)
                    kcat, vcat = kfull[a][krows], vfull[a][krows]
                    dk2 = jnp.zeros((tk, PAIR), F32)
                    dv2 = jnp.zeros((tk, PAIR), F32)
                    for e in range(2):
                        g = 2 * a + e
                        placed = {}
                        for r in range(rep):
                            h = g * rep + r
                            qp, pos = h // 2, h % 2
                            if pos not in placed:
                                placed[pos] = (_place_head(kcat, e, pos), _place_head(vcat, e, pos))
                            k_g, v_g = placed[pos]
                            q2 = q_ref[rows, qp * PAIR:(qp + 1) * PAIR]
                            do2 = do_ref[rows, qp * PAIR:(qp + 1) * PAIR]
                            lse_h = lse_ref[h:h + 1, rows]
                            dl_h = dl_ref[h:h + 1, rows]
                            p_t = jnp.exp(jnp.where(mask_t, _dot_nt(k_g, q2), NEG_INF) - lse_h)
                            dp_t = _dot_nt(v_g, do2)
                            dsb = (p_t * (dp_t - dl_h)).astype(BF16)
                            part = _dot_tn(k_g, dsb)
                            dq_t[qp] = part if dq_t[qp] is None else dq_t[qp] + part
                            dk2 = dk2 + _dot(dsb, _place_head(q2, pos, e) * 0.125)
                            dv2 = dv2 + _dot(p_t.astype(BF16), _place_head(do2, pos, e))
                            if sink is not None:
                                ds_sink = -jnp.sum(jnp.exp(sink_ref[0, h] - lse_h) * dl_h)
                                dsink_ref[h:h + 1, :] += jnp.full((1, LANES), ds_sink, F32)
                    dk_win[krows, ls] += dk2
                    dv_win[krows, ls] += dv2
                dq = jnp.concatenate(dq_t, axis=0).T
                dq_ref[rows, :] = _rope(dq, *_rope_tabs(cs_c[rows, :], e_ref[...]), -1.0).astype(BF16)

            @pl.when(i > 0)
            def _():
                dk_acc[slot_p, t - w:, :] += dk_win[:w, :]
                dv_acc[slot_p, t - w:, :] += dv_win[:w, :]

            @pl.when(i == 0)
            def _():
                dk_acc[slot_c] = dk_win[w:w + t, :]
                dv_acc[slot_c] = dv_win[w:w + t, :]

            @pl.when(i > 0)
            def _():
                dk_acc[slot_c] += dk_win[w:w + t, :]
                dv_acc[slot_c] += dv_win[w:w + t, :]

            dk_acc[slot_n] = jnp.zeros((t, kw), F32)
            dv_acc[slot_n] = jnp.zeros((t, kw), F32)
            dk_acc[slot_n, :w, :] = dk_win[w + t:, :]
            dv_acc[slot_n, :w, :] = dv_win[w + t:, :]

        @pl.when(i >= 1)
        def _():
            dk_ref[...] = _rope(dk_acc[slot_p], *_rope_tabs(cs_p[...], e_ref[...]), -1.0).astype(BF16)
            dv_ref[...] = dv_acc[slot_p].astype(BF16)

    row_c = lambda width: pl.BlockSpec((None, t, width), lambda s, i: (s, cur(s, i), 0))
    row_p = lambda width: pl.BlockSpec((None, t, width), lambda s, i: (s, jnp.maximum(i - 1, 0), 0))
    stat = pl.BlockSpec((None, hq, t), lambda s, i: (s, 0, cur(s, i)))
    in_specs = ([pl.BlockSpec((None, t, qw), lambda s, i: (s, cur(s, i), qcol))] + kv_specs
                + [row_c(qw), stat, stat, row_c(ROT_DIM), row_p(ROT_DIM),
                   pl.BlockSpec((ROT_DIM, 3 * LANES), lambda s, i: (0, 0))])
    args = [qkv] * 7 + [do, lse, delta, cs, cs, e_mat]
    out_specs = [row_c(qw), row_p(kw), row_p(kw)]
    out_shape = [jax.ShapeDtypeStruct((nseq, seq_len, qw), BF16),
                 jax.ShapeDtypeStruct((nseq, seq_len, kw), BF16),
                 jax.ShapeDtypeStruct((nseq, seq_len, kw), BF16)]
    if sink is not None:
        in_specs = [pl.BlockSpec(memory_space=pltpu.SMEM)] + in_specs
        args = [sink] + args
        out_specs.append(pl.BlockSpec((8, LANES), lambda s, i: (0, 0)))
        out_shape.append(jax.ShapeDtypeStruct((8, LANES), F32))
    return _pcall(
        body, name=name, grid=(nseq, n + 1), in_specs=in_specs, out_specs=out_specs, out_shape=out_shape,
        scratch_shapes=[pltpu.VMEM((3, t, kw), F32), pltpu.VMEM((3, t, kw), F32),
                        pltpu.VMEM((t + 2 * w, kw), F32), pltpu.VMEM((t + 2 * w, kw), F32)], args=args,
        dims=("arbitrary", "arbitrary"), comm=comm)


def _rms_parts(o, g):
    ms = jnp.mean(o * o, axis=-1, keepdims=True) + LN_EPS
    rinv = lax.rsqrt(ms)
    return o * rinv * g, rinv


def _pair_kv(kfull, vfull, qp, rep, krows):
    ks, vs, a_of = [], [], []
    for pos in range(2):
        g = (2 * qp + pos) // rep
        a_of.append(g // 2)
        ks.append(_place_head(kfull[g // 2][krows], g % 2, pos))
        vs.append(_place_head(vfull[g // 2][krows], g % 2, pos))
    assert a_of[0] == a_of[1]
    return jnp.concatenate(ks, axis=0), jnp.concatenate(vs, axis=0), a_of[0]


def _swa_fwd_p(qkv, *, qcol, kcol, vcol, hq, hkv, w, tq, sub, sink, name, comm=None):
    nseq, seq_len, _ = qkv.shape
    t = tq * sub
    n = seq_len // t
    rep = hq // hkv
    tk = tq + 2 * w
    kv_specs, cur = _halo_kv_specs(t, w, hkv, n, seq_len, kcol, vcol)

    def body(*refs):
        if sink is not None:
            sink_ref, refs = refs[0], refs[1:]
        q_ref, kp_ref, kc_ref, kn_ref, vp_ref, vc_ref, vn_ref, o_ref, lse_ref = refs
        i = pl.program_id(1)
        kfull, vfull = [], []
        for a in range(hkv // 2):
            ls = slice(a * PAIR, (a + 1) * PAIR)
            kfull.append(jnp.concatenate([kp_ref[:, ls], kc_ref[:, ls], kn_ref[:, ls]], axis=0) * 0.125)
            vfull.append(jnp.concatenate([vp_ref[:, ls], vc_ref[:, ls], vn_ref[:, ls]], axis=0))
        row_hi = lax.broadcasted_iota(jnp.int32, (PAIR, tq), 0) >= HEAD_DIM
        for jj in range(sub):
            rows = slice(jj * tq, (jj + 1) * tq)
            mask_t = _band_mask_t(i * t + jj * tq, tq, w, seq_len)
            o_t, lse_rows = [], []
            for qp in range(hq // 2):
                kst, vst, _ = _pair_kv(kfull, vfull, qp, rep, slice(jj * tq, jj * tq + tk))
                s2 = _dot_nt(kst, q_ref[rows, qp * PAIR:(qp + 1) * PAIR])
                ps, dens = [], []
                for pos in range(2):
                    h = 2 * qp + pos
                    s_t = jnp.where(mask_t, s2[pos * tk:(pos + 1) * tk], NEG_INF)
                    m = jnp.max(s_t, axis=0, keepdims=True)
                    if sink is not None:
                        m = jnp.maximum(m, sink_ref[0, h])
                    p_t = jnp.exp(s_t - m)
                    den = jnp.sum(p_t, axis=0, keepdims=True)
                    if sink is not None:
                        den = den + jnp.exp(sink_ref[0, h] - m)
                    ps.append(p_t.astype(BF16))
                    dens.append(den)
                    lse_rows.append(m + jnp.log(den))
                both = _dot_tn(vst, jnp.concatenate(ps, axis=0))
                o_t.append(both / jnp.where(row_hi, dens[1], dens[0]))
            o_ref[rows, :] = jnp.concatenate(o_t, axis=0).T
            lse_ref[:, rows] = jnp.concatenate(lse_rows, axis=0)

    in_specs = [pl.BlockSpec((None, t, hq * HEAD_DIM), lambda s, i: (s, i, qcol))] + kv_specs
    args = [qkv] * 7
    if sink is not None:
        in_specs = [pl.BlockSpec(memory_space=pltpu.SMEM)] + in_specs
        args = [sink] + args
    (o, lse), couts = _pcall(
        body, name=name, grid=(nseq, n), in_specs=in_specs,
        out_specs=[pl.BlockSpec((None, t, hq * HEAD_DIM), lambda s, i: (s, i, 0)),
                   pl.BlockSpec((None, hq, t), lambda s, i: (s, 0, i))],
        out_shape=[jax.ShapeDtypeStruct((nseq, seq_len, hq * HEAD_DIM), F32),
                   jax.ShapeDtypeStruct((nseq, hq, seq_len), F32)],
        args=args, dims=("parallel", "parallel"), comm=comm)
    return o, lse, couts


def _swa_bwd_p(qkv, do, lse, delta, cs, e_mat, *, qcol, kcol, vcol, hq, hkv, w, tq, sub, sink, name, comm=None):
    nseq, seq_len, _ = qkv.shape
    t = tq * sub
    n = seq_len // t
    rep = hq // hkv
    qw, kw = hq * HEAD_DIM, hkv * HEAD_DIM
    tk = tq + 2 * w
    kv_specs, cur = _halo_kv_specs(t, w, hkv, n, seq_len, kcol, vcol)

    def body(*refs):
        if sink is not None:
            sink_ref, refs = refs[0], refs[1:]
        (q_ref, kp_ref, kc_ref, kn_ref, vp_ref, vc_ref, vn_ref, do_ref, lse_ref, dl_ref,
         cs_c, cs_p, e_ref) = refs[:13]
        outs = refs[13:]
        if sink is not None:
            dq_ref, dk_ref, dv_ref, dsink_ref, dk_acc, dv_acc, dk_win, dv_win = outs
        else:
            dq_ref, dk_ref, dv_ref, dk_acc, dv_acc, dk_win, dv_win = outs
        s_id = pl.program_id(0)
        i = pl.program_id(1)
        slot_p, slot_c, slot_n = (i + 2) % 3, i % 3, (i + 1) % 3

        if sink is not None:
            @pl.when((s_id == 0) & (i == 0))
            def _():
                dsink_ref[...] = jnp.zeros_like(dsink_ref)

        @pl.when(i < n)
        def _():
            dk_win[...] = jnp.zeros_like(dk_win)
            dv_win[...] = jnp.zeros_like(dv_win)
            kfull, vfull = [], []
            for a in range(hkv // 2):
                ls = slice(a * PAIR, (a + 1) * PAIR)
                kfull.append(jnp.concatenate([kp_ref[:, ls], kc_ref[:, ls], kn_ref[:, ls]], axis=0) * 0.125)
                vfull.append(jnp.concatenate([vp_ref[:, ls], vc_ref[:, ls], vn_ref[:, ls]], axis=0))
            for jj in range(sub):
                rows = slice(jj * tq, (jj + 1) * tq)
                krows = slice(jj * tq, jj * tq + tk)
                mask_t = _band_mask_t(i * t + jj * tq, tq, w, seq_len)
                dq_t = []
                dk2 = [None] * (hkv // 2)
                dv2 = [None] * (hkv // 2)
                for qp in range(hq // 2):
                    kst, vst, a = _pair_kv(kfull, vfull, qp, rep, krows)
                    q2 = q_ref[rows, qp * PAIR:(qp + 1) * PAIR]
                    do2 = do_ref[rows, qp * PAIR:(qp + 1) * PAIR]
                    s2 = _dot_nt(kst, q2)
                    dp2 = _dot_nt(vst, do2)
                    ds, ps, q_at, do_at = [], [], [], []
                    for pos in range(2):
                        h = 2 * qp + pos
                        e = (h // rep) % 2
                        half = slice(pos * tk, (pos + 1) * tk)
                        lse_h = lse_ref[h:h + 1, rows]
                        dl_h = dl_ref[h:h + 1, rows]
                        p_t = jnp.exp(jnp.where(mask_t, s2[half], NEG_INF) - lse_h)
                        ds.append((p_t * (dp2[half] - dl_h)).astype(BF16))
                        ps.append(p_t.astype(BF16))
                        q_at.append(_place_head(q2, pos, e) * 0.125)
                        do_at.append(_place_head(do2, pos, e))
                        if sink is not None:
                            ds_sink = -jnp.sum(jnp.exp(sink_ref[0, h] - lse_h) * dl_h)
                            dsink_ref[h:h + 1, :] += jnp.full((1, LANES), ds_sink, F32)
                    dq_t.append(_rope_rows(_dot_tn(kst, jnp.concatenate(ds, axis=0)),
                                           cs_c[0:ROT_DIM // 2, rows], cs_c[ROT_DIM // 2:ROT_DIM, rows], -1.0))
                    dk_part = _dot(jnp.concatenate(ds, axis=1), jnp.concatenate(q_at, axis=0))
                    dv_part = _dot(jnp.concatenate(ps, axis=1), jnp.concatenate(do_at, axis=0))
                    dk2[a] = dk_part if dk2[a] is None else dk2[a] + dk_part
                    dv2[a] = dv_part if dv2[a] is None else dv2[a] + dv_part
                for a in range(hkv // 2):
                    ls = slice(a * PAIR, (a + 1) * PAIR)
                    dk_win[krows, ls] += dk2[a]
                    dv_win[krows, ls] += dv2[a]
                dq_ref[rows, :] = jnp.concatenate(dq_t, axis=0).T.astype(BF16)

            @pl.when(i > 0)
            def _():
                dk_acc[slot_p, t - w:, :] += dk_win[:w, :]
                dv_acc[slot_p, t - w:, :] += dv_win[:w, :]

            @pl.when(i == 0)
            def _():
                dk_acc[slot_c] = dk_win[w:w + t, :]
                dv_acc[slot_c] = dv_win[w:w + t, :]

            @pl.when(i > 0)
            def _():
                dk_acc[slot_c] += dk_win[w:w + t, :]
                dv_acc[slot_c] += dv_win[w:w + t, :]

            dk_acc[slot_n] = jnp.zeros((t, kw), F32)
            dv_acc[slot_n] = jnp.zeros((t, kw), F32)
            dk_acc[slot_n, :w, :] = dk_win[w + t:, :]
            dv_acc[slot_n, :w, :] = dv_win[w + t:, :]

        @pl.when(i >= 1)
        def _():
            dk_ref[...] = _rope(dk_acc[slot_p], *_rope_tabs(cs_p[...], e_ref[...]), -1.0).astype(BF16)
            dv_ref[...] = dv_acc[slot_p].astype(BF16)

    row_c = lambda width: pl.BlockSpec((None, t, width), lambda s, i: (s, cur(s, i), 0))
    row_p = lambda width: pl.BlockSpec((None, t, width), lambda s, i: (s, jnp.maximum(i - 1, 0), 0))
    stat = pl.BlockSpec((None, hq, t), lambda s, i: (s, 0, cur(s, i)))
    cs_rows = pl.BlockSpec((None, ROT_DIM, t), lambda s, i: (s, 0, cur(s, i)))
    in_specs = ([pl.BlockSpec((None, t, qw), lambda s, i: (s, cur(s, i), qcol))] + kv_specs
                + [row_c(qw), stat, stat, cs_rows, row_p(ROT_DIM),
                   pl.BlockSpec((ROT_DIM, 3 * LANES), lambda s, i: (0, 0))])
    args = [qkv] * 7 + [do, lse, delta, cs.transpose(0, 2, 1), cs, e_mat]
    out_specs = [row_c(qw), row_p(kw), row_p(kw)]
    out_shape = [jax.ShapeDtypeStruct((nseq, seq_len, qw), BF16),
                 jax.ShapeDtypeStruct((nseq, seq_len, kw), BF16),
                 jax.ShapeDtypeStruct((nseq, seq_len, kw), BF16)]
    if sink is not None:
        in_specs = [pl.BlockSpec(memory_space=pltpu.SMEM)] + in_specs
        args = [sink] + args
        out_specs.append(pl.BlockSpec((8, LANES), lambda s, i: (0, 0)))
        out_shape.append(jax.ShapeDtypeStruct((8, LANES), F32))
    return _pcall(
        body, name=name, grid=(nseq, n + 1), in_specs=in_specs, out_specs=out_specs, out_shape=out_shape,
        scratch_shapes=[pltpu.VMEM((3, t, kw), F32), pltpu.VMEM((3, t, kw), F32),
                        pltpu.VMEM((t + 2 * w, kw), F32), pltpu.VMEM((t + 2 * w, kw), F32)], args=args,
        dims=("arbitrary", "arbitrary"), comm=comm)


def _from_subsequences(ref, scr, dil, t):
    slabs = ref.shape[-1] // LANES
    if dil == 1:
        return ref[0].astype(F32)
    for c in range(dil):
        for sl in range(slabs):
            scr[sl, pl.ds(c, t // dil, stride=dil), :] = ref[c, :, sl * LANES:(sl + 1) * LANES].astype(F32)
    return jnp.concatenate([scr[sl] for sl in range(slabs)], axis=1)


def _to_subsequences(val, ref, scr, dil, t):
    slabs = val.shape[-1] // LANES
    if dil == 1:
        ref[0] = val.astype(ref.dtype)
        return
    for sl in range(slabs):
        scr[sl] = val[:, sl * LANES:(sl + 1) * LANES]
    for c in range(dil):
        for sl in range(slabs):
            ref[c, :, sl * LANES:(sl + 1) * LANES] = scr[sl, pl.ds(c, t // dil, stride=dil), :].astype(ref.dtype)


def _combine_fwd(out_a, o_g, lse_g, g_win, g_dil, *, t):
    s = out_a.shape[1]
    wd = DIL_SLOTS * HEAD_DIM

    def body(oa_ref, o0, o1, o2, l0, l1, l2, gw_ref, gd_ref, mixed_ref, ob_ref, lt_ref, scr):
        ls = [l0[...], l1[...], l2[...]]
        mx = jnp.maximum(jnp.maximum(ls[0], ls[1]), ls[2])
        ws = [jnp.exp(l - mx) for l in ls]
        tot = ws[0] + ws[1] + ws[2]
        lt_ref[...] = mx + jnp.log(tot)
        ws = [x / tot for x in ws]
        og = [_from_subsequences(o_ref, scr.at[gi], dil, t)
              for gi, (o_ref, dil) in enumerate(zip((o0, o1, o2), DILATIONS))]
        parts = []
        for h in range(DIL_SLOTS):
            hs = slice(h * HEAD_DIM, (h + 1) * HEAD_DIM)
            parts.append(ws[0][:, h:h + 1] * og[0][:, hs] + ws[1][:, h:h + 1] * og[1][:, hs]
                         + ws[2][:, h:h + 1] * og[2][:, hs])
        ob = jnp.concatenate(parts, axis=1)
        ob_ref[...] = ob
        na, _ = _rms_parts(oa_ref[...], gw_ref[...])
        nb, _ = _rms_parts(ob, gd_ref[...])
        mixed_ref[:, :wd] = na.astype(BF16)
        mixed_ref[:, wd:] = nb.astype(BF16)

    half = pl.BlockSpec((t, wd), lambda i: (i, 0))
    lanes = pl.BlockSpec((t, LANES), lambda i: (i, 0))
    grow = pl.BlockSpec((1, wd), lambda i: (0, 0))
    subseq = [pl.BlockSpec((dil, t // dil, wd), lambda i: (0, i, 0)) for dil in DILATIONS]
    return pl.pallas_call(
        body, name="combine_fwd", grid=(s // t,),
        in_specs=[pl.BlockSpec((None, t, wd), lambda i: (0, i, 0))] + subseq + [lanes, lanes, lanes, grow, grow],
        out_specs=[pl.BlockSpec((t, 2 * wd), lambda i: (i, 0)), half, lanes],
        out_shape=[jax.ShapeDtypeStruct((s, 2 * wd), BF16), jax.ShapeDtypeStruct((s, wd), F32),
                   jax.ShapeDtypeStruct((s, LANES), F32)],
        scratch_shapes=[pltpu.VMEM((len(DILATIONS), wd // LANES, t, LANES), F32)],
        compiler_params=_cparams(dimension_semantics=("parallel",)),
    )(out_a, *o_g, *lse_g, g_win, g_dil)


def _combine_bwd(dmixed, out_a, out_b, g_win, g_dil, *, t):
    s = out_b.shape[0]
    wd = DIL_SLOTS * HEAD_DIM

    def body(dm_ref, oa_ref, ob_ref, gw_ref, gd_ref, doa_ref, dob0, dob1, dob2, dla_ref, dlb_ref, st_ref, scr):
        i = pl.program_id(0)

        @pl.when(i == 0)
        def _():
            st_ref[...] = jnp.zeros_like(st_ref)

        lane = lax.broadcasted_iota(jnp.int32, (t, LANES), 1)
        for idx, (o_ref, g_ref, dl_ref) in enumerate(((oa_ref, gw_ref, dla_ref), (ob_ref, gd_ref, dlb_ref))):
            o = o_ref[...]
            dn = dm_ref[:, idx * wd:(idx + 1) * wd]
            _, rinv = _rms_parts(o, g_ref[...])
            wv = dn * g_ref[...]
            do = rinv * wv - o * (rinv * rinv * rinv) * jnp.mean(wv * o, axis=-1, keepdims=True)
            st_ref[idx:idx + 1, :] += jnp.sum(dn * o * rinv, axis=0, keepdims=True)
            if idx == 0:
                doa_ref[...] = do.astype(BF16)
            else:
                for do_ref, dil in zip((dob0, dob1, dob2), DILATIONS):
                    _to_subsequences(do, do_ref, scr, dil, t)
            prod = do * o
            acc = jnp.zeros((t, LANES), F32)
            for h in range(DIL_SLOTS):
                hs = slice(h * HEAD_DIM, (h + 1) * HEAD_DIM)
                acc = jnp.where(lane == h, jnp.sum(prod[:, hs], axis=1, keepdims=True), acc)
            dl_ref[...] = acc

    half = pl.BlockSpec((t, wd), lambda i: (i, 0))
    lanes = pl.BlockSpec((t, LANES), lambda i: (i, 0))
    grow = pl.BlockSpec((1, wd), lambda i: (0, 0))
    a_spec = pl.BlockSpec((None, t, wd), lambda i: (0, i, 0))
    subseq = [pl.BlockSpec((dil, t // dil, wd), lambda i: (0, i, 0)) for dil in DILATIONS]
    doa, dob0, dob1, dob2, dla, dlb, st = pl.pallas_call(
        body, name="combine_bwd", grid=(s // t,),
        in_specs=[pl.BlockSpec((t, 2 * wd), lambda i: (i, 0)), a_spec, half, grow, grow],
        out_specs=[a_spec] + subseq + [lanes, lanes, pl.BlockSpec((8, wd), lambda i: (0, 0))],
        out_shape=[jax.ShapeDtypeStruct((1, s, wd), BF16)]
        + [jax.ShapeDtypeStruct((dil, s // dil, wd), BF16) for dil in DILATIONS]
        + [jax.ShapeDtypeStruct((s, LANES), F32), jax.ShapeDtypeStruct((s, LANES), F32),
           jax.ShapeDtypeStruct((8, wd), F32)],
        scratch_shapes=[pltpu.VMEM((wd // LANES, t, LANES), F32)],
        compiler_params=_cparams(dimension_semantics=("arbitrary",)),
    )(dmixed, out_a, out_b, g_win, g_dil)
    return doa, [dob0, dob1, dob2], dla, dlb, st


def _assemble_dz(dqa, dka, dva, dqs, dks, dvs, *, t):
    s = dqa.shape[1]
    wd = DIL_SLOTS * HEAD_DIM

    def body(*refs):
        a_refs, g_refs, o_ref, scr = refs[:3], refs[3:12], refs[12], refs[13]
        col = 0
        for r in a_refs:
            o_ref[:, col:col + r.shape[-1]] = r[...]
            col += r.shape[-1]
        for part in range(3):
            for gi, dil in enumerate(DILATIONS):
                val = _from_subsequences(g_refs[3 * part + gi], scr, dil, t)
                o_ref[:, col:col + wd] = val.astype(BF16)
                col += wd

    a_specs = [pl.BlockSpec((None, t, a.shape[-1]), lambda i: (0, i, 0)) for a in (dqa, dka, dva)]
    g_specs = [pl.BlockSpec((dil, t // dil, wd), lambda i: (0, i, 0)) for _ in range(3) for dil in DILATIONS]
    return pl.pallas_call(
        body, name="assemble_dz", grid=(s // t,), in_specs=a_specs + g_specs,
        out_specs=pl.BlockSpec((t, IN_WIDTH), lambda i: (i, 0)),
        out_shape=jax.ShapeDtypeStruct((s, IN_WIDTH), BF16),
        scratch_shapes=[pltpu.VMEM((wd // LANES, t, LANES), F32)],
        compiler_params=_cparams(dimension_semantics=("parallel",)),
    )(dqa, dka, dva, *dqs, *dks, *dvs)


def _mixproj_fwd(mixed_b, w_mix_b, x, ln_in_g, ln_in_b, ln1_g, ln1_b, *, t):
    s = x.shape[0]

    def body(m_ref, w_ref, x_ref, g0, b0, g1, b1, r1_ref, h1_ref):
        h0 = _ln(x_ref[...], g0[...], b0[...])
        r1 = ALPHA * h0 + _dot(m_ref[...], w_ref[...])
        r1_ref[...] = r1
        h1_ref[...] = _ln(r1, g1[...], b1[...]).astype(BF16)

    tile = pl.BlockSpec((t, D_MODEL), lambda i: (i, 0))
    row = pl.BlockSpec((1, D_MODEL), lambda i: (0, 0))
    return pl.pallas_call(
        body, name="mixproj_fwd", grid=(s // t,),
        in_specs=[tile, pl.BlockSpec((D_MODEL, D_MODEL), lambda i: (0, 0)), tile, row, row, row, row],
        out_specs=[tile, tile],
        out_shape=[jax.ShapeDtypeStruct((s, D_MODEL), F32), jax.ShapeDtypeStruct((s, D_MODEL), BF16)],
        compiler_params=_cparams(dimension_semantics=("parallel",)),
    )(mixed_b, w_mix_b, x, ln_in_g, ln_in_b, ln1_g, ln1_b)


def _mem_fwd(mem, g, b, wk_b, wv_b):
    ml = mem.shape[0]

    def body(mem_ref, g_ref, b_ref, wk_ref, wv_ref, mn_ref, kx_ref, vx_ref):
        mn = _ln(mem_ref[...], g_ref[...], b_ref[...]).astype(BF16)
        mn_ref[...] = mn
        kx_ref[...] = _dot(mn, wk_ref[...]).astype(BF16)
        vx_ref[...] = _dot(mn, wv_ref[...]).astype(BF16)

    sh = jax.ShapeDtypeStruct((ml, D_MODEL), BF16)
    return pl.pallas_call(body, name="mem_fwd", out_shape=[sh, sh, sh], compiler_params=_cparams())(
        mem, g, b, wk_b, wv_b)


def _mem_bwd(dkx, dvx, mem, g, b, wk_b, wv_b):
    def body(dk_ref, dv_ref, mem_ref, g_ref, b_ref, wk_ref, wv_ref, dwk_ref, dwv_ref, st_ref):
        mem_v = mem_ref[...]
        mn = _ln(mem_v, g_ref[...], b_ref[...]).astype(BF16)
        dkb = dk_ref[...].astype(BF16)
        dvb = dv_ref[...].astype(BF16)
        dwk_ref[...] = _dot_tn(mn, dkb)
        dwv_ref[...] = _dot_tn(mn, dvb)
        dmn = _dot_nt(dkb, wk_ref[...]) + _dot_nt(dvb, wv_ref[...])
        _, dg, db = _ln_bwd_math(dmn, mem_v, g_ref[...])
        st_ref[...] = jnp.zeros_like(st_ref)
        st_ref[0:1, :] = dg
        st_ref[1:2, :] = db

    sw = jax.ShapeDtypeStruct((D_MODEL, D_MODEL), F32)
    return pl.pallas_call(body, name="mem_bwd", out_shape=[sw, sw, jax.ShapeDtypeStruct((8, D_MODEL), F32)],
                          compiler_params=_cparams())(dkx, dvx, mem, g, b, wk_b, wv_b)


def _xattn_fwd(h1b, r1, kx, vx, wq_b, wo_b, ln1_g, ln1_b, ln2_g, ln2_b, *, t):
    s = h1b.shape[0]
    scale = X_HEAD_DIM ** -0.5

    def body(h_ref, r1_ref, kx_ref, vx_ref, wq_ref, wo_ref, g1, b1, g2, b2, r2_ref, h2_ref, qx_ref, ox_ref, lse_ref):
        qxb = _dot(h_ref[...], wq_ref[...]).astype(BF16)
        qx_ref[...] = qxb
        lane = lax.broadcasted_iota(jnp.int32, (t, LANES), 1)
        lse_acc = jnp.zeros((t, LANES), F32)
        parts = []
        for h in range(X_HEADS):
            hs = slice(h * X_HEAD_DIM, (h + 1) * X_HEAD_DIM)
            sc = _dot_nt(qxb[:, hs] * scale, kx_ref[:, hs])
            m = jnp.max(sc, axis=1, keepdims=True)
            p = jnp.exp(sc - m)
            den = jnp.sum(p, axis=1, keepdims=True)
            parts.append(_dot(p.astype(BF16), vx_ref[:, hs]) / den)
            lse_acc = jnp.where(lane == h, m + jnp.log(den), lse_acc)
        lse_ref[...] = lse_acc
        oxb = jnp.concatenate(parts, axis=1).astype(BF16)
        ox_ref[...] = oxb
        h1 = _ln(r1_ref[...], g1[...], b1[...])
        r2 = ALPHA * h1 + _dot(oxb, wo_ref[...])
        r2_ref[...] = r2
        h2_ref[...] = _ln(r2, g2[...], b2[...]).astype(BF16)

    tile = pl.BlockSpec((t, D_MODEL), lambda i: (i, 0))
    row = pl.BlockSpec((1, D_MODEL), lambda i: (0, 0))
    full = lambda r: pl.BlockSpec((r, D_MODEL), lambda i: (0, 0))
    ml = kx.shape[0]
    bsh = jax.ShapeDtypeStruct((s, D_MODEL), BF16)
    return pl.pallas_call(
        body, name="xattn_fwd", grid=(s // t,),
        in_specs=[tile, tile, full(ml), full(ml), full(D_MODEL), full(D_MODEL), row, row, row, row],
        out_specs=[tile, tile, tile, tile, pl.BlockSpec((t, LANES), lambda i: (i, 0))],
        out_shape=[jax.ShapeDtypeStruct((s, D_MODEL), F32), bsh, bsh, bsh, jax.ShapeDtypeStruct((s, LANES), F32)],
        compiler_params=_cparams(dimension_semantics=("parallel",)),
    )(h1b, r1, kx, vx, wq_b, wo_b, ln1_g, ln1_b, ln2_g, ln2_b)


def _xattn_bwd(dr2, qxb, oxb, lse, kx, vx, wq_b, wo_b, r1, ln1_g, *, t, comm=None):
    s = dr2.shape[0]
    ml = kx.shape[0]
    scale = X_HEAD_DIM ** -0.5

    def body(dr2_ref, qx_ref, ox_ref, lse_ref, kx_ref, vx_ref, wq_ref, wo_ref, r1_ref, g1_ref,
             dr1_ref, dr1b_ref, dqx_ref, dkx_ref, dvx_ref, st_ref):
        i = pl.program_id(0)

        @pl.when(i == 0)
        def _():
            dkx_ref[...] = jnp.zeros_like(dkx_ref)
            dvx_ref[...] = jnp.zeros_like(dvx_ref)
            st_ref[...] = jnp.zeros_like(st_ref)

        dr2v = dr2_ref[...]
        dox = _dot_nt(dr2v.astype(BF16), wo_ref[...])
        parts = []
        for h in range(X_HEADS):
            hs = slice(h * X_HEAD_DIM, (h + 1) * X_HEAD_DIM)
            doh = dox[:, hs]
            dohb = doh.astype(BF16)
            dl = jnp.sum(doh * ox_ref[:, hs].astype(F32), axis=1, keepdims=True)
            qh = qx_ref[:, hs] * scale
            p = jnp.exp(_dot_nt(qh, kx_ref[:, hs]) - lse_ref[:, h:h + 1])
            dp = _dot_nt(dohb, vx_ref[:, hs])
            dsb = (p * (dp - dl)).astype(BF16)
            parts.append(_dot(dsb, kx_ref[:, hs]) * scale)
            dkx_ref[:, hs] += _dot_tn(dsb, qh)
            dvx_ref[:, hs] += _dot_tn(p.astype(BF16), dohb)
        dqxb = jnp.concatenate(parts, axis=1).astype(BF16)
        dqx_ref[...] = dqxb
        dh1 = _dot_nt(dqxb, wq_ref[...]) + ALPHA * dr2v
        dr1, dg, db = _ln_bwd_math(dh1, r1_ref[...], g1_ref[...])
        dr1_ref[...] = dr1
        dr1b_ref[...] = dr1.astype(BF16)
        st_ref[0:1, :] += dg
        st_ref[1:2, :] += db

    tile = pl.BlockSpec((t, D_MODEL), lambda i: (i, 0))
    full = lambda r: pl.BlockSpec((r, D_MODEL), lambda i: (0, 0))
    bsh = jax.ShapeDtypeStruct((s, D_MODEL), BF16)
    return _pcall(
        body, name="xattn_bwd", grid=(s // t,),
        in_specs=[tile, tile, tile, pl.BlockSpec((t, LANES), lambda i: (i, 0)), full(ml), full(ml),
                  full(D_MODEL), full(D_MODEL), tile, full(1)],
        out_specs=[tile, tile, tile, full(ml), full(ml), full(8)],
        out_shape=[jax.ShapeDtypeStruct((s, D_MODEL), F32), bsh, bsh,
                   jax.ShapeDtypeStruct((ml, D_MODEL), F32), jax.ShapeDtypeStruct((ml, D_MODEL), F32),
                   jax.ShapeDtypeStruct((8, D_MODEL), F32)],
        args=[dr2, qxb, oxb, lse, kx, vx, wq_b, wo_b, r1, ln1_g], dims=("arbitrary",), comm=comm)


def _halo_specs(t, s, width):
    tb8 = t // 8
    return [pl.BlockSpec((t, width), lambda i: (i, 0)),
            pl.BlockSpec((8, width), lambda i: (jnp.maximum(i * tb8 - 1, 0), 0)),
            pl.BlockSpec((8, width), lambda i: (jnp.minimum((i + 1) * tb8, s // 8 - 1), 0))]


def _halo_rows(i, n, prev_ref, next_ref):
    prev_row = jnp.where(i > 0, prev_ref[7:8, :], 0.0)
    next_row = jnp.where(i < n - 1, next_ref[0:1, :], 0.0)
    return prev_row, next_row


def _gelu_parts(gc):
    cdf = 0.5 * (1.0 + lax.erf(gc * (2.0 ** -0.5)))
    pdf = jnp.exp(-0.5 * gc * gc) * (1.0 / math.sqrt(2.0 * math.pi))
    return gc * cdf, cdf + gc * pdf


def _conv_fwd(g, u, conv_w, conv_b, *, t):
    s = g.shape[0]
    n = s // t

    def body(g_ref, gp_ref, gn_ref, u_ref, cw_ref, cb_ref, o_ref):
        i = pl.program_id(0)
        gv = g_ref[...]
        prev_row, next_row = _halo_rows(i, n, gp_ref, gn_ref)
        gm1, gp1 = _shift_rows(gv, prev_row, next_row)
        gc = gm1 * cw_ref[0:1, :] + gv * cw_ref[1:2, :] + gp1 * cw_ref[2:3, :] + cb_ref[...]
        act, _ = _gelu_parts(gc)
        o_ref[...] = (act * u_ref[...]).astype(BF16)

    tile = pl.BlockSpec((t, D_FF), lambda i: (i, 0))
    return pl.pallas_call(
        body, name="conv_fwd", grid=(n,),
        in_specs=_halo_specs(t, s, D_FF) + [tile, pl.BlockSpec((3, D_FF), lambda i: (0, 0)),
                                            pl.BlockSpec((1, D_FF), lambda i: (0, 0))],
        out_specs=tile, out_shape=jax.ShapeDtypeStruct((s, D_FF), BF16),
        compiler_params=_cparams(dimension_semantics=("parallel",)),
    )(g, g, g, u, conv_w, conv_b)


def _down_ln3(tb, w_down_b, r2, target, ln2_g, ln2_b, ln3_g, ln3_b, *, t):
    s = r2.shape[0]

    def body(t_ref, w_ref, r2_ref, tg_ref, g2, b2, g3, b3, dr_ref, drb_ref, st_ref):
        i = pl.program_id(0)

        @pl.when(i == 0)
        def _():
            st_ref[...] = jnp.zeros_like(st_ref)

        h2 = _ln(r2_ref[...], g2[...], b2[...])
        r3 = ALPHA * h2 + _dot(t_ref[...], w_ref[...])
        y = _ln(r3, g3[...], b3[...])
        err = y - tg_ref[...]
        loss = 0.5 * jnp.sum(jnp.mean(err * err, axis=-1, keepdims=True))
        dy = err * (1.0 / D_MODEL)
        dr, dg, db = _ln_bwd_math(dy, r3, g3[...])
        dr_ref[...] = dr
        drb_ref[...] = dr.astype(BF16)
        st_ref[0:1, :] += dg
        st_ref[1:2, :] += db
        st_ref[2:3, :] += jnp.full((1, D_MODEL), loss, F32)

    tile = pl.BlockSpec((t, D_MODEL), lambda i: (i, 0))
    row = pl.BlockSpec((1, D_MODEL), lambda i: (0, 0))
    return pl.pallas_call(
        body, name="down_ln3", grid=(s // t,),
        in_specs=[pl.BlockSpec((t, D_FF), lambda i: (i, 0)), pl.BlockSpec((D_FF, D_MODEL), lambda i: (0, 0)),
                  tile, tile, row, row, row, row],
        out_specs=[tile, tile, pl.BlockSpec((8, D_MODEL), lambda i: (0, 0))],
        out_shape=[jax.ShapeDtypeStruct((s, D_MODEL), F32), jax.ShapeDtypeStruct((s, D_MODEL), BF16),
                   jax.ShapeDtypeStruct((8, D_MODEL), F32)],
        compiler_params=_cparams(dimension_semantics=("arbitrary",)),
    )(tb, w_down_b, r2, target, ln2_g, ln2_b, ln3_g, ln3_b)


def _ffn_out(g, u, conv_w, conv_b, w_down_b, r2, target, ln2_g, ln2_b, ln3_g, ln3_b, *, t):
    s = r2.shape[0]
    n = s // t

    def body(g_ref, gp_ref, gn_ref, u_ref, cw_ref, cb_ref, w_ref, r2_ref, tg_ref, g2, b2, g3, b3,
             t_ref, dr_ref, drb_ref, st_ref):
        i = pl.program_id(0)

        @pl.when(i == 0)
        def _():
            st_ref[...] = jnp.zeros_like(st_ref)

        gv = g_ref[...]
        prev_row, next_row = _halo_rows(i, n, gp_ref, gn_ref)
        gm1, gp1 = _shift_rows(gv, prev_row, next_row)
        gc = gm1 * cw_ref[0:1, :] + gv * cw_ref[1:2, :] + gp1 * cw_ref[2:3, :] + cb_ref[...]
        act, _ = _gelu_parts(gc)
        tb = (act * u_ref[...]).astype(BF16)
        t_ref[...] = tb
        h2 = _ln(r2_ref[...], g2[...], b2[...])
        r3 = ALPHA * h2 + _dot(tb, w_ref[...])
        y = _ln(r3, g3[...], b3[...])
        err = y - tg_ref[...]
        loss = 0.5 * jnp.sum(jnp.mean(err * err, axis=-1, keepdims=True))
        dr, dg, db = _ln_bwd_math(err * (1.0 / D_MODEL), r3, g3[...])
        dr_ref[...] = dr
        drb_ref[...] = dr.astype(BF16)
        st_ref[0:1, :] += dg
        st_ref[1:2, :] += db
        st_ref[2:3, :] += jnp.full((1, D_MODEL), loss, F32)

    wide = pl.BlockSpec((t, D_FF), lambda i: (i, 0))
    tile = pl.BlockSpec((t, D_MODEL), lambda i: (i, 0))
    row = pl.BlockSpec((1, D_MODEL), lambda i: (0, 0))
    return pl.pallas_call(
        body, name="ffn_out", grid=(n,),
        in_specs=_halo_specs(t, s, D_FF) + [wide, pl.BlockSpec((3, D_FF), lambda i: (0, 0)),
                                            pl.BlockSpec((1, D_FF), lambda i: (0, 0)),
                                            pl.BlockSpec((D_FF, D_MODEL), lambda i: (0, 0)),
                                            tile, tile, row, row, row, row],
        out_specs=[wide, tile, tile, pl.BlockSpec((8, D_MODEL), lambda i: (0, 0))],
        out_shape=[jax.ShapeDtypeStruct((s, D_FF), BF16), jax.ShapeDtypeStruct((s, D_MODEL), F32),
                   jax.ShapeDtypeStruct((s, D_MODEL), BF16), jax.ShapeDtypeStruct((8, D_MODEL), F32)],
        compiler_params=_cparams(dimension_semantics=("arbitrary",)),
    )(g, g, g, u, conv_w, conv_b, w_down_b, r2, target, ln2_g, ln2_b, ln3_g, ln3_b)


def _dh2_ln2(dgc, conv_w, du, w_gate_b, w_up_b, dr3, r2, ln2_g, *, t, comm=None):
    s = dgc.shape[0]
    n = s // t

    def body(d_ref, dp_ref, dn_ref, cw_ref, du_ref, wg_ref, wu_ref, dr3_ref, r2_ref, g2, dg_ref, dr_ref, drb_ref,
             st_ref):
        i = pl.program_id(0)

        @pl.when(i == 0)
        def _():
            st_ref[...] = jnp.zeros_like(st_ref)

        dv = d_ref[...]
        prev_row, next_row = _halo_rows(i, n, dp_ref, dn_ref)
        dm1, dp1 = _shift_rows(dv, prev_row, next_row)
        dgb = (dp1 * cw_ref[0:1, :] + dv * cw_ref[1:2, :] + dm1 * cw_ref[2:3, :]).astype(BF16)
        dg_ref[...] = dgb
        dh2 = _dot(dgb, wg_ref[...]) + _dot(du_ref[...], wu_ref[...]) + ALPHA * dr3_ref[...]
        dr, dg, db = _ln_bwd_math(dh2, r2_ref[...], g2[...])
        dr_ref[...] = dr
        drb_ref[...] = dr.astype(BF16)
        st_ref[0:1, :] += dg
        st_ref[1:2, :] += db

    wide = pl.BlockSpec((t, D_FF), lambda i: (i, 0))
    tile = pl.BlockSpec((t, D_MODEL), lambda i: (i, 0))
    wfull = pl.BlockSpec((D_FF, D_MODEL), lambda i: (0, 0))
    return _pcall(
        body, name="dh2_ln2", grid=(n,),
        in_specs=_halo_specs(t, s, D_FF) + [pl.BlockSpec((3, D_FF), lambda i: (0, 0)), wide, wfull, wfull,
                                            tile, tile, pl.BlockSpec((1, D_MODEL), lambda i: (0, 0))],
        out_specs=[wide, tile, tile, pl.BlockSpec((8, D_MODEL), lambda i: (0, 0))],
        out_shape=[jax.ShapeDtypeStruct((s, D_FF), BF16), jax.ShapeDtypeStruct((s, D_MODEL), F32),
                   jax.ShapeDtypeStruct((s, D_MODEL), BF16), jax.ShapeDtypeStruct((8, D_MODEL), F32)],
        args=[dgc, dgc, dgc, conv_w, du, w_gate_b, w_up_b, dr3, r2, ln2_g], dims=("arbitrary",), comm=comm)


def _conv_bwd_a(dr3b, w_down_b, g, u, conv_w, conv_b, *, t):
    s = g.shape[0]
    n = s // t

    def body(d_ref, w_ref, g_ref, gp_ref, gn_ref, u_ref, cw_ref, cb_ref, du_ref, dgc_ref, st_ref):
        i = pl.program_id(0)

        @pl.when(i == 0)
        def _():
            st_ref[...] = jnp.zeros_like(st_ref)

        dt = _dot_nt(d_ref[...], w_ref[...])
        gv = g_ref[...]
        prev_row, next_row = _halo_rows(i, n, gp_ref, gn_ref)
        gm1, gp1 = _shift_rows(gv, prev_row, next_row)
        gc = gm1 * cw_ref[0:1, :] + gv * cw_ref[1:2, :] + gp1 * cw_ref[2:3, :] + cb_ref[...]
        act, dact = _gelu_parts(gc)
        du_ref[...] = (dt * act).astype(BF16)
        dgc = dt * u_ref[...] * dact
        dgc_ref[...] = dgc
        st_ref[0:1, :] += jnp.sum(gm1 * dgc, axis=0, keepdims=True)
        st_ref[1:2, :] += jnp.sum(gv * dgc, axis=0, keepdims=True)
        st_ref[2:3, :] += jnp.sum(gp1 * dgc, axis=0, keepdims=True)
        st_ref[3:4, :] += jnp.sum(dgc, axis=0, keepdims=True)

    tile = pl.BlockSpec((t, D_FF), lambda i: (i, 0))
    return pl.pallas_call(
        body, name="conv_bwd_a", grid=(n,),
        in_specs=[pl.BlockSpec((t, D_MODEL), lambda i: (i, 0)), pl.BlockSpec((D_FF, D_MODEL), lambda i: (0, 0))]
        + _halo_specs(t, s, D_FF) + [tile, pl.BlockSpec((3, D_FF), lambda i: (0, 0)),
                                     pl.BlockSpec((1, D_FF), lambda i: (0, 0))],
        out_specs=[tile, tile, pl.BlockSpec((8, D_FF), lambda i: (0, 0))],
        out_shape=[jax.ShapeDtypeStruct((s, D_FF), BF16), jax.ShapeDtypeStruct((s, D_FF), F32),
                   jax.ShapeDtypeStruct((8, D_FF), F32)],
        compiler_params=_cparams(dimension_semantics=("arbitrary",)),
    )(dr3b, w_down_b, g, g, g, u, conv_w, conv_b)


def _conv_bwd_b(dgc, conv_w, *, t):
    s = dgc.shape[0]
    n = s // t

    def body(d_ref, dp_ref, dn_ref, cw_ref, o_ref):
        i = pl.program_id(0)
        dv = d_ref[...]
        prev_row, next_row = _halo_rows(i, n, dp_ref, dn_ref)
        dm1, dp1 = _shift_rows(dv, prev_row, next_row)
        o_ref[...] = (dp1 * cw_ref[0:1, :] + dv * cw_ref[1:2, :] + dm1 * cw_ref[2:3, :]).astype(BF16)

    return pl.pallas_call(
        body, name="conv_bwd_b", grid=(n,),
        in_specs=_halo_specs(t, s, D_FF) + [pl.BlockSpec((3, D_FF), lambda i: (0, 0))],
        out_specs=pl.BlockSpec((t, D_FF), lambda i: (i, 0)), out_shape=jax.ShapeDtypeStruct((s, D_FF), BF16),
        compiler_params=_cparams(dimension_semantics=("parallel",)),
    )(dgc, dgc, dgc, conv_w)


def _to_residue(a, dil):
    s, w = a.shape
    return a.reshape(s // dil, dil, w).transpose(1, 0, 2)


def _from_residue(a):
    dil, l, w = a.shape
    return a.transpose(1, 0, 2).reshape(dil * l, w)


def _stats_to_lanes(rows):
    dil, hq, l = rows.shape
    return jnp.pad(rows.transpose(2, 0, 1).reshape(dil * l, hq), ((0, 0), (0, LANES - hq)))


def _stats_to_rows(lanes, dil):
    s = lanes.shape[0]
    return lanes[:, :DIL_SLOTS].reshape(s // dil, dil, DIL_SLOTS).transpose(1, 2, 0)


def _rope_angles(positions):
    inv_freq = ROPE_THETA ** (-jnp.arange(0, ROT_DIM, 2, dtype=F32) / ROT_DIM)
    ang = positions.astype(F32)[:, None] * inv_freq
    return jnp.concatenate([jnp.cos(ang), jnp.sin(ang)], axis=1)


class _NoPlan:
    def gather(self, stage):
        return None

    def gathered(self, stage, couts, wb):
        pass

    def exchange(self, stage, grads):
        return None

    def exchanged(self, stage, couts):
        pass


def _local_step(x, mem, positions, target, wb, sp, plan=None, *, t_row=256, t_mm=512, tq_a=128, tq_b=128,
                sub_a=4, sub_b=4):
    s = x.shape[0]
    plan = plan or _NoPlan()
    cs = _rope_angles(positions)
    e_mat = _rope_select_matrix()

    (h0b, za, *zb), couts = _proj_all(x, sp["ln_in_g"], sp["ln_in_b"], wb["w_in"], cs, e_mat, t=t_mm,
                                      comm=plan.gather("proj"))
    plan.gathered("proj", couts, wb)
    sub_a = max(1, min(sub_a, s // tq_a))
    subs_b = [max(1, min(sub_b, s // dil // tq_b)) for dil in DILATIONS]
    out_a, lse_a, couts = _swa_fwd_p(za, qcol=0, kcol=4, vcol=5, hq=WIN_Q_HEADS, hkv=WIN_KV_HEADS, w=WIN_HALF,
                                     tq=tq_a, sub=sub_a, sink=sp["attn_sink"], name="attn_a_fwd",
                                     comm=plan.gather("attn_a"))
    plan.gathered("attn_a", couts, wb)
    o_g, lse_g = [], []
    for gi in range(3):
        o, l, couts = _swa_fwd_p(zb[gi], qcol=0, kcol=1, vcol=2, hq=DIL_SLOTS, hkv=DIL_SLOTS, w=DIL_HALF, tq=tq_b,
                                 sub=subs_b[gi], sink=None, name=f"attn_b{gi}_fwd",
                                 comm=plan.gather(f"attn_b{gi}"))
        plan.gathered(f"attn_b{gi}", couts, wb)
        o_g.append(o)
        lse_g.append(_stats_to_lanes(l))
    mixed_b, out_b, lse_b = _combine_fwd(out_a, o_g, lse_g, sp["g_win"], sp["g_dil"], t=t_row)
    r1, h1b = _mixproj_fwd(mixed_b, wb["w_mix_out"], x, sp["ln_in_g"], sp["ln_in_b"], sp["ln1_g"], sp["ln1_b"],
                           t=t_row)
    mem_nb, kx, vx = _mem_fwd(mem, sp["mem_ln_g"], sp["mem_ln_b"], wb["w_xk"], wb["w_xv"])
    r2, h2b, qxb, oxb, lse_x = _xattn_fwd(h1b, r1, kx, vx, wb["w_xq"], wb["w_xo"], sp["ln1_g"], sp["ln1_b"],
                                          sp["ln2_g"], sp["ln2_b"], t=t_row)
    g = _mm(h2b, wb["w_gate"], mode="nt", out_dtype=F32, tm=t_mm, tn=D_FF, name="ff_gate")
    u = _mm(h2b, wb["w_up"], mode="nt", out_dtype=F32, tm=t_mm, tn=D_FF, name="ff_up")
    tb, dr3, dr3b, st3 = _ffn_out(g, u, sp["conv_w"], sp["conv_b"], wb["w_down"], r2, target, sp["ln2_g"],
                                  sp["ln2_b"], sp["ln3_g"], sp["ln3_b"], t=t_row)

    grads = {}
    du, dgc, st_conv = _conv_bwd_a(dr3b, wb["w_down"], g, u, sp["conv_w"], sp["conv_b"], t=t_row)
    tk = min(1024, s)
    grads["w_down"] = _mm(tb, dr3b, mode="tn", out_dtype=BF16, tm=D_FF // 2, tn=D_MODEL, tk=tk, name="dw_down")
    grads["w_up"] = _mm(du, h2b, mode="tn", out_dtype=BF16, tm=D_FF // 2, tn=D_MODEL, tk=tk, name="dw_up")
    (dg, dr2, dr2b, st2), couts = _dh2_ln2(dgc, sp["conv_w"], du, wb["w_gate"], wb["w_up"], dr3, r2, sp["ln2_g"],
                                           t=t_row, comm=plan.exchange("dh2", grads))
    plan.exchanged("dh2", couts)
    grads["w_gate"] = _mm(dg, h2b, mode="tn", out_dtype=BF16, tm=D_FF // 2, tn=D_MODEL, tk=tk, name="dw_gate")

    (dr1, dr1b, dqxb, dkx, dvx, st1), couts = _xattn_bwd(
        dr2, qxb, oxb, lse_x, kx, vx, wb["w_xq"], wb["w_xo"], r1, sp["ln1_g"], t=t_row,
        comm=plan.exchange("xattn", grads))
    plan.exchanged("xattn", couts)
    grads["w_xo"] = _mm(oxb, dr2b, mode="tn", out_dtype=BF16, tm=D_MODEL, tn=D_MODEL, tk=tk, name="dw_xo")
    grads["w_xq"] = _mm(h1b, dqxb, mode="tn", out_dtype=BF16, tm=D_MODEL, tn=D_MODEL, tk=tk, name="dw_xq")
    grads["w_xk"], grads["w_xv"], st_mem = _mem_bwd(dkx, dvx, mem, sp["mem_ln_g"], sp["mem_ln_b"],
                                                    wb["w_xk"], wb["w_xv"])

    grads["w_mix_out"] = _mm(mixed_b, dr1b, mode="tn", out_dtype=BF16, tm=D_MODEL, tn=D_MODEL, tk=tk,
                             name="dw_mix")
    dmixed = _mm(dr1b, wb["w_mix_out"], mode="nt", out_dtype=F32, tm=t_mm, tn=D_MODEL, name="dmixed")
    do_a, do_b, dl_a, dl_b, st_mix = _combine_bwd(dmixed, out_a, out_b, sp["g_win"], sp["g_dil"], t=t_row)
    (dqa, dka, dva, dsink), couts = _swa_bwd_p(
        za, do_a, lse_a, _stats_to_rows(dl_a, 1), cs[None], e_mat, qcol=0, kcol=4, vcol=5, hq=WIN_Q_HEADS,
        hkv=WIN_KV_HEADS, w=WIN_HALF, tq=2 * tq_a, sub=max(1, sub_a // 2), sink=sp["attn_sink"], name="attn_a_bwd",
        comm=plan.exchange("attn_a", grads))
    plan.exchanged("attn_a", couts)
    dqs, dks, dvs = [], [], []
    for gi, dil in enumerate(DILATIONS):
        (dq, dk, dv), _ = _swa_bwd_p(
            zb[gi], do_b[gi], _stats_to_rows(lse_b, dil), _stats_to_rows(dl_b, dil),
            _to_residue(cs, dil), e_mat, qcol=0, kcol=1, vcol=2, hq=DIL_SLOTS, hkv=DIL_SLOTS, w=DIL_HALF, tq=tq_b,
            sub=subs_b[gi], sink=None, name=f"attn_b{gi}_bwd")
        dqs.append(dq)
        dks.append(dk)
        dvs.append(dv)
    dz = _assemble_dz(dqa, dka, dva, dqs, dks, dvs, t=t_row)
    grads["w_in"] = _mm(dz, h0b, mode="tn", out_dtype=BF16, tm=IN_WIDTH // 7, tn=D_MODEL, tk=tk, name="dw_in")
    comm = plan.exchange("dh0", grads)
    dh0 = _mm(dz, wb["w_in"], mode="nn", out_dtype=F32, tm=t_mm, tn=D_MODEL, add=dr1, add_scale=ALPHA, name="dh0",
              comm=comm)
    if comm is not None:
        dh0, couts = dh0
        plan.exchanged("dh0", couts)
    grad_x, st0 = _ln_bwd(dh0, x, sp["ln_in_g"], t=t_row, name="ln_in_bwd", want_bf16=False)

    small = {
        "loss": st3[2:3, 0:1],
        "ln_in_g": st0[0:1], "ln_in_b": st0[1:2],
        "attn_sink": dsink[:, 0].reshape(1, WIN_Q_HEADS),
        "g_win": st_mix[0:1], "g_dil": st_mix[1:2],
        "ln1_g": st1[0:1], "ln1_b": st1[1:2],
        "mem_ln_g": st_mem[0:1], "mem_ln_b": st_mem[1:2],
        "ln2_g": st2[0:1], "ln2_b": st2[1:2],
        "conv_w": st_conv[0:3], "conv_b": st_conv[3:4],
        "ln3_g": st3[0:1], "ln3_b": st3[1:2],
    }
    return grad_x, grads, small


def _swap_sibling(arrays):
    n = len(arrays)

    def body(*refs):
        src, dst = refs[:n], refs[n:2 * n]
        send_sems, recv_sems = refs[2 * n:]
        x, y, c, _ = _place()
        copies = [pltpu.make_async_remote_copy(
            src_ref=src[a], dst_ref=dst[a], send_sem=send_sems.at[a], recv_sem=recv_sems.at[a],
            device_id=(x, y, 1 - c), device_id_type=MESH_IDS) for a in range(n)]
        for cp in copies:
            cp.start()
        for cp in copies:
            cp.wait_recv()
        for cp in copies:
            cp.wait_send()

    return pl.pallas_call(
        body, name="swap_sibling", in_specs=[ANY] * n, out_specs=[ANY] * n,
        out_shape=[jax.ShapeDtypeStruct(a.shape, a.dtype) for a in arrays],
        scratch_shapes=[pltpu.SemaphoreType.DMA((n,)), pltpu.SemaphoreType.DMA((n,))],
    )(*arrays)


def _row_tile(rows, cols, itemsize=4, budget=1 << 20):
    best = None
    for t in range(16, rows + 1, 16):
        if rows % t == 0 and t * cols * itemsize <= budget:
            best = t
    return best or rows


def _sum_slots(stack, *, name):
    n, r, c = stack.shape
    t = _row_tile(r, c)

    def body(s_ref, o_ref):
        acc = s_ref[0].astype(F32)
        for q in range(1, n):
            acc = acc + s_ref[q].astype(F32)
        o_ref[...] = acc

    return pl.pallas_call(
        body, name=name, grid=(r // t,), in_specs=[pl.BlockSpec((n, t, c), lambda i: (0, i, 0))],
        out_specs=pl.BlockSpec((t, c), lambda i: (i, 0)), out_shape=jax.ShapeDtypeStruct((r, c), F32),
        compiler_params=_cparams(dimension_semantics=("parallel",)),
    )(stack)


def _adamw(w, m, v, p, q, *, name):
    r, c = w.shape
    t = _row_tile(r, c, budget=1 << 19)

    def body(*refs):
        if q is None:
            w_ref, m_ref, v_ref, p_ref, g_ref, d_ref, nm_ref, nv_ref = refs
            g = p_ref[...]
        else:
            w_ref, m_ref, v_ref, p_ref, q_ref, g_ref, d_ref, nm_ref, nv_ref = refs
            g = p_ref[...] + q_ref[...]
        nm = ADAM_B1 * m_ref[...] + (1.0 - ADAM_B1) * g
        nv = ADAM_B2 * v_ref[...] + (1.0 - ADAM_B2) * (g * g)
        m_hat = nm / (1.0 - ADAM_B1 ** ADAM_STEP)
        v_hat = nv / (1.0 - ADAM_B2 ** ADAM_STEP)
        g_ref[...] = g
        d_ref[...] = -ADAM_LR * (m_hat / (jnp.sqrt(v_hat) + ADAM_EPS) + ADAM_WD * w_ref[...])
        nm_ref[...] = nm
        nv_ref[...] = nv

    tile = pl.BlockSpec((t, c), lambda i: (i, 0))
    args = [w, m, v, p] + ([] if q is None else [q])
    sh = jax.ShapeDtypeStruct((r, c), F32)
    return pl.pallas_call(
        body, name=name, grid=(r // t,), in_specs=[tile] * len(args), out_specs=[tile] * 4, out_shape=[sh] * 4,
        compiler_params=_cparams(dimension_semantics=("parallel",)),
    )(*args)


BIG = ("w_in", "w_mix_out", "w_xq", "w_xk", "w_xv", "w_xo", "w_gate", "w_up", "w_down")
COL_SHARDED = ("w_in", "w_gate", "w_up")
WEIGHTS = ("ln_in_g", "ln_in_b", "w_in", "attn_sink", "g_win", "g_dil", "w_mix_out", "ln1_g", "ln1_b",
           "mem_ln_g", "mem_ln_b", "w_xq", "w_xk", "w_xv", "w_xo", "ln2_g", "ln2_b", "w_gate", "w_up",
           "conv_w", "conv_b", "w_down", "ln3_g", "ln3_b")
SMALL = tuple(k for k in WEIGHTS if k not in BIG)
PACK_COLS = 1024
CONV_SHARD = D_FF // N_CHIPS
CONV_WIDTH_ROWS = 3
SMALL_ROWS = 32


GATHER_STAGES = {"proj": ("w_mix_out", "w_xq", "w_xk", "w_xv", "w_xo"), "attn_a": ("w_gate", "w_up"),
                 "attn_b0": ("w_down",)}
EXCHANGE_STAGES = {"dh2": ("w_down", "w_up"), "xattn": ("w_gate",),
                   "attn_a": ("w_xo", "w_xq", "w_xk", "w_xv", "w_mix_out"), "dh0": ("w_in",)}


def _full_weight(k, g4):
    return g4.reshape(N_CHIPS * g4.shape[1], g4.shape[2])


def _grad_parts(k, gk):
    gk = gk.astype(BF16)
    return gk.reshape(N_CHIPS, gk.shape[0] // N_CHIPS, gk.shape[1])


class _Plan:
    def __init__(self, shards):
        self.shards = shards
        self.recv = {}

    def gather(self, stage):
        names = GATHER_STAGES.get(stage)
        return _ChipGather([self.shards[k] for k in names]) if names else None

    def gathered(self, stage, couts, wb):
        for k, g4 in zip(GATHER_STAGES.get(stage, ()), couts):
            wb[k] = _full_weight(k, g4)

    def exchange(self, stage, grads):
        names = EXCHANGE_STAGES.get(stage)
        return _ChipExchange([_grad_parts(k, grads[k]) for k in names]) if names else None

    def exchanged(self, stage, couts):
        for k, r4 in zip(EXCHANGE_STAGES.get(stage, ()), couts):
            self.recv[k] = r4


def _pack_rows(a):
    r, n = a.shape
    per = -(-n // PACK_COLS)
    return jnp.pad(a, ((0, 0), (0, per * PACK_COLS - n))).reshape(r * per, PACK_COLS)


def _unpack_rows(p, r, n):
    per = -(-n // PACK_COLS)
    return p.reshape(r, per * PACK_COLS)[:, :n]


def _pack(pieces, rows_total):
    cat = jnp.concatenate([_pack_rows(a) for a in pieces], axis=0)
    return jnp.pad(cat, ((0, rows_total - cat.shape[0]), (0, 0)))


def _unpack(p, shapes):
    out, at = [], 0
    for r, n in shapes:
        per = -(-n // PACK_COLS)
        out.append(_unpack_rows(p[at:at + r * per], r, n))
        at += r * per
    return out


def kernel(x, mem, positions, ln_in_g, ln_in_b, w_in, attn_sink, g_win, g_dil, w_mix_out, ln1_g, ln1_b, mem_ln_g, mem_ln_b, w_xq, w_xk, w_xv, w_xo, ln2_g, ln2_b, w_gate, w_up, conv_w, conv_b, w_down, ln3_g, ln3_b, loss_target, m_ln_in_g, m_ln_in_b, m_w_in, m_attn_sink, m_g_win, m_g_dil, m_w_mix_out, m_ln1_g, m_ln1_b, m_mem_ln_g, m_mem_ln_b, m_w_xq, m_w_xk, m_w_xv, m_w_xo, m_ln2_g, m_ln2_b, m_w_gate, m_w_up, m_conv_w, m_conv_b, m_w_down, m_ln3_g, m_ln3_b, v_ln_in_g, v_ln_in_b, v_w_in, v_attn_sink, v_g_win, v_g_dil, v_w_mix_out, v_ln1_g, v_ln1_b, v_mem_ln_g, v_mem_ln_b, v_w_xq, v_w_xk, v_w_xv, v_w_xo, v_ln2_g, v_ln2_b, v_w_gate, v_w_up, v_conv_w, v_conv_b, v_w_down, v_ln3_g, v_ln3_b):
    given = dict(locals())
    shape_of = {k: given[k].shape for k in WEIGHTS}
    as2d = lambda k, a: a.reshape(-1, a.shape[-1]).T if k in COL_SHARDED else a.reshape(-1, a.shape[-1])
    w2 = {k: as2d(k, given[k]) for k in WEIGHTS}
    m2 = {k: as2d(k, given["m_" + k]) for k in WEIGHTS}
    v2 = {k: as2d(k, given["v_" + k]) for k in WEIGHTS}
    chip = 2 * lax.axis_index("x") + lax.axis_index("y")

    plan = _Plan({k: w2[k].astype(BF16) for k in BIG})
    conv_pack = jnp.pad(w2["conv_w"], ((0, 16 - CONV_WIDTH_ROWS), (0, PACK_COLS - CONV_SHARD)))
    g_in, g_conv = _comm_only(_ChipGather([plan.shards["w_in"], conv_pack]), "gather_w_in")
    wb = {"w_in": _full_weight("w_in", g_in)}
    conv_full = g_conv[:, :CONV_WIDTH_ROWS, :CONV_SHARD].transpose(1, 0, 2).reshape(CONV_WIDTH_ROWS, D_FF)
    sp = {k: w2[k] for k in SMALL}
    sp["conv_w"] = conv_full

    grad_x, grads, small = _local_step(x[0], mem[0], positions[0], loss_target[0], wb, sp, plan)

    small_keys = ("loss",) + SMALL
    small_shapes = [small[k].shape for k in small_keys]
    small_pack = _pack([small[k] for k in small_keys], SMALL_ROWS)
    (small_all,) = _comm_only(_ChipExchange([], small_pack), "exchange_small")
    chip_sums = [_sum_slots(plan.recv[k], name=f"sum_chips_{k}") for k in BIG]
    sibling_sums = _swap_sibling(chip_sums)
    small_sum = _sum_slots(small_all, name="sum_small")
    small_g = dict(zip(small_keys, _unpack(small_sum, small_shapes)))
    loss = small_g["loss"][0, 0]

    res = {}
    for k, p, q in zip(BIG, chip_sums, sibling_sums):
        res[k] = _adamw(w2[k], m2[k], v2[k], p, q, name=f"adamw_{k}")
    small_g["conv_w"] = lax.dynamic_slice_in_dim(small_g["conv_w"], chip * CONV_SHARD, CONV_SHARD, axis=1)
    adam_shapes = [w2[k].shape for k in SMALL]
    packs = [_pack([d[k] for k in SMALL], SMALL_ROWS) for d in (w2, m2, v2, small_g)]
    small_res = [_unpack(o, adam_shapes) for o in _adamw(*packs, None, name="adamw_small")]
    for i, k in enumerate(SMALL):
        res[k] = tuple(o[i] for o in small_res)

    outs = [loss, grad_x[None]]
    for slot in range(4):
        outs += [(res[k][slot].T if k in COL_SHARDED else res[k][slot]).reshape(shape_of[k]) for k in WEIGHTS]
    return tuple(outs)
```

```python
import functools
import math

import jax
import jax.numpy as jnp
from jax import lax
from jax.experimental import pallas as pl
from jax.experimental.pallas import tpu as pltpu

F32 = jnp.float32
BF16 = jnp.bfloat16

D_MODEL = 1024
HEAD_DIM = 64
WIN_Q_HEADS = 8
WIN_KV_HEADS = 2
WIN_HALF = 128
DIL_SLOTS = 8
DILATIONS = (1, 4, 16)
DIL_HALF = 64
ROT_DIM = 16
ROPE_THETA = 500000.0
X_HEADS = 4
X_HEAD_DIM = 256
D_FF = 2816
A_Q = 512
A_KV = 128
A_WIDTH = A_Q + 2 * A_KV
B_QKV = 1536
IN_WIDTH = 5376
ALPHA = 2.0 ** 0.25
LN_EPS = 1e-5
NEG_INF = -1e30
LANES = 128
N_CHIPS = 4
N_DEV = 8

ADAM_LR = 0.001
ADAM_B1 = 0.9
ADAM_B2 = 0.999
ADAM_EPS = 1e-08
ADAM_WD = 0.01
ADAM_STEP = 10

VMEM_LIMIT = 56 * 1024 * 1024


def _cparams(**kw):
    return pltpu.CompilerParams(vmem_limit_bytes=VMEM_LIMIT, **kw)


def _dot(a, b):
    return lax.dot_general(a, b, (((1,), (0,)), ((), ())), preferred_element_type=F32)


def _dot_nt(a, b):
    return lax.dot_general(a, b, (((1,), (1,)), ((), ())), preferred_element_type=F32)


def _dot_tn(a, b):
    return lax.dot_general(a, b, (((0,), (0,)), ((), ())), preferred_element_type=F32)


def _ln(x, g, b):
    mu = jnp.mean(x, axis=-1, keepdims=True)
    xc = x - mu
    var = jnp.mean(xc * xc, axis=-1, keepdims=True)
    return xc * lax.rsqrt(var + LN_EPS) * g + b


def _ln_bwd_math(dy, r, g):
    mu = jnp.mean(r, axis=-1, keepdims=True)
    xc = r - mu
    var = jnp.mean(xc * xc, axis=-1, keepdims=True)
    rstd = lax.rsqrt(var + LN_EPS)
    xhat = xc * rstd
    dxhat = dy * g
    m1 = jnp.mean(dxhat, axis=-1, keepdims=True)
    m2 = jnp.mean(dxhat * xhat, axis=-1, keepdims=True)
    dr = rstd * (dxhat - m1 - xhat * m2)
    return dr, jnp.sum(dy * xhat, axis=0, keepdims=True), jnp.sum(dy, axis=0, keepdims=True)


def _rope(z, ta, tb, tc, sign):
    w = z.shape[1]
    reps = w // LANES
    a = jnp.tile(ta, (1, reps))
    b = jnp.tile(tb, (1, reps))
    c = jnp.tile(tc, (1, reps))
    return z * a + sign * (pltpu.roll(z, w - 8, 1) * b + pltpu.roll(z, 8, 1) * c)


def _shift_rows(x, prev_row, next_row):
    t = x.shape[0]
    row = lax.broadcasted_iota(jnp.int32, x.shape, 0)
    xm1 = jnp.where(row == 0, prev_row, pltpu.roll(x, 1, 0))
    xp1 = jnp.where(row == t - 1, next_row, pltpu.roll(x, t - 1, 0))
    return xm1, xp1


def _rope_tabs(cs, e_mat):
    hi = cs.astype(BF16)
    rest = cs - hi.astype(F32)
    mid = rest.astype(BF16)
    lo = (rest - mid.astype(F32)).astype(BF16)
    tabs = _dot(hi, e_mat) + _dot(mid, e_mat) + _dot(lo, e_mat)
    lane = lax.broadcasted_iota(jnp.int32, (cs.shape[0], LANES), 1)
    ones = jnp.where((lane & (HEAD_DIM - 1)) >= ROT_DIM, 1.0, 0.0)
    return tabs[:, :LANES] + ones, tabs[:, LANES:2 * LANES], tabs[:, 2 * LANES:]


def _rope_select_matrix():
    half = ROT_DIM // 2
    e = [[0.0] * (3 * LANES) for _ in range(ROT_DIM)]
    for lane in range(LANES):
        d = lane % HEAD_DIM
        if d < half:
            e[d][lane] = 1.0
            e[half + d][LANES + lane] = -1.0
        elif d < ROT_DIM:
            e[d - half][lane] = 1.0
            e[d][2 * LANES + lane] = 1.0
    return jnp.array(e, BF16)


def _rope_rows(x, cos_t, sin_t, sign):
    half = ROT_DIM // 2
    parts = []
    for base in (0, HEAD_DIM):
        r1, r2 = x[base:base + half], x[base + half:base + ROT_DIM]
        parts += [r1 * cos_t - sign * (r2 * sin_t), r2 * cos_t + sign * (r1 * sin_t), x[base + ROT_DIM:base + HEAD_DIM]]
    return jnp.concatenate(parts, axis=0)


MESH_IDS = pl.DeviceIdType.MESH
ANY = pl.BlockSpec(memory_space=pl.ANY)


def _place():
    x, y, c = lax.axis_index("x"), lax.axis_index("y"), lax.axis_index("c")
    other_chips = [(1 - x, y), (x, 1 - y), (1 - x, 1 - y)]
    return x, y, c, other_chips


class _ChipGather:
    def __init__(self, shards):
        self.inputs = list(shards)
        n = len(shards)
        self.out_shape = [jax.ShapeDtypeStruct((N_CHIPS,) + a.shape, a.dtype) for a in shards]
        self.scratch = [pltpu.SemaphoreType.DMA((6 * n,)), pltpu.SemaphoreType.DMA((6 * n,)),
                        pltpu.SemaphoreType.DMA((n,))]

    def _copies(self, src, dst, sems):
        send_sems, recv_sems, local_sems = sems
        x, y, c, chips = _place()
        mine = 2 * x + y
        n = len(src)
        local, sends, recvs, passes, pass_recvs = [], [], [], [], []
        for a in range(n):
            half = src[a].shape[0] // 2
            my_rows, other_rows = pl.ds(c * half, half), pl.ds((1 - c) * half, half)
            local.append(pltpu.make_async_copy(src[a], dst[a].at[mine], local_sems.at[a]))
            for j, (px, py) in enumerate(chips):
                k, k2, slot = 3 * a + j, 3 * n + 3 * a + j, 2 * px + py
                sends.append(pltpu.make_async_remote_copy(
                    src_ref=src[a].at[my_rows], dst_ref=dst[a].at[mine, my_rows], send_sem=send_sems.at[k],
                    recv_sem=recv_sems.at[k], device_id=(px, py, c), device_id_type=MESH_IDS))
                recvs.append(pltpu.make_async_remote_copy(
                    src_ref=src[a].at[my_rows], dst_ref=dst[a].at[slot, my_rows], send_sem=send_sems.at[k],
                    recv_sem=recv_sems.at[k], device_id=(px, py, c), device_id_type=MESH_IDS))
                passes.append(pltpu.make_async_remote_copy(
                    src_ref=dst[a].at[slot, my_rows], dst_ref=dst[a].at[slot, my_rows], send_sem=send_sems.at[k2],
                    recv_sem=recv_sems.at[k2], device_id=(x, y, 1 - c), device_id_type=MESH_IDS))
                pass_recvs.append(pltpu.make_async_remote_copy(
                    src_ref=dst[a].at[slot, my_rows], dst_ref=dst[a].at[slot, other_rows],
                    send_sem=send_sems.at[k2], recv_sem=recv_sems.at[k2], device_id=(x, y, 1 - c),
                    device_id_type=MESH_IDS))
        return local, sends, recvs, passes, pass_recvs

    def start(self, src, dst, sems):
        local, sends, _, _, _ = self._copies(src, dst, sems)
        for cp in local + sends:
            cp.start()

    def wait(self, src, dst, sems):
        local, sends, recvs, passes, pass_recvs = self._copies(src, dst, sems)
        for idx, landed in enumerate(recvs):
            landed.wait_recv()
            if passes:
                passes[idx].start()
        for cp in pass_recvs:
            cp.wait_recv()
        for cp in sends + passes:
            cp.wait_send()
        for cp in local:
            cp.wait()


class _ChipExchange:
    def __init__(self, parts, small=None):
        self.inputs = list(parts) + ([small] if small is not None else [])
        self.n = len(parts)
        self.has_small = small is not None
        self.out_shape = [jax.ShapeDtypeStruct(a.shape, a.dtype) for a in parts]
        n_sem, n_loc = 3 * self.n, self.n
        if self.has_small:
            self.out_shape.append(jax.ShapeDtypeStruct((N_DEV,) + small.shape, small.dtype))
            n_sem, n_loc = n_sem + N_DEV - 1, n_loc + 1
        self.scratch = [pltpu.SemaphoreType.DMA((n_sem,)), pltpu.SemaphoreType.DMA((n_sem,)),
                        pltpu.SemaphoreType.DMA((n_loc,))]

    def _copies(self, src, dst, sems):
        send_sems, recv_sems, local_sems = sems
        x, y, c, chips = _place()
        mine = 2 * x + y
        n = self.n
        local, sends, recvs = [], [], []
        for a in range(n):
            local.append(pltpu.make_async_copy(src[a].at[mine], dst[a].at[mine], local_sems.at[a]))
            for j, (px, py) in enumerate(chips):
                k = 3 * a + j
                sends.append(pltpu.make_async_remote_copy(
                    src_ref=src[a].at[2 * px + py], dst_ref=dst[a].at[mine], send_sem=send_sems.at[k],
                    recv_sem=recv_sems.at[k], device_id=(px, py, c), device_id_type=MESH_IDS))
                recvs.append(pltpu.make_async_remote_copy(
                    src_ref=src[a].at[mine], dst_ref=dst[a].at[2 * px + py], send_sem=send_sems.at[k],
                    recv_sem=recv_sems.at[k], device_id=(px, py, c), device_id_type=MESH_IDS))
        if self.has_small:
            me_dev = 4 * x + 2 * y + c
            local.append(pltpu.make_async_copy(src[n], dst[n].at[me_dev], local_sems.at[n]))
            for mask in range(1, N_DEV):
                px, py, pc = x ^ ((mask >> 2) & 1), y ^ ((mask >> 1) & 1), c ^ (mask & 1)
                k = 3 * n + mask - 1
                sends.append(pltpu.make_async_remote_copy(
                    src_ref=src[n], dst_ref=dst[n].at[me_dev], send_sem=send_sems.at[k], recv_sem=recv_sems.at[k],
                    device_id=(px, py, pc), device_id_type=MESH_IDS))
                recvs.append(pltpu.make_async_remote_copy(
                    src_ref=src[n], dst_ref=dst[n].at[4 * px + 2 * py + pc], send_sem=send_sems.at[k],
                    recv_sem=recv_sems.at[k], device_id=(px, py, pc), device_id_type=MESH_IDS))
        return local, sends, recvs, [], []

    start = _ChipGather.start
    wait = _ChipGather.wait


def _pcall(body, *, name, grid, in_specs, out_specs, out_shape, args, scratch_shapes=(), dims=None, comm=None):
    in_specs, out_specs, out_shape = list(in_specs), list(out_specs), list(out_shape)
    scratch_shapes = list(scratch_shapes)
    if comm is None:
        outs = pl.pallas_call(
            body, name=name, grid=grid, in_specs=in_specs, out_specs=out_specs, out_shape=out_shape,
            scratch_shapes=scratch_shapes, compiler_params=_cparams(dimension_semantics=dims),
        )(*args)
        return list(outs), []
    n_in, n_out, n_scr = len(in_specs), len(out_specs), len(scratch_shapes)
    n_cin, n_cout = len(comm.inputs), len(comm.out_shape)

    def wrapped(*refs):
        ins, refs = refs[:n_in], refs[n_in:]
        cins, refs = refs[:n_cin], refs[n_cin:]
        outs, refs = refs[:n_out], refs[n_out:]
        couts, refs = refs[:n_cout], refs[n_cout:]
        scr, csems = refs[:n_scr], refs[n_scr:]
        first = last = None
        for axis, size in enumerate(grid):
            pid = pl.program_id(axis)
            f, l = pid == 0, pid == size - 1
            first = f if first is None else first & f
            last = l if last is None else last & l

        @pl.when(first)
        def _():
            comm.start(cins, couts, csems)

        body(*ins, *outs, *scr)

        @pl.when(last)
        def _():
            comm.wait(cins, couts, csems)

    res = pl.pallas_call(
        wrapped, name=name, grid=grid, in_specs=in_specs + [ANY] * n_cin, out_specs=out_specs + [ANY] * n_cout,
        out_shape=out_shape + list(comm.out_shape), scratch_shapes=scratch_shapes + list(comm.scratch),
        compiler_params=_cparams(dimension_semantics=("arbitrary",) * len(grid)),
    )(*args, *comm.inputs)
    return list(res[:n_out]), list(res[n_out:])


def _comm_only(comm, name):
    def body(*refs):
        n_cin, n_cout = len(comm.inputs), len(comm.out_shape)
        cins, couts, csems = refs[:n_cin], refs[n_cin:n_cin + n_cout], refs[n_cin + n_cout:]
        comm.start(cins, couts, csems)
        comm.wait(cins, couts, csems)

    return list(pl.pallas_call(
        body, name=name, in_specs=[ANY] * len(comm.inputs), out_specs=[ANY] * len(comm.out_shape),
        out_shape=list(comm.out_shape), scratch_shapes=list(comm.scratch),
    )(*comm.inputs))


def _mm(a, b, *, mode, out_dtype, tm, tn, tk=None, add=None, add_scale=1.0, name, comm=None):
    if mode in ("nn", "nt"):
        m, k = a.shape
        n = b.shape[1] if mode == "nn" else b.shape[0]
        assert m % tm == 0 and n % tn == 0
        dot = _dot if mode == "nn" else _dot_nt

        def body(*refs):
            if add is None:
                a_ref, b_ref, o_ref = refs
                o_ref[...] = dot(a_ref[...], b_ref[...]).astype(out_dtype)
            else:
                a_ref, b_ref, c_ref, o_ref = refs
                o_ref[...] = (dot(a_ref[...], b_ref[...]) + add_scale * c_ref[...]).astype(out_dtype)

        b_spec = (pl.BlockSpec((k, tn), lambda i, j: (0, j)) if mode == "nn"
                  else pl.BlockSpec((tn, k), lambda i, j: (j, 0)))
        in_specs = [pl.BlockSpec((tm, k), lambda i, j: (i, 0)), b_spec]
        args = [a, b]
        if add is not None:
            in_specs.append(pl.BlockSpec((tm, tn), lambda i, j: (i, j)))
            args.append(add)
        outs, couts = _pcall(
            body, name=name, grid=(m // tm, n // tn), in_specs=in_specs,
            out_specs=[pl.BlockSpec((tm, tn), lambda i, j: (i, j))],
            out_shape=[jax.ShapeDtypeStruct((m, n), out_dtype)], args=args, dims=("parallel", "parallel"),
            comm=comm)
        return outs[0] if comm is None else (outs[0], couts)
    assert mode == "tn" and add is None and comm is None
    kk, m = a.shape
    n = b.shape[1]
    assert m % tm == 0 and n % tn == 0 and kk % tk == 0
    nk = kk // tk

    def body(a_ref, b_ref, o_ref, acc_ref):
        kstep = pl.program_id(2)

        @pl.when(kstep == 0)
        def _():
            acc_ref[...] = jnp.zeros_like(acc_ref)

        acc_ref[...] += _dot_tn(a_ref[...], b_ref[...])

        @pl.when(kstep == nk - 1)
        def _():
            o_ref[...] = acc_ref[...].astype(out_dtype)

    return pl.pallas_call(
        body, name=name, grid=(m // tm, n // tn, nk),
        in_specs=[pl.BlockSpec((tk, tm), lambda i, j, s: (s, i)), pl.BlockSpec((tk, tn), lambda i, j, s: (s, j))],
        out_specs=pl.BlockSpec((tm, tn), lambda i, j, s: (i, j)),
        out_shape=jax.ShapeDtypeStruct((m, n), out_dtype),
        scratch_shapes=[pltpu.VMEM((tm, tn), F32)],
        compiler_params=_cparams(dimension_semantics=("parallel", "parallel", "arbitrary")),
    )(a, b)


def _mm2_nt(a1, b1, a2, b2, add, *, add_scale, tm, name, comm=None):
    m, k = a1.shape
    n = b1.shape[0]

    def body(a1_ref, b1_ref, a2_ref, b2_ref, c_ref, o_ref):
        o_ref[...] = (_dot_nt(a1_ref[...], b1_ref[...]) + _dot_nt(a2_ref[...], b2_ref[...])
                      + add_scale * c_ref[...])

    a_spec = pl.BlockSpec((tm, k), lambda i: (i, 0))
    b_spec = pl.BlockSpec((n, k), lambda i: (0, 0))
    o_spec = pl.BlockSpec((tm, n), lambda i: (i, 0))
    outs, couts = _pcall(body, name=name, grid=(m // tm,), in_specs=[a_spec, b_spec, a_spec, b_spec, o_spec],
                         out_specs=[o_spec], out_shape=[jax.ShapeDtypeStruct((m, n), F32)],
                         args=[a1, b1, a2, b2, add], dims=("parallel",), comm=comm)
    return outs[0], couts


def _ln_bwd(dy, r, g, *, t, name, want_bf16):
    s = r.shape[0]

    def body(dy_ref, r_ref, g_ref, *outs):
        i = pl.program_id(0)
        dr, dg, db = _ln_bwd_math(dy_ref[...], r_ref[...], g_ref[...])
        outs[0][...] = dr
        if want_bf16:
            outs[1][...] = dr.astype(BF16)
        st_ref = outs[-1]

        @pl.when(i == 0)
        def _():
            st_ref[...] = jnp.zeros_like(st_ref)

        st_ref[0:1, :] += dg
        st_ref[1:2, :] += db

    tile = pl.BlockSpec((t, D_MODEL), lambda i: (i, 0))
    out_specs = [tile] + ([tile] if want_bf16 else []) + [pl.BlockSpec((8, D_MODEL), lambda i: (0, 0))]
    out_shape = ([jax.ShapeDtypeStruct((s, D_MODEL), F32)]
                 + ([jax.ShapeDtypeStruct((s, D_MODEL), BF16)] if want_bf16 else [])
                 + [jax.ShapeDtypeStruct((8, D_MODEL), F32)])
    return pl.pallas_call(
        body, name=name, grid=(s // t,),
        in_specs=[tile, tile, pl.BlockSpec((1, D_MODEL), lambda i: (0, 0))],
        out_specs=out_specs, out_shape=out_shape,
        compiler_params=_cparams(dimension_semantics=("arbitrary",)),
    )(dy, r, g)


PROJ_COLS = 256


def _proj_segments():
    wd = DIL_SLOTS * HEAD_DIM
    segs = [(1, [(0, 1), (PROJ_COLS, 1), (2 * PROJ_COLS, 2)])]
    for gi, dil in enumerate(DILATIONS):
        blocks = []
        for part, kind in enumerate((1, 1, 0)):
            col = A_WIDTH + part * B_QKV + gi * wd
            blocks += [(col, kind), (col + PROJ_COLS, kind)]
        segs.append((dil, blocks))
    return segs


PROJ_SEGMENTS = _proj_segments()


def _proj_all(x, g, b, w_t, cs, e_mat, *, t, comm=None):
    s = x.shape[0]
    cb = PROJ_COLS
    halves = cb // LANES

    def body(x_ref, g_ref, b_ref, w_ref, cs_ref, e_ref, h_ref, *rest):
        z_refs, scr = rest[:-1], rest[-1]
        h = _ln(x_ref[...], g_ref[...], b_ref[...]).astype(BF16)
        h_ref[...] = h
        ta, tb, tc = (jnp.tile(tab, (1, halves)) for tab in _rope_tabs(cs_ref[...], e_ref[...]))
        lane = lax.broadcasted_iota(jnp.int32, (t, cb), 1)
        slot = 0
        for z_ref, (dil, blocks) in zip(z_refs, PROJ_SEGMENTS):
            for jb, (col, kind) in enumerate(blocks):
                acc = _dot_nt(h, w_ref[col:col + cb, :])
                if kind:
                    z = acc * ta + (pltpu.roll(acc, cb - 8, 1) * tb + pltpu.roll(acc, 8, 1) * tc)
                    if kind == 2:
                        z = jnp.where(lane < LANES, z, acc)
                else:
                    z = acc
                if dil == 1:
                    z_ref[0, :, cb * jb:cb * (jb + 1)] = z.astype(BF16)
                    continue
                for half in range(halves):
                    scr[slot, half] = z[:, half * LANES:(half + 1) * LANES]
                for c in range(dil):
                    for half in range(halves):
                        rows = scr[slot, half, pl.ds(c, t // dil, stride=dil), :]
                        z_ref[c, :, cb * jb + half * LANES:cb * jb + (half + 1) * LANES] = rows.astype(BF16)
                slot = 1 - slot

    row = pl.BlockSpec((1, D_MODEL), lambda i: (0, 0))
    widths = [cb * len(blocks) for _, blocks in PROJ_SEGMENTS]
    dils = [dil for dil, _ in PROJ_SEGMENTS]
    outs, couts = _pcall(
        body, name="proj_all", grid=(s // t,),
        in_specs=[pl.BlockSpec((t, D_MODEL), lambda i: (i, 0)), row, row,
                  pl.BlockSpec((IN_WIDTH, D_MODEL), lambda i: (0, 0)),
                  pl.BlockSpec((t, ROT_DIM), lambda i: (i, 0)), pl.BlockSpec((ROT_DIM, 3 * LANES), lambda i: (0, 0))],
        out_specs=[pl.BlockSpec((t, D_MODEL), lambda i: (i, 0))]
        + [pl.BlockSpec((dil, t // dil, wd), lambda i: (0, i, 0)) for dil, wd in zip(dils, widths)],
        out_shape=[jax.ShapeDtypeStruct((s, D_MODEL), BF16)]
        + [jax.ShapeDtypeStruct((dil, s // dil, wd), BF16) for dil, wd in zip(dils, widths)],
        args=[x, g, b, w_t, cs, e_mat], scratch_shapes=[pltpu.VMEM((2, halves, t, LANES), F32)],
        dims=("parallel",), comm=comm)
    return outs, couts


def _window_mask(i, tq, w, seq_len):
    tk = tq + 2 * w
    qpos = i * tq + lax.broadcasted_iota(jnp.int32, (tq, tk), 0)
    kpos = i * tq - w + lax.broadcasted_iota(jnp.int32, (tq, tk), 1)
    return (jnp.abs(qpos - kpos) <= w) & (kpos >= 0) & (kpos < seq_len)


def _swa_specs(tq, hq, hkv, n, qcol, kcol, vcol):
    qw, kw = hq * HEAD_DIM, hkv * HEAD_DIM
    cur = lambda s, i: jnp.minimum(i, n - 1)
    prv = lambda s, i: jnp.maximum(jnp.minimum(i, n - 1) - 1, 0)
    nxt = lambda s, i: jnp.minimum(i + 1, n - 1)
    q_spec = pl.BlockSpec((None, tq, qw), lambda s, i: (s, cur(s, i), qcol))
    kv_specs = [pl.BlockSpec((None, tq, kw), (lambda s, i, f=f, c=c: (s, f(s, i), c)))
                for c in (kcol, vcol) for f in (prv, cur, nxt)]
    return q_spec, kv_specs, cur, prv


def _swa_fwd(qkv, *, qcol, kcol, vcol, hq, hkv, w, tq, sink, name, comm=None):
    nseq, seq_len, _ = qkv.shape
    n = seq_len // tq
    rep = hq // hkv
    q_spec, kv_specs, _, _ = _swa_specs(tq, hq, hkv, n, qcol, kcol, vcol)

    def body(*refs):
        if sink is not None:
            sink_ref, refs = refs[0], refs[1:]
        q_ref, kp_ref, kc_ref, kn_ref, vp_ref, vc_ref, vn_ref, o_ref, lse_ref = refs
        i = pl.program_id(1)
        mask = _window_mask(i, tq, w, seq_len)
        lane = lax.broadcasted_iota(jnp.int32, (tq, LANES), 1)
        lse_acc = jnp.zeros((tq, LANES), F32)
        for g in range(hkv):
            cs = slice(g * HEAD_DIM, (g + 1) * HEAD_DIM)
            kcat = jnp.concatenate([kp_ref[tq - w:, cs], kc_ref[:, cs], kn_ref[:w, cs]], axis=0)
            vcat = jnp.concatenate([vp_ref[tq - w:, cs], vc_ref[:, cs], vn_ref[:w, cs]], axis=0)
            for r in range(rep):
                h = g * rep + r
                hs = slice(h * HEAD_DIM, (h + 1) * HEAD_DIM)
                qh = q_ref[:, hs] * 0.125
                sc = jnp.where(mask, _dot_nt(qh, kcat), NEG_INF)
                m = jnp.max(sc, axis=1, keepdims=True)
                if sink is not None:
                    m = jnp.maximum(m, sink_ref[0, h])
                p = jnp.exp(sc - m)
                den = jnp.sum(p, axis=1, keepdims=True)
                if sink is not None:
                    den = den + jnp.exp(sink_ref[0, h] - m)
                o_ref[:, hs] = _dot(p.astype(BF16), vcat) / den
                lse_acc = jnp.where(lane == h, m + jnp.log(den), lse_acc)
        lse_ref[...] = lse_acc

    in_specs = [q_spec] + kv_specs
    args = [qkv] * 7
    if sink is not None:
        in_specs = [pl.BlockSpec(memory_space=pltpu.SMEM)] + in_specs
        args = [sink] + args
    (o, lse), couts = _pcall(
        body, name=name, grid=(nseq, n), in_specs=in_specs,
        out_specs=[pl.BlockSpec((None, tq, hq * HEAD_DIM), lambda s, i: (s, i, 0)),
                   pl.BlockSpec((None, tq, LANES), lambda s, i: (s, i, 0))],
        out_shape=[jax.ShapeDtypeStruct((nseq, seq_len, hq * HEAD_DIM), F32),
                   jax.ShapeDtypeStruct((nseq, seq_len, LANES), F32)],
        args=args, dims=("parallel", "parallel"), comm=comm)
    return o, lse, couts


def _swa_bwd(qkv, do, lse, delta, cs, e_mat, *, qcol, kcol, vcol, hq, hkv, w, tq, sink, name, comm=None):
    nseq, seq_len, _ = qkv.shape
    n = seq_len // tq
    rep = hq // hkv
    qw, kw = hq * HEAD_DIM, hkv * HEAD_DIM
    tk = tq + 2 * w
    q_spec, kv_specs, cur, prv = _swa_specs(tq, hq, hkv, n, qcol, kcol, vcol)

    def body(*refs):
        if sink is not None:
            sink_ref, refs = refs[0], refs[1:]
        (q_ref, kp_ref, kc_ref, kn_ref, vp_ref, vc_ref, vn_ref, do_ref, lse_ref, dl_ref,
         cs_c, cs_p, e_ref) = refs[:13]
        outs = refs[13:]
        if sink is not None:
            dq_ref, dk_ref, dv_ref, dsink_ref, dk_acc, dv_acc = outs
        else:
            dq_ref, dk_ref, dv_ref, dk_acc, dv_acc = outs
        s_id = pl.program_id(0)
        i = pl.program_id(1)
        slot_p, slot_c, slot_n = (i + 2) % 3, i % 3, (i + 1) % 3

        if sink is not None:
            @pl.when((s_id == 0) & (i == 0))
            def _():
                dsink_ref[...] = jnp.zeros_like(dsink_ref)

        @pl.when(i < n)
        def _():
            mask = _window_mask(i, tq, w, seq_len)
            dk_acc[slot_n] = jnp.zeros((tq, kw), F32)
            dv_acc[slot_n] = jnp.zeros((tq, kw), F32)

            @pl.when(i == 0)
            def _():
                dk_acc[slot_c] = jnp.zeros((tq, kw), F32)
                dv_acc[slot_c] = jnp.zeros((tq, kw), F32)

            dq_parts, dk_parts, dv_parts = [], [], []
            for g in range(hkv):
                cs = slice(g * HEAD_DIM, (g + 1) * HEAD_DIM)
                kcat = jnp.concatenate([kp_ref[tq - w:, cs], kc_ref[:, cs], kn_ref[:w, cs]], axis=0)
                vcat = jnp.concatenate([vp_ref[tq - w:, cs], vc_ref[:, cs], vn_ref[:w, cs]], axis=0)
                dkc = jnp.zeros((tk, HEAD_DIM), F32)
                dvc = jnp.zeros((tk, HEAD_DIM), F32)
                for r in range(rep):
                    h = g * rep + r
                    hs = slice(h * HEAD_DIM, (h + 1) * HEAD_DIM)
                    qh = q_ref[:, hs] * 0.125
                    sc = jnp.where(mask, _dot_nt(qh, kcat), NEG_INF)
                    lse_h = lse_ref[:, h:h + 1]
                    dl_h = dl_ref[:, h:h + 1]
                    p = jnp.exp(sc - lse_h)
                    doh = do_ref[:, hs]
                    dp = _dot_nt(doh, vcat)
                    dsb = (p * (dp - dl_h)).astype(BF16)
                    dq_parts.append(_dot(dsb, kcat) * 0.125)
                    dkc = dkc + _dot_tn(dsb, qh)
                    dvc = dvc + _dot_tn(p.astype(BF16), doh)
                    if sink is not None:
                        ds_sink = -jnp.sum(jnp.exp(sink_ref[0, h] - lse_h) * dl_h)
                        dsink_ref[h:h + 1, :] += jnp.full((1, LANES), ds_sink, F32)
                dk_parts.append(dkc)
                dv_parts.append(dvc)
            dq = jnp.concatenate(dq_parts, axis=1)
            dq_ref[...] = _rope(dq, *_rope_tabs(cs_c[...], e_ref[...]), -1.0).astype(BF16)
            dk_all = jnp.concatenate(dk_parts, axis=1)
            dv_all = jnp.concatenate(dv_parts, axis=1)

            @pl.when(i > 0)
            def _():
                dk_acc[slot_p, tq - w:, :] += dk_all[:w]
                dv_acc[slot_p, tq - w:, :] += dv_all[:w]

            dk_acc[slot_c] += dk_all[w:w + tq]
            dv_acc[slot_c] += dv_all[w:w + tq]
            dk_acc[slot_n, :w, :] += dk_all[w + tq:]
            dv_acc[slot_n, :w, :] += dv_all[w + tq:]

        @pl.when(i >= 1)
        def _():
            dk_ref[...] = _rope(dk_acc[slot_p], *_rope_tabs(cs_p[...], e_ref[...]), -1.0).astype(BF16)
            dv_ref[...] = dv_acc[slot_p].astype(BF16)

    row_c = lambda width: pl.BlockSpec((None, tq, width), lambda s, i: (s, cur(s, i), 0))
    row_p = lambda width: pl.BlockSpec((None, tq, width), lambda s, i: (s, jnp.maximum(i - 1, 0), 0))
    in_specs = ([q_spec] + kv_specs + [row_c(qw), row_c(LANES), row_c(LANES), row_c(ROT_DIM), row_p(ROT_DIM),
                                       pl.BlockSpec((ROT_DIM, 3 * LANES), lambda s, i: (0, 0))])
    args = [qkv] * 7 + [do, lse, delta, cs, cs, e_mat]
    out_specs = [row_c(qw), row_p(kw), row_p(kw)]
    out_shape = [jax.ShapeDtypeStruct((nseq, seq_len, qw), BF16),
                 jax.ShapeDtypeStruct((nseq, seq_len, kw), BF16),
                 jax.ShapeDtypeStruct((nseq, seq_len, kw), BF16)]
    if sink is not None:
        in_specs = [pl.BlockSpec(memory_space=pltpu.SMEM)] + in_specs
        args = [sink] + args
        out_specs.append(pl.BlockSpec((8, LANES), lambda s, i: (0, 0)))
        out_shape.append(jax.ShapeDtypeStruct((8, LANES), F32))
    return _pcall(
        body, name=name, grid=(nseq, n + 1), in_specs=in_specs, out_specs=out_specs, out_shape=out_shape,
        scratch_shapes=[pltpu.VMEM((3, tq, kw), F32), pltpu.VMEM((3, tq, kw), F32)], args=args,
        dims=("arbitrary", "arbitrary"), comm=comm)


PAIR = 2 * HEAD_DIM


def _window_mask_t(i, tq, w, seq_len):
    tk = tq + 2 * w
    kpos = i * tq - w + lax.broadcasted_iota(jnp.int32, (tk, tq), 0)
    qpos = i * tq + lax.broadcasted_iota(jnp.int32, (tk, tq), 1)
    return (jnp.abs(qpos - kpos) <= w) & (kpos >= 0) & (kpos < seq_len)


def _place_head(x2, src_pos, dst_pos):
    hi = lax.broadcasted_iota(jnp.int32, x2.shape, 1) >= HEAD_DIM
    src = x2 if src_pos == dst_pos else pltpu.roll(x2, HEAD_DIM, 1)
    return jnp.where(hi == (dst_pos == 1), src, jnp.zeros_like(src))


def _swa_fwd_t(qkv, *, qcol, kcol, vcol, hq, hkv, w, tq, sink, name, comm=None):
    nseq, seq_len, _ = qkv.shape
    n = seq_len // tq
    rep = hq // hkv
    q_spec, kv_specs, _, _ = _swa_specs(tq, hq, hkv, n, qcol, kcol, vcol)

    def body(*refs):
        if sink is not None:
            sink_ref, refs = refs[0], refs[1:]
        q_ref, kp_ref, kc_ref, kn_ref, vp_ref, vc_ref, vn_ref, o_ref, lse_ref = refs
        i = pl.program_id(1)
        mask_t = _window_mask_t(i, tq, w, seq_len)
        o_t = [None] * (hq // 2)
        lse_rows = [None] * hq
        for a in range(hkv // 2):
            ls = slice(a * PAIR, (a + 1) * PAIR)
            kcat = jnp.concatenate([kp_ref[tq - w:, ls], kc_ref[:, ls], kn_ref[:w, ls]], axis=0) * 0.125
            vcat = jnp.concatenate([vp_ref[tq - w:, ls], vc_ref[:, ls], vn_ref[:w, ls]], axis=0)
            for e in range(2):
                g = 2 * a + e
                placed = {}
                for r in range(rep):
                    h = g * rep + r
                    qp, pos = h // 2, h % 2
                    if pos not in placed:
                        placed[pos] = (_place_head(kcat, e, pos), _place_head(vcat, e, pos))
                    k_g, v_g = placed[pos]
                    s_t = jnp.where(mask_t, _dot_nt(k_g, q_ref[:, qp * PAIR:(qp + 1) * PAIR]), NEG_INF)
                    m = jnp.max(s_t, axis=0, keepdims=True)
                    if sink is not None:
                        m = jnp.maximum(m, sink_ref[0, h])
                    p_t = jnp.exp(s_t - m)
                    den = jnp.sum(p_t, axis=0, keepdims=True)
                    if sink is not None:
                        den = den + jnp.exp(sink_ref[0, h] - m)
                    part = _dot_tn(v_g, p_t.astype(BF16)) / den
                    o_t[qp] = part if o_t[qp] is None else o_t[qp] + part
                    lse_rows[h] = m + jnp.log(den)
        o_ref[...] = jnp.concatenate(o_t, axis=0).T
        lse_ref[...] = jnp.concatenate(lse_rows, axis=0)

    in_specs = [q_spec] + kv_specs
    args = [qkv] * 7
    if sink is not None:
        in_specs = [pl.BlockSpec(memory_space=pltpu.SMEM)] + in_specs
        args = [sink] + args
    (o, lse), couts = _pcall(
        body, name=name, grid=(nseq, n), in_specs=in_specs,
        out_specs=[pl.BlockSpec((None, tq, hq * HEAD_DIM), lambda s, i: (s, i, 0)),
                   pl.BlockSpec((None, hq, tq), lambda s, i: (s, 0, i))],
        out_shape=[jax.ShapeDtypeStruct((nseq, seq_len, hq * HEAD_DIM), F32),
                   jax.ShapeDtypeStruct((nseq, hq, seq_len), F32)],
        args=args, dims=("parallel", "parallel"), comm=comm)
    return o, lse, couts


def _swa_bwd_t(qkv, do, lse, delta, cs, e_mat, *, qcol, kcol, vcol, hq, hkv, w, tq, sink, name, comm=None):
    nseq, seq_len, _ = qkv.shape
    n = seq_len // tq
    rep = hq // hkv
    qw, kw = hq * HEAD_DIM, hkv * HEAD_DIM
    tk = tq + 2 * w
    q_spec, kv_specs, cur, prv = _swa_specs(tq, hq, hkv, n, qcol, kcol, vcol)

    def body(*refs):
        if sink is not None:
            sink_ref, refs = refs[0], refs[1:]
        (q_ref, kp_ref, kc_ref, kn_ref, vp_ref, vc_ref, vn_ref, do_ref, lse_ref, dl_ref,
         cs_c, cs_p, e_ref) = refs[:13]
        outs = refs[13:]
        if sink is not None:
            dq_ref, dk_ref, dv_ref, dsink_ref, dk_acc, dv_acc = outs
        else:
            dq_ref, dk_ref, dv_ref, dk_acc, dv_acc = outs
        s_id = pl.program_id(0)
        i = pl.program_id(1)
        slot_p, slot_c, slot_n = (i + 2) % 3, i % 3, (i + 1) % 3

        if sink is not None:
            @pl.when((s_id == 0) & (i == 0))
            def _():
                dsink_ref[...] = jnp.zeros_like(dsink_ref)

        @pl.when(i < n)
        def _():
            mask_t = _window_mask_t(i, tq, w, seq_len)
            dk_acc[slot_n] = jnp.zeros((tq, kw), F32)
            dv_acc[slot_n] = jnp.zeros((tq, kw), F32)

            @pl.when(i == 0)
            def _():
                dk_acc[slot_c] = jnp.zeros((tq, kw), F32)
                dv_acc[slot_c] = jnp.zeros((tq, kw), F32)

            dq_t = [None] * (hq // 2)
            dk_pairs, dv_pairs = [], []
            for a in range(hkv // 2):
                ls = slice(a * PAIR, (a + 1) * PAIR)
                kcat = jnp.concatenate([kp_ref[tq - w:, ls], kc_ref[:, ls], kn_ref[:w, ls]], axis=0) * 0.125
                vcat = jnp.concatenate([vp_ref[tq - w:, ls], vc_ref[:, ls], vn_ref[:w, ls]], axis=0)
                dk2 = jnp.zeros((tk, PAIR), F32)
                dv2 = jnp.zeros((tk, PAIR), F32)
                for e in range(2):
                    g = 2 * a + e
                    placed = {}
                    for r in range(rep):
                        h = g * rep + r
                        qp, pos = h // 2, h % 2
                        if pos not in placed:
                            placed[pos] = (_place_head(kcat, e, pos), _place_head(vcat, e, pos))
                        k_g, v_g = placed[pos]
                        q2 = q_ref[:, qp * PAIR:(qp + 1) * PAIR]
                        do2 = do_ref[:, qp * PAIR:(qp + 1) * PAIR]
                        lse_h = lse_ref[h:h + 1, :]
                        dl_h = dl_ref[h:h + 1, :]
                        p_t = jnp.exp(jnp.where(mask_t, _dot_nt(k_g, q2), NEG_INF) - lse_h)
                        dp_t = _dot_nt(v_g, do2)
                        dsb = (p_t * (dp_t - dl_h)).astype(BF16)
                        part = _dot_tn(k_g, dsb)
                        dq_t[qp] = part if dq_t[qp] is None else dq_t[qp] + part
                        dk2 = dk2 + _dot(dsb, _place_head(q2, pos, e) * 0.125)
                        dv2 = dv2 + _dot(p_t.astype(BF16), _place_head(do2, pos, e))
                        if sink is not None:
                            ds_sink = -jnp.sum(jnp.exp(sink_ref[0, h] - lse_h) * dl_h)
                            dsink_ref[h:h + 1, :] += jnp.full((1, LANES), ds_sink, F32)
                dk_pairs.append(dk2)
                dv_pairs.append(dv2)
            dq = jnp.concatenate(dq_t, axis=0).T
            dq_ref[...] = _rope(dq, *_rope_tabs(cs_c[...], e_ref[...]), -1.0).astype(BF16)
            dk_all = dk_pairs[0] if len(dk_pairs) == 1 else jnp.concatenate(dk_pairs, axis=1)
            dv_all = dv_pairs[0] if len(dv_pairs) == 1 else jnp.concatenate(dv_pairs, axis=1)

            @pl.when(i > 0)
            def _():
                dk_acc[slot_p, tq - w:, :] += dk_all[:w]
                dv_acc[slot_p, tq - w:, :] += dv_all[:w]

            dk_acc[slot_c] += dk_all[w:w + tq]
            dv_acc[slot_c] += dv_all[w:w + tq]
            dk_acc[slot_n, :w, :] += dk_all[w + tq:]
            dv_acc[slot_n, :w, :] += dv_all[w + tq:]

        @pl.when(i >= 1)
        def _():
            dk_ref[...] = _rope(dk_acc[slot_p], *_rope_tabs(cs_p[...], e_ref[...]), -1.0).astype(BF16)
            dv_ref[...] = dv_acc[slot_p].astype(BF16)

    row_c = lambda width: pl.BlockSpec((None, tq, width), lambda s, i: (s, cur(s, i), 0))
    row_p = lambda width: pl.BlockSpec((None, tq, width), lambda s, i: (s, jnp.maximum(i - 1, 0), 0))
    stat = pl.BlockSpec((None, hq, tq), lambda s, i: (s, 0, cur(s, i)))
    in_specs = ([q_spec] + kv_specs + [row_c(qw), stat, stat, row_c(ROT_DIM), row_p(ROT_DIM),
                                       pl.BlockSpec((ROT_DIM, 3 * LANES), lambda s, i: (0, 0))])
    args = [qkv] * 7 + [do, lse, delta, cs, cs, e_mat]
    out_specs = [row_c(qw), row_p(kw), row_p(kw)]
    out_shape = [jax.ShapeDtypeStruct((nseq, seq_len, qw), BF16),
                 jax.ShapeDtypeStruct((nseq, seq_len, kw), BF16),
                 jax.ShapeDtypeStruct((nseq, seq_len, kw), BF16)]
    if sink is not None:
        in_specs = [pl.BlockSpec(memory_space=pltpu.SMEM)] + in_specs
        args = [sink] + args
        out_specs.append(pl.BlockSpec((8, LANES), lambda s, i: (0, 0)))
        out_shape.append(jax.ShapeDtypeStruct((8, LANES), F32))
    return _pcall(
        body, name=name, grid=(nseq, n + 1), in_specs=in_specs, out_specs=out_specs, out_shape=out_shape,
        scratch_shapes=[pltpu.VMEM((3, tq, kw), F32), pltpu.VMEM((3, tq, kw), F32)], args=args,
        dims=("arbitrary", "arbitrary"), comm=comm)


def _band_mask_t(row0, tq, w, seq_len):
    tk = tq + 2 * w
    kk = lax.broadcasted_iota(jnp.int32, (tk, tq), 0)
    qq = lax.broadcasted_iota(jnp.int32, (tk, tq), 1)
    kpos = row0 - w + kk
    return (jnp.abs(qq + w - kk) <= w) & (kpos >= 0) & (kpos < seq_len)


def _halo_kv_specs(t, w, hkv, n, seq_len, kcol, vcol):
    kw = hkv * HEAD_DIM
    per, last = t // w, seq_len // w - 1
    cur = lambda s, i: jnp.minimum(i, n - 1)
    specs = []
    for c in (kcol, vcol):
        specs += [pl.BlockSpec((None, w, kw), lambda s, i, c=c: (s, jnp.maximum(cur(s, i) * per - 1, 0), c)),
                  pl.BlockSpec((None, t, kw), lambda s, i, c=c: (s, cur(s, i), c)),
                  pl.BlockSpec((None, w, kw), lambda s, i, c=c: (s, jnp.minimum((cur(s, i) + 1) * per, last), c))]
    return specs, cur


def _swa_fwd_s(qkv, *, qcol, kcol, vcol, hq, hkv, w, tq, sub, sink, name, comm=None):
    nseq, seq_len, _ = qkv.shape
    t = tq * sub
    n = seq_len // t
    rep = hq // hkv
    tk = tq + 2 * w
    kv_specs, cur = _halo_kv_specs(t, w, hkv, n, seq_len, kcol, vcol)

    def body(*refs):
        if sink is not None:
            sink_ref, refs = refs[0], refs[1:]
        q_ref, kp_ref, kc_ref, kn_ref, vp_ref, vc_ref, vn_ref, o_ref, lse_ref = refs
        i = pl.program_id(1)
        kfull, vfull = [], []
        for a in range(hkv // 2):
            ls = slice(a * PAIR, (a + 1) * PAIR)
            kfull.append(jnp.concatenate([kp_ref[:, ls], kc_ref[:, ls], kn_ref[:, ls]], axis=0) * 0.125)
            vfull.append(jnp.concatenate([vp_ref[:, ls], vc_ref[:, ls], vn_ref[:, ls]], axis=0))
        for jj in range(sub):
            rows = slice(jj * tq, (jj + 1) * tq)
            mask_t = _band_mask_t(i * t + jj * tq, tq, w, seq_len)
            o_t = [None] * (hq // 2)
            lse_rows = [None] * hq
            for a in range(hkv // 2):
                kcat = kfull[a][jj * tq:jj * tq + tk]
                vcat = vfull[a][jj * tq:jj * tq + tk]
                for e in range(2):
                    g = 2 * a + e
                    placed = {}
                    for r in range(rep):
                        h = g * rep + r
                        qp, pos = h // 2, h % 2
                        if pos not in placed:
                            placed[pos] = (_place_head(kcat, e, pos), _place_head(vcat, e, pos))
                        k_g, v_g = placed[pos]
                        s_t = jnp.where(mask_t, _dot_nt(k_g, q_ref[rows, qp * PAIR:(qp + 1) * PAIR]), NEG_INF)
                        m = jnp.max(s_t, axis=0, keepdims=True)
                        if sink is not None:
                            m = jnp.maximum(m, sink_ref[0, h])
                        p_t = jnp.exp(s_t - m)
                        den = jnp.sum(p_t, axis=0, keepdims=True)
                        if sink is not None:
                            den = den + jnp.exp(sink_ref[0, h] - m)
                        part = _dot_tn(v_g, p_t.astype(BF16)) / den
                        o_t[qp] = part if o_t[qp] is None else o_t[qp] + part
                        lse_rows[h] = m + jnp.log(den)
            o_ref[rows, :] = jnp.concatenate(o_t, axis=0).T
            lse_ref[:, rows] = jnp.concatenate(lse_rows, axis=0)

    in_specs = [pl.BlockSpec((None, t, hq * HEAD_DIM), lambda s, i: (s, i, qcol))] + kv_specs
    args = [qkv] * 7
    if sink is not None:
        in_specs = [pl.BlockSpec(memory_space=pltpu.SMEM)] + in_specs
        args = [sink] + args
    (o, lse), couts = _pcall(
        body, name=name, grid=(nseq, n), in_specs=in_specs,
        out_specs=[pl.BlockSpec((None, t, hq * HEAD_DIM), lambda s, i: (s, i, 0)),
                   pl.BlockSpec((None, hq, t), lambda s, i: (s, 0, i))],
        out_shape=[jax.ShapeDtypeStruct((nseq, seq_len, hq * HEAD_DIM), F32),
                   jax.ShapeDtypeStruct((nseq, hq, seq_len), F32)],
        args=args, dims=("parallel", "parallel"), comm=comm)
    return o, lse, couts


def _swa_bwd_s(qkv, do, lse, delta, cs, e_mat, *, qcol, kcol, vcol, hq, hkv, w, tq, sub, sink, name, comm=None):
    nseq, seq_len, _ = qkv.shape
    t = tq * sub
    n = seq_len // t
    rep = hq // hkv
    qw, kw = hq * HEAD_DIM, hkv * HEAD_DIM
    tk = tq + 2 * w
    kv_specs, cur = _halo_kv_specs(t, w, hkv, n, seq_len, kcol, vcol)

    def body(*refs):
        if sink is not None:
            sink_ref, refs = refs[0], refs[1:]
        (q_ref, kp_ref, kc_ref, kn_ref, vp_ref, vc_ref, vn_ref, do_ref, lse_ref, dl_ref,
         cs_c, cs_p, e_ref) = refs[:13]
        outs = refs[13:]
        if sink is not None:
            dq_ref, dk_ref, dv_ref, dsink_ref, dk_acc, dv_acc, dk_win, dv_win = outs
        else:
            dq_ref, dk_ref, dv_ref, dk_acc, dv_acc, dk_win, dv_win = outs
        s_id = pl.program_id(0)
        i = pl.program_id(1)
        slot_p, slot_c, slot_n = (i + 2) % 3, i % 3, (i + 1) % 3

        if sink is not None:
            @pl.when((s_id == 0) & (i == 0))
            def _():
                dsink_ref[...] = jnp.zeros_like(dsink_ref)

        @pl.when(i < n)
        def _():
            dk_win[...] = jnp.zeros_like(dk_win)
            dv_win[...] = jnp.zeros_like(dv_win)
            kfull, vfull = [], []
            for a in range(hkv // 2):
                ls = slice(a * PAIR, (a + 1) * PAIR)
                kfull.append(jnp.concatenate([kp_ref[:, ls], kc_ref[:, ls], kn_ref[:, ls]], axis=0) * 0.125)
                vfull.append(jnp.concatenate([vp_ref[:, ls], vc_ref[:, ls], vn_ref[:, ls]], axis=0))
            for jj in range(sub):
                rows = slice(jj * tq, (jj + 1) * tq)
                krows = slice(jj * tq, jj * tq + tk)
                mask_t = _band_mask_t(i * t + jj * tq, tq, w, seq_len)
                dq_t = [None] * (hq // 2)
                for a in range(hkv // 2):
                    ls = slice(a * PAIR, (a + 1) * PAIR)
                    kcat, vcat = kfull[a][krows], vfull[a][krows]
                    dk2 = jnp.zeros((tk, PAIR), F32)
                    dv2 = jnp.zeros((tk, PAIR), F32)
                    for e in range(2):
                        g = 2 * a + e
                        placed = {}
                        for r in range(rep):
                            h = g * rep + r
                            qp, pos = h // 2, h % 2
                            if pos not in placed:
                                placed[pos] = (_place_head(kcat, e, pos), _place_head(vcat, e, pos))
                            k_g, v_g = placed[pos]
                            q2 = q_ref[rows, qp * PAIR:(qp + 1) * PAIR]
                            do2 = do_ref[rows, qp * PAIR:(qp + 1) * PAIR]
                            lse_h = lse_ref[h:h + 1, rows]
                            dl_h = dl_ref[h:h + 1, rows]
                            p_t = jnp.exp(jnp.where(mask_t, _dot_nt(k_g, q2), NEG_INF) - lse_h)
                            dp_t = _dot_nt(v_g, do2)
                            dsb = (p_t * (dp_t - dl_h)).astype(BF16)
                            part = _dot_tn(k_g, dsb)
                            dq_t[qp] = part if dq_t[qp] is None else dq_t[qp] + part
                            dk2 = dk2 + _dot(dsb, _place_head(q2, pos, e) * 0.125)
                            dv2 = dv2 + _dot(p_t.astype(BF16), _place_head(do2, pos, e))
                            if sink is not None:
                                ds_sink = -jnp.sum(jnp.exp(sink_ref[0, h] - lse_h) * dl_h)
                                dsink_ref[h:h + 1, :] += jnp.full((1, LANES), ds_sink, F32)
                    dk_win[krows, ls] += dk2
                    dv_win[krows, ls] += dv2
                dq = jnp.concatenate(dq_t, axis=0).T
                dq_ref[rows, :] = _rope(dq, *_rope_tabs(cs_c[rows, :], e_ref[...]), -1.0).astype(BF16)

            @pl.when(i > 0)
            def _():
                dk_acc[slot_p, t - w:, :] += dk_win[:w, :]
                dv_acc[slot_p, t - w:, :] += dv_win[:w, :]

            @pl.when(i == 0)
            def _():
                dk_acc[slot_c] = dk_win[w:w + t, :]
                dv_acc[slot_c] = dv_win[w:w + t, :]

            @pl.when(i > 0)
            def _():
                dk_acc[slot_c] += dk_win[w:w + t, :]
                dv_acc[slot_c] += dv_win[w:w + t, :]

            dk_acc[slot_n] = jnp.zeros((t, kw), F32)
            dv_acc[slot_n] = jnp.zeros((t, kw), F32)
            dk_acc[slot_n, :w, :] = dk_win[w + t:, :]
            dv_acc[slot_n, :w, :] = dv_win[w + t:, :]

        @pl.when(i >= 1)
        def _():
            dk_ref[...] = _rope(dk_acc[slot_p], *_rope_tabs(cs_p[...], e_ref[...]), -1.0).astype(BF16)
            dv_ref[...] = dv_acc[slot_p].astype(BF16)

    row_c = lambda width: pl.BlockSpec((None, t, width), lambda s, i: (s, cur(s, i), 0))
    row_p = lambda width: pl.BlockSpec((None, t, width), lambda s, i: (s, jnp.maximum(i - 1, 0), 0))
    stat = pl.BlockSpec((None, hq, t), lambda s, i: (s, 0, cur(s, i)))
    in_specs = ([pl.BlockSpec((None, t, qw), lambda s, i: (s, cur(s, i), qcol))] + kv_specs
                + [row_c(qw), stat, stat, row_c(ROT_DIM), row_p(ROT_DIM),
                   pl.BlockSpec((ROT_DIM, 3 * LANES), lambda s, i: (0, 0))])
    args = [qkv] * 7 + [do, lse, delta, cs, cs, e_mat]
    out_specs = [row_c(qw), row_p(kw), row_p(kw)]
    out_shape = [jax.ShapeDtypeStruct((nseq, seq_len, qw), BF16),
                 jax.ShapeDtypeStruct((nseq, seq_len, kw), BF16),
                 jax.ShapeDtypeStruct((nseq, seq_len, kw), BF16)]
    if sink is not None:
        in_specs = [pl.BlockSpec(memory_space=pltpu.SMEM)] + in_specs
        args = [sink] + args
        out_specs.append(pl.BlockSpec((8, LANES), lambda s, i: (0, 0)))
        out_shape.append(jax.ShapeDtypeStruct((8, LANES), F32))
    return _pcall(
        body, name=name, grid=(nseq, n + 1), in_specs=in_specs, out_specs=out_specs, out_shape=out_shape,
        scratch_shapes=[pltpu.VMEM((3, t, kw), F32), pltpu.VMEM((3, t, kw), F32),
                        pltpu.VMEM((t + 2 * w, kw), F32), pltpu.VMEM((t + 2 * w, kw), F32)], args=args,
        dims=("arbitrary", "arbitrary"), comm=comm)


def _rms_parts(o, g):
    ms = jnp.mean(o * o, axis=-1, keepdims=True) + LN_EPS
    rinv = lax.rsqrt(ms)
    return o * rinv * g, rinv


def _pair_kv(kfull, vfull, qp, rep, krows):
    ks, vs, a_of = [], [], []
    for pos in range(2):
        g = (2 * qp + pos) // rep
        a_of.append(g // 2)
        ks.append(_place_head(kfull[g // 2][krows], g % 2, pos))
        vs.append(_place_head(vfull[g // 2][krows], g % 2, pos))
    assert a_of[0] == a_of[1]
    return jnp.concatenate(ks, axis=0), jnp.concatenate(vs, axis=0), a_of[0]


def _swa_fwd_p(qkv, *, qcol, kcol, vcol, hq, hkv, w, tq, sub, sink, name, comm=None):
    nseq, seq_len, _ = qkv.shape
    t = tq * sub
    n = seq_len // t
    rep = hq // hkv
    tk = tq + 2 * w
    kv_specs, cur = _halo_kv_specs(t, w, hkv, n, seq_len, kcol, vcol)

    def body(*refs):
        if sink is not None:
            sink_ref, refs = refs[0], refs[1:]
        q_ref, kp_ref, kc_ref, kn_ref, vp_ref, vc_ref, vn_ref, o_ref, lse_ref = refs
        i = pl.program_id(1)
        kfull, vfull = [], []
        for a in range(hkv // 2):
            ls = slice(a * PAIR, (a + 1) * PAIR)
            kfull.append(jnp.concatenate([kp_ref[:, ls], kc_ref[:, ls], kn_ref[:, ls]], axis=0) * 0.125)
            vfull.append(jnp.concatenate([vp_ref[:, ls], vc_ref[:, ls], vn_ref[:, ls]], axis=0))
        row_hi = lax.broadcasted_iota(jnp.int32, (PAIR, tq), 0) >= HEAD_DIM
        for jj in range(sub):
            rows = slice(jj * tq, (jj + 1) * tq)
            mask_t = _band_mask_t(i * t + jj * tq, tq, w, seq_len)
            o_t, lse_rows = [], []
            for qp in range(hq // 2):
                kst, vst, _ = _pair_kv(kfull, vfull, qp, rep, slice(jj * tq, jj * tq + tk))
                s2 = _dot_nt(kst, q_ref[rows, qp * PAIR:(qp + 1) * PAIR])
                ps, dens = [], []
                for pos in range(2):
                    h = 2 * qp + pos
                    s_t = jnp.where(mask_t, s2[pos * tk:(pos + 1) * tk], NEG_INF)
                    m = jnp.max(s_t, axis=0, keepdims=True)
                    if sink is not None:
                        m = jnp.maximum(m, sink_ref[0, h])
                    p_t = jnp.exp(s_t - m)
                    den = jnp.sum(p_t, axis=0, keepdims=True)
                    if sink is not None:
                        den = den + jnp.exp(sink_ref[0, h] - m)
                    ps.append(p_t.astype(BF16))
                    dens.append(den)
                    lse_rows.append(m + jnp.log(den))
                both = _dot_tn(vst, jnp.concatenate(ps, axis=0))
                o_t.append(both / jnp.where(row_hi, dens[1], dens[0]))
            o_ref[rows, :] = jnp.concatenate(o_t, axis=0).T
            lse_ref[:, rows] = jnp.concatenate(lse_rows, axis=0)

    in_specs = [pl.BlockSpec((None, t, hq * HEAD_DIM), lambda s, i: (s, i, qcol))] + kv_specs
    args = [qkv] * 7
    if sink is not None:
        in_specs = [pl.BlockSpec(memory_space=pltpu.SMEM)] + in_specs
        args = [sink] + args
    (o, lse), couts = _pcall(
        body, name=name, grid=(nseq, n), in_specs=in_specs,
        out_specs=[pl.BlockSpec((None, t, hq * HEAD_DIM), lambda s, i: (s, i, 0)),
                   pl.BlockSpec((None, hq, t), lambda s, i: (s, 0, i))],
        out_shape=[jax.ShapeDtypeStruct((nseq, seq_len, hq * HEAD_DIM), F32),
                   jax.ShapeDtypeStruct((nseq, hq, seq_len), F32)],
        args=args, dims=("parallel", "parallel"), comm=comm)
    return o, lse, couts


def _swa_bwd_p(qkv, do, lse, delta, cs, e_mat, *, qcol, kcol, vcol, hq, hkv, w, tq, sub, sink, name, comm=None):
    nseq, seq_len, _ = qkv.shape
    t = tq * sub
    n = seq_len // t
    rep = hq // hkv
    qw, kw = hq * HEAD_DIM, hkv * HEAD_DIM
    tk = tq + 2 * w
    kv_specs, cur = _halo_kv_specs(t, w, hkv, n, seq_len, kcol, vcol)

    def body(*refs):
        if sink is not None:
            sink_ref, refs = refs[0], refs[1:]
        (q_ref, kp_ref, kc_ref, kn_ref, vp_ref, vc_ref, vn_ref, do_ref, lse_ref, dl_ref,
         cs_c, cs_p, e_ref) = refs[:13]
        outs = refs[13:]
        if sink is not None:
            dq_ref, dk_ref, dv_ref, dsink_ref, dk_acc, dv_acc, dk_win, dv_win = outs
        else:
            dq_ref, dk_ref, dv_ref, dk_acc, dv_acc, dk_win, dv_win = outs
        s_id = pl.program_id(0)
        i = pl.program_id(1)
        slot_p, slot_c, slot_n = (i + 2) % 3, i % 3, (i + 1) % 3

        if sink is not None:
            @pl.when((s_id == 0) & (i == 0))
            def _():
                dsink_ref[...] = jnp.zeros_like(dsink_ref)

        @pl.when(i < n)
        def _():
            dk_win[...] = jnp.zeros_like(dk_win)
            dv_win[...] = jnp.zeros_like(dv_win)
            kfull, vfull = [], []
            for a in range(hkv // 2):
                ls = slice(a * PAIR, (a + 1) * PAIR)
                kfull.append(jnp.concatenate([kp_ref[:, ls], kc_ref[:, ls], kn_ref[:, ls]], axis=0) * 0.125)
                vfull.append(jnp.concatenate([vp_ref[:, ls], vc_ref[:, ls], vn_ref[:, ls]], axis=0))
            for jj in range(sub):
                rows = slice(jj * tq, (jj + 1) * tq)
                krows = slice(jj * tq, jj * tq + tk)
                mask_t = _band_mask_t(i * t + jj * tq, tq, w, seq_len)
                dq_t = []
                dk2 = [None] * (hkv // 2)
                dv2 = [None] * (hkv // 2)
                for qp in range(hq // 2):
                    kst, vst, a = _pair_kv(kfull, vfull, qp, rep, krows)
                    q2 = q_ref[rows, qp * PAIR:(qp + 1) * PAIR]
                    do2 = do_ref[rows, qp * PAIR:(qp + 1) * PAIR]
                    s2 = _dot_nt(kst, q2)
                    dp2 = _dot_nt(vst, do2)
                    ds, ps, q_at, do_at = [], [], [], []
                    for pos in range(2):
                        h = 2 * qp + pos
                        e = (h // rep) % 2
                        half = slice(pos * tk, (pos + 1) * tk)
                        lse_h = lse_ref[h:h + 1, rows]
                        dl_h = dl_ref[h:h + 1, rows]
                        p_t = jnp.exp(jnp.where(mask_t, s2[half], NEG_INF) - lse_h)
                        ds.append((p_t * (dp2[half] - dl_h)).astype(BF16))
                        ps.append(p_t.astype(BF16))
                        q_at.append(_place_head(q2, pos, e) * 0.125)
                        do_at.append(_place_head(do2, pos, e))
                        if sink is not None:
                            ds_sink = -jnp.sum(jnp.exp(sink_ref[0, h] - lse_h) * dl_h)
                            dsink_ref[h:h + 1, :] += jnp.full((1, LANES), ds_sink, F32)
                    dq_t.append(_rope_rows(_dot_tn(kst, jnp.concatenate(ds, axis=0)),
                                           cs_c[0:ROT_DIM // 2, rows], cs_c[ROT_DIM // 2:ROT_DIM, rows], -1.0))
                    dk_part = _dot(jnp.concatenate(ds, axis=1), jnp.concatenate(q_at, axis=0))
                    dv_part = _dot(jnp.concatenate(ps, axis=1), jnp.concatenate(do_at, axis=0))
                    dk2[a] = dk_part if dk2[a] is None else dk2[a] + dk_part
                    dv2[a] = dv_part if dv2[a] is None else dv2[a] + dv_part
                for a in range(hkv // 2):
                    ls = slice(a * PAIR, (a + 1) * PAIR)
                    dk_win[krows, ls] += dk2[a]
                    dv_win[krows, ls] += dv2[a]
                dq_ref[rows, :] = jnp.concatenate(dq_t, axis=0).T.astype(BF16)

            @pl.when(i > 0)
            def _():
                dk_acc[slot_p, t - w:, :] += dk_win[:w, :]
                dv_acc[slot_p, t - w:, :] += dv_win[:w, :]

            @pl.when(i == 0)
            def _():
                dk_acc[slot_c] = dk_win[w:w + t, :]
                dv_acc[slot_c] = dv_win[w:w + t, :]

            @pl.when(i > 0)
            def _():
                dk_acc[slot_c] += dk_win[w:w + t, :]
                dv_acc[slot_c] += dv_win[w:w + t, :]

            dk_acc[slot_n] = jnp.zeros((t, kw), F32)
            dv_acc[slot_n] = jnp.zeros((t, kw), F32)
            dk_acc[slot_n, :w, :] = dk_win[w + t:, :]
            dv_acc[slot_n, :w, :] = dv_win[w + t:, :]

        @pl.when(i >= 1)
        def _():
            dk_ref[...] = _rope(dk_acc[slot_p], *_rope_tabs(cs_p[...], e_ref[...]), -1.0).astype(BF16)
            dv_ref[...] = dv_acc[slot_p].astype(BF16)

    row_c = lambda width: pl.BlockSpec((None, t, width), lambda s, i: (s, cur(s, i), 0))
    row_p = lambda width: pl.BlockSpec((None, t, width), lambda s, i: (s, jnp.maximum(i - 1, 0), 0))
    stat = pl.BlockSpec((None, hq, t), lambda s, i: (s, 0, cur(s, i)))
    cs_rows = pl.BlockSpec((None, ROT_DIM, t), lambda s, i: (s, 0, cur(s, i)))
    in_specs = ([pl.BlockSpec((None, t, qw), lambda s, i: (s, cur(s, i), qcol))] + kv_specs
                + [row_c(qw), stat, stat, cs_rows, row_p(ROT_DIM),
                   pl.BlockSpec((ROT_DIM, 3 * LANES), lambda s, i: (0, 0))])
    args = [qkv] * 7 + [do, lse, delta, cs.transpose(0, 2, 1), cs, e_mat]
    out_specs = [row_c(qw), row_p(kw), row_p(kw)]
    out_shape = [jax.ShapeDtypeStruct((nseq, seq_len, qw), BF16),
                 jax.ShapeDtypeStruct((nseq, seq_len, kw), BF16),
                 jax.ShapeDtypeStruct((nseq, seq_len, kw), BF16)]
    if sink is not None:
        in_specs = [pl.BlockSpec(memory_space=pltpu.SMEM)] + in_specs
        args = [sink] + args
        out_specs.append(pl.BlockSpec((8, LANES), lambda s, i: (0, 0)))
        out_shape.append(jax.ShapeDtypeStruct((8, LANES), F32))
    return _pcall(
        body, name=name, grid=(nseq, n + 1), in_specs=in_specs, out_specs=out_specs, out_shape=out_shape,
        scratch_shapes=[pltpu.VMEM((3, t, kw), F32), pltpu.VMEM((3, t, kw), F32),
                        pltpu.VMEM((t + 2 * w, kw), F32), pltpu.VMEM((t + 2 * w, kw), F32)], args=args,
        dims=("arbitrary", "arbitrary"), comm=comm)


def _from_subsequences(ref, scr, dil, t):
    slabs = ref.shape[-1] // LANES
    if dil == 1:
        return ref[0].astype(F32)
    for c in range(dil):
        for sl in range(slabs):
            scr[sl, pl.ds(c, t // dil, stride=dil), :] = ref[c, :, sl * LANES:(sl + 1) * LANES].astype(F32)
    return jnp.concatenate([scr[sl] for sl in range(slabs)], axis=1)


def _to_subsequences(val, ref, scr, dil, t):
    slabs = val.shape[-1] // LANES
    if dil == 1:
        ref[0] = val.astype(ref.dtype)
        return
    for sl in range(slabs):
        scr[sl] = val[:, sl * LANES:(sl + 1) * LANES]
    for c in range(dil):
        for sl in range(slabs):
            ref[c, :, sl * LANES:(sl + 1) * LANES] = scr[sl, pl.ds(c, t // dil, stride=dil), :].astype(ref.dtype)


def _combine_fwd(out_a, o_g, lse_g, g_win, g_dil, *, t):
    s = out_a.shape[1]
    wd = DIL_SLOTS * HEAD_DIM

    def body(oa_ref, o0, o1, o2, l0, l1, l2, gw_ref, gd_ref, mixed_ref, ob_ref, lt_ref, scr):
        ls = [l0[...], l1[...], l2[...]]
        mx = jnp.maximum(jnp.maximum(ls[0], ls[1]), ls[2])
        ws = [jnp.exp(l - mx) for l in ls]
        tot = ws[0] + ws[1] + ws[2]
        lt_ref[...] = mx + jnp.log(tot)
        ws = [x / tot for x in ws]
        og = [_from_subsequences(o_ref, scr.at[gi], dil, t)
              for gi, (o_ref, dil) in enumerate(zip((o0, o1, o2), DILATIONS))]
        parts = []
        for h in range(DIL_SLOTS):
            hs = slice(h * HEAD_DIM, (h + 1) * HEAD_DIM)
            parts.append(ws[0][:, h:h + 1] * og[0][:, hs] + ws[1][:, h:h + 1] * og[1][:, hs]
                         + ws[2][:, h:h + 1] * og[2][:, hs])
        ob = jnp.concatenate(parts, axis=1)
        ob_ref[...] = ob
        na, _ = _rms_parts(oa_ref[...], gw_ref[...])
        nb, _ = _rms_parts(ob, gd_ref[...])
        mixed_ref[:, :wd] = na.astype(BF16)
        mixed_ref[:, wd:] = nb.astype(BF16)

    half = pl.BlockSpec((t, wd), lambda i: (i, 0))
    lanes = pl.BlockSpec((t, LANES), lambda i: (i, 0))
    grow = pl.BlockSpec((1, wd), lambda i: (0, 0))
    subseq = [pl.BlockSpec((dil, t // dil, wd), lambda i: (0, i, 0)) for dil in DILATIONS]
    return pl.pallas_call(
        body, name="combine_fwd", grid=(s // t,),
        in_specs=[pl.BlockSpec((None, t, wd), lambda i: (0, i, 0))] + subseq + [lanes, lanes, lanes, grow, grow],
        out_specs=[pl.BlockSpec((t, 2 * wd), lambda i: (i, 0)), half, lanes],
        out_shape=[jax.ShapeDtypeStruct((s, 2 * wd), BF16), jax.ShapeDtypeStruct((s, wd), F32),
                   jax.ShapeDtypeStruct((s, LANES), F32)],
        scratch_shapes=[pltpu.VMEM((len(DILATIONS), wd // LANES, t, LANES), F32)],
        compiler_params=_cparams(dimension_semantics=("parallel",)),
    )(out_a, *o_g, *lse_g, g_win, g_dil)


def _combine_bwd(dmixed, out_a, out_b, g_win, g_dil, *, t):
    s = out_b.shape[0]
    wd = DIL_SLOTS * HEAD_DIM

    def body(dm_ref, oa_ref, ob_ref, gw_ref, gd_ref, doa_ref, dob0, dob1, dob2, dla_ref, dlb_ref, st_ref, scr):
        i = pl.program_id(0)

        @pl.when(i == 0)
        def _():
            st_ref[...] = jnp.zeros_like(st_ref)

        lane = lax.broadcasted_iota(jnp.int32, (t, LANES), 1)
        for idx, (o_ref, g_ref, dl_ref) in enumerate(((oa_ref, gw_ref, dla_ref), (ob_ref, gd_ref, dlb_ref))):
            o = o_ref[...]
            dn = dm_ref[:, idx * wd:(idx + 1) * wd]
            _, rinv = _rms_parts(o, g_ref[...])
            wv = dn * g_ref[...]
            do = rinv * wv - o * (rinv * rinv * rinv) * jnp.mean(wv * o, axis=-1, keepdims=True)
            st_ref[idx:idx + 1, :] += jnp.sum(dn * o * rinv, axis=0, keepdims=True)
            if idx == 0:
                doa_ref[...] = do.astype(BF16)
            else:
                for do_ref, dil in zip((dob0, dob1, dob2), DILATIONS):
                    _to_subsequences(do, do_ref, scr, dil, t)
            prod = do * o
            acc = jnp.zeros((t, LANES), F32)
            for h in range(DIL_SLOTS):
                hs = slice(h * HEAD_DIM, (h + 1) * HEAD_DIM)
                acc = jnp.where(lane == h, jnp.sum(prod[:, hs], axis=1, keepdims=True), acc)
            dl_ref[...] = acc

    half = pl.BlockSpec((t, wd), lambda i: (i, 0))
    lanes = pl.BlockSpec((t, LANES), lambda i: (i, 0))
    grow = pl.BlockSpec((1, wd), lambda i: (0, 0))
    a_spec = pl.BlockSpec((None, t, wd), lambda i: (0, i, 0))
    subseq = [pl.BlockSpec((dil, t // dil, wd), lambda i: (0, i, 0)) for dil in DILATIONS]
    doa, dob0, dob1, dob2, dla, dlb, st = pl.pallas_call(
        body, name="combine_bwd", grid=(s // t,),
        in_specs=[pl.BlockSpec((t, 2 * wd), lambda i: (i, 0)), a_spec, half, grow, grow],
        out_specs=[a_spec] + subseq + [lanes, lanes, pl.BlockSpec((8, wd), lambda i: (0, 0))],
        out_shape=[jax.ShapeDtypeStruct((1, s, wd), BF16)]
        + [jax.ShapeDtypeStruct((dil, s // dil, wd), BF16) for dil in DILATIONS]
        + [jax.ShapeDtypeStruct((s, LANES), F32), jax.ShapeDtypeStruct((s, LANES), F32),
           jax.ShapeDtypeStruct((8, wd), F32)],
        scratch_shapes=[pltpu.VMEM((wd // LANES, t, LANES), F32)],
        compiler_params=_cparams(dimension_semantics=("arbitrary",)),
    )(dmixed, out_a, out_b, g_win, g_dil)
    return doa, [dob0, dob1, dob2], dla, dlb, st


def _assemble_dz(dqa, dka, dva, dqs, dks, dvs, *, t):
    s = dqa.shape[1]
    wd = DIL_SLOTS * HEAD_DIM

    def body(*refs):
        a_refs, g_refs, o_ref, scr = refs[:3], refs[3:12], refs[12], refs[13]
        col = 0
        for r in a_refs:
            o_ref[:, col:col + r.shape[-1]] = r[...]
            col += r.shape[-1]
        for part in range(3):
            for gi, dil in enumerate(DILATIONS):
                val = _from_subsequences(g_refs[3 * part + gi], scr, dil, t)
                o_ref[:, col:col + wd] = val.astype(BF16)
                col += wd

    a_specs = [pl.BlockSpec((None, t, a.shape[-1]), lambda i: (0, i, 0)) for a in (dqa, dka, dva)]
    g_specs = [pl.BlockSpec((dil, t // dil, wd), lambda i: (0, i, 0)) for _ in range(3) for dil in DILATIONS]
    return pl.pallas_call(
        body, name="assemble_dz", grid=(s // t,), in_specs=a_specs + g_specs,
        out_specs=pl.BlockSpec((t, IN_WIDTH), lambda i: (i, 0)),
        out_shape=jax.ShapeDtypeStruct((s, IN_WIDTH), BF16),
        scratch_shapes=[pltpu.VMEM((wd // LANES, t, LANES), F32)],
        compiler_params=_cparams(dimension_semantics=("parallel",)),
    )(dqa, dka, dva, *dqs, *dks, *dvs)


def _mixproj_fwd(mixed_b, w_mix_b, x, ln_in_g, ln_in_b, ln1_g, ln1_b, *, t):
    s = x.shape[0]

    def body(m_ref, w_ref, x_ref, g0, b0, g1, b1, r1_ref, h1_ref):
        h0 = _ln(x_ref[...], g0[...], b0[...])
        r1 = ALPHA * h0 + _dot(m_ref[...], w_ref[...])
        r1_ref[...] = r1
        h1_ref[...] = _ln(r1, g1[...], b1[...]).astype(BF16)

    tile = pl.BlockSpec((t, D_MODEL), lambda i: (i, 0))
    row = pl.BlockSpec((1, D_MODEL), lambda i: (0, 0))
    return pl.pallas_call(
        body, name="mixproj_fwd", grid=(s // t,),
        in_specs=[tile, pl.BlockSpec((D_MODEL, D_MODEL), lambda i: (0, 0)), tile, row, row, row, row],
        out_specs=[tile, tile],
        out_shape=[jax.ShapeDtypeStruct((s, D_MODEL), F32), jax.ShapeDtypeStruct((s, D_MODEL), BF16)],
        compiler_params=_cparams(dimension_semantics=("parallel",)),
    )(mixed_b, w_mix_b, x, ln_in_g, ln_in_b, ln1_g, ln1_b)


def _mem_fwd(mem, g, b, wk_b, wv_b):
    ml = mem.shape[0]

    def body(mem_ref, g_ref, b_ref, wk_ref, wv_ref, mn_ref, kx_ref, vx_ref):
        mn = _ln(mem_ref[...], g_ref[...], b_ref[...]).astype(BF16)
        mn_ref[...] = mn
        kx_ref[...] = _dot(mn, wk_ref[...]).astype(BF16)
        vx_ref[...] = _dot(mn, wv_ref[...]).astype(BF16)

    sh = jax.ShapeDtypeStruct((ml, D_MODEL), BF16)
    return pl.pallas_call(body, name="mem_fwd", out_shape=[sh, sh, sh], compiler_params=_cparams())(
        mem, g, b, wk_b, wv_b)


def _mem_bwd(dkx, dvx, mem, g, b, wk_b, wv_b):
    def body(dk_ref, dv_ref, mem_ref, g_ref, b_ref, wk_ref, wv_ref, dwk_ref, dwv_ref, st_ref):
        mem_v = mem_ref[...]
        mn = _ln(mem_v, g_ref[...], b_ref[...]).astype(BF16)
        dkb = dk_ref[...].astype(BF16)
        dvb = dv_ref[...].astype(BF16)
        dwk_ref[...] = _dot_tn(mn, dkb)
        dwv_ref[...] = _dot_tn(mn, dvb)
        dmn = _dot_nt(dkb, wk_ref[...]) + _dot_nt(dvb, wv_ref[...])
        _, dg, db = _ln_bwd_math(dmn, mem_v, g_ref[...])
        st_ref[...] = jnp.zeros_like(st_ref)
        st_ref[0:1, :] = dg
        st_ref[1:2, :] = db

    sw = jax.ShapeDtypeStruct((D_MODEL, D_MODEL), F32)
    return pl.pallas_call(body, name="mem_bwd", out_shape=[sw, sw, jax.ShapeDtypeStruct((8, D_MODEL), F32)],
                          compiler_params=_cparams())(dkx, dvx, mem, g, b, wk_b, wv_b)


def _xattn_fwd(h1b, r1, kx, vx, wq_b, wo_b, ln1_g, ln1_b, ln2_g, ln2_b, *, t):
    s = h1b.shape[0]
    scale = X_HEAD_DIM ** -0.5

    def body(h_ref, r1_ref, kx_ref, vx_ref, wq_ref, wo_ref, g1, b1, g2, b2, r2_ref, h2_ref, qx_ref, ox_ref, lse_ref):
        qxb = _dot(h_ref[...], wq_ref[...]).astype(BF16)
        qx_ref[...] = qxb
        lane = lax.broadcasted_iota(jnp.int32, (t, LANES), 1)
        lse_acc = jnp.zeros((t, LANES), F32)
        parts = []
        for h in range(X_HEADS):
            hs = slice(h * X_HEAD_DIM, (h + 1) * X_HEAD_DIM)
            sc = _dot_nt(qxb[:, hs] * scale, kx_ref[:, hs])
            m = jnp.max(sc, axis=1, keepdims=True)
            p = jnp.exp(sc - m)
            den = jnp.sum(p, axis=1, keepdims=True)
            parts.append(_dot(p.astype(BF16), vx_ref[:, hs]) / den)
            lse_acc = jnp.where(lane == h, m + jnp.log(den), lse_acc)
        lse_ref[...] = lse_acc
        oxb = jnp.concatenate(parts, axis=1).astype(BF16)
        ox_ref[...] = oxb
        h1 = _ln(r1_ref[...], g1[...], b1[...])
        r2 = ALPHA * h1 + _dot(oxb, wo_ref[...])
        r2_ref[...] = r2
        h2_ref[...] = _ln(r2, g2[...], b2[...]).astype(BF16)

    tile = pl.BlockSpec((t, D_MODEL), lambda i: (i, 0))
    row = pl.BlockSpec((1, D_MODEL), lambda i: (0, 0))
    full = lambda r: pl.BlockSpec((r, D_MODEL), lambda i: (0, 0))
    ml = kx.shape[0]
    bsh = jax.ShapeDtypeStruct((s, D_MODEL), BF16)
    return pl.pallas_call(
        body, name="xattn_fwd", grid=(s // t,),
        in_specs=[tile, tile, full(ml), full(ml), full(D_MODEL), full(D_MODEL), row, row, row, row],
        out_specs=[tile, tile, tile, tile, pl.BlockSpec((t, LANES), lambda i: (i, 0))],
        out_shape=[jax.ShapeDtypeStruct((s, D_MODEL), F32), bsh, bsh, bsh, jax.ShapeDtypeStruct((s, LANES), F32)],
        compiler_params=_cparams(dimension_semantics=("parallel",)),
    )(h1b, r1, kx, vx, wq_b, wo_b, ln1_g, ln1_b, ln2_g, ln2_b)


def _xattn_bwd(dr2, qxb, oxb, lse, kx, vx, wq_b, wo_b, r1, ln1_g, *, t, comm=None):
    s = dr2.shape[0]
    ml = kx.shape[0]
    scale = X_HEAD_DIM ** -0.5

    def body(dr2_ref, qx_ref, ox_ref, lse_ref, kx_ref, vx_ref, wq_ref, wo_ref, r1_ref, g1_ref,
             dr1_ref, dr1b_ref, dqx_ref, dkx_ref, dvx_ref, st_ref):
        i = pl.program_id(0)

        @pl.when(i == 0)
        def _():
            dkx_ref[...] = jnp.zeros_like(dkx_ref)
            dvx_ref[...] = jnp.zeros_like(dvx_ref)
            st_ref[...] = jnp.zeros_like(st_ref)

        dr2v = dr2_ref[...]
        dox = _dot_nt(dr2v.astype(BF16), wo_ref[...])
        parts = []
        for h in range(X_HEADS):
            hs = slice(h * X_HEAD_DIM, (h + 1) * X_HEAD_DIM)
            doh = dox[:, hs]
            dohb = doh.astype(BF16)
            dl = jnp.sum(doh * ox_ref[:, hs].astype(F32), axis=1, keepdims=True)
            qh = qx_ref[:, hs] * scale
            p = jnp.exp(_dot_nt(qh, kx_ref[:, hs]) - lse_ref[:, h:h + 1])
            dp = _dot_nt(dohb, vx_ref[:, hs])
            dsb = (p * (dp - dl)).astype(BF16)
            parts.append(_dot(dsb, kx_ref[:, hs]) * scale)
            dkx_ref[:, hs] += _dot_tn(dsb, qh)
            dvx_ref[:, hs] += _dot_tn(p.astype(BF16), dohb)
        dqxb = jnp.concatenate(parts, axis=1).astype(BF16)
        dqx_ref[...] = dqxb
        dh1 = _dot_nt(dqxb, wq_ref[...]) + ALPHA * dr2v
        dr1, dg, db = _ln_bwd_math(dh1, r1_ref[...], g1_ref[...])
        dr1_ref[...] = dr1
        dr1b_ref[...] = dr1.astype(BF16)
        st_ref[0:1, :] += dg
        st_ref[1:2, :] += db

    tile = pl.BlockSpec((t, D_MODEL), lambda i: (i, 0))
    full = lambda r: pl.BlockSpec((r, D_MODEL), lambda i: (0, 0))
    bsh = jax.ShapeDtypeStruct((s, D_MODEL), BF16)
    return _pcall(
        body, name="xattn_bwd", grid=(s // t,),
        in_specs=[tile, tile, tile, pl.BlockSpec((t, LANES), lambda i: (i, 0)), full(ml), full(ml),
                  full(D_MODEL), full(D_MODEL), tile, full(1)],
        out_specs=[tile, tile, tile, full(ml), full(ml), full(8)],
        out_shape=[jax.ShapeDtypeStruct((s, D_MODEL), F32), bsh, bsh,
                   jax.ShapeDtypeStruct((ml, D_MODEL), F32), jax.ShapeDtypeStruct((ml, D_MODEL), F32),
                   jax.ShapeDtypeStruct((8, D_MODEL), F32)],
        args=[dr2, qxb, oxb, lse, kx, vx, wq_b, wo_b, r1, ln1_g], dims=("arbitrary",), comm=comm)


def _halo_specs(t, s, width):
    tb8 = t // 8
    return [pl.BlockSpec((t, width), lambda i: (i, 0)),
            pl.BlockSpec((8, width), lambda i: (jnp.maximum(i * tb8 - 1, 0), 0)),
            pl.BlockSpec((8, width), lambda i: (jnp.minimum((i + 1) * tb8, s // 8 - 1), 0))]


def _halo_rows(i, n, prev_ref, next_ref):
    prev_row = jnp.where(i > 0, prev_ref[7:8, :], 0.0)
    next_row = jnp.where(i < n - 1, next_ref[0:1, :], 0.0)
    return prev_row, next_row


def _gelu_parts(gc):
    cdf = 0.5 * (1.0 + lax.erf(gc * (2.0 ** -0.5)))
    pdf = jnp.exp(-0.5 * gc * gc) * (1.0 / math.sqrt(2.0 * math.pi))
    return gc * cdf, cdf + gc * pdf


def _conv_fwd(g, u, conv_w, conv_b, *, t):
    s = g.shape[0]
    n = s // t

    def body(g_ref, gp_ref, gn_ref, u_ref, cw_ref, cb_ref, o_ref):
        i = pl.program_id(0)
        gv = g_ref[...]
        prev_row, next_row = _halo_rows(i, n, gp_ref, gn_ref)
        gm1, gp1 = _shift_rows(gv, prev_row, next_row)
        gc = gm1 * cw_ref[0:1, :] + gv * cw_ref[1:2, :] + gp1 * cw_ref[2:3, :] + cb_ref[...]
        act, _ = _gelu_parts(gc)
        o_ref[...] = (act * u_ref[...]).astype(BF16)

    tile = pl.BlockSpec((t, D_FF), lambda i: (i, 0))
    return pl.pallas_call(
        body, name="conv_fwd", grid=(n,),
        in_specs=_halo_specs(t, s, D_FF) + [tile, pl.BlockSpec((3, D_FF), lambda i: (0, 0)),
                                            pl.BlockSpec((1, D_FF), lambda i: (0, 0))],
        out_specs=tile, out_shape=jax.ShapeDtypeStruct((s, D_FF), BF16),
        compiler_params=_cparams(dimension_semantics=("parallel",)),
    )(g, g, g, u, conv_w, conv_b)


def _down_ln3(tb, w_down_b, r2, target, ln2_g, ln2_b, ln3_g, ln3_b, *, t):
    s = r2.shape[0]

    def body(t_ref, w_ref, r2_ref, tg_ref, g2, b2, g3, b3, dr_ref, drb_ref, st_ref):
        i = pl.program_id(0)

        @pl.when(i == 0)
        def _():
            st_ref[...] = jnp.zeros_like(st_ref)

        h2 = _ln(r2_ref[...], g2[...], b2[...])
        r3 = ALPHA * h2 + _dot(t_ref[...], w_ref[...])
        y = _ln(r3, g3[...], b3[...])
        err = y - tg_ref[...]
        loss = 0.5 * jnp.sum(jnp.mean(err * err, axis=-1, keepdims=True))
        dy = err * (1.0 / D_MODEL)
        dr, dg, db = _ln_bwd_math(dy, r3, g3[...])
        dr_ref[...] = dr
        drb_ref[...] = dr.astype(BF16)
        st_ref[0:1, :] += dg
        st_ref[1:2, :] += db
        st_ref[2:3, :] += jnp.full((1, D_MODEL), loss, F32)

    tile = pl.BlockSpec((t, D_MODEL), lambda i: (i, 0))
    row = pl.BlockSpec((1, D_MODEL), lambda i: (0, 0))
    return pl.pallas_call(
        body, name="down_ln3", grid=(s // t,),
        in_specs=[pl.BlockSpec((t, D_FF), lambda i: (i, 0)), pl.BlockSpec((D_FF, D_MODEL), lambda i: (0, 0)),
                  tile, tile, row, row, row, row],
        out_specs=[tile, tile, pl.BlockSpec((8, D_MODEL), lambda i: (0, 0))],
        out_shape=[jax.ShapeDtypeStruct((s, D_MODEL), F32), jax.ShapeDtypeStruct((s, D_MODEL), BF16),
                   jax.ShapeDtypeStruct((8, D_MODEL), F32)],
        compiler_params=_cparams(dimension_semantics=("arbitrary",)),
    )(tb, w_down_b, r2, target, ln2_g, ln2_b, ln3_g, ln3_b)


def _ffn_out(g, u, conv_w, conv_b, w_down_b, r2, target, ln2_g, ln2_b, ln3_g, ln3_b, *, t):
    s = r2.shape[0]
    n = s // t

    def body(g_ref, gp_ref, gn_ref, u_ref, cw_ref, cb_ref, w_ref, r2_ref, tg_ref, g2, b2, g3, b3,
             t_ref, dr_ref, drb_ref, st_ref):
        i = pl.program_id(0)

        @pl.when(i == 0)
        def _():
            st_ref[...] = jnp.zeros_like(st_ref)

        gv = g_ref[...]
        prev_row, next_row = _halo_rows(i, n, gp_ref, gn_ref)
        gm1, gp1 = _shift_rows(gv, prev_row, next_row)
        gc = gm1 * cw_ref[0:1, :] + gv * cw_ref[1:2, :] + gp1 * cw_ref[2:3, :] + cb_ref[...]
        act, _ = _gelu_parts(gc)
        tb = (act * u_ref[...]).astype(BF16)
        t_ref[...] = tb
        h2 = _ln(r2_ref[...], g2[...], b2[...])
        r3 = ALPHA * h2 + _dot(tb, w_ref[...])
        y = _ln(r3, g3[...], b3[...])
        err = y - tg_ref[...]
        loss = 0.5 * jnp.sum(jnp.mean(err * err, axis=-1, keepdims=True))
        dr, dg, db = _ln_bwd_math(err * (1.0 / D_MODEL), r3, g3[...])
        dr_ref[...] = dr
        drb_ref[...] = dr.astype(BF16)
        st_ref[0:1, :] += dg
        st_ref[1:2, :] += db
        st_ref[2:3, :] += jnp.full((1, D_MODEL), loss, F32)

    wide = pl.BlockSpec((t, D_FF), lambda i: (i, 0))
    tile = pl.BlockSpec((t, D_MODEL), lambda i: (i, 0))
    row = pl.BlockSpec((1, D_MODEL), lambda i: (0, 0))
    return pl.pallas_call(
        body, name="ffn_out", grid=(n,),
        in_specs=_halo_specs(t, s, D_FF) + [wide, pl.BlockSpec((3, D_FF), lambda i: (0, 0)),
                                            pl.BlockSpec((1, D_FF), lambda i: (0, 0)),
                                            pl.BlockSpec((D_FF, D_MODEL), lambda i: (0, 0)),
                                            tile, tile, row, row, row, row],
        out_specs=[wide, tile, tile, pl.BlockSpec((8, D_MODEL), lambda i: (0, 0))],
        out_shape=[jax.ShapeDtypeStruct((s, D_FF), BF16), jax.ShapeDtypeStruct((s, D_MODEL), F32),
                   jax.ShapeDtypeStruct((s, D_MODEL), BF16), jax.ShapeDtypeStruct((8, D_MODEL), F32)],
        compiler_params=_cparams(dimension_semantics=("arbitrary",)),
    )(g, g, g, u, conv_w, conv_b, w_down_b, r2, target, ln2_g, ln2_b, ln3_g, ln3_b)


def _dh2_ln2(dgc, conv_w, du, w_gate_b, w_up_b, dr3, r2, ln2_g, *, t, comm=None):
    s = dgc.shape[0]
    n = s // t

    def body(d_ref, dp_ref, dn_ref, cw_ref, du_ref, wg_ref, wu_ref, dr3_ref, r2_ref, g2, dg_ref, dr_ref, drb_ref,
             st_ref):
        i = pl.program_id(0)

        @pl.when(i == 0)
        def _():
            st_ref[...] = jnp.zeros_like(st_ref)

        dv = d_ref[...]
        prev_row, next_row = _halo_rows(i, n, dp_ref, dn_ref)
        dm1, dp1 = _shift_rows(dv, prev_row, next_row)
        dgb = (dp1 * cw_ref[0:1, :] + dv * cw_ref[1:2, :] + dm1 * cw_ref[2:3, :]).astype(BF16)
        dg_ref[...] = dgb
        dh2 = _dot(dgb, wg_ref[...]) + _dot(du_ref[...], wu_ref[...]) + ALPHA * dr3_ref[...]
        dr, dg, db = _ln_bwd_math(dh2, r2_ref[...], g2[...])
        dr_ref[...] = dr
        drb_ref[...] = dr.astype(BF16)
        st_ref[0:1, :] += dg
        st_ref[1:2, :] += db

    wide = pl.BlockSpec((t, D_FF), lambda i: (i, 0))
    tile = pl.BlockSpec((t, D_MODEL), lambda i: (i, 0))
    wfull = pl.BlockSpec((D_FF, D_MODEL), lambda i: (0, 0), pipeline_mode=pl.Buffered(1))
    return _pcall(
        body, name="dh2_ln2", grid=(n,),
        in_specs=_halo_specs(t, s, D_FF) + [pl.BlockSpec((3, D_FF), lambda i: (0, 0)), wide, wfull, wfull,
                                            tile, tile, pl.BlockSpec((1, D_MODEL), lambda i: (0, 0))],
        out_specs=[wide, tile, tile, pl.BlockSpec((8, D_MODEL), lambda i: (0, 0))],
        out_shape=[jax.ShapeDtypeStruct((s, D_FF), BF16), jax.ShapeDtypeStruct((s, D_MODEL), F32),
                   jax.ShapeDtypeStruct((s, D_MODEL), BF16), jax.ShapeDtypeStruct((8, D_MODEL), F32)],
        args=[dgc, dgc, dgc, conv_w, du, w_gate_b, w_up_b, dr3, r2, ln2_g], dims=("arbitrary",), comm=comm)


def _conv_bwd_a(dr3b, w_down_b, g, u, conv_w, conv_b, *, t):
    s = g.shape[0]
    n = s // t

    def body(d_ref, w_ref, g_ref, gp_ref, gn_ref, u_ref, cw_ref, cb_ref, du_ref, dgc_ref, st_ref):
        i = pl.program_id(0)

        @pl.when(i == 0)
        def _():
            st_ref[...] = jnp.zeros_like(st_ref)

        dt = _dot_nt(d_ref[...], w_ref[...])
        gv = g_ref[...]
        prev_row, next_row = _halo_rows(i, n, gp_ref, gn_ref)
        gm1, gp1 = _shift_rows(gv, prev_row, next_row)
        gc = gm1 * cw_ref[0:1, :] + gv * cw_ref[1:2, :] + gp1 * cw_ref[2:3, :] + cb_ref[...]
        act, dact = _gelu_parts(gc)
        du_ref[...] = (dt * act).astype(BF16)
        dgc = dt * u_ref[...] * dact
        dgc_ref[...] = dgc
        st_ref[0:1, :] += jnp.sum(gm1 * dgc, axis=0, keepdims=True)
        st_ref[1:2, :] += jnp.sum(gv * dgc, axis=0, keepdims=True)
        st_ref[2:3, :] += jnp.sum(gp1 * dgc, axis=0, keepdims=True)
        st_ref[3:4, :] += jnp.sum(dgc, axis=0, keepdims=True)

    tile = pl.BlockSpec((t, D_FF), lambda i: (i, 0))
    return pl.pallas_call(
        body, name="conv_bwd_a", grid=(n,),
        in_specs=[pl.BlockSpec((t, D_MODEL), lambda i: (i, 0)), pl.BlockSpec((D_FF, D_MODEL), lambda i: (0, 0))]
        + _halo_specs(t, s, D_FF) + [tile, pl.BlockSpec((3, D_FF), lambda i: (0, 0)),
                                     pl.BlockSpec((1, D_FF), lambda i: (0, 0))],
        out_specs=[tile, tile, pl.BlockSpec((8, D_FF), lambda i: (0, 0))],
        out_shape=[jax.ShapeDtypeStruct((s, D_FF), BF16), jax.ShapeDtypeStruct((s, D_FF), F32),
                   jax.ShapeDtypeStruct((8, D_FF), F32)],
        compiler_params=_cparams(dimension_semantics=("arbitrary",)),
    )(dr3b, w_down_b, g, g, g, u, conv_w, conv_b)


def _conv_bwd_b(dgc, conv_w, *, t):
    s = dgc.shape[0]
    n = s // t

    def body(d_ref, dp_ref, dn_ref, cw_ref, o_ref):
        i = pl.program_id(0)
        dv = d_ref[...]
        prev_row, next_row = _halo_rows(i, n, dp_ref, dn_ref)
        dm1, dp1 = _shift_rows(dv, prev_row, next_row)
        o_ref[...] = (dp1 * cw_ref[0:1, :] + dv * cw_ref[1:2, :] + dm1 * cw_ref[2:3, :]).astype(BF16)

    return pl.pallas_call(
        body, name="conv_bwd_b", grid=(n,),
        in_specs=_halo_specs(t, s, D_FF) + [pl.BlockSpec((3, D_FF), lambda i: (0, 0))],
        out_specs=pl.BlockSpec((t, D_FF), lambda i: (i, 0)), out_shape=jax.ShapeDtypeStruct((s, D_FF), BF16),
        compiler_params=_cparams(dimension_semantics=("parallel",)),
    )(dgc, dgc, dgc, conv_w)


def _to_residue(a, dil):
    s, w = a.shape
    return a.reshape(s // dil, dil, w).transpose(1, 0, 2)


def _from_residue(a):
    dil, l, w = a.shape
    return a.transpose(1, 0, 2).reshape(dil * l, w)


def _stats_to_lanes(rows):
    dil, hq, l = rows.shape
    return jnp.pad(rows.transpose(2, 0, 1).reshape(dil * l, hq), ((0, 0), (0, LANES - hq)))


def _stats_to_rows(lanes, dil):
    s = lanes.shape[0]
    return lanes[:, :DIL_SLOTS].reshape(s // dil, dil, DIL_SLOTS).transpose(1, 2, 0)


def _rope_angles(positions):
    inv_freq = ROPE_THETA ** (-jnp.arange(0, ROT_DIM, 2, dtype=F32) / ROT_DIM)
    ang = positions.astype(F32)[:, None] * inv_freq
    return jnp.concatenate([jnp.cos(ang), jnp.sin(ang)], axis=1)


class _NoPlan:
    def gather(self, stage):
        return None

    def gathered(self, stage, couts, wb):
        pass

    def exchange(self, stage, grads):
        return None

    def exchanged(self, stage, couts):
        pass


def _local_step(x, mem, positions, target, wb, sp, plan=None, *, t_row=256, t_mm=512, tq_a=128, tq_b=128,
                sub_a=4, sub_b=4):
    s = x.shape[0]
    plan = plan or _NoPlan()
    cs = _rope_angles(positions)
    e_mat = _rope_select_matrix()

    (h0b, za, *zb), couts = _proj_all(x, sp["ln_in_g"], sp["ln_in_b"], wb["w_in"], cs, e_mat, t=t_mm,
                                      comm=plan.gather("proj"))
    plan.gathered("proj", couts, wb)
    sub_a = max(1, min(sub_a, s // tq_a))
    subs_b = [max(1, min(sub_b, s // dil // tq_b)) for dil in DILATIONS]
    out_a, lse_a, couts = _swa_fwd_p(za, qcol=0, kcol=4, vcol=5, hq=WIN_Q_HEADS, hkv=WIN_KV_HEADS, w=WIN_HALF,
                                     tq=tq_a, sub=sub_a, sink=sp["attn_sink"], name="attn_a_fwd",
                                     comm=plan.gather("attn_a"))
    plan.gathered("attn_a", couts, wb)
    o_g, lse_g = [], []
    for gi in range(3):
        o, l, couts = _swa_fwd_p(zb[gi], qcol=0, kcol=1, vcol=2, hq=DIL_SLOTS, hkv=DIL_SLOTS, w=DIL_HALF, tq=tq_b,
                                 sub=subs_b[gi], sink=None, name=f"attn_b{gi}_fwd",
                                 comm=plan.gather(f"attn_b{gi}"))
        plan.gathered(f"attn_b{gi}", couts, wb)
        o_g.append(o)
        lse_g.append(_stats_to_lanes(l))
    mixed_b, out_b, lse_b = _combine_fwd(out_a, o_g, lse_g, sp["g_win"], sp["g_dil"], t=t_row)
    r1, h1b = _mixproj_fwd(mixed_b, wb["w_mix_out"], x, sp["ln_in_g"], sp["ln_in_b"], sp["ln1_g"], sp["ln1_b"],
                           t=t_mm)
    mem_nb, kx, vx = _mem_fwd(mem, sp["mem_ln_g"], sp["mem_ln_b"], wb["w_xk"], wb["w_xv"])
    r2, h2b, qxb, oxb, lse_x = _xattn_fwd(h1b, r1, kx, vx, wb["w_xq"], wb["w_xo"], sp["ln1_g"], sp["ln1_b"],
                                          sp["ln2_g"], sp["ln2_b"], t=t_mm)
    g = _mm(h2b, wb["w_gate"], mode="nt", out_dtype=F32, tm=t_mm, tn=D_FF, name="ff_gate")
    u = _mm(h2b, wb["w_up"], mode="nt", out_dtype=F32, tm=t_mm, tn=D_FF, name="ff_up")
    tb, dr3, dr3b, st3 = _ffn_out(g, u, sp["conv_w"], sp["conv_b"], wb["w_down"], r2, target, sp["ln2_g"],
                                  sp["ln2_b"], sp["ln3_g"], sp["ln3_b"], t=t_row)

    grads = {}
    du, dgc, st_conv = _conv_bwd_a(dr3b, wb["w_down"], g, u, sp["conv_w"], sp["conv_b"], t=t_row)
    tk = min(1024, s)
    grads["w_down"] = _mm(tb, dr3b, mode="tn", out_dtype=BF16, tm=D_FF // 2, tn=D_MODEL, tk=tk, name="dw_down")
    grads["w_up"] = _mm(du, h2b, mode="tn", out_dtype=BF16, tm=D_FF // 2, tn=D_MODEL, tk=tk, name="dw_up")
    (dg, dr2, dr2b, st2), couts = _dh2_ln2(dgc, sp["conv_w"], du, wb["w_gate"], wb["w_up"], dr3, r2, sp["ln2_g"],
                                           t=t_mm, comm=plan.exchange("dh2", grads))
    plan.exchanged("dh2", couts)
    grads["w_gate"] = _mm(dg, h2b, mode="tn", out_dtype=BF16, tm=D_FF // 2, tn=D_MODEL, tk=tk, name="dw_gate")

    (dr1, dr1b, dqxb, dkx, dvx, st1), couts = _xattn_bwd(
        dr2, qxb, oxb, lse_x, kx, vx, wb["w_xq"], wb["w_xo"], r1, sp["ln1_g"], t=t_mm,
        comm=plan.exchange("xattn", grads))
    plan.exchanged("xattn", couts)
    grads["w_xo"] = _mm(oxb, dr2b, mode="tn", out_dtype=BF16, tm=D_MODEL, tn=D_MODEL, tk=tk, name="dw_xo")
    grads["w_xq"] = _mm(h1b, dqxb, mode="tn", out_dtype=BF16, tm=D_MODEL, tn=D_MODEL, tk=tk, name="dw_xq")
    grads["w_xk"], grads["w_xv"], st_mem = _mem_bwd(dkx, dvx, mem, sp["mem_ln_g"], sp["mem_ln_b"],
                                                    wb["w_xk"], wb["w_xv"])

    grads["w_mix_out"] = _mm(mixed_b, dr1b, mode="tn", out_dtype=BF16, tm=D_MODEL, tn=D_MODEL, tk=tk,
                             name="dw_mix")
    dmixed = _mm(dr1b, wb["w_mix_out"], mode="nt", out_dtype=F32, tm=t_mm, tn=D_MODEL, name="dmixed")
    do_a, do_b, dl_a, dl_b, st_mix = _combine_bwd(dmixed, out_a, out_b, sp["g_win"], sp["g_dil"], t=t_row)
    (dqa, dka, dva, dsink), couts = _swa_bwd_p(
        za, do_a, lse_a, _stats_to_rows(dl_a, 1), cs[None], e_mat, qcol=0, kcol=4, vcol=5, hq=WIN_Q_HEADS,
        hkv=WIN_KV_HEADS, w=WIN_HALF, tq=2 * tq_a, sub=max(1, sub_a // 2), sink=sp["attn_sink"], name="attn_a_bwd",
        comm=plan.exchange("attn_a", grads))
    plan.exchanged("attn_a", couts)
    dqs, dks, dvs = [], [], []
    for gi, dil in enumerate(DILATIONS):
        (dq, dk, dv), _ = _swa_bwd_p(
            zb[gi], do_b[gi], _stats_to_rows(lse_b, dil), _stats_to_rows(dl_b, dil),
            _to_residue(cs, dil), e_mat, qcol=0, kcol=1, vcol=2, hq=DIL_SLOTS, hkv=DIL_SLOTS, w=DIL_HALF, tq=tq_b,
            sub=subs_b[gi], sink=None, name=f"attn_b{gi}_bwd")
        dqs.append(dq)
        dks.append(dk)
        dvs.append(dv)
    dz = _assemble_dz(dqa, dka, dva, dqs, dks, dvs, t=t_row)
    grads["w_in"] = _mm(dz, h0b, mode="tn", out_dtype=BF16, tm=IN_WIDTH // 7, tn=D_MODEL, tk=tk, name="dw_in")
    comm = plan.exchange("dh0", grads)
    dh0 = _mm(dz, wb["w_in"], mode="nn", out_dtype=F32, tm=t_mm, tn=D_MODEL, add=dr1, add_scale=ALPHA, name="dh0",
              comm=comm)
    if comm is not None:
        dh0, couts = dh0
        plan.exchanged("dh0", couts)
    grad_x, st0 = _ln_bwd(dh0, x, sp["ln_in_g"], t=t_row, name="ln_in_bwd", want_bf16=False)

    small = {
        "loss": st3[2:3, 0:1],
        "ln_in_g": st0[0:1], "ln_in_b": st0[1:2],
        "attn_sink": dsink[:, 0].reshape(1, WIN_Q_HEADS),
        "g_win": st_mix[0:1], "g_dil": st_mix[1:2],
        "ln1_g": st1[0:1], "ln1_b": st1[1:2],
        "mem_ln_g": st_mem[0:1], "mem_ln_b": st_mem[1:2],
        "ln2_g": st2[0:1], "ln2_b": st2[1:2],
        "conv_w": st_conv[0:3], "conv_b": st_conv[3:4],
        "ln3_g": st3[0:1], "ln3_b": st3[1:2],
    }
    return grad_x, grads, small


class _SiblingSwap:
    def __init__(self, arrays):
        self.inputs = list(arrays)
        n = len(arrays)
        self.out_shape = [jax.ShapeDtypeStruct(a.shape, a.dtype) for a in arrays]
        self.scratch = [pltpu.SemaphoreType.DMA((n,)), pltpu.SemaphoreType.DMA((n,))]

    def _copies(self, src, dst, sems):
        send_sems, recv_sems = sems
        x, y, c, _ = _place()
        return [pltpu.make_async_remote_copy(
            src_ref=src[a], dst_ref=dst[a], send_sem=send_sems.at[a], recv_sem=recv_sems.at[a],
            device_id=(x, y, 1 - c), device_id_type=MESH_IDS) for a in range(len(src))]

    def start(self, src, dst, sems):
        for cp in self._copies(src, dst, sems):
            cp.start()

    def wait(self, src, dst, sems):
        copies = self._copies(src, dst, sems)
        for cp in copies:
            cp.wait_recv()
        for cp in copies:
            cp.wait_send()


class _Both:
    def __init__(self, first, second):
        self.parts = (first, second)
        self.inputs = first.inputs + second.inputs
        self.out_shape = first.out_shape + second.out_shape
        self.scratch = first.scratch + second.scratch

    def _split(self, src, dst, sems):
        a = self.parts[0]
        ni, no, ns = len(a.inputs), len(a.out_shape), len(a.scratch)
        return ((src[:ni], dst[:no], sems[:ns]), (src[ni:], dst[no:], sems[ns:]))

    def start(self, src, dst, sems):
        for part, args in zip(self.parts, self._split(src, dst, sems)):
            part.start(*args)

    def wait(self, src, dst, sems):
        for part, args in zip(self.parts, self._split(src, dst, sems)):
            part.wait(*args)


def _row_tile(rows, cols, itemsize=4, budget=1 << 20):
    best = None
    for t in range(16, rows + 1, 16):
        if rows % t == 0 and t * cols * itemsize <= budget:
            best = t
    return best or rows


def _sum_slots(stack, *, name):
    n, r, c = stack.shape
    t = _row_tile(r, c)

    def body(s_ref, o_ref):
        acc = s_ref[0].astype(F32)
        for q in range(1, n):
            acc = acc + s_ref[q].astype(F32)
        o_ref[...] = acc

    return pl.pallas_call(
        body, name=name, grid=(r // t,), in_specs=[pl.BlockSpec((n, t, c), lambda i: (0, i, 0))],
        out_specs=pl.BlockSpec((t, c), lambda i: (i, 0)), out_shape=jax.ShapeDtypeStruct((r, c), F32),
        compiler_params=_cparams(dimension_semantics=("parallel",)),
    )(stack)


def _adamw(w, m, v, p, q, *, name):
    r, c = w.shape
    t = _row_tile(r, c, budget=1 << 20)

    def body(*refs):
        if q is None:
            w_ref, m_ref, v_ref, p_ref, g_ref, d_ref, nm_ref, nv_ref = refs
            g = p_ref[...]
        else:
            w_ref, m_ref, v_ref, p_ref, q_ref, g_ref, d_ref, nm_ref, nv_ref = refs
            g = p_ref[...] + q_ref[...]
        nm = ADAM_B1 * m_ref[...] + (1.0 - ADAM_B1) * g
        nv = ADAM_B2 * v_ref[...] + (1.0 - ADAM_B2) * (g * g)
        m_hat = nm / (1.0 - ADAM_B1 ** ADAM_STEP)
        v_hat = nv / (1.0 - ADAM_B2 ** ADAM_STEP)
        g_ref[...] = g
        d_ref[...] = -ADAM_LR * (m_hat / (jnp.sqrt(v_hat) + ADAM_EPS) + ADAM_WD * w_ref[...])
        nm_ref[...] = nm
        nv_ref[...] = nv

    tile = pl.BlockSpec((t, c), lambda i: (i, 0))
    args = [w, m, v, p] + ([] if q is None else [q])
    sh = jax.ShapeDtypeStruct((r, c), F32)
    return pl.pallas_call(
        body, name=name, grid=(r // t,), in_specs=[tile] * len(args), out_specs=[tile] * 4, out_shape=[sh] * 4,
        compiler_params=_cparams(dimension_semantics=("parallel",)),
    )(*args)


BIG = ("w_in", "w_mix_out", "w_xq", "w_xk", "w_xv", "w_xo", "w_gate", "w_up", "w_down")
COL_SHARDED = ("w_in", "w_gate", "w_up")
WEIGHTS = ("ln_in_g", "ln_in_b", "w_in", "attn_sink", "g_win", "g_dil", "w_mix_out", "ln1_g", "ln1_b",
           "mem_ln_g", "mem_ln_b", "w_xq", "w_xk", "w_xv", "w_xo", "ln2_g", "ln2_b", "w_gate", "w_up",
           "conv_w", "conv_b", "w_down", "ln3_g", "ln3_b")
SMALL = tuple(k for k in WEIGHTS if k not in BIG)
PACK_COLS = 1024
CONV_SHARD = D_FF // N_CHIPS
CONV_WIDTH_ROWS = 3
SMALL_ROWS = 32


GATHER_STAGES = {"proj": ("w_mix_out", "w_xq", "w_xk", "w_xv", "w_xo"), "attn_a": ("w_gate",),
                 "attn_b0": ("w_up",), "attn_b1": ("w_down",)}
EXCHANGE_STAGES = {"dh2": ("w_down", "w_up"), "xattn": ("w_gate",),
                   "attn_a": ("w_xo", "w_xq", "w_xk", "w_xv", "w_mix_out"), "dh0": ("w_in",)}


def _full_weight(k, g4):
    return g4.reshape(N_CHIPS * g4.shape[1], g4.shape[2])


def _grad_parts(k, gk):
    gk = gk.astype(BF16)
    return gk.reshape(N_CHIPS, gk.shape[0] // N_CHIPS, gk.shape[1])


class _Plan:
    def __init__(self, shards):
        self.shards = shards
        self.recv = {}

    def gather(self, stage):
        names = GATHER_STAGES.get(stage)
        return _ChipGather([self.shards[k] for k in names]) if names else None

    def gathered(self, stage, couts, wb):
        for k, g4 in zip(GATHER_STAGES.get(stage, ()), couts):
            wb[k] = _full_weight(k, g4)

    def exchange(self, stage, grads):
        names = EXCHANGE_STAGES.get(stage)
        return _ChipExchange([_grad_parts(k, grads[k]) for k in names]) if names else None

    def exchanged(self, stage, couts):
        for k, r4 in zip(EXCHANGE_STAGES.get(stage, ()), couts):
            self.recv[k] = r4


def _pack_rows(a):
    r, n = a.shape
    per = -(-n // PACK_COLS)
    return jnp.pad(a, ((0, 0), (0, per * PACK_COLS - n))).reshape(r * per, PACK_COLS)


def _unpack_rows(p, r, n):
    per = -(-n // PACK_COLS)
    return p.reshape(r, per * PACK_COLS)[:, :n]


def _pack(pieces, rows_total):
    cat = jnp.concatenate([_pack_rows(a) for a in pieces], axis=0)
    return jnp.pad(cat, ((0, rows_total - cat.shape[0]), (0, 0)))


def _unpack(p, shapes):
    out, at = [], 0
    for r, n in shapes:
        per = -(-n // PACK_COLS)
        out.append(_unpack_rows(p[at:at + r * per], r, n))
        at += r * per
    return out


def kernel(x, mem, positions, ln_in_g, ln_in_b, w_in, attn_sink, g_win, g_dil, w_mix_out, ln1_g, ln1_b, mem_ln_g, mem_ln_b, w_xq, w_xk, w_xv, w_xo, ln2_g, ln2_b, w_gate, w_up, conv_w, conv_b, w_down, ln3_g, ln3_b, loss_target, m_ln_in_g, m_ln_in_b, m_w_in, m_attn_sink, m_g_win, m_g_dil, m_w_mix_out, m_ln1_g, m_ln1_b, m_mem_ln_g, m_mem_ln_b, m_w_xq, m_w_xk, m_w_xv, m_w_xo, m_ln2_g, m_ln2_b, m_w_gate, m_w_up, m_conv_w, m_conv_b, m_w_down, m_ln3_g, m_ln3_b, v_ln_in_g, v_ln_in_b, v_w_in, v_attn_sink, v_g_win, v_g_dil, v_w_mix_out, v_ln1_g, v_ln1_b, v_mem_ln_g, v_mem_ln_b, v_w_xq, v_w_xk, v_w_xv, v_w_xo, v_ln2_g, v_ln2_b, v_w_gate, v_w_up, v_conv_w, v_conv_b, v_w_down, v_ln3_g, v_ln3_b):
    given = dict(locals())
    shape_of = {k: given[k].shape for k in WEIGHTS}
    as2d = lambda k, a: a.reshape(-1, a.shape[-1]).T if k in COL_SHARDED else a.reshape(-1, a.shape[-1])
    w2 = {k: as2d(k, given[k]) for k in WEIGHTS}
    m2 = {k: as2d(k, given["m_" + k]) for k in WEIGHTS}
    v2 = {k: as2d(k, given["v_" + k]) for k in WEIGHTS}
    chip = 2 * lax.axis_index("x") + lax.axis_index("y")

    plan = _Plan({k: w2[k].astype(BF16) for k in BIG})
    conv_pack = jnp.pad(w2["conv_w"], ((0, 16 - CONV_WIDTH_ROWS), (0, PACK_COLS - CONV_SHARD)))
    g_in, g_conv = _comm_only(_ChipGather([plan.shards["w_in"], conv_pack]), "gather_w_in")
    wb = {"w_in": _full_weight("w_in", g_in)}
    conv_full = g_conv[:, :CONV_WIDTH_ROWS, :CONV_SHARD].transpose(1, 0, 2).reshape(CONV_WIDTH_ROWS, D_FF)
    sp = {k: w2[k] for k in SMALL}
    sp["conv_w"] = conv_full

    grad_x, grads, small = _local_step(x[0], mem[0], positions[0], loss_target[0], wb, sp, plan)

    small_keys = ("loss",) + SMALL
    small_shapes = [small[k].shape for k in small_keys]
    small_pack = _pack([small[k] for k in small_keys], SMALL_ROWS)
    chip_sums = [_sum_slots(plan.recv[k], name=f"sum_chips_{k}") for k in BIG]
    *sibling_sums, small_all = _comm_only(_Both(_SiblingSwap(chip_sums), _ChipExchange([], small_pack)),
                                          "swap_and_small")
    small_sum = _sum_slots(small_all, name="sum_small")
    small_g = dict(zip(small_keys, _unpack(small_sum, small_shapes)))
    loss = small_g["loss"][0, 0]

    res = {}
    for k, p, q in zip(BIG, chip_sums, sibling_sums):
        res[k] = _adamw(w2[k], m2[k], v2[k], p, q, name=f"adamw_{k}")
    small_g["conv_w"] = lax.dynamic_slice_in_dim(small_g["conv_w"], chip * CONV_SHARD, CONV_SHARD, axis=1)
    adam_shapes = [w2[k].shape for k in SMALL]
    packs = [_pack([d[k] for k in SMALL], SMALL_ROWS) for d in (w2, m2, v2, small_g)]
    small_res = [_unpack(o, adam_shapes) for o in _adamw(*packs, None, name="adamw_small")]
    for i, k in enumerate(SMALL):
        res[k] = tuple(o[i] for o in small_res)

    outs = [loss, grad_x[None]]
    for slot in range(4):
        outs += [(res[k][slot].T if k in COL_SHARDED else res[k][slot]).reshape(shape_of[k]) for k in WEIGHTS]
    return tuple(outs)
```

```python
import functools
import math

import jax
import jax.numpy as jnp
from jax import lax
from jax.experimental import pallas as pl
from jax.experimental.pallas import tpu as pltpu

F32 = jnp.float32
BF16 = jnp.bfloat16

D_MODEL = 1024
HEAD_DIM = 64
WIN_Q_HEADS = 8
WIN_KV_HEADS = 2
WIN_HALF = 128
DIL_SLOTS = 8
DILATIONS = (1, 4, 16)
DIL_HALF = 64
ROT_DIM = 16
ROPE_THETA = 500000.0
X_HEADS = 4
X_HEAD_DIM = 256
D_FF = 2816
A_Q = 512
A_KV = 128
A_WIDTH = A_Q + 2 * A_KV
B_QKV = 1536
IN_WIDTH = 5376
ALPHA = 2.0 ** 0.25
LN_EPS = 1e-5
NEG_INF = -1e30
LANES = 128
N_CHIPS = 4
N_DEV = 8

ADAM_LR = 0.001
ADAM_B1 = 0.9
ADAM_B2 = 0.999
ADAM_EPS = 1e-08
ADAM_WD = 0.01
ADAM_STEP = 10

VMEM_LIMIT = 56 * 1024 * 1024


def _cparams(**kw):
    return pltpu.CompilerParams(vmem_limit_bytes=VMEM_LIMIT, **kw)


def _dot(a, b):
    return lax.dot_general(a, b, (((1,), (0,)), ((), ())), preferred_element_type=F32)


def _dot_nt(a, b):
    return lax.dot_general(a, b, (((1,), (1,)), ((), ())), preferred_element_type=F32)


def _dot_tn(a, b):
    return lax.dot_general(a, b, (((0,), (0,)), ((), ())), preferred_element_type=F32)


def _ln(x, g, b):
    mu = jnp.mean(x, axis=-1, keepdims=True)
    xc = x - mu
    var = jnp.mean(xc * xc, axis=-1, keepdims=True)
    return xc * lax.rsqrt(var + LN_EPS) * g + b


def _ln_bwd_math(dy, r, g):
    mu = jnp.mean(r, axis=-1, keepdims=True)
    xc = r - mu
    var = jnp.mean(xc * xc, axis=-1, keepdims=True)
    rstd = lax.rsqrt(var + LN_EPS)
    xhat = xc * rstd
    dxhat = dy * g
    m1 = jnp.mean(dxhat, axis=-1, keepdims=True)
    m2 = jnp.mean(dxhat * xhat, axis=-1, keepdims=True)
    dr = rstd * (dxhat - m1 - xhat * m2)
    return dr, jnp.sum(dy * xhat, axis=0, keepdims=True), jnp.sum(dy, axis=0, keepdims=True)


def _rope(z, ta, tb, tc, sign):
    w = z.shape[1]
    reps = w // LANES
    a = jnp.tile(ta, (1, reps))
    b = jnp.tile(tb, (1, reps))
    c = jnp.tile(tc, (1, reps))
    return z * a + sign * (pltpu.roll(z, w - 8, 1) * b + pltpu.roll(z, 8, 1) * c)


def _shift_rows(x, prev_row, next_row):
    t = x.shape[0]
    row = lax.broadcasted_iota(jnp.int32, x.shape, 0)
    xm1 = jnp.where(row == 0, prev_row, pltpu.roll(x, 1, 0))
    xp1 = jnp.where(row == t - 1, next_row, pltpu.roll(x, t - 1, 0))
    return xm1, xp1


def _rope_tabs(cs, e_mat):
    hi = cs.astype(BF16)
    rest = cs - hi.astype(F32)
    mid = rest.astype(BF16)
    lo = (rest - mid.astype(F32)).astype(BF16)
    tabs = _dot(hi, e_mat) + _dot(mid, e_mat) + _dot(lo, e_mat)
    lane = lax.broadcasted_iota(jnp.int32, (cs.shape[0], LANES), 1)
    ones = jnp.where((lane & (HEAD_DIM - 1)) >= ROT_DIM, 1.0, 0.0)
    return tabs[:, :LANES] + ones, tabs[:, LANES:2 * LANES], tabs[:, 2 * LANES:]


def _rope_select_matrix():
    half = ROT_DIM // 2
    e = [[0.0] * (3 * LANES) for _ in range(ROT_DIM)]
    for lane in range(LANES):
        d = lane % HEAD_DIM
        if d < half:
            e[d][lane] = 1.0
            e[half + d][LANES + lane] = -1.0
        elif d < ROT_DIM:
            e[d - half][lane] = 1.0
            e[d][2 * LANES + lane] = 1.0
    return jnp.array(e, BF16)


def _rope_rows(x, cos_t, sin_t, sign):
    half = ROT_DIM // 2
    parts = []
    for base in (0, HEAD_DIM):
        r1, r2 = x[base:base + half], x[base + half:base + ROT_DIM]
        parts += [r1 * cos_t - sign * (r2 * sin_t), r2 * cos_t + sign * (r1 * sin_t), x[base + ROT_DIM:base + HEAD_DIM]]
    return jnp.concatenate(parts, axis=0)


MESH_IDS = pl.DeviceIdType.MESH
ANY = pl.BlockSpec(memory_space=pl.ANY)


def _place():
    x, y, c = lax.axis_index("x"), lax.axis_index("y"), lax.axis_index("c")
    other_chips = [(1 - x, y), (x, 1 - y), (1 - x, 1 - y)]
    return x, y, c, other_chips


class _ChipGather:
    def __init__(self, shards):
        self.inputs = list(shards)
        n = len(shards)
        self.out_shape = [jax.ShapeDtypeStruct((N_CHIPS,) + a.shape, a.dtype) for a in shards]
        self.scratch = [pltpu.SemaphoreType.DMA((6 * n,)), pltpu.SemaphoreType.DMA((6 * n,)),
                        pltpu.SemaphoreType.DMA((n,))]

    def _copies(self, src, dst, sems):
        send_sems, recv_sems, local_sems = sems
        x, y, c, chips = _place()
        mine = 2 * x + y
        n = len(src)
        local, sends, recvs, passes, pass_recvs = [], [], [], [], []
        for a in range(n):
            half = src[a].shape[0] // 2
            my_rows, other_rows = pl.ds(c * half, half), pl.ds((1 - c) * half, half)
            local.append(pltpu.make_async_copy(src[a], dst[a].at[mine], local_sems.at[a]))
            for j, (px, py) in enumerate(chips):
                k, k2, slot = 3 * a + j, 3 * n + 3 * a + j, 2 * px + py
                sends.append(pltpu.make_async_remote_copy(
                    src_ref=src[a].at[my_rows], dst_ref=dst[a].at[mine, my_rows], send_sem=send_sems.at[k],
                    recv_sem=recv_sems.at[k], device_id=(px, py, c), device_id_type=MESH_IDS))
                recvs.append(pltpu.make_async_remote_copy(
                    src_ref=src[a].at[my_rows], dst_ref=dst[a].at[slot, my_rows], send_sem=send_sems.at[k],
                    recv_sem=recv_sems.at[k], device_id=(px, py, c), device_id_type=MESH_IDS))
                passes.append(pltpu.make_async_remote_copy(
                    src_ref=dst[a].at[slot, my_rows], dst_ref=dst[a].at[slot, my_rows], send_sem=send_sems.at[k2],
                    recv_sem=recv_sems.at[k2], device_id=(x, y, 1 - c), device_id_type=MESH_IDS))
                pass_recvs.append(pltpu.make_async_remote_copy(
                    src_ref=dst[a].at[slot, my_rows], dst_ref=dst[a].at[slot, other_rows],
                    send_sem=send_sems.at[k2], recv_sem=recv_sems.at[k2], device_id=(x, y, 1 - c),
                    device_id_type=MESH_IDS))
        return local, sends, recvs, passes, pass_recvs

    def start(self, src, dst, sems):
        local, sends, _, _, _ = self._copies(src, dst, sems)
        for cp in local + sends:
            cp.start()

    def wait(self, src, dst, sems):
        local, sends, recvs, passes, pass_recvs = self._copies(src, dst, sems)
        for idx, landed in enumerate(recvs):
            landed.wait_recv()
            if passes:
                passes[idx].start()
        for cp in pass_recvs:
            cp.wait_recv()
        for cp in sends + passes:
            cp.wait_send()
        for cp in local:
            cp.wait()


class _ChipExchange:
    def __init__(self, parts, small=None):
        self.inputs = list(parts) + ([small] if small is not None else [])
        self.n = len(parts)
        self.has_small = small is not None
        self.out_shape = [jax.ShapeDtypeStruct(a.shape, a.dtype) for a in parts]
        n_sem, n_loc = 3 * self.n, self.n
        if self.has_small:
            self.out_shape.append(jax.ShapeDtypeStruct((N_DEV,) + small.shape, small.dtype))
            n_sem, n_loc = n_sem + N_DEV - 1, n_loc + 1
        self.scratch = [pltpu.SemaphoreType.DMA((n_sem,)), pltpu.SemaphoreType.DMA((n_sem,)),
                        pltpu.SemaphoreType.DMA((n_loc,))]

    def _copies(self, src, dst, sems):
        send_sems, recv_sems, local_sems = sems
        x, y, c, chips = _place()
        mine = 2 * x + y
        n = self.n
        local, sends, recvs = [], [], []
        for a in range(n):
            local.append(pltpu.make_async_copy(src[a].at[mine], dst[a].at[mine], local_sems.at[a]))
            for j, (px, py) in enumerate(chips):
                k = 3 * a + j
                sends.append(pltpu.make_async_remote_copy(
                    src_ref=src[a].at[2 * px + py], dst_ref=dst[a].at[mine], send_sem=send_sems.at[k],
                    recv_sem=recv_sems.at[k], device_id=(px, py, c), device_id_type=MESH_IDS))
                recvs.append(pltpu.make_async_remote_copy(
                    src_ref=src[a].at[mine], dst_ref=dst[a].at[2 * px + py], send_sem=send_sems.at[k],
                    recv_sem=recv_sems.at[k], device_id=(px, py, c), device_id_type=MESH_IDS))
        if self.has_small:
            me_dev = 4 * x + 2 * y + c
            local.append(pltpu.make_async_copy(src[n], dst[n].at[me_dev], local_sems.at[n]))
            for mask in range(1, N_DEV):
                px, py, pc = x ^ ((mask >> 2) & 1), y ^ ((mask >> 1) & 1), c ^ (mask & 1)
                k = 3 * n + mask - 1
                sends.append(pltpu.make_async_remote_copy(
                    src_ref=src[n], dst_ref=dst[n].at[me_dev], send_sem=send_sems.at[k], recv_sem=recv_sems.at[k],
                    device_id=(px, py, pc), device_id_type=MESH_IDS))
                recvs.append(pltpu.make_async_remote_copy(
                    src_ref=src[n], dst_ref=dst[n].at[4 * px + 2 * py + pc], send_sem=send_sems.at[k],
                    recv_sem=recv_sems.at[k], device_id=(px, py, pc), device_id_type=MESH_IDS))
        return local, sends, recvs, [], []

    start = _ChipGather.start
    wait = _ChipGather.wait


def _pcall(body, *, name, grid, in_specs, out_specs, out_shape, args, scratch_shapes=(), dims=None, comm=None):
    in_specs, out_specs, out_shape = list(in_specs), list(out_specs), list(out_shape)
    scratch_shapes = list(scratch_shapes)
    if comm is None:
        outs = pl.pallas_call(
            body, name=name, grid=grid, in_specs=in_specs, out_specs=out_specs, out_shape=out_shape,
            scratch_shapes=scratch_shapes, compiler_params=_cparams(dimension_semantics=dims),
        )(*args)
        return list(outs), []
    n_in, n_out, n_scr = len(in_specs), len(out_specs), len(scratch_shapes)
    n_cin, n_cout = len(comm.inputs), len(comm.out_shape)

    def wrapped(*refs):
        ins, refs = refs[:n_in], refs[n_in:]
        cins, refs = refs[:n_cin], refs[n_cin:]
        outs, refs = refs[:n_out], refs[n_out:]
        couts, refs = refs[:n_cout], refs[n_cout:]
        scr, csems = refs[:n_scr], refs[n_scr:]
        first = last = None
        for axis, size in enumerate(grid):
            pid = pl.program_id(axis)
            f, l = pid == 0, pid == size - 1
            first = f if first is None else first & f
            last = l if last is None else last & l

        @pl.when(first)
        def _():
            comm.start(cins, couts, csems)

        body(*ins, *outs, *scr)

        @pl.when(last)
        def _():
            comm.wait(cins, couts, csems)

    res = pl.pallas_call(
        wrapped, name=name, grid=grid, in_specs=in_specs + [ANY] * n_cin, out_specs=out_specs + [ANY] * n_cout,
        out_shape=out_shape + list(comm.out_shape), scratch_shapes=scratch_shapes + list(comm.scratch),
        compiler_params=_cparams(dimension_semantics=("arbitrary",) * len(grid)),
    )(*args, *comm.inputs)
    return list(res[:n_out]), list(res[n_out:])


def _comm_only(comm, name):
    def body(*refs):
        n_cin, n_cout = len(comm.inputs), len(comm.out_shape)
        cins, couts, csems = refs[:n_cin], refs[n_cin:n_cin + n_cout], refs[n_cin + n_cout:]
        comm.start(cins, couts, csems)
        comm.wait(cins, couts, csems)

    return list(pl.pallas_call(
        body, name=name, in_specs=[ANY] * len(comm.inputs), out_specs=[ANY] * len(comm.out_shape),
        out_shape=list(comm.out_shape), scratch_shapes=list(comm.scratch),
    )(*comm.inputs))


def _mm(a, b, *, mode, out_dtype, tm, tn, tk=None, add=None, add_scale=1.0, name, comm=None):
    if mode in ("nn", "nt"):
        m, k = a.shape
        n = b.shape[1] if mode == "nn" else b.shape[0]
        assert m % tm == 0 and n % tn == 0
        dot = _dot if mode == "nn" else _dot_nt

        def body(*refs):
            if add is None:
                a_ref, b_ref, o_ref = refs
                o_ref[...] = dot(a_ref[...], b_ref[...]).astype(out_dtype)
            else:
                a_ref, b_ref, c_ref, o_ref = refs
                o_ref[...] = (dot(a_ref[...], b_ref[...]) + add_scale * c_ref[...]).astype(out_dtype)

        b_spec = (pl.BlockSpec((k, tn), lambda i, j: (0, j)) if mode == "nn"
                  else pl.BlockSpec((tn, k), lambda i, j: (j, 0)))
        in_specs = [pl.BlockSpec((tm, k), lambda i, j: (i, 0)), b_spec]
        args = [a, b]
        if add is not None:
            in_specs.append(pl.BlockSpec((tm, tn), lambda i, j: (i, j)))
            args.append(add)
        outs, couts = _pcall(
            body, name=name, grid=(m // tm, n // tn), in_specs=in_specs,
            out_specs=[pl.BlockSpec((tm, tn), lambda i, j: (i, j))],
            out_shape=[jax.ShapeDtypeStruct((m, n), out_dtype)], args=args, dims=("parallel", "parallel"),
            comm=comm)
        return outs[0] if comm is None else (outs[0], couts)
    assert mode == "tn" and add is None and comm is None
    kk, m = a.shape
    n = b.shape[1]
    assert m % tm == 0 and n % tn == 0 and kk % tk == 0
    nk = kk // tk

    def body(a_ref, b_ref, o_ref, acc_ref):
        kstep = pl.program_id(2)

        @pl.when(kstep == 0)
        def _():
            acc_ref[...] = jnp.zeros_like(acc_ref)

        acc_ref[...] += _dot_tn(a_ref[...], b_ref[...])

        @pl.when(kstep == nk - 1)
        def _():
            o_ref[...] = acc_ref[...].astype(out_dtype)

    return pl.pallas_call(
        body, name=name, grid=(m // tm, n // tn, nk),
        in_specs=[pl.BlockSpec((tk, tm), lambda i, j, s: (s, i)), pl.BlockSpec((tk, tn), lambda i, j, s: (s, j))],
        out_specs=pl.BlockSpec((tm, tn), lambda i, j, s: (i, j)),
        out_shape=jax.ShapeDtypeStruct((m, n), out_dtype),
        scratch_shapes=[pltpu.VMEM((tm, tn), F32)],
        compiler_params=_cparams(dimension_semantics=("parallel", "parallel", "arbitrary")),
    )(a, b)


def _mm2_nt(a1, b1, a2, b2, add, *, add_scale, tm, name, comm=None):
    m, k = a1.shape
    n = b1.shape[0]

    def body(a1_ref, b1_ref, a2_ref, b2_ref, c_ref, o_ref):
        o_ref[...] = (_dot_nt(a1_ref[...], b1_ref[...]) + _dot_nt(a2_ref[...], b2_ref[...])
                      + add_scale * c_ref[...])

    a_spec = pl.BlockSpec((tm, k), lambda i: (i, 0))
    b_spec = pl.BlockSpec((n, k), lambda i: (0, 0))
    o_spec = pl.BlockSpec((tm, n), lambda i: (i, 0))
    outs, couts = _pcall(body, name=name, grid=(m // tm,), in_specs=[a_spec, b_spec, a_spec, b_spec, o_spec],
                         out_specs=[o_spec], out_shape=[jax.ShapeDtypeStruct((m, n), F32)],
                         args=[a1, b1, a2, b2, add], dims=("parallel",), comm=comm)
    return outs[0], couts


def _ln_bwd(dy, r, g, *, t, name, want_bf16):
    s = r.shape[0]

    def body(dy_ref, r_ref, g_ref, *outs):
        i = pl.program_id(0)
        dr, dg, db = _ln_bwd_math(dy_ref[...], r_ref[...], g_ref[...])
        outs[0][...] = dr
        if want_bf16:
            outs[1][...] = dr.astype(BF16)
        st_ref = outs[-1]

        @pl.when(i == 0)
        def _():
            st_ref[...] = jnp.zeros_like(st_ref)

        st_ref[0:1, :] += dg
        st_ref[1:2, :] += db

    tile = pl.BlockSpec((t, D_MODEL), lambda i: (i, 0))
    out_specs = [tile] + ([tile] if want_bf16 else []) + [pl.BlockSpec((8, D_MODEL), lambda i: (0, 0))]
    out_shape = ([jax.ShapeDtypeStruct((s, D_MODEL), F32)]
                 + ([jax.ShapeDtypeStruct((s, D_MODEL), BF16)] if want_bf16 else [])
                 + [jax.ShapeDtypeStruct((8, D_MODEL), F32)])
    return pl.pallas_call(
        body, name=name, grid=(s // t,),
        in_specs=[tile, tile, pl.BlockSpec((1, D_MODEL), lambda i: (0, 0))],
        out_specs=out_specs, out_shape=out_shape,
        compiler_params=_cparams(dimension_semantics=("arbitrary",)),
    )(dy, r, g)


PROJ_COLS = 256


def _proj_segments():
    wd = DIL_SLOTS * HEAD_DIM
    segs = [(1, [(0, 1), (PROJ_COLS, 1), (2 * PROJ_COLS, 2)])]
    for gi, dil in enumerate(DILATIONS):
        blocks = []
        for part, kind in enumerate((1, 1, 0)):
            col = A_WIDTH + part * B_QKV + gi * wd
            blocks += [(col, kind), (col + PROJ_COLS, kind)]
        segs.append((dil, blocks))
    return segs


PROJ_SEGMENTS = _proj_segments()


def _proj_all(x, g, b, w_t, cs, e_mat, *, t, comm=None):
    s = x.shape[0]
    cb = PROJ_COLS
    halves = cb // LANES

    def body(x_ref, g_ref, b_ref, w_ref, cs_ref, e_ref, h_ref, *rest):
        z_refs, scr = rest[:-1], rest[-1]
        h = _ln(x_ref[...], g_ref[...], b_ref[...]).astype(BF16)
        h_ref[...] = h
        ta, tb, tc = (jnp.tile(tab, (1, halves)) for tab in _rope_tabs(cs_ref[...], e_ref[...]))
        lane = lax.broadcasted_iota(jnp.int32, (t, cb), 1)
        slot = 0
        for z_ref, (dil, blocks) in zip(z_refs, PROJ_SEGMENTS):
            for jb, (col, kind) in enumerate(blocks):
                acc = _dot_nt(h, w_ref[col:col + cb, :])
                if kind:
                    z = acc * ta + (pltpu.roll(acc, cb - 8, 1) * tb + pltpu.roll(acc, 8, 1) * tc)
                    if kind == 2:
                        z = jnp.where(lane < LANES, z, acc)
                else:
                    z = acc
                if dil == 1:
                    z_ref[0, :, cb * jb:cb * (jb + 1)] = z.astype(BF16)
                    continue
                for half in range(halves):
                    scr[slot, half] = z[:, half * LANES:(half + 1) * LANES]
                for c in range(dil):
                    for half in range(halves):
                        rows = scr[slot, half, pl.ds(c, t // dil, stride=dil), :]
                        z_ref[c, :, cb * jb + half * LANES:cb * jb + (half + 1) * LANES] = rows.astype(BF16)
                slot = 1 - slot

    row = pl.BlockSpec((1, D_MODEL), lambda i: (0, 0))
    widths = [cb * len(blocks) for _, blocks in PROJ_SEGMENTS]
    dils = [dil for dil, _ in PROJ_SEGMENTS]
    outs, couts = _pcall(
        body, name="proj_all", grid=(s // t,),
        in_specs=[pl.BlockSpec((t, D_MODEL), lambda i: (i, 0)), row, row,
                  pl.BlockSpec((IN_WIDTH, D_MODEL), lambda i: (0, 0), pipeline_mode=pl.Buffered(1)),
                  pl.BlockSpec((t, ROT_DIM), lambda i: (i, 0)), pl.BlockSpec((ROT_DIM, 3 * LANES), lambda i: (0, 0))],
        out_specs=[pl.BlockSpec((t, D_MODEL), lambda i: (i, 0))]
        + [pl.BlockSpec((dil, t // dil, wd), lambda i: (0, i, 0)) for dil, wd in zip(dils, widths)],
        out_shape=[jax.ShapeDtypeStruct((s, D_MODEL), BF16)]
        + [jax.ShapeDtypeStruct((dil, s // dil, wd), BF16) for dil, wd in zip(dils, widths)],
        args=[x, g, b, w_t, cs, e_mat], scratch_shapes=[pltpu.VMEM((2, halves, t, LANES), F32)],
        dims=("parallel",), comm=comm)
    return outs, couts


def _window_mask(i, tq, w, seq_len):
    tk = tq + 2 * w
    qpos = i * tq + lax.broadcasted_iota(jnp.int32, (tq, tk), 0)
    kpos = i * tq - w + lax.broadcasted_iota(jnp.int32, (tq, tk), 1)
    return (jnp.abs(qpos - kpos) <= w) & (kpos >= 0) & (kpos < seq_len)


def _swa_specs(tq, hq, hkv, n, qcol, kcol, vcol):
    qw, kw = hq * HEAD_DIM, hkv * HEAD_DIM
    cur = lambda s, i: jnp.minimum(i, n - 1)
    prv = lambda s, i: jnp.maximum(jnp.minimum(i, n - 1) - 1, 0)
    nxt = lambda s, i: jnp.minimum(i + 1, n - 1)
    q_spec = pl.BlockSpec((None, tq, qw), lambda s, i: (s, cur(s, i), qcol))
    kv_specs = [pl.BlockSpec((None, tq, kw), (lambda s, i, f=f, c=c: (s, f(s, i), c)))
                for c in (kcol, vcol) for f in (prv, cur, nxt)]
    return q_spec, kv_specs, cur, prv


def _swa_fwd(qkv, *, qcol, kcol, vcol, hq, hkv, w, tq, sink, name, comm=None):
    nseq, seq_len, _ = qkv.shape
    n = seq_len // tq
    rep = hq // hkv
    q_spec, kv_specs, _, _ = _swa_specs(tq, hq, hkv, n, qcol, kcol, vcol)

    def body(*refs):
        if sink is not None:
            sink_ref, refs = refs[0], refs[1:]
        q_ref, kp_ref, kc_ref, kn_ref, vp_ref, vc_ref, vn_ref, o_ref, lse_ref = refs
        i = pl.program_id(1)
        mask = _window_mask(i, tq, w, seq_len)
        lane = lax.broadcasted_iota(jnp.int32, (tq, LANES), 1)
        lse_acc = jnp.zeros((tq, LANES), F32)
        for g in range(hkv):
            cs = slice(g * HEAD_DIM, (g + 1) * HEAD_DIM)
            kcat = jnp.concatenate([kp_ref[tq - w:, cs], kc_ref[:, cs], kn_ref[:w, cs]], axis=0)
            vcat = jnp.concatenate([vp_ref[tq - w:, cs], vc_ref[:, cs], vn_ref[:w, cs]], axis=0)
            for r in range(rep):
                h = g * rep + r
                hs = slice(h * HEAD_DIM, (h + 1) * HEAD_DIM)
                qh = q_ref[:, hs] * 0.125
                sc = jnp.where(mask, _dot_nt(qh, kcat), NEG_INF)
                m = jnp.max(sc, axis=1, keepdims=True)
                if sink is not None:
                    m = jnp.maximum(m, sink_ref[0, h])
                p = jnp.exp(sc - m)
                den = jnp.sum(p, axis=1, keepdims=True)
                if sink is not None:
                    den = den + jnp.exp(sink_ref[0, h] - m)
                o_ref[:, hs] = _dot(p.astype(BF16), vcat) / den
                lse_acc = jnp.where(lane == h, m + jnp.log(den), lse_acc)
        lse_ref[...] = lse_acc

    in_specs = [q_spec] + kv_specs
    args = [qkv] * 7
    if sink is not None:
        in_specs = [pl.BlockSpec(memory_space=pltpu.SMEM)] + in_specs
        args = [sink] + args
    (o, lse), couts = _pcall(
        body, name=name, grid=(nseq, n), in_specs=in_specs,
        out_specs=[pl.BlockSpec((None, tq, hq * HEAD_DIM), lambda s, i: (s, i, 0)),
                   pl.BlockSpec((None, tq, LANES), lambda s, i: (s, i, 0))],
        out_shape=[jax.ShapeDtypeStruct((nseq, seq_len, hq * HEAD_DIM), F32),
                   jax.ShapeDtypeStruct((nseq, seq_len, LANES), F32)],
        args=args, dims=("parallel", "parallel"), comm=comm)
    return o, lse, couts


def _swa_bwd(qkv, do, lse, delta, cs, e_mat, *, qcol, kcol, vcol, hq, hkv, w, tq, sink, name, comm=None):
    nseq, seq_len, _ = qkv.shape
    n = seq_len // tq
    rep = hq // hkv
    qw, kw = hq * HEAD_DIM, hkv * HEAD_DIM
    tk = tq + 2 * w
    q_spec, kv_specs, cur, prv = _swa_specs(tq, hq, hkv, n, qcol, kcol, vcol)

    def body(*refs):
        if sink is not None:
            sink_ref, refs = refs[0], refs[1:]
        (q_ref, kp_ref, kc_ref, kn_ref, vp_ref, vc_ref, vn_ref, do_ref, lse_ref, dl_ref,
         cs_c, cs_p, e_ref) = refs[:13]
        outs = refs[13:]
        if sink is not None:
            dq_ref, dk_ref, dv_ref, dsink_ref, dk_acc, dv_acc = outs
        else:
            dq_ref, dk_ref, dv_ref, dk_acc, dv_acc = outs
        s_id = pl.program_id(0)
        i = pl.program_id(1)
        slot_p, slot_c, slot_n = (i + 2) % 3, i % 3, (i + 1) % 3

        if sink is not None:
            @pl.when((s_id == 0) & (i == 0))
            def _():
                dsink_ref[...] = jnp.zeros_like(dsink_ref)

        @pl.when(i < n)
        def _():
            mask = _window_mask(i, tq, w, seq_len)
            dk_acc[slot_n] = jnp.zeros((tq, kw), F32)
            dv_acc[slot_n] = jnp.zeros((tq, kw), F32)

            @pl.when(i == 0)
            def _():
                dk_acc[slot_c] = jnp.zeros((tq, kw), F32)
                dv_acc[slot_c] = jnp.zeros((tq, kw), F32)

            dq_parts, dk_parts, dv_parts = [], [], []
            for g in range(hkv):
                cs = slice(g * HEAD_DIM, (g + 1) * HEAD_DIM)
                kcat = jnp.concatenate([kp_ref[tq - w:, cs], kc_ref[:, cs], kn_ref[:w, cs]], axis=0)
                vcat = jnp.concatenate([vp_ref[tq - w:, cs], vc_ref[:, cs], vn_ref[:w, cs]], axis=0)
                dkc = jnp.zeros((tk, HEAD_DIM), F32)
                dvc = jnp.zeros((tk, HEAD_DIM), F32)
                for r in range(rep):
                    h = g * rep + r
                    hs = slice(h * HEAD_DIM, (h + 1) * HEAD_DIM)
                    qh = q_ref[:, hs] * 0.125
                    sc = jnp.where(mask, _dot_nt(qh, kcat), NEG_INF)
                    lse_h = lse_ref[:, h:h + 1]
                    dl_h = dl_ref[:, h:h + 1]
                    p = jnp.exp(sc - lse_h)
                    doh = do_ref[:, hs]
                    dp = _dot_nt(doh, vcat)
                    dsb = (p * (dp - dl_h)).astype(BF16)
                    dq_parts.append(_dot(dsb, kcat) * 0.125)
                    dkc = dkc + _dot_tn(dsb, qh)
                    dvc = dvc + _dot_tn(p.astype(BF16), doh)
                    if sink is not None:
                        ds_sink = -jnp.sum(jnp.exp(sink_ref[0, h] - lse_h) * dl_h)
                        dsink_ref[h:h + 1, :] += jnp.full((1, LANES), ds_sink, F32)
                dk_parts.append(dkc)
                dv_parts.append(dvc)
            dq = jnp.concatenate(dq_parts, axis=1)
            dq_ref[...] = _rope(dq, *_rope_tabs(cs_c[...], e_ref[...]), -1.0).astype(BF16)
            dk_all = jnp.concatenate(dk_parts, axis=1)
            dv_all = jnp.concatenate(dv_parts, axis=1)

            @pl.when(i > 0)
            def _():
                dk_acc[slot_p, tq - w:, :] += dk_all[:w]
                dv_acc[slot_p, tq - w:, :] += dv_all[:w]

            dk_acc[slot_c] += dk_all[w:w + tq]
            dv_acc[slot_c] += dv_all[w:w + tq]
            dk_acc[slot_n, :w, :] += dk_all[w + tq:]
            dv_acc[slot_n, :w, :] += dv_all[w + tq:]

        @pl.when(i >= 1)
        def _():
            dk_ref[...] = _rope(dk_acc[slot_p], *_rope_tabs(cs_p[...], e_ref[...]), -1.0).astype(BF16)
            dv_ref[...] = dv_acc[slot_p].astype(BF16)

    row_c = lambda width: pl.BlockSpec((None, tq, width), lambda s, i: (s, cur(s, i), 0))
    row_p = lambda width: pl.BlockSpec((None, tq, width), lambda s, i: (s, jnp.maximum(i - 1, 0), 0))
    in_specs = ([q_spec] + kv_specs + [row_c(qw), row_c(LANES), row_c(LANES), row_c(ROT_DIM), row_p(ROT_DIM),
                                       pl.BlockSpec((ROT_DIM, 3 * LANES), lambda s, i: (0, 0))])
    args = [qkv] * 7 + [do, lse, delta, cs, cs, e_mat]
    out_specs = [row_c(qw), row_p(kw), row_p(kw)]
    out_shape = [jax.ShapeDtypeStruct((nseq, seq_len, qw), BF16),
                 jax.ShapeDtypeStruct((nseq, seq_len, kw), BF16),
                 jax.ShapeDtypeStruct((nseq, seq_len, kw), BF16)]
    if sink is not None:
        in_specs = [pl.BlockSpec(memory_space=pltpu.SMEM)] + in_specs
        args = [sink] + args
        out_specs.append(pl.BlockSpec((8, LANES), lambda s, i: (0, 0)))
        out_shape.append(jax.ShapeDtypeStruct((8, LANES), F32))
    return _pcall(
        body, name=name, grid=(nseq, n + 1), in_specs=in_specs, out_specs=out_specs, out_shape=out_shape,
        scratch_shapes=[pltpu.VMEM((3, tq, kw), F32), pltpu.VMEM((3, tq, kw), F32)], args=args,
        dims=("arbitrary", "arbitrary"), comm=comm)


PAIR = 2 * HEAD_DIM


def _window_mask_t(i, tq, w, seq_len):
    tk = tq + 2 * w
    kpos = i * tq - w + lax.broadcasted_iota(jnp.int32, (tk, tq), 0)
    qpos = i * tq + lax.broadcasted_iota(jnp.int32, (tk, tq), 1)
    return (jnp.abs(qpos - kpos) <= w) & (kpos >= 0) & (kpos < seq_len)


def _place_head(x2, src_pos, dst_pos):
    hi = lax.broadcasted_iota(jnp.int32, x2.shape, 1) >= HEAD_DIM
    src = x2 if src_pos == dst_pos else pltpu.roll(x2, HEAD_DIM, 1)
    return jnp.where(hi == (dst_pos == 1), src, jnp.zeros_like(src))


def _swa_fwd_t(qkv, *, qcol, kcol, vcol, hq, hkv, w, tq, sink, name, comm=None):
    nseq, seq_len, _ = qkv.shape
    n = seq_len // tq
    rep = hq // hkv
    q_spec, kv_specs, _, _ = _swa_specs(tq, hq, hkv, n, qcol, kcol, vcol)

    def body(*refs):
        if sink is not None:
            sink_ref, refs = refs[0], refs[1:]
        q_ref, kp_ref, kc_ref, kn_ref, vp_ref, vc_ref, vn_ref, o_ref, lse_ref = refs
        i = pl.program_id(1)
        mask_t = _window_mask_t(i, tq, w, seq_len)
        o_t = [None] * (hq // 2)
        lse_rows = [None] * hq
        for a in range(hkv // 2):
            ls = slice(a * PAIR, (a + 1) * PAIR)
            kcat = jnp.concatenate([kp_ref[tq - w:, ls], kc_ref[:, ls], kn_ref[:w, ls]], axis=0) * 0.125
            vcat = jnp.concatenate([vp_ref[tq - w:, ls], vc_ref[:, ls], vn_ref[:w, ls]], axis=0)
            for e in range(2):
                g = 2 * a + e
                placed = {}
                for r in range(rep):
                    h = g * rep + r
                    qp, pos = h // 2, h % 2
                    if pos not in placed:
                        placed[pos] = (_place_head(kcat, e, pos), _place_head(vcat, e, pos))
                    k_g, v_g = placed[pos]
                    s_t = jnp.where(mask_t, _dot_nt(k_g, q_ref[:, qp * PAIR:(qp + 1) * PAIR]), NEG_INF)
                    m = jnp.max(s_t, axis=0, keepdims=True)
                    if sink is not None:
                        m = jnp.maximum(m, sink_ref[0, h])
                    p_t = jnp.exp(s_t - m)
                    den = jnp.sum(p_t, axis=0, keepdims=True)
                    if sink is not None:
                        den = den + jnp.exp(sink_ref[0, h] - m)
                    part = _dot_tn(v_g, p_t.astype(BF16)) / den
                    o_t[qp] = part if o_t[qp] is None else o_t[qp] + part
                    lse_rows[h] = m + jnp.log(den)
        o_ref[...] = jnp.concatenate(o_t, axis=0).T
        lse_ref[...] = jnp.concatenate(lse_rows, axis=0)

    in_specs = [q_spec] + kv_specs
    args = [qkv] * 7
    if sink is not None:
        in_specs = [pl.BlockSpec(memory_space=pltpu.SMEM)] + in_specs
        args = [sink] + args
    (o, lse), couts = _pcall(
        body, name=name, grid=(nseq, n), in_specs=in_specs,
        out_specs=[pl.BlockSpec((None, tq, hq * HEAD_DIM), lambda s, i: (s, i, 0)),
                   pl.BlockSpec((None, hq, tq), lambda s, i: (s, 0, i))],
        out_shape=[jax.ShapeDtypeStruct((nseq, seq_len, hq * HEAD_DIM), F32),
                   jax.ShapeDtypeStruct((nseq, hq, seq_len), F32)],
        args=args, dims=("parallel", "parallel"), comm=comm)
    return o, lse, couts


def _swa_bwd_t(qkv, do, lse, delta, cs, e_mat, *, qcol, kcol, vcol, hq, hkv, w, tq, sink, name, comm=None):
    nseq, seq_len, _ = qkv.shape
    n = seq_len // tq
    rep = hq // hkv
    qw, kw = hq * HEAD_DIM, hkv * HEAD_DIM
    tk = tq + 2 * w
    q_spec, kv_specs, cur, prv = _swa_specs(tq, hq, hkv, n, qcol, kcol, vcol)

    def body(*refs):
        if sink is not None:
            sink_ref, refs = refs[0], refs[1:]
        (q_ref, kp_ref, kc_ref, kn_ref, vp_ref, vc_ref, vn_ref, do_ref, lse_ref, dl_ref,
         cs_c, cs_p, e_ref) = refs[:13]
        outs = refs[13:]
        if sink is not None:
            dq_ref, dk_ref, dv_ref, dsink_ref, dk_acc, dv_acc = outs
        else:
            dq_ref, dk_ref, dv_ref, dk_acc, dv_acc = outs
        s_id = pl.program_id(0)
        i = pl.program_id(1)
        slot_p, slot_c, slot_n = (i + 2) % 3, i % 3, (i + 1) % 3

        if sink is not None:
            @pl.when((s_id == 0) & (i == 0))
            def _():
                dsink_ref[...] = jnp.zeros_like(dsink_ref)

        @pl.when(i < n)
        def _():
            mask_t = _window_mask_t(i, tq, w, seq_len)
            dk_acc[slot_n] = jnp.zeros((tq, kw), F32)
            dv_acc[slot_n] = jnp.zeros((tq, kw), F32)

            @pl.when(i == 0)
            def _():
                dk_acc[slot_c] = jnp.zeros((tq, kw), F32)
                dv_acc[slot_c] = jnp.zeros((tq, kw), F32)

            dq_t = [None] * (hq // 2)
            dk_pairs, dv_pairs = [], []
            for a in range(hkv // 2):
                ls = slice(a * PAIR, (a + 1) * PAIR)
                kcat = jnp.concatenate([kp_ref[tq - w:, ls], kc_ref[:, ls], kn_ref[:w, ls]], axis=0) * 0.125
                vcat = jnp.concatenate([vp_ref[tq - w:, ls], vc_ref[:, ls], vn_ref[:w, ls]], axis=0)
                dk2 = jnp.zeros((tk, PAIR), F32)
                dv2 = jnp.zeros((tk, PAIR), F32)
                for e in range(2):
                    g = 2 * a + e
                    placed = {}
                    for r in range(rep):
                        h = g * rep + r
                        qp, pos = h // 2, h % 2
                        if pos not in placed:
                            placed[pos] = (_place_head(kcat, e, pos), _place_head(vcat, e, pos))
                        k_g, v_g = placed[pos]
                        q2 = q_ref[:, qp * PAIR:(qp + 1) * PAIR]
                        do2 = do_ref[:, qp * PAIR:(qp + 1) * PAIR]
                        lse_h = lse_ref[h:h + 1, :]
                        dl_h = dl_ref[h:h + 1, :]
                        p_t = jnp.exp(jnp.where(mask_t, _dot_nt(k_g, q2), NEG_INF) - lse_h)
                        dp_t = _dot_nt(v_g, do2)
                        dsb = (p_t * (dp_t - dl_h)).astype(BF16)
                        part = _dot_tn(k_g, dsb)
                        dq_t[qp] = part if dq_t[qp] is None else dq_t[qp] + part
                        dk2 = dk2 + _dot(dsb, _place_head(q2, pos, e) * 0.125)
                        dv2 = dv2 + _dot(p_t.astype(BF16), _place_head(do2, pos, e))
                        if sink is not None:
                            ds_sink = -jnp.sum(jnp.exp(sink_ref[0, h] - lse_h) * dl_h)
                            dsink_ref[h:h + 1, :] += jnp.full((1, LANES), ds_sink, F32)
                dk_pairs.append(dk2)
                dv_pairs.append(dv2)
            dq = jnp.concatenate(dq_t, axis=0).T
            dq_ref[...] = _rope(dq, *_rope_tabs(cs_c[...], e_ref[...]), -1.0).astype(BF16)
            dk_all = dk_pairs[0] if len(dk_pairs) == 1 else jnp.concatenate(dk_pairs, axis=1)
            dv_all = dv_pairs[0] if len(dv_pairs) == 1 else jnp.concatenate(dv_pairs, axis=1)

            @pl.when(i > 0)
            def _():
                dk_acc[slot_p, tq - w:, :] += dk_all[:w]
                dv_acc[slot_p, tq - w:, :] += dv_all[:w]

            dk_acc[slot_c] += dk_all[w:w + tq]
            dv_acc[slot_c] += dv_all[w:w + tq]
            dk_acc[slot_n, :w, :] += dk_all[w + tq:]
            dv_acc[slot_n, :w, :] += dv_all[w + tq:]

        @pl.when(i >= 1)
        def _():
            dk_ref[...] = _rope(dk_acc[slot_p], *_rope_tabs(cs_p[...], e_ref[...]), -1.0).astype(BF16)
            dv_ref[...] = dv_acc[slot_p].astype(BF16)

    row_c = lambda width: pl.BlockSpec((None, tq, width), lambda s, i: (s, cur(s, i), 0))
    row_p = lambda width: pl.BlockSpec((None, tq, width), lambda s, i: (s, jnp.maximum(i - 1, 0), 0))
    stat = pl.BlockSpec((None, hq, tq), lambda s, i: (s, 0, cur(s, i)))
    in_specs = ([q_spec] + kv_specs + [row_c(qw), stat, stat, row_c(ROT_DIM), row_p(ROT_DIM),
                                       pl.BlockSpec((ROT_DIM, 3 * LANES), lambda s, i: (0, 0))])
    args = [qkv] * 7 + [do, lse, delta, cs, cs, e_mat]
    out_specs = [row_c(qw), row_p(kw), row_p(kw)]
    out_shape = [jax.ShapeDtypeStruct((nseq, seq_len, qw), BF16),
                 jax.ShapeDtypeStruct((nseq, seq_len, kw), BF16),
                 jax.ShapeDtypeStruct((nseq, seq_len, kw), BF16)]
    if sink is not None:
        in_specs = [pl.BlockSpec(memory_space=pltpu.SMEM)] + in_specs
        args = [sink] + args
        out_specs.append(pl.BlockSpec((8, LANES), lambda s, i: (0, 0)))
        out_shape.append(jax.ShapeDtypeStruct((8, LANES), F32))
    return _pcall(
        body, name=name, grid=(nseq, n + 1), in_specs=in_specs, out_specs=out_specs, out_shape=out_shape,
        scratch_shapes=[pltpu.VMEM((3, tq, kw), F32), pltpu.VMEM((3, tq, kw), F32)], args=args,
        dims=("arbitrary", "arbitrary"), comm=comm)


def _band_mask_t(row0, tq, w, seq_len):
    tk = tq + 2 * w
    kk = lax.broadcasted_iota(jnp.int32, (tk, tq), 0)
    qq = lax.broadcasted_iota(jnp.int32, (tk, tq), 1)
    kpos = row0 - w + kk
    return (jnp.abs(qq + w - kk) <= w) & (kpos >= 0) & (kpos < seq_len)


def _halo_kv_specs(t, w, hkv, n, seq_len, kcol, vcol):
    kw = hkv * HEAD_DIM
    per, last = t // w, seq_len // w - 1
    cur = lambda s, i: jnp.minimum(i, n - 1)
    specs = []
    for c in (kcol, vcol):
        specs += [pl.BlockSpec((None, w, kw), lambda s, i, c=c: (s, jnp.maximum(cur(s, i) * per - 1, 0), c)),
                  pl.BlockSpec((None, t, kw), lambda s, i, c=c: (s, cur(s, i), c)),
                  pl.BlockSpec((None, w, kw), lambda s, i, c=c: (s, jnp.minimum((cur(s, i) + 1) * per, last), c))]
    return specs, cur


def _swa_fwd_s(qkv, *, qcol, kcol, vcol, hq, hkv, w, tq, sub, sink, name, comm=None):
    nseq, seq_len, _ = qkv.shape
    t = tq * sub
    n = seq_len // t
    rep = hq // hkv
    tk = tq + 2 * w
    kv_specs, cur = _halo_kv_specs(t, w, hkv, n, seq_len, kcol, vcol)

    def body(*refs):
        if sink is not None:
            sink_ref, refs = refs[0], refs[1:]
        q_ref, kp_ref, kc_ref, kn_ref, vp_ref, vc_ref, vn_ref, o_ref, lse_ref = refs
        i = pl.program_id(1)
        kfull, vfull = [], []
        for a in range(hkv // 2):
            ls = slice(a * PAIR, (a + 1) * PAIR)
            kfull.append(jnp.concatenate([kp_ref[:, ls], kc_ref[:, ls], kn_ref[:, ls]], axis=0) * 0.125)
            vfull.append(jnp.concatenate([vp_ref[:, ls], vc_ref[:, ls], vn_ref[:, ls]], axis=0))
        for jj in range(sub):
            rows = slice(jj * tq, (jj + 1) * tq)
            mask_t = _band_mask_t(i * t + jj * tq, tq, w, seq_len)
            o_t = [None] * (hq // 2)
            lse_rows = [None] * hq
            for a in range(hkv // 2):
                kcat = kfull[a][jj * tq:jj * tq + tk]
                vcat = vfull[a][jj * tq:jj * tq + tk]
                for e in range(2):
                    g = 2 * a + e
                    placed = {}
                    for r in range(rep):
                        h = g * rep + r
                        qp, pos = h // 2, h % 2
                        if pos not in placed:
                            placed[pos] = (_place_head(kcat, e, pos), _place_head(vcat, e, pos))
                        k_g, v_g = placed[pos]
                        s_t = jnp.where(mask_t, _dot_nt(k_g, q_ref[rows, qp * PAIR:(qp + 1) * PAIR]), NEG_INF)
                        m = jnp.max(s_t, axis=0, keepdims=True)
                        if sink is not None:
                            m = jnp.maximum(m, sink_ref[0, h])
                        p_t = jnp.exp(s_t - m)
                        den = jnp.sum(p_t, axis=0, keepdims=True)
                        if sink is not None:
                            den = den + jnp.exp(sink_ref[0, h] - m)
                        part = _dot_tn(v_g, p_t.astype(BF16)) / den
                        o_t[qp] = part if o_t[qp] is None else o_t[qp] + part
                        lse_rows[h] = m + jnp.log(den)
            o_ref[rows, :] = jnp.concatenate(o_t, axis=0).T
            lse_ref[:, rows] = jnp.concatenate(lse_rows, axis=0)

    in_specs = [pl.BlockSpec((None, t, hq * HEAD_DIM), lambda s, i: (s, i, qcol))] + kv_specs
    args = [qkv] * 7
    if sink is not None:
        in_specs = [pl.BlockSpec(memory_space=pltpu.SMEM)] + in_specs
        args = [sink] + args
    (o, lse), couts = _pcall(
        body, name=name, grid=(nseq, n), in_specs=in_specs,
        out_specs=[pl.BlockSpec((None, t, hq * HEAD_DIM), lambda s, i: (s, i, 0)),
                   pl.BlockSpec((None, hq, t), lambda s, i: (s, 0, i))],
        out_shape=[jax.ShapeDtypeStruct((nseq, seq_len, hq * HEAD_DIM), F32),
                   jax.ShapeDtypeStruct((nseq, hq, seq_len), F32)],
        args=args, dims=("parallel", "parallel"), comm=comm)
    return o, lse, couts


def _swa_bwd_s(qkv, do, lse, delta, cs, e_mat, *, qcol, kcol, vcol, hq, hkv, w, tq, sub, sink, name, comm=None):
    nseq, seq_len, _ = qkv.shape
    t = tq * sub
    n = seq_len // t
    rep = hq // hkv
    qw, kw = hq * HEAD_DIM, hkv * HEAD_DIM
    tk = tq + 2 * w
    kv_specs, cur = _halo_kv_specs(t, w, hkv, n, seq_len, kcol, vcol)

    def body(*refs):
        if sink is not None:
            sink_ref, refs = refs[0], refs[1:]
        (q_ref, kp_ref, kc_ref, kn_ref, vp_ref, vc_ref, vn_ref, do_ref, lse_ref, dl_ref,
         cs_c, cs_p, e_ref) = refs[:13]
        outs = refs[13:]
        if sink is not None:
            dq_ref, dk_ref, dv_ref, dsink_ref, dk_acc, dv_acc, dk_win, dv_win = outs
        else:
            dq_ref, dk_ref, dv_ref, dk_acc, dv_acc, dk_win, dv_win = outs
        s_id = pl.program_id(0)
        i = pl.program_id(1)
        slot_p, slot_c, slot_n = (i + 2) % 3, i % 3, (i + 1) % 3

        if sink is not None:
            @pl.when((s_id == 0) & (i == 0))
            def _():
                dsink_ref[...] = jnp.zeros_like(dsink_ref)

        @pl.when(i < n)
        def _():
            dk_win[...] = jnp.zeros_like(dk_win)
            dv_win[...] = jnp.zeros_like(dv_win)
            kfull, vfull = [], []
            for a in range(hkv // 2):
                ls = slice(a * PAIR, (a + 1) * PAIR)
                kfull.append(jnp.concatenate([kp_ref[:, ls], kc_ref[:, ls], kn_ref[:, ls]], axis=0) * 0.125)
                vfull.append(jnp.concatenate([vp_ref[:, ls], vc_ref[:, ls], vn_ref[:, ls]], axis=0))
            for jj in range(sub):
                rows = slice(jj * tq, (jj + 1) * tq)
                krows = slice(jj * tq, jj * tq + tk)
                mask_t = _band_mask_t(i * t + jj * tq, tq, w, seq_len)
                dq_t = [None] * (hq // 2)
                for a in range(hkv // 2):
                    ls = slice(a * PAIR, (a + 1) * PAIR)
                    kcat, vcat = kfull[a][krows], vfull[a][krows]
                    dk2 = jnp.zeros((tk, PAIR), F32)
                    dv2 = jnp.zeros((tk, PAIR), F32)
                    for e in range(2):
                        g = 2 * a + e
                        placed = {}
                        for r in range(rep):
                            h = g * rep + r
                            qp, pos = h // 2, h % 2
                            if pos not in placed:
                                placed[pos] = (_place_head(kcat, e, pos), _place_head(vcat, e, pos))
                            k_g, v_g = placed[pos]
                            q2 = q_ref[rows, qp * PAIR:(qp + 1) * PAIR]
                            do2 = do_ref[rows, qp * PAIR:(qp + 1) * PAIR]
                            lse_h = lse_ref[h:h + 1, rows]
                            dl_h = dl_ref[h:h + 1, rows]
                            p_t = jnp.exp(jnp.where(mask_t, _dot_nt(k_g, q2), NEG_INF) - lse_h)
                            dp_t = _dot_nt(v_g, do2)
                            dsb = (p_t * (dp_t - dl_h)).astype(BF16)
                            part = _dot_tn(k_g, dsb)
                            dq_t[qp] = part if dq_t[qp] is None else dq_t[qp] + part
                            dk2 = dk2 + _dot(dsb, _place_head(q2, pos, e) * 0.125)
                            dv2 = dv2 + _dot(p_t.astype(BF16), _place_head(do2, pos, e))
                            if sink is not None:
                                ds_sink = -jnp.sum(jnp.exp(sink_ref[0, h] - lse_h) * dl_h)
                                dsink_ref[h:h + 1, :] += jnp.full((1, LANES), ds_sink, F32)
                    dk_win[krows, ls] += dk2
                    dv_win[krows, ls] += dv2
                dq = jnp.concatenate(dq_t, axis=0).T
                dq_ref[rows, :] = _rope(dq, *_rope_tabs(cs_c[rows, :], e_ref[...]), -1.0).astype(BF16)

            @pl.when(i > 0)
            def _():
                dk_acc[slot_p, t - w:, :] += dk_win[:w, :]
                dv_acc[slot_p, t - w:, :] += dv_win[:w, :]

            @pl.when(i == 0)
            def _():
                dk_acc[slot_c] = dk_win[w:w + t, :]
                dv_acc[slot_c] = dv_win[w:w + t, :]

            @pl.when(i > 0)
            def _():
                dk_acc[slot_c] += dk_win[w:w + t, :]
                dv_acc[slot_c] += dv_win[w:w + t, :]

            dk_acc[slot_n] = jnp.zeros((t, kw), F32)
            dv_acc[slot_n] = jnp.zeros((t, kw), F32)
            dk_acc[slot_n, :w, :] = dk_win[w + t:, :]
            dv_acc[slot_n, :w, :] = dv_win[w + t:, :]

        @pl.when(i >= 1)
        def _():
            dk_ref[...] = _rope(dk_acc[slot_p], *_rope_tabs(cs_p[...], e_ref[...]), -1.0).astype(BF16)
            dv_ref[...] = dv_acc[slot_p].astype(BF16)

    row_c = lambda width: pl.BlockSpec((None, t, width), lambda s, i: (s, cur(s, i), 0))
    row_p = lambda width: pl.BlockSpec((None, t, width), lambda s, i: (s, jnp.maximum(i - 1, 0), 0))
    stat = pl.BlockSpec((None, hq, t), lambda s, i: (s, 0, cur(s, i)))
    in_specs = ([pl.BlockSpec((None, t, qw), lambda s, i: (s, cur(s, i), qcol))] + kv_specs
                + [row_c(qw), stat, stat, row_c(ROT_DIM), row_p(ROT_DIM),
                   pl.BlockSpec((ROT_DIM, 3 * LANES), lambda s, i: (0, 0))])
    args = [qkv] * 7 + [do, lse, delta, cs, cs, e_mat]
    out_specs = [row_c(qw), row_p(kw), row_p(kw)]
    out_shape = [jax.ShapeDtypeStruct((nseq, seq_len, qw), BF16),
                 jax.ShapeDtypeStruct((nseq, seq_len, kw), BF16),
                 jax.ShapeDtypeStruct((nseq, seq_len, kw), BF16)]
    if sink is not None:
        in_specs = [pl.BlockSpec(memory_space=pltpu.SMEM)] + in_specs
        args = [sink] + args
        out_specs.append(pl.BlockSpec((8, LANES), lambda s, i: (0, 0)))
        out_shape.append(jax.ShapeDtypeStruct((8, LANES), F32))
    return _pcall(
        body, name=name, grid=(nseq, n + 1), in_specs=in_specs, out_specs=out_specs, out_shape=out_shape,
        scratch_shapes=[pltpu.VMEM((3, t, kw), F32), pltpu.VMEM((3, t, kw), F32),
                        pltpu.VMEM((t + 2 * w, kw), F32), pltpu.VMEM((t + 2 * w, kw), F32)], args=args,
        dims=("arbitrary", "arbitrary"), comm=comm)


def _rms_parts(o, g):
    ms = jnp.mean(o * o, axis=-1, keepdims=True) + LN_EPS
    rinv = lax.rsqrt(ms)
    return o * rinv * g, rinv


def _pair_kv(kfull, vfull, qp, rep, krows):
    ks, vs, a_of = [], [], []
    for pos in range(2):
        g = (2 * qp + pos) // rep
        a_of.append(g // 2)
        ks.append(_place_head(kfull[g // 2][krows], g % 2, pos))
        vs.append(_place_head(vfull[g // 2][krows], g % 2, pos))
    assert a_of[0] == a_of[1]
    return jnp.concatenate(ks, axis=0), jnp.concatenate(vs, axis=0), a_of[0]


def _swa_fwd_p(qkv, *, qcol, kcol, vcol, hq, hkv, w, tq, sub, sink, name, comm=None):
    nseq, seq_len, _ = qkv.shape
    t = tq * sub
    n = seq_len // t
    rep = hq // hkv
    tk = tq + 2 * w
    kv_specs, cur = _halo_kv_specs(t, w, hkv, n, seq_len, kcol, vcol)

    def body(*refs):
        if sink is not None:
            sink_ref, refs = refs[0], refs[1:]
        q_ref, kp_ref, kc_ref, kn_ref, vp_ref, vc_ref, vn_ref, o_ref, lse_ref = refs
        i = pl.program_id(1)
        kfull, vfull = [], []
        for a in range(hkv // 2):
            ls = slice(a * PAIR, (a + 1) * PAIR)
            kfull.append(jnp.concatenate([kp_ref[:, ls], kc_ref[:, ls], kn_ref[:, ls]], axis=0) * 0.125)
            vfull.append(jnp.concatenate([vp_ref[:, ls], vc_ref[:, ls], vn_ref[:, ls]], axis=0))
        row_hi = lax.broadcasted_iota(jnp.int32, (PAIR, tq), 0) >= HEAD_DIM
        for jj in range(sub):
            rows = slice(jj * tq, (jj + 1) * tq)
            mask_t = _band_mask_t(i * t + jj * tq, tq, w, seq_len)
            o_t, lse_rows = [], []
            for qp in range(hq // 2):
                kst, vst, _ = _pair_kv(kfull, vfull, qp, rep, slice(jj * tq, jj * tq + tk))
                s2 = _dot_nt(kst, q_ref[rows, qp * PAIR:(qp + 1) * PAIR])
                ps, dens = [], []
                for pos in range(2):
                    h = 2 * qp + pos
                    s_t = jnp.where(mask_t, s2[pos * tk:(pos + 1) * tk], NEG_INF)
                    m = jnp.max(s_t, axis=0, keepdims=True)
                    if sink is not None:
                        m = jnp.maximum(m, sink_ref[0, h])
                    p_t = jnp.exp(s_t - m)
                    den = jnp.sum(p_t, axis=0, keepdims=True)
                    if sink is not None:
                        den = den + jnp.exp(sink_ref[0, h] - m)
                    ps.append(p_t.astype(BF16))
                    dens.append(den)
                    lse_rows.append(m + jnp.log(den))
                both = _dot_tn(vst, jnp.concatenate(ps, axis=0))
                o_t.append(both / jnp.where(row_hi, dens[1], dens[0]))
            o_ref[rows, :] = jnp.concatenate(o_t, axis=0).T
            lse_ref[:, rows] = jnp.concatenate(lse_rows, axis=0)

    in_specs = [pl.BlockSpec((None, t, hq * HEAD_DIM), lambda s, i: (s, i, qcol))] + kv_specs
    args = [qkv] * 7
    if sink is not None:
        in_specs = [pl.BlockSpec(memory_space=pltpu.SMEM)] + in_specs
        args = [sink] + args
    (o, lse), couts = _pcall(
        body, name=name, grid=(nseq, n), in_specs=in_specs,
        out_specs=[pl.BlockSpec((None, t, hq * HEAD_DIM), lambda s, i: (s, i, 0)),
                   pl.BlockSpec((None, hq, t), lambda s, i: (s, 0, i))],
        out_shape=[jax.ShapeDtypeStruct((nseq, seq_len, hq * HEAD_DIM), F32),
                   jax.ShapeDtypeStruct((nseq, hq, seq_len), F32)],
        args=args, dims=("parallel", "parallel"), comm=comm)
    return o, lse, couts


def _swa_bwd_p(qkv, do, lse, delta, cs, e_mat, *, qcol, kcol, vcol, hq, hkv, w, tq, sub, sink, name, comm=None):
    nseq, seq_len, _ = qkv.shape
    t = tq * sub
    n = seq_len // t
    rep = hq // hkv
    qw, kw = hq * HEAD_DIM, hkv * HEAD_DIM
    tk = tq + 2 * w
    kv_specs, cur = _halo_kv_specs(t, w, hkv, n, seq_len, kcol, vcol)

    def body(*refs):
        if sink is not None:
            sink_ref, refs = refs[0], refs[1:]
        (q_ref, kp_ref, kc_ref, kn_ref, vp_ref, vc_ref, vn_ref, do_ref, lse_ref, dl_ref,
         cs_c, cs_p, e_ref) = refs[:13]
        outs = refs[13:]
        if sink is not None:
            dq_ref, dk_ref, dv_ref, dsink_ref, dk_acc, dv_acc, dk_win, dv_win = outs
        else:
            dq_ref, dk_ref, dv_ref, dk_acc, dv_acc, dk_win, dv_win = outs
        s_id = pl.program_id(0)
        i = pl.program_id(1)
        slot_p, slot_c, slot_n = (i + 2) % 3, i % 3, (i + 1) % 3

        if sink is not None:
            @pl.when((s_id == 0) & (i == 0))
            def _():
                dsink_ref[...] = jnp.zeros_like(dsink_ref)

        @pl.when(i < n)
        def _():
            dk_win[...] = jnp.zeros_like(dk_win)
            dv_win[...] = jnp.zeros_like(dv_win)
            kfull, vfull = [], []
            for a in range(hkv // 2):
                ls = slice(a * PAIR, (a + 1) * PAIR)
                kfull.append(jnp.concatenate([kp_ref[:, ls], kc_ref[:, ls], kn_ref[:, ls]], axis=0) * 0.125)
                vfull.append(jnp.concatenate([vp_ref[:, ls], vc_ref[:, ls], vn_ref[:, ls]], axis=0))
            for jj in range(sub):
                rows = slice(jj * tq, (jj + 1) * tq)
                krows = slice(jj * tq, jj * tq + tk)
                mask_t = _band_mask_t(i * t + jj * tq, tq, w, seq_len)
                dq_t = []
                dk2 = [None] * (hkv // 2)
                dv2 = [None] * (hkv // 2)
                for qp in range(hq // 2):
                    kst, vst, a = _pair_kv(kfull, vfull, qp, rep, krows)
                    q2 = q_ref[rows, qp * PAIR:(qp + 1) * PAIR]
                    do2 = do_ref[rows, qp * PAIR:(qp + 1) * PAIR]
                    s2 = _dot_nt(kst, q2)
                    dp2 = _dot_nt(vst, do2)
                    ds, ps, q_at, do_at = [], [], [], []
                    for pos in range(2):
                        h = 2 * qp + pos
                        e = (h // rep) % 2
                        half = slice(pos * tk, (pos + 1) * tk)
                        lse_h = lse_ref[h:h + 1, rows]
                        dl_h = dl_ref[h:h + 1, rows]
                        p_t = jnp.exp(jnp.where(mask_t, s2[half], NEG_INF) - lse_h)
                        ds.append((p_t * (dp2[half] - dl_h)).astype(BF16))
                        ps.append(p_t.astype(BF16))
                        q_at.append(_place_head(q2, pos, e) * 0.125)
                        do_at.append(_place_head(do2, pos, e))
                        if sink is not None:
                            ds_sink = -jnp.sum(jnp.exp(sink_ref[0, h] - lse_h) * dl_h)
                            dsink_ref[h:h + 1, :] += jnp.full((1, LANES), ds_sink, F32)
                    dq_t.append(_rope_rows(_dot_tn(kst, jnp.concatenate(ds, axis=0)),
                                           cs_c[0:ROT_DIM // 2, rows], cs_c[ROT_DIM // 2:ROT_DIM, rows], -1.0))
                    dk_part = _dot(jnp.concatenate(ds, axis=1), jnp.concatenate(q_at, axis=0))
                    dv_part = _dot(jnp.concatenate(ps, axis=1), jnp.concatenate(do_at, axis=0))
                    dk2[a] = dk_part if dk2[a] is None else dk2[a] + dk_part
                    dv2[a] = dv_part if dv2[a] is None else dv2[a] + dv_part
                for a in range(hkv // 2):
                    ls = slice(a * PAIR, (a + 1) * PAIR)
                    dk_win[krows, ls] += dk2[a]
                    dv_win[krows, ls] += dv2[a]
                dq_ref[rows, :] = jnp.concatenate(dq_t, axis=0).T.astype(BF16)

            @pl.when(i > 0)
            def _():
                dk_acc[slot_p, t - w:, :] += dk_win[:w, :]
                dv_acc[slot_p, t - w:, :] += dv_win[:w, :]

            @pl.when(i == 0)
            def _():
                dk_acc[slot_c] = dk_win[w:w + t, :]
                dv_acc[slot_c] = dv_win[w:w + t, :]

            @pl.when(i > 0)
            def _():
                dk_acc[slot_c] += dk_win[w:w + t, :]
                dv_acc[slot_c] += dv_win[w:w + t, :]

            dk_acc[slot_n] = jnp.zeros((t, kw), F32)
            dv_acc[slot_n] = jnp.zeros((t, kw), F32)
            dk_acc[slot_n, :w, :] = dk_win[w + t:, :]
            dv_acc[slot_n, :w, :] = dv_win[w + t:, :]

        @pl.when(i >= 1)
        def _():
            dk_ref[...] = _rope(dk_acc[slot_p], *_rope_tabs(cs_p[...], e_ref[...]), -1.0).astype(BF16)
            dv_ref[...] = dv_acc[slot_p].astype(BF16)

    row_c = lambda width: pl.BlockSpec((None, t, width), lambda s, i: (s, cur(s, i), 0))
    row_p = lambda width: pl.BlockSpec((None, t, width), lambda s, i: (s, jnp.maximum(i - 1, 0), 0))
    stat = pl.BlockSpec((None, hq, t), lambda s, i: (s, 0, cur(s, i)))
    cs_rows = pl.BlockSpec((None, ROT_DIM, t), lambda s, i: (s, 0, cur(s, i)))
    in_specs = ([pl.BlockSpec((None, t, qw), lambda s, i: (s, cur(s, i), qcol))] + kv_specs
                + [row_c(qw), stat, stat, cs_rows, row_p(ROT_DIM),
                   pl.BlockSpec((ROT_DIM, 3 * LANES), lambda s, i: (0, 0))])
    args = [qkv] * 7 + [do, lse, delta, cs.transpose(0, 2, 1), cs, e_mat]
    out_specs = [row_c(qw), row_p(kw), row_p(kw)]
    out_shape = [jax.ShapeDtypeStruct((nseq, seq_len, qw), BF16),
                 jax.ShapeDtypeStruct((nseq, seq_len, kw), BF16),
                 jax.ShapeDtypeStruct((nseq, seq_len, kw), BF16)]
    if sink is not None:
        in_specs = [pl.BlockSpec(memory_space=pltpu.SMEM)] + in_specs
        args = [sink] + args
        out_specs.append(pl.BlockSpec((8, LANES), lambda s, i: (0, 0)))
        out_shape.append(jax.ShapeDtypeStruct((8, LANES), F32))
    return _pcall(
        body, name=name, grid=(nseq, n + 1), in_specs=in_specs, out_specs=out_specs, out_shape=out_shape,
        scratch_shapes=[pltpu.VMEM((3, t, kw), F32), pltpu.VMEM((3, t, kw), F32),
                        pltpu.VMEM((t + 2 * w, kw), F32), pltpu.VMEM((t + 2 * w, kw), F32)], args=args,
        dims=("arbitrary", "arbitrary"), comm=comm)


def _from_subsequences(ref, scr, dil, t):
    slabs = ref.shape[-1] // LANES
    if dil == 1:
        return ref[0].astype(F32)
    for c in range(dil):
        for sl in range(slabs):
            scr[sl, pl.ds(c, t // dil, stride=dil), :] = ref[c, :, sl * LANES:(sl + 1) * LANES].astype(F32)
    return jnp.concatenate([scr[sl] for sl in range(slabs)], axis=1)


def _to_subsequences(val, ref, scr, dil, t):
    slabs = val.shape[-1] // LANES
    if dil == 1:
        ref[0] = val.astype(ref.dtype)
        return
    for sl in range(slabs):
        scr[sl] = val[:, sl * LANES:(sl + 1) * LANES]
    for c in range(dil):
        for sl in range(slabs):
            ref[c, :, sl * LANES:(sl + 1) * LANES] = scr[sl, pl.ds(c, t // dil, stride=dil), :].astype(ref.dtype)


def _combine_fwd(out_a, o_g, lse_g, g_win, g_dil, *, t):
    s = out_a.shape[1]
    wd = DIL_SLOTS * HEAD_DIM

    def body(oa_ref, o0, o1, o2, l0, l1, l2, gw_ref, gd_ref, mixed_ref, ob_ref, lt_ref, scr):
        ls = [l0[...], l1[...], l2[...]]
        mx = jnp.maximum(jnp.maximum(ls[0], ls[1]), ls[2])
        ws = [jnp.exp(l - mx) for l in ls]
        tot = ws[0] + ws[1] + ws[2]
        lt_ref[...] = mx + jnp.log(tot)
        ws = [x / tot for x in ws]
        og = [_from_subsequences(o_ref, scr.at[gi], dil, t)
              for gi, (o_ref, dil) in enumerate(zip((o0, o1, o2), DILATIONS))]
        parts = []
        for h in range(DIL_SLOTS):
            hs = slice(h * HEAD_DIM, (h + 1) * HEAD_DIM)
            parts.append(ws[0][:, h:h + 1] * og[0][:, hs] + ws[1][:, h:h + 1] * og[1][:, hs]
                         + ws[2][:, h:h + 1] * og[2][:, hs])
        ob = jnp.concatenate(parts, axis=1)
        ob_ref[...] = ob
        na, _ = _rms_parts(oa_ref[...], gw_ref[...])
        nb, _ = _rms_parts(ob, gd_ref[...])
        mixed_ref[:, :wd] = na.astype(BF16)
        mixed_ref[:, wd:] = nb.astype(BF16)

    half = pl.BlockSpec((t, wd), lambda i: (i, 0))
    lanes = pl.BlockSpec((t, LANES), lambda i: (i, 0))
    grow = pl.BlockSpec((1, wd), lambda i: (0, 0))
    subseq = [pl.BlockSpec((dil, t // dil, wd), lambda i: (0, i, 0)) for dil in DILATIONS]
    return pl.pallas_call(
        body, name="combine_fwd", grid=(s // t,),
        in_specs=[pl.BlockSpec((None, t, wd), lambda i: (0, i, 0))] + subseq + [lanes, lanes, lanes, grow, grow],
        out_specs=[pl.BlockSpec((t, 2 * wd), lambda i: (i, 0)), half, lanes],
        out_shape=[jax.ShapeDtypeStruct((s, 2 * wd), BF16), jax.ShapeDtypeStruct((s, wd), F32),
                   jax.ShapeDtypeStruct((s, LANES), F32)],
        scratch_shapes=[pltpu.VMEM((len(DILATIONS), wd // LANES, t, LANES), F32)],
        compiler_params=_cparams(dimension_semantics=("parallel",)),
    )(out_a, *o_g, *lse_g, g_win, g_dil)


def _combine_bwd(dmixed, out_a, out_b, g_win, g_dil, *, t):
    s = out_b.shape[0]
    wd = DIL_SLOTS * HEAD_DIM

    def body(dm_ref, oa_ref, ob_ref, gw_ref, gd_ref, doa_ref, dob0, dob1, dob2, dla_ref, dlb_ref, st_ref, scr):
        i = pl.program_id(0)

        @pl.when(i == 0)
        def _():
            st_ref[...] = jnp.zeros_like(st_ref)

        lane = lax.broadcasted_iota(jnp.int32, (t, LANES), 1)
        for idx, (o_ref, g_ref, dl_ref) in enumerate(((oa_ref, gw_ref, dla_ref), (ob_ref, gd_ref, dlb_ref))):
            o = o_ref[...]
            dn = dm_ref[:, idx * wd:(idx + 1) * wd]
            _, rinv = _rms_parts(o, g_ref[...])
            wv = dn * g_ref[...]
            do = rinv * wv - o * (rinv * rinv * rinv) * jnp.mean(wv * o, axis=-1, keepdims=True)
            st_ref[idx:idx + 1, :] += jnp.sum(dn * o * rinv, axis=0, keepdims=True)
            if idx == 0:
                doa_ref[...] = do.astype(BF16)
            else:
                for do_ref, dil in zip((dob0, dob1, dob2), DILATIONS):
                    _to_subsequences(do, do_ref, scr, dil, t)
            prod = do * o
            acc = jnp.zeros((t, LANES), F32)
            for h in range(DIL_SLOTS):
                hs = slice(h * HEAD_DIM, (h + 1) * HEAD_DIM)
                acc = jnp.where(lane == h, jnp.sum(prod[:, hs], axis=1, keepdims=True), acc)
            dl_ref[...] = acc

    half = pl.BlockSpec((t, wd), lambda i: (i, 0))
    lanes = pl.BlockSpec((t, LANES), lambda i: (i, 0))
    grow = pl.BlockSpec((1, wd), lambda i: (0, 0))
    a_spec = pl.BlockSpec((None, t, wd), lambda i: (0, i, 0))
    subseq = [pl.BlockSpec((dil, t // dil, wd), lambda i: (0, i, 0)) for dil in DILATIONS]
    doa, dob0, dob1, dob2, dla, dlb, st = pl.pallas_call(
        body, name="combine_bwd", grid=(s // t,),
        in_specs=[pl.BlockSpec((t, 2 * wd), lambda i: (i, 0)), a_spec, half, grow, grow],
        out_specs=[a_spec] + subseq + [lanes, lanes, pl.BlockSpec((8, wd), lambda i: (0, 0))],
        out_shape=[jax.ShapeDtypeStruct((1, s, wd), BF16)]
        + [jax.ShapeDtypeStruct((dil, s // dil, wd), BF16) for dil in DILATIONS]
        + [jax.ShapeDtypeStruct((s, LANES), F32), jax.ShapeDtypeStruct((s, LANES), F32),
           jax.ShapeDtypeStruct((8, wd), F32)],
        scratch_shapes=[pltpu.VMEM((wd // LANES, t, LANES), F32)],
        compiler_params=_cparams(dimension_semantics=("arbitrary",)),
    )(dmixed, out_a, out_b, g_win, g_dil)
    return doa, [dob0, dob1, dob2], dla, dlb, st


def _assemble_dz(dqa, dka, dva, dqs, dks, dvs, *, t):
    s = dqa.shape[1]
    wd = DIL_SLOTS * HEAD_DIM

    def body(*refs):
        a_refs, g_refs, o_ref, scr = refs[:3], refs[3:12], refs[12], refs[13]
        col = 0
        for r in a_refs:
            o_ref[:, col:col + r.shape[-1]] = r[...]
            col += r.shape[-1]
        for part in range(3):
            for gi, dil in enumerate(DILATIONS):
                val = _from_subsequences(g_refs[3 * part + gi], scr, dil, t)
                o_ref[:, col:col + wd] = val.astype(BF16)
                col += wd

    a_specs = [pl.BlockSpec((None, t, a.shape[-1]), lambda i: (0, i, 0)) for a in (dqa, dka, dva)]
    g_specs = [pl.BlockSpec((dil, t // dil, wd), lambda i: (0, i, 0)) for _ in range(3) for dil in DILATIONS]
    return pl.pallas_call(
        body, name="assemble_dz", grid=(s // t,), in_specs=a_specs + g_specs,
        out_specs=pl.BlockSpec((t, IN_WIDTH), lambda i: (i, 0)),
        out_shape=jax.ShapeDtypeStruct((s, IN_WIDTH), BF16),
        scratch_shapes=[pltpu.VMEM((wd // LANES, t, LANES), F32)],
        compiler_params=_cparams(dimension_semantics=("parallel",)),
    )(dqa, dka, dva, *dqs, *dks, *dvs)


def _mixproj_fwd(mixed_b, w_mix_b, x, ln_in_g, ln_in_b, ln1_g, ln1_b, *, t):
    s = x.shape[0]

    def body(m_ref, w_ref, x_ref, g0, b0, g1, b1, r1_ref, h1_ref):
        h0 = _ln(x_ref[...], g0[...], b0[...])
        r1 = ALPHA * h0 + _dot(m_ref[...], w_ref[...])
        r1_ref[...] = r1
        h1_ref[...] = _ln(r1, g1[...], b1[...]).astype(BF16)

    tile = pl.BlockSpec((t, D_MODEL), lambda i: (i, 0))
    row = pl.BlockSpec((1, D_MODEL), lambda i: (0, 0))
    return pl.pallas_call(
        body, name="mixproj_fwd", grid=(s // t,),
        in_specs=[tile, pl.BlockSpec((D_MODEL, D_MODEL), lambda i: (0, 0)), tile, row, row, row, row],
        out_specs=[tile, tile],
        out_shape=[jax.ShapeDtypeStruct((s, D_MODEL), F32), jax.ShapeDtypeStruct((s, D_MODEL), BF16)],
        compiler_params=_cparams(dimension_semantics=("parallel",)),
    )(mixed_b, w_mix_b, x, ln_in_g, ln_in_b, ln1_g, ln1_b)


def _mem_fwd(mem, g, b, wk_b, wv_b):
    ml = mem.shape[0]

    def body(mem_ref, g_ref, b_ref, wk_ref, wv_ref, mn_ref, kx_ref, vx_ref):
        mn = _ln(mem_ref[...], g_ref[...], b_ref[...]).astype(BF16)
        mn_ref[...] = mn
        kx_ref[...] = _dot(mn, wk_ref[...]).astype(BF16)
        vx_ref[...] = _dot(mn, wv_ref[...]).astype(BF16)

    sh = jax.ShapeDtypeStruct((ml, D_MODEL), BF16)
    return pl.pallas_call(body, name="mem_fwd", out_shape=[sh, sh, sh], compiler_params=_cparams())(
        mem, g, b, wk_b, wv_b)


def _mem_bwd(dkx, dvx, mem, g, b, wk_b, wv_b):
    def body(dk_ref, dv_ref, mem_ref, g_ref, b_ref, wk_ref, wv_ref, dwk_ref, dwv_ref, st_ref):
        mem_v = mem_ref[...]
        mn = _ln(mem_v, g_ref[...], b_ref[...]).astype(BF16)
        dkb = dk_ref[...].astype(BF16)
        dvb = dv_ref[...].astype(BF16)
        dwk_ref[...] = _dot_tn(mn, dkb)
        dwv_ref[...] = _dot_tn(mn, dvb)
        dmn = _dot_nt(dkb, wk_ref[...]) + _dot_nt(dvb, wv_ref[...])
        _, dg, db = _ln_bwd_math(dmn, mem_v, g_ref[...])
        st_ref[...] = jnp.zeros_like(st_ref)
        st_ref[0:1, :] = dg
        st_ref[1:2, :] = db

    sw = jax.ShapeDtypeStruct((D_MODEL, D_MODEL), F32)
    return pl.pallas_call(body, name="mem_bwd", out_shape=[sw, sw, jax.ShapeDtypeStruct((8, D_MODEL), F32)],
                          compiler_params=_cparams())(dkx, dvx, mem, g, b, wk_b, wv_b)


def _xattn_fwd(h1b, r1, kx, vx, wq_b, wo_b, ln1_g, ln1_b, ln2_g, ln2_b, *, t):
    s = h1b.shape[0]
    scale = X_HEAD_DIM ** -0.5

    def body(h_ref, r1_ref, kx_ref, vx_ref, wq_ref, wo_ref, g1, b1, g2, b2, r2_ref, h2_ref, qx_ref, ox_ref, lse_ref):
        qxb = _dot(h_ref[...], wq_ref[...]).astype(BF16)
        qx_ref[...] = qxb
        lane = lax.broadcasted_iota(jnp.int32, (t, LANES), 1)
        lse_acc = jnp.zeros((t, LANES), F32)
        parts = []
        for h in range(X_HEADS):
            hs = slice(h * X_HEAD_DIM, (h + 1) * X_HEAD_DIM)
            sc = _dot_nt(qxb[:, hs] * scale, kx_ref[:, hs])
            m = jnp.max(sc, axis=1, keepdims=True)
            p = jnp.exp(sc - m)
            den = jnp.sum(p, axis=1, keepdims=True)
            parts.append(_dot(p.astype(BF16), vx_ref[:, hs]) / den)
            lse_acc = jnp.where(lane == h, m + jnp.log(den), lse_acc)
        lse_ref[...] = lse_acc
        oxb = jnp.concatenate(parts, axis=1).astype(BF16)
        ox_ref[...] = oxb
        h1 = _ln(r1_ref[...], g1[...], b1[...])
        r2 = ALPHA * h1 + _dot(oxb, wo_ref[...])
        r2_ref[...] = r2
        h2_ref[...] = _ln(r2, g2[...], b2[...]).astype(BF16)

    tile = pl.BlockSpec((t, D_MODEL), lambda i: (i, 0))
    row = pl.BlockSpec((1, D_MODEL), lambda i: (0, 0))
    full = lambda r: pl.BlockSpec((r, D_MODEL), lambda i: (0, 0))
    ml = kx.shape[0]
    bsh = jax.ShapeDtypeStruct((s, D_MODEL), BF16)
    return pl.pallas_call(
        body, name="xattn_fwd", grid=(s // t,),
        in_specs=[tile, tile, full(ml), full(ml), full(D_MODEL), full(D_MODEL), row, row, row, row],
        out_specs=[tile, tile, tile, tile, pl.BlockSpec((t, LANES), lambda i: (i, 0))],
        out_shape=[jax.ShapeDtypeStruct((s, D_MODEL), F32), bsh, bsh, bsh, jax.ShapeDtypeStruct((s, LANES), F32)],
        compiler_params=_cparams(dimension_semantics=("parallel",)),
    )(h1b, r1, kx, vx, wq_b, wo_b, ln1_g, ln1_b, ln2_g, ln2_b)


def _xattn_bwd(dr2, qxb, oxb, lse, kx, vx, wq_b, wo_b, r1, ln1_g, *, t, comm=None):
    s = dr2.shape[0]
    ml = kx.shape[0]
    scale = X_HEAD_DIM ** -0.5

    def body(dr2_ref, qx_ref, ox_ref, lse_ref, kx_ref, vx_ref, wq_ref, wo_ref, r1_ref, g1_ref,
             dr1_ref, dr1b_ref, dqx_ref, dkx_ref, dvx_ref, st_ref):
        i = pl.program_id(0)

        @pl.when(i == 0)
        def _():
            dkx_ref[...] = jnp.zeros_like(dkx_ref)
            dvx_ref[...] = jnp.zeros_like(dvx_ref)
            st_ref[...] = jnp.zeros_like(st_ref)

        dr2v = dr2_ref[...]
        dox = _dot_nt(dr2v.astype(BF16), wo_ref[...])
        parts = []
        for h in range(X_HEADS):
            hs = slice(h * X_HEAD_DIM, (h + 1) * X_HEAD_DIM)
            doh = dox[:, hs]
            dohb = doh.astype(BF16)
            dl = jnp.sum(doh * ox_ref[:, hs].astype(F32), axis=1, keepdims=True)
            qh = qx_ref[:, hs] * scale
            p = jnp.exp(_dot_nt(qh, kx_ref[:, hs]) - lse_ref[:, h:h + 1])
            dp = _dot_nt(dohb, vx_ref[:, hs])
            dsb = (p * (dp - dl)).astype(BF16)
            parts.append(_dot(dsb, kx_ref[:, hs]) * scale)
            dkx_ref[:, hs] += _dot_tn(dsb, qh)
            dvx_ref[:, hs] += _dot_tn(p.astype(BF16), dohb)
        dqxb = jnp.concatenate(parts, axis=1).astype(BF16)
        dqx_ref[...] = dqxb
        dh1 = _dot_nt(dqxb, wq_ref[...]) + ALPHA * dr2v
        dr1, dg, db = _ln_bwd_math(dh1, r1_ref[...], g1_ref[...])
        dr1_ref[...] = dr1
        dr1b_ref[...] = dr1.astype(BF16)
        st_ref[0:1, :] += dg
        st_ref[1:2, :] += db

    tile = pl.BlockSpec((t, D_MODEL), lambda i: (i, 0))
    full = lambda r: pl.BlockSpec((r, D_MODEL), lambda i: (0, 0))
    bsh = jax.ShapeDtypeStruct((s, D_MODEL), BF16)
    return _pcall(
        body, name="xattn_bwd", grid=(s // t,),
        in_specs=[tile, tile, tile, pl.BlockSpec((t, LANES), lambda i: (i, 0)), full(ml), full(ml),
                  full(D_MODEL), full(D_MODEL), tile, full(1)],
        out_specs=[tile, tile, tile, full(ml), full(ml), full(8)],
        out_shape=[jax.ShapeDtypeStruct((s, D_MODEL), F32), bsh, bsh,
                   jax.ShapeDtypeStruct((ml, D_MODEL), F32), jax.ShapeDtypeStruct((ml, D_MODEL), F32),
                   jax.ShapeDtypeStruct((8, D_MODEL), F32)],
        args=[dr2, qxb, oxb, lse, kx, vx, wq_b, wo_b, r1, ln1_g], dims=("arbitrary",), comm=comm)


def _halo_specs(t, s, width):
    tb8 = t // 8
    return [pl.BlockSpec((t, width), lambda i: (i, 0)),
            pl.BlockSpec((8, width), lambda i: (jnp.maximum(i * tb8 - 1, 0), 0)),
            pl.BlockSpec((8, width), lambda i: (jnp.minimum((i + 1) * tb8, s // 8 - 1), 0))]


def _halo_rows(i, n, prev_ref, next_ref):
    prev_row = jnp.where(i > 0, prev_ref[7:8, :], 0.0)
    next_row = jnp.where(i < n - 1, next_ref[0:1, :], 0.0)
    return prev_row, next_row


def _gelu_parts(gc):
    cdf = 0.5 * (1.0 + lax.erf(gc * (2.0 ** -0.5)))
    pdf = jnp.exp(-0.5 * gc * gc) * (1.0 / math.sqrt(2.0 * math.pi))
    return gc * cdf, cdf + gc * pdf


def _conv_fwd(g, u, conv_w, conv_b, *, t):
    s = g.shape[0]
    n = s // t

    def body(g_ref, gp_ref, gn_ref, u_ref, cw_ref, cb_ref, o_ref):
        i = pl.program_id(0)
        gv = g_ref[...]
        prev_row, next_row = _halo_rows(i, n, gp_ref, gn_ref)
        gm1, gp1 = _shift_rows(gv, prev_row, next_row)
        gc = gm1 * cw_ref[0:1, :] + gv * cw_ref[1:2, :] + gp1 * cw_ref[2:3, :] + cb_ref[...]
        act, _ = _gelu_parts(gc)
        o_ref[...] = (act * u_ref[...]).astype(BF16)

    tile = pl.BlockSpec((t, D_FF), lambda i: (i, 0))
    return pl.pallas_call(
        body, name="conv_fwd", grid=(n,),
        in_specs=_halo_specs(t, s, D_FF) + [tile, pl.BlockSpec((3, D_FF), lambda i: (0, 0)),
                                            pl.BlockSpec((1, D_FF), lambda i: (0, 0))],
        out_specs=tile, out_shape=jax.ShapeDtypeStruct((s, D_FF), BF16),
        compiler_params=_cparams(dimension_semantics=("parallel",)),
    )(g, g, g, u, conv_w, conv_b)


def _down_ln3(tb, w_down_b, r2, target, ln2_g, ln2_b, ln3_g, ln3_b, *, t):
    s = r2.shape[0]

    def body(t_ref, w_ref, r2_ref, tg_ref, g2, b2, g3, b3, dr_ref, drb_ref, st_ref):
        i = pl.program_id(0)

        @pl.when(i == 0)
        def _():
            st_ref[...] = jnp.zeros_like(st_ref)

        h2 = _ln(r2_ref[...], g2[...], b2[...])
        r3 = ALPHA * h2 + _dot(t_ref[...], w_ref[...])
        y = _ln(r3, g3[...], b3[...])
        err = y - tg_ref[...]
        loss = 0.5 * jnp.sum(jnp.mean(err * err, axis=-1, keepdims=True))
        dy = err * (1.0 / D_MODEL)
        dr, dg, db = _ln_bwd_math(dy, r3, g3[...])
        dr_ref[...] = dr
        drb_ref[...] = dr.astype(BF16)
        st_ref[0:1, :] += dg
        st_ref[1:2, :] += db
        st_ref[2:3, :] += jnp.full((1, D_MODEL), loss, F32)

    tile = pl.BlockSpec((t, D_MODEL), lambda i: (i, 0))
    row = pl.BlockSpec((1, D_MODEL), lambda i: (0, 0))
    return pl.pallas_call(
        body, name="down_ln3", grid=(s // t,),
        in_specs=[pl.BlockSpec((t, D_FF), lambda i: (i, 0)), pl.BlockSpec((D_FF, D_MODEL), lambda i: (0, 0)),
                  tile, tile, row, row, row, row],
        out_specs=[tile, tile, pl.BlockSpec((8, D_MODEL), lambda i: (0, 0))],
        out_shape=[jax.ShapeDtypeStruct((s, D_MODEL), F32), jax.ShapeDtypeStruct((s, D_MODEL), BF16),
                   jax.ShapeDtypeStruct((8, D_MODEL), F32)],
        compiler_params=_cparams(dimension_semantics=("arbitrary",)),
    )(tb, w_down_b, r2, target, ln2_g, ln2_b, ln3_g, ln3_b)


def _ffn_out(g, u, conv_w, conv_b, w_down_b, r2, target, ln2_g, ln2_b, ln3_g, ln3_b, *, t):
    s = r2.shape[0]
    n = s // t

    def body(g_ref, gp_ref, gn_ref, u_ref, cw_ref, cb_ref, w_ref, r2_ref, tg_ref, g2, b2, g3, b3,
             t_ref, dr_ref, drb_ref, st_ref):
        i = pl.program_id(0)

        @pl.when(i == 0)
        def _():
            st_ref[...] = jnp.zeros_like(st_ref)

        gv = g_ref[...]
        prev_row, next_row = _halo_rows(i, n, gp_ref, gn_ref)
        gm1, gp1 = _shift_rows(gv, prev_row, next_row)
        gc = gm1 * cw_ref[0:1, :] + gv * cw_ref[1:2, :] + gp1 * cw_ref[2:3, :] + cb_ref[...]
        act, _ = _gelu_parts(gc)
        tb = (act * u_ref[...]).astype(BF16)
        t_ref[...] = tb
        h2 = _ln(r2_ref[...], g2[...], b2[...])
        r3 = ALPHA * h2 + _dot(tb, w_ref[...])
        y = _ln(r3, g3[...], b3[...])
        err = y - tg_ref[...]
        loss = 0.5 * jnp.sum(jnp.mean(err * err, axis=-1, keepdims=True))
        dr, dg, db = _ln_bwd_math(err * (1.0 / D_MODEL), r3, g3[...])
        dr_ref[...] = dr
        drb_ref[...] = dr.astype(BF16)
        st_ref[0:1, :] += dg
        st_ref[1:2, :] += db
        st_ref[2:3, :] += jnp.full((1, D_MODEL), loss, F32)

    wide = pl.BlockSpec((t, D_FF), lambda i: (i, 0))
    tile = pl.BlockSpec((t, D_MODEL), lambda i: (i, 0))
    row = pl.BlockSpec((1, D_MODEL), lambda i: (0, 0))
    return pl.pallas_call(
        body, name="ffn_out", grid=(n,),
        in_specs=_halo_specs(t, s, D_FF) + [wide, pl.BlockSpec((3, D_FF), lambda i: (0, 0)),
                                            pl.BlockSpec((1, D_FF), lambda i: (0, 0)),
                                            pl.BlockSpec((D_FF, D_MODEL), lambda i: (0, 0)),
                                            tile, tile, row, row, row, row],
        out_specs=[wide, tile, tile, pl.BlockSpec((8, D_MODEL), lambda i: (0, 0))],
        out_shape=[jax.ShapeDtypeStruct((s, D_FF), BF16), jax.ShapeDtypeStruct((s, D_MODEL), F32),
                   jax.ShapeDtypeStruct((s, D_MODEL), BF16), jax.ShapeDtypeStruct((8, D_MODEL), F32)],
        compiler_params=_cparams(dimension_semantics=("arbitrary",)),
    )(g, g, g, u, conv_w, conv_b, w_down_b, r2, target, ln2_g, ln2_b, ln3_g, ln3_b)


def _dh2_ln2(dgc, conv_w, du, w_gate_b, w_up_b, dr3, r2, ln2_g, *, t, comm=None):
    s = dgc.shape[0]
    n = s // t

    def body(d_ref, dp_ref, dn_ref, cw_ref, du_ref, wg_ref, wu_ref, dr3_ref, r2_ref, g2, dg_ref, dr_ref, drb_ref,
             st_ref):
        i = pl.program_id(0)

        @pl.when(i == 0)
        def _():
            st_ref[...] = jnp.zeros_like(st_ref)

        dv = d_ref[...]
        prev_row, next_row = _halo_rows(i, n, dp_ref, dn_ref)
        dm1, dp1 = _shift_rows(dv, prev_row, next_row)
        dgb = (dp1 * cw_ref[0:1, :] + dv * cw_ref[1:2, :] + dm1 * cw_ref[2:3, :]).astype(BF16)
        dg_ref[...] = dgb
        dh2 = _dot(dgb, wg_ref[...]) + _dot(du_ref[...], wu_ref[...]) + ALPHA * dr3_ref[...]
        dr, dg, db = _ln_bwd_math(dh2, r2_ref[...], g2[...])
        dr_ref[...] = dr
        drb_ref[...] = dr.astype(BF16)
        st_ref[0:1, :] += dg
        st_ref[1:2, :] += db

    wide = pl.BlockSpec((t, D_FF), lambda i: (i, 0))
    tile = pl.BlockSpec((t, D_MODEL), lambda i: (i, 0))
    wfull = pl.BlockSpec((D_FF, D_MODEL), lambda i: (0, 0), pipeline_mode=pl.Buffered(1))
    return _pcall(
        body, name="dh2_ln2", grid=(n,),
        in_specs=_halo_specs(t, s, D_FF) + [pl.BlockSpec((3, D_FF), lambda i: (0, 0)), wide, wfull, wfull,
                                            tile, tile, pl.BlockSpec((1, D_MODEL), lambda i: (0, 0))],
        out_specs=[wide, tile, tile, pl.BlockSpec((8, D_MODEL), lambda i: (0, 0))],
        out_shape=[jax.ShapeDtypeStruct((s, D_FF), BF16), jax.ShapeDtypeStruct((s, D_MODEL), F32),
                   jax.ShapeDtypeStruct((s, D_MODEL), BF16), jax.ShapeDtypeStruct((8, D_MODEL), F32)],
        args=[dgc, dgc, dgc, conv_w, du, w_gate_b, w_up_b, dr3, r2, ln2_g], dims=("arbitrary",), comm=comm)


def _conv_bwd_a(dr3b, w_down_b, g, u, conv_w, conv_b, *, t):
    s = g.shape[0]
    n = s // t

    def body(d_ref, w_ref, g_ref, gp_ref, gn_ref, u_ref, cw_ref, cb_ref, du_ref, dgc_ref, st_ref):
        i = pl.program_id(0)

        @pl.when(i == 0)
        def _():
            st_ref[...] = jnp.zeros_like(st_ref)

        dt = _dot_nt(d_ref[...], w_ref[...])
        gv = g_ref[...]
        prev_row, next_row = _halo_rows(i, n, gp_ref, gn_ref)
        gm1, gp1 = _shift_rows(gv, prev_row, next_row)
        gc = gm1 * cw_ref[0:1, :] + gv * cw_ref[1:2, :] + gp1 * cw_ref[2:3, :] + cb_ref[...]
        act, dact = _gelu_parts(gc)
        du_ref[...] = (dt * act).astype(BF16)
        dgc = dt * u_ref[...] * dact
        dgc_ref[...] = dgc
        st_ref[0:1, :] += jnp.sum(gm1 * dgc, axis=0, keepdims=True)
        st_ref[1:2, :] += jnp.sum(gv * dgc, axis=0, keepdims=True)
        st_ref[2:3, :] += jnp.sum(gp1 * dgc, axis=0, keepdims=True)
        st_ref[3:4, :] += jnp.sum(dgc, axis=0, keepdims=True)

    tile = pl.BlockSpec((t, D_FF), lambda i: (i, 0))
    return pl.pallas_call(
        body, name="conv_bwd_a", grid=(n,),
        in_specs=[pl.BlockSpec((t, D_MODEL), lambda i: (i, 0)), pl.BlockSpec((D_FF, D_MODEL), lambda i: (0, 0))]
        + _halo_specs(t, s, D_FF) + [tile, pl.BlockSpec((3, D_FF), lambda i: (0, 0)),
                                     pl.BlockSpec((1, D_FF), lambda i: (0, 0))],
        out_specs=[tile, tile, pl.BlockSpec((8, D_FF), lambda i: (0, 0))],
        out_shape=[jax.ShapeDtypeStruct((s, D_FF), BF16), jax.ShapeDtypeStruct((s, D_FF), F32),
                   jax.ShapeDtypeStruct((8, D_FF), F32)],
        compiler_params=_cparams(dimension_semantics=("arbitrary",)),
    )(dr3b, w_down_b, g, g, g, u, conv_w, conv_b)


def _conv_bwd_b(dgc, conv_w, *, t):
    s = dgc.shape[0]
    n = s // t

    def body(d_ref, dp_ref, dn_ref, cw_ref, o_ref):
        i = pl.program_id(0)
        dv = d_ref[...]
        prev_row, next_row = _halo_rows(i, n, dp_ref, dn_ref)
        dm1, dp1 = _shift_rows(dv, prev_row, next_row)
        o_ref[...] = (dp1 * cw_ref[0:1, :] + dv * cw_ref[1:2, :] + dm1 * cw_ref[2:3, :]).astype(BF16)

    return pl.pallas_call(
        body, name="conv_bwd_b", grid=(n,),
        in_specs=_halo_specs(t, s, D_FF) + [pl.BlockSpec((3, D_FF), lambda i: (0, 0))],
        out_specs=pl.BlockSpec((t, D_FF), lambda i: (i, 0)), out_shape=jax.ShapeDtypeStruct((s, D_FF), BF16),
        compiler_params=_cparams(dimension_semantics=("parallel",)),
    )(dgc, dgc, dgc, conv_w)


def _to_residue(a, dil):
    s, w = a.shape
    return a.reshape(s // dil, dil, w).transpose(1, 0, 2)


def _from_residue(a):
    dil, l, w = a.shape
    return a.transpose(1, 0, 2).reshape(dil * l, w)


def _stats_to_lanes(rows):
    dil, hq, l = rows.shape
    return jnp.pad(rows.transpose(2, 0, 1).reshape(dil * l, hq), ((0, 0), (0, LANES - hq)))


def _stats_to_rows(lanes, dil):
    s = lanes.shape[0]
    return lanes[:, :DIL_SLOTS].reshape(s // dil, dil, DIL_SLOTS).transpose(1, 2, 0)


def _rope_angles(positions):
    inv_freq = ROPE_THETA ** (-jnp.arange(0, ROT_DIM, 2, dtype=F32) / ROT_DIM)
    ang = positions.astype(F32)[:, None] * inv_freq
    return jnp.concatenate([jnp.cos(ang), jnp.sin(ang)], axis=1)


class _NoPlan:
    def gather(self, stage):
        return None

    def gathered(self, stage, couts, wb):
        pass

    def exchange(self, stage, grads):
        return None

    def exchanged(self, stage, couts):
        pass


def _local_step(x, mem, positions, target, wb, sp, plan=None, *, t_row=256, t_mm=512, tq_a=128, tq_b=128,
                sub_a=4, sub_b=4):
    s = x.shape[0]
    plan = plan or _NoPlan()
    cs = _rope_angles(positions)
    e_mat = _rope_select_matrix()

    (h0b, za, *zb), couts = _proj_all(x, sp["ln_in_g"], sp["ln_in_b"], wb["w_in"], cs, e_mat, t=min(2 * t_mm, s),
                                      comm=plan.gather("proj"))
    plan.gathered("proj", couts, wb)
    sub_a = max(1, min(sub_a, s // tq_a))
    subs_b = [max(1, min(sub_b, s // dil // tq_b)) for dil in DILATIONS]
    out_a, lse_a, couts = _swa_fwd_p(za, qcol=0, kcol=4, vcol=5, hq=WIN_Q_HEADS, hkv=WIN_KV_HEADS, w=WIN_HALF,
                                     tq=tq_a, sub=sub_a, sink=sp["attn_sink"], name="attn_a_fwd",
                                     comm=plan.gather("attn_a"))
    plan.gathered("attn_a", couts, wb)
    o_g, lse_g = [], []
    for gi in range(3):
        o, l, couts = _swa_fwd_p(zb[gi], qcol=0, kcol=1, vcol=2, hq=DIL_SLOTS, hkv=DIL_SLOTS, w=DIL_HALF, tq=tq_b,
                                 sub=subs_b[gi], sink=None, name=f"attn_b{gi}_fwd",
                                 comm=plan.gather(f"attn_b{gi}"))
        plan.gathered(f"attn_b{gi}", couts, wb)
        o_g.append(o)
        lse_g.append(_stats_to_lanes(l))
    mixed_b, out_b, lse_b = _combine_fwd(out_a, o_g, lse_g, sp["g_win"], sp["g_dil"], t=t_row)
    r1, h1b = _mixproj_fwd(mixed_b, wb["w_mix_out"], x, sp["ln_in_g"], sp["ln_in_b"], sp["ln1_g"], sp["ln1_b"],
                           t=t_mm)
    mem_nb, kx, vx = _mem_fwd(mem, sp["mem_ln_g"], sp["mem_ln_b"], wb["w_xk"], wb["w_xv"])
    r2, h2b, qxb, oxb, lse_x = _xattn_fwd(h1b, r1, kx, vx, wb["w_xq"], wb["w_xo"], sp["ln1_g"], sp["ln1_b"],
                                          sp["ln2_g"], sp["ln2_b"], t=t_mm)
    g = _mm(h2b, wb["w_gate"], mode="nt", out_dtype=F32, tm=t_mm, tn=D_FF, name="ff_gate")
    u = _mm(h2b, wb["w_up"], mode="nt", out_dtype=F32, tm=t_mm, tn=D_FF, name="ff_up")
    tb, dr3, dr3b, st3 = _ffn_out(g, u, sp["conv_w"], sp["conv_b"], wb["w_down"], r2, target, sp["ln2_g"],
                                  sp["ln2_b"], sp["ln3_g"], sp["ln3_b"], t=t_row)

    grads = {}
    du, dgc, st_conv = _conv_bwd_a(dr3b, wb["w_down"], g, u, sp["conv_w"], sp["conv_b"], t=t_row)
    tk = min(1024, s)
    grads["w_down"] = _mm(tb, dr3b, mode="tn", out_dtype=BF16, tm=D_FF // 2, tn=D_MODEL, tk=tk, name="dw_down")
    grads["w_up"] = _mm(du, h2b, mode="tn", out_dtype=BF16, tm=D_FF // 2, tn=D_MODEL, tk=tk, name="dw_up")
    (dg, dr2, dr2b, st2), couts = _dh2_ln2(dgc, sp["conv_w"], du, wb["w_gate"], wb["w_up"], dr3, r2, sp["ln2_g"],
                                           t=t_mm, comm=plan.exchange("dh2", grads))
    plan.exchanged("dh2", couts)
    grads["w_gate"] = _mm(dg, h2b, mode="tn", out_dtype=BF16, tm=D_FF // 2, tn=D_MODEL, tk=tk, name="dw_gate")

    (dr1, dr1b, dqxb, dkx, dvx, st1), couts = _xattn_bwd(
        dr2, qxb, oxb, lse_x, kx, vx, wb["w_xq"], wb["w_xo"], r1, sp["ln1_g"], t=t_mm,
        comm=plan.exchange("xattn", grads))
    plan.exchanged("xattn", couts)
    grads["w_xo"] = _mm(oxb, dr2b, mode="tn", out_dtype=BF16, tm=D_MODEL, tn=D_MODEL, tk=tk, name="dw_xo")
    grads["w_xq"] = _mm(h1b, dqxb, mode="tn", out_dtype=BF16, tm=D_MODEL, tn=D_MODEL, tk=tk, name="dw_xq")
    grads["w_xk"], grads["w_xv"], st_mem = _mem_bwd(dkx, dvx, mem, sp["mem_ln_g"], sp["mem_ln_b"],
                                                    wb["w_xk"], wb["w_xv"])

    grads["w_mix_out"] = _mm(mixed_b, dr1b, mode="tn", out_dtype=BF16, tm=D_MODEL, tn=D_MODEL, tk=tk,
                             name="dw_mix")
    dmixed = _mm(dr1b, wb["w_mix_out"], mode="nt", out_dtype=F32, tm=t_mm, tn=D_MODEL, name="dmixed")
    do_a, do_b, dl_a, dl_b, st_mix = _combine_bwd(dmixed, out_a, out_b, sp["g_win"], sp["g_dil"], t=t_row)
    (dqa, dka, dva, dsink), couts = _swa_bwd_p(
        za, do_a, lse_a, _stats_to_rows(dl_a, 1), cs[None], e_mat, qcol=0, kcol=4, vcol=5, hq=WIN_Q_HEADS,
        hkv=WIN_KV_HEADS, w=WIN_HALF, tq=2 * tq_a, sub=max(1, sub_a // 2), sink=sp["attn_sink"], name="attn_a_bwd",
        comm=plan.exchange("attn_a", grads))
    plan.exchanged("attn_a", couts)
    dqs, dks, dvs = [], [], []
    for gi, dil in enumerate(DILATIONS):
        (dq, dk, dv), couts = _swa_bwd_p(
            zb[gi], do_b[gi], _stats_to_rows(lse_b, dil), _stats_to_rows(dl_b, dil),
            _to_residue(cs, dil), e_mat, qcol=0, kcol=1, vcol=2, hq=DIL_SLOTS, hkv=DIL_SLOTS, w=DIL_HALF, tq=tq_b,
            sub=subs_b[gi], sink=None, name=f"attn_b{gi}_bwd", comm=plan.exchange(f"attn_b{gi}", grads))
        plan.exchanged(f"attn_b{gi}", couts)
        dqs.append(dq)
        dks.append(dk)
        dvs.append(dv)
    dz = _assemble_dz(dqa, dka, dva, dqs, dks, dvs, t=t_row)
    grads["w_in"] = _mm(dz, h0b, mode="tn", out_dtype=BF16, tm=IN_WIDTH // 7, tn=D_MODEL, tk=tk, name="dw_in")
    comm = plan.exchange("dh0", grads)
    dh0 = _mm(dz, wb["w_in"], mode="nn", out_dtype=F32, tm=t_mm, tn=D_MODEL, add=dr1, add_scale=ALPHA, name="dh0",
              comm=comm)
    if comm is not None:
        dh0, couts = dh0
        plan.exchanged("dh0", couts)
    grad_x, st0 = _ln_bwd(dh0, x, sp["ln_in_g"], t=t_row, name="ln_in_bwd", want_bf16=False)

    small = {
        "loss": st3[2:3, 0:1],
        "ln_in_g": st0[0:1], "ln_in_b": st0[1:2],
        "attn_sink": dsink[:, 0].reshape(1, WIN_Q_HEADS),
        "g_win": st_mix[0:1], "g_dil": st_mix[1:2],
        "ln1_g": st1[0:1], "ln1_b": st1[1:2],
        "mem_ln_g": st_mem[0:1], "mem_ln_b": st_mem[1:2],
        "ln2_g": st2[0:1], "ln2_b": st2[1:2],
        "conv_w": st_conv[0:3], "conv_b": st_conv[3:4],
        "ln3_g": st3[0:1], "ln3_b": st3[1:2],
    }
    return grad_x, grads, small


class _SiblingSwap:
    def __init__(self, arrays):
        self.inputs = list(arrays)
        n = len(arrays)
        self.out_shape = [jax.ShapeDtypeStruct(a.shape, a.dtype) for a in arrays]
        self.scratch = [pltpu.SemaphoreType.DMA((n,)), pltpu.SemaphoreType.DMA((n,))]

    def _copies(self, src, dst, sems):
        send_sems, recv_sems = sems
        x, y, c, _ = _place()
        return [pltpu.make_async_remote_copy(
            src_ref=src[a], dst_ref=dst[a], send_sem=send_sems.at[a], recv_sem=recv_sems.at[a],
            device_id=(x, y, 1 - c), device_id_type=MESH_IDS) for a in range(len(src))]

    def start(self, src, dst, sems):
        for cp in self._copies(src, dst, sems):
            cp.start()

    def wait(self, src, dst, sems):
        copies = self._copies(src, dst, sems)
        for cp in copies:
            cp.wait_recv()
        for cp in copies:
            cp.wait_send()


class _Both:
    def __init__(self, first, second):
        self.parts = (first, second)
        self.inputs = first.inputs + second.inputs
        self.out_shape = first.out_shape + second.out_shape
        self.scratch = first.scratch + second.scratch

    def _split(self, src, dst, sems):
        a = self.parts[0]
        ni, no, ns = len(a.inputs), len(a.out_shape), len(a.scratch)
        return ((src[:ni], dst[:no], sems[:ns]), (src[ni:], dst[no:], sems[ns:]))

    def start(self, src, dst, sems):
        for part, args in zip(self.parts, self._split(src, dst, sems)):
            part.start(*args)

    def wait(self, src, dst, sems):
        for part, args in zip(self.parts, self._split(src, dst, sems)):
            part.wait(*args)


def _row_tile(rows, cols, itemsize=4, budget=1 << 20):
    best = None
    for t in range(16, rows + 1, 16):
        if rows % t == 0 and t * cols * itemsize <= budget:
            best = t
    return best or rows


def _sum_slots(stack, *, name):
    n, r, c = stack.shape
    t = _row_tile(r, c)

    def body(s_ref, o_ref):
        acc = s_ref[0].astype(F32)
        for q in range(1, n):
            acc = acc + s_ref[q].astype(F32)
        o_ref[...] = acc

    return pl.pallas_call(
        body, name=name, grid=(r // t,), in_specs=[pl.BlockSpec((n, t, c), lambda i: (0, i, 0))],
        out_specs=pl.BlockSpec((t, c), lambda i: (i, 0)), out_shape=jax.ShapeDtypeStruct((r, c), F32),
        compiler_params=_cparams(dimension_semantics=("parallel",)),
    )(stack)


def _adamw(w, m, v, p, q, *, name):
    r, c = w.shape
    t = _row_tile(r, c, budget=1 << 20)

    def body(*refs):
        if q is None:
            w_ref, m_ref, v_ref, p_ref, g_ref, d_ref, nm_ref, nv_ref = refs
            g = p_ref[...]
        else:
            w_ref, m_ref, v_ref, p_ref, q_ref, g_ref, d_ref, nm_ref, nv_ref = refs
            g = p_ref[...] + q_ref[...]
        nm = ADAM_B1 * m_ref[...] + (1.0 - ADAM_B1) * g
        nv = ADAM_B2 * v_ref[...] + (1.0 - ADAM_B2) * (g * g)
        m_hat = nm / (1.0 - ADAM_B1 ** ADAM_STEP)
        v_hat = nv / (1.0 - ADAM_B2 ** ADAM_STEP)
        g_ref[...] = g
        d_ref[...] = -ADAM_LR * (m_hat / (jnp.sqrt(v_hat) + ADAM_EPS) + ADAM_WD * w_ref[...])
        nm_ref[...] = nm
        nv_ref[...] = nv

    tile = pl.BlockSpec((t, c), lambda i: (i, 0))
    args = [w, m, v, p] + ([] if q is None else [q])
    sh = jax.ShapeDtypeStruct((r, c), F32)
    return pl.pallas_call(
        body, name=name, grid=(r // t,), in_specs=[tile] * len(args), out_specs=[tile] * 4, out_shape=[sh] * 4,
        compiler_params=_cparams(dimension_semantics=("parallel",)),
    )(*args)


BIG = ("w_in", "w_mix_out", "w_xq", "w_xk", "w_xv", "w_xo", "w_gate", "w_up", "w_down")
COL_SHARDED = ("w_in", "w_gate", "w_up")
WEIGHTS = ("ln_in_g", "ln_in_b", "w_in", "attn_sink", "g_win", "g_dil", "w_mix_out", "ln1_g", "ln1_b",
           "mem_ln_g", "mem_ln_b", "w_xq", "w_xk", "w_xv", "w_xo", "ln2_g", "ln2_b", "w_gate", "w_up",
           "conv_w", "conv_b", "w_down", "ln3_g", "ln3_b")
SMALL = tuple(k for k in WEIGHTS if k not in BIG)
PACK_COLS = 1024
CONV_SHARD = D_FF // N_CHIPS
CONV_WIDTH_ROWS = 3
SMALL_ROWS = 32


GATHER_STAGES = {"proj": ("w_mix_out", "w_xq", "w_xk", "w_xv", "w_xo", "w_up", "w_down"), "attn_a": ("w_gate",)}
EXCHANGE_STAGES = {"dh2": ("w_down",), "xattn": ("w_up",), "attn_a": ("w_gate", "w_xo", "w_xq"),
                   "attn_b0": ("w_xk", "w_xv", "w_mix_out"), "dh0": ("w_in",)}


def _full_weight(k, g4):
    return g4.reshape(N_CHIPS * g4.shape[1], g4.shape[2])


def _grad_parts(k, gk):
    gk = gk.astype(BF16)
    return gk.reshape(N_CHIPS, gk.shape[0] // N_CHIPS, gk.shape[1])


EARLY_SWAP_STAGE = "attn_b2"


class _Plan:
    def __init__(self, shards):
        self.shards = shards
        self.recv = {}
        self.chip_sums = {}
        self.sibling_sums = {}

    def gather(self, stage):
        names = GATHER_STAGES.get(stage)
        return _ChipGather([self.shards[k] for k in names]) if names else None

    def gathered(self, stage, couts, wb):
        for k, g4 in zip(GATHER_STAGES.get(stage, ()), couts):
            wb[k] = _full_weight(k, g4)

    def exchange(self, stage, grads):
        if stage == EARLY_SWAP_STAGE:
            self.early = [k for k in BIG if k in self.recv]
            for k in self.early:
                self.chip_sums[k] = _sum_slots(self.recv[k], name=f"sum_chips_{k}")
            return _SiblingSwap([self.chip_sums[k] for k in self.early])
        names = EXCHANGE_STAGES.get(stage)
        return _ChipExchange([_grad_parts(k, grads[k]) for k in names]) if names else None

    def exchanged(self, stage, couts):
        if stage == EARLY_SWAP_STAGE:
            self.sibling_sums.update(zip(self.early, couts))
            return
        for k, r4 in zip(EXCHANGE_STAGES.get(stage, ()), couts):
            self.recv[k] = r4


def _pack_rows(a):
    r, n = a.shape
    per = -(-n // PACK_COLS)
    return jnp.pad(a, ((0, 0), (0, per * PACK_COLS - n))).reshape(r * per, PACK_COLS)


def _unpack_rows(p, r, n):
    per = -(-n // PACK_COLS)
    return p.reshape(r, per * PACK_COLS)[:, :n]


def _pack(pieces, rows_total):
    cat = jnp.concatenate([_pack_rows(a) for a in pieces], axis=0)
    return jnp.pad(cat, ((0, rows_total - cat.shape[0]), (0, 0)))


def _unpack(p, shapes):
    out, at = [], 0
    for r, n in shapes:
        per = -(-n // PACK_COLS)
        out.append(_unpack_rows(p[at:at + r * per], r, n))
        at += r * per
    return out


def kernel(x, mem, positions, ln_in_g, ln_in_b, w_in, attn_sink, g_win, g_dil, w_mix_out, ln1_g, ln1_b, mem_ln_g, mem_ln_b, w_xq, w_xk, w_xv, w_xo, ln2_g, ln2_b, w_gate, w_up, conv_w, conv_b, w_down, ln3_g, ln3_b, loss_target, m_ln_in_g, m_ln_in_b, m_w_in, m_attn_sink, m_g_win, m_g_dil, m_w_mix_out, m_ln1_g, m_ln1_b, m_mem_ln_g, m_mem_ln_b, m_w_xq, m_w_xk, m_w_xv, m_w_xo, m_ln2_g, m_ln2_b, m_w_gate, m_w_up, m_conv_w, m_conv_b, m_w_down, m_ln3_g, m_ln3_b, v_ln_in_g, v_ln_in_b, v_w_in, v_attn_sink, v_g_win, v_g_dil, v_w_mix_out, v_ln1_g, v_ln1_b, v_mem_ln_g, v_mem_ln_b, v_w_xq, v_w_xk, v_w_xv, v_w_xo, v_ln2_g, v_ln2_b, v_w_gate, v_w_up, v_conv_w, v_conv_b, v_w_down, v_ln3_g, v_ln3_b):
    given = dict(locals())
    shape_of = {k: given[k].shape for k in WEIGHTS}
    as2d = lambda k, a: a.reshape(-1, a.shape[-1]).T if k in COL_SHARDED else a.reshape(-1, a.shape[-1])
    w2 = {k: as2d(k, given[k]) for k in WEIGHTS}
    m2 = {k: as2d(k, given["m_" + k]) for k in WEIGHTS}
    v2 = {k: as2d(k, given["v_" + k]) for k in WEIGHTS}
    chip = 2 * lax.axis_index("x") + lax.axis_index("y")

    plan = _Plan({k: w2[k].astype(BF16) for k in BIG})
    conv_pack = jnp.pad(w2["conv_w"], ((0, 16 - CONV_WIDTH_ROWS), (0, PACK_COLS - CONV_SHARD)))
    g_in, g_conv = _comm_only(_ChipGather([plan.shards["w_in"], conv_pack]), "gather_w_in")
    wb = {"w_in": _full_weight("w_in", g_in)}
    conv_full = g_conv[:, :CONV_WIDTH_ROWS, :CONV_SHARD].transpose(1, 0, 2).reshape(CONV_WIDTH_ROWS, D_FF)
    sp = {k: w2[k] for k in SMALL}
    sp["conv_w"] = conv_full

    grad_x, grads, small = _local_step(x[0], mem[0], positions[0], loss_target[0], wb, sp, plan)

    small_keys = ("loss",) + SMALL
    small_shapes = [small[k].shape for k in small_keys]
    small_pack = _pack([small[k] for k in small_keys], SMALL_ROWS)
    late = [k for k in BIG if k not in plan.chip_sums]
    for k in late:
        plan.chip_sums[k] = _sum_slots(plan.recv[k], name=f"sum_chips_{k}")
    *late_sibling, small_all = _comm_only(
        _Both(_SiblingSwap([plan.chip_sums[k] for k in late]), _ChipExchange([], small_pack)), "swap_and_small")
    plan.sibling_sums.update(zip(late, late_sibling))
    chip_sums = [plan.chip_sums[k] for k in BIG]
    sibling_sums = [plan.sibling_sums[k] for k in BIG]
    small_sum = _sum_slots(small_all, name="sum_small")
    small_g = dict(zip(small_keys, _unpack(small_sum, small_shapes)))
    loss = small_g["loss"][0, 0]

    res = {}
    for k, p, q in zip(BIG, chip_sums, sibling_sums):
        res[k] = _adamw(w2[k], m2[k], v2[k], p, q, name=f"adamw_{k}")
    small_g["conv_w"] = lax.dynamic_slice_in_dim(small_g["conv_w"], chip * CONV_SHARD, CONV_SHARD, axis=1)
    adam_shapes = [w2[k].shape for k in SMALL]
    packs = [_pack([d[k] for k in SMALL], SMALL_ROWS) for d in (w2, m2, v2, small_g)]
    small_res = [_unpack(o, adam_shapes) for o in _adamw(*packs, None, name="adamw_small")]
    for i, k in enumerate(SMALL):
        res[k] = tuple(o[i] for o in small_res)

    outs = [loss, grad_x[None]]
    for slot in range(4):
        outs += [(res[k][slot].T if k in COL_SHARDED else res[k][slot]).reshape(shape_of[k]) for k in WEIGHTS]
    return tuple(outs)
```

```python
import functools
import math

import jax
import jax.numpy as jnp
from jax import lax
from jax.experimental import pallas as pl
from jax.experimental.pallas import tpu as pltpu

F32 = jnp.float32
BF16 = jnp.bfloat16

D_MODEL = 1024
HEAD_DIM = 64
WIN_Q_HEADS = 8
WIN_KV_HEADS = 2
WIN_HALF = 128
DIL_SLOTS = 8
DILATIONS = (1, 4, 16)
DIL_HALF = 64
ROT_DIM = 16
ROPE_THETA = 500000.0
X_HEADS = 4
X_HEAD_DIM = 256
D_FF = 2816
A_Q = 512
A_KV = 128
A_WIDTH = A_Q + 2 * A_KV
B_QKV = 1536
IN_WIDTH = 5376
ALPHA = 2.0 ** 0.25
LN_EPS = 1e-5
NEG_INF = -1e30
LANES = 128
N_CHIPS = 4
N_DEV = 8

ADAM_LR = 0.001
ADAM_B1 = 0.9
ADAM_B2 = 0.999
ADAM_EPS = 1e-08
ADAM_WD = 0.01
ADAM_STEP = 10

VMEM_LIMIT = 56 * 1024 * 1024


def _cparams(**kw):
    return pltpu.CompilerParams(vmem_limit_bytes=VMEM_LIMIT, **kw)


def _dot(a, b):
    return lax.dot_general(a, b, (((1,), (0,)), ((), ())), preferred_element_type=F32)


def _dot_nt(a, b):
    return lax.dot_general(a, b, (((1,), (1,)), ((), ())), preferred_element_type=F32)


def _dot_tn(a, b):
    return lax.dot_general(a, b, (((0,), (0,)), ((), ())), preferred_element_type=F32)


def _ln(x, g, b):
    mu = jnp.mean(x, axis=-1, keepdims=True)
    xc = x - mu
    var = jnp.mean(xc * xc, axis=-1, keepdims=True)
    return xc * lax.rsqrt(var + LN_EPS) * g + b


def _ln_bwd_math(dy, r, g):
    mu = jnp.mean(r, axis=-1, keepdims=True)
    xc = r - mu
    var = jnp.mean(xc * xc, axis=-1, keepdims=True)
    rstd = lax.rsqrt(var + LN_EPS)
    xhat = xc * rstd
    dxhat = dy * g
    m1 = jnp.mean(dxhat, axis=-1, keepdims=True)
    m2 = jnp.mean(dxhat * xhat, axis=-1, keepdims=True)
    dr = rstd * (dxhat - m1 - xhat * m2)
    return dr, jnp.sum(dy * xhat, axis=0, keepdims=True), jnp.sum(dy, axis=0, keepdims=True)


def _rope(z, ta, tb, tc, sign):
    w = z.shape[1]
    reps = w // LANES
    a = jnp.tile(ta, (1, reps))
    b = jnp.tile(tb, (1, reps))
    c = jnp.tile(tc, (1, reps))
    return z * a + sign * (pltpu.roll(z, w - 8, 1) * b + pltpu.roll(z, 8, 1) * c)


def _shift_rows(x, prev_row, next_row):
    t = x.shape[0]
    row = lax.broadcasted_iota(jnp.int32, x.shape, 0)
    xm1 = jnp.where(row == 0, prev_row, pltpu.roll(x, 1, 0))
    xp1 = jnp.where(row == t - 1, next_row, pltpu.roll(x, t - 1, 0))
    return xm1, xp1


def _rope_tabs(cs, e_mat):
    hi = cs.astype(BF16)
    rest = cs - hi.astype(F32)
    mid = rest.astype(BF16)
    lo = (rest - mid.astype(F32)).astype(BF16)
    tabs = _dot(hi, e_mat) + _dot(mid, e_mat) + _dot(lo, e_mat)
    lane = lax.broadcasted_iota(jnp.int32, (cs.shape[0], LANES), 1)
    ones = jnp.where((lane & (HEAD_DIM - 1)) >= ROT_DIM, 1.0, 0.0)
    return tabs[:, :LANES] + ones, tabs[:, LANES:2 * LANES], tabs[:, 2 * LANES:]


def _rope_select_matrix():
    half = ROT_DIM // 2
    e = [[0.0] * (3 * LANES) for _ in range(ROT_DIM)]
    for lane in range(LANES):
        d = lane % HEAD_DIM
        if d < half:
            e[d][lane] = 1.0
            e[half + d][LANES + lane] = -1.0
        elif d < ROT_DIM:
            e[d - half][lane] = 1.0
            e[d][2 * LANES + lane] = 1.0
    return jnp.array(e, BF16)


def _rope_rows(x, cos_t, sin_t, sign):
    half = ROT_DIM // 2
    parts = []
    for base in (0, HEAD_DIM):
        r1, r2 = x[base:base + half], x[base + half:base + ROT_DIM]
        parts += [r1 * cos_t - sign * (r2 * sin_t), r2 * cos_t + sign * (r1 * sin_t), x[base + ROT_DIM:base + HEAD_DIM]]
    return jnp.concatenate(parts, axis=0)


MESH_IDS = pl.DeviceIdType.MESH
ANY = pl.BlockSpec(memory_space=pl.ANY)


def _place():
    x, y, c = lax.axis_index("x"), lax.axis_index("y"), lax.axis_index("c")
    other_chips = [(1 - x, y), (x, 1 - y), (1 - x, 1 - y)]
    return x, y, c, other_chips


class _ChipGather:
    def __init__(self, shards):
        self.inputs = list(shards)
        n = len(shards)
        self.out_shape = [jax.ShapeDtypeStruct((N_CHIPS,) + a.shape, a.dtype) for a in shards]
        self.scratch = [pltpu.SemaphoreType.DMA((6 * n,)), pltpu.SemaphoreType.DMA((6 * n,)),
                        pltpu.SemaphoreType.DMA((n,))]

    def _copies(self, src, dst, sems):
        send_sems, recv_sems, local_sems = sems
        x, y, c, chips = _place()
        mine = 2 * x + y
        n = len(src)
        local, sends, recvs, passes, pass_recvs = [], [], [], [], []
        for a in range(n):
            half = src[a].shape[0] // 2
            my_rows, other_rows = pl.ds(c * half, half), pl.ds((1 - c) * half, half)
            local.append(pltpu.make_async_copy(src[a], dst[a].at[mine], local_sems.at[a]))
            for j, (px, py) in enumerate(chips):
                k, k2, slot = 3 * a + j, 3 * n + 3 * a + j, 2 * px + py
                sends.append(pltpu.make_async_remote_copy(
                    src_ref=src[a].at[my_rows], dst_ref=dst[a].at[mine, my_rows], send_sem=send_sems.at[k],
                    recv_sem=recv_sems.at[k], device_id=(px, py, c), device_id_type=MESH_IDS))
                recvs.append(pltpu.make_async_remote_copy(
                    src_ref=src[a].at[my_rows], dst_ref=dst[a].at[slot, my_rows], send_sem=send_sems.at[k],
                    recv_sem=recv_sems.at[k], device_id=(px, py, c), device_id_type=MESH_IDS))
                passes.append(pltpu.make_async_remote_copy(
                    src_ref=dst[a].at[slot, my_rows], dst_ref=dst[a].at[slot, my_rows], send_sem=send_sems.at[k2],
                    recv_sem=recv_sems.at[k2], device_id=(x, y, 1 - c), device_id_type=MESH_IDS))
                pass_recvs.append(pltpu.make_async_remote_copy(
                    src_ref=dst[a].at[slot, my_rows], dst_ref=dst[a].at[slot, other_rows],
                    send_sem=send_sems.at[k2], recv_sem=recv_sems.at[k2], device_id=(x, y, 1 - c),
                    device_id_type=MESH_IDS))
        return local, sends, recvs, passes, pass_recvs

    def start(self, src, dst, sems):
        local, sends, _, _, _ = self._copies(src, dst, sems)
        for cp in local + sends:
            cp.start()

    def wait(self, src, dst, sems):
        local, sends, recvs, passes, pass_recvs = self._copies(src, dst, sems)
        for idx, landed in enumerate(recvs):
            landed.wait_recv()
            if passes:
                passes[idx].start()
        for cp in pass_recvs:
            cp.wait_recv()
        for cp in sends + passes:
            cp.wait_send()
        for cp in local:
            cp.wait()


class _ChipExchange:
    def __init__(self, parts, small=None):
        self.inputs = list(parts) + ([small] if small is not None else [])
        self.n = len(parts)
        self.has_small = small is not None
        self.out_shape = [jax.ShapeDtypeStruct(a.shape, a.dtype) for a in parts]
        n_sem, n_loc = 3 * self.n, self.n
        if self.has_small:
            self.out_shape.append(jax.ShapeDtypeStruct((N_DEV,) + small.shape, small.dtype))
            n_sem, n_loc = n_sem + N_DEV - 1, n_loc + 1
        self.scratch = [pltpu.SemaphoreType.DMA((n_sem,)), pltpu.SemaphoreType.DMA((n_sem,)),
                        pltpu.SemaphoreType.DMA((n_loc,))]

    def _copies(self, src, dst, sems):
        send_sems, recv_sems, local_sems = sems
        x, y, c, chips = _place()
        mine = 2 * x + y
        n = self.n
        local, sends, recvs = [], [], []
        for a in range(n):
            local.append(pltpu.make_async_copy(src[a].at[mine], dst[a].at[mine], local_sems.at[a]))
            for j, (px, py) in enumerate(chips):
                k = 3 * a + j
                sends.append(pltpu.make_async_remote_copy(
                    src_ref=src[a].at[2 * px + py], dst_ref=dst[a].at[mine], send_sem=send_sems.at[k],
                    recv_sem=recv_sems.at[k], device_id=(px, py, c), device_id_type=MESH_IDS))
                recvs.append(pltpu.make_async_remote_copy(
                    src_ref=src[a].at[mine], dst_ref=dst[a].at[2 * px + py], send_sem=send_sems.at[k],
                    recv_sem=recv_sems.at[k], device_id=(px, py, c), device_id_type=MESH_IDS))
        if self.has_small:
            me_dev = 4 * x + 2 * y + c
            local.append(pltpu.make_async_copy(src[n], dst[n].at[me_dev], local_sems.at[n]))
            for mask in range(1, N_DEV):
                px, py, pc = x ^ ((mask >> 2) & 1), y ^ ((mask >> 1) & 1), c ^ (mask & 1)
                k = 3 * n + mask - 1
                sends.append(pltpu.make_async_remote_copy(
                    src_ref=src[n], dst_ref=dst[n].at[me_dev], send_sem=send_sems.at[k], recv_sem=recv_sems.at[k],
                    device_id=(px, py, pc), device_id_type=MESH_IDS))
                recvs.append(pltpu.make_async_remote_copy(
                    src_ref=src[n], dst_ref=dst[n].at[4 * px + 2 * py + pc], send_sem=send_sems.at[k],
                    recv_sem=recv_sems.at[k], device_id=(px, py, pc), device_id_type=MESH_IDS))
        return local, sends, recvs, [], []

    start = _ChipGather.start
    wait = _ChipGather.wait


def _pcall(body, *, name, grid, in_specs, out_specs, out_shape, args, scratch_shapes=(), dims=None, comm=None):
    in_specs, out_specs, out_shape = list(in_specs), list(out_specs), list(out_shape)
    scratch_shapes = list(scratch_shapes)
    if comm is None:
        outs = pl.pallas_call(
            body, name=name, grid=grid, in_specs=in_specs, out_specs=out_specs, out_shape=out_shape,
            scratch_shapes=scratch_shapes, compiler_params=_cparams(dimension_semantics=dims),
        )(*args)
        return list(outs), []
    n_in, n_out, n_scr = len(in_specs), len(out_specs), len(scratch_shapes)
    n_cin, n_cout = len(comm.inputs), len(comm.out_shape)

    def wrapped(*refs):
        ins, refs = refs[:n_in], refs[n_in:]
        cins, refs = refs[:n_cin], refs[n_cin:]
        outs, refs = refs[:n_out], refs[n_out:]
        couts, refs = refs[:n_cout], refs[n_cout:]
        scr, csems = refs[:n_scr], refs[n_scr:]
        first = last = None
        for axis, size in enumerate(grid):
            pid = pl.program_id(axis)
            f, l = pid == 0, pid == size - 1
            first = f if first is None else first & f
            last = l if last is None else last & l

        @pl.when(first)
        def _():
            comm.start(cins, couts, csems)

        body(*ins, *outs, *scr)

        @pl.when(last)
        def _():
            comm.wait(cins, couts, csems)

    res = pl.pallas_call(
        wrapped, name=name, grid=grid, in_specs=in_specs + [ANY] * n_cin, out_specs=out_specs + [ANY] * n_cout,
        out_shape=out_shape + list(comm.out_shape), scratch_shapes=scratch_shapes + list(comm.scratch),
        compiler_params=_cparams(dimension_semantics=("arbitrary",) * len(grid)),
    )(*args, *comm.inputs)
    return list(res[:n_out]), list(res[n_out:])


def _comm_only(comm, name):
    def body(*refs):
        n_cin, n_cout = len(comm.inputs), len(comm.out_shape)
        cins, couts, csems = refs[:n_cin], refs[n_cin:n_cin + n_cout], refs[n_cin + n_cout:]
        comm.start(cins, couts, csems)
        comm.wait(cins, couts, csems)

    return list(pl.pallas_call(
        body, name=name, in_specs=[ANY] * len(comm.inputs), out_specs=[ANY] * len(comm.out_shape),
        out_shape=list(comm.out_shape), scratch_shapes=list(comm.scratch),
    )(*comm.inputs))


def _mm(a, b, *, mode, out_dtype, tm, tn, tk=None, add=None, add_scale=1.0, name, comm=None):
    if mode in ("nn", "nt"):
        m, k = a.shape
        n = b.shape[1] if mode == "nn" else b.shape[0]
        assert m % tm == 0 and n % tn == 0
        dot = _dot if mode == "nn" else _dot_nt

        def body(*refs):
            if add is None:
                a_ref, b_ref, o_ref = refs
                o_ref[...] = dot(a_ref[...], b_ref[...]).astype(out_dtype)
            else:
                a_ref, b_ref, c_ref, o_ref = refs
                o_ref[...] = (dot(a_ref[...], b_ref[...]) + add_scale * c_ref[...]).astype(out_dtype)

        b_spec = (pl.BlockSpec((k, tn), lambda i, j: (0, j)) if mode == "nn"
                  else pl.BlockSpec((tn, k), lambda i, j: (j, 0)))
        in_specs = [pl.BlockSpec((tm, k), lambda i, j: (i, 0)), b_spec]
        args = [a, b]
        if add is not None:
            in_specs.append(pl.BlockSpec((tm, tn), lambda i, j: (i, j)))
            args.append(add)
        outs, couts = _pcall(
            body, name=name, grid=(m // tm, n // tn), in_specs=in_specs,
            out_specs=[pl.BlockSpec((tm, tn), lambda i, j: (i, j))],
            out_shape=[jax.ShapeDtypeStruct((m, n), out_dtype)], args=args, dims=("parallel", "parallel"),
            comm=comm)
        return outs[0] if comm is None else (outs[0], couts)
    assert mode == "tn" and add is None and comm is None
    kk, m = a.shape
    n = b.shape[1]
    assert m % tm == 0 and n % tn == 0 and kk % tk == 0
    nk = kk // tk

    def body(a_ref, b_ref, o_ref, acc_ref):
        kstep = pl.program_id(2)

        @pl.when(kstep == 0)
        def _():
            acc_ref[...] = jnp.zeros_like(acc_ref)

        acc_ref[...] += _dot_tn(a_ref[...], b_ref[...])

        @pl.when(kstep == nk - 1)
        def _():
            o_ref[...] = acc_ref[...].astype(out_dtype)

    return pl.pallas_call(
        body, name=name, grid=(m // tm, n // tn, nk),
        in_specs=[pl.BlockSpec((tk, tm), lambda i, j, s: (s, i)), pl.BlockSpec((tk, tn), lambda i, j, s: (s, j))],
        out_specs=pl.BlockSpec((tm, tn), lambda i, j, s: (i, j)),
        out_shape=jax.ShapeDtypeStruct((m, n), out_dtype),
        scratch_shapes=[pltpu.VMEM((tm, tn), F32)],
        compiler_params=_cparams(dimension_semantics=("parallel", "parallel", "arbitrary")),
    )(a, b)


def _ln_bwd(dy, r, g, *, t, name, want_bf16):
    s = r.shape[0]

    def body(dy_ref, r_ref, g_ref, *outs):
        i = pl.program_id(0)
        dr, dg, db = _ln_bwd_math(dy_ref[...], r_ref[...], g_ref[...])
        outs[0][...] = dr
        if want_bf16:
            outs[1][...] = dr.astype(BF16)
        st_ref = outs[-1]

        @pl.when(i == 0)
        def _():
            st_ref[...] = jnp.zeros_like(st_ref)

        st_ref[0:1, :] += dg
        st_ref[1:2, :] += db

    tile = pl.BlockSpec((t, D_MODEL), lambda i: (i, 0))
    out_specs = [tile] + ([tile] if want_bf16 else []) + [pl.BlockSpec((8, D_MODEL), lambda i: (0, 0))]
    out_shape = ([jax.ShapeDtypeStruct((s, D_MODEL), F32)]
                 + ([jax.ShapeDtypeStruct((s, D_MODEL), BF16)] if want_bf16 else [])
                 + [jax.ShapeDtypeStruct((8, D_MODEL), F32)])
    return pl.pallas_call(
        body, name=name, grid=(s // t,),
        in_specs=[tile, tile, pl.BlockSpec((1, D_MODEL), lambda i: (0, 0))],
        out_specs=out_specs, out_shape=out_shape,
        compiler_params=_cparams(dimension_semantics=("arbitrary",)),
    )(dy, r, g)


PROJ_COLS = 256


def _proj_segments():
    wd = DIL_SLOTS * HEAD_DIM
    segs = [(1, [(0, 1), (PROJ_COLS, 1), (2 * PROJ_COLS, 2)])]
    for gi, dil in enumerate(DILATIONS):
        blocks = []
        for part, kind in enumerate((1, 1, 0)):
            col = A_WIDTH + part * B_QKV + gi * wd
            blocks += [(col, kind), (col + PROJ_COLS, kind)]
        segs.append((dil, blocks))
    return segs


PROJ_SEGMENTS = _proj_segments()


def _proj_all(x, g, b, w_t, cs, e_mat, *, t, comm=None):
    s = x.shape[0]
    cb = PROJ_COLS
    halves = cb // LANES

    def body(x_ref, g_ref, b_ref, w_ref, cs_ref, e_ref, h_ref, *rest):
        z_refs, scr = rest[:-1], rest[-1]
        h = _ln(x_ref[...], g_ref[...], b_ref[...]).astype(BF16)
        h_ref[...] = h
        ta, tb, tc = (jnp.tile(tab, (1, halves)) for tab in _rope_tabs(cs_ref[...], e_ref[...]))
        lane = lax.broadcasted_iota(jnp.int32, (t, cb), 1)
        slot = 0
        for z_ref, (dil, blocks) in zip(z_refs, PROJ_SEGMENTS):
            for jb, (col, kind) in enumerate(blocks):
                acc = _dot_nt(h, w_ref[col:col + cb, :])
                if kind:
                    z = acc * ta + (pltpu.roll(acc, cb - 8, 1) * tb + pltpu.roll(acc, 8, 1) * tc)
                    if kind == 2:
                        z = jnp.where(lane < LANES, z, acc)
                else:
                    z = acc
                if dil == 1:
                    z_ref[0, :, cb * jb:cb * (jb + 1)] = z.astype(BF16)
                    continue
                for half in range(halves):
                    scr[slot, half] = z[:, half * LANES:(half + 1) * LANES]
                for c in range(dil):
                    for half in range(halves):
                        rows = scr[slot, half, pl.ds(c, t // dil, stride=dil), :]
                        z_ref[c, :, cb * jb + half * LANES:cb * jb + (half + 1) * LANES] = rows.astype(BF16)
                slot = 1 - slot

    row = pl.BlockSpec((1, D_MODEL), lambda i: (0, 0))
    widths = [cb * len(blocks) for _, blocks in PROJ_SEGMENTS]
    dils = [dil for dil, _ in PROJ_SEGMENTS]
    outs, couts = _pcall(
        body, name="proj_all", grid=(s // t,),
        in_specs=[pl.BlockSpec((t, D_MODEL), lambda i: (i, 0)), row, row,
                  pl.BlockSpec((IN_WIDTH, D_MODEL), lambda i: (0, 0), pipeline_mode=pl.Buffered(1)),
                  pl.BlockSpec((t, ROT_DIM), lambda i: (i, 0)), pl.BlockSpec((ROT_DIM, 3 * LANES), lambda i: (0, 0))],
        out_specs=[pl.BlockSpec((t, D_MODEL), lambda i: (i, 0))]
        + [pl.BlockSpec((dil, t // dil, wd), lambda i: (0, i, 0)) for dil, wd in zip(dils, widths)],
        out_shape=[jax.ShapeDtypeStruct((s, D_MODEL), BF16)]
        + [jax.ShapeDtypeStruct((dil, s // dil, wd), BF16) for dil, wd in zip(dils, widths)],
        args=[x, g, b, w_t, cs, e_mat], scratch_shapes=[pltpu.VMEM((2, halves, t, LANES), F32)],
        dims=("parallel",), comm=comm)
    return outs, couts


PAIR = 2 * HEAD_DIM


def _place_head(x2, src_pos, dst_pos):
    hi = lax.broadcasted_iota(jnp.int32, x2.shape, 1) >= HEAD_DIM
    src = x2 if src_pos == dst_pos else pltpu.roll(x2, HEAD_DIM, 1)
    return jnp.where(hi == (dst_pos == 1), src, jnp.zeros_like(src))


def _band_mask_t(row0, tq, w, seq_len):
    tk = tq + 2 * w
    kk = lax.broadcasted_iota(jnp.int32, (tk, tq), 0)
    qq = lax.broadcasted_iota(jnp.int32, (tk, tq), 1)
    kpos = row0 - w + kk
    return (jnp.abs(qq + w - kk) <= w) & (kpos >= 0) & (kpos < seq_len)


def _halo_kv_specs(t, w, hkv, n, seq_len, kcol, vcol):
    kw = hkv * HEAD_DIM
    per, last = t // w, seq_len // w - 1
    cur = lambda s, i: jnp.minimum(i, n - 1)
    specs = []
    for c in (kcol, vcol):
        specs += [pl.BlockSpec((None, w, kw), lambda s, i, c=c: (s, jnp.maximum(cur(s, i) * per - 1, 0), c)),
                  pl.BlockSpec((None, t, kw), lambda s, i, c=c: (s, cur(s, i), c)),
                  pl.BlockSpec((None, w, kw), lambda s, i, c=c: (s, jnp.minimum((cur(s, i) + 1) * per, last), c))]
    return specs, cur


def _pair_kv(kfull, vfull, qp, rep, krows):
    ks, vs, a_of = [], [], []
    for pos in range(2):
        g = (2 * qp + pos) // rep
        a_of.append(g // 2)
        ks.append(_place_head(kfull[g // 2][krows], g % 2, pos))
        vs.append(_place_head(vfull[g // 2][krows], g % 2, pos))
    assert a_of[0] == a_of[1]
    return jnp.concatenate(ks, axis=0), jnp.concatenate(vs, axis=0), a_of[0]


def _swa_fwd_p(qkv, *, qcol, kcol, vcol, hq, hkv, w, tq, sub, sink, name, comm=None):
    nseq, seq_len, _ = qkv.shape
    t = tq * sub
    n = seq_len // t
    rep = hq // hkv
    tk = tq + 2 * w
    kv_specs, cur = _halo_kv_specs(t, w, hkv, n, seq_len, kcol, vcol)

    def body(*refs):
        if sink is not None:
            sink_ref, refs = refs[0], refs[1:]
        q_ref, kp_ref, kc_ref, kn_ref, vp_ref, vc_ref, vn_ref, o_ref, lse_ref = refs
        i = pl.program_id(1)
        kfull, vfull = [], []
        for a in range(hkv // 2):
            ls = slice(a * PAIR, (a + 1) * PAIR)
            kfull.append(jnp.concatenate([kp_ref[:, ls], kc_ref[:, ls], kn_ref[:, ls]], axis=0) * 0.125)
            vfull.append(jnp.concatenate([vp_ref[:, ls], vc_ref[:, ls], vn_ref[:, ls]], axis=0))
        row_hi = lax.broadcasted_iota(jnp.int32, (PAIR, tq), 0) >= HEAD_DIM
        for jj in range(sub):
            rows = slice(jj * tq, (jj + 1) * tq)
            mask_t = _band_mask_t(i * t + jj * tq, tq, w, seq_len)
            o_t, lse_rows = [], []
            for qp in range(hq // 2):
                kst, vst, _ = _pair_kv(kfull, vfull, qp, rep, slice(jj * tq, jj * tq + tk))
                s2 = _dot_nt(kst, q_ref[rows, qp * PAIR:(qp + 1) * PAIR])
                ps, dens = [], []
                for pos in range(2):
                    h = 2 * qp + pos
                    s_t = jnp.where(mask_t, s2[pos * tk:(pos + 1) * tk], NEG_INF)
                    m = jnp.max(s_t, axis=0, keepdims=True)
                    if sink is not None:
                        m = jnp.maximum(m, sink_ref[0, h])
                    p_t = jnp.exp(s_t - m)
                    den = jnp.sum(p_t, axis=0, keepdims=True)
                    if sink is not None:
                        den = den + jnp.exp(sink_ref[0, h] - m)
                    ps.append(p_t.astype(BF16))
                    dens.append(den)
                    lse_rows.append(m + jnp.log(den))
                both = _dot_tn(vst, jnp.concatenate(ps, axis=0))
                o_t.append(both / jnp.where(row_hi, dens[1], dens[0]))
            o_ref[rows, :] = jnp.concatenate(o_t, axis=0).T
            lse_ref[:, rows] = jnp.concatenate(lse_rows, axis=0)

    in_specs = [pl.BlockSpec((None, t, hq * HEAD_DIM), lambda s, i: (s, i, qcol))] + kv_specs
    args = [qkv] * 7
    if sink is not None:
        in_specs = [pl.BlockSpec(memory_space=pltpu.SMEM)] + in_specs
        args = [sink] + args
    (o, lse), couts = _pcall(
        body, name=name, grid=(nseq, n), in_specs=in_specs,
        out_specs=[pl.BlockSpec((None, t, hq * HEAD_DIM), lambda s, i: (s, i, 0)),
                   pl.BlockSpec((None, hq, t), lambda s, i: (s, 0, i))],
        out_shape=[jax.ShapeDtypeStruct((nseq, seq_len, hq * HEAD_DIM), F32),
                   jax.ShapeDtypeStruct((nseq, hq, seq_len), F32)],
        args=args, dims=("parallel", "parallel"), comm=comm)
    return o, lse, couts


def _swa_bwd_p(qkv, do, lse, delta, cs, e_mat, *, qcol, kcol, vcol, hq, hkv, w, tq, sub, sink, name, comm=None):
    nseq, seq_len, _ = qkv.shape
    t = tq * sub
    n = seq_len // t
    rep = hq // hkv
    qw, kw = hq * HEAD_DIM, hkv * HEAD_DIM
    tk = tq + 2 * w
    kv_specs, cur = _halo_kv_specs(t, w, hkv, n, seq_len, kcol, vcol)

    def body(*refs):
        if sink is not None:
            sink_ref, refs = refs[0], refs[1:]
        (q_ref, kp_ref, kc_ref, kn_ref, vp_ref, vc_ref, vn_ref, do_ref, lse_ref, dl_ref,
         cs_c, cs_p, e_ref) = refs[:13]
        outs = refs[13:]
        if sink is not None:
            dq_ref, dk_ref, dv_ref, dsink_ref, dk_acc, dv_acc, dk_win, dv_win = outs
        else:
            dq_ref, dk_ref, dv_ref, dk_acc, dv_acc, dk_win, dv_win = outs
        s_id = pl.program_id(0)
        i = pl.program_id(1)
        slot_p, slot_c, slot_n = (i + 2) % 3, i % 3, (i + 1) % 3

        if sink is not None:
            @pl.when((s_id == 0) & (i == 0))
            def _():
                dsink_ref[...] = jnp.zeros_like(dsink_ref)

        @pl.when(i < n)
        def _():
            dk_win[...] = jnp.zeros_like(dk_win)
            dv_win[...] = jnp.zeros_like(dv_win)
            kfull, vfull = [], []
            for a in range(hkv // 2):
                ls = slice(a * PAIR, (a + 1) * PAIR)
                kfull.append(jnp.concatenate([kp_ref[:, ls], kc_ref[:, ls], kn_ref[:, ls]], axis=0) * 0.125)
                vfull.append(jnp.concatenate([vp_ref[:, ls], vc_ref[:, ls], vn_ref[:, ls]], axis=0))
            for jj in range(sub):
                rows = slice(jj * tq, (jj + 1) * tq)
                krows = slice(jj * tq, jj * tq + tk)
                mask_t = _band_mask_t(i * t + jj * tq, tq, w, seq_len)
                dq_t = []
                dk2 = [None] * (hkv // 2)
                dv2 = [None] * (hkv // 2)
                for qp in range(hq // 2):
                    kst, vst, a = _pair_kv(kfull, vfull, qp, rep, krows)
                    q2 = q_ref[rows, qp * PAIR:(qp + 1) * PAIR]
                    do2 = do_ref[rows, qp * PAIR:(qp + 1) * PAIR]
                    s2 = _dot_nt(kst, q2)
                    dp2 = _dot_nt(vst, do2)
                    ds, ps, q_at, do_at = [], [], [], []
                    for pos in range(2):
                        h = 2 * qp + pos
                        e = (h // rep) % 2
                        half = slice(pos * tk, (pos + 1) * tk)
                        lse_h = lse_ref[h:h + 1, rows]
                        dl_h = dl_ref[h:h + 1, rows]
                        p_t = jnp.exp(jnp.where(mask_t, s2[half], NEG_INF) - lse_h)
                        ds.append((p_t * (dp2[half] - dl_h)).astype(BF16))
                        ps.append(p_t.astype(BF16))
                        q_at.append(_place_head(q2, pos, e) * 0.125)
                        do_at.append(_place_head(do2, pos, e))
                        if sink is not None:
                            ds_sink = -jnp.sum(jnp.exp(sink_ref[0, h] - lse_h) * dl_h)
                            dsink_ref[h:h + 1, :] += jnp.full((1, LANES), ds_sink, F32)
                    dq_t.append(_rope_rows(_dot_tn(kst, jnp.concatenate(ds, axis=0)),
                                           cs_c[0:ROT_DIM // 2, rows], cs_c[ROT_DIM // 2:ROT_DIM, rows], -1.0))
                    dk_part = _dot(jnp.concatenate(ds, axis=1), jnp.concatenate(q_at, axis=0))
                    dv_part = _dot(jnp.concatenate(ps, axis=1), jnp.concatenate(do_at, axis=0))
                    dk2[a] = dk_part if dk2[a] is None else dk2[a] + dk_part
                    dv2[a] = dv_part if dv2[a] is None else dv2[a] + dv_part
                for a in range(hkv // 2):
                    ls = slice(a * PAIR, (a + 1) * PAIR)
                    dk_win[krows, ls] += dk2[a]
                    dv_win[krows, ls] += dv2[a]
                dq_ref[rows, :] = jnp.concatenate(dq_t, axis=0).T.astype(BF16)

            @pl.when(i > 0)
            def _():
                dk_acc[slot_p, t - w:, :] += dk_win[:w, :]
                dv_acc[slot_p, t - w:, :] += dv_win[:w, :]

            @pl.when(i == 0)
            def _():
                dk_acc[slot_c] = dk_win[w:w + t, :]
                dv_acc[slot_c] = dv_win[w:w + t, :]

            @pl.when(i > 0)
            def _():
                dk_acc[slot_c] += dk_win[w:w + t, :]
                dv_acc[slot_c] += dv_win[w:w + t, :]

            dk_acc[slot_n] = jnp.zeros((t, kw), F32)
            dv_acc[slot_n] = jnp.zeros((t, kw), F32)
            dk_acc[slot_n, :w, :] = dk_win[w + t:, :]
            dv_acc[slot_n, :w, :] = dv_win[w + t:, :]

        @pl.when(i >= 1)
        def _():
            dk_ref[...] = _rope(dk_acc[slot_p], *_rope_tabs(cs_p[...], e_ref[...]), -1.0).astype(BF16)
            dv_ref[...] = dv_acc[slot_p].astype(BF16)

    row_c = lambda width: pl.BlockSpec((None, t, width), lambda s, i: (s, cur(s, i), 0))
    row_p = lambda width: pl.BlockSpec((None, t, width), lambda s, i: (s, jnp.maximum(i - 1, 0), 0))
    stat = pl.BlockSpec((None, hq, t), lambda s, i: (s, 0, cur(s, i)))
    cs_rows = pl.BlockSpec((None, ROT_DIM, t), lambda s, i: (s, 0, cur(s, i)))
    in_specs = ([pl.BlockSpec((None, t, qw), lambda s, i: (s, cur(s, i), qcol))] + kv_specs
                + [row_c(qw), stat, stat, cs_rows, row_p(ROT_DIM),
                   pl.BlockSpec((ROT_DIM, 3 * LANES), lambda s, i: (0, 0))])
    args = [qkv] * 7 + [do, lse, delta, cs.transpose(0, 2, 1), cs, e_mat]
    out_specs = [row_c(qw), row_p(kw), row_p(kw)]
    out_shape = [jax.ShapeDtypeStruct((nseq, seq_len, qw), BF16),
                 jax.ShapeDtypeStruct((nseq, seq_len, kw), BF16),
                 jax.ShapeDtypeStruct((nseq, seq_len, kw), BF16)]
    if sink is not None:
        in_specs = [pl.BlockSpec(memory_space=pltpu.SMEM)] + in_specs
        args = [sink] + args
        out_specs.append(pl.BlockSpec((8, LANES), lambda s, i: (0, 0)))
        out_shape.append(jax.ShapeDtypeStruct((8, LANES), F32))
    return _pcall(
        body, name=name, grid=(nseq, n + 1), in_specs=in_specs, out_specs=out_specs, out_shape=out_shape,
        scratch_shapes=[pltpu.VMEM((3, t, kw), F32), pltpu.VMEM((3, t, kw), F32),
                        pltpu.VMEM((t + 2 * w, kw), F32), pltpu.VMEM((t + 2 * w, kw), F32)], args=args,
        dims=("arbitrary", "arbitrary"), comm=comm)


def _rms_parts(o, g):
    ms = jnp.mean(o * o, axis=-1, keepdims=True) + LN_EPS
    rinv = lax.rsqrt(ms)
    return o * rinv * g, rinv


def _from_subsequences(ref, scr, dil, t):
    slabs = ref.shape[-1] // LANES
    if dil == 1:
        return ref[0].astype(F32)
    for c in range(dil):
        for sl in range(slabs):
            scr[sl, pl.ds(c, t // dil, stride=dil), :] = ref[c, :, sl * LANES:(sl + 1) * LANES].astype(F32)
    return jnp.concatenate([scr[sl] for sl in range(slabs)], axis=1)


def _to_subsequences(val, ref, scr, dil, t):
    slabs = val.shape[-1] // LANES
    if dil == 1:
        ref[0] = val.astype(ref.dtype)
        return
    for sl in range(slabs):
        scr[sl] = val[:, sl * LANES:(sl + 1) * LANES]
    for c in range(dil):
        for sl in range(slabs):
            ref[c, :, sl * LANES:(sl + 1) * LANES] = scr[sl, pl.ds(c, t // dil, stride=dil), :].astype(ref.dtype)


def _combine_fwd(out_a, o_g, lse_g, g_win, g_dil, *, t):
    s = out_a.shape[1]
    wd = DIL_SLOTS * HEAD_DIM

    def body(oa_ref, o0, o1, o2, l0, l1, l2, gw_ref, gd_ref, mixed_ref, ob_ref, lt_ref, scr):
        ls = [l0[...], l1[...], l2[...]]
        mx = jnp.maximum(jnp.maximum(ls[0], ls[1]), ls[2])
        ws = [jnp.exp(l - mx) for l in ls]
        tot = ws[0] + ws[1] + ws[2]
        lt_ref[...] = mx + jnp.log(tot)
        ws = [x / tot for x in ws]
        og = [_from_subsequences(o_ref, scr.at[gi], dil, t)
              for gi, (o_ref, dil) in enumerate(zip((o0, o1, o2), DILATIONS))]
        parts = []
        for h in range(DIL_SLOTS):
            hs = slice(h * HEAD_DIM, (h + 1) * HEAD_DIM)
            parts.append(ws[0][:, h:h + 1] * og[0][:, hs] + ws[1][:, h:h + 1] * og[1][:, hs]
                         + ws[2][:, h:h + 1] * og[2][:, hs])
        ob = jnp.concatenate(parts, axis=1)
        ob_ref[...] = ob
        na, _ = _rms_parts(oa_ref[...], gw_ref[...])
        nb, _ = _rms_parts(ob, gd_ref[...])
        mixed_ref[:, :wd] = na.astype(BF16)
        mixed_ref[:, wd:] = nb.astype(BF16)

    half = pl.BlockSpec((t, wd), lambda i: (i, 0))
    lanes = pl.BlockSpec((t, LANES), lambda i: (i, 0))
    grow = pl.BlockSpec((1, wd), lambda i: (0, 0))
    subseq = [pl.BlockSpec((dil, t // dil, wd), lambda i: (0, i, 0)) for dil in DILATIONS]
    return pl.pallas_call(
        body, name="combine_fwd", grid=(s // t,),
        in_specs=[pl.BlockSpec((None, t, wd), lambda i: (0, i, 0))] + subseq + [lanes, lanes, lanes, grow, grow],
        out_specs=[pl.BlockSpec((t, 2 * wd), lambda i: (i, 0)), half, lanes],
        out_shape=[jax.ShapeDtypeStruct((s, 2 * wd), BF16), jax.ShapeDtypeStruct((s, wd), F32),
                   jax.ShapeDtypeStruct((s, LANES), F32)],
        scratch_shapes=[pltpu.VMEM((len(DILATIONS), wd // LANES, t, LANES), F32)],
        compiler_params=_cparams(dimension_semantics=("parallel",)),
    )(out_a, *o_g, *lse_g, g_win, g_dil)


def _combine_bwd(dmixed, out_a, out_b, g_win, g_dil, *, t):
    s = out_b.shape[0]
    wd = DIL_SLOTS * HEAD_DIM

    def body(dm_ref, oa_ref, ob_ref, gw_ref, gd_ref, doa_ref, dob0, dob1, dob2, dla_ref, dlb_ref, st_ref, scr):
        i = pl.program_id(0)

        @pl.when(i == 0)
        def _():
            st_ref[...] = jnp.zeros_like(st_ref)

        lane = lax.broadcasted_iota(jnp.int32, (t, LANES), 1)
        for idx, (o_ref, g_ref, dl_ref) in enumerate(((oa_ref, gw_ref, dla_ref), (ob_ref, gd_ref, dlb_ref))):
            o = o_ref[...]
            dn = dm_ref[:, idx * wd:(idx + 1) * wd]
            _, rinv = _rms_parts(o, g_ref[...])
            wv = dn * g_ref[...]
            do = rinv * wv - o * (rinv * rinv * rinv) * jnp.mean(wv * o, axis=-1, keepdims=True)
            st_ref[idx:idx + 1, :] += jnp.sum(dn * o * rinv, axis=0, keepdims=True)
            if idx == 0:
                doa_ref[...] = do.astype(BF16)
            else:
                for do_ref, dil in zip((dob0, dob1, dob2), DILATIONS):
                    _to_subsequences(do, do_ref, scr, dil, t)
            prod = do * o
            acc = jnp.zeros((t, LANES), F32)
            for h in range(DIL_SLOTS):
                hs = slice(h * HEAD_DIM, (h + 1) * HEAD_DIM)
                acc = jnp.where(lane == h, jnp.sum(prod[:, hs], axis=1, keepdims=True), acc)
            dl_ref[...] = acc

    half = pl.BlockSpec((t, wd), lambda i: (i, 0))
    lanes = pl.BlockSpec((t, LANES), lambda i: (i, 0))
    grow = pl.BlockSpec((1, wd), lambda i: (0, 0))
    a_spec = pl.BlockSpec((None, t, wd), lambda i: (0, i, 0))
    subseq = [pl.BlockSpec((dil, t // dil, wd), lambda i: (0, i, 0)) for dil in DILATIONS]
    doa, dob0, dob1, dob2, dla, dlb, st = pl.pallas_call(
        body, name="combine_bwd", grid=(s // t,),
        in_specs=[pl.BlockSpec((t, 2 * wd), lambda i: (i, 0)), a_spec, half, grow, grow],
        out_specs=[a_spec] + subseq + [lanes, lanes, pl.BlockSpec((8, wd), lambda i: (0, 0))],
        out_shape=[jax.ShapeDtypeStruct((1, s, wd), BF16)]
        + [jax.ShapeDtypeStruct((dil, s // dil, wd), BF16) for dil in DILATIONS]
        + [jax.ShapeDtypeStruct((s, LANES), F32), jax.ShapeDtypeStruct((s, LANES), F32),
           jax.ShapeDtypeStruct((8, wd), F32)],
        scratch_shapes=[pltpu.VMEM((wd // LANES, t, LANES), F32)],
        compiler_params=_cparams(dimension_semantics=("arbitrary",)),
    )(dmixed, out_a, out_b, g_win, g_dil)
    return doa, [dob0, dob1, dob2], dla, dlb, st


def _assemble_dz(dqa, dka, dva, dqs, dks, dvs, *, t):
    s = dqa.shape[1]
    wd = DIL_SLOTS * HEAD_DIM

    def body(*refs):
        a_refs, g_refs, o_ref, scr = refs[:3], refs[3:12], refs[12], refs[13]
        col = 0
        for r in a_refs:
            o_ref[:, col:col + r.shape[-1]] = r[...]
            col += r.shape[-1]
        for part in range(3):
            for gi, dil in enumerate(DILATIONS):
                val = _from_subsequences(g_refs[3 * part + gi], scr, dil, t)
                o_ref[:, col:col + wd] = val.astype(BF16)
                col += wd

    a_specs = [pl.BlockSpec((None, t, a.shape[-1]), lambda i: (0, i, 0)) for a in (dqa, dka, dva)]
    g_specs = [pl.BlockSpec((dil, t // dil, wd), lambda i: (0, i, 0)) for _ in range(3) for dil in DILATIONS]
    return pl.pallas_call(
        body, name="assemble_dz", grid=(s // t,), in_specs=a_specs + g_specs,
        out_specs=pl.BlockSpec((t, IN_WIDTH), lambda i: (i, 0)),
        out_shape=jax.ShapeDtypeStruct((s, IN_WIDTH), BF16),
        scratch_shapes=[pltpu.VMEM((wd // LANES, t, LANES), F32)],
        compiler_params=_cparams(dimension_semantics=("parallel",)),
    )(dqa, dka, dva, *dqs, *dks, *dvs)


def _mixproj_fwd(mixed_b, w_mix_b, x, ln_in_g, ln_in_b, ln1_g, ln1_b, *, t):
    s = x.shape[0]

    def body(m_ref, w_ref, x_ref, g0, b0, g1, b1, r1_ref, h1_ref):
        h0 = _ln(x_ref[...], g0[...], b0[...])
        r1 = ALPHA * h0 + _dot(m_ref[...], w_ref[...])
        r1_ref[...] = r1
        h1_ref[...] = _ln(r1, g1[...], b1[...]).astype(BF16)

    tile = pl.BlockSpec((t, D_MODEL), lambda i: (i, 0))
    row = pl.BlockSpec((1, D_MODEL), lambda i: (0, 0))
    return pl.pallas_call(
        body, name="mixproj_fwd", grid=(s // t,),
        in_specs=[tile, pl.BlockSpec((D_MODEL, D_MODEL), lambda i: (0, 0)), tile, row, row, row, row],
        out_specs=[tile, tile],
        out_shape=[jax.ShapeDtypeStruct((s, D_MODEL), F32), jax.ShapeDtypeStruct((s, D_MODEL), BF16)],
        compiler_params=_cparams(dimension_semantics=("parallel",)),
    )(mixed_b, w_mix_b, x, ln_in_g, ln_in_b, ln1_g, ln1_b)


def _mem_fwd(mem, g, b, wk_b, wv_b):
    ml = mem.shape[0]

    def body(mem_ref, g_ref, b_ref, wk_ref, wv_ref, mn_ref, kx_ref, vx_ref):
        mn = _ln(mem_ref[...], g_ref[...], b_ref[...]).astype(BF16)
        mn_ref[...] = mn
        kx_ref[...] = _dot(mn, wk_ref[...]).astype(BF16)
        vx_ref[...] = _dot(mn, wv_ref[...]).astype(BF16)

    sh = jax.ShapeDtypeStruct((ml, D_MODEL), BF16)
    return pl.pallas_call(body, name="mem_fwd", out_shape=[sh, sh, sh], compiler_params=_cparams())(
        mem, g, b, wk_b, wv_b)


def _mem_bwd(dkx, dvx, mem, g, b, wk_b, wv_b):
    def body(dk_ref, dv_ref, mem_ref, g_ref, b_ref, wk_ref, wv_ref, dwk_ref, dwv_ref, st_ref):
        mem_v = mem_ref[...]
        mn = _ln(mem_v, g_ref[...], b_ref[...]).astype(BF16)
        dkb = dk_ref[...].astype(BF16)
        dvb = dv_ref[...].astype(BF16)
        dwk_ref[...] = _dot_tn(mn, dkb)
        dwv_ref[...] = _dot_tn(mn, dvb)
        dmn = _dot_nt(dkb, wk_ref[...]) + _dot_nt(dvb, wv_ref[...])
        _, dg, db = _ln_bwd_math(dmn, mem_v, g_ref[...])
        st_ref[...] = jnp.zeros_like(st_ref)
        st_ref[0:1, :] = dg
        st_ref[1:2, :] = db

    sw = jax.ShapeDtypeStruct((D_MODEL, D_MODEL), F32)
    return pl.pallas_call(body, name="mem_bwd", out_shape=[sw, sw, jax.ShapeDtypeStruct((8, D_MODEL), F32)],
                          compiler_params=_cparams())(dkx, dvx, mem, g, b, wk_b, wv_b)


def _xattn_fwd(h1b, r1, kx, vx, wq_b, wo_b, ln1_g, ln1_b, ln2_g, ln2_b, *, t):
    s = h1b.shape[0]
    scale = X_HEAD_DIM ** -0.5

    def body(h_ref, r1_ref, kx_ref, vx_ref, wq_ref, wo_ref, g1, b1, g2, b2, r2_ref, h2_ref, qx_ref, ox_ref, lse_ref):
        qxb = _dot(h_ref[...], wq_ref[...]).astype(BF16)
        qx_ref[...] = qxb
        lane = lax.broadcasted_iota(jnp.int32, (t, LANES), 1)
        lse_acc = jnp.zeros((t, LANES), F32)
        parts = []
        for h in range(X_HEADS):
            hs = slice(h * X_HEAD_DIM, (h + 1) * X_HEAD_DIM)
            sc = _dot_nt(qxb[:, hs] * scale, kx_ref[:, hs])
            m = jnp.max(sc, axis=1, keepdims=True)
            p = jnp.exp(sc - m)
            den = jnp.sum(p, axis=1, keepdims=True)
            parts.append(_dot(p.astype(BF16), vx_ref[:, hs]) / den)
            lse_acc = jnp.where(lane == h, m + jnp.log(den), lse_acc)
        lse_ref[...] = lse_acc
        oxb = jnp.concatenate(parts, axis=1).astype(BF16)
        ox_ref[...] = oxb
        h1 = _ln(r1_ref[...], g1[...], b1[...])
        r2 = ALPHA * h1 + _dot(oxb, wo_ref[...])
        r2_ref[...] = r2
        h2_ref[...] = _ln(r2, g2[...], b2[...]).astype(BF16)

    tile = pl.BlockSpec((t, D_MODEL), lambda i: (i, 0))
    row = pl.BlockSpec((1, D_MODEL), lambda i: (0, 0))
    full = lambda r: pl.BlockSpec((r, D_MODEL), lambda i: (0, 0))
    ml = kx.shape[0]
    bsh = jax.ShapeDtypeStruct((s, D_MODEL), BF16)
    return pl.pallas_call(
        body, name="xattn_fwd", grid=(s // t,),
        in_specs=[tile, tile, full(ml), full(ml), full(D_MODEL), full(D_MODEL), row, row, row, row],
        out_specs=[tile, tile, tile, tile, pl.BlockSpec((t, LANES), lambda i: (i, 0))],
        out_shape=[jax.ShapeDtypeStruct((s, D_MODEL), F32), bsh, bsh, bsh, jax.ShapeDtypeStruct((s, LANES), F32)],
        compiler_params=_cparams(dimension_semantics=("parallel",)),
    )(h1b, r1, kx, vx, wq_b, wo_b, ln1_g, ln1_b, ln2_g, ln2_b)


def _xattn_bwd(dr2, qxb, oxb, lse, kx, vx, wq_b, wo_b, r1, ln1_g, *, t, comm=None):
    s = dr2.shape[0]
    ml = kx.shape[0]
    scale = X_HEAD_DIM ** -0.5

    def body(dr2_ref, qx_ref, ox_ref, lse_ref, kx_ref, vx_ref, wq_ref, wo_ref, r1_ref, g1_ref,
             dr1_ref, dr1b_ref, dqx_ref, dkx_ref, dvx_ref, st_ref):
        i = pl.program_id(0)

        @pl.when(i == 0)
        def _():
            dkx_ref[...] = jnp.zeros_like(dkx_ref)
            dvx_ref[...] = jnp.zeros_like(dvx_ref)
            st_ref[...] = jnp.zeros_like(st_ref)

        dr2v = dr2_ref[...]
        dox = _dot_nt(dr2v.astype(BF16), wo_ref[...])
        parts = []
        for h in range(X_HEADS):
            hs = slice(h * X_HEAD_DIM, (h + 1) * X_HEAD_DIM)
            doh = dox[:, hs]
            dohb = doh.astype(BF16)
            dl = jnp.sum(doh * ox_ref[:, hs].astype(F32), axis=1, keepdims=True)
            qh = qx_ref[:, hs] * scale
            p = jnp.exp(_dot_nt(qh, kx_ref[:, hs]) - lse_ref[:, h:h + 1])
            dp = _dot_nt(dohb, vx_ref[:, hs])
            dsb = (p * (dp - dl)).astype(BF16)
            parts.append(_dot(dsb, kx_ref[:, hs]) * scale)
            dkx_ref[:, hs] += _dot_tn(dsb, qh)
            dvx_ref[:, hs] += _dot_tn(p.astype(BF16), dohb)
        dqxb = jnp.concatenate(parts, axis=1).astype(BF16)
        dqx_ref[...] = dqxb
        dh1 = _dot_nt(dqxb, wq_ref[...]) + ALPHA * dr2v
        dr1, dg, db = _ln_bwd_math(dh1, r1_ref[...], g1_ref[...])
        dr1_ref[...] = dr1
        dr1b_ref[...] = dr1.astype(BF16)
        st_ref[0:1, :] += dg
        st_ref[1:2, :] += db

    tile = pl.BlockSpec((t, D_MODEL), lambda i: (i, 0))
    full = lambda r: pl.BlockSpec((r, D_MODEL), lambda i: (0, 0))
    bsh = jax.ShapeDtypeStruct((s, D_MODEL), BF16)
    return _pcall(
        body, name="xattn_bwd", grid=(s // t,),
        in_specs=[tile, tile, tile, pl.BlockSpec((t, LANES), lambda i: (i, 0)), full(ml), full(ml),
                  full(D_MODEL), full(D_MODEL), tile, full(1)],
        out_specs=[tile, tile, tile, full(ml), full(ml), full(8)],
        out_shape=[jax.ShapeDtypeStruct((s, D_MODEL), F32), bsh, bsh,
                   jax.ShapeDtypeStruct((ml, D_MODEL), F32), jax.ShapeDtypeStruct((ml, D_MODEL), F32),
                   jax.ShapeDtypeStruct((8, D_MODEL), F32)],
        args=[dr2, qxb, oxb, lse, kx, vx, wq_b, wo_b, r1, ln1_g], dims=("arbitrary",), comm=comm)


def _halo_specs(t, s, width):
    tb8 = t // 8
    return [pl.BlockSpec((t, width), lambda i: (i, 0)),
            pl.BlockSpec((8, width), lambda i: (jnp.maximum(i * tb8 - 1, 0), 0)),
            pl.BlockSpec((8, width), lambda i: (jnp.minimum((i + 1) * tb8, s // 8 - 1), 0))]


def _halo_rows(i, n, prev_ref, next_ref):
    prev_row = jnp.where(i > 0, prev_ref[7:8, :], 0.0)
    next_row = jnp.where(i < n - 1, next_ref[0:1, :], 0.0)
    return prev_row, next_row


def _gelu_parts(gc):
    cdf = 0.5 * (1.0 + lax.erf(gc * (2.0 ** -0.5)))
    pdf = jnp.exp(-0.5 * gc * gc) * (1.0 / math.sqrt(2.0 * math.pi))
    return gc * cdf, cdf + gc * pdf


def _ffn_out(g, u, conv_w, conv_b, w_down_b, r2, target, ln2_g, ln2_b, ln3_g, ln3_b, *, t):
    s = r2.shape[0]
    n = s // t

    def body(g_ref, gp_ref, gn_ref, u_ref, cw_ref, cb_ref, w_ref, r2_ref, tg_ref, g2, b2, g3, b3,
             t_ref, dr_ref, drb_ref, st_ref):
        i = pl.program_id(0)

        @pl.when(i == 0)
        def _():
            st_ref[...] = jnp.zeros_like(st_ref)

        gv = g_ref[...]
        prev_row, next_row = _halo_rows(i, n, gp_ref, gn_ref)
        gm1, gp1 = _shift_rows(gv, prev_row, next_row)
        gc = gm1 * cw_ref[0:1, :] + gv * cw_ref[1:2, :] + gp1 * cw_ref[2:3, :] + cb_ref[...]
        act, _ = _gelu_parts(gc)
        tb = (act * u_ref[...]).astype(BF16)
        t_ref[...] = tb
        h2 = _ln(r2_ref[...], g2[...], b2[...])
        r3 = ALPHA * h2 + _dot(tb, w_ref[...])
        y = _ln(r3, g3[...], b3[...])
        err = y - tg_ref[...]
        loss = 0.5 * jnp.sum(jnp.mean(err * err, axis=-1, keepdims=True))
        dr, dg, db = _ln_bwd_math(err * (1.0 / D_MODEL), r3, g3[...])
        dr_ref[...] = dr
        drb_ref[...] = dr.astype(BF16)
        st_ref[0:1, :] += dg
        st_ref[1:2, :] += db
        st_ref[2:3, :] += jnp.full((1, D_MODEL), loss, F32)

    wide = pl.BlockSpec((t, D_FF), lambda i: (i, 0))
    tile = pl.BlockSpec((t, D_MODEL), lambda i: (i, 0))
    row = pl.BlockSpec((1, D_MODEL), lambda i: (0, 0))
    return pl.pallas_call(
        body, name="ffn_out", grid=(n,),
        in_specs=_halo_specs(t, s, D_FF) + [wide, pl.BlockSpec((3, D_FF), lambda i: (0, 0)),
                                            pl.BlockSpec((1, D_FF), lambda i: (0, 0)),
                                            pl.BlockSpec((D_FF, D_MODEL), lambda i: (0, 0)),
                                            tile, tile, row, row, row, row],
        out_specs=[wide, tile, tile, pl.BlockSpec((8, D_MODEL), lambda i: (0, 0))],
        out_shape=[jax.ShapeDtypeStruct((s, D_FF), BF16), jax.ShapeDtypeStruct((s, D_MODEL), F32),
                   jax.ShapeDtypeStruct((s, D_MODEL), BF16), jax.ShapeDtypeStruct((8, D_MODEL), F32)],
        compiler_params=_cparams(dimension_semantics=("arbitrary",)),
    )(g, g, g, u, conv_w, conv_b, w_down_b, r2, target, ln2_g, ln2_b, ln3_g, ln3_b)


def _dh2_ln2(dgc, conv_w, du, w_gate_b, w_up_b, dr3, r2, ln2_g, *, t, comm=None):
    s = dgc.shape[0]
    n = s // t

    def body(d_ref, dp_ref, dn_ref, cw_ref, du_ref, wg_ref, wu_ref, dr3_ref, r2_ref, g2, dg_ref, dr_ref, drb_ref,
             st_ref):
        i = pl.program_id(0)

        @pl.when(i == 0)
        def _():
            st_ref[...] = jnp.zeros_like(st_ref)

        dv = d_ref[...]
        prev_row, next_row = _halo_rows(i, n, dp_ref, dn_ref)
        dm1, dp1 = _shift_rows(dv, prev_row, next_row)
        dgb = (dp1 * cw_ref[0:1, :] + dv * cw_ref[1:2, :] + dm1 * cw_ref[2:3, :]).astype(BF16)
        dg_ref[...] = dgb
        dh2 = _dot(dgb, wg_ref[...]) + _dot(du_ref[...], wu_ref[...]) + ALPHA * dr3_ref[...]
        dr, dg, db = _ln_bwd_math(dh2, r2_ref[...], g2[...])
        dr_ref[...] = dr
        drb_ref[...] = dr.astype(BF16)
        st_ref[0:1, :] += dg
        st_ref[1:2, :] += db

    wide = pl.BlockSpec((t, D_FF), lambda i: (i, 0))
    tile = pl.BlockSpec((t, D_MODEL), lambda i: (i, 0))
    wfull = pl.BlockSpec((D_FF, D_MODEL), lambda i: (0, 0), pipeline_mode=pl.Buffered(1))
    return _pcall(
        body, name="dh2_ln2", grid=(n,),
        in_specs=_halo_specs(t, s, D_FF) + [pl.BlockSpec((3, D_FF), lambda i: (0, 0)), wide, wfull, wfull,
                                            tile, tile, pl.BlockSpec((1, D_MODEL), lambda i: (0, 0))],
        out_specs=[wide, tile, tile, pl.BlockSpec((8, D_MODEL), lambda i: (0, 0))],
        out_shape=[jax.ShapeDtypeStruct((s, D_FF), BF16), jax.ShapeDtypeStruct((s, D_MODEL), F32),
                   jax.ShapeDtypeStruct((s, D_MODEL), BF16), jax.ShapeDtypeStruct((8, D_MODEL), F32)],
        args=[dgc, dgc, dgc, conv_w, du, w_gate_b, w_up_b, dr3, r2, ln2_g], dims=("arbitrary",), comm=comm)


def _conv_bwd_a(dr3b, w_down_b, g, u, conv_w, conv_b, *, t):
    s = g.shape[0]
    n = s // t

    def body(d_ref, w_ref, g_ref, gp_ref, gn_ref, u_ref, cw_ref, cb_ref, du_ref, dgc_ref, st_ref):
        i = pl.program_id(0)

        @pl.when(i == 0)
        def _():
            st_ref[...] = jnp.zeros_like(st_ref)

        dt = _dot_nt(d_ref[...], w_ref[...])
        gv = g_ref[...]
        prev_row, next_row = _halo_rows(i, n, gp_ref, gn_ref)
        gm1, gp1 = _shift_rows(gv, prev_row, next_row)
        gc = gm1 * cw_ref[0:1, :] + gv * cw_ref[1:2, :] + gp1 * cw_ref[2:3, :] + cb_ref[...]
        act, dact = _gelu_parts(gc)
        du_ref[...] = (dt * act).astype(BF16)
        dgc = dt * u_ref[...] * dact
        dgc_ref[...] = dgc
        st_ref[0:1, :] += jnp.sum(gm1 * dgc, axis=0, keepdims=True)
        st_ref[1:2, :] += jnp.sum(gv * dgc, axis=0, keepdims=True)
        st_ref[2:3, :] += jnp.sum(gp1 * dgc, axis=0, keepdims=True)
        st_ref[3:4, :] += jnp.sum(dgc, axis=0, keepdims=True)

    tile = pl.BlockSpec((t, D_FF), lambda i: (i, 0))
    return pl.pallas_call(
        body, name="conv_bwd_a", grid=(n,),
        in_specs=[pl.BlockSpec((t, D_MODEL), lambda i: (i, 0)), pl.BlockSpec((D_FF, D_MODEL), lambda i: (0, 0))]
        + _halo_specs(t, s, D_FF) + [tile, pl.BlockSpec((3, D_FF), lambda i: (0, 0)),
                                     pl.BlockSpec((1, D_FF), lambda i: (0, 0))],
        out_specs=[tile, tile, pl.BlockSpec((8, D_FF), lambda i: (0, 0))],
        out_shape=[jax.ShapeDtypeStruct((s, D_FF), BF16), jax.ShapeDtypeStruct((s, D_FF), F32),
                   jax.ShapeDtypeStruct((8, D_FF), F32)],
        compiler_params=_cparams(dimension_semantics=("arbitrary",)),
    )(dr3b, w_down_b, g, g, g, u, conv_w, conv_b)


def _to_residue(a, dil):
    s, w = a.shape
    return a.reshape(s // dil, dil, w).transpose(1, 0, 2)


def _stats_to_lanes(rows):
    dil, hq, l = rows.shape
    return jnp.pad(rows.transpose(2, 0, 1).reshape(dil * l, hq), ((0, 0), (0, LANES - hq)))


def _stats_to_rows(lanes, dil):
    s = lanes.shape[0]
    return lanes[:, :DIL_SLOTS].reshape(s // dil, dil, DIL_SLOTS).transpose(1, 2, 0)


def _rope_angles(positions):
    inv_freq = ROPE_THETA ** (-jnp.arange(0, ROT_DIM, 2, dtype=F32) / ROT_DIM)
    ang = positions.astype(F32)[:, None] * inv_freq
    return jnp.concatenate([jnp.cos(ang), jnp.sin(ang)], axis=1)


class _NoPlan:
    def gather(self, stage):
        return None

    def gathered(self, stage, couts, wb):
        pass

    def exchange(self, stage, grads):
        return None

    def exchanged(self, stage, couts):
        pass


def _local_step(x, mem, positions, target, wb, sp, plan=None, *, t_row=256, t_mm=512, tq_a=128, tq_b=128,
                sub_a=4, sub_b=4):
    s = x.shape[0]
    plan = plan or _NoPlan()
    cs = _rope_angles(positions)
    e_mat = _rope_select_matrix()

    (h0b, za, *zb), couts = _proj_all(x, sp["ln_in_g"], sp["ln_in_b"], wb["w_in"], cs, e_mat, t=min(2 * t_mm, s),
                                      comm=plan.gather("proj"))
    plan.gathered("proj", couts, wb)
    sub_a = max(1, min(sub_a, s // tq_a))
    subs_b = [max(1, min(sub_b, s // dil // tq_b)) for dil in DILATIONS]
    out_a, lse_a, couts = _swa_fwd_p(za, qcol=0, kcol=4, vcol=5, hq=WIN_Q_HEADS, hkv=WIN_KV_HEADS, w=WIN_HALF,
                                     tq=tq_a, sub=sub_a, sink=sp["attn_sink"], name="attn_a_fwd",
                                     comm=plan.gather("attn_a"))
    plan.gathered("attn_a", couts, wb)
    o_g, lse_g = [], []
    for gi in range(3):
        o, l, couts = _swa_fwd_p(zb[gi], qcol=0, kcol=1, vcol=2, hq=DIL_SLOTS, hkv=DIL_SLOTS, w=DIL_HALF, tq=tq_b,
                                 sub=subs_b[gi], sink=None, name=f"attn_b{gi}_fwd",
                                 comm=plan.gather(f"attn_b{gi}"))
        plan.gathered(f"attn_b{gi}", couts, wb)
        o_g.append(o)
        lse_g.append(_stats_to_lanes(l))
    mixed_b, out_b, lse_b = _combine_fwd(out_a, o_g, lse_g, sp["g_win"], sp["g_dil"], t=t_row)
    r1, h1b = _mixproj_fwd(mixed_b, wb["w_mix_out"], x, sp["ln_in_g"], sp["ln_in_b"], sp["ln1_g"], sp["ln1_b"],
                           t=t_mm)
    mem_nb, kx, vx = _mem_fwd(mem, sp["mem_ln_g"], sp["mem_ln_b"], wb["w_xk"], wb["w_xv"])
    r2, h2b, qxb, oxb, lse_x = _xattn_fwd(h1b, r1, kx, vx, wb["w_xq"], wb["w_xo"], sp["ln1_g"], sp["ln1_b"],
                                          sp["ln2_g"], sp["ln2_b"], t=t_mm)
    g = _mm(h2b, wb["w_gate"], mode="nt", out_dtype=F32, tm=t_mm, tn=D_FF, name="ff_gate")
    u = _mm(h2b, wb["w_up"], mode="nt", out_dtype=F32, tm=t_mm, tn=D_FF, name="ff_up")
    tb, dr3, dr3b, st3 = _ffn_out(g, u, sp["conv_w"], sp["conv_b"], wb["w_down"], r2, target, sp["ln2_g"],
                                  sp["ln2_b"], sp["ln3_g"], sp["ln3_b"], t=t_row)

    grads = {}
    du, dgc, st_conv = _conv_bwd_a(dr3b, wb["w_down"], g, u, sp["conv_w"], sp["conv_b"], t=t_row)
    tk = min(1024, s)
    grads["w_down"] = _mm(tb, dr3b, mode="tn", out_dtype=BF16, tm=D_FF // 2, tn=D_MODEL, tk=tk, name="dw_down")
    grads["w_up"] = _mm(du, h2b, mode="tn", out_dtype=BF16, tm=D_FF // 2, tn=D_MODEL, tk=tk, name="dw_up")
    (dg, dr2, dr2b, st2), couts = _dh2_ln2(dgc, sp["conv_w"], du, wb["w_gate"], wb["w_up"], dr3, r2, sp["ln2_g"],
                                           t=t_mm, comm=plan.exchange("dh2", grads))
    plan.exchanged("dh2", couts)
    grads["w_gate"] = _mm(dg, h2b, mode="tn", out_dtype=BF16, tm=D_FF // 2, tn=D_MODEL, tk=tk, name="dw_gate")

    (dr1, dr1b, dqxb, dkx, dvx, st1), couts = _xattn_bwd(
        dr2, qxb, oxb, lse_x, kx, vx, wb["w_xq"], wb["w_xo"], r1, sp["ln1_g"], t=t_mm,
        comm=plan.exchange("xattn", grads))
    plan.exchanged("xattn", couts)
    grads["w_xo"] = _mm(oxb, dr2b, mode="tn", out_dtype=BF16, tm=D_MODEL, tn=D_MODEL, tk=tk, name="dw_xo")
    grads["w_xq"] = _mm(h1b, dqxb, mode="tn", out_dtype=BF16, tm=D_MODEL, tn=D_MODEL, tk=tk, name="dw_xq")
    grads["w_xk"], grads["w_xv"], st_mem = _mem_bwd(dkx, dvx, mem, sp["mem_ln_g"], sp["mem_ln_b"],
                                                    wb["w_xk"], wb["w_xv"])

    grads["w_mix_out"] = _mm(mixed_b, dr1b, mode="tn", out_dtype=BF16, tm=D_MODEL, tn=D_MODEL, tk=tk,
                             name="dw_mix")
    dmixed = _mm(dr1b, wb["w_mix_out"], mode="nt", out_dtype=F32, tm=t_mm, tn=D_MODEL, name="dmixed")
    do_a, do_b, dl_a, dl_b, st_mix = _combine_bwd(dmixed, out_a, out_b, sp["g_win"], sp["g_dil"], t=t_row)
    (dqa, dka, dva, dsink), couts = _swa_bwd_p(
        za, do_a, lse_a, _stats_to_rows(dl_a, 1), cs[None], e_mat, qcol=0, kcol=4, vcol=5, hq=WIN_Q_HEADS,
        hkv=WIN_KV_HEADS, w=WIN_HALF, tq=2 * tq_a, sub=max(1, sub_a // 2), sink=sp["attn_sink"], name="attn_a_bwd",
        comm=plan.exchange("attn_a", grads))
    plan.exchanged("attn_a", couts)
    dqs, dks, dvs = [], [], []
    for gi, dil in enumerate(DILATIONS):
        (dq, dk, dv), couts = _swa_bwd_p(
            zb[gi], do_b[gi], _stats_to_rows(lse_b, dil), _stats_to_rows(dl_b, dil),
            _to_residue(cs, dil), e_mat, qcol=0, kcol=1, vcol=2, hq=DIL_SLOTS, hkv=DIL_SLOTS, w=DIL_HALF, tq=tq_b,
            sub=subs_b[gi], sink=None, name=f"attn_b{gi}_bwd", comm=plan.exchange(f"attn_b{gi}", grads))
        plan.exchanged(f"attn_b{gi}", couts)
        dqs.append(dq)
        dks.append(dk)
        dvs.append(dv)
    dz = _assemble_dz(dqa, dka, dva, dqs, dks, dvs, t=t_row)
    grads["w_in"] = _mm(dz, h0b, mode="tn", out_dtype=BF16, tm=IN_WIDTH // 7, tn=D_MODEL, tk=tk, name="dw_in")
    comm = plan.exchange("dh0", grads)
    dh0 = _mm(dz, wb["w_in"], mode="nn", out_dtype=F32, tm=t_mm, tn=D_MODEL, add=dr1, add_scale=ALPHA, name="dh0",
              comm=comm)
    if comm is not None:
        dh0, couts = dh0
        plan.exchanged("dh0", couts)
    grad_x, st0 = _ln_bwd(dh0, x, sp["ln_in_g"], t=t_row, name="ln_in_bwd", want_bf16=False)

    small = {
        "loss": st3[2:3, 0:1],
        "ln_in_g": st0[0:1], "ln_in_b": st0[1:2],
        "attn_sink": dsink[:, 0].reshape(1, WIN_Q_HEADS),
        "g_win": st_mix[0:1], "g_dil": st_mix[1:2],
        "ln1_g": st1[0:1], "ln1_b": st1[1:2],
        "mem_ln_g": st_mem[0:1], "mem_ln_b": st_mem[1:2],
        "ln2_g": st2[0:1], "ln2_b": st2[1:2],
        "conv_w": st_conv[0:3], "conv_b": st_conv[3:4],
        "ln3_g": st3[0:1], "ln3_b": st3[1:2],
    }
    return grad_x, grads, small


class _SiblingSwap:
    def __init__(self, arrays):
        self.inputs = list(arrays)
        n = len(arrays)
        self.out_shape = [jax.ShapeDtypeStruct(a.shape, a.dtype) for a in arrays]
        self.scratch = [pltpu.SemaphoreType.DMA((n,)), pltpu.SemaphoreType.DMA((n,))]

    def _copies(self, src, dst, sems):
        send_sems, recv_sems = sems
        x, y, c, _ = _place()
        return [pltpu.make_async_remote_copy(
            src_ref=src[a], dst_ref=dst[a], send_sem=send_sems.at[a], recv_sem=recv_sems.at[a],
            device_id=(x, y, 1 - c), device_id_type=MESH_IDS) for a in range(len(src))]

    def start(self, src, dst, sems):
        for cp in self._copies(src, dst, sems):
            cp.start()

    def wait(self, src, dst, sems):
        copies = self._copies(src, dst, sems)
        for cp in copies:
            cp.wait_recv()
        for cp in copies:
            cp.wait_send()


class _Both:
    def __init__(self, first, second):
        self.parts = (first, second)
        self.inputs = first.inputs + second.inputs
        self.out_shape = first.out_shape + second.out_shape
        self.scratch = first.scratch + second.scratch

    def _split(self, src, dst, sems):
        a = self.parts[0]
        ni, no, ns = len(a.inputs), len(a.out_shape), len(a.scratch)
        return ((src[:ni], dst[:no], sems[:ns]), (src[ni:], dst[no:], sems[ns:]))

    def start(self, src, dst, sems):
        for part, args in zip(self.parts, self._split(src, dst, sems)):
            part.start(*args)

    def wait(self, src, dst, sems):
        for part, args in zip(self.parts, self._split(src, dst, sems)):
            part.wait(*args)


def _row_tile(rows, cols, itemsize=4, budget=1 << 20):
    best = None
    for t in range(16, rows + 1, 16):
        if rows % t == 0 and t * cols * itemsize <= budget:
            best = t
    return best or rows


def _sum_slots(stack, *, name):
    n, r, c = stack.shape
    t = _row_tile(r, c)

    def body(s_ref, o_ref):
        acc = s_ref[0].astype(F32)
        for q in range(1, n):
            acc = acc + s_ref[q].astype(F32)
        o_ref[...] = acc

    return pl.pallas_call(
        body, name=name, grid=(r // t,), in_specs=[pl.BlockSpec((n, t, c), lambda i: (0, i, 0))],
        out_specs=pl.BlockSpec((t, c), lambda i: (i, 0)), out_shape=jax.ShapeDtypeStruct((r, c), F32),
        compiler_params=_cparams(dimension_semantics=("parallel",)),
    )(stack)


def _adamw(w, m, v, p, q, *, name):
    r, c = w.shape
    t = _row_tile(r, c, budget=1 << 20)

    def body(*refs):
        if q is None:
            w_ref, m_ref, v_ref, p_ref, g_ref, d_ref, nm_ref, nv_ref = refs
            g = p_ref[...]
        else:
            w_ref, m_ref, v_ref, p_ref, q_ref, g_ref, d_ref, nm_ref, nv_ref = refs
            g = p_ref[...] + q_ref[...]
        nm = ADAM_B1 * m_ref[...] + (1.0 - ADAM_B1) * g
        nv = ADAM_B2 * v_ref[...] + (1.0 - ADAM_B2) * (g * g)
        m_hat = nm / (1.0 - ADAM_B1 ** ADAM_STEP)
        v_hat = nv / (1.0 - ADAM_B2 ** ADAM_STEP)
        g_ref[...] = g
        d_ref[...] = -ADAM_LR * (m_hat / (jnp.sqrt(v_hat) + ADAM_EPS) + ADAM_WD * w_ref[...])
        nm_ref[...] = nm
        nv_ref[...] = nv

    tile = pl.BlockSpec((t, c), lambda i: (i, 0))
    args = [w, m, v, p] + ([] if q is None else [q])
    sh = jax.ShapeDtypeStruct((r, c), F32)
    return pl.pallas_call(
        body, name=name, grid=(r // t,), in_specs=[tile] * len(args), out_specs=[tile] * 4, out_shape=[sh] * 4,
        compiler_params=_cparams(dimension_semantics=("parallel",)),
    )(*args)


BIG = ("w_in", "w_mix_out", "w_xq", "w_xk", "w_xv", "w_xo", "w_gate", "w_up", "w_down")
COL_SHARDED = ("w_in", "w_gate", "w_up")
WEIGHTS = ("ln_in_g", "ln_in_b", "w_in", "attn_sink", "g_win", "g_dil", "w_mix_out", "ln1_g", "ln1_b",
           "mem_ln_g", "mem_ln_b", "w_xq", "w_xk", "w_xv", "w_xo", "ln2_g", "ln2_b", "w_gate", "w_up",
           "conv_w", "conv_b", "w_down", "ln3_g", "ln3_b")
SMALL = tuple(k for k in WEIGHTS if k not in BIG)
PACK_COLS = 1024
CONV_SHARD = D_FF // N_CHIPS
CONV_WIDTH_ROWS = 3
SMALL_ROWS = 32


GATHER_STAGES = {"proj": ("w_mix_out", "w_xq", "w_xk", "w_xv", "w_xo", "w_up"), "attn_a": ("w_gate",),
                 "attn_b0": ("w_down",)}
EXCHANGE_STAGES = {"dh2": ("w_down",), "xattn": ("w_up",), "attn_a": ("w_gate", "w_xo", "w_xq"),
                   "attn_b0": ("w_xk", "w_xv", "w_mix_out"), "dh0": ("w_in",)}


def _full_weight(k, g4):
    return g4.reshape(N_CHIPS * g4.shape[1], g4.shape[2])


def _grad_parts(k, gk):
    gk = gk.astype(BF16)
    return gk.reshape(N_CHIPS, gk.shape[0] // N_CHIPS, gk.shape[1])


EARLY_SWAP_STAGE = "attn_b2"


class _Plan:
    def __init__(self, shards):
        self.shards = shards
        self.recv = {}
        self.chip_sums = {}
        self.sibling_sums = {}

    def gather(self, stage):
        names = GATHER_STAGES.get(stage)
        return _ChipGather([self.shards[k] for k in names]) if names else None

    def gathered(self, stage, couts, wb):
        for k, g4 in zip(GATHER_STAGES.get(stage, ()), couts):
            wb[k] = _full_weight(k, g4)

    def exchange(self, stage, grads):
        if stage == EARLY_SWAP_STAGE:
            self.early = [k for k in BIG if k in self.recv]
            for k in self.early:
                self.chip_sums[k] = _sum_slots(self.recv[k], name=f"sum_chips_{k}")
            return _SiblingSwap([self.chip_sums[k] for k in self.early])
        names = EXCHANGE_STAGES.get(stage)
        return _ChipExchange([_grad_parts(k, grads[k]) for k in names]) if names else None

    def exchanged(self, stage, couts):
        if stage == EARLY_SWAP_STAGE:
            self.sibling_sums.update(zip(self.early, couts))
            return
        for k, r4 in zip(EXCHANGE_STAGES.get(stage, ()), couts):
            self.recv[k] = r4


def _pack_rows(a):
    r, n = a.shape
    per = -(-n // PACK_COLS)
    return jnp.pad(a, ((0, 0), (0, per * PACK_COLS - n))).reshape(r * per, PACK_COLS)


def _unpack_rows(p, r, n):
    per = -(-n // PACK_COLS)
    return p.reshape(r, per * PACK_COLS)[:, :n]


def _pack(pieces, rows_total):
    cat = jnp.concatenate([_pack_rows(a) for a in pieces], axis=0)
    return jnp.pad(cat, ((0, rows_total - cat.shape[0]), (0, 0)))


def _unpack(p, shapes):
    out, at = [], 0
    for r, n in shapes:
        per = -(-n // PACK_COLS)
        out.append(_unpack_rows(p[at:at + r * per], r, n))
        at += r * per
    return out


def kernel(x, mem, positions, ln_in_g, ln_in_b, w_in, attn_sink, g_win, g_dil, w_mix_out, ln1_g, ln1_b, mem_ln_g, mem_ln_b, w_xq, w_xk, w_xv, w_xo, ln2_g, ln2_b, w_gate, w_up, conv_w, conv_b, w_down, ln3_g, ln3_b, loss_target, m_ln_in_g, m_ln_in_b, m_w_in, m_attn_sink, m_g_win, m_g_dil, m_w_mix_out, m_ln1_g, m_ln1_b, m_mem_ln_g, m_mem_ln_b, m_w_xq, m_w_xk, m_w_xv, m_w_xo, m_ln2_g, m_ln2_b, m_w_gate, m_w_up, m_conv_w, m_conv_b, m_w_down, m_ln3_g, m_ln3_b, v_ln_in_g, v_ln_in_b, v_w_in, v_attn_sink, v_g_win, v_g_dil, v_w_mix_out, v_ln1_g, v_ln1_b, v_mem_ln_g, v_mem_ln_b, v_w_xq, v_w_xk, v_w_xv, v_w_xo, v_ln2_g, v_ln2_b, v_w_gate, v_w_up, v_conv_w, v_conv_b, v_w_down, v_ln3_g, v_ln3_b):
    given = dict(locals())
    shape_of = {k: given[k].shape for k in WEIGHTS}
    as2d = lambda k, a: a.reshape(-1, a.shape[-1]).T if k in COL_SHARDED else a.reshape(-1, a.shape[-1])
    w2 = {k: as2d(k, given[k]) for k in WEIGHTS}
    m2 = {k: as2d(k, given["m_" + k]) for k in WEIGHTS}
    v2 = {k: as2d(k, given["v_" + k]) for k in WEIGHTS}
    chip = 2 * lax.axis_index("x") + lax.axis_index("y")

    plan = _Plan({k: w2[k].astype(BF16) for k in BIG})
    conv_pack = jnp.pad(w2["conv_w"], ((0, 16 - CONV_WIDTH_ROWS), (0, PACK_COLS - CONV_SHARD)))
    g_in, g_conv = _comm_only(_ChipGather([plan.shards["w_in"], conv_pack]), "gather_w_in")
    wb = {"w_in": _full_weight("w_in", g_in)}
    conv_full = g_conv[:, :CONV_WIDTH_ROWS, :CONV_SHARD].transpose(1, 0, 2).reshape(CONV_WIDTH_ROWS, D_FF)
    sp = {k: w2[k] for k in SMALL}
    sp["conv_w"] = conv_full

    grad_x, grads, small = _local_step(x[0], mem[0], positions[0], loss_target[0], wb, sp, plan)

    small_keys = ("loss",) + SMALL
    small_shapes = [small[k].shape for k in small_keys]
    small_pack = _pack([small[k] for k in small_keys], SMALL_ROWS)
    late = [k for k in BIG if k not in plan.chip_sums]
    for k in late:
        plan.chip_sums[k] = _sum_slots(plan.recv[k], name=f"sum_chips_{k}")
    *late_sibling, small_all = _comm_only(
        _Both(_SiblingSwap([plan.chip_sums[k] for k in late]), _ChipExchange([], small_pack)), "swap_and_small")
    plan.sibling_sums.update(zip(late, late_sibling))
    chip_sums = [plan.chip_sums[k] for k in BIG]
    sibling_sums = [plan.sibling_sums[k] for k in BIG]
    small_sum = _sum_slots(small_all, name="sum_small")
    small_g = dict(zip(small_keys, _unpack(small_sum, small_shapes)))
    loss = small_g["loss"][0, 0]

    res = {}
    for k, p, q in zip(BIG, chip_sums, sibling_sums):
        res[k] = _adamw(w2[k], m2[k], v2[k], p, q, name=f"adamw_{k}")
    small_g["conv_w"] = lax.dynamic_slice_in_dim(small_g["conv_w"], chip * CONV_SHARD, CONV_SHARD, axis=1)
    adam_shapes = [w2[k].shape for k in SMALL]
    packs = [_pack([d[k] for k in SMALL], SMALL_ROWS) for d in (w2, m2, v2, small_g)]
    small_res = [_unpack(o, adam_shapes) for o in _adamw(*packs, None, name="adamw_small")]
    for i, k in enumerate(SMALL):
        res[k] = tuple(o[i] for o in small_res)

    outs = [loss, grad_x[None]]
    for slot in range(4):
        outs += [(res[k][slot].T if k in COL_SHARDED else res[k][slot]).reshape(shape_of[k]) for k in WEIGHTS]
    return tuple(outs)
```

```python
import functools
import math

import jax
import jax.numpy as jnp
from jax import lax
from jax.experimental import pallas as pl
from jax.experimental.pallas import tpu as pltpu

F32 = jnp.float32
BF16 = jnp.bfloat16

D_MODEL = 1024
HEAD_DIM = 64
WIN_Q_HEADS = 8
WIN_KV_HEADS = 2
WIN_HALF = 128
DIL_SLOTS = 8
DILATIONS = (1, 4, 16)
DIL_HALF = 64
ROT_DIM = 16
ROPE_THETA = 500000.0
X_HEADS = 4
X_HEAD_DIM = 256
D_FF = 2816
A_Q = 512
A_KV = 128
A_WIDTH = A_Q + 2 * A_KV
B_QKV = 1536
IN_WIDTH = 5376
ALPHA = 2.0 ** 0.25
LN_EPS = 1e-5
NEG_INF = -1e30
LANES = 128
N_CHIPS = 4
N_DEV = 8

ADAM_LR = 0.001
ADAM_B1 = 0.9
ADAM_B2 = 0.999
ADAM_EPS = 1e-08
ADAM_WD = 0.01
ADAM_STEP = 10

VMEM_LIMIT = 56 * 1024 * 1024


def _cparams(**kw):
    return pltpu.CompilerParams(vmem_limit_bytes=VMEM_LIMIT, **kw)


def _dot(a, b):
    return lax.dot_general(a, b, (((1,), (0,)), ((), ())), preferred_element_type=F32)


def _dot_nt(a, b):
    return lax.dot_general(a, b, (((1,), (1,)), ((), ())), preferred_element_type=F32)


def _dot_tn(a, b):
    return lax.dot_general(a, b, (((0,), (0,)), ((), ())), preferred_element_type=F32)


def _ln(x, g, b):
    mu = jnp.mean(x, axis=-1, keepdims=True)
    xc = x - mu
    var = jnp.mean(xc * xc, axis=-1, keepdims=True)
    return xc * lax.rsqrt(var + LN_EPS) * g + b


def _ln_bwd_math(dy, r, g):
    mu = jnp.mean(r, axis=-1, keepdims=True)
    xc = r - mu
    var = jnp.mean(xc * xc, axis=-1, keepdims=True)
    rstd = lax.rsqrt(var + LN_EPS)
    xhat = xc * rstd
    dxhat = dy * g
    m1 = jnp.mean(dxhat, axis=-1, keepdims=True)
    m2 = jnp.mean(dxhat * xhat, axis=-1, keepdims=True)
    dr = rstd * (dxhat - m1 - xhat * m2)
    return dr, jnp.sum(dy * xhat, axis=0, keepdims=True), jnp.sum(dy, axis=0, keepdims=True)


def _rope(z, ta, tb, tc, sign):
    w = z.shape[1]
    reps = w // LANES
    a = jnp.tile(ta, (1, reps))
    b = jnp.tile(tb, (1, reps))
    c = jnp.tile(tc, (1, reps))
    return z * a + sign * (pltpu.roll(z, w - 8, 1) * b + pltpu.roll(z, 8, 1) * c)


def _shift_rows(x, prev_row, next_row):
    t = x.shape[0]
    row = lax.broadcasted_iota(jnp.int32, x.shape, 0)
    xm1 = jnp.where(row == 0, prev_row, pltpu.roll(x, 1, 0))
    xp1 = jnp.where(row == t - 1, next_row, pltpu.roll(x, t - 1, 0))
    return xm1, xp1


def _rope_tabs(cs, e_mat):
    hi = cs.astype(BF16)
    rest = cs - hi.astype(F32)
    mid = rest.astype(BF16)
    lo = (rest - mid.astype(F32)).astype(BF16)
    tabs = _dot(hi, e_mat) + _dot(mid, e_mat) + _dot(lo, e_mat)
    lane = lax.broadcasted_iota(jnp.int32, (cs.shape[0], LANES), 1)
    ones = jnp.where((lane & (HEAD_DIM - 1)) >= ROT_DIM, 1.0, 0.0)
    return tabs[:, :LANES] + ones, tabs[:, LANES:2 * LANES], tabs[:, 2 * LANES:]


def _rope_select_matrix():
    half = ROT_DIM // 2
    e = [[0.0] * (3 * LANES) for _ in range(ROT_DIM)]
    for lane in range(LANES):
        d = lane % HEAD_DIM
        if d < half:
            e[d][lane] = 1.0
            e[half + d][LANES + lane] = -1.0
        elif d < ROT_DIM:
            e[d - half][lane] = 1.0
            e[d][2 * LANES + lane] = 1.0
    return jnp.array(e, BF16)


def _rope_rows(x, cos_t, sin_t, sign):
    half = ROT_DIM // 2
    parts = []
    for base in (0, HEAD_DIM):
        r1, r2 = x[base:base + half], x[base + half:base + ROT_DIM]
        parts += [r1 * cos_t - sign * (r2 * sin_t), r2 * cos_t + sign * (r1 * sin_t), x[base + ROT_DIM:base + HEAD_DIM]]
    return jnp.concatenate(parts, axis=0)


MESH_IDS = pl.DeviceIdType.MESH
ANY = pl.BlockSpec(memory_space=pl.ANY)


def _place():
    x, y, c = lax.axis_index("x"), lax.axis_index("y"), lax.axis_index("c")
    other_chips = [(1 - x, y), (x, 1 - y), (1 - x, 1 - y)]
    return x, y, c, other_chips


class _ChipGather:
    def __init__(self, shards):
        self.inputs = list(shards)
        n = len(shards)
        self.out_shape = [jax.ShapeDtypeStruct((N_CHIPS,) + a.shape, a.dtype) for a in shards]
        self.scratch = [pltpu.SemaphoreType.DMA((6 * n,)), pltpu.SemaphoreType.DMA((6 * n,)),
                        pltpu.SemaphoreType.DMA((n,))]

    def _copies(self, src, dst, sems):
        send_sems, recv_sems, local_sems = sems
        x, y, c, chips = _place()
        mine = 2 * x + y
        n = len(src)
        local, sends, recvs, passes, pass_recvs = [], [], [], [], []
        for a in range(n):
            half = src[a].shape[0] // 2
            my_rows, other_rows = pl.ds(c * half, half), pl.ds((1 - c) * half, half)
            local.append(pltpu.make_async_copy(src[a], dst[a].at[mine], local_sems.at[a]))
            for j, (px, py) in enumerate(chips):
                k, k2, slot = 3 * a + j, 3 * n + 3 * a + j, 2 * px + py
                sends.append(pltpu.make_async_remote_copy(
                    src_ref=src[a].at[my_rows], dst_ref=dst[a].at[mine, my_rows], send_sem=send_sems.at[k],
                    recv_sem=recv_sems.at[k], device_id=(px, py, c), device_id_type=MESH_IDS))
                recvs.append(pltpu.make_async_remote_copy(
                    src_ref=src[a].at[my_rows], dst_ref=dst[a].at[slot, my_rows], send_sem=send_sems.at[k],
                    recv_sem=recv_sems.at[k], device_id=(px, py, c), device_id_type=MESH_IDS))
                passes.append(pltpu.make_async_remote_copy(
                    src_ref=dst[a].at[slot, my_rows], dst_ref=dst[a].at[slot, my_rows], send_sem=send_sems.at[k2],
                    recv_sem=recv_sems.at[k2], device_id=(x, y, 1 - c), device_id_type=MESH_IDS))
                pass_recvs.append(pltpu.make_async_remote_copy(
                    src_ref=dst[a].at[slot, my_rows], dst_ref=dst[a].at[slot, other_rows],
                    send_sem=send_sems.at[k2], recv_sem=recv_sems.at[k2], device_id=(x, y, 1 - c),
                    device_id_type=MESH_IDS))
        return local, sends, recvs, passes, pass_recvs

    def start(self, src, dst, sems):
        local, sends, _, _, _ = self._copies(src, dst, sems)
        for cp in local + sends:
            cp.start()

    def wait(self, src, dst, sems):
        local, sends, recvs, passes, pass_recvs = self._copies(src, dst, sems)
        for idx, landed in enumerate(recvs):
            landed.wait_recv()
            if passes:
                passes[idx].start()
        for cp in pass_recvs:
            cp.wait_recv()
        for cp in sends + passes:
            cp.wait_send()
        for cp in local:
            cp.wait()


class _ChipExchange:
    def __init__(self, parts, small=None):
        self.inputs = list(parts) + ([small] if small is not None else [])
        self.n = len(parts)
        self.has_small = small is not None
        self.out_shape = [jax.ShapeDtypeStruct(a.shape, a.dtype) for a in parts]
        n_sem, n_loc = 3 * self.n, self.n
        if self.has_small:
            self.out_shape.append(jax.ShapeDtypeStruct((N_DEV,) + small.shape, small.dtype))
            n_sem, n_loc = n_sem + N_DEV - 1, n_loc + 1
        self.scratch = [pltpu.SemaphoreType.DMA((n_sem,)), pltpu.SemaphoreType.DMA((n_sem,)),
                        pltpu.SemaphoreType.DMA((n_loc,))]

    def _copies(self, src, dst, sems):
        send_sems, recv_sems, local_sems = sems
        x, y, c, chips = _place()
        mine = 2 * x + y
        n = self.n
        local, sends, recvs = [], [], []
        for a in range(n):
            local.append(pltpu.make_async_copy(src[a].at[mine], dst[a].at[mine], local_sems.at[a]))
            for j, (px, py) in enumerate(chips):
                k = 3 * a + j
                sends.append(pltpu.make_async_remote_copy(
                    src_ref=src[a].at[2 * px + py], dst_ref=dst[a].at[mine], send_sem=send_sems.at[k],
                    recv_sem=recv_sems.at[k], device_id=(px, py, c), device_id_type=MESH_IDS))
                recvs.append(pltpu.make_async_remote_copy(
                    src_ref=src[a].at[mine], dst_ref=dst[a].at[2 * px + py], send_sem=send_sems.at[k],
                    recv_sem=recv_sems.at[k], device_id=(px, py, c), device_id_type=MESH_IDS))
        if self.has_small:
            me_dev = 4 * x + 2 * y + c
            local.append(pltpu.make_async_copy(src[n], dst[n].at[me_dev], local_sems.at[n]))
            for mask in range(1, N_DEV):
                px, py, pc = x ^ ((mask >> 2) & 1), y ^ ((mask >> 1) & 1), c ^ (mask & 1)
                k = 3 * n + mask - 1
                sends.append(pltpu.make_async_remote_copy(
                    src_ref=src[n], dst_ref=dst[n].at[me_dev], send_sem=send_sems.at[k], recv_sem=recv_sems.at[k],
                    device_id=(px, py, pc), device_id_type=MESH_IDS))
                recvs.append(pltpu.make_async_remote_copy(
                    src_ref=src[n], dst_ref=dst[n].at[4 * px + 2 * py + pc], send_sem=send_sems.at[k],
                    recv_sem=recv_sems.at[k], device_id=(px, py, pc), device_id_type=MESH_IDS))
        return local, sends, recvs, [], []

    start = _ChipGather.start
    wait = _ChipGather.wait


def _pcall(body, *, name, grid, in_specs, out_specs, out_shape, args, scratch_shapes=(), dims=None, comm=None):
    in_specs, out_specs, out_shape = list(in_specs), list(out_specs), list(out_shape)
    scratch_shapes = list(scratch_shapes)
    if comm is None:
        outs = pl.pallas_call(
            body, name=name, grid=grid, in_specs=in_specs, out_specs=out_specs, out_shape=out_shape,
            scratch_shapes=scratch_shapes, compiler_params=_cparams(dimension_semantics=dims),
        )(*args)
        return list(outs), []
    n_in, n_out, n_scr = len(in_specs), len(out_specs), len(scratch_shapes)
    n_cin, n_cout = len(comm.inputs), len(comm.out_shape)

    def wrapped(*refs):
        ins, refs = refs[:n_in], refs[n_in:]
        cins, refs = refs[:n_cin], refs[n_cin:]
        outs, refs = refs[:n_out], refs[n_out:]
        couts, refs = refs[:n_cout], refs[n_cout:]
        scr, csems = refs[:n_scr], refs[n_scr:]
        first = last = None
        for axis, size in enumerate(grid):
            pid = pl.program_id(axis)
            f, l = pid == 0, pid == size - 1
            first = f if first is None else first & f
            last = l if last is None else last & l

        @pl.when(first)
        def _():
            comm.start(cins, couts, csems)

        body(*ins, *outs, *scr)

        @pl.when(last)
        def _():
            comm.wait(cins, couts, csems)

    res = pl.pallas_call(
        wrapped, name=name, grid=grid, in_specs=in_specs + [ANY] * n_cin, out_specs=out_specs + [ANY] * n_cout,
        out_shape=out_shape + list(comm.out_shape), scratch_shapes=scratch_shapes + list(comm.scratch),
        compiler_params=_cparams(dimension_semantics=("arbitrary",) * len(grid)),
    )(*args, *comm.inputs)
    return list(res[:n_out]), list(res[n_out:])


def _comm_only(comm, name):
    def body(*refs):
        n_cin, n_cout = len(comm.inputs), len(comm.out_shape)
        cins, couts, csems = refs[:n_cin], refs[n_cin:n_cin + n_cout], refs[n_cin + n_cout:]
        comm.start(cins, couts, csems)
        comm.wait(cins, couts, csems)

    return list(pl.pallas_call(
        body, name=name, in_specs=[ANY] * len(comm.inputs), out_specs=[ANY] * len(comm.out_shape),
        out_shape=list(comm.out_shape), scratch_shapes=list(comm.scratch),
    )(*comm.inputs))


def _mm(a, b, *, mode, out_dtype, tm, tn, tk=None, add=None, add_scale=1.0, name, comm=None):
    if mode in ("nn", "nt"):
        m, k = a.shape
        n = b.shape[1] if mode == "nn" else b.shape[0]
        assert m % tm == 0 and n % tn == 0
        dot = _dot if mode == "nn" else _dot_nt

        def body(*refs):
            if add is None:
                a_ref, b_ref, o_ref = refs
                o_ref[...] = dot(a_ref[...], b_ref[...]).astype(out_dtype)
            else:
                a_ref, b_ref, c_ref, o_ref = refs
                o_ref[...] = (dot(a_ref[...], b_ref[...]) + add_scale * c_ref[...]).astype(out_dtype)

        b_spec = (pl.BlockSpec((k, tn), lambda i, j: (0, j)) if mode == "nn"
                  else pl.BlockSpec((tn, k), lambda i, j: (j, 0)))
        in_specs = [pl.BlockSpec((tm, k), lambda i, j: (i, 0)), b_spec]
        args = [a, b]
        if add is not None:
            in_specs.append(pl.BlockSpec((tm, tn), lambda i, j: (i, j)))
            args.append(add)
        outs, couts = _pcall(
            body, name=name, grid=(m // tm, n // tn), in_specs=in_specs,
            out_specs=[pl.BlockSpec((tm, tn), lambda i, j: (i, j))],
            out_shape=[jax.ShapeDtypeStruct((m, n), out_dtype)], args=args, dims=("parallel", "parallel"),
            comm=comm)
        return outs[0] if comm is None else (outs[0], couts)
    assert mode == "tn" and add is None and comm is None
    kk, m = a.shape
    n = b.shape[1]
    assert m % tm == 0 and n % tn == 0 and kk % tk == 0
    nk = kk // tk

    def body(a_ref, b_ref, o_ref, acc_ref):
        kstep = pl.program_id(2)

        @pl.when(kstep == 0)
        def _():
            acc_ref[...] = jnp.zeros_like(acc_ref)

        acc_ref[...] += _dot_tn(a_ref[...], b_ref[...])

        @pl.when(kstep == nk - 1)
        def _():
            o_ref[...] = acc_ref[...].astype(out_dtype)

    return pl.pallas_call(
        body, name=name, grid=(m // tm, n // tn, nk),
        in_specs=[pl.BlockSpec((tk, tm), lambda i, j, s: (s, i)), pl.BlockSpec((tk, tn), lambda i, j, s: (s, j))],
        out_specs=pl.BlockSpec((tm, tn), lambda i, j, s: (i, j)),
        out_shape=jax.ShapeDtypeStruct((m, n), out_dtype),
        scratch_shapes=[pltpu.VMEM((tm, tn), F32)],
        compiler_params=_cparams(dimension_semantics=("parallel", "parallel", "arbitrary")),
    )(a, b)


def _ln_bwd(dy, r, g, *, t, name, want_bf16):
    s = r.shape[0]

    def body(dy_ref, r_ref, g_ref, *outs):
        i = pl.program_id(0)
        dr, dg, db = _ln_bwd_math(dy_ref[...], r_ref[...], g_ref[...])
        outs[0][...] = dr
        if want_bf16:
            outs[1][...] = dr.astype(BF16)
        st_ref = outs[-1]

        @pl.when(i == 0)
        def _():
            st_ref[...] = jnp.zeros_like(st_ref)

        st_ref[0:1, :] += dg
        st_ref[1:2, :] += db

    tile = pl.BlockSpec((t, D_MODEL), lambda i: (i, 0))
    out_specs = [tile] + ([tile] if want_bf16 else []) + [pl.BlockSpec((8, D_MODEL), lambda i: (0, 0))]
    out_shape = ([jax.ShapeDtypeStruct((s, D_MODEL), F32)]
                 + ([jax.ShapeDtypeStruct((s, D_MODEL), BF16)] if want_bf16 else [])
                 + [jax.ShapeDtypeStruct((8, D_MODEL), F32)])
    return pl.pallas_call(
        body, name=name, grid=(s // t,),
        in_specs=[tile, tile, pl.BlockSpec((1, D_MODEL), lambda i: (0, 0))],
        out_specs=out_specs, out_shape=out_shape,
        compiler_params=_cparams(dimension_semantics=("arbitrary",)),
    )(dy, r, g)


PROJ_COLS = 256


def _proj_segments():
    wd = DIL_SLOTS * HEAD_DIM
    segs = [(1, [(0, 1), (PROJ_COLS, 1), (2 * PROJ_COLS, 2)])]
    for gi, dil in enumerate(DILATIONS):
        blocks = []
        for part, kind in enumerate((1, 1, 0)):
            col = A_WIDTH + part * B_QKV + gi * wd
            blocks += [(col, kind), (col + PROJ_COLS, kind)]
        segs.append((dil, blocks))
    return segs


PROJ_SEGMENTS = _proj_segments()


def _ln_in_fwd(x, g, b, *, t, comm=None):
    s = x.shape[0]

    def body(x_ref, g_ref, b_ref, o_ref):
        o_ref[...] = _ln(x_ref[...], g_ref[...], b_ref[...]).astype(BF16)

    row = pl.BlockSpec((1, D_MODEL), lambda i: (0, 0))
    tile = pl.BlockSpec((t, D_MODEL), lambda i: (i, 0))
    outs, couts = _pcall(body, name="ln_in_fwd", grid=(s // t,), in_specs=[tile, row, row], out_specs=[tile],
                         out_shape=[jax.ShapeDtypeStruct((s, D_MODEL), BF16)], args=[x, g, b], dims=("parallel",),
                         comm=comm)
    return outs[0], couts


def _proj_all(h0b, w_t, cs, e_mat, *, t, comm=None):
    s = h0b.shape[0]
    cb = PROJ_COLS
    halves = cb // LANES

    def body(h_ref, w_ref, cs_ref, e_ref, *rest):
        z_refs, scr = rest[:-1], rest[-1]
        h = h_ref[...]
        ta, tb, tc = (jnp.tile(tab, (1, halves)) for tab in _rope_tabs(cs_ref[...], e_ref[...]))
        lane = lax.broadcasted_iota(jnp.int32, (t, cb), 1)
        slot = 0
        for z_ref, (dil, blocks) in zip(z_refs, PROJ_SEGMENTS):
            for jb, (col, kind) in enumerate(blocks):
                acc = _dot_nt(h, w_ref[col:col + cb, :])
                if kind:
                    z = acc * ta + (pltpu.roll(acc, cb - 8, 1) * tb + pltpu.roll(acc, 8, 1) * tc)
                    if kind == 2:
                        z = jnp.where(lane < LANES, z, acc)
                else:
                    z = acc
                if dil == 1:
                    z_ref[0, :, cb * jb:cb * (jb + 1)] = z.astype(BF16)
                    continue
                for half in range(halves):
                    scr[slot, half] = z[:, half * LANES:(half + 1) * LANES]
                for c in range(dil):
                    for half in range(halves):
                        rows = scr[slot, half, pl.ds(c, t // dil, stride=dil), :]
                        z_ref[c, :, cb * jb + half * LANES:cb * jb + (half + 1) * LANES] = rows.astype(BF16)
                slot = 1 - slot

    widths = [cb * len(blocks) for _, blocks in PROJ_SEGMENTS]
    dils = [dil for dil, _ in PROJ_SEGMENTS]
    outs, couts = _pcall(
        body, name="proj_all", grid=(s // t,),
        in_specs=[pl.BlockSpec((t, D_MODEL), lambda i: (i, 0)),
                  pl.BlockSpec((IN_WIDTH, D_MODEL), lambda i: (0, 0), pipeline_mode=pl.Buffered(1)),
                  pl.BlockSpec((t, ROT_DIM), lambda i: (i, 0)), pl.BlockSpec((ROT_DIM, 3 * LANES), lambda i: (0, 0))],
        out_specs=[pl.BlockSpec((dil, t // dil, wd), lambda i: (0, i, 0)) for dil, wd in zip(dils, widths)],
        out_shape=[jax.ShapeDtypeStruct((dil, s // dil, wd), BF16) for dil, wd in zip(dils, widths)],
        args=[h0b, w_t, cs, e_mat], scratch_shapes=[pltpu.VMEM((2, halves, t, LANES), F32)],
        dims=("parallel",), comm=comm)
    return outs, couts


PAIR = 2 * HEAD_DIM


def _place_head(x2, src_pos, dst_pos):
    hi = lax.broadcasted_iota(jnp.int32, x2.shape, 1) >= HEAD_DIM
    src = x2 if src_pos == dst_pos else pltpu.roll(x2, HEAD_DIM, 1)
    return jnp.where(hi == (dst_pos == 1), src, jnp.zeros_like(src))


def _band_mask_t(row0, tq, w, seq_len):
    tk = tq + 2 * w
    kk = lax.broadcasted_iota(jnp.int32, (tk, tq), 0)
    qq = lax.broadcasted_iota(jnp.int32, (tk, tq), 1)
    kpos = row0 - w + kk
    return (jnp.abs(qq + w - kk) <= w) & (kpos >= 0) & (kpos < seq_len)


def _halo_kv_specs(t, w, hkv, n, seq_len, kcol, vcol):
    kw = hkv * HEAD_DIM
    per, last = t // w, seq_len // w - 1
    cur = lambda s, i: jnp.minimum(i, n - 1)
    specs = []
    for c in (kcol, vcol):
        specs += [pl.BlockSpec((None, w, kw), lambda s, i, c=c: (s, jnp.maximum(cur(s, i) * per - 1, 0), c)),
                  pl.BlockSpec((None, t, kw), lambda s, i, c=c: (s, cur(s, i), c)),
                  pl.BlockSpec((None, w, kw), lambda s, i, c=c: (s, jnp.minimum((cur(s, i) + 1) * per, last), c))]
    return specs, cur


def _pair_kv(kfull, vfull, qp, rep, krows):
    ks, vs, a_of = [], [], []
    for pos in range(2):
        g = (2 * qp + pos) // rep
        a_of.append(g // 2)
        ks.append(_place_head(kfull[g // 2][krows], g % 2, pos))
        vs.append(_place_head(vfull[g // 2][krows], g % 2, pos))
    assert a_of[0] == a_of[1]
    return jnp.concatenate(ks, axis=0), jnp.concatenate(vs, axis=0), a_of[0]


def _swa_fwd_p(qkv, *, qcol, kcol, vcol, hq, hkv, w, tq, sub, sink, name, comm=None):
    nseq, seq_len, _ = qkv.shape
    t = tq * sub
    n = seq_len // t
    rep = hq // hkv
    tk = tq + 2 * w
    kv_specs, cur = _halo_kv_specs(t, w, hkv, n, seq_len, kcol, vcol)

    def body(*refs):
        if sink is not None:
            sink_ref, refs = refs[0], refs[1:]
        q_ref, kp_ref, kc_ref, kn_ref, vp_ref, vc_ref, vn_ref, o_ref, lse_ref = refs
        i = pl.program_id(1)
        kfull, vfull = [], []
        for a in range(hkv // 2):
            ls = slice(a * PAIR, (a + 1) * PAIR)
            kfull.append(jnp.concatenate([kp_ref[:, ls], kc_ref[:, ls], kn_ref[:, ls]], axis=0) * 0.125)
            vfull.append(jnp.concatenate([vp_ref[:, ls], vc_ref[:, ls], vn_ref[:, ls]], axis=0))
        row_hi = lax.broadcasted_iota(jnp.int32, (PAIR, tq), 0) >= HEAD_DIM
        for jj in range(sub):
            rows = slice(jj * tq, (jj + 1) * tq)
            mask_t = _band_mask_t(i * t + jj * tq, tq, w, seq_len)
            o_t, lse_rows = [], []
            for qp in range(hq // 2):
                kst, vst, _ = _pair_kv(kfull, vfull, qp, rep, slice(jj * tq, jj * tq + tk))
                s2 = _dot_nt(kst, q_ref[rows, qp * PAIR:(qp + 1) * PAIR])
                ps, dens = [], []
                for pos in range(2):
                    h = 2 * qp + pos
                    s_t = jnp.where(mask_t, s2[pos * tk:(pos + 1) * tk], NEG_INF)
                    m = jnp.max(s_t, axis=0, keepdims=True)
                    if sink is not None:
                        m = jnp.maximum(m, sink_ref[0, h])
                    p_t = jnp.exp(s_t - m)
                    den = jnp.sum(p_t, axis=0, keepdims=True)
                    if sink is not None:
                        den = den + jnp.exp(sink_ref[0, h] - m)
                    ps.append(p_t.astype(BF16))
                    dens.append(den)
                    lse_rows.append(m + jnp.log(den))
                both = _dot_tn(vst, jnp.concatenate(ps, axis=0))
                o_t.append(both / jnp.where(row_hi, dens[1], dens[0]))
            o_ref[rows, :] = jnp.concatenate(o_t, axis=0).T
            lse_ref[:, rows] = jnp.concatenate(lse_rows, axis=0)

    in_specs = [pl.BlockSpec((None, t, hq * HEAD_DIM), lambda s, i: (s, i, qcol))] + kv_specs
    args = [qkv] * 7
    if sink is not None:
        in_specs = [pl.BlockSpec(memory_space=pltpu.SMEM)] + in_specs
        args = [sink] + args
    (o, lse), couts = _pcall(
        body, name=name, grid=(nseq, n), in_specs=in_specs,
        out_specs=[pl.BlockSpec((None, t, hq * HEAD_DIM), lambda s, i: (s, i, 0)),
                   pl.BlockSpec((None, hq, t), lambda s, i: (s, 0, i))],
        out_shape=[jax.ShapeDtypeStruct((nseq, seq_len, hq * HEAD_DIM), F32),
                   jax.ShapeDtypeStruct((nseq, hq, seq_len), F32)],
        args=args, dims=("parallel", "parallel"), comm=comm)
    return o, lse, couts


def _swa_bwd_p(qkv, do, lse, delta, cs, e_mat, *, qcol, kcol, vcol, hq, hkv, w, tq, sub, sink, name, comm=None):
    nseq, seq_len, _ = qkv.shape
    t = tq * sub
    n = seq_len // t
    rep = hq // hkv
    qw, kw = hq * HEAD_DIM, hkv * HEAD_DIM
    tk = tq + 2 * w
    kv_specs, cur = _halo_kv_specs(t, w, hkv, n, seq_len, kcol, vcol)

    def body(*refs):
        if sink is not None:
            sink_ref, refs = refs[0], refs[1:]
        (q_ref, kp_ref, kc_ref, kn_ref, vp_ref, vc_ref, vn_ref, do_ref, lse_ref, dl_ref,
         cs_c, cs_p, e_ref) = refs[:13]
        outs = refs[13:]
        if sink is not None:
            dq_ref, dk_ref, dv_ref, dsink_ref, dk_acc, dv_acc, dk_win, dv_win = outs
        else:
            dq_ref, dk_ref, dv_ref, dk_acc, dv_acc, dk_win, dv_win = outs
        s_id = pl.program_id(0)
        i = pl.program_id(1)
        slot_p, slot_c, slot_n = (i + 2) % 3, i % 3, (i + 1) % 3

        if sink is not None:
            @pl.when((s_id == 0) & (i == 0))
            def _():
                dsink_ref[...] = jnp.zeros_like(dsink_ref)

        @pl.when(i < n)
        def _():
            dk_win[...] = jnp.zeros_like(dk_win)
            dv_win[...] = jnp.zeros_like(dv_win)
            kfull, vfull = [], []
            for a in range(hkv // 2):
                ls = slice(a * PAIR, (a + 1) * PAIR)
                kfull.append(jnp.concatenate([kp_ref[:, ls], kc_ref[:, ls], kn_ref[:, ls]], axis=0) * 0.125)
                vfull.append(jnp.concatenate([vp_ref[:, ls], vc_ref[:, ls], vn_ref[:, ls]], axis=0))
            for jj in range(sub):
                rows = slice(jj * tq, (jj + 1) * tq)
                krows = slice(jj * tq, jj * tq + tk)
                mask_t = _band_mask_t(i * t + jj * tq, tq, w, seq_len)
                dq_t = []
                dk2 = [None] * (hkv // 2)
                dv2 = [None] * (hkv // 2)
                for qp in range(hq // 2):
                    kst, vst, a = _pair_kv(kfull, vfull, qp, rep, krows)
                    q2 = q_ref[rows, qp * PAIR:(qp + 1) * PAIR]
                    do2 = do_ref[rows, qp * PAIR:(qp + 1) * PAIR]
                    s2 = _dot_nt(kst, q2)
                    dp2 = _dot_nt(vst, do2)
                    ds, ps, q_at, do_at = [], [], [], []
                    for pos in range(2):
                        h = 2 * qp + pos
                        e = (h // rep) % 2
                        half = slice(pos * tk, (pos + 1) * tk)
                        lse_h = lse_ref[h:h + 1, rows]
                        dl_h = dl_ref[h:h + 1, rows]
                        p_t = jnp.exp(jnp.where(mask_t, s2[half], NEG_INF) - lse_h)
                        ds.append((p_t * (dp2[half] - dl_h)).astype(BF16))
                        ps.append(p_t.astype(BF16))
                        q_at.append(_place_head(q2, pos, e) * 0.125)
                        do_at.append(_place_head(do2, pos, e))
                        if sink is not None:
                            ds_sink = -jnp.sum(jnp.exp(sink_ref[0, h] - lse_h) * dl_h)
                            dsink_ref[h:h + 1, :] += jnp.full((1, LANES), ds_sink, F32)
                    dq_t.append(_rope_rows(_dot_tn(kst, jnp.concatenate(ds, axis=0)),
                                           cs_c[0:ROT_DIM // 2, rows], cs_c[ROT_DIM // 2:ROT_DIM, rows], -1.0))
                    dk_part = _dot(jnp.concatenate(ds, axis=1), jnp.concatenate(q_at, axis=0))
                    dv_part = _dot(jnp.concatenate(ps, axis=1), jnp.concatenate(do_at, axis=0))
                    dk2[a] = dk_part if dk2[a] is None else dk2[a] + dk_part
                    dv2[a] = dv_part if dv2[a] is None else dv2[a] + dv_part
                for a in range(hkv // 2):
                    ls = slice(a * PAIR, (a + 1) * PAIR)
                    dk_win[krows, ls] += dk2[a]
                    dv_win[krows, ls] += dv2[a]
                dq_ref[rows, :] = jnp.concatenate(dq_t, axis=0).T.astype(BF16)

            @pl.when(i > 0)
            def _():
                dk_acc[slot_p, t - w:, :] += dk_win[:w, :]
                dv_acc[slot_p, t - w:, :] += dv_win[:w, :]

            @pl.when(i == 0)
            def _():
                dk_acc[slot_c] = dk_win[w:w + t, :]
                dv_acc[slot_c] = dv_win[w:w + t, :]

            @pl.when(i > 0)
            def _():
                dk_acc[slot_c] += dk_win[w:w + t, :]
                dv_acc[slot_c] += dv_win[w:w + t, :]

            dk_acc[slot_n] = jnp.zeros((t, kw), F32)
            dv_acc[slot_n] = jnp.zeros((t, kw), F32)
            dk_acc[slot_n, :w, :] = dk_win[w + t:, :]
            dv_acc[slot_n, :w, :] = dv_win[w + t:, :]

        @pl.when(i >= 1)
        def _():
            dk_ref[...] = _rope(dk_acc[slot_p], *_rope_tabs(cs_p[...], e_ref[...]), -1.0).astype(BF16)
            dv_ref[...] = dv_acc[slot_p].astype(BF16)

    row_c = lambda width: pl.BlockSpec((None, t, width), lambda s, i: (s, cur(s, i), 0))
    row_p = lambda width: pl.BlockSpec((None, t, width), lambda s, i: (s, jnp.maximum(i - 1, 0), 0))
    stat = pl.BlockSpec((None, hq, t), lambda s, i: (s, 0, cur(s, i)))
    cs_rows = pl.BlockSpec((None, ROT_DIM, t), lambda s, i: (s, 0, cur(s, i)))
    in_specs = ([pl.BlockSpec((None, t, qw), lambda s, i: (s, cur(s, i), qcol))] + kv_specs
                + [row_c(qw), stat, stat, cs_rows, row_p(ROT_DIM),
                   pl.BlockSpec((ROT_DIM, 3 * LANES), lambda s, i: (0, 0))])
    args = [qkv] * 7 + [do, lse, delta, cs.transpose(0, 2, 1), cs, e_mat]
    out_specs = [row_c(qw), row_p(kw), row_p(kw)]
    out_shape = [jax.ShapeDtypeStruct((nseq, seq_len, qw), BF16),
                 jax.ShapeDtypeStruct((nseq, seq_len, kw), BF16),
                 jax.ShapeDtypeStruct((nseq, seq_len, kw), BF16)]
    if sink is not None:
        in_specs = [pl.BlockSpec(memory_space=pltpu.SMEM)] + in_specs
        args = [sink] + args
        out_specs.append(pl.BlockSpec((8, LANES), lambda s, i: (0, 0)))
        out_shape.append(jax.ShapeDtypeStruct((8, LANES), F32))
    return _pcall(
        body, name=name, grid=(nseq, n + 1), in_specs=in_specs, out_specs=out_specs, out_shape=out_shape,
        scratch_shapes=[pltpu.VMEM((3, t, kw), F32), pltpu.VMEM((3, t, kw), F32),
                        pltpu.VMEM((t + 2 * w, kw), F32), pltpu.VMEM((t + 2 * w, kw), F32)], args=args,
        dims=("arbitrary", "arbitrary"), comm=comm)


def _rms_parts(o, g):
    ms = jnp.mean(o * o, axis=-1, keepdims=True) + LN_EPS
    rinv = lax.rsqrt(ms)
    return o * rinv * g, rinv


def _from_subsequences(ref, scr, dil, t):
    slabs = ref.shape[-1] // LANES
    if dil == 1:
        return ref[0].astype(F32)
    for c in range(dil):
        for sl in range(slabs):
            scr[sl, pl.ds(c, t // dil, stride=dil), :] = ref[c, :, sl * LANES:(sl + 1) * LANES].astype(F32)
    return jnp.concatenate([scr[sl] for sl in range(slabs)], axis=1)


def _to_subsequences(val, ref, scr, dil, t):
    slabs = val.shape[-1] // LANES
    if dil == 1:
        ref[0] = val.astype(ref.dtype)
        return
    for sl in range(slabs):
        scr[sl] = val[:, sl * LANES:(sl + 1) * LANES]
    for c in range(dil):
        for sl in range(slabs):
            ref[c, :, sl * LANES:(sl + 1) * LANES] = scr[sl, pl.ds(c, t // dil, stride=dil), :].astype(ref.dtype)


def _combine_fwd(out_a, o_g, lse_g, g_win, g_dil, w_mix_b, x, ln_in_g, ln_in_b, ln1_g, ln1_b, *, t):
    s = out_a.shape[1]
    wd = DIL_SLOTS * HEAD_DIM

    def body(oa_ref, o0, o1, o2, l0, l1, l2, gw_ref, gd_ref, w_ref, x_ref, g0, b0, g1, b1,
             mixed_ref, ob_ref, lt_ref, r1_ref, h1_ref, scr):
        ls = [l0[...], l1[...], l2[...]]
        mx = jnp.maximum(jnp.maximum(ls[0], ls[1]), ls[2])
        ws = [jnp.exp(l - mx) for l in ls]
        tot = ws[0] + ws[1] + ws[2]
        lt_ref[...] = mx + jnp.log(tot)
        ws = [x / tot for x in ws]
        og = [_from_subsequences(o_ref, scr.at[gi], dil, t)
              for gi, (o_ref, dil) in enumerate(zip((o0, o1, o2), DILATIONS))]
        parts = []
        for h in range(DIL_SLOTS):
            hs = slice(h * HEAD_DIM, (h + 1) * HEAD_DIM)
            parts.append(ws[0][:, h:h + 1] * og[0][:, hs] + ws[1][:, h:h + 1] * og[1][:, hs]
                         + ws[2][:, h:h + 1] * og[2][:, hs])
        ob = jnp.concatenate(parts, axis=1)
        ob_ref[...] = ob
        na, _ = _rms_parts(oa_ref[...], gw_ref[...])
        nb, _ = _rms_parts(ob, gd_ref[...])
        mixed = jnp.concatenate([na.astype(BF16), nb.astype(BF16)], axis=1)
        mixed_ref[...] = mixed
        h0 = _ln(x_ref[...], g0[...], b0[...])
        r1 = ALPHA * h0 + _dot(mixed, w_ref[...])
        r1_ref[...] = r1
        h1_ref[...] = _ln(r1, g1[...], b1[...]).astype(BF16)

    half = pl.BlockSpec((t, wd), lambda i: (i, 0))
    full = pl.BlockSpec((t, D_MODEL), lambda i: (i, 0))
    lanes = pl.BlockSpec((t, LANES), lambda i: (i, 0))
    grow = pl.BlockSpec((1, wd), lambda i: (0, 0))
    row = pl.BlockSpec((1, D_MODEL), lambda i: (0, 0))
    subseq = [pl.BlockSpec((dil, t // dil, wd), lambda i: (0, i, 0)) for dil in DILATIONS]
    return pl.pallas_call(
        body, name="combine_fwd", grid=(s // t,),
        in_specs=[pl.BlockSpec((None, t, wd), lambda i: (0, i, 0))] + subseq
        + [lanes, lanes, lanes, grow, grow, pl.BlockSpec((D_MODEL, D_MODEL), lambda i: (0, 0)), full,
           row, row, row, row],
        out_specs=[full, half, lanes, full, full],
        out_shape=[jax.ShapeDtypeStruct((s, D_MODEL), BF16), jax.ShapeDtypeStruct((s, wd), F32),
                   jax.ShapeDtypeStruct((s, LANES), F32), jax.ShapeDtypeStruct((s, D_MODEL), F32),
                   jax.ShapeDtypeStruct((s, D_MODEL), BF16)],
        scratch_shapes=[pltpu.VMEM((len(DILATIONS), wd // LANES, t, LANES), F32)],
        compiler_params=_cparams(dimension_semantics=("parallel",)),
    )(out_a, *o_g, *lse_g, g_win, g_dil, w_mix_b, x, ln_in_g, ln_in_b, ln1_g, ln1_b)


def _combine_bwd(dr1b, w_mix_b, out_a, out_b, g_win, g_dil, *, t):
    s = out_b.shape[0]
    wd = DIL_SLOTS * HEAD_DIM

    def body(dr_ref, w_ref, oa_ref, ob_ref, gw_ref, gd_ref, doa_ref, dob0, dob1, dob2, dla_ref, dlb_ref, st_ref,
             scr):
        i = pl.program_id(0)
        dm = _dot_nt(dr_ref[...], w_ref[...])

        @pl.when(i == 0)
        def _():
            st_ref[...] = jnp.zeros_like(st_ref)

        lane = lax.broadcasted_iota(jnp.int32, (t, LANES), 1)
        for idx, (o_ref, g_ref, dl_ref) in enumerate(((oa_ref, gw_ref, dla_ref), (ob_ref, gd_ref, dlb_ref))):
            o = o_ref[...]
            dn = dm[:, idx * wd:(idx + 1) * wd]
            _, rinv = _rms_parts(o, g_ref[...])
            wv = dn * g_ref[...]
            do = rinv * wv - o * (rinv * rinv * rinv) * jnp.mean(wv * o, axis=-1, keepdims=True)
            st_ref[idx:idx + 1, :] += jnp.sum(dn * o * rinv, axis=0, keepdims=True)
            if idx == 0:
                doa_ref[...] = do.astype(BF16)
            else:
                for do_ref, dil in zip((dob0, dob1, dob2), DILATIONS):
                    _to_subsequences(do, do_ref, scr, dil, t)
            prod = do * o
            acc = jnp.zeros((t, LANES), F32)
            for h in range(DIL_SLOTS):
                hs = slice(h * HEAD_DIM, (h + 1) * HEAD_DIM)
                acc = jnp.where(lane == h, jnp.sum(prod[:, hs], axis=1, keepdims=True), acc)
            dl_ref[...] = acc

    half = pl.BlockSpec((t, wd), lambda i: (i, 0))
    lanes = pl.BlockSpec((t, LANES), lambda i: (i, 0))
    grow = pl.BlockSpec((1, wd), lambda i: (0, 0))
    a_spec = pl.BlockSpec((None, t, wd), lambda i: (0, i, 0))
    subseq = [pl.BlockSpec((dil, t // dil, wd), lambda i: (0, i, 0)) for dil in DILATIONS]
    doa, dob0, dob1, dob2, dla, dlb, st = pl.pallas_call(
        body, name="combine_bwd", grid=(s // t,),
        in_specs=[pl.BlockSpec((t, D_MODEL), lambda i: (i, 0)), pl.BlockSpec((D_MODEL, D_MODEL), lambda i: (0, 0)),
                  a_spec, half, grow, grow],
        out_specs=[a_spec] + subseq + [lanes, lanes, pl.BlockSpec((8, wd), lambda i: (0, 0))],
        out_shape=[jax.ShapeDtypeStruct((1, s, wd), BF16)]
        + [jax.ShapeDtypeStruct((dil, s // dil, wd), BF16) for dil in DILATIONS]
        + [jax.ShapeDtypeStruct((s, LANES), F32), jax.ShapeDtypeStruct((s, LANES), F32),
           jax.ShapeDtypeStruct((8, wd), F32)],
        scratch_shapes=[pltpu.VMEM((wd // LANES, t, LANES), F32)],
        compiler_params=_cparams(dimension_semantics=("arbitrary",)),
    )(dr1b, w_mix_b, out_a, out_b, g_win, g_dil)
    return doa, [dob0, dob1, dob2], dla, dlb, st


def _assemble_dz(dqa, dka, dva, dqs, dks, dvs, *, t):
    s = dqa.shape[1]
    wd = DIL_SLOTS * HEAD_DIM

    def body(*refs):
        a_refs, g_refs, o_ref, scr = refs[:3], refs[3:12], refs[12], refs[13]
        col = 0
        for r in a_refs:
            o_ref[:, col:col + r.shape[-1]] = r[...]
            col += r.shape[-1]
        for part in range(3):
            for gi, dil in enumerate(DILATIONS):
                val = _from_subsequences(g_refs[3 * part + gi], scr, dil, t)
                o_ref[:, col:col + wd] = val.astype(BF16)
                col += wd

    a_specs = [pl.BlockSpec((None, t, a.shape[-1]), lambda i: (0, i, 0)) for a in (dqa, dka, dva)]
    g_specs = [pl.BlockSpec((dil, t // dil, wd), lambda i: (0, i, 0)) for _ in range(3) for dil in DILATIONS]
    return pl.pallas_call(
        body, name="assemble_dz", grid=(s // t,), in_specs=a_specs + g_specs,
        out_specs=pl.BlockSpec((t, IN_WIDTH), lambda i: (i, 0)),
        out_shape=jax.ShapeDtypeStruct((s, IN_WIDTH), BF16),
        scratch_shapes=[pltpu.VMEM((wd // LANES, t, LANES), F32)],
        compiler_params=_cparams(dimension_semantics=("parallel",)),
    )(dqa, dka, dva, *dqs, *dks, *dvs)


def _mem_fwd(mem, g, b, wk_b, wv_b):
    ml = mem.shape[0]

    def body(mem_ref, g_ref, b_ref, wk_ref, wv_ref, mn_ref, kx_ref, vx_ref):
        mn = _ln(mem_ref[...], g_ref[...], b_ref[...]).astype(BF16)
        mn_ref[...] = mn
        kx_ref[...] = _dot(mn, wk_ref[...]).astype(BF16)
        vx_ref[...] = _dot(mn, wv_ref[...]).astype(BF16)

    sh = jax.ShapeDtypeStruct((ml, D_MODEL), BF16)
    return pl.pallas_call(body, name="mem_fwd", out_shape=[sh, sh, sh], compiler_params=_cparams())(
        mem, g, b, wk_b, wv_b)


def _mem_bwd(dkx, dvx, mem, g, b, wk_b, wv_b):
    def body(dk_ref, dv_ref, mem_ref, g_ref, b_ref, wk_ref, wv_ref, dwk_ref, dwv_ref, st_ref):
        mem_v = mem_ref[...]
        mn = _ln(mem_v, g_ref[...], b_ref[...]).astype(BF16)
        dkb = dk_ref[...].astype(BF16)
        dvb = dv_ref[...].astype(BF16)
        dwk_ref[...] = _dot_tn(mn, dkb)
        dwv_ref[...] = _dot_tn(mn, dvb)
        dmn = _dot_nt(dkb, wk_ref[...]) + _dot_nt(dvb, wv_ref[...])
        _, dg, db = _ln_bwd_math(dmn, mem_v, g_ref[...])
        st_ref[...] = jnp.zeros_like(st_ref)
        st_ref[0:1, :] = dg
        st_ref[1:2, :] = db

    sw = jax.ShapeDtypeStruct((D_MODEL, D_MODEL), F32)
    return pl.pallas_call(body, name="mem_bwd", out_shape=[sw, sw, jax.ShapeDtypeStruct((8, D_MODEL), F32)],
                          compiler_params=_cparams())(dkx, dvx, mem, g, b, wk_b, wv_b)


def _xattn_fwd(h1b, r1, kx, vx, wq_b, wo_b, ln1_g, ln1_b, ln2_g, ln2_b, *, t):
    s = h1b.shape[0]
    scale = X_HEAD_DIM ** -0.5

    def body(h_ref, r1_ref, kx_ref, vx_ref, wq_ref, wo_ref, g1, b1, g2, b2, r2_ref, h2_ref, qx_ref, ox_ref, lse_ref):
        qxb = _dot(h_ref[...], wq_ref[...]).astype(BF16)
        qx_ref[...] = qxb
        lane = lax.broadcasted_iota(jnp.int32, (t, LANES), 1)
        lse_acc = jnp.zeros((t, LANES), F32)
        parts = []
        for h in range(X_HEADS):
            hs = slice(h * X_HEAD_DIM, (h + 1) * X_HEAD_DIM)
            sc = _dot_nt(qxb[:, hs] * scale, kx_ref[:, hs])
            m = jnp.max(sc, axis=1, keepdims=True)
            p = jnp.exp(sc - m)
            den = jnp.sum(p, axis=1, keepdims=True)
            parts.append(_dot(p.astype(BF16), vx_ref[:, hs]) / den)
            lse_acc = jnp.where(lane == h, m + jnp.log(den), lse_acc)
        lse_ref[...] = lse_acc
        oxb = jnp.concatenate(parts, axis=1).astype(BF16)
        ox_ref[...] = oxb
        h1 = _ln(r1_ref[...], g1[...], b1[...])
        r2 = ALPHA * h1 + _dot(oxb, wo_ref[...])
        r2_ref[...] = r2
        h2_ref[...] = _ln(r2, g2[...], b2[...]).astype(BF16)

    tile = pl.BlockSpec((t, D_MODEL), lambda i: (i, 0))
    row = pl.BlockSpec((1, D_MODEL), lambda i: (0, 0))
    full = lambda r: pl.BlockSpec((r, D_MODEL), lambda i: (0, 0))
    ml = kx.shape[0]
    bsh = jax.ShapeDtypeStruct((s, D_MODEL), BF16)
    return pl.pallas_call(
        body, name="xattn_fwd", grid=(s // t,),
        in_specs=[tile, tile, full(ml), full(ml), full(D_MODEL), full(D_MODEL), row, row, row, row],
        out_specs=[tile, tile, tile, tile, pl.BlockSpec((t, LANES), lambda i: (i, 0))],
        out_shape=[jax.ShapeDtypeStruct((s, D_MODEL), F32), bsh, bsh, bsh, jax.ShapeDtypeStruct((s, LANES), F32)],
        compiler_params=_cparams(dimension_semantics=("parallel",)),
    )(h1b, r1, kx, vx, wq_b, wo_b, ln1_g, ln1_b, ln2_g, ln2_b)


def _xattn_bwd(dr2, qxb, oxb, lse, kx, vx, wq_b, wo_b, r1, ln1_g, *, t, comm=None):
    s = dr2.shape[0]
    ml = kx.shape[0]
    scale = X_HEAD_DIM ** -0.5

    def body(dr2_ref, qx_ref, ox_ref, lse_ref, kx_ref, vx_ref, wq_ref, wo_ref, r1_ref, g1_ref,
             dr1_ref, dr1b_ref, dqx_ref, dkx_ref, dvx_ref, st_ref):
        i = pl.program_id(0)

        @pl.when(i == 0)
        def _():
            dkx_ref[...] = jnp.zeros_like(dkx_ref)
            dvx_ref[...] = jnp.zeros_like(dvx_ref)
            st_ref[...] = jnp.zeros_like(st_ref)

        dr2v = dr2_ref[...]
        dox = _dot_nt(dr2v.astype(BF16), wo_ref[...])
        parts = []
        for h in range(X_HEADS):
            hs = slice(h * X_HEAD_DIM, (h + 1) * X_HEAD_DIM)
            doh = dox[:, hs]
            dohb = doh.astype(BF16)
            dl = jnp.sum(doh * ox_ref[:, hs].astype(F32), axis=1, keepdims=True)
            qh = qx_ref[:, hs] * scale
            p = jnp.exp(_dot_nt(qh, kx_ref[:, hs]) - lse_ref[:, h:h + 1])
            dp = _dot_nt(dohb, vx_ref[:, hs])
            dsb = (p * (dp - dl)).astype(BF16)
            parts.append(_dot(dsb, kx_ref[:, hs]) * scale)
            dkx_ref[:, hs] += _dot_tn(dsb, qh)
            dvx_ref[:, hs] += _dot_tn(p.astype(BF16), dohb)
        dqxb = jnp.concatenate(parts, axis=1).astype(BF16)
        dqx_ref[...] = dqxb
        dh1 = _dot_nt(dqxb, wq_ref[...]) + ALPHA * dr2v
        dr1, dg, db = _ln_bwd_math(dh1, r1_ref[...], g1_ref[...])
        dr1_ref[...] = dr1
        dr1b_ref[...] = dr1.astype(BF16)
        st_ref[0:1, :] += dg
        st_ref[1:2, :] += db

    tile = pl.BlockSpec((t, D_MODEL), lambda i: (i, 0))
    full = lambda r: pl.BlockSpec((r, D_MODEL), lambda i: (0, 0))
    bsh = jax.ShapeDtypeStruct((s, D_MODEL), BF16)
    return _pcall(
        body, name="xattn_bwd", grid=(s // t,),
        in_specs=[tile, tile, tile, pl.BlockSpec((t, LANES), lambda i: (i, 0)), full(ml), full(ml),
                  full(D_MODEL), full(D_MODEL), tile, full(1)],
        out_specs=[tile, tile, tile, full(ml), full(ml), full(8)],
        out_shape=[jax.ShapeDtypeStruct((s, D_MODEL), F32), bsh, bsh,
                   jax.ShapeDtypeStruct((ml, D_MODEL), F32), jax.ShapeDtypeStruct((ml, D_MODEL), F32),
                   jax.ShapeDtypeStruct((8, D_MODEL), F32)],
        args=[dr2, qxb, oxb, lse, kx, vx, wq_b, wo_b, r1, ln1_g], dims=("arbitrary",), comm=comm)


def _halo_specs(t, s, width):
    tb8 = t // 8
    return [pl.BlockSpec((t, width), lambda i: (i, 0)),
            pl.BlockSpec((8, width), lambda i: (jnp.maximum(i * tb8 - 1, 0), 0)),
            pl.BlockSpec((8, width), lambda i: (jnp.minimum((i + 1) * tb8, s // 8 - 1), 0))]


def _halo_rows(i, n, prev_ref, next_ref):
    prev_row = jnp.where(i > 0, prev_ref[7:8, :], 0.0)
    next_row = jnp.where(i < n - 1, next_ref[0:1, :], 0.0)
    return prev_row, next_row


def _gelu_parts(gc):
    cdf = 0.5 * (1.0 + lax.erf(gc * (2.0 ** -0.5)))
    pdf = jnp.exp(-0.5 * gc * gc) * (1.0 / math.sqrt(2.0 * math.pi))
    return gc * cdf, cdf + gc * pdf


def _ffn_out(g, u, conv_w, conv_b, w_down_b, r2, target, ln2_g, ln2_b, ln3_g, ln3_b, *, t):
    s = r2.shape[0]
    n = s // t

    def body(g_ref, gp_ref, gn_ref, u_ref, cw_ref, cb_ref, w_ref, r2_ref, tg_ref, g2, b2, g3, b3,
             t_ref, dr_ref, drb_ref, st_ref):
        i = pl.program_id(0)

        @pl.when(i == 0)
        def _():
            st_ref[...] = jnp.zeros_like(st_ref)

        gv = g_ref[...]
        prev_row, next_row = _halo_rows(i, n, gp_ref, gn_ref)
        gm1, gp1 = _shift_rows(gv, prev_row, next_row)
        gc = gm1 * cw_ref[0:1, :] + gv * cw_ref[1:2, :] + gp1 * cw_ref[2:3, :] + cb_ref[...]
        act, _ = _gelu_parts(gc)
        tb = (act * u_ref[...]).astype(BF16)
        t_ref[...] = tb
        h2 = _ln(r2_ref[...], g2[...], b2[...])
        r3 = ALPHA * h2 + _dot(tb, w_ref[...])
        y = _ln(r3, g3[...], b3[...])
        err = y - tg_ref[...]
        loss = 0.5 * jnp.sum(jnp.mean(err * err, axis=-1, keepdims=True))
        dr, dg, db = _ln_bwd_math(err * (1.0 / D_MODEL), r3, g3[...])
        dr_ref[...] = dr
        drb_ref[...] = dr.astype(BF16)
        st_ref[0:1, :] += dg
        st_ref[1:2, :] += db
        st_ref[2:3, :] += jnp.full((1, D_MODEL), loss, F32)

    wide = pl.BlockSpec((t, D_FF), lambda i: (i, 0))
    tile = pl.BlockSpec((t, D_MODEL), lambda i: (i, 0))
    row = pl.BlockSpec((1, D_MODEL), lambda i: (0, 0))
    return pl.pallas_call(
        body, name="ffn_out", grid=(n,),
        in_specs=_halo_specs(t, s, D_FF) + [wide, pl.BlockSpec((3, D_FF), lambda i: (0, 0)),
                                            pl.BlockSpec((1, D_FF), lambda i: (0, 0)),
                                            pl.BlockSpec((D_FF, D_MODEL), lambda i: (0, 0)),
                                            tile, tile, row, row, row, row],
        out_specs=[wide, tile, tile, pl.BlockSpec((8, D_MODEL), lambda i: (0, 0))],
        out_shape=[jax.ShapeDtypeStruct((s, D_FF), BF16), jax.ShapeDtypeStruct((s, D_MODEL), F32),
                   jax.ShapeDtypeStruct((s, D_MODEL), BF16), jax.ShapeDtypeStruct((8, D_MODEL), F32)],
        compiler_params=_cparams(dimension_semantics=("arbitrary",)),
    )(g, g, g, u, conv_w, conv_b, w_down_b, r2, target, ln2_g, ln2_b, ln3_g, ln3_b)


def _dh2_ln2(dgc, conv_w, du, w_gate_b, w_up_b, dr3, r2, ln2_g, *, t, comm=None):
    s = dgc.shape[0]
    n = s // t

    def body(d_ref, dp_ref, dn_ref, cw_ref, du_ref, wg_ref, wu_ref, dr3_ref, r2_ref, g2, dg_ref, dr_ref, drb_ref,
             st_ref):
        i = pl.program_id(0)

        @pl.when(i == 0)
        def _():
            st_ref[...] = jnp.zeros_like(st_ref)

        dv = d_ref[...]
        prev_row, next_row = _halo_rows(i, n, dp_ref, dn_ref)
        dm1, dp1 = _shift_rows(dv, prev_row, next_row)
        dgb = (dp1 * cw_ref[0:1, :] + dv * cw_ref[1:2, :] + dm1 * cw_ref[2:3, :]).astype(BF16)
        dg_ref[...] = dgb
        dh2 = _dot(dgb, wg_ref[...]) + _dot(du_ref[...], wu_ref[...]) + ALPHA * dr3_ref[...]
        dr, dg, db = _ln_bwd_math(dh2, r2_ref[...], g2[...])
        dr_ref[...] = dr
        drb_ref[...] = dr.astype(BF16)
        st_ref[0:1, :] += dg
        st_ref[1:2, :] += db

    wide = pl.BlockSpec((t, D_FF), lambda i: (i, 0))
    tile = pl.BlockSpec((t, D_MODEL), lambda i: (i, 0))
    wfull = pl.BlockSpec((D_FF, D_MODEL), lambda i: (0, 0), pipeline_mode=pl.Buffered(1))
    return _pcall(
        body, name="dh2_ln2", grid=(n,),
        in_specs=_halo_specs(t, s, D_FF) + [pl.BlockSpec((3, D_FF), lambda i: (0, 0)), wide, wfull, wfull,
                                            tile, tile, pl.BlockSpec((1, D_MODEL), lambda i: (0, 0))],
        out_specs=[wide, tile, tile, pl.BlockSpec((8, D_MODEL), lambda i: (0, 0))],
        out_shape=[jax.ShapeDtypeStruct((s, D_FF), BF16), jax.ShapeDtypeStruct((s, D_MODEL), F32),
                   jax.ShapeDtypeStruct((s, D_MODEL), BF16), jax.ShapeDtypeStruct((8, D_MODEL), F32)],
        args=[dgc, dgc, dgc, conv_w, du, w_gate_b, w_up_b, dr3, r2, ln2_g], dims=("arbitrary",), comm=comm)


def _conv_bwd_a(dr3b, w_down_b, g, u, conv_w, conv_b, *, t):
    s = g.shape[0]
    n = s // t

    def body(d_ref, w_ref, g_ref, gp_ref, gn_ref, u_ref, cw_ref, cb_ref, du_ref, dgc_ref, st_ref):
        i = pl.program_id(0)

        @pl.when(i == 0)
        def _():
            st_ref[...] = jnp.zeros_like(st_ref)

        dt = _dot_nt(d_ref[...], w_ref[...])
        gv = g_ref[...]
        prev_row, next_row = _halo_rows(i, n, gp_ref, gn_ref)
        gm1, gp1 = _shift_rows(gv, prev_row, next_row)
        gc = gm1 * cw_ref[0:1, :] + gv * cw_ref[1:2, :] + gp1 * cw_ref[2:3, :] + cb_ref[...]
        act, dact = _gelu_parts(gc)
        du_ref[...] = (dt * act).astype(BF16)
        dgc = dt * u_ref[...] * dact
        dgc_ref[...] = dgc
        st_ref[0:1, :] += jnp.sum(gm1 * dgc, axis=0, keepdims=True)
        st_ref[1:2, :] += jnp.sum(gv * dgc, axis=0, keepdims=True)
        st_ref[2:3, :] += jnp.sum(gp1 * dgc, axis=0, keepdims=True)
        st_ref[3:4, :] += jnp.sum(dgc, axis=0, keepdims=True)

    tile = pl.BlockSpec((t, D_FF), lambda i: (i, 0))
    return pl.pallas_call(
        body, name="conv_bwd_a", grid=(n,),
        in_specs=[pl.BlockSpec((t, D_MODEL), lambda i: (i, 0)), pl.BlockSpec((D_FF, D_MODEL), lambda i: (0, 0))]
        + _halo_specs(t, s, D_FF) + [tile, pl.BlockSpec((3, D_FF), lambda i: (0, 0)),
                                     pl.BlockSpec((1, D_FF), lambda i: (0, 0))],
        out_specs=[tile, tile, pl.BlockSpec((8, D_FF), lambda i: (0, 0))],
        out_shape=[jax.ShapeDtypeStruct((s, D_FF), BF16), jax.ShapeDtypeStruct((s, D_FF), F32),
                   jax.ShapeDtypeStruct((8, D_FF), F32)],
        compiler_params=_cparams(dimension_semantics=("arbitrary",)),
    )(dr3b, w_down_b, g, g, g, u, conv_w, conv_b)


def _to_residue(a, dil):
    s, w = a.shape
    return a.reshape(s // dil, dil, w).transpose(1, 0, 2)


def _stats_to_lanes(rows):
    dil, hq, l = rows.shape
    return jnp.pad(rows.transpose(2, 0, 1).reshape(dil * l, hq), ((0, 0), (0, LANES - hq)))


def _stats_to_rows(lanes, dil):
    s = lanes.shape[0]
    return lanes[:, :DIL_SLOTS].reshape(s // dil, dil, DIL_SLOTS).transpose(1, 2, 0)


def _rope_angles(positions):
    inv_freq = ROPE_THETA ** (-jnp.arange(0, ROT_DIM, 2, dtype=F32) / ROT_DIM)
    ang = positions.astype(F32)[:, None] * inv_freq
    return jnp.concatenate([jnp.cos(ang), jnp.sin(ang)], axis=1)


class _NoPlan:
    def gather(self, stage):
        return None

    def gathered(self, stage, couts, wb):
        pass

    def exchange(self, stage, grads):
        return None

    def exchanged(self, stage, couts):
        pass


def _local_step(x, mem, positions, target, wb, sp, plan=None, *, t_row=256, t_mm=512, tq_a=128, tq_b=128,
                sub_a=4, sub_b=4):
    s = x.shape[0]
    plan = plan or _NoPlan()
    cs = _rope_angles(positions)
    e_mat = _rope_select_matrix()

    h0b, couts = _ln_in_fwd(x, sp["ln_in_g"], sp["ln_in_b"], t=t_mm, comm=plan.gather("ln_in"))
    plan.gathered("ln_in", couts, wb)
    sp = dict(sp, conv_w=wb.get("conv_w", sp.get("conv_w")))
    (za, *zb), couts = _proj_all(h0b, wb["w_in"], cs, e_mat, t=min(2 * t_mm, s), comm=plan.gather("proj"))
    plan.gathered("proj", couts, wb)
    sub_a = max(1, min(sub_a, s // tq_a))
    subs_b = [max(1, min(sub_b, s // dil // tq_b)) for dil in DILATIONS]
    out_a, lse_a, couts = _swa_fwd_p(za, qcol=0, kcol=4, vcol=5, hq=WIN_Q_HEADS, hkv=WIN_KV_HEADS, w=WIN_HALF,
                                     tq=tq_a, sub=sub_a, sink=sp["attn_sink"], name="attn_a_fwd",
                                     comm=plan.gather("attn_a"))
    plan.gathered("attn_a", couts, wb)
    o_g, lse_g = [], []
    for gi in range(3):
        o, l, couts = _swa_fwd_p(zb[gi], qcol=0, kcol=1, vcol=2, hq=DIL_SLOTS, hkv=DIL_SLOTS, w=DIL_HALF, tq=tq_b,
                                 sub=subs_b[gi], sink=None, name=f"attn_b{gi}_fwd",
                                 comm=plan.gather(f"attn_b{gi}"))
        plan.gathered(f"attn_b{gi}", couts, wb)
        o_g.append(o)
        lse_g.append(_stats_to_lanes(l))
    mixed_b, out_b, lse_b, r1, h1b = _combine_fwd(
        out_a, o_g, lse_g, sp["g_win"], sp["g_dil"], wb["w_mix_out"], x, sp["ln_in_g"], sp["ln_in_b"],
        sp["ln1_g"], sp["ln1_b"], t=t_row)
    mem_nb, kx, vx = _mem_fwd(mem, sp["mem_ln_g"], sp["mem_ln_b"], wb["w_xk"], wb["w_xv"])
    r2, h2b, qxb, oxb, lse_x = _xattn_fwd(h1b, r1, kx, vx, wb["w_xq"], wb["w_xo"], sp["ln1_g"], sp["ln1_b"],
                                          sp["ln2_g"], sp["ln2_b"], t=t_mm)
    g = _mm(h2b, wb["w_gate"], mode="nt", out_dtype=F32, tm=t_mm, tn=D_FF, name="ff_gate")
    u = _mm(h2b, wb["w_up"], mode="nt", out_dtype=F32, tm=t_mm, tn=D_FF, name="ff_up")
    tb, dr3, dr3b, st3 = _ffn_out(g, u, sp["conv_w"], sp["conv_b"], wb["w_down"], r2, target, sp["ln2_g"],
                                  sp["ln2_b"], sp["ln3_g"], sp["ln3_b"], t=t_row)

    grads = {}
    du, dgc, st_conv = _conv_bwd_a(dr3b, wb["w_down"], g, u, sp["conv_w"], sp["conv_b"], t=t_row)
    tk = min(1024, s)
    grads["w_down"] = _mm(tb, dr3b, mode="tn", out_dtype=BF16, tm=D_FF // 2, tn=D_MODEL, tk=tk, name="dw_down")
    grads["w_up"] = _mm(du, h2b, mode="tn", out_dtype=BF16, tm=D_FF // 2, tn=D_MODEL, tk=tk, name="dw_up")
    (dg, dr2, dr2b, st2), couts = _dh2_ln2(dgc, sp["conv_w"], du, wb["w_gate"], wb["w_up"], dr3, r2, sp["ln2_g"],
                                           t=t_mm, comm=plan.exchange("dh2", grads))
    plan.exchanged("dh2", couts)
    grads["w_gate"] = _mm(dg, h2b, mode="tn", out_dtype=BF16, tm=D_FF // 2, tn=D_MODEL, tk=tk, name="dw_gate")

    (dr1, dr1b, dqxb, dkx, dvx, st1), couts = _xattn_bwd(
        dr2, qxb, oxb, lse_x, kx, vx, wb["w_xq"], wb["w_xo"], r1, sp["ln1_g"], t=t_mm,
        comm=plan.exchange("xattn", grads))
    plan.exchanged("xattn", couts)
    grads["w_xo"] = _mm(oxb, dr2b, mode="tn", out_dtype=BF16, tm=D_MODEL, tn=D_MODEL, tk=tk, name="dw_xo")
    grads["w_xq"] = _mm(h1b, dqxb, mode="tn", out_dtype=BF16, tm=D_MODEL, tn=D_MODEL, tk=tk, name="dw_xq")
    grads["w_xk"], grads["w_xv"], st_mem = _mem_bwd(dkx, dvx, mem, sp["mem_ln_g"], sp["mem_ln_b"],
                                                    wb["w_xk"], wb["w_xv"])

    grads["w_mix_out"] = _mm(mixed_b, dr1b, mode="tn", out_dtype=BF16, tm=D_MODEL, tn=D_MODEL, tk=tk,
                             name="dw_mix")
    do_a, do_b, dl_a, dl_b, st_mix = _combine_bwd(dr1b, wb["w_mix_out"], out_a, out_b, sp["g_win"], sp["g_dil"],
                                                  t=t_row)
    (dqa, dka, dva, dsink), couts = _swa_bwd_p(
        za, do_a, lse_a, _stats_to_rows(dl_a, 1), cs[None], e_mat, qcol=0, kcol=4, vcol=5, hq=WIN_Q_HEADS,
        hkv=WIN_KV_HEADS, w=WIN_HALF, tq=2 * tq_a, sub=max(1, sub_a // 2), sink=sp["attn_sink"], name="attn_a_bwd",
        comm=plan.exchange("attn_a", grads))
    plan.exchanged("attn_a", couts)
    dqs, dks, dvs = [], [], []
    for gi, dil in enumerate(DILATIONS):
        (dq, dk, dv), couts = _swa_bwd_p(
            zb[gi], do_b[gi], _stats_to_rows(lse_b, dil), _stats_to_rows(dl_b, dil),
            _to_residue(cs, dil), e_mat, qcol=0, kcol=1, vcol=2, hq=DIL_SLOTS, hkv=DIL_SLOTS, w=DIL_HALF, tq=tq_b,
            sub=subs_b[gi], sink=None, name=f"attn_b{gi}_bwd", comm=plan.exchange(f"attn_b{gi}", grads))
        plan.exchanged(f"attn_b{gi}", couts)
        dqs.append(dq)
        dks.append(dk)
        dvs.append(dv)
    dz = _assemble_dz(dqa, dka, dva, dqs, dks, dvs, t=t_row)
    grads["w_in"] = _mm(dz, h0b, mode="tn", out_dtype=BF16, tm=IN_WIDTH // 7, tn=D_MODEL, tk=tk, name="dw_in")
    comm = plan.exchange("dh0", grads)
    dh0 = _mm(dz, wb["w_in"], mode="nn", out_dtype=F32, tm=t_mm, tn=D_MODEL, add=dr1, add_scale=ALPHA, name="dh0",
              comm=comm)
    if comm is not None:
        dh0, couts = dh0
        plan.exchanged("dh0", couts)
    grad_x, st0 = _ln_bwd(dh0, x, sp["ln_in_g"], t=t_row, name="ln_in_bwd", want_bf16=False)

    small = {
        "loss": st3[2:3, 0:1],
        "ln_in_g": st0[0:1], "ln_in_b": st0[1:2],
        "attn_sink": dsink[:, 0].reshape(1, WIN_Q_HEADS),
        "g_win": st_mix[0:1], "g_dil": st_mix[1:2],
        "ln1_g": st1[0:1], "ln1_b": st1[1:2],
        "mem_ln_g": st_mem[0:1], "mem_ln_b": st_mem[1:2],
        "ln2_g": st2[0:1], "ln2_b": st2[1:2],
        "conv_w": st_conv[0:3], "conv_b": st_conv[3:4],
        "ln3_g": st3[0:1], "ln3_b": st3[1:2],
    }
    return grad_x, grads, small


class _SiblingSwap:
    def __init__(self, arrays):
        self.inputs = list(arrays)
        n = len(arrays)
        self.out_shape = [jax.ShapeDtypeStruct(a.shape, a.dtype) for a in arrays]
        self.scratch = [pltpu.SemaphoreType.DMA((n,)), pltpu.SemaphoreType.DMA((n,))]

    def _copies(self, src, dst, sems):
        send_sems, recv_sems = sems
        x, y, c, _ = _place()
        return [pltpu.make_async_remote_copy(
            src_ref=src[a], dst_ref=dst[a], send_sem=send_sems.at[a], recv_sem=recv_sems.at[a],
            device_id=(x, y, 1 - c), device_id_type=MESH_IDS) for a in range(len(src))]

    def start(self, src, dst, sems):
        for cp in self._copies(src, dst, sems):
            cp.start()

    def wait(self, src, dst, sems):
        copies = self._copies(src, dst, sems)
        for cp in copies:
            cp.wait_recv()
        for cp in copies:
            cp.wait_send()


class _Both:
    def __init__(self, first, second):
        self.parts = (first, second)
        self.inputs = first.inputs + second.inputs
        self.out_shape = first.out_shape + second.out_shape
        self.scratch = first.scratch + second.scratch

    def _split(self, src, dst, sems):
        a = self.parts[0]
        ni, no, ns = len(a.inputs), len(a.out_shape), len(a.scratch)
        return ((src[:ni], dst[:no], sems[:ns]), (src[ni:], dst[no:], sems[ns:]))

    def start(self, src, dst, sems):
        for part, args in zip(self.parts, self._split(src, dst, sems)):
            part.start(*args)

    def wait(self, src, dst, sems):
        for part, args in zip(self.parts, self._split(src, dst, sems)):
            part.wait(*args)


def _row_tile(rows, cols, itemsize=4, budget=1 << 20):
    best = None
    for t in range(16, rows + 1, 16):
        if rows % t == 0 and t * cols * itemsize <= budget:
            best = t
    return best or rows


def _sum_slots(stack, *, name):
    n, r, c = stack.shape
    t = _row_tile(r, c)

    def body(s_ref, o_ref):
        acc = s_ref[0].astype(F32)
        for q in range(1, n):
            acc = acc + s_ref[q].astype(F32)
        o_ref[...] = acc

    return pl.pallas_call(
        body, name=name, grid=(r // t,), in_specs=[pl.BlockSpec((n, t, c), lambda i: (0, i, 0))],
        out_specs=pl.BlockSpec((t, c), lambda i: (i, 0)), out_shape=jax.ShapeDtypeStruct((r, c), F32),
        compiler_params=_cparams(dimension_semantics=("parallel",)),
    )(stack)


def _adamw(w, m, v, p, q, *, name):
    r, c = w.shape
    t = _row_tile(r, c, budget=1 << 20)

    def body(*refs):
        if q is None:
            w_ref, m_ref, v_ref, p_ref, g_ref, d_ref, nm_ref, nv_ref = refs
            g = p_ref[...]
        else:
            w_ref, m_ref, v_ref, p_ref, q_ref, g_ref, d_ref, nm_ref, nv_ref = refs
            g = p_ref[...] + q_ref[...]
        nm = ADAM_B1 * m_ref[...] + (1.0 - ADAM_B1) * g
        nv = ADAM_B2 * v_ref[...] + (1.0 - ADAM_B2) * (g * g)
        m_hat = nm / (1.0 - ADAM_B1 ** ADAM_STEP)
        v_hat = nv / (1.0 - ADAM_B2 ** ADAM_STEP)
        g_ref[...] = g
        d_ref[...] = -ADAM_LR * (m_hat / (jnp.sqrt(v_hat) + ADAM_EPS) + ADAM_WD * w_ref[...])
        nm_ref[...] = nm
        nv_ref[...] = nv

    tile = pl.BlockSpec((t, c), lambda i: (i, 0))
    args = [w, m, v, p] + ([] if q is None else [q])
    sh = jax.ShapeDtypeStruct((r, c), F32)
    return pl.pallas_call(
        body, name=name, grid=(r // t,), in_specs=[tile] * len(args), out_specs=[tile] * 4, out_shape=[sh] * 4,
        compiler_params=_cparams(dimension_semantics=("parallel",)),
    )(*args)


BIG = ("w_in", "w_mix_out", "w_xq", "w_xk", "w_xv", "w_xo", "w_gate", "w_up", "w_down")
COL_SHARDED = ("w_in", "w_gate", "w_up")
WEIGHTS = ("ln_in_g", "ln_in_b", "w_in", "attn_sink", "g_win", "g_dil", "w_mix_out", "ln1_g", "ln1_b",
           "mem_ln_g", "mem_ln_b", "w_xq", "w_xk", "w_xv", "w_xo", "ln2_g", "ln2_b", "w_gate", "w_up",
           "conv_w", "conv_b", "w_down", "ln3_g", "ln3_b")
SMALL = tuple(k for k in WEIGHTS if k not in BIG)
PACK_COLS = 1024
CONV_SHARD = D_FF // N_CHIPS
CONV_WIDTH_ROWS = 3
SMALL_ROWS = 32


GATHER_STAGES = {"ln_in": ("w_in", "conv_w"), "proj": ("w_mix_out", "w_xq", "w_xk", "w_xv", "w_xo", "w_up"),
                 "attn_a": ("w_gate",), "attn_b0": ("w_down",)}
EXCHANGE_STAGES = {"dh2": ("w_down",), "xattn": ("w_up",), "attn_a": ("w_gate", "w_xo", "w_xq"),
                   "attn_b0": ("w_xk", "w_xv", "w_mix_out"), "dh0": ("w_in",)}


def _full_weight(k, g4):
    return g4.reshape(N_CHIPS * g4.shape[1], g4.shape[2])


def _grad_parts(k, gk):
    gk = gk.astype(BF16)
    return gk.reshape(N_CHIPS, gk.shape[0] // N_CHIPS, gk.shape[1])


EARLY_SWAP_STAGE = "attn_b2"


class _Plan:
    def __init__(self, shards):
        self.shards = shards
        self.recv = {}
        self.chip_sums = {}
        self.sibling_sums = {}

    def gather(self, stage):
        names = GATHER_STAGES.get(stage)
        return _ChipGather([self.shards[k] for k in names]) if names else None

    def gathered(self, stage, couts, wb):
        for k, g4 in zip(GATHER_STAGES.get(stage, ()), couts):
            if k == "conv_w":
                taps = g4[:, :CONV_WIDTH_ROWS, :CONV_SHARD]
                wb[k] = taps.transpose(1, 0, 2).reshape(CONV_WIDTH_ROWS, D_FF)
            else:
                wb[k] = _full_weight(k, g4)

    def exchange(self, stage, grads):
        if stage == EARLY_SWAP_STAGE:
            self.early = [k for k in BIG if k in self.recv]
            for k in self.early:
                self.chip_sums[k] = _sum_slots(self.recv[k], name=f"sum_chips_{k}")
            return _SiblingSwap([self.chip_sums[k] for k in self.early])
        names = EXCHANGE_STAGES.get(stage)
        return _ChipExchange([_grad_parts(k, grads[k]) for k in names]) if names else None

    def exchanged(self, stage, couts):
        if stage == EARLY_SWAP_STAGE:
            self.sibling_sums.update(zip(self.early, couts))
            return
        for k, r4 in zip(EXCHANGE_STAGES.get(stage, ()), couts):
            self.recv[k] = r4


def _pack_rows(a):
    r, n = a.shape
    per = -(-n // PACK_COLS)
    return jnp.pad(a, ((0, 0), (0, per * PACK_COLS - n))).reshape(r * per, PACK_COLS)


def _unpack_rows(p, r, n):
    per = -(-n // PACK_COLS)
    return p.reshape(r, per * PACK_COLS)[:, :n]


def _pack(pieces, rows_total):
    cat = jnp.concatenate([_pack_rows(a) for a in pieces], axis=0)
    return jnp.pad(cat, ((0, rows_total - cat.shape[0]), (0, 0)))


def _unpack(p, shapes):
    out, at = [], 0
    for r, n in shapes:
        per = -(-n // PACK_COLS)
        out.append(_unpack_rows(p[at:at + r * per], r, n))
        at += r * per
    return out


def kernel(x, mem, positions, ln_in_g, ln_in_b, w_in, attn_sink, g_win, g_dil, w_mix_out, ln1_g, ln1_b, mem_ln_g, mem_ln_b, w_xq, w_xk, w_xv, w_xo, ln2_g, ln2_b, w_gate, w_up, conv_w, conv_b, w_down, ln3_g, ln3_b, loss_target, m_ln_in_g, m_ln_in_b, m_w_in, m_attn_sink, m_g_win, m_g_dil, m_w_mix_out, m_ln1_g, m_ln1_b, m_mem_ln_g, m_mem_ln_b, m_w_xq, m_w_xk, m_w_xv, m_w_xo, m_ln2_g, m_ln2_b, m_w_gate, m_w_up, m_conv_w, m_conv_b, m_w_down, m_ln3_g, m_ln3_b, v_ln_in_g, v_ln_in_b, v_w_in, v_attn_sink, v_g_win, v_g_dil, v_w_mix_out, v_ln1_g, v_ln1_b, v_mem_ln_g, v_mem_ln_b, v_w_xq, v_w_xk, v_w_xv, v_w_xo, v_ln2_g, v_ln2_b, v_w_gate, v_w_up, v_conv_w, v_conv_b, v_w_down, v_ln3_g, v_ln3_b):
    given = dict(locals())
    shape_of = {k: given[k].shape for k in WEIGHTS}
    as2d = lambda k, a: a.reshape(-1, a.shape[-1]).T if k in COL_SHARDED else a.reshape(-1, a.shape[-1])
    w2 = {k: as2d(k, given[k]) for k in WEIGHTS}
    m2 = {k: as2d(k, given["m_" + k]) for k in WEIGHTS}
    v2 = {k: as2d(k, given["v_" + k]) for k in WEIGHTS}
    chip = 2 * lax.axis_index("x") + lax.axis_index("y")

    shards = {k: w2[k].astype(BF16) for k in BIG}
    shards["conv_w"] = jnp.pad(w2["conv_w"], ((0, 16 - CONV_WIDTH_ROWS), (0, PACK_COLS - CONV_SHARD)))
    plan = _Plan(shards)
    sp = {k: w2[k] for k in SMALL if k != "conv_w"}

    grad_x, grads, small = _local_step(x[0], mem[0], positions[0], loss_target[0], {}, sp, plan)

    small_keys = ("loss",) + SMALL
    small_shapes = [small[k].shape for k in small_keys]
    small_pack = _pack([small[k] for k in small_keys], SMALL_ROWS)
    late = [k for k in BIG if k not in plan.chip_sums]
    for k in late:
        plan.chip_sums[k] = _sum_slots(plan.recv[k], name=f"sum_chips_{k}")
    *late_sibling, small_all = _comm_only(
        _Both(_SiblingSwap([plan.chip_sums[k] for k in late]), _ChipExchange([], small_pack)), "swap_and_small")
    plan.sibling_sums.update(zip(late, late_sibling))
    chip_sums = [plan.chip_sums[k] for k in BIG]
    sibling_sums = [plan.sibling_sums[k] for k in BIG]
    small_sum = _sum_slots(small_all, name="sum_small")
    small_g = dict(zip(small_keys, _unpack(small_sum, small_shapes)))
    loss = small_g["loss"][0, 0]

    res = {}
    for k, p, q in zip(BIG, chip_sums, sibling_sums):
        res[k] = _adamw(w2[k], m2[k], v2[k], p, q, name=f"adamw_{k}")
    small_g["conv_w"] = lax.dynamic_slice_in_dim(small_g["conv_w"], chip * CONV_SHARD, CONV_SHARD, axis=1)
    adam_shapes = [w2[k].shape for k in SMALL]
    packs = [_pack([d[k] for k in SMALL], SMALL_ROWS) for d in (w2, m2, v2, small_g)]
    small_res = [_unpack(o, adam_shapes) for o in _adamw(*packs, None, name="adamw_small")]
    for i, k in enumerate(SMALL):
        res[k] = tuple(o[i] for o in small_res)

    outs = [loss, grad_x[None]]
    for slot in range(4):
        outs += [(res[k][slot].T if k in COL_SHARDED else res[k][slot]).reshape(shape_of[k]) for k in WEIGHTS]
    return tuple(outs)
```

```python
import functools
import math

import jax
import jax.numpy as jnp
from jax import lax
from jax.experimental import pallas as pl
from jax.experimental.pallas import tpu as pltpu

F32 = jnp.float32
BF16 = jnp.bfloat16

D_MODEL = 1024
HEAD_DIM = 64
WIN_Q_HEADS = 8
WIN_KV_HEADS = 2
WIN_HALF = 128
DIL_SLOTS = 8
DILATIONS = (1, 4, 16)
DIL_HALF = 64
ROT_DIM = 16
ROPE_THETA = 500000.0
X_HEADS = 4
X_HEAD_DIM = 256
D_FF = 2816
A_Q = 512
A_KV = 128
A_WIDTH = A_Q + 2 * A_KV
B_QKV = 1536
IN_WIDTH = 5376
ALPHA = 2.0 ** 0.25
LN_EPS = 1e-5
NEG_INF = -1e30
LANES = 128
N_CHIPS = 4
N_DEV = 8

ADAM_LR = 0.001
ADAM_B1 = 0.9
ADAM_B2 = 0.999
ADAM_EPS = 1e-08
ADAM_WD = 0.01
ADAM_STEP = 10

VMEM_LIMIT = 56 * 1024 * 1024


def _cparams(**kw):
    return pltpu.CompilerParams(vmem_limit_bytes=VMEM_LIMIT, **kw)


def _dot(a, b):
    return lax.dot_general(a, b, (((1,), (0,)), ((), ())), preferred_element_type=F32)


def _dot_nt(a, b):
    return lax.dot_general(a, b, (((1,), (1,)), ((), ())), preferred_element_type=F32)


def _dot_tn(a, b):
    return lax.dot_general(a, b, (((0,), (0,)), ((), ())), preferred_element_type=F32)


def _ln(x, g, b):
    mu = jnp.mean(x, axis=-1, keepdims=True)
    xc = x - mu
    var = jnp.mean(xc * xc, axis=-1, keepdims=True)
    return xc * lax.rsqrt(var + LN_EPS) * g + b


def _ln_bwd_math(dy, r, g):
    mu = jnp.mean(r, axis=-1, keepdims=True)
    xc = r - mu
    var = jnp.mean(xc * xc, axis=-1, keepdims=True)
    rstd = lax.rsqrt(var + LN_EPS)
    xhat = xc * rstd
    dxhat = dy * g
    m1 = jnp.mean(dxhat, axis=-1, keepdims=True)
    m2 = jnp.mean(dxhat * xhat, axis=-1, keepdims=True)
    dr = rstd * (dxhat - m1 - xhat * m2)
    return dr, jnp.sum(dy * xhat, axis=0, keepdims=True), jnp.sum(dy, axis=0, keepdims=True)


def _rope(z, ta, tb, tc, sign):
    w = z.shape[1]
    reps = w // LANES
    a = jnp.tile(ta, (1, reps))
    b = jnp.tile(tb, (1, reps))
    c = jnp.tile(tc, (1, reps))
    return z * a + sign * (pltpu.roll(z, w - 8, 1) * b + pltpu.roll(z, 8, 1) * c)


def _shift_rows(x, prev_row, next_row):
    t = x.shape[0]
    row = lax.broadcasted_iota(jnp.int32, x.shape, 0)
    xm1 = jnp.where(row == 0, prev_row, pltpu.roll(x, 1, 0))
    xp1 = jnp.where(row == t - 1, next_row, pltpu.roll(x, t - 1, 0))
    return xm1, xp1


def _rope_tabs(cs, e_mat):
    hi = cs.astype(BF16)
    rest = cs - hi.astype(F32)
    mid = rest.astype(BF16)
    lo = (rest - mid.astype(F32)).astype(BF16)
    tabs = _dot(hi, e_mat) + _dot(mid, e_mat) + _dot(lo, e_mat)
    lane = lax.broadcasted_iota(jnp.int32, (cs.shape[0], LANES), 1)
    ones = jnp.where((lane & (HEAD_DIM - 1)) >= ROT_DIM, 1.0, 0.0)
    return tabs[:, :LANES] + ones, tabs[:, LANES:2 * LANES], tabs[:, 2 * LANES:]


def _rope_select_matrix():
    half = ROT_DIM // 2
    e = [[0.0] * (3 * LANES) for _ in range(ROT_DIM)]
    for lane in range(LANES):
        d = lane % HEAD_DIM
        if d < half:
            e[d][lane] = 1.0
            e[half + d][LANES + lane] = -1.0
        elif d < ROT_DIM:
            e[d - half][lane] = 1.0
            e[d][2 * LANES + lane] = 1.0
    return jnp.array(e, BF16)


def _rope_rows(x, cos_t, sin_t, sign):
    half = ROT_DIM // 2
    parts = []
    for base in (0, HEAD_DIM):
        r1, r2 = x[base:base + half], x[base + half:base + ROT_DIM]
        parts += [r1 * cos_t - sign * (r2 * sin_t), r2 * cos_t + sign * (r1 * sin_t), x[base + ROT_DIM:base + HEAD_DIM]]
    return jnp.concatenate(parts, axis=0)


MESH_IDS = pl.DeviceIdType.MESH
ANY = pl.BlockSpec(memory_space=pl.ANY)


def _place():
    x, y, c = lax.axis_index("x"), lax.axis_index("y"), lax.axis_index("c")
    other_chips = [(1 - x, y), (x, 1 - y), (1 - x, 1 - y)]
    return x, y, c, other_chips


class _ChipGather:
    def __init__(self, shards):
        self.inputs = list(shards)
        n = len(shards)
        self.out_shape = [jax.ShapeDtypeStruct((N_CHIPS,) + a.shape, a.dtype) for a in shards]
        self.scratch = [pltpu.SemaphoreType.DMA((6 * n,)), pltpu.SemaphoreType.DMA((6 * n,)),
                        pltpu.SemaphoreType.DMA((n,))]

    def _copies(self, src, dst, sems):
        send_sems, recv_sems, local_sems = sems
        x, y, c, chips = _place()
        mine = 2 * x + y
        n = len(src)
        local, sends, recvs, passes, pass_recvs = [], [], [], [], []
        for a in range(n):
            half = src[a].shape[0] // 2
            my_rows, other_rows = pl.ds(c * half, half), pl.ds((1 - c) * half, half)
            local.append(pltpu.make_async_copy(src[a], dst[a].at[mine], local_sems.at[a]))
            for j, (px, py) in enumerate(chips):
                k, k2, slot = 3 * a + j, 3 * n + 3 * a + j, 2 * px + py
                sends.append(pltpu.make_async_remote_copy(
                    src_ref=src[a].at[my_rows], dst_ref=dst[a].at[mine, my_rows], send_sem=send_sems.at[k],
                    recv_sem=recv_sems.at[k], device_id=(px, py, c), device_id_type=MESH_IDS))
                recvs.append(pltpu.make_async_remote_copy(
                    src_ref=src[a].at[my_rows], dst_ref=dst[a].at[slot, my_rows], send_sem=send_sems.at[k],
                    recv_sem=recv_sems.at[k], device_id=(px, py, c), device_id_type=MESH_IDS))
                passes.append(pltpu.make_async_remote_copy(
                    src_ref=dst[a].at[slot, my_rows], dst_ref=dst[a].at[slot, my_rows], send_sem=send_sems.at[k2],
                    recv_sem=recv_sems.at[k2], device_id=(x, y, 1 - c), device_id_type=MESH_IDS))
                pass_recvs.append(pltpu.make_async_remote_copy(
                    src_ref=dst[a].at[slot, my_rows], dst_ref=dst[a].at[slot, other_rows],
                    send_sem=send_sems.at[k2], recv_sem=recv_sems.at[k2], device_id=(x, y, 1 - c),
                    device_id_type=MESH_IDS))
        return local, sends, recvs, passes, pass_recvs

    def start(self, src, dst, sems):
        local, sends, _, _, _ = self._copies(src, dst, sems)
        for cp in local + sends:
            cp.start()

    def wait(self, src, dst, sems):
        local, sends, recvs, passes, pass_recvs = self._copies(src, dst, sems)
        for idx, landed in enumerate(recvs):
            landed.wait_recv()
            if passes:
                passes[idx].start()
        for cp in pass_recvs:
            cp.wait_recv()
        for cp in sends + passes:
            cp.wait_send()
        for cp in local:
            cp.wait()


class _ChipExchange:
    def __init__(self, parts, small=None):
        self.inputs = list(parts) + ([small] if small is not None else [])
        self.n = len(parts)
        self.has_small = small is not None
        self.out_shape = [jax.ShapeDtypeStruct(a.shape, a.dtype) for a in parts]
        n_sem, n_loc = 3 * self.n, self.n
        if self.has_small:
            self.out_shape.append(jax.ShapeDtypeStruct((N_DEV,) + small.shape, small.dtype))
            n_sem, n_loc = n_sem + N_DEV - 1, n_loc + 1
        self.scratch = [pltpu.SemaphoreType.DMA((n_sem,)), pltpu.SemaphoreType.DMA((n_sem,)),
                        pltpu.SemaphoreType.DMA((n_loc,))]

    def _copies(self, src, dst, sems):
        send_sems, recv_sems, local_sems = sems
        x, y, c, chips = _place()
        mine = 2 * x + y
        n = self.n
        local, sends, recvs = [], [], []
        for a in range(n):
            local.append(pltpu.make_async_copy(src[a].at[mine], dst[a].at[mine], local_sems.at[a]))
            for j, (px, py) in enumerate(chips):
                k = 3 * a + j
                sends.append(pltpu.make_async_remote_copy(
                    src_ref=src[a].at[2 * px + py], dst_ref=dst[a].at[mine], send_sem=send_sems.at[k],
                    recv_sem=recv_sems.at[k], device_id=(px, py, c), device_id_type=MESH_IDS))
                recvs.append(pltpu.make_async_remote_copy(
                    src_ref=src[a].at[mine], dst_ref=dst[a].at[2 * px + py], send_sem=send_sems.at[k],
                    recv_sem=recv_sems.at[k], device_id=(px, py, c), device_id_type=MESH_IDS))
        if self.has_small:
            me_dev = 4 * x + 2 * y + c
            local.append(pltpu.make_async_copy(src[n], dst[n].at[me_dev], local_sems.at[n]))
            for mask in range(1, N_DEV):
                px, py, pc = x ^ ((mask >> 2) & 1), y ^ ((mask >> 1) & 1), c ^ (mask & 1)
                k = 3 * n + mask - 1
                sends.append(pltpu.make_async_remote_copy(
                    src_ref=src[n], dst_ref=dst[n].at[me_dev], send_sem=send_sems.at[k], recv_sem=recv_sems.at[k],
                    device_id=(px, py, pc), device_id_type=MESH_IDS))
                recvs.append(pltpu.make_async_remote_copy(
                    src_ref=src[n], dst_ref=dst[n].at[4 * px + 2 * py + pc], send_sem=send_sems.at[k],
                    recv_sem=recv_sems.at[k], device_id=(px, py, pc), device_id_type=MESH_IDS))
        return local, sends, recvs, [], []

    start = _ChipGather.start
    wait = _ChipGather.wait


def _pcall(body, *, name, grid, in_specs, out_specs, out_shape, args, scratch_shapes=(), dims=None, comm=None):
    in_specs, out_specs, out_shape = list(in_specs), list(out_specs), list(out_shape)
    scratch_shapes = list(scratch_shapes)
    if comm is None:
        outs = pl.pallas_call(
            body, name=name, grid=grid, in_specs=in_specs, out_specs=out_specs, out_shape=out_shape,
            scratch_shapes=scratch_shapes, compiler_params=_cparams(dimension_semantics=dims),
        )(*args)
        return list(outs), []
    n_in, n_out, n_scr = len(in_specs), len(out_specs), len(scratch_shapes)
    n_cin, n_cout = len(comm.inputs), len(comm.out_shape)

    def wrapped(*refs):
        ins, refs = refs[:n_in], refs[n_in:]
        cins, refs = refs[:n_cin], refs[n_cin:]
        outs, refs = refs[:n_out], refs[n_out:]
        couts, refs = refs[:n_cout], refs[n_cout:]
        scr, csems = refs[:n_scr], refs[n_scr:]
        first = last = None
        for axis, size in enumerate(grid):
            pid = pl.program_id(axis)
            f, l = pid == 0, pid == size - 1
            first = f if first is None else first & f
            last = l if last is None else last & l

        @pl.when(first)
        def _():
            comm.start(cins, couts, csems)

        body(*ins, *outs, *scr)

        @pl.when(last)
        def _():
            comm.wait(cins, couts, csems)

    res = pl.pallas_call(
        wrapped, name=name, grid=grid, in_specs=in_specs + [ANY] * n_cin, out_specs=out_specs + [ANY] * n_cout,
        out_shape=out_shape + list(comm.out_shape), scratch_shapes=scratch_shapes + list(comm.scratch),
        compiler_params=_cparams(dimension_semantics=("arbitrary",) * len(grid)),
    )(*args, *comm.inputs)
    return list(res[:n_out]), list(res[n_out:])


def _comm_only(comm, name):
    def body(*refs):
        n_cin, n_cout = len(comm.inputs), len(comm.out_shape)
        cins, couts, csems = refs[:n_cin], refs[n_cin:n_cin + n_cout], refs[n_cin + n_cout:]
        comm.start(cins, couts, csems)
        comm.wait(cins, couts, csems)

    return list(pl.pallas_call(
        body, name=name, in_specs=[ANY] * len(comm.inputs), out_specs=[ANY] * len(comm.out_shape),
        out_shape=list(comm.out_shape), scratch_shapes=list(comm.scratch),
    )(*comm.inputs))


def _mm(a, b, *, mode, out_dtype, tm, tn, tk=None, add=None, add_scale=1.0, name, comm=None):
    if mode in ("nn", "nt"):
        m, k = a.shape
        n = b.shape[1] if mode == "nn" else b.shape[0]
        assert m % tm == 0 and n % tn == 0
        dot = _dot if mode == "nn" else _dot_nt

        def body(*refs):
            if add is None:
                a_ref, b_ref, o_ref = refs
                o_ref[...] = dot(a_ref[...], b_ref[...]).astype(out_dtype)
            else:
                a_ref, b_ref, c_ref, o_ref = refs
                o_ref[...] = (dot(a_ref[...], b_ref[...]) + add_scale * c_ref[...]).astype(out_dtype)

        b_spec = (pl.BlockSpec((k, tn), lambda i, j: (0, j)) if mode == "nn"
                  else pl.BlockSpec((tn, k), lambda i, j: (j, 0)))
        in_specs = [pl.BlockSpec((tm, k), lambda i, j: (i, 0)), b_spec]
        args = [a, b]
        if add is not None:
            in_specs.append(pl.BlockSpec((tm, tn), lambda i, j: (i, j)))
            args.append(add)
        outs, couts = _pcall(
            body, name=name, grid=(m // tm, n // tn), in_specs=in_specs,
            out_specs=[pl.BlockSpec((tm, tn), lambda i, j: (i, j))],
            out_shape=[jax.ShapeDtypeStruct((m, n), out_dtype)], args=args, dims=("parallel", "parallel"),
            comm=comm)
        return outs[0] if comm is None else (outs[0], couts)
    assert mode == "tn" and add is None and comm is None
    kk, m = a.shape
    n = b.shape[1]
    assert m % tm == 0 and n % tn == 0 and kk % tk == 0
    nk = kk // tk

    def body(a_ref, b_ref, o_ref, acc_ref):
        kstep = pl.program_id(2)

        @pl.when(kstep == 0)
        def _():
            acc_ref[...] = jnp.zeros_like(acc_ref)

        acc_ref[...] += _dot_tn(a_ref[...], b_ref[...])

        @pl.when(kstep == nk - 1)
        def _():
            o_ref[...] = acc_ref[...].astype(out_dtype)

    return pl.pallas_call(
        body, name=name, grid=(m // tm, n // tn, nk),
        in_specs=[pl.BlockSpec((tk, tm), lambda i, j, s: (s, i)), pl.BlockSpec((tk, tn), lambda i, j, s: (s, j))],
        out_specs=pl.BlockSpec((tm, tn), lambda i, j, s: (i, j)),
        out_shape=jax.ShapeDtypeStruct((m, n), out_dtype),
        scratch_shapes=[pltpu.VMEM((tm, tn), F32)],
        compiler_params=_cparams(dimension_semantics=("parallel", "parallel", "arbitrary")),
    )(a, b)


def _ln_bwd(dy, r, g, *, t, name, want_bf16):
    s = r.shape[0]

    def body(dy_ref, r_ref, g_ref, *outs):
        i = pl.program_id(0)
        dr, dg, db = _ln_bwd_math(dy_ref[...], r_ref[...], g_ref[...])
        outs[0][...] = dr
        if want_bf16:
            outs[1][...] = dr.astype(BF16)
        st_ref = outs[-1]

        @pl.when(i == 0)
        def _():
            st_ref[...] = jnp.zeros_like(st_ref)

        st_ref[0:1, :] += dg
        st_ref[1:2, :] += db

    tile = pl.BlockSpec((t, D_MODEL), lambda i: (i, 0))
    out_specs = [tile] + ([tile] if want_bf16 else []) + [pl.BlockSpec((8, D_MODEL), lambda i: (0, 0))]
    out_shape = ([jax.ShapeDtypeStruct((s, D_MODEL), F32)]
                 + ([jax.ShapeDtypeStruct((s, D_MODEL), BF16)] if want_bf16 else [])
                 + [jax.ShapeDtypeStruct((8, D_MODEL), F32)])
    return pl.pallas_call(
        body, name=name, grid=(s // t,),
        in_specs=[tile, tile, pl.BlockSpec((1, D_MODEL), lambda i: (0, 0))],
        out_specs=out_specs, out_shape=out_shape,
        compiler_params=_cparams(dimension_semantics=("arbitrary",)),
    )(dy, r, g)


PROJ_COLS = 256


def _proj_segments():
    wd = DIL_SLOTS * HEAD_DIM
    segs = [(1, [(0, 1), (PROJ_COLS, 1), (2 * PROJ_COLS, 2)])]
    for gi, dil in enumerate(DILATIONS):
        blocks = []
        for part, kind in enumerate((1, 1, 0)):
            col = A_WIDTH + part * B_QKV + gi * wd
            blocks += [(col, kind), (col + PROJ_COLS, kind)]
        segs.append((dil, blocks))
    return segs


PROJ_SEGMENTS = _proj_segments()


def _dh0_ln_in(dz, w_t, dr1, x, ln_in_g, *, t, comm=None):
    s, k = dz.shape

    def body(dz_ref, w_ref, dr1_ref, x_ref, g_ref, gx_ref, st_ref):
        i = pl.program_id(0)

        @pl.when(i == 0)
        def _():
            st_ref[...] = jnp.zeros_like(st_ref)

        dh0 = _dot(dz_ref[...], w_ref[...]) + ALPHA * dr1_ref[...]
        dx, dg, db = _ln_bwd_math(dh0, x_ref[...], g_ref[...])
        gx_ref[...] = dx
        st_ref[0:1, :] += dg
        st_ref[1:2, :] += db

    tile = pl.BlockSpec((t, D_MODEL), lambda i: (i, 0))
    return _pcall(
        body, name="dh0_ln_in", grid=(s // t,),
        in_specs=[pl.BlockSpec((t, k), lambda i: (i, 0)),
                  pl.BlockSpec((k, D_MODEL), lambda i: (0, 0), pipeline_mode=pl.Buffered(1)),
                  tile, tile, pl.BlockSpec((1, D_MODEL), lambda i: (0, 0))],
        out_specs=[tile, pl.BlockSpec((8, D_MODEL), lambda i: (0, 0))],
        out_shape=[jax.ShapeDtypeStruct((s, D_MODEL), F32), jax.ShapeDtypeStruct((8, D_MODEL), F32)],
        args=[dz, w_t, dr1, x, ln_in_g], dims=("arbitrary",), comm=comm)


def _ln_in_fwd(x, g, b, *, t, comm=None):
    s = x.shape[0]

    def body(x_ref, g_ref, b_ref, o_ref):
        o_ref[...] = _ln(x_ref[...], g_ref[...], b_ref[...]).astype(BF16)

    row = pl.BlockSpec((1, D_MODEL), lambda i: (0, 0))
    tile = pl.BlockSpec((t, D_MODEL), lambda i: (i, 0))
    outs, couts = _pcall(body, name="ln_in_fwd", grid=(s // t,), in_specs=[tile, row, row], out_specs=[tile],
                         out_shape=[jax.ShapeDtypeStruct((s, D_MODEL), BF16)], args=[x, g, b], dims=("parallel",),
                         comm=comm)
    return outs[0], couts


def _proj_all(h0b, w_t, cs, e_mat, *, t, comm=None):
    s = h0b.shape[0]
    cb = PROJ_COLS
    halves = cb // LANES

    def body(h_ref, w_ref, cs_ref, e_ref, *rest):
        z_refs, scr = rest[:-1], rest[-1]
        h = h_ref[...]
        ta, tb, tc = (jnp.tile(tab, (1, halves)) for tab in _rope_tabs(cs_ref[...], e_ref[...]))
        lane = lax.broadcasted_iota(jnp.int32, (t, cb), 1)
        slot = 0
        for z_ref, (dil, blocks) in zip(z_refs, PROJ_SEGMENTS):
            for jb, (col, kind) in enumerate(blocks):
                acc = _dot_nt(h, w_ref[col:col + cb, :])
                if kind:
                    z = acc * ta + (pltpu.roll(acc, cb - 8, 1) * tb + pltpu.roll(acc, 8, 1) * tc)
                    if kind == 2:
                        z = jnp.where(lane < LANES, z, acc)
                else:
                    z = acc
                if dil == 1:
                    z_ref[0, :, cb * jb:cb * (jb + 1)] = z.astype(BF16)
                    continue
                for half in range(halves):
                    scr[slot, half] = z[:, half * LANES:(half + 1) * LANES]
                for c in range(dil):
                    for half in range(halves):
                        rows = scr[slot, half, pl.ds(c, t // dil, stride=dil), :]
                        z_ref[c, :, cb * jb + half * LANES:cb * jb + (half + 1) * LANES] = rows.astype(BF16)
                slot = 1 - slot

    widths = [cb * len(blocks) for _, blocks in PROJ_SEGMENTS]
    dils = [dil for dil, _ in PROJ_SEGMENTS]
    outs, couts = _pcall(
        body, name="proj_all", grid=(s // t,),
        in_specs=[pl.BlockSpec((t, D_MODEL), lambda i: (i, 0)),
                  pl.BlockSpec((IN_WIDTH, D_MODEL), lambda i: (0, 0), pipeline_mode=pl.Buffered(1)),
                  pl.BlockSpec((t, ROT_DIM), lambda i: (i, 0)), pl.BlockSpec((ROT_DIM, 3 * LANES), lambda i: (0, 0))],
        out_specs=[pl.BlockSpec((dil, t // dil, wd), lambda i: (0, i, 0)) for dil, wd in zip(dils, widths)],
        out_shape=[jax.ShapeDtypeStruct((dil, s // dil, wd), BF16) for dil, wd in zip(dils, widths)],
        args=[h0b, w_t, cs, e_mat], scratch_shapes=[pltpu.VMEM((2, halves, t, LANES), F32)],
        dims=("parallel",), comm=comm)
    return outs, couts


PAIR = 2 * HEAD_DIM


def _place_head(x2, src_pos, dst_pos):
    hi = lax.broadcasted_iota(jnp.int32, x2.shape, 1) >= HEAD_DIM
    src = x2 if src_pos == dst_pos else pltpu.roll(x2, HEAD_DIM, 1)
    return jnp.where(hi == (dst_pos == 1), src, jnp.zeros_like(src))


def _band_mask_t(row0, tq, w, seq_len):
    tk = tq + 2 * w
    kk = lax.broadcasted_iota(jnp.int32, (tk, tq), 0)
    qq = lax.broadcasted_iota(jnp.int32, (tk, tq), 1)
    kpos = row0 - w + kk
    return (jnp.abs(qq + w - kk) <= w) & (kpos >= 0) & (kpos < seq_len)


def _halo_kv_specs(t, w, hkv, n, seq_len, kcol, vcol):
    kw = hkv * HEAD_DIM
    per, last = t // w, seq_len // w - 1
    cur = lambda s, i: jnp.minimum(i, n - 1)
    specs = []
    for c in (kcol, vcol):
        specs += [pl.BlockSpec((None, w, kw), lambda s, i, c=c: (s, jnp.maximum(cur(s, i) * per - 1, 0), c)),
                  pl.BlockSpec((None, t, kw), lambda s, i, c=c: (s, cur(s, i), c)),
                  pl.BlockSpec((None, w, kw), lambda s, i, c=c: (s, jnp.minimum((cur(s, i) + 1) * per, last), c))]
    return specs, cur


def _pair_kv(kfull, vfull, qp, rep, krows):
    ks, vs, a_of = [], [], []
    for pos in range(2):
        g = (2 * qp + pos) // rep
        a_of.append(g // 2)
        ks.append(_place_head(kfull[g // 2][krows], g % 2, pos))
        vs.append(_place_head(vfull[g // 2][krows], g % 2, pos))
    assert a_of[0] == a_of[1]
    return jnp.concatenate(ks, axis=0), jnp.concatenate(vs, axis=0), a_of[0]


def _swa_fwd_p(qkv, *, qcol, kcol, vcol, hq, hkv, w, tq, sub, sink, name, comm=None):
    nseq, seq_len, _ = qkv.shape
    t = tq * sub
    n = seq_len // t
    rep = hq // hkv
    tk = tq + 2 * w
    kv_specs, cur = _halo_kv_specs(t, w, hkv, n, seq_len, kcol, vcol)

    def body(*refs):
        if sink is not None:
            sink_ref, refs = refs[0], refs[1:]
        q_ref, kp_ref, kc_ref, kn_ref, vp_ref, vc_ref, vn_ref, o_ref, lse_ref = refs
        i = pl.program_id(1)
        kfull, vfull = [], []
        for a in range(hkv // 2):
            ls = slice(a * PAIR, (a + 1) * PAIR)
            kfull.append(jnp.concatenate([kp_ref[:, ls], kc_ref[:, ls], kn_ref[:, ls]], axis=0) * 0.125)
            vfull.append(jnp.concatenate([vp_ref[:, ls], vc_ref[:, ls], vn_ref[:, ls]], axis=0))
        row_hi = lax.broadcasted_iota(jnp.int32, (PAIR, tq), 0) >= HEAD_DIM
        for jj in range(sub):
            rows = slice(jj * tq, (jj + 1) * tq)
            mask_t = _band_mask_t(i * t + jj * tq, tq, w, seq_len)
            o_t, lse_rows = [], []
            for qp in range(hq // 2):
                kst, vst, _ = _pair_kv(kfull, vfull, qp, rep, slice(jj * tq, jj * tq + tk))
                s2 = _dot_nt(kst, q_ref[rows, qp * PAIR:(qp + 1) * PAIR])
                ps, dens = [], []
                for pos in range(2):
                    h = 2 * qp + pos
                    s_t = jnp.where(mask_t, s2[pos * tk:(pos + 1) * tk], NEG_INF)
                    m = jnp.max(s_t, axis=0, keepdims=True)
                    if sink is not None:
                        m = jnp.maximum(m, sink_ref[0, h])
                    p_t = jnp.exp(s_t - m)
                    den = jnp.sum(p_t, axis=0, keepdims=True)
                    if sink is not None:
                        den = den + jnp.exp(sink_ref[0, h] - m)
                    ps.append(p_t.astype(BF16))
                    dens.append(den)
                    lse_rows.append(m + jnp.log(den))
                both = _dot_tn(vst, jnp.concatenate(ps, axis=0))
                o_t.append(both / jnp.where(row_hi, dens[1], dens[0]))
            o_ref[rows, :] = jnp.concatenate(o_t, axis=0).T
            lse_ref[:, rows] = jnp.concatenate(lse_rows, axis=0)

    in_specs = [pl.BlockSpec((None, t, hq * HEAD_DIM), lambda s, i: (s, i, qcol))] + kv_specs
    args = [qkv] * 7
    if sink is not None:
        in_specs = [pl.BlockSpec(memory_space=pltpu.SMEM)] + in_specs
        args = [sink] + args
    (o, lse), couts = _pcall(
        body, name=name, grid=(nseq, n), in_specs=in_specs,
        out_specs=[pl.BlockSpec((None, t, hq * HEAD_DIM), lambda s, i: (s, i, 0)),
                   pl.BlockSpec((None, hq, t), lambda s, i: (s, 0, i))],
        out_shape=[jax.ShapeDtypeStruct((nseq, seq_len, hq * HEAD_DIM), F32),
                   jax.ShapeDtypeStruct((nseq, hq, seq_len), F32)],
        args=args, dims=("parallel", "parallel"), comm=comm)
    return o, lse, couts


def _swa_bwd_p(qkv, do, lse, delta, cs, e_mat, *, qcol, kcol, vcol, hq, hkv, w, tq, sub, sink, name, comm=None):
    nseq, seq_len, _ = qkv.shape
    t = tq * sub
    n = seq_len // t
    rep = hq // hkv
    qw, kw = hq * HEAD_DIM, hkv * HEAD_DIM
    tk = tq + 2 * w
    kv_specs, cur = _halo_kv_specs(t, w, hkv, n, seq_len, kcol, vcol)

    def body(*refs):
        if sink is not None:
            sink_ref, refs = refs[0], refs[1:]
        (q_ref, kp_ref, kc_ref, kn_ref, vp_ref, vc_ref, vn_ref, do_ref, lse_ref, dl_ref,
         cs_c, cs_p, e_ref) = refs[:13]
        outs = refs[13:]
        if sink is not None:
            dq_ref, dk_ref, dv_ref, dsink_ref, dk_acc, dv_acc, dk_win, dv_win = outs
        else:
            dq_ref, dk_ref, dv_ref, dk_acc, dv_acc, dk_win, dv_win = outs
        s_id = pl.program_id(0)
        i = pl.program_id(1)
        slot_p, slot_c, slot_n = (i + 2) % 3, i % 3, (i + 1) % 3

        if sink is not None:
            @pl.when((s_id == 0) & (i == 0))
            def _():
                dsink_ref[...] = jnp.zeros_like(dsink_ref)

        @pl.when(i < n)
        def _():
            dk_win[...] = jnp.zeros_like(dk_win)
            dv_win[...] = jnp.zeros_like(dv_win)
            kfull, vfull = [], []
            for a in range(hkv // 2):
                ls = slice(a * PAIR, (a + 1) * PAIR)
                kfull.append(jnp.concatenate([kp_ref[:, ls], kc_ref[:, ls], kn_ref[:, ls]], axis=0) * 0.125)
                vfull.append(jnp.concatenate([vp_ref[:, ls], vc_ref[:, ls], vn_ref[:, ls]], axis=0))
            for jj in range(sub):
                rows = slice(jj * tq, (jj + 1) * tq)
                krows = slice(jj * tq, jj * tq + tk)
                mask_t = _band_mask_t(i * t + jj * tq, tq, w, seq_len)
                dq_t = []
                dk2 = [None] * (hkv // 2)
                dv2 = [None] * (hkv // 2)
                for qp in range(hq // 2):
                    kst, vst, a = _pair_kv(kfull, vfull, qp, rep, krows)
                    q2 = q_ref[rows, qp * PAIR:(qp + 1) * PAIR]
                    do2 = do_ref[rows, qp * PAIR:(qp + 1) * PAIR]
                    s2 = _dot_nt(kst, q2)
                    dp2 = _dot_nt(vst, do2)
                    ds, ps, q_at, do_at = [], [], [], []
                    for pos in range(2):
                        h = 2 * qp + pos
                        e = (h // rep) % 2
                        half = slice(pos * tk, (pos + 1) * tk)
                        lse_h = lse_ref[h:h + 1, rows]
                        dl_h = dl_ref[h:h + 1, rows]
                        p_t = jnp.exp(jnp.where(mask_t, s2[half], NEG_INF) - lse_h)
                        ds.append((p_t * (dp2[half] - dl_h)).astype(BF16))
                        ps.append(p_t.astype(BF16))
                        q_at.append(_place_head(q2, pos, e) * 0.125)
                        do_at.append(_place_head(do2, pos, e))
                        if sink is not None:
                            ds_sink = -jnp.sum(jnp.exp(sink_ref[0, h] - lse_h) * dl_h)
                            dsink_ref[h:h + 1, :] += jnp.full((1, LANES), ds_sink, F32)
                    dq_t.append(_rope_rows(_dot_tn(kst, jnp.concatenate(ds, axis=0)),
                                           cs_c[0:ROT_DIM // 2, rows], cs_c[ROT_DIM // 2:ROT_DIM, rows], -1.0))
                    dk_part = _dot(jnp.concatenate(ds, axis=1), jnp.concatenate(q_at, axis=0))
                    dv_part = _dot(jnp.concatenate(ps, axis=1), jnp.concatenate(do_at, axis=0))
                    dk2[a] = dk_part if dk2[a] is None else dk2[a] + dk_part
                    dv2[a] = dv_part if dv2[a] is None else dv2[a] + dv_part
                for a in range(hkv // 2):
                    ls = slice(a * PAIR, (a + 1) * PAIR)
                    dk_win[krows, ls] += dk2[a]
                    dv_win[krows, ls] += dv2[a]
                dq_ref[rows, :] = jnp.concatenate(dq_t, axis=0).T.astype(BF16)

            @pl.when(i > 0)
            def _():
                dk_acc[slot_p, t - w:, :] += dk_win[:w, :]
                dv_acc[slot_p, t - w:, :] += dv_win[:w, :]

            @pl.when(i == 0)
            def _():
                dk_acc[slot_c] = dk_win[w:w + t, :]
                dv_acc[slot_c] = dv_win[w:w + t, :]

            @pl.when(i > 0)
            def _():
                dk_acc[slot_c] += dk_win[w:w + t, :]
                dv_acc[slot_c] += dv_win[w:w + t, :]

            dk_acc[slot_n] = jnp.zeros((t, kw), F32)
            dv_acc[slot_n] = jnp.zeros((t, kw), F32)
            dk_acc[slot_n, :w, :] = dk_win[w + t:, :]
            dv_acc[slot_n, :w, :] = dv_win[w + t:, :]

        @pl.when(i >= 1)
        def _():
            dk_ref[...] = _rope(dk_acc[slot_p], *_rope_tabs(cs_p[...], e_ref[...]), -1.0).astype(BF16)
            dv_ref[...] = dv_acc[slot_p].astype(BF16)

    row_c = lambda width: pl.BlockSpec((None, t, width), lambda s, i: (s, cur(s, i), 0))
    row_p = lambda width: pl.BlockSpec((None, t, width), lambda s, i: (s, jnp.maximum(i - 1, 0), 0))
    stat = pl.BlockSpec((None, hq, t), lambda s, i: (s, 0, cur(s, i)))
    cs_rows = pl.BlockSpec((None, ROT_DIM, t), lambda s, i: (s, 0, cur(s, i)))
    in_specs = ([pl.BlockSpec((None, t, qw), lambda s, i: (s, cur(s, i), qcol))] + kv_specs
                + [row_c(qw), stat, stat, cs_rows, row_p(ROT_DIM),
                   pl.BlockSpec((ROT_DIM, 3 * LANES), lambda s, i: (0, 0))])
    args = [qkv] * 7 + [do, lse, delta, cs.transpose(0, 2, 1), cs, e_mat]
    out_specs = [row_c(qw), row_p(kw), row_p(kw)]
    out_shape = [jax.ShapeDtypeStruct((nseq, seq_len, qw), BF16),
                 jax.ShapeDtypeStruct((nseq, seq_len, kw), BF16),
                 jax.ShapeDtypeStruct((nseq, seq_len, kw), BF16)]
    if sink is not None:
        in_specs = [pl.BlockSpec(memory_space=pltpu.SMEM)] + in_specs
        args = [sink] + args
        out_specs.append(pl.BlockSpec((8, LANES), lambda s, i: (0, 0)))
        out_shape.append(jax.ShapeDtypeStruct((8, LANES), F32))
    return _pcall(
        body, name=name, grid=(nseq, n + 1), in_specs=in_specs, out_specs=out_specs, out_shape=out_shape,
        scratch_shapes=[pltpu.VMEM((3, t, kw), F32), pltpu.VMEM((3, t, kw), F32),
                        pltpu.VMEM((t + 2 * w, kw), F32), pltpu.VMEM((t + 2 * w, kw), F32)], args=args,
        dims=("arbitrary", "arbitrary"), comm=comm)


def _rms_parts(o, g):
    ms = jnp.mean(o * o, axis=-1, keepdims=True) + LN_EPS
    rinv = lax.rsqrt(ms)
    return o * rinv * g, rinv


def _from_subsequences(ref, scr, dil, t):
    slabs = ref.shape[-1] // LANES
    if dil == 1:
        return ref[0].astype(F32)
    for c in range(dil):
        for sl in range(slabs):
            scr[sl, pl.ds(c, t // dil, stride=dil), :] = ref[c, :, sl * LANES:(sl + 1) * LANES].astype(F32)
    return jnp.concatenate([scr[sl] for sl in range(slabs)], axis=1)


def _to_subsequences(val, ref, scr, dil, t):
    slabs = val.shape[-1] // LANES
    if dil == 1:
        ref[0] = val.astype(ref.dtype)
        return
    for sl in range(slabs):
        scr[sl] = val[:, sl * LANES:(sl + 1) * LANES]
    for c in range(dil):
        for sl in range(slabs):
            ref[c, :, sl * LANES:(sl + 1) * LANES] = scr[sl, pl.ds(c, t // dil, stride=dil), :].astype(ref.dtype)


def _combine_fwd(out_a, o_g, lse_g, g_win, g_dil, w_mix_b, x, ln_in_g, ln_in_b, ln1_g, ln1_b, *, t):
    s = out_a.shape[1]
    wd = DIL_SLOTS * HEAD_DIM

    def body(oa_ref, o0, o1, o2, l0, l1, l2, gw_ref, gd_ref, w_ref, x_ref, g0, b0, g1, b1,
             mixed_ref, ob_ref, lt_ref, r1_ref, h1_ref, scr):
        ls = [l0[...], l1[...], l2[...]]
        mx = jnp.maximum(jnp.maximum(ls[0], ls[1]), ls[2])
        ws = [jnp.exp(l - mx) for l in ls]
        tot = ws[0] + ws[1] + ws[2]
        lt_ref[...] = mx + jnp.log(tot)
        ws = [x / tot for x in ws]
        og = [_from_subsequences(o_ref, scr.at[gi], dil, t)
              for gi, (o_ref, dil) in enumerate(zip((o0, o1, o2), DILATIONS))]
        parts = []
        for h in range(DIL_SLOTS):
            hs = slice(h * HEAD_DIM, (h + 1) * HEAD_DIM)
            parts.append(ws[0][:, h:h + 1] * og[0][:, hs] + ws[1][:, h:h + 1] * og[1][:, hs]
                         + ws[2][:, h:h + 1] * og[2][:, hs])
        ob = jnp.concatenate(parts, axis=1)
        ob_ref[...] = ob
        na, _ = _rms_parts(oa_ref[...], gw_ref[...])
        nb, _ = _rms_parts(ob, gd_ref[...])
        mixed = jnp.concatenate([na.astype(BF16), nb.astype(BF16)], axis=1)
        mixed_ref[...] = mixed
        h0 = _ln(x_ref[...], g0[...], b0[...])
        r1 = ALPHA * h0 + _dot(mixed, w_ref[...])
        r1_ref[...] = r1
        h1_ref[...] = _ln(r1, g1[...], b1[...]).astype(BF16)

    half = pl.BlockSpec((t, wd), lambda i: (i, 0))
    full = pl.BlockSpec((t, D_MODEL), lambda i: (i, 0))
    lanes = pl.BlockSpec((t, LANES), lambda i: (i, 0))
    grow = pl.BlockSpec((1, wd), lambda i: (0, 0))
    row = pl.BlockSpec((1, D_MODEL), lambda i: (0, 0))
    subseq = [pl.BlockSpec((dil, t // dil, wd), lambda i: (0, i, 0)) for dil in DILATIONS]
    return pl.pallas_call(
        body, name="combine_fwd", grid=(s // t,),
        in_specs=[pl.BlockSpec((None, t, wd), lambda i: (0, i, 0))] + subseq
        + [lanes, lanes, lanes, grow, grow, pl.BlockSpec((D_MODEL, D_MODEL), lambda i: (0, 0)), full,
           row, row, row, row],
        out_specs=[full, half, lanes, full, full],
        out_shape=[jax.ShapeDtypeStruct((s, D_MODEL), BF16), jax.ShapeDtypeStruct((s, wd), F32),
                   jax.ShapeDtypeStruct((s, LANES), F32), jax.ShapeDtypeStruct((s, D_MODEL), F32),
                   jax.ShapeDtypeStruct((s, D_MODEL), BF16)],
        scratch_shapes=[pltpu.VMEM((len(DILATIONS), wd // LANES, t, LANES), F32)],
        compiler_params=_cparams(dimension_semantics=("parallel",)),
    )(out_a, *o_g, *lse_g, g_win, g_dil, w_mix_b, x, ln_in_g, ln_in_b, ln1_g, ln1_b)


def _combine_bwd(dr1b, w_mix_b, out_a, out_b, g_win, g_dil, *, t):
    s = out_b.shape[0]
    wd = DIL_SLOTS * HEAD_DIM

    def body(dr_ref, w_ref, oa_ref, ob_ref, gw_ref, gd_ref, doa_ref, dob0, dob1, dob2, dla_ref, dlb_ref, st_ref,
             scr):
        i = pl.program_id(0)
        dm = _dot_nt(dr_ref[...], w_ref[...])

        @pl.when(i == 0)
        def _():
            st_ref[...] = jnp.zeros_like(st_ref)

        lane = lax.broadcasted_iota(jnp.int32, (t, LANES), 1)
        for idx, (o_ref, g_ref, dl_ref) in enumerate(((oa_ref, gw_ref, dla_ref), (ob_ref, gd_ref, dlb_ref))):
            o = o_ref[...]
            dn = dm[:, idx * wd:(idx + 1) * wd]
            _, rinv = _rms_parts(o, g_ref[...])
            wv = dn * g_ref[...]
            do = rinv * wv - o * (rinv * rinv * rinv) * jnp.mean(wv * o, axis=-1, keepdims=True)
            st_ref[idx:idx + 1, :] += jnp.sum(dn * o * rinv, axis=0, keepdims=True)
            if idx == 0:
                doa_ref[...] = do.astype(BF16)
            else:
                for do_ref, dil in zip((dob0, dob1, dob2), DILATIONS):
                    _to_subsequences(do, do_ref, scr, dil, t)
            prod = do * o
            acc = jnp.zeros((t, LANES), F32)
            for h in range(DIL_SLOTS):
                hs = slice(h * HEAD_DIM, (h + 1) * HEAD_DIM)
                acc = jnp.where(lane == h, jnp.sum(prod[:, hs], axis=1, keepdims=True), acc)
            dl_ref[...] = acc

    half = pl.BlockSpec((t, wd), lambda i: (i, 0))
    lanes = pl.BlockSpec((t, LANES), lambda i: (i, 0))
    grow = pl.BlockSpec((1, wd), lambda i: (0, 0))
    a_spec = pl.BlockSpec((None, t, wd), lambda i: (0, i, 0))
    subseq = [pl.BlockSpec((dil, t // dil, wd), lambda i: (0, i, 0)) for dil in DILATIONS]
    doa, dob0, dob1, dob2, dla, dlb, st = pl.pallas_call(
        body, name="combine_bwd", grid=(s // t,),
        in_specs=[pl.BlockSpec((t, D_MODEL), lambda i: (i, 0)), pl.BlockSpec((D_MODEL, D_MODEL), lambda i: (0, 0)),
                  a_spec, half, grow, grow],
        out_specs=[a_spec] + subseq + [lanes, lanes, pl.BlockSpec((8, wd), lambda i: (0, 0))],
        out_shape=[jax.ShapeDtypeStruct((1, s, wd), BF16)]
        + [jax.ShapeDtypeStruct((dil, s // dil, wd), BF16) for dil in DILATIONS]
        + [jax.ShapeDtypeStruct((s, LANES), F32), jax.ShapeDtypeStruct((s, LANES), F32),
           jax.ShapeDtypeStruct((8, wd), F32)],
        scratch_shapes=[pltpu.VMEM((wd // LANES, t, LANES), F32)],
        compiler_params=_cparams(dimension_semantics=("arbitrary",)),
    )(dr1b, w_mix_b, out_a, out_b, g_win, g_dil)
    return doa, [dob0, dob1, dob2], dla, dlb, st


def _assemble_dz(dqa, dka, dva, dqs, dks, dvs, *, t):
    s = dqa.shape[1]
    wd = DIL_SLOTS * HEAD_DIM

    def body(*refs):
        a_refs, g_refs, o_ref, scr = refs[:3], refs[3:12], refs[12], refs[13]
        col = 0
        for r in a_refs:
            o_ref[:, col:col + r.shape[-1]] = r[...]
            col += r.shape[-1]
        for part in range(3):
            for gi, dil in enumerate(DILATIONS):
                val = _from_subsequences(g_refs[3 * part + gi], scr, dil, t)
                o_ref[:, col:col + wd] = val.astype(BF16)
                col += wd

    a_specs = [pl.BlockSpec((None, t, a.shape[-1]), lambda i: (0, i, 0)) for a in (dqa, dka, dva)]
    g_specs = [pl.BlockSpec((dil, t // dil, wd), lambda i: (0, i, 0)) for _ in range(3) for dil in DILATIONS]
    return pl.pallas_call(
        body, name="assemble_dz", grid=(s // t,), in_specs=a_specs + g_specs,
        out_specs=pl.BlockSpec((t, IN_WIDTH), lambda i: (i, 0)),
        out_shape=jax.ShapeDtypeStruct((s, IN_WIDTH), BF16),
        scratch_shapes=[pltpu.VMEM((wd // LANES, t, LANES), F32)],
        compiler_params=_cparams(dimension_semantics=("parallel",)),
    )(dqa, dka, dva, *dqs, *dks, *dvs)


def _mem_fwd(mem, g, b, wk_b, wv_b):
    ml = mem.shape[0]

    def body(mem_ref, g_ref, b_ref, wk_ref, wv_ref, mn_ref, kx_ref, vx_ref):
        mn = _ln(mem_ref[...], g_ref[...], b_ref[...]).astype(BF16)
        mn_ref[...] = mn
        kx_ref[...] = _dot(mn, wk_ref[...]).astype(BF16)
        vx_ref[...] = _dot(mn, wv_ref[...]).astype(BF16)

    sh = jax.ShapeDtypeStruct((ml, D_MODEL), BF16)
    return pl.pallas_call(body, name="mem_fwd", out_shape=[sh, sh, sh], compiler_params=_cparams())(
        mem, g, b, wk_b, wv_b)


def _mem_bwd(dkx, dvx, mem, g, b, wk_b, wv_b):
    def body(dk_ref, dv_ref, mem_ref, g_ref, b_ref, wk_ref, wv_ref, dwk_ref, dwv_ref, st_ref):
        mem_v = mem_ref[...]
        mn = _ln(mem_v, g_ref[...], b_ref[...]).astype(BF16)
        dkb = dk_ref[...].astype(BF16)
        dvb = dv_ref[...].astype(BF16)
        dwk_ref[...] = _dot_tn(mn, dkb)
        dwv_ref[...] = _dot_tn(mn, dvb)
        dmn = _dot_nt(dkb, wk_ref[...]) + _dot_nt(dvb, wv_ref[...])
        _, dg, db = _ln_bwd_math(dmn, mem_v, g_ref[...])
        st_ref[...] = jnp.zeros_like(st_ref)
        st_ref[0:1, :] = dg
        st_ref[1:2, :] = db

    sw = jax.ShapeDtypeStruct((D_MODEL, D_MODEL), F32)
    return pl.pallas_call(body, name="mem_bwd", out_shape=[sw, sw, jax.ShapeDtypeStruct((8, D_MODEL), F32)],
                          compiler_params=_cparams())(dkx, dvx, mem, g, b, wk_b, wv_b)


def _xattn_fwd(h1b, r1, kx, vx, wq_b, wo_b, ln1_g, ln1_b, ln2_g, ln2_b, *, t):
    s = h1b.shape[0]
    scale = X_HEAD_DIM ** -0.5

    def body(h_ref, r1_ref, kx_ref, vx_ref, wq_ref, wo_ref, g1, b1, g2, b2, r2_ref, h2_ref, qx_ref, ox_ref, lse_ref):
        qxb = _dot(h_ref[...], wq_ref[...]).astype(BF16)
        qx_ref[...] = qxb
        lane = lax.broadcasted_iota(jnp.int32, (t, LANES), 1)
        lse_acc = jnp.zeros((t, LANES), F32)
        parts = []
        for h in range(X_HEADS):
            hs = slice(h * X_HEAD_DIM, (h + 1) * X_HEAD_DIM)
            sc = _dot_nt(qxb[:, hs] * scale, kx_ref[:, hs])
            m = jnp.max(sc, axis=1, keepdims=True)
            p = jnp.exp(sc - m)
            den = jnp.sum(p, axis=1, keepdims=True)
            parts.append(_dot(p.astype(BF16), vx_ref[:, hs]) / den)
            lse_acc = jnp.where(lane == h, m + jnp.log(den), lse_acc)
        lse_ref[...] = lse_acc
        oxb = jnp.concatenate(parts, axis=1).astype(BF16)
        ox_ref[...] = oxb
        h1 = _ln(r1_ref[...], g1[...], b1[...])
        r2 = ALPHA * h1 + _dot(oxb, wo_ref[...])
        r2_ref[...] = r2
        h2_ref[...] = _ln(r2, g2[...], b2[...]).astype(BF16)

    tile = pl.BlockSpec((t, D_MODEL), lambda i: (i, 0))
    row = pl.BlockSpec((1, D_MODEL), lambda i: (0, 0))
    full = lambda r: pl.BlockSpec((r, D_MODEL), lambda i: (0, 0))
    ml = kx.shape[0]
    bsh = jax.ShapeDtypeStruct((s, D_MODEL), BF16)
    return pl.pallas_call(
        body, name="xattn_fwd", grid=(s // t,),
        in_specs=[tile, tile, full(ml), full(ml), full(D_MODEL), full(D_MODEL), row, row, row, row],
        out_specs=[tile, tile, tile, tile, pl.BlockSpec((t, LANES), lambda i: (i, 0))],
        out_shape=[jax.ShapeDtypeStruct((s, D_MODEL), F32), bsh, bsh, bsh, jax.ShapeDtypeStruct((s, LANES), F32)],
        compiler_params=_cparams(dimension_semantics=("parallel",)),
    )(h1b, r1, kx, vx, wq_b, wo_b, ln1_g, ln1_b, ln2_g, ln2_b)


def _xattn_bwd(dr2, qxb, oxb, lse, kx, vx, wq_b, wo_b, r1, ln1_g, *, t, comm=None):
    s = dr2.shape[0]
    ml = kx.shape[0]
    scale = X_HEAD_DIM ** -0.5

    def body(dr2_ref, qx_ref, ox_ref, lse_ref, kx_ref, vx_ref, wq_ref, wo_ref, r1_ref, g1_ref,
             dr1_ref, dr1b_ref, dqx_ref, dkx_ref, dvx_ref, st_ref):
        i = pl.program_id(0)

        @pl.when(i == 0)
        def _():
            dkx_ref[...] = jnp.zeros_like(dkx_ref)
            dvx_ref[...] = jnp.zeros_like(dvx_ref)
            st_ref[...] = jnp.zeros_like(st_ref)

        dr2v = dr2_ref[...]
        dox = _dot_nt(dr2v.astype(BF16), wo_ref[...])
        parts = []
        for h in range(X_HEADS):
            hs = slice(h * X_HEAD_DIM, (h + 1) * X_HEAD_DIM)
            doh = dox[:, hs]
            dohb = doh.astype(BF16)
            dl = jnp.sum(doh * ox_ref[:, hs].astype(F32), axis=1, keepdims=True)
            qh = qx_ref[:, hs] * scale
            p = jnp.exp(_dot_nt(qh, kx_ref[:, hs]) - lse_ref[:, h:h + 1])
            dp = _dot_nt(dohb, vx_ref[:, hs])
            dsb = (p * (dp - dl)).astype(BF16)
            parts.append(_dot(dsb, kx_ref[:, hs]) * scale)
            dkx_ref[:, hs] += _dot_tn(dsb, qh)
            dvx_ref[:, hs] += _dot_tn(p.astype(BF16), dohb)
        dqxb = jnp.concatenate(parts, axis=1).astype(BF16)
        dqx_ref[...] = dqxb
        dh1 = _dot_nt(dqxb, wq_ref[...]) + ALPHA * dr2v
        dr1, dg, db = _ln_bwd_math(dh1, r1_ref[...], g1_ref[...])
        dr1_ref[...] = dr1
        dr1b_ref[...] = dr1.astype(BF16)
        st_ref[0:1, :] += dg
        st_ref[1:2, :] += db

    tile = pl.BlockSpec((t, D_MODEL), lambda i: (i, 0))
    full = lambda r: pl.BlockSpec((r, D_MODEL), lambda i: (0, 0))
    bsh = jax.ShapeDtypeStruct((s, D_MODEL), BF16)
    return _pcall(
        body, name="xattn_bwd", grid=(s // t,),
        in_specs=[tile, tile, tile, pl.BlockSpec((t, LANES), lambda i: (i, 0)), full(ml), full(ml),
                  full(D_MODEL), full(D_MODEL), tile, full(1)],
        out_specs=[tile, tile, tile, full(ml), full(ml), full(8)],
        out_shape=[jax.ShapeDtypeStruct((s, D_MODEL), F32), bsh, bsh,
                   jax.ShapeDtypeStruct((ml, D_MODEL), F32), jax.ShapeDtypeStruct((ml, D_MODEL), F32),
                   jax.ShapeDtypeStruct((8, D_MODEL), F32)],
        args=[dr2, qxb, oxb, lse, kx, vx, wq_b, wo_b, r1, ln1_g], dims=("arbitrary",), comm=comm)


def _halo_specs(t, s, width):
    tb8 = t // 8
    return [pl.BlockSpec((t, width), lambda i: (i, 0)),
            pl.BlockSpec((8, width), lambda i: (jnp.maximum(i * tb8 - 1, 0), 0)),
            pl.BlockSpec((8, width), lambda i: (jnp.minimum((i + 1) * tb8, s // 8 - 1), 0))]


def _halo_rows(i, n, prev_ref, next_ref):
    prev_row = jnp.where(i > 0, prev_ref[7:8, :], 0.0)
    next_row = jnp.where(i < n - 1, next_ref[0:1, :], 0.0)
    return prev_row, next_row


def _gelu_parts(gc):
    cdf = 0.5 * (1.0 + lax.erf(gc * (2.0 ** -0.5)))
    pdf = jnp.exp(-0.5 * gc * gc) * (1.0 / math.sqrt(2.0 * math.pi))
    return gc * cdf, cdf + gc * pdf


def _ffn_out(g, u, conv_w, conv_b, w_down_b, r2, target, ln2_g, ln2_b, ln3_g, ln3_b, *, t):
    s = r2.shape[0]
    n = s // t

    def body(g_ref, gp_ref, gn_ref, u_ref, cw_ref, cb_ref, w_ref, r2_ref, tg_ref, g2, b2, g3, b3,
             t_ref, dr_ref, drb_ref, st_ref):
        i = pl.program_id(0)

        @pl.when(i == 0)
        def _():
            st_ref[...] = jnp.zeros_like(st_ref)

        gv = g_ref[...]
        prev_row, next_row = _halo_rows(i, n, gp_ref, gn_ref)
        gm1, gp1 = _shift_rows(gv, prev_row, next_row)
        gc = gm1 * cw_ref[0:1, :] + gv * cw_ref[1:2, :] + gp1 * cw_ref[2:3, :] + cb_ref[...]
        act, _ = _gelu_parts(gc)
        tb = (act * u_ref[...]).astype(BF16)
        t_ref[...] = tb
        h2 = _ln(r2_ref[...], g2[...], b2[...])
        r3 = ALPHA * h2 + _dot(tb, w_ref[...])
        y = _ln(r3, g3[...], b3[...])
        err = y - tg_ref[...]
        loss = 0.5 * jnp.sum(jnp.mean(err * err, axis=-1, keepdims=True))
        dr, dg, db = _ln_bwd_math(err * (1.0 / D_MODEL), r3, g3[...])
        dr_ref[...] = dr
        drb_ref[...] = dr.astype(BF16)
        st_ref[0:1, :] += dg
        st_ref[1:2, :] += db
        st_ref[2:3, :] += jnp.full((1, D_MODEL), loss, F32)

    wide = pl.BlockSpec((t, D_FF), lambda i: (i, 0))
    tile = pl.BlockSpec((t, D_MODEL), lambda i: (i, 0))
    row = pl.BlockSpec((1, D_MODEL), lambda i: (0, 0))
    return pl.pallas_call(
        body, name="ffn_out", grid=(n,),
        in_specs=_halo_specs(t, s, D_FF) + [wide, pl.BlockSpec((3, D_FF), lambda i: (0, 0)),
                                            pl.BlockSpec((1, D_FF), lambda i: (0, 0)),
                                            pl.BlockSpec((D_FF, D_MODEL), lambda i: (0, 0)),
                                            tile, tile, row, row, row, row],
        out_specs=[wide, tile, tile, pl.BlockSpec((8, D_MODEL), lambda i: (0, 0))],
        out_shape=[jax.ShapeDtypeStruct((s, D_FF), BF16), jax.ShapeDtypeStruct((s, D_MODEL), F32),
                   jax.ShapeDtypeStruct((s, D_MODEL), BF16), jax.ShapeDtypeStruct((8, D_MODEL), F32)],
        compiler_params=_cparams(dimension_semantics=("arbitrary",)),
    )(g, g, g, u, conv_w, conv_b, w_down_b, r2, target, ln2_g, ln2_b, ln3_g, ln3_b)


def _dh2_ln2(dgc, conv_w, du, w_gate_b, w_up_b, dr3, r2, ln2_g, *, t, comm=None):
    s = dgc.shape[0]
    n = s // t

    def body(d_ref, dp_ref, dn_ref, cw_ref, du_ref, wg_ref, wu_ref, dr3_ref, r2_ref, g2, dg_ref, dr_ref, drb_ref,
             st_ref):
        i = pl.program_id(0)

        @pl.when(i == 0)
        def _():
            st_ref[...] = jnp.zeros_like(st_ref)

        dv = d_ref[...]
        prev_row, next_row = _halo_rows(i, n, dp_ref, dn_ref)
        dm1, dp1 = _shift_rows(dv, prev_row, next_row)
        dgb = (dp1 * cw_ref[0:1, :] + dv * cw_ref[1:2, :] + dm1 * cw_ref[2:3, :]).astype(BF16)
        dg_ref[...] = dgb
        dh2 = _dot(dgb, wg_ref[...]) + _dot(du_ref[...], wu_ref[...]) + ALPHA * dr3_ref[...]
        dr, dg, db = _ln_bwd_math(dh2, r2_ref[...], g2[...])
        dr_ref[...] = dr
        drb_ref[...] = dr.astype(BF16)
        st_ref[0:1, :] += dg
        st_ref[1:2, :] += db

    wide = pl.BlockSpec((t, D_FF), lambda i: (i, 0))
    tile = pl.BlockSpec((t, D_MODEL), lambda i: (i, 0))
    wfull = pl.BlockSpec((D_FF, D_MODEL), lambda i: (0, 0), pipeline_mode=pl.Buffered(1))
    return _pcall(
        body, name="dh2_ln2", grid=(n,),
        in_specs=_halo_specs(t, s, D_FF) + [pl.BlockSpec((3, D_FF), lambda i: (0, 0)), wide, wfull, wfull,
                                            tile, tile, pl.BlockSpec((1, D_MODEL), lambda i: (0, 0))],
        out_specs=[wide, tile, tile, pl.BlockSpec((8, D_MODEL), lambda i: (0, 0))],
        out_shape=[jax.ShapeDtypeStruct((s, D_FF), BF16), jax.ShapeDtypeStruct((s, D_MODEL), F32),
                   jax.ShapeDtypeStruct((s, D_MODEL), BF16), jax.ShapeDtypeStruct((8, D_MODEL), F32)],
        args=[dgc, dgc, dgc, conv_w, du, w_gate_b, w_up_b, dr3, r2, ln2_g], dims=("arbitrary",), comm=comm)


def _conv_bwd_a(dr3b, w_down_b, g, u, conv_w, conv_b, *, t):
    s = g.shape[0]
    n = s // t

    def body(d_ref, w_ref, g_ref, gp_ref, gn_ref, u_ref, cw_ref, cb_ref, du_ref, dgc_ref, st_ref):
        i = pl.program_id(0)

        @pl.when(i == 0)
        def _():
            st_ref[...] = jnp.zeros_like(st_ref)

        dt = _dot_nt(d_ref[...], w_ref[...])
        gv = g_ref[...]
        prev_row, next_row = _halo_rows(i, n, gp_ref, gn_ref)
        gm1, gp1 = _shift_rows(gv, prev_row, next_row)
        gc = gm1 * cw_ref[0:1, :] + gv * cw_ref[1:2, :] + gp1 * cw_ref[2:3, :] + cb_ref[...]
        act, dact = _gelu_parts(gc)
        du_ref[...] = (dt * act).astype(BF16)
        dgc = dt * u_ref[...] * dact
        dgc_ref[...] = dgc
        st_ref[0:1, :] += jnp.sum(gm1 * dgc, axis=0, keepdims=True)
        st_ref[1:2, :] += jnp.sum(gv * dgc, axis=0, keepdims=True)
        st_ref[2:3, :] += jnp.sum(gp1 * dgc, axis=0, keepdims=True)
        st_ref[3:4, :] += jnp.sum(dgc, axis=0, keepdims=True)

    tile = pl.BlockSpec((t, D_FF), lambda i: (i, 0))
    return pl.pallas_call(
        body, name="conv_bwd_a", grid=(n,),
        in_specs=[pl.BlockSpec((t, D_MODEL), lambda i: (i, 0)), pl.BlockSpec((D_FF, D_MODEL), lambda i: (0, 0))]
        + _halo_specs(t, s, D_FF) + [tile, pl.BlockSpec((3, D_FF), lambda i: (0, 0)),
                                     pl.BlockSpec((1, D_FF), lambda i: (0, 0))],
        out_specs=[tile, tile, pl.BlockSpec((8, D_FF), lambda i: (0, 0))],
        out_shape=[jax.ShapeDtypeStruct((s, D_FF), BF16), jax.ShapeDtypeStruct((s, D_FF), F32),
                   jax.ShapeDtypeStruct((8, D_FF), F32)],
        compiler_params=_cparams(dimension_semantics=("arbitrary",)),
    )(dr3b, w_down_b, g, g, g, u, conv_w, conv_b)


def _to_residue(a, dil):
    s, w = a.shape
    return a.reshape(s // dil, dil, w).transpose(1, 0, 2)


def _stats_to_lanes(rows):
    dil, hq, l = rows.shape
    return jnp.pad(rows.transpose(2, 0, 1).reshape(dil * l, hq), ((0, 0), (0, LANES - hq)))


def _stats_to_rows(lanes, dil):
    s = lanes.shape[0]
    return lanes[:, :DIL_SLOTS].reshape(s // dil, dil, DIL_SLOTS).transpose(1, 2, 0)


def _rope_angles(positions):
    inv_freq = ROPE_THETA ** (-jnp.arange(0, ROT_DIM, 2, dtype=F32) / ROT_DIM)
    ang = positions.astype(F32)[:, None] * inv_freq
    return jnp.concatenate([jnp.cos(ang), jnp.sin(ang)], axis=1)


class _NoPlan:
    def gather(self, stage):
        return None

    def gathered(self, stage, couts, wb):
        pass

    def exchange(self, stage, grads):
        return None

    def exchanged(self, stage, couts):
        pass


def _local_step(x, mem, positions, target, wb, sp, plan=None, *, t_row=256, t_mm=512, tq_a=128, tq_b=128,
                sub_a=4, sub_b=4):
    s = x.shape[0]
    plan = plan or _NoPlan()
    cs = _rope_angles(positions)
    e_mat = _rope_select_matrix()

    h0b, couts = _ln_in_fwd(x, sp["ln_in_g"], sp["ln_in_b"], t=t_mm, comm=plan.gather("ln_in"))
    plan.gathered("ln_in", couts, wb)
    sp = dict(sp, conv_w=wb.get("conv_w", sp.get("conv_w")))
    (za, *zb), couts = _proj_all(h0b, wb["w_in"], cs, e_mat, t=min(2 * t_mm, s), comm=plan.gather("proj"))
    plan.gathered("proj", couts, wb)
    sub_a = max(1, min(sub_a, s // tq_a))
    subs_b = [max(1, min(sub_b, s // dil // tq_b)) for dil in DILATIONS]
    out_a, lse_a, couts = _swa_fwd_p(za, qcol=0, kcol=4, vcol=5, hq=WIN_Q_HEADS, hkv=WIN_KV_HEADS, w=WIN_HALF,
                                     tq=tq_a, sub=sub_a, sink=sp["attn_sink"], name="attn_a_fwd",
                                     comm=plan.gather("attn_a"))
    plan.gathered("attn_a", couts, wb)
    o_g, lse_g = [], []
    for gi in range(3):
        o, l, couts = _swa_fwd_p(zb[gi], qcol=0, kcol=1, vcol=2, hq=DIL_SLOTS, hkv=DIL_SLOTS, w=DIL_HALF, tq=tq_b,
                                 sub=subs_b[gi], sink=None, name=f"attn_b{gi}_fwd",
                                 comm=plan.gather(f"attn_b{gi}"))
        plan.gathered(f"attn_b{gi}", couts, wb)
        o_g.append(o)
        lse_g.append(_stats_to_lanes(l))
    mixed_b, out_b, lse_b, r1, h1b = _combine_fwd(
        out_a, o_g, lse_g, sp["g_win"], sp["g_dil"], wb["w_mix_out"], x, sp["ln_in_g"], sp["ln_in_b"],
        sp["ln1_g"], sp["ln1_b"], t=t_row)
    mem_nb, kx, vx = _mem_fwd(mem, sp["mem_ln_g"], sp["mem_ln_b"], wb["w_xk"], wb["w_xv"])
    r2, h2b, qxb, oxb, lse_x = _xattn_fwd(h1b, r1, kx, vx, wb["w_xq"], wb["w_xo"], sp["ln1_g"], sp["ln1_b"],
                                          sp["ln2_g"], sp["ln2_b"], t=t_mm)
    g = _mm(h2b, wb["w_gate"], mode="nt", out_dtype=F32, tm=t_mm, tn=D_FF, name="ff_gate")
    u = _mm(h2b, wb["w_up"], mode="nt", out_dtype=F32, tm=t_mm, tn=D_FF, name="ff_up")
    tb, dr3, dr3b, st3 = _ffn_out(g, u, sp["conv_w"], sp["conv_b"], wb["w_down"], r2, target, sp["ln2_g"],
                                  sp["ln2_b"], sp["ln3_g"], sp["ln3_b"], t=t_row)

    grads = {}
    du, dgc, st_conv = _conv_bwd_a(dr3b, wb["w_down"], g, u, sp["conv_w"], sp["conv_b"], t=t_row)
    tk = min(1024, s)
    grads["w_down"] = _mm(tb, dr3b, mode="tn", out_dtype=BF16, tm=D_FF // 2, tn=D_MODEL, tk=tk, name="dw_down")
    grads["w_up"] = _mm(du, h2b, mode="tn", out_dtype=BF16, tm=D_FF // 2, tn=D_MODEL, tk=tk, name="dw_up")
    (dg, dr2, dr2b, st2), couts = _dh2_ln2(dgc, sp["conv_w"], du, wb["w_gate"], wb["w_up"], dr3, r2, sp["ln2_g"],
                                           t=t_mm, comm=plan.exchange("dh2", grads))
    plan.exchanged("dh2", couts)
    grads["w_gate"] = _mm(dg, h2b, mode="tn", out_dtype=BF16, tm=D_FF // 2, tn=D_MODEL, tk=tk, name="dw_gate")

    (dr1, dr1b, dqxb, dkx, dvx, st1), couts = _xattn_bwd(
        dr2, qxb, oxb, lse_x, kx, vx, wb["w_xq"], wb["w_xo"], r1, sp["ln1_g"], t=t_mm,
        comm=plan.exchange("xattn", grads))
    plan.exchanged("xattn", couts)
    grads["w_xo"] = _mm(oxb, dr2b, mode="tn", out_dtype=BF16, tm=D_MODEL, tn=D_MODEL, tk=tk, name="dw_xo")
    grads["w_xq"] = _mm(h1b, dqxb, mode="tn", out_dtype=BF16, tm=D_MODEL, tn=D_MODEL, tk=tk, name="dw_xq")
    grads["w_xk"], grads["w_xv"], st_mem = _mem_bwd(dkx, dvx, mem, sp["mem_ln_g"], sp["mem_ln_b"],
                                                    wb["w_xk"], wb["w_xv"])

    grads["w_mix_out"] = _mm(mixed_b, dr1b, mode="tn", out_dtype=BF16, tm=D_MODEL, tn=D_MODEL, tk=tk,
                             name="dw_mix")
    do_a, do_b, dl_a, dl_b, st_mix = _combine_bwd(dr1b, wb["w_mix_out"], out_a, out_b, sp["g_win"], sp["g_dil"],
                                                  t=t_row)
    (dqa, dka, dva, dsink), couts = _swa_bwd_p(
        za, do_a, lse_a, _stats_to_rows(dl_a, 1), cs[None], e_mat, qcol=0, kcol=4, vcol=5, hq=WIN_Q_HEADS,
        hkv=WIN_KV_HEADS, w=WIN_HALF, tq=2 * tq_a, sub=max(1, sub_a // 2), sink=sp["attn_sink"], name="attn_a_bwd",
        comm=plan.exchange("attn_a", grads))
    plan.exchanged("attn_a", couts)
    dqs, dks, dvs = [], [], []
    for gi, dil in enumerate(DILATIONS):
        (dq, dk, dv), couts = _swa_bwd_p(
            zb[gi], do_b[gi], _stats_to_rows(lse_b, dil), _stats_to_rows(dl_b, dil),
            _to_residue(cs, dil), e_mat, qcol=0, kcol=1, vcol=2, hq=DIL_SLOTS, hkv=DIL_SLOTS, w=DIL_HALF, tq=tq_b,
            sub=subs_b[gi], sink=None, name=f"attn_b{gi}_bwd", comm=plan.exchange(f"attn_b{gi}", grads))
        plan.exchanged(f"attn_b{gi}", couts)
        dqs.append(dq)
        dks.append(dk)
        dvs.append(dv)
    dz = _assemble_dz(dqa, dka, dva, dqs, dks, dvs, t=t_row)
    grads["w_in"] = _mm(dz, h0b, mode="tn", out_dtype=BF16, tm=IN_WIDTH // 7, tn=D_MODEL, tk=tk, name="dw_in")
    (grad_x, st0), couts = _dh0_ln_in(dz, wb["w_in"], dr1, x, sp["ln_in_g"], t=t_mm,
                                      comm=plan.exchange("dh0", grads))
    plan.exchanged("dh0", couts)

    small = {
        "loss": st3[2:3, 0:1],
        "ln_in_g": st0[0:1], "ln_in_b": st0[1:2],
        "attn_sink": dsink[:, 0].reshape(1, WIN_Q_HEADS),
        "g_win": st_mix[0:1], "g_dil": st_mix[1:2],
        "ln1_g": st1[0:1], "ln1_b": st1[1:2],
        "mem_ln_g": st_mem[0:1], "mem_ln_b": st_mem[1:2],
        "ln2_g": st2[0:1], "ln2_b": st2[1:2],
        "conv_w": st_conv[0:3], "conv_b": st_conv[3:4],
        "ln3_g": st3[0:1], "ln3_b": st3[1:2],
    }
    return grad_x, grads, small


class _SiblingSwap:
    def __init__(self, arrays):
        self.inputs = list(arrays)
        n = len(arrays)
        self.out_shape = [jax.ShapeDtypeStruct(a.shape, a.dtype) for a in arrays]
        self.scratch = [pltpu.SemaphoreType.DMA((n,)), pltpu.SemaphoreType.DMA((n,))]

    def _copies(self, src, dst, sems):
        send_sems, recv_sems = sems
        x, y, c, _ = _place()
        return [pltpu.make_async_remote_copy(
            src_ref=src[a], dst_ref=dst[a], send_sem=send_sems.at[a], recv_sem=recv_sems.at[a],
            device_id=(x, y, 1 - c), device_id_type=MESH_IDS) for a in range(len(src))]

    def start(self, src, dst, sems):
        for cp in self._copies(src, dst, sems):
            cp.start()

    def wait(self, src, dst, sems):
        copies = self._copies(src, dst, sems)
        for cp in copies:
            cp.wait_recv()
        for cp in copies:
            cp.wait_send()


class _Both:
    def __init__(self, first, second):
        self.parts = (first, second)
        self.inputs = first.inputs + second.inputs
        self.out_shape = first.out_shape + second.out_shape
        self.scratch = first.scratch + second.scratch

    def _split(self, src, dst, sems):
        a = self.parts[0]
        ni, no, ns = len(a.inputs), len(a.out_shape), len(a.scratch)
        return ((src[:ni], dst[:no], sems[:ns]), (src[ni:], dst[no:], sems[ns:]))

    def start(self, src, dst, sems):
        for part, args in zip(self.parts, self._split(src, dst, sems)):
            part.start(*args)

    def wait(self, src, dst, sems):
        for part, args in zip(self.parts, self._split(src, dst, sems)):
            part.wait(*args)


def _row_tile(rows, cols, itemsize=4, budget=1 << 20):
    best = None
    for t in range(16, rows + 1, 16):
        if rows % t == 0 and t * cols * itemsize <= budget:
            best = t
    return best or rows


def _sum_slots(stack, *, name):
    n, r, c = stack.shape
    t = _row_tile(r, c)

    def body(s_ref, o_ref):
        acc = s_ref[0].astype(F32)
        for q in range(1, n):
            acc = acc + s_ref[q].astype(F32)
        o_ref[...] = acc

    return pl.pallas_call(
        body, name=name, grid=(r // t,), in_specs=[pl.BlockSpec((n, t, c), lambda i: (0, i, 0))],
        out_specs=pl.BlockSpec((t, c), lambda i: (i, 0)), out_shape=jax.ShapeDtypeStruct((r, c), F32),
        compiler_params=_cparams(dimension_semantics=("parallel",)),
    )(stack)


def _adamw(w, m, v, p, q, *, name):
    r, c = w.shape
    t = _row_tile(r, c, budget=1 << 20)

    def body(*refs):
        if q is None:
            w_ref, m_ref, v_ref, p_ref, g_ref, d_ref, nm_ref, nv_ref = refs
            g = p_ref[...]
        else:
            w_ref, m_ref, v_ref, p_ref, q_ref, g_ref, d_ref, nm_ref, nv_ref = refs
            g = p_ref[...] + q_ref[...]
        nm = ADAM_B1 * m_ref[...] + (1.0 - ADAM_B1) * g
        nv = ADAM_B2 * v_ref[...] + (1.0 - ADAM_B2) * (g * g)
        m_hat = nm / (1.0 - ADAM_B1 ** ADAM_STEP)
        v_hat = nv / (1.0 - ADAM_B2 ** ADAM_STEP)
        g_ref[...] = g
        d_ref[...] = -ADAM_LR * (m_hat / (jnp.sqrt(v_hat) + ADAM_EPS) + ADAM_WD * w_ref[...])
        nm_ref[...] = nm
        nv_ref[...] = nv

    tile = pl.BlockSpec((t, c), lambda i: (i, 0))
    args = [w, m, v, p] + ([] if q is None else [q])
    sh = jax.ShapeDtypeStruct((r, c), F32)
    return pl.pallas_call(
        body, name=name, grid=(r // t,), in_specs=[tile] * len(args), out_specs=[tile] * 4, out_shape=[sh] * 4,
        compiler_params=_cparams(dimension_semantics=("parallel",)),
    )(*args)


BIG = ("w_in", "w_mix_out", "w_xq", "w_xk", "w_xv", "w_xo", "w_gate", "w_up", "w_down")
COL_SHARDED = ("w_in", "w_gate", "w_up")
WEIGHTS = ("ln_in_g", "ln_in_b", "w_in", "attn_sink", "g_win", "g_dil", "w_mix_out", "ln1_g", "ln1_b",
           "mem_ln_g", "mem_ln_b", "w_xq", "w_xk", "w_xv", "w_xo", "ln2_g", "ln2_b", "w_gate", "w_up",
           "conv_w", "conv_b", "w_down", "ln3_g", "ln3_b")
SMALL = tuple(k for k in WEIGHTS if k not in BIG)
PACK_COLS = 1024
CONV_SHARD = D_FF // N_CHIPS
CONV_WIDTH_ROWS = 3
SMALL_ROWS = 32


GATHER_STAGES = {"ln_in": ("w_in", "conv_w"), "proj": ("w_mix_out", "w_xq", "w_xk", "w_xv", "w_xo", "w_up"),
                 "attn_a": ("w_gate",), "attn_b0": ("w_down",)}
EXCHANGE_STAGES = {"dh2": ("w_down",), "xattn": ("w_up",), "attn_a": ("w_gate", "w_xo", "w_xq"),
                   "attn_b0": ("w_xk", "w_xv", "w_mix_out"), "dh0": ("w_in",)}


def _full_weight(k, g4):
    return g4.reshape(N_CHIPS * g4.shape[1], g4.shape[2])


def _grad_parts(k, gk):
    gk = gk.astype(BF16)
    return gk.reshape(N_CHIPS, gk.shape[0] // N_CHIPS, gk.shape[1])


EARLY_SWAP_STAGE = "attn_b2"


class _Plan:
    def __init__(self, shards):
        self.shards = shards
        self.recv = {}
        self.chip_sums = {}
        self.sibling_sums = {}

    def gather(self, stage):
        names = GATHER_STAGES.get(stage)
        return _ChipGather([self.shards[k] for k in names]) if names else None

    def gathered(self, stage, couts, wb):
        for k, g4 in zip(GATHER_STAGES.get(stage, ()), couts):
            if k == "conv_w":
                taps = g4[:, :CONV_WIDTH_ROWS, :CONV_SHARD]
                wb[k] = taps.transpose(1, 0, 2).reshape(CONV_WIDTH_ROWS, D_FF)
            else:
                wb[k] = _full_weight(k, g4)

    def exchange(self, stage, grads):
        if stage == EARLY_SWAP_STAGE:
            self.early = [k for k in BIG if k in self.recv]
            for k in self.early:
                self.chip_sums[k] = _sum_slots(self.recv[k], name=f"sum_chips_{k}")
            return _SiblingSwap([self.chip_sums[k] for k in self.early])
        names = EXCHANGE_STAGES.get(stage)
        return _ChipExchange([_grad_parts(k, grads[k]) for k in names]) if names else None

    def exchanged(self, stage, couts):
        if stage == EARLY_SWAP_STAGE:
            self.sibling_sums.update(zip(self.early, couts))
            return
        for k, r4 in zip(EXCHANGE_STAGES.get(stage, ()), couts):
            self.recv[k] = r4


def _pack_rows(a):
    r, n = a.shape
    per = -(-n // PACK_COLS)
    return jnp.pad(a, ((0, 0), (0, per * PACK_COLS - n))).reshape(r * per, PACK_COLS)


def _unpack_rows(p, r, n):
    per = -(-n // PACK_COLS)
    return p.reshape(r, per * PACK_COLS)[:, :n]


def _pack(pieces, rows_total):
    cat = jnp.concatenate([_pack_rows(a) for a in pieces], axis=0)
    return jnp.pad(cat, ((0, rows_total - cat.shape[0]), (0, 0)))


def _unpack(p, shapes):
    out, at = [], 0
    for r, n in shapes:
        per = -(-n // PACK_COLS)
        out.append(_unpack_rows(p[at:at + r * per], r, n))
        at += r * per
    return out


def kernel(x, mem, positions, ln_in_g, ln_in_b, w_in, attn_sink, g_win, g_dil, w_mix_out, ln1_g, ln1_b, mem_ln_g, mem_ln_b, w_xq, w_xk, w_xv, w_xo, ln2_g, ln2_b, w_gate, w_up, conv_w, conv_b, w_down, ln3_g, ln3_b, loss_target, m_ln_in_g, m_ln_in_b, m_w_in, m_attn_sink, m_g_win, m_g_dil, m_w_mix_out, m_ln1_g, m_ln1_b, m_mem_ln_g, m_mem_ln_b, m_w_xq, m_w_xk, m_w_xv, m_w_xo, m_ln2_g, m_ln2_b, m_w_gate, m_w_up, m_conv_w, m_conv_b, m_w_down, m_ln3_g, m_ln3_b, v_ln_in_g, v_ln_in_b, v_w_in, v_attn_sink, v_g_win, v_g_dil, v_w_mix_out, v_ln1_g, v_ln1_b, v_mem_ln_g, v_mem_ln_b, v_w_xq, v_w_xk, v_w_xv, v_w_xo, v_ln2_g, v_ln2_b, v_w_gate, v_w_up, v_conv_w, v_conv_b, v_w_down, v_ln3_g, v_ln3_b):
    given = dict(locals())
    shape_of = {k: given[k].shape for k in WEIGHTS}
    as2d = lambda k, a: a.reshape(-1, a.shape[-1]).T if k in COL_SHARDED else a.reshape(-1, a.shape[-1])
    w2 = {k: as2d(k, given[k]) for k in WEIGHTS}
    m2 = {k: as2d(k, given["m_" + k]) for k in WEIGHTS}
    v2 = {k: as2d(k, given["v_" + k]) for k in WEIGHTS}
    chip = 2 * lax.axis_index("x") + lax.axis_index("y")

    shards = {k: w2[k].astype(BF16) for k in BIG}
    shards["conv_w"] = jnp.pad(w2["conv_w"], ((0, 16 - CONV_WIDTH_ROWS), (0, PACK_COLS - CONV_SHARD)))
    plan = _Plan(shards)
    sp = {k: w2[k] for k in SMALL if k != "conv_w"}

    grad_x, grads, small = _local_step(x[0], mem[0], positions[0], loss_target[0], {}, sp, plan)

    small_keys = ("loss",) + SMALL
    small_shapes = [small[k].shape for k in small_keys]
    small_pack = _pack([small[k] for k in small_keys], SMALL_ROWS)
    late = [k for k in BIG if k not in plan.chip_sums]
    for k in late:
        plan.chip_sums[k] = _sum_slots(plan.recv[k], name=f"sum_chips_{k}")
    *late_sibling, small_all = _comm_only(
        _Both(_SiblingSwap([plan.chip_sums[k] for k in late]), _ChipExchange([], small_pack)), "swap_and_small")
    plan.sibling_sums.update(zip(late, late_sibling))
    chip_sums = [plan.chip_sums[k] for k in BIG]
    sibling_sums = [plan.sibling_sums[k] for k in BIG]
    small_sum = _sum_slots(small_all, name="sum_small")
    small_g = dict(zip(small_keys, _unpack(small_sum, small_shapes)))
    loss = small_g["loss"][0, 0]

    res = {}
    for k, p, q in zip(BIG, chip_sums, sibling_sums):
        res[k] = _adamw(w2[k], m2[k], v2[k], p, q, name=f"adamw_{k}")
    small_g["conv_w"] = lax.dynamic_slice_in_dim(small_g["conv_w"], chip * CONV_SHARD, CONV_SHARD, axis=1)
    adam_shapes = [w2[k].shape for k in SMALL]
    packs = [_pack([d[k] for k in SMALL], SMALL_ROWS) for d in (w2, m2, v2, small_g)]
    small_res = [_unpack(o, adam_shapes) for o in _adamw(*packs, None, name="adamw_small")]
    for i, k in enumerate(SMALL):
        res[k] = tuple(o[i] for o in small_res)

    outs = [loss, grad_x[None]]
    for slot in range(4):
        outs += [(res[k][slot].T if k in COL_SHARDED else res[k][slot]).reshape(shape_of[k]) for k in WEIGHTS]
    return tuple(outs)
```

```python
import functools
import math

import jax
import jax.numpy as jnp
from jax import lax
from jax.experimental import pallas as pl
from jax.experimental.pallas import tpu as pltpu

F32 = jnp.float32
BF16 = jnp.bfloat16

D_MODEL = 1024
HEAD_DIM = 64
WIN_Q_HEADS = 8
WIN_KV_HEADS = 2
WIN_HALF = 128
DIL_SLOTS = 8
DILATIONS = (1, 4, 16)
DIL_HALF = 64
ROT_DIM = 16
ROPE_THETA = 500000.0
X_HEADS = 4
X_HEAD_DIM = 256
D_FF = 2816
A_Q = 512
A_KV = 128
A_WIDTH = A_Q + 2 * A_KV
B_QKV = 1536
IN_WIDTH = 5376
ALPHA = 2.0 ** 0.25
LN_EPS = 1e-5
NEG_INF = -1e30
LANES = 128
N_CHIPS = 4
N_DEV = 8

ADAM_LR = 0.001
ADAM_B1 = 0.9
ADAM_B2 = 0.999
ADAM_EPS = 1e-08
ADAM_WD = 0.01
ADAM_STEP = 10

VMEM_LIMIT = 56 * 1024 * 1024


def _cparams(**kw):
    return pltpu.CompilerParams(vmem_limit_bytes=VMEM_LIMIT, **kw)


def _dot(a, b):
    return lax.dot_general(a, b, (((1,), (0,)), ((), ())), preferred_element_type=F32)


def _dot_nt(a, b):
    return lax.dot_general(a, b, (((1,), (1,)), ((), ())), preferred_element_type=F32)


def _dot_tn(a, b):
    return lax.dot_general(a, b, (((0,), (0,)), ((), ())), preferred_element_type=F32)


def _ln(x, g, b):
    mu = jnp.mean(x, axis=-1, keepdims=True)
    xc = x - mu
    var = jnp.mean(xc * xc, axis=-1, keepdims=True)
    return xc * lax.rsqrt(var + LN_EPS) * g + b


def _ln_bwd_math(dy, r, g):
    mu = jnp.mean(r, axis=-1, keepdims=True)
    xc = r - mu
    var = jnp.mean(xc * xc, axis=-1, keepdims=True)
    rstd = lax.rsqrt(var + LN_EPS)
    xhat = xc * rstd
    dxhat = dy * g
    m1 = jnp.mean(dxhat, axis=-1, keepdims=True)
    m2 = jnp.mean(dxhat * xhat, axis=-1, keepdims=True)
    dr = rstd * (dxhat - m1 - xhat * m2)
    return dr, jnp.sum(dy * xhat, axis=0, keepdims=True), jnp.sum(dy, axis=0, keepdims=True)


def _rope(z, ta, tb, tc, sign):
    w = z.shape[1]
    reps = w // LANES
    a = jnp.tile(ta, (1, reps))
    b = jnp.tile(tb, (1, reps))
    c = jnp.tile(tc, (1, reps))
    return z * a + sign * (pltpu.roll(z, w - 8, 1) * b + pltpu.roll(z, 8, 1) * c)


def _shift_rows(x, prev_row, next_row):
    t = x.shape[0]
    sub = 8
    row = lax.broadcasted_iota(jnp.int32, (sub, x.shape[1]), 0)
    down, up = pltpu.roll(x, 1, 0), pltpu.roll(x, t - 1, 0)
    xm1 = jnp.concatenate([jnp.where(row == 0, prev_row, down[:sub]), down[sub:]], axis=0)
    xp1 = jnp.concatenate([up[:t - sub], jnp.where(row == sub - 1, next_row, up[t - sub:])], axis=0)
    return xm1, xp1


def _rope_tabs(cs, e_mat):
    hi = cs.astype(BF16)
    rest = cs - hi.astype(F32)
    mid = rest.astype(BF16)
    lo = (rest - mid.astype(F32)).astype(BF16)
    tabs = _dot(hi, e_mat) + _dot(mid, e_mat) + _dot(lo, e_mat)
    lane = lax.broadcasted_iota(jnp.int32, (cs.shape[0], LANES), 1)
    ones = jnp.where((lane & (HEAD_DIM - 1)) >= ROT_DIM, 1.0, 0.0)
    return tabs[:, :LANES] + ones, tabs[:, LANES:2 * LANES], tabs[:, 2 * LANES:]


def _rope_select_matrix():
    half = ROT_DIM // 2
    e = [[0.0] * (3 * LANES) for _ in range(ROT_DIM)]
    for lane in range(LANES):
        d = lane % HEAD_DIM
        if d < half:
            e[d][lane] = 1.0
            e[half + d][LANES + lane] = -1.0
        elif d < ROT_DIM:
            e[d - half][lane] = 1.0
            e[d][2 * LANES + lane] = 1.0
    return jnp.array(e, BF16)


def _rope_rows(x, cos_t, sin_t, sign):
    half = ROT_DIM // 2
    parts = []
    for base in (0, HEAD_DIM):
        r1, r2 = x[base:base + half], x[base + half:base + ROT_DIM]
        parts += [r1 * cos_t - sign * (r2 * sin_t), r2 * cos_t + sign * (r1 * sin_t), x[base + ROT_DIM:base + HEAD_DIM]]
    return jnp.concatenate(parts, axis=0)


MESH_IDS = pl.DeviceIdType.MESH
ANY = pl.BlockSpec(memory_space=pl.ANY)


def _place():
    x, y, c = lax.axis_index("x"), lax.axis_index("y"), lax.axis_index("c")
    other_chips = [(1 - x, y), (x, 1 - y), (1 - x, 1 - y)]
    return x, y, c, other_chips


class _ChipGather:
    def __init__(self, shards):
        self.inputs = list(shards)
        n = len(shards)
        self.out_shape = [jax.ShapeDtypeStruct((N_CHIPS,) + a.shape, a.dtype) for a in shards]
        self.scratch = [pltpu.SemaphoreType.DMA((6 * n,)), pltpu.SemaphoreType.DMA((6 * n,)),
                        pltpu.SemaphoreType.DMA((n,))]

    def _copies(self, src, dst, sems):
        send_sems, recv_sems, local_sems = sems
        x, y, c, chips = _place()
        mine = 2 * x + y
        n = len(src)
        local, sends, recvs, passes, pass_recvs = [], [], [], [], []
        for a in range(n):
            half = src[a].shape[0] // 2
            my_rows, other_rows = pl.ds(c * half, half), pl.ds((1 - c) * half, half)
            local.append(pltpu.make_async_copy(src[a], dst[a].at[mine], local_sems.at[a]))
            for j, (px, py) in enumerate(chips):
                k, k2, slot = 3 * a + j, 3 * n + 3 * a + j, 2 * px + py
                sends.append(pltpu.make_async_remote_copy(
                    src_ref=src[a].at[my_rows], dst_ref=dst[a].at[mine, my_rows], send_sem=send_sems.at[k],
                    recv_sem=recv_sems.at[k], device_id=(px, py, c), device_id_type=MESH_IDS))
                recvs.append(pltpu.make_async_remote_copy(
                    src_ref=src[a].at[my_rows], dst_ref=dst[a].at[slot, my_rows], send_sem=send_sems.at[k],
                    recv_sem=recv_sems.at[k], device_id=(px, py, c), device_id_type=MESH_IDS))
                passes.append(pltpu.make_async_remote_copy(
                    src_ref=dst[a].at[slot, my_rows], dst_ref=dst[a].at[slot, my_rows], send_sem=send_sems.at[k2],
                    recv_sem=recv_sems.at[k2], device_id=(x, y, 1 - c), device_id_type=MESH_IDS))
                pass_recvs.append(pltpu.make_async_remote_copy(
                    src_ref=dst[a].at[slot, my_rows], dst_ref=dst[a].at[slot, other_rows],
                    send_sem=send_sems.at[k2], recv_sem=recv_sems.at[k2], device_id=(x, y, 1 - c),
                    device_id_type=MESH_IDS))
        return local, sends, recvs, passes, pass_recvs

    def start(self, src, dst, sems):
        local, sends, _, _, _ = self._copies(src, dst, sems)
        for cp in local + sends:
            cp.start()

    def wait(self, src, dst, sems):
        local, sends, recvs, passes, pass_recvs = self._copies(src, dst, sems)
        for idx, landed in enumerate(recvs):
            landed.wait_recv()
            if passes:
                passes[idx].start()
        for cp in pass_recvs:
            cp.wait_recv()
        for cp in sends + passes:
            cp.wait_send()
        for cp in local:
            cp.wait()


class _ChipExchange:
    def __init__(self, parts, small=None):
        self.inputs = list(parts) + ([small] if small is not None else [])
        self.n = len(parts)
        self.has_small = small is not None
        self.out_shape = [jax.ShapeDtypeStruct(a.shape, a.dtype) for a in parts]
        n_sem, n_loc = 3 * self.n, self.n
        if self.has_small:
            self.out_shape.append(jax.ShapeDtypeStruct((N_DEV,) + small.shape, small.dtype))
            n_sem, n_loc = n_sem + N_DEV - 1, n_loc + 1
        self.scratch = [pltpu.SemaphoreType.DMA((n_sem,)), pltpu.SemaphoreType.DMA((n_sem,)),
                        pltpu.SemaphoreType.DMA((n_loc,))]

    def _copies(self, src, dst, sems):
        send_sems, recv_sems, local_sems = sems
        x, y, c, chips = _place()
        mine = 2 * x + y
        n = self.n
        local, sends, recvs = [], [], []
        for a in range(n):
            local.append(pltpu.make_async_copy(src[a].at[mine], dst[a].at[mine], local_sems.at[a]))
            for j, (px, py) in enumerate(chips):
                k = 3 * a + j
                sends.append(pltpu.make_async_remote_copy(
                    src_ref=src[a].at[2 * px + py], dst_ref=dst[a].at[mine], send_sem=send_sems.at[k],
                    recv_sem=recv_sems.at[k], device_id=(px, py, c), device_id_type=MESH_IDS))
                recvs.append(pltpu.make_async_remote_copy(
                    src_ref=src[a].at[mine], dst_ref=dst[a].at[2 * px + py], send_sem=send_sems.at[k],
                    recv_sem=recv_sems.at[k], device_id=(px, py, c), device_id_type=MESH_IDS))
        if self.has_small:
            me_dev = 4 * x + 2 * y + c
            local.append(pltpu.make_async_copy(src[n], dst[n].at[me_dev], local_sems.at[n]))
            for mask in range(1, N_DEV):
                px, py, pc = x ^ ((mask >> 2) & 1), y ^ ((mask >> 1) & 1), c ^ (mask & 1)
                k = 3 * n + mask - 1
                sends.append(pltpu.make_async_remote_copy(
                    src_ref=src[n], dst_ref=dst[n].at[me_dev], send_sem=send_sems.at[k], recv_sem=recv_sems.at[k],
                    device_id=(px, py, pc), device_id_type=MESH_IDS))
                recvs.append(pltpu.make_async_remote_copy(
                    src_ref=src[n], dst_ref=dst[n].at[4 * px + 2 * py + pc], send_sem=send_sems.at[k],
                    recv_sem=recv_sems.at[k], device_id=(px, py, pc), device_id_type=MESH_IDS))
        return local, sends, recvs, [], []

    start = _ChipGather.start
    wait = _ChipGather.wait


def _pcall(body, *, name, grid, in_specs, out_specs, out_shape, args, scratch_shapes=(), dims=None, comm=None):
    in_specs, out_specs, out_shape = list(in_specs), list(out_specs), list(out_shape)
    scratch_shapes = list(scratch_shapes)
    if comm is None:
        outs = pl.pallas_call(
            body, name=name, grid=grid, in_specs=in_specs, out_specs=out_specs, out_shape=out_shape,
            scratch_shapes=scratch_shapes, compiler_params=_cparams(dimension_semantics=dims),
        )(*args)
        return list(outs), []
    n_in, n_out, n_scr = len(in_specs), len(out_specs), len(scratch_shapes)
    n_cin, n_cout = len(comm.inputs), len(comm.out_shape)

    def wrapped(*refs):
        ins, refs = refs[:n_in], refs[n_in:]
        cins, refs = refs[:n_cin], refs[n_cin:]
        outs, refs = refs[:n_out], refs[n_out:]
        couts, refs = refs[:n_cout], refs[n_cout:]
        scr, csems = refs[:n_scr], refs[n_scr:]
        first = last = None
        for axis, size in enumerate(grid):
            pid = pl.program_id(axis)
            f, l = pid == 0, pid == size - 1
            first = f if first is None else first & f
            last = l if last is None else last & l

        @pl.when(first)
        def _():
            comm.start(cins, couts, csems)

        body(*ins, *outs, *scr)

        @pl.when(last)
        def _():
            comm.wait(cins, couts, csems)

    res = pl.pallas_call(
        wrapped, name=name, grid=grid, in_specs=in_specs + [ANY] * n_cin, out_specs=out_specs + [ANY] * n_cout,
        out_shape=out_shape + list(comm.out_shape), scratch_shapes=scratch_shapes + list(comm.scratch),
        compiler_params=_cparams(dimension_semantics=("arbitrary",) * len(grid)),
    )(*args, *comm.inputs)
    return list(res[:n_out]), list(res[n_out:])


def _comm_only(comm, name):
    def body(*refs):
        n_cin, n_cout = len(comm.inputs), len(comm.out_shape)
        cins, couts, csems = refs[:n_cin], refs[n_cin:n_cin + n_cout], refs[n_cin + n_cout:]
        comm.start(cins, couts, csems)
        comm.wait(cins, couts, csems)

    return list(pl.pallas_call(
        body, name=name, in_specs=[ANY] * len(comm.inputs), out_specs=[ANY] * len(comm.out_shape),
        out_shape=list(comm.out_shape), scratch_shapes=list(comm.scratch),
    )(*comm.inputs))


def _mm(a, b, *, mode, out_dtype, tm, tn, tk=None, name):
    if mode == "nt":
        m, k = a.shape
        n = b.shape[0]
        assert m % tm == 0 and n % tn == 0

        def body(a_ref, b_ref, o_ref):
            o_ref[...] = _dot_nt(a_ref[...], b_ref[...]).astype(out_dtype)

        return pl.pallas_call(
            body, name=name, grid=(m // tm, n // tn),
            in_specs=[pl.BlockSpec((tm, k), lambda i, j: (i, 0)), pl.BlockSpec((tn, k), lambda i, j: (j, 0))],
            out_specs=pl.BlockSpec((tm, tn), lambda i, j: (i, j)),
            out_shape=jax.ShapeDtypeStruct((m, n), out_dtype),
            compiler_params=_cparams(dimension_semantics=("parallel", "parallel")),
        )(a, b)
    assert mode == "tn"
    kk, m = a.shape
    n = b.shape[1]
    assert m % tm == 0 and n % tn == 0 and kk % tk == 0
    nk = kk // tk

    def body(a_ref, b_ref, o_ref, acc_ref):
        kstep = pl.program_id(2)

        @pl.when(kstep == 0)
        def _():
            acc_ref[...] = jnp.zeros_like(acc_ref)

        acc_ref[...] += _dot_tn(a_ref[...], b_ref[...])

        @pl.when(kstep == nk - 1)
        def _():
            o_ref[...] = acc_ref[...].astype(out_dtype)

    return pl.pallas_call(
        body, name=name, grid=(m // tm, n // tn, nk),
        in_specs=[pl.BlockSpec((tk, tm), lambda i, j, s: (s, i)), pl.BlockSpec((tk, tn), lambda i, j, s: (s, j))],
        out_specs=pl.BlockSpec((tm, tn), lambda i, j, s: (i, j)),
        out_shape=jax.ShapeDtypeStruct((m, n), out_dtype),
        scratch_shapes=[pltpu.VMEM((tm, tn), F32)],
        compiler_params=_cparams(dimension_semantics=("parallel", "parallel", "arbitrary")),
    )(a, b)


PROJ_COLS = 256


def _proj_segments():
    wd = DIL_SLOTS * HEAD_DIM
    segs = [(1, [(0, 1), (PROJ_COLS, 1), (2 * PROJ_COLS, 2)])]
    for gi, dil in enumerate(DILATIONS):
        blocks = []
        for part, kind in enumerate((1, 1, 0)):
            col = A_WIDTH + part * B_QKV + gi * wd
            blocks += [(col, kind), (col + PROJ_COLS, kind)]
        segs.append((dil, blocks))
    return segs


PROJ_SEGMENTS = _proj_segments()


def _dh0_ln_in(dz, w_t, dr1, x, ln_in_g, *, t, comm=None):
    s, k = dz.shape

    def body(dz_ref, w_ref, dr1_ref, x_ref, g_ref, gx_ref, st_ref):
        i = pl.program_id(0)

        @pl.when(i == 0)
        def _():
            st_ref[...] = jnp.zeros_like(st_ref)

        dh0 = _dot(dz_ref[...], w_ref[...]) + ALPHA * dr1_ref[...]
        dx, dg, db = _ln_bwd_math(dh0, x_ref[...], g_ref[...])
        gx_ref[...] = dx
        st_ref[0:1, :] += dg
        st_ref[1:2, :] += db

    tile = pl.BlockSpec((t, D_MODEL), lambda i: (i, 0))
    return _pcall(
        body, name="dh0_ln_in", grid=(s // t,),
        in_specs=[pl.BlockSpec((t, k), lambda i: (i, 0)),
                  pl.BlockSpec((k, D_MODEL), lambda i: (0, 0), pipeline_mode=pl.Buffered(1)),
                  tile, tile, pl.BlockSpec((1, D_MODEL), lambda i: (0, 0))],
        out_specs=[tile, pl.BlockSpec((8, D_MODEL), lambda i: (0, 0))],
        out_shape=[jax.ShapeDtypeStruct((s, D_MODEL), F32), jax.ShapeDtypeStruct((8, D_MODEL), F32)],
        args=[dz, w_t, dr1, x, ln_in_g], dims=("arbitrary",), comm=comm)


def _ln_in_fwd(x, g, b, *, t, comm=None):
    s = x.shape[0]

    def body(x_ref, g_ref, b_ref, o_ref):
        o_ref[...] = _ln(x_ref[...], g_ref[...], b_ref[...]).astype(BF16)

    row = pl.BlockSpec((1, D_MODEL), lambda i: (0, 0))
    tile = pl.BlockSpec((t, D_MODEL), lambda i: (i, 0))
    outs, couts = _pcall(body, name="ln_in_fwd", grid=(s // t,), in_specs=[tile, row, row], out_specs=[tile],
                         out_shape=[jax.ShapeDtypeStruct((s, D_MODEL), BF16)], args=[x, g, b], dims=("parallel",),
                         comm=comm)
    return outs[0], couts


def _proj_all(h0b, w_t, cs, e_mat, *, t, comm=None):
    s = h0b.shape[0]
    cb = PROJ_COLS
    halves = cb // LANES

    def body(h_ref, w_ref, cs_ref, e_ref, *rest):
        z_refs, scr = rest[:-1], rest[-1]
        h = h_ref[...]
        ta, tb, tc = (jnp.tile(tab, (1, halves)) for tab in _rope_tabs(cs_ref[...], e_ref[...]))
        lane = lax.broadcasted_iota(jnp.int32, (t, cb), 1)
        slot = 0
        for z_ref, (dil, blocks) in zip(z_refs, PROJ_SEGMENTS):
            for jb, (col, kind) in enumerate(blocks):
                acc = _dot_nt(h, w_ref[col:col + cb, :])
                if kind:
                    z = acc * ta + (pltpu.roll(acc, cb - 8, 1) * tb + pltpu.roll(acc, 8, 1) * tc)
                    if kind == 2:
                        z = jnp.where(lane < LANES, z, acc)
                else:
                    z = acc
                if dil == 1:
                    z_ref[0, :, cb * jb:cb * (jb + 1)] = z.astype(BF16)
                    continue
                for half in range(halves):
                    scr[slot, half] = z[:, half * LANES:(half + 1) * LANES]
                for c in range(dil):
                    for half in range(halves):
                        rows = scr[slot, half, pl.ds(c, t // dil, stride=dil), :]
                        z_ref[c, :, cb * jb + half * LANES:cb * jb + (half + 1) * LANES] = rows.astype(BF16)
                slot = 1 - slot

    widths = [cb * len(blocks) for _, blocks in PROJ_SEGMENTS]
    dils = [dil for dil, _ in PROJ_SEGMENTS]
    outs, couts = _pcall(
        body, name="proj_all", grid=(s // t,),
        in_specs=[pl.BlockSpec((t, D_MODEL), lambda i: (i, 0)),
                  pl.BlockSpec((IN_WIDTH, D_MODEL), lambda i: (0, 0), pipeline_mode=pl.Buffered(1)),
                  pl.BlockSpec((t, ROT_DIM), lambda i: (i, 0)), pl.BlockSpec((ROT_DIM, 3 * LANES), lambda i: (0, 0))],
        out_specs=[pl.BlockSpec((dil, t // dil, wd), lambda i: (0, i, 0)) for dil, wd in zip(dils, widths)],
        out_shape=[jax.ShapeDtypeStruct((dil, s // dil, wd), BF16) for dil, wd in zip(dils, widths)],
        args=[h0b, w_t, cs, e_mat], scratch_shapes=[pltpu.VMEM((2, halves, t, LANES), F32)],
        dims=("parallel",), comm=comm)
    return outs, couts


PAIR = 2 * HEAD_DIM


def _place_head(x2, src_pos, dst_pos):
    hi = lax.broadcasted_iota(jnp.int32, x2.shape, 1) >= HEAD_DIM
    src = x2 if src_pos == dst_pos else pltpu.roll(x2, HEAD_DIM, 1)
    return jnp.where(hi == (dst_pos == 1), src, jnp.zeros_like(src))


def _band_mask_t(row0, tq, w, seq_len):
    tk = tq + 2 * w
    kk = lax.broadcasted_iota(jnp.int32, (tk, tq), 0)
    qq = lax.broadcasted_iota(jnp.int32, (tk, tq), 1)
    kpos = row0 - w + kk
    return (jnp.abs(qq + w - kk) <= w) & (kpos >= 0) & (kpos < seq_len)


def _halo_kv_specs(t, w, hkv, n, seq_len, kcol, vcol):
    kw = hkv * HEAD_DIM
    per, last = t // w, seq_len // w - 1
    cur = lambda s, i: jnp.minimum(i, n - 1)
    specs = []
    for c in (kcol, vcol):
        specs += [pl.BlockSpec((None, w, kw), lambda s, i, c=c: (s, jnp.maximum(cur(s, i) * per - 1, 0), c)),
                  pl.BlockSpec((None, t, kw), lambda s, i, c=c: (s, cur(s, i), c)),
                  pl.BlockSpec((None, w, kw), lambda s, i, c=c: (s, jnp.minimum((cur(s, i) + 1) * per, last), c))]
    return specs, cur


def _pair_kv(kfull, vfull, qp, rep, krows):
    ks, vs, a_of = [], [], []
    for pos in range(2):
        g = (2 * qp + pos) // rep
        a_of.append(g // 2)
        ks.append(_place_head(kfull[g // 2][krows], g % 2, pos))
        vs.append(_place_head(vfull[g // 2][krows], g % 2, pos))
    assert a_of[0] == a_of[1]
    return jnp.concatenate(ks, axis=0), jnp.concatenate(vs, axis=0), a_of[0]


def _swa_fwd_p(qkv, *, qcol, kcol, vcol, hq, hkv, w, tq, sub, sink, name, comm=None):
    nseq, seq_len, _ = qkv.shape
    t = tq * sub
    n = seq_len // t
    rep = hq // hkv
    tk = tq + 2 * w
    kv_specs, cur = _halo_kv_specs(t, w, hkv, n, seq_len, kcol, vcol)

    def body(*refs):
        if sink is not None:
            sink_ref, refs = refs[0], refs[1:]
        q_ref, kp_ref, kc_ref, kn_ref, vp_ref, vc_ref, vn_ref, o_ref, lse_ref = refs
        i = pl.program_id(1)
        kfull, vfull = [], []
        for a in range(hkv // 2):
            ls = slice(a * PAIR, (a + 1) * PAIR)
            kfull.append(jnp.concatenate([kp_ref[:, ls], kc_ref[:, ls], kn_ref[:, ls]], axis=0) * 0.125)
            vfull.append(jnp.concatenate([vp_ref[:, ls], vc_ref[:, ls], vn_ref[:, ls]], axis=0))
        row_hi = lax.broadcasted_iota(jnp.int32, (PAIR, tq), 0) >= HEAD_DIM
        for jj in range(sub):
            rows = slice(jj * tq, (jj + 1) * tq)
            mask_t = _band_mask_t(i * t + jj * tq, tq, w, seq_len)
            o_t, lse_rows = [], []
            for qp in range(hq // 2):
                kst, vst, _ = _pair_kv(kfull, vfull, qp, rep, slice(jj * tq, jj * tq + tk))
                s2 = _dot_nt(kst, q_ref[rows, qp * PAIR:(qp + 1) * PAIR])
                ps, dens = [], []
                for pos in range(2):
                    h = 2 * qp + pos
                    s_t = jnp.where(mask_t, s2[pos * tk:(pos + 1) * tk], NEG_INF)
                    m = jnp.max(s_t, axis=0, keepdims=True)
                    if sink is not None:
                        m = jnp.maximum(m, sink_ref[0, h])
                    p_t = jnp.exp(s_t - m)
                    den = jnp.sum(p_t, axis=0, keepdims=True)
                    if sink is not None:
                        den = den + jnp.exp(sink_ref[0, h] - m)
                    ps.append(p_t.astype(BF16))
                    dens.append(den)
                    lse_rows.append(m + jnp.log(den))
                both = _dot_tn(vst, jnp.concatenate(ps, axis=0))
                o_t.append(both / jnp.where(row_hi, dens[1], dens[0]))
            o_ref[rows, :] = jnp.concatenate(o_t, axis=0).T
            lse_ref[:, rows] = jnp.concatenate(lse_rows, axis=0)

    in_specs = [pl.BlockSpec((None, t, hq * HEAD_DIM), lambda s, i: (s, i, qcol))] + kv_specs
    args = [qkv] * 7
    if sink is not None:
        in_specs = [pl.BlockSpec(memory_space=pltpu.SMEM)] + in_specs
        args = [sink] + args
    (o, lse), couts = _pcall(
        body, name=name, grid=(nseq, n), in_specs=in_specs,
        out_specs=[pl.BlockSpec((None, t, hq * HEAD_DIM), lambda s, i: (s, i, 0)),
                   pl.BlockSpec((None, hq, t), lambda s, i: (s, 0, i))],
        out_shape=[jax.ShapeDtypeStruct((nseq, seq_len, hq * HEAD_DIM), F32),
                   jax.ShapeDtypeStruct((nseq, hq, seq_len), F32)],
        args=args, dims=("parallel", "parallel"), comm=comm)
    return o, lse, couts


def _swa_bwd_p(qkv, do, lse, delta, cs, e_mat, *, qcol, kcol, vcol, hq, hkv, w, tq, sub, sink, name, comm=None):
    nseq, seq_len, _ = qkv.shape
    t = tq * sub
    n = seq_len // t
    rep = hq // hkv
    qw, kw = hq * HEAD_DIM, hkv * HEAD_DIM
    tk = tq + 2 * w
    kv_specs, cur = _halo_kv_specs(t, w, hkv, n, seq_len, kcol, vcol)

    def body(*refs):
        if sink is not None:
            sink_ref, refs = refs[0], refs[1:]
        (q_ref, kp_ref, kc_ref, kn_ref, vp_ref, vc_ref, vn_ref, do_ref, lse_ref, dl_ref,
         cs_c, cs_p, e_ref) = refs[:13]
        outs = refs[13:]
        if sink is not None:
            dq_ref, dk_ref, dv_ref, dsink_ref, dk_acc, dv_acc, dk_win, dv_win = outs
        else:
            dq_ref, dk_ref, dv_ref, dk_acc, dv_acc, dk_win, dv_win = outs
        s_id = pl.program_id(0)
        i = pl.program_id(1)
        slot_p, slot_c, slot_n = (i + 2) % 3, i % 3, (i + 1) % 3

        if sink is not None:
            @pl.when((s_id == 0) & (i == 0))
            def _():
                dsink_ref[...] = jnp.zeros_like(dsink_ref)

        @pl.when(i < n)
        def _():
            dk_win[...] = jnp.zeros_like(dk_win)
            dv_win[...] = jnp.zeros_like(dv_win)
            kfull, vfull = [], []
            for a in range(hkv // 2):
                ls = slice(a * PAIR, (a + 1) * PAIR)
                kfull.append(jnp.concatenate([kp_ref[:, ls], kc_ref[:, ls], kn_ref[:, ls]], axis=0) * 0.125)
                vfull.append(jnp.concatenate([vp_ref[:, ls], vc_ref[:, ls], vn_ref[:, ls]], axis=0))
            for jj in range(sub):
                rows = slice(jj * tq, (jj + 1) * tq)
                krows = slice(jj * tq, jj * tq + tk)
                mask_t = _band_mask_t(i * t + jj * tq, tq, w, seq_len)
                dq_t = []
                dk2 = [None] * (hkv // 2)
                dv2 = [None] * (hkv // 2)
                for qp in range(hq // 2):
                    kst, vst, a = _pair_kv(kfull, vfull, qp, rep, krows)
                    q2 = q_ref[rows, qp * PAIR:(qp + 1) * PAIR]
                    do2 = do_ref[rows, qp * PAIR:(qp + 1) * PAIR]
                    s2 = _dot_nt(kst, q2)
                    dp2 = _dot_nt(vst, do2)
                    ds, ps, q_at, do_at = [], [], [], []
                    for pos in range(2):
                        h = 2 * qp + pos
                        e = (h // rep) % 2
                        half = slice(pos * tk, (pos + 1) * tk)
                        lse_h = lse_ref[h:h + 1, rows]
                        dl_h = dl_ref[h:h + 1, rows]
                        p_t = jnp.exp(jnp.where(mask_t, s2[half], NEG_INF) - lse_h)
                        ds.append((p_t * (dp2[half] - dl_h)).astype(BF16))
                        ps.append(p_t.astype(BF16))
                        q_at.append(_place_head(q2, pos, e) * 0.125)
                        do_at.append(_place_head(do2, pos, e))
                        if sink is not None:
                            ds_sink = -jnp.sum(jnp.exp(sink_ref[0, h] - lse_h) * dl_h)
                            dsink_ref[h:h + 1, :] += jnp.full((1, LANES), ds_sink, F32)
                    dq_t.append(_rope_rows(_dot_tn(kst, jnp.concatenate(ds, axis=0)),
                                           cs_c[0:ROT_DIM // 2, rows], cs_c[ROT_DIM // 2:ROT_DIM, rows], -1.0))
                    dk_part = _dot(jnp.concatenate(ds, axis=1), jnp.concatenate(q_at, axis=0))
                    dv_part = _dot(jnp.concatenate(ps, axis=1), jnp.concatenate(do_at, axis=0))
                    dk2[a] = dk_part if dk2[a] is None else dk2[a] + dk_part
                    dv2[a] = dv_part if dv2[a] is None else dv2[a] + dv_part
                for a in range(hkv // 2):
                    ls = slice(a * PAIR, (a + 1) * PAIR)
                    dk_win[krows, ls] += dk2[a]
                    dv_win[krows, ls] += dv2[a]
                dq_ref[rows, :] = jnp.concatenate(dq_t, axis=0).T.astype(BF16)

            @pl.when(i > 0)
            def _():
                dk_acc[slot_p, t - w:, :] += dk_win[:w, :]
                dv_acc[slot_p, t - w:, :] += dv_win[:w, :]

            @pl.when(i == 0)
            def _():
                dk_acc[slot_c] = dk_win[w:w + t, :]
                dv_acc[slot_c] = dv_win[w:w + t, :]

            @pl.when(i > 0)
            def _():
                dk_acc[slot_c] += dk_win[w:w + t, :]
                dv_acc[slot_c] += dv_win[w:w + t, :]

            dk_acc[slot_n] = jnp.zeros((t, kw), F32)
            dv_acc[slot_n] = jnp.zeros((t, kw), F32)
            dk_acc[slot_n, :w, :] = dk_win[w + t:, :]
            dv_acc[slot_n, :w, :] = dv_win[w + t:, :]

        @pl.when(i >= 1)
        def _():
            dk_ref[...] = _rope(dk_acc[slot_p], *_rope_tabs(cs_p[...], e_ref[...]), -1.0).astype(BF16)
            dv_ref[...] = dv_acc[slot_p].astype(BF16)

    row_c = lambda width: pl.BlockSpec((None, t, width), lambda s, i: (s, cur(s, i), 0))
    row_p = lambda width: pl.BlockSpec((None, t, width), lambda s, i: (s, jnp.maximum(i - 1, 0), 0))
    stat = pl.BlockSpec((None, hq, t), lambda s, i: (s, 0, cur(s, i)))
    cs_rows = pl.BlockSpec((None, ROT_DIM, t), lambda s, i: (s, 0, cur(s, i)))
    in_specs = ([pl.BlockSpec((None, t, qw), lambda s, i: (s, cur(s, i), qcol))] + kv_specs
                + [row_c(qw), stat, stat, cs_rows, row_p(ROT_DIM),
                   pl.BlockSpec((ROT_DIM, 3 * LANES), lambda s, i: (0, 0))])
    args = [qkv] * 7 + [do, lse, delta, cs.transpose(0, 2, 1), cs, e_mat]
    out_specs = [row_c(qw), row_p(kw), row_p(kw)]
    out_shape = [jax.ShapeDtypeStruct((nseq, seq_len, qw), BF16),
                 jax.ShapeDtypeStruct((nseq, seq_len, kw), BF16),
                 jax.ShapeDtypeStruct((nseq, seq_len, kw), BF16)]
    if sink is not None:
        in_specs = [pl.BlockSpec(memory_space=pltpu.SMEM)] + in_specs
        args = [sink] + args
        out_specs.append(pl.BlockSpec((8, LANES), lambda s, i: (0, 0)))
        out_shape.append(jax.ShapeDtypeStruct((8, LANES), F32))
    return _pcall(
        body, name=name, grid=(nseq, n + 1), in_specs=in_specs, out_specs=out_specs, out_shape=out_shape,
        scratch_shapes=[pltpu.VMEM((3, t, kw), F32), pltpu.VMEM((3, t, kw), F32),
                        pltpu.VMEM((t + 2 * w, kw), F32), pltpu.VMEM((t + 2 * w, kw), F32)], args=args,
        dims=("arbitrary", "arbitrary"), comm=comm)


def _rms_parts(o, g):
    ms = jnp.mean(o * o, axis=-1, keepdims=True) + LN_EPS
    rinv = lax.rsqrt(ms)
    return o * rinv * g, rinv


def _from_subsequences(ref, scr, dil, t):
    slabs = ref.shape[-1] // LANES
    if dil == 1:
        return ref[0].astype(F32)
    for c in range(dil):
        for sl in range(slabs):
            scr[sl, pl.ds(c, t // dil, stride=dil), :] = ref[c, :, sl * LANES:(sl + 1) * LANES].astype(F32)
    return jnp.concatenate([scr[sl] for sl in range(slabs)], axis=1)


def _to_subsequences(val, ref, scr, dil, t):
    slabs = val.shape[-1] // LANES
    if dil == 1:
        ref[0] = val.astype(ref.dtype)
        return
    for sl in range(slabs):
        scr[sl] = val[:, sl * LANES:(sl + 1) * LANES]
    for c in range(dil):
        for sl in range(slabs):
            ref[c, :, sl * LANES:(sl + 1) * LANES] = scr[sl, pl.ds(c, t // dil, stride=dil), :].astype(ref.dtype)


def _combine_fwd(out_a, o_g, lse_g, g_win, g_dil, w_mix_b, x, ln_in_g, ln_in_b, ln1_g, ln1_b, *, t):
    s = out_a.shape[1]
    wd = DIL_SLOTS * HEAD_DIM

    def body(oa_ref, o0, o1, o2, l0, l1, l2, gw_ref, gd_ref, w_ref, x_ref, g0, b0, g1, b1,
             mixed_ref, ob_ref, lt_ref, r1_ref, h1_ref, scr):
        ls = [l0[...], l1[...], l2[...]]
        mx = jnp.maximum(jnp.maximum(ls[0], ls[1]), ls[2])
        ws = [jnp.exp(l - mx) for l in ls]
        tot = ws[0] + ws[1] + ws[2]
        lt_ref[...] = mx + jnp.log(tot)
        ws = [x / tot for x in ws]
        og = [_from_subsequences(o_ref, scr.at[gi], dil, t)
              for gi, (o_ref, dil) in enumerate(zip((o0, o1, o2), DILATIONS))]
        parts = []
        for h in range(DIL_SLOTS):
            hs = slice(h * HEAD_DIM, (h + 1) * HEAD_DIM)
            parts.append(ws[0][:, h:h + 1] * og[0][:, hs] + ws[1][:, h:h + 1] * og[1][:, hs]
                         + ws[2][:, h:h + 1] * og[2][:, hs])
        ob = jnp.concatenate(parts, axis=1)
        ob_ref[...] = ob
        na, _ = _rms_parts(oa_ref[...], gw_ref[...])
        nb, _ = _rms_parts(ob, gd_ref[...])
        mixed = jnp.concatenate([na.astype(BF16), nb.astype(BF16)], axis=1)
        mixed_ref[...] = mixed
        h0 = _ln(x_ref[...], g0[...], b0[...])
        r1 = ALPHA * h0 + _dot(mixed, w_ref[...])
        r1_ref[...] = r1
        h1_ref[...] = _ln(r1, g1[...], b1[...]).astype(BF16)

    half = pl.BlockSpec((t, wd), lambda i: (i, 0))
    full = pl.BlockSpec((t, D_MODEL), lambda i: (i, 0))
    lanes = pl.BlockSpec((t, LANES), lambda i: (i, 0))
    grow = pl.BlockSpec((1, wd), lambda i: (0, 0))
    row = pl.BlockSpec((1, D_MODEL), lambda i: (0, 0))
    subseq = [pl.BlockSpec((dil, t // dil, wd), lambda i: (0, i, 0)) for dil in DILATIONS]
    return pl.pallas_call(
        body, name="combine_fwd", grid=(s // t,),
        in_specs=[pl.BlockSpec((None, t, wd), lambda i: (0, i, 0))] + subseq
        + [lanes, lanes, lanes, grow, grow, pl.BlockSpec((D_MODEL, D_MODEL), lambda i: (0, 0)), full,
           row, row, row, row],
        out_specs=[full, half, lanes, full, full],
        out_shape=[jax.ShapeDtypeStruct((s, D_MODEL), BF16), jax.ShapeDtypeStruct((s, wd), F32),
                   jax.ShapeDtypeStruct((s, LANES), F32), jax.ShapeDtypeStruct((s, D_MODEL), F32),
                   jax.ShapeDtypeStruct((s, D_MODEL), BF16)],
        scratch_shapes=[pltpu.VMEM((len(DILATIONS), wd // LANES, t, LANES), F32)],
        compiler_params=_cparams(dimension_semantics=("parallel",)),
    )(out_a, *o_g, *lse_g, g_win, g_dil, w_mix_b, x, ln_in_g, ln_in_b, ln1_g, ln1_b)


def _combine_bwd(dr1b, w_mix_b, out_a, out_b, g_win, g_dil, *, t):
    s = out_b.shape[0]
    wd = DIL_SLOTS * HEAD_DIM

    def body(dr_ref, w_ref, oa_ref, ob_ref, gw_ref, gd_ref, doa_ref, dob0, dob1, dob2, dla_ref, dlb_ref, st_ref,
             scr):
        i = pl.program_id(0)
        dm = _dot_nt(dr_ref[...], w_ref[...])

        @pl.when(i == 0)
        def _():
            st_ref[...] = jnp.zeros_like(st_ref)

        lane = lax.broadcasted_iota(jnp.int32, (t, LANES), 1)
        for idx, (o_ref, g_ref, dl_ref) in enumerate(((oa_ref, gw_ref, dla_ref), (ob_ref, gd_ref, dlb_ref))):
            o = o_ref[...]
            dn = dm[:, idx * wd:(idx + 1) * wd]
            _, rinv = _rms_parts(o, g_ref[...])
            wv = dn * g_ref[...]
            do = rinv * wv - o * (rinv * rinv * rinv) * jnp.mean(wv * o, axis=-1, keepdims=True)
            st_ref[idx:idx + 1, :] += jnp.sum(dn * o * rinv, axis=0, keepdims=True)
            if idx == 0:
                doa_ref[...] = do.astype(BF16)
            else:
                for do_ref, dil in zip((dob0, dob1, dob2), DILATIONS):
                    _to_subsequences(do, do_ref, scr, dil, t)
            prod = do * o
            acc = jnp.zeros((t, LANES), F32)
            for h in range(DIL_SLOTS):
                hs = slice(h * HEAD_DIM, (h + 1) * HEAD_DIM)
                acc = jnp.where(lane == h, jnp.sum(prod[:, hs], axis=1, keepdims=True), acc)
            dl_ref[...] = acc

    half = pl.BlockSpec((t, wd), lambda i: (i, 0))
    lanes = pl.BlockSpec((t, LANES), lambda i: (i, 0))
    grow = pl.BlockSpec((1, wd), lambda i: (0, 0))
    a_spec = pl.BlockSpec((None, t, wd), lambda i: (0, i, 0))
    subseq = [pl.BlockSpec((dil, t // dil, wd), lambda i: (0, i, 0)) for dil in DILATIONS]
    doa, dob0, dob1, dob2, dla, dlb, st = pl.pallas_call(
        body, name="combine_bwd", grid=(s // t,),
        in_specs=[pl.BlockSpec((t, D_MODEL), lambda i: (i, 0)), pl.BlockSpec((D_MODEL, D_MODEL), lambda i: (0, 0)),
                  a_spec, half, grow, grow],
        out_specs=[a_spec] + subseq + [lanes, lanes, pl.BlockSpec((8, wd), lambda i: (0, 0))],
        out_shape=[jax.ShapeDtypeStruct((1, s, wd), BF16)]
        + [jax.ShapeDtypeStruct((dil, s // dil, wd), BF16) for dil in DILATIONS]
        + [jax.ShapeDtypeStruct((s, LANES), F32), jax.ShapeDtypeStruct((s, LANES), F32),
           jax.ShapeDtypeStruct((8, wd), F32)],
        scratch_shapes=[pltpu.VMEM((wd // LANES, t, LANES), F32)],
        compiler_params=_cparams(dimension_semantics=("arbitrary",)),
    )(dr1b, w_mix_b, out_a, out_b, g_win, g_dil)
    return doa, [dob0, dob1, dob2], dla, dlb, st


def _assemble_dz(dqa, dka, dva, dqs, dks, dvs, *, t):
    s = dqa.shape[1]
    wd = DIL_SLOTS * HEAD_DIM

    def body(*refs):
        a_refs, g_refs, o_ref, scr = refs[:3], refs[3:12], refs[12], refs[13]
        col = 0
        for r in a_refs:
            o_ref[:, col:col + r.shape[-1]] = r[...]
            col += r.shape[-1]
        for part in range(3):
            for gi, dil in enumerate(DILATIONS):
                val = _from_subsequences(g_refs[3 * part + gi], scr, dil, t)
                o_ref[:, col:col + wd] = val.astype(BF16)
                col += wd

    a_specs = [pl.BlockSpec((None, t, a.shape[-1]), lambda i: (0, i, 0)) for a in (dqa, dka, dva)]
    g_specs = [pl.BlockSpec((dil, t // dil, wd), lambda i: (0, i, 0)) for _ in range(3) for dil in DILATIONS]
    return pl.pallas_call(
        body, name="assemble_dz", grid=(s // t,), in_specs=a_specs + g_specs,
        out_specs=pl.BlockSpec((t, IN_WIDTH), lambda i: (i, 0)),
        out_shape=jax.ShapeDtypeStruct((s, IN_WIDTH), BF16),
        scratch_shapes=[pltpu.VMEM((wd // LANES, t, LANES), F32)],
        compiler_params=_cparams(dimension_semantics=("parallel",)),
    )(dqa, dka, dva, *dqs, *dks, *dvs)


def _mem_fwd(mem, g, b, wk_b, wv_b):
    ml = mem.shape[0]

    def body(mem_ref, g_ref, b_ref, wk_ref, wv_ref, mn_ref, kx_ref, vx_ref):
        mn = _ln(mem_ref[...], g_ref[...], b_ref[...]).astype(BF16)
        mn_ref[...] = mn
        kx_ref[...] = _dot(mn, wk_ref[...]).astype(BF16)
        vx_ref[...] = _dot(mn, wv_ref[...]).astype(BF16)

    sh = jax.ShapeDtypeStruct((ml, D_MODEL), BF16)
    return pl.pallas_call(body, name="mem_fwd", out_shape=[sh, sh, sh], compiler_params=_cparams())(
        mem, g, b, wk_b, wv_b)


def _mem_bwd(dkx, dvx, mem, g, b, wk_b, wv_b):
    def body(dk_ref, dv_ref, mem_ref, g_ref, b_ref, wk_ref, wv_ref, dwk_ref, dwv_ref, st_ref):
        mem_v = mem_ref[...]
        mn = _ln(mem_v, g_ref[...], b_ref[...]).astype(BF16)
        dkb = dk_ref[...].astype(BF16)
        dvb = dv_ref[...].astype(BF16)
        dwk_ref[...] = _dot_tn(mn, dkb)
        dwv_ref[...] = _dot_tn(mn, dvb)
        dmn = _dot_nt(dkb, wk_ref[...]) + _dot_nt(dvb, wv_ref[...])
        _, dg, db = _ln_bwd_math(dmn, mem_v, g_ref[...])
        st_ref[...] = jnp.zeros_like(st_ref)
        st_ref[0:1, :] = dg
        st_ref[1:2, :] = db

    sw = jax.ShapeDtypeStruct((D_MODEL, D_MODEL), F32)
    return pl.pallas_call(body, name="mem_bwd", out_shape=[sw, sw, jax.ShapeDtypeStruct((8, D_MODEL), F32)],
                          compiler_params=_cparams())(dkx, dvx, mem, g, b, wk_b, wv_b)


def _xattn_fwd(h1b, r1, kx, vx, wq_b, wo_b, ln1_g, ln1_b, ln2_g, ln2_b, *, t):
    s = h1b.shape[0]
    scale = X_HEAD_DIM ** -0.5

    def body(h_ref, r1_ref, kx_ref, vx_ref, wq_ref, wo_ref, g1, b1, g2, b2, r2_ref, h2_ref, qx_ref, ox_ref, lse_ref):
        qxb = _dot(h_ref[...], wq_ref[...]).astype(BF16)
        qx_ref[...] = qxb
        lane = lax.broadcasted_iota(jnp.int32, (t, LANES), 1)
        lse_acc = jnp.zeros((t, LANES), F32)
        parts = []
        for h in range(X_HEADS):
            hs = slice(h * X_HEAD_DIM, (h + 1) * X_HEAD_DIM)
            sc = _dot_nt(qxb[:, hs] * scale, kx_ref[:, hs])
            m = jnp.max(sc, axis=1, keepdims=True)
            p = jnp.exp(sc - m)
            den = jnp.sum(p, axis=1, keepdims=True)
            parts.append(_dot(p.astype(BF16), vx_ref[:, hs]) / den)
            lse_acc = jnp.where(lane == h, m + jnp.log(den), lse_acc)
        lse_ref[...] = lse_acc
        oxb = jnp.concatenate(parts, axis=1).astype(BF16)
        ox_ref[...] = oxb
        h1 = _ln(r1_ref[...], g1[...], b1[...])
        r2 = ALPHA * h1 + _dot(oxb, wo_ref[...])
        r2_ref[...] = r2
        h2_ref[...] = _ln(r2, g2[...], b2[...]).astype(BF16)

    tile = pl.BlockSpec((t, D_MODEL), lambda i: (i, 0))
    row = pl.BlockSpec((1, D_MODEL), lambda i: (0, 0))
    full = lambda r: pl.BlockSpec((r, D_MODEL), lambda i: (0, 0))
    ml = kx.shape[0]
    bsh = jax.ShapeDtypeStruct((s, D_MODEL), BF16)
    return pl.pallas_call(
        body, name="xattn_fwd", grid=(s // t,),
        in_specs=[tile, tile, full(ml), full(ml), full(D_MODEL), full(D_MODEL), row, row, row, row],
        out_specs=[tile, tile, tile, tile, pl.BlockSpec((t, LANES), lambda i: (i, 0))],
        out_shape=[jax.ShapeDtypeStruct((s, D_MODEL), F32), bsh, bsh, bsh, jax.ShapeDtypeStruct((s, LANES), F32)],
        compiler_params=_cparams(dimension_semantics=("parallel",)),
    )(h1b, r1, kx, vx, wq_b, wo_b, ln1_g, ln1_b, ln2_g, ln2_b)


def _xattn_bwd(dr2, qxb, oxb, lse, kx, vx, wq_b, wo_b, r1, ln1_g, *, t, comm=None):
    s = dr2.shape[0]
    ml = kx.shape[0]
    scale = X_HEAD_DIM ** -0.5

    def body(dr2_ref, qx_ref, ox_ref, lse_ref, kx_ref, vx_ref, wq_ref, wo_ref, r1_ref, g1_ref,
             dr1_ref, dr1b_ref, dqx_ref, dkx_ref, dvx_ref, st_ref):
        i = pl.program_id(0)

        @pl.when(i == 0)
        def _():
            dkx_ref[...] = jnp.zeros_like(dkx_ref)
            dvx_ref[...] = jnp.zeros_like(dvx_ref)
            st_ref[...] = jnp.zeros_like(st_ref)

        dr2v = dr2_ref[...]
        dox = _dot_nt(dr2v.astype(BF16), wo_ref[...])
        parts = []
        for h in range(X_HEADS):
            hs = slice(h * X_HEAD_DIM, (h + 1) * X_HEAD_DIM)
            doh = dox[:, hs]
            dohb = doh.astype(BF16)
            dl = jnp.sum(doh * ox_ref[:, hs].astype(F32), axis=1, keepdims=True)
            qh = qx_ref[:, hs] * scale
            p = jnp.exp(_dot_nt(qh, kx_ref[:, hs]) - lse_ref[:, h:h + 1])
            dp = _dot_nt(dohb, vx_ref[:, hs])
            dsb = (p * (dp - dl)).astype(BF16)
            parts.append(_dot(dsb, kx_ref[:, hs]) * scale)
            dkx_ref[:, hs] += _dot_tn(dsb, qh)
            dvx_ref[:, hs] += _dot_tn(p.astype(BF16), dohb)
        dqxb = jnp.concatenate(parts, axis=1).astype(BF16)
        dqx_ref[...] = dqxb
        dh1 = _dot_nt(dqxb, wq_ref[...]) + ALPHA * dr2v
        dr1, dg, db = _ln_bwd_math(dh1, r1_ref[...], g1_ref[...])
        dr1_ref[...] = dr1
        dr1b_ref[...] = dr1.astype(BF16)
        st_ref[0:1, :] += dg
        st_ref[1:2, :] += db

    tile = pl.BlockSpec((t, D_MODEL), lambda i: (i, 0))
    full = lambda r: pl.BlockSpec((r, D_MODEL), lambda i: (0, 0))
    bsh = jax.ShapeDtypeStruct((s, D_MODEL), BF16)
    return _pcall(
        body, name="xattn_bwd", grid=(s // t,),
        in_specs=[tile, tile, tile, pl.BlockSpec((t, LANES), lambda i: (i, 0)), full(ml), full(ml),
                  full(D_MODEL), full(D_MODEL), tile, full(1)],
        out_specs=[tile, tile, tile, full(ml), full(ml), full(8)],
        out_shape=[jax.ShapeDtypeStruct((s, D_MODEL), F32), bsh, bsh,
                   jax.ShapeDtypeStruct((ml, D_MODEL), F32), jax.ShapeDtypeStruct((ml, D_MODEL), F32),
                   jax.ShapeDtypeStruct((8, D_MODEL), F32)],
        args=[dr2, qxb, oxb, lse, kx, vx, wq_b, wo_b, r1, ln1_g], dims=("arbitrary",), comm=comm)


def _halo_specs(t, s, width):
    tb8 = t // 8
    return [pl.BlockSpec((t, width), lambda i: (i, 0)),
            pl.BlockSpec((8, width), lambda i: (jnp.maximum(i * tb8 - 1, 0), 0)),
            pl.BlockSpec((8, width), lambda i: (jnp.minimum((i + 1) * tb8, s // 8 - 1), 0))]


def _halo_rows(i, n, prev_ref, next_ref):
    prev_row = jnp.where(i > 0, prev_ref[7:8, :], 0.0)
    next_row = jnp.where(i < n - 1, next_ref[0:1, :], 0.0)
    return prev_row, next_row


def _gelu_parts(gc):
    cdf = 0.5 * (1.0 + lax.erf(gc * (2.0 ** -0.5)))
    pdf = jnp.exp(-0.5 * gc * gc) * (1.0 / math.sqrt(2.0 * math.pi))
    return gc * cdf, cdf + gc * pdf


def _ffn_out(g, u, conv_w, conv_b, w_down_b, r2, target, ln2_g, ln2_b, ln3_g, ln3_b, *, t):
    s = r2.shape[0]
    n = s // t

    def body(g_ref, gp_ref, gn_ref, u_ref, cw_ref, cb_ref, w_ref, r2_ref, tg_ref, g2, b2, g3, b3,
             t_ref, dr_ref, drb_ref, st_ref):
        i = pl.program_id(0)

        @pl.when(i == 0)
        def _():
            st_ref[...] = jnp.zeros_like(st_ref)

        gv = g_ref[...]
        prev_row, next_row = _halo_rows(i, n, gp_ref, gn_ref)
        gm1, gp1 = _shift_rows(gv, prev_row, next_row)
        gc = gm1 * cw_ref[0:1, :] + gv * cw_ref[1:2, :] + gp1 * cw_ref[2:3, :] + cb_ref[...]
        act, _ = _gelu_parts(gc)
        tb = (act * u_ref[...]).astype(BF16)
        t_ref[...] = tb
        h2 = _ln(r2_ref[...], g2[...], b2[...])
        r3 = ALPHA * h2 + _dot(tb, w_ref[...])
        y = _ln(r3, g3[...], b3[...])
        err = y - tg_ref[...]
        loss = 0.5 * jnp.sum(jnp.mean(err * err, axis=-1, keepdims=True))
        dr, dg, db = _ln_bwd_math(err * (1.0 / D_MODEL), r3, g3[...])
        dr_ref[...] = dr
        drb_ref[...] = dr.astype(BF16)
        st_ref[0:1, :] += dg
        st_ref[1:2, :] += db
        st_ref[2:3, :] += jnp.full((1, D_MODEL), loss, F32)

    wide = pl.BlockSpec((t, D_FF), lambda i: (i, 0))
    tile = pl.BlockSpec((t, D_MODEL), lambda i: (i, 0))
    row = pl.BlockSpec((1, D_MODEL), lambda i: (0, 0))
    return pl.pallas_call(
        body, name="ffn_out", grid=(n,),
        in_specs=_halo_specs(t, s, D_FF) + [wide, pl.BlockSpec((3, D_FF), lambda i: (0, 0)),
                                            pl.BlockSpec((1, D_FF), lambda i: (0, 0)),
                                            pl.BlockSpec((D_FF, D_MODEL), lambda i: (0, 0)),
                                            tile, tile, row, row, row, row],
        out_specs=[wide, tile, tile, pl.BlockSpec((8, D_MODEL), lambda i: (0, 0))],
        out_shape=[jax.ShapeDtypeStruct((s, D_FF), BF16), jax.ShapeDtypeStruct((s, D_MODEL), F32),
                   jax.ShapeDtypeStruct((s, D_MODEL), BF16), jax.ShapeDtypeStruct((8, D_MODEL), F32)],
        compiler_params=_cparams(dimension_semantics=("arbitrary",)),
    )(g, g, g, u, conv_w, conv_b, w_down_b, r2, target, ln2_g, ln2_b, ln3_g, ln3_b)


def _dh2_ln2(dgc, conv_w, du, w_gate_b, w_up_b, dr3, r2, ln2_g, *, t, comm=None):
    s = dgc.shape[0]
    n = s // t

    def body(d_ref, dp_ref, dn_ref, cw_ref, du_ref, wg_ref, wu_ref, dr3_ref, r2_ref, g2, dg_ref, dr_ref, drb_ref,
             st_ref):
        i = pl.program_id(0)

        @pl.when(i == 0)
        def _():
            st_ref[...] = jnp.zeros_like(st_ref)

        dv = d_ref[...]
        prev_row, next_row = _halo_rows(i, n, dp_ref, dn_ref)
        dm1, dp1 = _shift_rows(dv, prev_row, next_row)
        dgb = (dp1 * cw_ref[0:1, :] + dv * cw_ref[1:2, :] + dm1 * cw_ref[2:3, :]).astype(BF16)
        dg_ref[...] = dgb
        dh2 = _dot(dgb, wg_ref[...]) + _dot(du_ref[...], wu_ref[...]) + ALPHA * dr3_ref[...]
        dr, dg, db = _ln_bwd_math(dh2, r2_ref[...], g2[...])
        dr_ref[...] = dr
        drb_ref[...] = dr.astype(BF16)
        st_ref[0:1, :] += dg
        st_ref[1:2, :] += db

    wide = pl.BlockSpec((t, D_FF), lambda i: (i, 0))
    tile = pl.BlockSpec((t, D_MODEL), lambda i: (i, 0))
    wfull = pl.BlockSpec((D_FF, D_MODEL), lambda i: (0, 0), pipeline_mode=pl.Buffered(1))
    return _pcall(
        body, name="dh2_ln2", grid=(n,),
        in_specs=_halo_specs(t, s, D_FF) + [pl.BlockSpec((3, D_FF), lambda i: (0, 0)), wide, wfull, wfull,
                                            tile, tile, pl.BlockSpec((1, D_MODEL), lambda i: (0, 0))],
        out_specs=[wide, tile, tile, pl.BlockSpec((8, D_MODEL), lambda i: (0, 0))],
        out_shape=[jax.ShapeDtypeStruct((s, D_FF), BF16), jax.ShapeDtypeStruct((s, D_MODEL), F32),
                   jax.ShapeDtypeStruct((s, D_MODEL), BF16), jax.ShapeDtypeStruct((8, D_MODEL), F32)],
        args=[dgc, dgc, dgc, conv_w, du, w_gate_b, w_up_b, dr3, r2, ln2_g], dims=("arbitrary",), comm=comm)


def _conv_bwd_a(dr3b, w_down_b, g, u, conv_w, conv_b, *, t):
    s = g.shape[0]
    n = s // t

    def body(d_ref, w_ref, g_ref, gp_ref, gn_ref, u_ref, cw_ref, cb_ref, du_ref, dgc_ref, st_ref):
        i = pl.program_id(0)

        @pl.when(i == 0)
        def _():
            st_ref[...] = jnp.zeros_like(st_ref)

        dt = _dot_nt(d_ref[...], w_ref[...])
        gv = g_ref[...]
        prev_row, next_row = _halo_rows(i, n, gp_ref, gn_ref)
        gm1, gp1 = _shift_rows(gv, prev_row, next_row)
        gc = gm1 * cw_ref[0:1, :] + gv * cw_ref[1:2, :] + gp1 * cw_ref[2:3, :] + cb_ref[...]
        act, dact = _gelu_parts(gc)
        du_ref[...] = (dt * act).astype(BF16)
        dgc = dt * u_ref[...] * dact
        dgc_ref[...] = dgc
        st_ref[0:1, :] += jnp.sum(gm1 * dgc, axis=0, keepdims=True)
        st_ref[1:2, :] += jnp.sum(gv * dgc, axis=0, keepdims=True)
        st_ref[2:3, :] += jnp.sum(gp1 * dgc, axis=0, keepdims=True)
        st_ref[3:4, :] += jnp.sum(dgc, axis=0, keepdims=True)

    tile = pl.BlockSpec((t, D_FF), lambda i: (i, 0))
    return pl.pallas_call(
        body, name="conv_bwd_a", grid=(n,),
        in_specs=[pl.BlockSpec((t, D_MODEL), lambda i: (i, 0)), pl.BlockSpec((D_FF, D_MODEL), lambda i: (0, 0))]
        + _halo_specs(t, s, D_FF) + [tile, pl.BlockSpec((3, D_FF), lambda i: (0, 0)),
                                     pl.BlockSpec((1, D_FF), lambda i: (0, 0))],
        out_specs=[tile, tile, pl.BlockSpec((8, D_FF), lambda i: (0, 0))],
        out_shape=[jax.ShapeDtypeStruct((s, D_FF), BF16), jax.ShapeDtypeStruct((s, D_FF), F32),
                   jax.ShapeDtypeStruct((8, D_FF), F32)],
        compiler_params=_cparams(dimension_semantics=("arbitrary",)),
    )(dr3b, w_down_b, g, g, g, u, conv_w, conv_b)


def _to_residue(a, dil):
    s, w = a.shape
    return a.reshape(s // dil, dil, w).transpose(1, 0, 2)


def _stats_to_lanes(rows):
    dil, hq, l = rows.shape
    return jnp.pad(rows.transpose(2, 0, 1).reshape(dil * l, hq), ((0, 0), (0, LANES - hq)))


def _stats_to_rows(lanes, dil):
    s = lanes.shape[0]
    return lanes[:, :DIL_SLOTS].reshape(s // dil, dil, DIL_SLOTS).transpose(1, 2, 0)


def _rope_angles(positions):
    inv_freq = ROPE_THETA ** (-jnp.arange(0, ROT_DIM, 2, dtype=F32) / ROT_DIM)
    ang = positions.astype(F32)[:, None] * inv_freq
    return jnp.concatenate([jnp.cos(ang), jnp.sin(ang)], axis=1)


class _NoPlan:
    def gather(self, stage):
        return None

    def gathered(self, stage, couts, wb):
        pass

    def exchange(self, stage, grads):
        return None

    def exchanged(self, stage, couts):
        pass


def _local_step(x, mem, positions, target, wb, sp, plan=None, *, t_row=256, t_mm=512, tq_a=128, tq_b=128,
                sub_a=4, sub_b=4):
    s = x.shape[0]
    plan = plan or _NoPlan()
    cs = _rope_angles(positions)
    e_mat = _rope_select_matrix()

    h0b, couts = _ln_in_fwd(x, sp["ln_in_g"], sp["ln_in_b"], t=t_mm, comm=plan.gather("ln_in"))
    plan.gathered("ln_in", couts, wb)
    sp = dict(sp, conv_w=wb.get("conv_w", sp.get("conv_w")))
    (za, *zb), couts = _proj_all(h0b, wb["w_in"], cs, e_mat, t=min(2 * t_mm, s), comm=plan.gather("proj"))
    plan.gathered("proj", couts, wb)
    sub_a = max(1, min(sub_a, s // tq_a))
    subs_b = [max(1, min(sub_b, s // dil // tq_b)) for dil in DILATIONS]
    out_a, lse_a, couts = _swa_fwd_p(za, qcol=0, kcol=4, vcol=5, hq=WIN_Q_HEADS, hkv=WIN_KV_HEADS, w=WIN_HALF,
                                     tq=tq_a, sub=sub_a, sink=sp["attn_sink"], name="attn_a_fwd",
                                     comm=plan.gather("attn_a"))
    plan.gathered("attn_a", couts, wb)
    o_g, lse_g = [], []
    for gi in range(3):
        o, l, couts = _swa_fwd_p(zb[gi], qcol=0, kcol=1, vcol=2, hq=DIL_SLOTS, hkv=DIL_SLOTS, w=DIL_HALF, tq=tq_b,
                                 sub=subs_b[gi], sink=None, name=f"attn_b{gi}_fwd",
                                 comm=plan.gather(f"attn_b{gi}"))
        plan.gathered(f"attn_b{gi}", couts, wb)
        o_g.append(o)
        lse_g.append(_stats_to_lanes(l))
    mixed_b, out_b, lse_b, r1, h1b = _combine_fwd(
        out_a, o_g, lse_g, sp["g_win"], sp["g_dil"], wb["w_mix_out"], x, sp["ln_in_g"], sp["ln_in_b"],
        sp["ln1_g"], sp["ln1_b"], t=t_row)
    mem_nb, kx, vx = _mem_fwd(mem, sp["mem_ln_g"], sp["mem_ln_b"], wb["w_xk"], wb["w_xv"])
    r2, h2b, qxb, oxb, lse_x = _xattn_fwd(h1b, r1, kx, vx, wb["w_xq"], wb["w_xo"], sp["ln1_g"], sp["ln1_b"],
                                          sp["ln2_g"], sp["ln2_b"], t=t_mm)
    g = _mm(h2b, wb["w_gate"], mode="nt", out_dtype=F32, tm=t_mm, tn=D_FF, name="ff_gate")
    u = _mm(h2b, wb["w_up"], mode="nt", out_dtype=F32, tm=t_mm, tn=D_FF, name="ff_up")
    tb, dr3, dr3b, st3 = _ffn_out(g, u, sp["conv_w"], sp["conv_b"], wb["w_down"], r2, target, sp["ln2_g"],
                                  sp["ln2_b"], sp["ln3_g"], sp["ln3_b"], t=t_row)

    grads = {}
    du, dgc, st_conv = _conv_bwd_a(dr3b, wb["w_down"], g, u, sp["conv_w"], sp["conv_b"], t=t_row)
    tk = min(2048, s)
    grads["w_down"] = _mm(tb, dr3b, mode="tn", out_dtype=BF16, tm=D_FF // 2, tn=D_MODEL, tk=tk, name="dw_down")
    grads["w_up"] = _mm(du, h2b, mode="tn", out_dtype=BF16, tm=D_FF // 2, tn=D_MODEL, tk=tk, name="dw_up")
    (dg, dr2, dr2b, st2), couts = _dh2_ln2(dgc, sp["conv_w"], du, wb["w_gate"], wb["w_up"], dr3, r2, sp["ln2_g"],
                                           t=t_mm, comm=plan.exchange("dh2", grads))
    plan.exchanged("dh2", couts)
    grads["w_gate"] = _mm(dg, h2b, mode="tn", out_dtype=BF16, tm=D_FF // 2, tn=D_MODEL, tk=tk, name="dw_gate")

    (dr1, dr1b, dqxb, dkx, dvx, st1), couts = _xattn_bwd(
        dr2, qxb, oxb, lse_x, kx, vx, wb["w_xq"], wb["w_xo"], r1, sp["ln1_g"], t=t_mm,
        comm=plan.exchange("xattn", grads))
    plan.exchanged("xattn", couts)
    grads["w_xo"] = _mm(oxb, dr2b, mode="tn", out_dtype=BF16, tm=D_MODEL, tn=D_MODEL, tk=tk, name="dw_xo")
    grads["w_xq"] = _mm(h1b, dqxb, mode="tn", out_dtype=BF16, tm=D_MODEL, tn=D_MODEL, tk=tk, name="dw_xq")
    grads["w_xk"], grads["w_xv"], st_mem = _mem_bwd(dkx, dvx, mem, sp["mem_ln_g"], sp["mem_ln_b"],
                                                    wb["w_xk"], wb["w_xv"])

    grads["w_mix_out"] = _mm(mixed_b, dr1b, mode="tn", out_dtype=BF16, tm=D_MODEL, tn=D_MODEL, tk=tk,
                             name="dw_mix")
    do_a, do_b, dl_a, dl_b, st_mix = _combine_bwd(dr1b, wb["w_mix_out"], out_a, out_b, sp["g_win"], sp["g_dil"],
                                                  t=t_row)
    (dqa, dka, dva, dsink), couts = _swa_bwd_p(
        za, do_a, lse_a, _stats_to_rows(dl_a, 1), cs[None], e_mat, qcol=0, kcol=4, vcol=5, hq=WIN_Q_HEADS,
        hkv=WIN_KV_HEADS, w=WIN_HALF, tq=2 * tq_a, sub=max(1, sub_a // 2), sink=sp["attn_sink"], name="attn_a_bwd",
        comm=plan.exchange("attn_a", grads))
    plan.exchanged("attn_a", couts)
    dqs, dks, dvs = [], [], []
    for gi, dil in enumerate(DILATIONS):
        (dq, dk, dv), couts = _swa_bwd_p(
            zb[gi], do_b[gi], _stats_to_rows(lse_b, dil), _stats_to_rows(dl_b, dil),
            _to_residue(cs, dil), e_mat, qcol=0, kcol=1, vcol=2, hq=DIL_SLOTS, hkv=DIL_SLOTS, w=DIL_HALF, tq=tq_b,
            sub=subs_b[gi], sink=None, name=f"attn_b{gi}_bwd", comm=plan.exchange(f"attn_b{gi}", grads))
        plan.exchanged(f"attn_b{gi}", couts)
        dqs.append(dq)
        dks.append(dk)
        dvs.append(dv)
    dz = _assemble_dz(dqa, dka, dva, dqs, dks, dvs, t=t_row)
    grads["w_in"] = _mm(dz, h0b, mode="tn", out_dtype=BF16, tm=IN_WIDTH // 7, tn=D_MODEL, tk=tk, name="dw_in")
    (grad_x, st0), couts = _dh0_ln_in(dz, wb["w_in"], dr1, x, sp["ln_in_g"], t=t_mm,
                                      comm=plan.exchange("dh0", grads))
    plan.exchanged("dh0", couts)

    small = {
        "loss": st3[2:3, 0:1],
        "ln_in_g": st0[0:1], "ln_in_b": st0[1:2],
        "attn_sink": dsink[:, 0].reshape(1, WIN_Q_HEADS),
        "g_win": st_mix[0:1], "g_dil": st_mix[1:2],
        "ln1_g": st1[0:1], "ln1_b": st1[1:2],
        "mem_ln_g": st_mem[0:1], "mem_ln_b": st_mem[1:2],
        "ln2_g": st2[0:1], "ln2_b": st2[1:2],
        "conv_w": st_conv[0:3], "conv_b": st_conv[3:4],
        "ln3_g": st3[0:1], "ln3_b": st3[1:2],
    }
    return grad_x, grads, small


class _SiblingSwap:
    def __init__(self, arrays):
        self.inputs = list(arrays)
        n = len(arrays)
        self.out_shape = [jax.ShapeDtypeStruct(a.shape, a.dtype) for a in arrays]
        self.scratch = [pltpu.SemaphoreType.DMA((n,)), pltpu.SemaphoreType.DMA((n,))]

    def _copies(self, src, dst, sems):
        send_sems, recv_sems = sems
        x, y, c, _ = _place()
        return [pltpu.make_async_remote_copy(
            src_ref=src[a], dst_ref=dst[a], send_sem=send_sems.at[a], recv_sem=recv_sems.at[a],
            device_id=(x, y, 1 - c), device_id_type=MESH_IDS) for a in range(len(src))]

    def start(self, src, dst, sems):
        for cp in self._copies(src, dst, sems):
            cp.start()

    def wait(self, src, dst, sems):
        copies = self._copies(src, dst, sems)
        for cp in copies:
            cp.wait_recv()
        for cp in copies:
            cp.wait_send()


class _Both:
    def __init__(self, first, second):
        self.parts = (first, second)
        self.inputs = first.inputs + second.inputs
        self.out_shape = first.out_shape + second.out_shape
        self.scratch = first.scratch + second.scratch

    def _split(self, src, dst, sems):
        a = self.parts[0]
        ni, no, ns = len(a.inputs), len(a.out_shape), len(a.scratch)
        return ((src[:ni], dst[:no], sems[:ns]), (src[ni:], dst[no:], sems[ns:]))

    def start(self, src, dst, sems):
        for part, args in zip(self.parts, self._split(src, dst, sems)):
            part.start(*args)

    def wait(self, src, dst, sems):
        for part, args in zip(self.parts, self._split(src, dst, sems)):
            part.wait(*args)


def _row_tile(rows, cols, itemsize=4, budget=1 << 20):
    best = None
    for t in range(16, rows + 1, 16):
        if rows % t == 0 and t * cols * itemsize <= budget:
            best = t
    return best or rows


def _sum_slots(stack, *, name):
    n, r, c = stack.shape
    t = _row_tile(r, c)

    def body(s_ref, o_ref):
        acc = s_ref[0].astype(F32)
        for q in range(1, n):
            acc = acc + s_ref[q].astype(F32)
        o_ref[...] = acc

    return pl.pallas_call(
        body, name=name, grid=(r // t,), in_specs=[pl.BlockSpec((n, t, c), lambda i: (0, i, 0))],
        out_specs=pl.BlockSpec((t, c), lambda i: (i, 0)), out_shape=jax.ShapeDtypeStruct((r, c), F32),
        compiler_params=_cparams(dimension_semantics=("parallel",)),
    )(stack)


def _adamw(w, m, v, p, q, *, name):
    r, c = w.shape
    t = _row_tile(r, c, budget=1 << 20)

    def body(*refs):
        if q is None:
            w_ref, m_ref, v_ref, p_ref, g_ref, d_ref, nm_ref, nv_ref = refs
            g = p_ref[...]
        else:
            w_ref, m_ref, v_ref, p_ref, q_ref, g_ref, d_ref, nm_ref, nv_ref = refs
            g = p_ref[...] + q_ref[...]
        nm = ADAM_B1 * m_ref[...] + (1.0 - ADAM_B1) * g
        nv = ADAM_B2 * v_ref[...] + (1.0 - ADAM_B2) * (g * g)
        m_hat = nm / (1.0 - ADAM_B1 ** ADAM_STEP)
        v_hat = nv / (1.0 - ADAM_B2 ** ADAM_STEP)
        g_ref[...] = g
        d_ref[...] = -ADAM_LR * (m_hat / (jnp.sqrt(v_hat) + ADAM_EPS) + ADAM_WD * w_ref[...])
        nm_ref[...] = nm
        nv_ref[...] = nv

    tile = pl.BlockSpec((t, c), lambda i: (i, 0))
    args = [w, m, v, p] + ([] if q is None else [q])
    sh = jax.ShapeDtypeStruct((r, c), F32)
    return pl.pallas_call(
        body, name=name, grid=(r // t,), in_specs=[tile] * len(args), out_specs=[tile] * 4, out_shape=[sh] * 4,
        compiler_params=_cparams(dimension_semantics=("parallel",)),
    )(*args)


BIG = ("w_in", "w_mix_out", "w_xq", "w_xk", "w_xv", "w_xo", "w_gate", "w_up", "w_down")
COL_SHARDED = ("w_in", "w_gate", "w_up")
WEIGHTS = ("ln_in_g", "ln_in_b", "w_in", "attn_sink", "g_win", "g_dil", "w_mix_out", "ln1_g", "ln1_b",
           "mem_ln_g", "mem_ln_b", "w_xq", "w_xk", "w_xv", "w_xo", "ln2_g", "ln2_b", "w_gate", "w_up",
           "conv_w", "conv_b", "w_down", "ln3_g", "ln3_b")
SMALL = tuple(k for k in WEIGHTS if k not in BIG)
PACK_COLS = 1024
CONV_SHARD = D_FF // N_CHIPS
CONV_WIDTH_ROWS = 3
SMALL_ROWS = 32


GATHER_STAGES = {"ln_in": ("w_in", "conv_w"), "proj": ("w_mix_out", "w_xq", "w_xk", "w_xv", "w_xo", "w_up"),
                 "attn_a": ("w_gate",), "attn_b0": ("w_down",)}
EXCHANGE_STAGES = {"dh2": ("w_down",), "xattn": ("w_up",), "attn_a": ("w_gate", "w_xo", "w_xq"),
                   "attn_b0": ("w_xk", "w_xv", "w_mix_out"), "dh0": ("w_in",)}


def _full_weight(k, g4):
    return g4.reshape(N_CHIPS * g4.shape[1], g4.shape[2])


def _grad_parts(k, gk):
    gk = gk.astype(BF16)
    return gk.reshape(N_CHIPS, gk.shape[0] // N_CHIPS, gk.shape[1])


EARLY_SWAP_STAGE = "attn_b2"


class _Plan:
    def __init__(self, shards):
        self.shards = shards
        self.recv = {}
        self.chip_sums = {}
        self.sibling_sums = {}

    def gather(self, stage):
        names = GATHER_STAGES.get(stage)
        return _ChipGather([self.shards[k] for k in names]) if names else None

    def gathered(self, stage, couts, wb):
        for k, g4 in zip(GATHER_STAGES.get(stage, ()), couts):
            if k == "conv_w":
                taps = g4[:, :CONV_WIDTH_ROWS, :CONV_SHARD]
                wb[k] = taps.transpose(1, 0, 2).reshape(CONV_WIDTH_ROWS, D_FF)
            else:
                wb[k] = _full_weight(k, g4)

    def exchange(self, stage, grads):
        if stage == EARLY_SWAP_STAGE:
            self.early = [k for k in BIG if k in self.recv]
            for k in self.early:
                self.chip_sums[k] = _sum_slots(self.recv[k], name=f"sum_chips_{k}")
            return _SiblingSwap([self.chip_sums[k] for k in self.early])
        names = EXCHANGE_STAGES.get(stage)
        return _ChipExchange([_grad_parts(k, grads[k]) for k in names]) if names else None

    def exchanged(self, stage, couts):
        if stage == EARLY_SWAP_STAGE:
            self.sibling_sums.update(zip(self.early, couts))
            return
        for k, r4 in zip(EXCHANGE_STAGES.get(stage, ()), couts):
            self.recv[k] = r4


def _pack_rows(a):
    r, n = a.shape
    per = -(-n // PACK_COLS)
    return jnp.pad(a, ((0, 0), (0, per * PACK_COLS - n))).reshape(r * per, PACK_COLS)


def _unpack_rows(p, r, n):
    per = -(-n // PACK_COLS)
    return p.reshape(r, per * PACK_COLS)[:, :n]


def _pack(pieces, rows_total):
    cat = jnp.concatenate([_pack_rows(a) for a in pieces], axis=0)
    return jnp.pad(cat, ((0, rows_total - cat.shape[0]), (0, 0)))


def _unpack(p, shapes):
    out, at = [], 0
    for r, n in shapes:
        per = -(-n // PACK_COLS)
        out.append(_unpack_rows(p[at:at + r * per], r, n))
        at += r * per
    return out


def kernel(x, mem, positions, ln_in_g, ln_in_b, w_in, attn_sink, g_win, g_dil, w_mix_out, ln1_g, ln1_b, mem_ln_g, mem_ln_b, w_xq, w_xk, w_xv, w_xo, ln2_g, ln2_b, w_gate, w_up, conv_w, conv_b, w_down, ln3_g, ln3_b, loss_target, m_ln_in_g, m_ln_in_b, m_w_in, m_attn_sink, m_g_win, m_g_dil, m_w_mix_out, m_ln1_g, m_ln1_b, m_mem_ln_g, m_mem_ln_b, m_w_xq, m_w_xk, m_w_xv, m_w_xo, m_ln2_g, m_ln2_b, m_w_gate, m_w_up, m_conv_w, m_conv_b, m_w_down, m_ln3_g, m_ln3_b, v_ln_in_g, v_ln_in_b, v_w_in, v_attn_sink, v_g_win, v_g_dil, v_w_mix_out, v_ln1_g, v_ln1_b, v_mem_ln_g, v_mem_ln_b, v_w_xq, v_w_xk, v_w_xv, v_w_xo, v_ln2_g, v_ln2_b, v_w_gate, v_w_up, v_conv_w, v_conv_b, v_w_down, v_ln3_g, v_ln3_b):
    given = dict(locals())
    shape_of = {k: given[k].shape for k in WEIGHTS}
    as2d = lambda k, a: a.reshape(-1, a.shape[-1]).T if k in COL_SHARDED else a.reshape(-1, a.shape[-1])
    w2 = {k: as2d(k, given[k]) for k in WEIGHTS}
    m2 = {k: as2d(k, given["m_" + k]) for k in WEIGHTS}
    v2 = {k: as2d(k, given["v_" + k]) for k in WEIGHTS}
    chip = 2 * lax.axis_index("x") + lax.axis_index("y")

    shards = {k: w2[k].astype(BF16) for k in BIG}
    shards["conv_w"] = jnp.pad(w2["conv_w"], ((0, 16 - CONV_WIDTH_ROWS), (0, PACK_COLS - CONV_SHARD)))
    plan = _Plan(shards)
    sp = {k: w2[k] for k in SMALL if k != "conv_w"}

    grad_x, grads, small = _local_step(x[0], mem[0], positions[0], loss_target[0], {}, sp, plan)

    small_keys = ("loss",) + SMALL
    small_shapes = [small[k].shape for k in small_keys]
    small_pack = _pack([small[k] for k in small_keys], SMALL_ROWS)
    late = [k for k in BIG if k not in plan.chip_sums]
    for k in late:
        plan.chip_sums[k] = _sum_slots(plan.recv[k], name=f"sum_chips_{k}")
    *late_sibling, small_all = _comm_only(
        _Both(_SiblingSwap([plan.chip_sums[k] for k in late]), _ChipExchange([], small_pack)), "swap_and_small")
    plan.sibling_sums.update(zip(late, late_sibling))
    chip_sums = [plan.chip_sums[k] for k in BIG]
    sibling_sums = [plan.sibling_sums[k] for k in BIG]
    small_sum = _sum_slots(small_all, name="sum_small")
    small_g = dict(zip(small_keys, _unpack(small_sum, small_shapes)))
    loss = small_g["loss"][0, 0]

    res = {}
    for k, p, q in zip(BIG, chip_sums, sibling_sums):
        res[k] = _adamw(w2[k], m2[k], v2[k], p, q, name=f"adamw_{k}")
    small_g["conv_w"] = lax.dynamic_slice_in_dim(small_g["conv_w"], chip * CONV_SHARD, CONV_SHARD, axis=1)
    adam_shapes = [w2[k].shape for k in SMALL]
    packs = [_pack([d[k] for k in SMALL], SMALL_ROWS) for d in (w2, m2, v2, small_g)]
    small_res = [_unpack(o, adam_shapes) for o in _adamw(*packs, None, name="adamw_small")]
    for i, k in enumerate(SMALL):
        res[k] = tuple(o[i] for o in small_res)

    outs = [loss, grad_x[None]]
    for slot in range(4):
        outs += [(res[k][slot].T if k in COL_SHARDED else res[k][slot]).reshape(shape_of[k]) for k in WEIGHTS]
    return tuple(outs)
```

```python
import functools
import math

import jax
import jax.numpy as jnp
from jax import lax
from jax.experimental import pallas as pl
from jax.experimental.pallas import tpu as pltpu

F32 = jnp.float32
BF16 = jnp.bfloat16

D_MODEL = 1024
HEAD_DIM = 64
WIN_Q_HEADS = 8
WIN_KV_HEADS = 2
WIN_HALF = 128
DIL_SLOTS = 8
DILATIONS = (1, 4, 16)
DIL_HALF = 64
ROT_DIM = 16
ROPE_THETA = 500000.0
X_HEADS = 4
X_HEAD_DIM = 256
D_FF = 2816
A_Q = 512
A_KV = 128
A_WIDTH = A_Q + 2 * A_KV
B_QKV = 1536
IN_WIDTH = 5376
ALPHA = 2.0 ** 0.25
LN_EPS = 1e-5
NEG_INF = -1e30
LANES = 128
N_CHIPS = 4
N_DEV = 8

ADAM_LR = 0.001
ADAM_B1 = 0.9
ADAM_B2 = 0.999
ADAM_EPS = 1e-08
ADAM_WD = 0.01
ADAM_STEP = 10

VMEM_LIMIT = 56 * 1024 * 1024


def _cparams(**kw):
    return pltpu.CompilerParams(vmem_limit_bytes=VMEM_LIMIT, **kw)


def _dot(a, b):
    return lax.dot_general(a, b, (((1,), (0,)), ((), ())), preferred_element_type=F32)


def _dot_nt(a, b):
    return lax.dot_general(a, b, (((1,), (1,)), ((), ())), preferred_element_type=F32)


def _dot_tn(a, b):
    return lax.dot_general(a, b, (((0,), (0,)), ((), ())), preferred_element_type=F32)


def _ln(x, g, b):
    mu = jnp.mean(x, axis=-1, keepdims=True)
    xc = x - mu
    var = jnp.mean(xc * xc, axis=-1, keepdims=True)
    return xc * lax.rsqrt(var + LN_EPS) * g + b


def _ln_bwd_math(dy, r, g):
    mu = jnp.mean(r, axis=-1, keepdims=True)
    xc = r - mu
    var = jnp.mean(xc * xc, axis=-1, keepdims=True)
    rstd = lax.rsqrt(var + LN_EPS)
    xhat = xc * rstd
    dxhat = dy * g
    m1 = jnp.mean(dxhat, axis=-1, keepdims=True)
    m2 = jnp.mean(dxhat * xhat, axis=-1, keepdims=True)
    dr = rstd * (dxhat - m1 - xhat * m2)
    return dr, jnp.sum(dy * xhat, axis=0, keepdims=True), jnp.sum(dy, axis=0, keepdims=True)


def _rope(z, ta, tb, tc, sign):
    w = z.shape[1]
    reps = w // LANES
    a = jnp.tile(ta, (1, reps))
    b = jnp.tile(tb, (1, reps))
    c = jnp.tile(tc, (1, reps))
    return z * a + sign * (pltpu.roll(z, w - 8, 1) * b + pltpu.roll(z, 8, 1) * c)


def _shift_rows(x, prev_row, next_row):
    t = x.shape[0]
    sub = 8
    row = lax.broadcasted_iota(jnp.int32, (sub, x.shape[1]), 0)
    down, up = pltpu.roll(x, 1, 0), pltpu.roll(x, t - 1, 0)
    xm1 = jnp.concatenate([jnp.where(row == 0, prev_row, down[:sub]), down[sub:]], axis=0)
    xp1 = jnp.concatenate([up[:t - sub], jnp.where(row == sub - 1, next_row, up[t - sub:])], axis=0)
    return xm1, xp1


def _rope_tabs(cs, e_mat):
    hi = cs.astype(BF16)
    rest = cs - hi.astype(F32)
    mid = rest.astype(BF16)
    lo = (rest - mid.astype(F32)).astype(BF16)
    tabs = _dot(hi, e_mat) + _dot(mid, e_mat) + _dot(lo, e_mat)
    lane = lax.broadcasted_iota(jnp.int32, (cs.shape[0], LANES), 1)
    ones = jnp.where((lane & (HEAD_DIM - 1)) >= ROT_DIM, 1.0, 0.0)
    return tabs[:, :LANES] + ones, tabs[:, LANES:2 * LANES], tabs[:, 2 * LANES:]


def _rope_select_matrix():
    half = ROT_DIM // 2
    e = [[0.0] * (3 * LANES) for _ in range(ROT_DIM)]
    for lane in range(LANES):
        d = lane % HEAD_DIM
        if d < half:
            e[d][lane] = 1.0
            e[half + d][LANES + lane] = -1.0
        elif d < ROT_DIM:
            e[d - half][lane] = 1.0
            e[d][2 * LANES + lane] = 1.0
    return jnp.array(e, BF16)


def _rope_rows(x, cos_t, sin_t, sign):
    half = ROT_DIM // 2
    parts = []
    for base in (0, HEAD_DIM):
        r1, r2 = x[base:base + half], x[base + half:base + ROT_DIM]
        parts += [r1 * cos_t - sign * (r2 * sin_t), r2 * cos_t + sign * (r1 * sin_t), x[base + ROT_DIM:base + HEAD_DIM]]
    return jnp.concatenate(parts, axis=0)


MESH_IDS = pl.DeviceIdType.MESH
ANY = pl.BlockSpec(memory_space=pl.ANY)


def _place():
    x, y, c = lax.axis_index("x"), lax.axis_index("y"), lax.axis_index("c")
    other_chips = [(1 - x, y), (x, 1 - y), (1 - x, 1 - y)]
    return x, y, c, other_chips


class _ChipGather:
    def __init__(self, shards):
        self.inputs = list(shards)
        n = len(shards)
        self.out_shape = [jax.ShapeDtypeStruct((N_CHIPS,) + a.shape, a.dtype) for a in shards]
        self.scratch = [pltpu.SemaphoreType.DMA((6 * n,)), pltpu.SemaphoreType.DMA((6 * n,)),
                        pltpu.SemaphoreType.DMA((n,))]

    def _copies(self, src, dst, sems):
        send_sems, recv_sems, local_sems = sems
        x, y, c, chips = _place()
        mine = 2 * x + y
        n = len(src)
        local, sends, recvs, passes, pass_recvs = [], [], [], [], []
        for a in range(n):
            half = src[a].shape[0] // 2
            my_rows, other_rows = pl.ds(c * half, half), pl.ds((1 - c) * half, half)
            local.append(pltpu.make_async_copy(src[a], dst[a].at[mine], local_sems.at[a]))
            for j, (px, py) in enumerate(chips):
                k, k2, slot = 3 * a + j, 3 * n + 3 * a + j, 2 * px + py
                sends.append(pltpu.make_async_remote_copy(
                    src_ref=src[a].at[my_rows], dst_ref=dst[a].at[mine, my_rows], send_sem=send_sems.at[k],
                    recv_sem=recv_sems.at[k], device_id=(px, py, c), device_id_type=MESH_IDS))
                recvs.append(pltpu.make_async_remote_copy(
                    src_ref=src[a].at[my_rows], dst_ref=dst[a].at[slot, my_rows], send_sem=send_sems.at[k],
                    recv_sem=recv_sems.at[k], device_id=(px, py, c), device_id_type=MESH_IDS))
                passes.append(pltpu.make_async_remote_copy(
                    src_ref=dst[a].at[slot, my_rows], dst_ref=dst[a].at[slot, my_rows], send_sem=send_sems.at[k2],
                    recv_sem=recv_sems.at[k2], device_id=(x, y, 1 - c), device_id_type=MESH_IDS))
                pass_recvs.append(pltpu.make_async_remote_copy(
                    src_ref=dst[a].at[slot, my_rows], dst_ref=dst[a].at[slot, other_rows],
                    send_sem=send_sems.at[k2], recv_sem=recv_sems.at[k2], device_id=(x, y, 1 - c),
                    device_id_type=MESH_IDS))
        return local, sends, recvs, passes, pass_recvs

    def start(self, src, dst, sems):
        local, sends, _, _, _ = self._copies(src, dst, sems)
        for cp in local + sends:
            cp.start()

    def wait(self, src, dst, sems):
        local, sends, recvs, passes, pass_recvs = self._copies(src, dst, sems)
        for idx, landed in enumerate(recvs):
            landed.wait_recv()
            if passes:
                passes[idx].start()
        for cp in pass_recvs:
            cp.wait_recv()
        for cp in sends + passes:
            cp.wait_send()
        for cp in local:
            cp.wait()


class _ChipExchange:
    def __init__(self, parts, small=None):
        self.inputs = list(parts) + ([small] if small is not None else [])
        self.n = len(parts)
        self.has_small = small is not None
        self.out_shape = [jax.ShapeDtypeStruct(a.shape, a.dtype) for a in parts]
        n_sem, n_loc = 3 * self.n, self.n
        if self.has_small:
            self.out_shape.append(jax.ShapeDtypeStruct((N_DEV,) + small.shape, small.dtype))
            n_sem, n_loc = n_sem + N_DEV - 1, n_loc + 1
        self.scratch = [pltpu.SemaphoreType.DMA((n_sem,)), pltpu.SemaphoreType.DMA((n_sem,)),
                        pltpu.SemaphoreType.DMA((n_loc,))]

    def _copies(self, src, dst, sems):
        send_sems, recv_sems, local_sems = sems
        x, y, c, chips = _place()
        mine = 2 * x + y
        n = self.n
        local, sends, recvs = [], [], []
        for a in range(n):
            local.append(pltpu.make_async_copy(src[a].at[mine], dst[a].at[mine], local_sems.at[a]))
            for j, (px, py) in enumerate(chips):
                k = 3 * a + j
                sends.append(pltpu.make_async_remote_copy(
                    src_ref=src[a].at[2 * px + py], dst_ref=dst[a].at[mine], send_sem=send_sems.at[k],
                    recv_sem=recv_sems.at[k], device_id=(px, py, c), device_id_type=MESH_IDS))
                recvs.append(pltpu.make_async_remote_copy(
                    src_ref=src[a].at[mine], dst_ref=dst[a].at[2 * px + py], send_sem=send_sems.at[k],
                    recv_sem=recv_sems.at[k], device_id=(px, py, c), device_id_type=MESH_IDS))
        if self.has_small:
            me_dev = 4 * x + 2 * y + c
            local.append(pltpu.make_async_copy(src[n], dst[n].at[me_dev], local_sems.at[n]))
            for mask in range(1, N_DEV):
                px, py, pc = x ^ ((mask >> 2) & 1), y ^ ((mask >> 1) & 1), c ^ (mask & 1)
                k = 3 * n + mask - 1
                sends.append(pltpu.make_async_remote_copy(
                    src_ref=src[n], dst_ref=dst[n].at[me_dev], send_sem=send_sems.at[k], recv_sem=recv_sems.at[k],
                    device_id=(px, py, pc), device_id_type=MESH_IDS))
                recvs.append(pltpu.make_async_remote_copy(
                    src_ref=src[n], dst_ref=dst[n].at[4 * px + 2 * py + pc], send_sem=send_sems.at[k],
                    recv_sem=recv_sems.at[k], device_id=(px, py, pc), device_id_type=MESH_IDS))
        return local, sends, recvs, [], []

    start = _ChipGather.start
    wait = _ChipGather.wait


def _pcall(body, *, name, grid, in_specs, out_specs, out_shape, args, scratch_shapes=(), dims=None, comm=None):
    in_specs, out_specs, out_shape = list(in_specs), list(out_specs), list(out_shape)
    scratch_shapes = list(scratch_shapes)
    if comm is None:
        outs = pl.pallas_call(
            body, name=name, grid=grid, in_specs=in_specs, out_specs=out_specs, out_shape=out_shape,
            scratch_shapes=scratch_shapes, compiler_params=_cparams(dimension_semantics=dims),
        )(*args)
        return list(outs), []
    n_in, n_out, n_scr = len(in_specs), len(out_specs), len(scratch_shapes)
    n_cin, n_cout = len(comm.inputs), len(comm.out_shape)

    def wrapped(*refs):
        ins, refs = refs[:n_in], refs[n_in:]
        cins, refs = refs[:n_cin], refs[n_cin:]
        outs, refs = refs[:n_out], refs[n_out:]
        couts, refs = refs[:n_cout], refs[n_cout:]
        scr, csems = refs[:n_scr], refs[n_scr:]
        first = last = None
        for axis, size in enumerate(grid):
            pid = pl.program_id(axis)
            f, l = pid == 0, pid == size - 1
            first = f if first is None else first & f
            last = l if last is None else last & l

        @pl.when(first)
        def _():
            comm.start(cins, couts, csems)

        body(*ins, *outs, *scr)

        @pl.when(last)
        def _():
            comm.wait(cins, couts, csems)

    res = pl.pallas_call(
        wrapped, name=name, grid=grid, in_specs=in_specs + [ANY] * n_cin, out_specs=out_specs + [ANY] * n_cout,
        out_shape=out_shape + list(comm.out_shape), scratch_shapes=scratch_shapes + list(comm.scratch),
        compiler_params=_cparams(dimension_semantics=("arbitrary",) * len(grid)),
    )(*args, *comm.inputs)
    return list(res[:n_out]), list(res[n_out:])


def _comm_only(comm, name):
    def body(*refs):
        n_cin, n_cout = len(comm.inputs), len(comm.out_shape)
        cins, couts, csems = refs[:n_cin], refs[n_cin:n_cin + n_cout], refs[n_cin + n_cout:]
        comm.start(cins, couts, csems)
        comm.wait(cins, couts, csems)

    return list(pl.pallas_call(
        body, name=name, in_specs=[ANY] * len(comm.inputs), out_specs=[ANY] * len(comm.out_shape),
        out_shape=list(comm.out_shape), scratch_shapes=list(comm.scratch),
    )(*comm.inputs))


def _mm(a, b, *, mode, out_dtype, tm, tn, tk=None, name):
    if mode == "nt":
        m, k = a.shape
        n = b.shape[0]
        assert m % tm == 0 and n % tn == 0

        def body(a_ref, b_ref, o_ref):
            o_ref[...] = _dot_nt(a_ref[...], b_ref[...]).astype(out_dtype)

        return pl.pallas_call(
            body, name=name, grid=(m // tm, n // tn),
            in_specs=[pl.BlockSpec((tm, k), lambda i, j: (i, 0)), pl.BlockSpec((tn, k), lambda i, j: (j, 0))],
            out_specs=pl.BlockSpec((tm, tn), lambda i, j: (i, j)),
            out_shape=jax.ShapeDtypeStruct((m, n), out_dtype),
            compiler_params=_cparams(dimension_semantics=("parallel", "parallel")),
        )(a, b)
    assert mode == "tn"
    kk, m = a.shape
    n = b.shape[1]
    assert m % tm == 0 and n % tn == 0 and kk % tk == 0
    nk = kk // tk

    def body(a_ref, b_ref, o_ref, acc_ref):
        kstep = pl.program_id(2)

        @pl.when(kstep == 0)
        def _():
            acc_ref[...] = jnp.zeros_like(acc_ref)

        acc_ref[...] += _dot_tn(a_ref[...], b_ref[...])

        @pl.when(kstep == nk - 1)
        def _():
            o_ref[...] = acc_ref[...].astype(out_dtype)

    return pl.pallas_call(
        body, name=name, grid=(m // tm, n // tn, nk),
        in_specs=[pl.BlockSpec((tk, tm), lambda i, j, s: (s, i)), pl.BlockSpec((tk, tn), lambda i, j, s: (s, j))],
        out_specs=pl.BlockSpec((tm, tn), lambda i, j, s: (i, j)),
        out_shape=jax.ShapeDtypeStruct((m, n), out_dtype),
        scratch_shapes=[pltpu.VMEM((tm, tn), F32)],
        compiler_params=_cparams(dimension_semantics=("parallel", "parallel", "arbitrary")),
    )(a, b)


PROJ_COLS = 256


def _proj_segments():
    wd = DIL_SLOTS * HEAD_DIM
    segs = [(1, [(0, 1), (PROJ_COLS, 1), (2 * PROJ_COLS, 2)])]
    for gi, dil in enumerate(DILATIONS):
        blocks = []
        for part, kind in enumerate((1, 1, 0)):
            col = A_WIDTH + part * B_QKV + gi * wd
            blocks += [(col, kind), (col + PROJ_COLS, kind)]
        segs.append((dil, blocks))
    return segs


PROJ_SEGMENTS = _proj_segments()


def _dh0_ln_in(dz, w_t, dr1, x, ln_in_g, *, t, comm=None):
    s, k = dz.shape

    def body(dz_ref, w_ref, dr1_ref, x_ref, g_ref, gx_ref, st_ref):
        i = pl.program_id(0)

        @pl.when(i == 0)
        def _():
            st_ref[...] = jnp.zeros_like(st_ref)

        dh0 = _dot(dz_ref[...], w_ref[...]) + ALPHA * dr1_ref[...]
        dx, dg, db = _ln_bwd_math(dh0, x_ref[...], g_ref[...])
        gx_ref[...] = dx
        st_ref[0:1, :] += dg
        st_ref[1:2, :] += db

    tile = pl.BlockSpec((t, D_MODEL), lambda i: (i, 0))
    return _pcall(
        body, name="dh0_ln_in", grid=(s // t,),
        in_specs=[pl.BlockSpec((t, k), lambda i: (i, 0)),
                  pl.BlockSpec((k, D_MODEL), lambda i: (0, 0), pipeline_mode=pl.Buffered(1)),
                  tile, tile, pl.BlockSpec((1, D_MODEL), lambda i: (0, 0))],
        out_specs=[tile, pl.BlockSpec((8, D_MODEL), lambda i: (0, 0))],
        out_shape=[jax.ShapeDtypeStruct((s, D_MODEL), F32), jax.ShapeDtypeStruct((8, D_MODEL), F32)],
        args=[dz, w_t, dr1, x, ln_in_g], dims=("arbitrary",), comm=comm)


def _ln_in_fwd(x, g, b, *, t, comm=None):
    s = x.shape[0]

    def body(x_ref, g_ref, b_ref, o_ref):
        o_ref[...] = _ln(x_ref[...], g_ref[...], b_ref[...]).astype(BF16)

    row = pl.BlockSpec((1, D_MODEL), lambda i: (0, 0))
    tile = pl.BlockSpec((t, D_MODEL), lambda i: (i, 0))
    outs, couts = _pcall(body, name="ln_in_fwd", grid=(s // t,), in_specs=[tile, row, row], out_specs=[tile],
                         out_shape=[jax.ShapeDtypeStruct((s, D_MODEL), BF16)], args=[x, g, b], dims=("parallel",),
                         comm=comm)
    return outs[0], couts


def _proj_all(h0b, w_t, cs, e_mat, *, t, comm=None):
    s = h0b.shape[0]
    cb = PROJ_COLS
    halves = cb // LANES

    def body(h_ref, w_ref, cs_ref, e_ref, *rest):
        z_refs, scr = rest[:-1], rest[-1]
        h = h_ref[...]
        ta, tb, tc = (jnp.tile(tab, (1, halves)) for tab in _rope_tabs(cs_ref[...], e_ref[...]))
        lane = lax.broadcasted_iota(jnp.int32, (t, cb), 1)
        slot = 0
        for z_ref, (dil, blocks) in zip(z_refs, PROJ_SEGMENTS):
            for jb, (col, kind) in enumerate(blocks):
                acc = _dot_nt(h, w_ref[col:col + cb, :])
                if kind:
                    z = acc * ta + (pltpu.roll(acc, cb - 8, 1) * tb + pltpu.roll(acc, 8, 1) * tc)
                    if kind == 2:
                        z = jnp.where(lane < LANES, z, acc)
                else:
                    z = acc
                if dil == 1:
                    z_ref[0, :, cb * jb:cb * (jb + 1)] = z.astype(BF16)
                    continue
                for half in range(halves):
                    scr[slot, half] = z[:, half * LANES:(half + 1) * LANES]
                for c in range(dil):
                    for half in range(halves):
                        rows = scr[slot, half, pl.ds(c, t // dil, stride=dil), :]
                        z_ref[c, :, cb * jb + half * LANES:cb * jb + (half + 1) * LANES] = rows.astype(BF16)
                slot = 1 - slot

    widths = [cb * len(blocks) for _, blocks in PROJ_SEGMENTS]
    dils = [dil for dil, _ in PROJ_SEGMENTS]
    outs, couts = _pcall(
        body, name="proj_all", grid=(s // t,),
        in_specs=[pl.BlockSpec((t, D_MODEL), lambda i: (i, 0)),
                  pl.BlockSpec((IN_WIDTH, D_MODEL), lambda i: (0, 0), pipeline_mode=pl.Buffered(1)),
                  pl.BlockSpec((t, ROT_DIM), lambda i: (i, 0)), pl.BlockSpec((ROT_DIM, 3 * LANES), lambda i: (0, 0))],
        out_specs=[pl.BlockSpec((dil, t // dil, wd), lambda i: (0, i, 0)) for dil, wd in zip(dils, widths)],
        out_shape=[jax.ShapeDtypeStruct((dil, s // dil, wd), BF16) for dil, wd in zip(dils, widths)],
        args=[h0b, w_t, cs, e_mat], scratch_shapes=[pltpu.VMEM((2, halves, t, LANES), F32)],
        dims=("parallel",), comm=comm)
    return outs, couts


PAIR = 2 * HEAD_DIM


def _place_head(x2, src_pos, dst_pos):
    hi = lax.broadcasted_iota(jnp.int32, x2.shape, 1) >= HEAD_DIM
    src = x2 if src_pos == dst_pos else pltpu.roll(x2, HEAD_DIM, 1)
    return jnp.where(hi == (dst_pos == 1), src, jnp.zeros_like(src))


def _band_mask_t(row0, tq, w, seq_len):
    tk = tq + 2 * w
    kk = lax.broadcasted_iota(jnp.int32, (tk, tq), 0)
    qq = lax.broadcasted_iota(jnp.int32, (tk, tq), 1)
    kpos = row0 - w + kk
    return (jnp.abs(qq + w - kk) <= w) & (kpos >= 0) & (kpos < seq_len)


def _halo_kv_specs(t, w, hkv, n, seq_len, kcol, vcol):
    kw = hkv * HEAD_DIM
    per, last = t // w, seq_len // w - 1
    cur = lambda s, i: jnp.minimum(i, n - 1)
    specs = []
    for c in (kcol, vcol):
        specs += [pl.BlockSpec((None, w, kw), lambda s, i, c=c: (s, jnp.maximum(cur(s, i) * per - 1, 0), c)),
                  pl.BlockSpec((None, t, kw), lambda s, i, c=c: (s, cur(s, i), c)),
                  pl.BlockSpec((None, w, kw), lambda s, i, c=c: (s, jnp.minimum((cur(s, i) + 1) * per, last), c))]
    return specs, cur


def _pair_kv(kfull, vfull, qp, rep, krows):
    ks, vs, a_of = [], [], []
    for pos in range(2):
        g = (2 * qp + pos) // rep
        a_of.append(g // 2)
        ks.append(_place_head(kfull[g // 2][krows], g % 2, pos))
        vs.append(_place_head(vfull[g // 2][krows], g % 2, pos))
    assert a_of[0] == a_of[1]
    return jnp.concatenate(ks, axis=0), jnp.concatenate(vs, axis=0), a_of[0]


def _swa_fwd_p(qkv, *, qcol, kcol, vcol, hq, hkv, w, tq, sub, sink, name, comm=None):
    nseq, seq_len, _ = qkv.shape
    t = tq * sub
    n = seq_len // t
    rep = hq // hkv
    tk = tq + 2 * w
    kv_specs, cur = _halo_kv_specs(t, w, hkv, n, seq_len, kcol, vcol)

    def body(*refs):
        if sink is not None:
            sink_ref, refs = refs[0], refs[1:]
        q_ref, kp_ref, kc_ref, kn_ref, vp_ref, vc_ref, vn_ref, o_ref, lse_ref = refs
        i = pl.program_id(1)
        kfull, vfull = [], []
        for a in range(hkv // 2):
            ls = slice(a * PAIR, (a + 1) * PAIR)
            kfull.append(jnp.concatenate([kp_ref[:, ls], kc_ref[:, ls], kn_ref[:, ls]], axis=0) * 0.125)
            vfull.append(jnp.concatenate([vp_ref[:, ls], vc_ref[:, ls], vn_ref[:, ls]], axis=0))
        row_hi = lax.broadcasted_iota(jnp.int32, (PAIR, tq), 0) >= HEAD_DIM
        for jj in range(sub):
            rows = slice(jj * tq, (jj + 1) * tq)
            mask_t = _band_mask_t(i * t + jj * tq, tq, w, seq_len)
            o_t, lse_rows = [], []
            for qp in range(hq // 2):
                kst, vst, _ = _pair_kv(kfull, vfull, qp, rep, slice(jj * tq, jj * tq + tk))
                s2 = _dot_nt(kst, q_ref[rows, qp * PAIR:(qp + 1) * PAIR])
                ps, dens = [], []
                for pos in range(2):
                    h = 2 * qp + pos
                    s_t = jnp.where(mask_t, s2[pos * tk:(pos + 1) * tk], NEG_INF)
                    m = jnp.max(s_t, axis=0, keepdims=True)
                    if sink is not None:
                        m = jnp.maximum(m, sink_ref[0, h])
                    p_t = jnp.exp(s_t - m)
                    den = jnp.sum(p_t, axis=0, keepdims=True)
                    if sink is not None:
                        den = den + jnp.exp(sink_ref[0, h] - m)
                    ps.append(p_t.astype(BF16))
                    dens.append(den)
                    lse_rows.append(m + jnp.log(den))
                both = _dot_tn(vst, jnp.concatenate(ps, axis=0))
                o_t.append(both / jnp.where(row_hi, dens[1], dens[0]))
            o_ref[rows, :] = jnp.concatenate(o_t, axis=0).T
            lse_ref[:, rows] = jnp.concatenate(lse_rows, axis=0)

    in_specs = [pl.BlockSpec((None, t, hq * HEAD_DIM), lambda s, i: (s, i, qcol))] + kv_specs
    args = [qkv] * 7
    if sink is not None:
        in_specs = [pl.BlockSpec(memory_space=pltpu.SMEM)] + in_specs
        args = [sink] + args
    (o, lse), couts = _pcall(
        body, name=name, grid=(nseq, n), in_specs=in_specs,
        out_specs=[pl.BlockSpec((None, t, hq * HEAD_DIM), lambda s, i: (s, i, 0)),
                   pl.BlockSpec((None, hq, t), lambda s, i: (s, 0, i))],
        out_shape=[jax.ShapeDtypeStruct((nseq, seq_len, hq * HEAD_DIM), F32),
                   jax.ShapeDtypeStruct((nseq, hq, seq_len), F32)],
        args=args, dims=("parallel", "parallel"), comm=comm)
    return o, lse, couts


def _swa_bwd_p(qkv, do, lse, delta, cs, e_mat, *, qcol, kcol, vcol, hq, hkv, w, tq, sub, sink, name, comm=None):
    nseq, seq_len, _ = qkv.shape
    t = tq * sub
    n = seq_len // t
    rep = hq // hkv
    qw, kw = hq * HEAD_DIM, hkv * HEAD_DIM
    tk = tq + 2 * w
    kv_specs, cur = _halo_kv_specs(t, w, hkv, n, seq_len, kcol, vcol)

    def body(*refs):
        if sink is not None:
            sink_ref, refs = refs[0], refs[1:]
        (q_ref, kp_ref, kc_ref, kn_ref, vp_ref, vc_ref, vn_ref, do_ref, lse_ref, dl_ref,
         cs_c, cs_p, e_ref) = refs[:13]
        outs = refs[13:]
        if sink is not None:
            dq_ref, dk_ref, dv_ref, dsink_ref, dk_acc, dv_acc, dk_win, dv_win = outs
        else:
            dq_ref, dk_ref, dv_ref, dk_acc, dv_acc, dk_win, dv_win = outs
        s_id = pl.program_id(0)
        i = pl.program_id(1)
        slot_p, slot_c, slot_n = (i + 2) % 3, i % 3, (i + 1) % 3

        if sink is not None:
            @pl.when((s_id == 0) & (i == 0))
            def _():
                dsink_ref[...] = jnp.zeros_like(dsink_ref)

        @pl.when(i < n)
        def _():
            dk_win[...] = jnp.zeros_like(dk_win)
            dv_win[...] = jnp.zeros_like(dv_win)
            kfull, vfull = [], []
            for a in range(hkv // 2):
                ls = slice(a * PAIR, (a + 1) * PAIR)
                kfull.append(jnp.concatenate([kp_ref[:, ls], kc_ref[:, ls], kn_ref[:, ls]], axis=0) * 0.125)
                vfull.append(jnp.concatenate([vp_ref[:, ls], vc_ref[:, ls], vn_ref[:, ls]], axis=0))
            for jj in range(sub):
                rows = slice(jj * tq, (jj + 1) * tq)
                krows = slice(jj * tq, jj * tq + tk)
                mask_t = _band_mask_t(i * t + jj * tq, tq, w, seq_len)
                dq_t = []
                dk2 = [None] * (hkv // 2)
                dv2 = [None] * (hkv // 2)
                for qp in range(hq // 2):
                    kst, vst, a = _pair_kv(kfull, vfull, qp, rep, krows)
                    q2 = q_ref[rows, qp * PAIR:(qp + 1) * PAIR]
                    do2 = do_ref[rows, qp * PAIR:(qp + 1) * PAIR]
                    s2 = _dot_nt(kst, q2)
                    dp2 = _dot_nt(vst, do2)
                    ds, ps, q_at, do_at = [], [], [], []
                    for pos in range(2):
                        h = 2 * qp + pos
                        e = (h // rep) % 2
                        half = slice(pos * tk, (pos + 1) * tk)
                        lse_h = lse_ref[h:h + 1, rows]
                        dl_h = dl_ref[h:h + 1, rows]
                        p_t = jnp.exp(jnp.where(mask_t, s2[half], NEG_INF) - lse_h)
                        ds.append((p_t * (dp2[half] - dl_h)).astype(BF16))
                        ps.append(p_t.astype(BF16))
                        q_at.append(_place_head(q2, pos, e) * 0.125)
                        do_at.append(_place_head(do2, pos, e))
                        if sink is not None:
                            ds_sink = -jnp.sum(jnp.exp(sink_ref[0, h] - lse_h) * dl_h)
                            dsink_ref[h:h + 1, :] += jnp.full((1, LANES), ds_sink, F32)
                    dq_t.append(_rope_rows(_dot_tn(kst, jnp.concatenate(ds, axis=0)),
                                           cs_c[0:ROT_DIM // 2, rows], cs_c[ROT_DIM // 2:ROT_DIM, rows], -1.0))
                    dk_part = _dot(jnp.concatenate(ds, axis=1), jnp.concatenate(q_at, axis=0))
                    dv_part = _dot(jnp.concatenate(ps, axis=1), jnp.concatenate(do_at, axis=0))
                    dk2[a] = dk_part if dk2[a] is None else dk2[a] + dk_part
                    dv2[a] = dv_part if dv2[a] is None else dv2[a] + dv_part
                for a in range(hkv // 2):
                    ls = slice(a * PAIR, (a + 1) * PAIR)
                    dk_win[krows, ls] += dk2[a]
                    dv_win[krows, ls] += dv2[a]
                dq_ref[rows, :] = jnp.concatenate(dq_t, axis=0).T.astype(BF16)

            @pl.when(i > 0)
            def _():
                dk_acc[slot_p, t - w:, :] += dk_win[:w, :]
                dv_acc[slot_p, t - w:, :] += dv_win[:w, :]

            @pl.when(i == 0)
            def _():
                dk_acc[slot_c] = dk_win[w:w + t, :]
                dv_acc[slot_c] = dv_win[w:w + t, :]

            @pl.when(i > 0)
            def _():
                dk_acc[slot_c] += dk_win[w:w + t, :]
                dv_acc[slot_c] += dv_win[w:w + t, :]

            dk_acc[slot_n] = jnp.zeros((t, kw), F32)
            dv_acc[slot_n] = jnp.zeros((t, kw), F32)
            dk_acc[slot_n, :w, :] = dk_win[w + t:, :]
            dv_acc[slot_n, :w, :] = dv_win[w + t:, :]

        @pl.when(i >= 1)
        def _():
            dk_ref[...] = _rope(dk_acc[slot_p], *_rope_tabs(cs_p[...], e_ref[...]), -1.0).astype(BF16)
            dv_ref[...] = dv_acc[slot_p].astype(BF16)

    row_c = lambda width: pl.BlockSpec((None, t, width), lambda s, i: (s, cur(s, i), 0))
    row_p = lambda width: pl.BlockSpec((None, t, width), lambda s, i: (s, jnp.maximum(i - 1, 0), 0))
    stat = pl.BlockSpec((None, hq, t), lambda s, i: (s, 0, cur(s, i)))
    cs_rows = pl.BlockSpec((None, ROT_DIM, t), lambda s, i: (s, 0, cur(s, i)))
    in_specs = ([pl.BlockSpec((None, t, qw), lambda s, i: (s, cur(s, i), qcol))] + kv_specs
                + [row_c(qw), stat, stat, cs_rows, row_p(ROT_DIM),
                   pl.BlockSpec((ROT_DIM, 3 * LANES), lambda s, i: (0, 0))])
    args = [qkv] * 7 + [do, lse, delta, cs.transpose(0, 2, 1), cs, e_mat]
    out_specs = [row_c(qw), row_p(kw), row_p(kw)]
    out_shape = [jax.ShapeDtypeStruct((nseq, seq_len, qw), BF16),
                 jax.ShapeDtypeStruct((nseq, seq_len, kw), BF16),
                 jax.ShapeDtypeStruct((nseq, seq_len, kw), BF16)]
    if sink is not None:
        in_specs = [pl.BlockSpec(memory_space=pltpu.SMEM)] + in_specs
        args = [sink] + args
        out_specs.append(pl.BlockSpec((8, LANES), lambda s, i: (0, 0)))
        out_shape.append(jax.ShapeDtypeStruct((8, LANES), F32))
    return _pcall(
        body, name=name, grid=(nseq, n + 1), in_specs=in_specs, out_specs=out_specs, out_shape=out_shape,
        scratch_shapes=[pltpu.VMEM((3, t, kw), F32), pltpu.VMEM((3, t, kw), F32),
                        pltpu.VMEM((t + 2 * w, kw), F32), pltpu.VMEM((t + 2 * w, kw), F32)], args=args,
        dims=("arbitrary", "arbitrary"), comm=comm)


def _rms_parts(o, g):
    ms = jnp.mean(o * o, axis=-1, keepdims=True) + LN_EPS
    rinv = lax.rsqrt(ms)
    return o * rinv * g, rinv


def _from_subsequences(ref, scr, dil, t):
    slabs = ref.shape[-1] // LANES
    if dil == 1:
        return ref[0].astype(F32)
    for c in range(dil):
        for sl in range(slabs):
            scr[sl, pl.ds(c, t // dil, stride=dil), :] = ref[c, :, sl * LANES:(sl + 1) * LANES].astype(F32)
    return jnp.concatenate([scr[sl] for sl in range(slabs)], axis=1)


def _to_subsequences(val, ref, scr, dil, t):
    slabs = val.shape[-1] // LANES
    if dil == 1:
        ref[0] = val.astype(ref.dtype)
        return
    for sl in range(slabs):
        scr[sl] = val[:, sl * LANES:(sl + 1) * LANES]
    for c in range(dil):
        for sl in range(slabs):
            ref[c, :, sl * LANES:(sl + 1) * LANES] = scr[sl, pl.ds(c, t // dil, stride=dil), :].astype(ref.dtype)


def _combine_fwd(out_a, o_g, lse_g, g_win, g_dil, w_mix_b, x, ln_in_g, ln_in_b, ln1_g, ln1_b, *, t):
    s = out_a.shape[1]
    wd = DIL_SLOTS * HEAD_DIM

    def body(oa_ref, o0, o1, o2, l0, l1, l2, gw_ref, gd_ref, w_ref, x_ref, g0, b0, g1, b1,
             mixed_ref, ob_ref, lt_ref, r1_ref, h1_ref, scr):
        ls = [l0[...], l1[...], l2[...]]
        mx = jnp.maximum(jnp.maximum(ls[0], ls[1]), ls[2])
        ws = [jnp.exp(l - mx) for l in ls]
        tot = ws[0] + ws[1] + ws[2]
        lt_ref[...] = mx + jnp.log(tot)
        ws = [x / tot for x in ws]
        og = [_from_subsequences(o_ref, scr.at[gi], dil, t)
              for gi, (o_ref, dil) in enumerate(zip((o0, o1, o2), DILATIONS))]
        parts = []
        for h in range(DIL_SLOTS):
            hs = slice(h * HEAD_DIM, (h + 1) * HEAD_DIM)
            parts.append(ws[0][:, h:h + 1] * og[0][:, hs] + ws[1][:, h:h + 1] * og[1][:, hs]
                         + ws[2][:, h:h + 1] * og[2][:, hs])
        ob = jnp.concatenate(parts, axis=1)
        ob_ref[...] = ob
        na, _ = _rms_parts(oa_ref[...], gw_ref[...])
        nb, _ = _rms_parts(ob, gd_ref[...])
        mixed = jnp.concatenate([na.astype(BF16), nb.astype(BF16)], axis=1)
        mixed_ref[...] = mixed
        h0 = _ln(x_ref[...], g0[...], b0[...])
        r1 = ALPHA * h0 + _dot(mixed, w_ref[...])
        r1_ref[...] = r1
        h1_ref[...] = _ln(r1, g1[...], b1[...]).astype(BF16)

    half = pl.BlockSpec((t, wd), lambda i: (i, 0))
    full = pl.BlockSpec((t, D_MODEL), lambda i: (i, 0))
    lanes = pl.BlockSpec((t, LANES), lambda i: (i, 0))
    grow = pl.BlockSpec((1, wd), lambda i: (0, 0))
    row = pl.BlockSpec((1, D_MODEL), lambda i: (0, 0))
    subseq = [pl.BlockSpec((dil, t // dil, wd), lambda i: (0, i, 0)) for dil in DILATIONS]
    return pl.pallas_call(
        body, name="combine_fwd", grid=(s // t,),
        in_specs=[pl.BlockSpec((None, t, wd), lambda i: (0, i, 0))] + subseq
        + [lanes, lanes, lanes, grow, grow, pl.BlockSpec((D_MODEL, D_MODEL), lambda i: (0, 0)), full,
           row, row, row, row],
        out_specs=[full, half, lanes, full, full],
        out_shape=[jax.ShapeDtypeStruct((s, D_MODEL), BF16), jax.ShapeDtypeStruct((s, wd), F32),
                   jax.ShapeDtypeStruct((s, LANES), F32), jax.ShapeDtypeStruct((s, D_MODEL), F32),
                   jax.ShapeDtypeStruct((s, D_MODEL), BF16)],
        scratch_shapes=[pltpu.VMEM((len(DILATIONS), wd // LANES, t, LANES), F32)],
        compiler_params=_cparams(dimension_semantics=("parallel",)),
    )(out_a, *o_g, *lse_g, g_win, g_dil, w_mix_b, x, ln_in_g, ln_in_b, ln1_g, ln1_b)


def _combine_bwd(dr1b, w_mix_b, out_a, out_b, g_win, g_dil, *, t):
    s = out_b.shape[0]
    wd = DIL_SLOTS * HEAD_DIM

    def body(dr_ref, w_ref, oa_ref, ob_ref, gw_ref, gd_ref, doa_ref, dob0, dob1, dob2, dla_ref, dlb_ref, st_ref,
             scr):
        i = pl.program_id(0)
        dm = _dot_nt(dr_ref[...], w_ref[...])

        @pl.when(i == 0)
        def _():
            st_ref[...] = jnp.zeros_like(st_ref)

        lane = lax.broadcasted_iota(jnp.int32, (t, LANES), 1)
        for idx, (o_ref, g_ref, dl_ref) in enumerate(((oa_ref, gw_ref, dla_ref), (ob_ref, gd_ref, dlb_ref))):
            o = o_ref[...]
            dn = dm[:, idx * wd:(idx + 1) * wd]
            _, rinv = _rms_parts(o, g_ref[...])
            wv = dn * g_ref[...]
            do = rinv * wv - o * (rinv * rinv * rinv) * jnp.mean(wv * o, axis=-1, keepdims=True)
            st_ref[idx:idx + 1, :] += jnp.sum(dn * o * rinv, axis=0, keepdims=True)
            if idx == 0:
                doa_ref[...] = do.astype(BF16)
            else:
                for do_ref, dil in zip((dob0, dob1, dob2), DILATIONS):
                    _to_subsequences(do, do_ref, scr, dil, t)
            prod = do * o
            acc = jnp.zeros((t, LANES), F32)
            for h in range(DIL_SLOTS):
                hs = slice(h * HEAD_DIM, (h + 1) * HEAD_DIM)
                acc = jnp.where(lane == h, jnp.sum(prod[:, hs], axis=1, keepdims=True), acc)
            dl_ref[...] = acc

    half = pl.BlockSpec((t, wd), lambda i: (i, 0))
    lanes = pl.BlockSpec((t, LANES), lambda i: (i, 0))
    grow = pl.BlockSpec((1, wd), lambda i: (0, 0))
    a_spec = pl.BlockSpec((None, t, wd), lambda i: (0, i, 0))
    subseq = [pl.BlockSpec((dil, t // dil, wd), lambda i: (0, i, 0)) for dil in DILATIONS]
    doa, dob0, dob1, dob2, dla, dlb, st = pl.pallas_call(
        body, name="combine_bwd", grid=(s // t,),
        in_specs=[pl.BlockSpec((t, D_MODEL), lambda i: (i, 0)), pl.BlockSpec((D_MODEL, D_MODEL), lambda i: (0, 0)),
                  a_spec, half, grow, grow],
        out_specs=[a_spec] + subseq + [lanes, lanes, pl.BlockSpec((8, wd), lambda i: (0, 0))],
        out_shape=[jax.ShapeDtypeStruct((1, s, wd), BF16)]
        + [jax.ShapeDtypeStruct((dil, s // dil, wd), BF16) for dil in DILATIONS]
        + [jax.ShapeDtypeStruct((s, LANES), F32), jax.ShapeDtypeStruct((s, LANES), F32),
           jax.ShapeDtypeStruct((8, wd), F32)],
        scratch_shapes=[pltpu.VMEM((wd // LANES, t, LANES), F32)],
        compiler_params=_cparams(dimension_semantics=("arbitrary",)),
    )(dr1b, w_mix_b, out_a, out_b, g_win, g_dil)
    return doa, [dob0, dob1, dob2], dla, dlb, st


def _assemble_dz(dqa, dka, dva, dqs, dks, dvs, *, t):
    s = dqa.shape[1]
    wd = DIL_SLOTS * HEAD_DIM

    def body(*refs):
        a_refs, g_refs, o_ref, scr = refs[:3], refs[3:12], refs[12], refs[13]
        col = 0
        for r in a_refs:
            o_ref[:, col:col + r.shape[-1]] = r[...]
            col += r.shape[-1]
        for part in range(3):
            for gi, dil in enumerate(DILATIONS):
                val = _from_subsequences(g_refs[3 * part + gi], scr, dil, t)
                o_ref[:, col:col + wd] = val.astype(BF16)
                col += wd

    a_specs = [pl.BlockSpec((None, t, a.shape[-1]), lambda i: (0, i, 0)) for a in (dqa, dka, dva)]
    g_specs = [pl.BlockSpec((dil, t // dil, wd), lambda i: (0, i, 0)) for _ in range(3) for dil in DILATIONS]
    return pl.pallas_call(
        body, name="assemble_dz", grid=(s // t,), in_specs=a_specs + g_specs,
        out_specs=pl.BlockSpec((t, IN_WIDTH), lambda i: (i, 0)),
        out_shape=jax.ShapeDtypeStruct((s, IN_WIDTH), BF16),
        scratch_shapes=[pltpu.VMEM((wd // LANES, t, LANES), F32)],
        compiler_params=_cparams(dimension_semantics=("parallel",)),
    )(dqa, dka, dva, *dqs, *dks, *dvs)


def _mem_fwd(mem, g, b, wk_b, wv_b):
    ml = mem.shape[0]

    def body(mem_ref, g_ref, b_ref, wk_ref, wv_ref, mn_ref, kx_ref, vx_ref):
        mn = _ln(mem_ref[...], g_ref[...], b_ref[...]).astype(BF16)
        mn_ref[...] = mn
        kx_ref[...] = _dot(mn, wk_ref[...]).astype(BF16)
        vx_ref[...] = _dot(mn, wv_ref[...]).astype(BF16)

    sh = jax.ShapeDtypeStruct((ml, D_MODEL), BF16)
    return pl.pallas_call(body, name="mem_fwd", out_shape=[sh, sh, sh], compiler_params=_cparams())(
        mem, g, b, wk_b, wv_b)


def _mem_bwd(dkx, dvx, mem, g, b, wk_b, wv_b):
    def body(dk_ref, dv_ref, mem_ref, g_ref, b_ref, wk_ref, wv_ref, dwk_ref, dwv_ref, st_ref):
        mem_v = mem_ref[...]
        mn = _ln(mem_v, g_ref[...], b_ref[...]).astype(BF16)
        dkb = dk_ref[...].astype(BF16)
        dvb = dv_ref[...].astype(BF16)
        dwk_ref[...] = _dot_tn(mn, dkb)
        dwv_ref[...] = _dot_tn(mn, dvb)
        dmn = _dot_nt(dkb, wk_ref[...]) + _dot_nt(dvb, wv_ref[...])
        _, dg, db = _ln_bwd_math(dmn, mem_v, g_ref[...])
        st_ref[...] = jnp.zeros_like(st_ref)
        st_ref[0:1, :] = dg
        st_ref[1:2, :] = db

    sw = jax.ShapeDtypeStruct((D_MODEL, D_MODEL), F32)
    return pl.pallas_call(body, name="mem_bwd", out_shape=[sw, sw, jax.ShapeDtypeStruct((8, D_MODEL), F32)],
                          compiler_params=_cparams())(dkx, dvx, mem, g, b, wk_b, wv_b)


def _xattn_fwd(h1b, r1, kx, vx, wq_b, wo_b, ln1_g, ln1_b, ln2_g, ln2_b, *, t):
    s = h1b.shape[0]
    scale = X_HEAD_DIM ** -0.5

    def body(h_ref, r1_ref, kx_ref, vx_ref, wq_ref, wo_ref, g1, b1, g2, b2, r2_ref, h2_ref, qx_ref, ox_ref, lse_ref):
        qxb = _dot(h_ref[...], wq_ref[...]).astype(BF16)
        qx_ref[...] = qxb
        lane = lax.broadcasted_iota(jnp.int32, (t, LANES), 1)
        lse_acc = jnp.zeros((t, LANES), F32)
        parts = []
        for h in range(X_HEADS):
            hs = slice(h * X_HEAD_DIM, (h + 1) * X_HEAD_DIM)
            sc = _dot_nt(qxb[:, hs] * scale, kx_ref[:, hs])
            m = jnp.max(sc, axis=1, keepdims=True)
            p = jnp.exp(sc - m)
            den = jnp.sum(p, axis=1, keepdims=True)
            parts.append(_dot(p.astype(BF16), vx_ref[:, hs]) / den)
            lse_acc = jnp.where(lane == h, m + jnp.log(den), lse_acc)
        lse_ref[...] = lse_acc
        oxb = jnp.concatenate(parts, axis=1).astype(BF16)
        ox_ref[...] = oxb
        h1 = _ln(r1_ref[...], g1[...], b1[...])
        r2 = ALPHA * h1 + _dot(oxb, wo_ref[...])
        r2_ref[...] = r2
        h2_ref[...] = _ln(r2, g2[...], b2[...]).astype(BF16)

    tile = pl.BlockSpec((t, D_MODEL), lambda i: (i, 0))
    row = pl.BlockSpec((1, D_MODEL), lambda i: (0, 0))
    full = lambda r: pl.BlockSpec((r, D_MODEL), lambda i: (0, 0))
    ml = kx.shape[0]
    bsh = jax.ShapeDtypeStruct((s, D_MODEL), BF16)
    return pl.pallas_call(
        body, name="xattn_fwd", grid=(s // t,),
        in_specs=[tile, tile, full(ml), full(ml), full(D_MODEL), full(D_MODEL), row, row, row, row],
        out_specs=[tile, tile, tile, tile, pl.BlockSpec((t, LANES), lambda i: (i, 0))],
        out_shape=[jax.ShapeDtypeStruct((s, D_MODEL), F32), bsh, bsh, bsh, jax.ShapeDtypeStruct((s, LANES), F32)],
        compiler_params=_cparams(dimension_semantics=("parallel",)),
    )(h1b, r1, kx, vx, wq_b, wo_b, ln1_g, ln1_b, ln2_g, ln2_b)


def _xattn_bwd(dr2, qxb, oxb, lse, kx, vx, wq_b, wo_b, r1, ln1_g, *, t, comm=None):
    s = dr2.shape[0]
    ml = kx.shape[0]
    scale = X_HEAD_DIM ** -0.5

    def body(dr2_ref, qx_ref, ox_ref, lse_ref, kx_ref, vx_ref, wq_ref, wo_ref, r1_ref, g1_ref,
             dr1_ref, dr1b_ref, dqx_ref, dkx_ref, dvx_ref, st_ref):
        i = pl.program_id(0)

        @pl.when(i == 0)
        def _():
            dkx_ref[...] = jnp.zeros_like(dkx_ref)
            dvx_ref[...] = jnp.zeros_like(dvx_ref)
            st_ref[...] = jnp.zeros_like(st_ref)

        dr2v = dr2_ref[...]
        dox = _dot_nt(dr2v.astype(BF16), wo_ref[...])
        parts = []
        for h in range(X_HEADS):
            hs = slice(h * X_HEAD_DIM, (h + 1) * X_HEAD_DIM)
            doh = dox[:, hs]
            dohb = doh.astype(BF16)
            dl = jnp.sum(doh * ox_ref[:, hs].astype(F32), axis=1, keepdims=True)
            qh = qx_ref[:, hs] * scale
            p = jnp.exp(_dot_nt(qh, kx_ref[:, hs]) - lse_ref[:, h:h + 1])
            dp = _dot_nt(dohb, vx_ref[:, hs])
            dsb = (p * (dp - dl)).astype(BF16)
            parts.append(_dot(dsb, kx_ref[:, hs]) * scale)
            dkx_ref[:, hs] += _dot_tn(dsb, qh)
            dvx_ref[:, hs] += _dot_tn(p.astype(BF16), dohb)
        dqxb = jnp.concatenate(parts, axis=1).astype(BF16)
        dqx_ref[...] = dqxb
        dh1 = _dot_nt(dqxb, wq_ref[...]) + ALPHA * dr2v
        dr1, dg, db = _ln_bwd_math(dh1, r1_ref[...], g1_ref[...])
        dr1_ref[...] = dr1
        dr1b_ref[...] = dr1.astype(BF16)
        st_ref[0:1, :] += dg
        st_ref[1:2, :] += db

    tile = pl.BlockSpec((t, D_MODEL), lambda i: (i, 0))
    full = lambda r: pl.BlockSpec((r, D_MODEL), lambda i: (0, 0))
    bsh = jax.ShapeDtypeStruct((s, D_MODEL), BF16)
    return _pcall(
        body, name="xattn_bwd", grid=(s // t,),
        in_specs=[tile, tile, tile, pl.BlockSpec((t, LANES), lambda i: (i, 0)), full(ml), full(ml),
                  full(D_MODEL), full(D_MODEL), tile, full(1)],
        out_specs=[tile, tile, tile, full(ml), full(ml), full(8)],
        out_shape=[jax.ShapeDtypeStruct((s, D_MODEL), F32), bsh, bsh,
                   jax.ShapeDtypeStruct((ml, D_MODEL), F32), jax.ShapeDtypeStruct((ml, D_MODEL), F32),
                   jax.ShapeDtypeStruct((8, D_MODEL), F32)],
        args=[dr2, qxb, oxb, lse, kx, vx, wq_b, wo_b, r1, ln1_g], dims=("arbitrary",), comm=comm)


def _halo_specs(t, s, width):
    tb8 = t // 8
    return [pl.BlockSpec((t, width), lambda i: (i, 0)),
            pl.BlockSpec((8, width), lambda i: (jnp.maximum(i * tb8 - 1, 0), 0)),
            pl.BlockSpec((8, width), lambda i: (jnp.minimum((i + 1) * tb8, s // 8 - 1), 0))]


def _halo_rows(i, n, prev_ref, next_ref):
    prev_row = jnp.where(i > 0, prev_ref[7:8, :], 0.0)
    next_row = jnp.where(i < n - 1, next_ref[0:1, :], 0.0)
    return prev_row, next_row


def _gelu_parts(gc):
    cdf = 0.5 * (1.0 + lax.erf(gc * (2.0 ** -0.5)))
    pdf = jnp.exp(-0.5 * gc * gc) * (1.0 / math.sqrt(2.0 * math.pi))
    return gc * cdf, cdf + gc * pdf


def _ffn_out(g, u, conv_w, conv_b, w_down_b, r2, target, ln2_g, ln2_b, ln3_g, ln3_b, *, t):
    s = r2.shape[0]
    n = s // t

    def body(g_ref, gp_ref, gn_ref, u_ref, cw_ref, cb_ref, w_ref, r2_ref, tg_ref, g2, b2, g3, b3,
             t_ref, dr_ref, drb_ref, st_ref):
        i = pl.program_id(0)

        @pl.when(i == 0)
        def _():
            st_ref[...] = jnp.zeros_like(st_ref)

        gv = g_ref[...]
        prev_row, next_row = _halo_rows(i, n, gp_ref, gn_ref)
        gm1, gp1 = _shift_rows(gv, prev_row, next_row)
        gc = gm1 * cw_ref[0:1, :] + gv * cw_ref[1:2, :] + gp1 * cw_ref[2:3, :] + cb_ref[...]
        act, _ = _gelu_parts(gc)
        tb = (act * u_ref[...]).astype(BF16)
        t_ref[...] = tb
        h2 = _ln(r2_ref[...], g2[...], b2[...])
        r3 = ALPHA * h2 + _dot(tb, w_ref[...])
        y = _ln(r3, g3[...], b3[...])
        err = y - tg_ref[...]
        loss = 0.5 * jnp.sum(jnp.mean(err * err, axis=-1, keepdims=True))
        dr, dg, db = _ln_bwd_math(err * (1.0 / D_MODEL), r3, g3[...])
        dr_ref[...] = dr
        drb_ref[...] = dr.astype(BF16)
        st_ref[0:1, :] += dg
        st_ref[1:2, :] += db
        st_ref[2:3, :] += jnp.full((1, D_MODEL), loss, F32)

    wide = pl.BlockSpec((t, D_FF), lambda i: (i, 0))
    tile = pl.BlockSpec((t, D_MODEL), lambda i: (i, 0))
    row = pl.BlockSpec((1, D_MODEL), lambda i: (0, 0))
    return pl.pallas_call(
        body, name="ffn_out", grid=(n,),
        in_specs=_halo_specs(t, s, D_FF) + [wide, pl.BlockSpec((3, D_FF), lambda i: (0, 0)),
                                            pl.BlockSpec((1, D_FF), lambda i: (0, 0)),
                                            pl.BlockSpec((D_FF, D_MODEL), lambda i: (0, 0)),
                                            tile, tile, row, row, row, row],
        out_specs=[wide, tile, tile, pl.BlockSpec((8, D_MODEL), lambda i: (0, 0))],
        out_shape=[jax.ShapeDtypeStruct((s, D_FF), BF16), jax.ShapeDtypeStruct((s, D_MODEL), F32),
                   jax.ShapeDtypeStruct((s, D_MODEL), BF16), jax.ShapeDtypeStruct((8, D_MODEL), F32)],
        compiler_params=_cparams(dimension_semantics=("arbitrary",)),
    )(g, g, g, u, conv_w, conv_b, w_down_b, r2, target, ln2_g, ln2_b, ln3_g, ln3_b)


def _dh2_ln2(dgc, conv_w, du, w_gate_b, w_up_b, dr3, r2, ln2_g, *, t, comm=None):
    s = dgc.shape[0]
    n = s // t

    def body(d_ref, dp_ref, dn_ref, cw_ref, du_ref, wg_ref, wu_ref, dr3_ref, r2_ref, g2, dg_ref, dr_ref, drb_ref,
             st_ref):
        i = pl.program_id(0)

        @pl.when(i == 0)
        def _():
            st_ref[...] = jnp.zeros_like(st_ref)

        dv = d_ref[...]
        prev_row, next_row = _halo_rows(i, n, dp_ref, dn_ref)
        dm1, dp1 = _shift_rows(dv, prev_row, next_row)
        dgb = (dp1 * cw_ref[0:1, :] + dv * cw_ref[1:2, :] + dm1 * cw_ref[2:3, :]).astype(BF16)
        dg_ref[...] = dgb
        dh2 = _dot(dgb, wg_ref[...]) + _dot(du_ref[...], wu_ref[...]) + ALPHA * dr3_ref[...]
        dr, dg, db = _ln_bwd_math(dh2, r2_ref[...], g2[...])
        dr_ref[...] = dr
        drb_ref[...] = dr.astype(BF16)
        st_ref[0:1, :] += dg
        st_ref[1:2, :] += db

    wide = pl.BlockSpec((t, D_FF), lambda i: (i, 0))
    tile = pl.BlockSpec((t, D_MODEL), lambda i: (i, 0))
    wfull = pl.BlockSpec((D_FF, D_MODEL), lambda i: (0, 0), pipeline_mode=pl.Buffered(1))
    return _pcall(
        body, name="dh2_ln2", grid=(n,),
        in_specs=_halo_specs(t, s, D_FF) + [pl.BlockSpec((3, D_FF), lambda i: (0, 0)), wide, wfull, wfull,
                                            tile, tile, pl.BlockSpec((1, D_MODEL), lambda i: (0, 0))],
        out_specs=[wide, tile, tile, pl.BlockSpec((8, D_MODEL), lambda i: (0, 0))],
        out_shape=[jax.ShapeDtypeStruct((s, D_FF), BF16), jax.ShapeDtypeStruct((s, D_MODEL), F32),
                   jax.ShapeDtypeStruct((s, D_MODEL), BF16), jax.ShapeDtypeStruct((8, D_MODEL), F32)],
        args=[dgc, dgc, dgc, conv_w, du, w_gate_b, w_up_b, dr3, r2, ln2_g], dims=("arbitrary",), comm=comm)


def _conv_bwd_a(dr3b, w_down_b, g, u, conv_w, conv_b, *, t):
    s = g.shape[0]
    n = s // t

    def body(d_ref, w_ref, g_ref, gp_ref, gn_ref, u_ref, cw_ref, cb_ref, du_ref, dgc_ref, st_ref):
        i = pl.program_id(0)

        @pl.when(i == 0)
        def _():
            st_ref[...] = jnp.zeros_like(st_ref)

        dt = _dot_nt(d_ref[...], w_ref[...])
        gv = g_ref[...]
        prev_row, next_row = _halo_rows(i, n, gp_ref, gn_ref)
        gm1, gp1 = _shift_rows(gv, prev_row, next_row)
        gc = gm1 * cw_ref[0:1, :] + gv * cw_ref[1:2, :] + gp1 * cw_ref[2:3, :] + cb_ref[...]
        act, dact = _gelu_parts(gc)
        du_ref[...] = (dt * act).astype(BF16)
        dgc = dt * u_ref[...] * dact
        dgc_ref[...] = dgc
        st_ref[0:1, :] += jnp.sum(gm1 * dgc, axis=0, keepdims=True)
        st_ref[1:2, :] += jnp.sum(gv * dgc, axis=0, keepdims=True)
        st_ref[2:3, :] += jnp.sum(gp1 * dgc, axis=0, keepdims=True)
        st_ref[3:4, :] += jnp.sum(dgc, axis=0, keepdims=True)

    tile = pl.BlockSpec((t, D_FF), lambda i: (i, 0))
    return pl.pallas_call(
        body, name="conv_bwd_a", grid=(n,),
        in_specs=[pl.BlockSpec((t, D_MODEL), lambda i: (i, 0)), pl.BlockSpec((D_FF, D_MODEL), lambda i: (0, 0))]
        + _halo_specs(t, s, D_FF) + [tile, pl.BlockSpec((3, D_FF), lambda i: (0, 0)),
                                     pl.BlockSpec((1, D_FF), lambda i: (0, 0))],
        out_specs=[tile, tile, pl.BlockSpec((8, D_FF), lambda i: (0, 0))],
        out_shape=[jax.ShapeDtypeStruct((s, D_FF), BF16), jax.ShapeDtypeStruct((s, D_FF), F32),
                   jax.ShapeDtypeStruct((8, D_FF), F32)],
        compiler_params=_cparams(dimension_semantics=("arbitrary",)),
    )(dr3b, w_down_b, g, g, g, u, conv_w, conv_b)


def _to_residue(a, dil):
    s, w = a.shape
    return a.reshape(s // dil, dil, w).transpose(1, 0, 2)


def _stats_to_lanes(rows):
    dil, hq, l = rows.shape
    return jnp.pad(rows.transpose(2, 0, 1).reshape(dil * l, hq), ((0, 0), (0, LANES - hq)))


def _stats_to_rows(lanes, dil):
    s = lanes.shape[0]
    return lanes[:, :DIL_SLOTS].reshape(s // dil, dil, DIL_SLOTS).transpose(1, 2, 0)


def _rope_angles(positions):
    inv_freq = ROPE_THETA ** (-jnp.arange(0, ROT_DIM, 2, dtype=F32) / ROT_DIM)
    ang = positions.astype(F32)[:, None] * inv_freq
    return jnp.concatenate([jnp.cos(ang), jnp.sin(ang)], axis=1)


class _NoPlan:
    def gather(self, stage):
        return None

    def gathered(self, stage, couts, wb):
        pass

    def exchange(self, stage, grads):
        return None

    def exchanged(self, stage, couts):
        pass


def _local_step(x, mem, positions, target, wb, sp, plan=None, *, t_row=256, t_mm=512, tq_a=128, tq_b=128,
                sub_a=4, sub_b=4):
    s = x.shape[0]
    plan = plan or _NoPlan()
    cs = _rope_angles(positions)
    e_mat = _rope_select_matrix()

    h0b, couts = _ln_in_fwd(x, sp["ln_in_g"], sp["ln_in_b"], t=t_mm, comm=plan.gather("ln_in"))
    plan.gathered("ln_in", couts, wb)
    sp = dict(sp, conv_w=wb.get("conv_w", sp.get("conv_w")))
    (za, *zb), couts = _proj_all(h0b, wb["w_in"], cs, e_mat, t=min(2 * t_mm, s), comm=plan.gather("proj"))
    plan.gathered("proj", couts, wb)
    sub_a = max(1, min(sub_a, s // tq_a))
    subs_b = [max(1, min(sub_b, s // dil // tq_b)) for dil in DILATIONS]
    out_a, lse_a, couts = _swa_fwd_p(za, qcol=0, kcol=4, vcol=5, hq=WIN_Q_HEADS, hkv=WIN_KV_HEADS, w=WIN_HALF,
                                     tq=tq_a, sub=sub_a, sink=sp["attn_sink"], name="attn_a_fwd",
                                     comm=plan.gather("attn_a"))
    plan.gathered("attn_a", couts, wb)
    o_g, lse_g = [], []
    for gi in range(3):
        o, l, couts = _swa_fwd_p(zb[gi], qcol=0, kcol=1, vcol=2, hq=DIL_SLOTS, hkv=DIL_SLOTS, w=DIL_HALF, tq=tq_b,
                                 sub=subs_b[gi], sink=None, name=f"attn_b{gi}_fwd",
                                 comm=plan.gather(f"attn_b{gi}"))
        plan.gathered(f"attn_b{gi}", couts, wb)
        o_g.append(o)
        lse_g.append(_stats_to_lanes(l))
    mixed_b, out_b, lse_b, r1, h1b = _combine_fwd(
        out_a, o_g, lse_g, sp["g_win"], sp["g_dil"], wb["w_mix_out"], x, sp["ln_in_g"], sp["ln_in_b"],
        sp["ln1_g"], sp["ln1_b"], t=t_row)
    mem_nb, kx, vx = _mem_fwd(mem, sp["mem_ln_g"], sp["mem_ln_b"], wb["w_xk"], wb["w_xv"])
    r2, h2b, qxb, oxb, lse_x = _xattn_fwd(h1b, r1, kx, vx, wb["w_xq"], wb["w_xo"], sp["ln1_g"], sp["ln1_b"],
                                          sp["ln2_g"], sp["ln2_b"], t=t_mm)
    g = _mm(h2b, wb["w_gate"], mode="nt", out_dtype=F32, tm=t_mm, tn=D_FF, name="ff_gate")
    u = _mm(h2b, wb["w_up"], mode="nt", out_dtype=F32, tm=t_mm, tn=D_FF, name="ff_up")
    tb, dr3, dr3b, st3 = _ffn_out(g, u, sp["conv_w"], sp["conv_b"], wb["w_down"], r2, target, sp["ln2_g"],
                                  sp["ln2_b"], sp["ln3_g"], sp["ln3_b"], t=t_row)

    grads = {}
    du, dgc, st_conv = _conv_bwd_a(dr3b, wb["w_down"], g, u, sp["conv_w"], sp["conv_b"], t=t_row)
    tk = min(2048, s)
    grads["w_down"] = _mm(tb, dr3b, mode="tn", out_dtype=BF16, tm=D_FF // 2, tn=D_MODEL, tk=tk, name="dw_down")
    grads["w_up"] = _mm(du, h2b, mode="tn", out_dtype=BF16, tm=D_FF // 2, tn=D_MODEL, tk=tk, name="dw_up")
    (dg, dr2, dr2b, st2), couts = _dh2_ln2(dgc, sp["conv_w"], du, wb["w_gate"], wb["w_up"], dr3, r2, sp["ln2_g"],
                                           t=t_mm, comm=plan.exchange("dh2", grads))
    plan.exchanged("dh2", couts)
    grads["w_gate"] = _mm(dg, h2b, mode="tn", out_dtype=BF16, tm=D_FF // 2, tn=D_MODEL, tk=tk, name="dw_gate")

    (dr1, dr1b, dqxb, dkx, dvx, st1), couts = _xattn_bwd(
        dr2, qxb, oxb, lse_x, kx, vx, wb["w_xq"], wb["w_xo"], r1, sp["ln1_g"], t=t_mm,
        comm=plan.exchange("xattn", grads))
    plan.exchanged("xattn", couts)
    grads["w_xo"] = _mm(oxb, dr2b, mode="tn", out_dtype=BF16, tm=D_MODEL, tn=D_MODEL, tk=tk, name="dw_xo")
    grads["w_xq"] = _mm(h1b, dqxb, mode="tn", out_dtype=BF16, tm=D_MODEL, tn=D_MODEL, tk=tk, name="dw_xq")
    grads["w_xk"], grads["w_xv"], st_mem = _mem_bwd(dkx, dvx, mem, sp["mem_ln_g"], sp["mem_ln_b"],
                                                    wb["w_xk"], wb["w_xv"])

    grads["w_mix_out"] = _mm(mixed_b, dr1b, mode="tn", out_dtype=BF16, tm=D_MODEL, tn=D_MODEL, tk=tk,
                             name="dw_mix")
    do_a, do_b, dl_a, dl_b, st_mix = _combine_bwd(dr1b, wb["w_mix_out"], out_a, out_b, sp["g_win"], sp["g_dil"],
                                                  t=t_row)
    (dqa, dka, dva, dsink), couts = _swa_bwd_p(
        za, do_a, lse_a, _stats_to_rows(dl_a, 1), cs[None], e_mat, qcol=0, kcol=4, vcol=5, hq=WIN_Q_HEADS,
        hkv=WIN_KV_HEADS, w=WIN_HALF, tq=2 * tq_a, sub=max(1, sub_a // 2), sink=sp["attn_sink"], name="attn_a_bwd",
        comm=plan.exchange("attn_a", grads))
    plan.exchanged("attn_a", couts)
    dqs, dks, dvs = [], [], []
    for gi, dil in enumerate(DILATIONS):
        (dq, dk, dv), couts = _swa_bwd_p(
            zb[gi], do_b[gi], _stats_to_rows(lse_b, dil), _stats_to_rows(dl_b, dil),
            _to_residue(cs, dil), e_mat, qcol=0, kcol=1, vcol=2, hq=DIL_SLOTS, hkv=DIL_SLOTS, w=DIL_HALF, tq=tq_b,
            sub=subs_b[gi], sink=None, name=f"attn_b{gi}_bwd", comm=plan.exchange(f"attn_b{gi}", grads))
        plan.exchanged(f"attn_b{gi}", couts)
        dqs.append(dq)
        dks.append(dk)
        dvs.append(dv)
    dz = _assemble_dz(dqa, dka, dva, dqs, dks, dvs, t=t_row)
    grads["w_in"] = _mm(dz, h0b, mode="tn", out_dtype=BF16, tm=IN_WIDTH // 7, tn=D_MODEL, tk=tk, name="dw_in")
    (grad_x, st0), couts = _dh0_ln_in(dz, wb["w_in"], dr1, x, sp["ln_in_g"], t=t_mm,
                                      comm=plan.exchange("dh0", grads))
    plan.exchanged("dh0", couts)

    small = {
        "loss": st3[2:3, 0:1],
        "ln_in_g": st0[0:1], "ln_in_b": st0[1:2],
        "attn_sink": dsink[:, 0].reshape(1, WIN_Q_HEADS),
        "g_win": st_mix[0:1], "g_dil": st_mix[1:2],
        "ln1_g": st1[0:1], "ln1_b": st1[1:2],
        "mem_ln_g": st_mem[0:1], "mem_ln_b": st_mem[1:2],
        "ln2_g": st2[0:1], "ln2_b": st2[1:2],
        "conv_w": st_conv[0:3], "conv_b": st_conv[3:4],
        "ln3_g": st3[0:1], "ln3_b": st3[1:2],
    }
    return grad_x, grads, small


class _SiblingSwap:
    def __init__(self, arrays):
        self.inputs = list(arrays)
        n = len(arrays)
        self.out_shape = [jax.ShapeDtypeStruct(a.shape, a.dtype) for a in arrays]
        self.scratch = [pltpu.SemaphoreType.DMA((n,)), pltpu.SemaphoreType.DMA((n,))]

    def _copies(self, src, dst, sems):
        send_sems, recv_sems = sems
        x, y, c, _ = _place()
        return [pltpu.make_async_remote_copy(
            src_ref=src[a], dst_ref=dst[a], send_sem=send_sems.at[a], recv_sem=recv_sems.at[a],
            device_id=(x, y, 1 - c), device_id_type=MESH_IDS) for a in range(len(src))]

    def start(self, src, dst, sems):
        for cp in self._copies(src, dst, sems):
            cp.start()

    def wait(self, src, dst, sems):
        copies = self._copies(src, dst, sems)
        for cp in copies:
            cp.wait_recv()
        for cp in copies:
            cp.wait_send()


class _Both:
    def __init__(self, first, second):
        self.parts = (first, second)
        self.inputs = first.inputs + second.inputs
        self.out_shape = first.out_shape + second.out_shape
        self.scratch = first.scratch + second.scratch

    def _split(self, src, dst, sems):
        a = self.parts[0]
        ni, no, ns = len(a.inputs), len(a.out_shape), len(a.scratch)
        return ((src[:ni], dst[:no], sems[:ns]), (src[ni:], dst[no:], sems[ns:]))

    def start(self, src, dst, sems):
        for part, args in zip(self.parts, self._split(src, dst, sems)):
            part.start(*args)

    def wait(self, src, dst, sems):
        for part, args in zip(self.parts, self._split(src, dst, sems)):
            part.wait(*args)


def _row_tile(rows, cols, itemsize=4, budget=1 << 20):
    best = None
    for t in range(16, rows + 1, 16):
        if rows % t == 0 and t * cols * itemsize <= budget:
            best = t
    return best or rows


def _sum_slots(stack, *, name):
    n, r, c = stack.shape
    t = _row_tile(r, c)

    def body(s_ref, o_ref):
        acc = s_ref[0].astype(F32)
        for q in range(1, n):
            acc = acc + s_ref[q].astype(F32)
        o_ref[...] = acc

    return pl.pallas_call(
        body, name=name, grid=(r // t,), in_specs=[pl.BlockSpec((n, t, c), lambda i: (0, i, 0))],
        out_specs=pl.BlockSpec((t, c), lambda i: (i, 0)), out_shape=jax.ShapeDtypeStruct((r, c), F32),
        compiler_params=_cparams(dimension_semantics=("parallel",)),
    )(stack)


def _adamw(w, m, v, p, q, *, name):
    r, c = w.shape
    t = _row_tile(r, c, budget=1 << 20)

    def total(ref):
        if len(ref.shape) == 2:
            return ref[...]
        acc = ref[0].astype(F32)
        for slot in range(1, ref.shape[0]):
            acc = acc + ref[slot].astype(F32)
        return acc

    def body(*refs):
        if q is None:
            w_ref, m_ref, v_ref, p_ref, g_ref, d_ref, nm_ref, nv_ref = refs
            g = total(p_ref)
        else:
            w_ref, m_ref, v_ref, p_ref, q_ref, g_ref, d_ref, nm_ref, nv_ref = refs
            g = total(p_ref) + total(q_ref)
        nm = ADAM_B1 * m_ref[...] + (1.0 - ADAM_B1) * g
        nv = ADAM_B2 * v_ref[...] + (1.0 - ADAM_B2) * (g * g)
        m_hat = nm / (1.0 - ADAM_B1 ** ADAM_STEP)
        v_hat = nv / (1.0 - ADAM_B2 ** ADAM_STEP)
        g_ref[...] = g
        d_ref[...] = -ADAM_LR * (m_hat / (jnp.sqrt(v_hat) + ADAM_EPS) + ADAM_WD * w_ref[...])
        nm_ref[...] = nm
        nv_ref[...] = nv

    tile = pl.BlockSpec((t, c), lambda i: (i, 0))
    args = [w, m, v, p] + ([] if q is None else [q])
    in_specs = [tile if a.ndim == 2 else pl.BlockSpec((a.shape[0], t, c), lambda i: (0, i, 0)) for a in args]
    sh = jax.ShapeDtypeStruct((r, c), F32)
    return pl.pallas_call(
        body, name=name, grid=(r // t,), in_specs=in_specs, out_specs=[tile] * 4, out_shape=[sh] * 4,
        compiler_params=_cparams(dimension_semantics=("parallel",)),
    )(*args)


BIG = ("w_in", "w_mix_out", "w_xq", "w_xk", "w_xv", "w_xo", "w_gate", "w_up", "w_down")
COL_SHARDED = ("w_in", "w_gate", "w_up")
WEIGHTS = ("ln_in_g", "ln_in_b", "w_in", "attn_sink", "g_win", "g_dil", "w_mix_out", "ln1_g", "ln1_b",
           "mem_ln_g", "mem_ln_b", "w_xq", "w_xk", "w_xv", "w_xo", "ln2_g", "ln2_b", "w_gate", "w_up",
           "conv_w", "conv_b", "w_down", "ln3_g", "ln3_b")
SMALL = tuple(k for k in WEIGHTS if k not in BIG)
PACK_COLS = 1024
CONV_SHARD = D_FF // N_CHIPS
CONV_WIDTH_ROWS = 3
SMALL_ROWS = 32


GATHER_STAGES = {"ln_in": ("w_in", "conv_w"), "proj": ("w_mix_out", "w_xq", "w_xk", "w_xv", "w_xo", "w_up"),
                 "attn_a": ("w_gate",), "attn_b0": ("w_down",)}
EXCHANGE_STAGES = {"dh2": ("w_down",), "xattn": ("w_up",), "attn_a": ("w_gate", "w_xo", "w_xq"),
                   "attn_b0": ("w_xk", "w_xv", "w_mix_out"), "dh0": ("w_in",)}


def _full_weight(k, g4):
    return g4.reshape(N_CHIPS * g4.shape[1], g4.shape[2])


def _grad_parts(k, gk):
    gk = gk.astype(BF16)
    return gk.reshape(N_CHIPS, gk.shape[0] // N_CHIPS, gk.shape[1])


EARLY_SWAP_STAGE = "attn_b2"


class _Plan:
    def __init__(self, shards):
        self.shards = shards
        self.recv = {}
        self.chip_sums = {}
        self.sibling_sums = {}

    def gather(self, stage):
        names = GATHER_STAGES.get(stage)
        return _ChipGather([self.shards[k] for k in names]) if names else None

    def gathered(self, stage, couts, wb):
        for k, g4 in zip(GATHER_STAGES.get(stage, ()), couts):
            if k == "conv_w":
                taps = g4[:, :CONV_WIDTH_ROWS, :CONV_SHARD]
                wb[k] = taps.transpose(1, 0, 2).reshape(CONV_WIDTH_ROWS, D_FF)
            else:
                wb[k] = _full_weight(k, g4)

    def exchange(self, stage, grads):
        if stage == EARLY_SWAP_STAGE:
            self.early = [k for k in BIG if k in self.recv]
            for k in self.early:
                self.chip_sums[k] = self.recv[k]
            return _SiblingSwap([self.chip_sums[k] for k in self.early])
        names = EXCHANGE_STAGES.get(stage)
        return _ChipExchange([_grad_parts(k, grads[k]) for k in names]) if names else None

    def exchanged(self, stage, couts):
        if stage == EARLY_SWAP_STAGE:
            self.sibling_sums.update(zip(self.early, couts))
            return
        for k, r4 in zip(EXCHANGE_STAGES.get(stage, ()), couts):
            self.recv[k] = r4


def _pack_rows(a):
    r, n = a.shape
    per = -(-n // PACK_COLS)
    return jnp.pad(a, ((0, 0), (0, per * PACK_COLS - n))).reshape(r * per, PACK_COLS)


def _unpack_rows(p, r, n):
    per = -(-n // PACK_COLS)
    return p.reshape(r, per * PACK_COLS)[:, :n]


def _pack(pieces, rows_total):
    cat = jnp.concatenate([_pack_rows(a) for a in pieces], axis=0)
    return jnp.pad(cat, ((0, rows_total - cat.shape[0]), (0, 0)))


def _unpack(p, shapes):
    out, at = [], 0
    for r, n in shapes:
        per = -(-n // PACK_COLS)
        out.append(_unpack_rows(p[at:at + r * per], r, n))
        at += r * per
    return out


def kernel(x, mem, positions, ln_in_g, ln_in_b, w_in, attn_sink, g_win, g_dil, w_mix_out, ln1_g, ln1_b, mem_ln_g, mem_ln_b, w_xq, w_xk, w_xv, w_xo, ln2_g, ln2_b, w_gate, w_up, conv_w, conv_b, w_down, ln3_g, ln3_b, loss_target, m_ln_in_g, m_ln_in_b, m_w_in, m_attn_sink, m_g_win, m_g_dil, m_w_mix_out, m_ln1_g, m_ln1_b, m_mem_ln_g, m_mem_ln_b, m_w_xq, m_w_xk, m_w_xv, m_w_xo, m_ln2_g, m_ln2_b, m_w_gate, m_w_up, m_conv_w, m_conv_b, m_w_down, m_ln3_g, m_ln3_b, v_ln_in_g, v_ln_in_b, v_w_in, v_attn_sink, v_g_win, v_g_dil, v_w_mix_out, v_ln1_g, v_ln1_b, v_mem_ln_g, v_mem_ln_b, v_w_xq, v_w_xk, v_w_xv, v_w_xo, v_ln2_g, v_ln2_b, v_w_gate, v_w_up, v_conv_w, v_conv_b, v_w_down, v_ln3_g, v_ln3_b):
    given = dict(locals())
    shape_of = {k: given[k].shape for k in WEIGHTS}
    as2d = lambda k, a: a.reshape(-1, a.shape[-1]).T if k in COL_SHARDED else a.reshape(-1, a.shape[-1])
    w2 = {k: as2d(k, given[k]) for k in WEIGHTS}
    m2 = {k: as2d(k, given["m_" + k]) for k in WEIGHTS}
    v2 = {k: as2d(k, given["v_" + k]) for k in WEIGHTS}
    chip = 2 * lax.axis_index("x") + lax.axis_index("y")

    shards = {k: w2[k].astype(BF16) for k in BIG}
    shards["conv_w"] = jnp.pad(w2["conv_w"], ((0, 16 - CONV_WIDTH_ROWS), (0, PACK_COLS - CONV_SHARD)))
    plan = _Plan(shards)
    sp = {k: w2[k] for k in SMALL if k != "conv_w"}

    grad_x, grads, small = _local_step(x[0], mem[0], positions[0], loss_target[0], {}, sp, plan)

    small_keys = ("loss",) + SMALL
    small_shapes = [small[k].shape for k in small_keys]
    small_pack = _pack([small[k] for k in small_keys], SMALL_ROWS)
    late = [k for k in BIG if k not in plan.chip_sums]
    for k in late:
        plan.chip_sums[k] = _sum_slots(plan.recv[k], name=f"sum_chips_{k}")
    *late_sibling, small_all = _comm_only(
        _Both(_SiblingSwap([plan.chip_sums[k] for k in late]), _ChipExchange([], small_pack)), "swap_and_small")
    plan.sibling_sums.update(zip(late, late_sibling))
    chip_sums = [plan.chip_sums[k] for k in BIG]
    sibling_sums = [plan.sibling_sums[k] for k in BIG]
    small_sum = _sum_slots(small_all, name="sum_small")
    small_g = dict(zip(small_keys, _unpack(small_sum, small_shapes)))
    loss = small_g["loss"][0, 0]

    res = {}
    for k, p, q in zip(BIG, chip_sums, sibling_sums):
        res[k] = _adamw(w2[k], m2[k], v2[k], p, q, name=f"adamw_{k}")
    small_g["conv_w"] = lax.dynamic_slice_in_dim(small_g["conv_w"], chip * CONV_SHARD, CONV_SHARD, axis=1)
    adam_shapes = [w2[k].shape for k in SMALL]
    packs = [_pack([d[k] for k in SMALL], SMALL_ROWS) for d in (w2, m2, v2, small_g)]
    small_res = [_unpack(o, adam_shapes) for o in _adamw(*packs, None, name="adamw_small")]
    for i, k in enumerate(SMALL):
        res[k] = tuple(o[i] for o in small_res)

    outs = [loss, grad_x[None]]
    for slot in range(4):
        outs += [(res[k][slot].T if k in COL_SHARDED else res[k][slot]).reshape(shape_of[k]) for k in WEIGHTS]
    return tuple(outs)
```

```python
import functools
import math

import jax
import jax.numpy as jnp
from jax import lax
from jax.experimental import pallas as pl
from jax.experimental.pallas import tpu as pltpu

F32 = jnp.float32
BF16 = jnp.bfloat16

D_MODEL = 1024
HEAD_DIM = 64
WIN_Q_HEADS = 8
WIN_KV_HEADS = 2
WIN_HALF = 128
DIL_SLOTS = 8
DILATIONS = (1, 4, 16)
DIL_HALF = 64
ROT_DIM = 16
ROPE_THETA = 500000.0
X_HEADS = 4
X_HEAD_DIM = 256
D_FF = 2816
A_Q = 512
A_KV = 128
A_WIDTH = A_Q + 2 * A_KV
B_QKV = 1536
IN_WIDTH = 5376
ALPHA = 2.0 ** 0.25
LN_EPS = 1e-5
NEG_INF = -1e30
LANES = 128
N_CHIPS = 4
N_DEV = 8

ADAM_LR = 0.001
ADAM_B1 = 0.9
ADAM_B2 = 0.999
ADAM_EPS = 1e-08
ADAM_WD = 0.01
ADAM_STEP = 10

VMEM_LIMIT = 56 * 1024 * 1024


def _cparams(**kw):
    return pltpu.CompilerParams(vmem_limit_bytes=VMEM_LIMIT, **kw)


def _dot(a, b):
    return lax.dot_general(a, b, (((1,), (0,)), ((), ())), preferred_element_type=F32)


def _dot_nt(a, b):
    return lax.dot_general(a, b, (((1,), (1,)), ((), ())), preferred_element_type=F32)


def _dot_tn(a, b):
    return lax.dot_general(a, b, (((0,), (0,)), ((), ())), preferred_element_type=F32)


def _ln(x, g, b):
    mu = jnp.mean(x, axis=-1, keepdims=True)
    xc = x - mu
    var = jnp.mean(xc * xc, axis=-1, keepdims=True)
    return xc * lax.rsqrt(var + LN_EPS) * g + b


def _ln_bwd_math(dy, r, g):
    mu = jnp.mean(r, axis=-1, keepdims=True)
    xc = r - mu
    var = jnp.mean(xc * xc, axis=-1, keepdims=True)
    rstd = lax.rsqrt(var + LN_EPS)
    xhat = xc * rstd
    dxhat = dy * g
    m1 = jnp.mean(dxhat, axis=-1, keepdims=True)
    m2 = jnp.mean(dxhat * xhat, axis=-1, keepdims=True)
    dr = rstd * (dxhat - m1 - xhat * m2)
    return dr, jnp.sum(dy * xhat, axis=0, keepdims=True), jnp.sum(dy, axis=0, keepdims=True)


def _rope(z, ta, tb, tc, sign):
    w = z.shape[1]
    reps = w // LANES
    a = jnp.tile(ta, (1, reps))
    b = jnp.tile(tb, (1, reps))
    c = jnp.tile(tc, (1, reps))
    return z * a + sign * (pltpu.roll(z, w - 8, 1) * b + pltpu.roll(z, 8, 1) * c)


def _shift_rows(x, prev_row, next_row):
    t = x.shape[0]
    sub = 8
    row = lax.broadcasted_iota(jnp.int32, (sub, x.shape[1]), 0)
    down, up = pltpu.roll(x, 1, 0), pltpu.roll(x, t - 1, 0)
    xm1 = jnp.concatenate([jnp.where(row == 0, prev_row, down[:sub]), down[sub:]], axis=0)
    xp1 = jnp.concatenate([up[:t - sub], jnp.where(row == sub - 1, next_row, up[t - sub:])], axis=0)
    return xm1, xp1


def _rope_tabs(cs, e_mat):
    hi = cs.astype(BF16)
    rest = cs - hi.astype(F32)
    mid = rest.astype(BF16)
    lo = (rest - mid.astype(F32)).astype(BF16)
    tabs = _dot(hi, e_mat) + _dot(mid, e_mat) + _dot(lo, e_mat)
    lane = lax.broadcasted_iota(jnp.int32, (cs.shape[0], LANES), 1)
    ones = jnp.where((lane & (HEAD_DIM - 1)) >= ROT_DIM, 1.0, 0.0)
    return tabs[:, :LANES] + ones, tabs[:, LANES:2 * LANES], tabs[:, 2 * LANES:]


def _rope_select_matrix():
    half = ROT_DIM // 2
    e = [[0.0] * (3 * LANES) for _ in range(ROT_DIM)]
    for lane in range(LANES):
        d = lane % HEAD_DIM
        if d < half:
            e[d][lane] = 1.0
            e[half + d][LANES + lane] = -1.0
        elif d < ROT_DIM:
            e[d - half][lane] = 1.0
            e[d][2 * LANES + lane] = 1.0
    return jnp.array(e, BF16)


def _rope_rows(x, cos_t, sin_t, sign):
    half = ROT_DIM // 2
    parts = []
    for base in (0, HEAD_DIM):
        r1, r2 = x[base:base + half], x[base + half:base + ROT_DIM]
        parts += [r1 * cos_t - sign * (r2 * sin_t), r2 * cos_t + sign * (r1 * sin_t), x[base + ROT_DIM:base + HEAD_DIM]]
    return jnp.concatenate(parts, axis=0)


MESH_IDS = pl.DeviceIdType.MESH
ANY = pl.BlockSpec(memory_space=pl.ANY)


def _place():
    x, y, c = lax.axis_index("x"), lax.axis_index("y"), lax.axis_index("c")
    other_chips = [(1 - x, y), (x, 1 - y), (1 - x, 1 - y)]
    return x, y, c, other_chips


class _ChipGather:
    def __init__(self, shards):
        self.inputs = list(shards)
        n = len(shards)
        self.out_shape = [jax.ShapeDtypeStruct((N_CHIPS,) + a.shape, a.dtype) for a in shards]
        self.scratch = [pltpu.SemaphoreType.DMA((6 * n,)), pltpu.SemaphoreType.DMA((6 * n,)),
                        pltpu.SemaphoreType.DMA((n,))]

    def _copies(self, src, dst, sems):
        send_sems, recv_sems, local_sems = sems
        x, y, c, chips = _place()
        mine = 2 * x + y
        n = len(src)
        local, sends, recvs, passes, pass_recvs = [], [], [], [], []
        for a in range(n):
            half = src[a].shape[0] // 2
            my_rows, other_rows = pl.ds(c * half, half), pl.ds((1 - c) * half, half)
            local.append(pltpu.make_async_copy(src[a], dst[a].at[mine], local_sems.at[a]))
            for j, (px, py) in enumerate(chips):
                k, k2, slot = 3 * a + j, 3 * n + 3 * a + j, 2 * px + py
                sends.append(pltpu.make_async_remote_copy(
                    src_ref=src[a].at[my_rows], dst_ref=dst[a].at[mine, my_rows], send_sem=send_sems.at[k],
                    recv_sem=recv_sems.at[k], device_id=(px, py, c), device_id_type=MESH_IDS))
                recvs.append(pltpu.make_async_remote_copy(
                    src_ref=src[a].at[my_rows], dst_ref=dst[a].at[slot, my_rows], send_sem=send_sems.at[k],
                    recv_sem=recv_sems.at[k], device_id=(px, py, c), device_id_type=MESH_IDS))
                passes.append(pltpu.make_async_remote_copy(
                    src_ref=dst[a].at[slot, my_rows], dst_ref=dst[a].at[slot, my_rows], send_sem=send_sems.at[k2],
                    recv_sem=recv_sems.at[k2], device_id=(x, y, 1 - c), device_id_type=MESH_IDS))
                pass_recvs.append(pltpu.make_async_remote_copy(
                    src_ref=dst[a].at[slot, my_rows], dst_ref=dst[a].at[slot, other_rows],
                    send_sem=send_sems.at[k2], recv_sem=recv_sems.at[k2], device_id=(x, y, 1 - c),
                    device_id_type=MESH_IDS))
        return local, sends, recvs, passes, pass_recvs

    def start(self, src, dst, sems):
        local, sends, _, _, _ = self._copies(src, dst, sems)
        for cp in local + sends:
            cp.start()

    def wait(self, src, dst, sems):
        local, sends, recvs, passes, pass_recvs = self._copies(src, dst, sems)
        for idx, landed in enumerate(recvs):
            landed.wait_recv()
            if passes:
                passes[idx].start()
        for cp in pass_recvs:
            cp.wait_recv()
        for cp in sends + passes:
            cp.wait_send()
        for cp in local:
            cp.wait()


class _ChipExchange:
    def __init__(self, parts, small=None):
        self.inputs = list(parts) + ([small] if small is not None else [])
        self.n = len(parts)
        self.has_small = small is not None
        self.out_shape = [jax.ShapeDtypeStruct(a.shape, a.dtype) for a in parts]
        n_sem, n_loc = 3 * self.n, self.n
        if self.has_small:
            self.out_shape.append(jax.ShapeDtypeStruct((N_DEV,) + small.shape, small.dtype))
            n_sem, n_loc = n_sem + N_DEV - 1, n_loc + 1
        self.scratch = [pltpu.SemaphoreType.DMA((n_sem,)), pltpu.SemaphoreType.DMA((n_sem,)),
                        pltpu.SemaphoreType.DMA((n_loc,))]

    def _copies(self, src, dst, sems):
        send_sems, recv_sems, local_sems = sems
        x, y, c, chips = _place()
        mine = 2 * x + y
        n = self.n
        local, sends, recvs = [], [], []
        for a in range(n):
            local.append(pltpu.make_async_copy(src[a].at[mine], dst[a].at[mine], local_sems.at[a]))
            for j, (px, py) in enumerate(chips):
                k = 3 * a + j
                sends.append(pltpu.make_async_remote_copy(
                    src_ref=src[a].at[2 * px + py], dst_ref=dst[a].at[mine], send_sem=send_sems.at[k],
                    recv_sem=recv_sems.at[k], device_id=(px, py, c), device_id_type=MESH_IDS))
                recvs.append(pltpu.make_async_remote_copy(
                    src_ref=src[a].at[mine], dst_ref=dst[a].at[2 * px + py], send_sem=send_sems.at[k],
                    recv_sem=recv_sems.at[k], device_id=(px, py, c), device_id_type=MESH_IDS))
        if self.has_small:
            me_dev = 4 * x + 2 * y + c
            local.append(pltpu.make_async_copy(src[n], dst[n].at[me_dev], local_sems.at[n]))
            for mask in range(1, N_DEV):
                px, py, pc = x ^ ((mask >> 2) & 1), y ^ ((mask >> 1) & 1), c ^ (mask & 1)
                k = 3 * n + mask - 1
                sends.append(pltpu.make_async_remote_copy(
                    src_ref=src[n], dst_ref=dst[n].at[me_dev], send_sem=send_sems.at[k], recv_sem=recv_sems.at[k],
                    device_id=(px, py, pc), device_id_type=MESH_IDS))
                recvs.append(pltpu.make_async_remote_copy(
                    src_ref=src[n], dst_ref=dst[n].at[4 * px + 2 * py + pc], send_sem=send_sems.at[k],
                    recv_sem=recv_sems.at[k], device_id=(px, py, pc), device_id_type=MESH_IDS))
        return local, sends, recvs, [], []

    start = _ChipGather.start
    wait = _ChipGather.wait


def _pcall(body, *, name, grid, in_specs, out_specs, out_shape, args, scratch_shapes=(), dims=None, comm=None):
    in_specs, out_specs, out_shape = list(in_specs), list(out_specs), list(out_shape)
    scratch_shapes = list(scratch_shapes)
    if comm is None:
        outs = pl.pallas_call(
            body, name=name, grid=grid, in_specs=in_specs, out_specs=out_specs, out_shape=out_shape,
            scratch_shapes=scratch_shapes, compiler_params=_cparams(dimension_semantics=dims),
        )(*args)
        return list(outs), []
    n_in, n_out, n_scr = len(in_specs), len(out_specs), len(scratch_shapes)
    n_cin, n_cout = len(comm.inputs), len(comm.out_shape)

    def wrapped(*refs):
        ins, refs = refs[:n_in], refs[n_in:]
        cins, refs = refs[:n_cin], refs[n_cin:]
        outs, refs = refs[:n_out], refs[n_out:]
        couts, refs = refs[:n_cout], refs[n_cout:]
        scr, csems = refs[:n_scr], refs[n_scr:]
        first = last = None
        for axis, size in enumerate(grid):
            pid = pl.program_id(axis)
            f, l = pid == 0, pid == size - 1
            first = f if first is None else first & f
            last = l if last is None else last & l

        @pl.when(first)
        def _():
            comm.start(cins, couts, csems)

        body(*ins, *outs, *scr)

        @pl.when(last)
        def _():
            comm.wait(cins, couts, csems)

    res = pl.pallas_call(
        wrapped, name=name, grid=grid, in_specs=in_specs + [ANY] * n_cin, out_specs=out_specs + [ANY] * n_cout,
        out_shape=out_shape + list(comm.out_shape), scratch_shapes=scratch_shapes + list(comm.scratch),
        compiler_params=_cparams(dimension_semantics=("arbitrary",) * len(grid)),
    )(*args, *comm.inputs)
    return list(res[:n_out]), list(res[n_out:])


def _comm_only(comm, name):
    def body(*refs):
        n_cin, n_cout = len(comm.inputs), len(comm.out_shape)
        cins, couts, csems = refs[:n_cin], refs[n_cin:n_cin + n_cout], refs[n_cin + n_cout:]
        comm.start(cins, couts, csems)
        comm.wait(cins, couts, csems)

    return list(pl.pallas_call(
        body, name=name, in_specs=[ANY] * len(comm.inputs), out_specs=[ANY] * len(comm.out_shape),
        out_shape=list(comm.out_shape), scratch_shapes=list(comm.scratch),
    )(*comm.inputs))


def _mm(a, b, *, mode, out_dtype, tm, tn, tk=None, name):
    if mode == "nt":
        m, k = a.shape
        n = b.shape[0]
        assert m % tm == 0 and n % tn == 0

        def body(a_ref, b_ref, o_ref):
            o_ref[...] = _dot_nt(a_ref[...], b_ref[...]).astype(out_dtype)

        return pl.pallas_call(
            body, name=name, grid=(m // tm, n // tn),
            in_specs=[pl.BlockSpec((tm, k), lambda i, j: (i, 0)), pl.BlockSpec((tn, k), lambda i, j: (j, 0))],
            out_specs=pl.BlockSpec((tm, tn), lambda i, j: (i, j)),
            out_shape=jax.ShapeDtypeStruct((m, n), out_dtype),
            compiler_params=_cparams(dimension_semantics=("parallel", "parallel")),
        )(a, b)
    assert mode == "tn"
    kk, m = a.shape
    n = b.shape[1]
    assert m % tm == 0 and n % tn == 0 and kk % tk == 0
    nk = kk // tk

    def body(a_ref, b_ref, o_ref, acc_ref):
        kstep = pl.program_id(2)

        @pl.when(kstep == 0)
        def _():
            acc_ref[...] = jnp.zeros_like(acc_ref)

        acc_ref[...] += _dot_tn(a_ref[...], b_ref[...])

        @pl.when(kstep == nk - 1)
        def _():
            o_ref[...] = acc_ref[...].astype(out_dtype)

    return pl.pallas_call(
        body, name=name, grid=(m // tm, n // tn, nk),
        in_specs=[pl.BlockSpec((tk, tm), lambda i, j, s: (s, i)), pl.BlockSpec((tk, tn), lambda i, j, s: (s, j))],
        out_specs=pl.BlockSpec((tm, tn), lambda i, j, s: (i, j)),
        out_shape=jax.ShapeDtypeStruct((m, n), out_dtype),
        scratch_shapes=[pltpu.VMEM((tm, tn), F32)],
        compiler_params=_cparams(dimension_semantics=("parallel", "parallel", "arbitrary")),
    )(a, b)


PROJ_COLS = 256


def _proj_segments():
    wd = DIL_SLOTS * HEAD_DIM
    segs = [(1, [(0, 1), (PROJ_COLS, 1), (2 * PROJ_COLS, 2)])]
    for gi, dil in enumerate(DILATIONS):
        blocks = []
        for part, kind in enumerate((1, 1, 0)):
            col = A_WIDTH + part * B_QKV + gi * wd
            blocks += [(col, kind), (col + PROJ_COLS, kind)]
        segs.append((dil, blocks))
    return segs


PROJ_SEGMENTS = _proj_segments()


def _dh0_ln_in(dz, w_t, dr1, x, ln_in_g, *, t, comm=None):
    s, k = dz.shape

    def body(dz_ref, w_ref, dr1_ref, x_ref, g_ref, gx_ref, st_ref):
        i = pl.program_id(0)

        @pl.when(i == 0)
        def _():
            st_ref[...] = jnp.zeros_like(st_ref)

        dh0 = _dot(dz_ref[...], w_ref[...]) + ALPHA * dr1_ref[...]
        dx, dg, db = _ln_bwd_math(dh0, x_ref[...], g_ref[...])
        gx_ref[...] = dx
        st_ref[0:1, :] += dg
        st_ref[1:2, :] += db

    tile = pl.BlockSpec((t, D_MODEL), lambda i: (i, 0))
    return _pcall(
        body, name="dh0_ln_in", grid=(s // t,),
        in_specs=[pl.BlockSpec((t, k), lambda i: (i, 0)),
                  pl.BlockSpec((k, D_MODEL), lambda i: (0, 0), pipeline_mode=pl.Buffered(1)),
                  tile, tile, pl.BlockSpec((1, D_MODEL), lambda i: (0, 0))],
        out_specs=[tile, pl.BlockSpec((8, D_MODEL), lambda i: (0, 0))],
        out_shape=[jax.ShapeDtypeStruct((s, D_MODEL), F32), jax.ShapeDtypeStruct((8, D_MODEL), F32)],
        args=[dz, w_t, dr1, x, ln_in_g], dims=("arbitrary",), comm=comm)


def _ln_in_fwd(x, g, b, *, t, comm=None):
    s = x.shape[0]

    def body(x_ref, g_ref, b_ref, o_ref):
        o_ref[...] = _ln(x_ref[...], g_ref[...], b_ref[...]).astype(BF16)

    row = pl.BlockSpec((1, D_MODEL), lambda i: (0, 0))
    tile = pl.BlockSpec((t, D_MODEL), lambda i: (i, 0))
    outs, couts = _pcall(body, name="ln_in_fwd", grid=(s // t,), in_specs=[tile, row, row], out_specs=[tile],
                         out_shape=[jax.ShapeDtypeStruct((s, D_MODEL), BF16)], args=[x, g, b], dims=("parallel",),
                         comm=comm)
    return outs[0], couts


def _proj_all(h0b, w_t, cs, e_mat, *, t, comm=None):
    s = h0b.shape[0]
    cb = PROJ_COLS
    halves = cb // LANES

    def body(h_ref, w_ref, cs_ref, e_ref, *rest):
        z_refs, scr = rest[:-1], rest[-1]
        h = h_ref[...]
        ta, tb, tc = (jnp.tile(tab, (1, halves)) for tab in _rope_tabs(cs_ref[...], e_ref[...]))
        lane = lax.broadcasted_iota(jnp.int32, (t, cb), 1)
        slot = 0
        for z_ref, (dil, blocks) in zip(z_refs, PROJ_SEGMENTS):
            for jb, (col, kind) in enumerate(blocks):
                acc = _dot_nt(h, w_ref[col:col + cb, :])
                if kind:
                    z = acc * ta + (pltpu.roll(acc, cb - 8, 1) * tb + pltpu.roll(acc, 8, 1) * tc)
                    if kind == 2:
                        z = jnp.where(lane < LANES, z, acc)
                else:
                    z = acc
                if dil == 1:
                    z_ref[0, :, cb * jb:cb * (jb + 1)] = z.astype(BF16)
                    continue
                for half in range(halves):
                    scr[slot, half] = z[:, half * LANES:(half + 1) * LANES]
                for c in range(dil):
                    for half in range(halves):
                        rows = scr[slot, half, pl.ds(c, t // dil, stride=dil), :]
                        z_ref[c, :, cb * jb + half * LANES:cb * jb + (half + 1) * LANES] = rows.astype(BF16)
                slot = 1 - slot

    widths = [cb * len(blocks) for _, blocks in PROJ_SEGMENTS]
    dils = [dil for dil, _ in PROJ_SEGMENTS]
    outs, couts = _pcall(
        body, name="proj_all", grid=(s // t,),
        in_specs=[pl.BlockSpec((t, D_MODEL), lambda i: (i, 0)),
                  pl.BlockSpec((IN_WIDTH, D_MODEL), lambda i: (0, 0), pipeline_mode=pl.Buffered(1)),
                  pl.BlockSpec((t, ROT_DIM), lambda i: (i, 0)), pl.BlockSpec((ROT_DIM, 3 * LANES), lambda i: (0, 0))],
        out_specs=[pl.BlockSpec((dil, t // dil, wd), lambda i: (0, i, 0)) for dil, wd in zip(dils, widths)],
        out_shape=[jax.ShapeDtypeStruct((dil, s // dil, wd), BF16) for dil, wd in zip(dils, widths)],
        args=[h0b, w_t, cs, e_mat], scratch_shapes=[pltpu.VMEM((2, halves, t, LANES), F32)],
        dims=("parallel",), comm=comm)
    return outs, couts


PAIR = 2 * HEAD_DIM


def _place_head(x2, src_pos, dst_pos):
    hi = lax.broadcasted_iota(jnp.int32, x2.shape, 1) >= HEAD_DIM
    src = x2 if src_pos == dst_pos else pltpu.roll(x2, HEAD_DIM, 1)
    return jnp.where(hi == (dst_pos == 1), src, jnp.zeros_like(src))


def _band_mask_t(row0, tq, w, seq_len):
    tk = tq + 2 * w
    kk = lax.broadcasted_iota(jnp.int32, (tk, tq), 0)
    qq = lax.broadcasted_iota(jnp.int32, (tk, tq), 1)
    kpos = row0 - w + kk
    return (jnp.abs(qq + w - kk) <= w) & (kpos >= 0) & (kpos < seq_len)


def _halo_kv_specs(t, w, hkv, n, seq_len, kcol, vcol):
    kw = hkv * HEAD_DIM
    per, last = t // w, seq_len // w - 1
    cur = lambda s, i: jnp.minimum(i, n - 1)
    specs = []
    for c in (kcol, vcol):
        specs += [pl.BlockSpec((None, w, kw), lambda s, i, c=c: (s, jnp.maximum(cur(s, i) * per - 1, 0), c)),
                  pl.BlockSpec((None, t, kw), lambda s, i, c=c: (s, cur(s, i), c)),
                  pl.BlockSpec((None, w, kw), lambda s, i, c=c: (s, jnp.minimum((cur(s, i) + 1) * per, last), c))]
    return specs, cur


def _pair_kv(kfull, vfull, qp, rep, krows):
    ks, vs, a_of = [], [], []
    for pos in range(2):
        g = (2 * qp + pos) // rep
        a_of.append(g // 2)
        ks.append(_place_head(kfull[g // 2][krows], g % 2, pos))
        vs.append(_place_head(vfull[g // 2][krows], g % 2, pos))
    assert a_of[0] == a_of[1]
    return jnp.concatenate(ks, axis=0), jnp.concatenate(vs, axis=0), a_of[0]


def _swa_fwd_p(qkv, *, qcol, kcol, vcol, hq, hkv, w, tq, sub, sink, name, comm=None):
    nseq, seq_len, _ = qkv.shape
    t = tq * sub
    n = seq_len // t
    rep = hq // hkv
    tk = tq + 2 * w
    kv_specs, cur = _halo_kv_specs(t, w, hkv, n, seq_len, kcol, vcol)

    def body(*refs):
        if sink is not None:
            sink_ref, refs = refs[0], refs[1:]
        q_ref, kp_ref, kc_ref, kn_ref, vp_ref, vc_ref, vn_ref, o_ref, lse_ref = refs
        i = pl.program_id(1)
        kfull, vfull = [], []
        for a in range(hkv // 2):
            ls = slice(a * PAIR, (a + 1) * PAIR)
            kfull.append(jnp.concatenate([kp_ref[:, ls], kc_ref[:, ls], kn_ref[:, ls]], axis=0) * 0.125)
            vfull.append(jnp.concatenate([vp_ref[:, ls], vc_ref[:, ls], vn_ref[:, ls]], axis=0))
        row_hi = lax.broadcasted_iota(jnp.int32, (PAIR, tq), 0) >= HEAD_DIM
        for jj in range(sub):
            rows = slice(jj * tq, (jj + 1) * tq)
            mask_t = _band_mask_t(i * t + jj * tq, tq, w, seq_len)
            o_t, lse_rows = [], []
            for qp in range(hq // 2):
                kst, vst, _ = _pair_kv(kfull, vfull, qp, rep, slice(jj * tq, jj * tq + tk))
                s2 = _dot_nt(kst, q_ref[rows, qp * PAIR:(qp + 1) * PAIR])
                ps, dens = [], []
                for pos in range(2):
                    h = 2 * qp + pos
                    s_t = jnp.where(mask_t, s2[pos * tk:(pos + 1) * tk], NEG_INF)
                    m = jnp.max(s_t, axis=0, keepdims=True)
                    if sink is not None:
                        m = jnp.maximum(m, sink_ref[0, h])
                    p_t = jnp.exp(s_t - m)
                    den = jnp.sum(p_t, axis=0, keepdims=True)
                    if sink is not None:
                        den = den + jnp.exp(sink_ref[0, h] - m)
                    ps.append(p_t.astype(BF16))
                    dens.append(den)
                    lse_rows.append(m + jnp.log(den))
                both = _dot_tn(vst, jnp.concatenate(ps, axis=0))
                o_t.append(both / jnp.where(row_hi, dens[1], dens[0]))
            o_ref[rows, :] = jnp.concatenate(o_t, axis=0).T
            lse_ref[:, rows] = jnp.concatenate(lse_rows, axis=0)

    in_specs = [pl.BlockSpec((None, t, hq * HEAD_DIM), lambda s, i: (s, i, qcol))] + kv_specs
    args = [qkv] * 7
    if sink is not None:
        in_specs = [pl.BlockSpec(memory_space=pltpu.SMEM)] + in_specs
        args = [sink] + args
    (o, lse), couts = _pcall(
        body, name=name, grid=(nseq, n), in_specs=in_specs,
        out_specs=[pl.BlockSpec((None, t, hq * HEAD_DIM), lambda s, i: (s, i, 0)),
                   pl.BlockSpec((None, hq, t), lambda s, i: (s, 0, i))],
        out_shape=[jax.ShapeDtypeStruct((nseq, seq_len, hq * HEAD_DIM), F32),
                   jax.ShapeDtypeStruct((nseq, hq, seq_len), F32)],
        args=args, dims=("parallel", "parallel"), comm=comm)
    return o, lse, couts


def _swa_bwd_p(qkv, do, lse, delta, cs, e_mat, *, qcol, kcol, vcol, hq, hkv, w, tq, sub, sink, name, comm=None):
    nseq, seq_len, _ = qkv.shape
    t = tq * sub
    n = seq_len // t
    rep = hq // hkv
    qw, kw = hq * HEAD_DIM, hkv * HEAD_DIM
    tk = tq + 2 * w
    kv_specs, cur = _halo_kv_specs(t, w, hkv, n, seq_len, kcol, vcol)

    def body(*refs):
        if sink is not None:
            sink_ref, refs = refs[0], refs[1:]
        (q_ref, kp_ref, kc_ref, kn_ref, vp_ref, vc_ref, vn_ref, do_ref, lse_ref, dl_ref,
         cs_c, cs_p, e_ref) = refs[:13]
        outs = refs[13:]
        if sink is not None:
            dq_ref, dk_ref, dv_ref, dsink_ref, dk_acc, dv_acc, dk_win, dv_win = outs
        else:
            dq_ref, dk_ref, dv_ref, dk_acc, dv_acc, dk_win, dv_win = outs
        s_id = pl.program_id(0)
        i = pl.program_id(1)
        slot_p, slot_c, slot_n = (i + 2) % 3, i % 3, (i + 1) % 3

        if sink is not None:
            @pl.when((s_id == 0) & (i == 0))
            def _():
                dsink_ref[...] = jnp.zeros_like(dsink_ref)

        @pl.when(i < n)
        def _():
            dk_win[...] = jnp.zeros_like(dk_win)
            dv_win[...] = jnp.zeros_like(dv_win)
            kfull, vfull = [], []
            for a in range(hkv // 2):
                ls = slice(a * PAIR, (a + 1) * PAIR)
                kfull.append(jnp.concatenate([kp_ref[:, ls], kc_ref[:, ls], kn_ref[:, ls]], axis=0) * 0.125)
                vfull.append(jnp.concatenate([vp_ref[:, ls], vc_ref[:, ls], vn_ref[:, ls]], axis=0))
            for jj in range(sub):
                rows = slice(jj * tq, (jj + 1) * tq)
                krows = slice(jj * tq, jj * tq + tk)
                mask_t = _band_mask_t(i * t + jj * tq, tq, w, seq_len)
                dq_t = []
                dk2 = [None] * (hkv // 2)
                dv2 = [None] * (hkv // 2)
                for qp in range(hq // 2):
                    kst, vst, a = _pair_kv(kfull, vfull, qp, rep, krows)
                    q2 = q_ref[rows, qp * PAIR:(qp + 1) * PAIR]
                    do2 = do_ref[rows, qp * PAIR:(qp + 1) * PAIR]
                    s2 = _dot_nt(kst, q2)
                    dp2 = _dot_nt(vst, do2)
                    ds, ps, q_at, do_at = [], [], [], []
                    for pos in range(2):
                        h = 2 * qp + pos
                        e = (h // rep) % 2
                        half = slice(pos * tk, (pos + 1) * tk)
                        lse_h = lse_ref[h:h + 1, rows]
                        dl_h = dl_ref[h:h + 1, rows]
                        p_t = jnp.exp(jnp.where(mask_t, s2[half], NEG_INF) - lse_h)
                        ds.append((p_t * (dp2[half] - dl_h)).astype(BF16))
                        ps.append(p_t.astype(BF16))
                        q_at.append(_place_head(q2, pos, e) * 0.125)
                        do_at.append(_place_head(do2, pos, e))
                        if sink is not None:
                            ds_sink = -jnp.sum(jnp.exp(sink_ref[0, h] - lse_h) * dl_h)
                            dsink_ref[h:h + 1, :] += jnp.full((1, LANES), ds_sink, F32)
                    dq_t.append(_rope_rows(_dot_tn(kst, jnp.concatenate(ds, axis=0)),
                                           cs_c[0:ROT_DIM // 2, rows], cs_c[ROT_DIM // 2:ROT_DIM, rows], -1.0))
                    dk_part = _dot(jnp.concatenate(ds, axis=1), jnp.concatenate(q_at, axis=0))
                    dv_part = _dot(jnp.concatenate(ps, axis=1), jnp.concatenate(do_at, axis=0))
                    dk2[a] = dk_part if dk2[a] is None else dk2[a] + dk_part
                    dv2[a] = dv_part if dv2[a] is None else dv2[a] + dv_part
                for a in range(hkv // 2):
                    ls = slice(a * PAIR, (a + 1) * PAIR)
                    dk_win[krows, ls] += dk2[a]
                    dv_win[krows, ls] += dv2[a]
                dq_ref[rows, :] = jnp.concatenate(dq_t, axis=0).T.astype(BF16)

            @pl.when(i > 0)
            def _():
                dk_acc[slot_p, t - w:, :] += dk_win[:w, :]
                dv_acc[slot_p, t - w:, :] += dv_win[:w, :]

            @pl.when(i == 0)
            def _():
                dk_acc[slot_c] = dk_win[w:w + t, :]
                dv_acc[slot_c] = dv_win[w:w + t, :]

            @pl.when(i > 0)
            def _():
                dk_acc[slot_c] += dk_win[w:w + t, :]
                dv_acc[slot_c] += dv_win[w:w + t, :]

            dk_acc[slot_n] = jnp.zeros((t, kw), F32)
            dv_acc[slot_n] = jnp.zeros((t, kw), F32)
            dk_acc[slot_n, :w, :] = dk_win[w + t:, :]
            dv_acc[slot_n, :w, :] = dv_win[w + t:, :]

        @pl.when(i >= 1)
        def _():
            dk_ref[...] = _rope(dk_acc[slot_p], *_rope_tabs(cs_p[...], e_ref[...]), -1.0).astype(BF16)
            dv_ref[...] = dv_acc[slot_p].astype(BF16)

    row_c = lambda width: pl.BlockSpec((None, t, width), lambda s, i: (s, cur(s, i), 0))
    row_p = lambda width: pl.BlockSpec((None, t, width), lambda s, i: (s, jnp.maximum(i - 1, 0), 0))
    stat = pl.BlockSpec((None, hq, t), lambda s, i: (s, 0, cur(s, i)))
    cs_rows = pl.BlockSpec((None, ROT_DIM, t), lambda s, i: (s, 0, cur(s, i)))
    in_specs = ([pl.BlockSpec((None, t, qw), lambda s, i: (s, cur(s, i), qcol))] + kv_specs
                + [row_c(qw), stat, stat, cs_rows, row_p(ROT_DIM),
                   pl.BlockSpec((ROT_DIM, 3 * LANES), lambda s, i: (0, 0))])
    args = [qkv] * 7 + [do, lse, delta, cs.transpose(0, 2, 1), cs, e_mat]
    out_specs = [row_c(qw), row_p(kw), row_p(kw)]
    out_shape = [jax.ShapeDtypeStruct((nseq, seq_len, qw), BF16),
                 jax.ShapeDtypeStruct((nseq, seq_len, kw), BF16),
                 jax.ShapeDtypeStruct((nseq, seq_len, kw), BF16)]
    if sink is not None:
        in_specs = [pl.BlockSpec(memory_space=pltpu.SMEM)] + in_specs
        args = [sink] + args
        out_specs.append(pl.BlockSpec((8, LANES), lambda s, i: (0, 0)))
        out_shape.append(jax.ShapeDtypeStruct((8, LANES), F32))
    return _pcall(
        body, name=name, grid=(nseq, n + 1), in_specs=in_specs, out_specs=out_specs, out_shape=out_shape,
        scratch_shapes=[pltpu.VMEM((3, t, kw), F32), pltpu.VMEM((3, t, kw), F32),
                        pltpu.VMEM((t + 2 * w, kw), F32), pltpu.VMEM((t + 2 * w, kw), F32)], args=args,
        dims=("arbitrary", "arbitrary"), comm=comm)


def _rms_parts(o, g):
    ms = jnp.mean(o * o, axis=-1, keepdims=True) + LN_EPS
    rinv = lax.rsqrt(ms)
    return o * rinv * g, rinv


def _from_subsequences(ref, scr, dil, t):
    slabs = ref.shape[-1] // LANES
    if dil == 1:
        return ref[0].astype(F32)
    for c in range(dil):
        for sl in range(slabs):
            scr[sl, pl.ds(c, t // dil, stride=dil), :] = ref[c, :, sl * LANES:(sl + 1) * LANES].astype(F32)
    return jnp.concatenate([scr[sl] for sl in range(slabs)], axis=1)


def _to_subsequences(val, ref, scr, dil, t):
    slabs = val.shape[-1] // LANES
    if dil == 1:
        ref[0] = val.astype(ref.dtype)
        return
    for sl in range(slabs):
        scr[sl] = val[:, sl * LANES:(sl + 1) * LANES]
    for c in range(dil):
        for sl in range(slabs):
            ref[c, :, sl * LANES:(sl + 1) * LANES] = scr[sl, pl.ds(c, t // dil, stride=dil), :].astype(ref.dtype)


def _combine_fwd(out_a, o_g, lse_g, g_win, g_dil, w_mix_b, x, ln_in_g, ln_in_b, ln1_g, ln1_b, *, t, comm=None):
    s = out_a.shape[1]
    wd = DIL_SLOTS * HEAD_DIM

    def body(oa_ref, o0, o1, o2, l0, l1, l2, gw_ref, gd_ref, w_ref, x_ref, g0, b0, g1, b1,
             mixed_ref, ob_ref, lt_ref, r1_ref, h1_ref, scr):
        ls = [l0[...], l1[...], l2[...]]
        mx = jnp.maximum(jnp.maximum(ls[0], ls[1]), ls[2])
        ws = [jnp.exp(l - mx) for l in ls]
        tot = ws[0] + ws[1] + ws[2]
        lt_ref[...] = mx + jnp.log(tot)
        ws = [x / tot for x in ws]
        og = [_from_subsequences(o_ref, scr.at[gi], dil, t)
              for gi, (o_ref, dil) in enumerate(zip((o0, o1, o2), DILATIONS))]
        parts = []
        for h in range(DIL_SLOTS):
            hs = slice(h * HEAD_DIM, (h + 1) * HEAD_DIM)
            parts.append(ws[0][:, h:h + 1] * og[0][:, hs] + ws[1][:, h:h + 1] * og[1][:, hs]
                         + ws[2][:, h:h + 1] * og[2][:, hs])
        ob = jnp.concatenate(parts, axis=1)
        ob_ref[...] = ob
        na, _ = _rms_parts(oa_ref[...], gw_ref[...])
        nb, _ = _rms_parts(ob, gd_ref[...])
        mixed = jnp.concatenate([na.astype(BF16), nb.astype(BF16)], axis=1)
        mixed_ref[...] = mixed
        h0 = _ln(x_ref[...], g0[...], b0[...])
        r1 = ALPHA * h0 + _dot(mixed, w_ref[...])
        r1_ref[...] = r1
        h1_ref[...] = _ln(r1, g1[...], b1[...]).astype(BF16)

    half = pl.BlockSpec((t, wd), lambda i: (i, 0))
    full = pl.BlockSpec((t, D_MODEL), lambda i: (i, 0))
    lanes = pl.BlockSpec((t, LANES), lambda i: (i, 0))
    grow = pl.BlockSpec((1, wd), lambda i: (0, 0))
    row = pl.BlockSpec((1, D_MODEL), lambda i: (0, 0))
    subseq = [pl.BlockSpec((dil, t // dil, wd), lambda i: (0, i, 0)) for dil in DILATIONS]
    return _pcall(
        body, name="combine_fwd", grid=(s // t,),
        in_specs=[pl.BlockSpec((None, t, wd), lambda i: (0, i, 0))] + subseq
        + [lanes, lanes, lanes, grow, grow, pl.BlockSpec((D_MODEL, D_MODEL), lambda i: (0, 0)), full,
           row, row, row, row],
        out_specs=[full, half, lanes, full, full],
        out_shape=[jax.ShapeDtypeStruct((s, D_MODEL), BF16), jax.ShapeDtypeStruct((s, wd), F32),
                   jax.ShapeDtypeStruct((s, LANES), F32), jax.ShapeDtypeStruct((s, D_MODEL), F32),
                   jax.ShapeDtypeStruct((s, D_MODEL), BF16)],
        scratch_shapes=[pltpu.VMEM((len(DILATIONS), wd // LANES, t, LANES), F32)],
        args=[out_a, *o_g, *lse_g, g_win, g_dil, w_mix_b, x, ln_in_g, ln_in_b, ln1_g, ln1_b], dims=("parallel",),
        comm=comm)


def _combine_bwd(dr1b, w_mix_b, out_a, out_b, g_win, g_dil, *, t):
    s = out_b.shape[0]
    wd = DIL_SLOTS * HEAD_DIM

    def body(dr_ref, w_ref, oa_ref, ob_ref, gw_ref, gd_ref, doa_ref, dob0, dob1, dob2, dla_ref, dlb_ref, st_ref,
             scr):
        i = pl.program_id(0)
        dm = _dot_nt(dr_ref[...], w_ref[...])

        @pl.when(i == 0)
        def _():
            st_ref[...] = jnp.zeros_like(st_ref)

        lane = lax.broadcasted_iota(jnp.int32, (t, LANES), 1)
        for idx, (o_ref, g_ref, dl_ref) in enumerate(((oa_ref, gw_ref, dla_ref), (ob_ref, gd_ref, dlb_ref))):
            o = o_ref[...]
            dn = dm[:, idx * wd:(idx + 1) * wd]
            _, rinv = _rms_parts(o, g_ref[...])
            wv = dn * g_ref[...]
            do = rinv * wv - o * (rinv * rinv * rinv) * jnp.mean(wv * o, axis=-1, keepdims=True)
            st_ref[idx:idx + 1, :] += jnp.sum(dn * o * rinv, axis=0, keepdims=True)
            if idx == 0:
                doa_ref[...] = do.astype(BF16)
            else:
                for do_ref, dil in zip((dob0, dob1, dob2), DILATIONS):
                    _to_subsequences(do, do_ref, scr, dil, t)
            prod = do * o
            acc = jnp.zeros((t, LANES), F32)
            for h in range(DIL_SLOTS):
                hs = slice(h * HEAD_DIM, (h + 1) * HEAD_DIM)
                acc = jnp.where(lane == h, jnp.sum(prod[:, hs], axis=1, keepdims=True), acc)
            dl_ref[...] = acc

    half = pl.BlockSpec((t, wd), lambda i: (i, 0))
    lanes = pl.BlockSpec((t, LANES), lambda i: (i, 0))
    grow = pl.BlockSpec((1, wd), lambda i: (0, 0))
    a_spec = pl.BlockSpec((None, t, wd), lambda i: (0, i, 0))
    subseq = [pl.BlockSpec((dil, t // dil, wd), lambda i: (0, i, 0)) for dil in DILATIONS]
    doa, dob0, dob1, dob2, dla, dlb, st = pl.pallas_call(
        body, name="combine_bwd", grid=(s // t,),
        in_specs=[pl.BlockSpec((t, D_MODEL), lambda i: (i, 0)), pl.BlockSpec((D_MODEL, D_MODEL), lambda i: (0, 0)),
                  a_spec, half, grow, grow],
        out_specs=[a_spec] + subseq + [lanes, lanes, pl.BlockSpec((8, wd), lambda i: (0, 0))],
        out_shape=[jax.ShapeDtypeStruct((1, s, wd), BF16)]
        + [jax.ShapeDtypeStruct((dil, s // dil, wd), BF16) for dil in DILATIONS]
        + [jax.ShapeDtypeStruct((s, LANES), F32), jax.ShapeDtypeStruct((s, LANES), F32),
           jax.ShapeDtypeStruct((8, wd), F32)],
        scratch_shapes=[pltpu.VMEM((wd // LANES, t, LANES), F32)],
        compiler_params=_cparams(dimension_semantics=("arbitrary",)),
    )(dr1b, w_mix_b, out_a, out_b, g_win, g_dil)
    return doa, [dob0, dob1, dob2], dla, dlb, st


def _assemble_dz(dqa, dka, dva, dqs, dks, dvs, *, t):
    s = dqa.shape[1]
    wd = DIL_SLOTS * HEAD_DIM

    def body(*refs):
        a_refs, g_refs, o_ref, scr = refs[:3], refs[3:12], refs[12], refs[13]
        col = 0
        for r in a_refs:
            o_ref[:, col:col + r.shape[-1]] = r[...]
            col += r.shape[-1]
        for part in range(3):
            for gi, dil in enumerate(DILATIONS):
                val = _from_subsequences(g_refs[3 * part + gi], scr, dil, t)
                o_ref[:, col:col + wd] = val.astype(BF16)
                col += wd

    a_specs = [pl.BlockSpec((None, t, a.shape[-1]), lambda i: (0, i, 0)) for a in (dqa, dka, dva)]
    g_specs = [pl.BlockSpec((dil, t // dil, wd), lambda i: (0, i, 0)) for _ in range(3) for dil in DILATIONS]
    return pl.pallas_call(
        body, name="assemble_dz", grid=(s // t,), in_specs=a_specs + g_specs,
        out_specs=pl.BlockSpec((t, IN_WIDTH), lambda i: (i, 0)),
        out_shape=jax.ShapeDtypeStruct((s, IN_WIDTH), BF16),
        scratch_shapes=[pltpu.VMEM((wd // LANES, t, LANES), F32)],
        compiler_params=_cparams(dimension_semantics=("parallel",)),
    )(dqa, dka, dva, *dqs, *dks, *dvs)


def _mem_fwd(mem, g, b, wk_b, wv_b):
    ml = mem.shape[0]

    def body(mem_ref, g_ref, b_ref, wk_ref, wv_ref, mn_ref, kx_ref, vx_ref):
        mn = _ln(mem_ref[...], g_ref[...], b_ref[...]).astype(BF16)
        mn_ref[...] = mn
        kx_ref[...] = _dot(mn, wk_ref[...]).astype(BF16)
        vx_ref[...] = _dot(mn, wv_ref[...]).astype(BF16)

    sh = jax.ShapeDtypeStruct((ml, D_MODEL), BF16)
    return pl.pallas_call(body, name="mem_fwd", out_shape=[sh, sh, sh], compiler_params=_cparams())(
        mem, g, b, wk_b, wv_b)


def _mem_bwd(dkx, dvx, mem, g, b, wk_b, wv_b):
    def body(dk_ref, dv_ref, mem_ref, g_ref, b_ref, wk_ref, wv_ref, dwk_ref, dwv_ref, st_ref):
        mem_v = mem_ref[...]
        mn = _ln(mem_v, g_ref[...], b_ref[...]).astype(BF16)
        dkb = dk_ref[...].astype(BF16)
        dvb = dv_ref[...].astype(BF16)
        dwk_ref[...] = _dot_tn(mn, dkb)
        dwv_ref[...] = _dot_tn(mn, dvb)
        dmn = _dot_nt(dkb, wk_ref[...]) + _dot_nt(dvb, wv_ref[...])
        _, dg, db = _ln_bwd_math(dmn, mem_v, g_ref[...])
        st_ref[...] = jnp.zeros_like(st_ref)
        st_ref[0:1, :] = dg
        st_ref[1:2, :] = db

    sw = jax.ShapeDtypeStruct((D_MODEL, D_MODEL), F32)
    return pl.pallas_call(body, name="mem_bwd", out_shape=[sw, sw, jax.ShapeDtypeStruct((8, D_MODEL), F32)],
                          compiler_params=_cparams())(dkx, dvx, mem, g, b, wk_b, wv_b)


def _xattn_fwd(h1b, r1, kx, vx, wq_b, wo_b, ln1_g, ln1_b, ln2_g, ln2_b, *, t, comm=None):
    s = h1b.shape[0]
    scale = X_HEAD_DIM ** -0.5

    def body(h_ref, r1_ref, kx_ref, vx_ref, wq_ref, wo_ref, g1, b1, g2, b2, r2_ref, h2_ref, qx_ref, ox_ref, lse_ref):
        qxb = _dot(h_ref[...], wq_ref[...]).astype(BF16)
        qx_ref[...] = qxb
        lane = lax.broadcasted_iota(jnp.int32, (t, LANES), 1)
        lse_acc = jnp.zeros((t, LANES), F32)
        parts = []
        for h in range(X_HEADS):
            hs = slice(h * X_HEAD_DIM, (h + 1) * X_HEAD_DIM)
            sc = _dot_nt(qxb[:, hs] * scale, kx_ref[:, hs])
            m = jnp.max(sc, axis=1, keepdims=True)
            p = jnp.exp(sc - m)
            den = jnp.sum(p, axis=1, keepdims=True)
            parts.append(_dot(p.astype(BF16), vx_ref[:, hs]) / den)
            lse_acc = jnp.where(lane == h, m + jnp.log(den), lse_acc)
        lse_ref[...] = lse_acc
        oxb = jnp.concatenate(parts, axis=1).astype(BF16)
        ox_ref[...] = oxb
        h1 = _ln(r1_ref[...], g1[...], b1[...])
        r2 = ALPHA * h1 + _dot(oxb, wo_ref[...])
        r2_ref[...] = r2
        h2_ref[...] = _ln(r2, g2[...], b2[...]).astype(BF16)

    tile = pl.BlockSpec((t, D_MODEL), lambda i: (i, 0))
    row = pl.BlockSpec((1, D_MODEL), lambda i: (0, 0))
    full = lambda r: pl.BlockSpec((r, D_MODEL), lambda i: (0, 0))
    ml = kx.shape[0]
    bsh = jax.ShapeDtypeStruct((s, D_MODEL), BF16)
    return _pcall(
        body, name="xattn_fwd", grid=(s // t,),
        in_specs=[tile, tile, full(ml), full(ml), full(D_MODEL), full(D_MODEL), row, row, row, row],
        out_specs=[tile, tile, tile, tile, pl.BlockSpec((t, LANES), lambda i: (i, 0))],
        out_shape=[jax.ShapeDtypeStruct((s, D_MODEL), F32), bsh, bsh, bsh, jax.ShapeDtypeStruct((s, LANES), F32)],
        args=[h1b, r1, kx, vx, wq_b, wo_b, ln1_g, ln1_b, ln2_g, ln2_b], dims=("parallel",), comm=comm)


def _xattn_bwd(dr2, qxb, oxb, lse, kx, vx, wq_b, wo_b, r1, ln1_g, *, t, comm=None):
    s = dr2.shape[0]
    ml = kx.shape[0]
    scale = X_HEAD_DIM ** -0.5

    def body(dr2_ref, qx_ref, ox_ref, lse_ref, kx_ref, vx_ref, wq_ref, wo_ref, r1_ref, g1_ref,
             dr1_ref, dr1b_ref, dqx_ref, dkx_ref, dvx_ref, st_ref):
        i = pl.program_id(0)

        @pl.when(i == 0)
        def _():
            dkx_ref[...] = jnp.zeros_like(dkx_ref)
            dvx_ref[...] = jnp.zeros_like(dvx_ref)
            st_ref[...] = jnp.zeros_like(st_ref)

        dr2v = dr2_ref[...]
        dox = _dot_nt(dr2v.astype(BF16), wo_ref[...])
        parts = []
        for h in range(X_HEADS):
            hs = slice(h * X_HEAD_DIM, (h + 1) * X_HEAD_DIM)
            doh = dox[:, hs]
            dohb = doh.astype(BF16)
            dl = jnp.sum(doh * ox_ref[:, hs].astype(F32), axis=1, keepdims=True)
            qh = qx_ref[:, hs] * scale
            p = jnp.exp(_dot_nt(qh, kx_ref[:, hs]) - lse_ref[:, h:h + 1])
            dp = _dot_nt(dohb, vx_ref[:, hs])
            dsb = (p * (dp - dl)).astype(BF16)
            parts.append(_dot(dsb, kx_ref[:, hs]) * scale)
            dkx_ref[:, hs] += _dot_tn(dsb, qh)
            dvx_ref[:, hs] += _dot_tn(p.astype(BF16), dohb)
        dqxb = jnp.concatenate(parts, axis=1).astype(BF16)
        dqx_ref[...] = dqxb
        dh1 = _dot_nt(dqxb, wq_ref[...]) + ALPHA * dr2v
        dr1, dg, db = _ln_bwd_math(dh1, r1_ref[...], g1_ref[...])
        dr1_ref[...] = dr1
        dr1b_ref[...] = dr1.astype(BF16)
        st_ref[0:1, :] += dg
        st_ref[1:2, :] += db

    tile = pl.BlockSpec((t, D_MODEL), lambda i: (i, 0))
    full = lambda r: pl.BlockSpec((r, D_MODEL), lambda i: (0, 0))
    bsh = jax.ShapeDtypeStruct((s, D_MODEL), BF16)
    return _pcall(
        body, name="xattn_bwd", grid=(s // t,),
        in_specs=[tile, tile, tile, pl.BlockSpec((t, LANES), lambda i: (i, 0)), full(ml), full(ml),
                  full(D_MODEL), full(D_MODEL), tile, full(1)],
        out_specs=[tile, tile, tile, full(ml), full(ml), full(8)],
        out_shape=[jax.ShapeDtypeStruct((s, D_MODEL), F32), bsh, bsh,
                   jax.ShapeDtypeStruct((ml, D_MODEL), F32), jax.ShapeDtypeStruct((ml, D_MODEL), F32),
                   jax.ShapeDtypeStruct((8, D_MODEL), F32)],
        args=[dr2, qxb, oxb, lse, kx, vx, wq_b, wo_b, r1, ln1_g], dims=("arbitrary",), comm=comm)


def _halo_specs(t, s, width):
    tb8 = t // 8
    return [pl.BlockSpec((t, width), lambda i: (i, 0)),
            pl.BlockSpec((8, width), lambda i: (jnp.maximum(i * tb8 - 1, 0), 0)),
            pl.BlockSpec((8, width), lambda i: (jnp.minimum((i + 1) * tb8, s // 8 - 1), 0))]


def _halo_rows(i, n, prev_ref, next_ref):
    prev_row = jnp.where(i > 0, prev_ref[7:8, :], 0.0)
    next_row = jnp.where(i < n - 1, next_ref[0:1, :], 0.0)
    return prev_row, next_row


def _gelu_parts(gc):
    cdf = 0.5 * (1.0 + lax.erf(gc * (2.0 ** -0.5)))
    pdf = jnp.exp(-0.5 * gc * gc) * (1.0 / math.sqrt(2.0 * math.pi))
    return gc * cdf, cdf + gc * pdf


def _ffn_out(g, u, conv_w, conv_b, w_down_b, r2, target, ln2_g, ln2_b, ln3_g, ln3_b, *, t):
    s = r2.shape[0]
    n = s // t

    def body(g_ref, gp_ref, gn_ref, u_ref, cw_ref, cb_ref, w_ref, r2_ref, tg_ref, g2, b2, g3, b3,
             t_ref, dr_ref, drb_ref, st_ref):
        i = pl.program_id(0)

        @pl.when(i == 0)
        def _():
            st_ref[...] = jnp.zeros_like(st_ref)

        gv = g_ref[...]
        prev_row, next_row = _halo_rows(i, n, gp_ref, gn_ref)
        gm1, gp1 = _shift_rows(gv, prev_row, next_row)
        gc = gm1 * cw_ref[0:1, :] + gv * cw_ref[1:2, :] + gp1 * cw_ref[2:3, :] + cb_ref[...]
        act, _ = _gelu_parts(gc)
        tb = (act * u_ref[...]).astype(BF16)
        t_ref[...] = tb
        h2 = _ln(r2_ref[...], g2[...], b2[...])
        r3 = ALPHA * h2 + _dot(tb, w_ref[...])
        y = _ln(r3, g3[...], b3[...])
        err = y - tg_ref[...]
        loss = 0.5 * jnp.sum(jnp.mean(err * err, axis=-1, keepdims=True))
        dr, dg, db = _ln_bwd_math(err * (1.0 / D_MODEL), r3, g3[...])
        dr_ref[...] = dr
        drb_ref[...] = dr.astype(BF16)
        st_ref[0:1, :] += dg
        st_ref[1:2, :] += db
        st_ref[2:3, :] += jnp.full((1, D_MODEL), loss, F32)

    wide = pl.BlockSpec((t, D_FF), lambda i: (i, 0))
    tile = pl.BlockSpec((t, D_MODEL), lambda i: (i, 0))
    row = pl.BlockSpec((1, D_MODEL), lambda i: (0, 0))
    return pl.pallas_call(
        body, name="ffn_out", grid=(n,),
        in_specs=_halo_specs(t, s, D_FF) + [wide, pl.BlockSpec((3, D_FF), lambda i: (0, 0)),
                                            pl.BlockSpec((1, D_FF), lambda i: (0, 0)),
                                            pl.BlockSpec((D_FF, D_MODEL), lambda i: (0, 0)),
                                            tile, tile, row, row, row, row],
        out_specs=[wide, tile, tile, pl.BlockSpec((8, D_MODEL), lambda i: (0, 0))],
        out_shape=[jax.ShapeDtypeStruct((s, D_FF), BF16), jax.ShapeDtypeStruct((s, D_MODEL), F32),
                   jax.ShapeDtypeStruct((s, D_MODEL), BF16), jax.ShapeDtypeStruct((8, D_MODEL), F32)],
        compiler_params=_cparams(dimension_semantics=("arbitrary",)),
    )(g, g, g, u, conv_w, conv_b, w_down_b, r2, target, ln2_g, ln2_b, ln3_g, ln3_b)


def _dh2_ln2(dgc, conv_w, du, w_gate_b, w_up_b, dr3, r2, ln2_g, *, t, comm=None):
    s = dgc.shape[0]
    n = s // t

    def body(d_ref, dp_ref, dn_ref, cw_ref, du_ref, wg_ref, wu_ref, dr3_ref, r2_ref, g2, dg_ref, dr_ref, drb_ref,
             st_ref):
        i = pl.program_id(0)

        @pl.when(i == 0)
        def _():
            st_ref[...] = jnp.zeros_like(st_ref)

        dv = d_ref[...]
        prev_row, next_row = _halo_rows(i, n, dp_ref, dn_ref)
        dm1, dp1 = _shift_rows(dv, prev_row, next_row)
        dgb = (dp1 * cw_ref[0:1, :] + dv * cw_ref[1:2, :] + dm1 * cw_ref[2:3, :]).astype(BF16)
        dg_ref[...] = dgb
        dh2 = _dot(dgb, wg_ref[...]) + _dot(du_ref[...], wu_ref[...]) + ALPHA * dr3_ref[...]
        dr, dg, db = _ln_bwd_math(dh2, r2_ref[...], g2[...])
        dr_ref[...] = dr
        drb_ref[...] = dr.astype(BF16)
        st_ref[0:1, :] += dg
        st_ref[1:2, :] += db

    wide = pl.BlockSpec((t, D_FF), lambda i: (i, 0))
    tile = pl.BlockSpec((t, D_MODEL), lambda i: (i, 0))
    wfull = pl.BlockSpec((D_FF, D_MODEL), lambda i: (0, 0), pipeline_mode=pl.Buffered(1))
    return _pcall(
        body, name="dh2_ln2", grid=(n,),
        in_specs=_halo_specs(t, s, D_FF) + [pl.BlockSpec((3, D_FF), lambda i: (0, 0)), wide, wfull, wfull,
                                            tile, tile, pl.BlockSpec((1, D_MODEL), lambda i: (0, 0))],
        out_specs=[wide, tile, tile, pl.BlockSpec((8, D_MODEL), lambda i: (0, 0))],
        out_shape=[jax.ShapeDtypeStruct((s, D_FF), BF16), jax.ShapeDtypeStruct((s, D_MODEL), F32),
                   jax.ShapeDtypeStruct((s, D_MODEL), BF16), jax.ShapeDtypeStruct((8, D_MODEL), F32)],
        args=[dgc, dgc, dgc, conv_w, du, w_gate_b, w_up_b, dr3, r2, ln2_g], dims=("arbitrary",), comm=comm)


def _conv_bwd_a(dr3b, w_down_b, g, u, conv_w, conv_b, *, t):
    s = g.shape[0]
    n = s // t

    def body(d_ref, w_ref, g_ref, gp_ref, gn_ref, u_ref, cw_ref, cb_ref, du_ref, dgc_ref, st_ref):
        i = pl.program_id(0)

        @pl.when(i == 0)
        def _():
            st_ref[...] = jnp.zeros_like(st_ref)

        dt = _dot_nt(d_ref[...], w_ref[...])
        gv = g_ref[...]
        prev_row, next_row = _halo_rows(i, n, gp_ref, gn_ref)
        gm1, gp1 = _shift_rows(gv, prev_row, next_row)
        gc = gm1 * cw_ref[0:1, :] + gv * cw_ref[1:2, :] + gp1 * cw_ref[2:3, :] + cb_ref[...]
        act, dact = _gelu_parts(gc)
        du_ref[...] = (dt * act).astype(BF16)
        dgc = dt * u_ref[...] * dact
        dgc_ref[...] = dgc
        st_ref[0:1, :] += jnp.sum(gm1 * dgc, axis=0, keepdims=True)
        st_ref[1:2, :] += jnp.sum(gv * dgc, axis=0, keepdims=True)
        st_ref[2:3, :] += jnp.sum(gp1 * dgc, axis=0, keepdims=True)
        st_ref[3:4, :] += jnp.sum(dgc, axis=0, keepdims=True)

    tile = pl.BlockSpec((t, D_FF), lambda i: (i, 0))
    return pl.pallas_call(
        body, name="conv_bwd_a", grid=(n,),
        in_specs=[pl.BlockSpec((t, D_MODEL), lambda i: (i, 0)), pl.BlockSpec((D_FF, D_MODEL), lambda i: (0, 0))]
        + _halo_specs(t, s, D_FF) + [tile, pl.BlockSpec((3, D_FF), lambda i: (0, 0)),
                                     pl.BlockSpec((1, D_FF), lambda i: (0, 0))],
        out_specs=[tile, tile, pl.BlockSpec((8, D_FF), lambda i: (0, 0))],
        out_shape=[jax.ShapeDtypeStruct((s, D_FF), BF16), jax.ShapeDtypeStruct((s, D_FF), F32),
                   jax.ShapeDtypeStruct((8, D_FF), F32)],
        compiler_params=_cparams(dimension_semantics=("arbitrary",)),
    )(dr3b, w_down_b, g, g, g, u, conv_w, conv_b)


def _to_residue(a, dil):
    s, w = a.shape
    return a.reshape(s // dil, dil, w).transpose(1, 0, 2)


def _stats_to_lanes(rows):
    dil, hq, l = rows.shape
    return jnp.pad(rows.transpose(2, 0, 1).reshape(dil * l, hq), ((0, 0), (0, LANES - hq)))


def _stats_to_rows(lanes, dil):
    s = lanes.shape[0]
    return lanes[:, :DIL_SLOTS].reshape(s // dil, dil, DIL_SLOTS).transpose(1, 2, 0)


def _rope_angles(positions):
    inv_freq = ROPE_THETA ** (-jnp.arange(0, ROT_DIM, 2, dtype=F32) / ROT_DIM)
    ang = positions.astype(F32)[:, None] * inv_freq
    return jnp.concatenate([jnp.cos(ang), jnp.sin(ang)], axis=1)


class _NoPlan:
    def gather(self, stage):
        return None

    def gathered(self, stage, couts, wb):
        pass

    def exchange(self, stage, grads):
        return None

    def exchanged(self, stage, couts):
        pass


def _local_step(x, mem, positions, target, wb, sp, plan=None, *, t_row=256, t_mm=512, tq_a=128, tq_b=128,
                sub_a=4, sub_b=4):
    s = x.shape[0]
    plan = plan or _NoPlan()
    cs = _rope_angles(positions)
    e_mat = _rope_select_matrix()

    h0b, couts = _ln_in_fwd(x, sp["ln_in_g"], sp["ln_in_b"], t=t_mm, comm=plan.gather("ln_in"))
    plan.gathered("ln_in", couts, wb)
    sp = dict(sp, conv_w=wb.get("conv_w", sp.get("conv_w")))
    (za, *zb), couts = _proj_all(h0b, wb["w_in"], cs, e_mat, t=min(2 * t_mm, s), comm=plan.gather("proj"))
    plan.gathered("proj", couts, wb)
    sub_a = max(1, min(sub_a, s // tq_a))
    subs_b = [max(1, min(sub_b, s // dil // tq_b)) for dil in DILATIONS]
    out_a, lse_a, couts = _swa_fwd_p(za, qcol=0, kcol=4, vcol=5, hq=WIN_Q_HEADS, hkv=WIN_KV_HEADS, w=WIN_HALF,
                                     tq=tq_a, sub=sub_a, sink=sp["attn_sink"], name="attn_a_fwd",
                                     comm=plan.gather("attn_a"))
    plan.gathered("attn_a", couts, wb)
    o_g, lse_g = [], []
    for gi in range(3):
        o, l, couts = _swa_fwd_p(zb[gi], qcol=0, kcol=1, vcol=2, hq=DIL_SLOTS, hkv=DIL_SLOTS, w=DIL_HALF, tq=tq_b,
                                 sub=subs_b[gi], sink=None, name=f"attn_b{gi}_fwd",
                                 comm=plan.gather(f"attn_b{gi}"))
        plan.gathered(f"attn_b{gi}", couts, wb)
        o_g.append(o)
        lse_g.append(_stats_to_lanes(l))
    (mixed_b, out_b, lse_b, r1, h1b), couts = _combine_fwd(
        out_a, o_g, lse_g, sp["g_win"], sp["g_dil"], wb["w_mix_out"], x, sp["ln_in_g"], sp["ln_in_b"],
        sp["ln1_g"], sp["ln1_b"], t=t_row, comm=plan.gather("combine"))
    plan.gathered("combine", couts, wb)
    mem_nb, kx, vx = _mem_fwd(mem, sp["mem_ln_g"], sp["mem_ln_b"], wb["w_xk"], wb["w_xv"])
    (r2, h2b, qxb, oxb, lse_x), couts = _xattn_fwd(
        h1b, r1, kx, vx, wb["w_xq"], wb["w_xo"], sp["ln1_g"], sp["ln1_b"], sp["ln2_g"], sp["ln2_b"], t=t_mm,
        comm=plan.gather("xattn"))
    plan.gathered("xattn", couts, wb)
    g = _mm(h2b, wb["w_gate"], mode="nt", out_dtype=F32, tm=t_mm, tn=D_FF, name="ff_gate")
    u = _mm(h2b, wb["w_up"], mode="nt", out_dtype=F32, tm=t_mm, tn=D_FF, name="ff_up")
    tb, dr3, dr3b, st3 = _ffn_out(g, u, sp["conv_w"], sp["conv_b"], wb["w_down"], r2, target, sp["ln2_g"],
                                  sp["ln2_b"], sp["ln3_g"], sp["ln3_b"], t=t_row)

    grads = {}
    du, dgc, st_conv = _conv_bwd_a(dr3b, wb["w_down"], g, u, sp["conv_w"], sp["conv_b"], t=t_row)
    tk = min(2048, s)
    grads["w_down"] = _mm(tb, dr3b, mode="tn", out_dtype=BF16, tm=D_FF // 2, tn=D_MODEL, tk=tk, name="dw_down")
    grads["w_up"] = _mm(du, h2b, mode="tn", out_dtype=BF16, tm=D_FF // 2, tn=D_MODEL, tk=tk, name="dw_up")
    (dg, dr2, dr2b, st2), couts = _dh2_ln2(dgc, sp["conv_w"], du, wb["w_gate"], wb["w_up"], dr3, r2, sp["ln2_g"],
                                           t=t_mm, comm=plan.exchange("dh2", grads))
    plan.exchanged("dh2", couts)
    grads["w_gate"] = _mm(dg, h2b, mode="tn", out_dtype=BF16, tm=D_FF // 2, tn=D_MODEL, tk=tk, name="dw_gate")

    (dr1, dr1b, dqxb, dkx, dvx, st1), couts = _xattn_bwd(
        dr2, qxb, oxb, lse_x, kx, vx, wb["w_xq"], wb["w_xo"], r1, sp["ln1_g"], t=t_mm,
        comm=plan.exchange("xattn", grads))
    plan.exchanged("xattn", couts)
    grads["w_xo"] = _mm(oxb, dr2b, mode="tn", out_dtype=BF16, tm=D_MODEL, tn=D_MODEL, tk=tk, name="dw_xo")
    grads["w_xq"] = _mm(h1b, dqxb, mode="tn", out_dtype=BF16, tm=D_MODEL, tn=D_MODEL, tk=tk, name="dw_xq")
    grads["w_xk"], grads["w_xv"], st_mem = _mem_bwd(dkx, dvx, mem, sp["mem_ln_g"], sp["mem_ln_b"],
                                                    wb["w_xk"], wb["w_xv"])

    grads["w_mix_out"] = _mm(mixed_b, dr1b, mode="tn", out_dtype=BF16, tm=D_MODEL, tn=D_MODEL, tk=tk,
                             name="dw_mix")
    do_a, do_b, dl_a, dl_b, st_mix = _combine_bwd(dr1b, wb["w_mix_out"], out_a, out_b, sp["g_win"], sp["g_dil"],
                                                  t=t_row)
    (dqa, dka, dva, dsink), couts = _swa_bwd_p(
        za, do_a, lse_a, _stats_to_rows(dl_a, 1), cs[None], e_mat, qcol=0, kcol=4, vcol=5, hq=WIN_Q_HEADS,
        hkv=WIN_KV_HEADS, w=WIN_HALF, tq=2 * tq_a, sub=max(1, sub_a // 2), sink=sp["attn_sink"], name="attn_a_bwd",
        comm=plan.exchange("attn_a", grads))
    plan.exchanged("attn_a", couts)
    dqs, dks, dvs = [], [], []
    for gi, dil in enumerate(DILATIONS):
        (dq, dk, dv), couts = _swa_bwd_p(
            zb[gi], do_b[gi], _stats_to_rows(lse_b, dil), _stats_to_rows(dl_b, dil),
            _to_residue(cs, dil), e_mat, qcol=0, kcol=1, vcol=2, hq=DIL_SLOTS, hkv=DIL_SLOTS, w=DIL_HALF, tq=tq_b,
            sub=subs_b[gi], sink=None, name=f"attn_b{gi}_bwd", comm=plan.exchange(f"attn_b{gi}", grads))
        plan.exchanged(f"attn_b{gi}", couts)
        dqs.append(dq)
        dks.append(dk)
        dvs.append(dv)
    dz = _assemble_dz(dqa, dka, dva, dqs, dks, dvs, t=t_mm)
    grads["w_in"] = _mm(dz, h0b, mode="tn", out_dtype=BF16, tm=IN_WIDTH // 7, tn=D_MODEL, tk=tk, name="dw_in")
    (grad_x, st0), couts = _dh0_ln_in(dz, wb["w_in"], dr1, x, sp["ln_in_g"], t=t_mm,
                                      comm=plan.exchange("dh0", grads))
    plan.exchanged("dh0", couts)

    small = {
        "loss": st3[2:3, 0:1],
        "ln_in_g": st0[0:1], "ln_in_b": st0[1:2],
        "attn_sink": dsink[:, 0].reshape(1, WIN_Q_HEADS),
        "g_win": st_mix[0:1], "g_dil": st_mix[1:2],
        "ln1_g": st1[0:1], "ln1_b": st1[1:2],
        "mem_ln_g": st_mem[0:1], "mem_ln_b": st_mem[1:2],
        "ln2_g": st2[0:1], "ln2_b": st2[1:2],
        "conv_w": st_conv[0:3], "conv_b": st_conv[3:4],
        "ln3_g": st3[0:1], "ln3_b": st3[1:2],
    }
    return grad_x, grads, small


class _SiblingSwap:
    def __init__(self, arrays):
        self.inputs = list(arrays)
        n = len(arrays)
        self.out_shape = [jax.ShapeDtypeStruct(a.shape, a.dtype) for a in arrays]
        self.scratch = [pltpu.SemaphoreType.DMA((n,)), pltpu.SemaphoreType.DMA((n,))]

    def _copies(self, src, dst, sems):
        send_sems, recv_sems = sems
        x, y, c, _ = _place()
        return [pltpu.make_async_remote_copy(
            src_ref=src[a], dst_ref=dst[a], send_sem=send_sems.at[a], recv_sem=recv_sems.at[a],
            device_id=(x, y, 1 - c), device_id_type=MESH_IDS) for a in range(len(src))]

    def start(self, src, dst, sems):
        for cp in self._copies(src, dst, sems):
            cp.start()

    def wait(self, src, dst, sems):
        copies = self._copies(src, dst, sems)
        for cp in copies:
            cp.wait_recv()
        for cp in copies:
            cp.wait_send()


class _Both:
    def __init__(self, first, second):
        self.parts = (first, second)
        self.inputs = first.inputs + second.inputs
        self.out_shape = first.out_shape + second.out_shape
        self.scratch = first.scratch + second.scratch

    def _split(self, src, dst, sems):
        a = self.parts[0]
        ni, no, ns = len(a.inputs), len(a.out_shape), len(a.scratch)
        return ((src[:ni], dst[:no], sems[:ns]), (src[ni:], dst[no:], sems[ns:]))

    def start(self, src, dst, sems):
        for part, args in zip(self.parts, self._split(src, dst, sems)):
            part.start(*args)

    def wait(self, src, dst, sems):
        for part, args in zip(self.parts, self._split(src, dst, sems)):
            part.wait(*args)


def _row_tile(rows, cols, itemsize=4, budget=1 << 20):
    best = None
    for t in range(16, rows + 1, 16):
        if rows % t == 0 and t * cols * itemsize <= budget:
            best = t
    return best or rows


def _sum_slots(stack, *, name):
    n, r, c = stack.shape
    t = _row_tile(r, c)

    def body(s_ref, o_ref):
        acc = s_ref[0].astype(F32)
        for q in range(1, n):
            acc = acc + s_ref[q].astype(F32)
        o_ref[...] = acc

    return pl.pallas_call(
        body, name=name, grid=(r // t,), in_specs=[pl.BlockSpec((n, t, c), lambda i: (0, i, 0))],
        out_specs=pl.BlockSpec((t, c), lambda i: (i, 0)), out_shape=jax.ShapeDtypeStruct((r, c), F32),
        compiler_params=_cparams(dimension_semantics=("parallel",)),
    )(stack)


def _adamw(w, m, v, p, q, *, name):
    r, c = w.shape
    t = _row_tile(r, c, budget=1 << 20)

    def total(ref):
        if len(ref.shape) == 2:
            return ref[...]
        acc = ref[0].astype(F32)
        for slot in range(1, ref.shape[0]):
            acc = acc + ref[slot].astype(F32)
        return acc

    def body(*refs):
        if q is None:
            w_ref, m_ref, v_ref, p_ref, g_ref, d_ref, nm_ref, nv_ref = refs
            g = total(p_ref)
        else:
            w_ref, m_ref, v_ref, p_ref, q_ref, g_ref, d_ref, nm_ref, nv_ref = refs
            g = total(p_ref) + total(q_ref)
        nm = ADAM_B1 * m_ref[...] + (1.0 - ADAM_B1) * g
        nv = ADAM_B2 * v_ref[...] + (1.0 - ADAM_B2) * (g * g)
        m_hat = nm / (1.0 - ADAM_B1 ** ADAM_STEP)
        v_hat = nv / (1.0 - ADAM_B2 ** ADAM_STEP)
        g_ref[...] = g
        d_ref[...] = -ADAM_LR * (m_hat / (jnp.sqrt(v_hat) + ADAM_EPS) + ADAM_WD * w_ref[...])
        nm_ref[...] = nm
        nv_ref[...] = nv

    tile = pl.BlockSpec((t, c), lambda i: (i, 0))
    args = [w, m, v, p] + ([] if q is None else [q])
    in_specs = [tile if a.ndim == 2 else pl.BlockSpec((a.shape[0], t, c), lambda i: (0, i, 0)) for a in args]
    sh = jax.ShapeDtypeStruct((r, c), F32)
    return pl.pallas_call(
        body, name=name, grid=(r // t,), in_specs=in_specs, out_specs=[tile] * 4, out_shape=[sh] * 4,
        compiler_params=_cparams(dimension_semantics=("parallel",)),
    )(*args)


BIG = ("w_in", "w_mix_out", "w_xq", "w_xk", "w_xv", "w_xo", "w_gate", "w_up", "w_down")
COL_SHARDED = ("w_in", "w_gate", "w_up")
WEIGHTS = ("ln_in_g", "ln_in_b", "w_in", "attn_sink", "g_win", "g_dil", "w_mix_out", "ln1_g", "ln1_b",
           "mem_ln_g", "mem_ln_b", "w_xq", "w_xk", "w_xv", "w_xo", "ln2_g", "ln2_b", "w_gate", "w_up",
           "conv_w", "conv_b", "w_down", "ln3_g", "ln3_b")
SMALL = tuple(k for k in WEIGHTS if k not in BIG)
PACK_COLS = 1024
CONV_SHARD = D_FF // N_CHIPS
CONV_WIDTH_ROWS = 3
SMALL_ROWS = 32


GATHER_STAGES = {"ln_in": ("w_in", "conv_w"), "proj": ("w_mix_out", "w_xq", "w_xk", "w_xv", "w_xo", "w_up"),
                 "combine": ("w_down",), "xattn": ("w_gate",)}
EXCHANGE_STAGES = {"dh2": ("w_down",), "xattn": ("w_up",), "attn_a": ("w_gate", "w_xo", "w_xq"),
                   "attn_b0": ("w_xk", "w_xv", "w_mix_out"), "dh0": ("w_in",)}


def _full_weight(k, g4):
    return g4.reshape(N_CHIPS * g4.shape[1], g4.shape[2])


def _grad_parts(k, gk):
    gk = gk.astype(BF16)
    return gk.reshape(N_CHIPS, gk.shape[0] // N_CHIPS, gk.shape[1])


EARLY_SWAP_STAGE = "attn_b2"


class _Plan:
    def __init__(self, shards):
        self.shards = shards
        self.recv = {}
        self.chip_sums = {}
        self.sibling_sums = {}

    def gather(self, stage):
        names = GATHER_STAGES.get(stage)
        return _ChipGather([self.shards[k] for k in names]) if names else None

    def gathered(self, stage, couts, wb):
        for k, g4 in zip(GATHER_STAGES.get(stage, ()), couts):
            if k == "conv_w":
                taps = g4[:, :CONV_WIDTH_ROWS, :CONV_SHARD]
                wb[k] = taps.transpose(1, 0, 2).reshape(CONV_WIDTH_ROWS, D_FF)
            else:
                wb[k] = _full_weight(k, g4)

    def exchange(self, stage, grads):
        if stage == EARLY_SWAP_STAGE:
            self.early = [k for k in BIG if k in self.recv]
            for k in self.early:
                self.chip_sums[k] = self.recv[k]
            return _SiblingSwap([self.chip_sums[k] for k in self.early])
        names = EXCHANGE_STAGES.get(stage)
        return _ChipExchange([_grad_parts(k, grads[k]) for k in names]) if names else None

    def exchanged(self, stage, couts):
        if stage == EARLY_SWAP_STAGE:
            self.sibling_sums.update(zip(self.early, couts))
            return
        for k, r4 in zip(EXCHANGE_STAGES.get(stage, ()), couts):
            self.recv[k] = r4


def _pack_rows(a):
    r, n = a.shape
    per = -(-n // PACK_COLS)
    return jnp.pad(a, ((0, 0), (0, per * PACK_COLS - n))).reshape(r * per, PACK_COLS)


def _unpack_rows(p, r, n):
    per = -(-n // PACK_COLS)
    return p.reshape(r, per * PACK_COLS)[:, :n]


def _pack(pieces, rows_total):
    cat = jnp.concatenate([_pack_rows(a) for a in pieces], axis=0)
    return jnp.pad(cat, ((0, rows_total - cat.shape[0]), (0, 0)))


def _unpack(p, shapes):
    out, at = [], 0
    for r, n in shapes:
        per = -(-n // PACK_COLS)
        out.append(_unpack_rows(p[at:at + r * per], r, n))
        at += r * per
    return out


def kernel(x, mem, positions, ln_in_g, ln_in_b, w_in, attn_sink, g_win, g_dil, w_mix_out, ln1_g, ln1_b, mem_ln_g, mem_ln_b, w_xq, w_xk, w_xv, w_xo, ln2_g, ln2_b, w_gate, w_up, conv_w, conv_b, w_down, ln3_g, ln3_b, loss_target, m_ln_in_g, m_ln_in_b, m_w_in, m_attn_sink, m_g_win, m_g_dil, m_w_mix_out, m_ln1_g, m_ln1_b, m_mem_ln_g, m_mem_ln_b, m_w_xq, m_w_xk, m_w_xv, m_w_xo, m_ln2_g, m_ln2_b, m_w_gate, m_w_up, m_conv_w, m_conv_b, m_w_down, m_ln3_g, m_ln3_b, v_ln_in_g, v_ln_in_b, v_w_in, v_attn_sink, v_g_win, v_g_dil, v_w_mix_out, v_ln1_g, v_ln1_b, v_mem_ln_g, v_mem_ln_b, v_w_xq, v_w_xk, v_w_xv, v_w_xo, v_ln2_g, v_ln2_b, v_w_gate, v_w_up, v_conv_w, v_conv_b, v_w_down, v_ln3_g, v_ln3_b):
    given = dict(locals())
    shape_of = {k: given[k].shape for k in WEIGHTS}
    as2d = lambda k, a: a.reshape(-1, a.shape[-1]).T if k in COL_SHARDED else a.reshape(-1, a.shape[-1])
    w2 = {k: as2d(k, given[k]) for k in WEIGHTS}
    m2 = {k: as2d(k, given["m_" + k]) for k in WEIGHTS}
    v2 = {k: as2d(k, given["v_" + k]) for k in WEIGHTS}
    chip = 2 * lax.axis_index("x") + lax.axis_index("y")

    shards = {k: w2[k].astype(BF16) for k in BIG}
    shards["conv_w"] = jnp.pad(w2["conv_w"], ((0, 16 - CONV_WIDTH_ROWS), (0, PACK_COLS - CONV_SHARD)))
    plan = _Plan(shards)
    sp = {k: w2[k] for k in SMALL if k != "conv_w"}

    grad_x, grads, small = _local_step(x[0], mem[0], positions[0], loss_target[0], {}, sp, plan)

    small_keys = ("loss",) + SMALL
    small_shapes = [small[k].shape for k in small_keys]
    small_pack = _pack([small[k] for k in small_keys], SMALL_ROWS)
    late = [k for k in BIG if k not in plan.chip_sums]
    for k in late:
        plan.chip_sums[k] = _sum_slots(plan.recv[k], name=f"sum_chips_{k}")
    *late_sibling, small_all = _comm_only(
        _Both(_SiblingSwap([plan.chip_sums[k] for k in late]), _ChipExchange([], small_pack)), "swap_and_small")
    plan.sibling_sums.update(zip(late, late_sibling))
    chip_sums = [plan.chip_sums[k] for k in BIG]
    sibling_sums = [plan.sibling_sums[k] for k in BIG]
    small_sum = _sum_slots(small_all, name="sum_small")
    small_g = dict(zip(small_keys, _unpack(small_sum, small_shapes)))
    loss = small_g["loss"][0, 0]

    res = {}
    for k, p, q in zip(BIG, chip_sums, sibling_sums):
        res[k] = _adamw(w2[k], m2[k], v2[k], p, q, name=f"adamw_{k}")
    small_g["conv_w"] = lax.dynamic_slice_in_dim(small_g["conv_w"], chip * CONV_SHARD, CONV_SHARD, axis=1)
    adam_shapes = [w2[k].shape for k in SMALL]
    packs = [_pack([d[k] for k in SMALL], SMALL_ROWS) for d in (w2, m2, v2, small_g)]
    small_res = [_unpack(o, adam_shapes) for o in _adamw(*packs, None, name="adamw_small")]
    for i, k in enumerate(SMALL):
        res[k] = tuple(o[i] for o in small_res)

    outs = [loss, grad_x[None]]
    for slot in range(4):
        outs += [(res[k][slot].T if k in COL_SHARDED else res[k][slot]).reshape(shape_of[k]) for k in WEIGHTS]
    return tuple(outs)
```

```python
import functools
import math

import jax
import jax.numpy as jnp
from jax import lax
from jax.experimental import pallas as pl
from jax.experimental.pallas import tpu as pltpu

F32 = jnp.float32
BF16 = jnp.bfloat16

D_MODEL = 1024
HEAD_DIM = 64
WIN_Q_HEADS = 8
WIN_KV_HEADS = 2
WIN_HALF = 128
DIL_SLOTS = 8
DILATIONS = (1, 4, 16)
DIL_HALF = 64
ROT_DIM = 16
ROPE_THETA = 500000.0
X_HEADS = 4
X_HEAD_DIM = 256
D_FF = 2816
A_Q = 512
A_KV = 128
A_WIDTH = A_Q + 2 * A_KV
B_QKV = 1536
IN_WIDTH = 5376
ALPHA = 2.0 ** 0.25
LN_EPS = 1e-5
NEG_INF = -1e30
LANES = 128
N_CHIPS = 4
N_DEV = 8

ADAM_LR = 0.001
ADAM_B1 = 0.9
ADAM_B2 = 0.999
ADAM_EPS = 1e-08
ADAM_WD = 0.01
ADAM_STEP = 10

VMEM_LIMIT = 56 * 1024 * 1024


def _cparams(**kw):
    return pltpu.CompilerParams(vmem_limit_bytes=VMEM_LIMIT, **kw)


def _dot(a, b):
    return lax.dot_general(a, b, (((1,), (0,)), ((), ())), preferred_element_type=F32)


def _dot_nt(a, b):
    return lax.dot_general(a, b, (((1,), (1,)), ((), ())), preferred_element_type=F32)


def _dot_tn(a, b):
    return lax.dot_general(a, b, (((0,), (0,)), ((), ())), preferred_element_type=F32)


def _ln(x, g, b):
    mu = jnp.mean(x, axis=-1, keepdims=True)
    xc = x - mu
    var = jnp.mean(xc * xc, axis=-1, keepdims=True)
    return xc * lax.rsqrt(var + LN_EPS) * g + b


def _ln_bwd_math(dy, r, g):
    mu = jnp.mean(r, axis=-1, keepdims=True)
    xc = r - mu
    var = jnp.mean(xc * xc, axis=-1, keepdims=True)
    rstd = lax.rsqrt(var + LN_EPS)
    xhat = xc * rstd
    dxhat = dy * g
    m1 = jnp.mean(dxhat, axis=-1, keepdims=True)
    m2 = jnp.mean(dxhat * xhat, axis=-1, keepdims=True)
    dr = rstd * (dxhat - m1 - xhat * m2)
    return dr, jnp.sum(dy * xhat, axis=0, keepdims=True), jnp.sum(dy, axis=0, keepdims=True)


def _rope(z, ta, tb, tc, sign):
    w = z.shape[1]
    reps = w // LANES
    a = jnp.tile(ta, (1, reps))
    b = jnp.tile(tb, (1, reps))
    c = jnp.tile(tc, (1, reps))
    return z * a + sign * (pltpu.roll(z, w - 8, 1) * b + pltpu.roll(z, 8, 1) * c)


def _shift_rows(x, prev_row, next_row):
    t = x.shape[0]
    sub = 8
    row = lax.broadcasted_iota(jnp.int32, (sub, x.shape[1]), 0)
    down, up = pltpu.roll(x, 1, 0), pltpu.roll(x, t - 1, 0)
    xm1 = jnp.concatenate([jnp.where(row == 0, prev_row, down[:sub]), down[sub:]], axis=0)
    xp1 = jnp.concatenate([up[:t - sub], jnp.where(row == sub - 1, next_row, up[t - sub:])], axis=0)
    return xm1, xp1


def _rope_tabs(cs, e_mat):
    hi = cs.astype(BF16)
    rest = cs - hi.astype(F32)
    mid = rest.astype(BF16)
    lo = (rest - mid.astype(F32)).astype(BF16)
    tabs = _dot(hi, e_mat) + _dot(mid, e_mat) + _dot(lo, e_mat)
    lane = lax.broadcasted_iota(jnp.int32, (cs.shape[0], LANES), 1)
    ones = jnp.where((lane & (HEAD_DIM - 1)) >= ROT_DIM, 1.0, 0.0)
    return tabs[:, :LANES] + ones, tabs[:, LANES:2 * LANES], tabs[:, 2 * LANES:]


def _rope_select_matrix():
    half = ROT_DIM // 2
    e = [[0.0] * (3 * LANES) for _ in range(ROT_DIM)]
    for lane in range(LANES):
        d = lane % HEAD_DIM
        if d < half:
            e[d][lane] = 1.0
            e[half + d][LANES + lane] = -1.0
        elif d < ROT_DIM:
            e[d - half][lane] = 1.0
            e[d][2 * LANES + lane] = 1.0
    return jnp.array(e, BF16)


def _rope_rows(x, cos_t, sin_t, sign):
    half = ROT_DIM // 2
    parts = []
    for base in (0, HEAD_DIM):
        r1, r2 = x[base:base + half], x[base + half:base + ROT_DIM]
        parts += [r1 * cos_t - sign * (r2 * sin_t), r2 * cos_t + sign * (r1 * sin_t), x[base + ROT_DIM:base + HEAD_DIM]]
    return jnp.concatenate(parts, axis=0)


MESH_IDS = pl.DeviceIdType.MESH
ANY = pl.BlockSpec(memory_space=pl.ANY)


def _place():
    x, y, c = lax.axis_index("x"), lax.axis_index("y"), lax.axis_index("c")
    other_chips = [(1 - x, y), (x, 1 - y), (1 - x, 1 - y)]
    return x, y, c, other_chips


class _ChipGather:
    def __init__(self, shards):
        self.inputs = list(shards)
        n = len(shards)
        self.out_shape = [jax.ShapeDtypeStruct((N_CHIPS,) + a.shape, a.dtype) for a in shards]
        self.scratch = [pltpu.SemaphoreType.DMA((6 * n,)), pltpu.SemaphoreType.DMA((6 * n,)),
                        pltpu.SemaphoreType.DMA((n,))]

    def _copies(self, src, dst, sems):
        send_sems, recv_sems, local_sems = sems
        x, y, c, chips = _place()
        mine = 2 * x + y
        n = len(src)
        local, sends, recvs, passes, pass_recvs = [], [], [], [], []
        for a in range(n):
            half = src[a].shape[0] // 2
            my_rows, other_rows = pl.ds(c * half, half), pl.ds((1 - c) * half, half)
            local.append(pltpu.make_async_copy(src[a], dst[a].at[mine], local_sems.at[a]))
            for j, (px, py) in enumerate(chips):
                k, k2, slot = 3 * a + j, 3 * n + 3 * a + j, 2 * px + py
                sends.append(pltpu.make_async_remote_copy(
                    src_ref=src[a].at[my_rows], dst_ref=dst[a].at[mine, my_rows], send_sem=send_sems.at[k],
                    recv_sem=recv_sems.at[k], device_id=(px, py, c), device_id_type=MESH_IDS))
                recvs.append(pltpu.make_async_remote_copy(
                    src_ref=src[a].at[my_rows], dst_ref=dst[a].at[slot, my_rows], send_sem=send_sems.at[k],
                    recv_sem=recv_sems.at[k], device_id=(px, py, c), device_id_type=MESH_IDS))
                passes.append(pltpu.make_async_remote_copy(
                    src_ref=dst[a].at[slot, my_rows], dst_ref=dst[a].at[slot, my_rows], send_sem=send_sems.at[k2],
                    recv_sem=recv_sems.at[k2], device_id=(x, y, 1 - c), device_id_type=MESH_IDS))
                pass_recvs.append(pltpu.make_async_remote_copy(
                    src_ref=dst[a].at[slot, my_rows], dst_ref=dst[a].at[slot, other_rows],
                    send_sem=send_sems.at[k2], recv_sem=recv_sems.at[k2], device_id=(x, y, 1 - c),
                    device_id_type=MESH_IDS))
        return local, sends, recvs, passes, pass_recvs

    def start(self, src, dst, sems):
        local, sends, _, _, _ = self._copies(src, dst, sems)
        for cp in local + sends:
            cp.start()

    def wait(self, src, dst, sems):
        local, sends, recvs, passes, pass_recvs = self._copies(src, dst, sems)
        for idx, landed in enumerate(recvs):
            landed.wait_recv()
            if passes:
                passes[idx].start()
        for cp in pass_recvs:
            cp.wait_recv()
        for cp in sends + passes:
            cp.wait_send()
        for cp in local:
            cp.wait()


class _ChipExchange:
    def __init__(self, parts, small=None):
        self.inputs = list(parts) + ([small] if small is not None else [])
        self.n = len(parts)
        self.has_small = small is not None
        self.out_shape = [jax.ShapeDtypeStruct(a.shape, a.dtype) for a in parts]
        n_sem, n_loc = 3 * self.n, self.n
        if self.has_small:
            self.out_shape.append(jax.ShapeDtypeStruct((N_DEV,) + small.shape, small.dtype))
            n_sem, n_loc = n_sem + N_DEV - 1, n_loc + 1
        self.scratch = [pltpu.SemaphoreType.DMA((n_sem,)), pltpu.SemaphoreType.DMA((n_sem,)),
                        pltpu.SemaphoreType.DMA((n_loc,))]

    def _copies(self, src, dst, sems):
        send_sems, recv_sems, local_sems = sems
        x, y, c, chips = _place()
        mine = 2 * x + y
        n = self.n
        local, sends, recvs = [], [], []
        for a in range(n):
            local.append(pltpu.make_async_copy(src[a].at[mine], dst[a].at[mine], local_sems.at[a]))
            for j, (px, py) in enumerate(chips):
                k = 3 * a + j
                sends.append(pltpu.make_async_remote_copy(
                    src_ref=src[a].at[2 * px + py], dst_ref=dst[a].at[mine], send_sem=send_sems.at[k],
                    recv_sem=recv_sems.at[k], device_id=(px, py, c), device_id_type=MESH_IDS))
                recvs.append(pltpu.make_async_remote_copy(
                    src_ref=src[a].at[mine], dst_ref=dst[a].at[2 * px + py], send_sem=send_sems.at[k],
                    recv_sem=recv_sems.at[k], device_id=(px, py, c), device_id_type=MESH_IDS))
        if self.has_small:
            me_dev = 4 * x + 2 * y + c
            local.append(pltpu.make_async_copy(src[n], dst[n].at[me_dev], local_sems.at[n]))
            for mask in range(1, N_DEV):
                px, py, pc = x ^ ((mask >> 2) & 1), y ^ ((mask >> 1) & 1), c ^ (mask & 1)
                k = 3 * n + mask - 1
                sends.append(pltpu.make_async_remote_copy(
                    src_ref=src[n], dst_ref=dst[n].at[me_dev], send_sem=send_sems.at[k], recv_sem=recv_sems.at[k],
                    device_id=(px, py, pc), device_id_type=MESH_IDS))
                recvs.append(pltpu.make_async_remote_copy(
                    src_ref=src[n], dst_ref=dst[n].at[4 * px + 2 * py + pc], send_sem=send_sems.at[k],
                    recv_sem=recv_sems.at[k], device_id=(px, py, pc), device_id_type=MESH_IDS))
        return local, sends, recvs, [], []

    start = _ChipGather.start
    wait = _ChipGather.wait


def _pcall(body, *, name, grid, in_specs, out_specs, out_shape, args, scratch_shapes=(), dims=None, comm=None):
    in_specs, out_specs, out_shape = list(in_specs), list(out_specs), list(out_shape)
    scratch_shapes = list(scratch_shapes)
    if comm is None:
        outs = pl.pallas_call(
            body, name=name, grid=grid, in_specs=in_specs, out_specs=out_specs, out_shape=out_shape,
            scratch_shapes=scratch_shapes, compiler_params=_cparams(dimension_semantics=dims),
        )(*args)
        return list(outs), []
    n_in, n_out, n_scr = len(in_specs), len(out_specs), len(scratch_shapes)
    n_cin, n_cout = len(comm.inputs), len(comm.out_shape)

    def wrapped(*refs):
        ins, refs = refs[:n_in], refs[n_in:]
        cins, refs = refs[:n_cin], refs[n_cin:]
        outs, refs = refs[:n_out], refs[n_out:]
        couts, refs = refs[:n_cout], refs[n_cout:]
        scr, csems = refs[:n_scr], refs[n_scr:]
        first = last = None
        for axis, size in enumerate(grid):
            pid = pl.program_id(axis)
            f, l = pid == 0, pid == size - 1
            first = f if first is None else first & f
            last = l if last is None else last & l

        @pl.when(first)
        def _():
            comm.start(cins, couts, csems)

        body(*ins, *outs, *scr)

        @pl.when(last)
        def _():
            comm.wait(cins, couts, csems)

    res = pl.pallas_call(
        wrapped, name=name, grid=grid, in_specs=in_specs + [ANY] * n_cin, out_specs=out_specs + [ANY] * n_cout,
        out_shape=out_shape + list(comm.out_shape), scratch_shapes=scratch_shapes + list(comm.scratch),
        compiler_params=_cparams(dimension_semantics=("arbitrary",) * len(grid)),
    )(*args, *comm.inputs)
    return list(res[:n_out]), list(res[n_out:])


def _comm_only(comm, name):
    def body(*refs):
        n_cin, n_cout = len(comm.inputs), len(comm.out_shape)
        cins, couts, csems = refs[:n_cin], refs[n_cin:n_cin + n_cout], refs[n_cin + n_cout:]
        comm.start(cins, couts, csems)
        comm.wait(cins, couts, csems)

    return list(pl.pallas_call(
        body, name=name, in_specs=[ANY] * len(comm.inputs), out_specs=[ANY] * len(comm.out_shape),
        out_shape=list(comm.out_shape), scratch_shapes=list(comm.scratch),
    )(*comm.inputs))


def _mm(a, b, *, mode, out_dtype, tm, tn, tk=None, name):
    if mode == "nt":
        m, k = a.shape
        n = b.shape[0]
        assert m % tm == 0 and n % tn == 0

        def body(a_ref, b_ref, o_ref):
            o_ref[...] = _dot_nt(a_ref[...], b_ref[...]).astype(out_dtype)

        return pl.pallas_call(
            body, name=name, grid=(m // tm, n // tn),
            in_specs=[pl.BlockSpec((tm, k), lambda i, j: (i, 0)), pl.BlockSpec((tn, k), lambda i, j: (j, 0))],
            out_specs=pl.BlockSpec((tm, tn), lambda i, j: (i, j)),
            out_shape=jax.ShapeDtypeStruct((m, n), out_dtype),
            compiler_params=_cparams(dimension_semantics=("parallel", "parallel")),
        )(a, b)
    assert mode == "tn"
    kk, m = a.shape
    n = b.shape[1]
    assert m % tm == 0 and n % tn == 0 and kk % tk == 0
    nk = kk // tk

    def body(a_ref, b_ref, o_ref, acc_ref):
        kstep = pl.program_id(2)

        @pl.when(kstep == 0)
        def _():
            acc_ref[...] = jnp.zeros_like(acc_ref)

        acc_ref[...] += _dot_tn(a_ref[...], b_ref[...])

        @pl.when(kstep == nk - 1)
        def _():
            o_ref[...] = acc_ref[...].astype(out_dtype)

    return pl.pallas_call(
        body, name=name, grid=(m // tm, n // tn, nk),
        in_specs=[pl.BlockSpec((tk, tm), lambda i, j, s: (s, i)), pl.BlockSpec((tk, tn), lambda i, j, s: (s, j))],
        out_specs=pl.BlockSpec((tm, tn), lambda i, j, s: (i, j)),
        out_shape=jax.ShapeDtypeStruct((m, n), out_dtype),
        scratch_shapes=[pltpu.VMEM((tm, tn), F32)],
        compiler_params=_cparams(dimension_semantics=("parallel", "parallel", "arbitrary")),
    )(a, b)


PROJ_COLS = 256


def _proj_segments():
    wd = DIL_SLOTS * HEAD_DIM
    segs = [(1, [(0, 1), (PROJ_COLS, 1), (2 * PROJ_COLS, 2)])]
    for gi, dil in enumerate(DILATIONS):
        blocks = []
        for part, kind in enumerate((1, 1, 0)):
            col = A_WIDTH + part * B_QKV + gi * wd
            blocks += [(col, kind), (col + PROJ_COLS, kind)]
        segs.append((dil, blocks))
    return segs


PROJ_SEGMENTS = _proj_segments()


def _dh0_ln_in(dz, w_t, dr1, x, ln_in_g, *, t, comm=None):
    s, k = dz.shape

    def body(dz_ref, w_ref, dr1_ref, x_ref, g_ref, gx_ref, st_ref):
        i = pl.program_id(0)

        @pl.when(i == 0)
        def _():
            st_ref[...] = jnp.zeros_like(st_ref)

        dh0 = _dot(dz_ref[...], w_ref[...]) + ALPHA * dr1_ref[...]
        dx, dg, db = _ln_bwd_math(dh0, x_ref[...], g_ref[...])
        gx_ref[...] = dx
        st_ref[0:1, :] += dg
        st_ref[1:2, :] += db

    tile = pl.BlockSpec((t, D_MODEL), lambda i: (i, 0))
    return _pcall(
        body, name="dh0_ln_in", grid=(s // t,),
        in_specs=[pl.BlockSpec((t, k), lambda i: (i, 0)),
                  pl.BlockSpec((k, D_MODEL), lambda i: (0, 0), pipeline_mode=pl.Buffered(1)),
                  tile, tile, pl.BlockSpec((1, D_MODEL), lambda i: (0, 0))],
        out_specs=[tile, pl.BlockSpec((8, D_MODEL), lambda i: (0, 0))],
        out_shape=[jax.ShapeDtypeStruct((s, D_MODEL), F32), jax.ShapeDtypeStruct((8, D_MODEL), F32)],
        args=[dz, w_t, dr1, x, ln_in_g], dims=("arbitrary",), comm=comm)


def _ln_in_fwd(x, g, b, *, t, comm=None):
    s = x.shape[0]

    def body(x_ref, g_ref, b_ref, o_ref):
        o_ref[...] = _ln(x_ref[...], g_ref[...], b_ref[...]).astype(BF16)

    row = pl.BlockSpec((1, D_MODEL), lambda i: (0, 0))
    tile = pl.BlockSpec((t, D_MODEL), lambda i: (i, 0))
    outs, couts = _pcall(body, name="ln_in_fwd", grid=(s // t,), in_specs=[tile, row, row], out_specs=[tile],
                         out_shape=[jax.ShapeDtypeStruct((s, D_MODEL), BF16)], args=[x, g, b], dims=("parallel",),
                         comm=comm)
    return outs[0], couts


def _proj_all(h0b, w_t, cs, e_mat, *, t, comm=None):
    s = h0b.shape[0]
    cb = PROJ_COLS
    halves = cb // LANES

    def body(h_ref, w_ref, cs_ref, e_ref, *rest):
        z_refs, scr = rest[:-1], rest[-1]
        h = h_ref[...]
        ta, tb, tc = (jnp.tile(tab, (1, halves)) for tab in _rope_tabs(cs_ref[...], e_ref[...]))
        lane = lax.broadcasted_iota(jnp.int32, (t, cb), 1)
        slot = 0
        for z_ref, (dil, blocks) in zip(z_refs, PROJ_SEGMENTS):
            for jb, (col, kind) in enumerate(blocks):
                acc = _dot_nt(h, w_ref[col:col + cb, :])
                if kind:
                    z = acc * ta + (pltpu.roll(acc, cb - 8, 1) * tb + pltpu.roll(acc, 8, 1) * tc)
                    if kind == 2:
                        z = jnp.where(lane < LANES, z, acc)
                else:
                    z = acc
                if dil == 1:
                    z_ref[0, :, cb * jb:cb * (jb + 1)] = z.astype(BF16)
                    continue
                for half in range(halves):
                    scr[slot, half] = z[:, half * LANES:(half + 1) * LANES]
                for c in range(dil):
                    for half in range(halves):
                        rows = scr[slot, half, pl.ds(c, t // dil, stride=dil), :]
                        z_ref[c, :, cb * jb + half * LANES:cb * jb + (half + 1) * LANES] = rows.astype(BF16)
                slot = 1 - slot

    widths = [cb * len(blocks) for _, blocks in PROJ_SEGMENTS]
    dils = [dil for dil, _ in PROJ_SEGMENTS]
    outs, couts = _pcall(
        body, name="proj_all", grid=(s // t,),
        in_specs=[pl.BlockSpec((t, D_MODEL), lambda i: (i, 0)),
                  pl.BlockSpec((IN_WIDTH, D_MODEL), lambda i: (0, 0), pipeline_mode=pl.Buffered(1)),
                  pl.BlockSpec((t, ROT_DIM), lambda i: (i, 0)), pl.BlockSpec((ROT_DIM, 3 * LANES), lambda i: (0, 0))],
        out_specs=[pl.BlockSpec((dil, t // dil, wd), lambda i: (0, i, 0)) for dil, wd in zip(dils, widths)],
        out_shape=[jax.ShapeDtypeStruct((dil, s // dil, wd), BF16) for dil, wd in zip(dils, widths)],
        args=[h0b, w_t, cs, e_mat], scratch_shapes=[pltpu.VMEM((2, halves, t, LANES), F32)],
        dims=("parallel",), comm=comm)
    return outs, couts


PAIR = 2 * HEAD_DIM


def _place_head(x2, src_pos, dst_pos):
    hi = lax.broadcasted_iota(jnp.int32, x2.shape, 1) >= HEAD_DIM
    src = x2 if src_pos == dst_pos else pltpu.roll(x2, HEAD_DIM, 1)
    return jnp.where(hi == (dst_pos == 1), src, jnp.zeros_like(src))


def _band_mask_t(row0, tq, w, seq_len):
    tk = tq + 2 * w
    kk = lax.broadcasted_iota(jnp.int32, (tk, tq), 0)
    qq = lax.broadcasted_iota(jnp.int32, (tk, tq), 1)
    kpos = row0 - w + kk
    return (jnp.abs(qq + w - kk) <= w) & (kpos >= 0) & (kpos < seq_len)


def _halo_kv_specs(t, w, hkv, n, seq_len, kcol, vcol):
    kw = hkv * HEAD_DIM
    per, last = t // w, seq_len // w - 1
    cur = lambda s, i: jnp.minimum(i, n - 1)
    specs = []
    for c in (kcol, vcol):
        specs += [pl.BlockSpec((None, w, kw), lambda s, i, c=c: (s, jnp.maximum(cur(s, i) * per - 1, 0), c)),
                  pl.BlockSpec((None, t, kw), lambda s, i, c=c: (s, cur(s, i), c)),
                  pl.BlockSpec((None, w, kw), lambda s, i, c=c: (s, jnp.minimum((cur(s, i) + 1) * per, last), c))]
    return specs, cur


def _pair_kv(kfull, vfull, qp, rep, krows):
    ks, vs, a_of = [], [], []
    for pos in range(2):
        g = (2 * qp + pos) // rep
        a_of.append(g // 2)
        ks.append(_place_head(kfull[g // 2][krows], g % 2, pos))
        vs.append(_place_head(vfull[g // 2][krows], g % 2, pos))
    assert a_of[0] == a_of[1]
    return jnp.concatenate(ks, axis=0), jnp.concatenate(vs, axis=0), a_of[0]


def _swa_fwd_p(qkv, *, qcol, kcol, vcol, hq, hkv, w, tq, sub, sink, name, comm=None):
    nseq, seq_len, _ = qkv.shape
    t = tq * sub
    n = seq_len // t
    rep = hq // hkv
    tk = tq + 2 * w
    kv_specs, cur = _halo_kv_specs(t, w, hkv, n, seq_len, kcol, vcol)

    def body(*refs):
        if sink is not None:
            sink_ref, refs = refs[0], refs[1:]
        q_ref, kp_ref, kc_ref, kn_ref, vp_ref, vc_ref, vn_ref, o_ref, lse_ref = refs
        i = pl.program_id(1)
        kfull, vfull = [], []
        for a in range(hkv // 2):
            ls = slice(a * PAIR, (a + 1) * PAIR)
            kfull.append(jnp.concatenate([kp_ref[:, ls], kc_ref[:, ls], kn_ref[:, ls]], axis=0) * 0.125)
            vfull.append(jnp.concatenate([vp_ref[:, ls], vc_ref[:, ls], vn_ref[:, ls]], axis=0))
        row_hi = lax.broadcasted_iota(jnp.int32, (PAIR, tq), 0) >= HEAD_DIM
        for jj in range(sub):
            rows = slice(jj * tq, (jj + 1) * tq)
            mask_t = _band_mask_t(i * t + jj * tq, tq, w, seq_len)
            o_t, lse_rows = [], []
            for qp in range(hq // 2):
                kst, vst, _ = _pair_kv(kfull, vfull, qp, rep, slice(jj * tq, jj * tq + tk))
                s2 = _dot_nt(kst, q_ref[rows, qp * PAIR:(qp + 1) * PAIR])
                ps, dens = [], []
                for pos in range(2):
                    h = 2 * qp + pos
                    s_t = jnp.where(mask_t, s2[pos * tk:(pos + 1) * tk], NEG_INF)
                    m = jnp.max(s_t, axis=0, keepdims=True)
                    if sink is not None:
                        m = jnp.maximum(m, sink_ref[0, h])
                    p_t = jnp.exp(s_t - m)
                    den = jnp.sum(p_t, axis=0, keepdims=True)
                    if sink is not None:
                        den = den + jnp.exp(sink_ref[0, h] - m)
                    ps.append(p_t.astype(BF16))
                    dens.append(den)
                    lse_rows.append(m + jnp.log(den))
                both = _dot_tn(vst, jnp.concatenate(ps, axis=0))
                o_t.append(both / jnp.where(row_hi, dens[1], dens[0]))
            o_ref[rows, :] = jnp.concatenate(o_t, axis=0).T
            lse_ref[:, rows] = jnp.concatenate(lse_rows, axis=0)

    in_specs = [pl.BlockSpec((None, t, hq * HEAD_DIM), lambda s, i: (s, i, qcol))] + kv_specs
    args = [qkv] * 7
    if sink is not None:
        in_specs = [pl.BlockSpec(memory_space=pltpu.SMEM)] + in_specs
        args = [sink] + args
    (o, lse), couts = _pcall(
        body, name=name, grid=(nseq, n), in_specs=in_specs,
        out_specs=[pl.BlockSpec((None, t, hq * HEAD_DIM), lambda s, i: (s, i, 0)),
                   pl.BlockSpec((None, hq, t), lambda s, i: (s, 0, i))],
        out_shape=[jax.ShapeDtypeStruct((nseq, seq_len, hq * HEAD_DIM), F32),
                   jax.ShapeDtypeStruct((nseq, hq, seq_len), F32)],
        args=args, dims=("parallel", "parallel"), comm=comm)
    return o, lse, couts


def _swa_bwd_p(qkv, do, lse, delta, cs, e_mat, *, qcol, kcol, vcol, hq, hkv, w, tq, sub, sink, name, comm=None):
    nseq, seq_len, _ = qkv.shape
    t = tq * sub
    n = seq_len // t
    rep = hq // hkv
    qw, kw = hq * HEAD_DIM, hkv * HEAD_DIM
    tk = tq + 2 * w
    kv_specs, cur = _halo_kv_specs(t, w, hkv, n, seq_len, kcol, vcol)

    def body(*refs):
        if sink is not None:
            sink_ref, refs = refs[0], refs[1:]
        (q_ref, kp_ref, kc_ref, kn_ref, vp_ref, vc_ref, vn_ref, do_ref, lse_ref, dl_ref,
         cs_c, cs_p, e_ref) = refs[:13]
        outs = refs[13:]
        if sink is not None:
            dq_ref, dk_ref, dv_ref, dsink_ref, dk_acc, dv_acc, dk_win, dv_win = outs
        else:
            dq_ref, dk_ref, dv_ref, dk_acc, dv_acc, dk_win, dv_win = outs
        s_id = pl.program_id(0)
        i = pl.program_id(1)
        slot_p, slot_c, slot_n = (i + 2) % 3, i % 3, (i + 1) % 3

        if sink is not None:
            @pl.when((s_id == 0) & (i == 0))
            def _():
                dsink_ref[...] = jnp.zeros_like(dsink_ref)

        @pl.when(i < n)
        def _():
            dk_win[...] = jnp.zeros_like(dk_win)
            dv_win[...] = jnp.zeros_like(dv_win)
            kfull, vfull = [], []
            for a in range(hkv // 2):
                ls = slice(a * PAIR, (a + 1) * PAIR)
                kfull.append(jnp.concatenate([kp_ref[:, ls], kc_ref[:, ls], kn_ref[:, ls]], axis=0) * 0.125)
                vfull.append(jnp.concatenate([vp_ref[:, ls], vc_ref[:, ls], vn_ref[:, ls]], axis=0))
            for jj in range(sub):
                rows = slice(jj * tq, (jj + 1) * tq)
                krows = slice(jj * tq, jj * tq + tk)
                mask_t = _band_mask_t(i * t + jj * tq, tq, w, seq_len)
                dq_t = []
                dk2 = [None] * (hkv // 2)
                dv2 = [None] * (hkv // 2)
                for qp in range(hq // 2):
                    kst, vst, a = _pair_kv(kfull, vfull, qp, rep, krows)
                    q2 = q_ref[rows, qp * PAIR:(qp + 1) * PAIR]
                    do2 = do_ref[rows, qp * PAIR:(qp + 1) * PAIR]
                    s2 = _dot_nt(kst, q2)
                    dp2 = _dot_nt(vst, do2)
                    ds, ps, q_at, do_at = [], [], [], []
                    for pos in range(2):
                        h = 2 * qp + pos
                        e = (h // rep) % 2
                        half = slice(pos * tk, (pos + 1) * tk)
                        lse_h = lse_ref[h:h + 1, rows]
                        dl_h = dl_ref[h:h + 1, rows]
                        p_t = jnp.exp(jnp.where(mask_t, s2[half], NEG_INF) - lse_h)
                        ds.append((p_t * (dp2[half] - dl_h)).astype(BF16))
                        ps.append(p_t.astype(BF16))
                        q_at.append(_place_head(q2, pos, e) * 0.125)
                        do_at.append(_place_head(do2, pos, e))
                        if sink is not None:
                            ds_sink = -jnp.sum(jnp.exp(sink_ref[0, h] - lse_h) * dl_h)
                            dsink_ref[h:h + 1, :] += jnp.full((1, LANES), ds_sink, F32)
                    dq_t.append(_rope_rows(_dot_tn(kst, jnp.concatenate(ds, axis=0)),
                                           cs_c[0:ROT_DIM // 2, rows], cs_c[ROT_DIM // 2:ROT_DIM, rows], -1.0))
                    dk_part = _dot(jnp.concatenate(ds, axis=1), jnp.concatenate(q_at, axis=0))
                    dv_part = _dot(jnp.concatenate(ps, axis=1), jnp.concatenate(do_at, axis=0))
                    dk2[a] = dk_part if dk2[a] is None else dk2[a] + dk_part
                    dv2[a] = dv_part if dv2[a] is None else dv2[a] + dv_part
                for a in range(hkv // 2):
                    ls = slice(a * PAIR, (a + 1) * PAIR)
                    dk_win[krows, ls] += dk2[a]
                    dv_win[krows, ls] += dv2[a]
                dq_ref[rows, :] = jnp.concatenate(dq_t, axis=0).T.astype(BF16)

            @pl.when(i > 0)
            def _():
                dk_acc[slot_p, t - w:, :] += dk_win[:w, :]
                dv_acc[slot_p, t - w:, :] += dv_win[:w, :]

            @pl.when(i == 0)
            def _():
                dk_acc[slot_c] = dk_win[w:w + t, :]
                dv_acc[slot_c] = dv_win[w:w + t, :]

            @pl.when(i > 0)
            def _():
                dk_acc[slot_c] += dk_win[w:w + t, :]
                dv_acc[slot_c] += dv_win[w:w + t, :]

            dk_acc[slot_n] = jnp.zeros((t, kw), F32)
            dv_acc[slot_n] = jnp.zeros((t, kw), F32)
            dk_acc[slot_n, :w, :] = dk_win[w + t:, :]
            dv_acc[slot_n, :w, :] = dv_win[w + t:, :]

        @pl.when(i >= 1)
        def _():
            dk_ref[...] = _rope(dk_acc[slot_p], *_rope_tabs(cs_p[...], e_ref[...]), -1.0).astype(BF16)
            dv_ref[...] = dv_acc[slot_p].astype(BF16)

    row_c = lambda width: pl.BlockSpec((None, t, width), lambda s, i: (s, cur(s, i), 0))
    row_p = lambda width: pl.BlockSpec((None, t, width), lambda s, i: (s, jnp.maximum(i - 1, 0), 0))
    stat = pl.BlockSpec((None, hq, t), lambda s, i: (s, 0, cur(s, i)))
    cs_rows = pl.BlockSpec((None, ROT_DIM, t), lambda s, i: (s, 0, cur(s, i)))
    in_specs = ([pl.BlockSpec((None, t, qw), lambda s, i: (s, cur(s, i), qcol))] + kv_specs
                + [row_c(qw), stat, stat, cs_rows, row_p(ROT_DIM),
                   pl.BlockSpec((ROT_DIM, 3 * LANES), lambda s, i: (0, 0))])
    args = [qkv] * 7 + [do, lse, delta, cs.transpose(0, 2, 1), cs, e_mat]
    out_specs = [row_c(qw), row_p(kw), row_p(kw)]
    out_shape = [jax.ShapeDtypeStruct((nseq, seq_len, qw), BF16),
                 jax.ShapeDtypeStruct((nseq, seq_len, kw), BF16),
                 jax.ShapeDtypeStruct((nseq, seq_len, kw), BF16)]
    if sink is not None:
        in_specs = [pl.BlockSpec(memory_space=pltpu.SMEM)] + in_specs
        args = [sink] + args
        out_specs.append(pl.BlockSpec((8, LANES), lambda s, i: (0, 0)))
        out_shape.append(jax.ShapeDtypeStruct((8, LANES), F32))
    return _pcall(
        body, name=name, grid=(nseq, n + 1), in_specs=in_specs, out_specs=out_specs, out_shape=out_shape,
        scratch_shapes=[pltpu.VMEM((3, t, kw), F32), pltpu.VMEM((3, t, kw), F32),
                        pltpu.VMEM((t + 2 * w, kw), F32), pltpu.VMEM((t + 2 * w, kw), F32)], args=args,
        dims=("arbitrary", "arbitrary"), comm=comm)


def _rms_parts(o, g):
    ms = jnp.mean(o * o, axis=-1, keepdims=True) + LN_EPS
    rinv = lax.rsqrt(ms)
    return o * rinv * g, rinv


def _from_subsequences(ref, scr, dil, t):
    slabs = ref.shape[-1] // LANES
    if dil == 1:
        return ref[0].astype(F32)
    for c in range(dil):
        for sl in range(slabs):
            scr[sl, pl.ds(c, t // dil, stride=dil), :] = ref[c, :, sl * LANES:(sl + 1) * LANES].astype(F32)
    return jnp.concatenate([scr[sl] for sl in range(slabs)], axis=1)


def _to_subsequences(val, ref, scr, dil, t):
    slabs = val.shape[-1] // LANES
    if dil == 1:
        ref[0] = val.astype(ref.dtype)
        return
    for sl in range(slabs):
        scr[sl] = val[:, sl * LANES:(sl + 1) * LANES]
    for c in range(dil):
        for sl in range(slabs):
            ref[c, :, sl * LANES:(sl + 1) * LANES] = scr[sl, pl.ds(c, t // dil, stride=dil), :].astype(ref.dtype)


def _combine_fwd(out_a, o_g, lse_g, g_win, g_dil, w_mix_b, x, ln_in_g, ln_in_b, ln1_g, ln1_b, *, t, comm=None):
    s = out_a.shape[1]
    wd = DIL_SLOTS * HEAD_DIM

    def body(oa_ref, o0, o1, o2, l0, l1, l2, gw_ref, gd_ref, w_ref, x_ref, g0, b0, g1, b1,
             mixed_ref, ob_ref, lt_ref, r1_ref, h1_ref, scr):
        ls = [l0[...], l1[...], l2[...]]
        mx = jnp.maximum(jnp.maximum(ls[0], ls[1]), ls[2])
        ws = [jnp.exp(l - mx) for l in ls]
        tot = ws[0] + ws[1] + ws[2]
        lt_ref[...] = mx + jnp.log(tot)
        ws = [x / tot for x in ws]
        og = [_from_subsequences(o_ref, scr.at[gi], dil, t)
              for gi, (o_ref, dil) in enumerate(zip((o0, o1, o2), DILATIONS))]
        parts = []
        for h in range(DIL_SLOTS):
            hs = slice(h * HEAD_DIM, (h + 1) * HEAD_DIM)
            parts.append(ws[0][:, h:h + 1] * og[0][:, hs] + ws[1][:, h:h + 1] * og[1][:, hs]
                         + ws[2][:, h:h + 1] * og[2][:, hs])
        ob = jnp.concatenate(parts, axis=1)
        ob_ref[...] = ob
        na, _ = _rms_parts(oa_ref[...], gw_ref[...])
        nb, _ = _rms_parts(ob, gd_ref[...])
        mixed = jnp.concatenate([na.astype(BF16), nb.astype(BF16)], axis=1)
        mixed_ref[...] = mixed
        h0 = _ln(x_ref[...], g0[...], b0[...])
        r1 = ALPHA * h0 + _dot(mixed, w_ref[...])
        r1_ref[...] = r1
        h1_ref[...] = _ln(r1, g1[...], b1[...]).astype(BF16)

    half = pl.BlockSpec((t, wd), lambda i: (i, 0))
    full = pl.BlockSpec((t, D_MODEL), lambda i: (i, 0))
    lanes = pl.BlockSpec((t, LANES), lambda i: (i, 0))
    grow = pl.BlockSpec((1, wd), lambda i: (0, 0))
    row = pl.BlockSpec((1, D_MODEL), lambda i: (0, 0))
    subseq = [pl.BlockSpec((dil, t // dil, wd), lambda i: (0, i, 0)) for dil in DILATIONS]
    return _pcall(
        body, name="combine_fwd", grid=(s // t,),
        in_specs=[pl.BlockSpec((None, t, wd), lambda i: (0, i, 0))] + subseq
        + [lanes, lanes, lanes, grow, grow, pl.BlockSpec((D_MODEL, D_MODEL), lambda i: (0, 0)), full,
           row, row, row, row],
        out_specs=[full, half, lanes, full, full],
        out_shape=[jax.ShapeDtypeStruct((s, D_MODEL), BF16), jax.ShapeDtypeStruct((s, wd), F32),
                   jax.ShapeDtypeStruct((s, LANES), F32), jax.ShapeDtypeStruct((s, D_MODEL), F32),
                   jax.ShapeDtypeStruct((s, D_MODEL), BF16)],
        scratch_shapes=[pltpu.VMEM((len(DILATIONS), wd // LANES, t, LANES), F32)],
        args=[out_a, *o_g, *lse_g, g_win, g_dil, w_mix_b, x, ln_in_g, ln_in_b, ln1_g, ln1_b], dims=("parallel",),
        comm=comm)


def _combine_bwd(dr1b, w_mix_b, out_a, out_b, g_win, g_dil, *, t):
    s = out_b.shape[0]
    wd = DIL_SLOTS * HEAD_DIM

    def body(dr_ref, w_ref, oa_ref, ob_ref, gw_ref, gd_ref, doa_ref, dob0, dob1, dob2, dla_ref, dlb_ref, st_ref,
             scr):
        i = pl.program_id(0)
        dm = _dot_nt(dr_ref[...], w_ref[...])

        @pl.when(i == 0)
        def _():
            st_ref[...] = jnp.zeros_like(st_ref)

        lane = lax.broadcasted_iota(jnp.int32, (t, LANES), 1)
        for idx, (o_ref, g_ref, dl_ref) in enumerate(((oa_ref, gw_ref, dla_ref), (ob_ref, gd_ref, dlb_ref))):
            o = o_ref[...]
            dn = dm[:, idx * wd:(idx + 1) * wd]
            _, rinv = _rms_parts(o, g_ref[...])
            wv = dn * g_ref[...]
            do = rinv * wv - o * (rinv * rinv * rinv) * jnp.mean(wv * o, axis=-1, keepdims=True)
            st_ref[idx:idx + 1, :] += jnp.sum(dn * o * rinv, axis=0, keepdims=True)
            if idx == 0:
                doa_ref[...] = do.astype(BF16)
            else:
                for do_ref, dil in zip((dob0, dob1, dob2), DILATIONS):
                    _to_subsequences(do, do_ref, scr, dil, t)
            prod = do * o
            acc = jnp.zeros((t, LANES), F32)
            for h in range(DIL_SLOTS):
                hs = slice(h * HEAD_DIM, (h + 1) * HEAD_DIM)
                acc = jnp.where(lane == h, jnp.sum(prod[:, hs], axis=1, keepdims=True), acc)
            dl_ref[...] = acc

    half = pl.BlockSpec((t, wd), lambda i: (i, 0))
    lanes = pl.BlockSpec((t, LANES), lambda i: (i, 0))
    grow = pl.BlockSpec((1, wd), lambda i: (0, 0))
    a_spec = pl.BlockSpec((None, t, wd), lambda i: (0, i, 0))
    subseq = [pl.BlockSpec((dil, t // dil, wd), lambda i: (0, i, 0)) for dil in DILATIONS]
    doa, dob0, dob1, dob2, dla, dlb, st = pl.pallas_call(
        body, name="combine_bwd", grid=(s // t,),
        in_specs=[pl.BlockSpec((t, D_MODEL), lambda i: (i, 0)), pl.BlockSpec((D_MODEL, D_MODEL), lambda i: (0, 0)),
                  a_spec, half, grow, grow],
        out_specs=[a_spec] + subseq + [lanes, lanes, pl.BlockSpec((8, wd), lambda i: (0, 0))],
        out_shape=[jax.ShapeDtypeStruct((1, s, wd), BF16)]
        + [jax.ShapeDtypeStruct((dil, s // dil, wd), BF16) for dil in DILATIONS]
        + [jax.ShapeDtypeStruct((s, LANES), F32), jax.ShapeDtypeStruct((s, LANES), F32),
           jax.ShapeDtypeStruct((8, wd), F32)],
        scratch_shapes=[pltpu.VMEM((wd // LANES, t, LANES), F32)],
        compiler_params=_cparams(dimension_semantics=("arbitrary",)),
    )(dr1b, w_mix_b, out_a, out_b, g_win, g_dil)
    return doa, [dob0, dob1, dob2], dla, dlb, st


def _assemble_dz(dqa, dka, dva, dqs, dks, dvs, *, t):
    s = dqa.shape[1]
    wd = DIL_SLOTS * HEAD_DIM

    def body(*refs):
        a_refs, g_refs, o_ref, scr = refs[:3], refs[3:12], refs[12], refs[13]
        col = 0
        for r in a_refs:
            o_ref[:, col:col + r.shape[-1]] = r[...]
            col += r.shape[-1]
        for part in range(3):
            for gi, dil in enumerate(DILATIONS):
                val = _from_subsequences(g_refs[3 * part + gi], scr, dil, t)
                o_ref[:, col:col + wd] = val.astype(BF16)
                col += wd

    a_specs = [pl.BlockSpec((None, t, a.shape[-1]), lambda i: (0, i, 0)) for a in (dqa, dka, dva)]
    g_specs = [pl.BlockSpec((dil, t // dil, wd), lambda i: (0, i, 0)) for _ in range(3) for dil in DILATIONS]
    return pl.pallas_call(
        body, name="assemble_dz", grid=(s // t,), in_specs=a_specs + g_specs,
        out_specs=pl.BlockSpec((t, IN_WIDTH), lambda i: (i, 0)),
        out_shape=jax.ShapeDtypeStruct((s, IN_WIDTH), BF16),
        scratch_shapes=[pltpu.VMEM((wd // LANES, t, LANES), F32)],
        compiler_params=_cparams(dimension_semantics=("parallel",)),
    )(dqa, dka, dva, *dqs, *dks, *dvs)


def _mem_fwd(mem, g, b, wk_b, wv_b):
    ml = mem.shape[0]

    def body(mem_ref, g_ref, b_ref, wk_ref, wv_ref, mn_ref, kx_ref, vx_ref):
        mn = _ln(mem_ref[...], g_ref[...], b_ref[...]).astype(BF16)
        mn_ref[...] = mn
        kx_ref[...] = _dot(mn, wk_ref[...]).astype(BF16)
        vx_ref[...] = _dot(mn, wv_ref[...]).astype(BF16)

    sh = jax.ShapeDtypeStruct((ml, D_MODEL), BF16)
    return pl.pallas_call(body, name="mem_fwd", out_shape=[sh, sh, sh], compiler_params=_cparams())(
        mem, g, b, wk_b, wv_b)


def _mem_bwd(dkx, dvx, mem, g, b, wk_b, wv_b):
    def body(dk_ref, dv_ref, mem_ref, g_ref, b_ref, wk_ref, wv_ref, dwk_ref, dwv_ref, st_ref):
        mem_v = mem_ref[...]
        mn = _ln(mem_v, g_ref[...], b_ref[...]).astype(BF16)
        dkb = dk_ref[...].astype(BF16)
        dvb = dv_ref[...].astype(BF16)
        dwk_ref[...] = _dot_tn(mn, dkb)
        dwv_ref[...] = _dot_tn(mn, dvb)
        dmn = _dot_nt(dkb, wk_ref[...]) + _dot_nt(dvb, wv_ref[...])
        _, dg, db = _ln_bwd_math(dmn, mem_v, g_ref[...])
        st_ref[...] = jnp.zeros_like(st_ref)
        st_ref[0:1, :] = dg
        st_ref[1:2, :] = db

    sw = jax.ShapeDtypeStruct((D_MODEL, D_MODEL), F32)
    return pl.pallas_call(body, name="mem_bwd", out_shape=[sw, sw, jax.ShapeDtypeStruct((8, D_MODEL), F32)],
                          compiler_params=_cparams())(dkx, dvx, mem, g, b, wk_b, wv_b)


def _xattn_fwd(h1b, r1, kx, vx, wq_b, wo_b, ln1_g, ln1_b, ln2_g, ln2_b, *, t, comm=None):
    s = h1b.shape[0]
    scale = X_HEAD_DIM ** -0.5

    def body(h_ref, r1_ref, kx_ref, vx_ref, wq_ref, wo_ref, g1, b1, g2, b2, r2_ref, h2_ref, qx_ref, ox_ref, lse_ref):
        qxb = _dot(h_ref[...], wq_ref[...]).astype(BF16)
        qx_ref[...] = qxb
        lane = lax.broadcasted_iota(jnp.int32, (t, LANES), 1)
        lse_acc = jnp.zeros((t, LANES), F32)
        parts = []
        for h in range(X_HEADS):
            hs = slice(h * X_HEAD_DIM, (h + 1) * X_HEAD_DIM)
            sc = _dot_nt(qxb[:, hs] * scale, kx_ref[:, hs])
            m = jnp.max(sc, axis=1, keepdims=True)
            p = jnp.exp(sc - m)
            den = jnp.sum(p, axis=1, keepdims=True)
            parts.append(_dot(p.astype(BF16), vx_ref[:, hs]) / den)
            lse_acc = jnp.where(lane == h, m + jnp.log(den), lse_acc)
        lse_ref[...] = lse_acc
        oxb = jnp.concatenate(parts, axis=1).astype(BF16)
        ox_ref[...] = oxb
        h1 = _ln(r1_ref[...], g1[...], b1[...])
        r2 = ALPHA * h1 + _dot(oxb, wo_ref[...])
        r2_ref[...] = r2
        h2_ref[...] = _ln(r2, g2[...], b2[...]).astype(BF16)

    tile = pl.BlockSpec((t, D_MODEL), lambda i: (i, 0))
    row = pl.BlockSpec((1, D_MODEL), lambda i: (0, 0))
    full = lambda r: pl.BlockSpec((r, D_MODEL), lambda i: (0, 0))
    ml = kx.shape[0]
    bsh = jax.ShapeDtypeStruct((s, D_MODEL), BF16)
    return _pcall(
        body, name="xattn_fwd", grid=(s // t,),
        in_specs=[tile, tile, full(ml), full(ml), full(D_MODEL), full(D_MODEL), row, row, row, row],
        out_specs=[tile, tile, tile, tile, pl.BlockSpec((t, LANES), lambda i: (i, 0))],
        out_shape=[jax.ShapeDtypeStruct((s, D_MODEL), F32), bsh, bsh, bsh, jax.ShapeDtypeStruct((s, LANES), F32)],
        args=[h1b, r1, kx, vx, wq_b, wo_b, ln1_g, ln1_b, ln2_g, ln2_b], dims=("parallel",), comm=comm)


def _xattn_bwd(dr2, qxb, oxb, lse, kx, vx, wq_b, wo_b, r1, ln1_g, *, t, comm=None):
    s = dr2.shape[0]
    ml = kx.shape[0]
    scale = X_HEAD_DIM ** -0.5

    def body(dr2_ref, qx_ref, ox_ref, lse_ref, kx_ref, vx_ref, wq_ref, wo_ref, r1_ref, g1_ref,
             dr1_ref, dr1b_ref, dqx_ref, dkx_ref, dvx_ref, st_ref):
        i = pl.program_id(0)

        @pl.when(i == 0)
        def _():
            dkx_ref[...] = jnp.zeros_like(dkx_ref)
            dvx_ref[...] = jnp.zeros_like(dvx_ref)
            st_ref[...] = jnp.zeros_like(st_ref)

        dr2v = dr2_ref[...]
        dox = _dot_nt(dr2v.astype(BF16), wo_ref[...])
        parts = []
        for h in range(X_HEADS):
            hs = slice(h * X_HEAD_DIM, (h + 1) * X_HEAD_DIM)
            doh = dox[:, hs]
            dohb = doh.astype(BF16)
            dl = jnp.sum(doh * ox_ref[:, hs].astype(F32), axis=1, keepdims=True)
            qh = qx_ref[:, hs] * scale
            p = jnp.exp(_dot_nt(qh, kx_ref[:, hs]) - lse_ref[:, h:h + 1])
            dp = _dot_nt(dohb, vx_ref[:, hs])
            dsb = (p * (dp - dl)).astype(BF16)
            parts.append(_dot(dsb, kx_ref[:, hs]) * scale)
            dkx_ref[:, hs] += _dot_tn(dsb, qh)
            dvx_ref[:, hs] += _dot_tn(p.astype(BF16), dohb)
        dqxb = jnp.concatenate(parts, axis=1).astype(BF16)
        dqx_ref[...] = dqxb
        dh1 = _dot_nt(dqxb, wq_ref[...]) + ALPHA * dr2v
        dr1, dg, db = _ln_bwd_math(dh1, r1_ref[...], g1_ref[...])
        dr1_ref[...] = dr1
        dr1b_ref[...] = dr1.astype(BF16)
        st_ref[0:1, :] += dg
        st_ref[1:2, :] += db

    tile = pl.BlockSpec((t, D_MODEL), lambda i: (i, 0))
    full = lambda r: pl.BlockSpec((r, D_MODEL), lambda i: (0, 0))
    bsh = jax.ShapeDtypeStruct((s, D_MODEL), BF16)
    return _pcall(
        body, name="xattn_bwd", grid=(s // t,),
        in_specs=[tile, tile, tile, pl.BlockSpec((t, LANES), lambda i: (i, 0)), full(ml), full(ml),
                  full(D_MODEL), full(D_MODEL), tile, full(1)],
        out_specs=[tile, tile, tile, full(ml), full(ml), full(8)],
        out_shape=[jax.ShapeDtypeStruct((s, D_MODEL), F32), bsh, bsh,
                   jax.ShapeDtypeStruct((ml, D_MODEL), F32), jax.ShapeDtypeStruct((ml, D_MODEL), F32),
                   jax.ShapeDtypeStruct((8, D_MODEL), F32)],
        args=[dr2, qxb, oxb, lse, kx, vx, wq_b, wo_b, r1, ln1_g], dims=("arbitrary",), comm=comm)


def _halo_specs(t, s, width):
    tb8 = t // 8
    return [pl.BlockSpec((t, width), lambda i: (i, 0)),
            pl.BlockSpec((8, width), lambda i: (jnp.maximum(i * tb8 - 1, 0), 0)),
            pl.BlockSpec((8, width), lambda i: (jnp.minimum((i + 1) * tb8, s // 8 - 1), 0))]


def _halo_rows(i, n, prev_ref, next_ref):
    prev_row = jnp.where(i > 0, prev_ref[7:8, :], 0.0)
    next_row = jnp.where(i < n - 1, next_ref[0:1, :], 0.0)
    return prev_row, next_row


def _gelu_parts(gc):
    cdf = 0.5 * (1.0 + lax.erf(gc * (2.0 ** -0.5)))
    pdf = jnp.exp(-0.5 * gc * gc) * (1.0 / math.sqrt(2.0 * math.pi))
    return gc * cdf, cdf + gc * pdf


def _ffn_out(g, u, conv_w, conv_b, w_down_b, r2, target, ln2_g, ln2_b, ln3_g, ln3_b, *, t):
    s = r2.shape[0]
    n = s // t

    def body(g_ref, gp_ref, gn_ref, u_ref, cw_ref, cb_ref, w_ref, r2_ref, tg_ref, g2, b2, g3, b3,
             t_ref, dr_ref, drb_ref, st_ref):
        i = pl.program_id(0)

        @pl.when(i == 0)
        def _():
            st_ref[...] = jnp.zeros_like(st_ref)

        gv = g_ref[...]
        prev_row, next_row = _halo_rows(i, n, gp_ref, gn_ref)
        gm1, gp1 = _shift_rows(gv, prev_row, next_row)
        gc = gm1 * cw_ref[0:1, :] + gv * cw_ref[1:2, :] + gp1 * cw_ref[2:3, :] + cb_ref[...]
        act, _ = _gelu_parts(gc)
        tb = (act * u_ref[...]).astype(BF16)
        t_ref[...] = tb
        h2 = _ln(r2_ref[...], g2[...], b2[...])
        r3 = ALPHA * h2 + _dot(tb, w_ref[...])
        y = _ln(r3, g3[...], b3[...])
        err = y - tg_ref[...]
        loss = 0.5 * jnp.sum(jnp.mean(err * err, axis=-1, keepdims=True))
        dr, dg, db = _ln_bwd_math(err * (1.0 / D_MODEL), r3, g3[...])
        dr_ref[...] = dr
        drb_ref[...] = dr.astype(BF16)
        st_ref[0:1, :] += dg
        st_ref[1:2, :] += db
        st_ref[2:3, :] += jnp.full((1, D_MODEL), loss, F32)

    wide = pl.BlockSpec((t, D_FF), lambda i: (i, 0))
    tile = pl.BlockSpec((t, D_MODEL), lambda i: (i, 0))
    row = pl.BlockSpec((1, D_MODEL), lambda i: (0, 0))
    return pl.pallas_call(
        body, name="ffn_out", grid=(n,),
        in_specs=_halo_specs(t, s, D_FF) + [wide, pl.BlockSpec((3, D_FF), lambda i: (0, 0)),
                                            pl.BlockSpec((1, D_FF), lambda i: (0, 0)),
                                            pl.BlockSpec((D_FF, D_MODEL), lambda i: (0, 0)),
                                            tile, tile, row, row, row, row],
        out_specs=[wide, tile, tile, pl.BlockSpec((8, D_MODEL), lambda i: (0, 0))],
        out_shape=[jax.ShapeDtypeStruct((s, D_FF), BF16), jax.ShapeDtypeStruct((s, D_MODEL), F32),
                   jax.ShapeDtypeStruct((s, D_MODEL), BF16), jax.ShapeDtypeStruct((8, D_MODEL), F32)],
        compiler_params=_cparams(dimension_semantics=("arbitrary",)),
    )(g, g, g, u, conv_w, conv_b, w_down_b, r2, target, ln2_g, ln2_b, ln3_g, ln3_b)


def _dh2_ln2(dgc, conv_w, du, w_gate_b, w_up_b, dr3, r2, ln2_g, *, t, comm=None):
    s = dgc.shape[0]
    n = s // t

    def body(d_ref, dp_ref, dn_ref, cw_ref, du_ref, wg_ref, wu_ref, dr3_ref, r2_ref, g2, dg_ref, dr_ref, drb_ref,
             st_ref):
        i = pl.program_id(0)

        @pl.when(i == 0)
        def _():
            st_ref[...] = jnp.zeros_like(st_ref)

        dv = d_ref[...]
        prev_row, next_row = _halo_rows(i, n, dp_ref, dn_ref)
        dm1, dp1 = _shift_rows(dv, prev_row, next_row)
        dgb = (dp1 * cw_ref[0:1, :] + dv * cw_ref[1:2, :] + dm1 * cw_ref[2:3, :]).astype(BF16)
        dg_ref[...] = dgb
        dh2 = _dot(dgb, wg_ref[...]) + _dot(du_ref[...], wu_ref[...]) + ALPHA * dr3_ref[...]
        dr, dg, db = _ln_bwd_math(dh2, r2_ref[...], g2[...])
        dr_ref[...] = dr
        drb_ref[...] = dr.astype(BF16)
        st_ref[0:1, :] += dg
        st_ref[1:2, :] += db

    wide = pl.BlockSpec((t, D_FF), lambda i: (i, 0))
    tile = pl.BlockSpec((t, D_MODEL), lambda i: (i, 0))
    wfull = pl.BlockSpec((D_FF, D_MODEL), lambda i: (0, 0), pipeline_mode=pl.Buffered(1))
    return _pcall(
        body, name="dh2_ln2", grid=(n,),
        in_specs=_halo_specs(t, s, D_FF) + [pl.BlockSpec((3, D_FF), lambda i: (0, 0)), wide, wfull, wfull,
                                            tile, tile, pl.BlockSpec((1, D_MODEL), lambda i: (0, 0))],
        out_specs=[wide, tile, tile, pl.BlockSpec((8, D_MODEL), lambda i: (0, 0))],
        out_shape=[jax.ShapeDtypeStruct((s, D_FF), BF16), jax.ShapeDtypeStruct((s, D_MODEL), F32),
                   jax.ShapeDtypeStruct((s, D_MODEL), BF16), jax.ShapeDtypeStruct((8, D_MODEL), F32)],
        args=[dgc, dgc, dgc, conv_w, du, w_gate_b, w_up_b, dr3, r2, ln2_g], dims=("arbitrary",), comm=comm)


def _conv_bwd_a(dr3b, w_down_b, g, u, conv_w, conv_b, *, t):
    s = g.shape[0]
    n = s // t

    def body(d_ref, w_ref, g_ref, gp_ref, gn_ref, u_ref, cw_ref, cb_ref, du_ref, dgc_ref, st_ref):
        i = pl.program_id(0)

        @pl.when(i == 0)
        def _():
            st_ref[...] = jnp.zeros_like(st_ref)

        dt = _dot_nt(d_ref[...], w_ref[...])
        gv = g_ref[...]
        prev_row, next_row = _halo_rows(i, n, gp_ref, gn_ref)
        gm1, gp1 = _shift_rows(gv, prev_row, next_row)
        gc = gm1 * cw_ref[0:1, :] + gv * cw_ref[1:2, :] + gp1 * cw_ref[2:3, :] + cb_ref[...]
        act, dact = _gelu_parts(gc)
        du_ref[...] = (dt * act).astype(BF16)
        dgc = dt * u_ref[...] * dact
        dgc_ref[...] = dgc
        st_ref[0:1, :] += jnp.sum(gm1 * dgc, axis=0, keepdims=True)
        st_ref[1:2, :] += jnp.sum(gv * dgc, axis=0, keepdims=True)
        st_ref[2:3, :] += jnp.sum(gp1 * dgc, axis=0, keepdims=True)
        st_ref[3:4, :] += jnp.sum(dgc, axis=0, keepdims=True)

    tile = pl.BlockSpec((t, D_FF), lambda i: (i, 0))
    return pl.pallas_call(
        body, name="conv_bwd_a", grid=(n,),
        in_specs=[pl.BlockSpec((t, D_MODEL), lambda i: (i, 0)), pl.BlockSpec((D_FF, D_MODEL), lambda i: (0, 0))]
        + _halo_specs(t, s, D_FF) + [tile, pl.BlockSpec((3, D_FF), lambda i: (0, 0)),
                                     pl.BlockSpec((1, D_FF), lambda i: (0, 0))],
        out_specs=[tile, tile, pl.BlockSpec((8, D_FF), lambda i: (0, 0))],
        out_shape=[jax.ShapeDtypeStruct((s, D_FF), BF16), jax.ShapeDtypeStruct((s, D_FF), F32),
                   jax.ShapeDtypeStruct((8, D_FF), F32)],
        compiler_params=_cparams(dimension_semantics=("arbitrary",)),
    )(dr3b, w_down_b, g, g, g, u, conv_w, conv_b)


def _to_residue(a, dil):
    s, w = a.shape
    return a.reshape(s // dil, dil, w).transpose(1, 0, 2)


def _stats_to_lanes(rows):
    dil, hq, l = rows.shape
    return jnp.pad(rows.transpose(2, 0, 1).reshape(dil * l, hq), ((0, 0), (0, LANES - hq)))


def _stats_to_rows(lanes, dil):
    s = lanes.shape[0]
    return lanes[:, :DIL_SLOTS].reshape(s // dil, dil, DIL_SLOTS).transpose(1, 2, 0)


def _rope_angles(positions):
    inv_freq = ROPE_THETA ** (-jnp.arange(0, ROT_DIM, 2, dtype=F32) / ROT_DIM)
    ang = positions.astype(F32)[:, None] * inv_freq
    return jnp.concatenate([jnp.cos(ang), jnp.sin(ang)], axis=1)


class _NoPlan:
    def gather(self, stage):
        return None

    def gathered(self, stage, couts, wb):
        pass

    def exchange(self, stage, grads):
        return None

    def exchanged(self, stage, couts):
        pass


def _local_step(x, mem, positions, target, wb, sp, plan=None, *, t_row=256, t_mm=512, tq_a=128, tq_b=128,
                sub_a=4, sub_b=4):
    s = x.shape[0]
    plan = plan or _NoPlan()
    cs = _rope_angles(positions)
    e_mat = _rope_select_matrix()

    h0b, couts = _ln_in_fwd(x, sp["ln_in_g"], sp["ln_in_b"], t=t_mm, comm=plan.gather("ln_in"))
    plan.gathered("ln_in", couts, wb)
    sp = dict(sp, conv_w=wb.get("conv_w", sp.get("conv_w")))
    (za, *zb), couts = _proj_all(h0b, wb["w_in"], cs, e_mat, t=min(2 * t_mm, s), comm=plan.gather("proj"))
    plan.gathered("proj", couts, wb)
    sub_a = max(1, min(sub_a, s // tq_a))
    subs_b = [max(1, min(sub_b, s // dil // tq_b)) for dil in DILATIONS]
    out_a, lse_a, couts = _swa_fwd_p(za, qcol=0, kcol=4, vcol=5, hq=WIN_Q_HEADS, hkv=WIN_KV_HEADS, w=WIN_HALF,
                                     tq=tq_a, sub=sub_a, sink=sp["attn_sink"], name="attn_a_fwd",
                                     comm=plan.gather("attn_a"))
    plan.gathered("attn_a", couts, wb)
    o_g, lse_g = [], []
    for gi in range(3):
        o, l, couts = _swa_fwd_p(zb[gi], qcol=0, kcol=1, vcol=2, hq=DIL_SLOTS, hkv=DIL_SLOTS, w=DIL_HALF, tq=tq_b,
                                 sub=subs_b[gi], sink=None, name=f"attn_b{gi}_fwd",
                                 comm=plan.gather(f"attn_b{gi}"))
        plan.gathered(f"attn_b{gi}", couts, wb)
        o_g.append(o)
        lse_g.append(_stats_to_lanes(l))
    (mixed_b, out_b, lse_b, r1, h1b), couts = _combine_fwd(
        out_a, o_g, lse_g, sp["g_win"], sp["g_dil"], wb["w_mix_out"], x, sp["ln_in_g"], sp["ln_in_b"],
        sp["ln1_g"], sp["ln1_b"], t=t_row, comm=plan.gather("combine"))
    plan.gathered("combine", couts, wb)
    mem_nb, kx, vx = _mem_fwd(mem, sp["mem_ln_g"], sp["mem_ln_b"], wb["w_xk"], wb["w_xv"])
    (r2, h2b, qxb, oxb, lse_x), couts = _xattn_fwd(
        h1b, r1, kx, vx, wb["w_xq"], wb["w_xo"], sp["ln1_g"], sp["ln1_b"], sp["ln2_g"], sp["ln2_b"], t=t_mm,
        comm=plan.gather("xattn"))
    plan.gathered("xattn", couts, wb)
    g = _mm(h2b, wb["w_gate"], mode="nt", out_dtype=F32, tm=t_mm, tn=D_FF, name="ff_gate")
    u = _mm(h2b, wb["w_up"], mode="nt", out_dtype=F32, tm=t_mm, tn=D_FF, name="ff_up")
    tb, dr3, dr3b, st3 = _ffn_out(g, u, sp["conv_w"], sp["conv_b"], wb["w_down"], r2, target, sp["ln2_g"],
                                  sp["ln2_b"], sp["ln3_g"], sp["ln3_b"], t=t_row)

    grads = {}
    du, dgc, st_conv = _conv_bwd_a(dr3b, wb["w_down"], g, u, sp["conv_w"], sp["conv_b"], t=t_row)
    tk = min(2048, s)
    grads["w_down"] = _mm(tb, dr3b, mode="tn", out_dtype=BF16, tm=D_FF // 2, tn=D_MODEL, tk=tk, name="dw_down")
    grads["w_up"] = _mm(du, h2b, mode="tn", out_dtype=BF16, tm=D_FF // 2, tn=D_MODEL, tk=tk, name="dw_up")
    (dg, dr2, dr2b, st2), couts = _dh2_ln2(dgc, sp["conv_w"], du, wb["w_gate"], wb["w_up"], dr3, r2, sp["ln2_g"],
                                           t=t_mm, comm=plan.exchange("dh2", grads))
    plan.exchanged("dh2", couts)
    grads["w_gate"] = _mm(dg, h2b, mode="tn", out_dtype=BF16, tm=D_FF // 2, tn=D_MODEL, tk=tk, name="dw_gate")

    (dr1, dr1b, dqxb, dkx, dvx, st1), couts = _xattn_bwd(
        dr2, qxb, oxb, lse_x, kx, vx, wb["w_xq"], wb["w_xo"], r1, sp["ln1_g"], t=t_mm,
        comm=plan.exchange("xattn", grads))
    plan.exchanged("xattn", couts)
    grads["w_xo"] = _mm(oxb, dr2b, mode="tn", out_dtype=BF16, tm=D_MODEL, tn=D_MODEL, tk=tk, name="dw_xo")
    grads["w_xq"] = _mm(h1b, dqxb, mode="tn", out_dtype=BF16, tm=D_MODEL, tn=D_MODEL, tk=tk, name="dw_xq")
    grads["w_xk"], grads["w_xv"], st_mem = _mem_bwd(dkx, dvx, mem, sp["mem_ln_g"], sp["mem_ln_b"],
                                                    wb["w_xk"], wb["w_xv"])

    grads["w_mix_out"] = _mm(mixed_b, dr1b, mode="tn", out_dtype=BF16, tm=D_MODEL, tn=D_MODEL, tk=tk,
                             name="dw_mix")
    do_a, do_b, dl_a, dl_b, st_mix = _combine_bwd(dr1b, wb["w_mix_out"], out_a, out_b, sp["g_win"], sp["g_dil"],
                                                  t=t_row)
    (dqa, dka, dva, dsink), couts = _swa_bwd_p(
        za, do_a, lse_a, _stats_to_rows(dl_a, 1), cs[None], e_mat, qcol=0, kcol=4, vcol=5, hq=WIN_Q_HEADS,
        hkv=WIN_KV_HEADS, w=WIN_HALF, tq=2 * tq_a, sub=max(1, sub_a // 2), sink=sp["attn_sink"], name="attn_a_bwd",
        comm=plan.exchange("attn_a", grads))
    plan.exchanged("attn_a", couts)
    dqs, dks, dvs = [], [], []
    for gi, dil in enumerate(DILATIONS):
        (dq, dk, dv), couts = _swa_bwd_p(
            zb[gi], do_b[gi], _stats_to_rows(lse_b, dil), _stats_to_rows(dl_b, dil),
            _to_residue(cs, dil), e_mat, qcol=0, kcol=1, vcol=2, hq=DIL_SLOTS, hkv=DIL_SLOTS, w=DIL_HALF, tq=tq_b,
            sub=subs_b[gi], sink=None, name=f"attn_b{gi}_bwd", comm=plan.exchange(f"attn_b{gi}", grads))
        plan.exchanged(f"attn_b{gi}", couts)
        dqs.append(dq)
        dks.append(dk)
        dvs.append(dv)
    dz = _assemble_dz(dqa, dka, dva, dqs, dks, dvs, t=t_mm)
    grads["w_in"] = _mm(dz, h0b, mode="tn", out_dtype=BF16, tm=IN_WIDTH // 7, tn=D_MODEL, tk=tk, name="dw_in")
    (grad_x, st0), couts = _dh0_ln_in(dz, wb["w_in"], dr1, x, sp["ln_in_g"], t=t_mm,
                                      comm=plan.exchange("dh0", grads))
    plan.exchanged("dh0", couts)

    small = {
        "loss": st3[2:3, 0:1],
        "ln_in_g": st0[0:1], "ln_in_b": st0[1:2],
        "attn_sink": dsink[:, 0].reshape(1, WIN_Q_HEADS),
        "g_win": st_mix[0:1], "g_dil": st_mix[1:2],
        "ln1_g": st1[0:1], "ln1_b": st1[1:2],
        "mem_ln_g": st_mem[0:1], "mem_ln_b": st_mem[1:2],
        "ln2_g": st2[0:1], "ln2_b": st2[1:2],
        "conv_w": st_conv[0:3], "conv_b": st_conv[3:4],
        "ln3_g": st3[0:1], "ln3_b": st3[1:2],
    }
    return grad_x, grads, small


class _SiblingSwap:
    def __init__(self, arrays):
        self.inputs = list(arrays)
        n = len(arrays)
        self.out_shape = [jax.ShapeDtypeStruct(a.shape, a.dtype) for a in arrays]
        self.scratch = [pltpu.SemaphoreType.DMA((n,)), pltpu.SemaphoreType.DMA((n,))]

    def _copies(self, src, dst, sems):
        send_sems, recv_sems = sems
        x, y, c, _ = _place()
        return [pltpu.make_async_remote_copy(
            src_ref=src[a], dst_ref=dst[a], send_sem=send_sems.at[a], recv_sem=recv_sems.at[a],
            device_id=(x, y, 1 - c), device_id_type=MESH_IDS) for a in range(len(src))]

    def start(self, src, dst, sems):
        for cp in self._copies(src, dst, sems):
            cp.start()

    def wait(self, src, dst, sems):
        copies = self._copies(src, dst, sems)
        for cp in copies:
            cp.wait_recv()
        for cp in copies:
            cp.wait_send()


class _Both:
    def __init__(self, first, second):
        self.parts = (first, second)
        self.inputs = first.inputs + second.inputs
        self.out_shape = first.out_shape + second.out_shape
        self.scratch = first.scratch + second.scratch

    def _split(self, src, dst, sems):
        a = self.parts[0]
        ni, no, ns = len(a.inputs), len(a.out_shape), len(a.scratch)
        return ((src[:ni], dst[:no], sems[:ns]), (src[ni:], dst[no:], sems[ns:]))

    def start(self, src, dst, sems):
        for part, args in zip(self.parts, self._split(src, dst, sems)):
            part.start(*args)

    def wait(self, src, dst, sems):
        for part, args in zip(self.parts, self._split(src, dst, sems)):
            part.wait(*args)


def _row_tile(rows, cols, itemsize=4, budget=1 << 20):
    best = None
    for t in range(16, rows + 1, 16):
        if rows % t == 0 and t * cols * itemsize <= budget:
            best = t
    return best or rows


def _sum_slots(stack, *, name):
    n, r, c = stack.shape
    t = _row_tile(r, c)

    def body(s_ref, o_ref):
        acc = s_ref[0].astype(F32)
        for q in range(1, n):
            acc = acc + s_ref[q].astype(F32)
        o_ref[...] = acc

    return pl.pallas_call(
        body, name=name, grid=(r // t,), in_specs=[pl.BlockSpec((n, t, c), lambda i: (0, i, 0))],
        out_specs=pl.BlockSpec((t, c), lambda i: (i, 0)), out_shape=jax.ShapeDtypeStruct((r, c), F32),
        compiler_params=_cparams(dimension_semantics=("parallel",)),
    )(stack)


def _adamw(w, m, v, p, q, *, name):
    r, c = w.shape
    t = _row_tile(r, c, budget=1 << 20)

    def total(ref):
        if len(ref.shape) == 2:
            return ref[...]
        acc = ref[0].astype(F32)
        for slot in range(1, ref.shape[0]):
            acc = acc + ref[slot].astype(F32)
        return acc

    def body(*refs):
        if q is None:
            w_ref, m_ref, v_ref, p_ref, g_ref, d_ref, nm_ref, nv_ref = refs
            g = total(p_ref)
        else:
            w_ref, m_ref, v_ref, p_ref, q_ref, g_ref, d_ref, nm_ref, nv_ref = refs
            g = total(p_ref) + total(q_ref)
        nm = ADAM_B1 * m_ref[...] + (1.0 - ADAM_B1) * g
        nv = ADAM_B2 * v_ref[...] + (1.0 - ADAM_B2) * (g * g)
        m_hat = nm / (1.0 - ADAM_B1 ** ADAM_STEP)
        v_hat = nv / (1.0 - ADAM_B2 ** ADAM_STEP)
        g_ref[...] = g
        d_ref[...] = -ADAM_LR * (m_hat / (jnp.sqrt(v_hat) + ADAM_EPS) + ADAM_WD * w_ref[...])
        nm_ref[...] = nm
        nv_ref[...] = nv

    tile = pl.BlockSpec((t, c), lambda i: (i, 0))
    args = [w, m, v, p] + ([] if q is None else [q])
    in_specs = [tile if a.ndim == 2 else pl.BlockSpec((a.shape[0], t, c), lambda i: (0, i, 0)) for a in args]
    sh = jax.ShapeDtypeStruct((r, c), F32)
    return pl.pallas_call(
        body, name=name, grid=(r // t,), in_specs=in_specs, out_specs=[tile] * 4, out_shape=[sh] * 4,
        compiler_params=_cparams(dimension_semantics=("parallel",)),
    )(*args)


BIG = ("w_in", "w_mix_out", "w_xq", "w_xk", "w_xv", "w_xo", "w_gate", "w_up", "w_down")
COL_SHARDED = ("w_in", "w_gate", "w_up")
WEIGHTS = ("ln_in_g", "ln_in_b", "w_in", "attn_sink", "g_win", "g_dil", "w_mix_out", "ln1_g", "ln1_b",
           "mem_ln_g", "mem_ln_b", "w_xq", "w_xk", "w_xv", "w_xo", "ln2_g", "ln2_b", "w_gate", "w_up",
           "conv_w", "conv_b", "w_down", "ln3_g", "ln3_b")
SMALL = tuple(k for k in WEIGHTS if k not in BIG)
PACK_COLS = 1024
CONV_SHARD = D_FF // N_CHIPS
CONV_WIDTH_ROWS = 3
SMALL_ROWS = 32


GATHER_STAGES = {"ln_in": ("w_in", "conv_w"), "proj": ("w_mix_out", "w_xq", "w_xk", "w_xv", "w_xo", "w_up"),
                 "combine": ("w_gate", "w_down")}
EXCHANGE_STAGES = {"dh2": ("w_down", "w_up"), "attn_a": ("w_gate", "w_xo", "w_xq"),
                   "attn_b0": ("w_xk", "w_xv", "w_mix_out"), "dh0": ("w_in",)}


def _full_weight(k, g4):
    return g4.reshape(N_CHIPS * g4.shape[1], g4.shape[2])


def _grad_parts(k, gk):
    gk = gk.astype(BF16)
    return gk.reshape(N_CHIPS, gk.shape[0] // N_CHIPS, gk.shape[1])


EARLY_SWAP_STAGE = "attn_b2"


class _Plan:
    def __init__(self, shards):
        self.shards = shards
        self.recv = {}
        self.chip_sums = {}
        self.sibling_sums = {}

    def gather(self, stage):
        names = GATHER_STAGES.get(stage)
        return _ChipGather([self.shards[k] for k in names]) if names else None

    def gathered(self, stage, couts, wb):
        for k, g4 in zip(GATHER_STAGES.get(stage, ()), couts):
            if k == "conv_w":
                taps = g4[:, :CONV_WIDTH_ROWS, :CONV_SHARD]
                wb[k] = taps.transpose(1, 0, 2).reshape(CONV_WIDTH_ROWS, D_FF)
            else:
                wb[k] = _full_weight(k, g4)

    def exchange(self, stage, grads):
        if stage == EARLY_SWAP_STAGE:
            self.early = [k for k in BIG if k in self.recv]
            for k in self.early:
                self.chip_sums[k] = self.recv[k]
            return _SiblingSwap([self.chip_sums[k] for k in self.early])
        names = EXCHANGE_STAGES.get(stage)
        return _ChipExchange([_grad_parts(k, grads[k]) for k in names]) if names else None

    def exchanged(self, stage, couts):
        if stage == EARLY_SWAP_STAGE:
            self.sibling_sums.update(zip(self.early, couts))
            return
        for k, r4 in zip(EXCHANGE_STAGES.get(stage, ()), couts):
            self.recv[k] = r4


def _pack_rows(a):
    r, n = a.shape
    per = -(-n // PACK_COLS)
    return jnp.pad(a, ((0, 0), (0, per * PACK_COLS - n))).reshape(r * per, PACK_COLS)


def _unpack_rows(p, r, n):
    per = -(-n // PACK_COLS)
    return p.reshape(r, per * PACK_COLS)[:, :n]


def _pack(pieces, rows_total):
    cat = jnp.concatenate([_pack_rows(a) for a in pieces], axis=0)
    return jnp.pad(cat, ((0, rows_total - cat.shape[0]), (0, 0)))


def _unpack(p, shapes):
    out, at = [], 0
    for r, n in shapes:
        per = -(-n // PACK_COLS)
        out.append(_unpack_rows(p[at:at + r * per], r, n))
        at += r * per
    return out


def kernel(x, mem, positions, ln_in_g, ln_in_b, w_in, attn_sink, g_win, g_dil, w_mix_out, ln1_g, ln1_b, mem_ln_g, mem_ln_b, w_xq, w_xk, w_xv, w_xo, ln2_g, ln2_b, w_gate, w_up, conv_w, conv_b, w_down, ln3_g, ln3_b, loss_target, m_ln_in_g, m_ln_in_b, m_w_in, m_attn_sink, m_g_win, m_g_dil, m_w_mix_out, m_ln1_g, m_ln1_b, m_mem_ln_g, m_mem_ln_b, m_w_xq, m_w_xk, m_w_xv, m_w_xo, m_ln2_g, m_ln2_b, m_w_gate, m_w_up, m_conv_w, m_conv_b, m_w_down, m_ln3_g, m_ln3_b, v_ln_in_g, v_ln_in_b, v_w_in, v_attn_sink, v_g_win, v_g_dil, v_w_mix_out, v_ln1_g, v_ln1_b, v_mem_ln_g, v_mem_ln_b, v_w_xq, v_w_xk, v_w_xv, v_w_xo, v_ln2_g, v_ln2_b, v_w_gate, v_w_up, v_conv_w, v_conv_b, v_w_down, v_ln3_g, v_ln3_b):
    given = dict(locals())
    shape_of = {k: given[k].shape for k in WEIGHTS}
    as2d = lambda k, a: a.reshape(-1, a.shape[-1]).T if k in COL_SHARDED else a.reshape(-1, a.shape[-1])
    w2 = {k: as2d(k, given[k]) for k in WEIGHTS}
    m2 = {k: as2d(k, given["m_" + k]) for k in WEIGHTS}
    v2 = {k: as2d(k, given["v_" + k]) for k in WEIGHTS}
    chip = 2 * lax.axis_index("x") + lax.axis_index("y")

    shards = {k: w2[k].astype(BF16) for k in BIG}
    shards["conv_w"] = jnp.pad(w2["conv_w"], ((0, 16 - CONV_WIDTH_ROWS), (0, PACK_COLS - CONV_SHARD)))
    plan = _Plan(shards)
    sp = {k: w2[k] for k in SMALL if k != "conv_w"}

    grad_x, grads, small = _local_step(x[0], mem[0], positions[0], loss_target[0], {}, sp, plan)

    small_keys = ("loss",) + SMALL
    small_shapes = [small[k].shape for k in small_keys]
    small_pack = _pack([small[k] for k in small_keys], SMALL_ROWS)
    late = [k for k in BIG if k not in plan.chip_sums]
    for k in late:
        plan.chip_sums[k] = _sum_slots(plan.recv[k], name=f"sum_chips_{k}")
    *late_sibling, small_all = _comm_only(
        _Both(_SiblingSwap([plan.chip_sums[k] for k in late]), _ChipExchange([], small_pack)), "swap_and_small")
    plan.sibling_sums.update(zip(late, late_sibling))
    chip_sums = [plan.chip_sums[k] for k in BIG]
    sibling_sums = [plan.sibling_sums[k] for k in BIG]
    small_sum = _sum_slots(small_all, name="sum_small")
    small_g = dict(zip(small_keys, _unpack(small_sum, small_shapes)))
    loss = small_g["loss"][0, 0]

    res = {}
    for k, p, q in zip(BIG, chip_sums, sibling_sums):
        res[k] = _adamw(w2[k], m2[k], v2[k], p, q, name=f"adamw_{k}")
    small_g["conv_w"] = lax.dynamic_slice_in_dim(small_g["conv_w"], chip * CONV_SHARD, CONV_SHARD, axis=1)
    adam_shapes = [w2[k].shape for k in SMALL]
    packs = [_pack([d[k] for k in SMALL], SMALL_ROWS) for d in (w2, m2, v2, small_g)]
    small_res = [_unpack(o, adam_shapes) for o in _adamw(*packs, None, name="adamw_small")]
    for i, k in enumerate(SMALL):
        res[k] = tuple(o[i] for o in small_res)

    outs = [loss, grad_x[None]]
    for slot in range(4):
        outs += [(res[k][slot].T if k in COL_SHARDED else res[k][slot]).reshape(shape_of[k]) for k in WEIGHTS]
    return tuple(outs)
```

```python
import functools
import math

import jax
import jax.numpy as jnp
from jax import lax
from jax.experimental import pallas as pl
from jax.experimental.pallas import tpu as pltpu

F32 = jnp.float32
BF16 = jnp.bfloat16

D_MODEL = 1024
HEAD_DIM = 64
WIN_Q_HEADS = 8
WIN_KV_HEADS = 2
WIN_HALF = 128
DIL_SLOTS = 8
DILATIONS = (1, 4, 16)
DIL_HALF = 64
ROT_DIM = 16
ROPE_THETA = 500000.0
X_HEADS = 4
X_HEAD_DIM = 256
D_FF = 2816
A_Q = 512
A_KV = 128
A_WIDTH = A_Q + 2 * A_KV
B_QKV = 1536
IN_WIDTH = 5376
ALPHA = 2.0 ** 0.25
LN_EPS = 1e-5
NEG_INF = -1e30
LANES = 128
N_CHIPS = 4
N_DEV = 8

ADAM_LR = 0.001
ADAM_B1 = 0.9
ADAM_B2 = 0.999
ADAM_EPS = 1e-08
ADAM_WD = 0.01
ADAM_STEP = 10

VMEM_LIMIT = 56 * 1024 * 1024


def _cparams(**kw):
    return pltpu.CompilerParams(vmem_limit_bytes=VMEM_LIMIT, **kw)


def _dot(a, b):
    return lax.dot_general(a, b, (((1,), (0,)), ((), ())), preferred_element_type=F32)


def _dot_nt(a, b):
    return lax.dot_general(a, b, (((1,), (1,)), ((), ())), preferred_element_type=F32)


def _dot_tn(a, b):
    return lax.dot_general(a, b, (((0,), (0,)), ((), ())), preferred_element_type=F32)


def _ln(x, g, b):
    mu = jnp.mean(x, axis=-1, keepdims=True)
    xc = x - mu
    var = jnp.mean(xc * xc, axis=-1, keepdims=True)
    return xc * lax.rsqrt(var + LN_EPS) * g + b


def _ln_bwd_math(dy, r, g):
    mu = jnp.mean(r, axis=-1, keepdims=True)
    xc = r - mu
    var = jnp.mean(xc * xc, axis=-1, keepdims=True)
    rstd = lax.rsqrt(var + LN_EPS)
    xhat = xc * rstd
    dxhat = dy * g
    m1 = jnp.mean(dxhat, axis=-1, keepdims=True)
    m2 = jnp.mean(dxhat * xhat, axis=-1, keepdims=True)
    dr = rstd * (dxhat - m1 - xhat * m2)
    return dr, jnp.sum(dy * xhat, axis=0, keepdims=True), jnp.sum(dy, axis=0, keepdims=True)


def _rope(z, ta, tb, tc, sign):
    w = z.shape[1]
    reps = w // LANES
    a = jnp.tile(ta, (1, reps))
    b = jnp.tile(tb, (1, reps))
    c = jnp.tile(tc, (1, reps))
    return z * a + sign * (pltpu.roll(z, w - 8, 1) * b + pltpu.roll(z, 8, 1) * c)


def _shift_rows(x, prev_row, next_row):
    t = x.shape[0]
    sub = 8
    row = lax.broadcasted_iota(jnp.int32, (sub, x.shape[1]), 0)
    down, up = pltpu.roll(x, 1, 0), pltpu.roll(x, t - 1, 0)
    xm1 = jnp.concatenate([jnp.where(row == 0, prev_row, down[:sub]), down[sub:]], axis=0)
    xp1 = jnp.concatenate([up[:t - sub], jnp.where(row == sub - 1, next_row, up[t - sub:])], axis=0)
    return xm1, xp1


def _rope_tabs(cs, e_mat):
    hi = cs.astype(BF16)
    rest = cs - hi.astype(F32)
    mid = rest.astype(BF16)
    lo = (rest - mid.astype(F32)).astype(BF16)
    tabs = _dot(hi, e_mat) + _dot(mid, e_mat) + _dot(lo, e_mat)
    lane = lax.broadcasted_iota(jnp.int32, (cs.shape[0], LANES), 1)
    ones = jnp.where((lane & (HEAD_DIM - 1)) >= ROT_DIM, 1.0, 0.0)
    return tabs[:, :LANES] + ones, tabs[:, LANES:2 * LANES], tabs[:, 2 * LANES:]


def _rope_select_matrix():
    half = ROT_DIM // 2
    e = [[0.0] * (3 * LANES) for _ in range(ROT_DIM)]
    for lane in range(LANES):
        d = lane % HEAD_DIM
        if d < half:
            e[d][lane] = 1.0
            e[half + d][LANES + lane] = -1.0
        elif d < ROT_DIM:
            e[d - half][lane] = 1.0
            e[d][2 * LANES + lane] = 1.0
    return jnp.array(e, BF16)


def _rope_rows(x, cos_t, sin_t, sign):
    half = ROT_DIM // 2
    parts = []
    for base in (0, HEAD_DIM):
        r1, r2 = x[base:base + half], x[base + half:base + ROT_DIM]
        parts += [r1 * cos_t - sign * (r2 * sin_t), r2 * cos_t + sign * (r1 * sin_t), x[base + ROT_DIM:base + HEAD_DIM]]
    return jnp.concatenate(parts, axis=0)


MESH_IDS = pl.DeviceIdType.MESH
ANY = pl.BlockSpec(memory_space=pl.ANY)


def _place():
    x, y, c = lax.axis_index("x"), lax.axis_index("y"), lax.axis_index("c")
    other_chips = [(1 - x, y), (x, 1 - y), (1 - x, 1 - y)]
    return x, y, c, other_chips


class _ChipGather:
    def __init__(self, shards):
        self.inputs = list(shards)
        n = len(shards)
        self.out_shape = [jax.ShapeDtypeStruct((N_CHIPS,) + a.shape, a.dtype) for a in shards]
        self.scratch = [pltpu.SemaphoreType.DMA((6 * n,)), pltpu.SemaphoreType.DMA((6 * n,)),
                        pltpu.SemaphoreType.DMA((n,))]

    def _copies(self, src, dst, sems):
        send_sems, recv_sems, local_sems = sems
        x, y, c, chips = _place()
        mine = 2 * x + y
        n = len(src)
        local, sends, recvs, passes, pass_recvs = [], [], [], [], []
        for a in range(n):
            half = src[a].shape[0] // 2
            my_rows, other_rows = pl.ds(c * half, half), pl.ds((1 - c) * half, half)
            local.append(pltpu.make_async_copy(src[a], dst[a].at[mine], local_sems.at[a]))
            for j, (px, py) in enumerate(chips):
                k, k2, slot = 3 * a + j, 3 * n + 3 * a + j, 2 * px + py
                sends.append(pltpu.make_async_remote_copy(
                    src_ref=src[a].at[my_rows], dst_ref=dst[a].at[mine, my_rows], send_sem=send_sems.at[k],
                    recv_sem=recv_sems.at[k], device_id=(px, py, c), device_id_type=MESH_IDS))
                recvs.append(pltpu.make_async_remote_copy(
                    src_ref=src[a].at[my_rows], dst_ref=dst[a].at[slot, my_rows], send_sem=send_sems.at[k],
                    recv_sem=recv_sems.at[k], device_id=(px, py, c), device_id_type=MESH_IDS))
                passes.append(pltpu.make_async_remote_copy(
                    src_ref=dst[a].at[slot, my_rows], dst_ref=dst[a].at[slot, my_rows], send_sem=send_sems.at[k2],
                    recv_sem=recv_sems.at[k2], device_id=(x, y, 1 - c), device_id_type=MESH_IDS))
                pass_recvs.append(pltpu.make_async_remote_copy(
                    src_ref=dst[a].at[slot, my_rows], dst_ref=dst[a].at[slot, other_rows],
                    send_sem=send_sems.at[k2], recv_sem=recv_sems.at[k2], device_id=(x, y, 1 - c),
                    device_id_type=MESH_IDS))
        return local, sends, recvs, passes, pass_recvs

    def start(self, src, dst, sems):
        local, sends, _, _, _ = self._copies(src, dst, sems)
        for cp in local + sends:
            cp.start()

    def wait(self, src, dst, sems):
        local, sends, recvs, passes, pass_recvs = self._copies(src, dst, sems)
        for idx, landed in enumerate(recvs):
            landed.wait_recv()
            if passes:
                passes[idx].start()
        for cp in pass_recvs:
            cp.wait_recv()
        for cp in sends + passes:
            cp.wait_send()
        for cp in local:
            cp.wait()


class _ChipExchange:
    def __init__(self, parts, small=None):
        self.inputs = list(parts) + ([small] if small is not None else [])
        self.n = len(parts)
        self.has_small = small is not None
        self.out_shape = [jax.ShapeDtypeStruct(a.shape, a.dtype) for a in parts]
        n_sem, n_loc = 3 * self.n, self.n
        if self.has_small:
            self.out_shape.append(jax.ShapeDtypeStruct((N_DEV,) + small.shape, small.dtype))
            n_sem, n_loc = n_sem + N_DEV - 1, n_loc + 1
        self.scratch = [pltpu.SemaphoreType.DMA((n_sem,)), pltpu.SemaphoreType.DMA((n_sem,)),
                        pltpu.SemaphoreType.DMA((n_loc,))]

    def _copies(self, src, dst, sems):
        send_sems, recv_sems, local_sems = sems
        x, y, c, chips = _place()
        mine = 2 * x + y
        n = self.n
        local, sends, recvs = [], [], []
        for a in range(n):
            local.append(pltpu.make_async_copy(src[a].at[mine], dst[a].at[mine], local_sems.at[a]))
            for j, (px, py) in enumerate(chips):
                k = 3 * a + j
                sends.append(pltpu.make_async_remote_copy(
                    src_ref=src[a].at[2 * px + py], dst_ref=dst[a].at[mine], send_sem=send_sems.at[k],
                    recv_sem=recv_sems.at[k], device_id=(px, py, c), device_id_type=MESH_IDS))
                recvs.append(pltpu.make_async_remote_copy(
                    src_ref=src[a].at[mine], dst_ref=dst[a].at[2 * px + py], send_sem=send_sems.at[k],
                    recv_sem=recv_sems.at[k], device_id=(px, py, c), device_id_type=MESH_IDS))
        if self.has_small:
            me_dev = 4 * x + 2 * y + c
            local.append(pltpu.make_async_copy(src[n], dst[n].at[me_dev], local_sems.at[n]))
            for mask in range(1, N_DEV):
                px, py, pc = x ^ ((mask >> 2) & 1), y ^ ((mask >> 1) & 1), c ^ (mask & 1)
                k = 3 * n + mask - 1
                sends.append(pltpu.make_async_remote_copy(
                    src_ref=src[n], dst_ref=dst[n].at[me_dev], send_sem=send_sems.at[k], recv_sem=recv_sems.at[k],
                    device_id=(px, py, pc), device_id_type=MESH_IDS))
                recvs.append(pltpu.make_async_remote_copy(
                    src_ref=src[n], dst_ref=dst[n].at[4 * px + 2 * py + pc], send_sem=send_sems.at[k],
                    recv_sem=recv_sems.at[k], device_id=(px, py, pc), device_id_type=MESH_IDS))
        return local, sends, recvs, [], []

    start = _ChipGather.start
    wait = _ChipGather.wait


def _pcall(body, *, name, grid, in_specs, out_specs, out_shape, args, scratch_shapes=(), dims=None, comm=None):
    in_specs, out_specs, out_shape = list(in_specs), list(out_specs), list(out_shape)
    scratch_shapes = list(scratch_shapes)
    if comm is None:
        outs = pl.pallas_call(
            body, name=name, grid=grid, in_specs=in_specs, out_specs=out_specs, out_shape=out_shape,
            scratch_shapes=scratch_shapes, compiler_params=_cparams(dimension_semantics=dims),
        )(*args)
        return list(outs), []
    n_in, n_out, n_scr = len(in_specs), len(out_specs), len(scratch_shapes)
    n_cin, n_cout = len(comm.inputs), len(comm.out_shape)

    def wrapped(*refs):
        ins, refs = refs[:n_in], refs[n_in:]
        cins, refs = refs[:n_cin], refs[n_cin:]
        outs, refs = refs[:n_out], refs[n_out:]
        couts, refs = refs[:n_cout], refs[n_cout:]
        scr, csems = refs[:n_scr], refs[n_scr:]
        first = last = None
        for axis, size in enumerate(grid):
            pid = pl.program_id(axis)
            f, l = pid == 0, pid == size - 1
            first = f if first is None else first & f
            last = l if last is None else last & l

        @pl.when(first)
        def _():
            comm.start(cins, couts, csems)

        body(*ins, *outs, *scr)

        @pl.when(last)
        def _():
            comm.wait(cins, couts, csems)

    res = pl.pallas_call(
        wrapped, name=name, grid=grid, in_specs=in_specs + [ANY] * n_cin, out_specs=out_specs + [ANY] * n_cout,
        out_shape=out_shape + list(comm.out_shape), scratch_shapes=scratch_shapes + list(comm.scratch),
        compiler_params=_cparams(dimension_semantics=("arbitrary",) * len(grid)),
    )(*args, *comm.inputs)
    return list(res[:n_out]), list(res[n_out:])


def _comm_only(comm, name):
    def body(*refs):
        n_cin, n_cout = len(comm.inputs), len(comm.out_shape)
        cins, couts, csems = refs[:n_cin], refs[n_cin:n_cin + n_cout], refs[n_cin + n_cout:]
        comm.start(cins, couts, csems)
        comm.wait(cins, couts, csems)

    return list(pl.pallas_call(
        body, name=name, in_specs=[ANY] * len(comm.inputs), out_specs=[ANY] * len(comm.out_shape),
        out_shape=list(comm.out_shape), scratch_shapes=list(comm.scratch),
    )(*comm.inputs))


def _mm(a, b, *, mode, out_dtype, tm, tn, tk=None, name):
    if mode == "nt":
        m, k = a.shape
        n = b.shape[0]
        assert m % tm == 0 and n % tn == 0

        def body(a_ref, b_ref, o_ref):
            o_ref[...] = _dot_nt(a_ref[...], b_ref[...]).astype(out_dtype)

        return pl.pallas_call(
            body, name=name, grid=(m // tm, n // tn),
            in_specs=[pl.BlockSpec((tm, k), lambda i, j: (i, 0)), pl.BlockSpec((tn, k), lambda i, j: (j, 0))],
            out_specs=pl.BlockSpec((tm, tn), lambda i, j: (i, j)),
            out_shape=jax.ShapeDtypeStruct((m, n), out_dtype),
            compiler_params=_cparams(dimension_semantics=("parallel", "parallel")),
        )(a, b)
    assert mode == "tn"
    kk, m = a.shape
    n = b.shape[1]
    assert m % tm == 0 and n % tn == 0 and kk % tk == 0
    nk = kk // tk

    def body(a_ref, b_ref, o_ref, acc_ref):
        kstep = pl.program_id(2)

        @pl.when(kstep == 0)
        def _():
            acc_ref[...] = jnp.zeros_like(acc_ref)

        acc_ref[...] += _dot_tn(a_ref[...], b_ref[...])

        @pl.when(kstep == nk - 1)
        def _():
            o_ref[...] = acc_ref[...].astype(out_dtype)

    return pl.pallas_call(
        body, name=name, grid=(m // tm, n // tn, nk),
        in_specs=[pl.BlockSpec((tk, tm), lambda i, j, s: (s, i)), pl.BlockSpec((tk, tn), lambda i, j, s: (s, j))],
        out_specs=pl.BlockSpec((tm, tn), lambda i, j, s: (i, j)),
        out_shape=jax.ShapeDtypeStruct((m, n), out_dtype),
        scratch_shapes=[pltpu.VMEM((tm, tn), F32)],
        compiler_params=_cparams(dimension_semantics=("parallel", "parallel", "arbitrary")),
    )(a, b)


PROJ_COLS = 256


def _proj_segments():
    wd = DIL_SLOTS * HEAD_DIM
    segs = [(1, [(0, 1), (PROJ_COLS, 1), (2 * PROJ_COLS, 2)])]
    for gi, dil in enumerate(DILATIONS):
        blocks = []
        for part, kind in enumerate((1, 1, 0)):
            col = A_WIDTH + part * B_QKV + gi * wd
            blocks += [(col, kind), (col + PROJ_COLS, kind)]
        segs.append((dil, blocks))
    return segs


PROJ_SEGMENTS = _proj_segments()


def _dh0_ln_in(dz, w_t, dr1, x, ln_in_g, *, t, comm=None):
    s, k = dz.shape

    def body(dz_ref, w_ref, dr1_ref, x_ref, g_ref, gx_ref, st_ref):
        i = pl.program_id(0)

        @pl.when(i == 0)
        def _():
            st_ref[...] = jnp.zeros_like(st_ref)

        dh0 = _dot(dz_ref[...], w_ref[...]) + ALPHA * dr1_ref[...]
        dx, dg, db = _ln_bwd_math(dh0, x_ref[...], g_ref[...])
        gx_ref[...] = dx
        st_ref[0:1, :] += dg
        st_ref[1:2, :] += db

    tile = pl.BlockSpec((t, D_MODEL), lambda i: (i, 0))
    return _pcall(
        body, name="dh0_ln_in", grid=(s // t,),
        in_specs=[pl.BlockSpec((t, k), lambda i: (i, 0)),
                  pl.BlockSpec((k, D_MODEL), lambda i: (0, 0), pipeline_mode=pl.Buffered(1)),
                  tile, tile, pl.BlockSpec((1, D_MODEL), lambda i: (0, 0))],
        out_specs=[tile, pl.BlockSpec((8, D_MODEL), lambda i: (0, 0))],
        out_shape=[jax.ShapeDtypeStruct((s, D_MODEL), F32), jax.ShapeDtypeStruct((8, D_MODEL), F32)],
        args=[dz, w_t, dr1, x, ln_in_g], dims=("arbitrary",), comm=comm)


def _ln_in_fwd(x, g, b, *, t, comm=None):
    s = x.shape[0]

    def body(x_ref, g_ref, b_ref, o_ref):
        o_ref[...] = _ln(x_ref[...], g_ref[...], b_ref[...]).astype(BF16)

    row = pl.BlockSpec((1, D_MODEL), lambda i: (0, 0))
    tile = pl.BlockSpec((t, D_MODEL), lambda i: (i, 0))
    outs, couts = _pcall(body, name="ln_in_fwd", grid=(s // t,), in_specs=[tile, row, row], out_specs=[tile],
                         out_shape=[jax.ShapeDtypeStruct((s, D_MODEL), BF16)], args=[x, g, b], dims=("parallel",),
                         comm=comm)
    return outs[0], couts


def _proj_all(h0b, w_t, cs, e_mat, *, t, comm=None):
    s = h0b.shape[0]
    cb = PROJ_COLS
    halves = cb // LANES

    def body(h_ref, w_ref, cs_ref, e_ref, *rest):
        z_refs, scr = rest[:-1], rest[-1]
        h = h_ref[...]
        ta, tb, tc = (jnp.tile(tab, (1, halves)) for tab in _rope_tabs(cs_ref[...], e_ref[...]))
        lane = lax.broadcasted_iota(jnp.int32, (t, cb), 1)
        slot = 0
        for z_ref, (dil, blocks) in zip(z_refs, PROJ_SEGMENTS):
            for jb, (col, kind) in enumerate(blocks):
                acc = _dot_nt(h, w_ref[col:col + cb, :])
                if kind:
                    z = acc * ta + (pltpu.roll(acc, cb - 8, 1) * tb + pltpu.roll(acc, 8, 1) * tc)
                    if kind == 2:
                        z = jnp.where(lane < LANES, z, acc)
                else:
                    z = acc
                if dil == 1:
                    z_ref[0, :, cb * jb:cb * (jb + 1)] = z.astype(BF16)
                    continue
                for half in range(halves):
                    scr[slot, half] = z[:, half * LANES:(half + 1) * LANES]
                for c in range(dil):
                    for half in range(halves):
                        rows = scr[slot, half, pl.ds(c, t // dil, stride=dil), :]
                        z_ref[c, :, cb * jb + half * LANES:cb * jb + (half + 1) * LANES] = rows.astype(BF16)
                slot = 1 - slot

    widths = [cb * len(blocks) for _, blocks in PROJ_SEGMENTS]
    dils = [dil for dil, _ in PROJ_SEGMENTS]
    outs, couts = _pcall(
        body, name="proj_all", grid=(s // t,),
        in_specs=[pl.BlockSpec((t, D_MODEL), lambda i: (i, 0)),
                  pl.BlockSpec((IN_WIDTH, D_MODEL), lambda i: (0, 0), pipeline_mode=pl.Buffered(1)),
                  pl.BlockSpec((t, ROT_DIM), lambda i: (i, 0)), pl.BlockSpec((ROT_DIM, 3 * LANES), lambda i: (0, 0))],
        out_specs=[pl.BlockSpec((dil, t // dil, wd), lambda i: (0, i, 0)) for dil, wd in zip(dils, widths)],
        out_shape=[jax.ShapeDtypeStruct((dil, s // dil, wd), BF16) for dil, wd in zip(dils, widths)],
        args=[h0b, w_t, cs, e_mat], scratch_shapes=[pltpu.VMEM((2, halves, t, LANES), F32)],
        dims=("parallel",), comm=comm)
    return outs, couts


PAIR = 2 * HEAD_DIM


def _place_head(x2, src_pos, dst_pos):
    hi = lax.broadcasted_iota(jnp.int32, x2.shape, 1) >= HEAD_DIM
    src = x2 if src_pos == dst_pos else pltpu.roll(x2, HEAD_DIM, 1)
    return jnp.where(hi == (dst_pos == 1), src, jnp.zeros_like(src))


def _band_mask_t(row0, tq, w, seq_len):
    tk = tq + 2 * w
    kk = lax.broadcasted_iota(jnp.int32, (tk, tq), 0)
    qq = lax.broadcasted_iota(jnp.int32, (tk, tq), 1)
    kpos = row0 - w + kk
    return (jnp.abs(qq + w - kk) <= w) & (kpos >= 0) & (kpos < seq_len)


def _halo_kv_specs(t, w, hkv, n, seq_len, kcol, vcol):
    kw = hkv * HEAD_DIM
    per, last = t // w, seq_len // w - 1
    cur = lambda s, i: jnp.minimum(i, n - 1)
    specs = []
    for c in (kcol, vcol):
        specs += [pl.BlockSpec((None, w, kw), lambda s, i, c=c: (s, jnp.maximum(cur(s, i) * per - 1, 0), c)),
                  pl.BlockSpec((None, t, kw), lambda s, i, c=c: (s, cur(s, i), c)),
                  pl.BlockSpec((None, w, kw), lambda s, i, c=c: (s, jnp.minimum((cur(s, i) + 1) * per, last), c))]
    return specs, cur


def _pair_kv(kfull, vfull, qp, rep, krows):
    ks, vs, a_of = [], [], []
    for pos in range(2):
        g = (2 * qp + pos) // rep
        a_of.append(g // 2)
        ks.append(_place_head(kfull[g // 2][krows], g % 2, pos))
        vs.append(_place_head(vfull[g // 2][krows], g % 2, pos))
    assert a_of[0] == a_of[1]
    return jnp.concatenate(ks, axis=0), jnp.concatenate(vs, axis=0), a_of[0]


def _swa_fwd_p(qkv, *, qcol, kcol, vcol, hq, hkv, w, tq, sub, sink, name, comm=None):
    nseq, seq_len, _ = qkv.shape
    t = tq * sub
    n = seq_len // t
    rep = hq // hkv
    tk = tq + 2 * w
    kv_specs, cur = _halo_kv_specs(t, w, hkv, n, seq_len, kcol, vcol)

    def body(*refs):
        if sink is not None:
            sink_ref, refs = refs[0], refs[1:]
        q_ref, kp_ref, kc_ref, kn_ref, vp_ref, vc_ref, vn_ref, o_ref, lse_ref = refs
        i = pl.program_id(1)
        kfull, vfull = [], []
        for a in range(hkv // 2):
            ls = slice(a * PAIR, (a + 1) * PAIR)
            kfull.append(jnp.concatenate([kp_ref[:, ls], kc_ref[:, ls], kn_ref[:, ls]], axis=0) * 0.125)
            vfull.append(jnp.concatenate([vp_ref[:, ls], vc_ref[:, ls], vn_ref[:, ls]], axis=0))
        row_hi = lax.broadcasted_iota(jnp.int32, (PAIR, tq), 0) >= HEAD_DIM
        for jj in range(sub):
            rows = slice(jj * tq, (jj + 1) * tq)
            mask_t = _band_mask_t(i * t + jj * tq, tq, w, seq_len)
            o_t, lse_rows = [], []
            for qp in range(hq // 2):
                kst, vst, _ = _pair_kv(kfull, vfull, qp, rep, slice(jj * tq, jj * tq + tk))
                s2 = _dot_nt(kst, q_ref[rows, qp * PAIR:(qp + 1) * PAIR])
                ps, dens = [], []
                for pos in range(2):
                    h = 2 * qp + pos
                    s_t = jnp.where(mask_t, s2[pos * tk:(pos + 1) * tk], NEG_INF)
                    m = jnp.max(s_t, axis=0, keepdims=True)
                    if sink is not None:
                        m = jnp.maximum(m, sink_ref[0, h])
                    p_t = jnp.exp(s_t - m)
                    den = jnp.sum(p_t, axis=0, keepdims=True)
                    if sink is not None:
                        den = den + jnp.exp(sink_ref[0, h] - m)
                    ps.append(p_t.astype(BF16))
                    dens.append(den)
                    lse_rows.append(m + jnp.log(den))
                both = _dot_tn(vst, jnp.concatenate(ps, axis=0))
                o_t.append(both / jnp.where(row_hi, dens[1], dens[0]))
            o_ref[rows, :] = jnp.concatenate(o_t, axis=0).T
            lse_ref[:, rows] = jnp.concatenate(lse_rows, axis=0)

    in_specs = [pl.BlockSpec((None, t, hq * HEAD_DIM), lambda s, i: (s, i, qcol))] + kv_specs
    args = [qkv] * 7
    if sink is not None:
        in_specs = [pl.BlockSpec(memory_space=pltpu.SMEM)] + in_specs
        args = [sink] + args
    (o, lse), couts = _pcall(
        body, name=name, grid=(nseq, n), in_specs=in_specs,
        out_specs=[pl.BlockSpec((None, t, hq * HEAD_DIM), lambda s, i: (s, i, 0)),
                   pl.BlockSpec((None, hq, t), lambda s, i: (s, 0, i))],
        out_shape=[jax.ShapeDtypeStruct((nseq, seq_len, hq * HEAD_DIM), F32),
                   jax.ShapeDtypeStruct((nseq, hq, seq_len), F32)],
        args=args, dims=("parallel", "parallel"), comm=comm)
    return o, lse, couts


def _swa_bwd_p(qkv, do, lse, delta, cs, e_mat, *, qcol, kcol, vcol, hq, hkv, w, tq, sub, sink, name, comm=None):
    nseq, seq_len, _ = qkv.shape
    t = tq * sub
    n = seq_len // t
    rep = hq // hkv
    qw, kw = hq * HEAD_DIM, hkv * HEAD_DIM
    tk = tq + 2 * w
    kv_specs, cur = _halo_kv_specs(t, w, hkv, n, seq_len, kcol, vcol)

    def body(*refs):
        if sink is not None:
            sink_ref, refs = refs[0], refs[1:]
        (q_ref, kp_ref, kc_ref, kn_ref, vp_ref, vc_ref, vn_ref, do_ref, lse_ref, dl_ref,
         cs_c, cs_p, e_ref) = refs[:13]
        outs = refs[13:]
        if sink is not None:
            dq_ref, dk_ref, dv_ref, dsink_ref, dk_acc, dv_acc, dk_win, dv_win = outs
        else:
            dq_ref, dk_ref, dv_ref, dk_acc, dv_acc, dk_win, dv_win = outs
        s_id = pl.program_id(0)
        i = pl.program_id(1)
        slot_p, slot_c, slot_n = (i + 2) % 3, i % 3, (i + 1) % 3

        if sink is not None:
            @pl.when((s_id == 0) & (i == 0))
            def _():
                dsink_ref[...] = jnp.zeros_like(dsink_ref)

        @pl.when(i < n)
        def _():
            dk_win[...] = jnp.zeros_like(dk_win)
            dv_win[...] = jnp.zeros_like(dv_win)
            kfull, vfull = [], []
            for a in range(hkv // 2):
                ls = slice(a * PAIR, (a + 1) * PAIR)
                kfull.append(jnp.concatenate([kp_ref[:, ls], kc_ref[:, ls], kn_ref[:, ls]], axis=0) * 0.125)
                vfull.append(jnp.concatenate([vp_ref[:, ls], vc_ref[:, ls], vn_ref[:, ls]], axis=0))
            for jj in range(sub):
                rows = slice(jj * tq, (jj + 1) * tq)
                krows = slice(jj * tq, jj * tq + tk)
                mask_t = _band_mask_t(i * t + jj * tq, tq, w, seq_len)
                dq_t = []
                dk2 = [None] * (hkv // 2)
                dv2 = [None] * (hkv // 2)
                for qp in range(hq // 2):
                    kst, vst, a = _pair_kv(kfull, vfull, qp, rep, krows)
                    q2 = q_ref[rows, qp * PAIR:(qp + 1) * PAIR]
                    do2 = do_ref[rows, qp * PAIR:(qp + 1) * PAIR]
                    s2 = _dot_nt(kst, q2)
                    dp2 = _dot_nt(vst, do2)
                    ds, ps, q_at, do_at = [], [], [], []
                    for pos in range(2):
                        h = 2 * qp + pos
                        e = (h // rep) % 2
                        half = slice(pos * tk, (pos + 1) * tk)
                        lse_h = lse_ref[h:h + 1, rows]
                        dl_h = dl_ref[h:h + 1, rows]
                        p_t = jnp.exp(jnp.where(mask_t, s2[half], NEG_INF) - lse_h)
                        ds.append((p_t * (dp2[half] - dl_h)).astype(BF16))
                        ps.append(p_t.astype(BF16))
                        q_at.append(_place_head(q2, pos, e) * 0.125)
                        do_at.append(_place_head(do2, pos, e))
                        if sink is not None:
                            ds_sink = -jnp.sum(jnp.exp(sink_ref[0, h] - lse_h) * dl_h)
                            dsink_ref[h:h + 1, :] += jnp.full((1, LANES), ds_sink, F32)
                    dq_t.append(_rope_rows(_dot_tn(kst, jnp.concatenate(ds, axis=0)),
                                           cs_c[0:ROT_DIM // 2, rows], cs_c[ROT_DIM // 2:ROT_DIM, rows], -1.0))
                    dk_part = _dot(jnp.concatenate(ds, axis=1), jnp.concatenate(q_at, axis=0))
                    dv_part = _dot(jnp.concatenate(ps, axis=1), jnp.concatenate(do_at, axis=0))
                    dk2[a] = dk_part if dk2[a] is None else dk2[a] + dk_part
                    dv2[a] = dv_part if dv2[a] is None else dv2[a] + dv_part
                for a in range(hkv // 2):
                    ls = slice(a * PAIR, (a + 1) * PAIR)
                    dk_win[krows, ls] += dk2[a]
                    dv_win[krows, ls] += dv2[a]
                dq_ref[rows, :] = jnp.concatenate(dq_t, axis=0).T.astype(BF16)

            if n == 1:
                dk_ref[...] = _rope(dk_win[w:w + t, :], *_rope_tabs(cs_p[...], e_ref[...]), -1.0).astype(BF16)
                dv_ref[...] = dv_win[w:w + t, :].astype(BF16)
                return

            @pl.when(i > 0)
            def _():
                dk_acc[slot_p, t - w:, :] += dk_win[:w, :]
                dv_acc[slot_p, t - w:, :] += dv_win[:w, :]

            @pl.when(i == 0)
            def _():
                dk_acc[slot_c] = dk_win[w:w + t, :]
                dv_acc[slot_c] = dv_win[w:w + t, :]

            @pl.when(i > 0)
            def _():
                dk_acc[slot_c] += dk_win[w:w + t, :]
                dv_acc[slot_c] += dv_win[w:w + t, :]

            dk_acc[slot_n] = jnp.zeros((t, kw), F32)
            dv_acc[slot_n] = jnp.zeros((t, kw), F32)
            dk_acc[slot_n, :w, :] = dk_win[w + t:, :]
            dv_acc[slot_n, :w, :] = dv_win[w + t:, :]

        if n > 1:
            @pl.when(i >= 1)
            def _():
                dk_ref[...] = _rope(dk_acc[slot_p], *_rope_tabs(cs_p[...], e_ref[...]), -1.0).astype(BF16)
                dv_ref[...] = dv_acc[slot_p].astype(BF16)

    row_c = lambda width: pl.BlockSpec((None, t, width), lambda s, i: (s, cur(s, i), 0))
    row_p = lambda width: pl.BlockSpec((None, t, width), lambda s, i: (s, jnp.maximum(i - 1, 0), 0))
    stat = pl.BlockSpec((None, hq, t), lambda s, i: (s, 0, cur(s, i)))
    cs_rows = pl.BlockSpec((None, ROT_DIM, t), lambda s, i: (s, 0, cur(s, i)))
    in_specs = ([pl.BlockSpec((None, t, qw), lambda s, i: (s, cur(s, i), qcol))] + kv_specs
                + [row_c(qw), stat, stat, cs_rows, row_p(ROT_DIM),
                   pl.BlockSpec((ROT_DIM, 3 * LANES), lambda s, i: (0, 0))])
    args = [qkv] * 7 + [do, lse, delta, cs.transpose(0, 2, 1), cs, e_mat]
    out_specs = [row_c(qw), row_p(kw), row_p(kw)]
    out_shape = [jax.ShapeDtypeStruct((nseq, seq_len, qw), BF16),
                 jax.ShapeDtypeStruct((nseq, seq_len, kw), BF16),
                 jax.ShapeDtypeStruct((nseq, seq_len, kw), BF16)]
    if sink is not None:
        in_specs = [pl.BlockSpec(memory_space=pltpu.SMEM)] + in_specs
        args = [sink] + args
        out_specs.append(pl.BlockSpec((8, LANES), lambda s, i: (0, 0)))
        out_shape.append(jax.ShapeDtypeStruct((8, LANES), F32))
    return _pcall(
        body, name=name, grid=(nseq, n + 1 if n > 1 else 1), in_specs=in_specs, out_specs=out_specs,
        out_shape=out_shape,
        scratch_shapes=[pltpu.VMEM((3, t, kw), F32), pltpu.VMEM((3, t, kw), F32),
                        pltpu.VMEM((t + 2 * w, kw), F32), pltpu.VMEM((t + 2 * w, kw), F32)], args=args,
        dims=("arbitrary", "arbitrary"), comm=comm)


def _rms_parts(o, g):
    ms = jnp.mean(o * o, axis=-1, keepdims=True) + LN_EPS
    rinv = lax.rsqrt(ms)
    return o * rinv * g, rinv


def _from_subsequences(ref, scr, dil, t):
    slabs = ref.shape[-1] // LANES
    if dil == 1:
        return ref[0].astype(F32)
    for c in range(dil):
        for sl in range(slabs):
            scr[sl, pl.ds(c, t // dil, stride=dil), :] = ref[c, :, sl * LANES:(sl + 1) * LANES].astype(F32)
    return jnp.concatenate([scr[sl] for sl in range(slabs)], axis=1)


def _to_subsequences(val, ref, scr, dil, t):
    slabs = val.shape[-1] // LANES
    if dil == 1:
        ref[0] = val.astype(ref.dtype)
        return
    for sl in range(slabs):
        scr[sl] = val[:, sl * LANES:(sl + 1) * LANES]
    for c in range(dil):
        for sl in range(slabs):
            ref[c, :, sl * LANES:(sl + 1) * LANES] = scr[sl, pl.ds(c, t // dil, stride=dil), :].astype(ref.dtype)


def _combine_fwd(out_a, o_g, lse_g, g_win, g_dil, w_mix_b, x, ln_in_g, ln_in_b, ln1_g, ln1_b, *, t, comm=None):
    s = out_a.shape[1]
    wd = DIL_SLOTS * HEAD_DIM

    def body(oa_ref, o0, o1, o2, l0, l1, l2, gw_ref, gd_ref, w_ref, x_ref, g0, b0, g1, b1,
             mixed_ref, ob_ref, lt_ref, r1_ref, h1_ref, scr):
        ls = [l0[...], l1[...], l2[...]]
        mx = jnp.maximum(jnp.maximum(ls[0], ls[1]), ls[2])
        ws = [jnp.exp(l - mx) for l in ls]
        tot = ws[0] + ws[1] + ws[2]
        lt_ref[...] = mx + jnp.log(tot)
        ws = [x / tot for x in ws]
        og = [_from_subsequences(o_ref, scr.at[gi], dil, t)
              for gi, (o_ref, dil) in enumerate(zip((o0, o1, o2), DILATIONS))]
        parts = []
        for h in range(DIL_SLOTS):
            hs = slice(h * HEAD_DIM, (h + 1) * HEAD_DIM)
            parts.append(ws[0][:, h:h + 1] * og[0][:, hs] + ws[1][:, h:h + 1] * og[1][:, hs]
                         + ws[2][:, h:h + 1] * og[2][:, hs])
        ob = jnp.concatenate(parts, axis=1)
        ob_ref[...] = ob
        na, _ = _rms_parts(oa_ref[...], gw_ref[...])
        nb, _ = _rms_parts(ob, gd_ref[...])
        mixed = jnp.concatenate([na.astype(BF16), nb.astype(BF16)], axis=1)
        mixed_ref[...] = mixed
        h0 = _ln(x_ref[...], g0[...], b0[...])
        r1 = ALPHA * h0 + _dot(mixed, w_ref[...])
        r1_ref[...] = r1
        h1_ref[...] = _ln(r1, g1[...], b1[...]).astype(BF16)

    half = pl.BlockSpec((t, wd), lambda i: (i, 0))
    full = pl.BlockSpec((t, D_MODEL), lambda i: (i, 0))
    lanes = pl.BlockSpec((t, LANES), lambda i: (i, 0))
    grow = pl.BlockSpec((1, wd), lambda i: (0, 0))
    row = pl.BlockSpec((1, D_MODEL), lambda i: (0, 0))
    subseq = [pl.BlockSpec((dil, t // dil, wd), lambda i: (0, i, 0)) for dil in DILATIONS]
    return _pcall(
        body, name="combine_fwd", grid=(s // t,),
        in_specs=[pl.BlockSpec((None, t, wd), lambda i: (0, i, 0))] + subseq
        + [lanes, lanes, lanes, grow, grow, pl.BlockSpec((D_MODEL, D_MODEL), lambda i: (0, 0)), full,
           row, row, row, row],
        out_specs=[full, half, lanes, full, full],
        out_shape=[jax.ShapeDtypeStruct((s, D_MODEL), BF16), jax.ShapeDtypeStruct((s, wd), F32),
                   jax.ShapeDtypeStruct((s, LANES), F32), jax.ShapeDtypeStruct((s, D_MODEL), F32),
                   jax.ShapeDtypeStruct((s, D_MODEL), BF16)],
        scratch_shapes=[pltpu.VMEM((len(DILATIONS), wd // LANES, t, LANES), F32)],
        args=[out_a, *o_g, *lse_g, g_win, g_dil, w_mix_b, x, ln_in_g, ln_in_b, ln1_g, ln1_b], dims=("parallel",),
        comm=comm)


def _combine_bwd(dr1b, w_mix_b, out_a, out_b, g_win, g_dil, *, t):
    s = out_b.shape[0]
    wd = DIL_SLOTS * HEAD_DIM

    def body(dr_ref, w_ref, oa_ref, ob_ref, gw_ref, gd_ref, doa_ref, dob0, dob1, dob2, dla_ref, dlb_ref, st_ref,
             scr):
        i = pl.program_id(0)
        dm = _dot_nt(dr_ref[...], w_ref[...])

        @pl.when(i == 0)
        def _():
            st_ref[...] = jnp.zeros_like(st_ref)

        lane = lax.broadcasted_iota(jnp.int32, (t, LANES), 1)
        for idx, (o_ref, g_ref, dl_ref) in enumerate(((oa_ref, gw_ref, dla_ref), (ob_ref, gd_ref, dlb_ref))):
            o = o_ref[...]
            dn = dm[:, idx * wd:(idx + 1) * wd]
            _, rinv = _rms_parts(o, g_ref[...])
            wv = dn * g_ref[...]
            do = rinv * wv - o * (rinv * rinv * rinv) * jnp.mean(wv * o, axis=-1, keepdims=True)
            st_ref[idx:idx + 1, :] += jnp.sum(dn * o * rinv, axis=0, keepdims=True)
            if idx == 0:
                doa_ref[...] = do.astype(BF16)
            else:
                for do_ref, dil in zip((dob0, dob1, dob2), DILATIONS):
                    _to_subsequences(do, do_ref, scr, dil, t)
            prod = do * o
            acc = jnp.zeros((t, LANES), F32)
            for h in range(DIL_SLOTS):
                hs = slice(h * HEAD_DIM, (h + 1) * HEAD_DIM)
                acc = jnp.where(lane == h, jnp.sum(prod[:, hs], axis=1, keepdims=True), acc)
            dl_ref[...] = acc

    half = pl.BlockSpec((t, wd), lambda i: (i, 0))
    lanes = pl.BlockSpec((t, LANES), lambda i: (i, 0))
    grow = pl.BlockSpec((1, wd), lambda i: (0, 0))
    a_spec = pl.BlockSpec((None, t, wd), lambda i: (0, i, 0))
    subseq = [pl.BlockSpec((dil, t // dil, wd), lambda i: (0, i, 0)) for dil in DILATIONS]
    doa, dob0, dob1, dob2, dla, dlb, st = pl.pallas_call(
        body, name="combine_bwd", grid=(s // t,),
        in_specs=[pl.BlockSpec((t, D_MODEL), lambda i: (i, 0)), pl.BlockSpec((D_MODEL, D_MODEL), lambda i: (0, 0)),
                  a_spec, half, grow, grow],
        out_specs=[a_spec] + subseq + [lanes, lanes, pl.BlockSpec((8, wd), lambda i: (0, 0))],
        out_shape=[jax.ShapeDtypeStruct((1, s, wd), BF16)]
        + [jax.ShapeDtypeStruct((dil, s // dil, wd), BF16) for dil in DILATIONS]
        + [jax.ShapeDtypeStruct((s, LANES), F32), jax.ShapeDtypeStruct((s, LANES), F32),
           jax.ShapeDtypeStruct((8, wd), F32)],
        scratch_shapes=[pltpu.VMEM((wd // LANES, t, LANES), F32)],
        compiler_params=_cparams(dimension_semantics=("arbitrary",)),
    )(dr1b, w_mix_b, out_a, out_b, g_win, g_dil)
    return doa, [dob0, dob1, dob2], dla, dlb, st


def _assemble_dz(dqa, dka, dva, dqs, dks, dvs, *, t):
    s = dqa.shape[1]
    wd = DIL_SLOTS * HEAD_DIM

    def body(*refs):
        a_refs, g_refs, o_ref, scr = refs[:3], refs[3:12], refs[12], refs[13]
        col = 0
        for r in a_refs:
            o_ref[:, col:col + r.shape[-1]] = r[...]
            col += r.shape[-1]
        for part in range(3):
            for gi, dil in enumerate(DILATIONS):
                val = _from_subsequences(g_refs[3 * part + gi], scr, dil, t)
                o_ref[:, col:col + wd] = val.astype(BF16)
                col += wd

    a_specs = [pl.BlockSpec((None, t, a.shape[-1]), lambda i: (0, i, 0)) for a in (dqa, dka, dva)]
    g_specs = [pl.BlockSpec((dil, t // dil, wd), lambda i: (0, i, 0)) for _ in range(3) for dil in DILATIONS]
    return pl.pallas_call(
        body, name="assemble_dz", grid=(s // t,), in_specs=a_specs + g_specs,
        out_specs=pl.BlockSpec((t, IN_WIDTH), lambda i: (i, 0)),
        out_shape=jax.ShapeDtypeStruct((s, IN_WIDTH), BF16),
        scratch_shapes=[pltpu.VMEM((wd // LANES, t, LANES), F32)],
        compiler_params=_cparams(dimension_semantics=("parallel",)),
    )(dqa, dka, dva, *dqs, *dks, *dvs)


def _mem_fwd(mem, g, b, wk_b, wv_b):
    ml = mem.shape[0]

    def body(mem_ref, g_ref, b_ref, wk_ref, wv_ref, mn_ref, kx_ref, vx_ref):
        mn = _ln(mem_ref[...], g_ref[...], b_ref[...]).astype(BF16)
        mn_ref[...] = mn
        kx_ref[...] = _dot(mn, wk_ref[...]).astype(BF16)
        vx_ref[...] = _dot(mn, wv_ref[...]).astype(BF16)

    sh = jax.ShapeDtypeStruct((ml, D_MODEL), BF16)
    return pl.pallas_call(body, name="mem_fwd", out_shape=[sh, sh, sh], compiler_params=_cparams())(
        mem, g, b, wk_b, wv_b)


def _mem_bwd(dkx, dvx, mem, g, b, wk_b, wv_b):
    def body(dk_ref, dv_ref, mem_ref, g_ref, b_ref, wk_ref, wv_ref, dwk_ref, dwv_ref, st_ref):
        mem_v = mem_ref[...]
        mn = _ln(mem_v, g_ref[...], b_ref[...]).astype(BF16)
        dkb = dk_ref[...].astype(BF16)
        dvb = dv_ref[...].astype(BF16)
        dwk_ref[...] = _dot_tn(mn, dkb)
        dwv_ref[...] = _dot_tn(mn, dvb)
        dmn = _dot_nt(dkb, wk_ref[...]) + _dot_nt(dvb, wv_ref[...])
        _, dg, db = _ln_bwd_math(dmn, mem_v, g_ref[...])
        st_ref[...] = jnp.zeros_like(st_ref)
        st_ref[0:1, :] = dg
        st_ref[1:2, :] = db

    sw = jax.ShapeDtypeStruct((D_MODEL, D_MODEL), F32)
    return pl.pallas_call(body, name="mem_bwd", out_shape=[sw, sw, jax.ShapeDtypeStruct((8, D_MODEL), F32)],
                          compiler_params=_cparams())(dkx, dvx, mem, g, b, wk_b, wv_b)


def _xattn_fwd(h1b, r1, kx, vx, wq_b, wo_b, ln1_g, ln1_b, ln2_g, ln2_b, *, t, comm=None):
    s = h1b.shape[0]
    scale = X_HEAD_DIM ** -0.5

    def body(h_ref, r1_ref, kx_ref, vx_ref, wq_ref, wo_ref, g1, b1, g2, b2, r2_ref, h2_ref, qx_ref, ox_ref, lse_ref):
        qxb = _dot(h_ref[...], wq_ref[...]).astype(BF16)
        qx_ref[...] = qxb
        lane = lax.broadcasted_iota(jnp.int32, (t, LANES), 1)
        lse_acc = jnp.zeros((t, LANES), F32)
        parts = []
        for h in range(X_HEADS):
            hs = slice(h * X_HEAD_DIM, (h + 1) * X_HEAD_DIM)
            sc = _dot_nt(qxb[:, hs] * scale, kx_ref[:, hs])
            m = jnp.max(sc, axis=1, keepdims=True)
            p = jnp.exp(sc - m)
            den = jnp.sum(p, axis=1, keepdims=True)
            parts.append(_dot(p.astype(BF16), vx_ref[:, hs]) / den)
            lse_acc = jnp.where(lane == h, m + jnp.log(den), lse_acc)
        lse_ref[...] = lse_acc
        oxb = jnp.concatenate(parts, axis=1).astype(BF16)
        ox_ref[...] = oxb
        h1 = _ln(r1_ref[...], g1[...], b1[...])
        r2 = ALPHA * h1 + _dot(oxb, wo_ref[...])
        r2_ref[...] = r2
        h2_ref[...] = _ln(r2, g2[...], b2[...]).astype(BF16)

    tile = pl.BlockSpec((t, D_MODEL), lambda i: (i, 0))
    row = pl.BlockSpec((1, D_MODEL), lambda i: (0, 0))
    full = lambda r: pl.BlockSpec((r, D_MODEL), lambda i: (0, 0))
    ml = kx.shape[0]
    bsh = jax.ShapeDtypeStruct((s, D_MODEL), BF16)
    return _pcall(
        body, name="xattn_fwd", grid=(s // t,),
        in_specs=[tile, tile, full(ml), full(ml), full(D_MODEL), full(D_MODEL), row, row, row, row],
        out_specs=[tile, tile, tile, tile, pl.BlockSpec((t, LANES), lambda i: (i, 0))],
        out_shape=[jax.ShapeDtypeStruct((s, D_MODEL), F32), bsh, bsh, bsh, jax.ShapeDtypeStruct((s, LANES), F32)],
        args=[h1b, r1, kx, vx, wq_b, wo_b, ln1_g, ln1_b, ln2_g, ln2_b], dims=("parallel",), comm=comm)


def _xattn_bwd(dr2, qxb, oxb, lse, kx, vx, wq_b, wo_b, r1, ln1_g, *, t, comm=None):
    s = dr2.shape[0]
    ml = kx.shape[0]
    scale = X_HEAD_DIM ** -0.5

    def body(dr2_ref, qx_ref, ox_ref, lse_ref, kx_ref, vx_ref, wq_ref, wo_ref, r1_ref, g1_ref,
             dr1_ref, dr1b_ref, dqx_ref, dkx_ref, dvx_ref, st_ref):
        i = pl.program_id(0)

        @pl.when(i == 0)
        def _():
            dkx_ref[...] = jnp.zeros_like(dkx_ref)
            dvx_ref[...] = jnp.zeros_like(dvx_ref)
            st_ref[...] = jnp.zeros_like(st_ref)

        dr2v = dr2_ref[...]
        dox = _dot_nt(dr2v.astype(BF16), wo_ref[...])
        parts = []
        for h in range(X_HEADS):
            hs = slice(h * X_HEAD_DIM, (h + 1) * X_HEAD_DIM)
            doh = dox[:, hs]
            dohb = doh.astype(BF16)
            dl = jnp.sum(doh * ox_ref[:, hs].astype(F32), axis=1, keepdims=True)
            qh = qx_ref[:, hs] * scale
            p = jnp.exp(_dot_nt(qh, kx_ref[:, hs]) - lse_ref[:, h:h + 1])
            dp = _dot_nt(dohb, vx_ref[:, hs])
            dsb = (p * (dp - dl)).astype(BF16)
            parts.append(_dot(dsb, kx_ref[:, hs]) * scale)
            dkx_ref[:, hs] += _dot_tn(dsb, qh)
            dvx_ref[:, hs] += _dot_tn(p.astype(BF16), dohb)
        dqxb = jnp.concatenate(parts, axis=1).astype(BF16)
        dqx_ref[...] = dqxb
        dh1 = _dot_nt(dqxb, wq_ref[...]) + ALPHA * dr2v
        dr1, dg, db = _ln_bwd_math(dh1, r1_ref[...], g1_ref[...])
        dr1_ref[...] = dr1
        dr1b_ref[...] = dr1.astype(BF16)
        st_ref[0:1, :] += dg
        st_ref[1:2, :] += db

    tile = pl.BlockSpec((t, D_MODEL), lambda i: (i, 0))
    full = lambda r: pl.BlockSpec((r, D_MODEL), lambda i: (0, 0))
    bsh = jax.ShapeDtypeStruct((s, D_MODEL), BF16)
    return _pcall(
        body, name="xattn_bwd", grid=(s // t,),
        in_specs=[tile, tile, tile, pl.BlockSpec((t, LANES), lambda i: (i, 0)), full(ml), full(ml),
                  full(D_MODEL), full(D_MODEL), tile, full(1)],
        out_specs=[tile, tile, tile, full(ml), full(ml), full(8)],
        out_shape=[jax.ShapeDtypeStruct((s, D_MODEL), F32), bsh, bsh,
                   jax.ShapeDtypeStruct((ml, D_MODEL), F32), jax.ShapeDtypeStruct((ml, D_MODEL), F32),
                   jax.ShapeDtypeStruct((8, D_MODEL), F32)],
        args=[dr2, qxb, oxb, lse, kx, vx, wq_b, wo_b, r1, ln1_g], dims=("arbitrary",), comm=comm)


def _halo_specs(t, s, width):
    tb8 = t // 8
    return [pl.BlockSpec((t, width), lambda i: (i, 0)),
            pl.BlockSpec((8, width), lambda i: (jnp.maximum(i * tb8 - 1, 0), 0)),
            pl.BlockSpec((8, width), lambda i: (jnp.minimum((i + 1) * tb8, s // 8 - 1), 0))]


def _halo_rows(i, n, prev_ref, next_ref):
    prev_row = jnp.where(i > 0, prev_ref[7:8, :], 0.0)
    next_row = jnp.where(i < n - 1, next_ref[0:1, :], 0.0)
    return prev_row, next_row


def _gelu_parts(gc):
    cdf = 0.5 * (1.0 + lax.erf(gc * (2.0 ** -0.5)))
    pdf = jnp.exp(-0.5 * gc * gc) * (1.0 / math.sqrt(2.0 * math.pi))
    return gc * cdf, cdf + gc * pdf


def _ffn_out(g, u, conv_w, conv_b, w_down_b, r2, target, ln2_g, ln2_b, ln3_g, ln3_b, *, t):
    s = r2.shape[0]
    n = s // t

    def body(g_ref, gp_ref, gn_ref, u_ref, cw_ref, cb_ref, w_ref, r2_ref, tg_ref, g2, b2, g3, b3,
             t_ref, dr_ref, drb_ref, st_ref):
        i = pl.program_id(0)

        @pl.when(i == 0)
        def _():
            st_ref[...] = jnp.zeros_like(st_ref)

        gv = g_ref[...]
        prev_row, next_row = _halo_rows(i, n, gp_ref, gn_ref)
        gm1, gp1 = _shift_rows(gv, prev_row, next_row)
        gc = gm1 * cw_ref[0:1, :] + gv * cw_ref[1:2, :] + gp1 * cw_ref[2:3, :] + cb_ref[...]
        act, _ = _gelu_parts(gc)
        tb = (act * u_ref[...]).astype(BF16)
        t_ref[...] = tb
        h2 = _ln(r2_ref[...], g2[...], b2[...])
        r3 = ALPHA * h2 + _dot(tb, w_ref[...])
        y = _ln(r3, g3[...], b3[...])
        err = y - tg_ref[...]
        loss = 0.5 * jnp.sum(jnp.mean(err * err, axis=-1, keepdims=True))
        dr, dg, db = _ln_bwd_math(err * (1.0 / D_MODEL), r3, g3[...])
        dr_ref[...] = dr
        drb_ref[...] = dr.astype(BF16)
        st_ref[0:1, :] += dg
        st_ref[1:2, :] += db
        st_ref[2:3, :] += jnp.full((1, D_MODEL), loss, F32)

    wide = pl.BlockSpec((t, D_FF), lambda i: (i, 0))
    tile = pl.BlockSpec((t, D_MODEL), lambda i: (i, 0))
    row = pl.BlockSpec((1, D_MODEL), lambda i: (0, 0))
    return pl.pallas_call(
        body, name="ffn_out", grid=(n,),
        in_specs=_halo_specs(t, s, D_FF) + [wide, pl.BlockSpec((3, D_FF), lambda i: (0, 0)),
                                            pl.BlockSpec((1, D_FF), lambda i: (0, 0)),
                                            pl.BlockSpec((D_FF, D_MODEL), lambda i: (0, 0)),
                                            tile, tile, row, row, row, row],
        out_specs=[wide, tile, tile, pl.BlockSpec((8, D_MODEL), lambda i: (0, 0))],
        out_shape=[jax.ShapeDtypeStruct((s, D_FF), BF16), jax.ShapeDtypeStruct((s, D_MODEL), F32),
                   jax.ShapeDtypeStruct((s, D_MODEL), BF16), jax.ShapeDtypeStruct((8, D_MODEL), F32)],
        compiler_params=_cparams(dimension_semantics=("arbitrary",)),
    )(g, g, g, u, conv_w, conv_b, w_down_b, r2, target, ln2_g, ln2_b, ln3_g, ln3_b)


def _dh2_ln2(dgc, conv_w, du, w_gate_b, w_up_b, dr3, r2, ln2_g, *, t, comm=None):
    s = dgc.shape[0]
    n = s // t

    def body(d_ref, dp_ref, dn_ref, cw_ref, du_ref, wg_ref, wu_ref, dr3_ref, r2_ref, g2, dg_ref, dr_ref, drb_ref,
             st_ref):
        i = pl.program_id(0)

        @pl.when(i == 0)
        def _():
            st_ref[...] = jnp.zeros_like(st_ref)

        dv = d_ref[...]
        prev_row, next_row = _halo_rows(i, n, dp_ref, dn_ref)
        dm1, dp1 = _shift_rows(dv, prev_row, next_row)
        dgb = (dp1 * cw_ref[0:1, :] + dv * cw_ref[1:2, :] + dm1 * cw_ref[2:3, :]).astype(BF16)
        dg_ref[...] = dgb
        dh2 = _dot(dgb, wg_ref[...]) + _dot(du_ref[...], wu_ref[...]) + ALPHA * dr3_ref[...]
        dr, dg, db = _ln_bwd_math(dh2, r2_ref[...], g2[...])
        dr_ref[...] = dr
        drb_ref[...] = dr.astype(BF16)
        st_ref[0:1, :] += dg
        st_ref[1:2, :] += db

    wide = pl.BlockSpec((t, D_FF), lambda i: (i, 0))
    tile = pl.BlockSpec((t, D_MODEL), lambda i: (i, 0))
    wfull = pl.BlockSpec((D_FF, D_MODEL), lambda i: (0, 0), pipeline_mode=pl.Buffered(1))
    return _pcall(
        body, name="dh2_ln2", grid=(n,),
        in_specs=_halo_specs(t, s, D_FF) + [pl.BlockSpec((3, D_FF), lambda i: (0, 0)), wide, wfull, wfull,
                                            tile, tile, pl.BlockSpec((1, D_MODEL), lambda i: (0, 0))],
        out_specs=[wide, tile, tile, pl.BlockSpec((8, D_MODEL), lambda i: (0, 0))],
        out_shape=[jax.ShapeDtypeStruct((s, D_FF), BF16), jax.ShapeDtypeStruct((s, D_MODEL), F32),
                   jax.ShapeDtypeStruct((s, D_MODEL), BF16), jax.ShapeDtypeStruct((8, D_MODEL), F32)],
        args=[dgc, dgc, dgc, conv_w, du, w_gate_b, w_up_b, dr3, r2, ln2_g], dims=("arbitrary",), comm=comm)


def _conv_bwd_a(dr3b, w_down_b, g, u, conv_w, conv_b, *, t):
    s = g.shape[0]
    n = s // t

    def body(d_ref, w_ref, g_ref, gp_ref, gn_ref, u_ref, cw_ref, cb_ref, du_ref, dgc_ref, st_ref):
        i = pl.program_id(0)

        @pl.when(i == 0)
        def _():
            st_ref[...] = jnp.zeros_like(st_ref)

        dt = _dot_nt(d_ref[...], w_ref[...])
        gv = g_ref[...]
        prev_row, next_row = _halo_rows(i, n, gp_ref, gn_ref)
        gm1, gp1 = _shift_rows(gv, prev_row, next_row)
        gc = gm1 * cw_ref[0:1, :] + gv * cw_ref[1:2, :] + gp1 * cw_ref[2:3, :] + cb_ref[...]
        act, dact = _gelu_parts(gc)
        du_ref[...] = (dt * act).astype(BF16)
        dgc = dt * u_ref[...] * dact
        dgc_ref[...] = dgc
        st_ref[0:1, :] += jnp.sum(gm1 * dgc, axis=0, keepdims=True)
        st_ref[1:2, :] += jnp.sum(gv * dgc, axis=0, keepdims=True)
        st_ref[2:3, :] += jnp.sum(gp1 * dgc, axis=0, keepdims=True)
        st_ref[3:4, :] += jnp.sum(dgc, axis=0, keepdims=True)

    tile = pl.BlockSpec((t, D_FF), lambda i: (i, 0))
    return pl.pallas_call(
        body, name="conv_bwd_a", grid=(n,),
        in_specs=[pl.BlockSpec((t, D_MODEL), lambda i: (i, 0)), pl.BlockSpec((D_FF, D_MODEL), lambda i: (0, 0))]
        + _halo_specs(t, s, D_FF) + [tile, pl.BlockSpec((3, D_FF), lambda i: (0, 0)),
                                     pl.BlockSpec((1, D_FF), lambda i: (0, 0))],
        out_specs=[tile, tile, pl.BlockSpec((8, D_FF), lambda i: (0, 0))],
        out_shape=[jax.ShapeDtypeStruct((s, D_FF), BF16), jax.ShapeDtypeStruct((s, D_FF), F32),
                   jax.ShapeDtypeStruct((8, D_FF), F32)],
        compiler_params=_cparams(dimension_semantics=("arbitrary",)),
    )(dr3b, w_down_b, g, g, g, u, conv_w, conv_b)


def _to_residue(a, dil):
    s, w = a.shape
    return a.reshape(s // dil, dil, w).transpose(1, 0, 2)


def _stats_to_lanes(rows):
    dil, hq, l = rows.shape
    return jnp.pad(rows.transpose(2, 0, 1).reshape(dil * l, hq), ((0, 0), (0, LANES - hq)))


def _stats_to_rows(lanes, dil):
    s = lanes.shape[0]
    return lanes[:, :DIL_SLOTS].reshape(s // dil, dil, DIL_SLOTS).transpose(1, 2, 0)


def _rope_angles(positions):
    inv_freq = ROPE_THETA ** (-jnp.arange(0, ROT_DIM, 2, dtype=F32) / ROT_DIM)
    ang = positions.astype(F32)[:, None] * inv_freq
    return jnp.concatenate([jnp.cos(ang), jnp.sin(ang)], axis=1)


class _NoPlan:
    def gather(self, stage):
        return None

    def gathered(self, stage, couts, wb):
        pass

    def exchange(self, stage, grads):
        return None

    def exchanged(self, stage, couts):
        pass


def _local_step(x, mem, positions, target, wb, sp, plan=None, *, t_row=256, t_mm=512, tq_a=128, tq_b=128,
                sub_a=4, sub_b=4):
    s = x.shape[0]
    plan = plan or _NoPlan()
    cs = _rope_angles(positions)
    e_mat = _rope_select_matrix()

    h0b, couts = _ln_in_fwd(x, sp["ln_in_g"], sp["ln_in_b"], t=t_mm, comm=plan.gather("ln_in"))
    plan.gathered("ln_in", couts, wb)
    sp = dict(sp, conv_w=wb.get("conv_w", sp.get("conv_w")))
    (za, *zb), couts = _proj_all(h0b, wb["w_in"], cs, e_mat, t=min(2 * t_mm, s), comm=plan.gather("proj"))
    plan.gathered("proj", couts, wb)
    sub_a = max(1, min(sub_a, s // tq_a))
    subs_b = [max(1, min(sub_b, s // dil // tq_b)) for dil in DILATIONS]
    out_a, lse_a, couts = _swa_fwd_p(za, qcol=0, kcol=4, vcol=5, hq=WIN_Q_HEADS, hkv=WIN_KV_HEADS, w=WIN_HALF,
                                     tq=tq_a, sub=sub_a, sink=sp["attn_sink"], name="attn_a_fwd",
                                     comm=plan.gather("attn_a"))
    plan.gathered("attn_a", couts, wb)
    o_g, lse_g = [], []
    for gi in range(3):
        o, l, couts = _swa_fwd_p(zb[gi], qcol=0, kcol=1, vcol=2, hq=DIL_SLOTS, hkv=DIL_SLOTS, w=DIL_HALF, tq=tq_b,
                                 sub=subs_b[gi], sink=None, name=f"attn_b{gi}_fwd",
                                 comm=plan.gather(f"attn_b{gi}"))
        plan.gathered(f"attn_b{gi}", couts, wb)
        o_g.append(o)
        lse_g.append(_stats_to_lanes(l))
    (mixed_b, out_b, lse_b, r1, h1b), couts = _combine_fwd(
        out_a, o_g, lse_g, sp["g_win"], sp["g_dil"], wb["w_mix_out"], x, sp["ln_in_g"], sp["ln_in_b"],
        sp["ln1_g"], sp["ln1_b"], t=t_row, comm=plan.gather("combine"))
    plan.gathered("combine", couts, wb)
    mem_nb, kx, vx = _mem_fwd(mem, sp["mem_ln_g"], sp["mem_ln_b"], wb["w_xk"], wb["w_xv"])
    (r2, h2b, qxb, oxb, lse_x), couts = _xattn_fwd(
        h1b, r1, kx, vx, wb["w_xq"], wb["w_xo"], sp["ln1_g"], sp["ln1_b"], sp["ln2_g"], sp["ln2_b"], t=t_mm,
        comm=plan.gather("xattn"))
    plan.gathered("xattn", couts, wb)
    g = _mm(h2b, wb["w_gate"], mode="nt", out_dtype=F32, tm=t_mm, tn=D_FF, name="ff_gate")
    u = _mm(h2b, wb["w_up"], mode="nt", out_dtype=F32, tm=t_mm, tn=D_FF, name="ff_up")
    tb, dr3, dr3b, st3 = _ffn_out(g, u, sp["conv_w"], sp["conv_b"], wb["w_down"], r2, target, sp["ln2_g"],
                                  sp["ln2_b"], sp["ln3_g"], sp["ln3_b"], t=t_row)

    grads = {}
    du, dgc, st_conv = _conv_bwd_a(dr3b, wb["w_down"], g, u, sp["conv_w"], sp["conv_b"], t=t_row)
    tk = min(2048, s)
    grads["w_down"] = _mm(tb, dr3b, mode="tn", out_dtype=BF16, tm=D_FF // 2, tn=D_MODEL, tk=tk, name="dw_down")
    grads["w_up"] = _mm(du, h2b, mode="tn", out_dtype=BF16, tm=D_FF // 2, tn=D_MODEL, tk=tk, name="dw_up")
    (dg, dr2, dr2b, st2), couts = _dh2_ln2(dgc, sp["conv_w"], du, wb["w_gate"], wb["w_up"], dr3, r2, sp["ln2_g"],
                                           t=t_mm, comm=plan.exchange("dh2", grads))
    plan.exchanged("dh2", couts)
    grads["w_gate"] = _mm(dg, h2b, mode="tn", out_dtype=BF16, tm=D_FF // 2, tn=D_MODEL, tk=tk, name="dw_gate")

    (dr1, dr1b, dqxb, dkx, dvx, st1), couts = _xattn_bwd(
        dr2, qxb, oxb, lse_x, kx, vx, wb["w_xq"], wb["w_xo"], r1, sp["ln1_g"], t=t_mm,
        comm=plan.exchange("xattn", grads))
    plan.exchanged("xattn", couts)
    grads["w_xo"] = _mm(oxb, dr2b, mode="tn", out_dtype=BF16, tm=D_MODEL, tn=D_MODEL, tk=tk, name="dw_xo")
    grads["w_xq"] = _mm(h1b, dqxb, mode="tn", out_dtype=BF16, tm=D_MODEL, tn=D_MODEL, tk=tk, name="dw_xq")
    grads["w_xk"], grads["w_xv"], st_mem = _mem_bwd(dkx, dvx, mem, sp["mem_ln_g"], sp["mem_ln_b"],
                                                    wb["w_xk"], wb["w_xv"])

    grads["w_mix_out"] = _mm(mixed_b, dr1b, mode="tn", out_dtype=BF16, tm=D_MODEL, tn=D_MODEL, tk=tk,
                             name="dw_mix")
    do_a, do_b, dl_a, dl_b, st_mix = _combine_bwd(dr1b, wb["w_mix_out"], out_a, out_b, sp["g_win"], sp["g_dil"],
                                                  t=t_row)
    (dqa, dka, dva, dsink), couts = _swa_bwd_p(
        za, do_a, lse_a, _stats_to_rows(dl_a, 1), cs[None], e_mat, qcol=0, kcol=4, vcol=5, hq=WIN_Q_HEADS,
        hkv=WIN_KV_HEADS, w=WIN_HALF, tq=2 * tq_a, sub=max(1, sub_a // 2), sink=sp["attn_sink"], name="attn_a_bwd",
        comm=plan.exchange("attn_a", grads))
    plan.exchanged("attn_a", couts)
    dqs, dks, dvs = [], [], []
    for gi, dil in enumerate(DILATIONS):
        (dq, dk, dv), couts = _swa_bwd_p(
            zb[gi], do_b[gi], _stats_to_rows(lse_b, dil), _stats_to_rows(dl_b, dil),
            _to_residue(cs, dil), e_mat, qcol=0, kcol=1, vcol=2, hq=DIL_SLOTS, hkv=DIL_SLOTS, w=DIL_HALF, tq=tq_b,
            sub=subs_b[gi], sink=None, name=f"attn_b{gi}_bwd", comm=plan.exchange(f"attn_b{gi}", grads))
        plan.exchanged(f"attn_b{gi}", couts)
        dqs.append(dq)
        dks.append(dk)
        dvs.append(dv)
    dz = _assemble_dz(dqa, dka, dva, dqs, dks, dvs, t=t_mm)
    grads["w_in"] = _mm(dz, h0b, mode="tn", out_dtype=BF16, tm=IN_WIDTH // 7, tn=D_MODEL, tk=tk, name="dw_in")
    (grad_x, st0), couts = _dh0_ln_in(dz, wb["w_in"], dr1, x, sp["ln_in_g"], t=t_mm,
                                      comm=plan.exchange("dh0", grads))
    plan.exchanged("dh0", couts)

    small = {
        "loss": st3[2:3, 0:1],
        "ln_in_g": st0[0:1], "ln_in_b": st0[1:2],
        "attn_sink": dsink[:, 0].reshape(1, WIN_Q_HEADS),
        "g_win": st_mix[0:1], "g_dil": st_mix[1:2],
        "ln1_g": st1[0:1], "ln1_b": st1[1:2],
        "mem_ln_g": st_mem[0:1], "mem_ln_b": st_mem[1:2],
        "ln2_g": st2[0:1], "ln2_b": st2[1:2],
        "conv_w": st_conv[0:3], "conv_b": st_conv[3:4],
        "ln3_g": st3[0:1], "ln3_b": st3[1:2],
    }
    return grad_x, grads, small


class _SiblingSwap:
    def __init__(self, arrays):
        self.inputs = list(arrays)
        n = len(arrays)
        self.out_shape = [jax.ShapeDtypeStruct(a.shape, a.dtype) for a in arrays]
        self.scratch = [pltpu.SemaphoreType.DMA((n,)), pltpu.SemaphoreType.DMA((n,))]

    def _copies(self, src, dst, sems):
        send_sems, recv_sems = sems
        x, y, c, _ = _place()
        return [pltpu.make_async_remote_copy(
            src_ref=src[a], dst_ref=dst[a], send_sem=send_sems.at[a], recv_sem=recv_sems.at[a],
            device_id=(x, y, 1 - c), device_id_type=MESH_IDS) for a in range(len(src))]

    def start(self, src, dst, sems):
        for cp in self._copies(src, dst, sems):
            cp.start()

    def wait(self, src, dst, sems):
        copies = self._copies(src, dst, sems)
        for cp in copies:
            cp.wait_recv()
        for cp in copies:
            cp.wait_send()


class _Both:
    def __init__(self, first, second):
        self.parts = (first, second)
        self.inputs = first.inputs + second.inputs
        self.out_shape = first.out_shape + second.out_shape
        self.scratch = first.scratch + second.scratch

    def _split(self, src, dst, sems):
        a = self.parts[0]
        ni, no, ns = len(a.inputs), len(a.out_shape), len(a.scratch)
        return ((src[:ni], dst[:no], sems[:ns]), (src[ni:], dst[no:], sems[ns:]))

    def start(self, src, dst, sems):
        for part, args in zip(self.parts, self._split(src, dst, sems)):
            part.start(*args)

    def wait(self, src, dst, sems):
        for part, args in zip(self.parts, self._split(src, dst, sems)):
            part.wait(*args)


def _row_tile(rows, cols, itemsize=4, budget=1 << 20):
    best = None
    for t in range(16, rows + 1, 16):
        if rows % t == 0 and t * cols * itemsize <= budget:
            best = t
    return best or rows


def _sum_slots(stack, *, name):
    n, r, c = stack.shape
    t = _row_tile(r, c)

    def body(s_ref, o_ref):
        acc = s_ref[0].astype(F32)
        for q in range(1, n):
            acc = acc + s_ref[q].astype(F32)
        o_ref[...] = acc

    return pl.pallas_call(
        body, name=name, grid=(r // t,), in_specs=[pl.BlockSpec((n, t, c), lambda i: (0, i, 0))],
        out_specs=pl.BlockSpec((t, c), lambda i: (i, 0)), out_shape=jax.ShapeDtypeStruct((r, c), F32),
        compiler_params=_cparams(dimension_semantics=("parallel",)),
    )(stack)


def _adamw(w, m, v, p, q, *, name):
    r, c = w.shape
    t = _row_tile(r, c, budget=1 << 20)

    def total(ref):
        if len(ref.shape) == 2:
            return ref[...]
        acc = ref[0].astype(F32)
        for slot in range(1, ref.shape[0]):
            acc = acc + ref[slot].astype(F32)
        return acc

    def body(*refs):
        if q is None:
            w_ref, m_ref, v_ref, p_ref, g_ref, d_ref, nm_ref, nv_ref = refs
            g = total(p_ref)
        else:
            w_ref, m_ref, v_ref, p_ref, q_ref, g_ref, d_ref, nm_ref, nv_ref = refs
            g = total(p_ref) + total(q_ref)
        nm = ADAM_B1 * m_ref[...] + (1.0 - ADAM_B1) * g
        nv = ADAM_B2 * v_ref[...] + (1.0 - ADAM_B2) * (g * g)
        m_hat = nm / (1.0 - ADAM_B1 ** ADAM_STEP)
        v_hat = nv / (1.0 - ADAM_B2 ** ADAM_STEP)
        g_ref[...] = g
        d_ref[...] = -ADAM_LR * (m_hat / (jnp.sqrt(v_hat) + ADAM_EPS) + ADAM_WD * w_ref[...])
        nm_ref[...] = nm
        nv_ref[...] = nv

    tile = pl.BlockSpec((t, c), lambda i: (i, 0))
    args = [w, m, v, p] + ([] if q is None else [q])
    in_specs = [tile if a.ndim == 2 else pl.BlockSpec((a.shape[0], t, c), lambda i: (0, i, 0)) for a in args]
    sh = jax.ShapeDtypeStruct((r, c), F32)
    return pl.pallas_call(
        body, name=name, grid=(r // t,), in_specs=in_specs, out_specs=[tile] * 4, out_shape=[sh] * 4,
        compiler_params=_cparams(dimension_semantics=("parallel",)),
    )(*args)


BIG = ("w_in", "w_mix_out", "w_xq", "w_xk", "w_xv", "w_xo", "w_gate", "w_up", "w_down")
COL_SHARDED = ("w_in", "w_gate", "w_up")
WEIGHTS = ("ln_in_g", "ln_in_b", "w_in", "attn_sink", "g_win", "g_dil", "w_mix_out", "ln1_g", "ln1_b",
           "mem_ln_g", "mem_ln_b", "w_xq", "w_xk", "w_xv", "w_xo", "ln2_g", "ln2_b", "w_gate", "w_up",
           "conv_w", "conv_b", "w_down", "ln3_g", "ln3_b")
SMALL = tuple(k for k in WEIGHTS if k not in BIG)
PACK_COLS = 1024
CONV_SHARD = D_FF // N_CHIPS
CONV_WIDTH_ROWS = 3
SMALL_ROWS = 32


GATHER_STAGES = {"ln_in": ("w_in", "conv_w"), "proj": ("w_mix_out", "w_xq", "w_xk", "w_xv", "w_xo", "w_up"),
                 "combine": ("w_gate", "w_down")}
EXCHANGE_STAGES = {"dh2": ("w_down", "w_up"), "attn_a": ("w_gate", "w_xo", "w_xq"),
                   "attn_b0": ("w_xk", "w_xv", "w_mix_out"), "dh0": ("w_in",)}


def _full_weight(k, g4):
    return g4.reshape(N_CHIPS * g4.shape[1], g4.shape[2])


def _grad_parts(k, gk):
    gk = gk.astype(BF16)
    return gk.reshape(N_CHIPS, gk.shape[0] // N_CHIPS, gk.shape[1])


EARLY_SWAP_STAGE = "attn_b2"


class _Plan:
    def __init__(self, shards):
        self.shards = shards
        self.recv = {}
        self.chip_sums = {}
        self.sibling_sums = {}

    def gather(self, stage):
        names = GATHER_STAGES.get(stage)
        return _ChipGather([self.shards[k] for k in names]) if names else None

    def gathered(self, stage, couts, wb):
        for k, g4 in zip(GATHER_STAGES.get(stage, ()), couts):
            if k == "conv_w":
                taps = g4[:, :CONV_WIDTH_ROWS, :CONV_SHARD]
                wb[k] = taps.transpose(1, 0, 2).reshape(CONV_WIDTH_ROWS, D_FF)
            else:
                wb[k] = _full_weight(k, g4)

    def exchange(self, stage, grads):
        if stage == EARLY_SWAP_STAGE:
            self.early = [k for k in BIG if k in self.recv]
            for k in self.early:
                self.chip_sums[k] = self.recv[k]
            return _SiblingSwap([self.chip_sums[k] for k in self.early])
        names = EXCHANGE_STAGES.get(stage)
        return _ChipExchange([_grad_parts(k, grads[k]) for k in names]) if names else None

    def exchanged(self, stage, couts):
        if stage == EARLY_SWAP_STAGE:
            self.sibling_sums.update(zip(self.early, couts))
            return
        for k, r4 in zip(EXCHANGE_STAGES.get(stage, ()), couts):
            self.recv[k] = r4


def _pack_rows(a):
    r, n = a.shape
    per = -(-n // PACK_COLS)
    return jnp.pad(a, ((0, 0), (0, per * PACK_COLS - n))).reshape(r * per, PACK_COLS)


def _unpack_rows(p, r, n):
    per = -(-n // PACK_COLS)
    return p.reshape(r, per * PACK_COLS)[:, :n]


def _pack(pieces, rows_total):
    cat = jnp.concatenate([_pack_rows(a) for a in pieces], axis=0)
    return jnp.pad(cat, ((0, rows_total - cat.shape[0]), (0, 0)))


def _unpack(p, shapes):
    out, at = [], 0
    for r, n in shapes:
        per = -(-n // PACK_COLS)
        out.append(_unpack_rows(p[at:at + r * per], r, n))
        at += r * per
    return out


def kernel(x, mem, positions, ln_in_g, ln_in_b, w_in, attn_sink, g_win, g_dil, w_mix_out, ln1_g, ln1_b, mem_ln_g, mem_ln_b, w_xq, w_xk, w_xv, w_xo, ln2_g, ln2_b, w_gate, w_up, conv_w, conv_b, w_down, ln3_g, ln3_b, loss_target, m_ln_in_g, m_ln_in_b, m_w_in, m_attn_sink, m_g_win, m_g_dil, m_w_mix_out, m_ln1_g, m_ln1_b, m_mem_ln_g, m_mem_ln_b, m_w_xq, m_w_xk, m_w_xv, m_w_xo, m_ln2_g, m_ln2_b, m_w_gate, m_w_up, m_conv_w, m_conv_b, m_w_down, m_ln3_g, m_ln3_b, v_ln_in_g, v_ln_in_b, v_w_in, v_attn_sink, v_g_win, v_g_dil, v_w_mix_out, v_ln1_g, v_ln1_b, v_mem_ln_g, v_mem_ln_b, v_w_xq, v_w_xk, v_w_xv, v_w_xo, v_ln2_g, v_ln2_b, v_w_gate, v_w_up, v_conv_w, v_conv_b, v_w_down, v_ln3_g, v_ln3_b):
    given = dict(locals())
    shape_of = {k: given[k].shape for k in WEIGHTS}
    as2d = lambda k, a: a.reshape(-1, a.shape[-1]).T if k in COL_SHARDED else a.reshape(-1, a.shape[-1])
    w2 = {k: as2d(k, given[k]) for k in WEIGHTS}
    m2 = {k: as2d(k, given["m_" + k]) for k in WEIGHTS}
    v2 = {k: as2d(k, given["v_" + k]) for k in WEIGHTS}
    chip = 2 * lax.axis_index("x") + lax.axis_index("y")

    shards = {k: w2[k].astype(BF16) for k in BIG}
    shards["conv_w"] = jnp.pad(w2["conv_w"], ((0, 16 - CONV_WIDTH_ROWS), (0, PACK_COLS - CONV_SHARD)))
    plan = _Plan(shards)
    sp = {k: w2[k] for k in SMALL if k != "conv_w"}

    grad_x, grads, small = _local_step(x[0], mem[0], positions[0], loss_target[0], {}, sp, plan)

    small_keys = ("loss",) + SMALL
    small_shapes = [small[k].shape for k in small_keys]
    small_pack = _pack([small[k] for k in small_keys], SMALL_ROWS)
    late = [k for k in BIG if k not in plan.chip_sums]
    for k in late:
        plan.chip_sums[k] = _sum_slots(plan.recv[k], name=f"sum_chips_{k}")
    *late_sibling, small_all = _comm_only(
        _Both(_SiblingSwap([plan.chip_sums[k] for k in late]), _ChipExchange([], small_pack)), "swap_and_small")
    plan.sibling_sums.update(zip(late, late_sibling))
    chip_sums = [plan.chip_sums[k] for k in BIG]
    sibling_sums = [plan.sibling_sums[k] for k in BIG]
    small_sum = _sum_slots(small_all, name="sum_small")
    small_g = dict(zip(small_keys, _unpack(small_sum, small_shapes)))
    loss = small_g["loss"][0, 0]

    res = {}
    for k, p, q in zip(BIG, chip_sums, sibling_sums):
        res[k] = _adamw(w2[k], m2[k], v2[k], p, q, name=f"adamw_{k}")
    small_g["conv_w"] = lax.dynamic_slice_in_dim(small_g["conv_w"], chip * CONV_SHARD, CONV_SHARD, axis=1)
    adam_shapes = [w2[k].shape for k in SMALL]
    packs = [_pack([d[k] for k in SMALL], SMALL_ROWS) for d in (w2, m2, v2, small_g)]
    small_res = [_unpack(o, adam_shapes) for o in _adamw(*packs, None, name="adamw_small")]
    for i, k in enumerate(SMALL):
        res[k] = tuple(o[i] for o in small_res)

    outs = [loss, grad_x[None]]
    for slot in range(4):
        outs += [(res[k][slot].T if k in COL_SHARDED else res[k][slot]).reshape(shape_of[k]) for k in WEIGHTS]
    return tuple(outs)
```

```python
import functools
import math

import jax
import jax.numpy as jnp
from jax import lax
from jax.experimental import pallas as pl
from jax.experimental.pallas import tpu as pltpu

F32 = jnp.float32
BF16 = jnp.bfloat16

D_MODEL = 1024
HEAD_DIM = 64
WIN_Q_HEADS = 8
WIN_KV_HEADS = 2
WIN_HALF = 128
DIL_SLOTS = 8
DILATIONS = (1, 4, 16)
DIL_HALF = 64
ROT_DIM = 16
ROPE_THETA = 500000.0
X_HEADS = 4
X_HEAD_DIM = 256
D_FF = 2816
A_Q = 512
A_KV = 128
A_WIDTH = A_Q + 2 * A_KV
B_QKV = 1536
IN_WIDTH = 5376
ALPHA = 2.0 ** 0.25
LN_EPS = 1e-5
NEG_INF = -1e30
LANES = 128
N_CHIPS = 4
N_DEV = 8

ADAM_LR = 0.001
ADAM_B1 = 0.9
ADAM_B2 = 0.999
ADAM_EPS = 1e-08
ADAM_WD = 0.01
ADAM_STEP = 10

VMEM_LIMIT = 56 * 1024 * 1024


def _cparams(**kw):
    return pltpu.CompilerParams(vmem_limit_bytes=VMEM_LIMIT, **kw)


def _dot(a, b):
    return lax.dot_general(a, b, (((1,), (0,)), ((), ())), preferred_element_type=F32)


def _dot_nt(a, b):
    return lax.dot_general(a, b, (((1,), (1,)), ((), ())), preferred_element_type=F32)


def _dot_tn(a, b):
    return lax.dot_general(a, b, (((0,), (0,)), ((), ())), preferred_element_type=F32)


def _ln(x, g, b):
    mu = jnp.mean(x, axis=-1, keepdims=True)
    xc = x - mu
    var = jnp.mean(xc * xc, axis=-1, keepdims=True)
    return xc * lax.rsqrt(var + LN_EPS) * g + b


def _ln_bwd_math(dy, r, g):
    mu = jnp.mean(r, axis=-1, keepdims=True)
    xc = r - mu
    var = jnp.mean(xc * xc, axis=-1, keepdims=True)
    rstd = lax.rsqrt(var + LN_EPS)
    xhat = xc * rstd
    dxhat = dy * g
    m1 = jnp.mean(dxhat, axis=-1, keepdims=True)
    m2 = jnp.mean(dxhat * xhat, axis=-1, keepdims=True)
    dr = rstd * (dxhat - m1 - xhat * m2)
    return dr, jnp.sum(dy * xhat, axis=0, keepdims=True), jnp.sum(dy, axis=0, keepdims=True)


def _rope(z, ta, tb, tc, sign):
    w = z.shape[1]
    reps = w // LANES
    a = jnp.tile(ta, (1, reps))
    b = jnp.tile(tb, (1, reps))
    c = jnp.tile(tc, (1, reps))
    return z * a + sign * (pltpu.roll(z, w - 8, 1) * b + pltpu.roll(z, 8, 1) * c)


def _shift_rows(x, prev_row, next_row):
    t = x.shape[0]
    sub = 8
    row = lax.broadcasted_iota(jnp.int32, (sub, x.shape[1]), 0)
    down, up = pltpu.roll(x, 1, 0), pltpu.roll(x, t - 1, 0)
    xm1 = jnp.concatenate([jnp.where(row == 0, prev_row, down[:sub]), down[sub:]], axis=0)
    xp1 = jnp.concatenate([up[:t - sub], jnp.where(row == sub - 1, next_row, up[t - sub:])], axis=0)
    return xm1, xp1


def _rope_tabs(cs, e_mat):
    hi = cs.astype(BF16)
    rest = cs - hi.astype(F32)
    mid = rest.astype(BF16)
    lo = (rest - mid.astype(F32)).astype(BF16)
    tabs = _dot(hi, e_mat) + _dot(mid, e_mat) + _dot(lo, e_mat)
    lane = lax.broadcasted_iota(jnp.int32, (cs.shape[0], LANES), 1)
    ones = jnp.where((lane & (HEAD_DIM - 1)) >= ROT_DIM, 1.0, 0.0)
    return tabs[:, :LANES] + ones, tabs[:, LANES:2 * LANES], tabs[:, 2 * LANES:]


def _rope_select_matrix():
    half = ROT_DIM // 2
    e = [[0.0] * (3 * LANES) for _ in range(ROT_DIM)]
    for lane in range(LANES):
        d = lane % HEAD_DIM
        if d < half:
            e[d][lane] = 1.0
            e[half + d][LANES + lane] = -1.0
        elif d < ROT_DIM:
            e[d - half][lane] = 1.0
            e[d][2 * LANES + lane] = 1.0
    return jnp.array(e, BF16)


def _rope_rows(x, cos_t, sin_t, sign):
    half = ROT_DIM // 2
    parts = []
    for base in (0, HEAD_DIM):
        r1, r2 = x[base:base + half], x[base + half:base + ROT_DIM]
        parts += [r1 * cos_t - sign * (r2 * sin_t), r2 * cos_t + sign * (r1 * sin_t), x[base + ROT_DIM:base + HEAD_DIM]]
    return jnp.concatenate(parts, axis=0)


MESH_IDS = pl.DeviceIdType.MESH
ANY = pl.BlockSpec(memory_space=pl.ANY)


def _place():
    x, y, c = lax.axis_index("x"), lax.axis_index("y"), lax.axis_index("c")
    other_chips = [(1 - x, y), (x, 1 - y), (1 - x, 1 - y)]
    return x, y, c, other_chips


class _ChipGather:
    def __init__(self, shards):
        self.inputs = list(shards)
        n = len(shards)
        self.out_shape = [jax.ShapeDtypeStruct((N_CHIPS,) + a.shape, a.dtype) for a in shards]
        self.scratch = [pltpu.SemaphoreType.DMA((6 * n,)), pltpu.SemaphoreType.DMA((6 * n,)),
                        pltpu.SemaphoreType.DMA((n,))]

    def _copies(self, src, dst, sems):
        send_sems, recv_sems, local_sems = sems
        x, y, c, chips = _place()
        mine = 2 * x + y
        n = len(src)
        local, sends, recvs, passes, pass_recvs = [], [], [], [], []
        for a in range(n):
            half = src[a].shape[0] // 2
            my_rows, other_rows = pl.ds(c * half, half), pl.ds((1 - c) * half, half)
            local.append(pltpu.make_async_copy(src[a], dst[a].at[mine], local_sems.at[a]))
            for j, (px, py) in enumerate(chips):
                k, k2, slot = 3 * a + j, 3 * n + 3 * a + j, 2 * px + py
                sends.append(pltpu.make_async_remote_copy(
                    src_ref=src[a].at[my_rows], dst_ref=dst[a].at[mine, my_rows], send_sem=send_sems.at[k],
                    recv_sem=recv_sems.at[k], device_id=(px, py, c), device_id_type=MESH_IDS))
                recvs.append(pltpu.make_async_remote_copy(
                    src_ref=src[a].at[my_rows], dst_ref=dst[a].at[slot, my_rows], send_sem=send_sems.at[k],
                    recv_sem=recv_sems.at[k], device_id=(px, py, c), device_id_type=MESH_IDS))
                passes.append(pltpu.make_async_remote_copy(
                    src_ref=dst[a].at[slot, my_rows], dst_ref=dst[a].at[slot, my_rows], send_sem=send_sems.at[k2],
                    recv_sem=recv_sems.at[k2], device_id=(x, y, 1 - c), device_id_type=MESH_IDS))
                pass_recvs.append(pltpu.make_async_remote_copy(
                    src_ref=dst[a].at[slot, my_rows], dst_ref=dst[a].at[slot, other_rows],
                    send_sem=send_sems.at[k2], recv_sem=recv_sems.at[k2], device_id=(x, y, 1 - c),
                    device_id_type=MESH_IDS))
        return local, sends, recvs, passes, pass_recvs

    def start(self, src, dst, sems):
        local, sends, _, _, _ = self._copies(src, dst, sems)
        for cp in local + sends:
            cp.start()

    def wait(self, src, dst, sems):
        local, sends, recvs, passes, pass_recvs = self._copies(src, dst, sems)
        for idx, landed in enumerate(recvs):
            landed.wait_recv()
            if passes:
                passes[idx].start()
        for cp in pass_recvs:
            cp.wait_recv()
        for cp in sends + passes:
            cp.wait_send()
        for cp in local:
            cp.wait()


class _ChipExchange:
    def __init__(self, parts, small=None):
        self.inputs = list(parts) + ([small] if small is not None else [])
        self.n = len(parts)
        self.has_small = small is not None
        self.out_shape = [jax.ShapeDtypeStruct(a.shape, a.dtype) for a in parts]
        n_sem, n_loc = 3 * self.n, self.n
        if self.has_small:
            self.out_shape.append(jax.ShapeDtypeStruct((N_DEV,) + small.shape, small.dtype))
            n_sem, n_loc = n_sem + N_DEV - 1, n_loc + 1
        self.scratch = [pltpu.SemaphoreType.DMA((n_sem,)), pltpu.SemaphoreType.DMA((n_sem,)),
                        pltpu.SemaphoreType.DMA((n_loc,))]

    def _copies(self, src, dst, sems):
        send_sems, recv_sems, local_sems = sems
        x, y, c, chips = _place()
        mine = 2 * x + y
        n = self.n
        local, sends, recvs = [], [], []
        for a in range(n):
            local.append(pltpu.make_async_copy(src[a].at[mine], dst[a].at[mine], local_sems.at[a]))
            for j, (px, py) in enumerate(chips):
                k = 3 * a + j
                sends.append(pltpu.make_async_remote_copy(
                    src_ref=src[a].at[2 * px + py], dst_ref=dst[a].at[mine], send_sem=send_sems.at[k],
                    recv_sem=recv_sems.at[k], device_id=(px, py, c), device_id_type=MESH_IDS))
                recvs.append(pltpu.make_async_remote_copy(
                    src_ref=src[a].at[mine], dst_ref=dst[a].at[2 * px + py], send_sem=send_sems.at[k],
                    recv_sem=recv_sems.at[k], device_id=(px, py, c), device_id_type=MESH_IDS))
        if self.has_small:
            me_dev = 4 * x + 2 * y + c
            local.append(pltpu.make_async_copy(src[n], dst[n].at[me_dev], local_sems.at[n]))
            for mask in range(1, N_DEV):
                px, py, pc = x ^ ((mask >> 2) & 1), y ^ ((mask >> 1) & 1), c ^ (mask & 1)
                k = 3 * n + mask - 1
                sends.append(pltpu.make_async_remote_copy(
                    src_ref=src[n], dst_ref=dst[n].at[me_dev], send_sem=send_sems.at[k], recv_sem=recv_sems.at[k],
                    device_id=(px, py, pc), device_id_type=MESH_IDS))
                recvs.append(pltpu.make_async_remote_copy(
                    src_ref=src[n], dst_ref=dst[n].at[4 * px + 2 * py + pc], send_sem=send_sems.at[k],
                    recv_sem=recv_sems.at[k], device_id=(px, py, pc), device_id_type=MESH_IDS))
        return local, sends, recvs, [], []

    start = _ChipGather.start
    wait = _ChipGather.wait


def _pcall(body, *, name, grid, in_specs, out_specs, out_shape, args, scratch_shapes=(), dims=None, comm=None):
    in_specs, out_specs, out_shape = list(in_specs), list(out_specs), list(out_shape)
    scratch_shapes = list(scratch_shapes)
    if comm is None:
        outs = pl.pallas_call(
            body, name=name, grid=grid, in_specs=in_specs, out_specs=out_specs, out_shape=out_shape,
            scratch_shapes=scratch_shapes, compiler_params=_cparams(dimension_semantics=dims),
        )(*args)
        return list(outs), []
    n_in, n_out, n_scr = len(in_specs), len(out_specs), len(scratch_shapes)
    n_cin, n_cout = len(comm.inputs), len(comm.out_shape)

    def wrapped(*refs):
        ins, refs = refs[:n_in], refs[n_in:]
        cins, refs = refs[:n_cin], refs[n_cin:]
        outs, refs = refs[:n_out], refs[n_out:]
        couts, refs = refs[:n_cout], refs[n_cout:]
        scr, csems = refs[:n_scr], refs[n_scr:]
        first = last = None
        for axis, size in enumerate(grid):
            pid = pl.program_id(axis)
            f, l = pid == 0, pid == size - 1
            first = f if first is None else first & f
            last = l if last is None else last & l

        @pl.when(first)
        def _():
            comm.start(cins, couts, csems)

        body(*ins, *outs, *scr)

        @pl.when(last)
        def _():
            comm.wait(cins, couts, csems)

    res = pl.pallas_call(
        wrapped, name=name, grid=grid, in_specs=in_specs + [ANY] * n_cin, out_specs=out_specs + [ANY] * n_cout,
        out_shape=out_shape + list(comm.out_shape), scratch_shapes=scratch_shapes + list(comm.scratch),
        compiler_params=_cparams(dimension_semantics=("arbitrary",) * len(grid)),
    )(*args, *comm.inputs)
    return list(res[:n_out]), list(res[n_out:])


def _comm_only(comm, name):
    def body(*refs):
        n_cin, n_cout = len(comm.inputs), len(comm.out_shape)
        cins, couts, csems = refs[:n_cin], refs[n_cin:n_cin + n_cout], refs[n_cin + n_cout:]
        comm.start(cins, couts, csems)
        comm.wait(cins, couts, csems)

    return list(pl.pallas_call(
        body, name=name, in_specs=[ANY] * len(comm.inputs), out_specs=[ANY] * len(comm.out_shape),
        out_shape=list(comm.out_shape), scratch_shapes=list(comm.scratch),
    )(*comm.inputs))


def _mm(a, b, *, mode, out_dtype, tm, tn, tk=None, name):
    if mode == "nt":
        m, k = a.shape
        n = b.shape[0]
        assert m % tm == 0 and n % tn == 0

        def body(a_ref, b_ref, o_ref):
            o_ref[...] = _dot_nt(a_ref[...], b_ref[...]).astype(out_dtype)

        return pl.pallas_call(
            body, name=name, grid=(m // tm, n // tn),
            in_specs=[pl.BlockSpec((tm, k), lambda i, j: (i, 0)), pl.BlockSpec((tn, k), lambda i, j: (j, 0))],
            out_specs=pl.BlockSpec((tm, tn), lambda i, j: (i, j)),
            out_shape=jax.ShapeDtypeStruct((m, n), out_dtype),
            compiler_params=_cparams(dimension_semantics=("parallel", "parallel")),
        )(a, b)
    assert mode == "tn"
    kk, m = a.shape
    n = b.shape[1]
    assert m % tm == 0 and n % tn == 0 and kk % tk == 0
    nk = kk // tk

    def body(a_ref, b_ref, o_ref, acc_ref):
        kstep = pl.program_id(2)

        @pl.when(kstep == 0)
        def _():
            acc_ref[...] = jnp.zeros_like(acc_ref)

        acc_ref[...] += _dot_tn(a_ref[...], b_ref[...])

        @pl.when(kstep == nk - 1)
        def _():
            o_ref[...] = acc_ref[...].astype(out_dtype)

    return pl.pallas_call(
        body, name=name, grid=(m // tm, n // tn, nk),
        in_specs=[pl.BlockSpec((tk, tm), lambda i, j, s: (s, i)), pl.BlockSpec((tk, tn), lambda i, j, s: (s, j))],
        out_specs=pl.BlockSpec((tm, tn), lambda i, j, s: (i, j)),
        out_shape=jax.ShapeDtypeStruct((m, n), out_dtype),
        scratch_shapes=[pltpu.VMEM((tm, tn), F32)],
        compiler_params=_cparams(dimension_semantics=("parallel", "parallel", "arbitrary")),
    )(a, b)


PROJ_COLS = 256


def _proj_segments():
    wd = DIL_SLOTS * HEAD_DIM
    segs = [(1, [(0, 1), (PROJ_COLS, 1), (2 * PROJ_COLS, 2)])]
    for gi, dil in enumerate(DILATIONS):
        blocks = []
        for part, kind in enumerate((1, 1, 0)):
            col = A_WIDTH + part * B_QKV + gi * wd
            blocks += [(col, kind), (col + PROJ_COLS, kind)]
        segs.append((dil, blocks))
    return segs


PROJ_SEGMENTS = _proj_segments()


def _dh0_ln_in(dz, w_t, dr1, x, ln_in_g, *, t, comm=None):
    s, k = dz.shape

    def body(dz_ref, w_ref, dr1_ref, x_ref, g_ref, gx_ref, st_ref):
        i = pl.program_id(0)

        @pl.when(i == 0)
        def _():
            st_ref[...] = jnp.zeros_like(st_ref)

        dh0 = _dot(dz_ref[...], w_ref[...]) + ALPHA * dr1_ref[...]
        dx, dg, db = _ln_bwd_math(dh0, x_ref[...], g_ref[...])
        gx_ref[...] = dx
        st_ref[0:1, :] += dg
        st_ref[1:2, :] += db

    tile = pl.BlockSpec((t, D_MODEL), lambda i: (i, 0))
    return _pcall(
        body, name="dh0_ln_in", grid=(s // t,),
        in_specs=[pl.BlockSpec((t, k), lambda i: (i, 0)),
                  pl.BlockSpec((k, D_MODEL), lambda i: (0, 0), pipeline_mode=pl.Buffered(1)),
                  tile, tile, pl.BlockSpec((1, D_MODEL), lambda i: (0, 0))],
        out_specs=[tile, pl.BlockSpec((8, D_MODEL), lambda i: (0, 0))],
        out_shape=[jax.ShapeDtypeStruct((s, D_MODEL), F32), jax.ShapeDtypeStruct((8, D_MODEL), F32)],
        args=[dz, w_t, dr1, x, ln_in_g], dims=("arbitrary",), comm=comm)


def _ln_in_fwd(x, g, b, *, t, comm=None):
    s = x.shape[0]

    def body(x_ref, g_ref, b_ref, o_ref):
        o_ref[...] = _ln(x_ref[...], g_ref[...], b_ref[...]).astype(BF16)

    row = pl.BlockSpec((1, D_MODEL), lambda i: (0, 0))
    tile = pl.BlockSpec((t, D_MODEL), lambda i: (i, 0))
    outs, couts = _pcall(body, name="ln_in_fwd", grid=(s // t,), in_specs=[tile, row, row], out_specs=[tile],
                         out_shape=[jax.ShapeDtypeStruct((s, D_MODEL), BF16)], args=[x, g, b], dims=("parallel",),
                         comm=comm)
    return outs[0], couts


def _proj_all(h0b, w_t, cs, e_mat, *, t, comm=None):
    s = h0b.shape[0]
    cb = PROJ_COLS
    halves = cb // LANES

    def body(h_ref, w_ref, cs_ref, e_ref, *rest):
        z_refs, scr = rest[:-1], rest[-1]
        h = h_ref[...]
        ta, tb, tc = (jnp.tile(tab, (1, halves)) for tab in _rope_tabs(cs_ref[...], e_ref[...]))
        lane = lax.broadcasted_iota(jnp.int32, (t, cb), 1)
        slot = 0
        for z_ref, (dil, blocks) in zip(z_refs, PROJ_SEGMENTS):
            for jb, (col, kind) in enumerate(blocks):
                acc = _dot_nt(h, w_ref[col:col + cb, :])
                if kind:
                    z = acc * ta + (pltpu.roll(acc, cb - 8, 1) * tb + pltpu.roll(acc, 8, 1) * tc)
                    if kind == 2:
                        z = jnp.where(lane < LANES, z, acc)
                else:
                    z = acc
                if dil == 1:
                    z_ref[0, :, cb * jb:cb * (jb + 1)] = z.astype(BF16)
                    continue
                for half in range(halves):
                    scr[slot, half] = z[:, half * LANES:(half + 1) * LANES]
                for c in range(dil):
                    for half in range(halves):
                        rows = scr[slot, half, pl.ds(c, t // dil, stride=dil), :]
                        z_ref[c, :, cb * jb + half * LANES:cb * jb + (half + 1) * LANES] = rows.astype(BF16)
                slot = 1 - slot

    widths = [cb * len(blocks) for _, blocks in PROJ_SEGMENTS]
    dils = [dil for dil, _ in PROJ_SEGMENTS]
    outs, couts = _pcall(
        body, name="proj_all", grid=(s // t,),
        in_specs=[pl.BlockSpec((t, D_MODEL), lambda i: (i, 0)),
                  pl.BlockSpec((IN_WIDTH, D_MODEL), lambda i: (0, 0), pipeline_mode=pl.Buffered(1)),
                  pl.BlockSpec((t, ROT_DIM), lambda i: (i, 0)), pl.BlockSpec((ROT_DIM, 3 * LANES), lambda i: (0, 0))],
        out_specs=[pl.BlockSpec((dil, t // dil, wd), lambda i: (0, i, 0)) for dil, wd in zip(dils, widths)],
        out_shape=[jax.ShapeDtypeStruct((dil, s // dil, wd), BF16) for dil, wd in zip(dils, widths)],
        args=[h0b, w_t, cs, e_mat], scratch_shapes=[pltpu.VMEM((2, halves, t, LANES), F32)],
        dims=("parallel",), comm=comm)
    return outs, couts


PAIR = 2 * HEAD_DIM
ONE_TILE_ROWS = 2048


def _place_head(x2, src_pos, dst_pos):
    hi = lax.broadcasted_iota(jnp.int32, x2.shape, 1) >= HEAD_DIM
    src = x2 if src_pos == dst_pos else pltpu.roll(x2, HEAD_DIM, 1)
    return jnp.where(hi == (dst_pos == 1), src, jnp.zeros_like(src))


def _band_mask_t(row0, tq, w, seq_len):
    tk = tq + 2 * w
    kk = lax.broadcasted_iota(jnp.int32, (tk, tq), 0)
    qq = lax.broadcasted_iota(jnp.int32, (tk, tq), 1)
    kpos = row0 - w + kk
    return (jnp.abs(qq + w - kk) <= w) & (kpos >= 0) & (kpos < seq_len)


def _halo_kv_specs(t, w, hkv, n, seq_len, kcol, vcol):
    kw = hkv * HEAD_DIM
    per, last = t // w, seq_len // w - 1
    cur = lambda s, i: jnp.minimum(i, n - 1)
    specs = []
    for c in (kcol, vcol):
        specs += [pl.BlockSpec((None, w, kw), lambda s, i, c=c: (s, jnp.maximum(cur(s, i) * per - 1, 0), c)),
                  pl.BlockSpec((None, t, kw), lambda s, i, c=c: (s, cur(s, i), c)),
                  pl.BlockSpec((None, w, kw), lambda s, i, c=c: (s, jnp.minimum((cur(s, i) + 1) * per, last), c))]
    return specs, cur


def _pair_kv(kfull, vfull, qp, rep, krows):
    ks, vs, a_of = [], [], []
    for pos in range(2):
        g = (2 * qp + pos) // rep
        a_of.append(g // 2)
        ks.append(_place_head(kfull[g // 2][krows], g % 2, pos))
        vs.append(_place_head(vfull[g // 2][krows], g % 2, pos))
    assert a_of[0] == a_of[1]
    return jnp.concatenate(ks, axis=0), jnp.concatenate(vs, axis=0), a_of[0]


def _swa_fwd_p(qkv, *, qcol, kcol, vcol, hq, hkv, w, tq, sub, sink, name, comm=None):
    nseq, seq_len, _ = qkv.shape
    t = tq * sub
    n = seq_len // t
    rep = hq // hkv
    tk = tq + 2 * w
    kv_specs, cur = _halo_kv_specs(t, w, hkv, n, seq_len, kcol, vcol)

    def body(*refs):
        if sink is not None:
            sink_ref, refs = refs[0], refs[1:]
        q_ref, kp_ref, kc_ref, kn_ref, vp_ref, vc_ref, vn_ref, o_ref, lse_ref = refs
        i = pl.program_id(1)
        kfull, vfull = [], []
        for a in range(hkv // 2):
            ls = slice(a * PAIR, (a + 1) * PAIR)
            kfull.append(jnp.concatenate([kp_ref[:, ls], kc_ref[:, ls], kn_ref[:, ls]], axis=0) * 0.125)
            vfull.append(jnp.concatenate([vp_ref[:, ls], vc_ref[:, ls], vn_ref[:, ls]], axis=0))
        row_hi = lax.broadcasted_iota(jnp.int32, (PAIR, tq), 0) >= HEAD_DIM
        for jj in range(sub):
            rows = slice(jj * tq, (jj + 1) * tq)
            mask_t = _band_mask_t(i * t + jj * tq, tq, w, seq_len)
            o_t, lse_rows = [], []
            for qp in range(hq // 2):
                kst, vst, _ = _pair_kv(kfull, vfull, qp, rep, slice(jj * tq, jj * tq + tk))
                s2 = _dot_nt(kst, q_ref[rows, qp * PAIR:(qp + 1) * PAIR])
                ps, dens = [], []
                for pos in range(2):
                    h = 2 * qp + pos
                    s_t = jnp.where(mask_t, s2[pos * tk:(pos + 1) * tk], NEG_INF)
                    m = jnp.max(s_t, axis=0, keepdims=True)
                    if sink is not None:
                        m = jnp.maximum(m, sink_ref[0, h])
                    p_t = jnp.exp(s_t - m)
                    den = jnp.sum(p_t, axis=0, keepdims=True)
                    if sink is not None:
                        den = den + jnp.exp(sink_ref[0, h] - m)
                    ps.append(p_t.astype(BF16))
                    dens.append(den)
                    lse_rows.append(m + jnp.log(den))
                both = _dot_tn(vst, jnp.concatenate(ps, axis=0))
                o_t.append(both / jnp.where(row_hi, dens[1], dens[0]))
            o_ref[rows, :] = jnp.concatenate(o_t, axis=0).T
            lse_ref[:, rows] = jnp.concatenate(lse_rows, axis=0)

    in_specs = [pl.BlockSpec((None, t, hq * HEAD_DIM), lambda s, i: (s, i, qcol))] + kv_specs
    args = [qkv] * 7
    if sink is not None:
        in_specs = [pl.BlockSpec(memory_space=pltpu.SMEM)] + in_specs
        args = [sink] + args
    (o, lse), couts = _pcall(
        body, name=name, grid=(nseq, n), in_specs=in_specs,
        out_specs=[pl.BlockSpec((None, t, hq * HEAD_DIM), lambda s, i: (s, i, 0)),
                   pl.BlockSpec((None, hq, t), lambda s, i: (s, 0, i))],
        out_shape=[jax.ShapeDtypeStruct((nseq, seq_len, hq * HEAD_DIM), F32),
                   jax.ShapeDtypeStruct((nseq, hq, seq_len), F32)],
        args=args, dims=("parallel", "parallel"), comm=comm)
    return o, lse, couts


def _swa_bwd_p(qkv, do, lse, delta, cs, e_mat, *, qcol, kcol, vcol, hq, hkv, w, tq, sub, sink, name, comm=None):
    nseq, seq_len, _ = qkv.shape
    t = tq * sub
    n = seq_len // t
    rep = hq // hkv
    qw, kw = hq * HEAD_DIM, hkv * HEAD_DIM
    tk = tq + 2 * w
    kv_specs, cur = _halo_kv_specs(t, w, hkv, n, seq_len, kcol, vcol)

    def body(*refs):
        if sink is not None:
            sink_ref, refs = refs[0], refs[1:]
        (q_ref, kp_ref, kc_ref, kn_ref, vp_ref, vc_ref, vn_ref, do_ref, lse_ref, dl_ref,
         cs_c, cs_p, e_ref) = refs[:13]
        outs = refs[13:]
        dq_ref, dk_ref, dv_ref = outs[:3]
        dsink_ref = outs[3] if sink is not None else None
        dk_win, dv_win = outs[-2:]
        dk_acc, dv_acc = outs[-4:-2] if n > 1 else (None, None)
        s_id = pl.program_id(0)
        i = pl.program_id(1)
        slot_p, slot_c, slot_n = (i + 2) % 3, i % 3, (i + 1) % 3

        if sink is not None:
            @pl.when((s_id == 0) & (i == 0))
            def _():
                dsink_ref[...] = jnp.zeros_like(dsink_ref)

        @pl.when(i < n)
        def _():
            dk_win[...] = jnp.zeros_like(dk_win)
            dv_win[...] = jnp.zeros_like(dv_win)
            kfull, vfull = [], []
            for a in range(hkv // 2):
                ls = slice(a * PAIR, (a + 1) * PAIR)
                kfull.append(jnp.concatenate([kp_ref[:, ls], kc_ref[:, ls], kn_ref[:, ls]], axis=0) * 0.125)
                vfull.append(jnp.concatenate([vp_ref[:, ls], vc_ref[:, ls], vn_ref[:, ls]], axis=0))
            for jj in range(sub):
                rows = slice(jj * tq, (jj + 1) * tq)
                krows = slice(jj * tq, jj * tq + tk)
                mask_t = _band_mask_t(i * t + jj * tq, tq, w, seq_len)
                dq_t = []
                dk2 = [None] * (hkv // 2)
                dv2 = [None] * (hkv // 2)
                for qp in range(hq // 2):
                    kst, vst, a = _pair_kv(kfull, vfull, qp, rep, krows)
                    q2 = q_ref[rows, qp * PAIR:(qp + 1) * PAIR]
                    do2 = do_ref[rows, qp * PAIR:(qp + 1) * PAIR]
                    s2 = _dot_nt(kst, q2)
                    dp2 = _dot_nt(vst, do2)
                    ds, ps, q_at, do_at = [], [], [], []
                    for pos in range(2):
                        h = 2 * qp + pos
                        e = (h // rep) % 2
                        half = slice(pos * tk, (pos + 1) * tk)
                        lse_h = lse_ref[h:h + 1, rows]
                        dl_h = dl_ref[h:h + 1, rows]
                        p_t = jnp.exp(jnp.where(mask_t, s2[half], NEG_INF) - lse_h)
                        ds.append((p_t * (dp2[half] - dl_h)).astype(BF16))
                        ps.append(p_t.astype(BF16))
                        q_at.append(_place_head(q2, pos, e) * 0.125)
                        do_at.append(_place_head(do2, pos, e))
                        if sink is not None:
                            ds_sink = -jnp.sum(jnp.exp(sink_ref[0, h] - lse_h) * dl_h)
                            dsink_ref[h:h + 1, :] += jnp.full((1, LANES), ds_sink, F32)
                    dq_t.append(_rope_rows(_dot_tn(kst, jnp.concatenate(ds, axis=0)),
                                           cs_c[0:ROT_DIM // 2, rows], cs_c[ROT_DIM // 2:ROT_DIM, rows], -1.0))
                    dk_part = _dot(jnp.concatenate(ds, axis=1), jnp.concatenate(q_at, axis=0))
                    dv_part = _dot(jnp.concatenate(ps, axis=1), jnp.concatenate(do_at, axis=0))
                    dk2[a] = dk_part if dk2[a] is None else dk2[a] + dk_part
                    dv2[a] = dv_part if dv2[a] is None else dv2[a] + dv_part
                for a in range(hkv // 2):
                    ls = slice(a * PAIR, (a + 1) * PAIR)
                    dk_win[krows, ls] += dk2[a]
                    dv_win[krows, ls] += dv2[a]
                dq_ref[rows, :] = jnp.concatenate(dq_t, axis=0).T.astype(BF16)

            if n == 1:
                dk_ref[...] = _rope(dk_win[w:w + t, :], *_rope_tabs(cs_p[...], e_ref[...]), -1.0).astype(BF16)
                dv_ref[...] = dv_win[w:w + t, :].astype(BF16)
                return

            @pl.when(i > 0)
            def _():
                dk_acc[slot_p, t - w:, :] += dk_win[:w, :]
                dv_acc[slot_p, t - w:, :] += dv_win[:w, :]

            @pl.when(i == 0)
            def _():
                dk_acc[slot_c] = dk_win[w:w + t, :]
                dv_acc[slot_c] = dv_win[w:w + t, :]

            @pl.when(i > 0)
            def _():
                dk_acc[slot_c] += dk_win[w:w + t, :]
                dv_acc[slot_c] += dv_win[w:w + t, :]

            dk_acc[slot_n] = jnp.zeros((t, kw), F32)
            dv_acc[slot_n] = jnp.zeros((t, kw), F32)
            dk_acc[slot_n, :w, :] = dk_win[w + t:, :]
            dv_acc[slot_n, :w, :] = dv_win[w + t:, :]

        if n > 1:
            @pl.when(i >= 1)
            def _():
                dk_ref[...] = _rope(dk_acc[slot_p], *_rope_tabs(cs_p[...], e_ref[...]), -1.0).astype(BF16)
                dv_ref[...] = dv_acc[slot_p].astype(BF16)

    row_c = lambda width: pl.BlockSpec((None, t, width), lambda s, i: (s, cur(s, i), 0))
    row_p = lambda width: pl.BlockSpec((None, t, width), lambda s, i: (s, jnp.maximum(i - 1, 0), 0))
    stat = pl.BlockSpec((None, hq, t), lambda s, i: (s, 0, cur(s, i)))
    cs_rows = pl.BlockSpec((None, ROT_DIM, t), lambda s, i: (s, 0, cur(s, i)))
    in_specs = ([pl.BlockSpec((None, t, qw), lambda s, i: (s, cur(s, i), qcol))] + kv_specs
                + [row_c(qw), stat, stat, cs_rows, row_p(ROT_DIM),
                   pl.BlockSpec((ROT_DIM, 3 * LANES), lambda s, i: (0, 0))])
    args = [qkv] * 7 + [do, lse, delta, cs.transpose(0, 2, 1), cs, e_mat]
    out_specs = [row_c(qw), row_p(kw), row_p(kw)]
    out_shape = [jax.ShapeDtypeStruct((nseq, seq_len, qw), BF16),
                 jax.ShapeDtypeStruct((nseq, seq_len, kw), BF16),
                 jax.ShapeDtypeStruct((nseq, seq_len, kw), BF16)]
    if sink is not None:
        in_specs = [pl.BlockSpec(memory_space=pltpu.SMEM)] + in_specs
        args = [sink] + args
        out_specs.append(pl.BlockSpec((8, LANES), lambda s, i: (0, 0)))
        out_shape.append(jax.ShapeDtypeStruct((8, LANES), F32))
    return _pcall(
        body, name=name, grid=(nseq, n + 1 if n > 1 else 1), in_specs=in_specs, out_specs=out_specs,
        out_shape=out_shape,
        scratch_shapes=([pltpu.VMEM((3, t, kw), F32), pltpu.VMEM((3, t, kw), F32)] if n > 1 else [])
        + [pltpu.VMEM((t + 2 * w, kw), F32), pltpu.VMEM((t + 2 * w, kw), F32)], args=args,
        dims=("arbitrary", "arbitrary"), comm=comm)


def _rms_parts(o, g):
    ms = jnp.mean(o * o, axis=-1, keepdims=True) + LN_EPS
    rinv = lax.rsqrt(ms)
    return o * rinv * g, rinv


def _from_subsequences(ref, scr, dil, t):
    slabs = ref.shape[-1] // LANES
    if dil == 1:
        return ref[0].astype(F32)
    for c in range(dil):
        for sl in range(slabs):
            scr[sl, pl.ds(c, t // dil, stride=dil), :] = ref[c, :, sl * LANES:(sl + 1) * LANES].astype(F32)
    return jnp.concatenate([scr[sl] for sl in range(slabs)], axis=1)


def _to_subsequences(val, ref, scr, dil, t):
    slabs = val.shape[-1] // LANES
    if dil == 1:
        ref[0] = val.astype(ref.dtype)
        return
    for sl in range(slabs):
        scr[sl] = val[:, sl * LANES:(sl + 1) * LANES]
    for c in range(dil):
        for sl in range(slabs):
            ref[c, :, sl * LANES:(sl + 1) * LANES] = scr[sl, pl.ds(c, t // dil, stride=dil), :].astype(ref.dtype)


def _combine_fwd(out_a, o_g, lse_g, g_win, g_dil, w_mix_b, x, ln_in_g, ln_in_b, ln1_g, ln1_b, *, t, comm=None):
    s = out_a.shape[1]
    wd = DIL_SLOTS * HEAD_DIM

    def body(oa_ref, o0, o1, o2, l0, l1, l2, gw_ref, gd_ref, w_ref, x_ref, g0, b0, g1, b1,
             mixed_ref, ob_ref, lt_ref, r1_ref, h1_ref, scr):
        ls = [l0[...], l1[...], l2[...]]
        mx = jnp.maximum(jnp.maximum(ls[0], ls[1]), ls[2])
        ws = [jnp.exp(l - mx) for l in ls]
        tot = ws[0] + ws[1] + ws[2]
        lt_ref[...] = mx + jnp.log(tot)
        ws = [x / tot for x in ws]
        og = [_from_subsequences(o_ref, scr.at[gi], dil, t)
              for gi, (o_ref, dil) in enumerate(zip((o0, o1, o2), DILATIONS))]
        parts = []
        for h in range(DIL_SLOTS):
            hs = slice(h * HEAD_DIM, (h + 1) * HEAD_DIM)
            parts.append(ws[0][:, h:h + 1] * og[0][:, hs] + ws[1][:, h:h + 1] * og[1][:, hs]
                         + ws[2][:, h:h + 1] * og[2][:, hs])
        ob = jnp.concatenate(parts, axis=1)
        ob_ref[...] = ob
        na, _ = _rms_parts(oa_ref[...], gw_ref[...])
        nb, _ = _rms_parts(ob, gd_ref[...])
        mixed = jnp.concatenate([na.astype(BF16), nb.astype(BF16)], axis=1)
        mixed_ref[...] = mixed
        h0 = _ln(x_ref[...], g0[...], b0[...])
        r1 = ALPHA * h0 + _dot(mixed, w_ref[...])
        r1_ref[...] = r1
        h1_ref[...] = _ln(r1, g1[...], b1[...]).astype(BF16)

    half = pl.BlockSpec((t, wd), lambda i: (i, 0))
    full = pl.BlockSpec((t, D_MODEL), lambda i: (i, 0))
    lanes = pl.BlockSpec((t, LANES), lambda i: (i, 0))
    grow = pl.BlockSpec((1, wd), lambda i: (0, 0))
    row = pl.BlockSpec((1, D_MODEL), lambda i: (0, 0))
    subseq = [pl.BlockSpec((dil, t // dil, wd), lambda i: (0, i, 0)) for dil in DILATIONS]
    return _pcall(
        body, name="combine_fwd", grid=(s // t,),
        in_specs=[pl.BlockSpec((None, t, wd), lambda i: (0, i, 0))] + subseq
        + [lanes, lanes, lanes, grow, grow, pl.BlockSpec((D_MODEL, D_MODEL), lambda i: (0, 0)), full,
           row, row, row, row],
        out_specs=[full, half, lanes, full, full],
        out_shape=[jax.ShapeDtypeStruct((s, D_MODEL), BF16), jax.ShapeDtypeStruct((s, wd), F32),
                   jax.ShapeDtypeStruct((s, LANES), F32), jax.ShapeDtypeStruct((s, D_MODEL), F32),
                   jax.ShapeDtypeStruct((s, D_MODEL), BF16)],
        scratch_shapes=[pltpu.VMEM((len(DILATIONS), wd // LANES, t, LANES), F32)],
        args=[out_a, *o_g, *lse_g, g_win, g_dil, w_mix_b, x, ln_in_g, ln_in_b, ln1_g, ln1_b], dims=("parallel",),
        comm=comm)


def _combine_bwd(dr1b, w_mix_b, out_a, out_b, g_win, g_dil, *, t):
    s = out_b.shape[0]
    wd = DIL_SLOTS * HEAD_DIM

    def body(dr_ref, w_ref, oa_ref, ob_ref, gw_ref, gd_ref, doa_ref, dob0, dob1, dob2, dla_ref, dlb_ref, st_ref,
             scr):
        i = pl.program_id(0)
        dm = _dot_nt(dr_ref[...], w_ref[...])

        @pl.when(i == 0)
        def _():
            st_ref[...] = jnp.zeros_like(st_ref)

        lane = lax.broadcasted_iota(jnp.int32, (t, LANES), 1)
        for idx, (o_ref, g_ref, dl_ref) in enumerate(((oa_ref, gw_ref, dla_ref), (ob_ref, gd_ref, dlb_ref))):
            o = o_ref[...]
            dn = dm[:, idx * wd:(idx + 1) * wd]
            _, rinv = _rms_parts(o, g_ref[...])
            wv = dn * g_ref[...]
            do = rinv * wv - o * (rinv * rinv * rinv) * jnp.mean(wv * o, axis=-1, keepdims=True)
            st_ref[idx:idx + 1, :] += jnp.sum(dn * o * rinv, axis=0, keepdims=True)
            if idx == 0:
                doa_ref[...] = do.astype(BF16)
            else:
                for do_ref, dil in zip((dob0, dob1, dob2), DILATIONS):
                    _to_subsequences(do, do_ref, scr, dil, t)
            prod = do * o
            acc = jnp.zeros((t, LANES), F32)
            for h in range(DIL_SLOTS):
                hs = slice(h * HEAD_DIM, (h + 1) * HEAD_DIM)
                acc = jnp.where(lane == h, jnp.sum(prod[:, hs], axis=1, keepdims=True), acc)
            dl_ref[...] = acc

    half = pl.BlockSpec((t, wd), lambda i: (i, 0))
    lanes = pl.BlockSpec((t, LANES), lambda i: (i, 0))
    grow = pl.BlockSpec((1, wd), lambda i: (0, 0))
    a_spec = pl.BlockSpec((None, t, wd), lambda i: (0, i, 0))
    subseq = [pl.BlockSpec((dil, t // dil, wd), lambda i: (0, i, 0)) for dil in DILATIONS]
    doa, dob0, dob1, dob2, dla, dlb, st = pl.pallas_call(
        body, name="combine_bwd", grid=(s // t,),
        in_specs=[pl.BlockSpec((t, D_MODEL), lambda i: (i, 0)), pl.BlockSpec((D_MODEL, D_MODEL), lambda i: (0, 0)),
                  a_spec, half, grow, grow],
        out_specs=[a_spec] + subseq + [lanes, lanes, pl.BlockSpec((8, wd), lambda i: (0, 0))],
        out_shape=[jax.ShapeDtypeStruct((1, s, wd), BF16)]
        + [jax.ShapeDtypeStruct((dil, s // dil, wd), BF16) for dil in DILATIONS]
        + [jax.ShapeDtypeStruct((s, LANES), F32), jax.ShapeDtypeStruct((s, LANES), F32),
           jax.ShapeDtypeStruct((8, wd), F32)],
        scratch_shapes=[pltpu.VMEM((wd // LANES, t, LANES), F32)],
        compiler_params=_cparams(dimension_semantics=("arbitrary",)),
    )(dr1b, w_mix_b, out_a, out_b, g_win, g_dil)
    return doa, [dob0, dob1, dob2], dla, dlb, st


def _assemble_dz(dqa, dka, dva, dqs, dks, dvs, *, t):
    s = dqa.shape[1]
    wd = DIL_SLOTS * HEAD_DIM

    def body(*refs):
        a_refs, g_refs, o_ref, scr = refs[:3], refs[3:12], refs[12], refs[13]
        col = 0
        for r in a_refs:
            o_ref[:, col:col + r.shape[-1]] = r[...]
            col += r.shape[-1]
        for part in range(3):
            for gi, dil in enumerate(DILATIONS):
                val = _from_subsequences(g_refs[3 * part + gi], scr, dil, t)
                o_ref[:, col:col + wd] = val.astype(BF16)
                col += wd

    a_specs = [pl.BlockSpec((None, t, a.shape[-1]), lambda i: (0, i, 0)) for a in (dqa, dka, dva)]
    g_specs = [pl.BlockSpec((dil, t // dil, wd), lambda i: (0, i, 0)) for _ in range(3) for dil in DILATIONS]
    return pl.pallas_call(
        body, name="assemble_dz", grid=(s // t,), in_specs=a_specs + g_specs,
        out_specs=pl.BlockSpec((t, IN_WIDTH), lambda i: (i, 0)),
        out_shape=jax.ShapeDtypeStruct((s, IN_WIDTH), BF16),
        scratch_shapes=[pltpu.VMEM((wd // LANES, t, LANES), F32)],
        compiler_params=_cparams(dimension_semantics=("parallel",)),
    )(dqa, dka, dva, *dqs, *dks, *dvs)


def _mem_fwd(mem, g, b, wk_b, wv_b):
    ml = mem.shape[0]

    def body(mem_ref, g_ref, b_ref, wk_ref, wv_ref, mn_ref, kx_ref, vx_ref):
        mn = _ln(mem_ref[...], g_ref[...], b_ref[...]).astype(BF16)
        mn_ref[...] = mn
        kx_ref[...] = _dot(mn, wk_ref[...]).astype(BF16)
        vx_ref[...] = _dot(mn, wv_ref[...]).astype(BF16)

    sh = jax.ShapeDtypeStruct((ml, D_MODEL), BF16)
    return pl.pallas_call(body, name="mem_fwd", out_shape=[sh, sh, sh], compiler_params=_cparams())(
        mem, g, b, wk_b, wv_b)


def _mem_bwd(dkx, dvx, mem, g, b, wk_b, wv_b):
    def body(dk_ref, dv_ref, mem_ref, g_ref, b_ref, wk_ref, wv_ref, dwk_ref, dwv_ref, st_ref):
        mem_v = mem_ref[...]
        mn = _ln(mem_v, g_ref[...], b_ref[...]).astype(BF16)
        dkb = dk_ref[...].astype(BF16)
        dvb = dv_ref[...].astype(BF16)
        dwk_ref[...] = _dot_tn(mn, dkb)
        dwv_ref[...] = _dot_tn(mn, dvb)
        dmn = _dot_nt(dkb, wk_ref[...]) + _dot_nt(dvb, wv_ref[...])
        _, dg, db = _ln_bwd_math(dmn, mem_v, g_ref[...])
        st_ref[...] = jnp.zeros_like(st_ref)
        st_ref[0:1, :] = dg
        st_ref[1:2, :] = db

    sw = jax.ShapeDtypeStruct((D_MODEL, D_MODEL), F32)
    return pl.pallas_call(body, name="mem_bwd", out_shape=[sw, sw, jax.ShapeDtypeStruct((8, D_MODEL), F32)],
                          compiler_params=_cparams())(dkx, dvx, mem, g, b, wk_b, wv_b)


def _xattn_fwd(h1b, r1, kx, vx, wq_b, wo_b, ln1_g, ln1_b, ln2_g, ln2_b, *, t, comm=None):
    s = h1b.shape[0]
    scale = X_HEAD_DIM ** -0.5

    def body(h_ref, r1_ref, kx_ref, vx_ref, wq_ref, wo_ref, g1, b1, g2, b2, r2_ref, h2_ref, qx_ref, ox_ref, lse_ref):
        qxb = _dot(h_ref[...], wq_ref[...]).astype(BF16)
        qx_ref[...] = qxb
        lane = lax.broadcasted_iota(jnp.int32, (t, LANES), 1)
        lse_acc = jnp.zeros((t, LANES), F32)
        parts = []
        for h in range(X_HEADS):
            hs = slice(h * X_HEAD_DIM, (h + 1) * X_HEAD_DIM)
            sc = _dot_nt(qxb[:, hs] * scale, kx_ref[:, hs])
            m = jnp.max(sc, axis=1, keepdims=True)
            p = jnp.exp(sc - m)
            den = jnp.sum(p, axis=1, keepdims=True)
            parts.append(_dot(p.astype(BF16), vx_ref[:, hs]) / den)
            lse_acc = jnp.where(lane == h, m + jnp.log(den), lse_acc)
        lse_ref[...] = lse_acc
        oxb = jnp.concatenate(parts, axis=1).astype(BF16)
        ox_ref[...] = oxb
        h1 = _ln(r1_ref[...], g1[...], b1[...])
        r2 = ALPHA * h1 + _dot(oxb, wo_ref[...])
        r2_ref[...] = r2
        h2_ref[...] = _ln(r2, g2[...], b2[...]).astype(BF16)

    tile = pl.BlockSpec((t, D_MODEL), lambda i: (i, 0))
    row = pl.BlockSpec((1, D_MODEL), lambda i: (0, 0))
    full = lambda r: pl.BlockSpec((r, D_MODEL), lambda i: (0, 0))
    ml = kx.shape[0]
    bsh = jax.ShapeDtypeStruct((s, D_MODEL), BF16)
    return _pcall(
        body, name="xattn_fwd", grid=(s // t,),
        in_specs=[tile, tile, full(ml), full(ml), full(D_MODEL), full(D_MODEL), row, row, row, row],
        out_specs=[tile, tile, tile, tile, pl.BlockSpec((t, LANES), lambda i: (i, 0))],
        out_shape=[jax.ShapeDtypeStruct((s, D_MODEL), F32), bsh, bsh, bsh, jax.ShapeDtypeStruct((s, LANES), F32)],
        args=[h1b, r1, kx, vx, wq_b, wo_b, ln1_g, ln1_b, ln2_g, ln2_b], dims=("parallel",), comm=comm)


def _xattn_bwd(dr2, qxb, oxb, lse, kx, vx, wq_b, wo_b, r1, ln1_g, *, t, comm=None):
    s = dr2.shape[0]
    ml = kx.shape[0]
    scale = X_HEAD_DIM ** -0.5

    def body(dr2_ref, qx_ref, ox_ref, lse_ref, kx_ref, vx_ref, wq_ref, wo_ref, r1_ref, g1_ref,
             dr1_ref, dr1b_ref, dqx_ref, dkx_ref, dvx_ref, st_ref):
        i = pl.program_id(0)

        @pl.when(i == 0)
        def _():
            dkx_ref[...] = jnp.zeros_like(dkx_ref)
            dvx_ref[...] = jnp.zeros_like(dvx_ref)
            st_ref[...] = jnp.zeros_like(st_ref)

        dr2v = dr2_ref[...]
        dox = _dot_nt(dr2v.astype(BF16), wo_ref[...])
        parts = []
        for h in range(X_HEADS):
            hs = slice(h * X_HEAD_DIM, (h + 1) * X_HEAD_DIM)
            doh = dox[:, hs]
            dohb = doh.astype(BF16)
            dl = jnp.sum(doh * ox_ref[:, hs].astype(F32), axis=1, keepdims=True)
            qh = qx_ref[:, hs] * scale
            p = jnp.exp(_dot_nt(qh, kx_ref[:, hs]) - lse_ref[:, h:h + 1])
            dp = _dot_nt(dohb, vx_ref[:, hs])
            dsb = (p * (dp - dl)).astype(BF16)
            parts.append(_dot(dsb, kx_ref[:, hs]) * scale)
            dkx_ref[:, hs] += _dot_tn(dsb, qh)
            dvx_ref[:, hs] += _dot_tn(p.astype(BF16), dohb)
        dqxb = jnp.concatenate(parts, axis=1).astype(BF16)
        dqx_ref[...] = dqxb
        dh1 = _dot_nt(dqxb, wq_ref[...]) + ALPHA * dr2v
        dr1, dg, db = _ln_bwd_math(dh1, r1_ref[...], g1_ref[...])
        dr1_ref[...] = dr1
        dr1b_ref[...] = dr1.astype(BF16)
        st_ref[0:1, :] += dg
        st_ref[1:2, :] += db

    tile = pl.BlockSpec((t, D_MODEL), lambda i: (i, 0))
    full = lambda r: pl.BlockSpec((r, D_MODEL), lambda i: (0, 0))
    bsh = jax.ShapeDtypeStruct((s, D_MODEL), BF16)
    return _pcall(
        body, name="xattn_bwd", grid=(s // t,),
        in_specs=[tile, tile, tile, pl.BlockSpec((t, LANES), lambda i: (i, 0)), full(ml), full(ml),
                  full(D_MODEL), full(D_MODEL), tile, full(1)],
        out_specs=[tile, tile, tile, full(ml), full(ml), full(8)],
        out_shape=[jax.ShapeDtypeStruct((s, D_MODEL), F32), bsh, bsh,
                   jax.ShapeDtypeStruct((ml, D_MODEL), F32), jax.ShapeDtypeStruct((ml, D_MODEL), F32),
                   jax.ShapeDtypeStruct((8, D_MODEL), F32)],
        args=[dr2, qxb, oxb, lse, kx, vx, wq_b, wo_b, r1, ln1_g], dims=("arbitrary",), comm=comm)


def _halo_specs(t, s, width):
    tb8 = t // 8
    return [pl.BlockSpec((t, width), lambda i: (i, 0)),
            pl.BlockSpec((8, width), lambda i: (jnp.maximum(i * tb8 - 1, 0), 0)),
            pl.BlockSpec((8, width), lambda i: (jnp.minimum((i + 1) * tb8, s // 8 - 1), 0))]


def _halo_rows(i, n, prev_ref, next_ref):
    prev_row = jnp.where(i > 0, prev_ref[7:8, :], 0.0)
    next_row = jnp.where(i < n - 1, next_ref[0:1, :], 0.0)
    return prev_row, next_row


def _gelu_parts(gc):
    cdf = 0.5 * (1.0 + lax.erf(gc * (2.0 ** -0.5)))
    pdf = jnp.exp(-0.5 * gc * gc) * (1.0 / math.sqrt(2.0 * math.pi))
    return gc * cdf, cdf + gc * pdf


def _ffn_out(g, u, conv_w, conv_b, w_down_b, r2, target, ln2_g, ln2_b, ln3_g, ln3_b, *, t):
    s = r2.shape[0]
    n = s // t

    def body(g_ref, gp_ref, gn_ref, u_ref, cw_ref, cb_ref, w_ref, r2_ref, tg_ref, g2, b2, g3, b3,
             t_ref, dr_ref, drb_ref, st_ref):
        i = pl.program_id(0)

        @pl.when(i == 0)
        def _():
            st_ref[...] = jnp.zeros_like(st_ref)

        gv = g_ref[...]
        prev_row, next_row = _halo_rows(i, n, gp_ref, gn_ref)
        gm1, gp1 = _shift_rows(gv, prev_row, next_row)
        gc = gm1 * cw_ref[0:1, :] + gv * cw_ref[1:2, :] + gp1 * cw_ref[2:3, :] + cb_ref[...]
        act, _ = _gelu_parts(gc)
        tb = (act * u_ref[...]).astype(BF16)
        t_ref[...] = tb
        h2 = _ln(r2_ref[...], g2[...], b2[...])
        r3 = ALPHA * h2 + _dot(tb, w_ref[...])
        y = _ln(r3, g3[...], b3[...])
        err = y - tg_ref[...]
        loss = 0.5 * jnp.sum(jnp.mean(err * err, axis=-1, keepdims=True))
        dr, dg, db = _ln_bwd_math(err * (1.0 / D_MODEL), r3, g3[...])
        dr_ref[...] = dr
        drb_ref[...] = dr.astype(BF16)
        st_ref[0:1, :] += dg
        st_ref[1:2, :] += db
        st_ref[2:3, :] += jnp.full((1, D_MODEL), loss, F32)

    wide = pl.BlockSpec((t, D_FF), lambda i: (i, 0))
    tile = pl.BlockSpec((t, D_MODEL), lambda i: (i, 0))
    row = pl.BlockSpec((1, D_MODEL), lambda i: (0, 0))
    return pl.pallas_call(
        body, name="ffn_out", grid=(n,),
        in_specs=_halo_specs(t, s, D_FF) + [wide, pl.BlockSpec((3, D_FF), lambda i: (0, 0)),
                                            pl.BlockSpec((1, D_FF), lambda i: (0, 0)),
                                            pl.BlockSpec((D_FF, D_MODEL), lambda i: (0, 0)),
                                            tile, tile, row, row, row, row],
        out_specs=[wide, tile, tile, pl.BlockSpec((8, D_MODEL), lambda i: (0, 0))],
        out_shape=[jax.ShapeDtypeStruct((s, D_FF), BF16), jax.ShapeDtypeStruct((s, D_MODEL), F32),
                   jax.ShapeDtypeStruct((s, D_MODEL), BF16), jax.ShapeDtypeStruct((8, D_MODEL), F32)],
        compiler_params=_cparams(dimension_semantics=("arbitrary",)),
    )(g, g, g, u, conv_w, conv_b, w_down_b, r2, target, ln2_g, ln2_b, ln3_g, ln3_b)


def _dh2_ln2(dgc, conv_w, du, w_gate_b, w_up_b, dr3, r2, ln2_g, *, t, comm=None):
    s = dgc.shape[0]
    n = s // t

    def body(d_ref, dp_ref, dn_ref, cw_ref, du_ref, wg_ref, wu_ref, dr3_ref, r2_ref, g2, dg_ref, dr_ref, drb_ref,
             st_ref):
        i = pl.program_id(0)

        @pl.when(i == 0)
        def _():
            st_ref[...] = jnp.zeros_like(st_ref)

        dv = d_ref[...]
        prev_row, next_row = _halo_rows(i, n, dp_ref, dn_ref)
        dm1, dp1 = _shift_rows(dv, prev_row, next_row)
        dgb = (dp1 * cw_ref[0:1, :] + dv * cw_ref[1:2, :] + dm1 * cw_ref[2:3, :]).astype(BF16)
        dg_ref[...] = dgb
        dh2 = _dot(dgb, wg_ref[...]) + _dot(du_ref[...], wu_ref[...]) + ALPHA * dr3_ref[...]
        dr, dg, db = _ln_bwd_math(dh2, r2_ref[...], g2[...])
        dr_ref[...] = dr
        drb_ref[...] = dr.astype(BF16)
        st_ref[0:1, :] += dg
        st_ref[1:2, :] += db

    wide = pl.BlockSpec((t, D_FF), lambda i: (i, 0))
    tile = pl.BlockSpec((t, D_MODEL), lambda i: (i, 0))
    wfull = pl.BlockSpec((D_FF, D_MODEL), lambda i: (0, 0), pipeline_mode=pl.Buffered(1))
    return _pcall(
        body, name="dh2_ln2", grid=(n,),
        in_specs=_halo_specs(t, s, D_FF) + [pl.BlockSpec((3, D_FF), lambda i: (0, 0)), wide, wfull, wfull,
                                            tile, tile, pl.BlockSpec((1, D_MODEL), lambda i: (0, 0))],
        out_specs=[wide, tile, tile, pl.BlockSpec((8, D_MODEL), lambda i: (0, 0))],
        out_shape=[jax.ShapeDtypeStruct((s, D_FF), BF16), jax.ShapeDtypeStruct((s, D_MODEL), F32),
                   jax.ShapeDtypeStruct((s, D_MODEL), BF16), jax.ShapeDtypeStruct((8, D_MODEL), F32)],
        args=[dgc, dgc, dgc, conv_w, du, w_gate_b, w_up_b, dr3, r2, ln2_g], dims=("arbitrary",), comm=comm)


def _conv_bwd_a(dr3b, w_down_b, g, u, conv_w, conv_b, *, t):
    s = g.shape[0]
    n = s // t

    def body(d_ref, w_ref, g_ref, gp_ref, gn_ref, u_ref, cw_ref, cb_ref, du_ref, dgc_ref, st_ref):
        i = pl.program_id(0)

        @pl.when(i == 0)
        def _():
            st_ref[...] = jnp.zeros_like(st_ref)

        dt = _dot_nt(d_ref[...], w_ref[...])
        gv = g_ref[...]
        prev_row, next_row = _halo_rows(i, n, gp_ref, gn_ref)
        gm1, gp1 = _shift_rows(gv, prev_row, next_row)
        gc = gm1 * cw_ref[0:1, :] + gv * cw_ref[1:2, :] + gp1 * cw_ref[2:3, :] + cb_ref[...]
        act, dact = _gelu_parts(gc)
        du_ref[...] = (dt * act).astype(BF16)
        dgc = dt * u_ref[...] * dact
        dgc_ref[...] = dgc
        st_ref[0:1, :] += jnp.sum(gm1 * dgc, axis=0, keepdims=True)
        st_ref[1:2, :] += jnp.sum(gv * dgc, axis=0, keepdims=True)
        st_ref[2:3, :] += jnp.sum(gp1 * dgc, axis=0, keepdims=True)
        st_ref[3:4, :] += jnp.sum(dgc, axis=0, keepdims=True)

    tile = pl.BlockSpec((t, D_FF), lambda i: (i, 0))
    return pl.pallas_call(
        body, name="conv_bwd_a", grid=(n,),
        in_specs=[pl.BlockSpec((t, D_MODEL), lambda i: (i, 0)), pl.BlockSpec((D_FF, D_MODEL), lambda i: (0, 0))]
        + _halo_specs(t, s, D_FF) + [tile, pl.BlockSpec((3, D_FF), lambda i: (0, 0)),
                                     pl.BlockSpec((1, D_FF), lambda i: (0, 0))],
        out_specs=[tile, tile, pl.BlockSpec((8, D_FF), lambda i: (0, 0))],
        out_shape=[jax.ShapeDtypeStruct((s, D_FF), BF16), jax.ShapeDtypeStruct((s, D_FF), F32),
                   jax.ShapeDtypeStruct((8, D_FF), F32)],
        compiler_params=_cparams(dimension_semantics=("arbitrary",)),
    )(dr3b, w_down_b, g, g, g, u, conv_w, conv_b)


def _to_residue(a, dil):
    s, w = a.shape
    return a.reshape(s // dil, dil, w).transpose(1, 0, 2)


def _stats_to_lanes(rows):
    dil, hq, l = rows.shape
    return jnp.pad(rows.transpose(2, 0, 1).reshape(dil * l, hq), ((0, 0), (0, LANES - hq)))


def _stats_to_rows(lanes, dil):
    s = lanes.shape[0]
    return lanes[:, :DIL_SLOTS].reshape(s // dil, dil, DIL_SLOTS).transpose(1, 2, 0)


def _rope_angles(positions):
    inv_freq = ROPE_THETA ** (-jnp.arange(0, ROT_DIM, 2, dtype=F32) / ROT_DIM)
    ang = positions.astype(F32)[:, None] * inv_freq
    return jnp.concatenate([jnp.cos(ang), jnp.sin(ang)], axis=1)


class _NoPlan:
    def gather(self, stage):
        return None

    def gathered(self, stage, couts, wb):
        pass

    def exchange(self, stage, grads):
        return None

    def exchanged(self, stage, couts):
        pass


def _local_step(x, mem, positions, target, wb, sp, plan=None, *, t_row=256, t_mm=512, tq_a=128, tq_b=128,
                sub_a=4, sub_b=4):
    s = x.shape[0]
    plan = plan or _NoPlan()
    cs = _rope_angles(positions)
    e_mat = _rope_select_matrix()

    h0b, couts = _ln_in_fwd(x, sp["ln_in_g"], sp["ln_in_b"], t=t_mm, comm=plan.gather("ln_in"))
    plan.gathered("ln_in", couts, wb)
    sp = dict(sp, conv_w=wb.get("conv_w", sp.get("conv_w")))
    (za, *zb), couts = _proj_all(h0b, wb["w_in"], cs, e_mat, t=min(2 * t_mm, s), comm=plan.gather("proj"))
    plan.gathered("proj", couts, wb)
    sub_a = max(1, min(sub_a, s // tq_a))
    subs_b = [max(1, min(sub_b, s // dil // tq_b)) for dil in DILATIONS]
    subs_b_bwd = [s // dil // tq_b if s // dil <= ONE_TILE_ROWS else sb for dil, sb in zip(DILATIONS, subs_b)]
    out_a, lse_a, couts = _swa_fwd_p(za, qcol=0, kcol=4, vcol=5, hq=WIN_Q_HEADS, hkv=WIN_KV_HEADS, w=WIN_HALF,
                                     tq=tq_a, sub=sub_a, sink=sp["attn_sink"], name="attn_a_fwd",
                                     comm=plan.gather("attn_a"))
    plan.gathered("attn_a", couts, wb)
    o_g, lse_g = [], []
    for gi in range(3):
        o, l, couts = _swa_fwd_p(zb[gi], qcol=0, kcol=1, vcol=2, hq=DIL_SLOTS, hkv=DIL_SLOTS, w=DIL_HALF, tq=tq_b,
                                 sub=subs_b[gi], sink=None, name=f"attn_b{gi}_fwd",
                                 comm=plan.gather(f"attn_b{gi}"))
        plan.gathered(f"attn_b{gi}", couts, wb)
        o_g.append(o)
        lse_g.append(_stats_to_lanes(l))
    (mixed_b, out_b, lse_b, r1, h1b), couts = _combine_fwd(
        out_a, o_g, lse_g, sp["g_win"], sp["g_dil"], wb["w_mix_out"], x, sp["ln_in_g"], sp["ln_in_b"],
        sp["ln1_g"], sp["ln1_b"], t=t_row, comm=plan.gather("combine"))
    plan.gathered("combine", couts, wb)
    mem_nb, kx, vx = _mem_fwd(mem, sp["mem_ln_g"], sp["mem_ln_b"], wb["w_xk"], wb["w_xv"])
    (r2, h2b, qxb, oxb, lse_x), couts = _xattn_fwd(
        h1b, r1, kx, vx, wb["w_xq"], wb["w_xo"], sp["ln1_g"], sp["ln1_b"], sp["ln2_g"], sp["ln2_b"], t=t_mm,
        comm=plan.gather("xattn"))
    plan.gathered("xattn", couts, wb)
    g = _mm(h2b, wb["w_gate"], mode="nt", out_dtype=F32, tm=t_mm, tn=D_FF, name="ff_gate")
    u = _mm(h2b, wb["w_up"], mode="nt", out_dtype=F32, tm=t_mm, tn=D_FF, name="ff_up")
    tb, dr3, dr3b, st3 = _ffn_out(g, u, sp["conv_w"], sp["conv_b"], wb["w_down"], r2, target, sp["ln2_g"],
                                  sp["ln2_b"], sp["ln3_g"], sp["ln3_b"], t=t_row)

    grads = {}
    du, dgc, st_conv = _conv_bwd_a(dr3b, wb["w_down"], g, u, sp["conv_w"], sp["conv_b"], t=t_row)
    tk = min(2048, s)
    grads["w_down"] = _mm(tb, dr3b, mode="tn", out_dtype=BF16, tm=D_FF // 2, tn=D_MODEL, tk=tk, name="dw_down")
    grads["w_up"] = _mm(du, h2b, mode="tn", out_dtype=BF16, tm=D_FF // 2, tn=D_MODEL, tk=tk, name="dw_up")
    (dg, dr2, dr2b, st2), couts = _dh2_ln2(dgc, sp["conv_w"], du, wb["w_gate"], wb["w_up"], dr3, r2, sp["ln2_g"],
                                           t=t_mm, comm=plan.exchange("dh2", grads))
    plan.exchanged("dh2", couts)
    grads["w_gate"] = _mm(dg, h2b, mode="tn", out_dtype=BF16, tm=D_FF // 2, tn=D_MODEL, tk=tk, name="dw_gate")

    (dr1, dr1b, dqxb, dkx, dvx, st1), couts = _xattn_bwd(
        dr2, qxb, oxb, lse_x, kx, vx, wb["w_xq"], wb["w_xo"], r1, sp["ln1_g"], t=t_mm,
        comm=plan.exchange("xattn", grads))
    plan.exchanged("xattn", couts)
    grads["w_xo"] = _mm(oxb, dr2b, mode="tn", out_dtype=BF16, tm=D_MODEL, tn=D_MODEL, tk=tk, name="dw_xo")
    grads["w_xq"] = _mm(h1b, dqxb, mode="tn", out_dtype=BF16, tm=D_MODEL, tn=D_MODEL, tk=tk, name="dw_xq")
    grads["w_xk"], grads["w_xv"], st_mem = _mem_bwd(dkx, dvx, mem, sp["mem_ln_g"], sp["mem_ln_b"],
                                                    wb["w_xk"], wb["w_xv"])

    grads["w_mix_out"] = _mm(mixed_b, dr1b, mode="tn", out_dtype=BF16, tm=D_MODEL, tn=D_MODEL, tk=tk,
                             name="dw_mix")
    do_a, do_b, dl_a, dl_b, st_mix = _combine_bwd(dr1b, wb["w_mix_out"], out_a, out_b, sp["g_win"], sp["g_dil"],
                                                  t=t_row)
    (dqa, dka, dva, dsink), couts = _swa_bwd_p(
        za, do_a, lse_a, _stats_to_rows(dl_a, 1), cs[None], e_mat, qcol=0, kcol=4, vcol=5, hq=WIN_Q_HEADS,
        hkv=WIN_KV_HEADS, w=WIN_HALF, tq=2 * tq_a, sub=max(1, sub_a // 2), sink=sp["attn_sink"], name="attn_a_bwd",
        comm=plan.exchange("attn_a", grads))
    plan.exchanged("attn_a", couts)
    dqs, dks, dvs = [], [], []
    for gi, dil in enumerate(DILATIONS):
        (dq, dk, dv), couts = _swa_bwd_p(
            zb[gi], do_b[gi], _stats_to_rows(lse_b, dil), _stats_to_rows(dl_b, dil),
            _to_residue(cs, dil), e_mat, qcol=0, kcol=1, vcol=2, hq=DIL_SLOTS, hkv=DIL_SLOTS, w=DIL_HALF, tq=tq_b,
            sub=subs_b_bwd[gi], sink=None, name=f"attn_b{gi}_bwd", comm=plan.exchange(f"attn_b{gi}", grads))
        plan.exchanged(f"attn_b{gi}", couts)
        dqs.append(dq)
        dks.append(dk)
        dvs.append(dv)
    dz = _assemble_dz(dqa, dka, dva, dqs, dks, dvs, t=t_mm)
    grads["w_in"] = _mm(dz, h0b, mode="tn", out_dtype=BF16, tm=IN_WIDTH // 7, tn=D_MODEL, tk=tk, name="dw_in")
    (grad_x, st0), couts = _dh0_ln_in(dz, wb["w_in"], dr1, x, sp["ln_in_g"], t=t_mm,
                                      comm=plan.exchange("dh0", grads))
    plan.exchanged("dh0", couts)

    small = {
        "loss": st3[2:3, 0:1],
        "ln_in_g": st0[0:1], "ln_in_b": st0[1:2],
        "attn_sink": dsink[:, 0].reshape(1, WIN_Q_HEADS),
        "g_win": st_mix[0:1], "g_dil": st_mix[1:2],
        "ln1_g": st1[0:1], "ln1_b": st1[1:2],
        "mem_ln_g": st_mem[0:1], "mem_ln_b": st_mem[1:2],
        "ln2_g": st2[0:1], "ln2_b": st2[1:2],
        "conv_w": st_conv[0:3], "conv_b": st_conv[3:4],
        "ln3_g": st3[0:1], "ln3_b": st3[1:2],
    }
    return grad_x, grads, small


class _SiblingSwap:
    def __init__(self, arrays):
        self.inputs = list(arrays)
        n = len(arrays)
        self.out_shape = [jax.ShapeDtypeStruct(a.shape, a.dtype) for a in arrays]
        self.scratch = [pltpu.SemaphoreType.DMA((n,)), pltpu.SemaphoreType.DMA((n,))]

    def _copies(self, src, dst, sems):
        send_sems, recv_sems = sems
        x, y, c, _ = _place()
        return [pltpu.make_async_remote_copy(
            src_ref=src[a], dst_ref=dst[a], send_sem=send_sems.at[a], recv_sem=recv_sems.at[a],
            device_id=(x, y, 1 - c), device_id_type=MESH_IDS) for a in range(len(src))]

    def start(self, src, dst, sems):
        for cp in self._copies(src, dst, sems):
            cp.start()

    def wait(self, src, dst, sems):
        copies = self._copies(src, dst, sems)
        for cp in copies:
            cp.wait_recv()
        for cp in copies:
            cp.wait_send()


class _Both:
    def __init__(self, first, second):
        self.parts = (first, second)
        self.inputs = first.inputs + second.inputs
        self.out_shape = first.out_shape + second.out_shape
        self.scratch = first.scratch + second.scratch

    def _split(self, src, dst, sems):
        a = self.parts[0]
        ni, no, ns = len(a.inputs), len(a.out_shape), len(a.scratch)
        return ((src[:ni], dst[:no], sems[:ns]), (src[ni:], dst[no:], sems[ns:]))

    def start(self, src, dst, sems):
        for part, args in zip(self.parts, self._split(src, dst, sems)):
            part.start(*args)

    def wait(self, src, dst, sems):
        for part, args in zip(self.parts, self._split(src, dst, sems)):
            part.wait(*args)


def _row_tile(rows, cols, itemsize=4, budget=1 << 20):
    best = None
    for t in range(16, rows + 1, 16):
        if rows % t == 0 and t * cols * itemsize <= budget:
            best = t
    return best or rows


def _sum_slots(stack, *, name):
    n, r, c = stack.shape
    t = _row_tile(r, c)

    def body(s_ref, o_ref):
        acc = s_ref[0].astype(F32)
        for q in range(1, n):
            acc = acc + s_ref[q].astype(F32)
        o_ref[...] = acc

    return pl.pallas_call(
        body, name=name, grid=(r // t,), in_specs=[pl.BlockSpec((n, t, c), lambda i: (0, i, 0))],
        out_specs=pl.BlockSpec((t, c), lambda i: (i, 0)), out_shape=jax.ShapeDtypeStruct((r, c), F32),
        compiler_params=_cparams(dimension_semantics=("parallel",)),
    )(stack)


def _adamw(w, m, v, p, q, *, name):
    r, c = w.shape
    t = _row_tile(r, c, budget=1 << 20)

    def total(ref):
        if len(ref.shape) == 2:
            return ref[...]
        acc = ref[0].astype(F32)
        for slot in range(1, ref.shape[0]):
            acc = acc + ref[slot].astype(F32)
        return acc

    def body(*refs):
        if q is None:
            w_ref, m_ref, v_ref, p_ref, g_ref, d_ref, nm_ref, nv_ref = refs
            g = total(p_ref)
        else:
            w_ref, m_ref, v_ref, p_ref, q_ref, g_ref, d_ref, nm_ref, nv_ref = refs
            g = total(p_ref) + total(q_ref)
        nm = ADAM_B1 * m_ref[...] + (1.0 - ADAM_B1) * g
        nv = ADAM_B2 * v_ref[...] + (1.0 - ADAM_B2) * (g * g)
        m_hat = nm / (1.0 - ADAM_B1 ** ADAM_STEP)
        v_hat = nv / (1.0 - ADAM_B2 ** ADAM_STEP)
        g_ref[...] = g
        d_ref[...] = -ADAM_LR * (m_hat / (jnp.sqrt(v_hat) + ADAM_EPS) + ADAM_WD * w_ref[...])
        nm_ref[...] = nm
        nv_ref[...] = nv

    tile = pl.BlockSpec((t, c), lambda i: (i, 0))
    args = [w, m, v, p] + ([] if q is None else [q])
    in_specs = [tile if a.ndim == 2 else pl.BlockSpec((a.shape[0], t, c), lambda i: (0, i, 0)) for a in args]
    sh = jax.ShapeDtypeStruct((r, c), F32)
    return pl.pallas_call(
        body, name=name, grid=(r // t,), in_specs=in_specs, out_specs=[tile] * 4, out_shape=[sh] * 4,
        compiler_params=_cparams(dimension_semantics=("parallel",)),
    )(*args)


BIG = ("w_in", "w_mix_out", "w_xq", "w_xk", "w_xv", "w_xo", "w_gate", "w_up", "w_down")
COL_SHARDED = ("w_in", "w_gate", "w_up")
WEIGHTS = ("ln_in_g", "ln_in_b", "w_in", "attn_sink", "g_win", "g_dil", "w_mix_out", "ln1_g", "ln1_b",
           "mem_ln_g", "mem_ln_b", "w_xq", "w_xk", "w_xv", "w_xo", "ln2_g", "ln2_b", "w_gate", "w_up",
           "conv_w", "conv_b", "w_down", "ln3_g", "ln3_b")
SMALL = tuple(k for k in WEIGHTS if k not in BIG)
PACK_COLS = 1024
CONV_SHARD = D_FF // N_CHIPS
CONV_WIDTH_ROWS = 3
SMALL_ROWS = 32


GATHER_STAGES = {"ln_in": ("w_in", "conv_w"), "proj": ("w_mix_out", "w_xq", "w_xk", "w_xv", "w_xo", "w_up"),
                 "combine": ("w_gate", "w_down")}
EXCHANGE_STAGES = {"dh2": ("w_down", "w_up"), "attn_a": ("w_gate", "w_xo", "w_xq"),
                   "attn_b0": ("w_xk", "w_xv", "w_mix_out"), "dh0": ("w_in",)}


def _full_weight(k, g4):
    return g4.reshape(N_CHIPS * g4.shape[1], g4.shape[2])


def _grad_parts(k, gk):
    gk = gk.astype(BF16)
    return gk.reshape(N_CHIPS, gk.shape[0] // N_CHIPS, gk.shape[1])


EARLY_SWAP_STAGE = "attn_b2"


class _Plan:
    def __init__(self, shards):
        self.shards = shards
        self.recv = {}
        self.chip_sums = {}
        self.sibling_sums = {}

    def gather(self, stage):
        names = GATHER_STAGES.get(stage)
        return _ChipGather([self.shards[k] for k in names]) if names else None

    def gathered(self, stage, couts, wb):
        for k, g4 in zip(GATHER_STAGES.get(stage, ()), couts):
            if k == "conv_w":
                taps = g4[:, :CONV_WIDTH_ROWS, :CONV_SHARD]
                wb[k] = taps.transpose(1, 0, 2).reshape(CONV_WIDTH_ROWS, D_FF)
            else:
                wb[k] = _full_weight(k, g4)

    def exchange(self, stage, grads):
        if stage == EARLY_SWAP_STAGE:
            self.early = [k for k in BIG if k in self.recv]
            for k in self.early:
                self.chip_sums[k] = self.recv[k]
            return _SiblingSwap([self.chip_sums[k] for k in self.early])
        names = EXCHANGE_STAGES.get(stage)
        return _ChipExchange([_grad_parts(k, grads[k]) for k in names]) if names else None

    def exchanged(self, stage, couts):
        if stage == EARLY_SWAP_STAGE:
            self.sibling_sums.update(zip(self.early, couts))
            return
        for k, r4 in zip(EXCHANGE_STAGES.get(stage, ()), couts):
            self.recv[k] = r4


def _pack_rows(a):
    r, n = a.shape
    per = -(-n // PACK_COLS)
    return jnp.pad(a, ((0, 0), (0, per * PACK_COLS - n))).reshape(r * per, PACK_COLS)


def _unpack_rows(p, r, n):
    per = -(-n // PACK_COLS)
    return p.reshape(r, per * PACK_COLS)[:, :n]


def _pack(pieces, rows_total):
    cat = jnp.concatenate([_pack_rows(a) for a in pieces], axis=0)
    return jnp.pad(cat, ((0, rows_total - cat.shape[0]), (0, 0)))


def _unpack(p, shapes):
    out, at = [], 0
    for r, n in shapes:
        per = -(-n // PACK_COLS)
        out.append(_unpack_rows(p[at:at + r * per], r, n))
        at += r * per
    return out


def kernel(x, mem, positions, ln_in_g, ln_in_b, w_in, attn_sink, g_win, g_dil, w_mix_out, ln1_g, ln1_b, mem_ln_g, mem_ln_b, w_xq, w_xk, w_xv, w_xo, ln2_g, ln2_b, w_gate, w_up, conv_w, conv_b, w_down, ln3_g, ln3_b, loss_target, m_ln_in_g, m_ln_in_b, m_w_in, m_attn_sink, m_g_win, m_g_dil, m_w_mix_out, m_ln1_g, m_ln1_b, m_mem_ln_g, m_mem_ln_b, m_w_xq, m_w_xk, m_w_xv, m_w_xo, m_ln2_g, m_ln2_b, m_w_gate, m_w_up, m_conv_w, m_conv_b, m_w_down, m_ln3_g, m_ln3_b, v_ln_in_g, v_ln_in_b, v_w_in, v_attn_sink, v_g_win, v_g_dil, v_w_mix_out, v_ln1_g, v_ln1_b, v_mem_ln_g, v_mem_ln_b, v_w_xq, v_w_xk, v_w_xv, v_w_xo, v_ln2_g, v_ln2_b, v_w_gate, v_w_up, v_conv_w, v_conv_b, v_w_down, v_ln3_g, v_ln3_b):
    given = dict(locals())
    shape_of = {k: given[k].shape for k in WEIGHTS}
    as2d = lambda k, a: a.reshape(-1, a.shape[-1]).T if k in COL_SHARDED else a.reshape(-1, a.shape[-1])
    w2 = {k: as2d(k, given[k]) for k in WEIGHTS}
    m2 = {k: as2d(k, given["m_" + k]) for k in WEIGHTS}
    v2 = {k: as2d(k, given["v_" + k]) for k in WEIGHTS}
    chip = 2 * lax.axis_index("x") + lax.axis_index("y")

    shards = {k: w2[k].astype(BF16) for k in BIG}
    shards["conv_w"] = jnp.pad(w2["conv_w"], ((0, 16 - CONV_WIDTH_ROWS), (0, PACK_COLS - CONV_SHARD)))
    plan = _Plan(shards)
    sp = {k: w2[k] for k in SMALL if k != "conv_w"}

    grad_x, grads, small = _local_step(x[0], mem[0], positions[0], loss_target[0], {}, sp, plan)

    small_keys = ("loss",) + SMALL
    small_shapes = [small[k].shape for k in small_keys]
    small_pack = _pack([small[k] for k in small_keys], SMALL_ROWS)
    late = [k for k in BIG if k not in plan.chip_sums]
    for k in late:
        plan.chip_sums[k] = _sum_slots(plan.recv[k], name=f"sum_chips_{k}")
    *late_sibling, small_all = _comm_only(
        _Both(_SiblingSwap([plan.chip_sums[k] for k in late]), _ChipExchange([], small_pack)), "swap_and_small")
    plan.sibling_sums.update(zip(late, late_sibling))
    chip_sums = [plan.chip_sums[k] for k in BIG]
    sibling_sums = [plan.sibling_sums[k] for k in BIG]
    small_sum = _sum_slots(small_all, name="sum_small")
    small_g = dict(zip(small_keys, _unpack(small_sum, small_shapes)))
    loss = small_g["loss"][0, 0]

    res = {}
    for k, p, q in zip(BIG, chip_sums, sibling_sums):
        res[k] = _adamw(w2[k], m2[k], v2[k], p, q, name=f"adamw_{k}")
    small_g["conv_w"] = lax.dynamic_slice_in_dim(small_g["conv_w"], chip * CONV_SHARD, CONV_SHARD, axis=1)
    adam_shapes = [w2[k].shape for k in SMALL]
    packs = [_pack([d[k] for k in SMALL], SMALL_ROWS) for d in (w2, m2, v2, small_g)]
    small_res = [_unpack(o, adam_shapes) for o in _adamw(*packs, None, name="adamw_small")]
    for i, k in enumerate(SMALL):
        res[k] = tuple(o[i] for o in small_res)

    outs = [loss, grad_x[None]]
    for slot in range(4):
        outs += [(res[k][slot].T if k in COL_SHARDED else res[k][slot]).reshape(shape_of[k]) for k in WEIGHTS]
    return tuple(outs)
```

```python
import functools
import math

import jax
import jax.numpy as jnp
from jax import lax
from jax.experimental import pallas as pl
from jax.experimental.pallas import tpu as pltpu

F32 = jnp.float32
BF16 = jnp.bfloat16

D_MODEL = 1024
HEAD_DIM = 64
WIN_Q_HEADS = 8
WIN_KV_HEADS = 2
WIN_HALF = 128
DIL_SLOTS = 8
DILATIONS = (1, 4, 16)
DIL_HALF = 64
ROT_DIM = 16
ROPE_THETA = 500000.0
X_HEADS = 4
X_HEAD_DIM = 256
D_FF = 2816
A_Q = 512
A_KV = 128
A_WIDTH = A_Q + 2 * A_KV
B_QKV = 1536
IN_WIDTH = 5376
ALPHA = 2.0 ** 0.25
LN_EPS = 1e-5
NEG_INF = -1e30
LANES = 128
N_CHIPS = 4
N_DEV = 8

ADAM_LR = 0.001
ADAM_B1 = 0.9
ADAM_B2 = 0.999
ADAM_EPS = 1e-08
ADAM_WD = 0.01
ADAM_STEP = 10

VMEM_LIMIT = 56 * 1024 * 1024


def _cparams(**kw):
    return pltpu.CompilerParams(vmem_limit_bytes=VMEM_LIMIT, **kw)


def _dot(a, b):
    return lax.dot_general(a, b, (((1,), (0,)), ((), ())), preferred_element_type=F32)


def _dot_nt(a, b):
    return lax.dot_general(a, b, (((1,), (1,)), ((), ())), preferred_element_type=F32)


def _dot_tn(a, b):
    return lax.dot_general(a, b, (((0,), (0,)), ((), ())), preferred_element_type=F32)


def _ln(x, g, b):
    mu = jnp.mean(x, axis=-1, keepdims=True)
    xc = x - mu
    var = jnp.mean(xc * xc, axis=-1, keepdims=True)
    return xc * lax.rsqrt(var + LN_EPS) * g + b


def _ln_bwd_math(dy, r, g):
    mu = jnp.mean(r, axis=-1, keepdims=True)
    xc = r - mu
    var = jnp.mean(xc * xc, axis=-1, keepdims=True)
    rstd = lax.rsqrt(var + LN_EPS)
    xhat = xc * rstd
    dxhat = dy * g
    m1 = jnp.mean(dxhat, axis=-1, keepdims=True)
    m2 = jnp.mean(dxhat * xhat, axis=-1, keepdims=True)
    dr = rstd * (dxhat - m1 - xhat * m2)
    return dr, jnp.sum(dy * xhat, axis=0, keepdims=True), jnp.sum(dy, axis=0, keepdims=True)


def _rope(z, ta, tb, tc, sign):
    w = z.shape[1]
    reps = w // LANES
    a = jnp.tile(ta, (1, reps))
    b = jnp.tile(tb, (1, reps))
    c = jnp.tile(tc, (1, reps))
    return z * a + sign * (pltpu.roll(z, w - 8, 1) * b + pltpu.roll(z, 8, 1) * c)


def _shift_rows(x, prev_row, next_row):
    t = x.shape[0]
    sub = 8
    row = lax.broadcasted_iota(jnp.int32, (sub, x.shape[1]), 0)
    down, up = pltpu.roll(x, 1, 0), pltpu.roll(x, t - 1, 0)
    xm1 = jnp.concatenate([jnp.where(row == 0, prev_row, down[:sub]), down[sub:]], axis=0)
    xp1 = jnp.concatenate([up[:t - sub], jnp.where(row == sub - 1, next_row, up[t - sub:])], axis=0)
    return xm1, xp1


def _rope_tabs(cs, e_mat):
    hi = cs.astype(BF16)
    rest = cs - hi.astype(F32)
    mid = rest.astype(BF16)
    lo = (rest - mid.astype(F32)).astype(BF16)
    tabs = _dot(hi, e_mat) + _dot(mid, e_mat) + _dot(lo, e_mat)
    lane = lax.broadcasted_iota(jnp.int32, (cs.shape[0], LANES), 1)
    ones = jnp.where((lane & (HEAD_DIM - 1)) >= ROT_DIM, 1.0, 0.0)
    return tabs[:, :LANES] + ones, tabs[:, LANES:2 * LANES], tabs[:, 2 * LANES:]


def _rope_select_matrix():
    half = ROT_DIM // 2
    e = [[0.0] * (3 * LANES) for _ in range(ROT_DIM)]
    for lane in range(LANES):
        d = lane % HEAD_DIM
        if d < half:
            e[d][lane] = 1.0
            e[half + d][LANES + lane] = -1.0
        elif d < ROT_DIM:
            e[d - half][lane] = 1.0
            e[d][2 * LANES + lane] = 1.0
    return jnp.array(e, BF16)


def _rope_rows(x, cos_t, sin_t, sign):
    half = ROT_DIM // 2
    parts = []
    for base in (0, HEAD_DIM):
        r1, r2 = x[base:base + half], x[base + half:base + ROT_DIM]
        parts += [r1 * cos_t - sign * (r2 * sin_t), r2 * cos_t + sign * (r1 * sin_t), x[base + ROT_DIM:base + HEAD_DIM]]
    return jnp.concatenate(parts, axis=0)


MESH_IDS = pl.DeviceIdType.MESH
ANY = pl.BlockSpec(memory_space=pl.ANY)


def _place():
    x, y, c = lax.axis_index("x"), lax.axis_index("y"), lax.axis_index("c")
    other_chips = [(1 - x, y), (x, 1 - y), (1 - x, 1 - y)]
    return x, y, c, other_chips


class _ChipGather:
    def __init__(self, shards):
        self.inputs = list(shards)
        n = len(shards)
        self.out_shape = [jax.ShapeDtypeStruct((N_CHIPS,) + a.shape, a.dtype) for a in shards]
        self.scratch = [pltpu.SemaphoreType.DMA((6 * n,)), pltpu.SemaphoreType.DMA((6 * n,)),
                        pltpu.SemaphoreType.DMA((n,))]

    def _copies(self, src, dst, sems):
        send_sems, recv_sems, local_sems = sems
        x, y, c, chips = _place()
        mine = 2 * x + y
        n = len(src)
        local, sends, recvs, passes, pass_recvs = [], [], [], [], []
        for a in range(n):
            half = src[a].shape[0] // 2
            my_rows, other_rows = pl.ds(c * half, half), pl.ds((1 - c) * half, half)
            local.append(pltpu.make_async_copy(src[a], dst[a].at[mine], local_sems.at[a]))
            for j, (px, py) in enumerate(chips):
                k, k2, slot = 3 * a + j, 3 * n + 3 * a + j, 2 * px + py
                sends.append(pltpu.make_async_remote_copy(
                    src_ref=src[a].at[my_rows], dst_ref=dst[a].at[mine, my_rows], send_sem=send_sems.at[k],
                    recv_sem=recv_sems.at[k], device_id=(px, py, c), device_id_type=MESH_IDS))
                recvs.append(pltpu.make_async_remote_copy(
                    src_ref=src[a].at[my_rows], dst_ref=dst[a].at[slot, my_rows], send_sem=send_sems.at[k],
                    recv_sem=recv_sems.at[k], device_id=(px, py, c), device_id_type=MESH_IDS))
                passes.append(pltpu.make_async_remote_copy(
                    src_ref=dst[a].at[slot, my_rows], dst_ref=dst[a].at[slot, my_rows], send_sem=send_sems.at[k2],
                    recv_sem=recv_sems.at[k2], device_id=(x, y, 1 - c), device_id_type=MESH_IDS))
                pass_recvs.append(pltpu.make_async_remote_copy(
                    src_ref=dst[a].at[slot, my_rows], dst_ref=dst[a].at[slot, other_rows],
                    send_sem=send_sems.at[k2], recv_sem=recv_sems.at[k2], device_id=(x, y, 1 - c),
                    device_id_type=MESH_IDS))
        return local, sends, recvs, passes, pass_recvs

    def start(self, src, dst, sems):
        local, sends, _, _, _ = self._copies(src, dst, sems)
        for cp in local + sends:
            cp.start()

    def wait(self, src, dst, sems):
        local, sends, recvs, passes, pass_recvs = self._copies(src, dst, sems)
        for idx, landed in enumerate(recvs):
            landed.wait_recv()
            if passes:
                passes[idx].start()
        for cp in pass_recvs:
            cp.wait_recv()
        for cp in sends + passes:
            cp.wait_send()
        for cp in local:
            cp.wait()


class _ChipExchange:
    def __init__(self, parts, small=None):
        self.inputs = list(parts) + ([small] if small is not None else [])
        self.n = len(parts)
        self.has_small = small is not None
        self.out_shape = [jax.ShapeDtypeStruct(a.shape, a.dtype) for a in parts]
        n_sem, n_loc = 3 * self.n, self.n
        if self.has_small:
            self.out_shape.append(jax.ShapeDtypeStruct((N_DEV,) + small.shape, small.dtype))
            n_sem, n_loc = n_sem + N_DEV - 1, n_loc + 1
        self.scratch = [pltpu.SemaphoreType.DMA((n_sem,)), pltpu.SemaphoreType.DMA((n_sem,)),
                        pltpu.SemaphoreType.DMA((n_loc,))]

    def _copies(self, src, dst, sems):
        send_sems, recv_sems, local_sems = sems
        x, y, c, chips = _place()
        mine = 2 * x + y
        n = self.n
        local, sends, recvs = [], [], []
        for a in range(n):
            local.append(pltpu.make_async_copy(src[a].at[mine], dst[a].at[mine], local_sems.at[a]))
            for j, (px, py) in enumerate(chips):
                k = 3 * a + j
                sends.append(pltpu.make_async_remote_copy(
                    src_ref=src[a].at[2 * px + py], dst_ref=dst[a].at[mine], send_sem=send_sems.at[k],
                    recv_sem=recv_sems.at[k], device_id=(px, py, c), device_id_type=MESH_IDS))
                recvs.append(pltpu.make_async_remote_copy(
                    src_ref=src[a].at[mine], dst_ref=dst[a].at[2 * px + py], send_sem=send_sems.at[k],
                    recv_sem=recv_sems.at[k], device_id=(px, py, c), device_id_type=MESH_IDS))
        if self.has_small:
            me_dev = 4 * x + 2 * y + c
            local.append(pltpu.make_async_copy(src[n], dst[n].at[me_dev], local_sems.at[n]))
            for mask in range(1, N_DEV):
                px, py, pc = x ^ ((mask >> 2) & 1), y ^ ((mask >> 1) & 1), c ^ (mask & 1)
                k = 3 * n + mask - 1
                sends.append(pltpu.make_async_remote_copy(
                    src_ref=src[n], dst_ref=dst[n].at[me_dev], send_sem=send_sems.at[k], recv_sem=recv_sems.at[k],
                    device_id=(px, py, pc), device_id_type=MESH_IDS))
                recvs.append(pltpu.make_async_remote_copy(
                    src_ref=src[n], dst_ref=dst[n].at[4 * px + 2 * py + pc], send_sem=send_sems.at[k],
                    recv_sem=recv_sems.at[k], device_id=(px, py, pc), device_id_type=MESH_IDS))
        return local, sends, recvs, [], []

    start = _ChipGather.start
    wait = _ChipGather.wait


def _pcall(body, *, name, grid, in_specs, out_specs, out_shape, args, scratch_shapes=(), dims=None, comm=None):
    in_specs, out_specs, out_shape = list(in_specs), list(out_specs), list(out_shape)
    scratch_shapes = list(scratch_shapes)
    if comm is None:
        outs = pl.pallas_call(
            body, name=name, grid=grid, in_specs=in_specs, out_specs=out_specs, out_shape=out_shape,
            scratch_shapes=scratch_shapes, compiler_params=_cparams(dimension_semantics=dims),
        )(*args)
        return list(outs), []
    n_in, n_out, n_scr = len(in_specs), len(out_specs), len(scratch_shapes)
    n_cin, n_cout = len(comm.inputs), len(comm.out_shape)

    def wrapped(*refs):
        ins, refs = refs[:n_in], refs[n_in:]
        cins, refs = refs[:n_cin], refs[n_cin:]
        outs, refs = refs[:n_out], refs[n_out:]
        couts, refs = refs[:n_cout], refs[n_cout:]
        scr, csems = refs[:n_scr], refs[n_scr:]
        first = last = None
        for axis, size in enumerate(grid):
            pid = pl.program_id(axis)
            f, l = pid == 0, pid == size - 1
            first = f if first is None else first & f
            last = l if last is None else last & l

        @pl.when(first)
        def _():
            comm.start(cins, couts, csems)

        body(*ins, *outs, *scr)

        @pl.when(last)
        def _():
            comm.wait(cins, couts, csems)

    res = pl.pallas_call(
        wrapped, name=name, grid=grid, in_specs=in_specs + [ANY] * n_cin, out_specs=out_specs + [ANY] * n_cout,
        out_shape=out_shape + list(comm.out_shape), scratch_shapes=scratch_shapes + list(comm.scratch),
        compiler_params=_cparams(dimension_semantics=("arbitrary",) * len(grid)),
    )(*args, *comm.inputs)
    return list(res[:n_out]), list(res[n_out:])


def _comm_only(comm, name):
    def body(*refs):
        n_cin, n_cout = len(comm.inputs), len(comm.out_shape)
        cins, couts, csems = refs[:n_cin], refs[n_cin:n_cin + n_cout], refs[n_cin + n_cout:]
        comm.start(cins, couts, csems)
        comm.wait(cins, couts, csems)

    return list(pl.pallas_call(
        body, name=name, in_specs=[ANY] * len(comm.inputs), out_specs=[ANY] * len(comm.out_shape),
        out_shape=list(comm.out_shape), scratch_shapes=list(comm.scratch),
    )(*comm.inputs))


def _mm(a, b, *, mode, out_dtype, tm, tn, tk=None, name):
    if mode == "nt":
        m, k = a.shape
        n = b.shape[0]
        assert m % tm == 0 and n % tn == 0

        def body(a_ref, b_ref, o_ref):
            o_ref[...] = _dot_nt(a_ref[...], b_ref[...]).astype(out_dtype)

        return pl.pallas_call(
            body, name=name, grid=(m // tm, n // tn),
            in_specs=[pl.BlockSpec((tm, k), lambda i, j: (i, 0)), pl.BlockSpec((tn, k), lambda i, j: (j, 0))],
            out_specs=pl.BlockSpec((tm, tn), lambda i, j: (i, j)),
            out_shape=jax.ShapeDtypeStruct((m, n), out_dtype),
            compiler_params=_cparams(dimension_semantics=("parallel", "parallel")),
        )(a, b)
    assert mode == "tn"
    kk, m = a.shape
    n = b.shape[1]
    assert m % tm == 0 and n % tn == 0 and kk % tk == 0
    nk = kk // tk

    def body(a_ref, b_ref, o_ref, acc_ref):
        kstep = pl.program_id(2)

        @pl.when(kstep == 0)
        def _():
            acc_ref[...] = jnp.zeros_like(acc_ref)

        acc_ref[...] += _dot_tn(a_ref[...], b_ref[...])

        @pl.when(kstep == nk - 1)
        def _():
            o_ref[...] = acc_ref[...].astype(out_dtype)

    return pl.pallas_call(
        body, name=name, grid=(m // tm, n // tn, nk),
        in_specs=[pl.BlockSpec((tk, tm), lambda i, j, s: (s, i)), pl.BlockSpec((tk, tn), lambda i, j, s: (s, j))],
        out_specs=pl.BlockSpec((tm, tn), lambda i, j, s: (i, j)),
        out_shape=jax.ShapeDtypeStruct((m, n), out_dtype),
        scratch_shapes=[pltpu.VMEM((tm, tn), F32)],
        compiler_params=_cparams(dimension_semantics=("parallel", "parallel", "arbitrary")),
    )(a, b)


PROJ_COLS = 256


def _proj_segments():
    wd = DIL_SLOTS * HEAD_DIM
    segs = [(1, [(0, 1), (PROJ_COLS, 1), (2 * PROJ_COLS, 2)])]
    for gi, dil in enumerate(DILATIONS):
        blocks = []
        for part, kind in enumerate((1, 1, 0)):
            col = A_WIDTH + part * B_QKV + gi * wd
            blocks += [(col, kind), (col + PROJ_COLS, kind)]
        segs.append((dil, blocks))
    return segs


PROJ_SEGMENTS = _proj_segments()


def _dh0_ln_in(dz, w_t, dr1, x, ln_in_g, *, t, comm=None):
    s, k = dz.shape

    def body(dz_ref, w_ref, dr1_ref, x_ref, g_ref, gx_ref, st_ref):
        i = pl.program_id(0)

        @pl.when(i == 0)
        def _():
            st_ref[...] = jnp.zeros_like(st_ref)

        dh0 = _dot(dz_ref[...], w_ref[...]) + ALPHA * dr1_ref[...]
        dx, dg, db = _ln_bwd_math(dh0, x_ref[...], g_ref[...])
        gx_ref[...] = dx
        st_ref[0:1, :] += dg
        st_ref[1:2, :] += db

    tile = pl.BlockSpec((t, D_MODEL), lambda i: (i, 0))
    return _pcall(
        body, name="dh0_ln_in", grid=(s // t,),
        in_specs=[pl.BlockSpec((t, k), lambda i: (i, 0)),
                  pl.BlockSpec((k, D_MODEL), lambda i: (0, 0), pipeline_mode=pl.Buffered(1)),
                  tile, tile, pl.BlockSpec((1, D_MODEL), lambda i: (0, 0))],
        out_specs=[tile, pl.BlockSpec((8, D_MODEL), lambda i: (0, 0))],
        out_shape=[jax.ShapeDtypeStruct((s, D_MODEL), F32), jax.ShapeDtypeStruct((8, D_MODEL), F32)],
        args=[dz, w_t, dr1, x, ln_in_g], dims=("arbitrary",), comm=comm)


def _ln_in_fwd(x, g, b, *, t, comm=None):
    s = x.shape[0]

    def body(x_ref, g_ref, b_ref, o_ref):
        o_ref[...] = _ln(x_ref[...], g_ref[...], b_ref[...]).astype(BF16)

    row = pl.BlockSpec((1, D_MODEL), lambda i: (0, 0))
    tile = pl.BlockSpec((t, D_MODEL), lambda i: (i, 0))
    outs, couts = _pcall(body, name="ln_in_fwd", grid=(s // t,), in_specs=[tile, row, row], out_specs=[tile],
                         out_shape=[jax.ShapeDtypeStruct((s, D_MODEL), BF16)], args=[x, g, b], dims=("parallel",),
                         comm=comm)
    return outs[0], couts


def _proj_all(h0b, w_t, cs, e_mat, *, t, comm=None):
    s = h0b.shape[0]
    cb = PROJ_COLS
    halves = cb // LANES

    def body(h_ref, w_ref, cs_ref, e_ref, *rest):
        z_refs, scr = rest[:-1], rest[-1]
        h = h_ref[...]
        ta, tb, tc = (jnp.tile(tab, (1, halves)) for tab in _rope_tabs(cs_ref[...], e_ref[...]))
        lane = lax.broadcasted_iota(jnp.int32, (t, cb), 1)
        slot = 0
        for z_ref, (dil, blocks) in zip(z_refs, PROJ_SEGMENTS):
            for jb, (col, kind) in enumerate(blocks):
                acc = _dot_nt(h, w_ref[col:col + cb, :])
                if kind:
                    z = acc * ta + (pltpu.roll(acc, cb - 8, 1) * tb + pltpu.roll(acc, 8, 1) * tc)
                    if kind == 2:
                        z = jnp.where(lane < LANES, z, acc)
                else:
                    z = acc
                if dil == 1:
                    z_ref[0, :, cb * jb:cb * (jb + 1)] = z.astype(BF16)
                    continue
                for half in range(halves):
                    scr[slot, half] = z[:, half * LANES:(half + 1) * LANES]
                for c in range(dil):
                    for half in range(halves):
                        rows = scr[slot, half, pl.ds(c, t // dil, stride=dil), :]
                        z_ref[c, :, cb * jb + half * LANES:cb * jb + (half + 1) * LANES] = rows.astype(BF16)
                slot = 1 - slot

    widths = [cb * len(blocks) for _, blocks in PROJ_SEGMENTS]
    dils = [dil for dil, _ in PROJ_SEGMENTS]
    outs, couts = _pcall(
        body, name="proj_all", grid=(s // t,),
        in_specs=[pl.BlockSpec((t, D_MODEL), lambda i: (i, 0)),
                  pl.BlockSpec((IN_WIDTH, D_MODEL), lambda i: (0, 0), pipeline_mode=pl.Buffered(1)),
                  pl.BlockSpec((t, ROT_DIM), lambda i: (i, 0)), pl.BlockSpec((ROT_DIM, 3 * LANES), lambda i: (0, 0))],
        out_specs=[pl.BlockSpec((dil, t // dil, wd), lambda i: (0, i, 0)) for dil, wd in zip(dils, widths)],
        out_shape=[jax.ShapeDtypeStruct((dil, s // dil, wd), BF16) for dil, wd in zip(dils, widths)],
        args=[h0b, w_t, cs, e_mat], scratch_shapes=[pltpu.VMEM((2, halves, t, LANES), F32)],
        dims=("parallel",), comm=comm)
    return outs, couts


PAIR = 2 * HEAD_DIM
ONE_TILE_ROWS = 2048


def _place_head(x2, src_pos, dst_pos):
    hi = lax.broadcasted_iota(jnp.int32, x2.shape, 1) >= HEAD_DIM
    src = x2 if src_pos == dst_pos else pltpu.roll(x2, HEAD_DIM, 1)
    return jnp.where(hi == (dst_pos == 1), src, jnp.zeros_like(src))


def _band_mask_t(row0, tq, w, seq_len):
    tk = tq + 2 * w
    kk = lax.broadcasted_iota(jnp.int32, (tk, tq), 0)
    qq = lax.broadcasted_iota(jnp.int32, (tk, tq), 1)
    kpos = row0 - w + kk
    return (jnp.abs(qq + w - kk) <= w) & (kpos >= 0) & (kpos < seq_len)


def _halo_kv_specs(t, w, hkv, n, seq_len, kcol, vcol):
    kw = hkv * HEAD_DIM
    per, last = t // w, seq_len // w - 1
    cur = lambda s, i: jnp.minimum(i, n - 1)
    specs = []
    for c in (kcol, vcol):
        specs += [pl.BlockSpec((None, w, kw), lambda s, i, c=c: (s, jnp.maximum(cur(s, i) * per - 1, 0), c)),
                  pl.BlockSpec((None, t, kw), lambda s, i, c=c: (s, cur(s, i), c)),
                  pl.BlockSpec((None, w, kw), lambda s, i, c=c: (s, jnp.minimum((cur(s, i) + 1) * per, last), c))]
    return specs, cur


def _pair_kv(kfull, vfull, qp, rep, krows):
    ks, vs, a_of = [], [], []
    for pos in range(2):
        g = (2 * qp + pos) // rep
        a_of.append(g // 2)
        ks.append(_place_head(kfull[g // 2][krows], g % 2, pos))
        vs.append(_place_head(vfull[g // 2][krows], g % 2, pos))
    assert a_of[0] == a_of[1]
    return jnp.concatenate(ks, axis=0), jnp.concatenate(vs, axis=0), a_of[0]


def _swa_fwd_p(qkv, *, qcol, kcol, vcol, hq, hkv, w, tq, sub, sink, name, comm=None):
    nseq, seq_len, _ = qkv.shape
    t = tq * sub
    n = seq_len // t
    rep = hq // hkv
    tk = tq + 2 * w
    kv_specs, cur = _halo_kv_specs(t, w, hkv, n, seq_len, kcol, vcol)

    def body(*refs):
        if sink is not None:
            sink_ref, refs = refs[0], refs[1:]
        q_ref, kp_ref, kc_ref, kn_ref, vp_ref, vc_ref, vn_ref, o_ref, lse_ref = refs
        i = pl.program_id(1)
        kfull, vfull = [], []
        for a in range(hkv // 2):
            ls = slice(a * PAIR, (a + 1) * PAIR)
            kfull.append(jnp.concatenate([kp_ref[:, ls], kc_ref[:, ls], kn_ref[:, ls]], axis=0) * 0.125)
            vfull.append(jnp.concatenate([vp_ref[:, ls], vc_ref[:, ls], vn_ref[:, ls]], axis=0))
        row_hi = lax.broadcasted_iota(jnp.int32, (PAIR, tq), 0) >= HEAD_DIM
        for jj in range(sub):
            rows = slice(jj * tq, (jj + 1) * tq)
            mask_t = _band_mask_t(i * t + jj * tq, tq, w, seq_len)
            o_t, lse_rows = [], []
            for qp in range(hq // 2):
                kst, vst, _ = _pair_kv(kfull, vfull, qp, rep, slice(jj * tq, jj * tq + tk))
                s2 = _dot_nt(kst, q_ref[rows, qp * PAIR:(qp + 1) * PAIR])
                ps, dens = [], []
                for pos in range(2):
                    h = 2 * qp + pos
                    s_t = jnp.where(mask_t, s2[pos * tk:(pos + 1) * tk], NEG_INF)
                    m = jnp.max(s_t, axis=0, keepdims=True)
                    if sink is not None:
                        m = jnp.maximum(m, sink_ref[0, h])
                    p_t = jnp.exp(s_t - m)
                    den = jnp.sum(p_t, axis=0, keepdims=True)
                    if sink is not None:
                        den = den + jnp.exp(sink_ref[0, h] - m)
                    ps.append(p_t.astype(BF16))
                    dens.append(den)
                    lse_rows.append(m + jnp.log(den))
                both = _dot_tn(vst, jnp.concatenate(ps, axis=0))
                o_t.append(both / jnp.where(row_hi, dens[1], dens[0]))
            o_ref[rows, :] = jnp.concatenate(o_t, axis=0).T
            lse_ref[:, rows] = jnp.concatenate(lse_rows, axis=0)

    in_specs = [pl.BlockSpec((None, t, hq * HEAD_DIM), lambda s, i: (s, i, qcol))] + kv_specs
    args = [qkv] * 7
    if sink is not None:
        in_specs = [pl.BlockSpec(memory_space=pltpu.SMEM)] + in_specs
        args = [sink] + args
    (o, lse), couts = _pcall(
        body, name=name, grid=(nseq, n), in_specs=in_specs,
        out_specs=[pl.BlockSpec((None, t, hq * HEAD_DIM), lambda s, i: (s, i, 0)),
                   pl.BlockSpec((None, hq, t), lambda s, i: (s, 0, i))],
        out_shape=[jax.ShapeDtypeStruct((nseq, seq_len, hq * HEAD_DIM), F32),
                   jax.ShapeDtypeStruct((nseq, hq, seq_len), F32)],
        args=args, dims=("parallel", "parallel"), comm=comm)
    return o, lse, couts


def _swa_bwd_p(qkv, do, lse, delta, cs, e_mat, *, qcol, kcol, vcol, hq, hkv, w, tq, sub, sink, name, comm=None):
    nseq, seq_len, _ = qkv.shape
    t = tq * sub
    n = seq_len // t
    rep = hq // hkv
    qw, kw = hq * HEAD_DIM, hkv * HEAD_DIM
    tk = tq + 2 * w
    kv_specs, cur = _halo_kv_specs(t, w, hkv, n, seq_len, kcol, vcol)

    def body(*refs):
        if sink is not None:
            sink_ref, refs = refs[0], refs[1:]
        (q_ref, kp_ref, kc_ref, kn_ref, vp_ref, vc_ref, vn_ref, do_ref, lse_ref, dl_ref,
         cs_c, cs_p, e_ref) = refs[:13]
        outs = refs[13:]
        dq_ref, dk_ref, dv_ref = outs[:3]
        dsink_ref = outs[3] if sink is not None else None
        dk_win, dv_win = outs[-2:]
        dk_acc, dv_acc = outs[-4:-2] if n > 1 else (None, None)
        s_id = pl.program_id(0)
        i = pl.program_id(1)
        slot_p, slot_c, slot_n = (i + 2) % 3, i % 3, (i + 1) % 3

        if sink is not None:
            @pl.when((s_id == 0) & (i == 0))
            def _():
                dsink_ref[...] = jnp.zeros_like(dsink_ref)

        @pl.when(i < n)
        def _():
            dk_win[...] = jnp.zeros_like(dk_win)
            dv_win[...] = jnp.zeros_like(dv_win)
            kfull, vfull = [], []
            for a in range(hkv // 2):
                ls = slice(a * PAIR, (a + 1) * PAIR)
                kfull.append(jnp.concatenate([kp_ref[:, ls], kc_ref[:, ls], kn_ref[:, ls]], axis=0) * 0.125)
                vfull.append(jnp.concatenate([vp_ref[:, ls], vc_ref[:, ls], vn_ref[:, ls]], axis=0))
            for jj in range(sub):
                rows = slice(jj * tq, (jj + 1) * tq)
                krows = slice(jj * tq, jj * tq + tk)
                mask_t = _band_mask_t(i * t + jj * tq, tq, w, seq_len)
                dq_t = []
                dk2 = [None] * (hkv // 2)
                dv2 = [None] * (hkv // 2)
                for qp in range(hq // 2):
                    kst, vst, a = _pair_kv(kfull, vfull, qp, rep, krows)
                    q2 = q_ref[rows, qp * PAIR:(qp + 1) * PAIR]
                    do2 = do_ref[rows, qp * PAIR:(qp + 1) * PAIR]
                    s2 = _dot_nt(kst, q2)
                    dp2 = _dot_nt(vst, do2)
                    ds, ps, q_at, do_at = [], [], [], []
                    for pos in range(2):
                        h = 2 * qp + pos
                        e = (h // rep) % 2
                        half = slice(pos * tk, (pos + 1) * tk)
                        lse_h = lse_ref[h:h + 1, rows]
                        dl_h = dl_ref[h:h + 1, rows]
                        p_t = jnp.exp(jnp.where(mask_t, s2[half], NEG_INF) - lse_h)
                        ds.append((p_t * (dp2[half] - dl_h)).astype(BF16))
                        ps.append(p_t.astype(BF16))
                        q_at.append(_place_head(q2, pos, e) * 0.125)
                        do_at.append(_place_head(do2, pos, e))
                        if sink is not None:
                            ds_sink = -jnp.sum(jnp.exp(sink_ref[0, h] - lse_h) * dl_h)
                            dsink_ref[h:h + 1, :] += jnp.full((1, LANES), ds_sink, F32)
                    dq_t.append(_rope_rows(_dot_tn(kst, jnp.concatenate(ds, axis=0)),
                                           cs_c[0:ROT_DIM // 2, rows], cs_c[ROT_DIM // 2:ROT_DIM, rows], -1.0))
                    dk_part = _dot(jnp.concatenate(ds, axis=1), jnp.concatenate(q_at, axis=0))
                    dv_part = _dot(jnp.concatenate(ps, axis=1), jnp.concatenate(do_at, axis=0))
                    dk2[a] = dk_part if dk2[a] is None else dk2[a] + dk_part
                    dv2[a] = dv_part if dv2[a] is None else dv2[a] + dv_part
                for a in range(hkv // 2):
                    ls = slice(a * PAIR, (a + 1) * PAIR)
                    dk_win[krows, ls] += dk2[a]
                    dv_win[krows, ls] += dv2[a]
                dq_ref[rows, :] = jnp.concatenate(dq_t, axis=0).T.astype(BF16)

            if n == 1:
                dk_ref[...] = _rope(dk_win[w:w + t, :], *_rope_tabs(cs_p[...], e_ref[...]), -1.0).astype(BF16)
                dv_ref[...] = dv_win[w:w + t, :].astype(BF16)
                return

            @pl.when(i > 0)
            def _():
                dk_acc[slot_p, t - w:, :] += dk_win[:w, :]
                dv_acc[slot_p, t - w:, :] += dv_win[:w, :]

            @pl.when(i == 0)
            def _():
                dk_acc[slot_c] = dk_win[w:w + t, :]
                dv_acc[slot_c] = dv_win[w:w + t, :]

            @pl.when(i > 0)
            def _():
                dk_acc[slot_c] += dk_win[w:w + t, :]
                dv_acc[slot_c] += dv_win[w:w + t, :]

            dk_acc[slot_n] = jnp.zeros((t, kw), F32)
            dv_acc[slot_n] = jnp.zeros((t, kw), F32)
            dk_acc[slot_n, :w, :] = dk_win[w + t:, :]
            dv_acc[slot_n, :w, :] = dv_win[w + t:, :]

        if n > 1:
            @pl.when(i >= 1)
            def _():
                dk_ref[...] = _rope(dk_acc[slot_p], *_rope_tabs(cs_p[...], e_ref[...]), -1.0).astype(BF16)
                dv_ref[...] = dv_acc[slot_p].astype(BF16)

    row_c = lambda width: pl.BlockSpec((None, t, width), lambda s, i: (s, cur(s, i), 0))
    row_p = lambda width: pl.BlockSpec((None, t, width), lambda s, i: (s, jnp.maximum(i - 1, 0), 0))
    stat = pl.BlockSpec((None, hq, t), lambda s, i: (s, 0, cur(s, i)))
    cs_rows = pl.BlockSpec((None, ROT_DIM, t), lambda s, i: (s, 0, cur(s, i)))
    in_specs = ([pl.BlockSpec((None, t, qw), lambda s, i: (s, cur(s, i), qcol))] + kv_specs
                + [row_c(qw), stat, stat, cs_rows, row_p(ROT_DIM),
                   pl.BlockSpec((ROT_DIM, 3 * LANES), lambda s, i: (0, 0))])
    args = [qkv] * 7 + [do, lse, delta, cs.transpose(0, 2, 1), cs, e_mat]
    out_specs = [row_c(qw), row_p(kw), row_p(kw)]
    out_shape = [jax.ShapeDtypeStruct((nseq, seq_len, qw), BF16),
                 jax.ShapeDtypeStruct((nseq, seq_len, kw), BF16),
                 jax.ShapeDtypeStruct((nseq, seq_len, kw), BF16)]
    if sink is not None:
        in_specs = [pl.BlockSpec(memory_space=pltpu.SMEM)] + in_specs
        args = [sink] + args
        out_specs.append(pl.BlockSpec((8, LANES), lambda s, i: (0, 0)))
        out_shape.append(jax.ShapeDtypeStruct((8, LANES), F32))
    return _pcall(
        body, name=name, grid=(nseq, n + 1 if n > 1 else 1), in_specs=in_specs, out_specs=out_specs,
        out_shape=out_shape,
        scratch_shapes=([pltpu.VMEM((3, t, kw), F32), pltpu.VMEM((3, t, kw), F32)] if n > 1 else [])
        + [pltpu.VMEM((t + 2 * w, kw), F32), pltpu.VMEM((t + 2 * w, kw), F32)], args=args,
        dims=("arbitrary", "arbitrary"), comm=comm)


def _rms_parts(o, g):
    ms = jnp.mean(o * o, axis=-1, keepdims=True) + LN_EPS
    rinv = lax.rsqrt(ms)
    return o * rinv * g, rinv


def _from_subsequences(ref, scr, dil, t):
    slabs = ref.shape[-1] // LANES
    if dil == 1:
        return ref[0].astype(F32)
    for c in range(dil):
        for sl in range(slabs):
            scr[sl, pl.ds(c, t // dil, stride=dil), :] = ref[c, :, sl * LANES:(sl + 1) * LANES].astype(F32)
    return jnp.concatenate([scr[sl] for sl in range(slabs)], axis=1)


def _to_subsequences(val, ref, scr, dil, t):
    slabs = val.shape[-1] // LANES
    if dil == 1:
        ref[0] = val.astype(ref.dtype)
        return
    for sl in range(slabs):
        scr[sl] = val[:, sl * LANES:(sl + 1) * LANES]
    for c in range(dil):
        for sl in range(slabs):
            ref[c, :, sl * LANES:(sl + 1) * LANES] = scr[sl, pl.ds(c, t // dil, stride=dil), :].astype(ref.dtype)


def _combine_fwd(out_a, o_g, lse_g, g_win, g_dil, w_mix_b, x, ln_in_g, ln_in_b, ln1_g, ln1_b, *, t, comm=None):
    s = out_a.shape[1]
    wd = DIL_SLOTS * HEAD_DIM

    def body(oa_ref, o0, o1, o2, l0, l1, l2, gw_ref, gd_ref, w_ref, x_ref, g0, b0, g1, b1,
             mixed_ref, ob_ref, lt_ref, r1_ref, h1_ref, scr):
        ls = [l0[...], l1[...], l2[...]]
        mx = jnp.maximum(jnp.maximum(ls[0], ls[1]), ls[2])
        ws = [jnp.exp(l - mx) for l in ls]
        tot = ws[0] + ws[1] + ws[2]
        lt_ref[...] = mx + jnp.log(tot)
        ws = [x / tot for x in ws]
        og = [_from_subsequences(o_ref, scr.at[gi], dil, t)
              for gi, (o_ref, dil) in enumerate(zip((o0, o1, o2), DILATIONS))]
        parts = []
        for h in range(DIL_SLOTS):
            hs = slice(h * HEAD_DIM, (h + 1) * HEAD_DIM)
            parts.append(ws[0][:, h:h + 1] * og[0][:, hs] + ws[1][:, h:h + 1] * og[1][:, hs]
                         + ws[2][:, h:h + 1] * og[2][:, hs])
        ob = jnp.concatenate(parts, axis=1)
        ob_ref[...] = ob
        na, _ = _rms_parts(oa_ref[...], gw_ref[...])
        nb, _ = _rms_parts(ob, gd_ref[...])
        mixed = jnp.concatenate([na.astype(BF16), nb.astype(BF16)], axis=1)
        mixed_ref[...] = mixed
        h0 = _ln(x_ref[...], g0[...], b0[...])
        r1 = ALPHA * h0 + _dot(mixed, w_ref[...])
        r1_ref[...] = r1
        h1_ref[...] = _ln(r1, g1[...], b1[...]).astype(BF16)

    half = pl.BlockSpec((t, wd), lambda i: (i, 0))
    full = pl.BlockSpec((t, D_MODEL), lambda i: (i, 0))
    lanes = pl.BlockSpec((t, LANES), lambda i: (i, 0))
    grow = pl.BlockSpec((1, wd), lambda i: (0, 0))
    row = pl.BlockSpec((1, D_MODEL), lambda i: (0, 0))
    subseq = [pl.BlockSpec((dil, t // dil, wd), lambda i: (0, i, 0)) for dil in DILATIONS]
    return _pcall(
        body, name="combine_fwd", grid=(s // t,),
        in_specs=[pl.BlockSpec((None, t, wd), lambda i: (0, i, 0))] + subseq
        + [lanes, lanes, lanes, grow, grow, pl.BlockSpec((D_MODEL, D_MODEL), lambda i: (0, 0)), full,
           row, row, row, row],
        out_specs=[full, half, lanes, full, full],
        out_shape=[jax.ShapeDtypeStruct((s, D_MODEL), BF16), jax.ShapeDtypeStruct((s, wd), F32),
                   jax.ShapeDtypeStruct((s, LANES), F32), jax.ShapeDtypeStruct((s, D_MODEL), F32),
                   jax.ShapeDtypeStruct((s, D_MODEL), BF16)],
        scratch_shapes=[pltpu.VMEM((len(DILATIONS), wd // LANES, t, LANES), F32)],
        args=[out_a, *o_g, *lse_g, g_win, g_dil, w_mix_b, x, ln_in_g, ln_in_b, ln1_g, ln1_b], dims=("parallel",),
        comm=comm)


def _combine_bwd(dr1b, w_mix_b, out_a, out_b, g_win, g_dil, *, t):
    s = out_b.shape[0]
    wd = DIL_SLOTS * HEAD_DIM

    def body(dr_ref, w_ref, oa_ref, ob_ref, gw_ref, gd_ref, doa_ref, dob0, dob1, dob2, dla_ref, dlb_ref, st_ref,
             scr):
        i = pl.program_id(0)
        dm = _dot_nt(dr_ref[...], w_ref[...])

        @pl.when(i == 0)
        def _():
            st_ref[...] = jnp.zeros_like(st_ref)

        lane = lax.broadcasted_iota(jnp.int32, (t, LANES), 1)
        for idx, (o_ref, g_ref, dl_ref) in enumerate(((oa_ref, gw_ref, dla_ref), (ob_ref, gd_ref, dlb_ref))):
            o = o_ref[...]
            dn = dm[:, idx * wd:(idx + 1) * wd]
            _, rinv = _rms_parts(o, g_ref[...])
            wv = dn * g_ref[...]
            do = rinv * wv - o * (rinv * rinv * rinv) * jnp.mean(wv * o, axis=-1, keepdims=True)
            st_ref[idx:idx + 1, :] += jnp.sum(dn * o * rinv, axis=0, keepdims=True)
            if idx == 0:
                doa_ref[...] = do.astype(BF16)
            else:
                for do_ref, dil in zip((dob0, dob1, dob2), DILATIONS):
                    _to_subsequences(do, do_ref, scr, dil, t)
            prod = do * o
            acc = jnp.zeros((t, LANES), F32)
            for h in range(DIL_SLOTS):
                hs = slice(h * HEAD_DIM, (h + 1) * HEAD_DIM)
                acc = jnp.where(lane == h, jnp.sum(prod[:, hs], axis=1, keepdims=True), acc)
            dl_ref[...] = acc

    half = pl.BlockSpec((t, wd), lambda i: (i, 0))
    lanes = pl.BlockSpec((t, LANES), lambda i: (i, 0))
    grow = pl.BlockSpec((1, wd), lambda i: (0, 0))
    a_spec = pl.BlockSpec((None, t, wd), lambda i: (0, i, 0))
    subseq = [pl.BlockSpec((dil, t // dil, wd), lambda i: (0, i, 0)) for dil in DILATIONS]
    doa, dob0, dob1, dob2, dla, dlb, st = pl.pallas_call(
        body, name="combine_bwd", grid=(s // t,),
        in_specs=[pl.BlockSpec((t, D_MODEL), lambda i: (i, 0)), pl.BlockSpec((D_MODEL, D_MODEL), lambda i: (0, 0)),
                  a_spec, half, grow, grow],
        out_specs=[a_spec] + subseq + [lanes, lanes, pl.BlockSpec((8, wd), lambda i: (0, 0))],
        out_shape=[jax.ShapeDtypeStruct((1, s, wd), BF16)]
        + [jax.ShapeDtypeStruct((dil, s // dil, wd), BF16) for dil in DILATIONS]
        + [jax.ShapeDtypeStruct((s, LANES), F32), jax.ShapeDtypeStruct((s, LANES), F32),
           jax.ShapeDtypeStruct((8, wd), F32)],
        scratch_shapes=[pltpu.VMEM((wd // LANES, t, LANES), F32)],
        compiler_params=_cparams(dimension_semantics=("arbitrary",)),
    )(dr1b, w_mix_b, out_a, out_b, g_win, g_dil)
    return doa, [dob0, dob1, dob2], dla, dlb, st


def _assemble_dz(dqa, dka, dva, dqs, dks, dvs, *, t):
    s = dqa.shape[1]
    wd = DIL_SLOTS * HEAD_DIM

    def body(*refs):
        a_refs, g_refs, o_ref, scr = refs[:3], refs[3:12], refs[12], refs[13]
        col = 0
        for r in a_refs:
            o_ref[:, col:col + r.shape[-1]] = r[...]
            col += r.shape[-1]
        for part in range(3):
            for gi, dil in enumerate(DILATIONS):
                val = _from_subsequences(g_refs[3 * part + gi], scr, dil, t)
                o_ref[:, col:col + wd] = val.astype(BF16)
                col += wd

    a_specs = [pl.BlockSpec((None, t, a.shape[-1]), lambda i: (0, i, 0)) for a in (dqa, dka, dva)]
    g_specs = [pl.BlockSpec((dil, t // dil, wd), lambda i: (0, i, 0)) for _ in range(3) for dil in DILATIONS]
    return pl.pallas_call(
        body, name="assemble_dz", grid=(s // t,), in_specs=a_specs + g_specs,
        out_specs=pl.BlockSpec((t, IN_WIDTH), lambda i: (i, 0)),
        out_shape=jax.ShapeDtypeStruct((s, IN_WIDTH), BF16),
        scratch_shapes=[pltpu.VMEM((wd // LANES, t, LANES), F32)],
        compiler_params=_cparams(dimension_semantics=("parallel",)),
    )(dqa, dka, dva, *dqs, *dks, *dvs)


def _mem_fwd(mem, g, b, wk_b, wv_b):
    ml = mem.shape[0]

    def body(mem_ref, g_ref, b_ref, wk_ref, wv_ref, mn_ref, kx_ref, vx_ref):
        mn = _ln(mem_ref[...], g_ref[...], b_ref[...]).astype(BF16)
        mn_ref[...] = mn
        kx_ref[...] = _dot(mn, wk_ref[...]).astype(BF16)
        vx_ref[...] = _dot(mn, wv_ref[...]).astype(BF16)

    sh = jax.ShapeDtypeStruct((ml, D_MODEL), BF16)
    return pl.pallas_call(body, name="mem_fwd", out_shape=[sh, sh, sh], compiler_params=_cparams())(
        mem, g, b, wk_b, wv_b)


def _mem_bwd(dkx, dvx, mem, g, b, wk_b, wv_b):
    def body(dk_ref, dv_ref, mem_ref, g_ref, b_ref, wk_ref, wv_ref, dwk_ref, dwv_ref, st_ref):
        mem_v = mem_ref[...]
        mn = _ln(mem_v, g_ref[...], b_ref[...]).astype(BF16)
        dkb = dk_ref[...].astype(BF16)
        dvb = dv_ref[...].astype(BF16)
        dwk_ref[...] = _dot_tn(mn, dkb)
        dwv_ref[...] = _dot_tn(mn, dvb)
        dmn = _dot_nt(dkb, wk_ref[...]) + _dot_nt(dvb, wv_ref[...])
        _, dg, db = _ln_bwd_math(dmn, mem_v, g_ref[...])
        st_ref[...] = jnp.zeros_like(st_ref)
        st_ref[0:1, :] = dg
        st_ref[1:2, :] = db

    sw = jax.ShapeDtypeStruct((D_MODEL, D_MODEL), F32)
    return pl.pallas_call(body, name="mem_bwd", out_shape=[sw, sw, jax.ShapeDtypeStruct((8, D_MODEL), F32)],
                          compiler_params=_cparams())(dkx, dvx, mem, g, b, wk_b, wv_b)


def _xattn_fwd(h1b, r1, kx, vx, wq_b, wo_b, ln1_g, ln1_b, ln2_g, ln2_b, *, t, comm=None):
    s = h1b.shape[0]
    scale = X_HEAD_DIM ** -0.5

    def body(h_ref, r1_ref, kx_ref, vx_ref, wq_ref, wo_ref, g1, b1, g2, b2, r2_ref, h2_ref, qx_ref, ox_ref, lse_ref):
        qxb = _dot(h_ref[...], wq_ref[...]).astype(BF16)
        qx_ref[...] = qxb
        lane = lax.broadcasted_iota(jnp.int32, (t, LANES), 1)
        lse_acc = jnp.zeros((t, LANES), F32)
        parts = []
        for h in range(X_HEADS):
            hs = slice(h * X_HEAD_DIM, (h + 1) * X_HEAD_DIM)
            sc = _dot_nt(qxb[:, hs] * scale, kx_ref[:, hs])
            m = jnp.max(sc, axis=1, keepdims=True)
            p = jnp.exp(sc - m)
            den = jnp.sum(p, axis=1, keepdims=True)
            parts.append(_dot(p.astype(BF16), vx_ref[:, hs]) / den)
            lse_acc = jnp.where(lane == h, m + jnp.log(den), lse_acc)
        lse_ref[...] = lse_acc
        oxb = jnp.concatenate(parts, axis=1).astype(BF16)
        ox_ref[...] = oxb
        h1 = _ln(r1_ref[...], g1[...], b1[...])
        r2 = ALPHA * h1 + _dot(oxb, wo_ref[...])
        r2_ref[...] = r2
        h2_ref[...] = _ln(r2, g2[...], b2[...]).astype(BF16)

    tile = pl.BlockSpec((t, D_MODEL), lambda i: (i, 0))
    row = pl.BlockSpec((1, D_MODEL), lambda i: (0, 0))
    full = lambda r: pl.BlockSpec((r, D_MODEL), lambda i: (0, 0))
    ml = kx.shape[0]
    bsh = jax.ShapeDtypeStruct((s, D_MODEL), BF16)
    return _pcall(
        body, name="xattn_fwd", grid=(s // t,),
        in_specs=[tile, tile, full(ml), full(ml), full(D_MODEL), full(D_MODEL), row, row, row, row],
        out_specs=[tile, tile, tile, tile, pl.BlockSpec((t, LANES), lambda i: (i, 0))],
        out_shape=[jax.ShapeDtypeStruct((s, D_MODEL), F32), bsh, bsh, bsh, jax.ShapeDtypeStruct((s, LANES), F32)],
        args=[h1b, r1, kx, vx, wq_b, wo_b, ln1_g, ln1_b, ln2_g, ln2_b], dims=("parallel",), comm=comm)


def _xattn_bwd(dr2, qxb, oxb, lse, kx, vx, wq_b, wo_b, r1, ln1_g, *, t, comm=None):
    s = dr2.shape[0]
    ml = kx.shape[0]
    scale = X_HEAD_DIM ** -0.5

    def body(dr2_ref, qx_ref, ox_ref, lse_ref, kx_ref, vx_ref, wq_ref, wo_ref, r1_ref, g1_ref,
             dr1_ref, dr1b_ref, dqx_ref, dkx_ref, dvx_ref, st_ref):
        i = pl.program_id(0)

        @pl.when(i == 0)
        def _():
            dkx_ref[...] = jnp.zeros_like(dkx_ref)
            dvx_ref[...] = jnp.zeros_like(dvx_ref)
            st_ref[...] = jnp.zeros_like(st_ref)

        dr2v = dr2_ref[...]
        dox = _dot_nt(dr2v.astype(BF16), wo_ref[...])
        parts = []
        for h in range(X_HEADS):
            hs = slice(h * X_HEAD_DIM, (h + 1) * X_HEAD_DIM)
            doh = dox[:, hs]
            dohb = doh.astype(BF16)
            dl = jnp.sum(doh * ox_ref[:, hs].astype(F32), axis=1, keepdims=True)
            qh = qx_ref[:, hs] * scale
            p = jnp.exp(_dot_nt(qh, kx_ref[:, hs]) - lse_ref[:, h:h + 1])
            dp = _dot_nt(dohb, vx_ref[:, hs])
            dsb = (p * (dp - dl)).astype(BF16)
            parts.append(_dot(dsb, kx_ref[:, hs]) * scale)
            dkx_ref[:, hs] += _dot_tn(dsb, qh)
            dvx_ref[:, hs] += _dot_tn(p.astype(BF16), dohb)
        dqxb = jnp.concatenate(parts, axis=1).astype(BF16)
        dqx_ref[...] = dqxb
        dh1 = _dot_nt(dqxb, wq_ref[...]) + ALPHA * dr2v
        dr1, dg, db = _ln_bwd_math(dh1, r1_ref[...], g1_ref[...])
        dr1_ref[...] = dr1
        dr1b_ref[...] = dr1.astype(BF16)
        st_ref[0:1, :] += dg
        st_ref[1:2, :] += db

    tile = pl.BlockSpec((t, D_MODEL), lambda i: (i, 0))
    full = lambda r: pl.BlockSpec((r, D_MODEL), lambda i: (0, 0))
    bsh = jax.ShapeDtypeStruct((s, D_MODEL), BF16)
    return _pcall(
        body, name="xattn_bwd", grid=(s // t,),
        in_specs=[tile, tile, tile, pl.BlockSpec((t, LANES), lambda i: (i, 0)), full(ml), full(ml),
                  full(D_MODEL), full(D_MODEL), tile, full(1)],
        out_specs=[tile, tile, tile, full(ml), full(ml), full(8)],
        out_shape=[jax.ShapeDtypeStruct((s, D_MODEL), F32), bsh, bsh,
                   jax.ShapeDtypeStruct((ml, D_MODEL), F32), jax.ShapeDtypeStruct((ml, D_MODEL), F32),
                   jax.ShapeDtypeStruct((8, D_MODEL), F32)],
        args=[dr2, qxb, oxb, lse, kx, vx, wq_b, wo_b, r1, ln1_g], dims=("arbitrary",), comm=comm)


def _halo_specs(t, s, width):
    tb8 = t // 8
    return [pl.BlockSpec((t, width), lambda i: (i, 0)),
            pl.BlockSpec((8, width), lambda i: (jnp.maximum(i * tb8 - 1, 0), 0)),
            pl.BlockSpec((8, width), lambda i: (jnp.minimum((i + 1) * tb8, s // 8 - 1), 0))]


def _halo_rows(i, n, prev_ref, next_ref):
    prev_row = jnp.where(i > 0, prev_ref[7:8, :], 0.0)
    next_row = jnp.where(i < n - 1, next_ref[0:1, :], 0.0)
    return prev_row, next_row


def _gelu_parts(gc):
    cdf = 0.5 * (1.0 + lax.erf(gc * (2.0 ** -0.5)))
    pdf = jnp.exp(-0.5 * gc * gc) * (1.0 / math.sqrt(2.0 * math.pi))
    return gc * cdf, cdf + gc * pdf


def _ffn_out(g, u, conv_w, conv_b, w_down_b, r2, target, ln2_g, ln2_b, ln3_g, ln3_b, *, t):
    s = r2.shape[0]
    n = s // t

    def body(g_ref, gp_ref, gn_ref, u_ref, cw_ref, cb_ref, w_ref, r2_ref, tg_ref, g2, b2, g3, b3,
             t_ref, dr_ref, drb_ref, st_ref):
        i = pl.program_id(0)

        @pl.when(i == 0)
        def _():
            st_ref[...] = jnp.zeros_like(st_ref)

        gv = g_ref[...]
        prev_row, next_row = _halo_rows(i, n, gp_ref, gn_ref)
        gm1, gp1 = _shift_rows(gv, prev_row, next_row)
        gc = gm1 * cw_ref[0:1, :] + gv * cw_ref[1:2, :] + gp1 * cw_ref[2:3, :] + cb_ref[...]
        act, _ = _gelu_parts(gc)
        tb = (act * u_ref[...]).astype(BF16)
        t_ref[...] = tb
        h2 = _ln(r2_ref[...], g2[...], b2[...])
        r3 = ALPHA * h2 + _dot(tb, w_ref[...])
        y = _ln(r3, g3[...], b3[...])
        err = y - tg_ref[...]
        loss = 0.5 * jnp.sum(jnp.mean(err * err, axis=-1, keepdims=True))
        dr, dg, db = _ln_bwd_math(err * (1.0 / D_MODEL), r3, g3[...])
        dr_ref[...] = dr
        drb_ref[...] = dr.astype(BF16)
        st_ref[0:1, :] += dg
        st_ref[1:2, :] += db
        st_ref[2:3, :] += jnp.full((1, D_MODEL), loss, F32)

    wide = pl.BlockSpec((t, D_FF), lambda i: (i, 0))
    tile = pl.BlockSpec((t, D_MODEL), lambda i: (i, 0))
    row = pl.BlockSpec((1, D_MODEL), lambda i: (0, 0))
    return pl.pallas_call(
        body, name="ffn_out", grid=(n,),
        in_specs=_halo_specs(t, s, D_FF) + [wide, pl.BlockSpec((3, D_FF), lambda i: (0, 0)),
                                            pl.BlockSpec((1, D_FF), lambda i: (0, 0)),
                                            pl.BlockSpec((D_FF, D_MODEL), lambda i: (0, 0)),
                                            tile, tile, row, row, row, row],
        out_specs=[wide, tile, tile, pl.BlockSpec((8, D_MODEL), lambda i: (0, 0))],
        out_shape=[jax.ShapeDtypeStruct((s, D_FF), BF16), jax.ShapeDtypeStruct((s, D_MODEL), F32),
                   jax.ShapeDtypeStruct((s, D_MODEL), BF16), jax.ShapeDtypeStruct((8, D_MODEL), F32)],
        compiler_params=_cparams(dimension_semantics=("arbitrary",)),
    )(g, g, g, u, conv_w, conv_b, w_down_b, r2, target, ln2_g, ln2_b, ln3_g, ln3_b)


def _dh2_ln2(dgc, conv_w, du, w_gate_b, w_up_b, dr3, r2, ln2_g, *, t, comm=None):
    s = dgc.shape[0]
    n = s // t

    def body(d_ref, dp_ref, dn_ref, cw_ref, du_ref, wg_ref, wu_ref, dr3_ref, r2_ref, g2, dg_ref, dr_ref, drb_ref,
             st_ref):
        i = pl.program_id(0)

        @pl.when(i == 0)
        def _():
            st_ref[...] = jnp.zeros_like(st_ref)

        dv = d_ref[...]
        prev_row, next_row = _halo_rows(i, n, dp_ref, dn_ref)
        dm1, dp1 = _shift_rows(dv, prev_row, next_row)
        dgb = (dp1 * cw_ref[0:1, :] + dv * cw_ref[1:2, :] + dm1 * cw_ref[2:3, :]).astype(BF16)
        dg_ref[...] = dgb
        dh2 = _dot(dgb, wg_ref[...]) + _dot(du_ref[...], wu_ref[...]) + ALPHA * dr3_ref[...]
        dr, dg, db = _ln_bwd_math(dh2, r2_ref[...], g2[...])
        dr_ref[...] = dr
        drb_ref[...] = dr.astype(BF16)
        st_ref[0:1, :] += dg
        st_ref[1:2, :] += db

    wide = pl.BlockSpec((t, D_FF), lambda i: (i, 0))
    tile = pl.BlockSpec((t, D_MODEL), lambda i: (i, 0))
    wfull = pl.BlockSpec((D_FF, D_MODEL), lambda i: (0, 0), pipeline_mode=pl.Buffered(1))
    return _pcall(
        body, name="dh2_ln2", grid=(n,),
        in_specs=_halo_specs(t, s, D_FF) + [pl.BlockSpec((3, D_FF), lambda i: (0, 0)), wide, wfull, wfull,
                                            tile, tile, pl.BlockSpec((1, D_MODEL), lambda i: (0, 0))],
        out_specs=[wide, tile, tile, pl.BlockSpec((8, D_MODEL), lambda i: (0, 0))],
        out_shape=[jax.ShapeDtypeStruct((s, D_FF), BF16), jax.ShapeDtypeStruct((s, D_MODEL), F32),
                   jax.ShapeDtypeStruct((s, D_MODEL), BF16), jax.ShapeDtypeStruct((8, D_MODEL), F32)],
        args=[dgc, dgc, dgc, conv_w, du, w_gate_b, w_up_b, dr3, r2, ln2_g], dims=("arbitrary",), comm=comm)


def _conv_bwd_a(dr3b, w_down_b, g, u, conv_w, conv_b, *, t):
    s = g.shape[0]
    n = s // t

    def body(d_ref, w_ref, g_ref, gp_ref, gn_ref, u_ref, cw_ref, cb_ref, du_ref, dgc_ref, st_ref):
        i = pl.program_id(0)

        @pl.when(i == 0)
        def _():
            st_ref[...] = jnp.zeros_like(st_ref)

        dt = _dot_nt(d_ref[...], w_ref[...])
        gv = g_ref[...]
        prev_row, next_row = _halo_rows(i, n, gp_ref, gn_ref)
        gm1, gp1 = _shift_rows(gv, prev_row, next_row)
        gc = gm1 * cw_ref[0:1, :] + gv * cw_ref[1:2, :] + gp1 * cw_ref[2:3, :] + cb_ref[...]
        act, dact = _gelu_parts(gc)
        du_ref[...] = (dt * act).astype(BF16)
        dgc = dt * u_ref[...] * dact
        dgc_ref[...] = dgc
        st_ref[0:1, :] += jnp.sum(gm1 * dgc, axis=0, keepdims=True)
        st_ref[1:2, :] += jnp.sum(gv * dgc, axis=0, keepdims=True)
        st_ref[2:3, :] += jnp.sum(gp1 * dgc, axis=0, keepdims=True)
        st_ref[3:4, :] += jnp.sum(dgc, axis=0, keepdims=True)

    tile = pl.BlockSpec((t, D_FF), lambda i: (i, 0))
    return pl.pallas_call(
        body, name="conv_bwd_a", grid=(n,),
        in_specs=[pl.BlockSpec((t, D_MODEL), lambda i: (i, 0)), pl.BlockSpec((D_FF, D_MODEL), lambda i: (0, 0))]
        + _halo_specs(t, s, D_FF) + [tile, pl.BlockSpec((3, D_FF), lambda i: (0, 0)),
                                     pl.BlockSpec((1, D_FF), lambda i: (0, 0))],
        out_specs=[tile, tile, pl.BlockSpec((8, D_FF), lambda i: (0, 0))],
        out_shape=[jax.ShapeDtypeStruct((s, D_FF), BF16), jax.ShapeDtypeStruct((s, D_FF), F32),
                   jax.ShapeDtypeStruct((8, D_FF), F32)],
        compiler_params=_cparams(dimension_semantics=("arbitrary",)),
    )(dr3b, w_down_b, g, g, g, u, conv_w, conv_b)


def _to_residue(a, dil):
    s, w = a.shape
    return a.reshape(s // dil, dil, w).transpose(1, 0, 2)


def _stats_to_lanes(rows):
    dil, hq, l = rows.shape
    return jnp.pad(rows.transpose(2, 0, 1).reshape(dil * l, hq), ((0, 0), (0, LANES - hq)))


def _stats_to_rows(lanes, dil):
    s = lanes.shape[0]
    return lanes[:, :DIL_SLOTS].reshape(s // dil, dil, DIL_SLOTS).transpose(1, 2, 0)


def _rope_angles(positions):
    inv_freq = ROPE_THETA ** (-jnp.arange(0, ROT_DIM, 2, dtype=F32) / ROT_DIM)
    ang = positions.astype(F32)[:, None] * inv_freq
    return jnp.concatenate([jnp.cos(ang), jnp.sin(ang)], axis=1)


class _NoPlan:
    def gather(self, stage):
        return None

    def gathered(self, stage, couts, wb):
        pass

    def exchange(self, stage, grads):
        return None

    def exchanged(self, stage, couts):
        pass


def _local_step(x, mem, positions, target, wb, sp, plan=None, *, t_row=256, t_mm=512, tq_a=128, tq_b=128,
                sub_a=4, sub_b=4):
    s = x.shape[0]
    plan = plan or _NoPlan()
    cs = _rope_angles(positions)
    e_mat = _rope_select_matrix()

    h0b, couts = _ln_in_fwd(x, sp["ln_in_g"], sp["ln_in_b"], t=t_mm, comm=plan.gather("ln_in"))
    plan.gathered("ln_in", couts, wb)
    sp = dict(sp, conv_w=wb.get("conv_w", sp.get("conv_w")))
    (za, *zb), couts = _proj_all(h0b, wb["w_in"], cs, e_mat, t=min(2 * t_mm, s), comm=plan.gather("proj"))
    plan.gathered("proj", couts, wb)
    sub_a = max(1, min(sub_a, s // tq_a))
    subs_b = [max(1, min(sub_b, s // dil // tq_b)) for dil in DILATIONS]
    subs_b_bwd = [s // dil // tq_b if s // dil <= ONE_TILE_ROWS else min(2 * sb, s // dil // tq_b)
                  for dil, sb in zip(DILATIONS, subs_b)]
    out_a, lse_a, couts = _swa_fwd_p(za, qcol=0, kcol=4, vcol=5, hq=WIN_Q_HEADS, hkv=WIN_KV_HEADS, w=WIN_HALF,
                                     tq=tq_a, sub=sub_a, sink=sp["attn_sink"], name="attn_a_fwd",
                                     comm=plan.gather("attn_a"))
    plan.gathered("attn_a", couts, wb)
    o_g, lse_g = [], []
    for gi in range(3):
        o, l, couts = _swa_fwd_p(zb[gi], qcol=0, kcol=1, vcol=2, hq=DIL_SLOTS, hkv=DIL_SLOTS, w=DIL_HALF, tq=tq_b,
                                 sub=subs_b[gi], sink=None, name=f"attn_b{gi}_fwd",
                                 comm=plan.gather(f"attn_b{gi}"))
        plan.gathered(f"attn_b{gi}", couts, wb)
        o_g.append(o)
        lse_g.append(_stats_to_lanes(l))
    (mixed_b, out_b, lse_b, r1, h1b), couts = _combine_fwd(
        out_a, o_g, lse_g, sp["g_win"], sp["g_dil"], wb["w_mix_out"], x, sp["ln_in_g"], sp["ln_in_b"],
        sp["ln1_g"], sp["ln1_b"], t=t_row, comm=plan.gather("combine"))
    plan.gathered("combine", couts, wb)
    mem_nb, kx, vx = _mem_fwd(mem, sp["mem_ln_g"], sp["mem_ln_b"], wb["w_xk"], wb["w_xv"])
    (r2, h2b, qxb, oxb, lse_x), couts = _xattn_fwd(
        h1b, r1, kx, vx, wb["w_xq"], wb["w_xo"], sp["ln1_g"], sp["ln1_b"], sp["ln2_g"], sp["ln2_b"], t=t_mm,
        comm=plan.gather("xattn"))
    plan.gathered("xattn", couts, wb)
    g = _mm(h2b, wb["w_gate"], mode="nt", out_dtype=F32, tm=t_mm, tn=D_FF, name="ff_gate")
    u = _mm(h2b, wb["w_up"], mode="nt", out_dtype=F32, tm=t_mm, tn=D_FF, name="ff_up")
    tb, dr3, dr3b, st3 = _ffn_out(g, u, sp["conv_w"], sp["conv_b"], wb["w_down"], r2, target, sp["ln2_g"],
                                  sp["ln2_b"], sp["ln3_g"], sp["ln3_b"], t=t_row)

    grads = {}
    du, dgc, st_conv = _conv_bwd_a(dr3b, wb["w_down"], g, u, sp["conv_w"], sp["conv_b"], t=t_row)
    tk = min(2048, s)
    grads["w_down"] = _mm(tb, dr3b, mode="tn", out_dtype=BF16, tm=D_FF // 2, tn=D_MODEL, tk=tk, name="dw_down")
    grads["w_up"] = _mm(du, h2b, mode="tn", out_dtype=BF16, tm=D_FF // 2, tn=D_MODEL, tk=tk, name="dw_up")
    (dg, dr2, dr2b, st2), couts = _dh2_ln2(dgc, sp["conv_w"], du, wb["w_gate"], wb["w_up"], dr3, r2, sp["ln2_g"],
                                           t=t_mm, comm=plan.exchange("dh2", grads))
    plan.exchanged("dh2", couts)
    grads["w_gate"] = _mm(dg, h2b, mode="tn", out_dtype=BF16, tm=D_FF // 2, tn=D_MODEL, tk=tk, name="dw_gate")

    (dr1, dr1b, dqxb, dkx, dvx, st1), couts = _xattn_bwd(
        dr2, qxb, oxb, lse_x, kx, vx, wb["w_xq"], wb["w_xo"], r1, sp["ln1_g"], t=t_mm,
        comm=plan.exchange("xattn", grads))
    plan.exchanged("xattn", couts)
    grads["w_xo"] = _mm(oxb, dr2b, mode="tn", out_dtype=BF16, tm=D_MODEL, tn=D_MODEL, tk=tk, name="dw_xo")
    grads["w_xq"] = _mm(h1b, dqxb, mode="tn", out_dtype=BF16, tm=D_MODEL, tn=D_MODEL, tk=tk, name="dw_xq")
    grads["w_xk"], grads["w_xv"], st_mem = _mem_bwd(dkx, dvx, mem, sp["mem_ln_g"], sp["mem_ln_b"],
                                                    wb["w_xk"], wb["w_xv"])

    grads["w_mix_out"] = _mm(mixed_b, dr1b, mode="tn", out_dtype=BF16, tm=D_MODEL, tn=D_MODEL, tk=tk,
                             name="dw_mix")
    do_a, do_b, dl_a, dl_b, st_mix = _combine_bwd(dr1b, wb["w_mix_out"], out_a, out_b, sp["g_win"], sp["g_dil"],
                                                  t=t_row)
    (dqa, dka, dva, dsink), couts = _swa_bwd_p(
        za, do_a, lse_a, _stats_to_rows(dl_a, 1), cs[None], e_mat, qcol=0, kcol=4, vcol=5, hq=WIN_Q_HEADS,
        hkv=WIN_KV_HEADS, w=WIN_HALF, tq=2 * tq_a, sub=sub_a, sink=sp["attn_sink"], name="attn_a_bwd",
        comm=plan.exchange("attn_a", grads))
    plan.exchanged("attn_a", couts)
    dqs, dks, dvs = [], [], []
    for gi, dil in enumerate(DILATIONS):
        (dq, dk, dv), couts = _swa_bwd_p(
            zb[gi], do_b[gi], _stats_to_rows(lse_b, dil), _stats_to_rows(dl_b, dil),
            _to_residue(cs, dil), e_mat, qcol=0, kcol=1, vcol=2, hq=DIL_SLOTS, hkv=DIL_SLOTS, w=DIL_HALF, tq=tq_b,
            sub=subs_b_bwd[gi], sink=None, name=f"attn_b{gi}_bwd", comm=plan.exchange(f"attn_b{gi}", grads))
        plan.exchanged(f"attn_b{gi}", couts)
        dqs.append(dq)
        dks.append(dk)
        dvs.append(dv)
    dz = _assemble_dz(dqa, dka, dva, dqs, dks, dvs, t=t_mm)
    grads["w_in"] = _mm(dz, h0b, mode="tn", out_dtype=BF16, tm=IN_WIDTH // 7, tn=D_MODEL, tk=tk, name="dw_in")
    (grad_x, st0), couts = _dh0_ln_in(dz, wb["w_in"], dr1, x, sp["ln_in_g"], t=t_mm,
                                      comm=plan.exchange("dh0", grads))
    plan.exchanged("dh0", couts)

    small = {
        "loss": st3[2:3, 0:1],
        "ln_in_g": st0[0:1], "ln_in_b": st0[1:2],
        "attn_sink": dsink[:, 0].reshape(1, WIN_Q_HEADS),
        "g_win": st_mix[0:1], "g_dil": st_mix[1:2],
        "ln1_g": st1[0:1], "ln1_b": st1[1:2],
        "mem_ln_g": st_mem[0:1], "mem_ln_b": st_mem[1:2],
        "ln2_g": st2[0:1], "ln2_b": st2[1:2],
        "conv_w": st_conv[0:3], "conv_b": st_conv[3:4],
        "ln3_g": st3[0:1], "ln3_b": st3[1:2],
    }
    return grad_x, grads, small


class _SiblingSwap:
    def __init__(self, arrays):
        self.inputs = list(arrays)
        n = len(arrays)
        self.out_shape = [jax.ShapeDtypeStruct(a.shape, a.dtype) for a in arrays]
        self.scratch = [pltpu.SemaphoreType.DMA((n,)), pltpu.SemaphoreType.DMA((n,))]

    def _copies(self, src, dst, sems):
        send_sems, recv_sems = sems
        x, y, c, _ = _place()
        return [pltpu.make_async_remote_copy(
            src_ref=src[a], dst_ref=dst[a], send_sem=send_sems.at[a], recv_sem=recv_sems.at[a],
            device_id=(x, y, 1 - c), device_id_type=MESH_IDS) for a in range(len(src))]

    def start(self, src, dst, sems):
        for cp in self._copies(src, dst, sems):
            cp.start()

    def wait(self, src, dst, sems):
        copies = self._copies(src, dst, sems)
        for cp in copies:
            cp.wait_recv()
        for cp in copies:
            cp.wait_send()


class _Both:
    def __init__(self, first, second):
        self.parts = (first, second)
        self.inputs = first.inputs + second.inputs
        self.out_shape = first.out_shape + second.out_shape
        self.scratch = first.scratch + second.scratch

    def _split(self, src, dst, sems):
        a = self.parts[0]
        ni, no, ns = len(a.inputs), len(a.out_shape), len(a.scratch)
        return ((src[:ni], dst[:no], sems[:ns]), (src[ni:], dst[no:], sems[ns:]))

    def start(self, src, dst, sems):
        for part, args in zip(self.parts, self._split(src, dst, sems)):
            part.start(*args)

    def wait(self, src, dst, sems):
        for part, args in zip(self.parts, self._split(src, dst, sems)):
            part.wait(*args)


def _row_tile(rows, cols, itemsize=4, budget=1 << 20):
    best = None
    for t in range(16, rows + 1, 16):
        if rows % t == 0 and t * cols * itemsize <= budget:
            best = t
    return best or rows


def _sum_slots(stack, *, name):
    n, r, c = stack.shape
    t = _row_tile(r, c)

    def body(s_ref, o_ref):
        acc = s_ref[0].astype(F32)
        for q in range(1, n):
            acc = acc + s_ref[q].astype(F32)
        o_ref[...] = acc

    return pl.pallas_call(
        body, name=name, grid=(r // t,), in_specs=[pl.BlockSpec((n, t, c), lambda i: (0, i, 0))],
        out_specs=pl.BlockSpec((t, c), lambda i: (i, 0)), out_shape=jax.ShapeDtypeStruct((r, c), F32),
        compiler_params=_cparams(dimension_semantics=("parallel",)),
    )(stack)


def _adamw(w, m, v, p, q, *, name):
    r, c = w.shape
    t = _row_tile(r, c, budget=1 << 20)

    def total(ref):
        if len(ref.shape) == 2:
            return ref[...]
        acc = ref[0].astype(F32)
        for slot in range(1, ref.shape[0]):
            acc = acc + ref[slot].astype(F32)
        return acc

    def body(*refs):
        if q is None:
            w_ref, m_ref, v_ref, p_ref, g_ref, d_ref, nm_ref, nv_ref = refs
            g = total(p_ref)
        else:
            w_ref, m_ref, v_ref, p_ref, q_ref, g_ref, d_ref, nm_ref, nv_ref = refs
            g = total(p_ref) + total(q_ref)
        nm = ADAM_B1 * m_ref[...] + (1.0 - ADAM_B1) * g
        nv = ADAM_B2 * v_ref[...] + (1.0 - ADAM_B2) * (g * g)
        m_hat = nm / (1.0 - ADAM_B1 ** ADAM_STEP)
        v_hat = nv / (1.0 - ADAM_B2 ** ADAM_STEP)
        g_ref[...] = g
        d_ref[...] = -ADAM_LR * (m_hat / (jnp.sqrt(v_hat) + ADAM_EPS) + ADAM_WD * w_ref[...])
        nm_ref[...] = nm
        nv_ref[...] = nv

    tile = pl.BlockSpec((t, c), lambda i: (i, 0))
    args = [w, m, v, p] + ([] if q is None else [q])
    in_specs = [tile if a.ndim == 2 else pl.BlockSpec((a.shape[0], t, c), lambda i: (0, i, 0)) for a in args]
    sh = jax.ShapeDtypeStruct((r, c), F32)
    return pl.pallas_call(
        body, name=name, grid=(r // t,), in_specs=in_specs, out_specs=[tile] * 4, out_shape=[sh] * 4,
        compiler_params=_cparams(dimension_semantics=("parallel",)),
    )(*args)


BIG = ("w_in", "w_mix_out", "w_xq", "w_xk", "w_xv", "w_xo", "w_gate", "w_up", "w_down")
COL_SHARDED = ("w_in", "w_gate", "w_up")
WEIGHTS = ("ln_in_g", "ln_in_b", "w_in", "attn_sink", "g_win", "g_dil", "w_mix_out", "ln1_g", "ln1_b",
           "mem_ln_g", "mem_ln_b", "w_xq", "w_xk", "w_xv", "w_xo", "ln2_g", "ln2_b", "w_gate", "w_up",
           "conv_w", "conv_b", "w_down", "ln3_g", "ln3_b")
SMALL = tuple(k for k in WEIGHTS if k not in BIG)
PACK_COLS = 1024
CONV_SHARD = D_FF // N_CHIPS
CONV_WIDTH_ROWS = 3
SMALL_ROWS = 32


GATHER_STAGES = {"ln_in": ("w_in", "conv_w"), "proj": ("w_mix_out", "w_xq", "w_xk", "w_xv", "w_xo", "w_up"),
                 "combine": ("w_gate", "w_down")}
EXCHANGE_STAGES = {"dh2": ("w_down", "w_up"), "attn_a": ("w_gate", "w_xo", "w_xq"),
                   "attn_b0": ("w_xk", "w_xv", "w_mix_out"), "dh0": ("w_in",)}


def _full_weight(k, g4):
    return g4.reshape(N_CHIPS * g4.shape[1], g4.shape[2])


def _grad_parts(k, gk):
    gk = gk.astype(BF16)
    return gk.reshape(N_CHIPS, gk.shape[0] // N_CHIPS, gk.shape[1])


EARLY_SWAP_STAGE = "attn_b2"


class _Plan:
    def __init__(self, shards):
        self.shards = shards
        self.recv = {}
        self.chip_sums = {}
        self.sibling_sums = {}

    def gather(self, stage):
        names = GATHER_STAGES.get(stage)
        return _ChipGather([self.shards[k] for k in names]) if names else None

    def gathered(self, stage, couts, wb):
        for k, g4 in zip(GATHER_STAGES.get(stage, ()), couts):
            if k == "conv_w":
                taps = g4[:, :CONV_WIDTH_ROWS, :CONV_SHARD]
                wb[k] = taps.transpose(1, 0, 2).reshape(CONV_WIDTH_ROWS, D_FF)
            else:
                wb[k] = _full_weight(k, g4)

    def exchange(self, stage, grads):
        if stage == EARLY_SWAP_STAGE:
            self.early = [k for k in BIG if k in self.recv]
            for k in self.early:
                self.chip_sums[k] = self.recv[k]
            return _SiblingSwap([self.chip_sums[k] for k in self.early])
        names = EXCHANGE_STAGES.get(stage)
        return _ChipExchange([_grad_parts(k, grads[k]) for k in names]) if names else None

    def exchanged(self, stage, couts):
        if stage == EARLY_SWAP_STAGE:
            self.sibling_sums.update(zip(self.early, couts))
            return
        for k, r4 in zip(EXCHANGE_STAGES.get(stage, ()), couts):
            self.recv[k] = r4


def _pack_rows(a):
    r, n = a.shape
    per = -(-n // PACK_COLS)
    return jnp.pad(a, ((0, 0), (0, per * PACK_COLS - n))).reshape(r * per, PACK_COLS)


def _unpack_rows(p, r, n):
    per = -(-n // PACK_COLS)
    return p.reshape(r, per * PACK_COLS)[:, :n]


def _pack(pieces, rows_total):
    cat = jnp.concatenate([_pack_rows(a) for a in pieces], axis=0)
    return jnp.pad(cat, ((0, rows_total - cat.shape[0]), (0, 0)))


def _unpack(p, shapes):
    out, at = [], 0
    for r, n in shapes:
        per = -(-n // PACK_COLS)
        out.append(_unpack_rows(p[at:at + r * per], r, n))
        at += r * per
    return out


def kernel(x, mem, positions, ln_in_g, ln_in_b, w_in, attn_sink, g_win, g_dil, w_mix_out, ln1_g, ln1_b, mem_ln_g, mem_ln_b, w_xq, w_xk, w_xv, w_xo, ln2_g, ln2_b, w_gate, w_up, conv_w, conv_b, w_down, ln3_g, ln3_b, loss_target, m_ln_in_g, m_ln_in_b, m_w_in, m_attn_sink, m_g_win, m_g_dil, m_w_mix_out, m_ln1_g, m_ln1_b, m_mem_ln_g, m_mem_ln_b, m_w_xq, m_w_xk, m_w_xv, m_w_xo, m_ln2_g, m_ln2_b, m_w_gate, m_w_up, m_conv_w, m_conv_b, m_w_down, m_ln3_g, m_ln3_b, v_ln_in_g, v_ln_in_b, v_w_in, v_attn_sink, v_g_win, v_g_dil, v_w_mix_out, v_ln1_g, v_ln1_b, v_mem_ln_g, v_mem_ln_b, v_w_xq, v_w_xk, v_w_xv, v_w_xo, v_ln2_g, v_ln2_b, v_w_gate, v_w_up, v_conv_w, v_conv_b, v_w_down, v_ln3_g, v_ln3_b):
    given = dict(locals())
    shape_of = {k: given[k].shape for k in WEIGHTS}
    as2d = lambda k, a: a.reshape(-1, a.shape[-1]).T if k in COL_SHARDED else a.reshape(-1, a.shape[-1])
    w2 = {k: as2d(k, given[k]) for k in WEIGHTS}
    m2 = {k: as2d(k, given["m_" + k]) for k in WEIGHTS}
    v2 = {k: as2d(k, given["v_" + k]) for k in WEIGHTS}
    chip = 2 * lax.axis_index("x") + lax.axis_index("y")

    shards = {k: w2[k].astype(BF16) for k in BIG}
    shards["conv_w"] = jnp.pad(w2["conv_w"], ((0, 16 - CONV_WIDTH_ROWS), (0, PACK_COLS - CONV_SHARD)))
    plan = _Plan(shards)
    sp = {k: w2[k] for k in SMALL if k != "conv_w"}

    grad_x, grads, small = _local_step(x[0], mem[0], positions[0], loss_target[0], {}, sp, plan)

    small_keys = ("loss",) + SMALL
    small_shapes = [small[k].shape for k in small_keys]
    small_pack = _pack([small[k] for k in small_keys], SMALL_ROWS)
    late = [k for k in BIG if k not in plan.chip_sums]
    for k in late:
        plan.chip_sums[k] = _sum_slots(plan.recv[k], name=f"sum_chips_{k}")
    *late_sibling, small_all = _comm_only(
        _Both(_SiblingSwap([plan.chip_sums[k] for k in late]), _ChipExchange([], small_pack)), "swap_and_small")
    plan.sibling_sums.update(zip(late, late_sibling))
    chip_sums = [plan.chip_sums[k] for k in BIG]
    sibling_sums = [plan.sibling_sums[k] for k in BIG]
    small_sum = _sum_slots(small_all, name="sum_small")
    small_g = dict(zip(small_keys, _unpack(small_sum, small_shapes)))
    loss = small_g["loss"][0, 0]

    res = {}
    for k, p, q in zip(BIG, chip_sums, sibling_sums):
        res[k] = _adamw(w2[k], m2[k], v2[k], p, q, name=f"adamw_{k}")
    small_g["conv_w"] = lax.dynamic_slice_in_dim(small_g["conv_w"], chip * CONV_SHARD, CONV_SHARD, axis=1)
    adam_shapes = [w2[k].shape for k in SMALL]
    packs = [_pack([d[k] for k in SMALL], SMALL_ROWS) for d in (w2, m2, v2, small_g)]
    small_res = [_unpack(o, adam_shapes) for o in _adamw(*packs, None, name="adamw_small")]
    for i, k in enumerate(SMALL):
        res[k] = tuple(o[i] for o in small_res)

    outs = [loss, grad_x[None]]
    for slot in range(4):
        outs += [(res[k][slot].T if k in COL_SHARDED else res[k][slot]).reshape(shape_of[k]) for k in WEIGHTS]
    return tuple(outs)
```

```python
import functools
import math

import jax
import jax.numpy as jnp
from jax import lax
from jax.experimental import pallas as pl
from jax.experimental.pallas import tpu as pltpu

F32 = jnp.float32
BF16 = jnp.bfloat16

D_MODEL = 1024
HEAD_DIM = 64
WIN_Q_HEADS = 8
WIN_KV_HEADS = 2
WIN_HALF = 128
DIL_SLOTS = 8
DILATIONS = (1, 4, 16)
DIL_HALF = 64
ROT_DIM = 16
ROPE_THETA = 500000.0
X_HEADS = 4
X_HEAD_DIM = 256
D_FF = 2816
A_Q = 512
A_KV = 128
A_WIDTH = A_Q + 2 * A_KV
B_QKV = 1536
IN_WIDTH = 5376
ALPHA = 2.0 ** 0.25
LN_EPS = 1e-5
NEG_INF = -1e30
LANES = 128
N_CHIPS = 4
N_DEV = 8

ADAM_LR = 0.001
ADAM_B1 = 0.9
ADAM_B2 = 0.999
ADAM_EPS = 1e-08
ADAM_WD = 0.01
ADAM_STEP = 10

VMEM_LIMIT = 56 * 1024 * 1024


def _cparams(**kw):
    return pltpu.CompilerParams(vmem_limit_bytes=VMEM_LIMIT, **kw)


def _dot(a, b):
    return lax.dot_general(a, b, (((1,), (0,)), ((), ())), preferred_element_type=F32)


def _dot_nt(a, b):
    return lax.dot_general(a, b, (((1,), (1,)), ((), ())), preferred_element_type=F32)


def _dot_tn(a, b):
    return lax.dot_general(a, b, (((0,), (0,)), ((), ())), preferred_element_type=F32)


def _ln(x, g, b):
    mu = jnp.mean(x, axis=-1, keepdims=True)
    xc = x - mu
    var = jnp.mean(xc * xc, axis=-1, keepdims=True)
    return xc * lax.rsqrt(var + LN_EPS) * g + b


def _ln_bwd_math(dy, r, g):
    mu = jnp.mean(r, axis=-1, keepdims=True)
    xc = r - mu
    var = jnp.mean(xc * xc, axis=-1, keepdims=True)
    rstd = lax.rsqrt(var + LN_EPS)
    xhat = xc * rstd
    dxhat = dy * g
    m1 = jnp.mean(dxhat, axis=-1, keepdims=True)
    m2 = jnp.mean(dxhat * xhat, axis=-1, keepdims=True)
    dr = rstd * (dxhat - m1 - xhat * m2)
    return dr, jnp.sum(dy * xhat, axis=0, keepdims=True), jnp.sum(dy, axis=0, keepdims=True)


def _rope(z, ta, tb, tc, sign):
    w = z.shape[1]
    reps = w // LANES
    a = jnp.tile(ta, (1, reps))
    b = jnp.tile(tb, (1, reps))
    c = jnp.tile(tc, (1, reps))
    return z * a + sign * (pltpu.roll(z, w - 8, 1) * b + pltpu.roll(z, 8, 1) * c)


def _shift_rows(x, prev_row, next_row):
    t = x.shape[0]
    sub = 8
    row = lax.broadcasted_iota(jnp.int32, (sub, x.shape[1]), 0)
    down, up = pltpu.roll(x, 1, 0), pltpu.roll(x, t - 1, 0)
    xm1 = jnp.concatenate([jnp.where(row == 0, prev_row, down[:sub]), down[sub:]], axis=0)
    xp1 = jnp.concatenate([up[:t - sub], jnp.where(row == sub - 1, next_row, up[t - sub:])], axis=0)
    return xm1, xp1


def _rope_tabs(cs, e_mat):
    hi = cs.astype(BF16)
    rest = cs - hi.astype(F32)
    mid = rest.astype(BF16)
    lo = (rest - mid.astype(F32)).astype(BF16)
    tabs = _dot(hi, e_mat) + _dot(mid, e_mat) + _dot(lo, e_mat)
    lane = lax.broadcasted_iota(jnp.int32, (cs.shape[0], LANES), 1)
    ones = jnp.where((lane & (HEAD_DIM - 1)) >= ROT_DIM, 1.0, 0.0)
    return tabs[:, :LANES] + ones, tabs[:, LANES:2 * LANES], tabs[:, 2 * LANES:]


def _rope_select_matrix():
    half = ROT_DIM // 2
    e = [[0.0] * (3 * LANES) for _ in range(ROT_DIM)]
    for lane in range(LANES):
        d = lane % HEAD_DIM
        if d < half:
            e[d][lane] = 1.0
            e[half + d][LANES + lane] = -1.0
        elif d < ROT_DIM:
            e[d - half][lane] = 1.0
            e[d][2 * LANES + lane] = 1.0
    return jnp.array(e, BF16)


def _rope_rows(x, cos_t, sin_t, sign):
    half = ROT_DIM // 2
    parts = []
    for base in (0, HEAD_DIM):
        r1, r2 = x[base:base + half], x[base + half:base + ROT_DIM]
        parts += [r1 * cos_t - sign * (r2 * sin_t), r2 * cos_t + sign * (r1 * sin_t), x[base + ROT_DIM:base + HEAD_DIM]]
    return jnp.concatenate(parts, axis=0)


MESH_IDS = pl.DeviceIdType.MESH
ANY = pl.BlockSpec(memory_space=pl.ANY)


def _place():
    x, y, c = lax.axis_index("x"), lax.axis_index("y"), lax.axis_index("c")
    other_chips = [(1 - x, y), (x, 1 - y), (1 - x, 1 - y)]
    return x, y, c, other_chips


class _ChipGather:
    def __init__(self, shards):
        self.inputs = list(shards)
        n = len(shards)
        self.out_shape = [jax.ShapeDtypeStruct((N_CHIPS,) + a.shape, a.dtype) for a in shards]
        self.scratch = [pltpu.SemaphoreType.DMA((6 * n,)), pltpu.SemaphoreType.DMA((6 * n,)),
                        pltpu.SemaphoreType.DMA((n,))]

    def _copies(self, src, dst, sems):
        send_sems, recv_sems, local_sems = sems
        x, y, c, chips = _place()
        mine = 2 * x + y
        n = len(src)
        local, sends, recvs, passes, pass_recvs = [], [], [], [], []
        for a in range(n):
            half = src[a].shape[0] // 2
            my_rows, other_rows = pl.ds(c * half, half), pl.ds((1 - c) * half, half)
            local.append(pltpu.make_async_copy(src[a], dst[a].at[mine], local_sems.at[a]))
            for j, (px, py) in enumerate(chips):
                k, k2, slot = 3 * a + j, 3 * n + 3 * a + j, 2 * px + py
                sends.append(pltpu.make_async_remote_copy(
                    src_ref=src[a].at[my_rows], dst_ref=dst[a].at[mine, my_rows], send_sem=send_sems.at[k],
                    recv_sem=recv_sems.at[k], device_id=(px, py, c), device_id_type=MESH_IDS))
                recvs.append(pltpu.make_async_remote_copy(
                    src_ref=src[a].at[my_rows], dst_ref=dst[a].at[slot, my_rows], send_sem=send_sems.at[k],
                    recv_sem=recv_sems.at[k], device_id=(px, py, c), device_id_type=MESH_IDS))
                passes.append(pltpu.make_async_remote_copy(
                    src_ref=dst[a].at[slot, my_rows], dst_ref=dst[a].at[slot, my_rows], send_sem=send_sems.at[k2],
                    recv_sem=recv_sems.at[k2], device_id=(x, y, 1 - c), device_id_type=MESH_IDS))
                pass_recvs.append(pltpu.make_async_remote_copy(
                    src_ref=dst[a].at[slot, my_rows], dst_ref=dst[a].at[slot, other_rows],
                    send_sem=send_sems.at[k2], recv_sem=recv_sems.at[k2], device_id=(x, y, 1 - c),
                    device_id_type=MESH_IDS))
        return local, sends, recvs, passes, pass_recvs

    def start(self, src, dst, sems):
        local, sends, _, _, _ = self._copies(src, dst, sems)
        for cp in local + sends:
            cp.start()

    def wait(self, src, dst, sems):
        local, sends, recvs, passes, pass_recvs = self._copies(src, dst, sems)
        for idx, landed in enumerate(recvs):
            landed.wait_recv()
            if passes:
                passes[idx].start()
        for cp in pass_recvs:
            cp.wait_recv()
        for cp in sends + passes:
            cp.wait_send()
        for cp in local:
            cp.wait()


class _ChipExchange:
    def __init__(self, parts, small=None):
        self.inputs = list(parts) + ([small] if small is not None else [])
        self.n = len(parts)
        self.has_small = small is not None
        self.out_shape = [jax.ShapeDtypeStruct(a.shape, a.dtype) for a in parts]
        n_sem, n_loc = 3 * self.n, self.n
        if self.has_small:
            self.out_shape.append(jax.ShapeDtypeStruct((N_DEV,) + small.shape, small.dtype))
            n_sem, n_loc = n_sem + N_DEV - 1, n_loc + 1
        self.scratch = [pltpu.SemaphoreType.DMA((n_sem,)), pltpu.SemaphoreType.DMA((n_sem,)),
                        pltpu.SemaphoreType.DMA((n_loc,))]

    def _copies(self, src, dst, sems):
        send_sems, recv_sems, local_sems = sems
        x, y, c, chips = _place()
        mine = 2 * x + y
        n = self.n
        local, sends, recvs = [], [], []
        for a in range(n):
            local.append(pltpu.make_async_copy(src[a].at[mine], dst[a].at[mine], local_sems.at[a]))
            for j, (px, py) in enumerate(chips):
                k = 3 * a + j
                sends.append(pltpu.make_async_remote_copy(
                    src_ref=src[a].at[2 * px + py], dst_ref=dst[a].at[mine], send_sem=send_sems.at[k],
                    recv_sem=recv_sems.at[k], device_id=(px, py, c), device_id_type=MESH_IDS))
                recvs.append(pltpu.make_async_remote_copy(
                    src_ref=src[a].at[mine], dst_ref=dst[a].at[2 * px + py], send_sem=send_sems.at[k],
                    recv_sem=recv_sems.at[k], device_id=(px, py, c), device_id_type=MESH_IDS))
        if self.has_small:
            me_dev = 4 * x + 2 * y + c
            local.append(pltpu.make_async_copy(src[n], dst[n].at[me_dev], local_sems.at[n]))
            for mask in range(1, N_DEV):
                px, py, pc = x ^ ((mask >> 2) & 1), y ^ ((mask >> 1) & 1), c ^ (mask & 1)
                k = 3 * n + mask - 1
                sends.append(pltpu.make_async_remote_copy(
                    src_ref=src[n], dst_ref=dst[n].at[me_dev], send_sem=send_sems.at[k], recv_sem=recv_sems.at[k],
                    device_id=(px, py, pc), device_id_type=MESH_IDS))
                recvs.append(pltpu.make_async_remote_copy(
                    src_ref=src[n], dst_ref=dst[n].at[4 * px + 2 * py + pc], send_sem=send_sems.at[k],
                    recv_sem=recv_sems.at[k], device_id=(px, py, pc), device_id_type=MESH_IDS))
        return local, sends, recvs, [], []

    start = _ChipGather.start
    wait = _ChipGather.wait


def _pcall(body, *, name, grid, in_specs, out_specs, out_shape, args, scratch_shapes=(), dims=None, comm=None):
    in_specs, out_specs, out_shape = list(in_specs), list(out_specs), list(out_shape)
    scratch_shapes = list(scratch_shapes)
    if comm is None:
        outs = pl.pallas_call(
            body, name=name, grid=grid, in_specs=in_specs, out_specs=out_specs, out_shape=out_shape,
            scratch_shapes=scratch_shapes, compiler_params=_cparams(dimension_semantics=dims),
        )(*args)
        return list(outs), []
    n_in, n_out, n_scr = len(in_specs), len(out_specs), len(scratch_shapes)
    n_cin, n_cout = len(comm.inputs), len(comm.out_shape)

    def wrapped(*refs):
        ins, refs = refs[:n_in], refs[n_in:]
        cins, refs = refs[:n_cin], refs[n_cin:]
        outs, refs = refs[:n_out], refs[n_out:]
        couts, refs = refs[:n_cout], refs[n_cout:]
        scr, csems = refs[:n_scr], refs[n_scr:]
        first = last = None
        for axis, size in enumerate(grid):
            pid = pl.program_id(axis)
            f, l = pid == 0, pid == size - 1
            first = f if first is None else first & f
            last = l if last is None else last & l

        @pl.when(first)
        def _():
            comm.start(cins, couts, csems)

        body(*ins, *outs, *scr)

        @pl.when(last)
        def _():
            comm.wait(cins, couts, csems)

    res = pl.pallas_call(
        wrapped, name=name, grid=grid, in_specs=in_specs + [ANY] * n_cin, out_specs=out_specs + [ANY] * n_cout,
        out_shape=out_shape + list(comm.out_shape), scratch_shapes=scratch_shapes + list(comm.scratch),
        compiler_params=_cparams(dimension_semantics=("arbitrary",) * len(grid)),
    )(*args, *comm.inputs)
    return list(res[:n_out]), list(res[n_out:])


def _comm_only(comm, name):
    def body(*refs):
        n_cin, n_cout = len(comm.inputs), len(comm.out_shape)
        cins, couts, csems = refs[:n_cin], refs[n_cin:n_cin + n_cout], refs[n_cin + n_cout:]
        comm.start(cins, couts, csems)
        comm.wait(cins, couts, csems)

    return list(pl.pallas_call(
        body, name=name, in_specs=[ANY] * len(comm.inputs), out_specs=[ANY] * len(comm.out_shape),
        out_shape=list(comm.out_shape), scratch_shapes=list(comm.scratch),
    )(*comm.inputs))


def _mm(a, b, *, mode, out_dtype, tm, tn, tk=None, name):
    if mode == "nt":
        m, k = a.shape
        n = b.shape[0]
        assert m % tm == 0 and n % tn == 0

        def body(a_ref, b_ref, o_ref):
            o_ref[...] = _dot_nt(a_ref[...], b_ref[...]).astype(out_dtype)

        return pl.pallas_call(
            body, name=name, grid=(m // tm, n // tn),
            in_specs=[pl.BlockSpec((tm, k), lambda i, j: (i, 0)), pl.BlockSpec((tn, k), lambda i, j: (j, 0))],
            out_specs=pl.BlockSpec((tm, tn), lambda i, j: (i, j)),
            out_shape=jax.ShapeDtypeStruct((m, n), out_dtype),
            compiler_params=_cparams(dimension_semantics=("parallel", "parallel")),
        )(a, b)
    assert mode == "tn"
    kk, m = a.shape
    n = b.shape[1]
    assert m % tm == 0 and n % tn == 0 and kk % tk == 0
    nk = kk // tk

    def body(a_ref, b_ref, o_ref, acc_ref):
        kstep = pl.program_id(2)

        @pl.when(kstep == 0)
        def _():
            acc_ref[...] = jnp.zeros_like(acc_ref)

        acc_ref[...] += _dot_tn(a_ref[...], b_ref[...])

        @pl.when(kstep == nk - 1)
        def _():
            o_ref[...] = acc_ref[...].astype(out_dtype)

    return pl.pallas_call(
        body, name=name, grid=(m // tm, n // tn, nk),
        in_specs=[pl.BlockSpec((tk, tm), lambda i, j, s: (s, i)), pl.BlockSpec((tk, tn), lambda i, j, s: (s, j))],
        out_specs=pl.BlockSpec((tm, tn), lambda i, j, s: (i, j)),
        out_shape=jax.ShapeDtypeStruct((m, n), out_dtype),
        scratch_shapes=[pltpu.VMEM((tm, tn), F32)],
        compiler_params=_cparams(dimension_semantics=("parallel", "parallel", "arbitrary")),
    )(a, b)


PROJ_COLS = 256


def _proj_segments():
    wd = DIL_SLOTS * HEAD_DIM
    segs = [(1, [(0, 1), (PROJ_COLS, 1), (2 * PROJ_COLS, 2)])]
    for gi, dil in enumerate(DILATIONS):
        blocks = []
        for part, kind in enumerate((1, 1, 0)):
            col = A_WIDTH + part * B_QKV + gi * wd
            blocks += [(col, kind), (col + PROJ_COLS, kind)]
        segs.append((dil, blocks))
    return segs


PROJ_SEGMENTS = _proj_segments()


def _dh0_ln_in(dz, w_t, dr1, x, ln_in_g, *, t, comm=None):
    s, k = dz.shape

    def body(dz_ref, w_ref, dr1_ref, x_ref, g_ref, gx_ref, st_ref):
        i = pl.program_id(0)

        @pl.when(i == 0)
        def _():
            st_ref[...] = jnp.zeros_like(st_ref)

        dh0 = _dot(dz_ref[...], w_ref[...]) + ALPHA * dr1_ref[...]
        dx, dg, db = _ln_bwd_math(dh0, x_ref[...], g_ref[...])
        gx_ref[...] = dx
        st_ref[0:1, :] += dg
        st_ref[1:2, :] += db

    tile = pl.BlockSpec((t, D_MODEL), lambda i: (i, 0))
    return _pcall(
        body, name="dh0_ln_in", grid=(s // t,),
        in_specs=[pl.BlockSpec((t, k), lambda i: (i, 0)),
                  pl.BlockSpec((k, D_MODEL), lambda i: (0, 0), pipeline_mode=pl.Buffered(1)),
                  tile, tile, pl.BlockSpec((1, D_MODEL), lambda i: (0, 0))],
        out_specs=[tile, pl.BlockSpec((8, D_MODEL), lambda i: (0, 0))],
        out_shape=[jax.ShapeDtypeStruct((s, D_MODEL), F32), jax.ShapeDtypeStruct((8, D_MODEL), F32)],
        args=[dz, w_t, dr1, x, ln_in_g], dims=("arbitrary",), comm=comm)


def _ln_in_fwd(x, g, b, *, t, comm=None):
    s = x.shape[0]

    def body(x_ref, g_ref, b_ref, o_ref):
        o_ref[...] = _ln(x_ref[...], g_ref[...], b_ref[...]).astype(BF16)

    row = pl.BlockSpec((1, D_MODEL), lambda i: (0, 0))
    tile = pl.BlockSpec((t, D_MODEL), lambda i: (i, 0))
    outs, couts = _pcall(body, name="ln_in_fwd", grid=(s // t,), in_specs=[tile, row, row], out_specs=[tile],
                         out_shape=[jax.ShapeDtypeStruct((s, D_MODEL), BF16)], args=[x, g, b], dims=("parallel",),
                         comm=comm)
    return outs[0], couts


def _proj_all(h0b, w_t, cs, e_mat, *, t, comm=None):
    s = h0b.shape[0]
    cb = PROJ_COLS
    halves = cb // LANES

    def body(h_ref, w_ref, cs_ref, e_ref, *rest):
        z_refs, scr = rest[:-1], rest[-1]
        h = h_ref[...]
        ta, tb, tc = (jnp.tile(tab, (1, halves)) for tab in _rope_tabs(cs_ref[...], e_ref[...]))
        lane = lax.broadcasted_iota(jnp.int32, (t, cb), 1)
        slot = 0
        for z_ref, (dil, blocks) in zip(z_refs, PROJ_SEGMENTS):
            for jb, (col, kind) in enumerate(blocks):
                acc = _dot_nt(h, w_ref[col:col + cb, :])
                if kind:
                    z = acc * ta + (pltpu.roll(acc, cb - 8, 1) * tb + pltpu.roll(acc, 8, 1) * tc)
                    if kind == 2:
                        z = jnp.where(lane < LANES, z, acc)
                else:
                    z = acc
                if dil == 1:
                    z_ref[0, :, cb * jb:cb * (jb + 1)] = z.astype(BF16)
                    continue
                for half in range(halves):
                    scr[slot, half] = z[:, half * LANES:(half + 1) * LANES]
                for c in range(dil):
                    for half in range(halves):
                        rows = scr[slot, half, pl.ds(c, t // dil, stride=dil), :]
                        z_ref[c, :, cb * jb + half * LANES:cb * jb + (half + 1) * LANES] = rows.astype(BF16)
                slot = 1 - slot

    widths = [cb * len(blocks) for _, blocks in PROJ_SEGMENTS]
    dils = [dil for dil, _ in PROJ_SEGMENTS]
    outs, couts = _pcall(
        body, name="proj_all", grid=(s // t,),
        in_specs=[pl.BlockSpec((t, D_MODEL), lambda i: (i, 0)),
                  pl.BlockSpec((IN_WIDTH, D_MODEL), lambda i: (0, 0), pipeline_mode=pl.Buffered(1)),
                  pl.BlockSpec((t, ROT_DIM), lambda i: (i, 0)), pl.BlockSpec((ROT_DIM, 3 * LANES), lambda i: (0, 0))],
        out_specs=[pl.BlockSpec((dil, t // dil, wd), lambda i: (0, i, 0)) for dil, wd in zip(dils, widths)],
        out_shape=[jax.ShapeDtypeStruct((dil, s // dil, wd), BF16) for dil, wd in zip(dils, widths)],
        args=[h0b, w_t, cs, e_mat], scratch_shapes=[pltpu.VMEM((2, halves, t, LANES), F32)],
        dims=("parallel",), comm=comm)
    return outs, couts


PAIR = 2 * HEAD_DIM
ONE_TILE_ROWS = 2048


def _place_head(x2, src_pos, dst_pos):
    hi = lax.broadcasted_iota(jnp.int32, x2.shape, 1) >= HEAD_DIM
    src = x2 if src_pos == dst_pos else pltpu.roll(x2, HEAD_DIM, 1)
    return jnp.where(hi == (dst_pos == 1), src, jnp.zeros_like(src))


def _band_mask_t(row0, tq, w, seq_len):
    tk = tq + 2 * w
    kk = lax.broadcasted_iota(jnp.int32, (tk, tq), 0)
    qq = lax.broadcasted_iota(jnp.int32, (tk, tq), 1)
    kpos = row0 - w + kk
    return (jnp.abs(qq + w - kk) <= w) & (kpos >= 0) & (kpos < seq_len)


def _halo_kv_specs(t, w, hkv, n, seq_len, kcol, vcol):
    kw = hkv * HEAD_DIM
    per, last = t // w, seq_len // w - 1
    cur = lambda s, i: jnp.minimum(i, n - 1)
    specs = []
    for c in (kcol, vcol):
        specs += [pl.BlockSpec((None, w, kw), lambda s, i, c=c: (s, jnp.maximum(cur(s, i) * per - 1, 0), c)),
                  pl.BlockSpec((None, t, kw), lambda s, i, c=c: (s, cur(s, i), c)),
                  pl.BlockSpec((None, w, kw), lambda s, i, c=c: (s, jnp.minimum((cur(s, i) + 1) * per, last), c))]
    return specs, cur


def _pair_kv(kfull, vfull, qp, rep, krows):
    ks, vs, a_of = [], [], []
    for pos in range(2):
        g = (2 * qp + pos) // rep
        a_of.append(g // 2)
        ks.append(_place_head(kfull[g // 2][krows], g % 2, pos))
        vs.append(_place_head(vfull[g // 2][krows], g % 2, pos))
    assert a_of[0] == a_of[1]
    return jnp.concatenate(ks, axis=0), jnp.concatenate(vs, axis=0), a_of[0]


def _swa_fwd_p(qkv, *, qcol, kcol, vcol, hq, hkv, w, tq, sub, sink, name, comm=None):
    nseq, seq_len, _ = qkv.shape
    t = tq * sub
    n = seq_len // t
    rep = hq // hkv
    tk = tq + 2 * w
    kv_specs, cur = _halo_kv_specs(t, w, hkv, n, seq_len, kcol, vcol)

    def body(*refs):
        if sink is not None:
            sink_ref, refs = refs[0], refs[1:]
        q_ref, kp_ref, kc_ref, kn_ref, vp_ref, vc_ref, vn_ref, o_ref, lse_ref = refs
        i = pl.program_id(1)
        kfull, vfull = [], []
        for a in range(hkv // 2):
            ls = slice(a * PAIR, (a + 1) * PAIR)
            kfull.append(jnp.concatenate([kp_ref[:, ls], kc_ref[:, ls], kn_ref[:, ls]], axis=0) * 0.125)
            vfull.append(jnp.concatenate([vp_ref[:, ls], vc_ref[:, ls], vn_ref[:, ls]], axis=0))
        row_hi = lax.broadcasted_iota(jnp.int32, (PAIR, tq), 0) >= HEAD_DIM
        for jj in range(sub):
            rows = slice(jj * tq, (jj + 1) * tq)
            mask_t = _band_mask_t(i * t + jj * tq, tq, w, seq_len)
            o_t, lse_rows = [], []
            for qp in range(hq // 2):
                kst, vst, _ = _pair_kv(kfull, vfull, qp, rep, slice(jj * tq, jj * tq + tk))
                s2 = _dot_nt(kst, q_ref[rows, qp * PAIR:(qp + 1) * PAIR])
                ps, dens = [], []
                for pos in range(2):
                    h = 2 * qp + pos
                    s_t = jnp.where(mask_t, s2[pos * tk:(pos + 1) * tk], NEG_INF)
                    m = jnp.max(s_t, axis=0, keepdims=True)
                    if sink is not None:
                        m = jnp.maximum(m, sink_ref[0, h])
                    p_t = jnp.exp(s_t - m)
                    den = jnp.sum(p_t, axis=0, keepdims=True)
                    if sink is not None:
                        den = den + jnp.exp(sink_ref[0, h] - m)
                    ps.append(p_t.astype(BF16))
                    dens.append(den)
                    lse_rows.append(m + jnp.log(den))
                both = _dot_tn(vst, jnp.concatenate(ps, axis=0))
                o_t.append(both / jnp.where(row_hi, dens[1], dens[0]))
            o_ref[rows, :] = jnp.concatenate(o_t, axis=0).T
            lse_ref[:, rows] = jnp.concatenate(lse_rows, axis=0)

    in_specs = [pl.BlockSpec((None, t, hq * HEAD_DIM), lambda s, i: (s, i, qcol))] + kv_specs
    args = [qkv] * 7
    if sink is not None:
        in_specs = [pl.BlockSpec(memory_space=pltpu.SMEM)] + in_specs
        args = [sink] + args
    (o, lse), couts = _pcall(
        body, name=name, grid=(nseq, n), in_specs=in_specs,
        out_specs=[pl.BlockSpec((None, t, hq * HEAD_DIM), lambda s, i: (s, i, 0)),
                   pl.BlockSpec((None, hq, t), lambda s, i: (s, 0, i))],
        out_shape=[jax.ShapeDtypeStruct((nseq, seq_len, hq * HEAD_DIM), F32),
                   jax.ShapeDtypeStruct((nseq, hq, seq_len), F32)],
        args=args, dims=("parallel", "parallel"), comm=comm)
    return o, lse, couts


def _swa_bwd_p(qkv, do, lse, delta, cs, e_mat, *, qcol, kcol, vcol, hq, hkv, w, tq, sub, sink, name, comm=None):
    nseq, seq_len, _ = qkv.shape
    t = tq * sub
    n = seq_len // t
    rep = hq // hkv
    qw, kw = hq * HEAD_DIM, hkv * HEAD_DIM
    tk = tq + 2 * w
    kv_specs, cur = _halo_kv_specs(t, w, hkv, n, seq_len, kcol, vcol)

    def body(*refs):
        if sink is not None:
            sink_ref, refs = refs[0], refs[1:]
        (q_ref, kp_ref, kc_ref, kn_ref, vp_ref, vc_ref, vn_ref, do_ref, lse_ref, dl_ref,
         cs_c, cs_p, e_ref) = refs[:13]
        outs = refs[13:]
        dq_ref, dk_ref, dv_ref = outs[:3]
        dsink_ref = outs[3] if sink is not None else None
        dk_win, dv_win = outs[-2:]
        dk_acc, dv_acc = outs[-4:-2] if n > 1 else (None, None)
        s_id = pl.program_id(0)
        i = pl.program_id(1)
        slot_p, slot_c, slot_n = (i + 2) % 3, i % 3, (i + 1) % 3

        if sink is not None:
            @pl.when((s_id == 0) & (i == 0))
            def _():
                dsink_ref[...] = jnp.zeros_like(dsink_ref)

        @pl.when(i < n)
        def _():
            dk_win[...] = jnp.zeros_like(dk_win)
            dv_win[...] = jnp.zeros_like(dv_win)
            kfull, vfull = [], []
            for a in range(hkv // 2):
                ls = slice(a * PAIR, (a + 1) * PAIR)
                kfull.append(jnp.concatenate([kp_ref[:, ls], kc_ref[:, ls], kn_ref[:, ls]], axis=0) * 0.125)
                vfull.append(jnp.concatenate([vp_ref[:, ls], vc_ref[:, ls], vn_ref[:, ls]], axis=0))
            for jj in range(sub):
                rows = slice(jj * tq, (jj + 1) * tq)
                krows = slice(jj * tq, jj * tq + tk)
                mask_t = _band_mask_t(i * t + jj * tq, tq, w, seq_len)
                dq_t = []
                dk2 = [None] * (hkv // 2)
                dv2 = [None] * (hkv // 2)
                for qp in range(hq // 2):
                    kst, vst, a = _pair_kv(kfull, vfull, qp, rep, krows)
                    q2 = q_ref[rows, qp * PAIR:(qp + 1) * PAIR]
                    do2 = do_ref[rows, qp * PAIR:(qp + 1) * PAIR]
                    s2 = _dot_nt(kst, q2)
                    dp2 = _dot_nt(vst, do2)
                    ds, ps, q_at, do_at = [], [], [], []
                    for pos in range(2):
                        h = 2 * qp + pos
                        e = (h // rep) % 2
                        half = slice(pos * tk, (pos + 1) * tk)
                        lse_h = lse_ref[h:h + 1, rows]
                        dl_h = dl_ref[h:h + 1, rows]
                        p_t = jnp.exp(jnp.where(mask_t, s2[half], NEG_INF) - lse_h)
                        ds.append((p_t * (dp2[half] - dl_h)).astype(BF16))
                        ps.append(p_t.astype(BF16))
                        q_at.append(_place_head(q2, pos, e) * 0.125)
                        do_at.append(_place_head(do2, pos, e))
                        if sink is not None:
                            ds_sink = -jnp.sum(jnp.exp(sink_ref[0, h] - lse_h) * dl_h)
                            dsink_ref[h:h + 1, :] += jnp.full((1, LANES), ds_sink, F32)
                    dq_t.append(_rope_rows(_dot_tn(kst, jnp.concatenate(ds, axis=0)),
                                           cs_c[0:ROT_DIM // 2, rows], cs_c[ROT_DIM // 2:ROT_DIM, rows], -1.0))
                    dk_part = _dot(jnp.concatenate(ds, axis=1), jnp.concatenate(q_at, axis=0))
                    dv_part = _dot(jnp.concatenate(ps, axis=1), jnp.concatenate(do_at, axis=0))
                    dk2[a] = dk_part if dk2[a] is None else dk2[a] + dk_part
                    dv2[a] = dv_part if dv2[a] is None else dv2[a] + dv_part
                for a in range(hkv // 2):
                    ls = slice(a * PAIR, (a + 1) * PAIR)
                    dk_win[krows, ls] += dk2[a]
                    dv_win[krows, ls] += dv2[a]
                dq_ref[rows, :] = jnp.concatenate(dq_t, axis=0).T.astype(BF16)

            if n == 1:
                dk_ref[...] = _rope(dk_win[w:w + t, :], *_rope_tabs(cs_p[...], e_ref[...]), -1.0).astype(BF16)
                dv_ref[...] = dv_win[w:w + t, :].astype(BF16)
                return

            @pl.when(i > 0)
            def _():
                dk_acc[slot_p, t - w:, :] += dk_win[:w, :]
                dv_acc[slot_p, t - w:, :] += dv_win[:w, :]

            @pl.when(i == 0)
            def _():
                dk_acc[slot_c] = dk_win[w:w + t, :]
                dv_acc[slot_c] = dv_win[w:w + t, :]

            @pl.when(i > 0)
            def _():
                dk_acc[slot_c] += dk_win[w:w + t, :]
                dv_acc[slot_c] += dv_win[w:w + t, :]

            dk_acc[slot_n] = jnp.zeros((t, kw), F32)
            dv_acc[slot_n] = jnp.zeros((t, kw), F32)
            dk_acc[slot_n, :w, :] = dk_win[w + t:, :]
            dv_acc[slot_n, :w, :] = dv_win[w + t:, :]

        if n > 1:
            @pl.when(i >= 1)
            def _():
                dk_ref[...] = _rope(dk_acc[slot_p], *_rope_tabs(cs_p[...], e_ref[...]), -1.0).astype(BF16)
                dv_ref[...] = dv_acc[slot_p].astype(BF16)

    row_c = lambda width: pl.BlockSpec((None, t, width), lambda s, i: (s, cur(s, i), 0))
    row_p = lambda width: pl.BlockSpec((None, t, width), lambda s, i: (s, jnp.maximum(i - 1, 0), 0))
    stat = pl.BlockSpec((None, hq, t), lambda s, i: (s, 0, cur(s, i)))
    cs_rows = pl.BlockSpec((None, ROT_DIM, t), lambda s, i: (s, 0, cur(s, i)))
    in_specs = ([pl.BlockSpec((None, t, qw), lambda s, i: (s, cur(s, i), qcol))] + kv_specs
                + [row_c(qw), stat, stat, cs_rows, row_p(ROT_DIM),
                   pl.BlockSpec((ROT_DIM, 3 * LANES), lambda s, i: (0, 0))])
    args = [qkv] * 7 + [do, lse, delta, cs.transpose(0, 2, 1), cs, e_mat]
    out_specs = [row_c(qw), row_p(kw), row_p(kw)]
    out_shape = [jax.ShapeDtypeStruct((nseq, seq_len, qw), BF16),
                 jax.ShapeDtypeStruct((nseq, seq_len, kw), BF16),
                 jax.ShapeDtypeStruct((nseq, seq_len, kw), BF16)]
    if sink is not None:
        in_specs = [pl.BlockSpec(memory_space=pltpu.SMEM)] + in_specs
        args = [sink] + args
        out_specs.append(pl.BlockSpec((8, LANES), lambda s, i: (0, 0)))
        out_shape.append(jax.ShapeDtypeStruct((8, LANES), F32))
    return _pcall(
        body, name=name, grid=(nseq, n + 1 if n > 1 else 1), in_specs=in_specs, out_specs=out_specs,
        out_shape=out_shape,
        scratch_shapes=([pltpu.VMEM((3, t, kw), F32), pltpu.VMEM((3, t, kw), F32)] if n > 1 else [])
        + [pltpu.VMEM((t + 2 * w, kw), F32), pltpu.VMEM((t + 2 * w, kw), F32)], args=args,
        dims=("arbitrary", "arbitrary"), comm=comm)


def _rms_parts(o, g):
    ms = jnp.mean(o * o, axis=-1, keepdims=True) + LN_EPS
    rinv = lax.rsqrt(ms)
    return o * rinv * g, rinv


def _from_subsequences(ref, scr, dil, t):
    slabs = ref.shape[-1] // LANES
    if dil == 1:
        return ref[0].astype(F32)
    for c in range(dil):
        for sl in range(slabs):
            scr[sl, pl.ds(c, t // dil, stride=dil), :] = ref[c, :, sl * LANES:(sl + 1) * LANES].astype(F32)
    return jnp.concatenate([scr[sl] for sl in range(slabs)], axis=1)


def _to_subsequences(val, ref, scr, dil, t):
    slabs = val.shape[-1] // LANES
    if dil == 1:
        ref[0] = val.astype(ref.dtype)
        return
    for sl in range(slabs):
        scr[sl] = val[:, sl * LANES:(sl + 1) * LANES]
    for c in range(dil):
        for sl in range(slabs):
            ref[c, :, sl * LANES:(sl + 1) * LANES] = scr[sl, pl.ds(c, t // dil, stride=dil), :].astype(ref.dtype)


def _combine_fwd(out_a, o_g, lse_g, g_win, g_dil, w_mix_b, x, ln_in_g, ln_in_b, ln1_g, ln1_b, *, t, comm=None):
    s = out_a.shape[1]
    wd = DIL_SLOTS * HEAD_DIM

    def body(oa_ref, o0, o1, o2, l0, l1, l2, gw_ref, gd_ref, w_ref, x_ref, g0, b0, g1, b1,
             mixed_ref, ob_ref, lt_ref, r1_ref, h1_ref, scr):
        ls = [l0[...], l1[...], l2[...]]
        mx = jnp.maximum(jnp.maximum(ls[0], ls[1]), ls[2])
        ws = [jnp.exp(l - mx) for l in ls]
        tot = ws[0] + ws[1] + ws[2]
        lt_ref[...] = mx + jnp.log(tot)
        ws = [x / tot for x in ws]
        og = [_from_subsequences(o_ref, scr.at[gi], dil, t)
              for gi, (o_ref, dil) in enumerate(zip((o0, o1, o2), DILATIONS))]
        parts = []
        for h in range(DIL_SLOTS):
            hs = slice(h * HEAD_DIM, (h + 1) * HEAD_DIM)
            parts.append(ws[0][:, h:h + 1] * og[0][:, hs] + ws[1][:, h:h + 1] * og[1][:, hs]
                         + ws[2][:, h:h + 1] * og[2][:, hs])
        ob = jnp.concatenate(parts, axis=1)
        ob_ref[...] = ob
        na, _ = _rms_parts(oa_ref[...], gw_ref[...])
        nb, _ = _rms_parts(ob, gd_ref[...])
        mixed = jnp.concatenate([na.astype(BF16), nb.astype(BF16)], axis=1)
        mixed_ref[...] = mixed
        h0 = _ln(x_ref[...], g0[...], b0[...])
        r1 = ALPHA * h0 + _dot(mixed, w_ref[...])
        r1_ref[...] = r1
        h1_ref[...] = _ln(r1, g1[...], b1[...]).astype(BF16)

    half = pl.BlockSpec((t, wd), lambda i: (i, 0))
    full = pl.BlockSpec((t, D_MODEL), lambda i: (i, 0))
    lanes = pl.BlockSpec((t, LANES), lambda i: (i, 0))
    grow = pl.BlockSpec((1, wd), lambda i: (0, 0))
    row = pl.BlockSpec((1, D_MODEL), lambda i: (0, 0))
    subseq = [pl.BlockSpec((dil, t // dil, wd), lambda i: (0, i, 0)) for dil in DILATIONS]
    return _pcall(
        body, name="combine_fwd", grid=(s // t,),
        in_specs=[pl.BlockSpec((None, t, wd), lambda i: (0, i, 0))] + subseq
        + [lanes, lanes, lanes, grow, grow, pl.BlockSpec((D_MODEL, D_MODEL), lambda i: (0, 0)), full,
           row, row, row, row],
        out_specs=[full, half, lanes, full, full],
        out_shape=[jax.ShapeDtypeStruct((s, D_MODEL), BF16), jax.ShapeDtypeStruct((s, wd), F32),
                   jax.ShapeDtypeStruct((s, LANES), F32), jax.ShapeDtypeStruct((s, D_MODEL), F32),
                   jax.ShapeDtypeStruct((s, D_MODEL), BF16)],
        scratch_shapes=[pltpu.VMEM((len(DILATIONS), wd // LANES, t, LANES), F32)],
        args=[out_a, *o_g, *lse_g, g_win, g_dil, w_mix_b, x, ln_in_g, ln_in_b, ln1_g, ln1_b], dims=("parallel",),
        comm=comm)


def _combine_bwd(dr1b, w_mix_b, out_a, out_b, g_win, g_dil, *, t):
    s = out_b.shape[0]
    wd = DIL_SLOTS * HEAD_DIM

    def body(dr_ref, w_ref, oa_ref, ob_ref, gw_ref, gd_ref, doa_ref, dob0, dob1, dob2, dla_ref, dlb_ref, st_ref,
             scr):
        i = pl.program_id(0)
        dm = _dot_nt(dr_ref[...], w_ref[...])

        @pl.when(i == 0)
        def _():
            st_ref[...] = jnp.zeros_like(st_ref)

        lane = lax.broadcasted_iota(jnp.int32, (t, LANES), 1)
        for idx, (o_ref, g_ref, dl_ref) in enumerate(((oa_ref, gw_ref, dla_ref), (ob_ref, gd_ref, dlb_ref))):
            o = o_ref[...]
            dn = dm[:, idx * wd:(idx + 1) * wd]
            _, rinv = _rms_parts(o, g_ref[...])
            wv = dn * g_ref[...]
            do = rinv * wv - o * (rinv * rinv * rinv) * jnp.mean(wv * o, axis=-1, keepdims=True)
            st_ref[idx:idx + 1, :] += jnp.sum(dn * o * rinv, axis=0, keepdims=True)
            if idx == 0:
                doa_ref[...] = do.astype(BF16)
            else:
                for do_ref, dil in zip((dob0, dob1, dob2), DILATIONS):
                    _to_subsequences(do, do_ref, scr, dil, t)
            prod = do * o
            acc = jnp.zeros((t, LANES), F32)
            for h in range(DIL_SLOTS):
                hs = slice(h * HEAD_DIM, (h + 1) * HEAD_DIM)
                acc = jnp.where(lane == h, jnp.sum(prod[:, hs], axis=1, keepdims=True), acc)
            dl_ref[...] = acc

    half = pl.BlockSpec((t, wd), lambda i: (i, 0))
    lanes = pl.BlockSpec((t, LANES), lambda i: (i, 0))
    grow = pl.BlockSpec((1, wd), lambda i: (0, 0))
    a_spec = pl.BlockSpec((None, t, wd), lambda i: (0, i, 0))
    subseq = [pl.BlockSpec((dil, t // dil, wd), lambda i: (0, i, 0)) for dil in DILATIONS]
    doa, dob0, dob1, dob2, dla, dlb, st = pl.pallas_call(
        body, name="combine_bwd", grid=(s // t,),
        in_specs=[pl.BlockSpec((t, D_MODEL), lambda i: (i, 0)), pl.BlockSpec((D_MODEL, D_MODEL), lambda i: (0, 0)),
                  a_spec, half, grow, grow],
        out_specs=[a_spec] + subseq + [lanes, lanes, pl.BlockSpec((8, wd), lambda i: (0, 0))],
        out_shape=[jax.ShapeDtypeStruct((1, s, wd), BF16)]
        + [jax.ShapeDtypeStruct((dil, s // dil, wd), BF16) for dil in DILATIONS]
        + [jax.ShapeDtypeStruct((s, LANES), F32), jax.ShapeDtypeStruct((s, LANES), F32),
           jax.ShapeDtypeStruct((8, wd), F32)],
        scratch_shapes=[pltpu.VMEM((wd // LANES, t, LANES), F32)],
        compiler_params=_cparams(dimension_semantics=("arbitrary",)),
    )(dr1b, w_mix_b, out_a, out_b, g_win, g_dil)
    return doa, [dob0, dob1, dob2], dla, dlb, st


def _assemble_dz(dqa, dka, dva, dqs, dks, dvs, *, t):
    s = dqa.shape[1]
    wd = DIL_SLOTS * HEAD_DIM

    def body(*refs):
        a_refs, g_refs, o_ref, scr = refs[:3], refs[3:12], refs[12], refs[13]
        col = 0
        for r in a_refs:
            o_ref[:, col:col + r.shape[-1]] = r[...]
            col += r.shape[-1]
        for part in range(3):
            for gi, dil in enumerate(DILATIONS):
                val = _from_subsequences(g_refs[3 * part + gi], scr, dil, t)
                o_ref[:, col:col + wd] = val.astype(BF16)
                col += wd

    a_specs = [pl.BlockSpec((None, t, a.shape[-1]), lambda i: (0, i, 0)) for a in (dqa, dka, dva)]
    g_specs = [pl.BlockSpec((dil, t // dil, wd), lambda i: (0, i, 0)) for _ in range(3) for dil in DILATIONS]
    return pl.pallas_call(
        body, name="assemble_dz", grid=(s // t,), in_specs=a_specs + g_specs,
        out_specs=pl.BlockSpec((t, IN_WIDTH), lambda i: (i, 0)),
        out_shape=jax.ShapeDtypeStruct((s, IN_WIDTH), BF16),
        scratch_shapes=[pltpu.VMEM((wd // LANES, t, LANES), F32)],
        compiler_params=_cparams(dimension_semantics=("parallel",)),
    )(dqa, dka, dva, *dqs, *dks, *dvs)


def _mem_fwd(mem, g, b, wk_b, wv_b):
    ml = mem.shape[0]

    def body(mem_ref, g_ref, b_ref, wk_ref, wv_ref, mn_ref, kx_ref, vx_ref):
        mn = _ln(mem_ref[...], g_ref[...], b_ref[...]).astype(BF16)
        mn_ref[...] = mn
        kx_ref[...] = _dot(mn, wk_ref[...]).astype(BF16)
        vx_ref[...] = _dot(mn, wv_ref[...]).astype(BF16)

    sh = jax.ShapeDtypeStruct((ml, D_MODEL), BF16)
    return pl.pallas_call(body, name="mem_fwd", out_shape=[sh, sh, sh], compiler_params=_cparams())(
        mem, g, b, wk_b, wv_b)


def _mem_bwd(dkx, dvx, mem, g, b, wk_b, wv_b):
    def body(dk_ref, dv_ref, mem_ref, g_ref, b_ref, wk_ref, wv_ref, dwk_ref, dwv_ref, st_ref):
        mem_v = mem_ref[...]
        mn = _ln(mem_v, g_ref[...], b_ref[...]).astype(BF16)
        dkb = dk_ref[...].astype(BF16)
        dvb = dv_ref[...].astype(BF16)
        dwk_ref[...] = _dot_tn(mn, dkb)
        dwv_ref[...] = _dot_tn(mn, dvb)
        dmn = _dot_nt(dkb, wk_ref[...]) + _dot_nt(dvb, wv_ref[...])
        _, dg, db = _ln_bwd_math(dmn, mem_v, g_ref[...])
        st_ref[...] = jnp.zeros_like(st_ref)
        st_ref[0:1, :] = dg
        st_ref[1:2, :] = db

    sw = jax.ShapeDtypeStruct((D_MODEL, D_MODEL), F32)
    return pl.pallas_call(body, name="mem_bwd", out_shape=[sw, sw, jax.ShapeDtypeStruct((8, D_MODEL), F32)],
                          compiler_params=_cparams())(dkx, dvx, mem, g, b, wk_b, wv_b)


def _xattn_fwd(h1b, r1, kx, vx, wq_b, wo_b, ln1_g, ln1_b, ln2_g, ln2_b, *, t, comm=None):
    s = h1b.shape[0]
    scale = X_HEAD_DIM ** -0.5

    def body(h_ref, r1_ref, kx_ref, vx_ref, wq_ref, wo_ref, g1, b1, g2, b2, r2_ref, h2_ref, qx_ref, ox_ref, lse_ref):
        qxb = _dot(h_ref[...], wq_ref[...]).astype(BF16)
        qx_ref[...] = qxb
        lane = lax.broadcasted_iota(jnp.int32, (t, LANES), 1)
        lse_acc = jnp.zeros((t, LANES), F32)
        parts = []
        for h in range(X_HEADS):
            hs = slice(h * X_HEAD_DIM, (h + 1) * X_HEAD_DIM)
            sc = _dot_nt(qxb[:, hs] * scale, kx_ref[:, hs])
            m = jnp.max(sc, axis=1, keepdims=True)
            p = jnp.exp(sc - m)
            den = jnp.sum(p, axis=1, keepdims=True)
            parts.append(_dot(p.astype(BF16), vx_ref[:, hs]) / den)
            lse_acc = jnp.where(lane == h, m + jnp.log(den), lse_acc)
        lse_ref[...] = lse_acc
        oxb = jnp.concatenate(parts, axis=1).astype(BF16)
        ox_ref[...] = oxb
        h1 = _ln(r1_ref[...], g1[...], b1[...])
        r2 = ALPHA * h1 + _dot(oxb, wo_ref[...])
        r2_ref[...] = r2
        h2_ref[...] = _ln(r2, g2[...], b2[...]).astype(BF16)

    tile = pl.BlockSpec((t, D_MODEL), lambda i: (i, 0))
    row = pl.BlockSpec((1, D_MODEL), lambda i: (0, 0))
    full = lambda r: pl.BlockSpec((r, D_MODEL), lambda i: (0, 0))
    ml = kx.shape[0]
    bsh = jax.ShapeDtypeStruct((s, D_MODEL), BF16)
    return _pcall(
        body, name="xattn_fwd", grid=(s // t,),
        in_specs=[tile, tile, full(ml), full(ml), full(D_MODEL), full(D_MODEL), row, row, row, row],
        out_specs=[tile, tile, tile, tile, pl.BlockSpec((t, LANES), lambda i: (i, 0))],
        out_shape=[jax.ShapeDtypeStruct((s, D_MODEL), F32), bsh, bsh, bsh, jax.ShapeDtypeStruct((s, LANES), F32)],
        args=[h1b, r1, kx, vx, wq_b, wo_b, ln1_g, ln1_b, ln2_g, ln2_b], dims=("parallel",), comm=comm)


def _xattn_bwd(dr2, qxb, oxb, lse, kx, vx, wq_b, wo_b, r1, ln1_g, *, t, comm=None):
    s = dr2.shape[0]
    ml = kx.shape[0]
    scale = X_HEAD_DIM ** -0.5

    def body(dr2_ref, qx_ref, ox_ref, lse_ref, kx_ref, vx_ref, wq_ref, wo_ref, r1_ref, g1_ref,
             dr1_ref, dr1b_ref, dqx_ref, dkx_ref, dvx_ref, st_ref):
        i = pl.program_id(0)

        @pl.when(i == 0)
        def _():
            dkx_ref[...] = jnp.zeros_like(dkx_ref)
            dvx_ref[...] = jnp.zeros_like(dvx_ref)
            st_ref[...] = jnp.zeros_like(st_ref)

        dr2v = dr2_ref[...]
        dox = _dot_nt(dr2v.astype(BF16), wo_ref[...])
        parts = []
        for h in range(X_HEADS):
            hs = slice(h * X_HEAD_DIM, (h + 1) * X_HEAD_DIM)
            doh = dox[:, hs]
            dohb = doh.astype(BF16)
            dl = jnp.sum(doh * ox_ref[:, hs].astype(F32), axis=1, keepdims=True)
            qh = qx_ref[:, hs] * scale
            p = jnp.exp(_dot_nt(qh, kx_ref[:, hs]) - lse_ref[:, h:h + 1])
            dp = _dot_nt(dohb, vx_ref[:, hs])
            dsb = (p * (dp - dl)).astype(BF16)
            parts.append(_dot(dsb, kx_ref[:, hs]) * scale)
            dkx_ref[:, hs] += _dot_tn(dsb, qh)
            dvx_ref[:, hs] += _dot_tn(p.astype(BF16), dohb)
        dqxb = jnp.concatenate(parts, axis=1).astype(BF16)
        dqx_ref[...] = dqxb
        dh1 = _dot_nt(dqxb, wq_ref[...]) + ALPHA * dr2v
        dr1, dg, db = _ln_bwd_math(dh1, r1_ref[...], g1_ref[...])
        dr1_ref[...] = dr1
        dr1b_ref[...] = dr1.astype(BF16)
        st_ref[0:1, :] += dg
        st_ref[1:2, :] += db

    tile = pl.BlockSpec((t, D_MODEL), lambda i: (i, 0))
    full = lambda r: pl.BlockSpec((r, D_MODEL), lambda i: (0, 0))
    bsh = jax.ShapeDtypeStruct((s, D_MODEL), BF16)
    return _pcall(
        body, name="xattn_bwd", grid=(s // t,),
        in_specs=[tile, tile, tile, pl.BlockSpec((t, LANES), lambda i: (i, 0)), full(ml), full(ml),
                  full(D_MODEL), full(D_MODEL), tile, full(1)],
        out_specs=[tile, tile, tile, full(ml), full(ml), full(8)],
        out_shape=[jax.ShapeDtypeStruct((s, D_MODEL), F32), bsh, bsh,
                   jax.ShapeDtypeStruct((ml, D_MODEL), F32), jax.ShapeDtypeStruct((ml, D_MODEL), F32),
                   jax.ShapeDtypeStruct((8, D_MODEL), F32)],
        args=[dr2, qxb, oxb, lse, kx, vx, wq_b, wo_b, r1, ln1_g], dims=("arbitrary",), comm=comm)


def _halo_specs(t, s, width):
    tb8 = t // 8
    return [pl.BlockSpec((t, width), lambda i: (i, 0)),
            pl.BlockSpec((8, width), lambda i: (jnp.maximum(i * tb8 - 1, 0), 0)),
            pl.BlockSpec((8, width), lambda i: (jnp.minimum((i + 1) * tb8, s // 8 - 1), 0))]


def _halo_rows(i, n, prev_ref, next_ref):
    prev_row = jnp.where(i > 0, prev_ref[7:8, :], 0.0)
    next_row = jnp.where(i < n - 1, next_ref[0:1, :], 0.0)
    return prev_row, next_row


def _gelu_parts(gc):
    cdf = 0.5 * (1.0 + lax.erf(gc * (2.0 ** -0.5)))
    pdf = jnp.exp(-0.5 * gc * gc) * (1.0 / math.sqrt(2.0 * math.pi))
    return gc * cdf, cdf + gc * pdf


def _ffn_out(g, u, conv_w, conv_b, w_down_b, r2, target, ln2_g, ln2_b, ln3_g, ln3_b, *, t):
    s = r2.shape[0]
    n = s // t

    def body(g_ref, gp_ref, gn_ref, u_ref, cw_ref, cb_ref, w_ref, r2_ref, tg_ref, g2, b2, g3, b3,
             t_ref, dr_ref, drb_ref, st_ref):
        i = pl.program_id(0)

        @pl.when(i == 0)
        def _():
            st_ref[...] = jnp.zeros_like(st_ref)

        gv = g_ref[...]
        prev_row, next_row = _halo_rows(i, n, gp_ref, gn_ref)
        gm1, gp1 = _shift_rows(gv, prev_row, next_row)
        gc = gm1 * cw_ref[0:1, :] + gv * cw_ref[1:2, :] + gp1 * cw_ref[2:3, :] + cb_ref[...]
        act, _ = _gelu_parts(gc)
        tb = (act * u_ref[...]).astype(BF16)
        t_ref[...] = tb
        h2 = _ln(r2_ref[...], g2[...], b2[...])
        r3 = ALPHA * h2 + _dot(tb, w_ref[...])
        y = _ln(r3, g3[...], b3[...])
        err = y - tg_ref[...]
        loss = 0.5 * jnp.sum(jnp.mean(err * err, axis=-1, keepdims=True))
        dr, dg, db = _ln_bwd_math(err * (1.0 / D_MODEL), r3, g3[...])
        dr_ref[...] = dr
        drb_ref[...] = dr.astype(BF16)
        st_ref[0:1, :] += dg
        st_ref[1:2, :] += db
        st_ref[2:3, :] += jnp.full((1, D_MODEL), loss, F32)

    wide = pl.BlockSpec((t, D_FF), lambda i: (i, 0))
    tile = pl.BlockSpec((t, D_MODEL), lambda i: (i, 0))
    row = pl.BlockSpec((1, D_MODEL), lambda i: (0, 0))
    return pl.pallas_call(
        body, name="ffn_out", grid=(n,),
        in_specs=_halo_specs(t, s, D_FF) + [wide, pl.BlockSpec((3, D_FF), lambda i: (0, 0)),
                                            pl.BlockSpec((1, D_FF), lambda i: (0, 0)),
                                            pl.BlockSpec((D_FF, D_MODEL), lambda i: (0, 0)),
                                            tile, tile, row, row, row, row],
        out_specs=[wide, tile, tile, pl.BlockSpec((8, D_MODEL), lambda i: (0, 0))],
        out_shape=[jax.ShapeDtypeStruct((s, D_FF), BF16), jax.ShapeDtypeStruct((s, D_MODEL), F32),
                   jax.ShapeDtypeStruct((s, D_MODEL), BF16), jax.ShapeDtypeStruct((8, D_MODEL), F32)],
        compiler_params=_cparams(dimension_semantics=("arbitrary",)),
    )(g, g, g, u, conv_w, conv_b, w_down_b, r2, target, ln2_g, ln2_b, ln3_g, ln3_b)


def _dh2_ln2(dgc, conv_w, du, w_gate_b, w_up_b, dr3, r2, ln2_g, *, t, comm=None):
    s = dgc.shape[0]
    n = s // t

    def body(d_ref, dp_ref, dn_ref, cw_ref, du_ref, wg_ref, wu_ref, dr3_ref, r2_ref, g2, dg_ref, dr_ref, drb_ref,
             st_ref):
        i = pl.program_id(0)

        @pl.when(i == 0)
        def _():
            st_ref[...] = jnp.zeros_like(st_ref)

        dv = d_ref[...]
        prev_row, next_row = _halo_rows(i, n, dp_ref, dn_ref)
        dm1, dp1 = _shift_rows(dv, prev_row, next_row)
        dgb = (dp1 * cw_ref[0:1, :] + dv * cw_ref[1:2, :] + dm1 * cw_ref[2:3, :]).astype(BF16)
        dg_ref[...] = dgb
        dh2 = _dot(dgb, wg_ref[...]) + _dot(du_ref[...], wu_ref[...]) + ALPHA * dr3_ref[...]
        dr, dg, db = _ln_bwd_math(dh2, r2_ref[...], g2[...])
        dr_ref[...] = dr
        drb_ref[...] = dr.astype(BF16)
        st_ref[0:1, :] += dg
        st_ref[1:2, :] += db

    wide = pl.BlockSpec((t, D_FF), lambda i: (i, 0))
    tile = pl.BlockSpec((t, D_MODEL), lambda i: (i, 0))
    wfull = pl.BlockSpec((D_FF, D_MODEL), lambda i: (0, 0), pipeline_mode=pl.Buffered(1))
    return _pcall(
        body, name="dh2_ln2", grid=(n,),
        in_specs=_halo_specs(t, s, D_FF) + [pl.BlockSpec((3, D_FF), lambda i: (0, 0)), wide, wfull, wfull,
                                            tile, tile, pl.BlockSpec((1, D_MODEL), lambda i: (0, 0))],
        out_specs=[wide, tile, tile, pl.BlockSpec((8, D_MODEL), lambda i: (0, 0))],
        out_shape=[jax.ShapeDtypeStruct((s, D_FF), BF16), jax.ShapeDtypeStruct((s, D_MODEL), F32),
                   jax.ShapeDtypeStruct((s, D_MODEL), BF16), jax.ShapeDtypeStruct((8, D_MODEL), F32)],
        args=[dgc, dgc, dgc, conv_w, du, w_gate_b, w_up_b, dr3, r2, ln2_g], dims=("arbitrary",), comm=comm)


def _conv_bwd_a(dr3b, w_down_b, g, u, conv_w, conv_b, *, t):
    s = g.shape[0]
    n = s // t

    def body(d_ref, w_ref, g_ref, gp_ref, gn_ref, u_ref, cw_ref, cb_ref, du_ref, dgc_ref, st_ref):
        i = pl.program_id(0)

        @pl.when(i == 0)
        def _():
            st_ref[...] = jnp.zeros_like(st_ref)

        dt = _dot_nt(d_ref[...], w_ref[...])
        gv = g_ref[...]
        prev_row, next_row = _halo_rows(i, n, gp_ref, gn_ref)
        gm1, gp1 = _shift_rows(gv, prev_row, next_row)
        gc = gm1 * cw_ref[0:1, :] + gv * cw_ref[1:2, :] + gp1 * cw_ref[2:3, :] + cb_ref[...]
        act, dact = _gelu_parts(gc)
        du_ref[...] = (dt * act).astype(BF16)
        dgc = dt * u_ref[...] * dact
        dgc_ref[...] = dgc
        st_ref[0:1, :] += jnp.sum(gm1 * dgc, axis=0, keepdims=True)
        st_ref[1:2, :] += jnp.sum(gv * dgc, axis=0, keepdims=True)
        st_ref[2:3, :] += jnp.sum(gp1 * dgc, axis=0, keepdims=True)
        st_ref[3:4, :] += jnp.sum(dgc, axis=0, keepdims=True)

    tile = pl.BlockSpec((t, D_FF), lambda i: (i, 0))
    return pl.pallas_call(
        body, name="conv_bwd_a", grid=(n,),
        in_specs=[pl.BlockSpec((t, D_MODEL), lambda i: (i, 0)), pl.BlockSpec((D_FF, D_MODEL), lambda i: (0, 0))]
        + _halo_specs(t, s, D_FF) + [tile, pl.BlockSpec((3, D_FF), lambda i: (0, 0)),
                                     pl.BlockSpec((1, D_FF), lambda i: (0, 0))],
        out_specs=[tile, tile, pl.BlockSpec((8, D_FF), lambda i: (0, 0))],
        out_shape=[jax.ShapeDtypeStruct((s, D_FF), BF16), jax.ShapeDtypeStruct((s, D_FF), F32),
                   jax.ShapeDtypeStruct((8, D_FF), F32)],
        compiler_params=_cparams(dimension_semantics=("arbitrary",)),
    )(dr3b, w_down_b, g, g, g, u, conv_w, conv_b)


def _to_residue(a, dil):
    s, w = a.shape
    return a.reshape(s // dil, dil, w).transpose(1, 0, 2)


def _stats_to_lanes(rows):
    dil, hq, l = rows.shape
    return jnp.pad(rows.transpose(2, 0, 1).reshape(dil * l, hq), ((0, 0), (0, LANES - hq)))


def _stats_to_rows(lanes, dil):
    s = lanes.shape[0]
    return lanes[:, :DIL_SLOTS].reshape(s // dil, dil, DIL_SLOTS).transpose(1, 2, 0)


def _rope_angles(positions):
    inv_freq = ROPE_THETA ** (-jnp.arange(0, ROT_DIM, 2, dtype=F32) / ROT_DIM)
    ang = positions.astype(F32)[:, None] * inv_freq
    return jnp.concatenate([jnp.cos(ang), jnp.sin(ang)], axis=1)


class _NoPlan:
    def gather(self, stage):
        return None

    def gathered(self, stage, couts, wb):
        pass

    def exchange(self, stage, grads):
        return None

    def exchanged(self, stage, couts):
        pass


def _local_step(x, mem, positions, target, wb, sp, plan=None, *, t_row=256, t_mm=512, tq_a=128, tq_b=128,
                sub_a=4, sub_b=4):
    s = x.shape[0]
    plan = plan or _NoPlan()
    cs = _rope_angles(positions)
    e_mat = _rope_select_matrix()

    h0b, couts = _ln_in_fwd(x, sp["ln_in_g"], sp["ln_in_b"], t=t_mm, comm=plan.gather("ln_in"))
    plan.gathered("ln_in", couts, wb)
    sp = dict(sp, conv_w=wb.get("conv_w", sp.get("conv_w")))
    (za, *zb), couts = _proj_all(h0b, wb["w_in"], cs, e_mat, t=min(2 * t_mm, s), comm=plan.gather("proj"))
    plan.gathered("proj", couts, wb)
    sub_a = max(1, min(sub_a, s // tq_a))
    subs_b = [max(1, min(sub_b, s // dil // tq_b)) for dil in DILATIONS]
    subs_b_bwd = [s // dil // tq_b if s // dil <= ONE_TILE_ROWS else sb for dil, sb in zip(DILATIONS, subs_b)]
    out_a, lse_a, couts = _swa_fwd_p(za, qcol=0, kcol=4, vcol=5, hq=WIN_Q_HEADS, hkv=WIN_KV_HEADS, w=WIN_HALF,
                                     tq=tq_a, sub=sub_a, sink=sp["attn_sink"], name="attn_a_fwd",
                                     comm=plan.gather("attn_a"))
    plan.gathered("attn_a", couts, wb)
    o_g, lse_g = [], []
    for gi in range(3):
        o, l, couts = _swa_fwd_p(zb[gi], qcol=0, kcol=1, vcol=2, hq=DIL_SLOTS, hkv=DIL_SLOTS, w=DIL_HALF, tq=tq_b,
                                 sub=subs_b[gi], sink=None, name=f"attn_b{gi}_fwd",
                                 comm=plan.gather(f"attn_b{gi}"))
        plan.gathered(f"attn_b{gi}", couts, wb)
        o_g.append(o)
        lse_g.append(_stats_to_lanes(l))
    (mixed_b, out_b, lse_b, r1, h1b), couts = _combine_fwd(
        out_a, o_g, lse_g, sp["g_win"], sp["g_dil"], wb["w_mix_out"], x, sp["ln_in_g"], sp["ln_in_b"],
        sp["ln1_g"], sp["ln1_b"], t=t_mm, comm=plan.gather("combine"))
    plan.gathered("combine", couts, wb)
    mem_nb, kx, vx = _mem_fwd(mem, sp["mem_ln_g"], sp["mem_ln_b"], wb["w_xk"], wb["w_xv"])
    (r2, h2b, qxb, oxb, lse_x), couts = _xattn_fwd(
        h1b, r1, kx, vx, wb["w_xq"], wb["w_xo"], sp["ln1_g"], sp["ln1_b"], sp["ln2_g"], sp["ln2_b"], t=t_mm,
        comm=plan.gather("xattn"))
    plan.gathered("xattn", couts, wb)
    g = _mm(h2b, wb["w_gate"], mode="nt", out_dtype=F32, tm=t_mm, tn=D_FF, name="ff_gate")
    u = _mm(h2b, wb["w_up"], mode="nt", out_dtype=F32, tm=t_mm, tn=D_FF, name="ff_up")
    tb, dr3, dr3b, st3 = _ffn_out(g, u, sp["conv_w"], sp["conv_b"], wb["w_down"], r2, target, sp["ln2_g"],
                                  sp["ln2_b"], sp["ln3_g"], sp["ln3_b"], t=t_row)

    grads = {}
    du, dgc, st_conv = _conv_bwd_a(dr3b, wb["w_down"], g, u, sp["conv_w"], sp["conv_b"], t=t_row)
    tk = min(2048, s)
    grads["w_down"] = _mm(tb, dr3b, mode="tn", out_dtype=BF16, tm=D_FF // 2, tn=D_MODEL, tk=tk, name="dw_down")
    grads["w_up"] = _mm(du, h2b, mode="tn", out_dtype=BF16, tm=D_FF // 2, tn=D_MODEL, tk=tk, name="dw_up")
    (dg, dr2, dr2b, st2), couts = _dh2_ln2(dgc, sp["conv_w"], du, wb["w_gate"], wb["w_up"], dr3, r2, sp["ln2_g"],
                                           t=t_mm, comm=plan.exchange("dh2", grads))
    plan.exchanged("dh2", couts)
    grads["w_gate"] = _mm(dg, h2b, mode="tn", out_dtype=BF16, tm=D_FF // 2, tn=D_MODEL, tk=tk, name="dw_gate")

    (dr1, dr1b, dqxb, dkx, dvx, st1), couts = _xattn_bwd(
        dr2, qxb, oxb, lse_x, kx, vx, wb["w_xq"], wb["w_xo"], r1, sp["ln1_g"], t=t_mm,
        comm=plan.exchange("xattn", grads))
    plan.exchanged("xattn", couts)
    grads["w_xo"] = _mm(oxb, dr2b, mode="tn", out_dtype=BF16, tm=D_MODEL, tn=D_MODEL, tk=tk, name="dw_xo")
    grads["w_xq"] = _mm(h1b, dqxb, mode="tn", out_dtype=BF16, tm=D_MODEL, tn=D_MODEL, tk=tk, name="dw_xq")
    grads["w_xk"], grads["w_xv"], st_mem = _mem_bwd(dkx, dvx, mem, sp["mem_ln_g"], sp["mem_ln_b"],
                                                    wb["w_xk"], wb["w_xv"])

    grads["w_mix_out"] = _mm(mixed_b, dr1b, mode="tn", out_dtype=BF16, tm=D_MODEL, tn=D_MODEL, tk=tk,
                             name="dw_mix")
    do_a, do_b, dl_a, dl_b, st_mix = _combine_bwd(dr1b, wb["w_mix_out"], out_a, out_b, sp["g_win"], sp["g_dil"],
                                                  t=t_mm)
    (dqa, dka, dva, dsink), couts = _swa_bwd_p(
        za, do_a, lse_a, _stats_to_rows(dl_a, 1), cs[None], e_mat, qcol=0, kcol=4, vcol=5, hq=WIN_Q_HEADS,
        hkv=WIN_KV_HEADS, w=WIN_HALF, tq=2 * tq_a, sub=max(1, sub_a // 2), sink=sp["attn_sink"], name="attn_a_bwd",
        comm=plan.exchange("attn_a", grads))
    plan.exchanged("attn_a", couts)
    dqs, dks, dvs = [], [], []
    for gi, dil in enumerate(DILATIONS):
        (dq, dk, dv), couts = _swa_bwd_p(
            zb[gi], do_b[gi], _stats_to_rows(lse_b, dil), _stats_to_rows(dl_b, dil),
            _to_residue(cs, dil), e_mat, qcol=0, kcol=1, vcol=2, hq=DIL_SLOTS, hkv=DIL_SLOTS, w=DIL_HALF, tq=tq_b,
            sub=subs_b_bwd[gi], sink=None, name=f"attn_b{gi}_bwd", comm=plan.exchange(f"attn_b{gi}", grads))
        plan.exchanged(f"attn_b{gi}", couts)
        dqs.append(dq)
        dks.append(dk)
        dvs.append(dv)
    dz = _assemble_dz(dqa, dka, dva, dqs, dks, dvs, t=t_mm)
    grads["w_in"] = _mm(dz, h0b, mode="tn", out_dtype=BF16, tm=IN_WIDTH // 7, tn=D_MODEL, tk=tk, name="dw_in")
    (grad_x, st0), couts = _dh0_ln_in(dz, wb["w_in"], dr1, x, sp["ln_in_g"], t=t_mm,
                                      comm=plan.exchange("dh0", grads))
    plan.exchanged("dh0", couts)

    small = {
        "loss": st3[2:3, 0:1],
        "ln_in_g": st0[0:1], "ln_in_b": st0[1:2],
        "attn_sink": dsink[:, 0].reshape(1, WIN_Q_HEADS),
        "g_win": st_mix[0:1], "g_dil": st_mix[1:2],
        "ln1_g": st1[0:1], "ln1_b": st1[1:2],
        "mem_ln_g": st_mem[0:1], "mem_ln_b": st_mem[1:2],
        "ln2_g": st2[0:1], "ln2_b": st2[1:2],
        "conv_w": st_conv[0:3], "conv_b": st_conv[3:4],
        "ln3_g": st3[0:1], "ln3_b": st3[1:2],
    }
    return grad_x, grads, small


class _SiblingSwap:
    def __init__(self, arrays):
        self.inputs = list(arrays)
        n = len(arrays)
        self.out_shape = [jax.ShapeDtypeStruct(a.shape, a.dtype) for a in arrays]
        self.scratch = [pltpu.SemaphoreType.DMA((n,)), pltpu.SemaphoreType.DMA((n,))]

    def _copies(self, src, dst, sems):
        send_sems, recv_sems = sems
        x, y, c, _ = _place()
        return [pltpu.make_async_remote_copy(
            src_ref=src[a], dst_ref=dst[a], send_sem=send_sems.at[a], recv_sem=recv_sems.at[a],
            device_id=(x, y, 1 - c), device_id_type=MESH_IDS) for a in range(len(src))]

    def start(self, src, dst, sems):
        for cp in self._copies(src, dst, sems):
            cp.start()

    def wait(self, src, dst, sems):
        copies = self._copies(src, dst, sems)
        for cp in copies:
            cp.wait_recv()
        for cp in copies:
            cp.wait_send()


class _Both:
    def __init__(self, first, second):
        self.parts = (first, second)
        self.inputs = first.inputs + second.inputs
        self.out_shape = first.out_shape + second.out_shape
        self.scratch = first.scratch + second.scratch

    def _split(self, src, dst, sems):
        a = self.parts[0]
        ni, no, ns = len(a.inputs), len(a.out_shape), len(a.scratch)
        return ((src[:ni], dst[:no], sems[:ns]), (src[ni:], dst[no:], sems[ns:]))

    def start(self, src, dst, sems):
        for part, args in zip(self.parts, self._split(src, dst, sems)):
            part.start(*args)

    def wait(self, src, dst, sems):
        for part, args in zip(self.parts, self._split(src, dst, sems)):
            part.wait(*args)


def _row_tile(rows, cols, itemsize=4, budget=1 << 20):
    best = None
    for t in range(16, rows + 1, 16):
        if rows % t == 0 and t * cols * itemsize <= budget:
            best = t
    return best or rows


def _sum_slots(stack, *, name):
    n, r, c = stack.shape
    t = _row_tile(r, c)

    def body(s_ref, o_ref):
        acc = s_ref[0].astype(F32)
        for q in range(1, n):
            acc = acc + s_ref[q].astype(F32)
        o_ref[...] = acc

    return pl.pallas_call(
        body, name=name, grid=(r // t,), in_specs=[pl.BlockSpec((n, t, c), lambda i: (0, i, 0))],
        out_specs=pl.BlockSpec((t, c), lambda i: (i, 0)), out_shape=jax.ShapeDtypeStruct((r, c), F32),
        compiler_params=_cparams(dimension_semantics=("parallel",)),
    )(stack)


def _adamw(w, m, v, p, q, *, name):
    r, c = w.shape
    t = _row_tile(r, c, budget=1 << 20)

    def total(ref):
        if len(ref.shape) == 2:
            return ref[...]
        acc = ref[0].astype(F32)
        for slot in range(1, ref.shape[0]):
            acc = acc + ref[slot].astype(F32)
        return acc

    def body(*refs):
        if q is None:
            w_ref, m_ref, v_ref, p_ref, g_ref, d_ref, nm_ref, nv_ref = refs
            g = total(p_ref)
        else:
            w_ref, m_ref, v_ref, p_ref, q_ref, g_ref, d_ref, nm_ref, nv_ref = refs
            g = total(p_ref) + total(q_ref)
        nm = ADAM_B1 * m_ref[...] + (1.0 - ADAM_B1) * g
        nv = ADAM_B2 * v_ref[...] + (1.0 - ADAM_B2) * (g * g)
        m_hat = nm / (1.0 - ADAM_B1 ** ADAM_STEP)
        v_hat = nv / (1.0 - ADAM_B2 ** ADAM_STEP)
        g_ref[...] = g
        d_ref[...] = -ADAM_LR * (m_hat / (jnp.sqrt(v_hat) + ADAM_EPS) + ADAM_WD * w_ref[...])
        nm_ref[...] = nm
        nv_ref[...] = nv

    tile = pl.BlockSpec((t, c), lambda i: (i, 0))
    args = [w, m, v, p] + ([] if q is None else [q])
    in_specs = [tile if a.ndim == 2 else pl.BlockSpec((a.shape[0], t, c), lambda i: (0, i, 0)) for a in args]
    sh = jax.ShapeDtypeStruct((r, c), F32)
    return pl.pallas_call(
        body, name=name, grid=(r // t,), in_specs=in_specs, out_specs=[tile] * 4, out_shape=[sh] * 4,
        compiler_params=_cparams(dimension_semantics=("parallel",)),
    )(*args)


BIG = ("w_in", "w_mix_out", "w_xq", "w_xk", "w_xv", "w_xo", "w_gate", "w_up", "w_down")
COL_SHARDED = ("w_in", "w_gate", "w_up")
WEIGHTS = ("ln_in_g", "ln_in_b", "w_in", "attn_sink", "g_win", "g_dil", "w_mix_out", "ln1_g", "ln1_b",
           "mem_ln_g", "mem_ln_b", "w_xq", "w_xk", "w_xv", "w_xo", "ln2_g", "ln2_b", "w_gate", "w_up",
           "conv_w", "conv_b", "w_down", "ln3_g", "ln3_b")
SMALL = tuple(k for k in WEIGHTS if k not in BIG)
PACK_COLS = 1024
CONV_SHARD = D_FF // N_CHIPS
CONV_WIDTH_ROWS = 3
SMALL_ROWS = 32


GATHER_STAGES = {"ln_in": ("w_in", "conv_w"), "proj": ("w_mix_out", "w_xq", "w_xk", "w_xv", "w_xo", "w_up"),
                 "combine": ("w_gate", "w_down")}
EXCHANGE_STAGES = {"dh2": ("w_down", "w_up"), "attn_a": ("w_gate", "w_xo", "w_xq"),
                   "attn_b0": ("w_xk", "w_xv", "w_mix_out"), "dh0": ("w_in",)}


def _full_weight(k, g4):
    return g4.reshape(N_CHIPS * g4.shape[1], g4.shape[2])


def _grad_parts(k, gk):
    gk = gk.astype(BF16)
    return gk.reshape(N_CHIPS, gk.shape[0] // N_CHIPS, gk.shape[1])


EARLY_SWAP_STAGE = "attn_b2"


class _Plan:
    def __init__(self, shards):
        self.shards = shards
        self.recv = {}
        self.chip_sums = {}
        self.sibling_sums = {}

    def gather(self, stage):
        names = GATHER_STAGES.get(stage)
        return _ChipGather([self.shards[k] for k in names]) if names else None

    def gathered(self, stage, couts, wb):
        for k, g4 in zip(GATHER_STAGES.get(stage, ()), couts):
            if k == "conv_w":
                taps = g4[:, :CONV_WIDTH_ROWS, :CONV_SHARD]
                wb[k] = taps.transpose(1, 0, 2).reshape(CONV_WIDTH_ROWS, D_FF)
            else:
                wb[k] = _full_weight(k, g4)

    def exchange(self, stage, grads):
        if stage == EARLY_SWAP_STAGE:
            self.early = [k for k in BIG if k in self.recv]
            for k in self.early:
                self.chip_sums[k] = self.recv[k]
            return _SiblingSwap([self.chip_sums[k] for k in self.early])
        names = EXCHANGE_STAGES.get(stage)
        return _ChipExchange([_grad_parts(k, grads[k]) for k in names]) if names else None

    def exchanged(self, stage, couts):
        if stage == EARLY_SWAP_STAGE:
            self.sibling_sums.update(zip(self.early, couts))
            return
        for k, r4 in zip(EXCHANGE_STAGES.get(stage, ()), couts):
            self.recv[k] = r4


def _pack_rows(a):
    r, n = a.shape
    per = -(-n // PACK_COLS)
    return jnp.pad(a, ((0, 0), (0, per * PACK_COLS - n))).reshape(r * per, PACK_COLS)


def _unpack_rows(p, r, n):
    per = -(-n // PACK_COLS)
    return p.reshape(r, per * PACK_COLS)[:, :n]


def _pack(pieces, rows_total):
    cat = jnp.concatenate([_pack_rows(a) for a in pieces], axis=0)
    return jnp.pad(cat, ((0, rows_total - cat.shape[0]), (0, 0)))


def _unpack(p, shapes):
    out, at = [], 0
    for r, n in shapes:
        per = -(-n // PACK_COLS)
        out.append(_unpack_rows(p[at:at + r * per], r, n))
        at += r * per
    return out


def kernel(x, mem, positions, ln_in_g, ln_in_b, w_in, attn_sink, g_win, g_dil, w_mix_out, ln1_g, ln1_b, mem_ln_g, mem_ln_b, w_xq, w_xk, w_xv, w_xo, ln2_g, ln2_b, w_gate, w_up, conv_w, conv_b, w_down, ln3_g, ln3_b, loss_target, m_ln_in_g, m_ln_in_b, m_w_in, m_attn_sink, m_g_win, m_g_dil, m_w_mix_out, m_ln1_g, m_ln1_b, m_mem_ln_g, m_mem_ln_b, m_w_xq, m_w_xk, m_w_xv, m_w_xo, m_ln2_g, m_ln2_b, m_w_gate, m_w_up, m_conv_w, m_conv_b, m_w_down, m_ln3_g, m_ln3_b, v_ln_in_g, v_ln_in_b, v_w_in, v_attn_sink, v_g_win, v_g_dil, v_w_mix_out, v_ln1_g, v_ln1_b, v_mem_ln_g, v_mem_ln_b, v_w_xq, v_w_xk, v_w_xv, v_w_xo, v_ln2_g, v_ln2_b, v_w_gate, v_w_up, v_conv_w, v_conv_b, v_w_down, v_ln3_g, v_ln3_b):
    given = dict(locals())
    shape_of = {k: given[k].shape for k in WEIGHTS}
    as2d = lambda k, a: a.reshape(-1, a.shape[-1]).T if k in COL_SHARDED else a.reshape(-1, a.shape[-1])
    w2 = {k: as2d(k, given[k]) for k in WEIGHTS}
    m2 = {k: as2d(k, given["m_" + k]) for k in WEIGHTS}
    v2 = {k: as2d(k, given["v_" + k]) for k in WEIGHTS}
    chip = 2 * lax.axis_index("x") + lax.axis_index("y")

    shards = {k: w2[k].astype(BF16) for k in BIG}
    shards["conv_w"] = jnp.pad(w2["conv_w"], ((0, 16 - CONV_WIDTH_ROWS), (0, PACK_COLS - CONV_SHARD)))
    plan = _Plan(shards)
    sp = {k: w2[k] for k in SMALL if k != "conv_w"}

    grad_x, grads, small = _local_step(x[0], mem[0], positions[0], loss_target[0], {}, sp, plan)

    small_keys = ("loss",) + SMALL
    small_shapes = [small[k].shape for k in small_keys]
    small_pack = _pack([small[k] for k in small_keys], SMALL_ROWS)
    late = [k for k in BIG if k not in plan.chip_sums]
    for k in late:
        plan.chip_sums[k] = _sum_slots(plan.recv[k], name=f"sum_chips_{k}")
    *late_sibling, small_all = _comm_only(
        _Both(_SiblingSwap([plan.chip_sums[k] for k in late]), _ChipExchange([], small_pack)), "swap_and_small")
    plan.sibling_sums.update(zip(late, late_sibling))
    chip_sums = [plan.chip_sums[k] for k in BIG]
    sibling_sums = [plan.sibling_sums[k] for k in BIG]
    small_sum = _sum_slots(small_all, name="sum_small")
    small_g = dict(zip(small_keys, _unpack(small_sum, small_shapes)))
    loss = small_g["loss"][0, 0]

    res = {}
    for k, p, q in zip(BIG, chip_sums, sibling_sums):
        res[k] = _adamw(w2[k], m2[k], v2[k], p, q, name=f"adamw_{k}")
    small_g["conv_w"] = lax.dynamic_slice_in_dim(small_g["conv_w"], chip * CONV_SHARD, CONV_SHARD, axis=1)
    adam_shapes = [w2[k].shape for k in SMALL]
    packs = [_pack([d[k] for k in SMALL], SMALL_ROWS) for d in (w2, m2, v2, small_g)]
    small_res = [_unpack(o, adam_shapes) for o in _adamw(*packs, None, name="adamw_small")]
    for i, k in enumerate(SMALL):
        res[k] = tuple(o[i] for o in small_res)

    outs = [loss, grad_x[None]]
    for slot in range(4):
        outs += [(res[k][slot].T if k in COL_SHARDED else res[k][slot]).reshape(shape_of[k]) for k in WEIGHTS]
    return tuple(outs)
```

```python
import functools
import math

import jax
import jax.numpy as jnp
from jax import lax
from jax.experimental import pallas as pl
from jax.experimental.pallas import tpu as pltpu

F32 = jnp.float32
BF16 = jnp.bfloat16

D_MODEL = 1024
HEAD_DIM = 64
WIN_Q_HEADS = 8
WIN_KV_HEADS = 2
WIN_HALF = 128
DIL_SLOTS = 8
DILATIONS = (1, 4, 16)
DIL_HALF = 64
ROT_DIM = 16
ROPE_THETA = 500000.0
X_HEADS = 4
X_HEAD_DIM = 256
D_FF = 2816
A_Q = 512
A_KV = 128
A_WIDTH = A_Q + 2 * A_KV
B_QKV = 1536
IN_WIDTH = 5376
ALPHA = 2.0 ** 0.25
LN_EPS = 1e-5
NEG_INF = -1e30
LANES = 128
N_CHIPS = 4
N_DEV = 8

ADAM_LR = 0.001
ADAM_B1 = 0.9
ADAM_B2 = 0.999
ADAM_EPS = 1e-08
ADAM_WD = 0.01
ADAM_STEP = 10

VMEM_LIMIT = 56 * 1024 * 1024


def _cparams(**kw):
    return pltpu.CompilerParams(vmem_limit_bytes=VMEM_LIMIT, **kw)


def _dot(a, b):
    return lax.dot_general(a, b, (((1,), (0,)), ((), ())), preferred_element_type=F32)


def _dot_nt(a, b):
    return lax.dot_general(a, b, (((1,), (1,)), ((), ())), preferred_element_type=F32)


def _dot_tn(a, b):
    return lax.dot_general(a, b, (((0,), (0,)), ((), ())), preferred_element_type=F32)


def _ln(x, g, b):
    mu = jnp.mean(x, axis=-1, keepdims=True)
    xc = x - mu
    var = jnp.mean(xc * xc, axis=-1, keepdims=True)
    return xc * lax.rsqrt(var + LN_EPS) * g + b


def _ln_bwd_math(dy, r, g):
    mu = jnp.mean(r, axis=-1, keepdims=True)
    xc = r - mu
    var = jnp.mean(xc * xc, axis=-1, keepdims=True)
    rstd = lax.rsqrt(var + LN_EPS)
    xhat = xc * rstd
    dxhat = dy * g
    m1 = jnp.mean(dxhat, axis=-1, keepdims=True)
    m2 = jnp.mean(dxhat * xhat, axis=-1, keepdims=True)
    dr = rstd * (dxhat - m1 - xhat * m2)
    return dr, jnp.sum(dy * xhat, axis=0, keepdims=True), jnp.sum(dy, axis=0, keepdims=True)


def _rope(z, ta, tb, tc, sign):
    w = z.shape[1]
    reps = w // LANES
    a = jnp.tile(ta, (1, reps))
    b = jnp.tile(tb, (1, reps))
    c = jnp.tile(tc, (1, reps))
    return z * a + sign * (pltpu.roll(z, w - 8, 1) * b + pltpu.roll(z, 8, 1) * c)


def _shift_rows(x, prev_row, next_row):
    t = x.shape[0]
    sub = 8
    row = lax.broadcasted_iota(jnp.int32, (sub, x.shape[1]), 0)
    down, up = pltpu.roll(x, 1, 0), pltpu.roll(x, t - 1, 0)
    xm1 = jnp.concatenate([jnp.where(row == 0, prev_row, down[:sub]), down[sub:]], axis=0)
    xp1 = jnp.concatenate([up[:t - sub], jnp.where(row == sub - 1, next_row, up[t - sub:])], axis=0)
    return xm1, xp1


def _rope_tabs(cs, e_mat):
    hi = cs.astype(BF16)
    rest = cs - hi.astype(F32)
    mid = rest.astype(BF16)
    lo = (rest - mid.astype(F32)).astype(BF16)
    tabs = _dot(hi, e_mat) + _dot(mid, e_mat) + _dot(lo, e_mat)
    lane = lax.broadcasted_iota(jnp.int32, (cs.shape[0], LANES), 1)
    ones = jnp.where((lane & (HEAD_DIM - 1)) >= ROT_DIM, 1.0, 0.0)
    return tabs[:, :LANES] + ones, tabs[:, LANES:2 * LANES], tabs[:, 2 * LANES:]


def _rope_select_matrix():
    half = ROT_DIM // 2
    e = [[0.0] * (3 * LANES) for _ in range(ROT_DIM)]
    for lane in range(LANES):
        d = lane % HEAD_DIM
        if d < half:
            e[d][lane] = 1.0
            e[half + d][LANES + lane] = -1.0
        elif d < ROT_DIM:
            e[d - half][lane] = 1.0
            e[d][2 * LANES + lane] = 1.0
    return jnp.array(e, BF16)


def _rope_rows(x, cos_t, sin_t, sign):
    half = ROT_DIM // 2
    parts = []
    for base in (0, HEAD_DIM):
        r1, r2 = x[base:base + half], x[base + half:base + ROT_DIM]
        parts += [r1 * cos_t - sign * (r2 * sin_t), r2 * cos_t + sign * (r1 * sin_t), x[base + ROT_DIM:base + HEAD_DIM]]
    return jnp.concatenate(parts, axis=0)


MESH_IDS = pl.DeviceIdType.MESH
ANY = pl.BlockSpec(memory_space=pl.ANY)


def _place():
    x, y, c = lax.axis_index("x"), lax.axis_index("y"), lax.axis_index("c")
    other_chips = [(1 - x, y), (x, 1 - y), (1 - x, 1 - y)]
    return x, y, c, other_chips


class _ChipGather:
    def __init__(self, shards):
        self.inputs = list(shards)
        n = len(shards)
        self.out_shape = [jax.ShapeDtypeStruct((N_CHIPS,) + a.shape, a.dtype) for a in shards]
        self.scratch = [pltpu.SemaphoreType.DMA((6 * n,)), pltpu.SemaphoreType.DMA((6 * n,)),
                        pltpu.SemaphoreType.DMA((n,))]

    def _copies(self, src, dst, sems):
        send_sems, recv_sems, local_sems = sems
        x, y, c, chips = _place()
        mine = 2 * x + y
        n = len(src)
        local, sends, recvs, passes, pass_recvs = [], [], [], [], []
        for a in range(n):
            half = src[a].shape[0] // 2
            my_rows, other_rows = pl.ds(c * half, half), pl.ds((1 - c) * half, half)
            local.append(pltpu.make_async_copy(src[a], dst[a].at[mine], local_sems.at[a]))
            for j, (px, py) in enumerate(chips):
                k, k2, slot = 3 * a + j, 3 * n + 3 * a + j, 2 * px + py
                sends.append(pltpu.make_async_remote_copy(
                    src_ref=src[a].at[my_rows], dst_ref=dst[a].at[mine, my_rows], send_sem=send_sems.at[k],
                    recv_sem=recv_sems.at[k], device_id=(px, py, c), device_id_type=MESH_IDS))
                recvs.append(pltpu.make_async_remote_copy(
                    src_ref=src[a].at[my_rows], dst_ref=dst[a].at[slot, my_rows], send_sem=send_sems.at[k],
                    recv_sem=recv_sems.at[k], device_id=(px, py, c), device_id_type=MESH_IDS))
                passes.append(pltpu.make_async_remote_copy(
                    src_ref=dst[a].at[slot, my_rows], dst_ref=dst[a].at[slot, my_rows], send_sem=send_sems.at[k2],
                    recv_sem=recv_sems.at[k2], device_id=(x, y, 1 - c), device_id_type=MESH_IDS))
                pass_recvs.append(pltpu.make_async_remote_copy(
                    src_ref=dst[a].at[slot, my_rows], dst_ref=dst[a].at[slot, other_rows],
                    send_sem=send_sems.at[k2], recv_sem=recv_sems.at[k2], device_id=(x, y, 1 - c),
                    device_id_type=MESH_IDS))
        return local, sends, recvs, passes, pass_recvs

    def start(self, src, dst, sems):
        local, sends, _, _, _ = self._copies(src, dst, sems)
        for cp in local + sends:
            cp.start()

    def wait(self, src, dst, sems):
        local, sends, recvs, passes, pass_recvs = self._copies(src, dst, sems)
        for idx, landed in enumerate(recvs):
            landed.wait_recv()
            if passes:
                passes[idx].start()
        for cp in pass_recvs:
            cp.wait_recv()
        for cp in sends + passes:
            cp.wait_send()
        for cp in local:
            cp.wait()


class _ChipExchange:
    def __init__(self, parts, small=None):
        self.inputs = list(parts) + ([small] if small is not None else [])
        self.n = len(parts)
        self.has_small = small is not None
        self.out_shape = [jax.ShapeDtypeStruct(a.shape, a.dtype) for a in parts]
        n_sem, n_loc = 3 * self.n, self.n
        if self.has_small:
            self.out_shape.append(jax.ShapeDtypeStruct((N_DEV,) + small.shape, small.dtype))
            n_sem, n_loc = n_sem + N_DEV - 1, n_loc + 1
        self.scratch = [pltpu.SemaphoreType.DMA((n_sem,)), pltpu.SemaphoreType.DMA((n_sem,)),
                        pltpu.SemaphoreType.DMA((n_loc,))]

    def _copies(self, src, dst, sems):
        send_sems, recv_sems, local_sems = sems
        x, y, c, chips = _place()
        mine = 2 * x + y
        n = self.n
        local, sends, recvs = [], [], []
        for a in range(n):
            local.append(pltpu.make_async_copy(src[a].at[mine], dst[a].at[mine], local_sems.at[a]))
            for j, (px, py) in enumerate(chips):
                k = 3 * a + j
                sends.append(pltpu.make_async_remote_copy(
                    src_ref=src[a].at[2 * px + py], dst_ref=dst[a].at[mine], send_sem=send_sems.at[k],
                    recv_sem=recv_sems.at[k], device_id=(px, py, c), device_id_type=MESH_IDS))
                recvs.append(pltpu.make_async_remote_copy(
                    src_ref=src[a].at[mine], dst_ref=dst[a].at[2 * px + py], send_sem=send_sems.at[k],
                    recv_sem=recv_sems.at[k], device_id=(px, py, c), device_id_type=MESH_IDS))
        if self.has_small:
            me_dev = 4 * x + 2 * y + c
            local.append(pltpu.make_async_copy(src[n], dst[n].at[me_dev], local_sems.at[n]))
            for mask in range(1, N_DEV):
                px, py, pc = x ^ ((mask >> 2) & 1), y ^ ((mask >> 1) & 1), c ^ (mask & 1)
                k = 3 * n + mask - 1
                sends.append(pltpu.make_async_remote_copy(
                    src_ref=src[n], dst_ref=dst[n].at[me_dev], send_sem=send_sems.at[k], recv_sem=recv_sems.at[k],
                    device_id=(px, py, pc), device_id_type=MESH_IDS))
                recvs.append(pltpu.make_async_remote_copy(
                    src_ref=src[n], dst_ref=dst[n].at[4 * px + 2 * py + pc], send_sem=send_sems.at[k],
                    recv_sem=recv_sems.at[k], device_id=(px, py, pc), device_id_type=MESH_IDS))
        return local, sends, recvs, [], []

    start = _ChipGather.start
    wait = _ChipGather.wait


def _pcall(body, *, name, grid, in_specs, out_specs, out_shape, args, scratch_shapes=(), dims=None, comm=None):
    in_specs, out_specs, out_shape = list(in_specs), list(out_specs), list(out_shape)
    scratch_shapes = list(scratch_shapes)
    if comm is None:
        outs = pl.pallas_call(
            body, name=name, grid=grid, in_specs=in_specs, out_specs=out_specs, out_shape=out_shape,
            scratch_shapes=scratch_shapes, compiler_params=_cparams(dimension_semantics=dims),
        )(*args)
        return list(outs), []
    n_in, n_out, n_scr = len(in_specs), len(out_specs), len(scratch_shapes)
    n_cin, n_cout = len(comm.inputs), len(comm.out_shape)

    def wrapped(*refs):
        ins, refs = refs[:n_in], refs[n_in:]
        cins, refs = refs[:n_cin], refs[n_cin:]
        outs, refs = refs[:n_out], refs[n_out:]
        couts, refs = refs[:n_cout], refs[n_cout:]
        scr, csems = refs[:n_scr], refs[n_scr:]
        first = last = None
        for axis, size in enumerate(grid):
            pid = pl.program_id(axis)
            f, l = pid == 0, pid == size - 1
            first = f if first is None else first & f
            last = l if last is None else last & l

        @pl.when(first)
        def _():
            comm.start(cins, couts, csems)

        body(*ins, *outs, *scr)

        @pl.when(last)
        def _():
            comm.wait(cins, couts, csems)

    res = pl.pallas_call(
        wrapped, name=name, grid=grid, in_specs=in_specs + [ANY] * n_cin, out_specs=out_specs + [ANY] * n_cout,
        out_shape=out_shape + list(comm.out_shape), scratch_shapes=scratch_shapes + list(comm.scratch),
        compiler_params=_cparams(dimension_semantics=("arbitrary",) * len(grid)),
    )(*args, *comm.inputs)
    return list(res[:n_out]), list(res[n_out:])


def _comm_only(comm, name):
    def body(*refs):
        n_cin, n_cout = len(comm.inputs), len(comm.out_shape)
        cins, couts, csems = refs[:n_cin], refs[n_cin:n_cin + n_cout], refs[n_cin + n_cout:]
        comm.start(cins, couts, csems)
        comm.wait(cins, couts, csems)

    return list(pl.pallas_call(
        body, name=name, in_specs=[ANY] * len(comm.inputs), out_specs=[ANY] * len(comm.out_shape),
        out_shape=list(comm.out_shape), scratch_shapes=list(comm.scratch),
    )(*comm.inputs))


def _mm(a, b, *, mode, out_dtype, tm, tn, tk=None, name):
    if mode == "nt":
        m, k = a.shape
        n = b.shape[0]
        assert m % tm == 0 and n % tn == 0

        def body(a_ref, b_ref, o_ref):
            o_ref[...] = _dot_nt(a_ref[...], b_ref[...]).astype(out_dtype)

        return pl.pallas_call(
            body, name=name, grid=(m // tm, n // tn),
            in_specs=[pl.BlockSpec((tm, k), lambda i, j: (i, 0)), pl.BlockSpec((tn, k), lambda i, j: (j, 0))],
            out_specs=pl.BlockSpec((tm, tn), lambda i, j: (i, j)),
            out_shape=jax.ShapeDtypeStruct((m, n), out_dtype),
            compiler_params=_cparams(dimension_semantics=("parallel", "parallel")),
        )(a, b)
    assert mode == "tn"
    kk, m = a.shape
    n = b.shape[1]
    assert m % tm == 0 and n % tn == 0 and kk % tk == 0
    nk = kk // tk

    def body(a_ref, b_ref, o_ref, acc_ref):
        kstep = pl.program_id(2)

        @pl.when(kstep == 0)
        def _():
            acc_ref[...] = jnp.zeros_like(acc_ref)

        acc_ref[...] += _dot_tn(a_ref[...], b_ref[...])

        @pl.when(kstep == nk - 1)
        def _():
            o_ref[...] = acc_ref[...].astype(out_dtype)

    return pl.pallas_call(
        body, name=name, grid=(m // tm, n // tn, nk),
        in_specs=[pl.BlockSpec((tk, tm), lambda i, j, s: (s, i)), pl.BlockSpec((tk, tn), lambda i, j, s: (s, j))],
        out_specs=pl.BlockSpec((tm, tn), lambda i, j, s: (i, j)),
        out_shape=jax.ShapeDtypeStruct((m, n), out_dtype),
        scratch_shapes=[pltpu.VMEM((tm, tn), F32)],
        compiler_params=_cparams(dimension_semantics=("parallel", "parallel", "arbitrary")),
    )(a, b)


PROJ_COLS = 256


def _proj_segments():
    wd = DIL_SLOTS * HEAD_DIM
    segs = [(1, [(0, 1), (PROJ_COLS, 1), (2 * PROJ_COLS, 2)])]
    for gi, dil in enumerate(DILATIONS):
        blocks = []
        for part, kind in enumerate((1, 1, 0)):
            col = A_WIDTH + part * B_QKV + gi * wd
            blocks += [(col, kind), (col + PROJ_COLS, kind)]
        segs.append((dil, blocks))
    return segs


PROJ_SEGMENTS = _proj_segments()


def _dh0_ln_in(dz, w_t, dr1, x, ln_in_g, *, t, comm=None):
    s, k = dz.shape

    def body(dz_ref, w_ref, dr1_ref, x_ref, g_ref, gx_ref, st_ref):
        i = pl.program_id(0)

        @pl.when(i == 0)
        def _():
            st_ref[...] = jnp.zeros_like(st_ref)

        dh0 = _dot(dz_ref[...], w_ref[...]) + ALPHA * dr1_ref[...]
        dx, dg, db = _ln_bwd_math(dh0, x_ref[...], g_ref[...])
        gx_ref[...] = dx
        st_ref[0:1, :] += dg
        st_ref[1:2, :] += db

    tile = pl.BlockSpec((t, D_MODEL), lambda i: (i, 0))
    return _pcall(
        body, name="dh0_ln_in", grid=(s // t,),
        in_specs=[pl.BlockSpec((t, k), lambda i: (i, 0)),
                  pl.BlockSpec((k, D_MODEL), lambda i: (0, 0), pipeline_mode=pl.Buffered(1)),
                  tile, tile, pl.BlockSpec((1, D_MODEL), lambda i: (0, 0))],
        out_specs=[tile, pl.BlockSpec((8, D_MODEL), lambda i: (0, 0))],
        out_shape=[jax.ShapeDtypeStruct((s, D_MODEL), F32), jax.ShapeDtypeStruct((8, D_MODEL), F32)],
        args=[dz, w_t, dr1, x, ln_in_g], dims=("arbitrary",), comm=comm)


def _ln_in_fwd(x, g, b, *, t, comm=None):
    s = x.shape[0]

    def body(x_ref, g_ref, b_ref, o_ref):
        o_ref[...] = _ln(x_ref[...], g_ref[...], b_ref[...]).astype(BF16)

    row = pl.BlockSpec((1, D_MODEL), lambda i: (0, 0))
    tile = pl.BlockSpec((t, D_MODEL), lambda i: (i, 0))
    outs, couts = _pcall(body, name="ln_in_fwd", grid=(s // t,), in_specs=[tile, row, row], out_specs=[tile],
                         out_shape=[jax.ShapeDtypeStruct((s, D_MODEL), BF16)], args=[x, g, b], dims=("parallel",),
                         comm=comm)
    return outs[0], couts


def _proj_all(h0b, w_t, cs, e_mat, *, t, comm=None):
    s = h0b.shape[0]
    cb = PROJ_COLS
    halves = cb // LANES

    def body(h_ref, w_ref, cs_ref, e_ref, *rest):
        z_refs, scr = rest[:-1], rest[-1]
        h = h_ref[...]
        ta, tb, tc = (jnp.tile(tab, (1, halves)) for tab in _rope_tabs(cs_ref[...], e_ref[...]))
        lane = lax.broadcasted_iota(jnp.int32, (t, cb), 1)
        slot = 0
        for z_ref, (dil, blocks) in zip(z_refs, PROJ_SEGMENTS):
            for jb, (col, kind) in enumerate(blocks):
                acc = _dot_nt(h, w_ref[col:col + cb, :])
                if kind:
                    z = acc * ta + (pltpu.roll(acc, cb - 8, 1) * tb + pltpu.roll(acc, 8, 1) * tc)
                    if kind == 2:
                        z = jnp.where(lane < LANES, z, acc)
                else:
                    z = acc
                if dil == 1:
                    z_ref[0, :, cb * jb:cb * (jb + 1)] = z.astype(BF16)
                    continue
                for half in range(halves):
                    scr[slot, half] = z[:, half * LANES:(half + 1) * LANES]
                for c in range(dil):
                    for half in range(halves):
                        rows = scr[slot, half, pl.ds(c, t // dil, stride=dil), :]
                        z_ref[c, :, cb * jb + half * LANES:cb * jb + (half + 1) * LANES] = rows.astype(BF16)
                slot = 1 - slot

    widths = [cb * len(blocks) for _, blocks in PROJ_SEGMENTS]
    dils = [dil for dil, _ in PROJ_SEGMENTS]
    outs, couts = _pcall(
        body, name="proj_all", grid=(s // t,),
        in_specs=[pl.BlockSpec((t, D_MODEL), lambda i: (i, 0)),
                  pl.BlockSpec((IN_WIDTH, D_MODEL), lambda i: (0, 0), pipeline_mode=pl.Buffered(1)),
                  pl.BlockSpec((t, ROT_DIM), lambda i: (i, 0)), pl.BlockSpec((ROT_DIM, 3 * LANES), lambda i: (0, 0))],
        out_specs=[pl.BlockSpec((dil, t // dil, wd), lambda i: (0, i, 0)) for dil, wd in zip(dils, widths)],
        out_shape=[jax.ShapeDtypeStruct((dil, s // dil, wd), BF16) for dil, wd in zip(dils, widths)],
        args=[h0b, w_t, cs, e_mat], scratch_shapes=[pltpu.VMEM((2, halves, t, LANES), F32)],
        dims=("parallel",), comm=comm)
    return outs, couts


PAIR = 2 * HEAD_DIM
ONE_TILE_ROWS = 2048


def _place_head(x2, src_pos, dst_pos):
    hi = lax.broadcasted_iota(jnp.int32, x2.shape, 1) >= HEAD_DIM
    src = x2 if src_pos == dst_pos else pltpu.roll(x2, HEAD_DIM, 1)
    return jnp.where(hi == (dst_pos == 1), src, jnp.zeros_like(src))


def _band_mask_t(row0, tq, w, seq_len):
    tk = tq + 2 * w
    kk = lax.broadcasted_iota(jnp.int32, (tk, tq), 0)
    qq = lax.broadcasted_iota(jnp.int32, (tk, tq), 1)
    kpos = row0 - w + kk
    return (jnp.abs(qq + w - kk) <= w) & (kpos >= 0) & (kpos < seq_len)


def _halo_kv_specs(t, w, hkv, n, seq_len, kcol, vcol):
    kw = hkv * HEAD_DIM
    per, last = t // w, seq_len // w - 1
    cur = lambda s, i: jnp.minimum(i, n - 1)
    specs = []
    for c in (kcol, vcol):
        specs += [pl.BlockSpec((None, w, kw), lambda s, i, c=c: (s, jnp.maximum(cur(s, i) * per - 1, 0), c)),
                  pl.BlockSpec((None, t, kw), lambda s, i, c=c: (s, cur(s, i), c)),
                  pl.BlockSpec((None, w, kw), lambda s, i, c=c: (s, jnp.minimum((cur(s, i) + 1) * per, last), c))]
    return specs, cur


def _pair_kv(kfull, vfull, qp, rep, krows):
    ks, vs, a_of = [], [], []
    for pos in range(2):
        g = (2 * qp + pos) // rep
        a_of.append(g // 2)
        ks.append(_place_head(kfull[g // 2][krows], g % 2, pos))
        vs.append(_place_head(vfull[g // 2][krows], g % 2, pos))
    assert a_of[0] == a_of[1]
    return jnp.concatenate(ks, axis=0), jnp.concatenate(vs, axis=0), a_of[0]


def _swa_fwd_p(qkv, *, qcol, kcol, vcol, hq, hkv, w, tq, sub, sink, name, comm=None):
    nseq, seq_len, _ = qkv.shape
    t = tq * sub
    n = seq_len // t
    rep = hq // hkv
    tk = tq + 2 * w
    kv_specs, cur = _halo_kv_specs(t, w, hkv, n, seq_len, kcol, vcol)

    def body(*refs):
        if sink is not None:
            sink_ref, refs = refs[0], refs[1:]
        q_ref, kp_ref, kc_ref, kn_ref, vp_ref, vc_ref, vn_ref, o_ref, lse_ref = refs
        i = pl.program_id(1)
        kfull, vfull = [], []
        for a in range(hkv // 2):
            ls = slice(a * PAIR, (a + 1) * PAIR)
            kfull.append(jnp.concatenate([kp_ref[:, ls], kc_ref[:, ls], kn_ref[:, ls]], axis=0) * 0.125)
            vfull.append(jnp.concatenate([vp_ref[:, ls], vc_ref[:, ls], vn_ref[:, ls]], axis=0))
        row_hi = lax.broadcasted_iota(jnp.int32, (PAIR, tq), 0) >= HEAD_DIM
        for jj in range(sub):
            rows = slice(jj * tq, (jj + 1) * tq)
            mask_t = _band_mask_t(i * t + jj * tq, tq, w, seq_len)
            o_t, lse_rows = [], []
            for qp in range(hq // 2):
                kst, vst, _ = _pair_kv(kfull, vfull, qp, rep, slice(jj * tq, jj * tq + tk))
                s2 = _dot_nt(kst, q_ref[rows, qp * PAIR:(qp + 1) * PAIR])
                ps, dens = [], []
                for pos in range(2):
                    h = 2 * qp + pos
                    s_t = jnp.where(mask_t, s2[pos * tk:(pos + 1) * tk], NEG_INF)
                    m = jnp.max(s_t, axis=0, keepdims=True)
                    if sink is not None:
                        m = jnp.maximum(m, sink_ref[0, h])
                    p_t = jnp.exp(s_t - m)
                    den = jnp.sum(p_t, axis=0, keepdims=True)
                    if sink is not None:
                        den = den + jnp.exp(sink_ref[0, h] - m)
                    ps.append(p_t.astype(BF16))
                    dens.append(den)
                    lse_rows.append(m + jnp.log(den))
                both = _dot_tn(vst, jnp.concatenate(ps, axis=0))
                o_t.append(both / jnp.where(row_hi, dens[1], dens[0]))
            o_ref[rows, :] = jnp.concatenate(o_t, axis=0).T
            lse_ref[:, rows] = jnp.concatenate(lse_rows, axis=0)

    in_specs = [pl.BlockSpec((None, t, hq * HEAD_DIM), lambda s, i: (s, i, qcol))] + kv_specs
    args = [qkv] * 7
    if sink is not None:
        in_specs = [pl.BlockSpec(memory_space=pltpu.SMEM)] + in_specs
        args = [sink] + args
    (o, lse), couts = _pcall(
        body, name=name, grid=(nseq, n), in_specs=in_specs,
        out_specs=[pl.BlockSpec((None, t, hq * HEAD_DIM), lambda s, i: (s, i, 0)),
                   pl.BlockSpec((None, hq, t), lambda s, i: (s, 0, i))],
        out_shape=[jax.ShapeDtypeStruct((nseq, seq_len, hq * HEAD_DIM), F32),
                   jax.ShapeDtypeStruct((nseq, hq, seq_len), F32)],
        args=args, dims=("parallel", "parallel"), comm=comm)
    return o, lse, couts


def _swa_bwd_p(qkv, do, lse, delta, cs, e_mat, *, qcol, kcol, vcol, hq, hkv, w, tq, sub, sink, name, comm=None):
    nseq, seq_len, _ = qkv.shape
    t = tq * sub
    n = seq_len // t
    rep = hq // hkv
    qw, kw = hq * HEAD_DIM, hkv * HEAD_DIM
    tk = tq + 2 * w
    kv_specs, cur = _halo_kv_specs(t, w, hkv, n, seq_len, kcol, vcol)

    def body(*refs):
        if sink is not None:
            sink_ref, refs = refs[0], refs[1:]
        (q_ref, kp_ref, kc_ref, kn_ref, vp_ref, vc_ref, vn_ref, do_ref, lse_ref, dl_ref,
         cs_c, cs_p, e_ref) = refs[:13]
        outs = refs[13:]
        dq_ref, dk_ref, dv_ref = outs[:3]
        dsink_ref = outs[3] if sink is not None else None
        dk_win, dv_win = outs[-2:]
        dk_acc, dv_acc = outs[-4:-2] if n > 1 else (None, None)
        s_id = pl.program_id(0)
        i = pl.program_id(1)
        slot_p, slot_c, slot_n = (i + 2) % 3, i % 3, (i + 1) % 3

        if sink is not None:
            @pl.when((s_id == 0) & (i == 0))
            def _():
                dsink_ref[...] = jnp.zeros_like(dsink_ref)

        @pl.when(i < n)
        def _():
            dk_win[...] = jnp.zeros_like(dk_win)
            dv_win[...] = jnp.zeros_like(dv_win)
            kfull, vfull = [], []
            for a in range(hkv // 2):
                ls = slice(a * PAIR, (a + 1) * PAIR)
                kfull.append(jnp.concatenate([kp_ref[:, ls], kc_ref[:, ls], kn_ref[:, ls]], axis=0) * 0.125)
                vfull.append(jnp.concatenate([vp_ref[:, ls], vc_ref[:, ls], vn_ref[:, ls]], axis=0))
            for jj in range(sub):
                rows = slice(jj * tq, (jj + 1) * tq)
                krows = slice(jj * tq, jj * tq + tk)
                mask_t = _band_mask_t(i * t + jj * tq, tq, w, seq_len)
                dq_t = []
                dk2 = [None] * (hkv // 2)
                dv2 = [None] * (hkv // 2)
                for qp in range(hq // 2):
                    kst, vst, a = _pair_kv(kfull, vfull, qp, rep, krows)
                    q2 = q_ref[rows, qp * PAIR:(qp + 1) * PAIR]
                    do2 = do_ref[rows, qp * PAIR:(qp + 1) * PAIR]
                    s2 = _dot_nt(kst, q2)
                    dp2 = _dot_nt(vst, do2)
                    ds, ps, q_at, do_at = [], [], [], []
                    for pos in range(2):
                        h = 2 * qp + pos
                        e = (h // rep) % 2
                        half = slice(pos * tk, (pos + 1) * tk)
                        lse_h = lse_ref[h:h + 1, rows]
                        dl_h = dl_ref[h:h + 1, rows]
                        p_t = jnp.exp(jnp.where(mask_t, s2[half], NEG_INF) - lse_h)
                        ds.append((p_t * (dp2[half] - dl_h)).astype(BF16))
                        ps.append(p_t.astype(BF16))
                        q_at.append(_place_head(q2, pos, e) * 0.125)
                        do_at.append(_place_head(do2, pos, e))
                        if sink is not None:
                            ds_sink = -jnp.sum(jnp.exp(sink_ref[0, h] - lse_h) * dl_h)
                            dsink_ref[h:h + 1, :] += jnp.full((1, LANES), ds_sink, F32)
                    dq_t.append(_rope_rows(_dot_tn(kst, jnp.concatenate(ds, axis=0)),
                                           cs_c[0:ROT_DIM // 2, rows], cs_c[ROT_DIM // 2:ROT_DIM, rows], -1.0))
                    dk_part = _dot(jnp.concatenate(ds, axis=1), jnp.concatenate(q_at, axis=0))
                    dv_part = _dot(jnp.concatenate(ps, axis=1), jnp.concatenate(do_at, axis=0))
                    dk2[a] = dk_part if dk2[a] is None else dk2[a] + dk_part
                    dv2[a] = dv_part if dv2[a] is None else dv2[a] + dv_part
                for a in range(hkv // 2):
                    ls = slice(a * PAIR, (a + 1) * PAIR)
                    dk_win[krows, ls] += dk2[a]
                    dv_win[krows, ls] += dv2[a]
                dq_ref[rows, :] = jnp.concatenate(dq_t, axis=0).T.astype(BF16)

            if n == 1:
                dk_ref[...] = _rope(dk_win[w:w + t, :], *_rope_tabs(cs_p[...], e_ref[...]), -1.0).astype(BF16)
                dv_ref[...] = dv_win[w:w + t, :].astype(BF16)
                return

            @pl.when(i > 0)
            def _():
                dk_acc[slot_p, t - w:, :] += dk_win[:w, :]
                dv_acc[slot_p, t - w:, :] += dv_win[:w, :]

            @pl.when(i == 0)
            def _():
                dk_acc[slot_c] = dk_win[w:w + t, :]
                dv_acc[slot_c] = dv_win[w:w + t, :]

            @pl.when(i > 0)
            def _():
                dk_acc[slot_c] += dk_win[w:w + t, :]
                dv_acc[slot_c] += dv_win[w:w + t, :]

            dk_acc[slot_n] = jnp.zeros((t, kw), F32)
            dv_acc[slot_n] = jnp.zeros((t, kw), F32)
            dk_acc[slot_n, :w, :] = dk_win[w + t:, :]
            dv_acc[slot_n, :w, :] = dv_win[w + t:, :]

        if n > 1:
            @pl.when(i >= 1)
            def _():
                dk_ref[...] = _rope(dk_acc[slot_p], *_rope_tabs(cs_p[...], e_ref[...]), -1.0).astype(BF16)
                dv_ref[...] = dv_acc[slot_p].astype(BF16)

    row_c = lambda width: pl.BlockSpec((None, t, width), lambda s, i: (s, cur(s, i), 0))
    row_p = lambda width: pl.BlockSpec((None, t, width), lambda s, i: (s, jnp.maximum(i - 1, 0), 0))
    stat = pl.BlockSpec((None, hq, t), lambda s, i: (s, 0, cur(s, i)))
    cs_rows = pl.BlockSpec((None, ROT_DIM, t), lambda s, i: (s, 0, cur(s, i)))
    in_specs = ([pl.BlockSpec((None, t, qw), lambda s, i: (s, cur(s, i), qcol))] + kv_specs
                + [row_c(qw), stat, stat, cs_rows, row_p(ROT_DIM),
                   pl.BlockSpec((ROT_DIM, 3 * LANES), lambda s, i: (0, 0))])
    args = [qkv] * 7 + [do, lse, delta, cs.transpose(0, 2, 1), cs, e_mat]
    out_specs = [row_c(qw), row_p(kw), row_p(kw)]
    out_shape = [jax.ShapeDtypeStruct((nseq, seq_len, qw), BF16),
                 jax.ShapeDtypeStruct((nseq, seq_len, kw), BF16),
                 jax.ShapeDtypeStruct((nseq, seq_len, kw), BF16)]
    if sink is not None:
        in_specs = [pl.BlockSpec(memory_space=pltpu.SMEM)] + in_specs
        args = [sink] + args
        out_specs.append(pl.BlockSpec((8, LANES), lambda s, i: (0, 0)))
        out_shape.append(jax.ShapeDtypeStruct((8, LANES), F32))
    return _pcall(
        body, name=name, grid=(nseq, n + 1 if n > 1 else 1), in_specs=in_specs, out_specs=out_specs,
        out_shape=out_shape,
        scratch_shapes=([pltpu.VMEM((3, t, kw), F32), pltpu.VMEM((3, t, kw), F32)] if n > 1 else [])
        + [pltpu.VMEM((t + 2 * w, kw), F32), pltpu.VMEM((t + 2 * w, kw), F32)], args=args,
        dims=("arbitrary", "arbitrary"), comm=comm)


def _rms_parts(o, g):
    ms = jnp.mean(o * o, axis=-1, keepdims=True) + LN_EPS
    rinv = lax.rsqrt(ms)
    return o * rinv * g, rinv


def _from_subsequences(ref, scr, dil, t):
    slabs = ref.shape[-1] // LANES
    if dil == 1:
        return ref[0].astype(F32)
    for c in range(dil):
        for sl in range(slabs):
            scr[sl, pl.ds(c, t // dil, stride=dil), :] = ref[c, :, sl * LANES:(sl + 1) * LANES].astype(F32)
    return jnp.concatenate([scr[sl] for sl in range(slabs)], axis=1)


def _to_subsequences(val, ref, scr, dil, t):
    slabs = val.shape[-1] // LANES
    if dil == 1:
        ref[0] = val.astype(ref.dtype)
        return
    for sl in range(slabs):
        scr[sl] = val[:, sl * LANES:(sl + 1) * LANES]
    for c in range(dil):
        for sl in range(slabs):
            ref[c, :, sl * LANES:(sl + 1) * LANES] = scr[sl, pl.ds(c, t // dil, stride=dil), :].astype(ref.dtype)


def _combine_fwd(out_a, o_g, lse_g, g_win, g_dil, w_mix_b, x, ln_in_g, ln_in_b, ln1_g, ln1_b, *, t, comm=None):
    s = out_a.shape[1]
    wd = DIL_SLOTS * HEAD_DIM

    def body(oa_ref, o0, o1, o2, l0, l1, l2, gw_ref, gd_ref, w_ref, x_ref, g0, b0, g1, b1,
             mixed_ref, ob_ref, lt_ref, r1_ref, h1_ref, scr):
        ls = [l0[...], l1[...], l2[...]]
        mx = jnp.maximum(jnp.maximum(ls[0], ls[1]), ls[2])
        ws = [jnp.exp(l - mx) for l in ls]
        tot = ws[0] + ws[1] + ws[2]
        lt_ref[...] = mx + jnp.log(tot)
        ws = [x / tot for x in ws]
        og = [_from_subsequences(o_ref, scr.at[gi], dil, t)
              for gi, (o_ref, dil) in enumerate(zip((o0, o1, o2), DILATIONS))]
        parts = []
        for h in range(DIL_SLOTS):
            hs = slice(h * HEAD_DIM, (h + 1) * HEAD_DIM)
            parts.append(ws[0][:, h:h + 1] * og[0][:, hs] + ws[1][:, h:h + 1] * og[1][:, hs]
                         + ws[2][:, h:h + 1] * og[2][:, hs])
        ob = jnp.concatenate(parts, axis=1)
        ob_ref[...] = ob
        na, _ = _rms_parts(oa_ref[...], gw_ref[...])
        nb, _ = _rms_parts(ob, gd_ref[...])
        mixed = jnp.concatenate([na.astype(BF16), nb.astype(BF16)], axis=1)
        mixed_ref[...] = mixed
        h0 = _ln(x_ref[...], g0[...], b0[...])
        r1 = ALPHA * h0 + _dot(mixed, w_ref[...])
        r1_ref[...] = r1
        h1_ref[...] = _ln(r1, g1[...], b1[...]).astype(BF16)

    half = pl.BlockSpec((t, wd), lambda i: (i, 0))
    full = pl.BlockSpec((t, D_MODEL), lambda i: (i, 0))
    lanes = pl.BlockSpec((t, LANES), lambda i: (i, 0))
    grow = pl.BlockSpec((1, wd), lambda i: (0, 0))
    row = pl.BlockSpec((1, D_MODEL), lambda i: (0, 0))
    subseq = [pl.BlockSpec((dil, t // dil, wd), lambda i: (0, i, 0)) for dil in DILATIONS]
    return _pcall(
        body, name="combine_fwd", grid=(s // t,),
        in_specs=[pl.BlockSpec((None, t, wd), lambda i: (0, i, 0))] + subseq
        + [lanes, lanes, lanes, grow, grow, pl.BlockSpec((D_MODEL, D_MODEL), lambda i: (0, 0)), full,
           row, row, row, row],
        out_specs=[full, half, lanes, full, full],
        out_shape=[jax.ShapeDtypeStruct((s, D_MODEL), BF16), jax.ShapeDtypeStruct((s, wd), F32),
                   jax.ShapeDtypeStruct((s, LANES), F32), jax.ShapeDtypeStruct((s, D_MODEL), F32),
                   jax.ShapeDtypeStruct((s, D_MODEL), BF16)],
        scratch_shapes=[pltpu.VMEM((len(DILATIONS), wd // LANES, t, LANES), F32)],
        args=[out_a, *o_g, *lse_g, g_win, g_dil, w_mix_b, x, ln_in_g, ln_in_b, ln1_g, ln1_b], dims=("parallel",),
        comm=comm)


def _combine_bwd(dr1b, w_mix_b, out_a, out_b, g_win, g_dil, *, t):
    s = out_b.shape[0]
    wd = DIL_SLOTS * HEAD_DIM

    def body(dr_ref, w_ref, oa_ref, ob_ref, gw_ref, gd_ref, doa_ref, dob0, dob1, dob2, dla_ref, dlb_ref, st_ref,
             scr):
        i = pl.program_id(0)
        dm = _dot_nt(dr_ref[...], w_ref[...])

        @pl.when(i == 0)
        def _():
            st_ref[...] = jnp.zeros_like(st_ref)

        lane = lax.broadcasted_iota(jnp.int32, (t, LANES), 1)
        for idx, (o_ref, g_ref, dl_ref) in enumerate(((oa_ref, gw_ref, dla_ref), (ob_ref, gd_ref, dlb_ref))):
            o = o_ref[...]
            dn = dm[:, idx * wd:(idx + 1) * wd]
            _, rinv = _rms_parts(o, g_ref[...])
            wv = dn * g_ref[...]
            do = rinv * wv - o * (rinv * rinv * rinv) * jnp.mean(wv * o, axis=-1, keepdims=True)
            st_ref[idx:idx + 1, :] += jnp.sum(dn * o * rinv, axis=0, keepdims=True)
            if idx == 0:
                doa_ref[...] = do.astype(BF16)
            else:
                for do_ref, dil in zip((dob0, dob1, dob2), DILATIONS):
                    _to_subsequences(do, do_ref, scr, dil, t)
            prod = do * o
            acc = jnp.zeros((t, LANES), F32)
            for h in range(DIL_SLOTS):
                hs = slice(h * HEAD_DIM, (h + 1) * HEAD_DIM)
                acc = jnp.where(lane == h, jnp.sum(prod[:, hs], axis=1, keepdims=True), acc)
            dl_ref[...] = acc

    half = pl.BlockSpec((t, wd), lambda i: (i, 0))
    lanes = pl.BlockSpec((t, LANES), lambda i: (i, 0))
    grow = pl.BlockSpec((1, wd), lambda i: (0, 0))
    a_spec = pl.BlockSpec((None, t, wd), lambda i: (0, i, 0))
    subseq = [pl.BlockSpec((dil, t // dil, wd), lambda i: (0, i, 0)) for dil in DILATIONS]
    doa, dob0, dob1, dob2, dla, dlb, st = pl.pallas_call(
        body, name="combine_bwd", grid=(s // t,),
        in_specs=[pl.BlockSpec((t, D_MODEL), lambda i: (i, 0)), pl.BlockSpec((D_MODEL, D_MODEL), lambda i: (0, 0)),
                  a_spec, half, grow, grow],
        out_specs=[a_spec] + subseq + [lanes, lanes, pl.BlockSpec((8, wd), lambda i: (0, 0))],
        out_shape=[jax.ShapeDtypeStruct((1, s, wd), BF16)]
        + [jax.ShapeDtypeStruct((dil, s // dil, wd), BF16) for dil in DILATIONS]
        + [jax.ShapeDtypeStruct((s, LANES), F32), jax.ShapeDtypeStruct((s, LANES), F32),
           jax.ShapeDtypeStruct((8, wd), F32)],
        scratch_shapes=[pltpu.VMEM((wd // LANES, t, LANES), F32)],
        compiler_params=_cparams(dimension_semantics=("arbitrary",)),
    )(dr1b, w_mix_b, out_a, out_b, g_win, g_dil)
    return doa, [dob0, dob1, dob2], dla, dlb, st


def _assemble_dz(dqa, dka, dva, dqs, dks, dvs, *, t):
    s = dqa.shape[1]
    wd = DIL_SLOTS * HEAD_DIM

    def body(*refs):
        a_refs, g_refs, o_ref, scr = refs[:3], refs[3:12], refs[12], refs[13]
        col = 0
        for r in a_refs:
            o_ref[:, col:col + r.shape[-1]] = r[...]
            col += r.shape[-1]
        for part in range(3):
            for gi, dil in enumerate(DILATIONS):
                val = _from_subsequences(g_refs[3 * part + gi], scr, dil, t)
                o_ref[:, col:col + wd] = val.astype(BF16)
                col += wd

    a_specs = [pl.BlockSpec((None, t, a.shape[-1]), lambda i: (0, i, 0)) for a in (dqa, dka, dva)]
    g_specs = [pl.BlockSpec((dil, t // dil, wd), lambda i: (0, i, 0)) for _ in range(3) for dil in DILATIONS]
    return pl.pallas_call(
        body, name="assemble_dz", grid=(s // t,), in_specs=a_specs + g_specs,
        out_specs=pl.BlockSpec((t, IN_WIDTH), lambda i: (i, 0)),
        out_shape=jax.ShapeDtypeStruct((s, IN_WIDTH), BF16),
        scratch_shapes=[pltpu.VMEM((wd // LANES, t, LANES), F32)],
        compiler_params=_cparams(dimension_semantics=("parallel",)),
    )(dqa, dka, dva, *dqs, *dks, *dvs)


def _mem_fwd(mem, g, b, wk_b, wv_b):
    ml = mem.shape[0]

    def body(mem_ref, g_ref, b_ref, wk_ref, wv_ref, mn_ref, kx_ref, vx_ref):
        mn = _ln(mem_ref[...], g_ref[...], b_ref[...]).astype(BF16)
        mn_ref[...] = mn
        kx_ref[...] = _dot(mn, wk_ref[...]).astype(BF16)
        vx_ref[...] = _dot(mn, wv_ref[...]).astype(BF16)

    sh = jax.ShapeDtypeStruct((ml, D_MODEL), BF16)
    return pl.pallas_call(body, name="mem_fwd", out_shape=[sh, sh, sh], compiler_params=_cparams())(
        mem, g, b, wk_b, wv_b)


def _mem_bwd(dkx, dvx, mem, g, b, wk_b, wv_b):
    def body(dk_ref, dv_ref, mem_ref, g_ref, b_ref, wk_ref, wv_ref, dwk_ref, dwv_ref, st_ref):
        mem_v = mem_ref[...]
        mn = _ln(mem_v, g_ref[...], b_ref[...]).astype(BF16)
        dkb = dk_ref[...].astype(BF16)
        dvb = dv_ref[...].astype(BF16)
        dwk_ref[...] = _dot_tn(mn, dkb)
        dwv_ref[...] = _dot_tn(mn, dvb)
        dmn = _dot_nt(dkb, wk_ref[...]) + _dot_nt(dvb, wv_ref[...])
        _, dg, db = _ln_bwd_math(dmn, mem_v, g_ref[...])
        st_ref[...] = jnp.zeros_like(st_ref)
        st_ref[0:1, :] = dg
        st_ref[1:2, :] = db

    sw = jax.ShapeDtypeStruct((D_MODEL, D_MODEL), F32)
    return pl.pallas_call(body, name="mem_bwd", out_shape=[sw, sw, jax.ShapeDtypeStruct((8, D_MODEL), F32)],
                          compiler_params=_cparams())(dkx, dvx, mem, g, b, wk_b, wv_b)


def _xattn_fwd(h1b, r1, kx, vx, wq_b, wo_b, ln1_g, ln1_b, ln2_g, ln2_b, *, t, comm=None):
    s = h1b.shape[0]
    scale = X_HEAD_DIM ** -0.5

    def body(h_ref, r1_ref, kx_ref, vx_ref, wq_ref, wo_ref, g1, b1, g2, b2, r2_ref, h2_ref, qx_ref, ox_ref, lse_ref):
        qxb = _dot(h_ref[...], wq_ref[...]).astype(BF16)
        qx_ref[...] = qxb
        lane = lax.broadcasted_iota(jnp.int32, (t, LANES), 1)
        lse_acc = jnp.zeros((t, LANES), F32)
        parts = []
        for h in range(X_HEADS):
            hs = slice(h * X_HEAD_DIM, (h + 1) * X_HEAD_DIM)
            sc = _dot_nt(qxb[:, hs] * scale, kx_ref[:, hs])
            m = jnp.max(sc, axis=1, keepdims=True)
            p = jnp.exp(sc - m)
            den = jnp.sum(p, axis=1, keepdims=True)
            parts.append(_dot(p.astype(BF16), vx_ref[:, hs]) / den)
            lse_acc = jnp.where(lane == h, m + jnp.log(den), lse_acc)
        lse_ref[...] = lse_acc
        oxb = jnp.concatenate(parts, axis=1).astype(BF16)
        ox_ref[...] = oxb
        h1 = _ln(r1_ref[...], g1[...], b1[...])
        r2 = ALPHA * h1 + _dot(oxb, wo_ref[...])
        r2_ref[...] = r2
        h2_ref[...] = _ln(r2, g2[...], b2[...]).astype(BF16)

    tile = pl.BlockSpec((t, D_MODEL), lambda i: (i, 0))
    row = pl.BlockSpec((1, D_MODEL), lambda i: (0, 0))
    full = lambda r: pl.BlockSpec((r, D_MODEL), lambda i: (0, 0))
    ml = kx.shape[0]
    bsh = jax.ShapeDtypeStruct((s, D_MODEL), BF16)
    return _pcall(
        body, name="xattn_fwd", grid=(s // t,),
        in_specs=[tile, tile, full(ml), full(ml), full(D_MODEL), full(D_MODEL), row, row, row, row],
        out_specs=[tile, tile, tile, tile, pl.BlockSpec((t, LANES), lambda i: (i, 0))],
        out_shape=[jax.ShapeDtypeStruct((s, D_MODEL), F32), bsh, bsh, bsh, jax.ShapeDtypeStruct((s, LANES), F32)],
        args=[h1b, r1, kx, vx, wq_b, wo_b, ln1_g, ln1_b, ln2_g, ln2_b], dims=("parallel",), comm=comm)


def _xattn_bwd(dr2, qxb, oxb, lse, kx, vx, wq_b, wo_b, r1, ln1_g, *, t, comm=None):
    s = dr2.shape[0]
    ml = kx.shape[0]
    scale = X_HEAD_DIM ** -0.5

    def body(dr2_ref, qx_ref, ox_ref, lse_ref, kx_ref, vx_ref, wq_ref, wo_ref, r1_ref, g1_ref,
             dr1_ref, dr1b_ref, dqx_ref, dkx_ref, dvx_ref, st_ref):
        i = pl.program_id(0)

        @pl.when(i == 0)
        def _():
            dkx_ref[...] = jnp.zeros_like(dkx_ref)
            dvx_ref[...] = jnp.zeros_like(dvx_ref)
            st_ref[...] = jnp.zeros_like(st_ref)

        dr2v = dr2_ref[...]
        dox = _dot_nt(dr2v.astype(BF16), wo_ref[...])
        parts = []
        for h in range(X_HEADS):
            hs = slice(h * X_HEAD_DIM, (h + 1) * X_HEAD_DIM)
            doh = dox[:, hs]
            dohb = doh.astype(BF16)
            dl = jnp.sum(doh * ox_ref[:, hs].astype(F32), axis=1, keepdims=True)
            qh = qx_ref[:, hs] * scale
            p = jnp.exp(_dot_nt(qh, kx_ref[:, hs]) - lse_ref[:, h:h + 1])
            dp = _dot_nt(dohb, vx_ref[:, hs])
            dsb = (p * (dp - dl)).astype(BF16)
            parts.append(_dot(dsb, kx_ref[:, hs]) * scale)
            dkx_ref[:, hs] += _dot_tn(dsb, qh)
            dvx_ref[:, hs] += _dot_tn(p.astype(BF16), dohb)
        dqxb = jnp.concatenate(parts, axis=1).astype(BF16)
        dqx_ref[...] = dqxb
        dh1 = _dot_nt(dqxb, wq_ref[...]) + ALPHA * dr2v
        dr1, dg, db = _ln_bwd_math(dh1, r1_ref[...], g1_ref[...])
        dr1_ref[...] = dr1
        dr1b_ref[...] = dr1.astype(BF16)
        st_ref[0:1, :] += dg
        st_ref[1:2, :] += db

    tile = pl.BlockSpec((t, D_MODEL), lambda i: (i, 0))
    full = lambda r: pl.BlockSpec((r, D_MODEL), lambda i: (0, 0))
    bsh = jax.ShapeDtypeStruct((s, D_MODEL), BF16)
    return _pcall(
        body, name="xattn_bwd", grid=(s // t,),
        in_specs=[tile, tile, tile, pl.BlockSpec((t, LANES), lambda i: (i, 0)), full(ml), full(ml),
                  full(D_MODEL), full(D_MODEL), tile, full(1)],
        out_specs=[tile, tile, tile, full(ml), full(ml), full(8)],
        out_shape=[jax.ShapeDtypeStruct((s, D_MODEL), F32), bsh, bsh,
                   jax.ShapeDtypeStruct((ml, D_MODEL), F32), jax.ShapeDtypeStruct((ml, D_MODEL), F32),
                   jax.ShapeDtypeStruct((8, D_MODEL), F32)],
        args=[dr2, qxb, oxb, lse, kx, vx, wq_b, wo_b, r1, ln1_g], dims=("arbitrary",), comm=comm)


def _halo_specs(t, s, width):
    tb8 = t // 8
    return [pl.BlockSpec((t, width), lambda i: (i, 0)),
            pl.BlockSpec((8, width), lambda i: (jnp.maximum(i * tb8 - 1, 0), 0)),
            pl.BlockSpec((8, width), lambda i: (jnp.minimum((i + 1) * tb8, s // 8 - 1), 0))]


def _halo_rows(i, n, prev_ref, next_ref):
    prev_row = jnp.where(i > 0, prev_ref[7:8, :], 0.0)
    next_row = jnp.where(i < n - 1, next_ref[0:1, :], 0.0)
    return prev_row, next_row


def _gelu_parts(gc):
    cdf = 0.5 * (1.0 + lax.erf(gc * (2.0 ** -0.5)))
    pdf = jnp.exp(-0.5 * gc * gc) * (1.0 / math.sqrt(2.0 * math.pi))
    return gc * cdf, cdf + gc * pdf


def _ffn_out(g, u, conv_w, conv_b, w_down_b, r2, target, ln2_g, ln2_b, ln3_g, ln3_b, *, t):
    s = r2.shape[0]
    n = s // t

    def body(g_ref, gp_ref, gn_ref, u_ref, cw_ref, cb_ref, w_ref, r2_ref, tg_ref, g2, b2, g3, b3,
             t_ref, dr_ref, drb_ref, st_ref):
        i = pl.program_id(0)

        @pl.when(i == 0)
        def _():
            st_ref[...] = jnp.zeros_like(st_ref)

        gv = g_ref[...]
        prev_row, next_row = _halo_rows(i, n, gp_ref, gn_ref)
        gm1, gp1 = _shift_rows(gv, prev_row, next_row)
        gc = gm1 * cw_ref[0:1, :] + gv * cw_ref[1:2, :] + gp1 * cw_ref[2:3, :] + cb_ref[...]
        act, _ = _gelu_parts(gc)
        tb = (act * u_ref[...]).astype(BF16)
        t_ref[...] = tb
        h2 = _ln(r2_ref[...], g2[...], b2[...])
        r3 = ALPHA * h2 + _dot(tb, w_ref[...])
        y = _ln(r3, g3[...], b3[...])
        err = y - tg_ref[...]
        loss = 0.5 * jnp.sum(jnp.mean(err * err, axis=-1, keepdims=True))
        dr, dg, db = _ln_bwd_math(err * (1.0 / D_MODEL), r3, g3[...])
        dr_ref[...] = dr
        drb_ref[...] = dr.astype(BF16)
        st_ref[0:1, :] += dg
        st_ref[1:2, :] += db
        st_ref[2:3, :] += jnp.full((1, D_MODEL), loss, F32)

    wide = pl.BlockSpec((t, D_FF), lambda i: (i, 0))
    tile = pl.BlockSpec((t, D_MODEL), lambda i: (i, 0))
    row = pl.BlockSpec((1, D_MODEL), lambda i: (0, 0))
    return pl.pallas_call(
        body, name="ffn_out", grid=(n,),
        in_specs=_halo_specs(t, s, D_FF) + [wide, pl.BlockSpec((3, D_FF), lambda i: (0, 0)),
                                            pl.BlockSpec((1, D_FF), lambda i: (0, 0)),
                                            pl.BlockSpec((D_FF, D_MODEL), lambda i: (0, 0)),
                                            tile, tile, row, row, row, row],
        out_specs=[wide, tile, tile, pl.BlockSpec((8, D_MODEL), lambda i: (0, 0))],
        out_shape=[jax.ShapeDtypeStruct((s, D_FF), BF16), jax.ShapeDtypeStruct((s, D_MODEL), F32),
                   jax.ShapeDtypeStruct((s, D_MODEL), BF16), jax.ShapeDtypeStruct((8, D_MODEL), F32)],
        compiler_params=_cparams(dimension_semantics=("arbitrary",)),
    )(g, g, g, u, conv_w, conv_b, w_down_b, r2, target, ln2_g, ln2_b, ln3_g, ln3_b)


def _dh2_ln2(dgc, conv_w, du, w_gate_b, w_up_b, dr3, r2, ln2_g, *, t, comm=None):
    s = dgc.shape[0]
    n = s // t

    def body(d_ref, dp_ref, dn_ref, cw_ref, du_ref, wg_ref, wu_ref, dr3_ref, r2_ref, g2, dg_ref, dr_ref, drb_ref,
             st_ref):
        i = pl.program_id(0)

        @pl.when(i == 0)
        def _():
            st_ref[...] = jnp.zeros_like(st_ref)

        dv = d_ref[...]
        prev_row, next_row = _halo_rows(i, n, dp_ref, dn_ref)
        dm1, dp1 = _shift_rows(dv, prev_row, next_row)
        dgb = (dp1 * cw_ref[0:1, :] + dv * cw_ref[1:2, :] + dm1 * cw_ref[2:3, :]).astype(BF16)
        dg_ref[...] = dgb
        dh2 = _dot(dgb, wg_ref[...]) + _dot(du_ref[...], wu_ref[...]) + ALPHA * dr3_ref[...]
        dr, dg, db = _ln_bwd_math(dh2, r2_ref[...], g2[...])
        dr_ref[...] = dr
        drb_ref[...] = dr.astype(BF16)
        st_ref[0:1, :] += dg
        st_ref[1:2, :] += db

    wide = pl.BlockSpec((t, D_FF), lambda i: (i, 0))
    tile = pl.BlockSpec((t, D_MODEL), lambda i: (i, 0))
    wfull = pl.BlockSpec((D_FF, D_MODEL), lambda i: (0, 0), pipeline_mode=pl.Buffered(1))
    return _pcall(
        body, name="dh2_ln2", grid=(n,),
        in_specs=_halo_specs(t, s, D_FF) + [pl.BlockSpec((3, D_FF), lambda i: (0, 0)), wide, wfull, wfull,
                                            tile, tile, pl.BlockSpec((1, D_MODEL), lambda i: (0, 0))],
        out_specs=[wide, tile, tile, pl.BlockSpec((8, D_MODEL), lambda i: (0, 0))],
        out_shape=[jax.ShapeDtypeStruct((s, D_FF), BF16), jax.ShapeDtypeStruct((s, D_MODEL), F32),
                   jax.ShapeDtypeStruct((s, D_MODEL), BF16), jax.ShapeDtypeStruct((8, D_MODEL), F32)],
        args=[dgc, dgc, dgc, conv_w, du, w_gate_b, w_up_b, dr3, r2, ln2_g], dims=("arbitrary",), comm=comm)


def _conv_bwd_a(dr3b, w_down_b, g, u, conv_w, conv_b, *, t):
    s = g.shape[0]
    n = s // t

    def body(d_ref, w_ref, g_ref, gp_ref, gn_ref, u_ref, cw_ref, cb_ref, du_ref, dgc_ref, st_ref):
        i = pl.program_id(0)

        @pl.when(i == 0)
        def _():
            st_ref[...] = jnp.zeros_like(st_ref)

        dt = _dot_nt(d_ref[...], w_ref[...])
        gv = g_ref[...]
        prev_row, next_row = _halo_rows(i, n, gp_ref, gn_ref)
        gm1, gp1 = _shift_rows(gv, prev_row, next_row)
        gc = gm1 * cw_ref[0:1, :] + gv * cw_ref[1:2, :] + gp1 * cw_ref[2:3, :] + cb_ref[...]
        act, dact = _gelu_parts(gc)
        du_ref[...] = (dt * act).astype(BF16)
        dgc = dt * u_ref[...] * dact
        dgc_ref[...] = dgc
        st_ref[0:1, :] += jnp.sum(gm1 * dgc, axis=0, keepdims=True)
        st_ref[1:2, :] += jnp.sum(gv * dgc, axis=0, keepdims=True)
        st_ref[2:3, :] += jnp.sum(gp1 * dgc, axis=0, keepdims=True)
        st_ref[3:4, :] += jnp.sum(dgc, axis=0, keepdims=True)

    tile = pl.BlockSpec((t, D_FF), lambda i: (i, 0))
    return pl.pallas_call(
        body, name="conv_bwd_a", grid=(n,),
        in_specs=[pl.BlockSpec((t, D_MODEL), lambda i: (i, 0)), pl.BlockSpec((D_FF, D_MODEL), lambda i: (0, 0))]
        + _halo_specs(t, s, D_FF) + [tile, pl.BlockSpec((3, D_FF), lambda i: (0, 0)),
                                     pl.BlockSpec((1, D_FF), lambda i: (0, 0))],
        out_specs=[tile, tile, pl.BlockSpec((8, D_FF), lambda i: (0, 0))],
        out_shape=[jax.ShapeDtypeStruct((s, D_FF), BF16), jax.ShapeDtypeStruct((s, D_FF), F32),
                   jax.ShapeDtypeStruct((8, D_FF), F32)],
        compiler_params=_cparams(dimension_semantics=("arbitrary",)),
    )(dr3b, w_down_b, g, g, g, u, conv_w, conv_b)


def _to_residue(a, dil):
    s, w = a.shape
    return a.reshape(s // dil, dil, w).transpose(1, 0, 2)


def _stats_to_lanes(rows):
    dil, hq, l = rows.shape
    return jnp.pad(rows.transpose(2, 0, 1).reshape(dil * l, hq), ((0, 0), (0, LANES - hq)))


def _stats_to_rows(lanes, dil):
    s = lanes.shape[0]
    return lanes[:, :DIL_SLOTS].reshape(s // dil, dil, DIL_SLOTS).transpose(1, 2, 0)


def _rope_angles(positions):
    inv_freq = ROPE_THETA ** (-jnp.arange(0, ROT_DIM, 2, dtype=F32) / ROT_DIM)
    ang = positions.astype(F32)[:, None] * inv_freq
    return jnp.concatenate([jnp.cos(ang), jnp.sin(ang)], axis=1)


class _NoPlan:
    def gather(self, stage):
        return None

    def gathered(self, stage, couts, wb):
        pass

    def exchange(self, stage, grads):
        return None

    def exchanged(self, stage, couts):
        pass


def _local_step(x, mem, positions, target, wb, sp, plan=None, *, t_row=256, t_mm=512, tq_a=128, tq_b=128,
                sub_a=4, sub_b=4):
    s = x.shape[0]
    plan = plan or _NoPlan()
    cs = _rope_angles(positions)
    e_mat = _rope_select_matrix()

    h0b, couts = _ln_in_fwd(x, sp["ln_in_g"], sp["ln_in_b"], t=t_mm, comm=plan.gather("ln_in"))
    plan.gathered("ln_in", couts, wb)
    sp = dict(sp, conv_w=wb.get("conv_w", sp.get("conv_w")))
    (za, *zb), couts = _proj_all(h0b, wb["w_in"], cs, e_mat, t=min(2 * t_mm, s), comm=plan.gather("proj"))
    plan.gathered("proj", couts, wb)
    sub_a = max(1, min(sub_a, s // tq_a))
    subs_b = [max(1, min(sub_b, s // dil // tq_b)) for dil in DILATIONS]
    subs_b_bwd = [s // dil // tq_b if s // dil <= ONE_TILE_ROWS else sb for dil, sb in zip(DILATIONS, subs_b)]
    out_a, lse_a, couts = _swa_fwd_p(za, qcol=0, kcol=4, vcol=5, hq=WIN_Q_HEADS, hkv=WIN_KV_HEADS, w=WIN_HALF,
                                     tq=tq_a, sub=min(2 * sub_a, s // tq_a), sink=sp["attn_sink"], name="attn_a_fwd",
                                     comm=plan.gather("attn_a"))
    plan.gathered("attn_a", couts, wb)
    o_g, lse_g = [], []
    for gi in range(3):
        o, l, couts = _swa_fwd_p(zb[gi], qcol=0, kcol=1, vcol=2, hq=DIL_SLOTS, hkv=DIL_SLOTS, w=DIL_HALF, tq=tq_b,
                                 sub=min(2 * subs_b[gi], s // DILATIONS[gi] // tq_b), sink=None, name=f"attn_b{gi}_fwd",
                                 comm=plan.gather(f"attn_b{gi}"))
        plan.gathered(f"attn_b{gi}", couts, wb)
        o_g.append(o)
        lse_g.append(_stats_to_lanes(l))
    (mixed_b, out_b, lse_b, r1, h1b), couts = _combine_fwd(
        out_a, o_g, lse_g, sp["g_win"], sp["g_dil"], wb["w_mix_out"], x, sp["ln_in_g"], sp["ln_in_b"],
        sp["ln1_g"], sp["ln1_b"], t=t_mm, comm=plan.gather("combine"))
    plan.gathered("combine", couts, wb)
    mem_nb, kx, vx = _mem_fwd(mem, sp["mem_ln_g"], sp["mem_ln_b"], wb["w_xk"], wb["w_xv"])
    (r2, h2b, qxb, oxb, lse_x), couts = _xattn_fwd(
        h1b, r1, kx, vx, wb["w_xq"], wb["w_xo"], sp["ln1_g"], sp["ln1_b"], sp["ln2_g"], sp["ln2_b"], t=t_mm,
        comm=plan.gather("xattn"))
    plan.gathered("xattn", couts, wb)
    g = _mm(h2b, wb["w_gate"], mode="nt", out_dtype=F32, tm=t_mm, tn=D_FF, name="ff_gate")
    u = _mm(h2b, wb["w_up"], mode="nt", out_dtype=F32, tm=t_mm, tn=D_FF, name="ff_up")
    tb, dr3, dr3b, st3 = _ffn_out(g, u, sp["conv_w"], sp["conv_b"], wb["w_down"], r2, target, sp["ln2_g"],
                                  sp["ln2_b"], sp["ln3_g"], sp["ln3_b"], t=t_row)

    grads = {}
    du, dgc, st_conv = _conv_bwd_a(dr3b, wb["w_down"], g, u, sp["conv_w"], sp["conv_b"], t=t_row)
    tk = min(2048, s)
    grads["w_down"] = _mm(tb, dr3b, mode="tn", out_dtype=BF16, tm=D_FF // 2, tn=D_MODEL, tk=tk, name="dw_down")
    grads["w_up"] = _mm(du, h2b, mode="tn", out_dtype=BF16, tm=D_FF // 2, tn=D_MODEL, tk=tk, name="dw_up")
    (dg, dr2, dr2b, st2), couts = _dh2_ln2(dgc, sp["conv_w"], du, wb["w_gate"], wb["w_up"], dr3, r2, sp["ln2_g"],
                                           t=t_mm, comm=plan.exchange("dh2", grads))
    plan.exchanged("dh2", couts)
    grads["w_gate"] = _mm(dg, h2b, mode="tn", out_dtype=BF16, tm=D_FF // 2, tn=D_MODEL, tk=tk, name="dw_gate")

    (dr1, dr1b, dqxb, dkx, dvx, st1), couts = _xattn_bwd(
        dr2, qxb, oxb, lse_x, kx, vx, wb["w_xq"], wb["w_xo"], r1, sp["ln1_g"], t=t_mm,
        comm=plan.exchange("xattn", grads))
    plan.exchanged("xattn", couts)
    grads["w_xo"] = _mm(oxb, dr2b, mode="tn", out_dtype=BF16, tm=D_MODEL, tn=D_MODEL, tk=tk, name="dw_xo")
    grads["w_xq"] = _mm(h1b, dqxb, mode="tn", out_dtype=BF16, tm=D_MODEL, tn=D_MODEL, tk=tk, name="dw_xq")
    grads["w_xk"], grads["w_xv"], st_mem = _mem_bwd(dkx, dvx, mem, sp["mem_ln_g"], sp["mem_ln_b"],
                                                    wb["w_xk"], wb["w_xv"])

    grads["w_mix_out"] = _mm(mixed_b, dr1b, mode="tn", out_dtype=BF16, tm=D_MODEL, tn=D_MODEL, tk=tk,
                             name="dw_mix")
    do_a, do_b, dl_a, dl_b, st_mix = _combine_bwd(dr1b, wb["w_mix_out"], out_a, out_b, sp["g_win"], sp["g_dil"],
                                                  t=t_mm)
    (dqa, dka, dva, dsink), couts = _swa_bwd_p(
        za, do_a, lse_a, _stats_to_rows(dl_a, 1), cs[None], e_mat, qcol=0, kcol=4, vcol=5, hq=WIN_Q_HEADS,
        hkv=WIN_KV_HEADS, w=WIN_HALF, tq=2 * tq_a, sub=max(1, sub_a // 2), sink=sp["attn_sink"], name="attn_a_bwd",
        comm=plan.exchange("attn_a", grads))
    plan.exchanged("attn_a", couts)
    dqs, dks, dvs = [], [], []
    for gi, dil in enumerate(DILATIONS):
        (dq, dk, dv), couts = _swa_bwd_p(
            zb[gi], do_b[gi], _stats_to_rows(lse_b, dil), _stats_to_rows(dl_b, dil),
            _to_residue(cs, dil), e_mat, qcol=0, kcol=1, vcol=2, hq=DIL_SLOTS, hkv=DIL_SLOTS, w=DIL_HALF, tq=tq_b,
            sub=subs_b_bwd[gi], sink=None, name=f"attn_b{gi}_bwd", comm=plan.exchange(f"attn_b{gi}", grads))
        plan.exchanged(f"attn_b{gi}", couts)
        dqs.append(dq)
        dks.append(dk)
        dvs.append(dv)
    dz = _assemble_dz(dqa, dka, dva, dqs, dks, dvs, t=t_mm)
    grads["w_in"] = _mm(dz, h0b, mode="tn", out_dtype=BF16, tm=IN_WIDTH // 7, tn=D_MODEL, tk=tk, name="dw_in")
    (grad_x, st0), couts = _dh0_ln_in(dz, wb["w_in"], dr1, x, sp["ln_in_g"], t=t_mm,
                                      comm=plan.exchange("dh0", grads))
    plan.exchanged("dh0", couts)

    small = {
        "loss": st3[2:3, 0:1],
        "ln_in_g": st0[0:1], "ln_in_b": st0[1:2],
        "attn_sink": dsink[:, 0].reshape(1, WIN_Q_HEADS),
        "g_win": st_mix[0:1], "g_dil": st_mix[1:2],
        "ln1_g": st1[0:1], "ln1_b": st1[1:2],
        "mem_ln_g": st_mem[0:1], "mem_ln_b": st_mem[1:2],
        "ln2_g": st2[0:1], "ln2_b": st2[1:2],
        "conv_w": st_conv[0:3], "conv_b": st_conv[3:4],
        "ln3_g": st3[0:1], "ln3_b": st3[1:2],
    }
    return grad_x, grads, small


class _SiblingSwap:
    def __init__(self, arrays):
        self.inputs = list(arrays)
        n = len(arrays)
        self.out_shape = [jax.ShapeDtypeStruct(a.shape, a.dtype) for a in arrays]
        self.scratch = [pltpu.SemaphoreType.DMA((n,)), pltpu.SemaphoreType.DMA((n,))]

    def _copies(self, src, dst, sems):
        send_sems, recv_sems = sems
        x, y, c, _ = _place()
        return [pltpu.make_async_remote_copy(
            src_ref=src[a], dst_ref=dst[a], send_sem=send_sems.at[a], recv_sem=recv_sems.at[a],
            device_id=(x, y, 1 - c), device_id_type=MESH_IDS) for a in range(len(src))]

    def start(self, src, dst, sems):
        for cp in self._copies(src, dst, sems):
            cp.start()

    def wait(self, src, dst, sems):
        copies = self._copies(src, dst, sems)
        for cp in copies:
            cp.wait_recv()
        for cp in copies:
            cp.wait_send()


class _Both:
    def __init__(self, first, second):
        self.parts = (first, second)
        self.inputs = first.inputs + second.inputs
        self.out_shape = first.out_shape + second.out_shape
        self.scratch = first.scratch + second.scratch

    def _split(self, src, dst, sems):
        a = self.parts[0]
        ni, no, ns = len(a.inputs), len(a.out_shape), len(a.scratch)
        return ((src[:ni], dst[:no], sems[:ns]), (src[ni:], dst[no:], sems[ns:]))

    def start(self, src, dst, sems):
        for part, args in zip(self.parts, self._split(src, dst, sems)):
            part.start(*args)

    def wait(self, src, dst, sems):
        for part, args in zip(self.parts, self._split(src, dst, sems)):
            part.wait(*args)


def _row_tile(rows, cols, itemsize=4, budget=1 << 20):
    best = None
    for t in range(16, rows + 1, 16):
        if rows % t == 0 and t * cols * itemsize <= budget:
            best = t
    return best or rows


def _sum_slots(stack, *, name):
    n, r, c = stack.shape
    t = _row_tile(r, c)

    def body(s_ref, o_ref):
        acc = s_ref[0].astype(F32)
        for q in range(1, n):
            acc = acc + s_ref[q].astype(F32)
        o_ref[...] = acc

    return pl.pallas_call(
        body, name=name, grid=(r // t,), in_specs=[pl.BlockSpec((n, t, c), lambda i: (0, i, 0))],
        out_specs=pl.BlockSpec((t, c), lambda i: (i, 0)), out_shape=jax.ShapeDtypeStruct((r, c), F32),
        compiler_params=_cparams(dimension_semantics=("parallel",)),
    )(stack)


def _adamw(w, m, v, p, q, *, name):
    r, c = w.shape
    t = _row_tile(r, c, budget=1 << 20)

    def total(ref):
        if len(ref.shape) == 2:
            return ref[...]
        acc = ref[0].astype(F32)
        for slot in range(1, ref.shape[0]):
            acc = acc + ref[slot].astype(F32)
        return acc

    def body(*refs):
        if q is None:
            w_ref, m_ref, v_ref, p_ref, g_ref, d_ref, nm_ref, nv_ref = refs
            g = total(p_ref)
        else:
            w_ref, m_ref, v_ref, p_ref, q_ref, g_ref, d_ref, nm_ref, nv_ref = refs
            g = total(p_ref) + total(q_ref)
        nm = ADAM_B1 * m_ref[...] + (1.0 - ADAM_B1) * g
        nv = ADAM_B2 * v_ref[...] + (1.0 - ADAM_B2) * (g * g)
        m_hat = nm / (1.0 - ADAM_B1 ** ADAM_STEP)
        v_hat = nv / (1.0 - ADAM_B2 ** ADAM_STEP)
        g_ref[...] = g
        d_ref[...] = -ADAM_LR * (m_hat / (jnp.sqrt(v_hat) + ADAM_EPS) + ADAM_WD * w_ref[...])
        nm_ref[...] = nm
        nv_ref[...] = nv

    tile = pl.BlockSpec((t, c), lambda i: (i, 0))
    args = [w, m, v, p] + ([] if q is None else [q])
    in_specs = [tile if a.ndim == 2 else pl.BlockSpec((a.shape[0], t, c), lambda i: (0, i, 0)) for a in args]
    sh = jax.ShapeDtypeStruct((r, c), F32)
    return pl.pallas_call(
        body, name=name, grid=(r // t,), in_specs=in_specs, out_specs=[tile] * 4, out_shape=[sh] * 4,
        compiler_params=_cparams(dimension_semantics=("parallel",)),
    )(*args)


BIG = ("w_in", "w_mix_out", "w_xq", "w_xk", "w_xv", "w_xo", "w_gate", "w_up", "w_down")
COL_SHARDED = ("w_in", "w_gate", "w_up")
WEIGHTS = ("ln_in_g", "ln_in_b", "w_in", "attn_sink", "g_win", "g_dil", "w_mix_out", "ln1_g", "ln1_b",
           "mem_ln_g", "mem_ln_b", "w_xq", "w_xk", "w_xv", "w_xo", "ln2_g", "ln2_b", "w_gate", "w_up",
           "conv_w", "conv_b", "w_down", "ln3_g", "ln3_b")
SMALL = tuple(k for k in WEIGHTS if k not in BIG)
PACK_COLS = 1024
CONV_SHARD = D_FF // N_CHIPS
CONV_WIDTH_ROWS = 3
SMALL_ROWS = 32


GATHER_STAGES = {"ln_in": ("w_in", "conv_w"), "proj": ("w_mix_out", "w_xq", "w_xk", "w_xv", "w_xo", "w_up"),
                 "combine": ("w_gate", "w_down")}
EXCHANGE_STAGES = {"dh2": ("w_down", "w_up"), "attn_a": ("w_gate", "w_xo", "w_xq"),
                   "attn_b0": ("w_xk", "w_xv", "w_mix_out"), "dh0": ("w_in",)}


def _full_weight(k, g4):
    return g4.reshape(N_CHIPS * g4.shape[1], g4.shape[2])


def _grad_parts(k, gk):
    gk = gk.astype(BF16)
    return gk.reshape(N_CHIPS, gk.shape[0] // N_CHIPS, gk.shape[1])


EARLY_SWAP_STAGE = "attn_b2"


class _Plan:
    def __init__(self, shards):
        self.shards = shards
        self.recv = {}
        self.chip_sums = {}
        self.sibling_sums = {}

    def gather(self, stage):
        names = GATHER_STAGES.get(stage)
        return _ChipGather([self.shards[k] for k in names]) if names else None

    def gathered(self, stage, couts, wb):
        for k, g4 in zip(GATHER_STAGES.get(stage, ()), couts):
            if k == "conv_w":
                taps = g4[:, :CONV_WIDTH_ROWS, :CONV_SHARD]
                wb[k] = taps.transpose(1, 0, 2).reshape(CONV_WIDTH_ROWS, D_FF)
            else:
                wb[k] = _full_weight(k, g4)

    def exchange(self, stage, grads):
        if stage == EARLY_SWAP_STAGE:
            self.early = [k for k in BIG if k in self.recv]
            for k in self.early:
                self.chip_sums[k] = self.recv[k]
            return _SiblingSwap([self.chip_sums[k] for k in self.early])
        names = EXCHANGE_STAGES.get(stage)
        return _ChipExchange([_grad_parts(k, grads[k]) for k in names]) if names else None

    def exchanged(self, stage, couts):
        if stage == EARLY_SWAP_STAGE:
            self.sibling_sums.update(zip(self.early, couts))
            return
        for k, r4 in zip(EXCHANGE_STAGES.get(stage, ()), couts):
            self.recv[k] = r4


def _pack_rows(a):
    r, n = a.shape
    per = -(-n // PACK_COLS)
    return jnp.pad(a, ((0, 0), (0, per * PACK_COLS - n))).reshape(r * per, PACK_COLS)


def _unpack_rows(p, r, n):
    per = -(-n // PACK_COLS)
    return p.reshape(r, per * PACK_COLS)[:, :n]


def _pack(pieces, rows_total):
    cat = jnp.concatenate([_pack_rows(a) for a in pieces], axis=0)
    return jnp.pad(cat, ((0, rows_total - cat.shape[0]), (0, 0)))


def _unpack(p, shapes):
    out, at = [], 0
    for r, n in shapes:
        per = -(-n // PACK_COLS)
        out.append(_unpack_rows(p[at:at + r * per], r, n))
        at += r * per
    return out


def kernel(x, mem, positions, ln_in_g, ln_in_b, w_in, attn_sink, g_win, g_dil, w_mix_out, ln1_g, ln1_b, mem_ln_g, mem_ln_b, w_xq, w_xk, w_xv, w_xo, ln2_g, ln2_b, w_gate, w_up, conv_w, conv_b, w_down, ln3_g, ln3_b, loss_target, m_ln_in_g, m_ln_in_b, m_w_in, m_attn_sink, m_g_win, m_g_dil, m_w_mix_out, m_ln1_g, m_ln1_b, m_mem_ln_g, m_mem_ln_b, m_w_xq, m_w_xk, m_w_xv, m_w_xo, m_ln2_g, m_ln2_b, m_w_gate, m_w_up, m_conv_w, m_conv_b, m_w_down, m_ln3_g, m_ln3_b, v_ln_in_g, v_ln_in_b, v_w_in, v_attn_sink, v_g_win, v_g_dil, v_w_mix_out, v_ln1_g, v_ln1_b, v_mem_ln_g, v_mem_ln_b, v_w_xq, v_w_xk, v_w_xv, v_w_xo, v_ln2_g, v_ln2_b, v_w_gate, v_w_up, v_conv_w, v_conv_b, v_w_down, v_ln3_g, v_ln3_b):
    given = dict(locals())
    shape_of = {k: given[k].shape for k in WEIGHTS}
    as2d = lambda k, a: a.reshape(-1, a.shape[-1]).T if k in COL_SHARDED else a.reshape(-1, a.shape[-1])
    w2 = {k: as2d(k, given[k]) for k in WEIGHTS}
    m2 = {k: as2d(k, given["m_" + k]) for k in WEIGHTS}
    v2 = {k: as2d(k, given["v_" + k]) for k in WEIGHTS}
    chip = 2 * lax.axis_index("x") + lax.axis_index("y")

    shards = {k: w2[k].astype(BF16) for k in BIG}
    shards["conv_w"] = jnp.pad(w2["conv_w"], ((0, 16 - CONV_WIDTH_ROWS), (0, PACK_COLS - CONV_SHARD)))
    plan = _Plan(shards)
    sp = {k: w2[k] for k in SMALL if k != "conv_w"}

    grad_x, grads, small = _local_step(x[0], mem[0], positions[0], loss_target[0], {}, sp, plan)

    small_keys = ("loss",) + SMALL
    small_shapes = [small[k].shape for k in small_keys]
    small_pack = _pack([small[k] for k in small_keys], SMALL_ROWS)
    late = [k for k in BIG if k not in plan.chip_sums]
    for k in late:
        plan.chip_sums[k] = _sum_slots(plan.recv[k], name=f"sum_chips_{k}")
    *late_sibling, small_all = _comm_only(
        _Both(_SiblingSwap([plan.chip_sums[k] for k in late]), _ChipExchange([], small_pack)), "swap_and_small")
    plan.sibling_sums.update(zip(late, late_sibling))
    chip_sums = [plan.chip_sums[k] for k in BIG]
    sibling_sums = [plan.sibling_sums[k] for k in BIG]
    small_sum = _sum_slots(small_all, name="sum_small")
    small_g = dict(zip(small_keys, _unpack(small_sum, small_shapes)))
    loss = small_g["loss"][0, 0]

    res = {}
    for k, p, q in zip(BIG, chip_sums, sibling_sums):
        res[k] = _adamw(w2[k], m2[k], v2[k], p, q, name=f"adamw_{k}")
    small_g["conv_w"] = lax.dynamic_slice_in_dim(small_g["conv_w"], chip * CONV_SHARD, CONV_SHARD, axis=1)
    adam_shapes = [w2[k].shape for k in SMALL]
    packs = [_pack([d[k] for k in SMALL], SMALL_ROWS) for d in (w2, m2, v2, small_g)]
    small_res = [_unpack(o, adam_shapes) for o in _adamw(*packs, None, name="adamw_small")]
    for i, k in enumerate(SMALL):
        res[k] = tuple(o[i] for o in small_res)

    outs = [loss, grad_x[None]]
    for slot in range(4):
        outs += [(res[k][slot].T if k in COL_SHARDED else res[k][slot]).reshape(shape_of[k]) for k in WEIGHTS]
    return tuple(outs)
```
